```python
import math
import jax, jax.numpy as jnp
from jax import lax
import numpy as np

D_MODEL = 1024
BATCH = 8
SEQ = 4096
DEPTH = 1

PLE_DIM = 256
HEAD_DIM = 64
A_Q_HEADS = 8
A_KV_HEADS = 2
A_GROUP = A_Q_HEADS // A_KV_HEADS
A_WINDOW = 128
B_HEADS = 8
B_PATTERNS = ((128, 1), (512, 4), (2048, 16))
N_HEADS_TOTAL = A_Q_HEADS + B_HEADS
A_Q = A_Q_HEADS * HEAD_DIM
A_KV = A_KV_HEADS * HEAD_DIM
B_W = B_HEADS * HEAD_DIM
D_IN = A_Q + 2 * A_KV + 3 * B_W
D_MIX = A_Q + B_W
D_FF = 2816
NUM_BUCKETS = 32
MAX_DISTANCE = 2048
BLOCK = 128
EPS = 1e-6
NEG_INF = -1e30

kernel_name = "hymba_swa_sink_dilated_macaron_layer"


def rms_norm(x, g):
    xf = x.astype(jnp.float32)
    y = xf * lax.rsqrt(jnp.mean(xf * xf, axis=-1, keepdims=True) + EPS)
    return (y * g.astype(jnp.float32)).astype(x.dtype)


def swiglu(x, w_gu, w_down):
    g, u = jnp.split(x @ w_gu, 2, axis=-1)
    return (jax.nn.silu(g) * u) @ w_down


def t5_bucket(dist):
    max_exact = NUM_BUCKETS // 2
    n = jnp.maximum(dist, 0)
    nf = jnp.maximum(n, 1).astype(jnp.float32)
    large = max_exact + (jnp.log(nf / max_exact) / math.log(MAX_DISTANCE / max_exact)
                         * (NUM_BUCKETS - max_exact)).astype(jnp.int32)
    large = jnp.minimum(large, NUM_BUCKETS - 1)
    return jnp.where(n < max_exact, n, large)


def banded_attention(q, k, v, rel_bias, max_dist, stride):
    n, L, hkv, grp, dh = q.shape
    bq = math.gcd(L, BLOCK)
    nb = L // bq
    nk = bq + max_dist
    pad = ((0, 0), (max_dist, 0), (0, 0), (0, 0))
    k_pad = jnp.pad(k, pad)
    v_pad = jnp.pad(v, pad)
    key_idx = jnp.arange(nb)[:, None] * bq + jnp.arange(nk)[None, :]
    kb = k_pad[:, key_idx]
    vb = v_pad[:, key_idx]
    qb = q.reshape(n, nb, bq, hkv, grp, dh)
    logits = jnp.einsum('nbqhgd,nbkhd->nbhgqk', qb, kb,
                        preferred_element_type=jnp.float32) * (dh ** -0.5)
    rel = jnp.arange(bq)[:, None] + max_dist - jnp.arange(nk)[None, :]
    bias = rel_bias[t5_bucket(rel * stride)].astype(jnp.float32)
    bias = bias.reshape(bq, nk, hkv, grp).transpose(2, 3, 0, 1)
    in_band = (rel >= 0) & (rel <= max_dist)
    key_pos = key_idx - max_dist
    valid = in_band[None] & (key_pos >= 0)[:, None, :]
    logits = jnp.where(valid[None, :, None, None], logits + bias, NEG_INF)
    m = jnp.max(logits, axis=-1)
    pr = jnp.exp(logits - m[..., None])
    s = jnp.sum(pr, axis=-1)
    o = jnp.einsum('nbhgqk,nbkhd->nbqhgd', pr, vb.astype(jnp.float32))
    o = o.reshape(n, L, hkv, grp, dh)
    m = m.transpose(0, 1, 4, 2, 3).reshape(n, L, hkv, grp)
    s = s.transpose(0, 1, 4, 2, 3).reshape(n, L, hkv, grp)
    return o, m, s


def to_classes(t, r):
    b, s = t.shape[:2]
    rest = t.shape[2:]
    t = jnp.moveaxis(t.reshape((b, s // r, r) + rest), 2, 1)
    return t.reshape((b * r, s // r) + rest)


def from_classes(t, b, r):
    n, L = t.shape[:2]
    rest = t.shape[2:]
    t = jnp.moveaxis(t.reshape((b, r, L) + rest), 1, 2)
    return t.reshape((b, L * r) + rest)


def sink_swa_gqa(q_a, k_a, v_a, sinks, rel_bias_a):
    b, s, _ = q_a.shape
    q = q_a.reshape(b, s, A_KV_HEADS, A_GROUP, HEAD_DIM)
    k = k_a.reshape(b, s, A_KV_HEADS, HEAD_DIM)
    v = v_a.reshape(b, s, A_KV_HEADS, HEAD_DIM)
    o, m, den = banded_attention(q, k, v, rel_bias_a, A_WINDOW - 1, 1)
    sink = sinks.reshape(A_KV_HEADS, A_GROUP).astype(jnp.float32)
    m_all = jnp.maximum(m, sink)
    scale = jnp.exp(m - m_all)
    total = den * scale + jnp.exp(sink - m_all)
    o = o * (scale / total)[..., None]
    return o.reshape(b, s, A_Q)


def dilated_mixture(q_b, k_b, v_b, rel_bias_b):
    b, s, _ = q_b.shape
    q = q_b.reshape(b, s, B_HEADS, 1, HEAD_DIM)
    k = k_b.reshape(b, s, B_HEADS, HEAD_DIM)
    v = v_b.reshape(b, s, B_HEADS, HEAD_DIM)
    outs, maxes, dens = [], [], []
    for window, dil in B_PATTERNS:
        o, m, den = banded_attention(to_classes(q, dil), to_classes(k, dil), to_classes(v, dil),
                                     rel_bias_b, window // dil, dil)
        outs.append(from_classes(o, b, dil))
        maxes.append(from_classes(m, b, dil))
        dens.append(from_classes(den, b, dil))
    m_stack = jnp.stack(maxes)
    m_all = jnp.max(m_stack, axis=0)
    w = jnp.exp(m_stack - m_all)
    num = jnp.sum(w[..., None] * jnp.stack(outs), axis=0)
    den_all = jnp.sum(w * jnp.stack(dens), axis=0)
    return (num / den_all[..., None]).reshape(b, s, B_W)


def _fwd_setup_inputs(seed: int = 0) -> dict:
    key = jax.random.key(seed)
    ks = jax.random.split(key, 24)
    f32 = jnp.float32

    def w(k, shape, fan_in):
        return jax.random.normal(k, shape, f32) * (fan_in ** -0.5)

    def gain(k, shape):
        return 1.0 + 0.05 * jax.random.normal(k, shape, f32)

    def small(k, shape, scale=0.02):
        return scale * jax.random.normal(k, shape, f32)

    D = D_MODEL
    return {
        "x": jax.random.normal(ks[0], (BATCH, SEQ, D), f32),
        "p": jax.random.normal(ks[1], (DEPTH, BATCH, SEQ, PLE_DIM), f32),
        "rel_bias": small(ks[2], (NUM_BUCKETS, N_HEADS_TOTAL), 0.5),
        "ffn1_pre_g": gain(ks[3], (DEPTH, D)),
        "ffn1_w_gu": w(ks[4], (DEPTH, D, 2 * D_FF), D),
        "ffn1_w_down": w(ks[5], (DEPTH, D_FF, D), D_FF),
        "ffn1_post_g": gain(ks[6], (DEPTH, D)),
        "attn_pre_g": gain(ks[7], (DEPTH, D)),
        "w_in": w(ks[8], (DEPTH, D, D_IN), D),
        "b_in": small(ks[9], (DEPTH, D_IN)),
        "sinks": small(ks[10], (DEPTH, A_Q_HEADS), 0.5),
        "w_out": w(ks[11], (DEPTH, D_MIX, D), D_MIX),
        "b_out": small(ks[12], (DEPTH, D)),
        "attn_post_g": gain(ks[13], (DEPTH, D)),
        "ffn2_pre_g": gain(ks[14], (DEPTH, D)),
        "ffn2_w_gu": w(ks[15], (DEPTH, D, 2 * D_FF), D),
        "ffn2_w_down": w(ks[16], (DEPTH, D_FF, D), D_FF),
        "ffn2_post_g": gain(ks[17], (DEPTH, D)),
        "ple_pre_g": gain(ks[18], (DEPTH, D)),
        "w_ple_gate": w(ks[19], (DEPTH, D, D), D),
        "w_ple_proj": w(ks[20], (DEPTH, PLE_DIM, D), PLE_DIM),
        "ple_post_g": gain(ks[21], (DEPTH, D)),
    }


def _fwd_reference(x, p, rel_bias, ffn1_pre_g, ffn1_w_gu, ffn1_w_down, ffn1_post_g,
              attn_pre_g, w_in, b_in, sinks, w_out, b_out, attn_post_g,
              ffn2_pre_g, ffn2_w_gu, ffn2_w_down, ffn2_post_g,
              ple_pre_g, w_ple_gate, w_ple_proj, ple_post_g):
    h = x
    splits = [A_Q, A_Q + A_KV, A_Q + 2 * A_KV, A_Q + 2 * A_KV + B_W, A_Q + 2 * A_KV + 2 * B_W]
    for i in range(DEPTH):
        f = swiglu(rms_norm(h, ffn1_pre_g[i]), ffn1_w_gu[i], ffn1_w_down[i])
        h = h + 0.5 * rms_norm(f, ffn1_post_g[i])

        z = rms_norm(h, attn_pre_g[i]) @ w_in[i] + b_in[i]
        q_a, k_a, v_a, q_b, k_b, v_b = jnp.split(z, splits, axis=-1)
        out_a = sink_swa_gqa(q_a, k_a, v_a, sinks[i], rel_bias[:, :A_Q_HEADS])
        out_b = dilated_mixture(q_b, k_b, v_b, rel_bias[:, A_Q_HEADS:])
        mix = jnp.concatenate([out_a, out_b], axis=-1).astype(x.dtype)
        att = mix @ w_out[i] + b_out[i]
        h = h + rms_norm(att, attn_post_g[i])

        f = swiglu(rms_norm(h, ffn2_pre_g[i]), ffn2_w_gu[i], ffn2_w_down[i])
        h = h + 0.5 * rms_norm(f, ffn2_post_g[i])

        gate = jax.nn.sigmoid(rms_norm(h, ple_pre_g[i]) @ w_ple_gate[i])
        e = p[i] @ w_ple_proj[i]
        h = h + rms_norm(gate * e, ple_post_g[i])
    return h


import jax as _jax
import jax.numpy as _jnp

TWIN_FORMAT = 'train_step'
FWD_PARAMS = ['x', 'p', 'rel_bias', 'ffn1_pre_g', 'ffn1_w_gu', 'ffn1_w_down', 'ffn1_post_g', 'attn_pre_g', 'w_in', 'b_in', 'sinks', 'w_out', 'b_out', 'attn_post_g', 'ffn2_pre_g', 'ffn2_w_gu', 'ffn2_w_down', 'ffn2_post_g', 'ple_pre_g', 'w_ple_gate', 'w_ple_proj', 'ple_post_g']
TWIN_WEIGHTS = ['rel_bias', 'ffn1_pre_g', 'ffn1_w_gu', 'ffn1_w_down', 'ffn1_post_g', 'attn_pre_g', 'w_in', 'b_in', 'sinks', 'w_out', 'b_out', 'attn_post_g', 'ffn2_pre_g', 'ffn2_w_gu', 'ffn2_w_down', 'ffn2_post_g', 'ple_pre_g', 'w_ple_gate', 'w_ple_proj', 'ple_post_g']
TWIN_DIFF_INPUT = 'x'
TWIN_INPUTS = ['x', 'p', 'rel_bias', 'ffn1_pre_g', 'ffn1_w_gu', 'ffn1_w_down', 'ffn1_post_g', 'attn_pre_g', 'w_in', 'b_in', 'sinks', 'w_out', 'b_out', 'attn_post_g', 'ffn2_pre_g', 'ffn2_w_gu', 'ffn2_w_down', 'ffn2_post_g', 'ple_pre_g', 'w_ple_gate', 'w_ple_proj', 'ple_post_g', 'loss_target', 'm_rel_bias', 'm_ffn1_pre_g', 'm_ffn1_w_gu', 'm_ffn1_w_down', 'm_ffn1_post_g', 'm_attn_pre_g', 'm_w_in', 'm_b_in', 'm_sinks', 'm_w_out', 'm_b_out', 'm_attn_post_g', 'm_ffn2_pre_g', 'm_ffn2_w_gu', 'm_ffn2_w_down', 'm_ffn2_post_g', 'm_ple_pre_g', 'm_w_ple_gate', 'm_w_ple_proj', 'm_ple_post_g', 'v_rel_bias', 'v_ffn1_pre_g', 'v_ffn1_w_gu', 'v_ffn1_w_down', 'v_ffn1_post_g', 'v_attn_pre_g', 'v_w_in', 'v_b_in', 'v_sinks', 'v_w_out', 'v_b_out', 'v_attn_post_g', 'v_ffn2_pre_g', 'v_ffn2_w_gu', 'v_ffn2_w_down', 'v_ffn2_post_g', 'v_ple_pre_g', 'v_w_ple_gate', 'v_w_ple_proj', 'v_ple_post_g']
TWIN_OUTPUTS = ['loss', 'grad_x', 'grad_rel_bias', 'grad_ffn1_pre_g', 'grad_ffn1_w_gu', 'grad_ffn1_w_down', 'grad_ffn1_post_g', 'grad_attn_pre_g', 'grad_w_in', 'grad_b_in', 'grad_sinks', 'grad_w_out', 'grad_b_out', 'grad_attn_post_g', 'grad_ffn2_pre_g', 'grad_ffn2_w_gu', 'grad_ffn2_w_down', 'grad_ffn2_post_g', 'grad_ple_pre_g', 'grad_w_ple_gate', 'grad_w_ple_proj', 'grad_ple_post_g', 'delta_rel_bias', 'delta_ffn1_pre_g', 'delta_ffn1_w_gu', 'delta_ffn1_w_down', 'delta_ffn1_post_g', 'delta_attn_pre_g', 'delta_w_in', 'delta_b_in', 'delta_sinks', 'delta_w_out', 'delta_b_out', 'delta_attn_post_g', 'delta_ffn2_pre_g', 'delta_ffn2_w_gu', 'delta_ffn2_w_down', 'delta_ffn2_post_g', 'delta_ple_pre_g', 'delta_w_ple_gate', 'delta_w_ple_proj', 'delta_ple_post_g', 'new_m_rel_bias', 'new_m_ffn1_pre_g', 'new_m_ffn1_w_gu', 'new_m_ffn1_w_down', 'new_m_ffn1_post_g', 'new_m_attn_pre_g', 'new_m_w_in', 'new_m_b_in', 'new_m_sinks', 'new_m_w_out', 'new_m_b_out', 'new_m_attn_post_g', 'new_m_ffn2_pre_g', 'new_m_ffn2_w_gu', 'new_m_ffn2_w_down', 'new_m_ffn2_post_g', 'new_m_ple_pre_g', 'new_m_w_ple_gate', 'new_m_w_ple_proj', 'new_m_ple_post_g', 'new_v_rel_bias', 'new_v_ffn1_pre_g', 'new_v_ffn1_w_gu', 'new_v_ffn1_w_down', 'new_v_ffn1_post_g', 'new_v_attn_pre_g', 'new_v_w_in', 'new_v_b_in', 'new_v_sinks', 'new_v_w_out', 'new_v_b_out', 'new_v_attn_post_g', 'new_v_ffn2_pre_g', 'new_v_ffn2_w_gu', 'new_v_ffn2_w_down', 'new_v_ffn2_post_g', 'new_v_ple_pre_g', 'new_v_w_ple_gate', 'new_v_w_ple_proj', 'new_v_ple_post_g']
TWIN_LEAF_KINDS = {'loss': 'loss', 'grad_x': 'grad_x', 'grad_rel_bias': 'grad_w', 'grad_ffn1_pre_g': 'grad_w', 'grad_ffn1_w_gu': 'grad_w', 'grad_ffn1_w_down': 'grad_w', 'grad_ffn1_post_g': 'grad_w', 'grad_attn_pre_g': 'grad_w', 'grad_w_in': 'grad_w', 'grad_b_in': 'grad_w', 'grad_sinks': 'grad_w', 'grad_w_out': 'grad_w', 'grad_b_out': 'grad_w', 'grad_attn_post_g': 'grad_w', 'grad_ffn2_pre_g': 'grad_w', 'grad_ffn2_w_gu': 'grad_w', 'grad_ffn2_w_down': 'grad_w', 'grad_ffn2_post_g': 'grad_w', 'grad_ple_pre_g': 'grad_w', 'grad_w_ple_gate': 'grad_w', 'grad_w_ple_proj': 'grad_w', 'grad_ple_post_g': 'grad_w', 'delta_rel_bias': 'delta_w', 'delta_ffn1_pre_g': 'delta_w', 'delta_ffn1_w_gu': 'delta_w', 'delta_ffn1_w_down': 'delta_w', 'delta_ffn1_post_g': 'delta_w', 'delta_attn_pre_g': 'delta_w', 'delta_w_in': 'delta_w', 'delta_b_in': 'delta_w', 'delta_sinks': 'delta_w', 'delta_w_out': 'delta_w', 'delta_b_out': 'delta_w', 'delta_attn_post_g': 'delta_w', 'delta_ffn2_pre_g': 'delta_w', 'delta_ffn2_w_gu': 'delta_w', 'delta_ffn2_w_down': 'delta_w', 'delta_ffn2_post_g': 'delta_w', 'delta_ple_pre_g': 'delta_w', 'delta_w_ple_gate': 'delta_w', 'delta_w_ple_proj': 'delta_w', 'delta_ple_post_g': 'delta_w', 'new_m_rel_bias': 'new_m', 'new_m_ffn1_pre_g': 'new_m', 'new_m_ffn1_w_gu': 'new_m', 'new_m_ffn1_w_down': 'new_m', 'new_m_ffn1_post_g': 'new_m', 'new_m_attn_pre_g': 'new_m', 'new_m_w_in': 'new_m', 'new_m_b_in': 'new_m', 'new_m_sinks': 'new_m', 'new_m_w_out': 'new_m', 'new_m_b_out': 'new_m', 'new_m_attn_post_g': 'new_m', 'new_m_ffn2_pre_g': 'new_m', 'new_m_ffn2_w_gu': 'new_m', 'new_m_ffn2_w_down': 'new_m', 'new_m_ffn2_post_g': 'new_m', 'new_m_ple_pre_g': 'new_m', 'new_m_w_ple_gate': 'new_m', 'new_m_w_ple_proj': 'new_m', 'new_m_ple_post_g': 'new_m', 'new_v_rel_bias': 'new_v', 'new_v_ffn1_pre_g': 'new_v', 'new_v_ffn1_w_gu': 'new_v', 'new_v_ffn1_w_down': 'new_v', 'new_v_ffn1_post_g': 'new_v', 'new_v_attn_pre_g': 'new_v', 'new_v_w_in': 'new_v', 'new_v_b_in': 'new_v', 'new_v_sinks': 'new_v', 'new_v_w_out': 'new_v', 'new_v_b_out': 'new_v', 'new_v_attn_post_g': 'new_v', 'new_v_ffn2_pre_g': 'new_v', 'new_v_ffn2_w_gu': 'new_v', 'new_v_ffn2_w_down': 'new_v', 'new_v_ffn2_post_g': 'new_v', 'new_v_ple_pre_g': 'new_v', 'new_v_w_ple_gate': 'new_v', 'new_v_w_ple_proj': 'new_v', 'new_v_ple_post_g': 'new_v'}


def _forward(args):
    return _fwd_reference(*[args[k] for k in FWD_PARAMS])


def _output_shape():
    out = _jax.eval_shape(lambda: _forward(_fwd_setup_inputs(0)))
    return out.shape, out.dtype

N_MICROBATCH = 1
ADAM_LR = 0.001
ADAM_B1 = 0.9
ADAM_B2 = 0.999
ADAM_EPS = 1e-08
ADAM_WD = 0.01
ADAM_STEP = 10
PER_EXAMPLE_BATCH_AXIS = {'x': 0, 'p': 1, 'loss_target': 0}
SHARED_INPUTS = []
_WEIGHT_DTYPES = {'rel_bias': _jnp.float32, 'ffn1_pre_g': _jnp.float32, 'ffn1_w_gu': _jnp.float32, 'ffn1_w_down': _jnp.float32, 'ffn1_post_g': _jnp.float32, 'attn_pre_g': _jnp.float32, 'w_in': _jnp.float32, 'b_in': _jnp.float32, 'sinks': _jnp.float32, 'w_out': _jnp.float32, 'b_out': _jnp.float32, 'attn_post_g': _jnp.float32, 'ffn2_pre_g': _jnp.float32, 'ffn2_w_gu': _jnp.float32, 'ffn2_w_down': _jnp.float32, 'ffn2_post_g': _jnp.float32, 'ple_pre_g': _jnp.float32, 'w_ple_gate': _jnp.float32, 'w_ple_proj': _jnp.float32, 'ple_post_g': _jnp.float32}
MOMENT_SCALE = {'rel_bias': 4.720895e-01, 'ffn1_pre_g': 5.261369e-01, 'ffn1_w_gu': 2.319955e-01, 'ffn1_w_down': 3.870906e-01, 'ffn1_post_g': 7.786421e+00, 'attn_pre_g': 6.915131e-01, 'w_in': 4.733044e-01, 'b_in': 7.565430e+00, 'sinks': 9.171674e-02, 'w_out': 4.783326e-01, 'b_out': 1.150221e+01, 'attn_post_g': 3.208784e+01, 'ffn2_pre_g': 3.829443e-01, 'ffn2_w_gu': 1.613101e-01, 'ffn2_w_down': 3.032597e-01, 'ffn2_post_g': 8.043658e+00, 'ple_pre_g': 1.536284e-01, 'w_ple_gate': 1.555846e-01, 'w_ple_proj': 3.493765e-01, 'ple_post_g': 3.222236e+01}


def _to_microbatches(a, axis):
    t = _jnp.moveaxis(a, axis, 0)
    t = t.reshape((N_MICROBATCH, t.shape[0] // N_MICROBATCH) + t.shape[1:])
    return _jnp.moveaxis(t, 1, axis + 1)


def setup_inputs(seed: int = 0) -> dict:
    inp = _fwd_setup_inputs(seed)
    key = _jax.random.fold_in(_jax.random.key(seed), 7919)
    shape, _ = _output_shape()
    out = dict(inp)
    out["loss_target"] = _jax.random.normal(_jax.random.fold_in(key, 0), shape, _jnp.float32)
    for i, name in enumerate(TWIN_WEIGHTS):
        w = inp[name].astype(_jnp.float32)
        if MOMENT_SCALE is None:
            s = _jnp.sqrt(_jnp.mean(_jnp.square(w)) + 1e-30)
        else:
            s = MOMENT_SCALE[name]
        km, kv = _jax.random.split(_jax.random.fold_in(key, i + 1))
        out[name] = w
        out["m_" + name] = s * _jax.random.normal(km, w.shape, _jnp.float32)
        out["v_" + name] = (s * s) * _jax.random.uniform(kv, w.shape, _jnp.float32, 0.5, 1.5)
    if N_MICROBATCH > 1:
        for name, axis in PER_EXAMPLE_BATCH_AXIS.items():
            out[name] = _to_microbatches(out[name], axis)
    return {'x': out['x'], 'p': out['p'], 'rel_bias': out['rel_bias'], 'ffn1_pre_g': out['ffn1_pre_g'], 'ffn1_w_gu': out['ffn1_w_gu'], 'ffn1_w_down': out['ffn1_w_down'], 'ffn1_post_g': out['ffn1_post_g'], 'attn_pre_g': out['attn_pre_g'], 'w_in': out['w_in'], 'b_in': out['b_in'], 'sinks': out['sinks'], 'w_out': out['w_out'], 'b_out': out['b_out'], 'attn_post_g': out['attn_post_g'], 'ffn2_pre_g': out['ffn2_pre_g'], 'ffn2_w_gu': out['ffn2_w_gu'], 'ffn2_w_down': out['ffn2_w_down'], 'ffn2_post_g': out['ffn2_post_g'], 'ple_pre_g': out['ple_pre_g'], 'w_ple_gate': out['w_ple_gate'], 'w_ple_proj': out['w_ple_proj'], 'ple_post_g': out['ple_post_g'], 'loss_target': out['loss_target'], 'm_rel_bias': out['m_rel_bias'], 'm_ffn1_pre_g': out['m_ffn1_pre_g'], 'm_ffn1_w_gu': out['m_ffn1_w_gu'], 'm_ffn1_w_down': out['m_ffn1_w_down'], 'm_ffn1_post_g': out['m_ffn1_post_g'], 'm_attn_pre_g': out['m_attn_pre_g'], 'm_w_in': out['m_w_in'], 'm_b_in': out['m_b_in'], 'm_sinks': out['m_sinks'], 'm_w_out': out['m_w_out'], 'm_b_out': out['m_b_out'], 'm_attn_post_g': out['m_attn_post_g'], 'm_ffn2_pre_g': out['m_ffn2_pre_g'], 'm_ffn2_w_gu': out['m_ffn2_w_gu'], 'm_ffn2_w_down': out['m_ffn2_w_down'], 'm_ffn2_post_g': out['m_ffn2_post_g'], 'm_ple_pre_g': out['m_ple_pre_g'], 'm_w_ple_gate': out['m_w_ple_gate'], 'm_w_ple_proj': out['m_w_ple_proj'], 'm_ple_post_g': out['m_ple_post_g'], 'v_rel_bias': out['v_rel_bias'], 'v_ffn1_pre_g': out['v_ffn1_pre_g'], 'v_ffn1_w_gu': out['v_ffn1_w_gu'], 'v_ffn1_w_down': out['v_ffn1_w_down'], 'v_ffn1_post_g': out['v_ffn1_post_g'], 'v_attn_pre_g': out['v_attn_pre_g'], 'v_w_in': out['v_w_in'], 'v_b_in': out['v_b_in'], 'v_sinks': out['v_sinks'], 'v_w_out': out['v_w_out'], 'v_b_out': out['v_b_out'], 'v_attn_post_g': out['v_attn_post_g'], 'v_ffn2_pre_g': out['v_ffn2_pre_g'], 'v_ffn2_w_gu': out['v_ffn2_w_gu'], 'v_ffn2_w_down': out['v_ffn2_w_down'], 'v_ffn2_post_g': out['v_ffn2_post_g'], 'v_ple_pre_g': out['v_ple_pre_g'], 'v_w_ple_gate': out['v_w_ple_gate'], 'v_w_ple_proj': out['v_w_ple_proj'], 'v_ple_post_g': out['v_ple_post_g']}


def _loss(weights, diff, rest, loss_target):
    with _jax.named_scope("forward"):
        args = {**rest, TWIN_DIFF_INPUT: diff, **{k: w.astype(_WEIGHT_DTYPES[k]) for k, w in weights.items()}}
        y = _forward(args)
    with _jax.named_scope("loss_head"):
        err = _jnp.square(y.astype(_jnp.float32) - loss_target)
        return 0.5 * _jnp.sum(_jnp.mean(err, axis=-1)) if err.ndim else 0.5 * err


def _adamw(w, g, m, v):
    m = ADAM_B1 * m + (1.0 - ADAM_B1) * g
    v = ADAM_B2 * v + (1.0 - ADAM_B2) * _jnp.square(g)
    m_hat = m / (1.0 - ADAM_B1 ** ADAM_STEP)
    v_hat = v / (1.0 - ADAM_B2 ** ADAM_STEP)
    delta = -ADAM_LR * (m_hat / (_jnp.sqrt(v_hat) + ADAM_EPS) + ADAM_WD * w)
    return delta, m, v


def reference(x, p, rel_bias, ffn1_pre_g, ffn1_w_gu, ffn1_w_down, ffn1_post_g, attn_pre_g, w_in, b_in, sinks, w_out, b_out, attn_post_g, ffn2_pre_g, ffn2_w_gu, ffn2_w_down, ffn2_post_g, ple_pre_g, w_ple_gate, w_ple_proj, ple_post_g, loss_target, m_rel_bias, m_ffn1_pre_g, m_ffn1_w_gu, m_ffn1_w_down, m_ffn1_post_g, m_attn_pre_g, m_w_in, m_b_in, m_sinks, m_w_out, m_b_out, m_attn_post_g, m_ffn2_pre_g, m_ffn2_w_gu, m_ffn2_w_down, m_ffn2_post_g, m_ple_pre_g, m_w_ple_gate, m_w_ple_proj, m_ple_post_g, v_rel_bias, v_ffn1_pre_g, v_ffn1_w_gu, v_ffn1_w_down, v_ffn1_post_g, v_attn_pre_g, v_w_in, v_b_in, v_sinks, v_w_out, v_b_out, v_attn_post_g, v_ffn2_pre_g, v_ffn2_w_gu, v_ffn2_w_down, v_ffn2_post_g, v_ple_pre_g, v_w_ple_gate, v_w_ple_proj, v_ple_post_g):
    given = dict(x=x, p=p, rel_bias=rel_bias, ffn1_pre_g=ffn1_pre_g, ffn1_w_gu=ffn1_w_gu, ffn1_w_down=ffn1_w_down, ffn1_post_g=ffn1_post_g, attn_pre_g=attn_pre_g, w_in=w_in, b_in=b_in, sinks=sinks, w_out=w_out, b_out=b_out, attn_post_g=attn_post_g, ffn2_pre_g=ffn2_pre_g, ffn2_w_gu=ffn2_w_gu, ffn2_w_down=ffn2_w_down, ffn2_post_g=ffn2_post_g, ple_pre_g=ple_pre_g, w_ple_gate=w_ple_gate, w_ple_proj=w_ple_proj, ple_post_g=ple_post_g, loss_target=loss_target, m_rel_bias=m_rel_bias, m_ffn1_pre_g=m_ffn1_pre_g, m_ffn1_w_gu=m_ffn1_w_gu, m_ffn1_w_down=m_ffn1_w_down, m_ffn1_post_g=m_ffn1_post_g, m_attn_pre_g=m_attn_pre_g, m_w_in=m_w_in, m_b_in=m_b_in, m_sinks=m_sinks, m_w_out=m_w_out, m_b_out=m_b_out, m_attn_post_g=m_attn_post_g, m_ffn2_pre_g=m_ffn2_pre_g, m_ffn2_w_gu=m_ffn2_w_gu, m_ffn2_w_down=m_ffn2_w_down, m_ffn2_post_g=m_ffn2_post_g, m_ple_pre_g=m_ple_pre_g, m_w_ple_gate=m_w_ple_gate, m_w_ple_proj=m_w_ple_proj, m_ple_post_g=m_ple_post_g, v_rel_bias=v_rel_bias, v_ffn1_pre_g=v_ffn1_pre_g, v_ffn1_w_gu=v_ffn1_w_gu, v_ffn1_w_down=v_ffn1_w_down, v_ffn1_post_g=v_ffn1_post_g, v_attn_pre_g=v_attn_pre_g, v_w_in=v_w_in, v_b_in=v_b_in, v_sinks=v_sinks, v_w_out=v_w_out, v_b_out=v_b_out, v_attn_post_g=v_attn_post_g, v_ffn2_pre_g=v_ffn2_pre_g, v_ffn2_w_gu=v_ffn2_w_gu, v_ffn2_w_down=v_ffn2_w_down, v_ffn2_post_g=v_ffn2_post_g, v_ple_pre_g=v_ple_pre_g, v_w_ple_gate=v_w_ple_gate, v_w_ple_proj=v_w_ple_proj, v_ple_post_g=v_ple_post_g)
    weights = {n: given[n] for n in TWIN_WEIGHTS}
    shared = {n: given[n] for n in SHARED_INPUTS}
    per_example = {n: given[n] for n in ['x', 'p']}
    grad_fn = _jax.value_and_grad(_loss, argnums=(0, 1))

    def one_microbatch(ex, loss_target):
        ex = dict(ex)
        diff = ex.pop(TWIN_DIFF_INPUT)
        return grad_fn(weights, diff, {**shared, **ex}, loss_target)

    if N_MICROBATCH == 1:
        loss, (grad_w, grad_x) = one_microbatch(per_example, given["loss_target"])
    else:
        def body(carry, xs):
            loss_sum, grad_sum = carry
            l_k, (gw_k, gx_k) = one_microbatch(xs[0], xs[1])
            with _jax.named_scope("update"):
                return (loss_sum + l_k, _jax.tree.map(_jnp.add, grad_sum, gw_k)), gx_k

        init = (_jnp.zeros((), _jnp.float32), _jax.tree.map(_jnp.zeros_like, weights))
        (loss, grad_w), grad_x = _jax.lax.scan(body, init, (per_example, given["loss_target"]))
    with _jax.named_scope("update"):
        delta_w, new_m, new_v = {}, {}, {}
        for n in TWIN_WEIGHTS:
            delta_w[n], new_m[n], new_v[n] = _adamw(weights[n], grad_w[n], given["m_" + n], given["v_" + n])
    return (loss, grad_x, *[grad_w[n] for n in TWIN_WEIGHTS], *[delta_w[n] for n in TWIN_WEIGHTS],
            *[new_m[n] for n in TWIN_WEIGHTS], *[new_v[n] for n in TWIN_WEIGHTS])
```

```python
import math

import jax
import jax.numpy as jnp
import numpy as np
from jax import lax
from jax.experimental import pallas as pl
from jax.experimental.pallas import tpu as pltpu

F32 = jnp.float32
BF16 = jnp.bfloat16
MESH = pl.DeviceIdType.MESH

EPS = 1e-6
NEG = -1e30
LANES = 128
SUBLANES = 8
HEAD_DIM = 64
QBLK = 128
N_PAIRS = 4
A_WINDOW = 128
B_PATTERNS = ((128, 1), (512, 4), (2048, 16))
NUM_BUCKETS = 32
MAX_DISTANCE = 2048
Z_TILES = 18
ZQ_A, ZQ_B, ZK_B, ZV_B = 0, 6, 10, 14
N_CHIPS = 4
VMEM_BYTES_V7X = 64 * 2**20

ADAM_LR, ADAM_B1, ADAM_B2, ADAM_EPS, ADAM_WD, ADAM_STEP = 0.001, 0.9, 0.999, 1e-08, 0.01, 10

BIG = ("ffn1_w_gu", "ffn1_w_down", "w_in", "w_out", "ffn2_w_gu", "ffn2_w_down", "w_ple_gate", "w_ple_proj")
SMALL = ("rel_bias", "ffn1_pre_g", "ffn1_post_g", "attn_pre_g", "b_in", "sinks", "b_out", "attn_post_g",
         "ffn2_pre_g", "ffn2_post_g", "ple_pre_g", "ple_post_g")
WEIGHTS = ("rel_bias", "ffn1_pre_g", "ffn1_w_gu", "ffn1_w_down", "ffn1_post_g", "attn_pre_g", "w_in", "b_in",
           "sinks", "w_out", "b_out", "attn_post_g", "ffn2_pre_g", "ffn2_w_gu", "ffn2_w_down", "ffn2_post_g",
           "ple_pre_g", "w_ple_gate", "w_ple_proj", "ple_post_g")
PACK_UNIT = SUBLANES * LANES


def _vmem_limit(*block_bytes):
    need = 2 * sum(block_bytes) + max(block_bytes) * 2 + (4 << 20)
    return int(min(max(need, 32 << 20), VMEM_BYTES_V7X - (6 << 20)))


def _nbytes(shape, dtype):
    return int(np.prod(shape)) * jnp.dtype(dtype).itemsize


def _row_block(rows, cap, mult):
    for cand in range(min(rows, cap), 0, -1):
        if rows % cand == 0 and cand % mult == 0:
            return cand
    raise ValueError((rows, cap, mult))


def _mm_nn(a, b, *, bm, bn, out_dtype, name, bias=None):
    M, K = a.shape
    stacked = b.ndim == 3
    N = b.shape[0] * b.shape[2] if stacked else b.shape[1]
    assert M % bm == 0 and N % bn == 0 and (not stacked or bn == b.shape[2])

    def body(*refs):
        a_ref, b_ref = refs[0], refs[1]
        o_ref = refs[-1]
        acc = jnp.dot(a_ref[...], b_ref[...], preferred_element_type=F32)
        if bias is not None:
            acc = acc + refs[2][...]
        o_ref[...] = acc.astype(o_ref.dtype)

    if stacked:
        b_spec = pl.BlockSpec((None, K, bn), lambda i, j: (j, 0, 0))
    else:
        b_spec = pl.BlockSpec((K, bn), lambda i, j: (0, j))
    in_specs = [pl.BlockSpec((bm, K), lambda i, j: (i, 0)), b_spec]
    args = [a, b]
    if bias is not None:
        in_specs.append(pl.BlockSpec((1, bn), lambda i, j: (0, j)))
        args.append(bias)
    limit = _vmem_limit(_nbytes((bm, K), a.dtype), _nbytes((K, bn), b.dtype), _nbytes((bm, bn), F32))
    return pl.pallas_call(
        body, name=name, grid=(M // bm, N // bn), in_specs=in_specs,
        out_specs=pl.BlockSpec((bm, bn), lambda i, j: (i, j)),
        out_shape=jax.ShapeDtypeStruct((M, N), out_dtype),
        compiler_params=pltpu.CompilerParams(dimension_semantics=("parallel", "parallel"), vmem_limit_bytes=limit),
    )(*args)


def _mm_nt(a, b, *, bm, bn, out_dtype, name):
    M, K = a.shape
    stacked = b.ndim == 3
    N = b.shape[1] if stacked else b.shape[0]
    n_sh = b.shape[0] if stacked else 1
    ks = b.shape[2] if stacked else K
    assert M % bm == 0 and N % bn == 0 and n_sh * ks == K
    dn = (((1,), (1,)), ((), ()))

    def body(a_ref, b_ref, o_ref):
        if stacked:
            acc = lax.dot_general(a_ref[:, 0:ks], b_ref[0], dn, preferred_element_type=F32)
            for s in range(1, n_sh):
                acc = acc + lax.dot_general(a_ref[:, s * ks:(s + 1) * ks], b_ref[s], dn, preferred_element_type=F32)
        else:
            acc = lax.dot_general(a_ref[...], b_ref[...], dn, preferred_element_type=F32)
        o_ref[...] = acc.astype(o_ref.dtype)

    if stacked:
        b_spec = pl.BlockSpec((n_sh, bn, ks), lambda i, j: (0, j, 0))
    else:
        b_spec = pl.BlockSpec((bn, K), lambda i, j: (j, 0))
    limit = _vmem_limit(_nbytes((bm, K), a.dtype), _nbytes((bn, K), b.dtype), _nbytes((bm, bn), F32))
    return pl.pallas_call(
        body, name=name, grid=(M // bm, N // bn),
        in_specs=[pl.BlockSpec((bm, K), lambda i, j: (i, 0)), b_spec],
        out_specs=pl.BlockSpec((bm, bn), lambda i, j: (i, j)),
        out_shape=jax.ShapeDtypeStruct((M, N), out_dtype),
        compiler_params=pltpu.CompilerParams(dimension_semantics=("parallel", "parallel"), vmem_limit_bytes=limit),
    )(a, b)


def _mm_tn(a, b, *, bm, bn, out_dtype, name, n_stack=None):
    T, M = a.shape
    N = b.shape[1]
    assert M % bm == 0 and N % bn == 0 and (n_stack is None or bn * n_stack == N)
    dn = (((0,), (0,)), ((), ()))

    def body(a_ref, b_ref, o_ref):
        acc = lax.dot_general(a_ref[...], b_ref[...], dn, preferred_element_type=F32)
        o_ref[...] = acc.astype(o_ref.dtype)

    if n_stack is None:
        out_spec = pl.BlockSpec((bm, bn), lambda i, j: (i, j))
        out_shape = jax.ShapeDtypeStruct((M, N), out_dtype)
    else:
        out_spec = pl.BlockSpec((None, bm, bn), lambda i, j: (j, i, 0))
        out_shape = jax.ShapeDtypeStruct((n_stack, M, bn), out_dtype)
    limit = _vmem_limit(_nbytes((T, bm), a.dtype), _nbytes((T, bn), b.dtype), _nbytes((bm, bn), F32))
    return pl.pallas_call(
        body, name=name, grid=(M // bm, N // bn),
        in_specs=[pl.BlockSpec((T, bm), lambda i, j: (0, i)), pl.BlockSpec((T, bn), lambda i, j: (0, j))],
        out_specs=out_spec, out_shape=out_shape,
        compiler_params=pltpu.CompilerParams(dimension_semantics=("parallel", "parallel"), vmem_limit_bytes=limit),
    )(a, b)


ROW_BLK = 256


def _rstd(x):
    return lax.rsqrt(jnp.mean(x * x, axis=-1, keepdims=True) + EPS)


def _norm_bwd(xhat, rstd, dxhat):
    return rstd * (dxhat - xhat * jnp.mean(dxhat * xhat, axis=-1, keepdims=True))


def _row_spec(cols):
    return pl.BlockSpec((ROW_BLK, cols), lambda i: (i, 0))


def _vec_spec(cols):
    return pl.BlockSpec((1, cols), lambda i: (0, 0))


def _rms_fwd(x, g, *, name):
    T, D = x.shape

    def body(x_ref, g_ref, o_ref):
        xv = x_ref[...]
        o_ref[...] = (xv * _rstd(xv) * g_ref[...]).astype(o_ref.dtype)

    return pl.pallas_call(
        body, name=name, grid=(T // ROW_BLK,), in_specs=[_row_spec(D), _vec_spec(D)], out_specs=_row_spec(D),
        out_shape=jax.ShapeDtypeStruct((T, D), BF16),
        compiler_params=pltpu.CompilerParams(dimension_semantics=("parallel",)),
    )(x, g)


def _resid_norm(h, f, g_post, coef, g_next, *, name):
    T, D = h.shape

    def body(h_ref, f_ref, gp_ref, gn_ref, hn_ref, n_ref):
        fv = f_ref[...]
        hn = h_ref[...] + coef * (fv * _rstd(fv) * gp_ref[...])
        hn_ref[...] = hn
        n_ref[...] = (hn * _rstd(hn) * gn_ref[...]).astype(n_ref.dtype)

    return pl.pallas_call(
        body, name=name, grid=(T // ROW_BLK,),
        in_specs=[_row_spec(D), _row_spec(D), _vec_spec(D), _vec_spec(D)],
        out_specs=(_row_spec(D), _row_spec(D)),
        out_shape=(jax.ShapeDtypeStruct((T, D), F32), jax.ShapeDtypeStruct((T, D), BF16)),
        compiler_params=pltpu.CompilerParams(dimension_semantics=("parallel",)),
    )(h, f, g_post, g_next)


def _swiglu_fwd(gu, *, name):
    T, F2 = gu.shape
    F = F2 // 2

    def body(gu_ref, a_ref):
        g = gu_ref[:, 0:F].astype(F32)
        u = gu_ref[:, F:F2].astype(F32)
        a_ref[...] = (g * jax.nn.sigmoid(g) * u).astype(a_ref.dtype)

    return pl.pallas_call(
        body, name=name, grid=(T // ROW_BLK,), in_specs=[_row_spec(F2)], out_specs=_row_spec(F),
        out_shape=jax.ShapeDtypeStruct((T, F), BF16),
        compiler_params=pltpu.CompilerParams(dimension_semantics=("parallel",)),
    )(gu)


def _swiglu_bwd(gu, da, *, name):
    T, F2 = gu.shape
    F = F2 // 2

    def body(gu_ref, da_ref, o_ref):
        g = gu_ref[:, 0:F].astype(F32)
        u = gu_ref[:, F:F2].astype(F32)
        dav = da_ref[...].astype(F32)
        s = jax.nn.sigmoid(g)
        o_ref[:, 0:F] = (dav * u * (s * (1.0 + g * (1.0 - s)))).astype(o_ref.dtype)
        o_ref[:, F:F2] = (dav * (g * s)).astype(o_ref.dtype)

    return pl.pallas_call(
        body, name=name, grid=(T // ROW_BLK,), in_specs=[_row_spec(F2), _row_spec(F)], out_specs=_row_spec(F2),
        out_shape=jax.ShapeDtypeStruct((T, F2), BF16),
        compiler_params=pltpu.CompilerParams(dimension_semantics=("parallel",)),
    )(gu, da)


def _ple_loss(h3, gp, e, g_post, target, *, name):
    T, D = h3.shape

    def body(h_ref, gp_ref, e_ref, g_ref, t_ref, dh_ref, de_ref, dgp_ref, loss_ref, dg_ref):
        @pl.when(pl.program_id(0) == 0)
        def _():
            loss_ref[...] = jnp.zeros_like(loss_ref)
            dg_ref[...] = jnp.zeros_like(dg_ref)

        gate = jax.nn.sigmoid(gp_ref[...])
        ev = e_ref[...]
        ge = gate * ev
        r = _rstd(ge)
        gh = ge * r
        gpost = g_ref[...]
        diff = h_ref[...] + gh * gpost - t_ref[...]
        loss_ref[...] += jnp.sum(diff * diff) * (0.5 / D)
        dh = diff * (1.0 / D)
        dh_ref[...] = dh
        dg_ref[...] += jnp.sum(dh * gh, axis=0, keepdims=True)
        dge = _norm_bwd(gh, r, dh * gpost)
        de_ref[...] = (dge * gate).astype(de_ref.dtype)
        dgp_ref[...] = (dge * ev * gate * (1.0 - gate)).astype(dgp_ref.dtype)

    return pl.pallas_call(
        body, name=name, grid=(T // ROW_BLK,),
        in_specs=[_row_spec(D), _row_spec(D), _row_spec(D), _vec_spec(D), _row_spec(D)],
        out_specs=(_row_spec(D), _row_spec(D), _row_spec(D), _vec_spec(LANES), _vec_spec(D)),
        out_shape=(jax.ShapeDtypeStruct((T, D), F32), jax.ShapeDtypeStruct((T, D), BF16),
                   jax.ShapeDtypeStruct((T, D), BF16), jax.ShapeDtypeStruct((1, LANES), F32),
                   jax.ShapeDtypeStruct((1, D), F32)),
        compiler_params=pltpu.CompilerParams(dimension_semantics=("arbitrary",)),
    )(h3, gp, e, g_post, target)


def _norms_bwd(dh_in, h, g_pre, dn, f, g_post, coef, *, name):
    T, D = h.shape

    def body(dhi_ref, h_ref, gpre_ref, dn_ref, f_ref, gpost_ref, dh_ref, df_ref, dgpre_ref, dgpost_ref, dsum_ref):
        @pl.when(pl.program_id(0) == 0)
        def _():
            dgpre_ref[...] = jnp.zeros_like(dgpre_ref)
            dgpost_ref[...] = jnp.zeros_like(dgpost_ref)
            dsum_ref[...] = jnp.zeros_like(dsum_ref)

        hv = h_ref[...]
        r = _rstd(hv)
        hh = hv * r
        dnv = dn_ref[...]
        dgpre_ref[...] += jnp.sum(dnv * hh, axis=0, keepdims=True)
        dh = dhi_ref[...] + _norm_bwd(hh, r, dnv * gpre_ref[...])
        dh_ref[...] = dh
        fv = f_ref[...]
        rf = _rstd(fv)
        fh = fv * rf
        dy = coef * dh
        dgpost_ref[...] += jnp.sum(dy * fh, axis=0, keepdims=True)
        df = _norm_bwd(fh, rf, dy * gpost_ref[...])
        dsum_ref[...] += jnp.sum(df, axis=0, keepdims=True)
        df_ref[...] = df.astype(df_ref.dtype)

    return pl.pallas_call(
        body, name=name, grid=(T // ROW_BLK,),
        in_specs=[_row_spec(D), _row_spec(D), _vec_spec(D), _row_spec(D), _row_spec(D), _vec_spec(D)],
        out_specs=(_row_spec(D), _row_spec(D), _vec_spec(D), _vec_spec(D), _vec_spec(D)),
        out_shape=(jax.ShapeDtypeStruct((T, D), F32), jax.ShapeDtypeStruct((T, D), BF16),
                   jax.ShapeDtypeStruct((1, D), F32), jax.ShapeDtypeStruct((1, D), F32),
                   jax.ShapeDtypeStruct((1, D), F32)),
        compiler_params=pltpu.CompilerParams(dimension_semantics=("arbitrary",)),
    )(dh_in, h, g_pre, dn, f, g_post)


def _norm_bwd_last(dh_in, h, g_pre, dn, *, name):
    T, D = h.shape

    def body(dhi_ref, h_ref, gpre_ref, dn_ref, dh_ref, dgpre_ref):
        @pl.when(pl.program_id(0) == 0)
        def _():
            dgpre_ref[...] = jnp.zeros_like(dgpre_ref)

        hv = h_ref[...]
        r = _rstd(hv)
        hh = hv * r
        dnv = dn_ref[...]
        dgpre_ref[...] += jnp.sum(dnv * hh, axis=0, keepdims=True)
        dh_ref[...] = dhi_ref[...] + _norm_bwd(hh, r, dnv * gpre_ref[...])

    return pl.pallas_call(
        body, name=name, grid=(T // ROW_BLK,),
        in_specs=[_row_spec(D), _row_spec(D), _vec_spec(D), _row_spec(D)],
        out_specs=(_row_spec(D), _vec_spec(D)),
        out_shape=(jax.ShapeDtypeStruct((T, D), F32), jax.ShapeDtypeStruct((1, D), F32)),
        compiler_params=pltpu.CompilerParams(dimension_semantics=("arbitrary",)),
    )(dh_in, h, g_pre, dn)


def _t5_index(max_dist, stride):
    rho = np.arange(QBLK)[:, None]
    kap = np.arange(2 * QBLK)[None, :]
    d = rho + QBLK - kap
    valid = (d >= 0) & (d <= max_dist)
    n = np.maximum(d, 0) * stride
    max_exact = NUM_BUCKETS // 2
    nf = np.maximum(n, 1).astype(np.float32)
    large = max_exact + (np.log(nf / np.float32(max_exact)) / np.float32(math.log(MAX_DISTANCE / max_exact))
                         * np.float32(NUM_BUCKETS - max_exact)).astype(np.int32)
    large = np.minimum(large, NUM_BUCKETS - 1)
    bucket = np.where(n < max_exact, n, large)
    return np.where(valid, bucket, -1).astype(np.int32)


def _bias_tiles(rel_bias, idx, head_off, *, name):
    def body(rb_ref, idx_ref, o_ref):
        h = pl.program_id(0)
        idxv = idx_ref[...]
        acc = jnp.where(idxv < 0, NEG, 0.0).astype(F32)
        for b in range(NUM_BUCKETS):
            acc = acc + jnp.where(idxv == b, rb_ref[b, head_off + h], 0.0)
        o_ref[0] = acc

    return pl.pallas_call(
        body, name=name, grid=(2 * N_PAIRS,),
        in_specs=[pl.BlockSpec(memory_space=pltpu.SMEM), pl.BlockSpec((QBLK, 2 * QBLK), lambda h: (0, 0))],
        out_specs=pl.BlockSpec((1, QBLK, 2 * QBLK), lambda h: (h, 0, 0)),
        out_shape=jax.ShapeDtypeStruct((2 * N_PAIRS, QBLK, 2 * QBLK), F32),
        compiler_params=pltpu.CompilerParams(dimension_semantics=("parallel",)),
    )(rel_bias, idx)


def _bias_grad(d_bias, idx, *, name):
    def body(db_ref, idx_ref, o_ref):
        h = pl.program_id(0)

        @pl.when(h == 0)
        def _():
            o_ref[...] = jnp.zeros_like(o_ref)

        idxv = idx_ref[...]
        dv = db_ref[0]
        rows = lax.broadcasted_iota(jnp.int32, (NUM_BUCKETS, LANES), 0)
        lanes = lax.broadcasted_iota(jnp.int32, (NUM_BUCKETS, LANES), 1)
        acc = o_ref[...]
        for b in range(NUM_BUCKETS):
            s = jnp.sum(jnp.where(idxv == b, dv, 0.0))
            acc = acc + jnp.where((rows == b) & (lanes == h), s, 0.0)
        o_ref[...] = acc

    return pl.pallas_call(
        body, name=name, grid=(2 * N_PAIRS,),
        in_specs=[pl.BlockSpec((1, QBLK, 2 * QBLK), lambda h: (h, 0, 0)),
                  pl.BlockSpec((QBLK, 2 * QBLK), lambda h: (0, 0))],
        out_specs=pl.BlockSpec((NUM_BUCKETS, LANES), lambda h: (0, 0)),
        out_shape=jax.ShapeDtypeStruct((NUM_BUCKETS, LANES), F32),
        compiler_params=pltpu.CompilerParams(dimension_semantics=("arbitrary",)),
    )(d_bias, idx)


def _head_sel(h):
    lane = lax.broadcasted_iota(jnp.int32, (1, LANES), 1)
    return (lane < HEAD_DIM) if h == 0 else (lane >= HEAD_DIM)


def _first_block_mask(b):
    kap = lax.broadcasted_iota(jnp.int32, (1, 2 * QBLK), 1)
    return jnp.where((kap < QBLK) & (b == 0), NEG, 0.0).astype(F32)


def _band_fwd(qa, ka, va, bias, *, r, q_off, k_off, v_off, q_w, kv_w, name):
    L = qa.shape[0]
    nb = L // QBLK
    dn = (((1,), (1,)), ((), ()))

    def body(q_ref, kp_ref, kc_ref, vp_ref, vc_ref, bias_ref, o_ref, m_ref, l_ref):
        b = pl.program_id(2)
        q = q_ref[...]
        k = jnp.concatenate([kp_ref[...], kc_ref[...]], axis=0)
        v = jnp.concatenate([vp_ref[...], vc_ref[...]], axis=0)
        first = _first_block_mask(b)
        outs = []
        for h in range(2):
            sel = _head_sel(h)
            qh = jnp.where(sel, q, jnp.zeros_like(q))
            s = lax.dot_general(qh, k, dn, preferred_element_type=F32) * (HEAD_DIM ** -0.5) + bias_ref[h] + first
            m = jnp.max(s, axis=1, keepdims=True)
            p = jnp.exp(s - m)
            l = jnp.sum(p, axis=1, keepdims=True)
            o = jnp.dot(p.astype(v.dtype), v, preferred_element_type=F32)
            outs.append((o, m, l))
        sel0 = _head_sel(0)
        o_ref[...] = jnp.where(sel0, outs[0][0], outs[1][0])
        m_ref[...] = jnp.where(sel0, outs[0][1], outs[1][1])
        l_ref[...] = jnp.where(sel0, outs[0][2], outs[1][2])

    tile = (QBLK, LANES)
    q_spec = pl.BlockSpec(tile, lambda j, i, b: (b, q_off + j * q_w + i))
    kp_spec = pl.BlockSpec(tile, lambda j, i, b: (jnp.maximum(b - 1, 0), k_off + j * kv_w + i))
    kc_spec = pl.BlockSpec(tile, lambda j, i, b: (b, k_off + j * kv_w + i))
    vp_spec = pl.BlockSpec(tile, lambda j, i, b: (jnp.maximum(b - 1, 0), v_off + j * kv_w + i))
    vc_spec = pl.BlockSpec(tile, lambda j, i, b: (b, v_off + j * kv_w + i))
    bias_spec = pl.BlockSpec((2, QBLK, 2 * QBLK), lambda j, i, b: (i, 0, 0))
    out_spec = pl.BlockSpec(tile, lambda j, i, b: (b, j * N_PAIRS + i))
    out = jax.ShapeDtypeStruct((L, r * N_PAIRS * LANES), F32)
    return pl.pallas_call(
        body, name=name, grid=(r, N_PAIRS, nb),
        in_specs=[q_spec, kp_spec, kc_spec, vp_spec, vc_spec, bias_spec],
        out_specs=(out_spec, out_spec, out_spec), out_shape=(out, out, out),
        compiler_params=pltpu.CompilerParams(dimension_semantics=("parallel", "parallel", "parallel")),
    )(qa, ka, ka, va, va, bias)


def _merge(branches, sink, *, name):
    T, W = branches[0][0].shape
    nbr = len(branches)

    def body(*refs):
        sink_ref = refs[3 * nbr]
        out_ref, outb_ref, lse_ref = refs[3 * nbr + 1:]
        sk = sink_ref[...]
        mx = sk
        for t in range(nbr):
            mx = jnp.maximum(mx, refs[3 * t + 1][...])
        den = jnp.exp(sk - mx)
        num = jnp.zeros_like(mx)
        for t in range(nbr):
            w = jnp.exp(refs[3 * t + 1][...] - mx)
            den = den + w * refs[3 * t + 2][...]
            num = num + w * refs[3 * t][...]
        out = num / den
        out_ref[...] = out
        outb_ref[...] = out.astype(outb_ref.dtype)
        lse_ref[...] = mx + jnp.log(den)

    args = [a for br in branches for a in br] + [sink]
    return pl.pallas_call(
        body, name=name, grid=(T // ROW_BLK,),
        in_specs=[_row_spec(W)] * (3 * nbr) + [_vec_spec(W)],
        out_specs=(_row_spec(W), _row_spec(W), _row_spec(W)),
        out_shape=(jax.ShapeDtypeStruct((T, W), F32), jax.ShapeDtypeStruct((T, W), BF16),
                   jax.ShapeDtypeStruct((T, W), F32)),
        compiler_params=pltpu.CompilerParams(dimension_semantics=("parallel",)),
    )(*args)


def _band_bwd(qa, ka, va, d_out, out, lse, bias, sink, *, r, q_off, k_off, v_off, q_w, kv_w, name):
    L = qa.shape[0]
    nb = L // QBLK
    dn_nt = (((1,), (1,)), ((), ()))
    dn_tn = (((0,), (0,)), ((), ()))
    scale = HEAD_DIM ** -0.5

    def body(q_ref, kp_ref, kc_ref, vp_ref, vc_ref, do_ref, out_ref, lse_ref, bias_ref, sink_ref,
             dq_ref, dk_ref, dv_ref, db_ref, dsink_ref, dk_carry, dv_carry):
        j = pl.program_id(1)
        b = pl.program_id(2)

        @pl.when((j == 0) & (b == 0))
        def _():
            db_ref[...] = jnp.zeros_like(db_ref)
            dsink_ref[...] = jnp.zeros_like(dsink_ref)

        @pl.when(b < nb)
        def _():
            q = q_ref[...]
            k = jnp.concatenate([kp_ref[...], kc_ref[...]], axis=0)
            v = jnp.concatenate([vp_ref[...], vc_ref[...]], axis=0)
            do = do_ref[...]
            prod = do * out_ref[...]
            lse_t = lse_ref[...]
            sink_t = sink_ref[...]
            first = _first_block_mask(b)
            dq_acc = jnp.zeros((QBLK, LANES), F32)
            dk_acc = jnp.zeros((2 * QBLK, LANES), F32)
            dv_acc = jnp.zeros((2 * QBLK, LANES), F32)
            dsink_acc = jnp.zeros((1, LANES), F32)
            for h in range(2):
                sel = _head_sel(h)
                qh = jnp.where(sel, q, jnp.zeros_like(q))
                doh = jnp.where(sel, do, 0.0).astype(q.dtype)
                delta = jnp.sum(jnp.where(sel, prod, 0.0), axis=1, keepdims=True)
                lse_h = jnp.max(jnp.where(sel, lse_t, -jnp.inf), axis=1, keepdims=True)
                sink_h = jnp.max(jnp.where(sel, sink_t, -jnp.inf), axis=1, keepdims=True)
                s = lax.dot_general(qh, k, dn_nt, preferred_element_type=F32) * scale + bias_ref[h] + first
                p = jnp.exp(s - lse_h)
                dp = lax.dot_general(doh, v, dn_nt, preferred_element_type=F32)
                ds = p * (dp - delta)
                db_ref[h] += ds
                dsink_h = -jnp.sum(jnp.exp(sink_h - lse_h) * delta)
                dsink_acc = dsink_acc + jnp.where(sel, dsink_h, 0.0)
                dsb = ds.astype(q.dtype)
                dq_h = jnp.dot(dsb, k, preferred_element_type=F32)
                dq_acc = jnp.where(sel, dq_h, dq_acc)
                dk_acc = dk_acc + lax.dot_general(dsb, qh, dn_tn, preferred_element_type=F32)
                dv_acc = dv_acc + lax.dot_general(p.astype(q.dtype), doh, dn_tn, preferred_element_type=F32)
            dq_ref[...] = dq_acc * scale
            dsink_ref[...] += jnp.broadcast_to(dsink_acc, (SUBLANES, LANES))
            dk_acc = dk_acc * scale

            @pl.when(b >= 1)
            def _():
                dk_ref[...] = dk_carry[...] + dk_acc[0:QBLK]
                dv_ref[...] = dv_carry[...] + dv_acc[0:QBLK]

            dk_carry[...] = dk_acc[QBLK:2 * QBLK]
            dv_carry[...] = dv_acc[QBLK:2 * QBLK]

        @pl.when(b == nb)
        def _():
            dk_ref[...] = dk_carry[...]
            dv_ref[...] = dv_carry[...]

    def qb(b):
        return jnp.minimum(b, nb - 1)

    def pb(b):
        return jnp.maximum(qb(b) - 1, 0)

    tile = (QBLK, LANES)
    q_spec = pl.BlockSpec(tile, lambda i, j, b: (qb(b), q_off + j * q_w + i))
    kp_spec = pl.BlockSpec(tile, lambda i, j, b: (pb(b), k_off + j * kv_w + i))
    kc_spec = pl.BlockSpec(tile, lambda i, j, b: (qb(b), k_off + j * kv_w + i))
    vp_spec = pl.BlockSpec(tile, lambda i, j, b: (pb(b), v_off + j * kv_w + i))
    vc_spec = pl.BlockSpec(tile, lambda i, j, b: (qb(b), v_off + j * kv_w + i))
    tok_spec = pl.BlockSpec(tile, lambda i, j, b: (qb(b), j * N_PAIRS + i))
    bias_spec = pl.BlockSpec((2, QBLK, 2 * QBLK), lambda i, j, b: (i, 0, 0))
    sink_spec = pl.BlockSpec((1, LANES), lambda i, j, b: (0, i))
    dkv_spec = pl.BlockSpec(tile, lambda i, j, b: (jnp.maximum(b - 1, 0), j * N_PAIRS + i))
    dsink_spec = pl.BlockSpec((None, SUBLANES, LANES), lambda i, j, b: (i, 0, 0))
    grad = jax.ShapeDtypeStruct((L, r * N_PAIRS * LANES), F32)
    return pl.pallas_call(
        body, name=name, grid=(N_PAIRS, r, nb + 1),
        in_specs=[q_spec, kp_spec, kc_spec, vp_spec, vc_spec, tok_spec, tok_spec, tok_spec, bias_spec, sink_spec],
        out_specs=(tok_spec, dkv_spec, dkv_spec, bias_spec, dsink_spec),
        out_shape=(grad, grad, grad, jax.ShapeDtypeStruct((2 * N_PAIRS, QBLK, 2 * QBLK), F32),
                   jax.ShapeDtypeStruct((N_PAIRS, SUBLANES, LANES), F32)),
        scratch_shapes=[pltpu.VMEM(tile, F32), pltpu.VMEM(tile, F32)],
        compiler_params=pltpu.CompilerParams(dimension_semantics=("arbitrary", "arbitrary", "arbitrary")),
    )(qa, ka, ka, va, va, d_out, out, lse, bias, sink)


def _dz_assemble(dq_a, dk_a, dv_a, b_grads, *, name):
    T = dq_a.shape[0]
    W = N_PAIRS * LANES
    DZ = Z_TILES * LANES

    def group_sum(x):
        u0 = x[:, 0:LANES] + x[:, LANES:2 * LANES]
        u1 = x[:, 2 * LANES:3 * LANES] + x[:, 3 * LANES:4 * LANES]
        s0 = u0 + pltpu.roll(u0, HEAD_DIM, 1)
        s1 = u1 + pltpu.roll(u1, HEAD_DIM, 1)
        return jnp.where(_head_sel(0), s0, s1)

    def body(*refs):
        dqa_ref, dka_ref, dva_ref = refs[0:3]
        b_refs = refs[3:12]
        dz_ref, dsum_ref = refs[12:]

        @pl.when(pl.program_id(0) == 0)
        def _():
            dsum_ref[...] = jnp.zeros_like(dsum_ref)

        parts = [dqa_ref[...], group_sum(dka_ref[...]), group_sum(dva_ref[...])]
        for t in range(3):
            parts.append(b_refs[t][...] + b_refs[3 + t][...] + b_refs[6 + t][...])
        col = 0
        for part in parts:
            w = part.shape[1]
            dz_ref[:, col:col + w] = part.astype(dz_ref.dtype)
            dsum_ref[:, col:col + w] += jnp.sum(part, axis=0, keepdims=True)
            col += w

    args = [dq_a, dk_a, dv_a] + [g[t] for g in b_grads for t in range(3)]
    return pl.pallas_call(
        body, name=name, grid=(T // ROW_BLK,),
        in_specs=[_row_spec(W)] * 12,
        out_specs=(_row_spec(DZ), _vec_spec(DZ)),
        out_shape=(jax.ShapeDtypeStruct((T, DZ), BF16), jax.ShapeDtypeStruct((1, DZ), F32)),
        compiler_params=pltpu.CompilerParams(dimension_semantics=("arbitrary",)),
    )(*args)


ANY = pl.BlockSpec(memory_space=pl.ANY)


def _place():
    x, y, c = lax.axis_index("x"), lax.axis_index("y"), lax.axis_index("c")
    chips = [(1 - x, y), (x, 1 - y), (1 - x, 1 - y)]
    return x, y, c, chips


def _all_gather_weights(shards, *, name):
    n = len(shards)

    def body(*refs):
        sh = refs[0:n]
        full = refs[n:2 * n]
        send_sems, recv_sems, local_sems = refs[2 * n:]
        x, y, c, chips = _place()
        k = 2 * x + y
        sibling = (x, y, 1 - c)

        def piece(i, chip_idx, half):
            rs = sh[i].shape[0]
            rh = rs // 2
            return full[i].at[pl.ds(chip_idx * rs + half * rh, rh)]

        def copy(sem, src, dst, to):
            return pltpu.make_async_remote_copy(src_ref=src, dst_ref=dst, send_sem=send_sems.at[sem],
                                                recv_sem=recv_sems.at[sem], device_id=to, device_id_type=MESH)

        local = []
        for i in range(n):
            rs = sh[i].shape[0]
            cp = pltpu.make_async_copy(sh[i], full[i].at[pl.ds(k * rs, rs)], local_sems.at[i])
            cp.start()
            local.append(cp)
        started = []
        for i in range(n):
            rh = sh[i].shape[0] // 2
            for j, chip in enumerate(chips):
                cp = copy(3 * i + j, sh[i].at[pl.ds(c * rh, rh)], piece(i, k, c), (*chip, c))
                cp.start()
                started.append(cp)
        for i in range(n):
            for j, chip in enumerate(chips):
                kj = 2 * chip[0] + chip[1]
                region = piece(i, kj, c)
                copy(3 * i + j, region, region, (*chip, c)).wait_recv()
                cp = copy(3 * n + 3 * i + j, region, region, sibling)
                cp.start()
                started.append(cp)
        for i in range(n):
            for j, chip in enumerate(chips):
                kj = 2 * chip[0] + chip[1]
                region = piece(i, kj, 1 - c)
                copy(3 * n + 3 * i + j, region, region, sibling).wait_recv()
        for cp in started:
            cp.wait_send()
        for cp in local:
            cp.wait()

    return pl.pallas_call(
        body, name=name, in_specs=[ANY] * n, out_specs=tuple([ANY] * n),
        out_shape=tuple(jax.ShapeDtypeStruct((N_CHIPS * s.shape[0], s.shape[1]), s.dtype) for s in shards),
        scratch_shapes=[pltpu.SemaphoreType.DMA((6 * n,)), pltpu.SemaphoreType.DMA((6 * n,)),
                        pltpu.SemaphoreType.DMA((n,))],
    )(*shards)


def _pair_swap(grads, *, name):
    n = len(grads)

    def body(*refs):
        g = refs[0:n]
        got = refs[n:2 * n]
        send_sems, recv_sems = refs[2 * n:]
        x, y, c, _ = _place()
        copies = []
        for i in range(n):
            cp = pltpu.make_async_remote_copy(src_ref=g[i].at[:, 1 - c], dst_ref=got[i], send_sem=send_sems.at[i],
                                              recv_sem=recv_sems.at[i], device_id=(x, y, 1 - c), device_id_type=MESH)
            cp.start()
            copies.append(cp)
        for cp in copies:
            cp.wait()

    return pl.pallas_call(
        body, name=name, in_specs=[ANY] * n, out_specs=tuple([ANY] * n),
        out_shape=tuple(jax.ShapeDtypeStruct((a.shape[0], a.shape[2], a.shape[3]), a.dtype) for a in grads),
        scratch_shapes=[pltpu.SemaphoreType.DMA((n,)), pltpu.SemaphoreType.DMA((n,))],
    )(*grads)


def _pair_sum(g4, got, c_idx, *, name):
    _, _, rh, cs = g4.shape
    rb = _row_block(rh, 256, 16)

    def body(c_ref, own_ref, got_ref, o_ref):
        o_ref[...] = (own_ref[...].astype(F32) + got_ref[...].astype(F32)).astype(o_ref.dtype)

    grid_spec = pltpu.PrefetchScalarGridSpec(
        num_scalar_prefetch=1, grid=(N_CHIPS, rh // rb),
        in_specs=[pl.BlockSpec((None, None, rb, cs), lambda k, t, c_ref: (k, c_ref[0], t, 0)),
                  pl.BlockSpec((None, rb, cs), lambda k, t, c_ref: (k, t, 0))],
        out_specs=pl.BlockSpec((None, rb, cs), lambda k, t, c_ref: (k, t, 0)))
    return pl.pallas_call(
        body, name=name, grid_spec=grid_spec, out_shape=jax.ShapeDtypeStruct((N_CHIPS, rh, cs), BF16),
        compiler_params=pltpu.CompilerParams(dimension_semantics=("parallel", "parallel")),
    )(c_idx, g4, got)


def _chip_scatter(sums, *, name):
    n = len(sums)

    def body(*refs):
        s = refs[0:n]
        got = refs[n:2 * n]
        send_sems, recv_sems = refs[2 * n:]
        x, y, c, chips = _place()
        copies = []
        for i in range(n):
            for j, chip in enumerate(chips):
                kj = 2 * chip[0] + chip[1]
                cp = pltpu.make_async_remote_copy(src_ref=s[i].at[kj], dst_ref=got[i].at[j],
                                                  send_sem=send_sems.at[3 * i + j], recv_sem=recv_sems.at[3 * i + j],
                                                  device_id=(*chip, c), device_id_type=MESH)
                cp.start()
                copies.append(cp)
        for cp in copies:
            cp.wait()

    return pl.pallas_call(
        body, name=name, in_specs=[ANY] * n, out_specs=tuple([ANY] * n),
        out_shape=tuple(jax.ShapeDtypeStruct((3,) + a.shape[1:], a.dtype) for a in sums),
        scratch_shapes=[pltpu.SemaphoreType.DMA((3 * n,)), pltpu.SemaphoreType.DMA((3 * n,))],
    )(*sums)


def _chip_sum(sums, got, k_idx, *, name):
    _, rh, cs = sums.shape
    rb = _row_block(rh, 256, 16)

    def body(k_ref, own_ref, got_ref, o_ref):
        acc = own_ref[...].astype(F32)
        for j in range(3):
            acc = acc + got_ref[j].astype(F32)
        o_ref[...] = acc

    grid_spec = pltpu.PrefetchScalarGridSpec(
        num_scalar_prefetch=1, grid=(rh // rb,),
        in_specs=[pl.BlockSpec((None, rb, cs), lambda t, k_ref: (k_ref[0], t, 0)),
                  pl.BlockSpec((3, rb, cs), lambda t, k_ref: (0, t, 0))],
        out_specs=pl.BlockSpec((rb, cs), lambda t, k_ref: (t, 0)))
    return pl.pallas_call(
        body, name=name, grid_spec=grid_spec, out_shape=jax.ShapeDtypeStruct((rh, cs), F32),
        compiler_params=pltpu.CompilerParams(dimension_semantics=("parallel",)),
    )(k_idx, sums, got)


def _half_swap(halves, *, name):
    n = len(halves)

    def body(*refs):
        hv = refs[0:n]
        out = refs[n:2 * n]
        send_sems, recv_sems, local_sems = refs[2 * n:]
        x, y, c, _ = _place()
        copies, local = [], []
        for i in range(n):
            rh = hv[i].shape[0]
            mine = out[i].at[pl.ds(c * rh, rh)]
            lc = pltpu.make_async_copy(hv[i], mine, local_sems.at[i])
            lc.start()
            local.append(lc)
            cp = pltpu.make_async_remote_copy(src_ref=hv[i], dst_ref=mine, send_sem=send_sems.at[i],
                                              recv_sem=recv_sems.at[i], device_id=(x, y, 1 - c), device_id_type=MESH)
            cp.start()
            copies.append(cp)
        for i in range(n):
            rh = hv[i].shape[0]
            theirs = out[i].at[pl.ds((1 - c) * rh, rh)]
            pltpu.make_async_remote_copy(src_ref=hv[i], dst_ref=theirs, send_sem=send_sems.at[i],
                                         recv_sem=recv_sems.at[i], device_id=(x, y, 1 - c),
                                         device_id_type=MESH).wait_recv()
        for cp in copies:
            cp.wait_send()
        for lc in local:
            lc.wait()

    return pl.pallas_call(
        body, name=name, in_specs=[ANY] * n, out_specs=tuple([ANY] * n),
        out_shape=tuple(jax.ShapeDtypeStruct((2 * a.shape[0], a.shape[1]), a.dtype) for a in halves),
        scratch_shapes=[pltpu.SemaphoreType.DMA((n,)), pltpu.SemaphoreType.DMA((n,)), pltpu.SemaphoreType.DMA((n,))],
    )(*halves)


def _small_all_reduce(block, *, name):
    rows = block.shape[0]

    def body(x_ref, o_ref, buf, send_sems, recv_sems):
        x, y, c, _ = _place()
        me = 4 * x + 2 * y + c
        buf[me] = x_ref[...]
        copies = []
        for d in range(1, 8):
            to = (1 - x if d & 4 else x, 1 - y if d & 2 else y, 1 - c if d & 1 else c)
            cp = pltpu.make_async_remote_copy(src_ref=x_ref, dst_ref=buf.at[me], send_sem=send_sems.at[d - 1],
                                              recv_sem=recv_sems.at[d - 1], device_id=to, device_id_type=MESH)
            cp.start()
            copies.append((cp, to))
        for d in range(1, 8):
            to = copies[d - 1][1]
            sender = 4 * to[0] + 2 * to[1] + to[2]
            pltpu.make_async_remote_copy(src_ref=x_ref, dst_ref=buf.at[sender], send_sem=send_sems.at[d - 1],
                                         recv_sem=recv_sems.at[d - 1], device_id=to, device_id_type=MESH).wait_recv()
        for cp, _ in copies:
            cp.wait_send()
        acc = buf[0]
        for s in range(1, 8):
            acc = acc + buf[s]
        o_ref[...] = acc

    vm = pl.BlockSpec(memory_space=pltpu.VMEM)
    return pl.pallas_call(
        body, name=name, in_specs=[vm], out_specs=vm, out_shape=jax.ShapeDtypeStruct(block.shape, F32),
        scratch_shapes=[pltpu.VMEM((8, rows, LANES), F32), pltpu.SemaphoreType.DMA((7,)),
                        pltpu.SemaphoreType.DMA((7,))],
    )(block)


def _adamw(w, g, m, v, *, name):
    R, C = w.shape
    rb = _row_block(R, 256, SUBLANES)
    c1 = 1.0 - ADAM_B1 ** ADAM_STEP
    c2 = 1.0 - ADAM_B2 ** ADAM_STEP

    def body(w_ref, g_ref, m_ref, v_ref, d_ref, mo_ref, vo_ref):
        gv = g_ref[...]
        mn = ADAM_B1 * m_ref[...] + (1.0 - ADAM_B1) * gv
        vn = ADAM_B2 * v_ref[...] + (1.0 - ADAM_B2) * (gv * gv)
        d_ref[...] = -ADAM_LR * ((mn / c1) / (jnp.sqrt(vn / c2) + ADAM_EPS) + ADAM_WD * w_ref[...])
        mo_ref[...] = mn
        vo_ref[...] = vn

    spec = pl.BlockSpec((rb, C), lambda i: (i, 0))
    shp = jax.ShapeDtypeStruct((R, C), F32)
    return pl.pallas_call(
        body, name=name, grid=(R // rb,), in_specs=[spec] * 4, out_specs=(spec, spec, spec),
        out_shape=(shp, shp, shp), compiler_params=pltpu.CompilerParams(dimension_semantics=("parallel",)),
    )(w, g, m, v)


def _pack(parts):
    rows = []
    for a in parts:
        flat = a.reshape(-1).astype(F32)
        pad = (-flat.shape[0]) % PACK_UNIT
        rows.append(jnp.pad(flat, (0, pad)).reshape(-1, LANES))
    return jnp.concatenate(rows, axis=0)


def _unpack(block, shapes):
    out, row = [], 0
    for shp in shapes:
        size = int(np.prod(shp))
        nrows = -(-size // PACK_UNIT) * SUBLANES
        out.append(block[row:row + nrows].reshape(-1)[:size].reshape(shp))
        row += nrows
    return out


def _local_step(x, p, target, small, wfull):
    T = x.shape[0]
    W = N_PAIRS * LANES
    rel_bias = small["rel_bias"]

    n1 = _rms_fwd(x, small["ffn1_pre_g"], name="n1")
    gu1 = _mm_nn(n1, wfull["gu1"], bm=1024, bn=1408, out_dtype=BF16, name="ffn1_gu")
    a1 = _swiglu_fwd(gu1, name="ffn1_act")
    f1 = _mm_nn(a1, wfull["d1"], bm=1024, bn=512, out_dtype=F32, name="ffn1_down")
    h1, n2 = _resid_norm(x, f1, small["ffn1_post_g"], 0.5, small["attn_pre_g"], name="h1")
    z = _mm_nn(n2, wfull["win"], bm=1024, bn=768, out_dtype=BF16, name="in_proj", bias=small["b_in"])

    k_a = z[:, 512:640].reshape(T, 2, 1, HEAD_DIM)
    v_a = z[:, 640:768].reshape(T, 2, 1, HEAD_DIM)
    k_a = jnp.broadcast_to(k_a, (T, 2, 4, HEAD_DIM)).reshape(T, W)
    v_a = jnp.broadcast_to(v_a, (T, 2, 4, HEAD_DIM)).reshape(T, W)
    idx_a = jnp.asarray(_t5_index(A_WINDOW - 1, 1))
    bias_a = _bias_tiles(rel_bias, idx_a, 0, name="bias_a")
    a_args = dict(r=1, q_off=ZQ_A, k_off=0, v_off=0, q_w=Z_TILES, kv_w=N_PAIRS)
    o_a, m_a, l_a = _band_fwd(z, k_a, v_a, bias_a, name="band_a_fwd", **a_args)
    sink_row = jnp.repeat(small["sinks"], HEAD_DIM, axis=1)
    out_a, out_a_bf, lse_a = _merge([(o_a, m_a, l_a)], sink_row, name="merge_a")

    no_sink = jnp.full((1, W), NEG, F32)
    b_ctx = []
    branches = []
    for t, (window, dil) in enumerate(B_PATTERNS):
        idx = jnp.asarray(_t5_index(window // dil, dil))
        bias = _bias_tiles(rel_bias, idx, 2 * N_PAIRS, name=f"bias_b{t}")
        z_r = z.reshape(T // dil, dil * Z_TILES * LANES)
        args = dict(r=dil, q_off=ZQ_B, k_off=ZK_B, v_off=ZV_B, q_w=Z_TILES, kv_w=Z_TILES)
        o, m, l = _band_fwd(z_r, z_r, z_r, bias, name=f"band_b{t}_fwd", **args)
        branches.append((o.reshape(T, W), m.reshape(T, W), l.reshape(T, W)))
        b_ctx.append((idx, bias, z_r, args, dil))
    out_b, out_b_bf, lse_b = _merge(branches, no_sink, name="merge_b")

    mix = jnp.concatenate([out_a_bf, out_b_bf], axis=1)
    att = _mm_nn(mix, wfull["wout"], bm=1024, bn=1024, out_dtype=F32, name="out_proj", bias=small["b_out"])
    h2, n3 = _resid_norm(h1, att, small["attn_post_g"], 1.0, small["ffn2_pre_g"], name="h2")
    gu2 = _mm_nn(n3, wfull["gu2"], bm=1024, bn=1408, out_dtype=BF16, name="ffn2_gu")
    a2 = _swiglu_fwd(gu2, name="ffn2_act")
    f2 = _mm_nn(a2, wfull["d2"], bm=1024, bn=512, out_dtype=F32, name="ffn2_down")
    h3, n4 = _resid_norm(h2, f2, small["ffn2_post_g"], 0.5, small["ple_pre_g"], name="h3")
    gp = _mm_nn(n4, wfull["gate"], bm=1024, bn=1024, out_dtype=F32, name="ple_gate")
    p_bf = p.astype(BF16)
    e = _mm_nn(p_bf, wfull["ple"], bm=1024, bn=256, out_dtype=F32, name="ple_proj")
    dh4, de, dgp, loss, d_ple_post = _ple_loss(h3, gp, e, small["ple_post_g"], target, name="ple_loss")

    gw, gs = {}, {"ple_post_g": d_ple_post}
    gw["ple"] = _mm_tn(p_bf, de, bm=256, bn=256, out_dtype=BF16, name="d_w_ple", n_stack=N_CHIPS)
    gw["gate"] = _mm_tn(n4, dgp, bm=512, bn=1024, out_dtype=BF16, name="d_w_gate")
    dn4 = _mm_nt(dgp, wfull["gate"], bm=1024, bn=1024, out_dtype=F32, name="d_n4")
    dh3, df2, gs["ple_pre_g"], gs["ffn2_post_g"], _ = _norms_bwd(
        dh4, h3, small["ple_pre_g"], dn4, f2, small["ffn2_post_g"], 0.5, name="bwd_h3")

    def ffn_bwd(df, a, gu, n, w_down, w_gu, tag):
        da = _mm_nt(df, w_down, bm=1024, bn=1408, out_dtype=BF16, name=f"d_a{tag}")
        g_down = _mm_tn(a, df, bm=256, bn=1024, out_dtype=BF16, name=f"d_w_down{tag}")
        dgu = _swiglu_bwd(gu, da, name=f"d_gu{tag}")
        g_gu = _mm_tn(n, dgu, bm=512, bn=1408, out_dtype=BF16, name=f"d_w_gu{tag}", n_stack=N_CHIPS)
        dn = _mm_nt(dgu, w_gu, bm=512, bn=1024, out_dtype=F32, name=f"d_n{tag}")
        return g_down, g_gu, dn

    gw["d2"], gw["gu2"], dn3 = ffn_bwd(df2, a2, gu2, n3, wfull["d2"], wfull["gu2"], "2")
    dh2, datt, gs["ffn2_pre_g"], gs["attn_post_g"], gs["b_out"] = _norms_bwd(
        dh3, h2, small["ffn2_pre_g"], dn3, att, small["attn_post_g"], 1.0, name="bwd_h2")
    gw["wout"] = _mm_tn(mix, datt, bm=512, bn=1024, out_dtype=BF16, name="d_w_out")
    dmix = _mm_nt(datt, wfull["wout"], bm=1024, bn=1024, out_dtype=F32, name="d_mix")
    dmix_a, dmix_b = dmix[:, 0:W], dmix[:, W:2 * W]

    dq_a, dk_a, dv_a, dbias_a, dsink = _band_bwd(
        z, k_a, v_a, dmix_a, out_a, lse_a, bias_a, sink_row, name="band_a_bwd", **a_args)
    d_rel_a = _bias_grad(dbias_a, idx_a, name="d_rel_a")
    b_grads = []
    d_rel_b = None
    for t, (idx, bias, z_r, args, dil) in enumerate(b_ctx):
        shp = (T // dil, dil * W)
        dq, dk, dv, dbias, _ = _band_bwd(
            z_r, z_r, z_r, dmix_b.reshape(shp), out_b.reshape(shp), lse_b.reshape(shp), bias, no_sink,
            name=f"band_b{t}_bwd", **args)
        b_grads.append((dq.reshape(T, W), dk.reshape(T, W), dv.reshape(T, W)))
        d_rel = _bias_grad(dbias, idx, name=f"d_rel_b{t}")
        d_rel_b = d_rel if d_rel_b is None else d_rel_b + d_rel
    gs["rel_bias"] = jnp.concatenate([d_rel_a[:, 0:2 * N_PAIRS], d_rel_b[:, 0:2 * N_PAIRS]], axis=1)
    gs["sinks"] = dsink[:, 0, ::HEAD_DIM].reshape(1, 2 * N_PAIRS)
    dz, gs["b_in"] = _dz_assemble(dq_a, dk_a, dv_a, b_grads, name="d_z")
    gw["win"] = _mm_tn(n2, dz, bm=512, bn=768, out_dtype=BF16, name="d_w_in")
    dn2 = _mm_nt(dz, wfull["win"], bm=1024, bn=1024, out_dtype=F32, name="d_n2")
    dh1, df1, gs["attn_pre_g"], gs["ffn1_post_g"], _ = _norms_bwd(
        dh2, h1, small["attn_pre_g"], dn2, f1, small["ffn1_post_g"], 0.5, name="bwd_h1")
    gw["d1"], gw["gu1"], dn1 = ffn_bwd(df1, a1, gu1, n1, wfull["d1"], wfull["gu1"], "1")
    grad_x, gs["ffn1_pre_g"] = _norm_bwd_last(dh1, x, small["ffn1_pre_g"], dn1, name="bwd_x")
    return loss, grad_x, gs, gw


def kernel(x, p, rel_bias, ffn1_pre_g, ffn1_w_gu, ffn1_w_down, ffn1_post_g, attn_pre_g, w_in, b_in, sinks, w_out, b_out, attn_post_g, ffn2_pre_g, ffn2_w_gu, ffn2_w_down, ffn2_post_g, ple_pre_g, w_ple_gate, w_ple_proj, ple_post_g, loss_target, m_rel_bias, m_ffn1_pre_g, m_ffn1_w_gu, m_ffn1_w_down, m_ffn1_post_g, m_attn_pre_g, m_w_in, m_b_in, m_sinks, m_w_out, m_b_out, m_attn_post_g, m_ffn2_pre_g, m_ffn2_w_gu, m_ffn2_w_down, m_ffn2_post_g, m_ple_pre_g, m_w_ple_gate, m_w_ple_proj, m_ple_post_g, v_rel_bias, v_ffn1_pre_g, v_ffn1_w_gu, v_ffn1_w_down, v_ffn1_post_g, v_attn_pre_g, v_w_in, v_b_in, v_sinks, v_w_out, v_b_out, v_attn_post_g, v_ffn2_pre_g, v_ffn2_w_gu, v_ffn2_w_down, v_ffn2_post_g, v_ple_pre_g, v_w_ple_gate, v_w_ple_proj, v_ple_post_g):
    given = dict(locals())
    w = {n: given[n] for n in WEIGHTS}
    m = {n: given["m_" + n] for n in WEIGHTS}
    v = {n: given["v_" + n] for n in WEIGHTS}
    T, D = x.shape[1], x.shape[2]
    c_idx = lax.axis_index("c").astype(jnp.int32).reshape(1)
    k_idx = (2 * lax.axis_index("x") + lax.axis_index("y")).astype(jnp.int32).reshape(1)

    shards = [w[n][0].astype(BF16) for n in BIG]
    full = dict(zip(BIG, _all_gather_weights(shards, name="gather_weights")))

    def stacked(a):
        return a.reshape(N_CHIPS, a.shape[0] // N_CHIPS, a.shape[1])

    win_st = stacked(full["w_in"])
    wfull = {
        "gu1": stacked(full["ffn1_w_gu"]), "d1": full["ffn1_w_down"],
        "win": jnp.transpose(win_st, (1, 0, 2)).reshape(win_st.shape[1], -1),
        "wout": full["w_out"], "gu2": stacked(full["ffn2_w_gu"]), "d2": full["ffn2_w_down"],
        "gate": full["w_ple_gate"], "ple": stacked(full["w_ple_proj"]),
    }
    small = {n: (w[n] if n == "rel_bias" else w[n].reshape(1, -1)) for n in SMALL}

    loss_part, grad_x, gs, gw = _local_step(x[0], p[0, 0], loss_target[0], small, wfull)

    win_g = gw["win"].reshape(gw["win"].shape[0], N_CHIPS, -1)
    by_name = {
        "ffn1_w_gu": gw["gu1"], "ffn1_w_down": gw["d1"], "w_in": jnp.transpose(win_g, (1, 0, 2)),
        "w_out": gw["wout"], "ffn2_w_gu": gw["gu2"], "ffn2_w_down": gw["d2"],
        "w_ple_gate": gw["gate"], "w_ple_proj": gw["ple"],
    }
    g4 = []
    for n in BIG:
        rs, cs = w[n].shape[1], w[n].shape[2]
        g4.append(by_name[n].reshape(N_CHIPS, 2, rs // 2, cs))
    from_sibling = _pair_swap(g4, name="grad_pair_swap")
    pair_sums = [_pair_sum(a, b, c_idx, name=f"grad_pair_sum_{i}") for i, (a, b) in enumerate(zip(g4, from_sibling))]
    from_chips = _chip_scatter(pair_sums, name="grad_chip_scatter")
    halves = [_chip_sum(a, b, k_idx, name=f"grad_chip_sum_{i}")
              for i, (a, b) in enumerate(zip(pair_sums, from_chips))]
    grads_big = dict(zip(BIG, _half_swap(halves, name="grad_half_swap")))

    small_shapes = [w[n].shape for n in SMALL]
    packed = _pack([gs[n] for n in SMALL] + [loss_part[:, 0:1]])
    summed = _small_all_reduce(packed, name="small_all_reduce")
    unpacked = _unpack(summed, small_shapes + [(1,)])
    grads_small = dict(zip(SMALL, unpacked[:-1]))
    loss = unpacked[-1].reshape(())

    grads, delta, new_m, new_v = {}, {}, {}, {}
    for n in BIG:
        grads[n] = grads_big[n][None]
        d, mn, vn = _adamw(w[n][0], grads_big[n], m[n][0], v[n][0], name=f"adamw_{n}")
        delta[n], new_m[n], new_v[n] = d[None], mn[None], vn[None]
    n_rows = summed.shape[0] - SUBLANES
    d_s, m_s, v_s = _adamw(_pack([w[n] for n in SMALL]), summed[0:n_rows], _pack([m[n] for n in SMALL]),
                           _pack([v[n] for n in SMALL]), name="adamw_small")
    for name, dd, mm, vv, gg in zip(SMALL, _unpack(d_s, small_shapes), _unpack(m_s, small_shapes),
                                    _unpack(v_s, small_shapes), unpacked[:-1]):
        grads[name], delta[name], new_m[name], new_v[name] = gg, dd, mm, vv

    return (loss, grad_x[None], *[grads[n] for n in WEIGHTS], *[delta[n] for n in WEIGHTS],
            *[new_m[n] for n in WEIGHTS], *[new_v[n] for n in WEIGHTS])
```

```python
import math

import jax
import jax.numpy as jnp
import numpy as np
from jax import lax
from jax.experimental import pallas as pl
from jax.experimental.pallas import tpu as pltpu

F32 = jnp.float32
BF16 = jnp.bfloat16
MESH = pl.DeviceIdType.MESH

EPS = 1e-6
NEG = -1e30
LANES = 128
SUBLANES = 8
HEAD_DIM = 64
QBLK = 128
N_PAIRS = 4
MIX_W = N_PAIRS * LANES
A_WINDOW = 128
B_PATTERNS = ((128, 1), (512, 4), (2048, 16))
NUM_BUCKETS = 32
MAX_DISTANCE = 2048
ZA_W = 768
ZB_W = 1536
N_CHIPS = 4
VMEM_BYTES_V7X = 64 * 2**20

ADAM_LR, ADAM_B1, ADAM_B2, ADAM_EPS, ADAM_WD, ADAM_STEP = 0.001, 0.9, 0.999, 1e-08, 0.01, 10

BIG = ("ffn1_w_gu", "ffn1_w_down", "w_in", "w_out", "ffn2_w_gu", "ffn2_w_down", "w_ple_gate", "w_ple_proj")
SMALL = ("rel_bias", "ffn1_pre_g", "ffn1_post_g", "attn_pre_g", "b_in", "sinks", "b_out", "attn_post_g",
         "ffn2_pre_g", "ffn2_post_g", "ple_pre_g", "ple_post_g")
WEIGHTS = ("rel_bias", "ffn1_pre_g", "ffn1_w_gu", "ffn1_w_down", "ffn1_post_g", "attn_pre_g", "w_in", "b_in",
           "sinks", "w_out", "b_out", "attn_post_g", "ffn2_pre_g", "ffn2_w_gu", "ffn2_w_down", "ffn2_post_g",
           "ple_pre_g", "w_ple_gate", "w_ple_proj", "ple_post_g")
PACK_UNIT = SUBLANES * LANES


def _vmem_limit(*block_bytes):
    need = 2 * sum(block_bytes) + max(block_bytes) * 2 + (4 << 20)
    return int(min(max(need, 32 << 20), VMEM_BYTES_V7X - (6 << 20)))


def _nbytes(shape, dtype):
    return int(np.prod(shape)) * jnp.dtype(dtype).itemsize


def _row_block(rows, cap, mult):
    for cand in range(min(rows, cap), 0, -1):
        if rows % cand == 0 and cand % mult == 0:
            return cand
    raise ValueError((rows, cap, mult))


def _mm_nn(a, b, *, bm, bn, out_dtype, name, bias=None):
    M, K = a.shape
    stacked = b.ndim == 3
    N = b.shape[0] * b.shape[2] if stacked else b.shape[1]
    assert M % bm == 0 and N % bn == 0 and (not stacked or bn == b.shape[2])

    def body(*refs):
        a_ref, b_ref = refs[0], refs[1]
        o_ref = refs[-1]
        acc = jnp.dot(a_ref[...], b_ref[...], preferred_element_type=F32)
        if bias is not None:
            acc = acc + refs[2][...]
        o_ref[...] = acc.astype(o_ref.dtype)

    if stacked:
        b_spec = pl.BlockSpec((None, K, bn), lambda i, j: (j, 0, 0))
    else:
        b_spec = pl.BlockSpec((K, bn), lambda i, j: (0, j))
    in_specs = [pl.BlockSpec((bm, K), lambda i, j: (i, 0)), b_spec]
    args = [a, b]
    if bias is not None:
        in_specs.append(pl.BlockSpec((1, bn), lambda i, j: (0, j)))
        args.append(bias)
    limit = _vmem_limit(_nbytes((bm, K), a.dtype), _nbytes((K, bn), b.dtype), _nbytes((bm, bn), F32))
    return pl.pallas_call(
        body, name=name, grid=(M // bm, N // bn), in_specs=in_specs,
        out_specs=pl.BlockSpec((bm, bn), lambda i, j: (i, j)),
        out_shape=jax.ShapeDtypeStruct((M, N), out_dtype),
        compiler_params=pltpu.CompilerParams(dimension_semantics=("parallel", "parallel"), vmem_limit_bytes=limit),
    )(*args)


def _mm_nt(a, b, *, bm, bn, out_dtype, name, add=None):
    M, K = a.shape
    stacked = b.ndim == 3
    N = b.shape[1] if stacked else b.shape[0]
    n_sh = b.shape[0] if stacked else 1
    ks = b.shape[2] if stacked else K
    assert M % bm == 0 and N % bn == 0 and n_sh * ks == K
    dn = (((1,), (1,)), ((), ()))

    def body(*refs):
        a_ref, b_ref = refs[0], refs[1]
        o_ref = refs[-1]
        if stacked:
            acc = lax.dot_general(a_ref[:, 0:ks], b_ref[0], dn, preferred_element_type=F32)
            for s in range(1, n_sh):
                acc = acc + lax.dot_general(a_ref[:, s * ks:(s + 1) * ks], b_ref[s], dn, preferred_element_type=F32)
        else:
            acc = lax.dot_general(a_ref[...], b_ref[...], dn, preferred_element_type=F32)
        if add is not None:
            acc = acc + refs[2][...]
        o_ref[...] = acc.astype(o_ref.dtype)

    if stacked:
        b_spec = pl.BlockSpec((n_sh, bn, ks), lambda i, j: (0, j, 0))
    else:
        b_spec = pl.BlockSpec((bn, K), lambda i, j: (j, 0))
    in_specs = [pl.BlockSpec((bm, K), lambda i, j: (i, 0)), b_spec]
    args = [a, b]
    if add is not None:
        in_specs.append(pl.BlockSpec((bm, bn), lambda i, j: (i, j)))
        args.append(add)
    limit = _vmem_limit(_nbytes((bm, K), a.dtype), _nbytes((bn, K), b.dtype), _nbytes((bm, bn), F32))
    return pl.pallas_call(
        body, name=name, grid=(M // bm, N // bn), in_specs=in_specs,
        out_specs=pl.BlockSpec((bm, bn), lambda i, j: (i, j)),
        out_shape=jax.ShapeDtypeStruct((M, N), out_dtype),
        compiler_params=pltpu.CompilerParams(dimension_semantics=("parallel", "parallel"), vmem_limit_bytes=limit),
    )(*args)


def _mm_tn(a, b, *, bm, bn, out_dtype, name, n_stack=None):
    T, M = a.shape
    N = b.shape[1]
    assert M % bm == 0 and N % bn == 0 and (n_stack is None or bn * n_stack == N)
    dn = (((0,), (0,)), ((), ()))

    def body(a_ref, b_ref, o_ref):
        acc = lax.dot_general(a_ref[...], b_ref[...], dn, preferred_element_type=F32)
        o_ref[...] = acc.astype(o_ref.dtype)

    if n_stack is None:
        out_spec = pl.BlockSpec((bm, bn), lambda i, j: (i, j))
        out_shape = jax.ShapeDtypeStruct((M, N), out_dtype)
    else:
        out_spec = pl.BlockSpec((None, bm, bn), lambda i, j: (j, i, 0))
        out_shape = jax.ShapeDtypeStruct((n_stack, M, bn), out_dtype)
    limit = _vmem_limit(_nbytes((T, bm), a.dtype), _nbytes((T, bn), b.dtype), _nbytes((bm, bn), F32))
    return pl.pallas_call(
        body, name=name, grid=(M // bm, N // bn),
        in_specs=[pl.BlockSpec((T, bm), lambda i, j: (0, i)), pl.BlockSpec((T, bn), lambda i, j: (0, j))],
        out_specs=out_spec, out_shape=out_shape,
        compiler_params=pltpu.CompilerParams(dimension_semantics=("parallel", "parallel"), vmem_limit_bytes=limit),
    )(a, b)


ROW_BLK = 256


def _rstd(x):
    return lax.rsqrt(jnp.mean(x * x, axis=-1, keepdims=True) + EPS)


def _norm_bwd(xhat, rstd, dxhat):
    return rstd * (dxhat - xhat * jnp.mean(dxhat * xhat, axis=-1, keepdims=True))


def _row_spec(cols):
    return pl.BlockSpec((ROW_BLK, cols), lambda i: (i, 0))


def _vec_spec(cols):
    return pl.BlockSpec((1, cols), lambda i: (0, 0))


def _rms_fwd(x, g, *, name):
    T, D = x.shape

    def body(x_ref, g_ref, o_ref):
        xv = x_ref[...]
        o_ref[...] = (xv * _rstd(xv) * g_ref[...]).astype(o_ref.dtype)

    return pl.pallas_call(
        body, name=name, grid=(T // ROW_BLK,), in_specs=[_row_spec(D), _vec_spec(D)], out_specs=_row_spec(D),
        out_shape=jax.ShapeDtypeStruct((T, D), BF16),
        compiler_params=pltpu.CompilerParams(dimension_semantics=("parallel",)),
    )(x, g)


def _resid_norm(h, f, g_post, coef, g_next, *, name):
    T, D = h.shape

    def body(h_ref, f_ref, gp_ref, gn_ref, hn_ref, n_ref):
        fv = f_ref[...]
        hn = h_ref[...] + coef * (fv * _rstd(fv) * gp_ref[...])
        hn_ref[...] = hn
        n_ref[...] = (hn * _rstd(hn) * gn_ref[...]).astype(n_ref.dtype)

    return pl.pallas_call(
        body, name=name, grid=(T // ROW_BLK,),
        in_specs=[_row_spec(D), _row_spec(D), _vec_spec(D), _vec_spec(D)],
        out_specs=(_row_spec(D), _row_spec(D)),
        out_shape=(jax.ShapeDtypeStruct((T, D), F32), jax.ShapeDtypeStruct((T, D), BF16)),
        compiler_params=pltpu.CompilerParams(dimension_semantics=("parallel",)),
    )(h, f, g_post, g_next)


def _swiglu_fwd(gu, *, name):
    T, F2 = gu.shape
    F = F2 // 2

    def body(gu_ref, a_ref):
        g = gu_ref[:, 0:F].astype(F32)
        u = gu_ref[:, F:F2].astype(F32)
        a_ref[...] = (g * jax.nn.sigmoid(g) * u).astype(a_ref.dtype)

    return pl.pallas_call(
        body, name=name, grid=(T // ROW_BLK,), in_specs=[_row_spec(F2)], out_specs=_row_spec(F),
        out_shape=jax.ShapeDtypeStruct((T, F), BF16),
        compiler_params=pltpu.CompilerParams(dimension_semantics=("parallel",)),
    )(gu)


def _swiglu_bwd(gu, da, *, name):
    T, F2 = gu.shape
    F = F2 // 2

    def body(gu_ref, da_ref, o_ref):
        g = gu_ref[:, 0:F].astype(F32)
        u = gu_ref[:, F:F2].astype(F32)
        dav = da_ref[...].astype(F32)
        s = jax.nn.sigmoid(g)
        o_ref[:, 0:F] = (dav * u * (s * (1.0 + g * (1.0 - s)))).astype(o_ref.dtype)
        o_ref[:, F:F2] = (dav * (g * s)).astype(o_ref.dtype)

    return pl.pallas_call(
        body, name=name, grid=(T // ROW_BLK,), in_specs=[_row_spec(F2), _row_spec(F)], out_specs=_row_spec(F2),
        out_shape=jax.ShapeDtypeStruct((T, F2), BF16),
        compiler_params=pltpu.CompilerParams(dimension_semantics=("parallel",)),
    )(gu, da)


def _ple_loss(h3, gp, e, g_post, target, *, name):
    T, D = h3.shape

    def body(h_ref, gp_ref, e_ref, g_ref, t_ref, dh_ref, de_ref, dgp_ref, loss_ref, dg_ref):
        @pl.when(pl.program_id(0) == 0)
        def _():
            loss_ref[...] = jnp.zeros_like(loss_ref)
            dg_ref[...] = jnp.zeros_like(dg_ref)

        gate = jax.nn.sigmoid(gp_ref[...])
        ev = e_ref[...]
        ge = gate * ev
        r = _rstd(ge)
        gh = ge * r
        gpost = g_ref[...]
        diff = h_ref[...] + gh * gpost - t_ref[...]
        loss_ref[...] += jnp.sum(diff * diff) * (0.5 / D)
        dh = diff * (1.0 / D)
        dh_ref[...] = dh
        dg_ref[...] += jnp.sum(dh * gh, axis=0, keepdims=True)
        dge = _norm_bwd(gh, r, dh * gpost)
        de_ref[...] = (dge * gate).astype(de_ref.dtype)
        dgp_ref[...] = (dge * ev * gate * (1.0 - gate)).astype(dgp_ref.dtype)

    return pl.pallas_call(
        body, name=name, grid=(T // ROW_BLK,),
        in_specs=[_row_spec(D), _row_spec(D), _row_spec(D), _vec_spec(D), _row_spec(D)],
        out_specs=(_row_spec(D), _row_spec(D), _row_spec(D), _vec_spec(LANES), _vec_spec(D)),
        out_shape=(jax.ShapeDtypeStruct((T, D), F32), jax.ShapeDtypeStruct((T, D), BF16),
                   jax.ShapeDtypeStruct((T, D), BF16), jax.ShapeDtypeStruct((1, LANES), F32),
                   jax.ShapeDtypeStruct((1, D), F32)),
        compiler_params=pltpu.CompilerParams(dimension_semantics=("arbitrary",)),
    )(h3, gp, e, g_post, target)


def _norms_bwd(dh_in, h, g_pre, dn, f, g_post, coef, *, name):
    T, D = h.shape

    def body(dhi_ref, h_ref, gpre_ref, dn_ref, f_ref, gpost_ref, dh_ref, df_ref, dgpre_ref, dgpost_ref, dsum_ref):
        @pl.when(pl.program_id(0) == 0)
        def _():
            dgpre_ref[...] = jnp.zeros_like(dgpre_ref)
            dgpost_ref[...] = jnp.zeros_like(dgpost_ref)
            dsum_ref[...] = jnp.zeros_like(dsum_ref)

        hv = h_ref[...]
        r = _rstd(hv)
        hh = hv * r
        dnv = dn_ref[...]
        dgpre_ref[...] += jnp.sum(dnv * hh, axis=0, keepdims=True)
        dh = dhi_ref[...] + _norm_bwd(hh, r, dnv * gpre_ref[...])
        dh_ref[...] = dh
        fv = f_ref[...]
        rf = _rstd(fv)
        fh = fv * rf
        dy = coef * dh
        dgpost_ref[...] += jnp.sum(dy * fh, axis=0, keepdims=True)
        df = _norm_bwd(fh, rf, dy * gpost_ref[...])
        dsum_ref[...] += jnp.sum(df, axis=0, keepdims=True)
        df_ref[...] = df.astype(df_ref.dtype)

    return pl.pallas_call(
        body, name=name, grid=(T // ROW_BLK,),
        in_specs=[_row_spec(D), _row_spec(D), _vec_spec(D), _row_spec(D), _row_spec(D), _vec_spec(D)],
        out_specs=(_row_spec(D), _row_spec(D), _vec_spec(D), _vec_spec(D), _vec_spec(D)),
        out_shape=(jax.ShapeDtypeStruct((T, D), F32), jax.ShapeDtypeStruct((T, D), BF16),
                   jax.ShapeDtypeStruct((1, D), F32), jax.ShapeDtypeStruct((1, D), F32),
                   jax.ShapeDtypeStruct((1, D), F32)),
        compiler_params=pltpu.CompilerParams(dimension_semantics=("arbitrary",)),
    )(dh_in, h, g_pre, dn, f, g_post)


def _norm_bwd_last(dh_in, h, g_pre, dn, *, name):
    T, D = h.shape

    def body(dhi_ref, h_ref, gpre_ref, dn_ref, dh_ref, dgpre_ref):
        @pl.when(pl.program_id(0) == 0)
        def _():
            dgpre_ref[...] = jnp.zeros_like(dgpre_ref)

        hv = h_ref[...]
        r = _rstd(hv)
        hh = hv * r
        dnv = dn_ref[...]
        dgpre_ref[...] += jnp.sum(dnv * hh, axis=0, keepdims=True)
        dh_ref[...] = dhi_ref[...] + _norm_bwd(hh, r, dnv * gpre_ref[...])

    return pl.pallas_call(
        body, name=name, grid=(T // ROW_BLK,),
        in_specs=[_row_spec(D), _row_spec(D), _vec_spec(D), _row_spec(D)],
        out_specs=(_row_spec(D), _vec_spec(D)),
        out_shape=(jax.ShapeDtypeStruct((T, D), F32), jax.ShapeDtypeStruct((1, D), F32)),
        compiler_params=pltpu.CompilerParams(dimension_semantics=("arbitrary",)),
    )(dh_in, h, g_pre, dn)


def _t5_index(max_dist, stride):
    rho = np.arange(QBLK)[:, None]
    kap = np.arange(2 * QBLK)[None, :]
    d = rho + QBLK - kap
    valid = (d >= 0) & (d <= max_dist)
    n = np.maximum(d, 0) * stride
    max_exact = NUM_BUCKETS // 2
    nf = np.maximum(n, 1).astype(np.float32)
    large = max_exact + (np.log(nf / np.float32(max_exact)) / np.float32(math.log(MAX_DISTANCE / max_exact))
                         * np.float32(NUM_BUCKETS - max_exact)).astype(np.int32)
    large = np.minimum(large, NUM_BUCKETS - 1)
    bucket = np.where(n < max_exact, n, large)
    return np.where(valid, bucket, -1).astype(np.int32)


def _bias_tiles(rel_bias, idx, head_off, *, name):
    def body(rb_ref, idx_ref, o_ref):
        h = pl.program_id(0)
        idxv = idx_ref[...]
        acc = jnp.where(idxv < 0, NEG, 0.0).astype(F32)
        for b in range(NUM_BUCKETS):
            acc = acc + jnp.where(idxv == b, rb_ref[b, head_off + h], 0.0)
        o_ref[0] = acc

    return pl.pallas_call(
        body, name=name, grid=(2 * N_PAIRS,),
        in_specs=[pl.BlockSpec(memory_space=pltpu.SMEM), pl.BlockSpec((QBLK, 2 * QBLK), lambda h: (0, 0))],
        out_specs=pl.BlockSpec((1, QBLK, 2 * QBLK), lambda h: (h, 0, 0)),
        out_shape=jax.ShapeDtypeStruct((2 * N_PAIRS, QBLK, 2 * QBLK), F32),
        compiler_params=pltpu.CompilerParams(dimension_semantics=("parallel",)),
    )(rel_bias, idx)


def _bias_grad(d_bias, idx, *, name):
    def body(db_ref, idx_ref, o_ref):
        h = pl.program_id(0)

        @pl.when(h == 0)
        def _():
            o_ref[...] = jnp.zeros_like(o_ref)

        idxv = idx_ref[...]
        dv = db_ref[0]
        rows = lax.broadcasted_iota(jnp.int32, (NUM_BUCKETS, LANES), 0)
        lanes = lax.broadcasted_iota(jnp.int32, (NUM_BUCKETS, LANES), 1)
        acc = o_ref[...]
        for b in range(NUM_BUCKETS):
            s = jnp.sum(jnp.where(idxv == b, dv, 0.0))
            acc = acc + jnp.where((rows == b) & (lanes == h), s, 0.0)
        o_ref[...] = acc

    return pl.pallas_call(
        body, name=name, grid=(2 * N_PAIRS,),
        in_specs=[pl.BlockSpec((1, QBLK, 2 * QBLK), lambda h: (h, 0, 0)),
                  pl.BlockSpec((QBLK, 2 * QBLK), lambda h: (0, 0))],
        out_specs=pl.BlockSpec((NUM_BUCKETS, LANES), lambda h: (0, 0)),
        out_shape=jax.ShapeDtypeStruct((NUM_BUCKETS, LANES), F32),
        compiler_params=pltpu.CompilerParams(dimension_semantics=("arbitrary",)),
    )(d_bias, idx)


def _lane():
    return lax.broadcasted_iota(jnp.int32, (1, LANES), 1)


def _head_sel(h):
    return (_lane() < HEAD_DIM) if h == 0 else (_lane() >= HEAD_DIM)


def _stat_lo():
    return (_lane() % HEAD_DIM) < (HEAD_DIM // 2)


def _first_block_mask(b):
    kap = lax.broadcasted_iota(jnp.int32, (1, 2 * QBLK), 1)
    return jnp.where((kap < QBLK) & (b == 0), NEG, 0.0).astype(F32)


def _band_specs(r, nb, cols, stride, clamp):
    def qb(b):
        return jnp.minimum(b, nb - 1) if clamp else b

    blk = (QBLK, MIX_W)
    cur = pl.BlockSpec(blk, lambda j, b: (qb(b), cols + j * stride))
    prev = pl.BlockSpec(blk, lambda j, b: (jnp.maximum(qb(b) - 1, 0), cols + j * stride))
    return cur, prev


def _band_fwd(qa, ka, va, bias, *, r, q_col, k_col, v_col, stride, name):
    L = qa.shape[0]
    nb = L // QBLK
    dn = (((1,), (1,)), ((), ()))
    scale = HEAD_DIM ** -0.5

    def body(q_ref, kp_ref, kc_ref, vp_ref, vc_ref, bias_ref, o_ref, lse_ref):
        first = _first_block_mask(pl.program_id(1))
        sel0 = _head_sel(0)
        for i in range(N_PAIRS):
            cols = slice(i * LANES, (i + 1) * LANES)
            q = q_ref[:, cols]
            k = jnp.concatenate([kp_ref[:, cols], kc_ref[:, cols]], axis=0)
            v = jnp.concatenate([vp_ref[:, cols], vc_ref[:, cols]], axis=0)
            res = []
            for h in range(2):
                qh = jnp.where(_head_sel(h), q, jnp.zeros_like(q))
                s = lax.dot_general(qh, k, dn, preferred_element_type=F32) * scale + bias_ref[2 * i + h] + first
                m = jnp.max(s, axis=1, keepdims=True)
                p = jnp.exp(s - m)
                l = jnp.sum(p, axis=1, keepdims=True)
                o = jnp.dot(p.astype(v.dtype), v, preferred_element_type=F32)
                res.append((o / l, m + jnp.log(l)))
            o_ref[:, cols] = jnp.where(sel0, res[0][0], res[1][0]).astype(o_ref.dtype)
            lse_ref[:, cols] = jnp.where(sel0, res[0][1], res[1][1])

    q_spec, _ = _band_specs(r, nb, q_col, stride, False)
    kc_spec, kp_spec = _band_specs(r, nb, k_col, stride, False)
    vc_spec, vp_spec = _band_specs(r, nb, v_col, stride, False)
    bias_spec = pl.BlockSpec((2 * N_PAIRS, QBLK, 2 * QBLK), lambda j, b: (0, 0, 0))
    out_spec = pl.BlockSpec((QBLK, MIX_W), lambda j, b: (b, j))
    return pl.pallas_call(
        body, name=name, grid=(r, nb),
        in_specs=[q_spec, kp_spec, kc_spec, vp_spec, vc_spec, bias_spec],
        out_specs=(out_spec, out_spec),
        out_shape=(jax.ShapeDtypeStruct((L, r * MIX_W), BF16), jax.ShapeDtypeStruct((L, r * MIX_W), F32)),
        compiler_params=pltpu.CompilerParams(dimension_semantics=("parallel", "parallel")),
    )(qa, ka, ka, va, va, bias)


def _merge(branches, sink, *, name):
    T, W = branches[0][0].shape
    nbr = len(branches)

    def body(*refs):
        sink_ref = refs[2 * nbr]
        out_ref, outb_ref, lse_ref = refs[2 * nbr + 1:]
        sk = sink_ref[...]
        mx = sk
        for t in range(nbr):
            mx = jnp.maximum(mx, refs[2 * t + 1][...])
        den = jnp.exp(sk - mx)
        num = jnp.zeros_like(mx)
        for t in range(nbr):
            w = jnp.exp(refs[2 * t + 1][...] - mx)
            den = den + w
            num = num + w * refs[2 * t][...].astype(F32)
        out = num / den
        out_ref[...] = out
        outb_ref[...] = out.astype(outb_ref.dtype)
        lse_ref[...] = mx + jnp.log(den)

    args = [a for br in branches for a in br] + [sink]
    return pl.pallas_call(
        body, name=name, grid=(T // ROW_BLK,),
        in_specs=[_row_spec(W)] * (2 * nbr) + [_vec_spec(W)],
        out_specs=(_row_spec(W), _row_spec(W), _row_spec(W)),
        out_shape=(jax.ShapeDtypeStruct((T, W), F32), jax.ShapeDtypeStruct((T, W), BF16),
                   jax.ShapeDtypeStruct((T, W), F32)),
        compiler_params=pltpu.CompilerParams(dimension_semantics=("parallel",)),
    )(*args)


def _attn_delta(dmix, outs, lses, sink, *, name):
    T = dmix.shape[0]
    W = MIX_W

    def body(dmix_ref, oa_ref, ob_ref, la_ref, lb_ref, sink_ref, doa_ref, dob_ref, sta_ref, stb_ref, dsink_ref):
        @pl.when(pl.program_id(0) == 0)
        def _():
            dsink_ref[...] = jnp.zeros_like(dsink_ref)

        sel0, lo = _head_sel(0), _stat_lo()
        for mixer, (o_ref, l_ref, do_ref, st_ref) in enumerate(
                ((oa_ref, la_ref, doa_ref, sta_ref), (ob_ref, lb_ref, dob_ref, stb_ref))):
            for i in range(N_PAIRS):
                cols = slice(i * LANES, (i + 1) * LANES)
                do = dmix_ref[:, mixer * W + i * LANES:mixer * W + (i + 1) * LANES]
                prod = do * o_ref[:, cols]
                d0 = jnp.sum(jnp.where(sel0, prod, 0.0), axis=1, keepdims=True)
                d1 = jnp.sum(jnp.where(sel0, 0.0, prod), axis=1, keepdims=True)
                delta = jnp.where(sel0, d0, d1)
                lse = l_ref[:, cols]
                st_ref[:, cols] = jnp.where(lo, lse, delta)
                do_ref[:, cols] = do.astype(do_ref.dtype)
                if mixer == 0:
                    dsink_ref[:, cols] += -jnp.sum(jnp.exp(sink_ref[:, cols] - lse) * delta, axis=0, keepdims=True)

    return pl.pallas_call(
        body, name=name, grid=(T // ROW_BLK,),
        in_specs=[_row_spec(2 * W), _row_spec(W), _row_spec(W), _row_spec(W), _row_spec(W), _vec_spec(W)],
        out_specs=(_row_spec(W), _row_spec(W), _row_spec(W), _row_spec(W), _vec_spec(W)),
        out_shape=(jax.ShapeDtypeStruct((T, W), BF16), jax.ShapeDtypeStruct((T, W), BF16),
                   jax.ShapeDtypeStruct((T, W), F32), jax.ShapeDtypeStruct((T, W), F32),
                   jax.ShapeDtypeStruct((1, W), F32)),
        compiler_params=pltpu.CompilerParams(dimension_semantics=("arbitrary",)),
    )(dmix, outs[0], outs[1], lses[0], lses[1], sink)


def _band_bwd(qa, ka, va, d_out, stat, bias, *, r, q_col, k_col, v_col, stride, name):
    L = qa.shape[0]
    nb = L // QBLK
    dn_nt = (((1,), (1,)), ((), ()))
    dn_tn = (((0,), (0,)), ((), ()))
    scale = HEAD_DIM ** -0.5

    def body(q_ref, kp_ref, kc_ref, vp_ref, vc_ref, do_ref, st_ref, bias_ref,
             dq_ref, dk_ref, dv_ref, db_ref, dk_carry, dv_carry):
        j = pl.program_id(0)
        b = pl.program_id(1)

        @pl.when((j == 0) & (b == 0))
        def _():
            db_ref[...] = jnp.zeros_like(db_ref)

        @pl.when(b < nb)
        def _():
            first = _first_block_mask(b)
            lo = _stat_lo()
            for i in range(N_PAIRS):
                cols = slice(i * LANES, (i + 1) * LANES)
                q = q_ref[:, cols]
                k = jnp.concatenate([kp_ref[:, cols], kc_ref[:, cols]], axis=0)
                v = jnp.concatenate([vp_ref[:, cols], vc_ref[:, cols]], axis=0)
                do = do_ref[:, cols]
                st = st_ref[:, cols]
                dq_acc = jnp.zeros((QBLK, LANES), F32)
                dk_acc = jnp.zeros((2 * QBLK, LANES), F32)
                dv_acc = jnp.zeros((2 * QBLK, LANES), F32)
                for h in range(2):
                    sel = _head_sel(h)
                    qh = jnp.where(sel, q, jnp.zeros_like(q))
                    doh = jnp.where(sel, do, jnp.zeros_like(do))
                    lse_h = jnp.max(jnp.where(sel & lo, st, -jnp.inf), axis=1, keepdims=True)
                    delta = jnp.sum(jnp.where(sel & ~lo, st, 0.0), axis=1, keepdims=True) * (2.0 / HEAD_DIM)
                    s = lax.dot_general(qh, k, dn_nt, preferred_element_type=F32) * scale + bias_ref[2 * i + h] + first
                    p = jnp.exp(s - lse_h)
                    dp = lax.dot_general(doh, v, dn_nt, preferred_element_type=F32)
                    ds = p * (dp - delta)
                    db_ref[2 * i + h] += ds
                    dsb = ds.astype(q.dtype)
                    dq_acc = jnp.where(sel, jnp.dot(dsb, k, preferred_element_type=F32), dq_acc)
                    dk_acc = dk_acc + lax.dot_general(dsb, qh, dn_tn, preferred_element_type=F32)
                    dv_acc = dv_acc + lax.dot_general(p.astype(q.dtype), doh, dn_tn, preferred_element_type=F32)
                dq_ref[:, cols] = (dq_acc * scale).astype(dq_ref.dtype)
                dk_acc = dk_acc * scale

                @pl.when(b >= 1)
                def _():
                    dk_ref[:, cols] = (dk_carry[:, cols] + dk_acc[0:QBLK]).astype(dk_ref.dtype)
                    dv_ref[:, cols] = (dv_carry[:, cols] + dv_acc[0:QBLK]).astype(dv_ref.dtype)

                dk_carry[:, cols] = dk_acc[QBLK:2 * QBLK]
                dv_carry[:, cols] = dv_acc[QBLK:2 * QBLK]

        @pl.when(b == nb)
        def _():
            dk_ref[...] = dk_carry[...].astype(dk_ref.dtype)
            dv_ref[...] = dv_carry[...].astype(dv_ref.dtype)

    q_spec, _ = _band_specs(r, nb, q_col, stride, True)
    kc_spec, kp_spec = _band_specs(r, nb, k_col, stride, True)
    vc_spec, vp_spec = _band_specs(r, nb, v_col, stride, True)
    blk = (QBLK, MIX_W)
    tok_spec = pl.BlockSpec(blk, lambda j, b: (jnp.minimum(b, nb - 1), j))
    dkv_spec = pl.BlockSpec(blk, lambda j, b: (jnp.maximum(b - 1, 0), j))
    bias_spec = pl.BlockSpec((2 * N_PAIRS, QBLK, 2 * QBLK), lambda j, b: (0, 0, 0))
    grad = jax.ShapeDtypeStruct((L, r * MIX_W), BF16)
    return pl.pallas_call(
        body, name=name, grid=(r, nb + 1),
        in_specs=[q_spec, kp_spec, kc_spec, vp_spec, vc_spec, tok_spec, tok_spec, bias_spec],
        out_specs=(tok_spec, dkv_spec, dkv_spec, bias_spec),
        out_shape=(grad, grad, grad, jax.ShapeDtypeStruct((2 * N_PAIRS, QBLK, 2 * QBLK), F32)),
        scratch_shapes=[pltpu.VMEM(blk, F32), pltpu.VMEM(blk, F32)],
        compiler_params=pltpu.CompilerParams(dimension_semantics=("arbitrary", "arbitrary")),
    )(qa, ka, ka, va, va, d_out, stat, bias)


def _dz_assemble(dq_a, dk_a, dv_a, b_grads, *, name):
    T = dq_a.shape[0]
    W = MIX_W

    def group_sum(x):
        u0 = x[:, 0:LANES] + x[:, LANES:2 * LANES]
        u1 = x[:, 2 * LANES:3 * LANES] + x[:, 3 * LANES:4 * LANES]
        s0 = u0 + pltpu.roll(u0, HEAD_DIM, 1)
        s1 = u1 + pltpu.roll(u1, HEAD_DIM, 1)
        return jnp.where(_head_sel(0), s0, s1)

    def body(*refs):
        dqa_ref, dka_ref, dva_ref = refs[0:3]
        b_refs = refs[3:12]
        dza_ref, dzb_ref, sa_ref, sb_ref = refs[12:]

        @pl.when(pl.program_id(0) == 0)
        def _():
            sa_ref[...] = jnp.zeros_like(sa_ref)
            sb_ref[...] = jnp.zeros_like(sb_ref)

        def put(dst, acc, col, part):
            w = part.shape[1]
            dst[:, col:col + w] = part.astype(dst.dtype)
            acc[:, col:col + w] += jnp.sum(part, axis=0, keepdims=True)

        put(dza_ref, sa_ref, 0, dqa_ref[...].astype(F32))
        put(dza_ref, sa_ref, W, group_sum(dka_ref[...].astype(F32)))
        put(dza_ref, sa_ref, W + LANES, group_sum(dva_ref[...].astype(F32)))
        for t in range(3):
            part = (b_refs[t][...].astype(F32) + b_refs[3 + t][...].astype(F32) + b_refs[6 + t][...].astype(F32))
            put(dzb_ref, sb_ref, t * W, part)

    args = [dq_a, dk_a, dv_a] + [g[t] for g in b_grads for t in range(3)]
    return pl.pallas_call(
        body, name=name, grid=(T // ROW_BLK,),
        in_specs=[_row_spec(W)] * 12,
        out_specs=(_row_spec(ZA_W), _row_spec(ZB_W), _vec_spec(ZA_W), _vec_spec(ZB_W)),
        out_shape=(jax.ShapeDtypeStruct((T, ZA_W), BF16), jax.ShapeDtypeStruct((T, ZB_W), BF16),
                   jax.ShapeDtypeStruct((1, ZA_W), F32), jax.ShapeDtypeStruct((1, ZB_W), F32)),
        compiler_params=pltpu.CompilerParams(dimension_semantics=("arbitrary",)),
    )(*args)


ANY = pl.BlockSpec(memory_space=pl.ANY)


def _place():
    x, y, c = lax.axis_index("x"), lax.axis_index("y"), lax.axis_index("c")
    chips = [(1 - x, y), (x, 1 - y), (1 - x, 1 - y)]
    return x, y, c, chips


def _cast_place(shard, k_idx, *, name):
    rs, cs = shard.shape
    rb = _row_block(rs, 256, 16)
    nblk = rs // rb

    def body(k_ref, s_ref, o_ref):
        o_ref[...] = s_ref[...].astype(o_ref.dtype)

    grid_spec = pltpu.PrefetchScalarGridSpec(
        num_scalar_prefetch=1, grid=(nblk,),
        in_specs=[pl.BlockSpec((rb, cs), lambda t, k_ref: (t, 0))],
        out_specs=pl.BlockSpec((rb, cs), lambda t, k_ref: (k_ref[0] * nblk + t, 0)))
    return pl.pallas_call(
        body, name=name, grid_spec=grid_spec, out_shape=jax.ShapeDtypeStruct((N_CHIPS * rs, cs), BF16),
        compiler_params=pltpu.CompilerParams(dimension_semantics=("parallel",)),
    )(k_idx, shard)


def _all_gather_weights(fulls, *, name):
    n = len(fulls)

    def body(*refs):
        src = refs[0:n]
        full = refs[n:2 * n]
        send_sems, recv_sems = refs[2 * n:]
        x, y, c, chips = _place()
        k = 2 * x + y
        sibling = (x, y, 1 - c)

        def piece(ref, chip_idx, half):
            rs = ref.shape[0] // N_CHIPS
            rh = rs // 2
            return ref.at[pl.ds(chip_idx * rs + half * rh, rh)]

        def copy(sem, src_ref, dst_ref, to):
            return pltpu.make_async_remote_copy(src_ref=src_ref, dst_ref=dst_ref, send_sem=send_sems.at[sem],
                                                recv_sem=recv_sems.at[sem], device_id=to, device_id_type=MESH)

        started = []
        for i in range(n):
            for j, chip in enumerate(chips):
                cp = copy(3 * i + j, piece(src[i], k, c), piece(full[i], k, c), (*chip, c))
                cp.start()
                started.append(cp)
        for i in range(n):
            for j, chip in enumerate(chips):
                kj = 2 * chip[0] + chip[1]
                region = piece(full[i], kj, c)
                copy(3 * i + j, region, region, (*chip, c)).wait_recv()
                cp = copy(3 * n + 3 * i + j, region, region, sibling)
                cp.start()
                started.append(cp)
        for i in range(n):
            for j, chip in enumerate(chips):
                kj = 2 * chip[0] + chip[1]
                region = piece(full[i], kj, 1 - c)
                copy(3 * n + 3 * i + j, region, region, sibling).wait_recv()
        for cp in started:
            cp.wait_send()

    return pl.pallas_call(
        body, name=name, in_specs=[ANY] * n, out_specs=tuple([ANY] * n),
        out_shape=tuple(jax.ShapeDtypeStruct(a.shape, a.dtype) for a in fulls),
        input_output_aliases={i: i for i in range(n)},
        scratch_shapes=[pltpu.SemaphoreType.DMA((6 * n,)), pltpu.SemaphoreType.DMA((6 * n,))],
    )(*fulls)


def _pair_swap(grads, *, name):
    n = len(grads)

    def body(*refs):
        g = refs[0:n]
        got = refs[n:2 * n]
        send_sems, recv_sems = refs[2 * n:]
        x, y, c, _ = _place()
        copies = []
        for i in range(n):
            cp = pltpu.make_async_remote_copy(src_ref=g[i].at[:, 1 - c], dst_ref=got[i], send_sem=send_sems.at[i],
                                              recv_sem=recv_sems.at[i], device_id=(x, y, 1 - c), device_id_type=MESH)
            cp.start()
            copies.append(cp)
        for cp in copies:
            cp.wait()

    return pl.pallas_call(
        body, name=name, in_specs=[ANY] * n, out_specs=tuple([ANY] * n),
        out_shape=tuple(jax.ShapeDtypeStruct((a.shape[0], a.shape[2], a.shape[3]), a.dtype) for a in grads),
        scratch_shapes=[pltpu.SemaphoreType.DMA((n,)), pltpu.SemaphoreType.DMA((n,))],
    )(*grads)


def _pair_sum(g4, got, c_idx, *, name):
    _, _, rh, cs = g4.shape
    rb = _row_block(rh, 256, 16)

    def body(c_ref, own_ref, got_ref, o_ref):
        o_ref[...] = (own_ref[...].astype(F32) + got_ref[...].astype(F32)).astype(o_ref.dtype)

    grid_spec = pltpu.PrefetchScalarGridSpec(
        num_scalar_prefetch=1, grid=(N_CHIPS, rh // rb),
        in_specs=[pl.BlockSpec((None, None, rb, cs), lambda k, t, c_ref: (k, c_ref[0], t, 0)),
                  pl.BlockSpec((None, rb, cs), lambda k, t, c_ref: (k, t, 0))],
        out_specs=pl.BlockSpec((None, rb, cs), lambda k, t, c_ref: (k, t, 0)))
    return pl.pallas_call(
        body, name=name, grid_spec=grid_spec, out_shape=jax.ShapeDtypeStruct((N_CHIPS, rh, cs), BF16),
        compiler_params=pltpu.CompilerParams(dimension_semantics=("parallel", "parallel")),
    )(c_idx, g4, got)


def _chip_scatter(sums, *, name):
    n = len(sums)

    def body(*refs):
        s = refs[0:n]
        got = refs[n:2 * n]
        send_sems, recv_sems = refs[2 * n:]
        x, y, c, chips = _place()
        copies = []
        for i in range(n):
            for j, chip in enumerate(chips):
                kj = 2 * chip[0] + chip[1]
                cp = pltpu.make_async_remote_copy(src_ref=s[i].at[kj], dst_ref=got[i].at[j],
                                                  send_sem=send_sems.at[3 * i + j], recv_sem=recv_sems.at[3 * i + j],
                                                  device_id=(*chip, c), device_id_type=MESH)
                cp.start()
                copies.append(cp)
        for cp in copies:
            cp.wait()

    return pl.pallas_call(
        body, name=name, in_specs=[ANY] * n, out_specs=tuple([ANY] * n),
        out_shape=tuple(jax.ShapeDtypeStruct((3,) + a.shape[1:], a.dtype) for a in sums),
        scratch_shapes=[pltpu.SemaphoreType.DMA((3 * n,)), pltpu.SemaphoreType.DMA((3 * n,))],
    )(*sums)


def _chip_sum(sums, got, kc_idx, *, name):
    _, rh, cs = sums.shape
    rb = _row_block(rh, 256, 16)
    nblk = rh // rb

    def body(kc_ref, own_ref, got_ref, o_ref):
        acc = own_ref[...].astype(F32)
        for j in range(3):
            acc = acc + got_ref[j].astype(F32)
        o_ref[...] = acc

    grid_spec = pltpu.PrefetchScalarGridSpec(
        num_scalar_prefetch=1, grid=(nblk,),
        in_specs=[pl.BlockSpec((None, rb, cs), lambda t, kc: (kc[0], t, 0)),
                  pl.BlockSpec((3, rb, cs), lambda t, kc: (0, t, 0))],
        out_specs=pl.BlockSpec((rb, cs), lambda t, kc: (kc[1] * nblk + t, 0)))
    return pl.pallas_call(
        body, name=name, grid_spec=grid_spec, out_shape=jax.ShapeDtypeStruct((2 * rh, cs), F32),
        compiler_params=pltpu.CompilerParams(dimension_semantics=("parallel",)),
    )(kc_idx, sums, got)


def _half_swap(shards, *, name):
    n = len(shards)

    def body(*refs):
        src = refs[0:n]
        out = refs[n:2 * n]
        send_sems, recv_sems = refs[2 * n:]
        x, y, c, _ = _place()

        def copy(i, half):
            rh = src[i].shape[0] // 2
            return pltpu.make_async_remote_copy(
                src_ref=src[i].at[pl.ds(c * rh, rh)], dst_ref=out[i].at[pl.ds(half * rh, rh)],
                send_sem=send_sems.at[i], recv_sem=recv_sems.at[i], device_id=(x, y, 1 - c), device_id_type=MESH)

        copies = [copy(i, c) for i in range(n)]
        for cp in copies:
            cp.start()
        for i in range(n):
            copy(i, 1 - c).wait_recv()
        for cp in copies:
            cp.wait_send()

    return pl.pallas_call(
        body, name=name, in_specs=[ANY] * n, out_specs=tuple([ANY] * n),
        out_shape=tuple(jax.ShapeDtypeStruct(a.shape, a.dtype) for a in shards),
        input_output_aliases={i: i for i in range(n)},
        scratch_shapes=[pltpu.SemaphoreType.DMA((n,)), pltpu.SemaphoreType.DMA((n,))],
    )(*shards)


def _small_all_reduce(block, *, name):
    rows = block.shape[0]

    def body(x_ref, o_ref, buf, send_sems, recv_sems):
        x, y, c, _ = _place()
        me = 4 * x + 2 * y + c
        buf[me] = x_ref[...]
        copies = []
        for d in range(1, 8):
            to = (1 - x if d & 4 else x, 1 - y if d & 2 else y, 1 - c if d & 1 else c)
            cp = pltpu.make_async_remote_copy(src_ref=x_ref, dst_ref=buf.at[me], send_sem=send_sems.at[d - 1],
                                              recv_sem=recv_sems.at[d - 1], device_id=to, device_id_type=MESH)
            cp.start()
            copies.append((cp, to))
        for d in range(1, 8):
            to = copies[d - 1][1]
            sender = 4 * to[0] + 2 * to[1] + to[2]
            pltpu.make_async_remote_copy(src_ref=x_ref, dst_ref=buf.at[sender], send_sem=send_sems.at[d - 1],
                                         recv_sem=recv_sems.at[d - 1], device_id=to, device_id_type=MESH).wait_recv()
        for cp, _ in copies:
            cp.wait_send()
        acc = buf[0]
        for s in range(1, 8):
            acc = acc + buf[s]
        o_ref[...] = acc

    vm = pl.BlockSpec(memory_space=pltpu.VMEM)
    return pl.pallas_call(
        body, name=name, in_specs=[vm], out_specs=vm, out_shape=jax.ShapeDtypeStruct(block.shape, F32),
        scratch_shapes=[pltpu.VMEM((8, rows, LANES), F32), pltpu.SemaphoreType.DMA((7,)),
                        pltpu.SemaphoreType.DMA((7,))],
    )(block)


def _adamw(w, g, m, v, *, name):
    R, C = w.shape
    rb = _row_block(R, 256, SUBLANES)
    c1 = 1.0 - ADAM_B1 ** ADAM_STEP
    c2 = 1.0 - ADAM_B2 ** ADAM_STEP

    def body(w_ref, g_ref, m_ref, v_ref, d_ref, mo_ref, vo_ref):
        gv = g_ref[...]
        mn = ADAM_B1 * m_ref[...] + (1.0 - ADAM_B1) * gv
        vn = ADAM_B2 * v_ref[...] + (1.0 - ADAM_B2) * (gv * gv)
        d_ref[...] = -ADAM_LR * ((mn / c1) / (jnp.sqrt(vn / c2) + ADAM_EPS) + ADAM_WD * w_ref[...])
        mo_ref[...] = mn
        vo_ref[...] = vn

    spec = pl.BlockSpec((rb, C), lambda i: (i, 0))
    shp = jax.ShapeDtypeStruct((R, C), F32)
    return pl.pallas_call(
        body, name=name, grid=(R // rb,), in_specs=[spec] * 4, out_specs=(spec, spec, spec),
        out_shape=(shp, shp, shp), compiler_params=pltpu.CompilerParams(dimension_semantics=("parallel",)),
    )(w, g, m, v)


def _pack(parts):
    rows = []
    for a in parts:
        flat = a.reshape(-1).astype(F32)
        pad = (-flat.shape[0]) % PACK_UNIT
        rows.append(jnp.pad(flat, (0, pad)).reshape(-1, LANES))
    return jnp.concatenate(rows, axis=0)


def _unpack(block, shapes):
    out, row = [], 0
    for shp in shapes:
        size = int(np.prod(shp))
        nrows = -(-size // PACK_UNIT) * SUBLANES
        out.append(block[row:row + nrows].reshape(-1)[:size].reshape(shp))
        row += nrows
    return out


def _local_step(x, p, target, small, wfull):
    T = x.shape[0]
    W = MIX_W
    rel_bias = small["rel_bias"]
    b_in_a, b_in_b = small["b_in"][:, 0:ZA_W], small["b_in"][:, ZA_W:]

    n1 = _rms_fwd(x, small["ffn1_pre_g"], name="n1")
    gu1 = _mm_nn(n1, wfull["gu1"], bm=1024, bn=1408, out_dtype=BF16, name="ffn1_gu")
    a1 = _swiglu_fwd(gu1, name="ffn1_act")
    f1 = _mm_nn(a1, wfull["d1"], bm=1024, bn=512, out_dtype=F32, name="ffn1_down")
    h1, n2 = _resid_norm(x, f1, small["ffn1_post_g"], 0.5, small["attn_pre_g"], name="h1")
    za = _mm_nn(n2, wfull["win_a"], bm=1024, bn=ZA_W, out_dtype=BF16, name="in_proj_a", bias=b_in_a)
    zb = _mm_nn(n2, wfull["win_b"], bm=1024, bn=ZB_W // 2, out_dtype=BF16, name="in_proj_b", bias=b_in_b)

    k_a = za[:, W:W + LANES].reshape(T, 2, 1, HEAD_DIM)
    v_a = za[:, W + LANES:W + 2 * LANES].reshape(T, 2, 1, HEAD_DIM)
    k_a = jnp.broadcast_to(k_a, (T, 2, 4, HEAD_DIM)).reshape(T, W)
    v_a = jnp.broadcast_to(v_a, (T, 2, 4, HEAD_DIM)).reshape(T, W)
    idx_a = jnp.asarray(_t5_index(A_WINDOW - 1, 1))
    bias_a = _bias_tiles(rel_bias, idx_a, 0, name="bias_a")
    a_args = dict(r=1, q_col=0, k_col=0, v_col=0, stride=0)
    o_a, lse_a1 = _band_fwd(za, k_a, v_a, bias_a, name="band_a_fwd", **a_args)
    sink_row = jnp.repeat(small["sinks"], HEAD_DIM, axis=1)
    out_a, out_a_bf, lse_a = _merge([(o_a, lse_a1)], sink_row, name="merge_a")

    no_sink = jnp.full((1, W), NEG, F32)
    b_ctx, branches = [], []
    for t, (window, dil) in enumerate(B_PATTERNS):
        idx = jnp.asarray(_t5_index(window // dil, dil))
        bias = _bias_tiles(rel_bias, idx, 2 * N_PAIRS, name=f"bias_b{t}")
        zb_r = zb.reshape(T // dil, dil * ZB_W)
        args = dict(r=dil, q_col=0, k_col=1, v_col=2, stride=ZB_W // W)
        o, l = _band_fwd(zb_r, zb_r, zb_r, bias, name=f"band_b{t}_fwd", **args)
        branches.append((o.reshape(T, W), l.reshape(T, W)))
        b_ctx.append((idx, bias, zb_r, args, dil))
    out_b, out_b_bf, lse_b = _merge(branches, no_sink, name="merge_b")

    mix = jnp.concatenate([out_a_bf, out_b_bf], axis=1)
    att = _mm_nn(mix, wfull["wout"], bm=1024, bn=1024, out_dtype=F32, name="out_proj", bias=small["b_out"])
    h2, n3 = _resid_norm(h1, att, small["attn_post_g"], 1.0, small["ffn2_pre_g"], name="h2")
    gu2 = _mm_nn(n3, wfull["gu2"], bm=1024, bn=1408, out_dtype=BF16, name="ffn2_gu")
    a2 = _swiglu_fwd(gu2, name="ffn2_act")
    f2 = _mm_nn(a2, wfull["d2"], bm=1024, bn=512, out_dtype=F32, name="ffn2_down")
    h3, n4 = _resid_norm(h2, f2, small["ffn2_post_g"], 0.5, small["ple_pre_g"], name="h3")
    gp = _mm_nn(n4, wfull["gate"], bm=1024, bn=1024, out_dtype=F32, name="ple_gate")
    p_bf = p.astype(BF16)
    e = _mm_nn(p_bf, wfull["ple"], bm=1024, bn=256, out_dtype=F32, name="ple_proj")
    dh4, de, dgp, loss, d_ple_post = _ple_loss(h3, gp, e, small["ple_post_g"], target, name="ple_loss")

    gw, gs = {}, {"ple_post_g": d_ple_post}
    gw["ple"] = _mm_tn(p_bf, de, bm=256, bn=256, out_dtype=BF16, name="d_w_ple", n_stack=N_CHIPS)
    gw["gate"] = _mm_tn(n4, dgp, bm=512, bn=1024, out_dtype=BF16, name="d_w_gate")
    dn4 = _mm_nt(dgp, wfull["gate"], bm=1024, bn=1024, out_dtype=F32, name="d_n4")
    dh3, df2, gs["ple_pre_g"], gs["ffn2_post_g"], _ = _norms_bwd(
        dh4, h3, small["ple_pre_g"], dn4, f2, small["ffn2_post_g"], 0.5, name="bwd_h3")

    def ffn_bwd(df, a, gu, n, w_down, w_gu, tag):
        da = _mm_nt(df, w_down, bm=1024, bn=1408, out_dtype=BF16, name=f"ffn{tag}_d_a")
        g_down = _mm_tn(a, df, bm=256, bn=1024, out_dtype=BF16, name=f"ffn{tag}_d_w_down")
        dgu = _swiglu_bwd(gu, da, name=f"ffn{tag}_d_gu")
        g_gu = _mm_tn(n, dgu, bm=512, bn=1408, out_dtype=BF16, name=f"ffn{tag}_d_w_gu", n_stack=N_CHIPS)
        dn = _mm_nt(dgu, w_gu, bm=512, bn=1024, out_dtype=F32, name=f"ffn{tag}_d_n")
        return g_down, g_gu, dn

    gw["d2"], gw["gu2"], dn3 = ffn_bwd(df2, a2, gu2, n3, wfull["d2"], wfull["gu2"], "2")
    dh2, datt, gs["ffn2_pre_g"], gs["attn_post_g"], gs["b_out"] = _norms_bwd(
        dh3, h2, small["ffn2_pre_g"], dn3, att, small["attn_post_g"], 1.0, name="bwd_h2")
    gw["wout"] = _mm_tn(mix, datt, bm=512, bn=1024, out_dtype=BF16, name="d_w_out")
    dmix = _mm_nt(datt, wfull["wout"], bm=1024, bn=1024, out_dtype=F32, name="d_mix")

    do_a, do_b, stat_a, stat_b, dsink = _attn_delta(dmix, (out_a, out_b), (lse_a, lse_b), sink_row, name="attn_delta")
    gs["sinks"] = dsink[:, ::HEAD_DIM]
    dq_a, dk_a, dv_a, dbias_a = _band_bwd(za, k_a, v_a, do_a, stat_a, bias_a, name="band_a_bwd", **a_args)
    d_rel_a = _bias_grad(dbias_a, idx_a, name="d_rel_a")
    b_grads = []
    d_rel_b = None
    for t, (idx, bias, zb_r, args, dil) in enumerate(b_ctx):
        shp = (T // dil, dil * W)
        dq, dk, dv, dbias = _band_bwd(zb_r, zb_r, zb_r, do_b.reshape(shp), stat_b.reshape(shp), bias,
                                      name=f"band_b{t}_bwd", **args)
        b_grads.append((dq.reshape(T, W), dk.reshape(T, W), dv.reshape(T, W)))
        d_rel = _bias_grad(dbias, idx, name=f"d_rel_b{t}")
        d_rel_b = d_rel if d_rel_b is None else d_rel_b + d_rel
    gs["rel_bias"] = jnp.concatenate([d_rel_a[:, 0:2 * N_PAIRS], d_rel_b[:, 0:2 * N_PAIRS]], axis=1)
    dza, dzb, dsum_a, dsum_b = _dz_assemble(dq_a, dk_a, dv_a, b_grads, name="d_z")
    gs["b_in"] = jnp.concatenate([dsum_a, dsum_b], axis=1)
    gw["win"] = jnp.concatenate(
        [_mm_tn(n2, dza, bm=512, bn=ZA_W, out_dtype=BF16, name="d_w_in_a"),
         _mm_tn(n2, dzb, bm=512, bn=ZB_W // 2, out_dtype=BF16, name="d_w_in_b")], axis=1)
    dn2 = _mm_nt(dza, wfull["win_a"], bm=1024, bn=1024, out_dtype=F32, name="d_n2_a")
    dn2 = _mm_nt(dzb, wfull["win_b"], bm=1024, bn=1024, out_dtype=F32, name="d_n2_b", add=dn2)
    dh1, df1, gs["attn_pre_g"], gs["ffn1_post_g"], _ = _norms_bwd(
        dh2, h1, small["attn_pre_g"], dn2, f1, small["ffn1_post_g"], 0.5, name="bwd_h1")
    gw["d1"], gw["gu1"], dn1 = ffn_bwd(df1, a1, gu1, n1, wfull["d1"], wfull["gu1"], "1")
    grad_x, gs["ffn1_pre_g"] = _norm_bwd_last(dh1, x, small["ffn1_pre_g"], dn1, name="bwd_x")
    return loss, grad_x, gs, gw


def kernel(x, p, rel_bias, ffn1_pre_g, ffn1_w_gu, ffn1_w_down, ffn1_post_g, attn_pre_g, w_in, b_in, sinks, w_out, b_out, attn_post_g, ffn2_pre_g, ffn2_w_gu, ffn2_w_down, ffn2_post_g, ple_pre_g, w_ple_gate, w_ple_proj, ple_post_g, loss_target, m_rel_bias, m_ffn1_pre_g, m_ffn1_w_gu, m_ffn1_w_down, m_ffn1_post_g, m_attn_pre_g, m_w_in, m_b_in, m_sinks, m_w_out, m_b_out, m_attn_post_g, m_ffn2_pre_g, m_ffn2_w_gu, m_ffn2_w_down, m_ffn2_post_g, m_ple_pre_g, m_w_ple_gate, m_w_ple_proj, m_ple_post_g, v_rel_bias, v_ffn1_pre_g, v_ffn1_w_gu, v_ffn1_w_down, v_ffn1_post_g, v_attn_pre_g, v_w_in, v_b_in, v_sinks, v_w_out, v_b_out, v_attn_post_g, v_ffn2_pre_g, v_ffn2_w_gu, v_ffn2_w_down, v_ffn2_post_g, v_ple_pre_g, v_w_ple_gate, v_w_ple_proj, v_ple_post_g):
    given = dict(locals())
    w = {n: given[n] for n in WEIGHTS}
    m = {n: given["m_" + n] for n in WEIGHTS}
    v = {n: given["v_" + n] for n in WEIGHTS}
    c_pos = lax.axis_index("c").astype(jnp.int32)
    k_pos = (2 * lax.axis_index("x") + lax.axis_index("y")).astype(jnp.int32)
    c_idx, k_idx, kc_idx = c_pos.reshape(1), k_pos.reshape(1), jnp.stack([k_pos, c_pos])

    placed = [_cast_place(w[n][0], k_idx, name=f"place_{n}") for n in BIG]
    full = dict(zip(BIG, _all_gather_weights(placed, name="gather_weights")))

    def stacked(a):
        return a.reshape(N_CHIPS, a.shape[0] // N_CHIPS, a.shape[1])

    win_st = stacked(full["w_in"])
    win = jnp.transpose(win_st, (1, 0, 2)).reshape(win_st.shape[1], -1)
    wfull = {
        "gu1": stacked(full["ffn1_w_gu"]), "d1": full["ffn1_w_down"],
        "win_a": win[:, 0:ZA_W], "win_b": win[:, ZA_W:],
        "wout": full["w_out"], "gu2": stacked(full["ffn2_w_gu"]), "d2": full["ffn2_w_down"],
        "gate": full["w_ple_gate"], "ple": stacked(full["w_ple_proj"]),
    }
    small = {n: (w[n] if n == "rel_bias" else w[n].reshape(1, -1)) for n in SMALL}

    loss_part, grad_x, gs, gw = _local_step(x[0], p[0, 0], loss_target[0], small, wfull)

    win_g = gw["win"].reshape(gw["win"].shape[0], N_CHIPS, -1)
    by_name = {
        "ffn1_w_gu": gw["gu1"], "ffn1_w_down": gw["d1"], "w_in": jnp.transpose(win_g, (1, 0, 2)),
        "w_out": gw["wout"], "ffn2_w_gu": gw["gu2"], "ffn2_w_down": gw["d2"],
        "w_ple_gate": gw["gate"], "w_ple_proj": gw["ple"],
    }
    g4 = []
    for n in BIG:
        rs, cs = w[n].shape[1], w[n].shape[2]
        g4.append(by_name[n].reshape(N_CHIPS, 2, rs // 2, cs))
    from_sibling = _pair_swap(g4, name="grad_pair_swap")
    pair_sums = [_pair_sum(a, b, c_idx, name=f"grad_pair_sum_{n}") for n, a, b in zip(BIG, g4, from_sibling)]
    from_chips = _chip_scatter(pair_sums, name="grad_chip_scatter")
    halves = [_chip_sum(a, b, kc_idx, name=f"grad_chip_sum_{n}") for n, a, b in zip(BIG, pair_sums, from_chips)]
    grads_big = dict(zip(BIG, _half_swap(halves, name="grad_half_swap")))

    small_shapes = [w[n].shape for n in SMALL]
    packed = _pack([gs[n] for n in SMALL] + [loss_part[:, 0:1]])
    summed = _small_all_reduce(packed, name="small_all_reduce")
    unpacked = _unpack(summed, small_shapes + [(1,)])
    loss = unpacked[-1].reshape(())

    grads, delta, new_m, new_v = {}, {}, {}, {}
    for n in BIG:
        grads[n] = grads_big[n][None]
        d, mn, vn = _adamw(w[n][0], grads_big[n], m[n][0], v[n][0], name=f"adamw_{n}")
        delta[n], new_m[n], new_v[n] = d[None], mn[None], vn[None]
    n_rows = summed.shape[0] - SUBLANES
    d_s, m_s, v_s = _adamw(_pack([w[n] for n in SMALL]), summed[0:n_rows], _pack([m[n] for n in SMALL]),
                           _pack([v[n] for n in SMALL]), name="adamw_small")
    for name, dd, mm, vv, gg in zip(SMALL, _unpack(d_s, small_shapes), _unpack(m_s, small_shapes),
                                    _unpack(v_s, small_shapes), unpacked[:-1]):
        grads[name], delta[name], new_m[name], new_v[name] = gg, dd, mm, vv

    return (loss, grad_x[None], *[grads[n] for n in WEIGHTS], *[delta[n] for n in WEIGHTS],
            *[new_m[n] for n in WEIGHTS], *[new_v[n] for n in WEIGHTS])
```

```python
import math

import jax
import jax.numpy as jnp
import numpy as np
from jax import lax
from jax.experimental import pallas as pl
from jax.experimental.pallas import tpu as pltpu

F32 = jnp.float32
BF16 = jnp.bfloat16
MESH = pl.DeviceIdType.MESH

EPS = 1e-6
NEG = -1e30
LANES = 128
SUBLANES = 8
HEAD_DIM = 64
QBLK = 128
N_PAIRS = 4
MIX_W = N_PAIRS * LANES
A_WINDOW = 128
B_PATTERNS = ((128, 1), (512, 4), (2048, 16))
NUM_BUCKETS = 32
MAX_DISTANCE = 2048
ZA_W = 768
ZB_W = 1536
N_CHIPS = 4
VMEM_BYTES_V7X = 64 * 2**20

ADAM_LR, ADAM_B1, ADAM_B2, ADAM_EPS, ADAM_WD, ADAM_STEP = 0.001, 0.9, 0.999, 1e-08, 0.01, 10

BIG = ("ffn1_w_gu", "ffn1_w_down", "w_in", "w_out", "ffn2_w_gu", "ffn2_w_down", "w_ple_gate", "w_ple_proj")
SMALL = ("rel_bias", "ffn1_pre_g", "ffn1_post_g", "attn_pre_g", "b_in", "sinks", "b_out", "attn_post_g",
         "ffn2_pre_g", "ffn2_post_g", "ple_pre_g", "ple_post_g")
WEIGHTS = ("rel_bias", "ffn1_pre_g", "ffn1_w_gu", "ffn1_w_down", "ffn1_post_g", "attn_pre_g", "w_in", "b_in",
           "sinks", "w_out", "b_out", "attn_post_g", "ffn2_pre_g", "ffn2_w_gu", "ffn2_w_down", "ffn2_post_g",
           "ple_pre_g", "w_ple_gate", "w_ple_proj", "ple_post_g")
PACK_UNIT = SUBLANES * LANES


def _vmem_limit(*block_bytes):
    need = 2 * sum(block_bytes) + max(block_bytes) * 2 + (4 << 20)
    return int(min(max(need, 32 << 20), VMEM_BYTES_V7X - (6 << 20)))


def _nbytes(shape, dtype):
    return int(np.prod(shape)) * jnp.dtype(dtype).itemsize


def _row_block(rows, cap, mult):
    for cand in range(min(rows, cap), 0, -1):
        if rows % cand == 0 and cand % mult == 0:
            return cand
    raise ValueError((rows, cap, mult))


class _Carry:
    def __init__(self, operands, out_shapes, aliases, sems, start, finish):
        self.operands, self.out_shapes, self.aliases, self.sems = list(operands), list(out_shapes), dict(aliases), list(sems)
        self.start, self.finish = start, finish
        self.results = None


def _join(carries):
    carries = [c for c in carries if c is not None]
    if not carries:
        return None
    if len(carries) == 1:
        return carries[0]
    offs, pos = [], [0, 0, 0]
    for c in carries:
        offs.append(tuple(pos))
        pos = [pos[0] + len(c.operands), pos[1] + len(c.out_shapes), pos[2] + len(c.sems)]
    aliases = {}
    for c, (oi, oo, _) in zip(carries, offs):
        aliases.update({oi + i: oo + o for i, o in c.aliases.items()})

    def run(which):
        def fn(ins, outs, sems):
            for c, (oi, oo, os_) in zip(carries, offs):
                getattr(c, which)(ins[oi:oi + len(c.operands)], outs[oo:oo + len(c.out_shapes)],
                                  sems[os_:os_ + len(c.sems)])
        return fn

    joined = _Carry([a for c in carries for a in c.operands], [s for c in carries for s in c.out_shapes], aliases,
                    [s for c in carries for s in c.sems], run("start"), run("finish"))
    joined.parts = (carries, offs)
    return joined


def _set_results(carry, outs):
    carry.results = list(outs)
    if hasattr(carry, "parts"):
        for c, (_, oo, _) in zip(*carry.parts):
            _set_results(c, outs[oo:oo + len(c.out_shapes)])


ANY = pl.BlockSpec(memory_space=pl.ANY)


def _call(body, *, name, grid, in_specs, out_specs, out_shape, args, scratch=(), vmem_limit=None, carry=None,
          semantics=None):
    n_ci, n_co, n_cs = len(args), len(out_shape), len(scratch)
    semantics = semantics or ("parallel",) * len(grid)
    if carry is None:
        res = pl.pallas_call(
            body, name=name, grid=grid, in_specs=list(in_specs), out_specs=tuple(out_specs), out_shape=tuple(out_shape),
            scratch_shapes=list(scratch),
            compiler_params=pltpu.CompilerParams(dimension_semantics=semantics, vmem_limit_bytes=vmem_limit),
        )(*args)
        return list(res)
    n_pi, n_po = len(carry.operands), len(carry.out_shapes)

    def full_body(*refs):
        ci, pi = refs[:n_ci], refs[n_ci:n_ci + n_pi]
        o0 = n_ci + n_pi
        co, po = refs[o0:o0 + n_co], refs[o0 + n_co:o0 + n_co + n_po]
        s0 = o0 + n_co + n_po
        cs, ps = refs[s0:s0 + n_cs], refs[s0 + n_cs:]
        ids = [pl.program_id(ax) for ax in range(len(grid))]
        first, last = ids[0] == 0, ids[0] == grid[0] - 1
        for ax in range(1, len(grid)):
            first, last = first & (ids[ax] == 0), last & (ids[ax] == grid[ax] - 1)

        @pl.when(first)
        def _():
            carry.start(pi, po, ps)

        body(*ci, *co, *cs)

        @pl.when(last)
        def _():
            carry.finish(pi, po, ps)

    res = pl.pallas_call(
        full_body, name=name, grid=grid, in_specs=list(in_specs) + [ANY] * n_pi,
        out_specs=tuple(out_specs) + tuple([ANY] * n_po), out_shape=tuple(out_shape) + tuple(carry.out_shapes),
        scratch_shapes=list(scratch) + carry.sems,
        input_output_aliases={n_ci + i: n_co + o for i, o in carry.aliases.items()},
        compiler_params=pltpu.CompilerParams(dimension_semantics=("arbitrary",) * len(grid),
                                             vmem_limit_bytes=vmem_limit),
    )(*args, *carry.operands)
    _set_results(carry, res[n_co:])
    return list(res[:n_co])


def _comm_call(carry, *, name):
    n_pi, n_po = len(carry.operands), len(carry.out_shapes)

    def body(*refs):
        pi, po, ps = refs[:n_pi], refs[n_pi:n_pi + n_po], refs[n_pi + n_po:]
        carry.start(pi, po, ps)
        carry.finish(pi, po, ps)

    res = pl.pallas_call(
        body, name=name, in_specs=[ANY] * n_pi, out_specs=tuple([ANY] * n_po), out_shape=tuple(carry.out_shapes),
        scratch_shapes=carry.sems, input_output_aliases=dict(carry.aliases),
    )(*carry.operands)
    _set_results(carry, res)


def _mm_nn(a, b, *, bm, bn, out_dtype, name, bias=None, carry=None):
    M, K = a.shape
    stacked = b.ndim == 3
    N = b.shape[0] * b.shape[2] if stacked else b.shape[1]
    assert M % bm == 0 and N % bn == 0 and (not stacked or bn == b.shape[2])

    def body(*refs):
        a_ref, b_ref = refs[0], refs[1]
        o_ref = refs[-1]
        acc = jnp.dot(a_ref[...], b_ref[...], preferred_element_type=F32)
        if bias is not None:
            acc = acc + refs[2][...]
        o_ref[...] = acc.astype(o_ref.dtype)

    if stacked:
        b_spec = pl.BlockSpec((None, K, bn), lambda i, j: (j, 0, 0))
    else:
        b_spec = pl.BlockSpec((K, bn), lambda i, j: (0, j))
    in_specs = [pl.BlockSpec((bm, K), lambda i, j: (i, 0)), b_spec]
    args = [a, b]
    if bias is not None:
        in_specs.append(pl.BlockSpec((1, bn), lambda i, j: (0, j)))
        args.append(bias)
    limit = _vmem_limit(_nbytes((bm, K), a.dtype), _nbytes((K, bn), b.dtype), _nbytes((bm, bn), F32))
    return _call(body, name=name, grid=(M // bm, N // bn), in_specs=in_specs,
                 out_specs=[pl.BlockSpec((bm, bn), lambda i, j: (i, j))],
                 out_shape=[jax.ShapeDtypeStruct((M, N), out_dtype)], args=args, vmem_limit=limit, carry=carry)[0]


def _mm_nt(a, b, *, bm, bn, out_dtype, name, add=None, carry=None):
    M, K = a.shape
    stacked = b.ndim == 3
    N = b.shape[1] if stacked else b.shape[0]
    n_sh = b.shape[0] if stacked else 1
    ks = b.shape[2] if stacked else K
    assert M % bm == 0 and N % bn == 0 and n_sh * ks == K
    dn = (((1,), (1,)), ((), ()))

    def body(*refs):
        a_ref, b_ref = refs[0], refs[1]
        o_ref = refs[-1]
        if stacked:
            acc = lax.dot_general(a_ref[:, 0:ks], b_ref[0], dn, preferred_element_type=F32)
            for s in range(1, n_sh):
                acc = acc + lax.dot_general(a_ref[:, s * ks:(s + 1) * ks], b_ref[s], dn, preferred_element_type=F32)
        else:
            acc = lax.dot_general(a_ref[...], b_ref[...], dn, preferred_element_type=F32)
        if add is not None:
            acc = acc + refs[2][...]
        o_ref[...] = acc.astype(o_ref.dtype)

    if stacked:
        b_spec = pl.BlockSpec((n_sh, bn, ks), lambda i, j: (0, j, 0))
    else:
        b_spec = pl.BlockSpec((bn, K), lambda i, j: (j, 0))
    in_specs = [pl.BlockSpec((bm, K), lambda i, j: (i, 0)), b_spec]
    args = [a, b]
    if add is not None:
        in_specs.append(pl.BlockSpec((bm, bn), lambda i, j: (i, j)))
        args.append(add)
    limit = _vmem_limit(_nbytes((bm, K), a.dtype), _nbytes((bn, K), b.dtype), _nbytes((bm, bn), F32))
    return _call(body, name=name, grid=(M // bm, N // bn), in_specs=in_specs,
                 out_specs=[pl.BlockSpec((bm, bn), lambda i, j: (i, j))],
                 out_shape=[jax.ShapeDtypeStruct((M, N), out_dtype)], args=args, vmem_limit=limit, carry=carry)[0]


def _mm_tn(a, b, *, bm, bn, out_dtype, name, n_stack=None, carry=None):
    T, M = a.shape
    N = b.shape[1]
    assert M % bm == 0 and N % bn == 0 and (n_stack is None or bn * n_stack == N)
    dn = (((0,), (0,)), ((), ()))

    def body(a_ref, b_ref, o_ref):
        acc = lax.dot_general(a_ref[...], b_ref[...], dn, preferred_element_type=F32)
        o_ref[...] = acc.astype(o_ref.dtype)

    if n_stack is None:
        out_spec = pl.BlockSpec((bm, bn), lambda i, j: (i, j))
        out_shape = jax.ShapeDtypeStruct((M, N), out_dtype)
    else:
        out_spec = pl.BlockSpec((None, bm, bn), lambda i, j: (j, i, 0))
        out_shape = jax.ShapeDtypeStruct((n_stack, M, bn), out_dtype)
    limit = _vmem_limit(_nbytes((T, bm), a.dtype), _nbytes((T, bn), b.dtype), _nbytes((bm, bn), F32))
    return _call(body, name=name, grid=(M // bm, N // bn),
                 in_specs=[pl.BlockSpec((T, bm), lambda i, j: (0, i)), pl.BlockSpec((T, bn), lambda i, j: (0, j))],
                 out_specs=[out_spec], out_shape=[out_shape], args=[a, b], vmem_limit=limit, carry=carry)[0]


ROW_BLK = 256


def _rstd(x):
    return lax.rsqrt(jnp.mean(x * x, axis=-1, keepdims=True) + EPS)


def _norm_bwd(xhat, rstd, dxhat):
    return rstd * (dxhat - xhat * jnp.mean(dxhat * xhat, axis=-1, keepdims=True))


def _row_spec(cols):
    return pl.BlockSpec((ROW_BLK, cols), lambda i: (i, 0))


def _vec_spec(cols):
    return pl.BlockSpec((1, cols), lambda i: (0, 0))


def _rms_fwd(x, g, *, name):
    T, D = x.shape

    def body(x_ref, g_ref, o_ref):
        xv = x_ref[...]
        o_ref[...] = (xv * _rstd(xv) * g_ref[...]).astype(o_ref.dtype)

    return pl.pallas_call(
        body, name=name, grid=(T // ROW_BLK,), in_specs=[_row_spec(D), _vec_spec(D)], out_specs=_row_spec(D),
        out_shape=jax.ShapeDtypeStruct((T, D), BF16),
        compiler_params=pltpu.CompilerParams(dimension_semantics=("parallel",)),
    )(x, g)


def _resid_norm(h, f, g_post, coef, g_next, *, name):
    T, D = h.shape

    def body(h_ref, f_ref, gp_ref, gn_ref, hn_ref, n_ref):
        fv = f_ref[...]
        hn = h_ref[...] + coef * (fv * _rstd(fv) * gp_ref[...])
        hn_ref[...] = hn
        n_ref[...] = (hn * _rstd(hn) * gn_ref[...]).astype(n_ref.dtype)

    return pl.pallas_call(
        body, name=name, grid=(T // ROW_BLK,),
        in_specs=[_row_spec(D), _row_spec(D), _vec_spec(D), _vec_spec(D)],
        out_specs=(_row_spec(D), _row_spec(D)),
        out_shape=(jax.ShapeDtypeStruct((T, D), F32), jax.ShapeDtypeStruct((T, D), BF16)),
        compiler_params=pltpu.CompilerParams(dimension_semantics=("parallel",)),
    )(h, f, g_post, g_next)


def _swiglu_fwd(gu, *, name):
    T, F2 = gu.shape
    F = F2 // 2

    def body(gu_ref, a_ref):
        g = gu_ref[:, 0:F].astype(F32)
        u = gu_ref[:, F:F2].astype(F32)
        a_ref[...] = (g * jax.nn.sigmoid(g) * u).astype(a_ref.dtype)

    return pl.pallas_call(
        body, name=name, grid=(T // ROW_BLK,), in_specs=[_row_spec(F2)], out_specs=_row_spec(F),
        out_shape=jax.ShapeDtypeStruct((T, F), BF16),
        compiler_params=pltpu.CompilerParams(dimension_semantics=("parallel",)),
    )(gu)


def _swiglu_bwd(gu, da, *, name):
    T, F2 = gu.shape
    F = F2 // 2

    def body(gu_ref, da_ref, o_ref):
        g = gu_ref[:, 0:F].astype(F32)
        u = gu_ref[:, F:F2].astype(F32)
        dav = da_ref[...].astype(F32)
        s = jax.nn.sigmoid(g)
        o_ref[:, 0:F] = (dav * u * (s * (1.0 + g * (1.0 - s)))).astype(o_ref.dtype)
        o_ref[:, F:F2] = (dav * (g * s)).astype(o_ref.dtype)

    return pl.pallas_call(
        body, name=name, grid=(T // ROW_BLK,), in_specs=[_row_spec(F2), _row_spec(F)], out_specs=_row_spec(F2),
        out_shape=jax.ShapeDtypeStruct((T, F2), BF16),
        compiler_params=pltpu.CompilerParams(dimension_semantics=("parallel",)),
    )(gu, da)


def _ple_loss(h3, gp, e, g_post, target, *, name):
    T, D = h3.shape

    def body(h_ref, gp_ref, e_ref, g_ref, t_ref, dh_ref, de_ref, dgp_ref, loss_ref, dg_ref):
        @pl.when(pl.program_id(0) == 0)
        def _():
            loss_ref[...] = jnp.zeros_like(loss_ref)
            dg_ref[...] = jnp.zeros_like(dg_ref)

        gate = jax.nn.sigmoid(gp_ref[...])
        ev = e_ref[...]
        ge = gate * ev
        r = _rstd(ge)
        gh = ge * r
        gpost = g_ref[...]
        diff = h_ref[...] + gh * gpost - t_ref[...]
        loss_ref[...] += jnp.sum(diff * diff) * (0.5 / D)
        dh = diff * (1.0 / D)
        dh_ref[...] = dh
        dg_ref[...] += jnp.sum(dh * gh, axis=0, keepdims=True)
        dge = _norm_bwd(gh, r, dh * gpost)
        de_ref[...] = (dge * gate).astype(de_ref.dtype)
        dgp_ref[...] = (dge * ev * gate * (1.0 - gate)).astype(dgp_ref.dtype)

    return pl.pallas_call(
        body, name=name, grid=(T // ROW_BLK,),
        in_specs=[_row_spec(D), _row_spec(D), _row_spec(D), _vec_spec(D), _row_spec(D)],
        out_specs=(_row_spec(D), _row_spec(D), _row_spec(D), _vec_spec(LANES), _vec_spec(D)),
        out_shape=(jax.ShapeDtypeStruct((T, D), F32), jax.ShapeDtypeStruct((T, D), BF16),
                   jax.ShapeDtypeStruct((T, D), BF16), jax.ShapeDtypeStruct((1, LANES), F32),
                   jax.ShapeDtypeStruct((1, D), F32)),
        compiler_params=pltpu.CompilerParams(dimension_semantics=("arbitrary",)),
    )(h3, gp, e, g_post, target)


def _norms_bwd(dh_in, h, g_pre, dn, f, g_post, coef, *, name):
    T, D = h.shape

    def body(dhi_ref, h_ref, gpre_ref, dn_ref, f_ref, gpost_ref, dh_ref, df_ref, dgpre_ref, dgpost_ref, dsum_ref):
        @pl.when(pl.program_id(0) == 0)
        def _():
            dgpre_ref[...] = jnp.zeros_like(dgpre_ref)
            dgpost_ref[...] = jnp.zeros_like(dgpost_ref)
            dsum_ref[...] = jnp.zeros_like(dsum_ref)

        hv = h_ref[...]
        r = _rstd(hv)
        hh = hv * r
        dnv = dn_ref[...]
        dgpre_ref[...] += jnp.sum(dnv * hh, axis=0, keepdims=True)
        dh = dhi_ref[...] + _norm_bwd(hh, r, dnv * gpre_ref[...])
        dh_ref[...] = dh
        fv = f_ref[...]
        rf = _rstd(fv)
        fh = fv * rf
        dy = coef * dh
        dgpost_ref[...] += jnp.sum(dy * fh, axis=0, keepdims=True)
        df = _norm_bwd(fh, rf, dy * gpost_ref[...])
        dsum_ref[...] += jnp.sum(df, axis=0, keepdims=True)
        df_ref[...] = df.astype(df_ref.dtype)

    return pl.pallas_call(
        body, name=name, grid=(T // ROW_BLK,),
        in_specs=[_row_spec(D), _row_spec(D), _vec_spec(D), _row_spec(D), _row_spec(D), _vec_spec(D)],
        out_specs=(_row_spec(D), _row_spec(D), _vec_spec(D), _vec_spec(D), _vec_spec(D)),
        out_shape=(jax.ShapeDtypeStruct((T, D), F32), jax.ShapeDtypeStruct((T, D), BF16),
                   jax.ShapeDtypeStruct((1, D), F32), jax.ShapeDtypeStruct((1, D), F32),
                   jax.ShapeDtypeStruct((1, D), F32)),
        compiler_params=pltpu.CompilerParams(dimension_semantics=("arbitrary",)),
    )(dh_in, h, g_pre, dn, f, g_post)


def _norm_bwd_last(dh_in, h, g_pre, dn, *, name):
    T, D = h.shape

    def body(dhi_ref, h_ref, gpre_ref, dn_ref, dh_ref, dgpre_ref):
        @pl.when(pl.program_id(0) == 0)
        def _():
            dgpre_ref[...] = jnp.zeros_like(dgpre_ref)

        hv = h_ref[...]
        r = _rstd(hv)
        hh = hv * r
        dnv = dn_ref[...]
        dgpre_ref[...] += jnp.sum(dnv * hh, axis=0, keepdims=True)
        dh_ref[...] = dhi_ref[...] + _norm_bwd(hh, r, dnv * gpre_ref[...])

    return pl.pallas_call(
        body, name=name, grid=(T // ROW_BLK,),
        in_specs=[_row_spec(D), _row_spec(D), _vec_spec(D), _row_spec(D)],
        out_specs=(_row_spec(D), _vec_spec(D)),
        out_shape=(jax.ShapeDtypeStruct((T, D), F32), jax.ShapeDtypeStruct((1, D), F32)),
        compiler_params=pltpu.CompilerParams(dimension_semantics=("arbitrary",)),
    )(dh_in, h, g_pre, dn)


def _t5_index(max_dist, stride):
    rho = np.arange(QBLK)[:, None]
    kap = np.arange(2 * QBLK)[None, :]
    d = rho + QBLK - kap
    valid = (d >= 0) & (d <= max_dist)
    n = np.maximum(d, 0) * stride
    max_exact = NUM_BUCKETS // 2
    nf = np.maximum(n, 1).astype(np.float32)
    large = max_exact + (np.log(nf / np.float32(max_exact)) / np.float32(math.log(MAX_DISTANCE / max_exact))
                         * np.float32(NUM_BUCKETS - max_exact)).astype(np.int32)
    large = np.minimum(large, NUM_BUCKETS - 1)
    bucket = np.where(n < max_exact, n, large)
    return np.where(valid, bucket, -1).astype(np.int32)


def _bias_tiles(rel_bias, idx, head_off, *, name):
    def body(rb_ref, idx_ref, o_ref):
        h = pl.program_id(0)
        idxv = idx_ref[...]
        acc = jnp.where(idxv < 0, NEG, 0.0).astype(F32)
        for b in range(NUM_BUCKETS):
            acc = acc + jnp.where(idxv == b, rb_ref[b, head_off + h], 0.0)
        o_ref[0] = acc

    return pl.pallas_call(
        body, name=name, grid=(2 * N_PAIRS,),
        in_specs=[pl.BlockSpec(memory_space=pltpu.SMEM), pl.BlockSpec((QBLK, 2 * QBLK), lambda h: (0, 0))],
        out_specs=pl.BlockSpec((1, QBLK, 2 * QBLK), lambda h: (h, 0, 0)),
        out_shape=jax.ShapeDtypeStruct((2 * N_PAIRS, QBLK, 2 * QBLK), F32),
        compiler_params=pltpu.CompilerParams(dimension_semantics=("parallel",)),
    )(rel_bias, idx)


def _bias_grad(d_bias, idx, *, name):
    def body(db_ref, idx_ref, o_ref):
        h = pl.program_id(0)

        @pl.when(h == 0)
        def _():
            o_ref[...] = jnp.zeros_like(o_ref)

        idxv = idx_ref[...]
        dv = db_ref[0]
        rows = lax.broadcasted_iota(jnp.int32, (NUM_BUCKETS, LANES), 0)
        lanes = lax.broadcasted_iota(jnp.int32, (NUM_BUCKETS, LANES), 1)
        acc = o_ref[...]
        for b in range(NUM_BUCKETS):
            s = jnp.sum(jnp.where(idxv == b, dv, 0.0))
            acc = acc + jnp.where((rows == b) & (lanes == h), s, 0.0)
        o_ref[...] = acc

    return pl.pallas_call(
        body, name=name, grid=(2 * N_PAIRS,),
        in_specs=[pl.BlockSpec((1, QBLK, 2 * QBLK), lambda h: (h, 0, 0)),
                  pl.BlockSpec((QBLK, 2 * QBLK), lambda h: (0, 0))],
        out_specs=pl.BlockSpec((NUM_BUCKETS, LANES), lambda h: (0, 0)),
        out_shape=jax.ShapeDtypeStruct((NUM_BUCKETS, LANES), F32),
        compiler_params=pltpu.CompilerParams(dimension_semantics=("arbitrary",)),
    )(d_bias, idx)


def _lane():
    return lax.broadcasted_iota(jnp.int32, (1, LANES), 1)


def _head_sel(h):
    return (_lane() < HEAD_DIM) if h == 0 else (_lane() >= HEAD_DIM)


def _stat_lo():
    return (_lane() % HEAD_DIM) < (HEAD_DIM // 2)


def _first_block_mask(b):
    kap = lax.broadcasted_iota(jnp.int32, (1, 2 * QBLK), 1)
    return jnp.where((kap < QBLK) & (b == 0), NEG, 0.0).astype(F32)


def _band_specs(r, nb, cols, stride, clamp):
    def qb(b):
        return jnp.minimum(b, nb - 1) if clamp else b

    blk = (QBLK, MIX_W)
    cur = pl.BlockSpec(blk, lambda j, b: (qb(b), cols + j * stride))
    prev = pl.BlockSpec(blk, lambda j, b: (jnp.maximum(qb(b) - 1, 0), cols + j * stride))
    return cur, prev


def _band_fwd(qa, ka, va, bias, *, r, q_col, k_col, v_col, stride, name, carry=None):
    L = qa.shape[0]
    nb = L // QBLK
    dn = (((1,), (1,)), ((), ()))
    scale = HEAD_DIM ** -0.5

    def body(q_ref, kp_ref, kc_ref, vp_ref, vc_ref, bias_ref, o_ref, lse_ref):
        first = _first_block_mask(pl.program_id(1))
        sel0 = _head_sel(0)
        for i in range(N_PAIRS):
            cols = slice(i * LANES, (i + 1) * LANES)
            q = q_ref[:, cols]
            k = jnp.concatenate([kp_ref[:, cols], kc_ref[:, cols]], axis=0)
            v = jnp.concatenate([vp_ref[:, cols], vc_ref[:, cols]], axis=0)
            res = []
            for h in range(2):
                qh = jnp.where(_head_sel(h), q, jnp.zeros_like(q))
                s = lax.dot_general(qh, k, dn, preferred_element_type=F32) * scale + bias_ref[2 * i + h] + first
                m = jnp.max(s, axis=1, keepdims=True)
                p = jnp.exp(s - m)
                l = jnp.sum(p, axis=1, keepdims=True)
                o = jnp.dot(p.astype(v.dtype), v, preferred_element_type=F32)
                res.append((o / l, m + jnp.log(l)))
            o_ref[:, cols] = jnp.where(sel0, res[0][0], res[1][0]).astype(o_ref.dtype)
            lse_ref[:, cols] = jnp.where(sel0, res[0][1], res[1][1])

    q_spec, _ = _band_specs(r, nb, q_col, stride, False)
    kc_spec, kp_spec = _band_specs(r, nb, k_col, stride, False)
    vc_spec, vp_spec = _band_specs(r, nb, v_col, stride, False)
    bias_spec = pl.BlockSpec((2 * N_PAIRS, QBLK, 2 * QBLK), lambda j, b: (0, 0, 0))
    out_spec = pl.BlockSpec((QBLK, MIX_W), lambda j, b: (b, j))
    return _call(
        body, name=name, grid=(r, nb),
        in_specs=[q_spec, kp_spec, kc_spec, vp_spec, vc_spec, bias_spec], out_specs=[out_spec, out_spec],
        out_shape=[jax.ShapeDtypeStruct((L, r * MIX_W), BF16), jax.ShapeDtypeStruct((L, r * MIX_W), F32)],
        args=[qa, ka, ka, va, va, bias], carry=carry)


def _merge(branches, sink, *, name):
    T, W = branches[0][0].shape
    nbr = len(branches)

    def body(*refs):
        sink_ref = refs[2 * nbr]
        out_ref, outb_ref, lse_ref = refs[2 * nbr + 1:]
        sk = sink_ref[...]
        mx = sk
        for t in range(nbr):
            mx = jnp.maximum(mx, refs[2 * t + 1][...])
        den = jnp.exp(sk - mx)
        num = jnp.zeros_like(mx)
        for t in range(nbr):
            w = jnp.exp(refs[2 * t + 1][...] - mx)
            den = den + w
            num = num + w * refs[2 * t][...].astype(F32)
        out = num / den
        out_ref[...] = out
        outb_ref[...] = out.astype(outb_ref.dtype)
        lse_ref[...] = mx + jnp.log(den)

    args = [a for br in branches for a in br] + [sink]
    return pl.pallas_call(
        body, name=name, grid=(T // ROW_BLK,),
        in_specs=[_row_spec(W)] * (2 * nbr) + [_vec_spec(W)],
        out_specs=(_row_spec(W), _row_spec(W), _row_spec(W)),
        out_shape=(jax.ShapeDtypeStruct((T, W), F32), jax.ShapeDtypeStruct((T, W), BF16),
                   jax.ShapeDtypeStruct((T, W), F32)),
        compiler_params=pltpu.CompilerParams(dimension_semantics=("parallel",)),
    )(*args)


def _attn_delta(dmix, outs, lses, sink, *, name):
    T = dmix.shape[0]
    W = MIX_W

    def body(dmix_ref, oa_ref, ob_ref, la_ref, lb_ref, sink_ref, doa_ref, dob_ref, sta_ref, stb_ref, dsink_ref):
        @pl.when(pl.program_id(0) == 0)
        def _():
            dsink_ref[...] = jnp.zeros_like(dsink_ref)

        sel0, lo = _head_sel(0), _stat_lo()
        for mixer, (o_ref, l_ref, do_ref, st_ref) in enumerate(
                ((oa_ref, la_ref, doa_ref, sta_ref), (ob_ref, lb_ref, dob_ref, stb_ref))):
            for i in range(N_PAIRS):
                cols = slice(i * LANES, (i + 1) * LANES)
                do = dmix_ref[:, mixer * W + i * LANES:mixer * W + (i + 1) * LANES]
                prod = do * o_ref[:, cols]
                d0 = jnp.sum(jnp.where(sel0, prod, 0.0), axis=1, keepdims=True)
                d1 = jnp.sum(jnp.where(sel0, 0.0, prod), axis=1, keepdims=True)
                delta = jnp.where(sel0, d0, d1)
                lse = l_ref[:, cols]
                st_ref[:, cols] = jnp.where(lo, lse, delta)
                do_ref[:, cols] = do.astype(do_ref.dtype)
                if mixer == 0:
                    dsink_ref[:, cols] += -jnp.sum(jnp.exp(sink_ref[:, cols] - lse) * delta, axis=0, keepdims=True)

    return pl.pallas_call(
        body, name=name, grid=(T // ROW_BLK,),
        in_specs=[_row_spec(2 * W), _row_spec(W), _row_spec(W), _row_spec(W), _row_spec(W), _vec_spec(W)],
        out_specs=(_row_spec(W), _row_spec(W), _row_spec(W), _row_spec(W), _vec_spec(W)),
        out_shape=(jax.ShapeDtypeStruct((T, W), BF16), jax.ShapeDtypeStruct((T, W), BF16),
                   jax.ShapeDtypeStruct((T, W), F32), jax.ShapeDtypeStruct((T, W), F32),
                   jax.ShapeDtypeStruct((1, W), F32)),
        compiler_params=pltpu.CompilerParams(dimension_semantics=("arbitrary",)),
    )(dmix, outs[0], outs[1], lses[0], lses[1], sink)


def _band_bwd(qa, ka, va, d_out, stat, bias, *, r, q_col, k_col, v_col, stride, name, carry=None):
    L = qa.shape[0]
    nb = L // QBLK
    dn_nt = (((1,), (1,)), ((), ()))
    dn_tn = (((0,), (0,)), ((), ()))
    scale = HEAD_DIM ** -0.5

    def body(q_ref, kp_ref, kc_ref, vp_ref, vc_ref, do_ref, st_ref, bias_ref,
             dq_ref, dk_ref, dv_ref, db_ref, dk_carry, dv_carry):
        j = pl.program_id(0)
        b = pl.program_id(1)

        @pl.when((j == 0) & (b == 0))
        def _():
            db_ref[...] = jnp.zeros_like(db_ref)

        @pl.when(b < nb)
        def _():
            first = _first_block_mask(b)
            lo = _stat_lo()
            for i in range(N_PAIRS):
                cols = slice(i * LANES, (i + 1) * LANES)
                q = q_ref[:, cols]
                k = jnp.concatenate([kp_ref[:, cols], kc_ref[:, cols]], axis=0)
                v = jnp.concatenate([vp_ref[:, cols], vc_ref[:, cols]], axis=0)
                do = do_ref[:, cols]
                st = st_ref[:, cols]
                dq_acc = jnp.zeros((QBLK, LANES), F32)
                dk_acc = jnp.zeros((2 * QBLK, LANES), F32)
                dv_acc = jnp.zeros((2 * QBLK, LANES), F32)
                for h in range(2):
                    sel = _head_sel(h)
                    qh = jnp.where(sel, q, jnp.zeros_like(q))
                    doh = jnp.where(sel, do, jnp.zeros_like(do))
                    lse_h = jnp.max(jnp.where(sel & lo, st, -jnp.inf), axis=1, keepdims=True)
                    delta = jnp.sum(jnp.where(sel & ~lo, st, 0.0), axis=1, keepdims=True) * (2.0 / HEAD_DIM)
                    s = lax.dot_general(qh, k, dn_nt, preferred_element_type=F32) * scale + bias_ref[2 * i + h] + first
                    p = jnp.exp(s - lse_h)
                    dp = lax.dot_general(doh, v, dn_nt, preferred_element_type=F32)
                    ds = p * (dp - delta)
                    db_ref[2 * i + h] += ds
                    dsb = ds.astype(q.dtype)
                    dq_acc = jnp.where(sel, jnp.dot(dsb, k, preferred_element_type=F32), dq_acc)
                    dk_acc = dk_acc + lax.dot_general(dsb, qh, dn_tn, preferred_element_type=F32)
                    dv_acc = dv_acc + lax.dot_general(p.astype(q.dtype), doh, dn_tn, preferred_element_type=F32)
                dq_ref[:, cols] = (dq_acc * scale).astype(dq_ref.dtype)
                dk_acc = dk_acc * scale

                @pl.when(b >= 1)
                def _():
                    dk_ref[:, cols] = (dk_carry[:, cols] + dk_acc[0:QBLK]).astype(dk_ref.dtype)
                    dv_ref[:, cols] = (dv_carry[:, cols] + dv_acc[0:QBLK]).astype(dv_ref.dtype)

                dk_carry[:, cols] = dk_acc[QBLK:2 * QBLK]
                dv_carry[:, cols] = dv_acc[QBLK:2 * QBLK]

        @pl.when(b == nb)
        def _():
            dk_ref[...] = dk_carry[...].astype(dk_ref.dtype)
            dv_ref[...] = dv_carry[...].astype(dv_ref.dtype)

    q_spec, _ = _band_specs(r, nb, q_col, stride, True)
    kc_spec, kp_spec = _band_specs(r, nb, k_col, stride, True)
    vc_spec, vp_spec = _band_specs(r, nb, v_col, stride, True)
    blk = (QBLK, MIX_W)
    tok_spec = pl.BlockSpec(blk, lambda j, b: (jnp.minimum(b, nb - 1), j))
    dkv_spec = pl.BlockSpec(blk, lambda j, b: (jnp.maximum(b - 1, 0), j))
    bias_spec = pl.BlockSpec((2 * N_PAIRS, QBLK, 2 * QBLK), lambda j, b: (0, 0, 0))
    grad = jax.ShapeDtypeStruct((L, r * MIX_W), BF16)
    return _call(
        body, name=name, grid=(r, nb + 1),
        in_specs=[q_spec, kp_spec, kc_spec, vp_spec, vc_spec, tok_spec, tok_spec, bias_spec],
        out_specs=[tok_spec, dkv_spec, dkv_spec, bias_spec],
        out_shape=[grad, grad, grad, jax.ShapeDtypeStruct((2 * N_PAIRS, QBLK, 2 * QBLK), F32)],
        args=[qa, ka, ka, va, va, d_out, stat, bias], scratch=[pltpu.VMEM(blk, F32), pltpu.VMEM(blk, F32)],
        semantics=("arbitrary", "arbitrary"), carry=carry)


def _dz_assemble(dq_a, dk_a, dv_a, b_grads, *, name):
    T = dq_a.shape[0]
    W = MIX_W

    def group_sum(x):
        u0 = x[:, 0:LANES] + x[:, LANES:2 * LANES]
        u1 = x[:, 2 * LANES:3 * LANES] + x[:, 3 * LANES:4 * LANES]
        s0 = u0 + pltpu.roll(u0, HEAD_DIM, 1)
        s1 = u1 + pltpu.roll(u1, HEAD_DIM, 1)
        return jnp.where(_head_sel(0), s0, s1)

    def body(*refs):
        dqa_ref, dka_ref, dva_ref = refs[0:3]
        b_refs = refs[3:12]
        dza_ref, dzb_ref, sa_ref, sb_ref = refs[12:]

        @pl.when(pl.program_id(0) == 0)
        def _():
            sa_ref[...] = jnp.zeros_like(sa_ref)
            sb_ref[...] = jnp.zeros_like(sb_ref)

        def put(dst, acc, col, part):
            w = part.shape[1]
            dst[:, col:col + w] = part.astype(dst.dtype)
            acc[:, col:col + w] += jnp.sum(part, axis=0, keepdims=True)

        put(dza_ref, sa_ref, 0, dqa_ref[...].astype(F32))
        put(dza_ref, sa_ref, W, group_sum(dka_ref[...].astype(F32)))
        put(dza_ref, sa_ref, W + LANES, group_sum(dva_ref[...].astype(F32)))
        for t in range(3):
            part = (b_refs[t][...].astype(F32) + b_refs[3 + t][...].astype(F32) + b_refs[6 + t][...].astype(F32))
            put(dzb_ref, sb_ref, t * W, part)

    args = [dq_a, dk_a, dv_a] + [g[t] for g in b_grads for t in range(3)]
    return pl.pallas_call(
        body, name=name, grid=(T // ROW_BLK,),
        in_specs=[_row_spec(W)] * 12,
        out_specs=(_row_spec(ZA_W), _row_spec(ZB_W), _vec_spec(ZA_W), _vec_spec(ZB_W)),
        out_shape=(jax.ShapeDtypeStruct((T, ZA_W), BF16), jax.ShapeDtypeStruct((T, ZB_W), BF16),
                   jax.ShapeDtypeStruct((1, ZA_W), F32), jax.ShapeDtypeStruct((1, ZB_W), F32)),
        compiler_params=pltpu.CompilerParams(dimension_semantics=("arbitrary",)),
    )(*args)


def _place():
    x, y, c = lax.axis_index("x"), lax.axis_index("y"), lax.axis_index("c")
    chips = [(1 - x, y), (x, 1 - y), (1 - x, 1 - y)]
    return x, y, c, chips


def _cast_place(shard, k_idx, *, name):
    rs, cs = shard.shape
    rb = _row_block(rs, 256, 16)
    nblk = rs // rb

    def body(k_ref, s_ref, o_ref):
        o_ref[...] = s_ref[...].astype(o_ref.dtype)

    grid_spec = pltpu.PrefetchScalarGridSpec(
        num_scalar_prefetch=1, grid=(nblk,),
        in_specs=[pl.BlockSpec((rb, cs), lambda t, k_ref: (t, 0))],
        out_specs=pl.BlockSpec((rb, cs), lambda t, k_ref: (k_ref[0] * nblk + t, 0)))
    return pl.pallas_call(
        body, name=name, grid_spec=grid_spec, out_shape=jax.ShapeDtypeStruct((N_CHIPS * rs, cs), BF16),
        compiler_params=pltpu.CompilerParams(dimension_semantics=("parallel",)),
    )(k_idx, shard)


def _gather_carry(fulls, part=(0, 1)):
    n = len(fulls)
    p_idx, n_parts = part

    def piece(ref, chip_idx, half):
        rs = ref.shape[0] // N_CHIPS
        rh = rs // 2
        rows = rh // n_parts
        return ref.at[pl.ds(chip_idx * rs + half * rh + p_idx * rows, rows)]

    def copy(sems, sem, src_ref, dst_ref, to):
        return pltpu.make_async_remote_copy(src_ref=src_ref, dst_ref=dst_ref, send_sem=sems[0].at[sem],
                                            recv_sem=sems[1].at[sem], device_id=to, device_id_type=MESH)

    def ici_copy(ins, outs, sems, i, j, chip, k, c):
        return copy(sems, 3 * i + j, piece(ins[i], k, c), piece(outs[i], k, c), (*chip, c))

    def start(ins, outs, sems):
        x, y, c, chips = _place()
        for i in range(n):
            for j, chip in enumerate(chips):
                ici_copy(ins, outs, sems, i, j, chip, 2 * x + y, c).start()

    def finish(ins, outs, sems):
        x, y, c, chips = _place()
        sibling = (x, y, 1 - c)
        passed = []
        for i in range(n):
            for j, chip in enumerate(chips):
                region = piece(outs[i], 2 * chip[0] + chip[1], c)
                copy(sems, 3 * i + j, region, region, (*chip, c)).wait_recv()
                cp = copy(sems, 3 * n + 3 * i + j, region, region, sibling)
                cp.start()
                passed.append(cp)
        for i in range(n):
            for j, chip in enumerate(chips):
                region = piece(outs[i], 2 * chip[0] + chip[1], 1 - c)
                copy(sems, 3 * n + 3 * i + j, region, region, sibling).wait_recv()
        for i in range(n):
            for j, chip in enumerate(chips):
                ici_copy(ins, outs, sems, i, j, chip, 2 * x + y, c).wait_send()
        for cp in passed:
            cp.wait_send()

    return _Carry(fulls, [jax.ShapeDtypeStruct(a.shape, a.dtype) for a in fulls], {i: i for i in range(n)},
                  [pltpu.SemaphoreType.DMA((6 * n,)), pltpu.SemaphoreType.DMA((6 * n,))], start, finish)


def _pair_swap_carry(grads):
    n = len(grads)

    def copy(ins, outs, sems, i):
        x, y, c, _ = _place()
        return pltpu.make_async_remote_copy(src_ref=ins[i].at[:, 1 - c], dst_ref=outs[i], send_sem=sems[0].at[i],
                                            recv_sem=sems[1].at[i], device_id=(x, y, 1 - c), device_id_type=MESH)

    def start(ins, outs, sems):
        for i in range(n):
            copy(ins, outs, sems, i).start()

    def finish(ins, outs, sems):
        for i in range(n):
            copy(ins, outs, sems, i).wait()

    return _Carry(grads, [jax.ShapeDtypeStruct((a.shape[0], a.shape[2], a.shape[3]), a.dtype) for a in grads], {},
                  [pltpu.SemaphoreType.DMA((n,)), pltpu.SemaphoreType.DMA((n,))], start, finish)


def _pair_sum(g4, got, c_idx, *, name):
    _, _, rh, cs = g4.shape
    rb = _row_block(rh, 256, 16)

    def body(c_ref, own_ref, got_ref, o_ref):
        o_ref[...] = (own_ref[...].astype(F32) + got_ref[...].astype(F32)).astype(o_ref.dtype)

    grid_spec = pltpu.PrefetchScalarGridSpec(
        num_scalar_prefetch=1, grid=(N_CHIPS, rh // rb),
        in_specs=[pl.BlockSpec((None, None, rb, cs), lambda k, t, c_ref: (k, c_ref[0], t, 0)),
                  pl.BlockSpec((None, rb, cs), lambda k, t, c_ref: (k, t, 0))],
        out_specs=pl.BlockSpec((None, rb, cs), lambda k, t, c_ref: (k, t, 0)))
    return pl.pallas_call(
        body, name=name, grid_spec=grid_spec, out_shape=jax.ShapeDtypeStruct((N_CHIPS, rh, cs), BF16),
        compiler_params=pltpu.CompilerParams(dimension_semantics=("parallel", "parallel")),
    )(c_idx, g4, got)


def _chip_scatter_carry(sums):
    n = len(sums)

    def copies(ins, outs, sems):
        x, y, c, chips = _place()
        return [pltpu.make_async_remote_copy(src_ref=ins[i].at[2 * chip[0] + chip[1]], dst_ref=outs[i].at[j],
                                             send_sem=sems[0].at[3 * i + j], recv_sem=sems[1].at[3 * i + j],
                                             device_id=(*chip, c), device_id_type=MESH)
                for i in range(n) for j, chip in enumerate(chips)]

    def start(ins, outs, sems):
        for cp in copies(ins, outs, sems):
            cp.start()

    def finish(ins, outs, sems):
        for cp in copies(ins, outs, sems):
            cp.wait()

    return _Carry(sums, [jax.ShapeDtypeStruct((3,) + a.shape[1:], a.dtype) for a in sums], {},
                  [pltpu.SemaphoreType.DMA((3 * n,)), pltpu.SemaphoreType.DMA((3 * n,))], start, finish)


def _chip_sum(sums, got, kc_idx, *, name):
    _, rh, cs = sums.shape
    rb = _row_block(rh, 256, 16)
    nblk = rh // rb

    def body(kc_ref, own_ref, got_ref, o_ref):
        acc = own_ref[...].astype(F32)
        for j in range(3):
            acc = acc + got_ref[j].astype(F32)
        o_ref[...] = acc

    grid_spec = pltpu.PrefetchScalarGridSpec(
        num_scalar_prefetch=1, grid=(nblk,),
        in_specs=[pl.BlockSpec((None, rb, cs), lambda t, kc: (kc[0], t, 0)),
                  pl.BlockSpec((3, rb, cs), lambda t, kc: (0, t, 0))],
        out_specs=pl.BlockSpec((rb, cs), lambda t, kc: (kc[1] * nblk + t, 0)))
    return pl.pallas_call(
        body, name=name, grid_spec=grid_spec, out_shape=jax.ShapeDtypeStruct((2 * rh, cs), F32),
        compiler_params=pltpu.CompilerParams(dimension_semantics=("parallel",)),
    )(kc_idx, sums, got)


def _half_swap_carry(shards):
    n = len(shards)

    def copy(ins, outs, sems, i, to_my_half):
        x, y, c, _ = _place()
        rh = ins[i].shape[0] // 2
        half = c if to_my_half else 1 - c
        return pltpu.make_async_remote_copy(
            src_ref=ins[i].at[pl.ds(c * rh, rh)], dst_ref=outs[i].at[pl.ds(half * rh, rh)],
            send_sem=sems[0].at[i], recv_sem=sems[1].at[i], device_id=(x, y, 1 - c), device_id_type=MESH)

    def start(ins, outs, sems):
        for i in range(n):
            copy(ins, outs, sems, i, True).start()

    def finish(ins, outs, sems):
        for i in range(n):
            copy(ins, outs, sems, i, False).wait_recv()
        for i in range(n):
            copy(ins, outs, sems, i, True).wait_send()

    return _Carry(shards, [jax.ShapeDtypeStruct(a.shape, a.dtype) for a in shards], {i: i for i in range(n)},
                  [pltpu.SemaphoreType.DMA((n,)), pltpu.SemaphoreType.DMA((n,))], start, finish)


def _small_all_reduce(block, *, name):
    rows = block.shape[0]

    def body(x_ref, o_ref, buf, send_sems, recv_sems):
        x, y, c, _ = _place()
        me = 4 * x + 2 * y + c
        buf[me] = x_ref[...]
        copies = []
        for d in range(1, 8):
            to = (1 - x if d & 4 else x, 1 - y if d & 2 else y, 1 - c if d & 1 else c)
            cp = pltpu.make_async_remote_copy(src_ref=x_ref, dst_ref=buf.at[me], send_sem=send_sems.at[d - 1],
                                              recv_sem=recv_sems.at[d - 1], device_id=to, device_id_type=MESH)
            cp.start()
            copies.append((cp, to))
        for d in range(1, 8):
            to = copies[d - 1][1]
            sender = 4 * to[0] + 2 * to[1] + to[2]
            pltpu.make_async_remote_copy(src_ref=x_ref, dst_ref=buf.at[sender], send_sem=send_sems.at[d - 1],
                                         recv_sem=recv_sems.at[d - 1], device_id=to, device_id_type=MESH).wait_recv()
        for cp, _ in copies:
            cp.wait_send()
        acc = buf[0]
        for s in range(1, 8):
            acc = acc + buf[s]
        o_ref[...] = acc

    vm = pl.BlockSpec(memory_space=pltpu.VMEM)
    return pl.pallas_call(
        body, name=name, in_specs=[vm], out_specs=vm, out_shape=jax.ShapeDtypeStruct(block.shape, F32),
        scratch_shapes=[pltpu.VMEM((8, rows, LANES), F32), pltpu.SemaphoreType.DMA((7,)),
                        pltpu.SemaphoreType.DMA((7,))],
    )(block)


def _adamw(w, g, m, v, *, name):
    R, C = w.shape
    rb = _row_block(R, 256, SUBLANES)
    c1 = 1.0 - ADAM_B1 ** ADAM_STEP
    c2 = 1.0 - ADAM_B2 ** ADAM_STEP

    def body(w_ref, g_ref, m_ref, v_ref, d_ref, mo_ref, vo_ref):
        gv = g_ref[...]
        mn = ADAM_B1 * m_ref[...] + (1.0 - ADAM_B1) * gv
        vn = ADAM_B2 * v_ref[...] + (1.0 - ADAM_B2) * (gv * gv)
        d_ref[...] = -ADAM_LR * ((mn / c1) / (jnp.sqrt(vn / c2) + ADAM_EPS) + ADAM_WD * w_ref[...])
        mo_ref[...] = mn
        vo_ref[...] = vn

    spec = pl.BlockSpec((rb, C), lambda i: (i, 0))
    shp = jax.ShapeDtypeStruct((R, C), F32)
    return pl.pallas_call(
        body, name=name, grid=(R // rb,), in_specs=[spec] * 4, out_specs=(spec, spec, spec),
        out_shape=(shp, shp, shp), compiler_params=pltpu.CompilerParams(dimension_semantics=("parallel",)),
    )(w, g, m, v)


def _pack(parts):
    rows = []
    for a in parts:
        flat = a.reshape(-1).astype(F32)
        pad = (-flat.shape[0]) % PACK_UNIT
        rows.append(jnp.pad(flat, (0, pad)).reshape(-1, LANES))
    return jnp.concatenate(rows, axis=0)


def _unpack(block, shapes):
    out, row = [], 0
    for shp in shapes:
        size = int(np.prod(shp))
        nrows = -(-size // PACK_UNIT) * SUBLANES
        out.append(block[row:row + nrows].reshape(-1)[:size].reshape(shp))
        row += nrows
    return out


def _local_step(x, p, target, small, sched):
    T = x.shape[0]
    W = MIX_W
    rel_bias = small["rel_bias"]
    b_in_a, b_in_b = small["b_in"][:, 0:ZA_W], small["b_in"][:, ZA_W:]

    n1 = _rms_fwd(x, small["ffn1_pre_g"], name="n1")
    w_gu1 = sched.weight("ffn1_w_gu")
    gu1 = sched.run(_mm_nn, n1, w_gu1, bm=1024, bn=1408, out_dtype=BF16, name="ffn1_gu")
    a1 = _swiglu_fwd(gu1, name="ffn1_act")
    w_d1 = sched.weight("ffn1_w_down")
    f1 = sched.run(_mm_nn, a1, w_d1, bm=1024, bn=512, out_dtype=F32, name="ffn1_down")
    h1, n2 = _resid_norm(x, f1, small["ffn1_post_g"], 0.5, small["attn_pre_g"], name="h1")
    win_a, win_b = sched.weight("w_in")
    za = _mm_nn(n2, win_a, bm=1024, bn=ZA_W, out_dtype=BF16, name="in_proj_a", bias=b_in_a)
    zb = _mm_nn(n2, win_b, bm=1024, bn=ZB_W // 2, out_dtype=BF16, name="in_proj_b", bias=b_in_b)

    k_a = za[:, W:W + LANES].reshape(T, 2, 1, HEAD_DIM)
    v_a = za[:, W + LANES:W + 2 * LANES].reshape(T, 2, 1, HEAD_DIM)
    k_a = jnp.broadcast_to(k_a, (T, 2, 4, HEAD_DIM)).reshape(T, W)
    v_a = jnp.broadcast_to(v_a, (T, 2, 4, HEAD_DIM)).reshape(T, W)
    idx_a = jnp.asarray(_t5_index(A_WINDOW - 1, 1))
    bias_a = _bias_tiles(rel_bias, idx_a, 0, name="bias_a")
    a_args = dict(r=1, q_col=0, k_col=0, v_col=0, stride=0)
    o_a, lse_a1 = sched.run(_band_fwd, za, k_a, v_a, bias_a, name="band_a_fwd", **a_args)
    sink_row = jnp.repeat(small["sinks"], HEAD_DIM, axis=1)
    out_a, out_a_bf, lse_a = _merge([(o_a, lse_a1)], sink_row, name="merge_a")

    no_sink = jnp.full((1, W), NEG, F32)
    b_ctx, branches = [], []
    for t, (window, dil) in enumerate(B_PATTERNS):
        idx = jnp.asarray(_t5_index(window // dil, dil))
        bias = _bias_tiles(rel_bias, idx, 2 * N_PAIRS, name=f"bias_b{t}")
        zb_r = zb.reshape(T // dil, dil * ZB_W)
        args = dict(r=dil, q_col=0, k_col=1, v_col=2, stride=ZB_W // W)
        o, l = sched.run(_band_fwd, zb_r, zb_r, zb_r, bias, name=f"band_b{t}_fwd", **args)
        branches.append((o.reshape(T, W), l.reshape(T, W)))
        b_ctx.append((idx, bias, zb_r, args, dil))
    out_b, out_b_bf, lse_b = _merge(branches, no_sink, name="merge_b")

    mix = jnp.concatenate([out_a_bf, out_b_bf], axis=1)
    w_out = sched.weight("w_out")
    att = _mm_nn(mix, w_out, bm=1024, bn=1024, out_dtype=F32, name="out_proj", bias=small["b_out"])
    h2, n3 = _resid_norm(h1, att, small["attn_post_g"], 1.0, small["ffn2_pre_g"], name="h2")
    w_gu2, w_d2 = sched.weight("ffn2_w_gu"), sched.weight("ffn2_w_down")
    gu2 = _mm_nn(n3, w_gu2, bm=1024, bn=1408, out_dtype=BF16, name="ffn2_gu")
    a2 = _swiglu_fwd(gu2, name="ffn2_act")
    f2 = _mm_nn(a2, w_d2, bm=1024, bn=512, out_dtype=F32, name="ffn2_down")
    h3, n4 = _resid_norm(h2, f2, small["ffn2_post_g"], 0.5, small["ple_pre_g"], name="h3")
    w_gate = sched.weight("w_ple_gate")
    gp = _mm_nn(n4, w_gate, bm=1024, bn=1024, out_dtype=F32, name="ple_gate")
    p_bf = p.astype(BF16)
    e = _mm_nn(p_bf, sched.weight("w_ple_proj"), bm=1024, bn=256, out_dtype=F32, name="ple_proj")
    dh4, de, dgp, loss, d_ple_post = _ple_loss(h3, gp, e, small["ple_post_g"], target, name="ple_loss")

    gs = {"ple_post_g": d_ple_post}
    sched.grad("w_ple_proj", _mm_tn(p_bf, de, bm=256, bn=256, out_dtype=BF16, name="d_w_ple", n_stack=N_CHIPS))
    sched.grad("w_ple_gate", _mm_tn(n4, dgp, bm=512, bn=1024, out_dtype=BF16, name="d_w_gate"))
    dn4 = sched.run(_mm_nt, dgp, w_gate, bm=1024, bn=1024, out_dtype=F32, name="d_n4")
    dh3, df2, gs["ple_pre_g"], gs["ffn2_post_g"], _ = _norms_bwd(
        dh4, h3, small["ple_pre_g"], dn4, f2, small["ffn2_post_g"], 0.5, name="bwd_h3")

    def ffn_bwd(df, a, gu, n, w_down, w_gu, tag):
        da = sched.run(_mm_nt, df, w_down, bm=1024, bn=1408, out_dtype=BF16, name=f"ffn{tag}_d_a")
        sched.grad(f"ffn{tag}_w_down", _mm_tn(a, df, bm=256, bn=1024, out_dtype=BF16, name=f"ffn{tag}_d_w_down"))
        dgu = _swiglu_bwd(gu, da, name=f"ffn{tag}_d_gu")
        sched.grad(f"ffn{tag}_w_gu", sched.run(_mm_tn, n, dgu, bm=512, bn=1408, out_dtype=BF16,
                                              name=f"ffn{tag}_d_w_gu", n_stack=N_CHIPS))
        return sched.run(_mm_nt, dgu, w_gu, bm=512, bn=1024, out_dtype=F32, name=f"ffn{tag}_d_n")

    dn3 = ffn_bwd(df2, a2, gu2, n3, w_d2, w_gu2, "2")
    dh2, datt, gs["ffn2_pre_g"], gs["attn_post_g"], gs["b_out"] = _norms_bwd(
        dh3, h2, small["ffn2_pre_g"], dn3, att, small["attn_post_g"], 1.0, name="bwd_h2")
    sched.grad("w_out", _mm_tn(mix, datt, bm=512, bn=1024, out_dtype=BF16, name="d_w_out"))
    dmix = sched.run(_mm_nt, datt, w_out, bm=1024, bn=1024, out_dtype=F32, name="d_mix")

    do_a, do_b, stat_a, stat_b, dsink = _attn_delta(dmix, (out_a, out_b), (lse_a, lse_b), sink_row, name="attn_delta")
    gs["sinks"] = dsink[:, ::HEAD_DIM]
    dq_a, dk_a, dv_a, dbias_a = sched.run(_band_bwd, za, k_a, v_a, do_a, stat_a, bias_a, name="band_a_bwd", **a_args)
    d_rel_a = _bias_grad(dbias_a, idx_a, name="d_rel_a")
    b_grads = []
    d_rel_b = None
    for t, (idx, bias, zb_r, args, dil) in enumerate(b_ctx):
        shp = (T // dil, dil * W)
        dq, dk, dv, dbias = sched.run(_band_bwd, zb_r, zb_r, zb_r, do_b.reshape(shp), stat_b.reshape(shp), bias,
                                      name=f"band_b{t}_bwd", **args)
        b_grads.append((dq.reshape(T, W), dk.reshape(T, W), dv.reshape(T, W)))
        d_rel = _bias_grad(dbias, idx, name=f"d_rel_b{t}")
        d_rel_b = d_rel if d_rel_b is None else d_rel_b + d_rel
    gs["rel_bias"] = jnp.concatenate([d_rel_a[:, 0:2 * N_PAIRS], d_rel_b[:, 0:2 * N_PAIRS]], axis=1)
    dza, dzb, dsum_a, dsum_b = _dz_assemble(dq_a, dk_a, dv_a, b_grads, name="d_z")
    gs["b_in"] = jnp.concatenate([dsum_a, dsum_b], axis=1)
    sched.grad("w_in", (_mm_tn(n2, dza, bm=512, bn=ZA_W, out_dtype=BF16, name="d_w_in_a"),
                        _mm_tn(n2, dzb, bm=512, bn=ZB_W // 2, out_dtype=BF16, name="d_w_in_b")))
    dn2 = _mm_nt(dza, win_a, bm=1024, bn=1024, out_dtype=F32, name="d_n2_a")
    dn2 = sched.run(_mm_nt, dzb, win_b, bm=1024, bn=1024, out_dtype=F32, name="d_n2_b", add=dn2)
    dh1, df1, gs["attn_pre_g"], gs["ffn1_post_g"], _ = _norms_bwd(
        dh2, h1, small["attn_pre_g"], dn2, f1, small["ffn1_post_g"], 0.5, name="bwd_h1")
    dn1 = ffn_bwd(df1, a1, gu1, n1, w_d1, w_gu1, "1")
    grad_x, gs["ffn1_pre_g"] = _norm_bwd_last(dh1, x, small["ffn1_pre_g"], dn1, name="bwd_x")
    return loss, grad_x, gs


def _to_compute(name, full):
    st = full.reshape(N_CHIPS, full.shape[0] // N_CHIPS, full.shape[1])
    if name in ("ffn1_w_gu", "ffn2_w_gu", "w_ple_proj"):
        return st
    if name == "w_in":
        whole = jnp.transpose(st, (1, 0, 2)).reshape(st.shape[1], -1)
        return whole[:, 0:ZA_W], whole[:, ZA_W:]
    return full


def _to_comm(name, g):
    if name == "w_in":
        whole = jnp.concatenate(g, axis=1)
        g = jnp.transpose(whole.reshape(whole.shape[0], N_CHIPS, -1), (1, 0, 2))
    rs = g.shape[1] if g.ndim == 3 else g.shape[0] // N_CHIPS
    return g.reshape(N_CHIPS, 2, rs // 2, g.shape[-1])


class _Schedule:
    GATHER = {
        "ffn1_gu": [("ffn1_w_down", (0, 1)), ("w_in", (0, 1))],
        "ffn1_down": [("w_out", (0, 1))],
        "band_a_fwd": [("ffn2_w_gu", (0, 2))],
        "band_b0_fwd": [("ffn2_w_gu", (1, 2))],
        "band_b1_fwd": [("ffn2_w_down", (0, 1))],
        "band_b2_fwd": [("w_ple_gate", (0, 1)), ("w_ple_proj", (0, 1))],
    }
    SWAP = {"d_n4": ["w_ple_proj", "w_ple_gate"], "ffn2_d_w_gu": ["ffn2_w_down"], "ffn2_d_n": ["ffn2_w_gu"],
            "d_mix": ["w_out"], "d_n2_b": ["w_in"], "ffn1_d_w_gu": ["ffn1_w_down"], "ffn1_d_n": ["ffn1_w_gu"]}
    SCATTER = {"ffn2_d_a": ["w_ple_proj", "w_ple_gate"], "ffn2_d_n": ["ffn2_w_down"], "band_a_bwd": ["ffn2_w_gu"],
               "band_b0_bwd": ["w_out"], "ffn1_d_a": ["w_in"], "ffn1_d_n": ["ffn1_w_down"]}

    def __init__(self, placed, c_idx, kc_idx):
        self.full = dict(placed)
        self.missing = {n: 1.0 for n in placed}
        self.c_idx, self.kc_idx = c_idx, kc_idx
        self.grads, self.swapped, self.pair_sums, self.scattered = {}, {}, {}, {}

    def _gather(self, entries):
        carries, after = [], []
        for part in sorted({pt for _, pt in entries}):
            names = [n for n, pt in entries if pt == part]
            carry = _gather_carry([self.full[n] for n in names], part)
            carries.append(carry)

            def done(carry=carry, names=names, part=part):
                for n, a in zip(names, carry.results):
                    self.full[n] = a
                    self.missing[n] -= 1.0 / part[1]
            after.append(done)
        return carries, after

    def gather_now(self, names, *, name):
        carries, after = self._gather([(n, (0, 1)) for n in names])
        _comm_call(_join(carries), name=name)
        for fn in after:
            fn()

    def weight(self, name):
        assert abs(self.missing[name]) < 1e-9, (name, self.missing[name])
        return _to_compute(name, self.full[name])

    def grad(self, name, g):
        self.grads[name] = _to_comm(name, g)

    def _swap(self, names):
        carry = _pair_swap_carry([self.grads[n] for n in names])

        def done():
            self.swapped.update(zip(names, carry.results))
        return carry, done

    def _scatter(self, names):
        for n in names:
            self.pair_sums[n] = _pair_sum(self.grads[n], self.swapped[n], self.c_idx, name=f"grad_pair_sum_{n}")
        carry = _chip_scatter_carry([self.pair_sums[n] for n in names])

        def done():
            self.scattered.update(zip(names, carry.results))
        return carry, done

    def run(self, fn, *args, name, **kw):
        carries, after = self._gather(self.GATHER.get(name, []))
        for names, make in ((self.SWAP.get(name), self._swap), (self.SCATTER.get(name), self._scatter)):
            if names:
                carry, done = make(names)
                carries.append(carry)
                after.append(done)
        out = fn(*args, name=name, carry=_join(carries), **kw)
        for done in after:
            done()
        return out

    def finish(self):
        left = [n for n in BIG if n not in self.scattered]
        to_swap = [n for n in left if n not in self.swapped]
        if to_swap:
            carry, done = self._swap(to_swap)
            _comm_call(carry, name="grad_pair_swap")
            done()
        if left:
            carry, done = self._scatter(left)
            _comm_call(carry, name="grad_chip_scatter")
            done()
        halves = [_chip_sum(self.pair_sums[n], self.scattered[n], self.kc_idx, name=f"grad_chip_sum_{n}") for n in BIG]
        carry = _half_swap_carry(halves)
        _comm_call(carry, name="grad_half_swap")
        return dict(zip(BIG, carry.results))


def kernel(x, p, rel_bias, ffn1_pre_g, ffn1_w_gu, ffn1_w_down, ffn1_post_g, attn_pre_g, w_in, b_in, sinks, w_out, b_out, attn_post_g, ffn2_pre_g, ffn2_w_gu, ffn2_w_down, ffn2_post_g, ple_pre_g, w_ple_gate, w_ple_proj, ple_post_g, loss_target, m_rel_bias, m_ffn1_pre_g, m_ffn1_w_gu, m_ffn1_w_down, m_ffn1_post_g, m_attn_pre_g, m_w_in, m_b_in, m_sinks, m_w_out, m_b_out, m_attn_post_g, m_ffn2_pre_g, m_ffn2_w_gu, m_ffn2_w_down, m_ffn2_post_g, m_ple_pre_g, m_w_ple_gate, m_w_ple_proj, m_ple_post_g, v_rel_bias, v_ffn1_pre_g, v_ffn1_w_gu, v_ffn1_w_down, v_ffn1_post_g, v_attn_pre_g, v_w_in, v_b_in, v_sinks, v_w_out, v_b_out, v_attn_post_g, v_ffn2_pre_g, v_ffn2_w_gu, v_ffn2_w_down, v_ffn2_post_g, v_ple_pre_g, v_w_ple_gate, v_w_ple_proj, v_ple_post_g):
    given = dict(locals())
    w = {n: given[n] for n in WEIGHTS}
    m = {n: given["m_" + n] for n in WEIGHTS}
    v = {n: given["v_" + n] for n in WEIGHTS}
    c_pos = lax.axis_index("c").astype(jnp.int32)
    k_pos = (2 * lax.axis_index("x") + lax.axis_index("y")).astype(jnp.int32)
    c_idx, k_idx, kc_idx = c_pos.reshape(1), k_pos.reshape(1), jnp.stack([k_pos, c_pos])

    sched = _Schedule({n: _cast_place(w[n][0], k_idx, name=f"place_{n}") for n in BIG}, c_idx, kc_idx)
    sched.gather_now(["ffn1_w_gu"], name="gather_ffn1_w_gu")
    small = {n: (w[n] if n == "rel_bias" else w[n].reshape(1, -1)) for n in SMALL}

    loss_part, grad_x, gs = _local_step(x[0], p[0, 0], loss_target[0], small, sched)

    grads_big = sched.finish()

    small_shapes = [w[n].shape for n in SMALL]
    packed = _pack([gs[n] for n in SMALL] + [loss_part[:, 0:1]])
    summed = _small_all_reduce(packed, name="small_all_reduce")
    unpacked = _unpack(summed, small_shapes + [(1,)])
    loss = unpacked[-1].reshape(())

    grads, delta, new_m, new_v = {}, {}, {}, {}
    for n in BIG:
        grads[n] = grads_big[n][None]
        d, mn, vn = _adamw(w[n][0], grads_big[n], m[n][0], v[n][0], name=f"adamw_{n}")
        delta[n], new_m[n], new_v[n] = d[None], mn[None], vn[None]
    n_rows = summed.shape[0] - SUBLANES
    d_s, m_s, v_s = _adamw(_pack([w[n] for n in SMALL]), summed[0:n_rows], _pack([m[n] for n in SMALL]),
                           _pack([v[n] for n in SMALL]), name="adamw_small")
    for name, dd, mm, vv, gg in zip(SMALL, _unpack(d_s, small_shapes), _unpack(m_s, small_shapes),
                                    _unpack(v_s, small_shapes), unpacked[:-1]):
        grads[name], delta[name], new_m[name], new_v[name] = gg, dd, mm, vv

    return (loss, grad_x[None], *[grads[n] for n in WEIGHTS], *[delta[n] for n in WEIGHTS],
            *[new_m[n] for n in WEIGHTS], *[new_v[n] for n in WEIGHTS])
```

```python
import math

import jax
import jax.numpy as jnp
import numpy as np
from jax import lax
from jax.experimental import pallas as pl
from jax.experimental.pallas import tpu as pltpu

F32 = jnp.float32
BF16 = jnp.bfloat16
MESH = pl.DeviceIdType.MESH

EPS = 1e-6
NEG = -1e30
LANES = 128
SUBLANES = 8
HEAD_DIM = 64
QBLK = 128
N_PAIRS = 4
MIX_W = N_PAIRS * LANES
A_WINDOW = 128
B_PATTERNS = ((128, 1), (512, 4), (2048, 16))
NUM_BUCKETS = 32
MAX_DISTANCE = 2048
ZA_W = 768
ZB_W = 1536
N_CHIPS = 4
VMEM_BYTES_V7X = 64 * 2**20

ADAM_LR, ADAM_B1, ADAM_B2, ADAM_EPS, ADAM_WD, ADAM_STEP = 0.001, 0.9, 0.999, 1e-08, 0.01, 10

BIG = ("ffn1_w_gu", "ffn1_w_down", "w_in", "w_out", "ffn2_w_gu", "ffn2_w_down", "w_ple_gate", "w_ple_proj")
SMALL = ("rel_bias", "ffn1_pre_g", "ffn1_post_g", "attn_pre_g", "b_in", "sinks", "b_out", "attn_post_g",
         "ffn2_pre_g", "ffn2_post_g", "ple_pre_g", "ple_post_g")
WEIGHTS = ("rel_bias", "ffn1_pre_g", "ffn1_w_gu", "ffn1_w_down", "ffn1_post_g", "attn_pre_g", "w_in", "b_in",
           "sinks", "w_out", "b_out", "attn_post_g", "ffn2_pre_g", "ffn2_w_gu", "ffn2_w_down", "ffn2_post_g",
           "ple_pre_g", "w_ple_gate", "w_ple_proj", "ple_post_g")
PACK_UNIT = SUBLANES * LANES


def _vmem_limit(*block_bytes):
    need = 2 * sum(block_bytes) + max(block_bytes) * 2 + (4 << 20)
    return int(min(max(need, 32 << 20), VMEM_BYTES_V7X - (6 << 20)))


def _nbytes(shape, dtype):
    return int(np.prod(shape)) * jnp.dtype(dtype).itemsize


def _row_block(rows, cap, mult):
    for cand in range(min(rows, cap), 0, -1):
        if rows % cand == 0 and cand % mult == 0:
            return cand
    raise ValueError((rows, cap, mult))


class _Carry:
    def __init__(self, operands, out_shapes, aliases, sems, start, finish):
        self.operands, self.out_shapes, self.aliases, self.sems = list(operands), list(out_shapes), dict(aliases), list(sems)
        self.start, self.finish = start, finish
        self.results = None


def _join(carries):
    carries = [c for c in carries if c is not None]
    if not carries:
        return None
    if len(carries) == 1:
        return carries[0]
    offs, pos = [], [0, 0, 0]
    for c in carries:
        offs.append(tuple(pos))
        pos = [pos[0] + len(c.operands), pos[1] + len(c.out_shapes), pos[2] + len(c.sems)]
    aliases = {}
    for c, (oi, oo, _) in zip(carries, offs):
        aliases.update({oi + i: oo + o for i, o in c.aliases.items()})

    def run(which):
        def fn(ins, outs, sems):
            for c, (oi, oo, os_) in zip(carries, offs):
                getattr(c, which)(ins[oi:oi + len(c.operands)], outs[oo:oo + len(c.out_shapes)],
                                  sems[os_:os_ + len(c.sems)])
        return fn

    joined = _Carry([a for c in carries for a in c.operands], [s for c in carries for s in c.out_shapes], aliases,
                    [s for c in carries for s in c.sems], run("start"), run("finish"))
    joined.parts = (carries, offs)
    return joined


def _set_results(carry, outs):
    carry.results = list(outs)
    if hasattr(carry, "parts"):
        for c, (_, oo, _) in zip(*carry.parts):
            _set_results(c, outs[oo:oo + len(c.out_shapes)])


ANY = pl.BlockSpec(memory_space=pl.ANY)


def _call(body, *, name, grid, in_specs, out_specs, out_shape, args, scratch=(), vmem_limit=None, carry=None,
          semantics=None):
    n_ci, n_co, n_cs = len(args), len(out_shape), len(scratch)
    semantics = semantics or ("parallel",) * len(grid)
    if carry is None:
        res = pl.pallas_call(
            body, name=name, grid=grid, in_specs=list(in_specs), out_specs=tuple(out_specs), out_shape=tuple(out_shape),
            scratch_shapes=list(scratch),
            compiler_params=pltpu.CompilerParams(dimension_semantics=semantics, vmem_limit_bytes=vmem_limit),
        )(*args)
        return list(res)
    n_pi, n_po = len(carry.operands), len(carry.out_shapes)

    def full_body(*refs):
        ci, pi = refs[:n_ci], refs[n_ci:n_ci + n_pi]
        o0 = n_ci + n_pi
        co, po = refs[o0:o0 + n_co], refs[o0 + n_co:o0 + n_co + n_po]
        s0 = o0 + n_co + n_po
        cs, ps = refs[s0:s0 + n_cs], refs[s0 + n_cs:]
        ids = [pl.program_id(ax) for ax in range(len(grid))]
        first, last = ids[0] == 0, ids[0] == grid[0] - 1
        for ax in range(1, len(grid)):
            first, last = first & (ids[ax] == 0), last & (ids[ax] == grid[ax] - 1)

        @pl.when(first)
        def _():
            carry.start(pi, po, ps)

        body(*ci, *co, *cs)

        @pl.when(last)
        def _():
            carry.finish(pi, po, ps)

    res = pl.pallas_call(
        full_body, name=name, grid=grid, in_specs=list(in_specs) + [ANY] * n_pi,
        out_specs=tuple(out_specs) + tuple([ANY] * n_po), out_shape=tuple(out_shape) + tuple(carry.out_shapes),
        scratch_shapes=list(scratch) + carry.sems,
        input_output_aliases={n_ci + i: n_co + o for i, o in carry.aliases.items()},
        compiler_params=pltpu.CompilerParams(dimension_semantics=("arbitrary",) * len(grid),
                                             vmem_limit_bytes=vmem_limit),
    )(*args, *carry.operands)
    _set_results(carry, res[n_co:])
    return list(res[:n_co])


def _comm_call(carry, *, name):
    n_pi, n_po = len(carry.operands), len(carry.out_shapes)

    def body(*refs):
        pi, po, ps = refs[:n_pi], refs[n_pi:n_pi + n_po], refs[n_pi + n_po:]
        carry.start(pi, po, ps)
        carry.finish(pi, po, ps)

    res = pl.pallas_call(
        body, name=name, in_specs=[ANY] * n_pi, out_specs=tuple([ANY] * n_po), out_shape=tuple(carry.out_shapes),
        scratch_shapes=carry.sems, input_output_aliases=dict(carry.aliases),
    )(*carry.operands)
    _set_results(carry, res)


def _mm_nn(a, b, *, bm, bn, out_dtype, name, bias=None, carry=None):
    M, K = a.shape
    stacked = b.ndim == 3
    N = b.shape[0] * b.shape[2] if stacked else b.shape[1]
    assert M % bm == 0 and N % bn == 0 and (not stacked or bn == b.shape[2])

    def body(*refs):
        a_ref, b_ref = refs[0], refs[1]
        o_ref = refs[-1]
        acc = jnp.dot(a_ref[...], b_ref[...], preferred_element_type=F32)
        if bias is not None:
            acc = acc + refs[2][...]
        o_ref[...] = acc.astype(o_ref.dtype)

    if stacked:
        b_spec = pl.BlockSpec((None, K, bn), lambda i, j: (j, 0, 0))
    else:
        b_spec = pl.BlockSpec((K, bn), lambda i, j: (0, j))
    in_specs = [pl.BlockSpec((bm, K), lambda i, j: (i, 0)), b_spec]
    args = [a, b]
    if bias is not None:
        in_specs.append(pl.BlockSpec((1, bn), lambda i, j: (0, j)))
        args.append(bias)
    limit = _vmem_limit(_nbytes((bm, K), a.dtype), _nbytes((K, bn), b.dtype), _nbytes((bm, bn), F32))
    return _call(body, name=name, grid=(M // bm, N // bn), in_specs=in_specs,
                 out_specs=[pl.BlockSpec((bm, bn), lambda i, j: (i, j))],
                 out_shape=[jax.ShapeDtypeStruct((M, N), out_dtype)], args=args, vmem_limit=limit, carry=carry)[0]


def _mm_nt(a, b, *, bm, bn, out_dtype, name, add=None, carry=None):
    M, K = a.shape
    stacked = b.ndim == 3
    N = b.shape[1] if stacked else b.shape[0]
    n_sh = b.shape[0] if stacked else 1
    ks = b.shape[2] if stacked else K
    assert M % bm == 0 and N % bn == 0 and n_sh * ks == K
    dn = (((1,), (1,)), ((), ()))

    def body(*refs):
        a_ref, b_ref = refs[0], refs[1]
        o_ref = refs[-1]
        if stacked:
            acc = lax.dot_general(a_ref[:, 0:ks], b_ref[0], dn, preferred_element_type=F32)
            for s in range(1, n_sh):
                acc = acc + lax.dot_general(a_ref[:, s * ks:(s + 1) * ks], b_ref[s], dn, preferred_element_type=F32)
        else:
            acc = lax.dot_general(a_ref[...], b_ref[...], dn, preferred_element_type=F32)
        if add is not None:
            acc = acc + refs[2][...]
        o_ref[...] = acc.astype(o_ref.dtype)

    if stacked:
        b_spec = pl.BlockSpec((n_sh, bn, ks), lambda i, j: (0, j, 0))
    else:
        b_spec = pl.BlockSpec((bn, K), lambda i, j: (j, 0))
    in_specs = [pl.BlockSpec((bm, K), lambda i, j: (i, 0)), b_spec]
    args = [a, b]
    if add is not None:
        in_specs.append(pl.BlockSpec((bm, bn), lambda i, j: (i, j)))
        args.append(add)
    limit = _vmem_limit(_nbytes((bm, K), a.dtype), _nbytes((bn, K), b.dtype), _nbytes((bm, bn), F32))
    return _call(body, name=name, grid=(M // bm, N // bn), in_specs=in_specs,
                 out_specs=[pl.BlockSpec((bm, bn), lambda i, j: (i, j))],
                 out_shape=[jax.ShapeDtypeStruct((M, N), out_dtype)], args=args, vmem_limit=limit, carry=carry)[0]


def _mm_tn(a, b, *, bm, bn, out_dtype, name, n_stack=None, carry=None):
    T, M = a.shape
    N = b.shape[1]
    assert M % bm == 0 and N % bn == 0 and (n_stack is None or bn * n_stack == N)
    dn = (((0,), (0,)), ((), ()))

    def body(a_ref, b_ref, o_ref):
        acc = lax.dot_general(a_ref[...], b_ref[...], dn, preferred_element_type=F32)
        o_ref[...] = acc.astype(o_ref.dtype)

    if n_stack is None:
        out_spec = pl.BlockSpec((bm, bn), lambda i, j: (i, j))
        out_shape = jax.ShapeDtypeStruct((M, N), out_dtype)
    else:
        out_spec = pl.BlockSpec((None, bm, bn), lambda i, j: (j, i, 0))
        out_shape = jax.ShapeDtypeStruct((n_stack, M, bn), out_dtype)
    limit = _vmem_limit(_nbytes((T, bm), a.dtype), _nbytes((T, bn), b.dtype), _nbytes((bm, bn), F32))
    return _call(body, name=name, grid=(M // bm, N // bn),
                 in_specs=[pl.BlockSpec((T, bm), lambda i, j: (0, i)), pl.BlockSpec((T, bn), lambda i, j: (0, j))],
                 out_specs=[out_spec], out_shape=[out_shape], args=[a, b], vmem_limit=limit, carry=carry)[0]


ROW_BLK = 256


def _rstd(x):
    return lax.rsqrt(jnp.mean(x * x, axis=-1, keepdims=True) + EPS)


def _norm_bwd(xhat, rstd, dxhat):
    return rstd * (dxhat - xhat * jnp.mean(dxhat * xhat, axis=-1, keepdims=True))


def _row_spec(cols):
    return pl.BlockSpec((ROW_BLK, cols), lambda i: (i, 0))


def _vec_spec(cols):
    return pl.BlockSpec((1, cols), lambda i: (0, 0))


def _rms_fwd(x, g, *, name):
    T, D = x.shape

    def body(x_ref, g_ref, o_ref):
        xv = x_ref[...]
        o_ref[...] = (xv * _rstd(xv) * g_ref[...]).astype(o_ref.dtype)

    return pl.pallas_call(
        body, name=name, grid=(T // ROW_BLK,), in_specs=[_row_spec(D), _vec_spec(D)], out_specs=_row_spec(D),
        out_shape=jax.ShapeDtypeStruct((T, D), BF16),
        compiler_params=pltpu.CompilerParams(dimension_semantics=("parallel",)),
    )(x, g)


def _resid_norm(h, f, g_post, coef, g_next, *, name):
    T, D = h.shape

    def body(h_ref, f_ref, gp_ref, gn_ref, hn_ref, n_ref):
        fv = f_ref[...]
        hn = h_ref[...] + coef * (fv * _rstd(fv) * gp_ref[...])
        hn_ref[...] = hn
        n_ref[...] = (hn * _rstd(hn) * gn_ref[...]).astype(n_ref.dtype)

    return pl.pallas_call(
        body, name=name, grid=(T // ROW_BLK,),
        in_specs=[_row_spec(D), _row_spec(D), _vec_spec(D), _vec_spec(D)],
        out_specs=(_row_spec(D), _row_spec(D)),
        out_shape=(jax.ShapeDtypeStruct((T, D), F32), jax.ShapeDtypeStruct((T, D), BF16)),
        compiler_params=pltpu.CompilerParams(dimension_semantics=("parallel",)),
    )(h, f, g_post, g_next)


def _swiglu_fwd(gu, *, name):
    T, F2 = gu.shape
    F = F2 // 2

    def body(gu_ref, a_ref):
        g = gu_ref[:, 0:F].astype(F32)
        u = gu_ref[:, F:F2].astype(F32)
        a_ref[...] = (g * jax.nn.sigmoid(g) * u).astype(a_ref.dtype)

    return pl.pallas_call(
        body, name=name, grid=(T // ROW_BLK,), in_specs=[_row_spec(F2)], out_specs=_row_spec(F),
        out_shape=jax.ShapeDtypeStruct((T, F), BF16),
        compiler_params=pltpu.CompilerParams(dimension_semantics=("parallel",)),
    )(gu)


def _swiglu_bwd(gu, da, *, name):
    T, F2 = gu.shape
    F = F2 // 2

    def body(gu_ref, da_ref, o_ref):
        g = gu_ref[:, 0:F].astype(F32)
        u = gu_ref[:, F:F2].astype(F32)
        dav = da_ref[...].astype(F32)
        s = jax.nn.sigmoid(g)
        o_ref[:, 0:F] = (dav * u * (s * (1.0 + g * (1.0 - s)))).astype(o_ref.dtype)
        o_ref[:, F:F2] = (dav * (g * s)).astype(o_ref.dtype)

    return pl.pallas_call(
        body, name=name, grid=(T // ROW_BLK,), in_specs=[_row_spec(F2), _row_spec(F)], out_specs=_row_spec(F2),
        out_shape=jax.ShapeDtypeStruct((T, F2), BF16),
        compiler_params=pltpu.CompilerParams(dimension_semantics=("parallel",)),
    )(gu, da)


def _ple_loss(h3, gp, e, g_post, target, *, name):
    T, D = h3.shape

    def body(h_ref, gp_ref, e_ref, g_ref, t_ref, dh_ref, de_ref, dgp_ref, loss_ref, dg_ref):
        @pl.when(pl.program_id(0) == 0)
        def _():
            loss_ref[...] = jnp.zeros_like(loss_ref)
            dg_ref[...] = jnp.zeros_like(dg_ref)

        gate = jax.nn.sigmoid(gp_ref[...])
        ev = e_ref[...]
        ge = gate * ev
        r = _rstd(ge)
        gh = ge * r
        gpost = g_ref[...]
        diff = h_ref[...] + gh * gpost - t_ref[...]
        loss_ref[...] += jnp.sum(diff * diff) * (0.5 / D)
        dh = diff * (1.0 / D)
        dh_ref[...] = dh
        dg_ref[...] += jnp.sum(dh * gh, axis=0, keepdims=True)
        dge = _norm_bwd(gh, r, dh * gpost)
        de_ref[...] = (dge * gate).astype(de_ref.dtype)
        dgp_ref[...] = (dge * ev * gate * (1.0 - gate)).astype(dgp_ref.dtype)

    return pl.pallas_call(
        body, name=name, grid=(T // ROW_BLK,),
        in_specs=[_row_spec(D), _row_spec(D), _row_spec(D), _vec_spec(D), _row_spec(D)],
        out_specs=(_row_spec(D), _row_spec(D), _row_spec(D), _vec_spec(LANES), _vec_spec(D)),
        out_shape=(jax.ShapeDtypeStruct((T, D), F32), jax.ShapeDtypeStruct((T, D), BF16),
                   jax.ShapeDtypeStruct((T, D), BF16), jax.ShapeDtypeStruct((1, LANES), F32),
                   jax.ShapeDtypeStruct((1, D), F32)),
        compiler_params=pltpu.CompilerParams(dimension_semantics=("arbitrary",)),
    )(h3, gp, e, g_post, target)


def _norms_bwd(dh_in, h, g_pre, dn, f, g_post, coef, *, name):
    T, D = h.shape

    def body(dhi_ref, h_ref, gpre_ref, dn_ref, f_ref, gpost_ref, dh_ref, df_ref, dgpre_ref, dgpost_ref, dsum_ref):
        @pl.when(pl.program_id(0) == 0)
        def _():
            dgpre_ref[...] = jnp.zeros_like(dgpre_ref)
            dgpost_ref[...] = jnp.zeros_like(dgpost_ref)
            dsum_ref[...] = jnp.zeros_like(dsum_ref)

        hv = h_ref[...]
        r = _rstd(hv)
        hh = hv * r
        dnv = dn_ref[...]
        dgpre_ref[...] += jnp.sum(dnv * hh, axis=0, keepdims=True)
        dh = dhi_ref[...] + _norm_bwd(hh, r, dnv * gpre_ref[...])
        dh_ref[...] = dh
        fv = f_ref[...]
        rf = _rstd(fv)
        fh = fv * rf
        dy = coef * dh
        dgpost_ref[...] += jnp.sum(dy * fh, axis=0, keepdims=True)
        df = _norm_bwd(fh, rf, dy * gpost_ref[...])
        dsum_ref[...] += jnp.sum(df, axis=0, keepdims=True)
        df_ref[...] = df.astype(df_ref.dtype)

    return pl.pallas_call(
        body, name=name, grid=(T // ROW_BLK,),
        in_specs=[_row_spec(D), _row_spec(D), _vec_spec(D), _row_spec(D), _row_spec(D), _vec_spec(D)],
        out_specs=(_row_spec(D), _row_spec(D), _vec_spec(D), _vec_spec(D), _vec_spec(D)),
        out_shape=(jax.ShapeDtypeStruct((T, D), F32), jax.ShapeDtypeStruct((T, D), BF16),
                   jax.ShapeDtypeStruct((1, D), F32), jax.ShapeDtypeStruct((1, D), F32),
                   jax.ShapeDtypeStruct((1, D), F32)),
        compiler_params=pltpu.CompilerParams(dimension_semantics=("arbitrary",)),
    )(dh_in, h, g_pre, dn, f, g_post)


def _norm_bwd_last(dh_in, h, g_pre, dn, *, name):
    T, D = h.shape

    def body(dhi_ref, h_ref, gpre_ref, dn_ref, dh_ref, dgpre_ref):
        @pl.when(pl.program_id(0) == 0)
        def _():
            dgpre_ref[...] = jnp.zeros_like(dgpre_ref)

        hv = h_ref[...]
        r = _rstd(hv)
        hh = hv * r
        dnv = dn_ref[...]
        dgpre_ref[...] += jnp.sum(dnv * hh, axis=0, keepdims=True)
        dh_ref[...] = dhi_ref[...] + _norm_bwd(hh, r, dnv * gpre_ref[...])

    return pl.pallas_call(
        body, name=name, grid=(T // ROW_BLK,),
        in_specs=[_row_spec(D), _row_spec(D), _vec_spec(D), _row_spec(D)],
        out_specs=(_row_spec(D), _vec_spec(D)),
        out_shape=(jax.ShapeDtypeStruct((T, D), F32), jax.ShapeDtypeStruct((1, D), F32)),
        compiler_params=pltpu.CompilerParams(dimension_semantics=("arbitrary",)),
    )(dh_in, h, g_pre, dn)


def _t5_index(max_dist, stride):
    rho = np.arange(QBLK)[:, None]
    kap = np.arange(2 * QBLK)[None, :]
    d = rho + QBLK - kap
    valid = (d >= 0) & (d <= max_dist)
    n = np.maximum(d, 0) * stride
    max_exact = NUM_BUCKETS // 2
    nf = np.maximum(n, 1).astype(np.float32)
    large = max_exact + (np.log(nf / np.float32(max_exact)) / np.float32(math.log(MAX_DISTANCE / max_exact))
                         * np.float32(NUM_BUCKETS - max_exact)).astype(np.int32)
    large = np.minimum(large, NUM_BUCKETS - 1)
    bucket = np.where(n < max_exact, n, large)
    return np.where(valid, bucket, -1).astype(np.int32)


def _bias_tiles(rel_bias, idx, head_off, *, name):
    def body(rb_ref, idx_ref, o_ref):
        h = pl.program_id(0)
        idxv = idx_ref[...]
        acc = jnp.where(idxv < 0, NEG, 0.0).astype(F32)
        for b in range(NUM_BUCKETS):
            acc = acc + jnp.where(idxv == b, rb_ref[b, head_off + h], 0.0)
        o_ref[0, 0] = acc
        kap = lax.broadcasted_iota(jnp.int32, (1, 2 * QBLK), 1)
        o_ref[1, 0] = jnp.where(kap < QBLK, NEG, acc)

    return pl.pallas_call(
        body, name=name, grid=(2 * N_PAIRS,),
        in_specs=[pl.BlockSpec(memory_space=pltpu.SMEM), pl.BlockSpec((QBLK, 2 * QBLK), lambda h: (0, 0))],
        out_specs=pl.BlockSpec((2, 1, QBLK, 2 * QBLK), lambda h: (0, h, 0, 0)),
        out_shape=jax.ShapeDtypeStruct((2, 2 * N_PAIRS, QBLK, 2 * QBLK), F32),
        compiler_params=pltpu.CompilerParams(dimension_semantics=("parallel",)),
    )(rel_bias, idx)


def _bias_grad(d_bias, idx, *, name):
    def body(db_ref, idx_ref, o_ref):
        h = pl.program_id(0)

        @pl.when(h == 0)
        def _():
            o_ref[...] = jnp.zeros_like(o_ref)

        idxv = idx_ref[...]
        dv = db_ref[0]
        rows = lax.broadcasted_iota(jnp.int32, (NUM_BUCKETS, LANES), 0)
        lanes = lax.broadcasted_iota(jnp.int32, (NUM_BUCKETS, LANES), 1)
        acc = o_ref[...]
        for b in range(NUM_BUCKETS):
            s = jnp.sum(jnp.where(idxv == b, dv, 0.0))
            acc = acc + jnp.where((rows == b) & (lanes == h), s, 0.0)
        o_ref[...] = acc

    return pl.pallas_call(
        body, name=name, grid=(2 * N_PAIRS,),
        in_specs=[pl.BlockSpec((1, QBLK, 2 * QBLK), lambda h: (h, 0, 0)),
                  pl.BlockSpec((QBLK, 2 * QBLK), lambda h: (0, 0))],
        out_specs=pl.BlockSpec((NUM_BUCKETS, LANES), lambda h: (0, 0)),
        out_shape=jax.ShapeDtypeStruct((NUM_BUCKETS, LANES), F32),
        compiler_params=pltpu.CompilerParams(dimension_semantics=("arbitrary",)),
    )(d_bias, idx)


def _lane():
    return lax.broadcasted_iota(jnp.int32, (1, LANES), 1)


def _head_sel(h):
    return (_lane() < HEAD_DIM) if h == 0 else (_lane() >= HEAD_DIM)


def _stat_lo():
    return (_lane() % HEAD_DIM) < (HEAD_DIM // 2)


def _stack_heads(t, scale=None):
    if scale is not None:
        t = t * scale
    zero = jnp.zeros_like(t)
    return jnp.concatenate([jnp.where(_head_sel(0), t, zero), jnp.where(_head_sel(1), t, zero)], axis=0)


def _class_spec(L, r, cols, stride):
    mode = pl.Buffered(1) if r == 1 else None
    return pl.BlockSpec((L, MIX_W), lambda j: (0, cols + j * stride), pipeline_mode=mode)


def _rows(b, n_blocks=1):
    return pl.ds(pl.multiple_of(b * QBLK, QBLK), n_blocks * QBLK)


def _key_window(ref, b, cols, first):
    if first:
        blk = ref[0:QBLK, cols]
        return jnp.concatenate([blk, blk], axis=0)
    return ref[_rows(b - 1, 2), cols]


def _band_fwd(qa, ka, va, bias, *, r, q_col, k_col, v_col, stride, name, carry=None):
    L = qa.shape[0]
    nb = L // QBLK
    dn = (((1,), (1,)), ((), ()))
    scale = HEAD_DIM ** -0.5

    def body(q_ref, k_ref, v_ref, bias_ref, o_ref, lse_ref):
        sel0 = _head_sel(0)

        def block(b, first):
            rows = _rows(b)
            for i in range(N_PAIRS):
                cols = slice(i * LANES, (i + 1) * LANES)
                q2 = _stack_heads(q_ref[rows, cols], scale)
                k = _key_window(k_ref, b, cols, first)
                v = _key_window(v_ref, b, cols, first)
                bias2 = bias_ref[1 if first else 0, 2 * i:2 * i + 2].reshape(2 * QBLK, 2 * QBLK)
                s = lax.dot_general(q2, k, dn, preferred_element_type=F32) + bias2
                m = jnp.max(s, axis=1, keepdims=True)
                p = jnp.exp(s - m)
                l = jnp.sum(p, axis=1, keepdims=True)
                o = jnp.dot(p.astype(v.dtype), v, preferred_element_type=F32) / l
                lse = m + jnp.log(l)
                o_ref[rows, cols] = jnp.where(sel0, o[0:QBLK], o[QBLK:]).astype(o_ref.dtype)
                lse_ref[rows, cols] = jnp.where(sel0, lse[0:QBLK], lse[QBLK:])

        block(0, True)
        if nb > 1:
            pl.loop(1, nb)(lambda b: block(b, False))

    bias_spec = pl.BlockSpec((2, 2 * N_PAIRS, QBLK, 2 * QBLK), lambda j: (0, 0, 0, 0))
    out_spec = pl.BlockSpec((L, MIX_W), lambda j: (0, j), pipeline_mode=pl.Buffered(1) if r == 1 else None)
    blk_bytes = _nbytes((L, MIX_W), BF16)
    return _call(
        body, name=name, grid=(r,),
        in_specs=[_class_spec(L, r, q_col, stride), _class_spec(L, r, k_col, stride),
                  _class_spec(L, r, v_col, stride), bias_spec],
        out_specs=[out_spec, out_spec],
        out_shape=[jax.ShapeDtypeStruct((L, r * MIX_W), BF16), jax.ShapeDtypeStruct((L, r * MIX_W), F32)],
        args=[qa, ka, va, bias], vmem_limit=_vmem_limit(*([blk_bytes] * 6)), carry=carry)


def _merge(branches, sink, *, name):
    T, W = branches[0][0].shape
    nbr = len(branches)

    def body(*refs):
        sink_ref = refs[2 * nbr]
        out_ref, outb_ref, lse_ref = refs[2 * nbr + 1:]
        sk = sink_ref[...]
        mx = sk
        for t in range(nbr):
            mx = jnp.maximum(mx, refs[2 * t + 1][...])
        den = jnp.exp(sk - mx)
        num = jnp.zeros_like(mx)
        for t in range(nbr):
            w = jnp.exp(refs[2 * t + 1][...] - mx)
            den = den + w
            num = num + w * refs[2 * t][...].astype(F32)
        out = num / den
        out_ref[...] = out
        outb_ref[...] = out.astype(outb_ref.dtype)
        lse_ref[...] = mx + jnp.log(den)

    args = [a for br in branches for a in br] + [sink]
    return pl.pallas_call(
        body, name=name, grid=(T // ROW_BLK,),
        in_specs=[_row_spec(W)] * (2 * nbr) + [_vec_spec(W)],
        out_specs=(_row_spec(W), _row_spec(W), _row_spec(W)),
        out_shape=(jax.ShapeDtypeStruct((T, W), F32), jax.ShapeDtypeStruct((T, W), BF16),
                   jax.ShapeDtypeStruct((T, W), F32)),
        compiler_params=pltpu.CompilerParams(dimension_semantics=("parallel",)),
    )(*args)


def _attn_delta(dmix, outs, lses, sink, *, name):
    T = dmix.shape[0]
    W = MIX_W

    def body(dmix_ref, oa_ref, ob_ref, la_ref, lb_ref, sink_ref, doa_ref, dob_ref, sta_ref, stb_ref, dsink_ref):
        @pl.when(pl.program_id(0) == 0)
        def _():
            dsink_ref[...] = jnp.zeros_like(dsink_ref)

        sel0, lo = _head_sel(0), _stat_lo()
        for mixer, (o_ref, l_ref, do_ref, st_ref) in enumerate(
                ((oa_ref, la_ref, doa_ref, sta_ref), (ob_ref, lb_ref, dob_ref, stb_ref))):
            for i in range(N_PAIRS):
                cols = slice(i * LANES, (i + 1) * LANES)
                do = dmix_ref[:, mixer * W + i * LANES:mixer * W + (i + 1) * LANES]
                prod = do * o_ref[:, cols]
                d0 = jnp.sum(jnp.where(sel0, prod, 0.0), axis=1, keepdims=True)
                d1 = jnp.sum(jnp.where(sel0, 0.0, prod), axis=1, keepdims=True)
                delta = jnp.where(sel0, d0, d1)
                lse = l_ref[:, cols]
                st_ref[:, cols] = jnp.where(lo, lse, delta)
                do_ref[:, cols] = do.astype(do_ref.dtype)
                if mixer == 0:
                    dsink_ref[:, cols] += -jnp.sum(jnp.exp(sink_ref[:, cols] - lse) * delta, axis=0, keepdims=True)

    return pl.pallas_call(
        body, name=name, grid=(T // ROW_BLK,),
        in_specs=[_row_spec(2 * W), _row_spec(W), _row_spec(W), _row_spec(W), _row_spec(W), _vec_spec(W)],
        out_specs=(_row_spec(W), _row_spec(W), _row_spec(W), _row_spec(W), _vec_spec(W)),
        out_shape=(jax.ShapeDtypeStruct((T, W), BF16), jax.ShapeDtypeStruct((T, W), BF16),
                   jax.ShapeDtypeStruct((T, W), F32), jax.ShapeDtypeStruct((T, W), F32),
                   jax.ShapeDtypeStruct((1, W), F32)),
        compiler_params=pltpu.CompilerParams(dimension_semantics=("arbitrary",)),
    )(dmix, outs[0], outs[1], lses[0], lses[1], sink)


def _band_bwd(qa, ka, va, d_out, stat, bias, *, r, q_col, k_col, v_col, stride, name, carry=None):
    L = qa.shape[0]
    nb = L // QBLK
    dn_nt = (((1,), (1,)), ((), ()))
    dn_tn = (((0,), (0,)), ((), ()))
    scale = HEAD_DIM ** -0.5

    def body(q_ref, k_ref, v_ref, do_ref, st_ref, bias_ref, dq_ref, dk_ref, dv_ref, db_ref, dk_carry, dv_carry):
        @pl.when(pl.program_id(0) == 0)
        def _():
            db_ref[...] = jnp.zeros_like(db_ref)

        lo, sel0 = _stat_lo(), _head_sel(0)

        def block(b, first):
            rows = _rows(b)
            for i in range(N_PAIRS):
                cols = slice(i * LANES, (i + 1) * LANES)
                q2 = _stack_heads(q_ref[rows, cols], scale)
                k = _key_window(k_ref, b, cols, first)
                v = _key_window(v_ref, b, cols, first)
                do2 = _stack_heads(do_ref[rows, cols])
                st = st_ref[rows, cols]
                lse2 = jnp.concatenate(
                    [jnp.max(jnp.where(_head_sel(h) & lo, st, -jnp.inf), axis=1, keepdims=True) for h in range(2)], axis=0)
                delta2 = jnp.concatenate(
                    [jnp.sum(jnp.where(_head_sel(h) & ~lo, st, 0.0), axis=1, keepdims=True) for h in range(2)],
                    axis=0) * (2.0 / HEAD_DIM)
                bias2 = bias_ref[1 if first else 0, 2 * i:2 * i + 2].reshape(2 * QBLK, 2 * QBLK)
                s = lax.dot_general(q2, k, dn_nt, preferred_element_type=F32) + bias2
                p = jnp.exp(s - lse2)
                dp = lax.dot_general(do2, v, dn_nt, preferred_element_type=F32)
                ds = p * (dp - delta2)
                db_ref[2 * i:2 * i + 2] += ds.reshape(2, QBLK, 2 * QBLK)
                dsb = ds.astype(k.dtype)
                dq2 = jnp.dot(dsb, k, preferred_element_type=F32)
                dq_ref[rows, cols] = (jnp.where(sel0, dq2[0:QBLK], dq2[QBLK:]) * scale).astype(dq_ref.dtype)
                dk_acc = lax.dot_general(dsb, q2, dn_tn, preferred_element_type=F32)
                dv_acc = lax.dot_general(p.astype(k.dtype), do2, dn_tn, preferred_element_type=F32)
                if not first:
                    prev = _rows(b - 1)
                    dk_ref[prev, cols] = (dk_carry[:, cols] + dk_acc[0:QBLK]).astype(dk_ref.dtype)
                    dv_ref[prev, cols] = (dv_carry[:, cols] + dv_acc[0:QBLK]).astype(dv_ref.dtype)
                dk_carry[:, cols] = dk_acc[QBLK:2 * QBLK]
                dv_carry[:, cols] = dv_acc[QBLK:2 * QBLK]

        block(0, True)
        if nb > 1:
            pl.loop(1, nb)(lambda b: block(b, False))

        last = _rows(nb - 1)
        dk_ref[last, :] = dk_carry[...].astype(dk_ref.dtype)
        dv_ref[last, :] = dv_carry[...].astype(dv_ref.dtype)

    single = pl.Buffered(1) if r == 1 else None
    tok_spec = pl.BlockSpec((L, MIX_W), lambda j: (0, j), pipeline_mode=single)
    bias_spec = pl.BlockSpec((2, 2 * N_PAIRS, QBLK, 2 * QBLK), lambda j: (0, 0, 0, 0))
    dbias_spec = pl.BlockSpec((2 * N_PAIRS, QBLK, 2 * QBLK), lambda j: (0, 0, 0))
    grad = jax.ShapeDtypeStruct((L, r * MIX_W), BF16)
    blk_bytes = _nbytes((L, MIX_W), BF16)
    carry_shape = pltpu.VMEM((QBLK, MIX_W), F32)
    return _call(
        body, name=name, grid=(r,),
        in_specs=[_class_spec(L, r, q_col, stride), _class_spec(L, r, k_col, stride),
                  _class_spec(L, r, v_col, stride), tok_spec, tok_spec, bias_spec],
        out_specs=[tok_spec, tok_spec, tok_spec, dbias_spec],
        out_shape=[grad, grad, grad, jax.ShapeDtypeStruct((2 * N_PAIRS, QBLK, 2 * QBLK), F32)],
        args=[qa, ka, va, d_out, stat, bias], scratch=[carry_shape, carry_shape],
        vmem_limit=_vmem_limit(*([blk_bytes] * 9)), semantics=("arbitrary",), carry=carry)


def _dz_assemble(dq_a, dk_a, dv_a, b_grads, *, name):
    T = dq_a.shape[0]
    W = MIX_W

    def group_sum(x):
        u0 = x[:, 0:LANES] + x[:, LANES:2 * LANES]
        u1 = x[:, 2 * LANES:3 * LANES] + x[:, 3 * LANES:4 * LANES]
        s0 = u0 + pltpu.roll(u0, HEAD_DIM, 1)
        s1 = u1 + pltpu.roll(u1, HEAD_DIM, 1)
        return jnp.where(_head_sel(0), s0, s1)

    def body(*refs):
        dqa_ref, dka_ref, dva_ref = refs[0:3]
        b_refs = refs[3:12]
        dza_ref, dzb_ref, sa_ref, sb_ref = refs[12:]

        @pl.when(pl.program_id(0) == 0)
        def _():
            sa_ref[...] = jnp.zeros_like(sa_ref)
            sb_ref[...] = jnp.zeros_like(sb_ref)

        def put(dst, acc, col, part):
            w = part.shape[1]
            dst[:, col:col + w] = part.astype(dst.dtype)
            acc[:, col:col + w] += jnp.sum(part, axis=0, keepdims=True)

        put(dza_ref, sa_ref, 0, dqa_ref[...].astype(F32))
        put(dza_ref, sa_ref, W, group_sum(dka_ref[...].astype(F32)))
        put(dza_ref, sa_ref, W + LANES, group_sum(dva_ref[...].astype(F32)))
        for t in range(3):
            part = (b_refs[t][...].astype(F32) + b_refs[3 + t][...].astype(F32) + b_refs[6 + t][...].astype(F32))
            put(dzb_ref, sb_ref, t * W, part)

    args = [dq_a, dk_a, dv_a] + [g[t] for g in b_grads for t in range(3)]
    return pl.pallas_call(
        body, name=name, grid=(T // ROW_BLK,),
        in_specs=[_row_spec(W)] * 12,
        out_specs=(_row_spec(ZA_W), _row_spec(ZB_W), _vec_spec(ZA_W), _vec_spec(ZB_W)),
        out_shape=(jax.ShapeDtypeStruct((T, ZA_W), BF16), jax.ShapeDtypeStruct((T, ZB_W), BF16),
                   jax.ShapeDtypeStruct((1, ZA_W), F32), jax.ShapeDtypeStruct((1, ZB_W), F32)),
        compiler_params=pltpu.CompilerParams(dimension_semantics=("arbitrary",)),
    )(*args)


def _place():
    x, y, c = lax.axis_index("x"), lax.axis_index("y"), lax.axis_index("c")
    chips = [(1 - x, y), (x, 1 - y), (1 - x, 1 - y)]
    return x, y, c, chips


def _cast_place(shard, k_idx, *, name):
    rs, cs = shard.shape
    rb = _row_block(rs, 256, 16)
    nblk = rs // rb

    def body(k_ref, s_ref, o_ref):
        o_ref[...] = s_ref[...].astype(o_ref.dtype)

    grid_spec = pltpu.PrefetchScalarGridSpec(
        num_scalar_prefetch=1, grid=(nblk,),
        in_specs=[pl.BlockSpec((rb, cs), lambda t, k_ref: (t, 0))],
        out_specs=pl.BlockSpec((rb, cs), lambda t, k_ref: (k_ref[0] * nblk + t, 0)))
    return pl.pallas_call(
        body, name=name, grid_spec=grid_spec, out_shape=jax.ShapeDtypeStruct((N_CHIPS * rs, cs), BF16),
        compiler_params=pltpu.CompilerParams(dimension_semantics=("parallel",)),
    )(k_idx, shard)


def _gather_carry(fulls, part=(0, 1)):
    n = len(fulls)
    p_idx, n_parts = part

    def piece(ref, chip_idx, half):
        rs = ref.shape[0] // N_CHIPS
        rh = rs // 2
        rows = rh // n_parts
        return ref.at[pl.ds(chip_idx * rs + half * rh + p_idx * rows, rows)]

    def copy(sems, sem, src_ref, dst_ref, to):
        return pltpu.make_async_remote_copy(src_ref=src_ref, dst_ref=dst_ref, send_sem=sems[0].at[sem],
                                            recv_sem=sems[1].at[sem], device_id=to, device_id_type=MESH)

    def ici_copy(ins, outs, sems, i, j, chip, k, c):
        return copy(sems, 3 * i + j, piece(ins[i], k, c), piece(outs[i], k, c), (*chip, c))

    def start(ins, outs, sems):
        x, y, c, chips = _place()
        for i in range(n):
            for j, chip in enumerate(chips):
                ici_copy(ins, outs, sems, i, j, chip, 2 * x + y, c).start()

    def finish(ins, outs, sems):
        x, y, c, chips = _place()
        sibling = (x, y, 1 - c)
        passed = []
        for i in range(n):
            for j, chip in enumerate(chips):
                region = piece(outs[i], 2 * chip[0] + chip[1], c)
                copy(sems, 3 * i + j, region, region, (*chip, c)).wait_recv()
                cp = copy(sems, 3 * n + 3 * i + j, region, region, sibling)
                cp.start()
                passed.append(cp)
        for i in range(n):
            for j, chip in enumerate(chips):
                region = piece(outs[i], 2 * chip[0] + chip[1], 1 - c)
                copy(sems, 3 * n + 3 * i + j, region, region, sibling).wait_recv()
        for i in range(n):
            for j, chip in enumerate(chips):
                ici_copy(ins, outs, sems, i, j, chip, 2 * x + y, c).wait_send()
        for cp in passed:
            cp.wait_send()

    return _Carry(fulls, [jax.ShapeDtypeStruct(a.shape, a.dtype) for a in fulls], {i: i for i in range(n)},
                  [pltpu.SemaphoreType.DMA((6 * n,)), pltpu.SemaphoreType.DMA((6 * n,))], start, finish)


def _pair_swap_carry(grads):
    n = len(grads)

    def copy(ins, outs, sems, i):
        x, y, c, _ = _place()
        return pltpu.make_async_remote_copy(src_ref=ins[i].at[:, 1 - c], dst_ref=outs[i], send_sem=sems[0].at[i],
                                            recv_sem=sems[1].at[i], device_id=(x, y, 1 - c), device_id_type=MESH)

    def start(ins, outs, sems):
        for i in range(n):
            copy(ins, outs, sems, i).start()

    def finish(ins, outs, sems):
        for i in range(n):
            copy(ins, outs, sems, i).wait()

    return _Carry(grads, [jax.ShapeDtypeStruct((a.shape[0], a.shape[2], a.shape[3]), a.dtype) for a in grads], {},
                  [pltpu.SemaphoreType.DMA((n,)), pltpu.SemaphoreType.DMA((n,))], start, finish)


def _pair_sum(g4, got, c_idx, *, name):
    _, _, rh, cs = g4.shape
    rb = _row_block(rh, 256, 16)

    def body(c_ref, own_ref, got_ref, o_ref):
        o_ref[...] = (own_ref[...].astype(F32) + got_ref[...].astype(F32)).astype(o_ref.dtype)

    grid_spec = pltpu.PrefetchScalarGridSpec(
        num_scalar_prefetch=1, grid=(N_CHIPS, rh // rb),
        in_specs=[pl.BlockSpec((None, None, rb, cs), lambda k, t, c_ref: (k, c_ref[0], t, 0)),
                  pl.BlockSpec((None, rb, cs), lambda k, t, c_ref: (k, t, 0))],
        out_specs=pl.BlockSpec((None, rb, cs), lambda k, t, c_ref: (k, t, 0)))
    return pl.pallas_call(
        body, name=name, grid_spec=grid_spec, out_shape=jax.ShapeDtypeStruct((N_CHIPS, rh, cs), BF16),
        compiler_params=pltpu.CompilerParams(dimension_semantics=("parallel", "parallel")),
    )(c_idx, g4, got)


def _chip_scatter_carry(sums):
    n = len(sums)

    def copies(ins, outs, sems):
        x, y, c, chips = _place()
        return [pltpu.make_async_remote_copy(src_ref=ins[i].at[2 * chip[0] + chip[1]], dst_ref=outs[i].at[j],
                                             send_sem=sems[0].at[3 * i + j], recv_sem=sems[1].at[3 * i + j],
                                             device_id=(*chip, c), device_id_type=MESH)
                for i in range(n) for j, chip in enumerate(chips)]

    def start(ins, outs, sems):
        for cp in copies(ins, outs, sems):
            cp.start()

    def finish(ins, outs, sems):
        for cp in copies(ins, outs, sems):
            cp.wait()

    return _Carry(sums, [jax.ShapeDtypeStruct((3,) + a.shape[1:], a.dtype) for a in sums], {},
                  [pltpu.SemaphoreType.DMA((3 * n,)), pltpu.SemaphoreType.DMA((3 * n,))], start, finish)


def _chip_sum(sums, got, kc_idx, *, name):
    _, rh, cs = sums.shape
    rb = _row_block(rh, 256, 16)
    nblk = rh // rb

    def body(kc_ref, own_ref, got_ref, o_ref):
        acc = own_ref[...].astype(F32)
        for j in range(3):
            acc = acc + got_ref[j].astype(F32)
        o_ref[...] = acc

    grid_spec = pltpu.PrefetchScalarGridSpec(
        num_scalar_prefetch=1, grid=(nblk,),
        in_specs=[pl.BlockSpec((None, rb, cs), lambda t, kc: (kc[0], t, 0)),
                  pl.BlockSpec((3, rb, cs), lambda t, kc: (0, t, 0))],
        out_specs=pl.BlockSpec((rb, cs), lambda t, kc: (kc[1] * nblk + t, 0)))
    return pl.pallas_call(
        body, name=name, grid_spec=grid_spec, out_shape=jax.ShapeDtypeStruct((2 * rh, cs), F32),
        compiler_params=pltpu.CompilerParams(dimension_semantics=("parallel",)),
    )(kc_idx, sums, got)


def _half_swap_carry(shards):
    n = len(shards)

    def copy(ins, outs, sems, i, to_my_half):
        x, y, c, _ = _place()
        rh = ins[i].shape[0] // 2
        half = c if to_my_half else 1 - c
        return pltpu.make_async_remote_copy(
            src_ref=ins[i].at[pl.ds(c * rh, rh)], dst_ref=outs[i].at[pl.ds(half * rh, rh)],
            send_sem=sems[0].at[i], recv_sem=sems[1].at[i], device_id=(x, y, 1 - c), device_id_type=MESH)

    def start(ins, outs, sems):
        for i in range(n):
            copy(ins, outs, sems, i, True).start()

    def finish(ins, outs, sems):
        for i in range(n):
            copy(ins, outs, sems, i, False).wait_recv()
        for i in range(n):
            copy(ins, outs, sems, i, True).wait_send()

    return _Carry(shards, [jax.ShapeDtypeStruct(a.shape, a.dtype) for a in shards], {i: i for i in range(n)},
                  [pltpu.SemaphoreType.DMA((n,)), pltpu.SemaphoreType.DMA((n,))], start, finish)


def _small_all_reduce(block, *, name):
    rows = block.shape[0]

    def body(x_ref, o_ref, buf, send_sems, recv_sems):
        x, y, c, _ = _place()
        me = 4 * x + 2 * y + c
        buf[me] = x_ref[...]
        copies = []
        for d in range(1, 8):
            to = (1 - x if d & 4 else x, 1 - y if d & 2 else y, 1 - c if d & 1 else c)
            cp = pltpu.make_async_remote_copy(src_ref=x_ref, dst_ref=buf.at[me], send_sem=send_sems.at[d - 1],
                                              recv_sem=recv_sems.at[d - 1], device_id=to, device_id_type=MESH)
            cp.start()
            copies.append((cp, to))
        for d in range(1, 8):
            to = copies[d - 1][1]
            sender = 4 * to[0] + 2 * to[1] + to[2]
            pltpu.make_async_remote_copy(src_ref=x_ref, dst_ref=buf.at[sender], send_sem=send_sems.at[d - 1],
                                         recv_sem=recv_sems.at[d - 1], device_id=to, device_id_type=MESH).wait_recv()
        for cp, _ in copies:
            cp.wait_send()
        acc = buf[0]
        for s in range(1, 8):
            acc = acc + buf[s]
        o_ref[...] = acc

    vm = pl.BlockSpec(memory_space=pltpu.VMEM)
    return pl.pallas_call(
        body, name=name, in_specs=[vm], out_specs=vm, out_shape=jax.ShapeDtypeStruct(block.shape, F32),
        scratch_shapes=[pltpu.VMEM((8, rows, LANES), F32), pltpu.SemaphoreType.DMA((7,)),
                        pltpu.SemaphoreType.DMA((7,))],
    )(block)


def _adamw(w, g, m, v, *, name):
    R, C = w.shape
    rb = _row_block(R, 256, SUBLANES)
    c1 = 1.0 - ADAM_B1 ** ADAM_STEP
    c2 = 1.0 - ADAM_B2 ** ADAM_STEP

    def body(w_ref, g_ref, m_ref, v_ref, d_ref, mo_ref, vo_ref):
        gv = g_ref[...]
        mn = ADAM_B1 * m_ref[...] + (1.0 - ADAM_B1) * gv
        vn = ADAM_B2 * v_ref[...] + (1.0 - ADAM_B2) * (gv * gv)
        d_ref[...] = -ADAM_LR * ((mn / c1) / (jnp.sqrt(vn / c2) + ADAM_EPS) + ADAM_WD * w_ref[...])
        mo_ref[...] = mn
        vo_ref[...] = vn

    spec = pl.BlockSpec((rb, C), lambda i: (i, 0))
    shp = jax.ShapeDtypeStruct((R, C), F32)
    return pl.pallas_call(
        body, name=name, grid=(R // rb,), in_specs=[spec] * 4, out_specs=(spec, spec, spec),
        out_shape=(shp, shp, shp), compiler_params=pltpu.CompilerParams(dimension_semantics=("parallel",)),
    )(w, g, m, v)


def _pack(parts):
    rows = []
    for a in parts:
        flat = a.reshape(-1).astype(F32)
        pad = (-flat.shape[0]) % PACK_UNIT
        rows.append(jnp.pad(flat, (0, pad)).reshape(-1, LANES))
    return jnp.concatenate(rows, axis=0)


def _unpack(block, shapes):
    out, row = [], 0
    for shp in shapes:
        size = int(np.prod(shp))
        nrows = -(-size // PACK_UNIT) * SUBLANES
        out.append(block[row:row + nrows].reshape(-1)[:size].reshape(shp))
        row += nrows
    return out


def _local_step(x, p, target, small, sched):
    T = x.shape[0]
    W = MIX_W
    rel_bias = small["rel_bias"]
    b_in_a, b_in_b = small["b_in"][:, 0:ZA_W], small["b_in"][:, ZA_W:]

    n1 = _rms_fwd(x, small["ffn1_pre_g"], name="n1")
    w_gu1 = sched.weight("ffn1_w_gu")
    gu1 = sched.run(_mm_nn, n1, w_gu1, bm=1024, bn=1408, out_dtype=BF16, name="ffn1_gu")
    a1 = _swiglu_fwd(gu1, name="ffn1_act")
    w_d1 = sched.weight("ffn1_w_down")
    f1 = sched.run(_mm_nn, a1, w_d1, bm=1024, bn=512, out_dtype=F32, name="ffn1_down")
    h1, n2 = _resid_norm(x, f1, small["ffn1_post_g"], 0.5, small["attn_pre_g"], name="h1")
    win_a, win_b = sched.weight("w_in")
    za = _mm_nn(n2, win_a, bm=1024, bn=ZA_W, out_dtype=BF16, name="in_proj_a", bias=b_in_a)
    zb = _mm_nn(n2, win_b, bm=1024, bn=ZB_W // 2, out_dtype=BF16, name="in_proj_b", bias=b_in_b)

    k_a = za[:, W:W + LANES].reshape(T, 2, 1, HEAD_DIM)
    v_a = za[:, W + LANES:W + 2 * LANES].reshape(T, 2, 1, HEAD_DIM)
    k_a = jnp.broadcast_to(k_a, (T, 2, 4, HEAD_DIM)).reshape(T, W)
    v_a = jnp.broadcast_to(v_a, (T, 2, 4, HEAD_DIM)).reshape(T, W)
    idx_a = jnp.asarray(_t5_index(A_WINDOW - 1, 1))
    bias_a = _bias_tiles(rel_bias, idx_a, 0, name="bias_a")
    a_args = dict(r=1, q_col=0, k_col=0, v_col=0, stride=0)
    o_a, lse_a1 = sched.run(_band_fwd, za, k_a, v_a, bias_a, name="band_a_fwd", **a_args)
    sink_row = jnp.repeat(small["sinks"], HEAD_DIM, axis=1)
    out_a, out_a_bf, lse_a = _merge([(o_a, lse_a1)], sink_row, name="merge_a")

    no_sink = jnp.full((1, W), NEG, F32)
    b_ctx, branches = [], []
    for t, (window, dil) in enumerate(B_PATTERNS):
        idx = jnp.asarray(_t5_index(window // dil, dil))
        bias = _bias_tiles(rel_bias, idx, 2 * N_PAIRS, name=f"bias_b{t}")
        zb_r = zb.reshape(T // dil, dil * ZB_W)
        args = dict(r=dil, q_col=0, k_col=1, v_col=2, stride=ZB_W // W)
        o, l = sched.run(_band_fwd, zb_r, zb_r, zb_r, bias, name=f"band_b{t}_fwd", **args)
        branches.append((o.reshape(T, W), l.reshape(T, W)))
        b_ctx.append((idx, bias, zb_r, args, dil))
    out_b, out_b_bf, lse_b = _merge(branches, no_sink, name="merge_b")

    mix = jnp.concatenate([out_a_bf, out_b_bf], axis=1)
    w_out = sched.weight("w_out")
    att = _mm_nn(mix, w_out, bm=1024, bn=1024, out_dtype=F32, name="out_proj", bias=small["b_out"])
    h2, n3 = _resid_norm(h1, att, small["attn_post_g"], 1.0, small["ffn2_pre_g"], name="h2")
    w_gu2, w_d2 = sched.weight("ffn2_w_gu"), sched.weight("ffn2_w_down")
    gu2 = _mm_nn(n3, w_gu2, bm=1024, bn=1408, out_dtype=BF16, name="ffn2_gu")
    a2 = _swiglu_fwd(gu2, name="ffn2_act")
    f2 = _mm_nn(a2, w_d2, bm=1024, bn=512, out_dtype=F32, name="ffn2_down")
    h3, n4 = _resid_norm(h2, f2, small["ffn2_post_g"], 0.5, small["ple_pre_g"], name="h3")
    w_gate = sched.weight("w_ple_gate")
    gp = _mm_nn(n4, w_gate, bm=1024, bn=1024, out_dtype=F32, name="ple_gate")
    p_bf = p.astype(BF16)
    e = _mm_nn(p_bf, sched.weight("w_ple_proj"), bm=1024, bn=256, out_dtype=F32, name="ple_proj")
    dh4, de, dgp, loss, d_ple_post = _ple_loss(h3, gp, e, small["ple_post_g"], target, name="ple_loss")

    gs = {"ple_post_g": d_ple_post}
    sched.grad("w_ple_proj", _mm_tn(p_bf, de, bm=256, bn=256, out_dtype=BF16, name="d_w_ple", n_stack=N_CHIPS))
    sched.grad("w_ple_gate", _mm_tn(n4, dgp, bm=512, bn=1024, out_dtype=BF16, name="d_w_gate"))
    dn4 = sched.run(_mm_nt, dgp, w_gate, bm=1024, bn=1024, out_dtype=F32, name="d_n4")
    dh3, df2, gs["ple_pre_g"], gs["ffn2_post_g"], _ = _norms_bwd(
        dh4, h3, small["ple_pre_g"], dn4, f2, small["ffn2_post_g"], 0.5, name="bwd_h3")

    def ffn_bwd(df, a, gu, n, w_down, w_gu, tag):
        da = sched.run(_mm_nt, df, w_down, bm=1024, bn=1408, out_dtype=BF16, name=f"ffn{tag}_d_a")
        sched.grad(f"ffn{tag}_w_down", _mm_tn(a, df, bm=256, bn=1024, out_dtype=BF16, name=f"ffn{tag}_d_w_down"))
        dgu = _swiglu_bwd(gu, da, name=f"ffn{tag}_d_gu")
        sched.grad(f"ffn{tag}_w_gu", sched.run(_mm_tn, n, dgu, bm=512, bn=1408, out_dtype=BF16,
                                              name=f"ffn{tag}_d_w_gu", n_stack=N_CHIPS))
        return sched.run(_mm_nt, dgu, w_gu, bm=512, bn=1024, out_dtype=F32, name=f"ffn{tag}_d_n")

    dn3 = ffn_bwd(df2, a2, gu2, n3, w_d2, w_gu2, "2")
    dh2, datt, gs["ffn2_pre_g"], gs["attn_post_g"], gs["b_out"] = _norms_bwd(
        dh3, h2, small["ffn2_pre_g"], dn3, att, small["attn_post_g"], 1.0, name="bwd_h2")
    sched.grad("w_out", _mm_tn(mix, datt, bm=512, bn=1024, out_dtype=BF16, name="d_w_out"))
    dmix = sched.run(_mm_nt, datt, w_out, bm=1024, bn=1024, out_dtype=F32, name="d_mix")

    do_a, do_b, stat_a, stat_b, dsink = _attn_delta(dmix, (out_a, out_b), (lse_a, lse_b), sink_row, name="attn_delta")
    gs["sinks"] = dsink[:, ::HEAD_DIM]
    dq_a, dk_a, dv_a, dbias_a = sched.run(_band_bwd, za, k_a, v_a, do_a, stat_a, bias_a, name="band_a_bwd", **a_args)
    d_rel_a = _bias_grad(dbias_a, idx_a, name="d_rel_a")
    b_grads = []
    d_rel_b = None
    for t, (idx, bias, zb_r, args, dil) in enumerate(b_ctx):
        shp = (T // dil, dil * W)
        dq, dk, dv, dbias = sched.run(_band_bwd, zb_r, zb_r, zb_r, do_b.reshape(shp), stat_b.reshape(shp), bias,
                                      name=f"band_b{t}_bwd", **args)
        b_grads.append((dq.reshape(T, W), dk.reshape(T, W), dv.reshape(T, W)))
        d_rel = _bias_grad(dbias, idx, name=f"d_rel_b{t}")
        d_rel_b = d_rel if d_rel_b is None else d_rel_b + d_rel
    gs["rel_bias"] = jnp.concatenate([d_rel_a[:, 0:2 * N_PAIRS], d_rel_b[:, 0:2 * N_PAIRS]], axis=1)
    dza, dzb, dsum_a, dsum_b = _dz_assemble(dq_a, dk_a, dv_a, b_grads, name="d_z")
    gs["b_in"] = jnp.concatenate([dsum_a, dsum_b], axis=1)
    sched.grad("w_in", (_mm_tn(n2, dza, bm=512, bn=ZA_W, out_dtype=BF16, name="d_w_in_a"),
                        _mm_tn(n2, dzb, bm=512, bn=ZB_W // 2, out_dtype=BF16, name="d_w_in_b")))
    dn2 = _mm_nt(dza, win_a, bm=1024, bn=1024, out_dtype=F32, name="d_n2_a")
    dn2 = sched.run(_mm_nt, dzb, win_b, bm=1024, bn=1024, out_dtype=F32, name="d_n2_b", add=dn2)
    dh1, df1, gs["attn_pre_g"], gs["ffn1_post_g"], _ = _norms_bwd(
        dh2, h1, small["attn_pre_g"], dn2, f1, small["ffn1_post_g"], 0.5, name="bwd_h1")
    dn1 = ffn_bwd(df1, a1, gu1, n1, w_d1, w_gu1, "1")
    grad_x, gs["ffn1_pre_g"] = _norm_bwd_last(dh1, x, small["ffn1_pre_g"], dn1, name="bwd_x")
    return loss, grad_x, gs


def _to_compute(name, full):
    st = full.reshape(N_CHIPS, full.shape[0] // N_CHIPS, full.shape[1])
    if name in ("ffn1_w_gu", "ffn2_w_gu", "w_ple_proj"):
        return st
    if name == "w_in":
        whole = jnp.transpose(st, (1, 0, 2)).reshape(st.shape[1], -1)
        return whole[:, 0:ZA_W], whole[:, ZA_W:]
    return full


def _to_comm(name, g):
    if name == "w_in":
        whole = jnp.concatenate(g, axis=1)
        g = jnp.transpose(whole.reshape(whole.shape[0], N_CHIPS, -1), (1, 0, 2))
    rs = g.shape[1] if g.ndim == 3 else g.shape[0] // N_CHIPS
    return g.reshape(N_CHIPS, 2, rs // 2, g.shape[-1])


class _Schedule:
    GATHER = {
        "ffn1_gu": [("ffn1_w_down", (0, 1)), ("w_in", (0, 1))],
        "ffn1_down": [("w_out", (0, 1))],
        "band_a_fwd": [("ffn2_w_gu", (0, 2))],
        "band_b0_fwd": [("ffn2_w_gu", (1, 2))],
        "band_b1_fwd": [("ffn2_w_down", (0, 1))],
        "band_b2_fwd": [("w_ple_gate", (0, 1)), ("w_ple_proj", (0, 1))],
    }
    SWAP = {"d_n4": ["w_ple_proj", "w_ple_gate"], "ffn2_d_w_gu": ["ffn2_w_down"], "ffn2_d_n": ["ffn2_w_gu"],
            "d_mix": ["w_out"], "d_n2_b": ["w_in"], "ffn1_d_w_gu": ["ffn1_w_down"], "ffn1_d_n": ["ffn1_w_gu"]}
    SCATTER = {"ffn2_d_a": ["w_ple_proj", "w_ple_gate"], "ffn2_d_n": ["ffn2_w_down"], "band_a_bwd": ["ffn2_w_gu"],
               "band_b0_bwd": ["w_out"], "ffn1_d_a": ["w_in"], "ffn1_d_n": ["ffn1_w_down"]}

    def __init__(self, placed, c_idx, kc_idx):
        self.full = dict(placed)
        self.missing = {n: 1.0 for n in placed}
        self.c_idx, self.kc_idx = c_idx, kc_idx
        self.grads, self.swapped, self.pair_sums, self.scattered = {}, {}, {}, {}

    def _gather(self, entries):
        carries, after = [], []
        for part in sorted({pt for _, pt in entries}):
            names = [n for n, pt in entries if pt == part]
            carry = _gather_carry([self.full[n] for n in names], part)
            carries.append(carry)

            def done(carry=carry, names=names, part=part):
                for n, a in zip(names, carry.results):
                    self.full[n] = a
                    self.missing[n] -= 1.0 / part[1]
            after.append(done)
        return carries, after

    def gather_now(self, names, *, name):
        carries, after = self._gather([(n, (0, 1)) for n in names])
        _comm_call(_join(carries), name=name)
        for fn in after:
            fn()

    def weight(self, name):
        assert abs(self.missing[name]) < 1e-9, (name, self.missing[name])
        return _to_compute(name, self.full[name])

    def grad(self, name, g):
        self.grads[name] = _to_comm(name, g)

    def _swap(self, names):
        carry = _pair_swap_carry([self.grads[n] for n in names])

        def done():
            self.swapped.update(zip(names, carry.results))
        return carry, done

    def _scatter(self, names):
        for n in names:
            self.pair_sums[n] = _pair_sum(self.grads[n], self.swapped[n], self.c_idx, name=f"grad_pair_sum_{n}")
        carry = _chip_scatter_carry([self.pair_sums[n] for n in names])

        def done():
            self.scattered.update(zip(names, carry.results))
        return carry, done

    def run(self, fn, *args, name, **kw):
        carries, after = self._gather(self.GATHER.get(name, []))
        for names, make in ((self.SWAP.get(name), self._swap), (self.SCATTER.get(name), self._scatter)):
            if names:
                carry, done = make(names)
                carries.append(carry)
                after.append(done)
        out = fn(*args, name=name, carry=_join(carries), **kw)
        for done in after:
            done()
        return out

    def finish(self):
        left = [n for n in BIG if n not in self.scattered]
        to_swap = [n for n in left if n not in self.swapped]
        if to_swap:
            carry, done = self._swap(to_swap)
            _comm_call(carry, name="grad_pair_swap")
            done()
        if left:
            carry, done = self._scatter(left)
            _comm_call(carry, name="grad_chip_scatter")
            done()
        halves = [_chip_sum(self.pair_sums[n], self.scattered[n], self.kc_idx, name=f"grad_chip_sum_{n}") for n in BIG]
        carry = _half_swap_carry(halves)
        _comm_call(carry, name="grad_half_swap")
        return dict(zip(BIG, carry.results))


def kernel(x, p, rel_bias, ffn1_pre_g, ffn1_w_gu, ffn1_w_down, ffn1_post_g, attn_pre_g, w_in, b_in, sinks, w_out, b_out, attn_post_g, ffn2_pre_g, ffn2_w_gu, ffn2_w_down, ffn2_post_g, ple_pre_g, w_ple_gate, w_ple_proj, ple_post_g, loss_target, m_rel_bias, m_ffn1_pre_g, m_ffn1_w_gu, m_ffn1_w_down, m_ffn1_post_g, m_attn_pre_g, m_w_in, m_b_in, m_sinks, m_w_out, m_b_out, m_attn_post_g, m_ffn2_pre_g, m_ffn2_w_gu, m_ffn2_w_down, m_ffn2_post_g, m_ple_pre_g, m_w_ple_gate, m_w_ple_proj, m_ple_post_g, v_rel_bias, v_ffn1_pre_g, v_ffn1_w_gu, v_ffn1_w_down, v_ffn1_post_g, v_attn_pre_g, v_w_in, v_b_in, v_sinks, v_w_out, v_b_out, v_attn_post_g, v_ffn2_pre_g, v_ffn2_w_gu, v_ffn2_w_down, v_ffn2_post_g, v_ple_pre_g, v_w_ple_gate, v_w_ple_proj, v_ple_post_g):
    given = dict(locals())
    w = {n: given[n] for n in WEIGHTS}
    m = {n: given["m_" + n] for n in WEIGHTS}
    v = {n: given["v_" + n] for n in WEIGHTS}
    c_pos = lax.axis_index("c").astype(jnp.int32)
    k_pos = (2 * lax.axis_index("x") + lax.axis_index("y")).astype(jnp.int32)
    c_idx, k_idx, kc_idx = c_pos.reshape(1), k_pos.reshape(1), jnp.stack([k_pos, c_pos])

    sched = _Schedule({n: _cast_place(w[n][0], k_idx, name=f"place_{n}") for n in BIG}, c_idx, kc_idx)
    sched.gather_now(["ffn1_w_gu"], name="gather_ffn1_w_gu")
    small = {n: (w[n] if n == "rel_bias" else w[n].reshape(1, -1)) for n in SMALL}

    loss_part, grad_x, gs = _local_step(x[0], p[0, 0], loss_target[0], small, sched)

    grads_big = sched.finish()

    small_shapes = [w[n].shape for n in SMALL]
    packed = _pack([gs[n] for n in SMALL] + [loss_part[:, 0:1]])
    summed = _small_all_reduce(packed, name="small_all_reduce")
    unpacked = _unpack(summed, small_shapes + [(1,)])
    loss = unpacked[-1].reshape(())

    grads, delta, new_m, new_v = {}, {}, {}, {}
    for n in BIG:
        grads[n] = grads_big[n][None]
        d, mn, vn = _adamw(w[n][0], grads_big[n], m[n][0], v[n][0], name=f"adamw_{n}")
        delta[n], new_m[n], new_v[n] = d[None], mn[None], vn[None]
    n_rows = summed.shape[0] - SUBLANES
    d_s, m_s, v_s = _adamw(_pack([w[n] for n in SMALL]), summed[0:n_rows], _pack([m[n] for n in SMALL]),
                           _pack([v[n] for n in SMALL]), name="adamw_small")
    for name, dd, mm, vv, gg in zip(SMALL, _unpack(d_s, small_shapes), _unpack(m_s, small_shapes),
                                    _unpack(v_s, small_shapes), unpacked[:-1]):
        grads[name], delta[name], new_m[name], new_v[name] = gg, dd, mm, vv

    return (loss, grad_x[None], *[grads[n] for n in WEIGHTS], *[delta[n] for n in WEIGHTS],
            *[new_m[n] for n in WEIGHTS], *[new_v[n] for n in WEIGHTS])
```

```python
import math

import jax
import jax.numpy as jnp
import numpy as np
from jax import lax
from jax.experimental import pallas as pl
from jax.experimental.pallas import tpu as pltpu

F32 = jnp.float32
BF16 = jnp.bfloat16
MESH = pl.DeviceIdType.MESH

EPS = 1e-6
NEG = -1e30
LANES = 128
SUBLANES = 8
HEAD_DIM = 64
QBLK = 128
N_PAIRS = 4
MIX_W = N_PAIRS * LANES
A_WINDOW = 128
B_PATTERNS = ((128, 1), (512, 4), (2048, 16))
NUM_BUCKETS = 32
MAX_DISTANCE = 2048
ZA_W = 768
ZB_W = 1536
N_CHIPS = 4
VMEM_BYTES_V7X = 64 * 2**20

ADAM_LR, ADAM_B1, ADAM_B2, ADAM_EPS, ADAM_WD, ADAM_STEP = 0.001, 0.9, 0.999, 1e-08, 0.01, 10

BIG = ("ffn1_w_gu", "ffn1_w_down", "w_in", "w_out", "ffn2_w_gu", "ffn2_w_down", "w_ple_gate", "w_ple_proj")
SMALL = ("rel_bias", "ffn1_pre_g", "ffn1_post_g", "attn_pre_g", "b_in", "sinks", "b_out", "attn_post_g",
         "ffn2_pre_g", "ffn2_post_g", "ple_pre_g", "ple_post_g")
WEIGHTS = ("rel_bias", "ffn1_pre_g", "ffn1_w_gu", "ffn1_w_down", "ffn1_post_g", "attn_pre_g", "w_in", "b_in",
           "sinks", "w_out", "b_out", "attn_post_g", "ffn2_pre_g", "ffn2_w_gu", "ffn2_w_down", "ffn2_post_g",
           "ple_pre_g", "w_ple_gate", "w_ple_proj", "ple_post_g")
PACK_UNIT = SUBLANES * LANES


def _vmem_limit(*block_bytes):
    need = 2 * sum(block_bytes) + max(block_bytes) * 2 + (4 << 20)
    return int(min(max(need, 32 << 20), VMEM_BYTES_V7X - (6 << 20)))


def _nbytes(shape, dtype):
    return int(np.prod(shape)) * jnp.dtype(dtype).itemsize


def _row_block(rows, cap, mult):
    for cand in range(min(rows, cap), 0, -1):
        if rows % cand == 0 and cand % mult == 0:
            return cand
    raise ValueError((rows, cap, mult))


class _Carry:
    def __init__(self, operands, out_shapes, aliases, sems, start, finish):
        self.operands, self.out_shapes, self.aliases, self.sems = list(operands), list(out_shapes), dict(aliases), list(sems)
        self.start, self.finish = start, finish
        self.results = None


def _join(carries):
    carries = [c for c in carries if c is not None]
    if not carries:
        return None
    if len(carries) == 1:
        return carries[0]
    offs, pos = [], [0, 0, 0]
    for c in carries:
        offs.append(tuple(pos))
        pos = [pos[0] + len(c.operands), pos[1] + len(c.out_shapes), pos[2] + len(c.sems)]
    aliases = {}
    for c, (oi, oo, _) in zip(carries, offs):
        aliases.update({oi + i: oo + o for i, o in c.aliases.items()})

    def run(which):
        def fn(ins, outs, sems):
            for c, (oi, oo, os_) in zip(carries, offs):
                getattr(c, which)(ins[oi:oi + len(c.operands)], outs[oo:oo + len(c.out_shapes)],
                                  sems[os_:os_ + len(c.sems)])
        return fn

    joined = _Carry([a for c in carries for a in c.operands], [s for c in carries for s in c.out_shapes], aliases,
                    [s for c in carries for s in c.sems], run("start"), run("finish"))
    joined.parts = (carries, offs)
    return joined


def _set_results(carry, outs):
    carry.results = list(outs)
    if hasattr(carry, "parts"):
        for c, (_, oo, _) in zip(*carry.parts):
            _set_results(c, outs[oo:oo + len(c.out_shapes)])


ANY = pl.BlockSpec(memory_space=pl.ANY)


def _call(body, *, name, grid, in_specs, out_specs, out_shape, args, scratch=(), vmem_limit=None, carry=None,
          semantics=None):
    n_ci, n_co, n_cs = len(args), len(out_shape), len(scratch)
    semantics = semantics or ("parallel",) * len(grid)
    if carry is None:
        res = pl.pallas_call(
            body, name=name, grid=grid, in_specs=list(in_specs), out_specs=tuple(out_specs), out_shape=tuple(out_shape),
            scratch_shapes=list(scratch),
            compiler_params=pltpu.CompilerParams(dimension_semantics=semantics, vmem_limit_bytes=vmem_limit),
        )(*args)
        return list(res)
    n_pi, n_po = len(carry.operands), len(carry.out_shapes)

    def full_body(*refs):
        ci, pi = refs[:n_ci], refs[n_ci:n_ci + n_pi]
        o0 = n_ci + n_pi
        co, po = refs[o0:o0 + n_co], refs[o0 + n_co:o0 + n_co + n_po]
        s0 = o0 + n_co + n_po
        cs, ps = refs[s0:s0 + n_cs], refs[s0 + n_cs:]
        ids = [pl.program_id(ax) for ax in range(len(grid))]
        first, last = ids[0] == 0, ids[0] == grid[0] - 1
        for ax in range(1, len(grid)):
            first, last = first & (ids[ax] == 0), last & (ids[ax] == grid[ax] - 1)

        @pl.when(first)
        def _():
            carry.start(pi, po, ps)

        body(*ci, *co, *cs)

        @pl.when(last)
        def _():
            carry.finish(pi, po, ps)

    res = pl.pallas_call(
        full_body, name=name, grid=grid, in_specs=list(in_specs) + [ANY] * n_pi,
        out_specs=tuple(out_specs) + tuple([ANY] * n_po), out_shape=tuple(out_shape) + tuple(carry.out_shapes),
        scratch_shapes=list(scratch) + carry.sems,
        input_output_aliases={n_ci + i: n_co + o for i, o in carry.aliases.items()},
        compiler_params=pltpu.CompilerParams(dimension_semantics=("arbitrary",) * len(grid),
                                             vmem_limit_bytes=vmem_limit),
    )(*args, *carry.operands)
    _set_results(carry, res[n_co:])
    return list(res[:n_co])


def _comm_call(carry, *, name):
    n_pi, n_po = len(carry.operands), len(carry.out_shapes)

    def body(*refs):
        pi, po, ps = refs[:n_pi], refs[n_pi:n_pi + n_po], refs[n_pi + n_po:]
        carry.start(pi, po, ps)
        carry.finish(pi, po, ps)

    res = pl.pallas_call(
        body, name=name, in_specs=[ANY] * n_pi, out_specs=tuple([ANY] * n_po), out_shape=tuple(carry.out_shapes),
        scratch_shapes=carry.sems, input_output_aliases=dict(carry.aliases),
    )(*carry.operands)
    _set_results(carry, res)


def _mm_nn(a, b, *, bm, bn, out_dtype, name, bias=None, carry=None):
    M, K = a.shape
    stacked = b.ndim == 3
    N = b.shape[0] * b.shape[2] if stacked else b.shape[1]
    assert M % bm == 0 and N % bn == 0 and (not stacked or bn == b.shape[2])

    def body(*refs):
        a_ref, b_ref = refs[0], refs[1]
        o_ref = refs[-1]
        acc = jnp.dot(a_ref[...], b_ref[...], preferred_element_type=F32)
        if bias is not None:
            acc = acc + refs[2][...]
        o_ref[...] = acc.astype(o_ref.dtype)

    if stacked:
        b_spec = pl.BlockSpec((None, K, bn), lambda i, j: (j, 0, 0))
    else:
        b_spec = pl.BlockSpec((K, bn), lambda i, j: (0, j))
    in_specs = [pl.BlockSpec((bm, K), lambda i, j: (i, 0)), b_spec]
    args = [a, b]
    if bias is not None:
        in_specs.append(pl.BlockSpec((1, bn), lambda i, j: (0, j)))
        args.append(bias)
    limit = _vmem_limit(_nbytes((bm, K), a.dtype), _nbytes((K, bn), b.dtype), _nbytes((bm, bn), F32))
    return _call(body, name=name, grid=(M // bm, N // bn), in_specs=in_specs,
                 out_specs=[pl.BlockSpec((bm, bn), lambda i, j: (i, j))],
                 out_shape=[jax.ShapeDtypeStruct((M, N), out_dtype)], args=args, vmem_limit=limit, carry=carry)[0]


def _mm_nt(a, b, *, bm, bn, out_dtype, name, add=None, carry=None):
    M, K = a.shape
    stacked = b.ndim == 3
    N = b.shape[1] if stacked else b.shape[0]
    n_sh = b.shape[0] if stacked else 1
    ks = b.shape[2] if stacked else K
    assert M % bm == 0 and N % bn == 0 and n_sh * ks == K
    dn = (((1,), (1,)), ((), ()))

    def body(*refs):
        a_ref, b_ref = refs[0], refs[1]
        o_ref = refs[-1]
        if stacked:
            acc = lax.dot_general(a_ref[:, 0:ks], b_ref[0], dn, preferred_element_type=F32)
            for s in range(1, n_sh):
                acc = acc + lax.dot_general(a_ref[:, s * ks:(s + 1) * ks], b_ref[s], dn, preferred_element_type=F32)
        else:
            acc = lax.dot_general(a_ref[...], b_ref[...], dn, preferred_element_type=F32)
        if add is not None:
            acc = acc + refs[2][...]
        o_ref[...] = acc.astype(o_ref.dtype)

    if stacked:
        b_spec = pl.BlockSpec((n_sh, bn, ks), lambda i, j: (0, j, 0))
    else:
        b_spec = pl.BlockSpec((bn, K), lambda i, j: (j, 0))
    in_specs = [pl.BlockSpec((bm, K), lambda i, j: (i, 0)), b_spec]
    args = [a, b]
    if add is not None:
        in_specs.append(pl.BlockSpec((bm, bn), lambda i, j: (i, j)))
        args.append(add)
    limit = _vmem_limit(_nbytes((bm, K), a.dtype), _nbytes((bn, K), b.dtype), _nbytes((bm, bn), F32))
    return _call(body, name=name, grid=(M // bm, N // bn), in_specs=in_specs,
                 out_specs=[pl.BlockSpec((bm, bn), lambda i, j: (i, j))],
                 out_shape=[jax.ShapeDtypeStruct((M, N), out_dtype)], args=args, vmem_limit=limit, carry=carry)[0]


def _mm_tn(a, b, *, bm, bn, out_dtype, name, n_stack=None, carry=None):
    T, M = a.shape
    N = b.shape[1]
    assert M % bm == 0 and N % bn == 0 and (n_stack is None or bn * n_stack == N)
    dn = (((0,), (0,)), ((), ()))

    def body(a_ref, b_ref, o_ref):
        acc = lax.dot_general(a_ref[...], b_ref[...], dn, preferred_element_type=F32)
        o_ref[...] = acc.astype(o_ref.dtype)

    if n_stack is None:
        out_spec = pl.BlockSpec((bm, bn), lambda i, j: (i, j))
        out_shape = jax.ShapeDtypeStruct((M, N), out_dtype)
    else:
        out_spec = pl.BlockSpec((None, bm, bn), lambda i, j: (j, i, 0))
        out_shape = jax.ShapeDtypeStruct((n_stack, M, bn), out_dtype)
    limit = _vmem_limit(_nbytes((T, bm), a.dtype), _nbytes((T, bn), b.dtype), _nbytes((bm, bn), F32))
    return _call(body, name=name, grid=(M // bm, N // bn),
                 in_specs=[pl.BlockSpec((T, bm), lambda i, j: (0, i)), pl.BlockSpec((T, bn), lambda i, j: (0, j))],
                 out_specs=[out_spec], out_shape=[out_shape], args=[a, b], vmem_limit=limit, carry=carry)[0]


ROW_BLK = 256


def _rstd(x):
    return lax.rsqrt(jnp.mean(x * x, axis=-1, keepdims=True) + EPS)


def _norm_bwd(xhat, rstd, dxhat):
    return rstd * (dxhat - xhat * jnp.mean(dxhat * xhat, axis=-1, keepdims=True))


def _row_spec(cols):
    return pl.BlockSpec((ROW_BLK, cols), lambda i: (i, 0))


def _vec_spec(cols):
    return pl.BlockSpec((1, cols), lambda i: (0, 0))


def _rms_fwd(x, g, *, name):
    T, D = x.shape

    def body(x_ref, g_ref, o_ref):
        xv = x_ref[...]
        o_ref[...] = (xv * _rstd(xv) * g_ref[...]).astype(o_ref.dtype)

    return pl.pallas_call(
        body, name=name, grid=(T // ROW_BLK,), in_specs=[_row_spec(D), _vec_spec(D)], out_specs=_row_spec(D),
        out_shape=jax.ShapeDtypeStruct((T, D), BF16),
        compiler_params=pltpu.CompilerParams(dimension_semantics=("parallel",)),
    )(x, g)


def _resid_norm(h, f, g_post, coef, g_next, *, name):
    T, D = h.shape

    def body(h_ref, f_ref, gp_ref, gn_ref, hn_ref, n_ref):
        fv = f_ref[...]
        hn = h_ref[...] + coef * (fv * _rstd(fv) * gp_ref[...])
        hn_ref[...] = hn
        n_ref[...] = (hn * _rstd(hn) * gn_ref[...]).astype(n_ref.dtype)

    return pl.pallas_call(
        body, name=name, grid=(T // ROW_BLK,),
        in_specs=[_row_spec(D), _row_spec(D), _vec_spec(D), _vec_spec(D)],
        out_specs=(_row_spec(D), _row_spec(D)),
        out_shape=(jax.ShapeDtypeStruct((T, D), F32), jax.ShapeDtypeStruct((T, D), BF16)),
        compiler_params=pltpu.CompilerParams(dimension_semantics=("parallel",)),
    )(h, f, g_post, g_next)


def _ffn_up(n, w_gu, *, bm, name, carry=None):
    M, K = n.shape
    n_sh, _, ns = w_gu.shape
    half = n_sh // 2

    def body(n_ref, wg_ref, wu_ref, g_ref, u_ref, a_ref):
        nv = n_ref[...]
        g = jnp.dot(nv, wg_ref[...], preferred_element_type=F32)
        u = jnp.dot(nv, wu_ref[...], preferred_element_type=F32)
        g_ref[...] = g.astype(g_ref.dtype)
        u_ref[...] = u.astype(u_ref.dtype)
        a_ref[...] = (g * jax.nn.sigmoid(g) * u).astype(a_ref.dtype)

    out_spec = pl.BlockSpec((bm, ns), lambda i, j: (i, j))
    out = jax.ShapeDtypeStruct((M, half * ns), BF16)
    limit = _vmem_limit(_nbytes((bm, K), n.dtype), 2 * _nbytes((K, ns), w_gu.dtype), 3 * _nbytes((bm, ns), F32))
    return _call(body, name=name, grid=(M // bm, half),
                 in_specs=[pl.BlockSpec((bm, K), lambda i, j: (i, 0)),
                           pl.BlockSpec((None, K, ns), lambda i, j: (j, 0, 0)),
                           pl.BlockSpec((None, K, ns), lambda i, j: (j + half, 0, 0))],
                 out_specs=[out_spec, out_spec, out_spec], out_shape=[out, out, out], args=[n, w_gu, w_gu],
                 vmem_limit=limit, carry=carry)


def _ffn_down_bwd(df, w_down, g, u, *, bm, name, carry=None):
    M, D = df.shape
    F = w_down.shape[0]
    dn = (((1,), (1,)), ((), ()))

    def body(df_ref, w_ref, g_ref, u_ref, o_ref):
        da = lax.dot_general(df_ref[...], w_ref[...], dn, preferred_element_type=F32)
        gv = g_ref[...].astype(F32)
        s = jax.nn.sigmoid(gv)
        o_ref[:, 0:F] = (da * u_ref[...].astype(F32) * (s * (1.0 + gv * (1.0 - s)))).astype(o_ref.dtype)
        o_ref[:, F:2 * F] = (da * (gv * s)).astype(o_ref.dtype)

    row = lambda w: pl.BlockSpec((bm, w), lambda i: (i, 0))
    limit = _vmem_limit(_nbytes((F, D), w_down.dtype), 2 * _nbytes((bm, F), BF16), _nbytes((bm, 2 * F), BF16),
                        2 * _nbytes((bm, F), F32))
    return _call(body, name=name, grid=(M // bm,),
                 in_specs=[row(D), pl.BlockSpec((F, D), lambda i: (0, 0)), row(F), row(F)],
                 out_specs=[row(2 * F)], out_shape=[jax.ShapeDtypeStruct((M, 2 * F), BF16)],
                 args=[df, w_down, g, u], vmem_limit=limit, carry=carry)[0]


def _ple_loss(h3, gp, e, g_post, target, *, name):
    T, D = h3.shape

    def body(h_ref, gp_ref, e_ref, g_ref, t_ref, dh_ref, de_ref, dgp_ref, loss_ref, dg_ref):
        @pl.when(pl.program_id(0) == 0)
        def _():
            loss_ref[...] = jnp.zeros_like(loss_ref)
            dg_ref[...] = jnp.zeros_like(dg_ref)

        gate = jax.nn.sigmoid(gp_ref[...])
        ev = e_ref[...]
        ge = gate * ev
        r = _rstd(ge)
        gh = ge * r
        gpost = g_ref[...]
        diff = h_ref[...] + gh * gpost - t_ref[...]
        loss_ref[...] += jnp.sum(diff * diff) * (0.5 / D)
        dh = diff * (1.0 / D)
        dh_ref[...] = dh
        dg_ref[...] += jnp.sum(dh * gh, axis=0, keepdims=True)
        dge = _norm_bwd(gh, r, dh * gpost)
        de_ref[...] = (dge * gate).astype(de_ref.dtype)
        dgp_ref[...] = (dge * ev * gate * (1.0 - gate)).astype(dgp_ref.dtype)

    return pl.pallas_call(
        body, name=name, grid=(T // ROW_BLK,),
        in_specs=[_row_spec(D), _row_spec(D), _row_spec(D), _vec_spec(D), _row_spec(D)],
        out_specs=(_row_spec(D), _row_spec(D), _row_spec(D), _vec_spec(LANES), _vec_spec(D)),
        out_shape=(jax.ShapeDtypeStruct((T, D), F32), jax.ShapeDtypeStruct((T, D), BF16),
                   jax.ShapeDtypeStruct((T, D), BF16), jax.ShapeDtypeStruct((1, LANES), F32),
                   jax.ShapeDtypeStruct((1, D), F32)),
        compiler_params=pltpu.CompilerParams(dimension_semantics=("arbitrary",)),
    )(h3, gp, e, g_post, target)


def _norms_bwd(dh_in, h, g_pre, dn, f, g_post, coef, *, name):
    T, D = h.shape

    def body(dhi_ref, h_ref, gpre_ref, dn_ref, f_ref, gpost_ref, dh_ref, df_ref, dgpre_ref, dgpost_ref, dsum_ref):
        @pl.when(pl.program_id(0) == 0)
        def _():
            dgpre_ref[...] = jnp.zeros_like(dgpre_ref)
            dgpost_ref[...] = jnp.zeros_like(dgpost_ref)
            dsum_ref[...] = jnp.zeros_like(dsum_ref)

        hv = h_ref[...]
        r = _rstd(hv)
        hh = hv * r
        dnv = dn_ref[...]
        dgpre_ref[...] += jnp.sum(dnv * hh, axis=0, keepdims=True)
        dh = dhi_ref[...] + _norm_bwd(hh, r, dnv * gpre_ref[...])
        dh_ref[...] = dh
        fv = f_ref[...]
        rf = _rstd(fv)
        fh = fv * rf
        dy = coef * dh
        dgpost_ref[...] += jnp.sum(dy * fh, axis=0, keepdims=True)
        df = _norm_bwd(fh, rf, dy * gpost_ref[...])
        dsum_ref[...] += jnp.sum(df, axis=0, keepdims=True)
        df_ref[...] = df.astype(df_ref.dtype)

    return pl.pallas_call(
        body, name=name, grid=(T // ROW_BLK,),
        in_specs=[_row_spec(D), _row_spec(D), _vec_spec(D), _row_spec(D), _row_spec(D), _vec_spec(D)],
        out_specs=(_row_spec(D), _row_spec(D), _vec_spec(D), _vec_spec(D), _vec_spec(D)),
        out_shape=(jax.ShapeDtypeStruct((T, D), F32), jax.ShapeDtypeStruct((T, D), BF16),
                   jax.ShapeDtypeStruct((1, D), F32), jax.ShapeDtypeStruct((1, D), F32),
                   jax.ShapeDtypeStruct((1, D), F32)),
        compiler_params=pltpu.CompilerParams(dimension_semantics=("arbitrary",)),
    )(dh_in, h, g_pre, dn, f, g_post)


def _norm_bwd_last(dh_in, h, g_pre, dn, *, name):
    T, D = h.shape

    def body(dhi_ref, h_ref, gpre_ref, dn_ref, dh_ref, dgpre_ref):
        @pl.when(pl.program_id(0) == 0)
        def _():
            dgpre_ref[...] = jnp.zeros_like(dgpre_ref)

        hv = h_ref[...]
        r = _rstd(hv)
        hh = hv * r
        dnv = dn_ref[...]
        dgpre_ref[...] += jnp.sum(dnv * hh, axis=0, keepdims=True)
        dh_ref[...] = dhi_ref[...] + _norm_bwd(hh, r, dnv * gpre_ref[...])

    return pl.pallas_call(
        body, name=name, grid=(T // ROW_BLK,),
        in_specs=[_row_spec(D), _row_spec(D), _vec_spec(D), _row_spec(D)],
        out_specs=(_row_spec(D), _vec_spec(D)),
        out_shape=(jax.ShapeDtypeStruct((T, D), F32), jax.ShapeDtypeStruct((1, D), F32)),
        compiler_params=pltpu.CompilerParams(dimension_semantics=("arbitrary",)),
    )(dh_in, h, g_pre, dn)


def _t5_index(max_dist, stride):
    rho = np.arange(QBLK)[:, None]
    kap = np.arange(2 * QBLK)[None, :]
    d = rho + QBLK - kap
    valid = (d >= 0) & (d <= max_dist)
    n = np.maximum(d, 0) * stride
    max_exact = NUM_BUCKETS // 2
    nf = np.maximum(n, 1).astype(np.float32)
    large = max_exact + (np.log(nf / np.float32(max_exact)) / np.float32(math.log(MAX_DISTANCE / max_exact))
                         * np.float32(NUM_BUCKETS - max_exact)).astype(np.int32)
    large = np.minimum(large, NUM_BUCKETS - 1)
    bucket = np.where(n < max_exact, n, large)
    return np.where(valid, bucket, -1).astype(np.int32)


def _bias_tiles(rel_bias, idx, head_off, *, name):
    def body(rb_ref, idx_ref, o_ref):
        h = pl.program_id(0)
        idxv = idx_ref[...]
        acc = jnp.where(idxv < 0, NEG, 0.0).astype(F32)
        for b in range(NUM_BUCKETS):
            acc = acc + jnp.where(idxv == b, rb_ref[b, head_off + h], 0.0)
        o_ref[0, 0] = acc
        kap = lax.broadcasted_iota(jnp.int32, (1, 2 * QBLK), 1)
        o_ref[1, 0] = jnp.where(kap < QBLK, NEG, acc)

    return pl.pallas_call(
        body, name=name, grid=(2 * N_PAIRS,),
        in_specs=[pl.BlockSpec(memory_space=pltpu.SMEM), pl.BlockSpec((QBLK, 2 * QBLK), lambda h: (0, 0))],
        out_specs=pl.BlockSpec((2, 1, QBLK, 2 * QBLK), lambda h: (0, h, 0, 0)),
        out_shape=jax.ShapeDtypeStruct((2, 2 * N_PAIRS, QBLK, 2 * QBLK), F32),
        compiler_params=pltpu.CompilerParams(dimension_semantics=("parallel",)),
    )(rel_bias, idx)


def _bias_grad(d_bias, idx, *, name):
    def body(db_ref, idx_ref, o_ref):
        h = pl.program_id(0)

        @pl.when(h == 0)
        def _():
            o_ref[...] = jnp.zeros_like(o_ref)

        idxv = idx_ref[...]
        dv = db_ref[0]
        rows = lax.broadcasted_iota(jnp.int32, (NUM_BUCKETS, LANES), 0)
        lanes = lax.broadcasted_iota(jnp.int32, (NUM_BUCKETS, LANES), 1)
        acc = o_ref[...]
        for b in range(NUM_BUCKETS):
            s = jnp.sum(jnp.where(idxv == b, dv, 0.0))
            acc = acc + jnp.where((rows == b) & (lanes == h), s, 0.0)
        o_ref[...] = acc

    return pl.pallas_call(
        body, name=name, grid=(2 * N_PAIRS,),
        in_specs=[pl.BlockSpec((1, QBLK, 2 * QBLK), lambda h: (h, 0, 0)),
                  pl.BlockSpec((QBLK, 2 * QBLK), lambda h: (0, 0))],
        out_specs=pl.BlockSpec((NUM_BUCKETS, LANES), lambda h: (0, 0)),
        out_shape=jax.ShapeDtypeStruct((NUM_BUCKETS, LANES), F32),
        compiler_params=pltpu.CompilerParams(dimension_semantics=("arbitrary",)),
    )(d_bias, idx)


def _lane():
    return lax.broadcasted_iota(jnp.int32, (1, LANES), 1)


def _head_sel(h):
    return (_lane() < HEAD_DIM) if h == 0 else (_lane() >= HEAD_DIM)


def _stat_lo():
    return (_lane() % HEAD_DIM) < (HEAD_DIM // 2)


def _stack_heads(t, scale=None):
    if scale is not None:
        t = t * scale
    zero = jnp.zeros_like(t)
    return jnp.concatenate([jnp.where(_head_sel(0), t, zero), jnp.where(_head_sel(1), t, zero)], axis=0)


def _pairs_per_step(r):
    return N_PAIRS if r > 1 else N_PAIRS // 2


def _class_spec(L, r, cols, stride):
    chunks = N_PAIRS // _pairs_per_step(r)
    return pl.BlockSpec((L, MIX_W // chunks), lambda j, c: (0, (cols + j * stride) * chunks + c))


def _rows(b, n_blocks=1):
    return pl.ds(pl.multiple_of(b * QBLK, QBLK), n_blocks * QBLK)


def _key_window(ref, b, cols, first):
    if first:
        blk = ref[0:QBLK, cols]
        return jnp.concatenate([blk, blk], axis=0)
    return ref[_rows(b - 1, 2), cols]


def _band_fwd(qa, ka, va, bias, *, r, q_col, k_col, v_col, stride, name, carry=None):
    L = qa.shape[0]
    nb = L // QBLK
    dn = (((1,), (1,)), ((), ()))
    scale = HEAD_DIM ** -0.5

    pps = _pairs_per_step(r)

    def body(q_ref, k_ref, v_ref, bias_ref, o_ref, lse_ref):
        sel0 = _head_sel(0)
        pair0 = pl.program_id(1) * pps

        def block(b, first):
            rows = _rows(b)
            for i in range(pps):
                cols = slice(i * LANES, (i + 1) * LANES)
                q2 = _stack_heads(q_ref[rows, cols], scale)
                k = _key_window(k_ref, b, cols, first)
                v = _key_window(v_ref, b, cols, first)
                bias2 = bias_ref[1 if first else 0, pl.ds(2 * (pair0 + i), 2)].reshape(2 * QBLK, 2 * QBLK)
                s = lax.dot_general(q2, k, dn, preferred_element_type=F32) + bias2
                m = jnp.max(s, axis=1, keepdims=True)
                p = jnp.exp(s - m)
                l = jnp.sum(p, axis=1, keepdims=True)
                o = jnp.dot(p.astype(v.dtype), v, preferred_element_type=F32) / l
                lse = m + jnp.log(l)
                o_ref[rows, cols] = jnp.where(sel0, o[0:QBLK], o[QBLK:]).astype(o_ref.dtype)
                lse_ref[rows, cols] = jnp.where(sel0, lse[0:QBLK], lse[QBLK:])

        block(0, True)
        if nb > 1:
            pl.loop(1, nb)(lambda b: block(b, False))

    bias_spec = pl.BlockSpec((2, 2 * N_PAIRS, QBLK, 2 * QBLK), lambda j, c: (0, 0, 0, 0))
    out_spec = _class_spec(L, r, 0, 1)
    blk_bytes = _nbytes((L, pps * LANES), F32)
    return _call(
        body, name=name, grid=(r, N_PAIRS // pps),
        in_specs=[_class_spec(L, r, q_col, stride), _class_spec(L, r, k_col, stride),
                  _class_spec(L, r, v_col, stride), bias_spec],
        out_specs=[out_spec, out_spec],
        out_shape=[jax.ShapeDtypeStruct((L, r * MIX_W), BF16), jax.ShapeDtypeStruct((L, r * MIX_W), F32)],
        args=[qa, ka, va, bias], vmem_limit=_vmem_limit(*([blk_bytes] * 4)), carry=carry)


def _merge(branches, sink, *, name):
    T, W = branches[0][0].shape
    nbr = len(branches)

    def body(*refs):
        sink_ref = refs[2 * nbr]
        out_ref, outb_ref, lse_ref = refs[2 * nbr + 1:]
        sk = sink_ref[...]
        mx = sk
        for t in range(nbr):
            mx = jnp.maximum(mx, refs[2 * t + 1][...])
        den = jnp.exp(sk - mx)
        num = jnp.zeros_like(mx)
        for t in range(nbr):
            w = jnp.exp(refs[2 * t + 1][...] - mx)
            den = den + w
            num = num + w * refs[2 * t][...].astype(F32)
        out = num / den
        out_ref[...] = out
        outb_ref[...] = out.astype(outb_ref.dtype)
        lse_ref[...] = mx + jnp.log(den)

    args = [a for br in branches for a in br] + [sink]
    return pl.pallas_call(
        body, name=name, grid=(T // ROW_BLK,),
        in_specs=[_row_spec(W)] * (2 * nbr) + [_vec_spec(W)],
        out_specs=(_row_spec(W), _row_spec(W), _row_spec(W)),
        out_shape=(jax.ShapeDtypeStruct((T, W), F32), jax.ShapeDtypeStruct((T, W), BF16),
                   jax.ShapeDtypeStruct((T, W), F32)),
        compiler_params=pltpu.CompilerParams(dimension_semantics=("parallel",)),
    )(*args)


def _attn_delta(dmix, outs, lses, sink, *, name):
    T = dmix.shape[0]
    W = MIX_W

    def body(dmix_ref, oa_ref, ob_ref, la_ref, lb_ref, sink_ref, doa_ref, dob_ref, sta_ref, stb_ref, dsink_ref):
        @pl.when(pl.program_id(0) == 0)
        def _():
            dsink_ref[...] = jnp.zeros_like(dsink_ref)

        sel0, lo = _head_sel(0), _stat_lo()
        for mixer, (o_ref, l_ref, do_ref, st_ref) in enumerate(
                ((oa_ref, la_ref, doa_ref, sta_ref), (ob_ref, lb_ref, dob_ref, stb_ref))):
            for i in range(N_PAIRS):
                cols = slice(i * LANES, (i + 1) * LANES)
                do = dmix_ref[:, mixer * W + i * LANES:mixer * W + (i + 1) * LANES]
                prod = do * o_ref[:, cols]
                d0 = jnp.sum(jnp.where(sel0, prod, 0.0), axis=1, keepdims=True)
                d1 = jnp.sum(jnp.where(sel0, 0.0, prod), axis=1, keepdims=True)
                delta = jnp.where(sel0, d0, d1)
                lse = l_ref[:, cols]
                st_ref[:, cols] = jnp.where(lo, lse, delta)
                do_ref[:, cols] = do.astype(do_ref.dtype)
                if mixer == 0:
                    dsink_ref[:, cols] += -jnp.sum(jnp.exp(sink_ref[:, cols] - lse) * delta, axis=0, keepdims=True)

    return pl.pallas_call(
        body, name=name, grid=(T // ROW_BLK,),
        in_specs=[_row_spec(2 * W), _row_spec(W), _row_spec(W), _row_spec(W), _row_spec(W), _vec_spec(W)],
        out_specs=(_row_spec(W), _row_spec(W), _row_spec(W), _row_spec(W), _vec_spec(W)),
        out_shape=(jax.ShapeDtypeStruct((T, W), BF16), jax.ShapeDtypeStruct((T, W), BF16),
                   jax.ShapeDtypeStruct((T, W), F32), jax.ShapeDtypeStruct((T, W), F32),
                   jax.ShapeDtypeStruct((1, W), F32)),
        compiler_params=pltpu.CompilerParams(dimension_semantics=("arbitrary",)),
    )(dmix, outs[0], outs[1], lses[0], lses[1], sink)


def _band_bwd(qa, ka, va, d_out, stat, bias, *, r, q_col, k_col, v_col, stride, name, carry=None):
    L = qa.shape[0]
    nb = L // QBLK
    dn_nt = (((1,), (1,)), ((), ()))
    dn_tn = (((0,), (0,)), ((), ()))
    scale = HEAD_DIM ** -0.5

    pps = _pairs_per_step(r)

    def body(q_ref, k_ref, v_ref, do_ref, st_ref, bias_ref, dq_ref, dk_ref, dv_ref, db_ref, dk_carry, dv_carry):
        @pl.when((pl.program_id(0) == 0) & (pl.program_id(1) == 0))
        def _():
            db_ref[...] = jnp.zeros_like(db_ref)

        lo, sel0 = _stat_lo(), _head_sel(0)
        pair0 = pl.program_id(1) * pps

        def block(b, first):
            rows = _rows(b)
            for i in range(pps):
                heads = pl.ds(2 * (pair0 + i), 2)
                cols = slice(i * LANES, (i + 1) * LANES)
                q2 = _stack_heads(q_ref[rows, cols], scale)
                k = _key_window(k_ref, b, cols, first)
                v = _key_window(v_ref, b, cols, first)
                do2 = _stack_heads(do_ref[rows, cols])
                st = st_ref[rows, cols]
                lse2 = jnp.concatenate(
                    [jnp.max(jnp.where(_head_sel(h) & lo, st, -jnp.inf), axis=1, keepdims=True) for h in range(2)], axis=0)
                delta2 = jnp.concatenate(
                    [jnp.sum(jnp.where(_head_sel(h) & ~lo, st, 0.0), axis=1, keepdims=True) for h in range(2)],
                    axis=0) * (2.0 / HEAD_DIM)
                bias2 = bias_ref[1 if first else 0, heads].reshape(2 * QBLK, 2 * QBLK)
                s = lax.dot_general(q2, k, dn_nt, preferred_element_type=F32) + bias2
                p = jnp.exp(s - lse2)
                dp = lax.dot_general(do2, v, dn_nt, preferred_element_type=F32)
                ds = p * (dp - delta2)
                db_ref[heads] += ds.reshape(2, QBLK, 2 * QBLK)
                dsb = ds.astype(k.dtype)
                dq2 = jnp.dot(dsb, k, preferred_element_type=F32)
                dq_ref[rows, cols] = (jnp.where(sel0, dq2[0:QBLK], dq2[QBLK:]) * scale).astype(dq_ref.dtype)
                dk_acc = lax.dot_general(dsb, q2, dn_tn, preferred_element_type=F32)
                dv_acc = lax.dot_general(p.astype(k.dtype), do2, dn_tn, preferred_element_type=F32)
                if not first:
                    prev = _rows(b - 1)
                    dk_ref[prev, cols] = (dk_carry[:, cols] + dk_acc[0:QBLK]).astype(dk_ref.dtype)
                    dv_ref[prev, cols] = (dv_carry[:, cols] + dv_acc[0:QBLK]).astype(dv_ref.dtype)
                dk_carry[:, cols] = dk_acc[QBLK:2 * QBLK]
                dv_carry[:, cols] = dv_acc[QBLK:2 * QBLK]

        block(0, True)
        if nb > 1:
            pl.loop(1, nb)(lambda b: block(b, False))

        last = _rows(nb - 1)
        dk_ref[last, :] = dk_carry[...].astype(dk_ref.dtype)
        dv_ref[last, :] = dv_carry[...].astype(dv_ref.dtype)

    tok_spec = _class_spec(L, r, 0, 1)
    bias_spec = pl.BlockSpec((2, 2 * N_PAIRS, QBLK, 2 * QBLK), lambda j, c: (0, 0, 0, 0))
    dbias_spec = pl.BlockSpec((2 * N_PAIRS, QBLK, 2 * QBLK), lambda j, c: (0, 0, 0))
    grad = jax.ShapeDtypeStruct((L, r * MIX_W), BF16)
    blk_bytes = _nbytes((L, pps * LANES), BF16)
    carry_shape = pltpu.VMEM((QBLK, pps * LANES), F32)
    return _call(
        body, name=name, grid=(r, N_PAIRS // pps),
        in_specs=[_class_spec(L, r, q_col, stride), _class_spec(L, r, k_col, stride),
                  _class_spec(L, r, v_col, stride), tok_spec, tok_spec, bias_spec],
        out_specs=[tok_spec, tok_spec, tok_spec, dbias_spec],
        out_shape=[grad, grad, grad, jax.ShapeDtypeStruct((2 * N_PAIRS, QBLK, 2 * QBLK), F32)],
        args=[qa, ka, va, d_out, stat, bias], scratch=[carry_shape, carry_shape],
        vmem_limit=_vmem_limit(*([blk_bytes] * 9)), semantics=("arbitrary", "arbitrary"), carry=carry)


def _dz_assemble(dq_a, dk_a, dv_a, b_grads, *, name):
    T = dq_a.shape[0]
    W = MIX_W

    def group_sum(x):
        u0 = x[:, 0:LANES] + x[:, LANES:2 * LANES]
        u1 = x[:, 2 * LANES:3 * LANES] + x[:, 3 * LANES:4 * LANES]
        s0 = u0 + pltpu.roll(u0, HEAD_DIM, 1)
        s1 = u1 + pltpu.roll(u1, HEAD_DIM, 1)
        return jnp.where(_head_sel(0), s0, s1)

    def body(*refs):
        dqa_ref, dka_ref, dva_ref = refs[0:3]
        b_refs = refs[3:12]
        dza_ref, dzb_ref, sa_ref, sb_ref = refs[12:]

        @pl.when(pl.program_id(0) == 0)
        def _():
            sa_ref[...] = jnp.zeros_like(sa_ref)
            sb_ref[...] = jnp.zeros_like(sb_ref)

        def put(dst, acc, col, part):
            w = part.shape[1]
            dst[:, col:col + w] = part.astype(dst.dtype)
            acc[:, col:col + w] += jnp.sum(part, axis=0, keepdims=True)

        put(dza_ref, sa_ref, 0, dqa_ref[...].astype(F32))
        put(dza_ref, sa_ref, W, group_sum(dka_ref[...].astype(F32)))
        put(dza_ref, sa_ref, W + LANES, group_sum(dva_ref[...].astype(F32)))
        for t in range(3):
            part = (b_refs[t][...].astype(F32) + b_refs[3 + t][...].astype(F32) + b_refs[6 + t][...].astype(F32))
            put(dzb_ref, sb_ref, t * W, part)

    args = [dq_a, dk_a, dv_a] + [g[t] for g in b_grads for t in range(3)]
    return pl.pallas_call(
        body, name=name, grid=(T // ROW_BLK,),
        in_specs=[_row_spec(W)] * 12,
        out_specs=(_row_spec(ZA_W), _row_spec(ZB_W), _vec_spec(ZA_W), _vec_spec(ZB_W)),
        out_shape=(jax.ShapeDtypeStruct((T, ZA_W), BF16), jax.ShapeDtypeStruct((T, ZB_W), BF16),
                   jax.ShapeDtypeStruct((1, ZA_W), F32), jax.ShapeDtypeStruct((1, ZB_W), F32)),
        compiler_params=pltpu.CompilerParams(dimension_semantics=("arbitrary",)),
    )(*args)


def _place():
    x, y, c = lax.axis_index("x"), lax.axis_index("y"), lax.axis_index("c")
    chips = [(1 - x, y), (x, 1 - y), (1 - x, 1 - y)]
    return x, y, c, chips


def _cast_place(shard, k_idx, *, name):
    rs, cs = shard.shape
    rb = _row_block(rs, 256, 16)
    nblk = rs // rb

    def body(k_ref, s_ref, o_ref):
        o_ref[...] = s_ref[...].astype(o_ref.dtype)

    grid_spec = pltpu.PrefetchScalarGridSpec(
        num_scalar_prefetch=1, grid=(nblk,),
        in_specs=[pl.BlockSpec((rb, cs), lambda t, k_ref: (t, 0))],
        out_specs=pl.BlockSpec((rb, cs), lambda t, k_ref: (k_ref[0] * nblk + t, 0)))
    return pl.pallas_call(
        body, name=name, grid_spec=grid_spec, out_shape=jax.ShapeDtypeStruct((N_CHIPS * rs, cs), BF16),
        compiler_params=pltpu.CompilerParams(dimension_semantics=("parallel",)),
    )(k_idx, shard)


def _gather_carry(fulls, part=(0, 1)):
    n = len(fulls)
    p_idx, n_parts = part

    def piece(ref, chip_idx, half):
        rs = ref.shape[0] // N_CHIPS
        rh = rs // 2
        rows = rh // n_parts
        return ref.at[pl.ds(chip_idx * rs + half * rh + p_idx * rows, rows)]

    def copy(sems, sem, src_ref, dst_ref, to):
        return pltpu.make_async_remote_copy(src_ref=src_ref, dst_ref=dst_ref, send_sem=sems[0].at[sem],
                                            recv_sem=sems[1].at[sem], device_id=to, device_id_type=MESH)

    def ici_copy(ins, outs, sems, i, j, chip, k, c):
        return copy(sems, 3 * i + j, piece(ins[i], k, c), piece(outs[i], k, c), (*chip, c))

    def start(ins, outs, sems):
        x, y, c, chips = _place()
        for i in range(n):
            for j, chip in enumerate(chips):
                ici_copy(ins, outs, sems, i, j, chip, 2 * x + y, c).start()

    def finish(ins, outs, sems):
        x, y, c, chips = _place()
        sibling = (x, y, 1 - c)
        passed = []
        for i in range(n):
            for j, chip in enumerate(chips):
                region = piece(outs[i], 2 * chip[0] + chip[1], c)
                copy(sems, 3 * i + j, region, region, (*chip, c)).wait_recv()
                cp = copy(sems, 3 * n + 3 * i + j, region, region, sibling)
                cp.start()
                passed.append(cp)
        for i in range(n):
            for j, chip in enumerate(chips):
                region = piece(outs[i], 2 * chip[0] + chip[1], 1 - c)
                copy(sems, 3 * n + 3 * i + j, region, region, sibling).wait_recv()
        for i in range(n):
            for j, chip in enumerate(chips):
                ici_copy(ins, outs, sems, i, j, chip, 2 * x + y, c).wait_send()
        for cp in passed:
            cp.wait_send()

    return _Carry(fulls, [jax.ShapeDtypeStruct(a.shape, a.dtype) for a in fulls], {i: i for i in range(n)},
                  [pltpu.SemaphoreType.DMA((6 * n,)), pltpu.SemaphoreType.DMA((6 * n,))], start, finish)


def _pair_swap_carry(grads):
    n = len(grads)

    def copy(ins, outs, sems, i):
        x, y, c, _ = _place()
        return pltpu.make_async_remote_copy(src_ref=ins[i].at[:, 1 - c], dst_ref=outs[i], send_sem=sems[0].at[i],
                                            recv_sem=sems[1].at[i], device_id=(x, y, 1 - c), device_id_type=MESH)

    def start(ins, outs, sems):
        for i in range(n):
            copy(ins, outs, sems, i).start()

    def finish(ins, outs, sems):
        for i in range(n):
            copy(ins, outs, sems, i).wait()

    return _Carry(grads, [jax.ShapeDtypeStruct((a.shape[0], a.shape[2], a.shape[3]), a.dtype) for a in grads], {},
                  [pltpu.SemaphoreType.DMA((n,)), pltpu.SemaphoreType.DMA((n,))], start, finish)


def _pair_sum(g4, got, c_idx, *, name):
    _, _, rh, cs = g4.shape
    rb = _row_block(rh, 256, 16)

    def body(c_ref, own_ref, got_ref, o_ref):
        o_ref[...] = (own_ref[...].astype(F32) + got_ref[...].astype(F32)).astype(o_ref.dtype)

    grid_spec = pltpu.PrefetchScalarGridSpec(
        num_scalar_prefetch=1, grid=(N_CHIPS, rh // rb),
        in_specs=[pl.BlockSpec((None, None, rb, cs), lambda k, t, c_ref: (k, c_ref[0], t, 0)),
                  pl.BlockSpec((None, rb, cs), lambda k, t, c_ref: (k, t, 0))],
        out_specs=pl.BlockSpec((None, rb, cs), lambda k, t, c_ref: (k, t, 0)))
    return pl.pallas_call(
        body, name=name, grid_spec=grid_spec, out_shape=jax.ShapeDtypeStruct((N_CHIPS, rh, cs), BF16),
        compiler_params=pltpu.CompilerParams(dimension_semantics=("parallel", "parallel")),
    )(c_idx, g4, got)


def _chip_scatter_carry(sums):
    n = len(sums)

    def copies(ins, outs, sems):
        x, y, c, chips = _place()
        return [pltpu.make_async_remote_copy(src_ref=ins[i].at[2 * chip[0] + chip[1]], dst_ref=outs[i].at[j],
                                             send_sem=sems[0].at[3 * i + j], recv_sem=sems[1].at[3 * i + j],
                                             device_id=(*chip, c), device_id_type=MESH)
                for i in range(n) for j, chip in enumerate(chips)]

    def start(ins, outs, sems):
        for cp in copies(ins, outs, sems):
            cp.start()

    def finish(ins, outs, sems):
        for cp in copies(ins, outs, sems):
            cp.wait()

    return _Carry(sums, [jax.ShapeDtypeStruct((3,) + a.shape[1:], a.dtype) for a in sums], {},
                  [pltpu.SemaphoreType.DMA((3 * n,)), pltpu.SemaphoreType.DMA((3 * n,))], start, finish)


def _chip_sum(sums, got, kc_idx, *, name):
    _, rh, cs = sums.shape
    rb = _row_block(rh, 256, 16)
    nblk = rh // rb

    def body(kc_ref, own_ref, got_ref, o_ref):
        acc = own_ref[...].astype(F32)
        for j in range(3):
            acc = acc + got_ref[j].astype(F32)
        o_ref[...] = acc

    grid_spec = pltpu.PrefetchScalarGridSpec(
        num_scalar_prefetch=1, grid=(nblk,),
        in_specs=[pl.BlockSpec((None, rb, cs), lambda t, kc: (kc[0], t, 0)),
                  pl.BlockSpec((3, rb, cs), lambda t, kc: (0, t, 0))],
        out_specs=pl.BlockSpec((rb, cs), lambda t, kc: (kc[1] * nblk + t, 0)))
    return pl.pallas_call(
        body, name=name, grid_spec=grid_spec, out_shape=jax.ShapeDtypeStruct((2 * rh, cs), F32),
        compiler_params=pltpu.CompilerParams(dimension_semantics=("parallel",)),
    )(kc_idx, sums, got)


def _half_swap_carry(shards):
    n = len(shards)

    def copy(ins, outs, sems, i, to_my_half):
        x, y, c, _ = _place()
        rh = ins[i].shape[0] // 2
        half = c if to_my_half else 1 - c
        return pltpu.make_async_remote_copy(
            src_ref=ins[i].at[pl.ds(c * rh, rh)], dst_ref=outs[i].at[pl.ds(half * rh, rh)],
            send_sem=sems[0].at[i], recv_sem=sems[1].at[i], device_id=(x, y, 1 - c), device_id_type=MESH)

    def start(ins, outs, sems):
        for i in range(n):
            copy(ins, outs, sems, i, True).start()

    def finish(ins, outs, sems):
        for i in range(n):
            copy(ins, outs, sems, i, False).wait_recv()
        for i in range(n):
            copy(ins, outs, sems, i, True).wait_send()

    return _Carry(shards, [jax.ShapeDtypeStruct(a.shape, a.dtype) for a in shards], {i: i for i in range(n)},
                  [pltpu.SemaphoreType.DMA((n,)), pltpu.SemaphoreType.DMA((n,))], start, finish)


def _small_all_reduce(block, *, name):
    rows = block.shape[0]

    def body(x_ref, o_ref, buf, send_sems, recv_sems):
        x, y, c, _ = _place()
        me = 4 * x + 2 * y + c
        buf[me] = x_ref[...]
        copies = []
        for d in range(1, 8):
            to = (1 - x if d & 4 else x, 1 - y if d & 2 else y, 1 - c if d & 1 else c)
            cp = pltpu.make_async_remote_copy(src_ref=x_ref, dst_ref=buf.at[me], send_sem=send_sems.at[d - 1],
                                              recv_sem=recv_sems.at[d - 1], device_id=to, device_id_type=MESH)
            cp.start()
            copies.append((cp, to))
        for d in range(1, 8):
            to = copies[d - 1][1]
            sender = 4 * to[0] + 2 * to[1] + to[2]
            pltpu.make_async_remote_copy(src_ref=x_ref, dst_ref=buf.at[sender], send_sem=send_sems.at[d - 1],
                                         recv_sem=recv_sems.at[d - 1], device_id=to, device_id_type=MESH).wait_recv()
        for cp, _ in copies:
            cp.wait_send()
        acc = buf[0]
        for s in range(1, 8):
            acc = acc + buf[s]
        o_ref[...] = acc

    vm = pl.BlockSpec(memory_space=pltpu.VMEM)
    return pl.pallas_call(
        body, name=name, in_specs=[vm], out_specs=vm, out_shape=jax.ShapeDtypeStruct(block.shape, F32),
        scratch_shapes=[pltpu.VMEM((8, rows, LANES), F32), pltpu.SemaphoreType.DMA((7,)),
                        pltpu.SemaphoreType.DMA((7,))],
    )(block)


def _adamw(w, g, m, v, *, name):
    R, C = w.shape
    rb = _row_block(R, 256, SUBLANES)
    c1 = 1.0 - ADAM_B1 ** ADAM_STEP
    c2 = 1.0 - ADAM_B2 ** ADAM_STEP

    def body(w_ref, g_ref, m_ref, v_ref, d_ref, mo_ref, vo_ref):
        gv = g_ref[...]
        mn = ADAM_B1 * m_ref[...] + (1.0 - ADAM_B1) * gv
        vn = ADAM_B2 * v_ref[...] + (1.0 - ADAM_B2) * (gv * gv)
        d_ref[...] = -ADAM_LR * ((mn / c1) / (jnp.sqrt(vn / c2) + ADAM_EPS) + ADAM_WD * w_ref[...])
        mo_ref[...] = mn
        vo_ref[...] = vn

    spec = pl.BlockSpec((rb, C), lambda i: (i, 0))
    shp = jax.ShapeDtypeStruct((R, C), F32)
    return pl.pallas_call(
        body, name=name, grid=(R // rb,), in_specs=[spec] * 4, out_specs=(spec, spec, spec),
        out_shape=(shp, shp, shp), compiler_params=pltpu.CompilerParams(dimension_semantics=("parallel",)),
    )(w, g, m, v)


def _pack(parts):
    rows = []
    for a in parts:
        flat = a.reshape(-1).astype(F32)
        pad = (-flat.shape[0]) % PACK_UNIT
        rows.append(jnp.pad(flat, (0, pad)).reshape(-1, LANES))
    return jnp.concatenate(rows, axis=0)


def _unpack(block, shapes):
    out, row = [], 0
    for shp in shapes:
        size = int(np.prod(shp))
        nrows = -(-size // PACK_UNIT) * SUBLANES
        out.append(block[row:row + nrows].reshape(-1)[:size].reshape(shp))
        row += nrows
    return out


def _local_step(x, p, target, small, sched):
    T = x.shape[0]
    W = MIX_W
    rel_bias = small["rel_bias"]
    b_in_a, b_in_b = small["b_in"][:, 0:ZA_W], small["b_in"][:, ZA_W:]

    n1 = _rms_fwd(x, small["ffn1_pre_g"], name="n1")
    w_gu1 = sched.weight("ffn1_w_gu")
    *gu1, a1 = sched.run(_ffn_up, n1, w_gu1, bm=512, name="ffn1_gu")
    w_d1 = sched.weight("ffn1_w_down")
    f1 = sched.run(_mm_nn, a1, w_d1, bm=1024, bn=512, out_dtype=F32, name="ffn1_down")
    h1, n2 = _resid_norm(x, f1, small["ffn1_post_g"], 0.5, small["attn_pre_g"], name="h1")
    win_a, win_b = sched.weight("w_in")
    za = _mm_nn(n2, win_a, bm=1024, bn=ZA_W, out_dtype=BF16, name="in_proj_a", bias=b_in_a)
    zb = _mm_nn(n2, win_b, bm=1024, bn=ZB_W // 2, out_dtype=BF16, name="in_proj_b", bias=b_in_b)

    k_a = za[:, W:W + LANES].reshape(T, 2, 1, HEAD_DIM)
    v_a = za[:, W + LANES:W + 2 * LANES].reshape(T, 2, 1, HEAD_DIM)
    k_a = jnp.broadcast_to(k_a, (T, 2, 4, HEAD_DIM)).reshape(T, W)
    v_a = jnp.broadcast_to(v_a, (T, 2, 4, HEAD_DIM)).reshape(T, W)
    idx_a = jnp.asarray(_t5_index(A_WINDOW - 1, 1))
    bias_a = _bias_tiles(rel_bias, idx_a, 0, name="bias_a")
    a_args = dict(r=1, q_col=0, k_col=0, v_col=0, stride=0)
    o_a, lse_a1 = sched.run(_band_fwd, za, k_a, v_a, bias_a, name="band_a_fwd", **a_args)
    sink_row = jnp.repeat(small["sinks"], HEAD_DIM, axis=1)
    out_a, out_a_bf, lse_a = _merge([(o_a, lse_a1)], sink_row, name="merge_a")

    no_sink = jnp.full((1, W), NEG, F32)
    b_ctx, branches = [], []
    for t, (window, dil) in enumerate(B_PATTERNS):
        idx = jnp.asarray(_t5_index(window // dil, dil))
        bias = _bias_tiles(rel_bias, idx, 2 * N_PAIRS, name=f"bias_b{t}")
        zb_r = zb.reshape(T // dil, dil * ZB_W)
        args = dict(r=dil, q_col=0, k_col=1, v_col=2, stride=ZB_W // W)
        o, l = sched.run(_band_fwd, zb_r, zb_r, zb_r, bias, name=f"band_b{t}_fwd", **args)
        branches.append((o.reshape(T, W), l.reshape(T, W)))
        b_ctx.append((idx, bias, zb_r, args, dil))
    out_b, out_b_bf, lse_b = _merge(branches, no_sink, name="merge_b")

    mix = jnp.concatenate([out_a_bf, out_b_bf], axis=1)
    w_out = sched.weight("w_out")
    att = _mm_nn(mix, w_out, bm=1024, bn=1024, out_dtype=F32, name="out_proj", bias=small["b_out"])
    h2, n3 = _resid_norm(h1, att, small["attn_post_g"], 1.0, small["ffn2_pre_g"], name="h2")
    w_gu2, w_d2 = sched.weight("ffn2_w_gu"), sched.weight("ffn2_w_down")
    *gu2, a2 = _ffn_up(n3, w_gu2, bm=512, name="ffn2_gu")
    f2 = _mm_nn(a2, w_d2, bm=1024, bn=512, out_dtype=F32, name="ffn2_down")
    h3, n4 = _resid_norm(h2, f2, small["ffn2_post_g"], 0.5, small["ple_pre_g"], name="h3")
    w_gate = sched.weight("w_ple_gate")
    gp = _mm_nn(n4, w_gate, bm=1024, bn=1024, out_dtype=F32, name="ple_gate")
    p_bf = p.astype(BF16)
    e = _mm_nn(p_bf, sched.weight("w_ple_proj"), bm=1024, bn=256, out_dtype=F32, name="ple_proj")
    dh4, de, dgp, loss, d_ple_post = _ple_loss(h3, gp, e, small["ple_post_g"], target, name="ple_loss")

    gs = {"ple_post_g": d_ple_post}
    sched.grad("w_ple_proj", _mm_tn(p_bf, de, bm=256, bn=256, out_dtype=BF16, name="d_w_ple", n_stack=N_CHIPS))
    sched.grad("w_ple_gate", _mm_tn(n4, dgp, bm=512, bn=1024, out_dtype=BF16, name="d_w_gate"))
    dn4 = sched.run(_mm_nt, dgp, w_gate, bm=1024, bn=1024, out_dtype=F32, name="d_n4")
    dh3, df2, gs["ple_pre_g"], gs["ffn2_post_g"], _ = _norms_bwd(
        dh4, h3, small["ple_pre_g"], dn4, f2, small["ffn2_post_g"], 0.5, name="bwd_h3")

    def ffn_bwd(df, a, gu, n, w_down, w_gu, tag):
        dgu = sched.run(_ffn_down_bwd, df, w_down, gu[0], gu[1], bm=256, name=f"ffn{tag}_d_a")
        sched.grad(f"ffn{tag}_w_down", _mm_tn(a, df, bm=256, bn=1024, out_dtype=BF16, name=f"ffn{tag}_d_w_down"))
        sched.grad(f"ffn{tag}_w_gu", sched.run(_mm_tn, n, dgu, bm=512, bn=1408, out_dtype=BF16,
                                              name=f"ffn{tag}_d_w_gu", n_stack=N_CHIPS))
        return sched.run(_mm_nt, dgu, w_gu, bm=512, bn=1024, out_dtype=F32, name=f"ffn{tag}_d_n")

    dn3 = ffn_bwd(df2, a2, gu2, n3, w_d2, w_gu2, "2")
    dh2, datt, gs["ffn2_pre_g"], gs["attn_post_g"], gs["b_out"] = _norms_bwd(
        dh3, h2, small["ffn2_pre_g"], dn3, att, small["attn_post_g"], 1.0, name="bwd_h2")
    sched.grad("w_out", _mm_tn(mix, datt, bm=512, bn=1024, out_dtype=BF16, name="d_w_out"))
    dmix = sched.run(_mm_nt, datt, w_out, bm=1024, bn=1024, out_dtype=F32, name="d_mix")

    do_a, do_b, stat_a, stat_b, dsink = _attn_delta(dmix, (out_a, out_b), (lse_a, lse_b), sink_row, name="attn_delta")
    gs["sinks"] = dsink[:, ::HEAD_DIM]
    dq_a, dk_a, dv_a, dbias_a = sched.run(_band_bwd, za, k_a, v_a, do_a, stat_a, bias_a, name="band_a_bwd", **a_args)
    d_rel_a = _bias_grad(dbias_a, idx_a, name="d_rel_a")
    b_grads = []
    d_rel_b = None
    for t, (idx, bias, zb_r, args, dil) in enumerate(b_ctx):
        shp = (T // dil, dil * W)
        dq, dk, dv, dbias = sched.run(_band_bwd, zb_r, zb_r, zb_r, do_b.reshape(shp), stat_b.reshape(shp), bias,
                                      name=f"band_b{t}_bwd", **args)
        b_grads.append((dq.reshape(T, W), dk.reshape(T, W), dv.reshape(T, W)))
        d_rel = _bias_grad(dbias, idx, name=f"d_rel_b{t}")
        d_rel_b = d_rel if d_rel_b is None else d_rel_b + d_rel
    gs["rel_bias"] = jnp.concatenate([d_rel_a[:, 0:2 * N_PAIRS], d_rel_b[:, 0:2 * N_PAIRS]], axis=1)
    dza, dzb, dsum_a, dsum_b = _dz_assemble(dq_a, dk_a, dv_a, b_grads, name="d_z")
    gs["b_in"] = jnp.concatenate([dsum_a, dsum_b], axis=1)
    sched.grad("w_in", (_mm_tn(n2, dza, bm=512, bn=ZA_W, out_dtype=BF16, name="d_w_in_a"),
                        _mm_tn(n2, dzb, bm=512, bn=ZB_W // 2, out_dtype=BF16, name="d_w_in_b")))
    dn2 = _mm_nt(dza, win_a, bm=1024, bn=1024, out_dtype=F32, name="d_n2_a")
    dn2 = sched.run(_mm_nt, dzb, win_b, bm=1024, bn=1024, out_dtype=F32, name="d_n2_b", add=dn2)
    dh1, df1, gs["attn_pre_g"], gs["ffn1_post_g"], _ = _norms_bwd(
        dh2, h1, small["attn_pre_g"], dn2, f1, small["ffn1_post_g"], 0.5, name="bwd_h1")
    dn1 = ffn_bwd(df1, a1, gu1, n1, w_d1, w_gu1, "1")
    grad_x, gs["ffn1_pre_g"] = _norm_bwd_last(dh1, x, small["ffn1_pre_g"], dn1, name="bwd_x")
    return loss, grad_x, gs


def _to_compute(name, full):
    st = full.reshape(N_CHIPS, full.shape[0] // N_CHIPS, full.shape[1])
    if name in ("ffn1_w_gu", "ffn2_w_gu", "w_ple_proj"):
        return st
    if name == "w_in":
        whole = jnp.transpose(st, (1, 0, 2)).reshape(st.shape[1], -1)
        return whole[:, 0:ZA_W], whole[:, ZA_W:]
    return full


def _to_comm(name, g):
    if name == "w_in":
        whole = jnp.concatenate(g, axis=1)
        g = jnp.transpose(whole.reshape(whole.shape[0], N_CHIPS, -1), (1, 0, 2))
    rs = g.shape[1] if g.ndim == 3 else g.shape[0] // N_CHIPS
    return g.reshape(N_CHIPS, 2, rs // 2, g.shape[-1])


class _Schedule:
    GATHER = {
        "ffn1_gu": [("ffn1_w_down", (0, 1)), ("w_in", (0, 1))],
        "ffn1_down": [("w_out", (0, 1))],
        "band_a_fwd": [("ffn2_w_gu", (0, 2))],
        "band_b0_fwd": [("ffn2_w_gu", (1, 2))],
        "band_b1_fwd": [("ffn2_w_down", (0, 1))],
        "band_b2_fwd": [("w_ple_gate", (0, 1)), ("w_ple_proj", (0, 1))],
    }
    SWAP = {"d_n4": ["w_ple_proj", "w_ple_gate"], "ffn2_d_w_gu": ["ffn2_w_down"], "ffn2_d_n": ["ffn2_w_gu"],
            "d_mix": ["w_out"], "d_n2_b": ["w_in"], "ffn1_d_w_gu": ["ffn1_w_down"], "ffn1_d_n": ["ffn1_w_gu"]}
    SCATTER = {"ffn2_d_a": ["w_ple_proj", "w_ple_gate"], "ffn2_d_n": ["ffn2_w_down"], "band_a_bwd": ["ffn2_w_gu"],
               "band_b0_bwd": ["w_out"], "ffn1_d_a": ["w_in"], "ffn1_d_n": ["ffn1_w_down"]}

    def __init__(self, placed, c_idx, kc_idx):
        self.full = dict(placed)
        self.missing = {n: 1.0 for n in placed}
        self.c_idx, self.kc_idx = c_idx, kc_idx
        self.grads, self.swapped, self.pair_sums, self.scattered = {}, {}, {}, {}

    def _gather(self, entries):
        carries, after = [], []
        for part in sorted({pt for _, pt in entries}):
            names = [n for n, pt in entries if pt == part]
            carry = _gather_carry([self.full[n] for n in names], part)
            carries.append(carry)

            def done(carry=carry, names=names, part=part):
                for n, a in zip(names, carry.results):
                    self.full[n] = a
                    self.missing[n] -= 1.0 / part[1]
            after.append(done)
        return carries, after

    def gather_now(self, names, *, name):
        carries, after = self._gather([(n, (0, 1)) for n in names])
        _comm_call(_join(carries), name=name)
        for fn in after:
            fn()

    def weight(self, name):
        assert abs(self.missing[name]) < 1e-9, (name, self.missing[name])
        return _to_compute(name, self.full[name])

    def grad(self, name, g):
        self.grads[name] = _to_comm(name, g)

    def _swap(self, names):
        carry = _pair_swap_carry([self.grads[n] for n in names])

        def done():
            self.swapped.update(zip(names, carry.results))
        return carry, done

    def _scatter(self, names):
        for n in names:
            self.pair_sums[n] = _pair_sum(self.grads[n], self.swapped[n], self.c_idx, name=f"grad_pair_sum_{n}")
        carry = _chip_scatter_carry([self.pair_sums[n] for n in names])

        def done():
            self.scattered.update(zip(names, carry.results))
        return carry, done

    def run(self, fn, *args, name, **kw):
        carries, after = self._gather(self.GATHER.get(name, []))
        for names, make in ((self.SWAP.get(name), self._swap), (self.SCATTER.get(name), self._scatter)):
            if names:
                carry, done = make(names)
                carries.append(carry)
                after.append(done)
        out = fn(*args, name=name, carry=_join(carries), **kw)
        for done in after:
            done()
        return out

    def finish(self):
        left = [n for n in BIG if n not in self.scattered]
        to_swap = [n for n in left if n not in self.swapped]
        if to_swap:
            carry, done = self._swap(to_swap)
            _comm_call(carry, name="grad_pair_swap")
            done()
        if left:
            carry, done = self._scatter(left)
            _comm_call(carry, name="grad_chip_scatter")
            done()
        halves = [_chip_sum(self.pair_sums[n], self.scattered[n], self.kc_idx, name=f"grad_chip_sum_{n}") for n in BIG]
        carry = _half_swap_carry(halves)
        _comm_call(carry, name="grad_half_swap")
        return dict(zip(BIG, carry.results))


def kernel(x, p, rel_bias, ffn1_pre_g, ffn1_w_gu, ffn1_w_down, ffn1_post_g, attn_pre_g, w_in, b_in, sinks, w_out, b_out, attn_post_g, ffn2_pre_g, ffn2_w_gu, ffn2_w_down, ffn2_post_g, ple_pre_g, w_ple_gate, w_ple_proj, ple_post_g, loss_target, m_rel_bias, m_ffn1_pre_g, m_ffn1_w_gu, m_ffn1_w_down, m_ffn1_post_g, m_attn_pre_g, m_w_in, m_b_in, m_sinks, m_w_out, m_b_out, m_attn_post_g, m_ffn2_pre_g, m_ffn2_w_gu, m_ffn2_w_down, m_ffn2_post_g, m_ple_pre_g, m_w_ple_gate, m_w_ple_proj, m_ple_post_g, v_rel_bias, v_ffn1_pre_g, v_ffn1_w_gu, v_ffn1_w_down, v_ffn1_post_g, v_attn_pre_g, v_w_in, v_b_in, v_sinks, v_w_out, v_b_out, v_attn_post_g, v_ffn2_pre_g, v_ffn2_w_gu, v_ffn2_w_down, v_ffn2_post_g, v_ple_pre_g, v_w_ple_gate, v_w_ple_proj, v_ple_post_g):
    given = dict(locals())
    w = {n: given[n] for n in WEIGHTS}
    m = {n: given["m_" + n] for n in WEIGHTS}
    v = {n: given["v_" + n] for n in WEIGHTS}
    c_pos = lax.axis_index("c").astype(jnp.int32)
    k_pos = (2 * lax.axis_index("x") + lax.axis_index("y")).astype(jnp.int32)
    c_idx, k_idx, kc_idx = c_pos.reshape(1), k_pos.reshape(1), jnp.stack([k_pos, c_pos])

    sched = _Schedule({n: _cast_place(w[n][0], k_idx, name=f"place_{n}") for n in BIG}, c_idx, kc_idx)
    sched.gather_now(["ffn1_w_gu"], name="gather_ffn1_w_gu")
    small = {n: (w[n] if n == "rel_bias" else w[n].reshape(1, -1)) for n in SMALL}

    loss_part, grad_x, gs = _local_step(x[0], p[0, 0], loss_target[0], small, sched)

    grads_big = sched.finish()

    small_shapes = [w[n].shape for n in SMALL]
    packed = _pack([gs[n] for n in SMALL] + [loss_part[:, 0:1]])
    summed = _small_all_reduce(packed, name="small_all_reduce")
    unpacked = _unpack(summed, small_shapes + [(1,)])
    loss = unpacked[-1].reshape(())

    grads, delta, new_m, new_v = {}, {}, {}, {}
    for n in BIG:
        grads[n] = grads_big[n][None]
        d, mn, vn = _adamw(w[n][0], grads_big[n], m[n][0], v[n][0], name=f"adamw_{n}")
        delta[n], new_m[n], new_v[n] = d[None], mn[None], vn[None]
    n_rows = summed.shape[0] - SUBLANES
    d_s, m_s, v_s = _adamw(_pack([w[n] for n in SMALL]), summed[0:n_rows], _pack([m[n] for n in SMALL]),
                           _pack([v[n] for n in SMALL]), name="adamw_small")
    for name, dd, mm, vv, gg in zip(SMALL, _unpack(d_s, small_shapes), _unpack(m_s, small_shapes),
                                    _unpack(v_s, small_shapes), unpacked[:-1]):
        grads[name], delta[name], new_m[name], new_v[name] = gg, dd, mm, vv

    return (loss, grad_x[None], *[grads[n] for n in WEIGHTS], *[delta[n] for n in WEIGHTS],
            *[new_m[n] for n in WEIGHTS], *[new_v[n] for n in WEIGHTS])
```

```python
import math

import jax
import jax.numpy as jnp
import numpy as np
from jax import lax
from jax.experimental import pallas as pl
from jax.experimental.pallas import tpu as pltpu

F32 = jnp.float32
BF16 = jnp.bfloat16
MESH = pl.DeviceIdType.MESH

EPS = 1e-6
NEG = -1e30
LANES = 128
SUBLANES = 8
HEAD_DIM = 64
QBLK = 128
N_PAIRS = 4
MIX_W = N_PAIRS * LANES
A_WINDOW = 128
B_PATTERNS = ((128, 1), (512, 4), (2048, 16))
NUM_BUCKETS = 32
MAX_DISTANCE = 2048
ZA_W = 768
ZB_W = 1536
N_CHIPS = 4
VMEM_BYTES_V7X = 64 * 2**20

ADAM_LR, ADAM_B1, ADAM_B2, ADAM_EPS, ADAM_WD, ADAM_STEP = 0.001, 0.9, 0.999, 1e-08, 0.01, 10

BIG = ("ffn1_w_gu", "ffn1_w_down", "w_in", "w_out", "ffn2_w_gu", "ffn2_w_down", "w_ple_gate", "w_ple_proj")
SMALL = ("rel_bias", "ffn1_pre_g", "ffn1_post_g", "attn_pre_g", "b_in", "sinks", "b_out", "attn_post_g",
         "ffn2_pre_g", "ffn2_post_g", "ple_pre_g", "ple_post_g")
WEIGHTS = ("rel_bias", "ffn1_pre_g", "ffn1_w_gu", "ffn1_w_down", "ffn1_post_g", "attn_pre_g", "w_in", "b_in",
           "sinks", "w_out", "b_out", "attn_post_g", "ffn2_pre_g", "ffn2_w_gu", "ffn2_w_down", "ffn2_post_g",
           "ple_pre_g", "w_ple_gate", "w_ple_proj", "ple_post_g")
PACK_UNIT = SUBLANES * LANES


def _vmem_limit(*block_bytes):
    need = 2 * sum(block_bytes) + max(block_bytes) * 2 + (4 << 20)
    return int(min(max(need, 32 << 20), VMEM_BYTES_V7X - (6 << 20)))


def _nbytes(shape, dtype):
    return int(np.prod(shape)) * jnp.dtype(dtype).itemsize


def _row_block(rows, cap, mult):
    for cand in range(min(rows, cap), 0, -1):
        if rows % cand == 0 and cand % mult == 0:
            return cand
    raise ValueError((rows, cap, mult))


class _Carry:
    def __init__(self, operands, out_shapes, aliases, sems, start, finish):
        self.operands, self.out_shapes, self.aliases, self.sems = list(operands), list(out_shapes), dict(aliases), list(sems)
        self.start, self.finish = start, finish
        self.results = None


def _join(carries):
    carries = [c for c in carries if c is not None]
    if not carries:
        return None
    if len(carries) == 1:
        return carries[0]
    offs, pos = [], [0, 0, 0]
    for c in carries:
        offs.append(tuple(pos))
        pos = [pos[0] + len(c.operands), pos[1] + len(c.out_shapes), pos[2] + len(c.sems)]
    aliases = {}
    for c, (oi, oo, _) in zip(carries, offs):
        aliases.update({oi + i: oo + o for i, o in c.aliases.items()})

    def run(which):
        def fn(ins, outs, sems):
            for c, (oi, oo, os_) in zip(carries, offs):
                getattr(c, which)(ins[oi:oi + len(c.operands)], outs[oo:oo + len(c.out_shapes)],
                                  sems[os_:os_ + len(c.sems)])
        return fn

    joined = _Carry([a for c in carries for a in c.operands], [s for c in carries for s in c.out_shapes], aliases,
                    [s for c in carries for s in c.sems], run("start"), run("finish"))
    joined.parts = (carries, offs)
    return joined


def _set_results(carry, outs):
    carry.results = list(outs)
    if hasattr(carry, "parts"):
        for c, (_, oo, _) in zip(*carry.parts):
            _set_results(c, outs[oo:oo + len(c.out_shapes)])


ANY = pl.BlockSpec(memory_space=pl.ANY)


def _call(body, *, name, grid, in_specs, out_specs, out_shape, args, scratch=(), vmem_limit=None, carry=None,
          semantics=None):
    n_ci, n_co, n_cs = len(args), len(out_shape), len(scratch)
    semantics = semantics or ("parallel",) * len(grid)
    if carry is None:
        res = pl.pallas_call(
            body, name=name, grid=grid, in_specs=list(in_specs), out_specs=tuple(out_specs), out_shape=tuple(out_shape),
            scratch_shapes=list(scratch),
            compiler_params=pltpu.CompilerParams(dimension_semantics=semantics, vmem_limit_bytes=vmem_limit),
        )(*args)
        return list(res)
    n_pi, n_po = len(carry.operands), len(carry.out_shapes)

    def full_body(*refs):
        ci, pi = refs[:n_ci], refs[n_ci:n_ci + n_pi]
        o0 = n_ci + n_pi
        co, po = refs[o0:o0 + n_co], refs[o0 + n_co:o0 + n_co + n_po]
        s0 = o0 + n_co + n_po
        cs, ps = refs[s0:s0 + n_cs], refs[s0 + n_cs:]
        ids = [pl.program_id(ax) for ax in range(len(grid))]
        first, last = ids[0] == 0, ids[0] == grid[0] - 1
        for ax in range(1, len(grid)):
            first, last = first & (ids[ax] == 0), last & (ids[ax] == grid[ax] - 1)

        @pl.when(first)
        def _():
            carry.start(pi, po, ps)

        body(*ci, *co, *cs)

        @pl.when(last)
        def _():
            carry.finish(pi, po, ps)

    res = pl.pallas_call(
        full_body, name=name, grid=grid, in_specs=list(in_specs) + [ANY] * n_pi,
        out_specs=tuple(out_specs) + tuple([ANY] * n_po), out_shape=tuple(out_shape) + tuple(carry.out_shapes),
        scratch_shapes=list(scratch) + carry.sems,
        input_output_aliases={n_ci + i: n_co + o for i, o in carry.aliases.items()},
        compiler_params=pltpu.CompilerParams(dimension_semantics=("arbitrary",) * len(grid),
                                             vmem_limit_bytes=vmem_limit),
    )(*args, *carry.operands)
    _set_results(carry, res[n_co:])
    return list(res[:n_co])


def _comm_call(carry, *, name):
    n_pi, n_po = len(carry.operands), len(carry.out_shapes)

    def body(*refs):
        pi, po, ps = refs[:n_pi], refs[n_pi:n_pi + n_po], refs[n_pi + n_po:]
        carry.start(pi, po, ps)
        carry.finish(pi, po, ps)

    res = pl.pallas_call(
        body, name=name, in_specs=[ANY] * n_pi, out_specs=tuple([ANY] * n_po), out_shape=tuple(carry.out_shapes),
        scratch_shapes=carry.sems, input_output_aliases=dict(carry.aliases),
    )(*carry.operands)
    _set_results(carry, res)


def _mm_nn(a, b, *, bm, bn, out_dtype, name, bias=None, carry=None):
    M, K = a.shape
    stacked = b.ndim == 3
    N = b.shape[0] * b.shape[2] if stacked else b.shape[1]
    assert M % bm == 0 and N % bn == 0 and (not stacked or bn == b.shape[2])

    def body(*refs):
        a_ref, b_ref = refs[0], refs[1]
        o_ref = refs[-1]
        acc = jnp.dot(a_ref[...], b_ref[...], preferred_element_type=F32)
        if bias is not None:
            acc = acc + refs[2][...]
        o_ref[...] = acc.astype(o_ref.dtype)

    if stacked:
        b_spec = pl.BlockSpec((None, K, bn), lambda i, j: (j, 0, 0))
    else:
        b_spec = pl.BlockSpec((K, bn), lambda i, j: (0, j))
    in_specs = [pl.BlockSpec((bm, K), lambda i, j: (i, 0)), b_spec]
    args = [a, b]
    if bias is not None:
        in_specs.append(pl.BlockSpec((1, bn), lambda i, j: (0, j)))
        args.append(bias)
    limit = _vmem_limit(_nbytes((bm, K), a.dtype), _nbytes((K, bn), b.dtype), _nbytes((bm, bn), F32))
    return _call(body, name=name, grid=(M // bm, N // bn), in_specs=in_specs,
                 out_specs=[pl.BlockSpec((bm, bn), lambda i, j: (i, j))],
                 out_shape=[jax.ShapeDtypeStruct((M, N), out_dtype)], args=args, vmem_limit=limit, carry=carry)[0]


def _mm_nt(a, b, *, bm, bn, out_dtype, name, add=None, carry=None):
    M, K = a.shape
    stacked = b.ndim == 3
    N = b.shape[1] if stacked else b.shape[0]
    n_sh = b.shape[0] if stacked else 1
    ks = b.shape[2] if stacked else K
    assert M % bm == 0 and N % bn == 0 and n_sh * ks == K
    dn = (((1,), (1,)), ((), ()))

    def body(*refs):
        a_ref, b_ref = refs[0], refs[1]
        o_ref = refs[-1]
        if stacked:
            acc = lax.dot_general(a_ref[:, 0:ks], b_ref[0], dn, preferred_element_type=F32)
            for s in range(1, n_sh):
                acc = acc + lax.dot_general(a_ref[:, s * ks:(s + 1) * ks], b_ref[s], dn, preferred_element_type=F32)
        else:
            acc = lax.dot_general(a_ref[...], b_ref[...], dn, preferred_element_type=F32)
        if add is not None:
            acc = acc + refs[2][...]
        o_ref[...] = acc.astype(o_ref.dtype)

    if stacked:
        b_spec = pl.BlockSpec((n_sh, bn, ks), lambda i, j: (0, j, 0))
    else:
        b_spec = pl.BlockSpec((bn, K), lambda i, j: (j, 0))
    in_specs = [pl.BlockSpec((bm, K), lambda i, j: (i, 0)), b_spec]
    args = [a, b]
    if add is not None:
        in_specs.append(pl.BlockSpec((bm, bn), lambda i, j: (i, j)))
        args.append(add)
    limit = _vmem_limit(_nbytes((bm, K), a.dtype), _nbytes((bn, K), b.dtype), _nbytes((bm, bn), F32))
    return _call(body, name=name, grid=(M // bm, N // bn), in_specs=in_specs,
                 out_specs=[pl.BlockSpec((bm, bn), lambda i, j: (i, j))],
                 out_shape=[jax.ShapeDtypeStruct((M, N), out_dtype)], args=args, vmem_limit=limit, carry=carry)[0]


def _mm_tn(a, b, *, bm, bn, out_dtype, name, n_stack=None, carry=None):
    T, M = a.shape
    N = b.shape[1]
    assert M % bm == 0 and N % bn == 0 and (n_stack is None or bn * n_stack == N)
    dn = (((0,), (0,)), ((), ()))

    def body(a_ref, b_ref, o_ref):
        acc = lax.dot_general(a_ref[...], b_ref[...], dn, preferred_element_type=F32)
        o_ref[...] = acc.astype(o_ref.dtype)

    if n_stack is None:
        out_spec = pl.BlockSpec((bm, bn), lambda i, j: (i, j))
        out_shape = jax.ShapeDtypeStruct((M, N), out_dtype)
    else:
        out_spec = pl.BlockSpec((None, bm, bn), lambda i, j: (j, i, 0))
        out_shape = jax.ShapeDtypeStruct((n_stack, M, bn), out_dtype)
    limit = _vmem_limit(_nbytes((T, bm), a.dtype), _nbytes((T, bn), b.dtype), _nbytes((bm, bn), F32))
    return _call(body, name=name, grid=(M // bm, N // bn),
                 in_specs=[pl.BlockSpec((T, bm), lambda i, j: (0, i)), pl.BlockSpec((T, bn), lambda i, j: (0, j))],
                 out_specs=[out_spec], out_shape=[out_shape], args=[a, b], vmem_limit=limit, carry=carry)[0]


ROW_BLK = 256


def _rstd(x):
    return lax.rsqrt(jnp.mean(x * x, axis=-1, keepdims=True) + EPS)


def _norm_bwd(xhat, rstd, dxhat):
    return rstd * (dxhat - xhat * jnp.mean(dxhat * xhat, axis=-1, keepdims=True))


def _row_spec(cols):
    return pl.BlockSpec((ROW_BLK, cols), lambda i: (i, 0))


def _vec_spec(cols):
    return pl.BlockSpec((1, cols), lambda i: (0, 0))


def _rms_fwd(x, g, *, name):
    T, D = x.shape

    def body(x_ref, g_ref, o_ref):
        xv = x_ref[...]
        o_ref[...] = (xv * _rstd(xv) * g_ref[...]).astype(o_ref.dtype)

    return pl.pallas_call(
        body, name=name, grid=(T // ROW_BLK,), in_specs=[_row_spec(D), _vec_spec(D)], out_specs=_row_spec(D),
        out_shape=jax.ShapeDtypeStruct((T, D), BF16),
        compiler_params=pltpu.CompilerParams(dimension_semantics=("parallel",)),
    )(x, g)


def _mm_resid_norm(a, b, h, g_post, coef, g_next, *, bm, name, bias=None, carry=None):
    M, K = a.shape
    N = b.shape[1]

    def body(*refs):
        a_ref, b_ref, h_ref, gp_ref, gn_ref = refs[0:5]
        f_ref, hn_ref, n_ref = refs[-3:]
        f = jnp.dot(a_ref[...], b_ref[...], preferred_element_type=F32)
        if bias is not None:
            f = f + refs[5][...]
        f_ref[...] = f
        hn = h_ref[...] + coef * (f * _rstd(f) * gp_ref[...])
        hn_ref[...] = hn
        n_ref[...] = (hn * _rstd(hn) * gn_ref[...]).astype(n_ref.dtype)

    row = lambda w: pl.BlockSpec((bm, w), lambda i: (i, 0))
    vec = pl.BlockSpec((1, N), lambda i: (0, 0))
    in_specs = [row(K), pl.BlockSpec((K, N), lambda i: (0, 0), pipeline_mode=pl.Buffered(1)), row(N), vec, vec]
    args = [a, b, h, g_post, g_next]
    if bias is not None:
        in_specs.append(vec)
        args.append(bias)
    limit = _vmem_limit(_nbytes((bm, K), a.dtype), _nbytes((K, N), b.dtype) // 2, 4 * _nbytes((bm, N), F32))
    return _call(body, name=name, grid=(M // bm,), in_specs=in_specs, out_specs=[row(N), row(N), row(N)],
                 out_shape=[jax.ShapeDtypeStruct((M, N), F32), jax.ShapeDtypeStruct((M, N), F32),
                            jax.ShapeDtypeStruct((M, N), BF16)], args=args, vmem_limit=limit, carry=carry)


def _mm_nt_norms_bwd(a, b, dh_in, h, g_pre, f, g_post, coef, *, bm, name, add=None, carry=None):
    M, K = a.shape
    stacked = b.ndim == 3
    N = b.shape[1] if stacked else b.shape[0]
    n_sh = b.shape[0] if stacked else 1
    ks = b.shape[2] if stacked else K
    assert n_sh * ks == K
    dn_dims = (((1,), (1,)), ((), ()))
    with_f = f is not None
    n_in = 5 + (2 if with_f else 0) + (1 if add is not None else 0)

    def body(*refs):
        a_ref, b_ref, dhi_ref, h_ref, gpre_ref = refs[0:5]
        outs = refs[n_in:]
        if stacked:
            dn = lax.dot_general(a_ref[:, 0:ks], b_ref[0], dn_dims, preferred_element_type=F32)
            for s in range(1, n_sh):
                dn = dn + lax.dot_general(a_ref[:, s * ks:(s + 1) * ks], b_ref[s], dn_dims, preferred_element_type=F32)
        else:
            dn = lax.dot_general(a_ref[...], b_ref[...], dn_dims, preferred_element_type=F32)
        if add is not None:
            dn = dn + refs[n_in - 1][...]

        @pl.when(pl.program_id(0) == 0)
        def _():
            for acc in (outs[2:] if with_f else outs[1:]):
                acc[...] = jnp.zeros_like(acc)

        hv = h_ref[...]
        r = _rstd(hv)
        hh = hv * r
        dh = dhi_ref[...] + _norm_bwd(hh, r, dn * gpre_ref[...])
        outs[0][...] = dh
        if not with_f:
            outs[1][...] += jnp.sum(dn * hh, axis=0, keepdims=True)
            return
        f_ref, gpost_ref = refs[5], refs[6]
        df_ref, dgpre_ref, dgpost_ref, dsum_ref = outs[1:]
        dgpre_ref[...] += jnp.sum(dn * hh, axis=0, keepdims=True)
        fv = f_ref[...]
        rf = _rstd(fv)
        fh = fv * rf
        dy = coef * dh
        dgpost_ref[...] += jnp.sum(dy * fh, axis=0, keepdims=True)
        df = _norm_bwd(fh, rf, dy * gpost_ref[...])
        dsum_ref[...] += jnp.sum(df, axis=0, keepdims=True)
        df_ref[...] = df.astype(df_ref.dtype)

    row = lambda w: pl.BlockSpec((bm, w), lambda i: (i, 0))
    vec = pl.BlockSpec((1, N), lambda i: (0, 0))
    once = pl.Buffered(1)
    if stacked:
        b_spec = pl.BlockSpec((n_sh, N, ks), lambda i: (0, 0, 0), pipeline_mode=once)
    else:
        b_spec = pl.BlockSpec((N, K), lambda i: (0, 0), pipeline_mode=once)
    in_specs = [row(K), b_spec, row(N), row(N), vec]
    args = [a, b, dh_in, h, g_pre]
    if with_f:
        in_specs += [row(N), vec]
        args += [f, g_post]
    if add is not None:
        in_specs.append(row(N))
        args.append(add)
    vec_shape = jax.ShapeDtypeStruct((1, N), F32)
    out_specs = [row(N)] + ([row(N), vec, vec, vec] if with_f else [vec])
    out_shape = [jax.ShapeDtypeStruct((M, N), F32)]
    out_shape += [jax.ShapeDtypeStruct((M, N), BF16), vec_shape, vec_shape, vec_shape] if with_f else [vec_shape]
    limit = _vmem_limit(_nbytes((bm, K), a.dtype), _nbytes((N, K), b.dtype) // 2, 6 * _nbytes((bm, N), F32))
    return _call(body, name=name, grid=(M // bm,), in_specs=in_specs, out_specs=out_specs, out_shape=out_shape,
                 args=args, vmem_limit=limit, semantics=("arbitrary",), carry=carry)


def _ffn_up(n, w_gu, *, bm, name, carry=None):
    M, K = n.shape
    n_sh, _, ns = w_gu.shape
    half = n_sh // 2

    def body(n_ref, wg_ref, wu_ref, g_ref, u_ref, a_ref):
        nv = n_ref[...]
        g = jnp.dot(nv, wg_ref[...], preferred_element_type=F32)
        u = jnp.dot(nv, wu_ref[...], preferred_element_type=F32)
        g_ref[...] = g.astype(g_ref.dtype)
        u_ref[...] = u.astype(u_ref.dtype)
        a_ref[...] = (g * jax.nn.sigmoid(g) * u).astype(a_ref.dtype)

    out_spec = pl.BlockSpec((bm, ns), lambda i, j: (i, j))
    out = jax.ShapeDtypeStruct((M, half * ns), BF16)
    limit = _vmem_limit(_nbytes((bm, K), n.dtype), 2 * _nbytes((K, ns), w_gu.dtype), 3 * _nbytes((bm, ns), F32))
    return _call(body, name=name, grid=(M // bm, half),
                 in_specs=[pl.BlockSpec((bm, K), lambda i, j: (i, 0)),
                           pl.BlockSpec((None, K, ns), lambda i, j: (j, 0, 0)),
                           pl.BlockSpec((None, K, ns), lambda i, j: (j + half, 0, 0))],
                 out_specs=[out_spec, out_spec, out_spec], out_shape=[out, out, out], args=[n, w_gu, w_gu],
                 vmem_limit=limit, carry=carry)


def _ffn_down_bwd(df, w_down, g, u, *, bm, name, carry=None):
    M, D = df.shape
    F = w_down.shape[0]
    dn = (((1,), (1,)), ((), ()))

    def body(df_ref, w_ref, g_ref, u_ref, o_ref):
        da = lax.dot_general(df_ref[...], w_ref[...], dn, preferred_element_type=F32)
        gv = g_ref[...].astype(F32)
        s = jax.nn.sigmoid(gv)
        o_ref[:, 0:F] = (da * u_ref[...].astype(F32) * (s * (1.0 + gv * (1.0 - s)))).astype(o_ref.dtype)
        o_ref[:, F:2 * F] = (da * (gv * s)).astype(o_ref.dtype)

    row = lambda w: pl.BlockSpec((bm, w), lambda i: (i, 0))
    limit = _vmem_limit(_nbytes((F, D), w_down.dtype), 2 * _nbytes((bm, F), BF16), _nbytes((bm, 2 * F), BF16),
                        2 * _nbytes((bm, F), F32))
    return _call(body, name=name, grid=(M // bm,),
                 in_specs=[row(D), pl.BlockSpec((F, D), lambda i: (0, 0)), row(F), row(F)],
                 out_specs=[row(2 * F)], out_shape=[jax.ShapeDtypeStruct((M, 2 * F), BF16)],
                 args=[df, w_down, g, u], vmem_limit=limit, carry=carry)[0]


def _ple_loss(h3, gp, e, g_post, target, *, name):
    T, D = h3.shape

    def body(h_ref, gp_ref, e_ref, g_ref, t_ref, dh_ref, de_ref, dgp_ref, loss_ref, dg_ref):
        @pl.when(pl.program_id(0) == 0)
        def _():
            loss_ref[...] = jnp.zeros_like(loss_ref)
            dg_ref[...] = jnp.zeros_like(dg_ref)

        gate = jax.nn.sigmoid(gp_ref[...])
        ev = e_ref[...]
        ge = gate * ev
        r = _rstd(ge)
        gh = ge * r
        gpost = g_ref[...]
        diff = h_ref[...] + gh * gpost - t_ref[...]
        loss_ref[...] += jnp.sum(diff * diff) * (0.5 / D)
        dh = diff * (1.0 / D)
        dh_ref[...] = dh
        dg_ref[...] += jnp.sum(dh * gh, axis=0, keepdims=True)
        dge = _norm_bwd(gh, r, dh * gpost)
        de_ref[...] = (dge * gate).astype(de_ref.dtype)
        dgp_ref[...] = (dge * ev * gate * (1.0 - gate)).astype(dgp_ref.dtype)

    return pl.pallas_call(
        body, name=name, grid=(T // ROW_BLK,),
        in_specs=[_row_spec(D), _row_spec(D), _row_spec(D), _vec_spec(D), _row_spec(D)],
        out_specs=(_row_spec(D), _row_spec(D), _row_spec(D), _vec_spec(LANES), _vec_spec(D)),
        out_shape=(jax.ShapeDtypeStruct((T, D), F32), jax.ShapeDtypeStruct((T, D), BF16),
                   jax.ShapeDtypeStruct((T, D), BF16), jax.ShapeDtypeStruct((1, LANES), F32),
                   jax.ShapeDtypeStruct((1, D), F32)),
        compiler_params=pltpu.CompilerParams(dimension_semantics=("arbitrary",)),
    )(h3, gp, e, g_post, target)


def _t5_index(max_dist, stride):
    rho = np.arange(QBLK)[:, None]
    kap = np.arange(2 * QBLK)[None, :]
    d = rho + QBLK - kap
    valid = (d >= 0) & (d <= max_dist)
    n = np.maximum(d, 0) * stride
    max_exact = NUM_BUCKETS // 2
    nf = np.maximum(n, 1).astype(np.float32)
    large = max_exact + (np.log(nf / np.float32(max_exact)) / np.float32(math.log(MAX_DISTANCE / max_exact))
                         * np.float32(NUM_BUCKETS - max_exact)).astype(np.int32)
    large = np.minimum(large, NUM_BUCKETS - 1)
    bucket = np.where(n < max_exact, n, large)
    return np.where(valid, bucket, -1).astype(np.int32)


def _bias_tiles(rel_bias, idx, head_off, *, name):
    def body(rb_ref, idx_ref, o_ref):
        h = pl.program_id(0)
        idxv = idx_ref[...]
        acc = jnp.where(idxv < 0, NEG, 0.0).astype(F32)
        for b in range(NUM_BUCKETS):
            acc = acc + jnp.where(idxv == b, rb_ref[b, head_off + h], 0.0)
        o_ref[0, 0] = acc
        kap = lax.broadcasted_iota(jnp.int32, (1, 2 * QBLK), 1)
        o_ref[1, 0] = jnp.where(kap < QBLK, NEG, acc)

    return pl.pallas_call(
        body, name=name, grid=(2 * N_PAIRS,),
        in_specs=[pl.BlockSpec(memory_space=pltpu.SMEM), pl.BlockSpec((QBLK, 2 * QBLK), lambda h: (0, 0))],
        out_specs=pl.BlockSpec((2, 1, QBLK, 2 * QBLK), lambda h: (0, h, 0, 0)),
        out_shape=jax.ShapeDtypeStruct((2, 2 * N_PAIRS, QBLK, 2 * QBLK), F32),
        compiler_params=pltpu.CompilerParams(dimension_semantics=("parallel",)),
    )(rel_bias, idx)


def _bias_grad(d_bias, idx, *, name):
    def body(db_ref, idx_ref, o_ref):
        h = pl.program_id(0)

        @pl.when(h == 0)
        def _():
            o_ref[...] = jnp.zeros_like(o_ref)

        idxv = idx_ref[...]
        dv = db_ref[0]
        rows = lax.broadcasted_iota(jnp.int32, (NUM_BUCKETS, LANES), 0)
        lanes = lax.broadcasted_iota(jnp.int32, (NUM_BUCKETS, LANES), 1)
        acc = o_ref[...]
        for b in range(NUM_BUCKETS):
            s = jnp.sum(jnp.where(idxv == b, dv, 0.0))
            acc = acc + jnp.where((rows == b) & (lanes == h), s, 0.0)
        o_ref[...] = acc

    return pl.pallas_call(
        body, name=name, grid=(2 * N_PAIRS,),
        in_specs=[pl.BlockSpec((1, QBLK, 2 * QBLK), lambda h: (h, 0, 0)),
                  pl.BlockSpec((QBLK, 2 * QBLK), lambda h: (0, 0))],
        out_specs=pl.BlockSpec((NUM_BUCKETS, LANES), lambda h: (0, 0)),
        out_shape=jax.ShapeDtypeStruct((NUM_BUCKETS, LANES), F32),
        compiler_params=pltpu.CompilerParams(dimension_semantics=("arbitrary",)),
    )(d_bias, idx)


def _lane():
    return lax.broadcasted_iota(jnp.int32, (1, LANES), 1)


def _head_sel(h):
    return (_lane() < HEAD_DIM) if h == 0 else (_lane() >= HEAD_DIM)


def _stat_lo():
    return (_lane() % HEAD_DIM) < (HEAD_DIM // 2)


def _stack_heads(t, scale=None):
    if scale is not None:
        t = t * scale
    zero = jnp.zeros_like(t)
    return jnp.concatenate([jnp.where(_head_sel(0), t, zero), jnp.where(_head_sel(1), t, zero)], axis=0)


def _class_spec(L, r, cols, stride, pps):
    chunks = N_PAIRS // pps
    mode = pl.Buffered(1) if r * chunks == 1 else None
    return pl.BlockSpec((L, MIX_W // chunks), lambda j, c: (0, (cols + j * stride) * chunks + c), pipeline_mode=mode)


def _rows(b, n_blocks=1):
    return pl.ds(pl.multiple_of(b * QBLK, QBLK), n_blocks * QBLK)


def _key_window(ref, b, cols, first):
    if first:
        blk = ref[0:QBLK, cols]
        return jnp.concatenate([blk, blk], axis=0)
    return ref[_rows(b - 1, 2), cols]


def _band_fwd(qa, ka, va, bias, *, r, q_col, k_col, v_col, stride, name, carry=None):
    L = qa.shape[0]
    nb = L // QBLK
    dn = (((1,), (1,)), ((), ()))
    scale = HEAD_DIM ** -0.5

    pps = N_PAIRS

    def body(q_ref, k_ref, v_ref, bias_ref, o_ref, lse_ref):
        sel0 = _head_sel(0)
        pair0 = pl.program_id(1) * pps

        def block(b, first):
            rows = _rows(b)
            for i in range(pps):
                cols = slice(i * LANES, (i + 1) * LANES)
                q2 = _stack_heads(q_ref[rows, cols], scale)
                k = _key_window(k_ref, b, cols, first)
                v = _key_window(v_ref, b, cols, first)
                bias2 = bias_ref[1 if first else 0, pl.ds(2 * (pair0 + i), 2)].reshape(2 * QBLK, 2 * QBLK)
                s = lax.dot_general(q2, k, dn, preferred_element_type=F32) + bias2
                m = jnp.max(s, axis=1, keepdims=True)
                p = jnp.exp(s - m)
                l = jnp.sum(p, axis=1, keepdims=True)
                o = jnp.dot(p.astype(v.dtype), v, preferred_element_type=F32) / l
                lse = m + jnp.log(l)
                o_ref[rows, cols] = jnp.where(sel0, o[0:QBLK], o[QBLK:]).astype(o_ref.dtype)
                lse_ref[rows, cols] = jnp.where(sel0, lse[0:QBLK], lse[QBLK:])

        block(0, True)
        if nb > 1:
            pl.loop(1, nb)(lambda b: block(b, False))

    bias_spec = pl.BlockSpec((2, 2 * N_PAIRS, QBLK, 2 * QBLK), lambda j, c: (0, 0, 0, 0))
    out_spec = _class_spec(L, r, 0, 1, pps)
    blk_bytes = _nbytes((L, pps * LANES), F32)
    return _call(
        body, name=name, grid=(r, N_PAIRS // pps),
        in_specs=[_class_spec(L, r, q_col, stride, pps), _class_spec(L, r, k_col, stride, pps),
                  _class_spec(L, r, v_col, stride, pps), bias_spec],
        out_specs=[out_spec, out_spec],
        out_shape=[jax.ShapeDtypeStruct((L, r * MIX_W), BF16), jax.ShapeDtypeStruct((L, r * MIX_W), F32)],
        args=[qa, ka, va, bias], vmem_limit=_vmem_limit(*([blk_bytes] * 4)), carry=carry)


def _merge(branches, sink, *, name):
    T, W = branches[0][0].shape
    nbr = len(branches)

    def body(*refs):
        sink_ref = refs[2 * nbr]
        out_ref, outb_ref, lse_ref = refs[2 * nbr + 1:]
        sk = sink_ref[...]
        mx = sk
        for t in range(nbr):
            mx = jnp.maximum(mx, refs[2 * t + 1][...])
        den = jnp.exp(sk - mx)
        num = jnp.zeros_like(mx)
        for t in range(nbr):
            w = jnp.exp(refs[2 * t + 1][...] - mx)
            den = den + w
            num = num + w * refs[2 * t][...].astype(F32)
        out = num / den
        out_ref[...] = out
        outb_ref[...] = out.astype(outb_ref.dtype)
        lse_ref[...] = mx + jnp.log(den)

    args = [a for br in branches for a in br] + [sink]
    return pl.pallas_call(
        body, name=name, grid=(T // ROW_BLK,),
        in_specs=[_row_spec(W)] * (2 * nbr) + [_vec_spec(W)],
        out_specs=(_row_spec(W), _row_spec(W), _row_spec(W)),
        out_shape=(jax.ShapeDtypeStruct((T, W), F32), jax.ShapeDtypeStruct((T, W), BF16),
                   jax.ShapeDtypeStruct((T, W), F32)),
        compiler_params=pltpu.CompilerParams(dimension_semantics=("parallel",)),
    )(*args)


def _attn_delta(dmix, outs, lses, sink, *, name):
    T = dmix.shape[0]
    W = MIX_W

    def body(dmix_ref, oa_ref, ob_ref, la_ref, lb_ref, sink_ref, doa_ref, dob_ref, sta_ref, stb_ref, dsink_ref):
        @pl.when(pl.program_id(0) == 0)
        def _():
            dsink_ref[...] = jnp.zeros_like(dsink_ref)

        sel0, lo = _head_sel(0), _stat_lo()
        for mixer, (o_ref, l_ref, do_ref, st_ref) in enumerate(
                ((oa_ref, la_ref, doa_ref, sta_ref), (ob_ref, lb_ref, dob_ref, stb_ref))):
            for i in range(N_PAIRS):
                cols = slice(i * LANES, (i + 1) * LANES)
                do = dmix_ref[:, mixer * W + i * LANES:mixer * W + (i + 1) * LANES]
                prod = do * o_ref[:, cols]
                d0 = jnp.sum(jnp.where(sel0, prod, 0.0), axis=1, keepdims=True)
                d1 = jnp.sum(jnp.where(sel0, 0.0, prod), axis=1, keepdims=True)
                delta = jnp.where(sel0, d0, d1)
                lse = l_ref[:, cols]
                st_ref[:, cols] = jnp.where(lo, lse, delta)
                do_ref[:, cols] = do.astype(do_ref.dtype)
                if mixer == 0:
                    dsink_ref[:, cols] += -jnp.sum(jnp.exp(sink_ref[:, cols] - lse) * delta, axis=0, keepdims=True)

    return pl.pallas_call(
        body, name=name, grid=(T // ROW_BLK,),
        in_specs=[_row_spec(2 * W), _row_spec(W), _row_spec(W), _row_spec(W), _row_spec(W), _vec_spec(W)],
        out_specs=(_row_spec(W), _row_spec(W), _row_spec(W), _row_spec(W), _vec_spec(W)),
        out_shape=(jax.ShapeDtypeStruct((T, W), BF16), jax.ShapeDtypeStruct((T, W), BF16),
                   jax.ShapeDtypeStruct((T, W), F32), jax.ShapeDtypeStruct((T, W), F32),
                   jax.ShapeDtypeStruct((1, W), F32)),
        compiler_params=pltpu.CompilerParams(dimension_semantics=("arbitrary",)),
    )(dmix, outs[0], outs[1], lses[0], lses[1], sink)


def _band_bwd(qa, ka, va, d_out, stat, bias, *, r, q_col, k_col, v_col, stride, name, carry=None):
    L = qa.shape[0]
    nb = L // QBLK
    dn_nt = (((1,), (1,)), ((), ()))
    dn_tn = (((0,), (0,)), ((), ()))
    scale = HEAD_DIM ** -0.5

    pps = N_PAIRS if r > 1 else N_PAIRS // 2

    def body(q_ref, k_ref, v_ref, do_ref, st_ref, bias_ref, dq_ref, dk_ref, dv_ref, db_ref, dk_carry, dv_carry):
        @pl.when((pl.program_id(0) == 0) & (pl.program_id(1) == 0))
        def _():
            db_ref[...] = jnp.zeros_like(db_ref)

        lo, sel0 = _stat_lo(), _head_sel(0)
        pair0 = pl.program_id(1) * pps

        def block(b, first):
            rows = _rows(b)
            for i in range(pps):
                heads = pl.ds(2 * (pair0 + i), 2)
                cols = slice(i * LANES, (i + 1) * LANES)
                q2 = _stack_heads(q_ref[rows, cols], scale)
                k = _key_window(k_ref, b, cols, first)
                v = _key_window(v_ref, b, cols, first)
                do2 = _stack_heads(do_ref[rows, cols])
                st = st_ref[rows, cols]
                lse2 = jnp.concatenate(
                    [jnp.max(jnp.where(_head_sel(h) & lo, st, -jnp.inf), axis=1, keepdims=True) for h in range(2)], axis=0)
                delta2 = jnp.concatenate(
                    [jnp.sum(jnp.where(_head_sel(h) & ~lo, st, 0.0), axis=1, keepdims=True) for h in range(2)],
                    axis=0) * (2.0 / HEAD_DIM)
                bias2 = bias_ref[1 if first else 0, heads].reshape(2 * QBLK, 2 * QBLK)
                s = lax.dot_general(q2, k, dn_nt, preferred_element_type=F32) + bias2
                p = jnp.exp(s - lse2)
                dp = lax.dot_general(do2, v, dn_nt, preferred_element_type=F32)
                ds = p * (dp - delta2)
                db_ref[heads] += ds.reshape(2, QBLK, 2 * QBLK)
                dsb = ds.astype(k.dtype)
                dq2 = jnp.dot(dsb, k, preferred_element_type=F32)
                dq_ref[rows, cols] = (jnp.where(sel0, dq2[0:QBLK], dq2[QBLK:]) * scale).astype(dq_ref.dtype)
                dk_acc = lax.dot_general(dsb, q2, dn_tn, preferred_element_type=F32)
                dv_acc = lax.dot_general(p.astype(k.dtype), do2, dn_tn, preferred_element_type=F32)
                if not first:
                    prev = _rows(b - 1)
                    dk_ref[prev, cols] = (dk_carry[:, cols] + dk_acc[0:QBLK]).astype(dk_ref.dtype)
                    dv_ref[prev, cols] = (dv_carry[:, cols] + dv_acc[0:QBLK]).astype(dv_ref.dtype)
                dk_carry[:, cols] = dk_acc[QBLK:2 * QBLK]
                dv_carry[:, cols] = dv_acc[QBLK:2 * QBLK]

        block(0, True)
        if nb > 1:
            pl.loop(1, nb)(lambda b: block(b, False))

        last = _rows(nb - 1)
        dk_ref[last, :] = dk_carry[...].astype(dk_ref.dtype)
        dv_ref[last, :] = dv_carry[...].astype(dv_ref.dtype)

    tok_spec = _class_spec(L, r, 0, 1, pps)
    bias_spec = pl.BlockSpec((2, 2 * N_PAIRS, QBLK, 2 * QBLK), lambda j, c: (0, 0, 0, 0))
    dbias_spec = pl.BlockSpec((2 * N_PAIRS, QBLK, 2 * QBLK), lambda j, c: (0, 0, 0))
    grad = jax.ShapeDtypeStruct((L, r * MIX_W), BF16)
    blk_bytes = _nbytes((L, pps * LANES), BF16)
    carry_shape = pltpu.VMEM((QBLK, pps * LANES), F32)
    return _call(
        body, name=name, grid=(r, N_PAIRS // pps),
        in_specs=[_class_spec(L, r, q_col, stride, pps), _class_spec(L, r, k_col, stride, pps),
                  _class_spec(L, r, v_col, stride, pps), tok_spec, tok_spec, bias_spec],
        out_specs=[tok_spec, tok_spec, tok_spec, dbias_spec],
        out_shape=[grad, grad, grad, jax.ShapeDtypeStruct((2 * N_PAIRS, QBLK, 2 * QBLK), F32)],
        args=[qa, ka, va, d_out, stat, bias], scratch=[carry_shape, carry_shape],
        vmem_limit=_vmem_limit(*([blk_bytes] * 9)), semantics=("arbitrary", "arbitrary"), carry=carry)


def _dz_assemble(dq_a, dk_a, dv_a, b_grads, *, name):
    T = dq_a.shape[0]
    W = MIX_W

    def group_sum(x):
        u0 = x[:, 0:LANES] + x[:, LANES:2 * LANES]
        u1 = x[:, 2 * LANES:3 * LANES] + x[:, 3 * LANES:4 * LANES]
        s0 = u0 + pltpu.roll(u0, HEAD_DIM, 1)
        s1 = u1 + pltpu.roll(u1, HEAD_DIM, 1)
        return jnp.where(_head_sel(0), s0, s1)

    def body(*refs):
        dqa_ref, dka_ref, dva_ref = refs[0:3]
        b_refs = refs[3:12]
        dza_ref, dzb_ref, sa_ref, sb_ref = refs[12:]

        @pl.when(pl.program_id(0) == 0)
        def _():
            sa_ref[...] = jnp.zeros_like(sa_ref)
            sb_ref[...] = jnp.zeros_like(sb_ref)

        def put(dst, acc, col, part):
            w = part.shape[1]
            dst[:, col:col + w] = part.astype(dst.dtype)
            acc[:, col:col + w] += jnp.sum(part, axis=0, keepdims=True)

        put(dza_ref, sa_ref, 0, dqa_ref[...].astype(F32))
        put(dza_ref, sa_ref, W, group_sum(dka_ref[...].astype(F32)))
        put(dza_ref, sa_ref, W + LANES, group_sum(dva_ref[...].astype(F32)))
        for t in range(3):
            part = (b_refs[t][...].astype(F32) + b_refs[3 + t][...].astype(F32) + b_refs[6 + t][...].astype(F32))
            put(dzb_ref, sb_ref, t * W, part)

    args = [dq_a, dk_a, dv_a] + [g[t] for g in b_grads for t in range(3)]
    return pl.pallas_call(
        body, name=name, grid=(T // ROW_BLK,),
        in_specs=[_row_spec(W)] * 12,
        out_specs=(_row_spec(ZA_W), _row_spec(ZB_W), _vec_spec(ZA_W), _vec_spec(ZB_W)),
        out_shape=(jax.ShapeDtypeStruct((T, ZA_W), BF16), jax.ShapeDtypeStruct((T, ZB_W), BF16),
                   jax.ShapeDtypeStruct((1, ZA_W), F32), jax.ShapeDtypeStruct((1, ZB_W), F32)),
        compiler_params=pltpu.CompilerParams(dimension_semantics=("arbitrary",)),
    )(*args)


def _place():
    x, y, c = lax.axis_index("x"), lax.axis_index("y"), lax.axis_index("c")
    chips = [(1 - x, y), (x, 1 - y), (1 - x, 1 - y)]
    return x, y, c, chips


def _cast_place(shard, k_idx, *, name):
    rs, cs = shard.shape
    rb = _row_block(rs, 256, 16)
    nblk = rs // rb

    def body(k_ref, s_ref, o_ref):
        o_ref[...] = s_ref[...].astype(o_ref.dtype)

    grid_spec = pltpu.PrefetchScalarGridSpec(
        num_scalar_prefetch=1, grid=(nblk,),
        in_specs=[pl.BlockSpec((rb, cs), lambda t, k_ref: (t, 0))],
        out_specs=pl.BlockSpec((rb, cs), lambda t, k_ref: (k_ref[0] * nblk + t, 0)))
    return pl.pallas_call(
        body, name=name, grid_spec=grid_spec, out_shape=jax.ShapeDtypeStruct((N_CHIPS * rs, cs), BF16),
        compiler_params=pltpu.CompilerParams(dimension_semantics=("parallel",)),
    )(k_idx, shard)


def _gather_carry(fulls, part=(0, 1)):
    n = len(fulls)
    p_idx, n_parts = part

    def piece(ref, chip_idx, half):
        rs = ref.shape[0] // N_CHIPS
        rh = rs // 2
        rows = rh // n_parts
        return ref.at[pl.ds(chip_idx * rs + half * rh + p_idx * rows, rows)]

    def copy(sems, sem, src_ref, dst_ref, to):
        return pltpu.make_async_remote_copy(src_ref=src_ref, dst_ref=dst_ref, send_sem=sems[0].at[sem],
                                            recv_sem=sems[1].at[sem], device_id=to, device_id_type=MESH)

    def ici_copy(ins, outs, sems, i, j, chip, k, c):
        return copy(sems, 3 * i + j, piece(ins[i], k, c), piece(outs[i], k, c), (*chip, c))

    def start(ins, outs, sems):
        x, y, c, chips = _place()
        for i in range(n):
            for j, chip in enumerate(chips):
                ici_copy(ins, outs, sems, i, j, chip, 2 * x + y, c).start()

    def finish(ins, outs, sems):
        x, y, c, chips = _place()
        sibling = (x, y, 1 - c)
        passed = []
        for i in range(n):
            for j, chip in enumerate(chips):
                region = piece(outs[i], 2 * chip[0] + chip[1], c)
                copy(sems, 3 * i + j, region, region, (*chip, c)).wait_recv()
                cp = copy(sems, 3 * n + 3 * i + j, region, region, sibling)
                cp.start()
                passed.append(cp)
        for i in range(n):
            for j, chip in enumerate(chips):
                region = piece(outs[i], 2 * chip[0] + chip[1], 1 - c)
                copy(sems, 3 * n + 3 * i + j, region, region, sibling).wait_recv()
        for i in range(n):
            for j, chip in enumerate(chips):
                ici_copy(ins, outs, sems, i, j, chip, 2 * x + y, c).wait_send()
        for cp in passed:
            cp.wait_send()

    return _Carry(fulls, [jax.ShapeDtypeStruct(a.shape, a.dtype) for a in fulls], {i: i for i in range(n)},
                  [pltpu.SemaphoreType.DMA((6 * n,)), pltpu.SemaphoreType.DMA((6 * n,))], start, finish)


def _pair_swap_carry(grads):
    n = len(grads)

    def copy(ins, outs, sems, i):
        x, y, c, _ = _place()
        return pltpu.make_async_remote_copy(src_ref=ins[i].at[:, 1 - c], dst_ref=outs[i], send_sem=sems[0].at[i],
                                            recv_sem=sems[1].at[i], device_id=(x, y, 1 - c), device_id_type=MESH)

    def start(ins, outs, sems):
        for i in range(n):
            copy(ins, outs, sems, i).start()

    def finish(ins, outs, sems):
        for i in range(n):
            copy(ins, outs, sems, i).wait()

    return _Carry(grads, [jax.ShapeDtypeStruct((a.shape[0], a.shape[2], a.shape[3]), a.dtype) for a in grads], {},
                  [pltpu.SemaphoreType.DMA((n,)), pltpu.SemaphoreType.DMA((n,))], start, finish)


def _pair_sum(g4, got, c_idx, *, name):
    _, _, rh, cs = g4.shape
    rb = _row_block(rh, 256, 16)

    def body(c_ref, own_ref, got_ref, o_ref):
        o_ref[...] = (own_ref[...].astype(F32) + got_ref[...].astype(F32)).astype(o_ref.dtype)

    grid_spec = pltpu.PrefetchScalarGridSpec(
        num_scalar_prefetch=1, grid=(N_CHIPS, rh // rb),
        in_specs=[pl.BlockSpec((None, None, rb, cs), lambda k, t, c_ref: (k, c_ref[0], t, 0)),
                  pl.BlockSpec((None, rb, cs), lambda k, t, c_ref: (k, t, 0))],
        out_specs=pl.BlockSpec((None, rb, cs), lambda k, t, c_ref: (k, t, 0)))
    return pl.pallas_call(
        body, name=name, grid_spec=grid_spec, out_shape=jax.ShapeDtypeStruct((N_CHIPS, rh, cs), BF16),
        compiler_params=pltpu.CompilerParams(dimension_semantics=("parallel", "parallel")),
    )(c_idx, g4, got)


def _chip_scatter_carry(sums):
    n = len(sums)

    def copies(ins, outs, sems):
        x, y, c, chips = _place()
        return [pltpu.make_async_remote_copy(src_ref=ins[i].at[2 * chip[0] + chip[1]], dst_ref=outs[i].at[j],
                                             send_sem=sems[0].at[3 * i + j], recv_sem=sems[1].at[3 * i + j],
                                             device_id=(*chip, c), device_id_type=MESH)
                for i in range(n) for j, chip in enumerate(chips)]

    def start(ins, outs, sems):
        for cp in copies(ins, outs, sems):
            cp.start()

    def finish(ins, outs, sems):
        for cp in copies(ins, outs, sems):
            cp.wait()

    return _Carry(sums, [jax.ShapeDtypeStruct((3,) + a.shape[1:], a.dtype) for a in sums], {},
                  [pltpu.SemaphoreType.DMA((3 * n,)), pltpu.SemaphoreType.DMA((3 * n,))], start, finish)


def _chip_sum(sums, got, kc_idx, *, name):
    _, rh, cs = sums.shape
    rb = _row_block(rh, 256, 16)
    nblk = rh // rb

    def body(kc_ref, own_ref, got_ref, o_ref):
        acc = own_ref[...].astype(F32)
        for j in range(3):
            acc = acc + got_ref[j].astype(F32)
        o_ref[...] = acc

    grid_spec = pltpu.PrefetchScalarGridSpec(
        num_scalar_prefetch=1, grid=(nblk,),
        in_specs=[pl.BlockSpec((None, rb, cs), lambda t, kc: (kc[0], t, 0)),
                  pl.BlockSpec((3, rb, cs), lambda t, kc: (0, t, 0))],
        out_specs=pl.BlockSpec((rb, cs), lambda t, kc: (kc[1] * nblk + t, 0)))
    return pl.pallas_call(
        body, name=name, grid_spec=grid_spec, out_shape=jax.ShapeDtypeStruct((2 * rh, cs), F32),
        compiler_params=pltpu.CompilerParams(dimension_semantics=("parallel",)),
    )(kc_idx, sums, got)


def _half_swap_carry(shards):
    n = len(shards)

    def copy(ins, outs, sems, i, to_my_half):
        x, y, c, _ = _place()
        rh = ins[i].shape[0] // 2
        half = c if to_my_half else 1 - c
        return pltpu.make_async_remote_copy(
            src_ref=ins[i].at[pl.ds(c * rh, rh)], dst_ref=outs[i].at[pl.ds(half * rh, rh)],
            send_sem=sems[0].at[i], recv_sem=sems[1].at[i], device_id=(x, y, 1 - c), device_id_type=MESH)

    def start(ins, outs, sems):
        for i in range(n):
            copy(ins, outs, sems, i, True).start()

    def finish(ins, outs, sems):
        for i in range(n):
            copy(ins, outs, sems, i, False).wait_recv()
        for i in range(n):
            copy(ins, outs, sems, i, True).wait_send()

    return _Carry(shards, [jax.ShapeDtypeStruct(a.shape, a.dtype) for a in shards], {i: i for i in range(n)},
                  [pltpu.SemaphoreType.DMA((n,)), pltpu.SemaphoreType.DMA((n,))], start, finish)


def _small_all_reduce(block, *, name):
    rows = block.shape[0]

    def body(x_ref, o_ref, buf, send_sems, recv_sems):
        x, y, c, _ = _place()
        me = 4 * x + 2 * y + c
        buf[me] = x_ref[...]
        copies = []
        for d in range(1, 8):
            to = (1 - x if d & 4 else x, 1 - y if d & 2 else y, 1 - c if d & 1 else c)
            cp = pltpu.make_async_remote_copy(src_ref=x_ref, dst_ref=buf.at[me], send_sem=send_sems.at[d - 1],
                                              recv_sem=recv_sems.at[d - 1], device_id=to, device_id_type=MESH)
            cp.start()
            copies.append((cp, to))
        for d in range(1, 8):
            to = copies[d - 1][1]
            sender = 4 * to[0] + 2 * to[1] + to[2]
            pltpu.make_async_remote_copy(src_ref=x_ref, dst_ref=buf.at[sender], send_sem=send_sems.at[d - 1],
                                         recv_sem=recv_sems.at[d - 1], device_id=to, device_id_type=MESH).wait_recv()
        for cp, _ in copies:
            cp.wait_send()
        acc = buf[0]
        for s in range(1, 8):
            acc = acc + buf[s]
        o_ref[...] = acc

    vm = pl.BlockSpec(memory_space=pltpu.VMEM)
    return pl.pallas_call(
        body, name=name, in_specs=[vm], out_specs=vm, out_shape=jax.ShapeDtypeStruct(block.shape, F32),
        scratch_shapes=[pltpu.VMEM((8, rows, LANES), F32), pltpu.SemaphoreType.DMA((7,)),
                        pltpu.SemaphoreType.DMA((7,))],
    )(block)


def _adamw(w, g, m, v, *, name):
    R, C = w.shape
    rb = _row_block(R, 256, SUBLANES)
    c1 = 1.0 - ADAM_B1 ** ADAM_STEP
    c2 = 1.0 - ADAM_B2 ** ADAM_STEP

    def body(w_ref, g_ref, m_ref, v_ref, d_ref, mo_ref, vo_ref):
        gv = g_ref[...]
        mn = ADAM_B1 * m_ref[...] + (1.0 - ADAM_B1) * gv
        vn = ADAM_B2 * v_ref[...] + (1.0 - ADAM_B2) * (gv * gv)
        d_ref[...] = -ADAM_LR * ((mn / c1) / (jnp.sqrt(vn / c2) + ADAM_EPS) + ADAM_WD * w_ref[...])
        mo_ref[...] = mn
        vo_ref[...] = vn

    spec = pl.BlockSpec((rb, C), lambda i: (i, 0))
    shp = jax.ShapeDtypeStruct((R, C), F32)
    return pl.pallas_call(
        body, name=name, grid=(R // rb,), in_specs=[spec] * 4, out_specs=(spec, spec, spec),
        out_shape=(shp, shp, shp), compiler_params=pltpu.CompilerParams(dimension_semantics=("parallel",)),
    )(w, g, m, v)


def _pack(parts):
    rows = []
    for a in parts:
        flat = a.reshape(-1).astype(F32)
        pad = (-flat.shape[0]) % PACK_UNIT
        rows.append(jnp.pad(flat, (0, pad)).reshape(-1, LANES))
    return jnp.concatenate(rows, axis=0)


def _unpack(block, shapes):
    out, row = [], 0
    for shp in shapes:
        size = int(np.prod(shp))
        nrows = -(-size // PACK_UNIT) * SUBLANES
        out.append(block[row:row + nrows].reshape(-1)[:size].reshape(shp))
        row += nrows
    return out


def _local_step(x, p, target, small, sched):
    T = x.shape[0]
    W = MIX_W
    rel_bias = small["rel_bias"]
    b_in_a, b_in_b = small["b_in"][:, 0:ZA_W], small["b_in"][:, ZA_W:]

    n1 = _rms_fwd(x, small["ffn1_pre_g"], name="n1")
    w_gu1 = sched.weight("ffn1_w_gu")
    *gu1, a1 = sched.run(_ffn_up, n1, w_gu1, bm=512, name="ffn1_gu")
    w_d1 = sched.weight("ffn1_w_down")
    f1, h1, n2 = sched.run(_mm_resid_norm, a1, w_d1, x, small["ffn1_post_g"], 0.5, small["attn_pre_g"], bm=512,
                           name="ffn1_down")
    win_a, win_b = sched.weight("w_in")
    za = _mm_nn(n2, win_a, bm=1024, bn=ZA_W, out_dtype=BF16, name="in_proj_a", bias=b_in_a)
    zb = _mm_nn(n2, win_b, bm=1024, bn=ZB_W // 2, out_dtype=BF16, name="in_proj_b", bias=b_in_b)

    k_a = za[:, W:W + LANES].reshape(T, 2, 1, HEAD_DIM)
    v_a = za[:, W + LANES:W + 2 * LANES].reshape(T, 2, 1, HEAD_DIM)
    k_a = jnp.broadcast_to(k_a, (T, 2, 4, HEAD_DIM)).reshape(T, W)
    v_a = jnp.broadcast_to(v_a, (T, 2, 4, HEAD_DIM)).reshape(T, W)
    idx_a = jnp.asarray(_t5_index(A_WINDOW - 1, 1))
    bias_a = _bias_tiles(rel_bias, idx_a, 0, name="bias_a")
    a_args = dict(r=1, q_col=0, k_col=0, v_col=0, stride=0)
    o_a, lse_a1 = sched.run(_band_fwd, za, k_a, v_a, bias_a, name="band_a_fwd", **a_args)
    sink_row = jnp.repeat(small["sinks"], HEAD_DIM, axis=1)
    out_a, out_a_bf, lse_a = _merge([(o_a, lse_a1)], sink_row, name="merge_a")

    no_sink = jnp.full((1, W), NEG, F32)
    b_ctx, branches = [], []
    for t, (window, dil) in enumerate(B_PATTERNS):
        idx = jnp.asarray(_t5_index(window // dil, dil))
        bias = _bias_tiles(rel_bias, idx, 2 * N_PAIRS, name=f"bias_b{t}")
        zb_r = zb.reshape(T // dil, dil * ZB_W)
        args = dict(r=dil, q_col=0, k_col=1, v_col=2, stride=ZB_W // W)
        o, l = sched.run(_band_fwd, zb_r, zb_r, zb_r, bias, name=f"band_b{t}_fwd", **args)
        branches.append((o.reshape(T, W), l.reshape(T, W)))
        b_ctx.append((idx, bias, zb_r, args, dil))
    out_b, out_b_bf, lse_b = _merge(branches, no_sink, name="merge_b")

    mix = jnp.concatenate([out_a_bf, out_b_bf], axis=1)
    w_out = sched.weight("w_out")
    att, h2, n3 = _mm_resid_norm(mix, w_out, h1, small["attn_post_g"], 1.0, small["ffn2_pre_g"], bm=512,
                                 name="out_proj", bias=small["b_out"])
    w_gu2, w_d2 = sched.weight("ffn2_w_gu"), sched.weight("ffn2_w_down")
    *gu2, a2 = _ffn_up(n3, w_gu2, bm=512, name="ffn2_gu")
    f2, h3, n4 = _mm_resid_norm(a2, w_d2, h2, small["ffn2_post_g"], 0.5, small["ple_pre_g"], bm=512,
                                name="ffn2_down")
    w_gate = sched.weight("w_ple_gate")
    gp = _mm_nn(n4, w_gate, bm=1024, bn=1024, out_dtype=F32, name="ple_gate")
    p_bf = p.astype(BF16)
    e = _mm_nn(p_bf, sched.weight("w_ple_proj"), bm=1024, bn=256, out_dtype=F32, name="ple_proj")
    dh4, de, dgp, loss, d_ple_post = _ple_loss(h3, gp, e, small["ple_post_g"], target, name="ple_loss")

    gs = {"ple_post_g": d_ple_post}
    sched.grad("w_ple_proj", _mm_tn(p_bf, de, bm=256, bn=256, out_dtype=BF16, name="d_w_ple", n_stack=N_CHIPS))
    sched.grad("w_ple_gate", _mm_tn(n4, dgp, bm=512, bn=1024, out_dtype=BF16, name="d_w_gate"))
    dh3, df2, gs["ple_pre_g"], gs["ffn2_post_g"], _ = sched.run(
        _mm_nt_norms_bwd, dgp, w_gate, dh4, h3, small["ple_pre_g"], f2, small["ffn2_post_g"], 0.5, bm=512, name="d_n4")

    def ffn_bwd(df, a, gu, n, w_down, w_gu, tag, *norm_args):
        dgu = sched.run(_ffn_down_bwd, df, w_down, gu[0], gu[1], bm=256, name=f"ffn{tag}_d_a")
        sched.grad(f"ffn{tag}_w_down", _mm_tn(a, df, bm=256, bn=1024, out_dtype=BF16, name=f"ffn{tag}_d_w_down"))
        sched.grad(f"ffn{tag}_w_gu", sched.run(_mm_tn, n, dgu, bm=512, bn=1408, out_dtype=BF16,
                                              name=f"ffn{tag}_d_w_gu", n_stack=N_CHIPS))
        return sched.run(_mm_nt_norms_bwd, dgu, w_gu, *norm_args, bm=512, name=f"ffn{tag}_d_n")

    dh2, datt, gs["ffn2_pre_g"], gs["attn_post_g"], gs["b_out"] = ffn_bwd(
        df2, a2, gu2, n3, w_d2, w_gu2, "2", dh3, h2, small["ffn2_pre_g"], att, small["attn_post_g"], 1.0)
    sched.grad("w_out", _mm_tn(mix, datt, bm=512, bn=1024, out_dtype=BF16, name="d_w_out"))
    dmix = sched.run(_mm_nt, datt, w_out, bm=1024, bn=1024, out_dtype=F32, name="d_mix")

    do_a, do_b, stat_a, stat_b, dsink = _attn_delta(dmix, (out_a, out_b), (lse_a, lse_b), sink_row, name="attn_delta")
    gs["sinks"] = dsink[:, ::HEAD_DIM]
    dq_a, dk_a, dv_a, dbias_a = sched.run(_band_bwd, za, k_a, v_a, do_a, stat_a, bias_a, name="band_a_bwd", **a_args)
    d_rel_a = _bias_grad(dbias_a, idx_a, name="d_rel_a")
    b_grads = []
    d_rel_b = None
    for t, (idx, bias, zb_r, args, dil) in enumerate(b_ctx):
        shp = (T // dil, dil * W)
        dq, dk, dv, dbias = sched.run(_band_bwd, zb_r, zb_r, zb_r, do_b.reshape(shp), stat_b.reshape(shp), bias,
                                      name=f"band_b{t}_bwd", **args)
        b_grads.append((dq.reshape(T, W), dk.reshape(T, W), dv.reshape(T, W)))
        d_rel = _bias_grad(dbias, idx, name=f"d_rel_b{t}")
        d_rel_b = d_rel if d_rel_b is None else d_rel_b + d_rel
    gs["rel_bias"] = jnp.concatenate([d_rel_a[:, 0:2 * N_PAIRS], d_rel_b[:, 0:2 * N_PAIRS]], axis=1)
    dza, dzb, dsum_a, dsum_b = _dz_assemble(dq_a, dk_a, dv_a, b_grads, name="d_z")
    gs["b_in"] = jnp.concatenate([dsum_a, dsum_b], axis=1)
    sched.grad("w_in", (_mm_tn(n2, dza, bm=512, bn=ZA_W, out_dtype=BF16, name="d_w_in_a"),
                        _mm_tn(n2, dzb, bm=512, bn=ZB_W // 2, out_dtype=BF16, name="d_w_in_b")))
    dn2_a = _mm_nt(dza, win_a, bm=1024, bn=1024, out_dtype=F32, name="d_n2_a")
    dh1, df1, gs["attn_pre_g"], gs["ffn1_post_g"], _ = sched.run(
        _mm_nt_norms_bwd, dzb, win_b, dh2, h1, small["attn_pre_g"], f1, small["ffn1_post_g"], 0.5, bm=512,
        name="d_n2_b", add=dn2_a)
    grad_x, gs["ffn1_pre_g"] = ffn_bwd(df1, a1, gu1, n1, w_d1, w_gu1, "1", dh1, x, small["ffn1_pre_g"], None, None, 0.0)
    return loss, grad_x, gs


def _to_compute(name, full):
    st = full.reshape(N_CHIPS, full.shape[0] // N_CHIPS, full.shape[1])
    if name in ("ffn1_w_gu", "ffn2_w_gu", "w_ple_proj"):
        return st
    if name == "w_in":
        whole = jnp.transpose(st, (1, 0, 2)).reshape(st.shape[1], -1)
        return whole[:, 0:ZA_W], whole[:, ZA_W:]
    return full


def _to_comm(name, g):
    if name == "w_in":
        whole = jnp.concatenate(g, axis=1)
        g = jnp.transpose(whole.reshape(whole.shape[0], N_CHIPS, -1), (1, 0, 2))
    rs = g.shape[1] if g.ndim == 3 else g.shape[0] // N_CHIPS
    return g.reshape(N_CHIPS, 2, rs // 2, g.shape[-1])


class _Schedule:
    GATHER = {
        "ffn1_gu": [("ffn1_w_down", (0, 1)), ("w_in", (0, 1))],
        "ffn1_down": [("w_out", (0, 1))],
        "band_a_fwd": [("ffn2_w_gu", (0, 2))],
        "band_b0_fwd": [("ffn2_w_gu", (1, 2))],
        "band_b1_fwd": [("ffn2_w_down", (0, 1))],
        "band_b2_fwd": [("w_ple_gate", (0, 1)), ("w_ple_proj", (0, 1))],
    }
    SWAP = {"d_n4": ["w_ple_proj", "w_ple_gate"], "ffn2_d_w_gu": ["ffn2_w_down"], "ffn2_d_n": ["ffn2_w_gu"],
            "d_mix": ["w_out"], "d_n2_b": ["w_in"], "ffn1_d_w_gu": ["ffn1_w_down"], "ffn1_d_n": ["ffn1_w_gu"]}
    SCATTER = {"ffn2_d_a": ["w_ple_proj", "w_ple_gate"], "ffn2_d_n": ["ffn2_w_down"], "band_a_bwd": ["ffn2_w_gu"],
               "band_b0_bwd": ["w_out"], "ffn1_d_a": ["w_in"], "ffn1_d_n": ["ffn1_w_down"]}

    def __init__(self, placed, c_idx, kc_idx):
        self.full = dict(placed)
        self.missing = {n: 1.0 for n in placed}
        self.c_idx, self.kc_idx = c_idx, kc_idx
        self.grads, self.swapped, self.pair_sums, self.scattered = {}, {}, {}, {}

    def _gather(self, entries):
        carries, after = [], []
        for part in sorted({pt for _, pt in entries}):
            names = [n for n, pt in entries if pt == part]
            carry = _gather_carry([self.full[n] for n in names], part)
            carries.append(carry)

            def done(carry=carry, names=names, part=part):
                for n, a in zip(names, carry.results):
                    self.full[n] = a
                    self.missing[n] -= 1.0 / part[1]
            after.append(done)
        return carries, after

    def gather_now(self, names, *, name):
        carries, after = self._gather([(n, (0, 1)) for n in names])
        _comm_call(_join(carries), name=name)
        for fn in after:
            fn()

    def weight(self, name):
        assert abs(self.missing[name]) < 1e-9, (name, self.missing[name])
        return _to_compute(name, self.full[name])

    def grad(self, name, g):
        self.grads[name] = _to_comm(name, g)

    def _swap(self, names):
        carry = _pair_swap_carry([self.grads[n] for n in names])

        def done():
            self.swapped.update(zip(names, carry.results))
        return carry, done

    def _scatter(self, names):
        for n in names:
            self.pair_sums[n] = _pair_sum(self.grads[n], self.swapped[n], self.c_idx, name=f"grad_pair_sum_{n}")
        carry = _chip_scatter_carry([self.pair_sums[n] for n in names])

        def done():
            self.scattered.update(zip(names, carry.results))
        return carry, done

    def run(self, fn, *args, name, **kw):
        carries, after = self._gather(self.GATHER.get(name, []))
        for names, make in ((self.SWAP.get(name), self._swap), (self.SCATTER.get(name), self._scatter)):
            if names:
                carry, done = make(names)
                carries.append(carry)
                after.append(done)
        out = fn(*args, name=name, carry=_join(carries), **kw)
        for done in after:
            done()
        return out

    def finish(self):
        left = [n for n in BIG if n not in self.scattered]
        to_swap = [n for n in left if n not in self.swapped]
        if to_swap:
            carry, done = self._swap(to_swap)
            _comm_call(carry, name="grad_pair_swap")
            done()
        if left:
            carry, done = self._scatter(left)
            _comm_call(carry, name="grad_chip_scatter")
            done()
        halves = [_chip_sum(self.pair_sums[n], self.scattered[n], self.kc_idx, name=f"grad_chip_sum_{n}") for n in BIG]
        carry = _half_swap_carry(halves)
        _comm_call(carry, name="grad_half_swap")
        return dict(zip(BIG, carry.results))


def kernel(x, p, rel_bias, ffn1_pre_g, ffn1_w_gu, ffn1_w_down, ffn1_post_g, attn_pre_g, w_in, b_in, sinks, w_out, b_out, attn_post_g, ffn2_pre_g, ffn2_w_gu, ffn2_w_down, ffn2_post_g, ple_pre_g, w_ple_gate, w_ple_proj, ple_post_g, loss_target, m_rel_bias, m_ffn1_pre_g, m_ffn1_w_gu, m_ffn1_w_down, m_ffn1_post_g, m_attn_pre_g, m_w_in, m_b_in, m_sinks, m_w_out, m_b_out, m_attn_post_g, m_ffn2_pre_g, m_ffn2_w_gu, m_ffn2_w_down, m_ffn2_post_g, m_ple_pre_g, m_w_ple_gate, m_w_ple_proj, m_ple_post_g, v_rel_bias, v_ffn1_pre_g, v_ffn1_w_gu, v_ffn1_w_down, v_ffn1_post_g, v_attn_pre_g, v_w_in, v_b_in, v_sinks, v_w_out, v_b_out, v_attn_post_g, v_ffn2_pre_g, v_ffn2_w_gu, v_ffn2_w_down, v_ffn2_post_g, v_ple_pre_g, v_w_ple_gate, v_w_ple_proj, v_ple_post_g):
    given = dict(locals())
    w = {n: given[n] for n in WEIGHTS}
    m = {n: given["m_" + n] for n in WEIGHTS}
    v = {n: given["v_" + n] for n in WEIGHTS}
    c_pos = lax.axis_index("c").astype(jnp.int32)
    k_pos = (2 * lax.axis_index("x") + lax.axis_index("y")).astype(jnp.int32)
    c_idx, k_idx, kc_idx = c_pos.reshape(1), k_pos.reshape(1), jnp.stack([k_pos, c_pos])

    sched = _Schedule({n: _cast_place(w[n][0], k_idx, name=f"place_{n}") for n in BIG}, c_idx, kc_idx)
    sched.gather_now(["ffn1_w_gu"], name="gather_ffn1_w_gu")
    small = {n: (w[n] if n == "rel_bias" else w[n].reshape(1, -1)) for n in SMALL}

    loss_part, grad_x, gs = _local_step(x[0], p[0, 0], loss_target[0], small, sched)

    grads_big = sched.finish()

    small_shapes = [w[n].shape for n in SMALL]
    packed = _pack([gs[n] for n in SMALL] + [loss_part[:, 0:1]])
    summed = _small_all_reduce(packed, name="small_all_reduce")
    unpacked = _unpack(summed, small_shapes + [(1,)])
    loss = unpacked[-1].reshape(())

    grads, delta, new_m, new_v = {}, {}, {}, {}
    for n in BIG:
        grads[n] = grads_big[n][None]
        d, mn, vn = _adamw(w[n][0], grads_big[n], m[n][0], v[n][0], name=f"adamw_{n}")
        delta[n], new_m[n], new_v[n] = d[None], mn[None], vn[None]
    n_rows = summed.shape[0] - SUBLANES
    d_s, m_s, v_s = _adamw(_pack([w[n] for n in SMALL]), summed[0:n_rows], _pack([m[n] for n in SMALL]),
                           _pack([v[n] for n in SMALL]), name="adamw_small")
    for name, dd, mm, vv, gg in zip(SMALL, _unpack(d_s, small_shapes), _unpack(m_s, small_shapes),
                                    _unpack(v_s, small_shapes), unpacked[:-1]):
        grads[name], delta[name], new_m[name], new_v[name] = gg, dd, mm, vv

    return (loss, grad_x[None], *[grads[n] for n in WEIGHTS], *[delta[n] for n in WEIGHTS],
            *[new_m[n] for n in WEIGHTS], *[new_v[n] for n in WEIGHTS])
```

```python
import math

import jax
import jax.numpy as jnp
import numpy as np
from jax import lax
from jax.experimental import pallas as pl
from jax.experimental.pallas import tpu as pltpu

F32 = jnp.float32
BF16 = jnp.bfloat16
MESH = pl.DeviceIdType.MESH

EPS = 1e-6
NEG = -1e30
LANES = 128
SUBLANES = 8
HEAD_DIM = 64
QBLK = 128
N_PAIRS = 4
MIX_W = N_PAIRS * LANES
A_WINDOW = 128
B_PATTERNS = ((128, 1), (512, 4), (2048, 16))
NUM_BUCKETS = 32
MAX_DISTANCE = 2048
ZA_W = 768
ZB_W = 1536
N_CHIPS = 4
VMEM_BYTES_V7X = 64 * 2**20

ADAM_LR, ADAM_B1, ADAM_B2, ADAM_EPS, ADAM_WD, ADAM_STEP = 0.001, 0.9, 0.999, 1e-08, 0.01, 10

BIG = ("ffn1_w_gu", "ffn1_w_down", "w_in", "w_out", "ffn2_w_gu", "ffn2_w_down", "w_ple_gate", "w_ple_proj")
SMALL = ("rel_bias", "ffn1_pre_g", "ffn1_post_g", "attn_pre_g", "b_in", "sinks", "b_out", "attn_post_g",
         "ffn2_pre_g", "ffn2_post_g", "ple_pre_g", "ple_post_g")
WEIGHTS = ("rel_bias", "ffn1_pre_g", "ffn1_w_gu", "ffn1_w_down", "ffn1_post_g", "attn_pre_g", "w_in", "b_in",
           "sinks", "w_out", "b_out", "attn_post_g", "ffn2_pre_g", "ffn2_w_gu", "ffn2_w_down", "ffn2_post_g",
           "ple_pre_g", "w_ple_gate", "w_ple_proj", "ple_post_g")


def _vmem_limit(*block_bytes):
    need = 2 * sum(block_bytes) + max(block_bytes) * 2 + (4 << 20)
    return int(min(max(need, 32 << 20), VMEM_BYTES_V7X - (6 << 20)))


def _nbytes(shape, dtype):
    return int(np.prod(shape)) * jnp.dtype(dtype).itemsize


def _row_block(rows, cap, mult):
    for cand in range(min(rows, cap), 0, -1):
        if rows % cand == 0 and cand % mult == 0:
            return cand
    raise ValueError((rows, cap, mult))


class _Carry:
    def __init__(self, operands, out_shapes, aliases, sems, start, finish):
        self.operands, self.out_shapes, self.aliases, self.sems = list(operands), list(out_shapes), dict(aliases), list(sems)
        self.start, self.finish = start, finish
        self.results = None


def _join(carries):
    carries = [c for c in carries if c is not None]
    if not carries:
        return None
    if len(carries) == 1:
        return carries[0]
    offs, pos = [], [0, 0, 0]
    for c in carries:
        offs.append(tuple(pos))
        pos = [pos[0] + len(c.operands), pos[1] + len(c.out_shapes), pos[2] + len(c.sems)]
    aliases = {}
    for c, (oi, oo, _) in zip(carries, offs):
        aliases.update({oi + i: oo + o for i, o in c.aliases.items()})

    def run(which):
        def fn(ins, outs, sems):
            for c, (oi, oo, os_) in zip(carries, offs):
                getattr(c, which)(ins[oi:oi + len(c.operands)], outs[oo:oo + len(c.out_shapes)],
                                  sems[os_:os_ + len(c.sems)])
        return fn

    joined = _Carry([a for c in carries for a in c.operands], [s for c in carries for s in c.out_shapes], aliases,
                    [s for c in carries for s in c.sems], run("start"), run("finish"))
    joined.parts = (carries, offs)
    return joined


def _set_results(carry, outs):
    carry.results = list(outs)
    if hasattr(carry, "parts"):
        for c, (_, oo, _) in zip(*carry.parts):
            _set_results(c, outs[oo:oo + len(c.out_shapes)])


ANY = pl.BlockSpec(memory_space=pl.ANY)


def _call(body, *, name, grid, in_specs, out_specs, out_shape, args, scratch=(), vmem_limit=None, carry=None,
          semantics=None):
    n_ci, n_co, n_cs = len(args), len(out_shape), len(scratch)
    semantics = semantics or ("parallel",) * len(grid)
    if carry is None:
        res = pl.pallas_call(
            body, name=name, grid=grid, in_specs=list(in_specs), out_specs=tuple(out_specs), out_shape=tuple(out_shape),
            scratch_shapes=list(scratch),
            compiler_params=pltpu.CompilerParams(dimension_semantics=semantics, vmem_limit_bytes=vmem_limit),
        )(*args)
        return list(res)
    n_pi, n_po = len(carry.operands), len(carry.out_shapes)

    def full_body(*refs):
        ci, pi = refs[:n_ci], refs[n_ci:n_ci + n_pi]
        o0 = n_ci + n_pi
        co, po = refs[o0:o0 + n_co], refs[o0 + n_co:o0 + n_co + n_po]
        s0 = o0 + n_co + n_po
        cs, ps = refs[s0:s0 + n_cs], refs[s0 + n_cs:]
        ids = [pl.program_id(ax) for ax in range(len(grid))]
        first, last = ids[0] == 0, ids[0] == grid[0] - 1
        for ax in range(1, len(grid)):
            first, last = first & (ids[ax] == 0), last & (ids[ax] == grid[ax] - 1)

        @pl.when(first)
        def _():
            carry.start(pi, po, ps)

        body(*ci, *co, *cs)

        @pl.when(last)
        def _():
            carry.finish(pi, po, ps)

    res = pl.pallas_call(
        full_body, name=name, grid=grid, in_specs=list(in_specs) + [ANY] * n_pi,
        out_specs=tuple(out_specs) + tuple([ANY] * n_po), out_shape=tuple(out_shape) + tuple(carry.out_shapes),
        scratch_shapes=list(scratch) + carry.sems,
        input_output_aliases={n_ci + i: n_co + o for i, o in carry.aliases.items()},
        compiler_params=pltpu.CompilerParams(dimension_semantics=("arbitrary",) * len(grid),
                                             vmem_limit_bytes=vmem_limit),
    )(*args, *carry.operands)
    _set_results(carry, res[n_co:])
    return list(res[:n_co])


def _comm_call(carry, *, name):
    n_pi, n_po = len(carry.operands), len(carry.out_shapes)

    def body(*refs):
        pi, po, ps = refs[:n_pi], refs[n_pi:n_pi + n_po], refs[n_pi + n_po:]
        carry.start(pi, po, ps)
        carry.finish(pi, po, ps)

    res = pl.pallas_call(
        body, name=name, in_specs=[ANY] * n_pi, out_specs=tuple([ANY] * n_po), out_shape=tuple(carry.out_shapes),
        scratch_shapes=carry.sems, input_output_aliases=dict(carry.aliases),
    )(*carry.operands)
    _set_results(carry, res)


def _mm_nn(a, b, *, bm, bn, out_dtype, name, bias=None, carry=None):
    M, K = a.shape
    stacked = b.ndim == 3
    N = b.shape[0] * b.shape[2] if stacked else b.shape[1]
    assert M % bm == 0 and N % bn == 0 and (not stacked or bn == b.shape[2])

    def body(*refs):
        a_ref, b_ref = refs[0], refs[1]
        o_ref = refs[-1]
        acc = jnp.dot(a_ref[...], b_ref[...], preferred_element_type=F32)
        if bias is not None:
            acc = acc + refs[2][...]
        o_ref[...] = acc.astype(o_ref.dtype)

    if stacked:
        b_spec = pl.BlockSpec((None, K, bn), lambda i, j: (j, 0, 0))
    else:
        b_spec = pl.BlockSpec((K, bn), lambda i, j: (0, j))
    in_specs = [pl.BlockSpec((bm, K), lambda i, j: (i, 0)), b_spec]
    args = [a, b]
    if bias is not None:
        in_specs.append(pl.BlockSpec((1, bn), lambda i, j: (0, j)))
        args.append(bias)
    limit = _vmem_limit(_nbytes((bm, K), a.dtype), _nbytes((K, bn), b.dtype), _nbytes((bm, bn), F32))
    return _call(body, name=name, grid=(M // bm, N // bn), in_specs=in_specs,
                 out_specs=[pl.BlockSpec((bm, bn), lambda i, j: (i, j))],
                 out_shape=[jax.ShapeDtypeStruct((M, N), out_dtype)], args=args, vmem_limit=limit, carry=carry)[0]


def _mm_nt(a, b, *, bm, bn, out_dtype, name, add=None, carry=None):
    M, K = a.shape
    stacked = b.ndim == 3
    N = b.shape[1] if stacked else b.shape[0]
    n_sh = b.shape[0] if stacked else 1
    ks = b.shape[2] if stacked else K
    assert M % bm == 0 and N % bn == 0 and n_sh * ks == K
    dn = (((1,), (1,)), ((), ()))

    def body(*refs):
        a_ref, b_ref = refs[0], refs[1]
        o_ref = refs[-1]
        if stacked:
            acc = lax.dot_general(a_ref[:, 0:ks], b_ref[0], dn, preferred_element_type=F32)
            for s in range(1, n_sh):
                acc = acc + lax.dot_general(a_ref[:, s * ks:(s + 1) * ks], b_ref[s], dn, preferred_element_type=F32)
        else:
            acc = lax.dot_general(a_ref[...], b_ref[...], dn, preferred_element_type=F32)
        if add is not None:
            acc = acc + refs[2][...]
        o_ref[...] = acc.astype(o_ref.dtype)

    if stacked:
        b_spec = pl.BlockSpec((n_sh, bn, ks), lambda i, j: (0, j, 0))
    else:
        b_spec = pl.BlockSpec((bn, K), lambda i, j: (j, 0))
    in_specs = [pl.BlockSpec((bm, K), lambda i, j: (i, 0)), b_spec]
    args = [a, b]
    if add is not None:
        in_specs.append(pl.BlockSpec((bm, bn), lambda i, j: (i, j)))
        args.append(add)
    limit = _vmem_limit(_nbytes((bm, K), a.dtype), _nbytes((bn, K), b.dtype), _nbytes((bm, bn), F32))
    return _call(body, name=name, grid=(M // bm, N // bn), in_specs=in_specs,
                 out_specs=[pl.BlockSpec((bm, bn), lambda i, j: (i, j))],
                 out_shape=[jax.ShapeDtypeStruct((M, N), out_dtype)], args=args, vmem_limit=limit, carry=carry)[0]


def _mm_tn(a, b, *, bm, bn, out_dtype, name, n_stack=None, carry=None):
    T, M = a.shape
    N = b.shape[1]
    assert M % bm == 0 and N % bn == 0 and (n_stack is None or bn * n_stack == N)
    dn = (((0,), (0,)), ((), ()))

    def body(a_ref, b_ref, o_ref):
        acc = lax.dot_general(a_ref[...], b_ref[...], dn, preferred_element_type=F32)
        o_ref[...] = acc.astype(o_ref.dtype)

    if n_stack is None:
        out_spec = pl.BlockSpec((bm, bn), lambda i, j: (i, j))
        out_shape = jax.ShapeDtypeStruct((M, N), out_dtype)
    else:
        out_spec = pl.BlockSpec((None, bm, bn), lambda i, j: (j, i, 0))
        out_shape = jax.ShapeDtypeStruct((n_stack, M, bn), out_dtype)
    limit = _vmem_limit(_nbytes((T, bm), a.dtype), _nbytes((T, bn), b.dtype), _nbytes((bm, bn), F32))
    return _call(body, name=name, grid=(M // bm, N // bn),
                 in_specs=[pl.BlockSpec((T, bm), lambda i, j: (0, i)), pl.BlockSpec((T, bn), lambda i, j: (0, j))],
                 out_specs=[out_spec], out_shape=[out_shape], args=[a, b], vmem_limit=limit, carry=carry)[0]


ROW_BLK = 256


def _rstd(x):
    return lax.rsqrt(jnp.mean(x * x, axis=-1, keepdims=True) + EPS)


def _norm_bwd(xhat, rstd, dxhat):
    return rstd * (dxhat - xhat * jnp.mean(dxhat * xhat, axis=-1, keepdims=True))


def _row_spec(cols):
    return pl.BlockSpec((ROW_BLK, cols), lambda i: (i, 0))


def _vec_spec(cols):
    return pl.BlockSpec((1, cols), lambda i: (0, 0))


def _prologue(x, g, rel_bias, idx_all, *, name, carry=None):
    T, D = x.shape
    n_pat = idx_all.shape[0]
    heads = 2 * N_PAIRS
    steps = n_pat * heads
    rows = T // steps

    def body(rb_ref, x_ref, g_ref, idx_ref, n_ref, bias_ref):
        s = pl.program_id(0)
        head = jnp.where(s < heads, s, heads + s % heads)
        xv = x_ref[...]
        n_ref[...] = (xv * _rstd(xv) * g_ref[...]).astype(n_ref.dtype)
        idxv = idx_ref[...]
        acc = jnp.where(idxv < 0, NEG, 0.0).astype(F32)
        for b in range(NUM_BUCKETS):
            acc = acc + jnp.where(idxv == b, rb_ref[b, head], 0.0)
        bias_ref[0] = acc
        kap = lax.broadcasted_iota(jnp.int32, (1, 2 * QBLK), 1)
        bias_ref[1] = jnp.where(kap < QBLK, NEG, acc)

    return _call(
        body, name=name, grid=(steps,),
        in_specs=[pl.BlockSpec(memory_space=pltpu.SMEM), pl.BlockSpec((rows, D), lambda s: (s, 0)),
                  pl.BlockSpec((1, D), lambda s: (0, 0)),
                  pl.BlockSpec((None, QBLK, 2 * QBLK), lambda s: (s // heads, 0, 0))],
        out_specs=[pl.BlockSpec((rows, D), lambda s: (s, 0)),
                   pl.BlockSpec((None, 2, None, QBLK, 2 * QBLK), lambda s: (s // heads, 0, s % heads, 0, 0))],
        out_shape=[jax.ShapeDtypeStruct((T, D), BF16),
                   jax.ShapeDtypeStruct((n_pat, 2, heads, QBLK, 2 * QBLK), F32)],
        args=[rel_bias, x, g, idx_all], carry=carry)


def _mm_resid_norm(a, b, h, g_post, coef, g_next, *, bm, name, bias=None, carry=None):
    M, K = a.shape
    N = b.shape[1]

    def body(*refs):
        a_ref, b_ref, h_ref, gp_ref, gn_ref = refs[0:5]
        f_ref, hn_ref, n_ref = refs[-3:]
        f = jnp.dot(a_ref[...], b_ref[...], preferred_element_type=F32)
        if bias is not None:
            f = f + refs[5][...]
        f_ref[...] = f
        hn = h_ref[...] + coef * (f * _rstd(f) * gp_ref[...])
        hn_ref[...] = hn
        n_ref[...] = (hn * _rstd(hn) * gn_ref[...]).astype(n_ref.dtype)

    row = lambda w: pl.BlockSpec((bm, w), lambda i: (i, 0))
    vec = pl.BlockSpec((1, N), lambda i: (0, 0))
    in_specs = [row(K), pl.BlockSpec((K, N), lambda i: (0, 0), pipeline_mode=pl.Buffered(1)), row(N), vec, vec]
    args = [a, b, h, g_post, g_next]
    if bias is not None:
        in_specs.append(vec)
        args.append(bias)
    limit = _vmem_limit(_nbytes((bm, K), a.dtype), _nbytes((K, N), b.dtype) // 2, 4 * _nbytes((bm, N), F32))
    return _call(body, name=name, grid=(M // bm,), in_specs=in_specs, out_specs=[row(N), row(N), row(N)],
                 out_shape=[jax.ShapeDtypeStruct((M, N), F32), jax.ShapeDtypeStruct((M, N), F32),
                            jax.ShapeDtypeStruct((M, N), BF16)], args=args, vmem_limit=limit, carry=carry)


def _mm_nt_norms_bwd(a, b, dh_in, h, g_pre, f, g_post, coef, *, bm, name, add=None, carry=None):
    M, K = a.shape
    stacked = b.ndim == 3
    N = b.shape[1] if stacked else b.shape[0]
    n_sh = b.shape[0] if stacked else 1
    ks = b.shape[2] if stacked else K
    assert n_sh * ks == K
    dn_dims = (((1,), (1,)), ((), ()))
    with_f = f is not None
    n_in = 5 + (2 if with_f else 0) + (1 if add is not None else 0)

    def body(*refs):
        a_ref, b_ref, dhi_ref, h_ref, gpre_ref = refs[0:5]
        outs = refs[n_in:]
        if stacked:
            dn = lax.dot_general(a_ref[:, 0:ks], b_ref[0], dn_dims, preferred_element_type=F32)
            for s in range(1, n_sh):
                dn = dn + lax.dot_general(a_ref[:, s * ks:(s + 1) * ks], b_ref[s], dn_dims, preferred_element_type=F32)
        else:
            dn = lax.dot_general(a_ref[...], b_ref[...], dn_dims, preferred_element_type=F32)
        if add is not None:
            dn = dn + refs[n_in - 1][...]

        @pl.when(pl.program_id(0) == 0)
        def _():
            for acc in (outs[2:] if with_f else outs[1:]):
                acc[...] = jnp.zeros_like(acc)

        hv = h_ref[...]
        r = _rstd(hv)
        hh = hv * r
        dh = dhi_ref[...] + _norm_bwd(hh, r, dn * gpre_ref[...])
        outs[0][...] = dh
        if not with_f:
            outs[1][...] += jnp.sum(dn * hh, axis=0, keepdims=True)
            return
        f_ref, gpost_ref = refs[5], refs[6]
        df_ref, dgpre_ref, dgpost_ref, dsum_ref = outs[1:]
        dgpre_ref[...] += jnp.sum(dn * hh, axis=0, keepdims=True)
        fv = f_ref[...]
        rf = _rstd(fv)
        fh = fv * rf
        dy = coef * dh
        dgpost_ref[...] += jnp.sum(dy * fh, axis=0, keepdims=True)
        df = _norm_bwd(fh, rf, dy * gpost_ref[...])
        dsum_ref[...] += jnp.sum(df, axis=0, keepdims=True)
        df_ref[...] = df.astype(df_ref.dtype)

    row = lambda w: pl.BlockSpec((bm, w), lambda i: (i, 0))
    vec = pl.BlockSpec((1, N), lambda i: (0, 0))
    once = pl.Buffered(1)
    if stacked:
        b_spec = pl.BlockSpec((n_sh, N, ks), lambda i: (0, 0, 0), pipeline_mode=once)
    else:
        b_spec = pl.BlockSpec((N, K), lambda i: (0, 0), pipeline_mode=once)
    in_specs = [row(K), b_spec, row(N), row(N), vec]
    args = [a, b, dh_in, h, g_pre]
    if with_f:
        in_specs += [row(N), vec]
        args += [f, g_post]
    if add is not None:
        in_specs.append(row(N))
        args.append(add)
    vec_shape = jax.ShapeDtypeStruct((1, N), F32)
    out_specs = [row(N)] + ([row(N), vec, vec, vec] if with_f else [vec])
    out_shape = [jax.ShapeDtypeStruct((M, N), F32)]
    out_shape += [jax.ShapeDtypeStruct((M, N), BF16), vec_shape, vec_shape, vec_shape] if with_f else [vec_shape]
    limit = _vmem_limit(_nbytes((bm, K), a.dtype), _nbytes((N, K), b.dtype) // 2, 6 * _nbytes((bm, N), F32))
    return _call(body, name=name, grid=(M // bm,), in_specs=in_specs, out_specs=out_specs, out_shape=out_shape,
                 args=args, vmem_limit=limit, semantics=("arbitrary",), carry=carry)


def _ffn_up(n, w_gu, *, bm, name, carry=None):
    M, K = n.shape
    n_sh, _, ns = w_gu.shape
    half = n_sh // 2

    def body(n_ref, wg_ref, wu_ref, g_ref, u_ref, a_ref):
        nv = n_ref[...]
        g = jnp.dot(nv, wg_ref[...], preferred_element_type=F32)
        u = jnp.dot(nv, wu_ref[...], preferred_element_type=F32)
        g_ref[...] = g.astype(g_ref.dtype)
        u_ref[...] = u.astype(u_ref.dtype)
        a_ref[...] = (g * jax.nn.sigmoid(g) * u).astype(a_ref.dtype)

    out_spec = pl.BlockSpec((bm, ns), lambda i, j: (i, j))
    out = jax.ShapeDtypeStruct((M, half * ns), BF16)
    limit = _vmem_limit(_nbytes((bm, K), n.dtype), 2 * _nbytes((K, ns), w_gu.dtype), 3 * _nbytes((bm, ns), F32))
    return _call(body, name=name, grid=(M // bm, half),
                 in_specs=[pl.BlockSpec((bm, K), lambda i, j: (i, 0)),
                           pl.BlockSpec((None, K, ns), lambda i, j: (j, 0, 0)),
                           pl.BlockSpec((None, K, ns), lambda i, j: (j + half, 0, 0))],
                 out_specs=[out_spec, out_spec, out_spec], out_shape=[out, out, out], args=[n, w_gu, w_gu],
                 vmem_limit=limit, carry=carry)


def _ffn_down_bwd(df, w_down, g, u, *, bm, name, carry=None):
    M, D = df.shape
    F = w_down.shape[0]
    dn = (((1,), (1,)), ((), ()))

    def body(df_ref, w_ref, g_ref, u_ref, o_ref):
        da = lax.dot_general(df_ref[...], w_ref[...], dn, preferred_element_type=F32)
        gv = g_ref[...].astype(F32)
        s = jax.nn.sigmoid(gv)
        o_ref[:, 0:F] = (da * u_ref[...].astype(F32) * (s * (1.0 + gv * (1.0 - s)))).astype(o_ref.dtype)
        o_ref[:, F:2 * F] = (da * (gv * s)).astype(o_ref.dtype)

    row = lambda w: pl.BlockSpec((bm, w), lambda i: (i, 0))
    limit = _vmem_limit(_nbytes((F, D), w_down.dtype), 2 * _nbytes((bm, F), BF16), _nbytes((bm, 2 * F), BF16),
                        2 * _nbytes((bm, F), F32))
    return _call(body, name=name, grid=(M // bm,),
                 in_specs=[row(D), pl.BlockSpec((F, D), lambda i: (0, 0)), row(F), row(F)],
                 out_specs=[row(2 * F)], out_shape=[jax.ShapeDtypeStruct((M, 2 * F), BF16)],
                 args=[df, w_down, g, u], vmem_limit=limit, carry=carry)[0]


def _ple_head(n4, w_gate, p, w_ple, h3, g_post, target, *, bm, name):
    T, D = h3.shape
    kp = p.shape[1]

    def body(n_ref, wg_ref, p_ref, wp_ref, h_ref, g_ref, t_ref, dh_ref, de_ref, dgp_ref, loss_ref, dg_ref):
        @pl.when(pl.program_id(0) == 0)
        def _():
            loss_ref[...] = jnp.zeros_like(loss_ref)
            dg_ref[...] = jnp.zeros_like(dg_ref)

        gate = jax.nn.sigmoid(jnp.dot(n_ref[...], wg_ref[...], preferred_element_type=F32))
        ev = jnp.dot(p_ref[...], wp_ref[...], preferred_element_type=F32)
        ge = gate * ev
        r = _rstd(ge)
        gh = ge * r
        gpost = g_ref[...]
        diff = h_ref[...] + gh * gpost - t_ref[...]
        loss_ref[...] += jnp.sum(diff * diff) * (0.5 / D)
        dh = diff * (1.0 / D)
        dh_ref[...] = dh
        dg_ref[...] += jnp.sum(dh * gh, axis=0, keepdims=True)
        dge = _norm_bwd(gh, r, dh * gpost)
        de_ref[...] = (dge * gate).astype(de_ref.dtype)
        dgp_ref[...] = (dge * ev * gate * (1.0 - gate)).astype(dgp_ref.dtype)

    row = lambda w: pl.BlockSpec((bm, w), lambda i: (i, 0))
    whole = lambda a: pl.BlockSpec(a.shape, lambda i: (0, 0), pipeline_mode=pl.Buffered(1))
    limit = _vmem_limit(_nbytes((bm, D), BF16), _nbytes(w_gate.shape, w_gate.dtype) // 2, 6 * _nbytes((bm, D), F32))
    return pl.pallas_call(
        body, name=name, grid=(T // bm,),
        in_specs=[row(D), whole(w_gate), row(kp), whole(w_ple), row(D), _vec_spec(D), row(D)],
        out_specs=(row(D), row(D), row(D), _vec_spec(LANES), _vec_spec(D)),
        out_shape=(jax.ShapeDtypeStruct((T, D), F32), jax.ShapeDtypeStruct((T, D), BF16),
                   jax.ShapeDtypeStruct((T, D), BF16), jax.ShapeDtypeStruct((1, LANES), F32),
                   jax.ShapeDtypeStruct((1, D), F32)),
        compiler_params=pltpu.CompilerParams(dimension_semantics=("arbitrary",), vmem_limit_bytes=limit),
    )(n4, w_gate, p, w_ple, h3, g_post, target)


def _t5_index(max_dist, stride):
    rho = np.arange(QBLK)[:, None]
    kap = np.arange(2 * QBLK)[None, :]
    d = rho + QBLK - kap
    valid = (d >= 0) & (d <= max_dist)
    n = np.maximum(d, 0) * stride
    max_exact = NUM_BUCKETS // 2
    nf = np.maximum(n, 1).astype(np.float32)
    large = max_exact + (np.log(nf / np.float32(max_exact)) / np.float32(math.log(MAX_DISTANCE / max_exact))
                         * np.float32(NUM_BUCKETS - max_exact)).astype(np.int32)
    large = np.minimum(large, NUM_BUCKETS - 1)
    bucket = np.where(n < max_exact, n, large)
    return np.where(valid, bucket, -1).astype(np.int32)


def _bias_grad(d_bias, idx, *, name):
    def body(db_ref, idx_ref, o_ref):
        h = pl.program_id(0)

        @pl.when(h == 0)
        def _():
            o_ref[...] = jnp.zeros_like(o_ref)

        idxv = idx_ref[...]
        dv = db_ref[0]
        rows = lax.broadcasted_iota(jnp.int32, (NUM_BUCKETS, LANES), 0)
        lanes = lax.broadcasted_iota(jnp.int32, (NUM_BUCKETS, LANES), 1)
        acc = o_ref[...]
        for b in range(NUM_BUCKETS):
            s = jnp.sum(jnp.where(idxv == b, dv, 0.0))
            acc = acc + jnp.where((rows == b) & (lanes == h), s, 0.0)
        o_ref[...] = acc

    return pl.pallas_call(
        body, name=name, grid=(2 * N_PAIRS,),
        in_specs=[pl.BlockSpec((1, QBLK, 2 * QBLK), lambda h: (h, 0, 0)),
                  pl.BlockSpec((QBLK, 2 * QBLK), lambda h: (0, 0))],
        out_specs=pl.BlockSpec((NUM_BUCKETS, LANES), lambda h: (0, 0)),
        out_shape=jax.ShapeDtypeStruct((NUM_BUCKETS, LANES), F32),
        compiler_params=pltpu.CompilerParams(dimension_semantics=("arbitrary",)),
    )(d_bias, idx)


def _lane():
    return lax.broadcasted_iota(jnp.int32, (1, LANES), 1)


def _head_sel(h):
    return (_lane() < HEAD_DIM) if h == 0 else (_lane() >= HEAD_DIM)


def _stat_lo():
    return (_lane() % HEAD_DIM) < (HEAD_DIM // 2)


def _stack_heads(t, scale=None):
    if scale is not None:
        t = t * scale
    zero = jnp.zeros_like(t)
    return jnp.concatenate([jnp.where(_head_sel(0), t, zero), jnp.where(_head_sel(1), t, zero)], axis=0)


def _class_spec(L, r, cols, stride, pps):
    chunks = N_PAIRS // pps
    mode = pl.Buffered(1) if r * chunks == 1 else None
    return pl.BlockSpec((L, MIX_W // chunks), lambda j, c: (0, (cols + j * stride) * chunks + c), pipeline_mode=mode)


def _rows(b, n_blocks=1):
    return pl.ds(pl.multiple_of(b * QBLK, QBLK), n_blocks * QBLK)


def _key_window(ref, b, cols, first):
    if first:
        blk = ref[0:QBLK, cols]
        return jnp.concatenate([blk, blk], axis=0)
    return ref[_rows(b - 1, 2), cols]


def _band_fwd(qa, ka, va, bias, *, r, q_col, k_col, v_col, stride, pattern, name, carry=None):
    L = qa.shape[0]
    nb = L // QBLK
    dn = (((1,), (1,)), ((), ()))
    scale = HEAD_DIM ** -0.5

    pps = N_PAIRS

    def body(q_ref, k_ref, v_ref, bias_ref, o_ref, lse_ref):
        sel0 = _head_sel(0)
        pair0 = pl.program_id(1) * pps

        def block(b, first):
            rows = _rows(b)
            for i in range(pps):
                cols = slice(i * LANES, (i + 1) * LANES)
                q2 = _stack_heads(q_ref[rows, cols], scale)
                k = _key_window(k_ref, b, cols, first)
                v = _key_window(v_ref, b, cols, first)
                bias2 = bias_ref[1 if first else 0, pl.ds(2 * (pair0 + i), 2)].reshape(2 * QBLK, 2 * QBLK)
                s = lax.dot_general(q2, k, dn, preferred_element_type=F32) + bias2
                m = jnp.max(s, axis=1, keepdims=True)
                p = jnp.exp(s - m)
                l = jnp.sum(p, axis=1, keepdims=True)
                o = jnp.dot(p.astype(v.dtype), v, preferred_element_type=F32) / l
                lse = m + jnp.log(l)
                o_ref[rows, cols] = jnp.where(sel0, o[0:QBLK], o[QBLK:]).astype(o_ref.dtype)
                lse_ref[rows, cols] = jnp.where(sel0, lse[0:QBLK], lse[QBLK:])

        block(0, True)
        if nb > 1:
            pl.loop(1, nb)(lambda b: block(b, False))

    bias_spec = pl.BlockSpec((None, 2, 2 * N_PAIRS, QBLK, 2 * QBLK), lambda j, c: (pattern, 0, 0, 0, 0))
    out_spec = _class_spec(L, r, 0, 1, pps)
    blk_bytes = _nbytes((L, pps * LANES), F32)
    return _call(
        body, name=name, grid=(r, N_PAIRS // pps),
        in_specs=[_class_spec(L, r, q_col, stride, pps), _class_spec(L, r, k_col, stride, pps),
                  _class_spec(L, r, v_col, stride, pps), bias_spec],
        out_specs=[out_spec, out_spec],
        out_shape=[jax.ShapeDtypeStruct((L, r * MIX_W), BF16), jax.ShapeDtypeStruct((L, r * MIX_W), F32)],
        args=[qa, ka, va, bias], vmem_limit=_vmem_limit(*([blk_bytes] * 4)), carry=carry)


def _merge(branches, sink, *, name):
    T, W = branches[0][0].shape
    nbr = len(branches)

    def body(*refs):
        sink_ref = refs[2 * nbr]
        out_ref, outb_ref, lse_ref = refs[2 * nbr + 1:]
        sk = sink_ref[...]
        mx = sk
        for t in range(nbr):
            mx = jnp.maximum(mx, refs[2 * t + 1][...])
        den = jnp.exp(sk - mx)
        num = jnp.zeros_like(mx)
        for t in range(nbr):
            w = jnp.exp(refs[2 * t + 1][...] - mx)
            den = den + w
            num = num + w * refs[2 * t][...].astype(F32)
        out = num / den
        out_ref[...] = out
        outb_ref[...] = out.astype(outb_ref.dtype)
        lse_ref[...] = mx + jnp.log(den)

    args = [a for br in branches for a in br] + [sink]
    return pl.pallas_call(
        body, name=name, grid=(T // ROW_BLK,),
        in_specs=[_row_spec(W)] * (2 * nbr) + [_vec_spec(W)],
        out_specs=(_row_spec(W), _row_spec(W), _row_spec(W)),
        out_shape=(jax.ShapeDtypeStruct((T, W), F32), jax.ShapeDtypeStruct((T, W), BF16),
                   jax.ShapeDtypeStruct((T, W), F32)),
        compiler_params=pltpu.CompilerParams(dimension_semantics=("parallel",)),
    )(*args)


def _attn_delta(dmix, outs, lses, sink, *, name):
    T = dmix.shape[0]
    W = MIX_W

    def body(dmix_ref, oa_ref, ob_ref, la_ref, lb_ref, sink_ref, doa_ref, dob_ref, sta_ref, stb_ref, dsink_ref):
        @pl.when(pl.program_id(0) == 0)
        def _():
            dsink_ref[...] = jnp.zeros_like(dsink_ref)

        sel0, lo = _head_sel(0), _stat_lo()
        for mixer, (o_ref, l_ref, do_ref, st_ref) in enumerate(
                ((oa_ref, la_ref, doa_ref, sta_ref), (ob_ref, lb_ref, dob_ref, stb_ref))):
            for i in range(N_PAIRS):
                cols = slice(i * LANES, (i + 1) * LANES)
                do = dmix_ref[:, mixer * W + i * LANES:mixer * W + (i + 1) * LANES]
                prod = do * o_ref[:, cols]
                d0 = jnp.sum(jnp.where(sel0, prod, 0.0), axis=1, keepdims=True)
                d1 = jnp.sum(jnp.where(sel0, 0.0, prod), axis=1, keepdims=True)
                delta = jnp.where(sel0, d0, d1)
                lse = l_ref[:, cols]
                st_ref[:, cols] = jnp.where(lo, lse, delta)
                do_ref[:, cols] = do.astype(do_ref.dtype)
                if mixer == 0:
                    dsink_ref[:, cols] += -jnp.sum(jnp.exp(sink_ref[:, cols] - lse) * delta, axis=0, keepdims=True)

    return pl.pallas_call(
        body, name=name, grid=(T // ROW_BLK,),
        in_specs=[_row_spec(2 * W), _row_spec(W), _row_spec(W), _row_spec(W), _row_spec(W), _vec_spec(W)],
        out_specs=(_row_spec(W), _row_spec(W), _row_spec(W), _row_spec(W), _vec_spec(W)),
        out_shape=(jax.ShapeDtypeStruct((T, W), BF16), jax.ShapeDtypeStruct((T, W), BF16),
                   jax.ShapeDtypeStruct((T, W), F32), jax.ShapeDtypeStruct((T, W), F32),
                   jax.ShapeDtypeStruct((1, W), F32)),
        compiler_params=pltpu.CompilerParams(dimension_semantics=("arbitrary",)),
    )(dmix, outs[0], outs[1], lses[0], lses[1], sink)


def _band_bwd(qa, ka, va, d_out, stat, bias, *, r, q_col, k_col, v_col, stride, pattern, name, carry=None):
    L = qa.shape[0]
    nb = L // QBLK
    dn_nt = (((1,), (1,)), ((), ()))
    dn_tn = (((0,), (0,)), ((), ()))
    scale = HEAD_DIM ** -0.5

    pps = N_PAIRS if r > 1 else N_PAIRS // 2

    def body(q_ref, k_ref, v_ref, do_ref, st_ref, bias_ref, dq_ref, dk_ref, dv_ref, db_ref, dk_carry, dv_carry):
        @pl.when((pl.program_id(0) == 0) & (pl.program_id(1) == 0))
        def _():
            db_ref[...] = jnp.zeros_like(db_ref)

        lo, sel0 = _stat_lo(), _head_sel(0)
        pair0 = pl.program_id(1) * pps

        def block(b, first):
            rows = _rows(b)
            for i in range(pps):
                heads = pl.ds(2 * (pair0 + i), 2)
                cols = slice(i * LANES, (i + 1) * LANES)
                q2 = _stack_heads(q_ref[rows, cols], scale)
                k = _key_window(k_ref, b, cols, first)
                v = _key_window(v_ref, b, cols, first)
                do2 = _stack_heads(do_ref[rows, cols])
                st = st_ref[rows, cols]
                lse2 = jnp.concatenate(
                    [jnp.max(jnp.where(_head_sel(h) & lo, st, -jnp.inf), axis=1, keepdims=True) for h in range(2)], axis=0)
                delta2 = jnp.concatenate(
                    [jnp.sum(jnp.where(_head_sel(h) & ~lo, st, 0.0), axis=1, keepdims=True) for h in range(2)],
                    axis=0) * (2.0 / HEAD_DIM)
                bias2 = bias_ref[1 if first else 0, heads].reshape(2 * QBLK, 2 * QBLK)
                s = lax.dot_general(q2, k, dn_nt, preferred_element_type=F32) + bias2
                p = jnp.exp(s - lse2)
                dp = lax.dot_general(do2, v, dn_nt, preferred_element_type=F32)
                ds = p * (dp - delta2)
                db_ref[heads] += ds.reshape(2, QBLK, 2 * QBLK)
                dsb = ds.astype(k.dtype)
                dq2 = jnp.dot(dsb, k, preferred_element_type=F32)
                dq_ref[rows, cols] = (jnp.where(sel0, dq2[0:QBLK], dq2[QBLK:]) * scale).astype(dq_ref.dtype)
                dk_acc = lax.dot_general(dsb, q2, dn_tn, preferred_element_type=F32)
                dv_acc = lax.dot_general(p.astype(k.dtype), do2, dn_tn, preferred_element_type=F32)
                if not first:
                    prev = _rows(b - 1)
                    dk_ref[prev, cols] = (dk_carry[:, cols] + dk_acc[0:QBLK]).astype(dk_ref.dtype)
                    dv_ref[prev, cols] = (dv_carry[:, cols] + dv_acc[0:QBLK]).astype(dv_ref.dtype)
                dk_carry[:, cols] = dk_acc[QBLK:2 * QBLK]
                dv_carry[:, cols] = dv_acc[QBLK:2 * QBLK]

        block(0, True)
        if nb > 1:
            pl.loop(1, nb)(lambda b: block(b, False))

        last = _rows(nb - 1)
        dk_ref[last, :] = dk_carry[...].astype(dk_ref.dtype)
        dv_ref[last, :] = dv_carry[...].astype(dv_ref.dtype)

    tok_spec = _class_spec(L, r, 0, 1, pps)
    bias_spec = pl.BlockSpec((None, 2, 2 * N_PAIRS, QBLK, 2 * QBLK), lambda j, c: (pattern, 0, 0, 0, 0))
    dbias_spec = pl.BlockSpec((2 * N_PAIRS, QBLK, 2 * QBLK), lambda j, c: (0, 0, 0))
    grad = jax.ShapeDtypeStruct((L, r * MIX_W), BF16)
    blk_bytes = _nbytes((L, pps * LANES), BF16)
    carry_shape = pltpu.VMEM((QBLK, pps * LANES), F32)
    return _call(
        body, name=name, grid=(r, N_PAIRS // pps),
        in_specs=[_class_spec(L, r, q_col, stride, pps), _class_spec(L, r, k_col, stride, pps),
                  _class_spec(L, r, v_col, stride, pps), tok_spec, tok_spec, bias_spec],
        out_specs=[tok_spec, tok_spec, tok_spec, dbias_spec],
        out_shape=[grad, grad, grad, jax.ShapeDtypeStruct((2 * N_PAIRS, QBLK, 2 * QBLK), F32)],
        args=[qa, ka, va, d_out, stat, bias], scratch=[carry_shape, carry_shape],
        vmem_limit=_vmem_limit(*([blk_bytes] * 9)), semantics=("arbitrary", "arbitrary"), carry=carry)


def _dz_assemble(dq_a, dk_a, dv_a, b_grads, *, name):
    T = dq_a.shape[0]
    W = MIX_W

    def group_sum(x):
        u0 = x[:, 0:LANES] + x[:, LANES:2 * LANES]
        u1 = x[:, 2 * LANES:3 * LANES] + x[:, 3 * LANES:4 * LANES]
        s0 = u0 + pltpu.roll(u0, HEAD_DIM, 1)
        s1 = u1 + pltpu.roll(u1, HEAD_DIM, 1)
        return jnp.where(_head_sel(0), s0, s1)

    def body(*refs):
        dqa_ref, dka_ref, dva_ref = refs[0:3]
        b_refs = refs[3:12]
        dza_ref, dzb_ref, sa_ref, sb_ref = refs[12:]

        @pl.when(pl.program_id(0) == 0)
        def _():
            sa_ref[...] = jnp.zeros_like(sa_ref)
            sb_ref[...] = jnp.zeros_like(sb_ref)

        def put(dst, acc, col, part):
            w = part.shape[1]
            dst[:, col:col + w] = part.astype(dst.dtype)
            acc[:, col:col + w] += jnp.sum(part, axis=0, keepdims=True)

        put(dza_ref, sa_ref, 0, dqa_ref[...].astype(F32))
        put(dza_ref, sa_ref, W, group_sum(dka_ref[...].astype(F32)))
        put(dza_ref, sa_ref, W + LANES, group_sum(dva_ref[...].astype(F32)))
        for t in range(3):
            part = (b_refs[t][...].astype(F32) + b_refs[3 + t][...].astype(F32) + b_refs[6 + t][...].astype(F32))
            put(dzb_ref, sb_ref, t * W, part)

    args = [dq_a, dk_a, dv_a] + [g[t] for g in b_grads for t in range(3)]
    return pl.pallas_call(
        body, name=name, grid=(T // ROW_BLK,),
        in_specs=[_row_spec(W)] * 12,
        out_specs=(_row_spec(ZA_W), _row_spec(ZB_W), _vec_spec(ZA_W), _vec_spec(ZB_W)),
        out_shape=(jax.ShapeDtypeStruct((T, ZA_W), BF16), jax.ShapeDtypeStruct((T, ZB_W), BF16),
                   jax.ShapeDtypeStruct((1, ZA_W), F32), jax.ShapeDtypeStruct((1, ZB_W), F32)),
        compiler_params=pltpu.CompilerParams(dimension_semantics=("arbitrary",)),
    )(*args)


def _place():
    x, y, c = lax.axis_index("x"), lax.axis_index("y"), lax.axis_index("c")
    chips = [(1 - x, y), (x, 1 - y), (1 - x, 1 - y)]
    return x, y, c, chips


def _cast_place(shard, k_idx, *, name):
    rs, cs = shard.shape
    rb = _row_block(rs, 256, 16)
    nblk = rs // rb

    def body(k_ref, s_ref, o_ref):
        o_ref[...] = s_ref[...].astype(o_ref.dtype)

    grid_spec = pltpu.PrefetchScalarGridSpec(
        num_scalar_prefetch=1, grid=(nblk,),
        in_specs=[pl.BlockSpec((rb, cs), lambda t, k_ref: (t, 0))],
        out_specs=pl.BlockSpec((rb, cs), lambda t, k_ref: (k_ref[0] * nblk + t, 0)))
    return pl.pallas_call(
        body, name=name, grid_spec=grid_spec, out_shape=jax.ShapeDtypeStruct((N_CHIPS * rs, cs), BF16),
        compiler_params=pltpu.CompilerParams(dimension_semantics=("parallel",)),
    )(k_idx, shard)


def _gather_carry(fulls, part=(0, 1)):
    n = len(fulls)
    p_idx, n_parts = part

    def piece(ref, chip_idx, half):
        rs = ref.shape[0] // N_CHIPS
        rh = rs // 2
        rows = rh // n_parts
        return ref.at[pl.ds(chip_idx * rs + half * rh + p_idx * rows, rows)]

    def copy(sems, sem, src_ref, dst_ref, to):
        return pltpu.make_async_remote_copy(src_ref=src_ref, dst_ref=dst_ref, send_sem=sems[0].at[sem],
                                            recv_sem=sems[1].at[sem], device_id=to, device_id_type=MESH)

    def ici_copy(ins, outs, sems, i, j, chip, k, c):
        return copy(sems, 3 * i + j, piece(ins[i], k, c), piece(outs[i], k, c), (*chip, c))

    def start(ins, outs, sems):
        x, y, c, chips = _place()
        for i in range(n):
            for j, chip in enumerate(chips):
                ici_copy(ins, outs, sems, i, j, chip, 2 * x + y, c).start()

    def finish(ins, outs, sems):
        x, y, c, chips = _place()
        sibling = (x, y, 1 - c)
        passed = []
        for i in range(n):
            for j, chip in enumerate(chips):
                region = piece(outs[i], 2 * chip[0] + chip[1], c)
                copy(sems, 3 * i + j, region, region, (*chip, c)).wait_recv()
                cp = copy(sems, 3 * n + 3 * i + j, region, region, sibling)
                cp.start()
                passed.append(cp)
        for i in range(n):
            for j, chip in enumerate(chips):
                region = piece(outs[i], 2 * chip[0] + chip[1], 1 - c)
                copy(sems, 3 * n + 3 * i + j, region, region, sibling).wait_recv()
        for i in range(n):
            for j, chip in enumerate(chips):
                ici_copy(ins, outs, sems, i, j, chip, 2 * x + y, c).wait_send()
        for cp in passed:
            cp.wait_send()

    return _Carry(fulls, [jax.ShapeDtypeStruct(a.shape, a.dtype) for a in fulls], {i: i for i in range(n)},
                  [pltpu.SemaphoreType.DMA((6 * n,)), pltpu.SemaphoreType.DMA((6 * n,))], start, finish)


def _pair_swap_carry(grads):
    n = len(grads)

    def copy(ins, outs, sems, i):
        x, y, c, _ = _place()
        return pltpu.make_async_remote_copy(src_ref=ins[i].at[:, 1 - c], dst_ref=outs[i], send_sem=sems[0].at[i],
                                            recv_sem=sems[1].at[i], device_id=(x, y, 1 - c), device_id_type=MESH)

    def start(ins, outs, sems):
        for i in range(n):
            copy(ins, outs, sems, i).start()

    def finish(ins, outs, sems):
        for i in range(n):
            copy(ins, outs, sems, i).wait()

    return _Carry(grads, [jax.ShapeDtypeStruct((a.shape[0], a.shape[2], a.shape[3]), a.dtype) for a in grads], {},
                  [pltpu.SemaphoreType.DMA((n,)), pltpu.SemaphoreType.DMA((n,))], start, finish)


def _pair_sum(g4, got, c_idx, *, name):
    _, _, rh, cs = g4.shape
    rb = _row_block(rh, 256, 16)

    def body(c_ref, own_ref, got_ref, o_ref):
        o_ref[...] = (own_ref[...].astype(F32) + got_ref[...].astype(F32)).astype(o_ref.dtype)

    grid_spec = pltpu.PrefetchScalarGridSpec(
        num_scalar_prefetch=1, grid=(N_CHIPS, rh // rb),
        in_specs=[pl.BlockSpec((None, None, rb, cs), lambda k, t, c_ref: (k, c_ref[0], t, 0)),
                  pl.BlockSpec((None, rb, cs), lambda k, t, c_ref: (k, t, 0))],
        out_specs=pl.BlockSpec((None, rb, cs), lambda k, t, c_ref: (k, t, 0)))
    return pl.pallas_call(
        body, name=name, grid_spec=grid_spec, out_shape=jax.ShapeDtypeStruct((N_CHIPS, rh, cs), BF16),
        compiler_params=pltpu.CompilerParams(dimension_semantics=("parallel", "parallel")),
    )(c_idx, g4, got)


def _chip_scatter_carry(sums):
    n = len(sums)

    def copies(ins, outs, sems):
        x, y, c, chips = _place()
        return [pltpu.make_async_remote_copy(src_ref=ins[i].at[2 * chip[0] + chip[1]], dst_ref=outs[i].at[j],
                                             send_sem=sems[0].at[3 * i + j], recv_sem=sems[1].at[3 * i + j],
                                             device_id=(*chip, c), device_id_type=MESH)
                for i in range(n) for j, chip in enumerate(chips)]

    def start(ins, outs, sems):
        for cp in copies(ins, outs, sems):
            cp.start()

    def finish(ins, outs, sems):
        for cp in copies(ins, outs, sems):
            cp.wait()

    return _Carry(sums, [jax.ShapeDtypeStruct((3,) + a.shape[1:], a.dtype) for a in sums], {},
                  [pltpu.SemaphoreType.DMA((3 * n,)), pltpu.SemaphoreType.DMA((3 * n,))], start, finish)


def _chip_sum(sums, got, kc_idx, *, name):
    _, rh, cs = sums.shape
    rb = _row_block(rh, 256, 16)
    nblk = rh // rb

    def body(kc_ref, own_ref, got_ref, o_ref):
        acc = own_ref[...].astype(F32)
        for j in range(3):
            acc = acc + got_ref[j].astype(F32)
        o_ref[...] = acc

    grid_spec = pltpu.PrefetchScalarGridSpec(
        num_scalar_prefetch=1, grid=(nblk,),
        in_specs=[pl.BlockSpec((None, rb, cs), lambda t, kc: (kc[0], t, 0)),
                  pl.BlockSpec((3, rb, cs), lambda t, kc: (0, t, 0))],
        out_specs=pl.BlockSpec((rb, cs), lambda t, kc: (kc[1] * nblk + t, 0)))
    return pl.pallas_call(
        body, name=name, grid_spec=grid_spec, out_shape=jax.ShapeDtypeStruct((2 * rh, cs), F32),
        compiler_params=pltpu.CompilerParams(dimension_semantics=("parallel",)),
    )(kc_idx, sums, got)


def _half_swap_carry(shards):
    n = len(shards)

    def copy(ins, outs, sems, i, to_my_half):
        x, y, c, _ = _place()
        rh = ins[i].shape[0] // 2
        half = c if to_my_half else 1 - c
        return pltpu.make_async_remote_copy(
            src_ref=ins[i].at[pl.ds(c * rh, rh)], dst_ref=outs[i].at[pl.ds(half * rh, rh)],
            send_sem=sems[0].at[i], recv_sem=sems[1].at[i], device_id=(x, y, 1 - c), device_id_type=MESH)

    def start(ins, outs, sems):
        for i in range(n):
            copy(ins, outs, sems, i, True).start()

    def finish(ins, outs, sems):
        for i in range(n):
            copy(ins, outs, sems, i, False).wait_recv()
        for i in range(n):
            copy(ins, outs, sems, i, True).wait_send()

    return _Carry(shards, [jax.ShapeDtypeStruct(a.shape, a.dtype) for a in shards], {i: i for i in range(n)},
                  [pltpu.SemaphoreType.DMA((n,)), pltpu.SemaphoreType.DMA((n,))], start, finish)


SMALL_ROW = 1024


def _small_layout(shapes):
    starts, row = [], 0
    for r, c in shapes:
        starts.append(row)
        row += -(-c // SMALL_ROW) if r == 1 else r
    return starts, -(-row // SUBLANES) * SUBLANES


def _small_pieces(shape, start):
    r, c = shape
    if r == 1:
        return [(start + q, min(SMALL_ROW, c - q * SMALL_ROW), q * SMALL_ROW) for q in range(-(-c // SMALL_ROW))]
    return None


def _small_all_reduce(parts, *, name):
    shapes = [a.shape for a in parts]
    starts, rows = _small_layout(shapes)
    n = len(parts)

    def body(*refs):
        ins, outs = refs[0:n], refs[n:2 * n]
        buf, send_sems, recv_sems = refs[2 * n:]
        x, y, c, _ = _place()
        me = 4 * x + 2 * y + c
        mine = buf.at[me]
        mine[...] = jnp.zeros((rows, SMALL_ROW), F32)
        for ref, shape, start in zip(ins, shapes, starts):
            pieces = _small_pieces(shape, start)
            if pieces is None:
                mine[start:start + shape[0], 0:shape[1]] = ref[...]
            else:
                for row, width, col in pieces:
                    mine[row:row + 1, 0:width] = ref[:, col:col + width]
        copies = []
        for d in range(1, 8):
            to = (1 - x if d & 4 else x, 1 - y if d & 2 else y, 1 - c if d & 1 else c)
            cp = pltpu.make_async_remote_copy(src_ref=mine, dst_ref=mine, send_sem=send_sems.at[d - 1],
                                              recv_sem=recv_sems.at[d - 1], device_id=to, device_id_type=MESH)
            cp.start()
            copies.append((cp, to))
        for d in range(1, 8):
            to = copies[d - 1][1]
            sender = 4 * to[0] + 2 * to[1] + to[2]
            pltpu.make_async_remote_copy(src_ref=mine, dst_ref=buf.at[sender], send_sem=send_sems.at[d - 1],
                                         recv_sem=recv_sems.at[d - 1], device_id=to, device_id_type=MESH).wait_recv()
        for cp, _ in copies:
            cp.wait_send()
        acc = buf[0]
        for s in range(1, 8):
            acc = acc + buf[s]
        mine[...] = acc
        for ref, shape, start in zip(outs, shapes, starts):
            pieces = _small_pieces(shape, start)
            if pieces is None:
                ref[...] = mine[start:start + shape[0], 0:shape[1]]
            else:
                for row, width, col in pieces:
                    ref[:, col:col + width] = mine[row:row + 1, 0:width]

    vm = pl.BlockSpec(memory_space=pltpu.VMEM)
    return pl.pallas_call(
        body, name=name, in_specs=[vm] * n, out_specs=tuple([vm] * n),
        out_shape=tuple(jax.ShapeDtypeStruct(s, F32) for s in shapes),
        scratch_shapes=[pltpu.VMEM((8, rows, SMALL_ROW), F32), pltpu.SemaphoreType.DMA((7,)),
                        pltpu.SemaphoreType.DMA((7,))],
    )(*parts)


def _adamw_small(ws, gs, ms, vs, *, name):
    n = len(ws)
    c1 = 1.0 - ADAM_B1 ** ADAM_STEP
    c2 = 1.0 - ADAM_B2 ** ADAM_STEP

    def body(*refs):
        for k in range(n):
            w_ref, g_ref, m_ref, v_ref = (refs[t * n + k] for t in range(4))
            d_ref, mo_ref, vo_ref = (refs[(4 + t) * n + k] for t in range(3))
            gv = g_ref[...]
            mn = ADAM_B1 * m_ref[...] + (1.0 - ADAM_B1) * gv
            vn = ADAM_B2 * v_ref[...] + (1.0 - ADAM_B2) * (gv * gv)
            d_ref[...] = -ADAM_LR * ((mn / c1) / (jnp.sqrt(vn / c2) + ADAM_EPS) + ADAM_WD * w_ref[...])
            mo_ref[...] = mn
            vo_ref[...] = vn

    vm = pl.BlockSpec(memory_space=pltpu.VMEM)
    shapes = tuple(jax.ShapeDtypeStruct(a.shape, F32) for a in ws)
    res = pl.pallas_call(
        body, name=name, in_specs=[vm] * (4 * n), out_specs=tuple([vm] * (3 * n)), out_shape=shapes * 3,
    )(*ws, *gs, *ms, *vs)
    return res[0:n], res[n:2 * n], res[2 * n:3 * n]


def _adamw(w, g, m, v, *, name):
    R, C = w.shape
    rb = _row_block(R, 256, SUBLANES)
    c1 = 1.0 - ADAM_B1 ** ADAM_STEP
    c2 = 1.0 - ADAM_B2 ** ADAM_STEP

    def body(w_ref, g_ref, m_ref, v_ref, d_ref, mo_ref, vo_ref):
        gv = g_ref[...]
        mn = ADAM_B1 * m_ref[...] + (1.0 - ADAM_B1) * gv
        vn = ADAM_B2 * v_ref[...] + (1.0 - ADAM_B2) * (gv * gv)
        d_ref[...] = -ADAM_LR * ((mn / c1) / (jnp.sqrt(vn / c2) + ADAM_EPS) + ADAM_WD * w_ref[...])
        mo_ref[...] = mn
        vo_ref[...] = vn

    spec = pl.BlockSpec((rb, C), lambda i: (i, 0))
    shp = jax.ShapeDtypeStruct((R, C), F32)
    return pl.pallas_call(
        body, name=name, grid=(R // rb,), in_specs=[spec] * 4, out_specs=(spec, spec, spec),
        out_shape=(shp, shp, shp), compiler_params=pltpu.CompilerParams(dimension_semantics=("parallel",)),
    )(w, g, m, v)


def _local_step(x, p, target, small, sched):
    T = x.shape[0]
    W = MIX_W
    rel_bias = small["rel_bias"]
    b_in_a, b_in_b = small["b_in"][:, 0:ZA_W], small["b_in"][:, ZA_W:]

    idx_np = [_t5_index(A_WINDOW - 1, 1)] + [_t5_index(window // dil, dil) for window, dil in B_PATTERNS]
    idx_all = jnp.asarray(np.stack(idx_np))
    n1, bias_all = sched.run(_prologue, x, small["ffn1_pre_g"], rel_bias, idx_all, name="prologue")
    w_gu1 = sched.weight("ffn1_w_gu")
    *gu1, a1 = sched.run(_ffn_up, n1, w_gu1, bm=512, name="ffn1_gu")
    w_d1 = sched.weight("ffn1_w_down")
    f1, h1, n2 = sched.run(_mm_resid_norm, a1, w_d1, x, small["ffn1_post_g"], 0.5, small["attn_pre_g"], bm=512,
                           name="ffn1_down")
    win_a, win_b = sched.weight("w_in")
    za = _mm_nn(n2, win_a, bm=1024, bn=ZA_W, out_dtype=BF16, name="in_proj_a", bias=b_in_a)
    zb = _mm_nn(n2, win_b, bm=1024, bn=ZB_W // 2, out_dtype=BF16, name="in_proj_b", bias=b_in_b)

    k_a = za[:, W:W + LANES].reshape(T, 2, 1, HEAD_DIM)
    v_a = za[:, W + LANES:W + 2 * LANES].reshape(T, 2, 1, HEAD_DIM)
    k_a = jnp.broadcast_to(k_a, (T, 2, 4, HEAD_DIM)).reshape(T, W)
    v_a = jnp.broadcast_to(v_a, (T, 2, 4, HEAD_DIM)).reshape(T, W)
    a_args = dict(r=1, q_col=0, k_col=0, v_col=0, stride=0, pattern=0)
    o_a, lse_a1 = sched.run(_band_fwd, za, k_a, v_a, bias_all, name="band_a_fwd", **a_args)
    sink_row = jnp.repeat(small["sinks"], HEAD_DIM, axis=1)
    out_a, out_a_bf, lse_a = _merge([(o_a, lse_a1)], sink_row, name="merge_a")

    no_sink = jnp.full((1, W), NEG, F32)
    b_ctx, branches = [], []
    for t, (window, dil) in enumerate(B_PATTERNS):
        zb_r = zb.reshape(T // dil, dil * ZB_W)
        args = dict(r=dil, q_col=0, k_col=1, v_col=2, stride=ZB_W // W, pattern=1 + t)
        o, l = sched.run(_band_fwd, zb_r, zb_r, zb_r, bias_all, name=f"band_b{t}_fwd", **args)
        branches.append((o.reshape(T, W), l.reshape(T, W)))
        b_ctx.append((jnp.asarray(idx_np[1 + t]), zb_r, args, dil))
    out_b, out_b_bf, lse_b = _merge(branches, no_sink, name="merge_b")

    mix = jnp.concatenate([out_a_bf, out_b_bf], axis=1)
    w_out = sched.weight("w_out")
    att, h2, n3 = _mm_resid_norm(mix, w_out, h1, small["attn_post_g"], 1.0, small["ffn2_pre_g"], bm=512,
                                 name="out_proj", bias=small["b_out"])
    w_gu2, w_d2 = sched.weight("ffn2_w_gu"), sched.weight("ffn2_w_down")
    *gu2, a2 = _ffn_up(n3, w_gu2, bm=512, name="ffn2_gu")
    f2, h3, n4 = _mm_resid_norm(a2, w_d2, h2, small["ffn2_post_g"], 0.5, small["ple_pre_g"], bm=512,
                                name="ffn2_down")
    w_gate = sched.weight("w_ple_gate")
    p_bf = p.astype(BF16)
    dh4, de, dgp, loss, d_ple_post = _ple_head(n4, w_gate, p_bf, sched.weight("w_ple_proj"), h3, small["ple_post_g"],
                                               target, bm=512, name="ple_head")

    gs = {"ple_post_g": d_ple_post}
    sched.grad("w_ple_proj", _mm_tn(p_bf, de, bm=256, bn=1024, out_dtype=BF16, name="d_w_ple"))
    sched.grad("w_ple_gate", _mm_tn(n4, dgp, bm=512, bn=1024, out_dtype=BF16, name="d_w_gate"))
    dh3, df2, gs["ple_pre_g"], gs["ffn2_post_g"], _ = sched.run(
        _mm_nt_norms_bwd, dgp, w_gate, dh4, h3, small["ple_pre_g"], f2, small["ffn2_post_g"], 0.5, bm=512, name="d_n4")

    def ffn_bwd(df, a, gu, n, w_down, w_gu, tag, *norm_args):
        dgu = sched.run(_ffn_down_bwd, df, w_down, gu[0], gu[1], bm=256, name=f"ffn{tag}_d_a")
        sched.grad(f"ffn{tag}_w_down", _mm_tn(a, df, bm=256, bn=1024, out_dtype=BF16, name=f"ffn{tag}_d_w_down"))
        sched.grad(f"ffn{tag}_w_gu", sched.run(_mm_tn, n, dgu, bm=512, bn=1408, out_dtype=BF16,
                                              name=f"ffn{tag}_d_w_gu", n_stack=N_CHIPS))
        return sched.run(_mm_nt_norms_bwd, dgu, w_gu, *norm_args, bm=512, name=f"ffn{tag}_d_n")

    dh2, datt, gs["ffn2_pre_g"], gs["attn_post_g"], gs["b_out"] = ffn_bwd(
        df2, a2, gu2, n3, w_d2, w_gu2, "2", dh3, h2, small["ffn2_pre_g"], att, small["attn_post_g"], 1.0)
    sched.grad("w_out", _mm_tn(mix, datt, bm=512, bn=1024, out_dtype=BF16, name="d_w_out"))
    dmix = sched.run(_mm_nt, datt, w_out, bm=1024, bn=1024, out_dtype=F32, name="d_mix")

    do_a, do_b, stat_a, stat_b, dsink = _attn_delta(dmix, (out_a, out_b), (lse_a, lse_b), sink_row, name="attn_delta")
    gs["sinks"] = dsink[:, ::HEAD_DIM]
    dq_a, dk_a, dv_a, dbias_a = sched.run(_band_bwd, za, k_a, v_a, do_a, stat_a, bias_all, name="band_a_bwd", **a_args)
    d_rel_a = _bias_grad(dbias_a, jnp.asarray(idx_np[0]), name="d_rel_a")
    b_grads = []
    d_rel_b = None
    for t, (idx, zb_r, args, dil) in enumerate(b_ctx):
        shp = (T // dil, dil * W)
        dq, dk, dv, dbias = sched.run(_band_bwd, zb_r, zb_r, zb_r, do_b.reshape(shp), stat_b.reshape(shp), bias_all,
                                      name=f"band_b{t}_bwd", **args)
        b_grads.append((dq.reshape(T, W), dk.reshape(T, W), dv.reshape(T, W)))
        d_rel = _bias_grad(dbias, idx, name=f"d_rel_b{t}")
        d_rel_b = d_rel if d_rel_b is None else d_rel_b + d_rel
    gs["rel_bias"] = jnp.concatenate([d_rel_a[:, 0:2 * N_PAIRS], d_rel_b[:, 0:2 * N_PAIRS]], axis=1)
    dza, dzb, dsum_a, dsum_b = _dz_assemble(dq_a, dk_a, dv_a, b_grads, name="d_z")
    gs["b_in"] = jnp.concatenate([dsum_a, dsum_b], axis=1)
    sched.grad("w_in", (_mm_tn(n2, dza, bm=512, bn=ZA_W, out_dtype=BF16, name="d_w_in_a"),
                        _mm_tn(n2, dzb, bm=512, bn=ZB_W // 2, out_dtype=BF16, name="d_w_in_b")))
    dn2_a = _mm_nt(dza, win_a, bm=1024, bn=1024, out_dtype=F32, name="d_n2_a")
    dh1, df1, gs["attn_pre_g"], gs["ffn1_post_g"], _ = sched.run(
        _mm_nt_norms_bwd, dzb, win_b, dh2, h1, small["attn_pre_g"], f1, small["ffn1_post_g"], 0.5, bm=512,
        name="d_n2_b", add=dn2_a)
    grad_x, gs["ffn1_pre_g"] = ffn_bwd(df1, a1, gu1, n1, w_d1, w_gu1, "1", dh1, x, small["ffn1_pre_g"], None, None, 0.0)
    return loss, grad_x, gs


def _to_compute(name, full):
    st = full.reshape(N_CHIPS, full.shape[0] // N_CHIPS, full.shape[1])
    if name in ("ffn1_w_gu", "ffn2_w_gu"):
        return st
    if name in ("w_in", "w_ple_proj"):
        whole = jnp.transpose(st, (1, 0, 2)).reshape(st.shape[1], -1)
        return (whole[:, 0:ZA_W], whole[:, ZA_W:]) if name == "w_in" else whole
    return full


def _to_comm(name, g):
    if name in ("w_in", "w_ple_proj"):
        whole = jnp.concatenate(g, axis=1) if name == "w_in" else g
        g = jnp.transpose(whole.reshape(whole.shape[0], N_CHIPS, -1), (1, 0, 2))
    rs = g.shape[1] if g.ndim == 3 else g.shape[0] // N_CHIPS
    return g.reshape(N_CHIPS, 2, rs // 2, g.shape[-1])


class _Schedule:
    GATHER = {
        "prologue": [("ffn1_w_gu", (0, 1))],
        "ffn1_gu": [("ffn1_w_down", (0, 1)), ("w_in", (0, 1))],
        "ffn1_down": [("w_out", (0, 1))],
        "band_a_fwd": [("ffn2_w_gu", (0, 2))],
        "band_b0_fwd": [("ffn2_w_gu", (1, 2))],
        "band_b1_fwd": [("ffn2_w_down", (0, 1))],
        "band_b2_fwd": [("w_ple_gate", (0, 1)), ("w_ple_proj", (0, 1))],
    }
    SWAP = {"d_n4": ["w_ple_proj", "w_ple_gate"], "ffn2_d_w_gu": ["ffn2_w_down"], "ffn2_d_n": ["ffn2_w_gu"],
            "d_mix": ["w_out"], "d_n2_b": ["w_in"], "ffn1_d_w_gu": ["ffn1_w_down"], "ffn1_d_n": ["ffn1_w_gu"]}
    SCATTER = {"ffn2_d_a": ["w_ple_proj", "w_ple_gate"], "ffn2_d_n": ["ffn2_w_down"], "band_a_bwd": ["ffn2_w_gu"],
               "band_b0_bwd": ["w_out"], "ffn1_d_a": ["w_in"], "ffn1_d_n": ["ffn1_w_down"]}

    def __init__(self, placed, c_idx, kc_idx):
        self.full = dict(placed)
        self.missing = {n: 1.0 for n in placed}
        self.c_idx, self.kc_idx = c_idx, kc_idx
        self.grads, self.swapped, self.pair_sums, self.scattered = {}, {}, {}, {}

    def _gather(self, entries):
        carries, after = [], []
        for part in sorted({pt for _, pt in entries}):
            names = [n for n, pt in entries if pt == part]
            carry = _gather_carry([self.full[n] for n in names], part)
            carries.append(carry)

            def done(carry=carry, names=names, part=part):
                for n, a in zip(names, carry.results):
                    self.full[n] = a
                    self.missing[n] -= 1.0 / part[1]
            after.append(done)
        return carries, after

    def weight(self, name):
        assert abs(self.missing[name]) < 1e-9, (name, self.missing[name])
        return _to_compute(name, self.full[name])

    def grad(self, name, g):
        self.grads[name] = _to_comm(name, g)

    def _swap(self, names):
        carry = _pair_swap_carry([self.grads[n] for n in names])

        def done():
            self.swapped.update(zip(names, carry.results))
        return carry, done

    def _scatter(self, names):
        for n in names:
            self.pair_sums[n] = _pair_sum(self.grads[n], self.swapped[n], self.c_idx, name=f"grad_pair_sum_{n}")
        carry = _chip_scatter_carry([self.pair_sums[n] for n in names])

        def done():
            self.scattered.update(zip(names, carry.results))
        return carry, done

    def run(self, fn, *args, name, **kw):
        carries, after = self._gather(self.GATHER.get(name, []))
        for names, make in ((self.SWAP.get(name), self._swap), (self.SCATTER.get(name), self._scatter)):
            if names:
                carry, done = make(names)
                carries.append(carry)
                after.append(done)
        out = fn(*args, name=name, carry=_join(carries), **kw)
        for done in after:
            done()
        return out

    def finish(self):
        left = [n for n in BIG if n not in self.scattered]
        to_swap = [n for n in left if n not in self.swapped]
        if to_swap:
            carry, done = self._swap(to_swap)
            _comm_call(carry, name="grad_pair_swap")
            done()
        if left:
            carry, done = self._scatter(left)
            _comm_call(carry, name="grad_chip_scatter")
            done()
        halves = [_chip_sum(self.pair_sums[n], self.scattered[n], self.kc_idx, name=f"grad_chip_sum_{n}") for n in BIG]
        carry = _half_swap_carry(halves)
        _comm_call(carry, name="grad_half_swap")
        return dict(zip(BIG, carry.results))


def kernel(x, p, rel_bias, ffn1_pre_g, ffn1_w_gu, ffn1_w_down, ffn1_post_g, attn_pre_g, w_in, b_in, sinks, w_out, b_out, attn_post_g, ffn2_pre_g, ffn2_w_gu, ffn2_w_down, ffn2_post_g, ple_pre_g, w_ple_gate, w_ple_proj, ple_post_g, loss_target, m_rel_bias, m_ffn1_pre_g, m_ffn1_w_gu, m_ffn1_w_down, m_ffn1_post_g, m_attn_pre_g, m_w_in, m_b_in, m_sinks, m_w_out, m_b_out, m_attn_post_g, m_ffn2_pre_g, m_ffn2_w_gu, m_ffn2_w_down, m_ffn2_post_g, m_ple_pre_g, m_w_ple_gate, m_w_ple_proj, m_ple_post_g, v_rel_bias, v_ffn1_pre_g, v_ffn1_w_gu, v_ffn1_w_down, v_ffn1_post_g, v_attn_pre_g, v_w_in, v_b_in, v_sinks, v_w_out, v_b_out, v_attn_post_g, v_ffn2_pre_g, v_ffn2_w_gu, v_ffn2_w_down, v_ffn2_post_g, v_ple_pre_g, v_w_ple_gate, v_w_ple_proj, v_ple_post_g):
    given = dict(locals())
    w = {n: given[n] for n in WEIGHTS}
    m = {n: given["m_" + n] for n in WEIGHTS}
    v = {n: given["v_" + n] for n in WEIGHTS}
    c_pos = lax.axis_index("c").astype(jnp.int32)
    k_pos = (2 * lax.axis_index("x") + lax.axis_index("y")).astype(jnp.int32)
    c_idx, k_idx, kc_idx = c_pos.reshape(1), k_pos.reshape(1), jnp.stack([k_pos, c_pos])

    sched = _Schedule({n: _cast_place(w[n][0], k_idx, name=f"place_{n}") for n in BIG}, c_idx, kc_idx)
    small = {n: (w[n] if n == "rel_bias" else w[n].reshape(1, -1)) for n in SMALL}

    loss_part, grad_x, gs = _local_step(x[0], p[0, 0], loss_target[0], small, sched)

    grads_big = sched.finish()

    *small_grads, loss_row = _small_all_reduce([gs[n] for n in SMALL] + [loss_part], name="small_all_reduce")
    loss = loss_row[0, 0]

    grads, delta, new_m, new_v = {}, {}, {}, {}
    for n in BIG:
        grads[n] = grads_big[n][None]
        d, mn, vn = _adamw(w[n][0], grads_big[n], m[n][0], v[n][0], name=f"adamw_{n}")
        delta[n], new_m[n], new_v[n] = d[None], mn[None], vn[None]
    d_s, m_s, v_s = _adamw_small([w[n] for n in SMALL], small_grads, [m[n] for n in SMALL], [v[n] for n in SMALL],
                                 name="adamw_small")
    for name, gg, dd, mm, vv in zip(SMALL, small_grads, d_s, m_s, v_s):
        grads[name], delta[name], new_m[name], new_v[name] = gg, dd, mm, vv

    return (loss, grad_x[None], *[grads[n] for n in WEIGHTS], *[delta[n] for n in WEIGHTS],
            *[new_m[n] for n in WEIGHTS], *[new_v[n] for n in WEIGHTS])
```

```python
import math

import jax
import jax.numpy as jnp
import numpy as np
from jax import lax
from jax.experimental import pallas as pl
from jax.experimental.pallas import tpu as pltpu

F32 = jnp.float32
BF16 = jnp.bfloat16
MESH = pl.DeviceIdType.MESH

EPS = 1e-6
NEG = -1e30
LANES = 128
SUBLANES = 8
HEAD_DIM = 64
QBLK = 128
N_PAIRS = 4
MIX_W = N_PAIRS * LANES
A_WINDOW = 128
B_PATTERNS = ((128, 1), (512, 4), (2048, 16))
NUM_BUCKETS = 32
MAX_DISTANCE = 2048
ZA_W = 768
ZB_W = 1536
N_CHIPS = 4
VMEM_BYTES_V7X = 64 * 2**20

ADAM_LR, ADAM_B1, ADAM_B2, ADAM_EPS, ADAM_WD, ADAM_STEP = 0.001, 0.9, 0.999, 1e-08, 0.01, 10

BIG = ("ffn1_w_gu", "ffn1_w_down", "w_in", "w_out", "ffn2_w_gu", "ffn2_w_down", "w_ple_gate", "w_ple_proj")
SMALL = ("rel_bias", "ffn1_pre_g", "ffn1_post_g", "attn_pre_g", "b_in", "sinks", "b_out", "attn_post_g",
         "ffn2_pre_g", "ffn2_post_g", "ple_pre_g", "ple_post_g")
WEIGHTS = ("rel_bias", "ffn1_pre_g", "ffn1_w_gu", "ffn1_w_down", "ffn1_post_g", "attn_pre_g", "w_in", "b_in",
           "sinks", "w_out", "b_out", "attn_post_g", "ffn2_pre_g", "ffn2_w_gu", "ffn2_w_down", "ffn2_post_g",
           "ple_pre_g", "w_ple_gate", "w_ple_proj", "ple_post_g")


def _vmem_limit(*block_bytes):
    need = 2 * sum(block_bytes) + max(block_bytes) * 2 + (4 << 20)
    return int(min(max(need, 32 << 20), VMEM_BYTES_V7X - (6 << 20)))


def _nbytes(shape, dtype):
    return int(np.prod(shape)) * jnp.dtype(dtype).itemsize


def _row_block(rows, cap, mult):
    for cand in range(min(rows, cap), 0, -1):
        if rows % cand == 0 and cand % mult == 0:
            return cand
    raise ValueError((rows, cap, mult))


class _Carry:
    def __init__(self, operands, out_shapes, aliases, sems, start, finish):
        self.operands, self.out_shapes, self.aliases, self.sems = list(operands), list(out_shapes), dict(aliases), list(sems)
        self.start, self.finish = start, finish
        self.results = None


def _join(carries):
    carries = [c for c in carries if c is not None]
    if not carries:
        return None
    if len(carries) == 1:
        return carries[0]
    offs, pos = [], [0, 0, 0]
    for c in carries:
        offs.append(tuple(pos))
        pos = [pos[0] + len(c.operands), pos[1] + len(c.out_shapes), pos[2] + len(c.sems)]
    aliases = {}
    for c, (oi, oo, _) in zip(carries, offs):
        aliases.update({oi + i: oo + o for i, o in c.aliases.items()})

    def run(which):
        def fn(ins, outs, sems):
            for c, (oi, oo, os_) in zip(carries, offs):
                getattr(c, which)(ins[oi:oi + len(c.operands)], outs[oo:oo + len(c.out_shapes)],
                                  sems[os_:os_ + len(c.sems)])
        return fn

    joined = _Carry([a for c in carries for a in c.operands], [s for c in carries for s in c.out_shapes], aliases,
                    [s for c in carries for s in c.sems], run("start"), run("finish"))
    joined.parts = (carries, offs)
    return joined


def _set_results(carry, outs):
    carry.results = list(outs)
    if hasattr(carry, "parts"):
        for c, (_, oo, _) in zip(*carry.parts):
            _set_results(c, outs[oo:oo + len(c.out_shapes)])


ANY = pl.BlockSpec(memory_space=pl.ANY)


def _call(body, *, name, grid, in_specs, out_specs, out_shape, args, scratch=(), vmem_limit=None, carry=None,
          semantics=None):
    n_ci, n_co, n_cs = len(args), len(out_shape), len(scratch)
    semantics = semantics or ("parallel",) * len(grid)
    if carry is None:
        res = pl.pallas_call(
            body, name=name, grid=grid, in_specs=list(in_specs), out_specs=tuple(out_specs), out_shape=tuple(out_shape),
            scratch_shapes=list(scratch),
            compiler_params=pltpu.CompilerParams(dimension_semantics=semantics, vmem_limit_bytes=vmem_limit),
        )(*args)
        return list(res)
    n_pi, n_po = len(carry.operands), len(carry.out_shapes)

    def full_body(*refs):
        ci, pi = refs[:n_ci], refs[n_ci:n_ci + n_pi]
        o0 = n_ci + n_pi
        co, po = refs[o0:o0 + n_co], refs[o0 + n_co:o0 + n_co + n_po]
        s0 = o0 + n_co + n_po
        cs, ps = refs[s0:s0 + n_cs], refs[s0 + n_cs:]
        ids = [pl.program_id(ax) for ax in range(len(grid))]
        first, last = ids[0] == 0, ids[0] == grid[0] - 1
        for ax in range(1, len(grid)):
            first, last = first & (ids[ax] == 0), last & (ids[ax] == grid[ax] - 1)

        @pl.when(first)
        def _():
            carry.start(pi, po, ps)

        body(*ci, *co, *cs)

        @pl.when(last)
        def _():
            carry.finish(pi, po, ps)

    res = pl.pallas_call(
        full_body, name=name, grid=grid, in_specs=list(in_specs) + [ANY] * n_pi,
        out_specs=tuple(out_specs) + tuple([ANY] * n_po), out_shape=tuple(out_shape) + tuple(carry.out_shapes),
        scratch_shapes=list(scratch) + carry.sems,
        input_output_aliases={n_ci + i: n_co + o for i, o in carry.aliases.items()},
        compiler_params=pltpu.CompilerParams(dimension_semantics=("arbitrary",) * len(grid),
                                             vmem_limit_bytes=vmem_limit),
    )(*args, *carry.operands)
    _set_results(carry, res[n_co:])
    return list(res[:n_co])


def _comm_call(carry, *, name):
    n_pi, n_po = len(carry.operands), len(carry.out_shapes)

    def body(*refs):
        pi, po, ps = refs[:n_pi], refs[n_pi:n_pi + n_po], refs[n_pi + n_po:]
        carry.start(pi, po, ps)
        carry.finish(pi, po, ps)

    res = pl.pallas_call(
        body, name=name, in_specs=[ANY] * n_pi, out_specs=tuple([ANY] * n_po), out_shape=tuple(carry.out_shapes),
        scratch_shapes=carry.sems, input_output_aliases=dict(carry.aliases),
    )(*carry.operands)
    _set_results(carry, res)


def _mm_nn(a, b, *, bm, bn, out_dtype, name, bias=None, carry=None):
    M, K = a.shape
    stacked = b.ndim == 3
    N = b.shape[0] * b.shape[2] if stacked else b.shape[1]
    assert M % bm == 0 and N % bn == 0 and (not stacked or bn == b.shape[2])

    def body(*refs):
        a_ref, b_ref = refs[0], refs[1]
        o_ref = refs[-1]
        acc = jnp.dot(a_ref[...], b_ref[...], preferred_element_type=F32)
        if bias is not None:
            acc = acc + refs[2][...]
        o_ref[...] = acc.astype(o_ref.dtype)

    if stacked:
        b_spec = pl.BlockSpec((None, K, bn), lambda i, j: (j, 0, 0))
    else:
        b_spec = pl.BlockSpec((K, bn), lambda i, j: (0, j))
    in_specs = [pl.BlockSpec((bm, K), lambda i, j: (i, 0)), b_spec]
    args = [a, b]
    if bias is not None:
        in_specs.append(pl.BlockSpec((1, bn), lambda i, j: (0, j)))
        args.append(bias)
    limit = _vmem_limit(_nbytes((bm, K), a.dtype), _nbytes((K, bn), b.dtype), _nbytes((bm, bn), F32))
    return _call(body, name=name, grid=(M // bm, N // bn), in_specs=in_specs,
                 out_specs=[pl.BlockSpec((bm, bn), lambda i, j: (i, j))],
                 out_shape=[jax.ShapeDtypeStruct((M, N), out_dtype)], args=args, vmem_limit=limit, carry=carry)[0]


def _mm_nt(a, b, *, bm, bn, out_dtype, name, add=None, carry=None):
    M, K = a.shape
    stacked = b.ndim == 3
    N = b.shape[1] if stacked else b.shape[0]
    n_sh = b.shape[0] if stacked else 1
    ks = b.shape[2] if stacked else K
    assert M % bm == 0 and N % bn == 0 and n_sh * ks == K
    dn = (((1,), (1,)), ((), ()))

    def body(*refs):
        a_ref, b_ref = refs[0], refs[1]
        o_ref = refs[-1]
        if stacked:
            acc = lax.dot_general(a_ref[:, 0:ks], b_ref[0], dn, preferred_element_type=F32)
            for s in range(1, n_sh):
                acc = acc + lax.dot_general(a_ref[:, s * ks:(s + 1) * ks], b_ref[s], dn, preferred_element_type=F32)
        else:
            acc = lax.dot_general(a_ref[...], b_ref[...], dn, preferred_element_type=F32)
        if add is not None:
            acc = acc + refs[2][...]
        o_ref[...] = acc.astype(o_ref.dtype)

    if stacked:
        b_spec = pl.BlockSpec((n_sh, bn, ks), lambda i, j: (0, j, 0))
    else:
        b_spec = pl.BlockSpec((bn, K), lambda i, j: (j, 0))
    in_specs = [pl.BlockSpec((bm, K), lambda i, j: (i, 0)), b_spec]
    args = [a, b]
    if add is not None:
        in_specs.append(pl.BlockSpec((bm, bn), lambda i, j: (i, j)))
        args.append(add)
    limit = _vmem_limit(_nbytes((bm, K), a.dtype), _nbytes((bn, K), b.dtype), _nbytes((bm, bn), F32))
    return _call(body, name=name, grid=(M // bm, N // bn), in_specs=in_specs,
                 out_specs=[pl.BlockSpec((bm, bn), lambda i, j: (i, j))],
                 out_shape=[jax.ShapeDtypeStruct((M, N), out_dtype)], args=args, vmem_limit=limit, carry=carry)[0]


def _mm_tn(a, b, *, bm, bn, out_dtype, name, n_stack=None, carry=None):
    T, M = a.shape
    N = b.shape[1]
    assert M % bm == 0 and N % bn == 0 and (n_stack is None or bn * n_stack == N)
    dn = (((0,), (0,)), ((), ()))

    def body(a_ref, b_ref, o_ref):
        acc = lax.dot_general(a_ref[...], b_ref[...], dn, preferred_element_type=F32)
        o_ref[...] = acc.astype(o_ref.dtype)

    if n_stack is None:
        out_spec = pl.BlockSpec((bm, bn), lambda i, j: (i, j))
        out_shape = jax.ShapeDtypeStruct((M, N), out_dtype)
    else:
        out_spec = pl.BlockSpec((None, bm, bn), lambda i, j: (j, i, 0))
        out_shape = jax.ShapeDtypeStruct((n_stack, M, bn), out_dtype)
    limit = _vmem_limit(_nbytes((T, bm), a.dtype), _nbytes((T, bn), b.dtype), _nbytes((bm, bn), F32))
    return _call(body, name=name, grid=(M // bm, N // bn),
                 in_specs=[pl.BlockSpec((T, bm), lambda i, j: (0, i)), pl.BlockSpec((T, bn), lambda i, j: (0, j))],
                 out_specs=[out_spec], out_shape=[out_shape], args=[a, b], vmem_limit=limit, carry=carry)[0]


ROW_BLK = 256


def _rstd(x):
    return lax.rsqrt(jnp.mean(x * x, axis=-1, keepdims=True) + EPS)


def _norm_bwd(xhat, rstd, dxhat):
    return rstd * (dxhat - xhat * jnp.mean(dxhat * xhat, axis=-1, keepdims=True))


def _row_spec(cols):
    return pl.BlockSpec((ROW_BLK, cols), lambda i: (i, 0))


def _vec_spec(cols):
    return pl.BlockSpec((1, cols), lambda i: (0, 0))


def _prologue(x, g, rel_bias, idx_all, *, name, carry=None):
    T, D = x.shape
    n_pat = idx_all.shape[0]
    heads = 2 * N_PAIRS
    steps = n_pat * heads
    rows = T // steps

    def body(rb_ref, x_ref, g_ref, idx_ref, n_ref, bias_ref):
        s = pl.program_id(0)
        head = jnp.where(s < heads, s, heads + s % heads)
        xv = x_ref[...]
        n_ref[...] = (xv * _rstd(xv) * g_ref[...]).astype(n_ref.dtype)
        idxv = idx_ref[...]
        acc = jnp.where(idxv < 0, NEG, 0.0).astype(F32)
        for b in range(NUM_BUCKETS):
            acc = acc + jnp.where(idxv == b, rb_ref[b, head], 0.0)
        bias_ref[0] = acc
        kap = lax.broadcasted_iota(jnp.int32, (1, 2 * QBLK), 1)
        bias_ref[1] = jnp.where(kap < QBLK, NEG, acc)

    return _call(
        body, name=name, grid=(steps,),
        in_specs=[pl.BlockSpec(memory_space=pltpu.SMEM), pl.BlockSpec((rows, D), lambda s: (s, 0)),
                  pl.BlockSpec((1, D), lambda s: (0, 0)),
                  pl.BlockSpec((None, QBLK, 2 * QBLK), lambda s: (s // heads, 0, 0))],
        out_specs=[pl.BlockSpec((rows, D), lambda s: (s, 0)),
                   pl.BlockSpec((None, 2, None, QBLK, 2 * QBLK), lambda s: (s // heads, 0, s % heads, 0, 0))],
        out_shape=[jax.ShapeDtypeStruct((T, D), BF16),
                   jax.ShapeDtypeStruct((n_pat, 2, heads, QBLK, 2 * QBLK), F32)],
        args=[rel_bias, x, g, idx_all], carry=carry)


def _mm_resid_norm(a, b, h, g_post, coef, g_next, *, bm, name, bias=None, carry=None):
    M, K = a.shape
    N = b.shape[1]

    def body(*refs):
        a_ref, b_ref, h_ref, gp_ref, gn_ref = refs[0:5]
        f_ref, hn_ref, n_ref = refs[-3:]
        f = jnp.dot(a_ref[...], b_ref[...], preferred_element_type=F32)
        if bias is not None:
            f = f + refs[5][...]
        f_ref[...] = f
        hn = h_ref[...] + coef * (f * _rstd(f) * gp_ref[...])
        hn_ref[...] = hn
        n_ref[...] = (hn * _rstd(hn) * gn_ref[...]).astype(n_ref.dtype)

    row = lambda w: pl.BlockSpec((bm, w), lambda i: (i, 0))
    vec = pl.BlockSpec((1, N), lambda i: (0, 0))
    in_specs = [row(K), pl.BlockSpec((K, N), lambda i: (0, 0), pipeline_mode=pl.Buffered(1)), row(N), vec, vec]
    args = [a, b, h, g_post, g_next]
    if bias is not None:
        in_specs.append(vec)
        args.append(bias)
    limit = _vmem_limit(_nbytes((bm, K), a.dtype), _nbytes((K, N), b.dtype) // 2, 4 * _nbytes((bm, N), F32))
    return _call(body, name=name, grid=(M // bm,), in_specs=in_specs, out_specs=[row(N), row(N), row(N)],
                 out_shape=[jax.ShapeDtypeStruct((M, N), F32), jax.ShapeDtypeStruct((M, N), F32),
                            jax.ShapeDtypeStruct((M, N), BF16)], args=args, vmem_limit=limit, carry=carry)


def _mm_nt_norms_bwd(a, b, dh_in, h, g_pre, f, g_post, coef, *, bm, name, add=None, carry=None):
    M, K = a.shape
    stacked = b.ndim == 3
    N = b.shape[1] if stacked else b.shape[0]
    n_sh = b.shape[0] if stacked else 1
    ks = b.shape[2] if stacked else K
    assert n_sh * ks == K
    dn_dims = (((1,), (1,)), ((), ()))
    with_f = f is not None
    n_in = 5 + (2 if with_f else 0) + (1 if add is not None else 0)

    def body(*refs):
        a_ref, b_ref, dhi_ref, h_ref, gpre_ref = refs[0:5]
        outs = refs[n_in:]
        if stacked:
            dn = lax.dot_general(a_ref[:, 0:ks], b_ref[0], dn_dims, preferred_element_type=F32)
            for s in range(1, n_sh):
                dn = dn + lax.dot_general(a_ref[:, s * ks:(s + 1) * ks], b_ref[s], dn_dims, preferred_element_type=F32)
        else:
            dn = lax.dot_general(a_ref[...], b_ref[...], dn_dims, preferred_element_type=F32)
        if add is not None:
            dn = dn + refs[n_in - 1][...]

        @pl.when(pl.program_id(0) == 0)
        def _():
            for acc in (outs[2:] if with_f else outs[1:]):
                acc[...] = jnp.zeros_like(acc)

        hv = h_ref[...]
        r = _rstd(hv)
        hh = hv * r
        dh = dhi_ref[...] + _norm_bwd(hh, r, dn * gpre_ref[...])
        outs[0][...] = dh
        if not with_f:
            outs[1][...] += jnp.sum(dn * hh, axis=0, keepdims=True)
            return
        f_ref, gpost_ref = refs[5], refs[6]
        df_ref, dgpre_ref, dgpost_ref, dsum_ref = outs[1:]
        dgpre_ref[...] += jnp.sum(dn * hh, axis=0, keepdims=True)
        fv = f_ref[...]
        rf = _rstd(fv)
        fh = fv * rf
        dy = coef * dh
        dgpost_ref[...] += jnp.sum(dy * fh, axis=0, keepdims=True)
        df = _norm_bwd(fh, rf, dy * gpost_ref[...])
        dsum_ref[...] += jnp.sum(df, axis=0, keepdims=True)
        df_ref[...] = df.astype(df_ref.dtype)

    row = lambda w: pl.BlockSpec((bm, w), lambda i: (i, 0))
    vec = pl.BlockSpec((1, N), lambda i: (0, 0))
    once = pl.Buffered(1)
    if stacked:
        b_spec = pl.BlockSpec((n_sh, N, ks), lambda i: (0, 0, 0), pipeline_mode=once)
    else:
        b_spec = pl.BlockSpec((N, K), lambda i: (0, 0), pipeline_mode=once)
    in_specs = [row(K), b_spec, row(N), row(N), vec]
    args = [a, b, dh_in, h, g_pre]
    if with_f:
        in_specs += [row(N), vec]
        args += [f, g_post]
    if add is not None:
        in_specs.append(row(N))
        args.append(add)
    vec_shape = jax.ShapeDtypeStruct((1, N), F32)
    out_specs = [row(N)] + ([row(N), vec, vec, vec] if with_f else [vec])
    out_shape = [jax.ShapeDtypeStruct((M, N), F32)]
    out_shape += [jax.ShapeDtypeStruct((M, N), BF16), vec_shape, vec_shape, vec_shape] if with_f else [vec_shape]
    limit = _vmem_limit(_nbytes((bm, K), a.dtype), _nbytes((N, K), b.dtype) // 2, 6 * _nbytes((bm, N), F32))
    return _call(body, name=name, grid=(M // bm,), in_specs=in_specs, out_specs=out_specs, out_shape=out_shape,
                 args=args, vmem_limit=limit, semantics=("arbitrary",), carry=carry)


def _ffn_up(n, w_gu, *, bm, name, carry=None):
    M, K = n.shape
    n_sh, _, ns = w_gu.shape
    half = n_sh // 2

    def body(n_ref, wg_ref, wu_ref, g_ref, u_ref, a_ref):
        nv = n_ref[...]
        g = jnp.dot(nv, wg_ref[...], preferred_element_type=F32)
        u = jnp.dot(nv, wu_ref[...], preferred_element_type=F32)
        g_ref[...] = g.astype(g_ref.dtype)
        u_ref[...] = u.astype(u_ref.dtype)
        a_ref[...] = (g * jax.nn.sigmoid(g) * u).astype(a_ref.dtype)

    out_spec = pl.BlockSpec((bm, ns), lambda i, j: (i, j))
    out = jax.ShapeDtypeStruct((M, half * ns), BF16)
    limit = _vmem_limit(_nbytes((bm, K), n.dtype), 2 * _nbytes((K, ns), w_gu.dtype), 3 * _nbytes((bm, ns), F32))
    return _call(body, name=name, grid=(M // bm, half),
                 in_specs=[pl.BlockSpec((bm, K), lambda i, j: (i, 0)),
                           pl.BlockSpec((None, K, ns), lambda i, j: (j, 0, 0)),
                           pl.BlockSpec((None, K, ns), lambda i, j: (j + half, 0, 0))],
                 out_specs=[out_spec, out_spec, out_spec], out_shape=[out, out, out], args=[n, w_gu, w_gu],
                 vmem_limit=limit, carry=carry)


def _ffn_down_bwd(df, w_down, g, u, *, bm, name, carry=None):
    M, D = df.shape
    F = w_down.shape[0]
    dn = (((1,), (1,)), ((), ()))

    def body(df_ref, w_ref, g_ref, u_ref, o_ref):
        da = lax.dot_general(df_ref[...], w_ref[...], dn, preferred_element_type=F32)
        gv = g_ref[...].astype(F32)
        s = jax.nn.sigmoid(gv)
        o_ref[:, 0:F] = (da * u_ref[...].astype(F32) * (s * (1.0 + gv * (1.0 - s)))).astype(o_ref.dtype)
        o_ref[:, F:2 * F] = (da * (gv * s)).astype(o_ref.dtype)

    row = lambda w: pl.BlockSpec((bm, w), lambda i: (i, 0))
    limit = _vmem_limit(_nbytes((F, D), w_down.dtype), 2 * _nbytes((bm, F), BF16), _nbytes((bm, 2 * F), BF16),
                        2 * _nbytes((bm, F), F32))
    return _call(body, name=name, grid=(M // bm,),
                 in_specs=[row(D), pl.BlockSpec((F, D), lambda i: (0, 0)), row(F), row(F)],
                 out_specs=[row(2 * F)], out_shape=[jax.ShapeDtypeStruct((M, 2 * F), BF16)],
                 args=[df, w_down, g, u], vmem_limit=limit, carry=carry)[0]


def _ple_head(n4, w_gate, p, w_ple, h3, g_post, target, *, bm, name):
    T, D = h3.shape
    kp = p.shape[1]

    def body(n_ref, wg_ref, p_ref, wp_ref, h_ref, g_ref, t_ref, dh_ref, de_ref, dgp_ref, loss_ref, dg_ref):
        @pl.when(pl.program_id(0) == 0)
        def _():
            loss_ref[...] = jnp.zeros_like(loss_ref)
            dg_ref[...] = jnp.zeros_like(dg_ref)

        gate = jax.nn.sigmoid(jnp.dot(n_ref[...], wg_ref[...], preferred_element_type=F32))
        ev = jnp.dot(p_ref[...], wp_ref[...], preferred_element_type=F32)
        ge = gate * ev
        r = _rstd(ge)
        gh = ge * r
        gpost = g_ref[...]
        diff = h_ref[...] + gh * gpost - t_ref[...]
        loss_ref[...] += jnp.sum(diff * diff) * (0.5 / D)
        dh = diff * (1.0 / D)
        dh_ref[...] = dh
        dg_ref[...] += jnp.sum(dh * gh, axis=0, keepdims=True)
        dge = _norm_bwd(gh, r, dh * gpost)
        de_ref[...] = (dge * gate).astype(de_ref.dtype)
        dgp_ref[...] = (dge * ev * gate * (1.0 - gate)).astype(dgp_ref.dtype)

    row = lambda w: pl.BlockSpec((bm, w), lambda i: (i, 0))
    whole = lambda a: pl.BlockSpec(a.shape, lambda i: (0, 0), pipeline_mode=pl.Buffered(1))
    limit = _vmem_limit(_nbytes((bm, D), BF16), _nbytes(w_gate.shape, w_gate.dtype) // 2, 6 * _nbytes((bm, D), F32))
    return pl.pallas_call(
        body, name=name, grid=(T // bm,),
        in_specs=[row(D), whole(w_gate), row(kp), whole(w_ple), row(D), _vec_spec(D), row(D)],
        out_specs=(row(D), row(D), row(D), _vec_spec(LANES), _vec_spec(D)),
        out_shape=(jax.ShapeDtypeStruct((T, D), F32), jax.ShapeDtypeStruct((T, D), BF16),
                   jax.ShapeDtypeStruct((T, D), BF16), jax.ShapeDtypeStruct((1, LANES), F32),
                   jax.ShapeDtypeStruct((1, D), F32)),
        compiler_params=pltpu.CompilerParams(dimension_semantics=("arbitrary",), vmem_limit_bytes=limit),
    )(n4, w_gate, p, w_ple, h3, g_post, target)


def _t5_index(max_dist, stride):
    rho = np.arange(QBLK)[:, None]
    kap = np.arange(2 * QBLK)[None, :]
    d = rho + QBLK - kap
    valid = (d >= 0) & (d <= max_dist)
    n = np.maximum(d, 0) * stride
    max_exact = NUM_BUCKETS // 2
    nf = np.maximum(n, 1).astype(np.float32)
    large = max_exact + (np.log(nf / np.float32(max_exact)) / np.float32(math.log(MAX_DISTANCE / max_exact))
                         * np.float32(NUM_BUCKETS - max_exact)).astype(np.int32)
    large = np.minimum(large, NUM_BUCKETS - 1)
    bucket = np.where(n < max_exact, n, large)
    return np.where(valid, bucket, -1).astype(np.int32)


def _bias_grad(d_bias, idx, *, name):
    def body(db_ref, idx_ref, o_ref):
        h = pl.program_id(0)

        @pl.when(h == 0)
        def _():
            o_ref[...] = jnp.zeros_like(o_ref)

        idxv = idx_ref[...]
        dv = db_ref[0]
        rows = lax.broadcasted_iota(jnp.int32, (NUM_BUCKETS, LANES), 0)
        lanes = lax.broadcasted_iota(jnp.int32, (NUM_BUCKETS, LANES), 1)
        acc = o_ref[...]
        for b in range(NUM_BUCKETS):
            s = jnp.sum(jnp.where(idxv == b, dv, 0.0))
            acc = acc + jnp.where((rows == b) & (lanes == h), s, 0.0)
        o_ref[...] = acc

    return pl.pallas_call(
        body, name=name, grid=(2 * N_PAIRS,),
        in_specs=[pl.BlockSpec((1, QBLK, 2 * QBLK), lambda h: (h, 0, 0)),
                  pl.BlockSpec((QBLK, 2 * QBLK), lambda h: (0, 0))],
        out_specs=pl.BlockSpec((NUM_BUCKETS, LANES), lambda h: (0, 0)),
        out_shape=jax.ShapeDtypeStruct((NUM_BUCKETS, LANES), F32),
        compiler_params=pltpu.CompilerParams(dimension_semantics=("arbitrary",)),
    )(d_bias, idx)


def _lane():
    return lax.broadcasted_iota(jnp.int32, (1, LANES), 1)


def _head_sel(h):
    return (_lane() < HEAD_DIM) if h == 0 else (_lane() >= HEAD_DIM)


def _stat_lo():
    return (_lane() % HEAD_DIM) < (HEAD_DIM // 2)


def _stack_heads(t, scale=None):
    if scale is not None:
        t = t * scale
    zero = jnp.zeros_like(t)
    return jnp.concatenate([jnp.where(_head_sel(0), t, zero), jnp.where(_head_sel(1), t, zero)], axis=0)


def _class_spec(L, r, cols, stride, pps):
    chunks = N_PAIRS // pps
    mode = pl.Buffered(1) if r * chunks == 1 else None
    return pl.BlockSpec((L, MIX_W // chunks), lambda j, c: (0, (cols + j * stride) * chunks + c), pipeline_mode=mode)


def _rows(b, n_blocks=1):
    return pl.ds(pl.multiple_of(b * QBLK, QBLK), n_blocks * QBLK)


def _key_window(ref, b, cols, first):
    if first:
        blk = ref[0:QBLK, cols]
        return jnp.concatenate([blk, blk], axis=0)
    return ref[_rows(b - 1, 2), cols]


def _band_fwd(qa, ka, va, bias, *, r, q_col, k_col, v_col, stride, pattern, name, carry=None):
    L = qa.shape[0]
    nb = L // QBLK
    dn = (((1,), (1,)), ((), ()))
    scale = HEAD_DIM ** -0.5

    pps = N_PAIRS

    def body(q_ref, k_ref, v_ref, bias_ref, o_ref, lse_ref):
        sel0 = _head_sel(0)
        pair0 = pl.program_id(1) * pps

        def block(b, first):
            rows = _rows(b)
            for i in range(pps):
                cols = slice(i * LANES, (i + 1) * LANES)
                q2 = _stack_heads(q_ref[rows, cols], scale)
                k = _key_window(k_ref, b, cols, first)
                v = _key_window(v_ref, b, cols, first)
                bias2 = bias_ref[1 if first else 0, pl.ds(2 * (pair0 + i), 2)].reshape(2 * QBLK, 2 * QBLK)
                s = lax.dot_general(q2, k, dn, preferred_element_type=F32) + bias2
                m = jnp.max(s, axis=1, keepdims=True)
                p = jnp.exp(s - m)
                l = jnp.sum(p, axis=1, keepdims=True)
                o = jnp.dot(p.astype(v.dtype), v, preferred_element_type=F32) / l
                lse = m + jnp.log(l)
                o_ref[rows, cols] = jnp.where(sel0, o[0:QBLK], o[QBLK:]).astype(o_ref.dtype)
                lse_ref[rows, cols] = jnp.where(sel0, lse[0:QBLK], lse[QBLK:])

        block(0, True)
        if nb > 1:
            pl.loop(1, nb)(lambda b: block(b, False))

    bias_spec = pl.BlockSpec((None, 2, 2 * N_PAIRS, QBLK, 2 * QBLK), lambda j, c: (pattern, 0, 0, 0, 0))
    out_spec = _class_spec(L, r, 0, 1, pps)
    blk_bytes = _nbytes((L, pps * LANES), F32)
    return _call(
        body, name=name, grid=(r, N_PAIRS // pps),
        in_specs=[_class_spec(L, r, q_col, stride, pps), _class_spec(L, r, k_col, stride, pps),
                  _class_spec(L, r, v_col, stride, pps), bias_spec],
        out_specs=[out_spec, out_spec],
        out_shape=[jax.ShapeDtypeStruct((L, r * MIX_W), BF16), jax.ShapeDtypeStruct((L, r * MIX_W), F32)],
        args=[qa, ka, va, bias], vmem_limit=_vmem_limit(*([blk_bytes] * 4)), carry=carry)


def _class_rows_spec(rows, r, width):
    return pl.BlockSpec((rows // r, r * width), lambda i, *_: (i, 0))


def _token_scratch(rows, width):
    return pltpu.VMEM((width // LANES, rows, LANES), F32)


def _fill_tokens(scratch, value):
    for c in range(scratch.shape[0]):
        scratch[c] = value[:, c * LANES:(c + 1) * LANES]


def _read_tokens(scratch):
    return jnp.concatenate([scratch[c] for c in range(scratch.shape[0])], axis=1)


def _to_tokens(blk_ref, r, scratch):
    if r == 1:
        return blk_ref[...].astype(F32)
    nt, rows, _ = scratch.shape
    for j in range(r):
        for c in range(nt):
            lanes = slice((j * nt + c) * LANES, (j * nt + c + 1) * LANES)
            scratch.at[c][pl.ds(j, rows // r, stride=r), :] = blk_ref[:, lanes].astype(F32)
    return _read_tokens(scratch)


def _from_tokens(dst_ref, r, scratch):
    nt, rows, _ = scratch.shape
    if r == 1:
        dst_ref[...] = _read_tokens(scratch).astype(dst_ref.dtype)
        return
    for j in range(r):
        for c in range(nt):
            lanes = slice((j * nt + c) * LANES, (j * nt + c + 1) * LANES)
            dst_ref[:, lanes] = scratch.at[c][pl.ds(j, rows // r, stride=r), :].astype(dst_ref.dtype)


def _mm_nn_classes(a, b, bias, dils, *, bm, name):
    M, K = a.shape
    N = b.shape[1]

    def body(a_ref, b_ref, bias_ref, *rest):
        outs, tile = rest[:-1], rest[-1]
        _fill_tokens(tile, jnp.dot(a_ref[...], b_ref[...], preferred_element_type=F32) + bias_ref[...])
        for out, r in zip(outs, dils):
            _from_tokens(out, r, tile)

    limit = _vmem_limit(_nbytes((bm, K), a.dtype), _nbytes((K, N), b.dtype) // 2, (1 + len(dils)) * _nbytes((bm, N), F32))
    return pl.pallas_call(
        body, name=name, grid=(M // bm,),
        in_specs=[pl.BlockSpec((bm, K), lambda i: (i, 0)),
                  pl.BlockSpec((K, N), lambda i: (0, 0), pipeline_mode=pl.Buffered(1)),
                  pl.BlockSpec((1, N), lambda i: (0, 0))],
        out_specs=tuple(_class_rows_spec(bm, r, N) for r in dils),
        out_shape=tuple(jax.ShapeDtypeStruct((M // r, r * N), BF16) for r in dils),
        scratch_shapes=[_token_scratch(bm, N)],
        compiler_params=pltpu.CompilerParams(dimension_semantics=("parallel",), vmem_limit_bytes=limit),
    )(a, b, bias)


def _merge(branches, dils, sink, *, name):
    W = MIX_W
    T = branches[0][0].shape[0] * dils[0]
    nbr = len(branches)

    def body(*refs):
        sink_ref = refs[2 * nbr]
        out_ref, outb_ref, lse_ref, scratch = refs[2 * nbr + 1:]
        sk = sink_ref[...]
        lses = [_to_tokens(refs[2 * t + 1], dils[t], scratch) for t in range(nbr)]
        mx = sk
        for lse in lses:
            mx = jnp.maximum(mx, lse)
        den = jnp.exp(sk - mx)
        num = jnp.zeros_like(mx)
        for t in range(nbr):
            w = jnp.exp(lses[t] - mx)
            den = den + w
            num = num + w * _to_tokens(refs[2 * t], dils[t], scratch)
        out = num / den
        out_ref[...] = out
        outb_ref[...] = out.astype(outb_ref.dtype)
        lse_ref[...] = mx + jnp.log(den)

    args = [a for br in branches for a in br] + [sink]
    in_specs = [_class_rows_spec(ROW_BLK, r, W) for r in dils for _ in range(2)] + [_vec_spec(W)]
    return pl.pallas_call(
        body, name=name, grid=(T // ROW_BLK,), in_specs=in_specs,
        out_specs=(_row_spec(W), _row_spec(W), _row_spec(W)),
        out_shape=(jax.ShapeDtypeStruct((T, W), F32), jax.ShapeDtypeStruct((T, W), BF16),
                   jax.ShapeDtypeStruct((T, W), F32)),
        scratch_shapes=[_token_scratch(ROW_BLK, W)],
        compiler_params=pltpu.CompilerParams(dimension_semantics=("parallel",)),
    )(*args)


def _attn_delta(dmix, outs, lses, sink, dils, *, name):
    T = dmix.shape[0]
    W = MIX_W
    nd = len(dils)

    def body(dmix_ref, oa_ref, ob_ref, la_ref, lb_ref, sink_ref, *rest):
        doa_ref, sta_ref, dsink_ref = rest[0:3]
        dob_refs, stb_refs = rest[3:3 + nd], rest[3 + nd:3 + 2 * nd]
        do_tile, st_tile = rest[3 + 2 * nd:]

        @pl.when(pl.program_id(0) == 0)
        def _():
            dsink_ref[...] = jnp.zeros_like(dsink_ref)

        sel0, lo = _head_sel(0), _stat_lo()
        for mixer, (o_ref, l_ref) in enumerate(((oa_ref, la_ref), (ob_ref, lb_ref))):
            for i in range(N_PAIRS):
                cols = slice(i * LANES, (i + 1) * LANES)
                do = dmix_ref[:, mixer * W + i * LANES:mixer * W + (i + 1) * LANES]
                prod = do * o_ref[:, cols]
                d0 = jnp.sum(jnp.where(sel0, prod, 0.0), axis=1, keepdims=True)
                d1 = jnp.sum(jnp.where(sel0, 0.0, prod), axis=1, keepdims=True)
                delta = jnp.where(sel0, d0, d1)
                lse = l_ref[:, cols]
                stat = jnp.where(lo, lse, delta)
                if mixer == 0:
                    sta_ref[:, cols] = stat
                    doa_ref[:, cols] = do.astype(doa_ref.dtype)
                    dsink_ref[:, cols] += -jnp.sum(jnp.exp(sink_ref[:, cols] - lse) * delta, axis=0, keepdims=True)
                else:
                    st_tile[i] = stat
                    do_tile[i] = do
        for t, r in enumerate(dils):
            _from_tokens(dob_refs[t], r, do_tile)
            _from_tokens(stb_refs[t], r, st_tile)

    class_specs = [_class_rows_spec(ROW_BLK, r, W) for r in dils]
    out_shape = [jax.ShapeDtypeStruct((T, W), BF16), jax.ShapeDtypeStruct((T, W), F32), jax.ShapeDtypeStruct((1, W), F32)]
    out_shape += [jax.ShapeDtypeStruct((T // r, r * W), BF16) for r in dils]
    out_shape += [jax.ShapeDtypeStruct((T // r, r * W), F32) for r in dils]
    res = pl.pallas_call(
        body, name=name, grid=(T // ROW_BLK,),
        in_specs=[_row_spec(2 * W), _row_spec(W), _row_spec(W), _row_spec(W), _row_spec(W), _vec_spec(W)],
        out_specs=tuple([_row_spec(W), _row_spec(W), _vec_spec(W)] + class_specs + class_specs),
        out_shape=tuple(out_shape),
        scratch_shapes=[_token_scratch(ROW_BLK, W), _token_scratch(ROW_BLK, W)],
        compiler_params=pltpu.CompilerParams(dimension_semantics=("arbitrary",)),
    )(dmix, outs[0], outs[1], lses[0], lses[1], sink)
    return res[0], res[1], res[2], res[3:3 + nd], res[3 + nd:]


def _band_bwd(qa, ka, va, d_out, stat, bias, *, r, q_col, k_col, v_col, stride, pattern, name, carry=None):
    L = qa.shape[0]
    nb = L // QBLK
    dn_nt = (((1,), (1,)), ((), ()))
    dn_tn = (((0,), (0,)), ((), ()))
    scale = HEAD_DIM ** -0.5

    pps = N_PAIRS if r > 1 else N_PAIRS // 2

    def body(q_ref, k_ref, v_ref, do_ref, st_ref, bias_ref, dq_ref, dk_ref, dv_ref, db_ref, dk_carry, dv_carry):
        @pl.when((pl.program_id(0) == 0) & (pl.program_id(1) == 0))
        def _():
            db_ref[...] = jnp.zeros_like(db_ref)

        lo, sel0 = _stat_lo(), _head_sel(0)
        pair0 = pl.program_id(1) * pps

        def block(b, first):
            rows = _rows(b)
            for i in range(pps):
                heads = pl.ds(2 * (pair0 + i), 2)
                cols = slice(i * LANES, (i + 1) * LANES)
                q2 = _stack_heads(q_ref[rows, cols], scale)
                k = _key_window(k_ref, b, cols, first)
                v = _key_window(v_ref, b, cols, first)
                do2 = _stack_heads(do_ref[rows, cols])
                st = st_ref[rows, cols]
                lse2 = jnp.concatenate(
                    [jnp.max(jnp.where(_head_sel(h) & lo, st, -jnp.inf), axis=1, keepdims=True) for h in range(2)], axis=0)
                delta2 = jnp.concatenate(
                    [jnp.sum(jnp.where(_head_sel(h) & ~lo, st, 0.0), axis=1, keepdims=True) for h in range(2)],
                    axis=0) * (2.0 / HEAD_DIM)
                bias2 = bias_ref[1 if first else 0, heads].reshape(2 * QBLK, 2 * QBLK)
                s = lax.dot_general(q2, k, dn_nt, preferred_element_type=F32) + bias2
                p = jnp.exp(s - lse2)
                dp = lax.dot_general(do2, v, dn_nt, preferred_element_type=F32)
                ds = p * (dp - delta2)
                db_ref[heads] += ds.reshape(2, QBLK, 2 * QBLK)
                dsb = ds.astype(k.dtype)
                dq2 = jnp.dot(dsb, k, preferred_element_type=F32)
                dq_ref[rows, cols] = (jnp.where(sel0, dq2[0:QBLK], dq2[QBLK:]) * scale).astype(dq_ref.dtype)
                dk_acc = lax.dot_general(dsb, q2, dn_tn, preferred_element_type=F32)
                dv_acc = lax.dot_general(p.astype(k.dtype), do2, dn_tn, preferred_element_type=F32)
                if not first:
                    prev = _rows(b - 1)
                    dk_ref[prev, cols] = (dk_carry[:, cols] + dk_acc[0:QBLK]).astype(dk_ref.dtype)
                    dv_ref[prev, cols] = (dv_carry[:, cols] + dv_acc[0:QBLK]).astype(dv_ref.dtype)
                dk_carry[:, cols] = dk_acc[QBLK:2 * QBLK]
                dv_carry[:, cols] = dv_acc[QBLK:2 * QBLK]

        block(0, True)
        if nb > 1:
            pl.loop(1, nb)(lambda b: block(b, False))

        last = _rows(nb - 1)
        dk_ref[last, :] = dk_carry[...].astype(dk_ref.dtype)
        dv_ref[last, :] = dv_carry[...].astype(dv_ref.dtype)

    tok_spec = _class_spec(L, r, 0, 1, pps)
    bias_spec = pl.BlockSpec((None, 2, 2 * N_PAIRS, QBLK, 2 * QBLK), lambda j, c: (pattern, 0, 0, 0, 0))
    dbias_spec = pl.BlockSpec((2 * N_PAIRS, QBLK, 2 * QBLK), lambda j, c: (0, 0, 0))
    grad = jax.ShapeDtypeStruct((L, r * MIX_W), BF16)
    blk_bytes = _nbytes((L, pps * LANES), BF16)
    carry_shape = pltpu.VMEM((QBLK, pps * LANES), F32)
    return _call(
        body, name=name, grid=(r, N_PAIRS // pps),
        in_specs=[_class_spec(L, r, q_col, stride, pps), _class_spec(L, r, k_col, stride, pps),
                  _class_spec(L, r, v_col, stride, pps), tok_spec, tok_spec, bias_spec],
        out_specs=[tok_spec, tok_spec, tok_spec, dbias_spec],
        out_shape=[grad, grad, grad, jax.ShapeDtypeStruct((2 * N_PAIRS, QBLK, 2 * QBLK), F32)],
        args=[qa, ka, va, d_out, stat, bias], scratch=[carry_shape, carry_shape],
        vmem_limit=_vmem_limit(*([blk_bytes] * 9)), semantics=("arbitrary", "arbitrary"), carry=carry)


def _dz_assemble(dq_a, dk_a, dv_a, b_grads, dils, *, name):
    T = dq_a.shape[0]
    W = MIX_W

    def group_sum(x):
        u0 = x[:, 0:LANES] + x[:, LANES:2 * LANES]
        u1 = x[:, 2 * LANES:3 * LANES] + x[:, 3 * LANES:4 * LANES]
        s0 = u0 + pltpu.roll(u0, HEAD_DIM, 1)
        s1 = u1 + pltpu.roll(u1, HEAD_DIM, 1)
        return jnp.where(_head_sel(0), s0, s1)

    def body(*refs):
        dqa_ref, dka_ref, dva_ref = refs[0:3]
        b_refs = refs[3:12]
        dza_ref, dzb_ref, sa_ref, sb_ref, scratch = refs[12:]

        @pl.when(pl.program_id(0) == 0)
        def _():
            sa_ref[...] = jnp.zeros_like(sa_ref)
            sb_ref[...] = jnp.zeros_like(sb_ref)

        def put(dst, acc, col, part):
            w = part.shape[1]
            dst[:, col:col + w] = part.astype(dst.dtype)
            acc[:, col:col + w] += jnp.sum(part, axis=0, keepdims=True)

        put(dza_ref, sa_ref, 0, dqa_ref[...].astype(F32))
        put(dza_ref, sa_ref, W, group_sum(dka_ref[...].astype(F32)))
        put(dza_ref, sa_ref, W + LANES, group_sum(dva_ref[...].astype(F32)))
        for t in range(3):
            part = _to_tokens(b_refs[t], dils[0], scratch)
            for pat in range(1, len(dils)):
                part = part + _to_tokens(b_refs[3 * pat + t], dils[pat], scratch)
            put(dzb_ref, sb_ref, t * W, part)

    args = [dq_a, dk_a, dv_a] + [g[t] for g in b_grads for t in range(3)]
    return pl.pallas_call(
        body, name=name, grid=(T // ROW_BLK,),
        in_specs=[_row_spec(W)] * 3 + [_class_rows_spec(ROW_BLK, r, W) for r in dils for _ in range(3)],
        out_specs=(_row_spec(ZA_W), _row_spec(ZB_W), _vec_spec(ZA_W), _vec_spec(ZB_W)),
        out_shape=(jax.ShapeDtypeStruct((T, ZA_W), BF16), jax.ShapeDtypeStruct((T, ZB_W), BF16),
                   jax.ShapeDtypeStruct((1, ZA_W), F32), jax.ShapeDtypeStruct((1, ZB_W), F32)),
        scratch_shapes=[_token_scratch(ROW_BLK, W)],
        compiler_params=pltpu.CompilerParams(dimension_semantics=("arbitrary",)),
    )(*args)


def _place():
    x, y, c = lax.axis_index("x"), lax.axis_index("y"), lax.axis_index("c")
    chips = [(1 - x, y), (x, 1 - y), (1 - x, 1 - y)]
    return x, y, c, chips


def _cast_place(shard, k_idx, *, name):
    rs, cs = shard.shape
    rb = _row_block(rs, 256, 16)
    nblk = rs // rb

    def body(k_ref, s_ref, o_ref):
        o_ref[...] = s_ref[...].astype(o_ref.dtype)

    grid_spec = pltpu.PrefetchScalarGridSpec(
        num_scalar_prefetch=1, grid=(nblk,),
        in_specs=[pl.BlockSpec((rb, cs), lambda t, k_ref: (t, 0))],
        out_specs=pl.BlockSpec((rb, cs), lambda t, k_ref: (k_ref[0] * nblk + t, 0)))
    return pl.pallas_call(
        body, name=name, grid_spec=grid_spec, out_shape=jax.ShapeDtypeStruct((N_CHIPS * rs, cs), BF16),
        compiler_params=pltpu.CompilerParams(dimension_semantics=("parallel",)),
    )(k_idx, shard)


def _gather_carry(fulls, part=(0, 1)):
    n = len(fulls)
    p_idx, n_parts = part

    def piece(ref, chip_idx, half):
        rs = ref.shape[0] // N_CHIPS
        rh = rs // 2
        rows = rh // n_parts
        return ref.at[pl.ds(chip_idx * rs + half * rh + p_idx * rows, rows)]

    def copy(sems, sem, src_ref, dst_ref, to):
        return pltpu.make_async_remote_copy(src_ref=src_ref, dst_ref=dst_ref, send_sem=sems[0].at[sem],
                                            recv_sem=sems[1].at[sem], device_id=to, device_id_type=MESH)

    def ici_copy(ins, outs, sems, i, j, chip, k, c):
        return copy(sems, 3 * i + j, piece(ins[i], k, c), piece(outs[i], k, c), (*chip, c))

    def start(ins, outs, sems):
        x, y, c, chips = _place()
        for i in range(n):
            for j, chip in enumerate(chips):
                ici_copy(ins, outs, sems, i, j, chip, 2 * x + y, c).start()

    def finish(ins, outs, sems):
        x, y, c, chips = _place()
        sibling = (x, y, 1 - c)
        passed = []
        for i in range(n):
            for j, chip in enumerate(chips):
                region = piece(outs[i], 2 * chip[0] + chip[1], c)
                copy(sems, 3 * i + j, region, region, (*chip, c)).wait_recv()
                cp = copy(sems, 3 * n + 3 * i + j, region, region, sibling)
                cp.start()
                passed.append(cp)
        for i in range(n):
            for j, chip in enumerate(chips):
                region = piece(outs[i], 2 * chip[0] + chip[1], 1 - c)
                copy(sems, 3 * n + 3 * i + j, region, region, sibling).wait_recv()
        for i in range(n):
            for j, chip in enumerate(chips):
                ici_copy(ins, outs, sems, i, j, chip, 2 * x + y, c).wait_send()
        for cp in passed:
            cp.wait_send()

    return _Carry(fulls, [jax.ShapeDtypeStruct(a.shape, a.dtype) for a in fulls], {i: i for i in range(n)},
                  [pltpu.SemaphoreType.DMA((6 * n,)), pltpu.SemaphoreType.DMA((6 * n,))], start, finish)


def _pair_swap_carry(grads):
    n = len(grads)

    def copy(ins, outs, sems, i):
        x, y, c, _ = _place()
        return pltpu.make_async_remote_copy(src_ref=ins[i].at[:, 1 - c], dst_ref=outs[i], send_sem=sems[0].at[i],
                                            recv_sem=sems[1].at[i], device_id=(x, y, 1 - c), device_id_type=MESH)

    def start(ins, outs, sems):
        for i in range(n):
            copy(ins, outs, sems, i).start()

    def finish(ins, outs, sems):
        for i in range(n):
            copy(ins, outs, sems, i).wait()

    return _Carry(grads, [jax.ShapeDtypeStruct((a.shape[0], a.shape[2], a.shape[3]), a.dtype) for a in grads], {},
                  [pltpu.SemaphoreType.DMA((n,)), pltpu.SemaphoreType.DMA((n,))], start, finish)


def _pair_sum(g4, got, c_idx, *, name):
    _, _, rh, cs = g4.shape
    rb = _row_block(rh, 256, 16)

    def body(c_ref, own_ref, got_ref, o_ref):
        o_ref[...] = (own_ref[...].astype(F32) + got_ref[...].astype(F32)).astype(o_ref.dtype)

    grid_spec = pltpu.PrefetchScalarGridSpec(
        num_scalar_prefetch=1, grid=(N_CHIPS, rh // rb),
        in_specs=[pl.BlockSpec((None, None, rb, cs), lambda k, t, c_ref: (k, c_ref[0], t, 0)),
                  pl.BlockSpec((None, rb, cs), lambda k, t, c_ref: (k, t, 0))],
        out_specs=pl.BlockSpec((None, rb, cs), lambda k, t, c_ref: (k, t, 0)))
    return pl.pallas_call(
        body, name=name, grid_spec=grid_spec, out_shape=jax.ShapeDtypeStruct((N_CHIPS, rh, cs), BF16),
        compiler_params=pltpu.CompilerParams(dimension_semantics=("parallel", "parallel")),
    )(c_idx, g4, got)


def _chip_scatter_carry(sums):
    n = len(sums)

    def copies(ins, outs, sems):
        x, y, c, chips = _place()
        return [pltpu.make_async_remote_copy(src_ref=ins[i].at[2 * chip[0] + chip[1]], dst_ref=outs[i].at[j],
                                             send_sem=sems[0].at[3 * i + j], recv_sem=sems[1].at[3 * i + j],
                                             device_id=(*chip, c), device_id_type=MESH)
                for i in range(n) for j, chip in enumerate(chips)]

    def start(ins, outs, sems):
        for cp in copies(ins, outs, sems):
            cp.start()

    def finish(ins, outs, sems):
        for cp in copies(ins, outs, sems):
            cp.wait()

    return _Carry(sums, [jax.ShapeDtypeStruct((3,) + a.shape[1:], a.dtype) for a in sums], {},
                  [pltpu.SemaphoreType.DMA((3 * n,)), pltpu.SemaphoreType.DMA((3 * n,))], start, finish)


def _chip_sum(sums, got, kc_idx, *, name):
    _, rh, cs = sums.shape
    rb = _row_block(rh, 256, 16)
    nblk = rh // rb

    def body(kc_ref, own_ref, got_ref, o_ref):
        acc = own_ref[...].astype(F32)
        for j in range(3):
            acc = acc + got_ref[j].astype(F32)
        o_ref[...] = acc

    grid_spec = pltpu.PrefetchScalarGridSpec(
        num_scalar_prefetch=1, grid=(nblk,),
        in_specs=[pl.BlockSpec((None, rb, cs), lambda t, kc: (kc[0], t, 0)),
                  pl.BlockSpec((3, rb, cs), lambda t, kc: (0, t, 0))],
        out_specs=pl.BlockSpec((rb, cs), lambda t, kc: (kc[1] * nblk + t, 0)))
    return pl.pallas_call(
        body, name=name, grid_spec=grid_spec, out_shape=jax.ShapeDtypeStruct((2 * rh, cs), F32),
        compiler_params=pltpu.CompilerParams(dimension_semantics=("parallel",)),
    )(kc_idx, sums, got)


def _half_swap_carry(shards):
    n = len(shards)

    def copy(ins, outs, sems, i, to_my_half):
        x, y, c, _ = _place()
        rh = ins[i].shape[0] // 2
        half = c if to_my_half else 1 - c
        return pltpu.make_async_remote_copy(
            src_ref=ins[i].at[pl.ds(c * rh, rh)], dst_ref=outs[i].at[pl.ds(half * rh, rh)],
            send_sem=sems[0].at[i], recv_sem=sems[1].at[i], device_id=(x, y, 1 - c), device_id_type=MESH)

    def start(ins, outs, sems):
        for i in range(n):
            copy(ins, outs, sems, i, True).start()

    def finish(ins, outs, sems):
        for i in range(n):
            copy(ins, outs, sems, i, False).wait_recv()
        for i in range(n):
            copy(ins, outs, sems, i, True).wait_send()

    return _Carry(shards, [jax.ShapeDtypeStruct(a.shape, a.dtype) for a in shards], {i: i for i in range(n)},
                  [pltpu.SemaphoreType.DMA((n,)), pltpu.SemaphoreType.DMA((n,))], start, finish)


SMALL_ROW = 1024


def _small_layout(shapes):
    starts, row = [], 0
    for r, c in shapes:
        starts.append(row)
        row += -(-c // SMALL_ROW) if r == 1 else r
    return starts, -(-row // SUBLANES) * SUBLANES


def _small_pieces(shape, start):
    r, c = shape
    if r == 1:
        return [(start + q, min(SMALL_ROW, c - q * SMALL_ROW), q * SMALL_ROW) for q in range(-(-c // SMALL_ROW))]
    return None


def _small_all_reduce(parts, *, name):
    shapes = [a.shape for a in parts]
    starts, rows = _small_layout(shapes)
    n = len(parts)

    def body(*refs):
        ins, outs = refs[0:n], refs[n:2 * n]
        buf, send_sems, recv_sems = refs[2 * n:]
        x, y, c, _ = _place()
        me = 4 * x + 2 * y + c
        mine = buf.at[me]
        mine[...] = jnp.zeros((rows, SMALL_ROW), F32)
        for ref, shape, start in zip(ins, shapes, starts):
            pieces = _small_pieces(shape, start)
            if pieces is None:
                mine[start:start + shape[0], 0:shape[1]] = ref[...]
            else:
                for row, width, col in pieces:
                    mine[row:row + 1, 0:width] = ref[:, col:col + width]
        copies = []
        for d in range(1, 8):
            to = (1 - x if d & 4 else x, 1 - y if d & 2 else y, 1 - c if d & 1 else c)
            cp = pltpu.make_async_remote_copy(src_ref=mine, dst_ref=mine, send_sem=send_sems.at[d - 1],
                                              recv_sem=recv_sems.at[d - 1], device_id=to, device_id_type=MESH)
            cp.start()
            copies.append((cp, to))
        for d in range(1, 8):
            to = copies[d - 1][1]
            sender = 4 * to[0] + 2 * to[1] + to[2]
            pltpu.make_async_remote_copy(src_ref=mine, dst_ref=buf.at[sender], send_sem=send_sems.at[d - 1],
                                         recv_sem=recv_sems.at[d - 1], device_id=to, device_id_type=MESH).wait_recv()
        for cp, _ in copies:
            cp.wait_send()
        acc = buf[0]
        for s in range(1, 8):
            acc = acc + buf[s]
        mine[...] = acc
        for ref, shape, start in zip(outs, shapes, starts):
            pieces = _small_pieces(shape, start)
            if pieces is None:
                ref[...] = mine[start:start + shape[0], 0:shape[1]]
            else:
                for row, width, col in pieces:
                    ref[:, col:col + width] = mine[row:row + 1, 0:width]

    vm = pl.BlockSpec(memory_space=pltpu.VMEM)
    return pl.pallas_call(
        body, name=name, in_specs=[vm] * n, out_specs=tuple([vm] * n),
        out_shape=tuple(jax.ShapeDtypeStruct(s, F32) for s in shapes),
        scratch_shapes=[pltpu.VMEM((8, rows, SMALL_ROW), F32), pltpu.SemaphoreType.DMA((7,)),
                        pltpu.SemaphoreType.DMA((7,))],
    )(*parts)


def _adamw_small(ws, gs, ms, vs, *, name):
    n = len(ws)
    c1 = 1.0 - ADAM_B1 ** ADAM_STEP
    c2 = 1.0 - ADAM_B2 ** ADAM_STEP

    def body(*refs):
        for k in range(n):
            w_ref, g_ref, m_ref, v_ref = (refs[t * n + k] for t in range(4))
            d_ref, mo_ref, vo_ref = (refs[(4 + t) * n + k] for t in range(3))
            gv = g_ref[...]
            mn = ADAM_B1 * m_ref[...] + (1.0 - ADAM_B1) * gv
            vn = ADAM_B2 * v_ref[...] + (1.0 - ADAM_B2) * (gv * gv)
            d_ref[...] = -ADAM_LR * ((mn / c1) / (jnp.sqrt(vn / c2) + ADAM_EPS) + ADAM_WD * w_ref[...])
            mo_ref[...] = mn
            vo_ref[...] = vn

    vm = pl.BlockSpec(memory_space=pltpu.VMEM)
    shapes = tuple(jax.ShapeDtypeStruct(a.shape, F32) for a in ws)
    res = pl.pallas_call(
        body, name=name, in_specs=[vm] * (4 * n), out_specs=tuple([vm] * (3 * n)), out_shape=shapes * 3,
    )(*ws, *gs, *ms, *vs)
    return res[0:n], res[n:2 * n], res[2 * n:3 * n]


def _adamw(w, g, m, v, *, name):
    R, C = w.shape
    rb = _row_block(R, 256, SUBLANES)
    c1 = 1.0 - ADAM_B1 ** ADAM_STEP
    c2 = 1.0 - ADAM_B2 ** ADAM_STEP

    def body(w_ref, g_ref, m_ref, v_ref, d_ref, mo_ref, vo_ref):
        gv = g_ref[...]
        mn = ADAM_B1 * m_ref[...] + (1.0 - ADAM_B1) * gv
        vn = ADAM_B2 * v_ref[...] + (1.0 - ADAM_B2) * (gv * gv)
        d_ref[...] = -ADAM_LR * ((mn / c1) / (jnp.sqrt(vn / c2) + ADAM_EPS) + ADAM_WD * w_ref[...])
        mo_ref[...] = mn
        vo_ref[...] = vn

    spec = pl.BlockSpec((rb, C), lambda i: (i, 0))
    shp = jax.ShapeDtypeStruct((R, C), F32)
    return pl.pallas_call(
        body, name=name, grid=(R // rb,), in_specs=[spec] * 4, out_specs=(spec, spec, spec),
        out_shape=(shp, shp, shp), compiler_params=pltpu.CompilerParams(dimension_semantics=("parallel",)),
    )(w, g, m, v)


def _local_step(x, p, target, small, sched):
    T = x.shape[0]
    W = MIX_W
    rel_bias = small["rel_bias"]
    b_in_a, b_in_b = small["b_in"][:, 0:ZA_W], small["b_in"][:, ZA_W:]

    idx_np = [_t5_index(A_WINDOW - 1, 1)] + [_t5_index(window // dil, dil) for window, dil in B_PATTERNS]
    idx_all = jnp.asarray(np.stack(idx_np))
    n1, bias_all = sched.run(_prologue, x, small["ffn1_pre_g"], rel_bias, idx_all, name="prologue")
    w_gu1 = sched.weight("ffn1_w_gu")
    *gu1, a1 = sched.run(_ffn_up, n1, w_gu1, bm=512, name="ffn1_gu")
    w_d1 = sched.weight("ffn1_w_down")
    f1, h1, n2 = sched.run(_mm_resid_norm, a1, w_d1, x, small["ffn1_post_g"], 0.5, small["attn_pre_g"], bm=512,
                           name="ffn1_down")
    win_a, win_b = sched.weight("w_in")
    za = _mm_nn(n2, win_a, bm=1024, bn=ZA_W, out_dtype=BF16, name="in_proj_a", bias=b_in_a)
    dils = tuple(dil for _, dil in B_PATTERNS)
    zb_by_dil = _mm_nn_classes(n2, win_b, b_in_b, dils, bm=512, name="in_proj_b")

    k_a = za[:, W:W + LANES].reshape(T, 2, 1, HEAD_DIM)
    v_a = za[:, W + LANES:W + 2 * LANES].reshape(T, 2, 1, HEAD_DIM)
    k_a = jnp.broadcast_to(k_a, (T, 2, 4, HEAD_DIM)).reshape(T, W)
    v_a = jnp.broadcast_to(v_a, (T, 2, 4, HEAD_DIM)).reshape(T, W)
    a_args = dict(r=1, q_col=0, k_col=0, v_col=0, stride=0, pattern=0)
    o_a, lse_a1 = sched.run(_band_fwd, za, k_a, v_a, bias_all, name="band_a_fwd", **a_args)
    sink_row = jnp.repeat(small["sinks"], HEAD_DIM, axis=1)
    out_a, out_a_bf, lse_a = _merge([(o_a, lse_a1)], (1,), sink_row, name="merge_a")

    no_sink = jnp.full((1, W), NEG, F32)
    b_ctx, branches = [], []
    for t, dil in enumerate(dils):
        zb_r = zb_by_dil[t]
        args = dict(r=dil, q_col=0, k_col=1, v_col=2, stride=ZB_W // W, pattern=1 + t)
        branches.append(sched.run(_band_fwd, zb_r, zb_r, zb_r, bias_all, name=f"band_b{t}_fwd", **args))
        b_ctx.append((jnp.asarray(idx_np[1 + t]), zb_r, args))
    out_b, out_b_bf, lse_b = _merge(branches, dils, no_sink, name="merge_b")

    mix = jnp.concatenate([out_a_bf, out_b_bf], axis=1)
    w_out = sched.weight("w_out")
    att, h2, n3 = _mm_resid_norm(mix, w_out, h1, small["attn_post_g"], 1.0, small["ffn2_pre_g"], bm=512,
                                 name="out_proj", bias=small["b_out"])
    w_gu2, w_d2 = sched.weight("ffn2_w_gu"), sched.weight("ffn2_w_down")
    *gu2, a2 = _ffn_up(n3, w_gu2, bm=512, name="ffn2_gu")
    f2, h3, n4 = _mm_resid_norm(a2, w_d2, h2, small["ffn2_post_g"], 0.5, small["ple_pre_g"], bm=512,
                                name="ffn2_down")
    w_gate = sched.weight("w_ple_gate")
    p_bf = p.astype(BF16)
    dh4, de, dgp, loss, d_ple_post = _ple_head(n4, w_gate, p_bf, sched.weight("w_ple_proj"), h3, small["ple_post_g"],
                                               target, bm=512, name="ple_head")

    gs = {"ple_post_g": d_ple_post}
    sched.grad("w_ple_proj", _mm_tn(p_bf, de, bm=256, bn=1024, out_dtype=BF16, name="d_w_ple"))
    sched.grad("w_ple_gate", _mm_tn(n4, dgp, bm=512, bn=1024, out_dtype=BF16, name="d_w_gate"))
    dh3, df2, gs["ple_pre_g"], gs["ffn2_post_g"], _ = sched.run(
        _mm_nt_norms_bwd, dgp, w_gate, dh4, h3, small["ple_pre_g"], f2, small["ffn2_post_g"], 0.5, bm=512, name="d_n4")

    def ffn_bwd(df, a, gu, n, w_down, w_gu, tag, *norm_args):
        dgu = sched.run(_ffn_down_bwd, df, w_down, gu[0], gu[1], bm=256, name=f"ffn{tag}_d_a")
        sched.grad(f"ffn{tag}_w_down", _mm_tn(a, df, bm=256, bn=1024, out_dtype=BF16, name=f"ffn{tag}_d_w_down"))
        sched.grad(f"ffn{tag}_w_gu", sched.run(_mm_tn, n, dgu, bm=512, bn=1408, out_dtype=BF16,
                                              name=f"ffn{tag}_d_w_gu", n_stack=N_CHIPS))
        return sched.run(_mm_nt_norms_bwd, dgu, w_gu, *norm_args, bm=512, name=f"ffn{tag}_d_n")

    dh2, datt, gs["ffn2_pre_g"], gs["attn_post_g"], gs["b_out"] = ffn_bwd(
        df2, a2, gu2, n3, w_d2, w_gu2, "2", dh3, h2, small["ffn2_pre_g"], att, small["attn_post_g"], 1.0)
    sched.grad("w_out", _mm_tn(mix, datt, bm=512, bn=1024, out_dtype=BF16, name="d_w_out"))
    dmix = sched.run(_mm_nt, datt, w_out, bm=1024, bn=1024, out_dtype=F32, name="d_mix")

    do_a, stat_a, dsink, do_b, stat_b = _attn_delta(dmix, (out_a, out_b), (lse_a, lse_b), sink_row, dils,
                                                    name="attn_delta")
    gs["sinks"] = dsink[:, ::HEAD_DIM]
    dq_a, dk_a, dv_a, dbias_a = sched.run(_band_bwd, za, k_a, v_a, do_a, stat_a, bias_all, name="band_a_bwd", **a_args)
    d_rel_a = _bias_grad(dbias_a, jnp.asarray(idx_np[0]), name="d_rel_a")
    b_grads = []
    d_rel_b = None
    for t, (idx, zb_r, args) in enumerate(b_ctx):
        dq, dk, dv, dbias = sched.run(_band_bwd, zb_r, zb_r, zb_r, do_b[t], stat_b[t], bias_all,
                                      name=f"band_b{t}_bwd", **args)
        b_grads.append((dq, dk, dv))
        d_rel = _bias_grad(dbias, idx, name=f"d_rel_b{t}")
        d_rel_b = d_rel if d_rel_b is None else d_rel_b + d_rel
    gs["rel_bias"] = jnp.concatenate([d_rel_a[:, 0:2 * N_PAIRS], d_rel_b[:, 0:2 * N_PAIRS]], axis=1)
    dza, dzb, dsum_a, dsum_b = _dz_assemble(dq_a, dk_a, dv_a, b_grads, dils, name="d_z")
    gs["b_in"] = jnp.concatenate([dsum_a, dsum_b], axis=1)
    sched.grad("w_in", (_mm_tn(n2, dza, bm=512, bn=ZA_W, out_dtype=BF16, name="d_w_in_a"),
                        _mm_tn(n2, dzb, bm=512, bn=ZB_W // 2, out_dtype=BF16, name="d_w_in_b")))
    dn2_a = _mm_nt(dza, win_a, bm=1024, bn=1024, out_dtype=F32, name="d_n2_a")
    dh1, df1, gs["attn_pre_g"], gs["ffn1_post_g"], _ = sched.run(
        _mm_nt_norms_bwd, dzb, win_b, dh2, h1, small["attn_pre_g"], f1, small["ffn1_post_g"], 0.5, bm=512,
        name="d_n2_b", add=dn2_a)
    grad_x, gs["ffn1_pre_g"] = ffn_bwd(df1, a1, gu1, n1, w_d1, w_gu1, "1", dh1, x, small["ffn1_pre_g"], None, None, 0.0)
    return loss, grad_x, gs


def _to_compute(name, full):
    st = full.reshape(N_CHIPS, full.shape[0] // N_CHIPS, full.shape[1])
    if name in ("ffn1_w_gu", "ffn2_w_gu"):
        return st
    if name in ("w_in", "w_ple_proj"):
        whole = jnp.transpose(st, (1, 0, 2)).reshape(st.shape[1], -1)
        return (whole[:, 0:ZA_W], whole[:, ZA_W:]) if name == "w_in" else whole
    return full


def _to_comm(name, g):
    if name in ("w_in", "w_ple_proj"):
        whole = jnp.concatenate(g, axis=1) if name == "w_in" else g
        g = jnp.transpose(whole.reshape(whole.shape[0], N_CHIPS, -1), (1, 0, 2))
    rs = g.shape[1] if g.ndim == 3 else g.shape[0] // N_CHIPS
    return g.reshape(N_CHIPS, 2, rs // 2, g.shape[-1])


class _Schedule:
    GATHER = {
        "prologue": [("ffn1_w_gu", (0, 1))],
        "ffn1_gu": [("ffn1_w_down", (0, 1)), ("w_in", (0, 1))],
        "ffn1_down": [("w_out", (0, 1))],
        "band_a_fwd": [("ffn2_w_gu", (0, 2))],
        "band_b0_fwd": [("ffn2_w_gu", (1, 2))],
        "band_b1_fwd": [("ffn2_w_down", (0, 1))],
        "band_b2_fwd": [("w_ple_gate", (0, 1)), ("w_ple_proj", (0, 1))],
    }
    SWAP = {"d_n4": ["w_ple_proj", "w_ple_gate"], "ffn2_d_w_gu": ["ffn2_w_down"], "ffn2_d_n": ["ffn2_w_gu"],
            "d_mix": ["w_out"], "d_n2_b": ["w_in"], "ffn1_d_w_gu": ["ffn1_w_down"], "ffn1_d_n": ["ffn1_w_gu"]}
    SCATTER = {"ffn2_d_a": ["w_ple_proj", "w_ple_gate"], "ffn2_d_n": ["ffn2_w_down"], "band_a_bwd": ["ffn2_w_gu"],
               "band_b0_bwd": ["w_out"], "ffn1_d_a": ["w_in"], "ffn1_d_n": ["ffn1_w_down"]}

    def __init__(self, placed, c_idx, kc_idx):
        self.full = dict(placed)
        self.missing = {n: 1.0 for n in placed}
        self.c_idx, self.kc_idx = c_idx, kc_idx
        self.grads, self.swapped, self.pair_sums, self.scattered = {}, {}, {}, {}

    def _gather(self, entries):
        carries, after = [], []
        for part in sorted({pt for _, pt in entries}):
            names = [n for n, pt in entries if pt == part]
            carry = _gather_carry([self.full[n] for n in names], part)
            carries.append(carry)

            def done(carry=carry, names=names, part=part):
                for n, a in zip(names, carry.results):
                    self.full[n] = a
                    self.missing[n] -= 1.0 / part[1]
            after.append(done)
        return carries, after

    def weight(self, name):
        assert abs(self.missing[name]) < 1e-9, (name, self.missing[name])
        return _to_compute(name, self.full[name])

    def grad(self, name, g):
        self.grads[name] = _to_comm(name, g)

    def _swap(self, names):
        carry = _pair_swap_carry([self.grads[n] for n in names])

        def done():
            self.swapped.update(zip(names, carry.results))
        return carry, done

    def _scatter(self, names):
        for n in names:
            self.pair_sums[n] = _pair_sum(self.grads[n], self.swapped[n], self.c_idx, name=f"grad_pair_sum_{n}")
        carry = _chip_scatter_carry([self.pair_sums[n] for n in names])

        def done():
            self.scattered.update(zip(names, carry.results))
        return carry, done

    def run(self, fn, *args, name, **kw):
        carries, after = self._gather(self.GATHER.get(name, []))
        for names, make in ((self.SWAP.get(name), self._swap), (self.SCATTER.get(name), self._scatter)):
            if names:
                carry, done = make(names)
                carries.append(carry)
                after.append(done)
        out = fn(*args, name=name, carry=_join(carries), **kw)
        for done in after:
            done()
        return out

    def finish(self):
        left = [n for n in BIG if n not in self.scattered]
        to_swap = [n for n in left if n not in self.swapped]
        if to_swap:
            carry, done = self._swap(to_swap)
            _comm_call(carry, name="grad_pair_swap")
            done()
        if left:
            carry, done = self._scatter(left)
            _comm_call(carry, name="grad_chip_scatter")
            done()
        halves = [_chip_sum(self.pair_sums[n], self.scattered[n], self.kc_idx, name=f"grad_chip_sum_{n}") for n in BIG]
        carry = _half_swap_carry(halves)
        _comm_call(carry, name="grad_half_swap")
        return dict(zip(BIG, carry.results))


def kernel(x, p, rel_bias, ffn1_pre_g, ffn1_w_gu, ffn1_w_down, ffn1_post_g, attn_pre_g, w_in, b_in, sinks, w_out, b_out, attn_post_g, ffn2_pre_g, ffn2_w_gu, ffn2_w_down, ffn2_post_g, ple_pre_g, w_ple_gate, w_ple_proj, ple_post_g, loss_target, m_rel_bias, m_ffn1_pre_g, m_ffn1_w_gu, m_ffn1_w_down, m_ffn1_post_g, m_attn_pre_g, m_w_in, m_b_in, m_sinks, m_w_out, m_b_out, m_attn_post_g, m_ffn2_pre_g, m_ffn2_w_gu, m_ffn2_w_down, m_ffn2_post_g, m_ple_pre_g, m_w_ple_gate, m_w_ple_proj, m_ple_post_g, v_rel_bias, v_ffn1_pre_g, v_ffn1_w_gu, v_ffn1_w_down, v_ffn1_post_g, v_attn_pre_g, v_w_in, v_b_in, v_sinks, v_w_out, v_b_out, v_attn_post_g, v_ffn2_pre_g, v_ffn2_w_gu, v_ffn2_w_down, v_ffn2_post_g, v_ple_pre_g, v_w_ple_gate, v_w_ple_proj, v_ple_post_g):
    given = dict(locals())
    w = {n: given[n] for n in WEIGHTS}
    m = {n: given["m_" + n] for n in WEIGHTS}
    v = {n: given["v_" + n] for n in WEIGHTS}
    c_pos = lax.axis_index("c").astype(jnp.int32)
    k_pos = (2 * lax.axis_index("x") + lax.axis_index("y")).astype(jnp.int32)
    c_idx, k_idx, kc_idx = c_pos.reshape(1), k_pos.reshape(1), jnp.stack([k_pos, c_pos])

    sched = _Schedule({n: _cast_place(w[n][0], k_idx, name=f"place_{n}") for n in BIG}, c_idx, kc_idx)
    small = {n: (w[n] if n == "rel_bias" else w[n].reshape(1, -1)) for n in SMALL}

    loss_part, grad_x, gs = _local_step(x[0], p[0, 0], loss_target[0], small, sched)

    grads_big = sched.finish()

    *small_grads, loss_row = _small_all_reduce([gs[n] for n in SMALL] + [loss_part], name="small_all_reduce")
    loss = loss_row[0, 0]

    grads, delta, new_m, new_v = {}, {}, {}, {}
    for n in BIG:
        grads[n] = grads_big[n][None]
        d, mn, vn = _adamw(w[n][0], grads_big[n], m[n][0], v[n][0], name=f"adamw_{n}")
        delta[n], new_m[n], new_v[n] = d[None], mn[None], vn[None]
    d_s, m_s, v_s = _adamw_small([w[n] for n in SMALL], small_grads, [m[n] for n in SMALL], [v[n] for n in SMALL],
                                 name="adamw_small")
    for name, gg, dd, mm, vv in zip(SMALL, small_grads, d_s, m_s, v_s):
        grads[name], delta[name], new_m[name], new_v[name] = gg, dd, mm, vv

    return (loss, grad_x[None], *[grads[n] for n in WEIGHTS], *[delta[n] for n in WEIGHTS],
            *[new_m[n] for n in WEIGHTS], *[new_v[n] for n in WEIGHTS])
```

```python
import math

import jax
import jax.numpy as jnp
import numpy as np
from jax import lax
from jax.experimental import pallas as pl
from jax.experimental.pallas import tpu as pltpu

F32 = jnp.float32
BF16 = jnp.bfloat16
MESH = pl.DeviceIdType.MESH

EPS = 1e-6
NEG = -1e30
LANES = 128
SUBLANES = 8
HEAD_DIM = 64
QBLK = 128
N_PAIRS = 4
MIX_W = N_PAIRS * LANES
A_WINDOW = 128
B_PATTERNS = ((128, 1), (512, 4), (2048, 16))
NUM_BUCKETS = 32
MAX_DISTANCE = 2048
ZA_W = 768
ZB_W = 1536
N_CHIPS = 4
VMEM_BYTES_V7X = 64 * 2**20

ADAM_LR, ADAM_B1, ADAM_B2, ADAM_EPS, ADAM_WD, ADAM_STEP = 0.001, 0.9, 0.999, 1e-08, 0.01, 10

BIG = ("ffn1_w_gu", "ffn1_w_down", "w_in", "w_out", "ffn2_w_gu", "ffn2_w_down", "w_ple_gate", "w_ple_proj")
SMALL = ("rel_bias", "ffn1_pre_g", "ffn1_post_g", "attn_pre_g", "b_in", "sinks", "b_out", "attn_post_g",
         "ffn2_pre_g", "ffn2_post_g", "ple_pre_g", "ple_post_g")
WEIGHTS = ("rel_bias", "ffn1_pre_g", "ffn1_w_gu", "ffn1_w_down", "ffn1_post_g", "attn_pre_g", "w_in", "b_in",
           "sinks", "w_out", "b_out", "attn_post_g", "ffn2_pre_g", "ffn2_w_gu", "ffn2_w_down", "ffn2_post_g",
           "ple_pre_g", "w_ple_gate", "w_ple_proj", "ple_post_g")


def _vmem_limit(*block_bytes):
    need = 2 * sum(block_bytes) + max(block_bytes) * 2 + (4 << 20)
    return int(min(max(need, 32 << 20), VMEM_BYTES_V7X - (6 << 20)))


def _nbytes(shape, dtype):
    return int(np.prod(shape)) * jnp.dtype(dtype).itemsize


def _row_block(rows, cap, mult):
    for cand in range(min(rows, cap), 0, -1):
        if rows % cand == 0 and cand % mult == 0:
            return cand
    raise ValueError((rows, cap, mult))


class _Carry:
    def __init__(self, operands, out_shapes, aliases, sems, start, finish):
        self.operands, self.out_shapes, self.aliases, self.sems = list(operands), list(out_shapes), dict(aliases), list(sems)
        self.start, self.finish = start, finish
        self.results = None


def _join(carries):
    carries = [c for c in carries if c is not None]
    if not carries:
        return None
    if len(carries) == 1:
        return carries[0]
    offs, pos = [], [0, 0, 0]
    for c in carries:
        offs.append(tuple(pos))
        pos = [pos[0] + len(c.operands), pos[1] + len(c.out_shapes), pos[2] + len(c.sems)]
    aliases = {}
    for c, (oi, oo, _) in zip(carries, offs):
        aliases.update({oi + i: oo + o for i, o in c.aliases.items()})

    def run(which):
        def fn(ins, outs, sems):
            for c, (oi, oo, os_) in zip(carries, offs):
                getattr(c, which)(ins[oi:oi + len(c.operands)], outs[oo:oo + len(c.out_shapes)],
                                  sems[os_:os_ + len(c.sems)])
        return fn

    joined = _Carry([a for c in carries for a in c.operands], [s for c in carries for s in c.out_shapes], aliases,
                    [s for c in carries for s in c.sems], run("start"), run("finish"))
    joined.parts = (carries, offs)
    return joined


def _set_results(carry, outs):
    carry.results = list(outs)
    if hasattr(carry, "parts"):
        for c, (_, oo, _) in zip(*carry.parts):
            _set_results(c, outs[oo:oo + len(c.out_shapes)])


ANY = pl.BlockSpec(memory_space=pl.ANY)


def _call(body, *, name, grid, in_specs, out_specs, out_shape, args, scratch=(), vmem_limit=None, carry=None,
          semantics=None):
    n_ci, n_co, n_cs = len(args), len(out_shape), len(scratch)
    semantics = semantics or ("parallel",) * len(grid)
    if carry is None:
        res = pl.pallas_call(
            body, name=name, grid=grid, in_specs=list(in_specs), out_specs=tuple(out_specs), out_shape=tuple(out_shape),
            scratch_shapes=list(scratch),
            compiler_params=pltpu.CompilerParams(dimension_semantics=semantics, vmem_limit_bytes=vmem_limit),
        )(*args)
        return list(res)
    n_pi, n_po = len(carry.operands), len(carry.out_shapes)

    def full_body(*refs):
        ci, pi = refs[:n_ci], refs[n_ci:n_ci + n_pi]
        o0 = n_ci + n_pi
        co, po = refs[o0:o0 + n_co], refs[o0 + n_co:o0 + n_co + n_po]
        s0 = o0 + n_co + n_po
        cs, ps = refs[s0:s0 + n_cs], refs[s0 + n_cs:]
        ids = [pl.program_id(ax) for ax in range(len(grid))]
        first, last = ids[0] == 0, ids[0] == grid[0] - 1
        for ax in range(1, len(grid)):
            first, last = first & (ids[ax] == 0), last & (ids[ax] == grid[ax] - 1)

        @pl.when(first)
        def _():
            carry.start(pi, po, ps)

        body(*ci, *co, *cs)

        @pl.when(last)
        def _():
            carry.finish(pi, po, ps)

    res = pl.pallas_call(
        full_body, name=name, grid=grid, in_specs=list(in_specs) + [ANY] * n_pi,
        out_specs=tuple(out_specs) + tuple([ANY] * n_po), out_shape=tuple(out_shape) + tuple(carry.out_shapes),
        scratch_shapes=list(scratch) + carry.sems,
        input_output_aliases={n_ci + i: n_co + o for i, o in carry.aliases.items()},
        compiler_params=pltpu.CompilerParams(dimension_semantics=("arbitrary",) * len(grid),
                                             vmem_limit_bytes=vmem_limit),
    )(*args, *carry.operands)
    _set_results(carry, res[n_co:])
    return list(res[:n_co])


def _comm_call(carry, *, name):
    n_pi, n_po = len(carry.operands), len(carry.out_shapes)

    def body(*refs):
        pi, po, ps = refs[:n_pi], refs[n_pi:n_pi + n_po], refs[n_pi + n_po:]
        carry.start(pi, po, ps)
        carry.finish(pi, po, ps)

    res = pl.pallas_call(
        body, name=name, in_specs=[ANY] * n_pi, out_specs=tuple([ANY] * n_po), out_shape=tuple(carry.out_shapes),
        scratch_shapes=carry.sems, input_output_aliases=dict(carry.aliases),
    )(*carry.operands)
    _set_results(carry, res)


def _mm_nn(a, b, *, bm, bn, out_dtype, name, bias=None, carry=None):
    M, K = a.shape
    stacked = b.ndim == 3
    N = b.shape[0] * b.shape[2] if stacked else b.shape[1]
    assert M % bm == 0 and N % bn == 0 and (not stacked or bn == b.shape[2])

    def body(*refs):
        a_ref, b_ref = refs[0], refs[1]
        o_ref = refs[-1]
        acc = jnp.dot(a_ref[...], b_ref[...], preferred_element_type=F32)
        if bias is not None:
            acc = acc + refs[2][...]
        o_ref[...] = acc.astype(o_ref.dtype)

    if stacked:
        b_spec = pl.BlockSpec((None, K, bn), lambda i, j: (j, 0, 0))
    else:
        b_spec = pl.BlockSpec((K, bn), lambda i, j: (0, j))
    in_specs = [pl.BlockSpec((bm, K), lambda i, j: (i, 0)), b_spec]
    args = [a, b]
    if bias is not None:
        in_specs.append(pl.BlockSpec((1, bn), lambda i, j: (0, j)))
        args.append(bias)
    limit = _vmem_limit(_nbytes((bm, K), a.dtype), _nbytes((K, bn), b.dtype), _nbytes((bm, bn), F32))
    return _call(body, name=name, grid=(M // bm, N // bn), in_specs=in_specs,
                 out_specs=[pl.BlockSpec((bm, bn), lambda i, j: (i, j))],
                 out_shape=[jax.ShapeDtypeStruct((M, N), out_dtype)], args=args, vmem_limit=limit, carry=carry)[0]


def _mm_nt(a, b, *, bm, bn, out_dtype, name, add=None, carry=None):
    M, K = a.shape
    stacked = b.ndim == 3
    N = b.shape[1] if stacked else b.shape[0]
    n_sh = b.shape[0] if stacked else 1
    ks = b.shape[2] if stacked else K
    assert M % bm == 0 and N % bn == 0 and n_sh * ks == K
    dn = (((1,), (1,)), ((), ()))

    def body(*refs):
        a_ref, b_ref = refs[0], refs[1]
        o_ref = refs[-1]
        if stacked:
            acc = lax.dot_general(a_ref[:, 0:ks], b_ref[0], dn, preferred_element_type=F32)
            for s in range(1, n_sh):
                acc = acc + lax.dot_general(a_ref[:, s * ks:(s + 1) * ks], b_ref[s], dn, preferred_element_type=F32)
        else:
            acc = lax.dot_general(a_ref[...], b_ref[...], dn, preferred_element_type=F32)
        if add is not None:
            acc = acc + refs[2][...]
        o_ref[...] = acc.astype(o_ref.dtype)

    if stacked:
        b_spec = pl.BlockSpec((n_sh, bn, ks), lambda i, j: (0, j, 0))
    else:
        b_spec = pl.BlockSpec((bn, K), lambda i, j: (j, 0))
    in_specs = [pl.BlockSpec((bm, K), lambda i, j: (i, 0)), b_spec]
    args = [a, b]
    if add is not None:
        rows = bm if add.shape[0] == M else 1
        in_specs.append(pl.BlockSpec((rows, bn), lambda i, j: (i if rows == bm else 0, j)))
        args.append(add)
    limit = _vmem_limit(_nbytes((bm, K), a.dtype), _nbytes((bn, K), b.dtype), _nbytes((bm, bn), F32))
    return _call(body, name=name, grid=(M // bm, N // bn), in_specs=in_specs,
                 out_specs=[pl.BlockSpec((bm, bn), lambda i, j: (i, j))],
                 out_shape=[jax.ShapeDtypeStruct((M, N), out_dtype)], args=args, vmem_limit=limit, carry=carry)[0]


def _mm_tn(a, b, *, bm, bn, out_dtype, name, n_stack=None, carry=None):
    T, M = a.shape
    N = b.shape[1]
    assert M % bm == 0 and N % bn == 0 and (n_stack is None or bn * n_stack == N)
    dn = (((0,), (0,)), ((), ()))

    def body(a_ref, b_ref, o_ref):
        acc = lax.dot_general(a_ref[...], b_ref[...], dn, preferred_element_type=F32)
        o_ref[...] = acc.astype(o_ref.dtype)

    if n_stack is None:
        out_spec = pl.BlockSpec((bm, bn), lambda i, j: (i, j))
        out_shape = jax.ShapeDtypeStruct((M, N), out_dtype)
    else:
        out_spec = pl.BlockSpec((None, bm, bn), lambda i, j: (j, i, 0))
        out_shape = jax.ShapeDtypeStruct((n_stack, M, bn), out_dtype)
    limit = _vmem_limit(_nbytes((T, bm), a.dtype), _nbytes((T, bn), b.dtype), _nbytes((bm, bn), F32))
    return _call(body, name=name, grid=(M // bm, N // bn),
                 in_specs=[pl.BlockSpec((T, bm), lambda i, j: (0, i)), pl.BlockSpec((T, bn), lambda i, j: (0, j))],
                 out_specs=[out_spec], out_shape=[out_shape], args=[a, b], vmem_limit=limit, carry=carry)[0]


ROW_BLK = 256


def _rstd(x):
    return lax.rsqrt(jnp.mean(x * x, axis=-1, keepdims=True) + EPS)


def _norm_bwd(xhat, rstd, dxhat):
    return rstd * (dxhat - xhat * jnp.mean(dxhat * xhat, axis=-1, keepdims=True))


def _row_spec(cols):
    return pl.BlockSpec((ROW_BLK, cols), lambda i: (i, 0))


def _vec_spec(cols):
    return pl.BlockSpec((1, cols), lambda i: (0, 0))


def _prologue(x, g, rel_bias, idx_all, *, name, carry=None):
    T, D = x.shape
    n_pat = idx_all.shape[0]
    heads = 2 * N_PAIRS
    steps = n_pat * heads
    rows = T // steps

    def body(rb_ref, x_ref, g_ref, idx_ref, n_ref, bias_ref):
        s = pl.program_id(0)
        head = jnp.where(s < heads, s, heads + s % heads)
        xv = x_ref[...]
        n_ref[...] = (xv * _rstd(xv) * g_ref[...]).astype(n_ref.dtype)
        idxv = idx_ref[...]
        acc = jnp.where(idxv < 0, NEG, 0.0).astype(F32)
        for b in range(NUM_BUCKETS):
            acc = acc + jnp.where(idxv == b, rb_ref[b, head], 0.0)
        bias_ref[0] = acc
        kap = lax.broadcasted_iota(jnp.int32, (1, 2 * QBLK), 1)
        bias_ref[1] = jnp.where(kap < QBLK, NEG, acc)

    return _call(
        body, name=name, grid=(steps,),
        in_specs=[pl.BlockSpec(memory_space=pltpu.SMEM), pl.BlockSpec((rows, D), lambda s: (s, 0)),
                  pl.BlockSpec((1, D), lambda s: (0, 0)),
                  pl.BlockSpec((None, QBLK, 2 * QBLK), lambda s: (s // heads, 0, 0))],
        out_specs=[pl.BlockSpec((rows, D), lambda s: (s, 0)),
                   pl.BlockSpec((None, 2, None, QBLK, 2 * QBLK), lambda s: (s // heads, 0, s % heads, 0, 0))],
        out_shape=[jax.ShapeDtypeStruct((T, D), BF16),
                   jax.ShapeDtypeStruct((n_pat, 2, heads, QBLK, 2 * QBLK), F32)],
        args=[rel_bias, x, g, idx_all], carry=carry)


def _mm_resid_norm(a, b, h, g_post, coef, g_next, *, bm, name, bias=None, carry=None):
    M, K = a.shape
    N = b.shape[1]

    def body(*refs):
        a_ref, b_ref, h_ref, gp_ref, gn_ref = refs[0:5]
        f_ref, hn_ref, n_ref = refs[-3:]
        f = jnp.dot(a_ref[...], b_ref[...], preferred_element_type=F32)
        if bias is not None:
            f = f + refs[5][...]
        f_ref[...] = f
        hn = h_ref[...] + coef * (f * _rstd(f) * gp_ref[...])
        hn_ref[...] = hn
        n_ref[...] = (hn * _rstd(hn) * gn_ref[...]).astype(n_ref.dtype)

    row = lambda w: pl.BlockSpec((bm, w), lambda i: (i, 0))
    vec = pl.BlockSpec((1, N), lambda i: (0, 0))
    in_specs = [row(K), pl.BlockSpec((K, N), lambda i: (0, 0), pipeline_mode=pl.Buffered(1)), row(N), vec, vec]
    args = [a, b, h, g_post, g_next]
    if bias is not None:
        in_specs.append(vec)
        args.append(bias)
    limit = _vmem_limit(_nbytes((bm, K), a.dtype), _nbytes((K, N), b.dtype) // 2, 4 * _nbytes((bm, N), F32))
    return _call(body, name=name, grid=(M // bm,), in_specs=in_specs, out_specs=[row(N), row(N), row(N)],
                 out_shape=[jax.ShapeDtypeStruct((M, N), F32), jax.ShapeDtypeStruct((M, N), F32),
                            jax.ShapeDtypeStruct((M, N), BF16)], args=args, vmem_limit=limit, carry=carry)


def _mm_nt_norms_bwd(a, b, dh_in, h, g_pre, f, g_post, coef, *, bm, name, add=None, b_is_kn=False, carry=None):
    M, K = a.shape
    stacked = b.ndim == 3
    N = b.shape[1] if (stacked or b_is_kn) else b.shape[0]
    n_sh = b.shape[0] if stacked else 1
    ks = b.shape[2] if stacked else K
    assert n_sh * ks == K
    dn_dims = (((1,), (0,)), ((), ())) if b_is_kn else (((1,), (1,)), ((), ()))
    with_f = f is not None
    n_in = 5 + (2 if with_f else 0) + (1 if add is not None else 0)

    def body(*refs):
        a_ref, b_ref, dhi_ref, h_ref, gpre_ref = refs[0:5]
        outs = refs[n_in:]
        if stacked:
            dn = lax.dot_general(a_ref[:, 0:ks], b_ref[0], dn_dims, preferred_element_type=F32)
            for s in range(1, n_sh):
                dn = dn + lax.dot_general(a_ref[:, s * ks:(s + 1) * ks], b_ref[s], dn_dims, preferred_element_type=F32)
        else:
            dn = lax.dot_general(a_ref[...], b_ref[...], dn_dims, preferred_element_type=F32)
        if add is not None:
            dn = dn + refs[n_in - 1][...]

        @pl.when(pl.program_id(0) == 0)
        def _():
            for acc in (outs[2:] if with_f else outs[1:]):
                acc[...] = jnp.zeros_like(acc)

        hv = h_ref[...]
        r = _rstd(hv)
        hh = hv * r
        dh = dhi_ref[...] + _norm_bwd(hh, r, dn * gpre_ref[...])
        outs[0][...] = dh
        if not with_f:
            outs[1][...] += jnp.sum(dn * hh, axis=0, keepdims=True)
            return
        f_ref, gpost_ref = refs[5], refs[6]
        df_ref, dgpre_ref, dgpost_ref, dsum_ref = outs[1:]
        dgpre_ref[...] += jnp.sum(dn * hh, axis=0, keepdims=True)
        fv = f_ref[...]
        rf = _rstd(fv)
        fh = fv * rf
        dy = coef * dh
        dgpost_ref[...] += jnp.sum(dy * fh, axis=0, keepdims=True)
        df = _norm_bwd(fh, rf, dy * gpost_ref[...])
        dsum_ref[...] += jnp.sum(df, axis=0, keepdims=True)
        df_ref[...] = df.astype(df_ref.dtype)

    row = lambda w: pl.BlockSpec((bm, w), lambda i: (i, 0))
    vec = pl.BlockSpec((1, N), lambda i: (0, 0))
    once = pl.Buffered(1)
    if stacked:
        b_spec = pl.BlockSpec((n_sh, N, ks), lambda i: (0, 0, 0), pipeline_mode=once)
    else:
        b_spec = pl.BlockSpec(b.shape, lambda i: (0, 0), pipeline_mode=once)
    in_specs = [row(K), b_spec, row(N), row(N), vec]
    args = [a, b, dh_in, h, g_pre]
    if with_f:
        in_specs += [row(N), vec]
        args += [f, g_post]
    if add is not None:
        in_specs.append(row(N))
        args.append(add)
    vec_shape = jax.ShapeDtypeStruct((1, N), F32)
    out_specs = [row(N)] + ([row(N), vec, vec, vec] if with_f else [vec])
    out_shape = [jax.ShapeDtypeStruct((M, N), F32)]
    out_shape += [jax.ShapeDtypeStruct((M, N), BF16), vec_shape, vec_shape, vec_shape] if with_f else [vec_shape]
    limit = _vmem_limit(_nbytes((bm, K), a.dtype), _nbytes((N, K), b.dtype) // 2, 6 * _nbytes((bm, N), F32))
    return _call(body, name=name, grid=(M // bm,), in_specs=in_specs, out_specs=out_specs, out_shape=out_shape,
                 args=args, vmem_limit=limit, semantics=("arbitrary",), carry=carry)


def _ffn_up(n, w_gu, *, bm, name, carry=None):
    M, K = n.shape
    n_sh, _, ns = w_gu.shape
    half = n_sh // 2

    def body(n_ref, wg_ref, wu_ref, g_ref, u_ref, a_ref):
        nv = n_ref[...]
        g = jnp.dot(nv, wg_ref[...], preferred_element_type=F32)
        u = jnp.dot(nv, wu_ref[...], preferred_element_type=F32)
        g_ref[...] = g.astype(g_ref.dtype)
        u_ref[...] = u.astype(u_ref.dtype)
        a_ref[...] = (g * jax.nn.sigmoid(g) * u).astype(a_ref.dtype)

    out_spec = pl.BlockSpec((bm, ns), lambda i, j: (i, j))
    out = jax.ShapeDtypeStruct((M, half * ns), BF16)
    limit = _vmem_limit(_nbytes((bm, K), n.dtype), 2 * _nbytes((K, ns), w_gu.dtype), 3 * _nbytes((bm, ns), F32))
    return _call(body, name=name, grid=(M // bm, half),
                 in_specs=[pl.BlockSpec((bm, K), lambda i, j: (i, 0)),
                           pl.BlockSpec((None, K, ns), lambda i, j: (j, 0, 0)),
                           pl.BlockSpec((None, K, ns), lambda i, j: (j + half, 0, 0))],
                 out_specs=[out_spec, out_spec, out_spec], out_shape=[out, out, out], args=[n, w_gu, w_gu],
                 vmem_limit=limit, carry=carry)


def _ffn_down_bwd(df, w_down, g, u, *, bm, name, carry=None):
    M, D = df.shape
    F = w_down.shape[0]
    dn = (((1,), (1,)), ((), ()))

    def body(df_ref, w_ref, g_ref, u_ref, o_ref):
        da = lax.dot_general(df_ref[...], w_ref[...], dn, preferred_element_type=F32)
        gv = g_ref[...].astype(F32)
        s = jax.nn.sigmoid(gv)
        o_ref[:, 0:F] = (da * u_ref[...].astype(F32) * (s * (1.0 + gv * (1.0 - s)))).astype(o_ref.dtype)
        o_ref[:, F:2 * F] = (da * (gv * s)).astype(o_ref.dtype)

    row = lambda w: pl.BlockSpec((bm, w), lambda i: (i, 0))
    limit = _vmem_limit(_nbytes((F, D), w_down.dtype), 2 * _nbytes((bm, F), BF16), _nbytes((bm, 2 * F), BF16),
                        2 * _nbytes((bm, F), F32))
    return _call(body, name=name, grid=(M // bm,),
                 in_specs=[row(D), pl.BlockSpec((F, D), lambda i: (0, 0)), row(F), row(F)],
                 out_specs=[row(2 * F)], out_shape=[jax.ShapeDtypeStruct((M, 2 * F), BF16)],
                 args=[df, w_down, g, u], vmem_limit=limit, carry=carry)[0]


def _ple_head(n4, w_gate, p, w_ple, h3, g_post, target, *, bm, name):
    T, D = h3.shape
    kp = p.shape[1]

    def body(n_ref, wg_ref, p_ref, wp_ref, h_ref, g_ref, t_ref, dh_ref, de_ref, dgp_ref, loss_ref, dg_ref):
        @pl.when(pl.program_id(0) == 0)
        def _():
            loss_ref[...] = jnp.zeros_like(loss_ref)
            dg_ref[...] = jnp.zeros_like(dg_ref)

        gate = jax.nn.sigmoid(jnp.dot(n_ref[...], wg_ref[...], preferred_element_type=F32))
        ev = jnp.dot(p_ref[...], wp_ref[...], preferred_element_type=F32)
        ge = gate * ev
        r = _rstd(ge)
        gh = ge * r
        gpost = g_ref[...]
        diff = h_ref[...] + gh * gpost - t_ref[...]
        loss_ref[...] += jnp.sum(diff * diff) * (0.5 / D)
        dh = diff * (1.0 / D)
        dh_ref[...] = dh
        dg_ref[...] += jnp.sum(dh * gh, axis=0, keepdims=True)
        dge = _norm_bwd(gh, r, dh * gpost)
        de_ref[...] = (dge * gate).astype(de_ref.dtype)
        dgp_ref[...] = (dge * ev * gate * (1.0 - gate)).astype(dgp_ref.dtype)

    row = lambda w: pl.BlockSpec((bm, w), lambda i: (i, 0))
    whole = lambda a: pl.BlockSpec(a.shape, lambda i: (0, 0), pipeline_mode=pl.Buffered(1))
    limit = _vmem_limit(_nbytes((bm, D), BF16), _nbytes(w_gate.shape, w_gate.dtype) // 2, 6 * _nbytes((bm, D), F32))
    return pl.pallas_call(
        body, name=name, grid=(T // bm,),
        in_specs=[row(D), whole(w_gate), row(kp), whole(w_ple), row(D), _vec_spec(D), row(D)],
        out_specs=(row(D), row(D), row(D), _vec_spec(LANES), _vec_spec(D)),
        out_shape=(jax.ShapeDtypeStruct((T, D), F32), jax.ShapeDtypeStruct((T, D), BF16),
                   jax.ShapeDtypeStruct((T, D), BF16), jax.ShapeDtypeStruct((1, LANES), F32),
                   jax.ShapeDtypeStruct((1, D), F32)),
        compiler_params=pltpu.CompilerParams(dimension_semantics=("arbitrary",), vmem_limit_bytes=limit),
    )(n4, w_gate, p, w_ple, h3, g_post, target)


def _t5_index(max_dist, stride):
    rho = np.arange(QBLK)[:, None]
    kap = np.arange(2 * QBLK)[None, :]
    d = rho + QBLK - kap
    valid = (d >= 0) & (d <= max_dist)
    n = np.maximum(d, 0) * stride
    max_exact = NUM_BUCKETS // 2
    nf = np.maximum(n, 1).astype(np.float32)
    large = max_exact + (np.log(nf / np.float32(max_exact)) / np.float32(math.log(MAX_DISTANCE / max_exact))
                         * np.float32(NUM_BUCKETS - max_exact)).astype(np.int32)
    large = np.minimum(large, NUM_BUCKETS - 1)
    bucket = np.where(n < max_exact, n, large)
    return np.where(valid, bucket, -1).astype(np.int32)


def _bias_grad(d_bias, idx, *, name):
    def body(db_ref, idx_ref, o_ref):
        h = pl.program_id(0)

        @pl.when(h == 0)
        def _():
            o_ref[...] = jnp.zeros_like(o_ref)

        idxv = idx_ref[...]
        dv = db_ref[0]
        rows = lax.broadcasted_iota(jnp.int32, (NUM_BUCKETS, LANES), 0)
        lanes = lax.broadcasted_iota(jnp.int32, (NUM_BUCKETS, LANES), 1)
        acc = o_ref[...]
        for b in range(NUM_BUCKETS):
            s = jnp.sum(jnp.where(idxv == b, dv, 0.0))
            acc = acc + jnp.where((rows == b) & (lanes == h), s, 0.0)
        o_ref[...] = acc

    return pl.pallas_call(
        body, name=name, grid=(2 * N_PAIRS,),
        in_specs=[pl.BlockSpec((1, QBLK, 2 * QBLK), lambda h: (h, 0, 0)),
                  pl.BlockSpec((QBLK, 2 * QBLK), lambda h: (0, 0))],
        out_specs=pl.BlockSpec((NUM_BUCKETS, LANES), lambda h: (0, 0)),
        out_shape=jax.ShapeDtypeStruct((NUM_BUCKETS, LANES), F32),
        compiler_params=pltpu.CompilerParams(dimension_semantics=("arbitrary",)),
    )(d_bias, idx)


def _lane():
    return lax.broadcasted_iota(jnp.int32, (1, LANES), 1)


def _head_sel(h):
    return (_lane() < HEAD_DIM) if h == 0 else (_lane() >= HEAD_DIM)


def _stat_lo():
    return (_lane() % HEAD_DIM) < (HEAD_DIM // 2)


def _stack_heads(t, scale=None):
    if scale is not None:
        t = t * scale
    zero = jnp.zeros_like(t)
    return jnp.concatenate([jnp.where(_head_sel(0), t, zero), jnp.where(_head_sel(1), t, zero)], axis=0)


def _class_spec(L, r, cols, stride, pps):
    chunks = N_PAIRS // pps
    mode = pl.Buffered(1) if r * chunks == 1 else None
    return pl.BlockSpec((L, MIX_W // chunks), lambda j, c: (0, (cols + j * stride) * chunks + c), pipeline_mode=mode)


def _rows(b, n_blocks=1):
    return pl.ds(pl.multiple_of(b * QBLK, QBLK), n_blocks * QBLK)


def _key_window(ref, b, cols, first):
    if first:
        blk = ref[0:QBLK, cols]
        return jnp.concatenate([blk, blk], axis=0)
    return ref[_rows(b - 1, 2), cols]


def _band_fwd(qa, ka, va, bias, *, r, q_col, k_col, v_col, stride, pattern, name, carry=None):
    L = qa.shape[0]
    nb = L // QBLK
    dn = (((1,), (1,)), ((), ()))
    scale = HEAD_DIM ** -0.5

    pps = N_PAIRS

    def body(q_ref, k_ref, v_ref, bias_ref, o_ref, lse_ref):
        sel0 = _head_sel(0)
        pair0 = pl.program_id(1) * pps

        def block(b, first):
            rows = _rows(b)
            for i in range(pps):
                cols = slice(i * LANES, (i + 1) * LANES)
                q2 = _stack_heads(q_ref[rows, cols], scale)
                k = _key_window(k_ref, b, cols, first)
                v = _key_window(v_ref, b, cols, first)
                bias2 = bias_ref[1 if first else 0, pl.ds(2 * (pair0 + i), 2)].reshape(2 * QBLK, 2 * QBLK)
                s = lax.dot_general(q2, k, dn, preferred_element_type=F32) + bias2
                m = jnp.max(s, axis=1, keepdims=True)
                p = jnp.exp(s - m)
                l = jnp.sum(p, axis=1, keepdims=True)
                o = jnp.dot(p.astype(v.dtype), v, preferred_element_type=F32) / l
                lse = m + jnp.log(l)
                o_ref[rows, cols] = jnp.where(sel0, o[0:QBLK], o[QBLK:]).astype(o_ref.dtype)
                lse_ref[rows, cols] = jnp.where(sel0, lse[0:QBLK], lse[QBLK:])

        block(0, True)
        if nb > 1:
            pl.loop(1, nb)(lambda b: block(b, False))

    bias_spec = pl.BlockSpec((None, 2, 2 * N_PAIRS, QBLK, 2 * QBLK), lambda j, c: (pattern, 0, 0, 0, 0))
    out_spec = _class_spec(L, r, 0, 1, pps)
    blk_bytes = _nbytes((L, pps * LANES), F32)
    return _call(
        body, name=name, grid=(r, N_PAIRS // pps),
        in_specs=[_class_spec(L, r, q_col, stride, pps), _class_spec(L, r, k_col, stride, pps),
                  _class_spec(L, r, v_col, stride, pps), bias_spec],
        out_specs=[out_spec, out_spec],
        out_shape=[jax.ShapeDtypeStruct((L, r * MIX_W), BF16), jax.ShapeDtypeStruct((L, r * MIX_W), F32)],
        args=[qa, ka, va, bias], vmem_limit=_vmem_limit(*([blk_bytes] * 4)), carry=carry)


def _class_rows_spec(rows, r, width):
    return pl.BlockSpec((rows // r, r * width), lambda i, *_: (i, 0))


def _token_scratch(rows, width):
    return pltpu.VMEM((width // LANES, rows, LANES), F32)


def _fill_tokens(scratch, value):
    for c in range(scratch.shape[0]):
        scratch[c] = value[:, c * LANES:(c + 1) * LANES]


def _read_tokens(scratch):
    return jnp.concatenate([scratch[c] for c in range(scratch.shape[0])], axis=1)


def _to_tokens(blk_ref, r, scratch):
    if r == 1:
        return blk_ref[...].astype(F32)
    nt, rows, _ = scratch.shape
    for j in range(r):
        for c in range(nt):
            lanes = slice((j * nt + c) * LANES, (j * nt + c + 1) * LANES)
            scratch.at[c][pl.ds(j, rows // r, stride=r), :] = blk_ref[:, lanes].astype(F32)
    return _read_tokens(scratch)


def _from_tokens(dst_ref, r, scratch):
    nt, rows, _ = scratch.shape
    if r == 1:
        dst_ref[...] = _read_tokens(scratch).astype(dst_ref.dtype)
        return
    for j in range(r):
        for c in range(nt):
            lanes = slice((j * nt + c) * LANES, (j * nt + c + 1) * LANES)
            dst_ref[:, lanes] = scratch.at[c][pl.ds(j, rows // r, stride=r), :].astype(dst_ref.dtype)


def _mm_nt_classes(a, b, bias, dils, *, bm, name):
    M, K = a.shape
    N = b.shape[0]
    dn = (((1,), (1,)), ((), ()))

    def body(a_ref, b_ref, bias_ref, *rest):
        outs, tile = rest[:-1], rest[-1]
        _fill_tokens(tile, lax.dot_general(a_ref[...], b_ref[...], dn, preferred_element_type=F32) + bias_ref[...])
        for out, r in zip(outs, dils):
            _from_tokens(out, r, tile)

    limit = _vmem_limit(_nbytes((bm, K), a.dtype), _nbytes((K, N), b.dtype) // 2, (1 + len(dils)) * _nbytes((bm, N), F32))
    return pl.pallas_call(
        body, name=name, grid=(M // bm,),
        in_specs=[pl.BlockSpec((bm, K), lambda i: (i, 0)),
                  pl.BlockSpec((N, K), lambda i: (0, 0), pipeline_mode=pl.Buffered(1)),
                  pl.BlockSpec((1, N), lambda i: (0, 0))],
        out_specs=tuple(_class_rows_spec(bm, r, N) for r in dils),
        out_shape=tuple(jax.ShapeDtypeStruct((M // r, r * N), BF16) for r in dils),
        scratch_shapes=[_token_scratch(bm, N)],
        compiler_params=pltpu.CompilerParams(dimension_semantics=("parallel",), vmem_limit_bytes=limit),
    )(a, b, bias)


def _merge(branches, dils, sink, *, name):
    W = MIX_W
    T = branches[0][0].shape[0] * dils[0]
    nbr = len(branches)

    def body(*refs):
        sink_ref = refs[2 * nbr]
        out_ref, outb_ref, lse_ref, scratch = refs[2 * nbr + 1:]
        sk = sink_ref[...]
        lses = [_to_tokens(refs[2 * t + 1], dils[t], scratch) for t in range(nbr)]
        mx = sk
        for lse in lses:
            mx = jnp.maximum(mx, lse)
        den = jnp.exp(sk - mx)
        num = jnp.zeros_like(mx)
        for t in range(nbr):
            w = jnp.exp(lses[t] - mx)
            den = den + w
            num = num + w * _to_tokens(refs[2 * t], dils[t], scratch)
        out = num / den
        out_ref[...] = out
        outb_ref[...] = out.astype(outb_ref.dtype)
        lse_ref[...] = mx + jnp.log(den)

    args = [a for br in branches for a in br] + [sink]
    in_specs = [_class_rows_spec(ROW_BLK, r, W) for r in dils for _ in range(2)] + [_vec_spec(W)]
    return pl.pallas_call(
        body, name=name, grid=(T // ROW_BLK,), in_specs=in_specs,
        out_specs=(_row_spec(W), _row_spec(W), _row_spec(W)),
        out_shape=(jax.ShapeDtypeStruct((T, W), F32), jax.ShapeDtypeStruct((T, W), BF16),
                   jax.ShapeDtypeStruct((T, W), F32)),
        scratch_shapes=[_token_scratch(ROW_BLK, W)],
        compiler_params=pltpu.CompilerParams(dimension_semantics=("parallel",)),
    )(*args)


def _attn_delta(dmix, outs, lses, sink, dils, *, name):
    T = dmix.shape[0]
    W = MIX_W
    nd = len(dils)

    def body(dmix_ref, oa_ref, ob_ref, la_ref, lb_ref, sink_ref, *rest):
        doa_ref, sta_ref, dsink_ref = rest[0:3]
        dob_refs, stb_refs = rest[3:3 + nd], rest[3 + nd:3 + 2 * nd]
        do_tile, st_tile = rest[3 + 2 * nd:]

        @pl.when(pl.program_id(0) == 0)
        def _():
            dsink_ref[...] = jnp.zeros_like(dsink_ref)

        sel0, lo = _head_sel(0), _stat_lo()
        for mixer, (o_ref, l_ref) in enumerate(((oa_ref, la_ref), (ob_ref, lb_ref))):
            for i in range(N_PAIRS):
                cols = slice(i * LANES, (i + 1) * LANES)
                do = dmix_ref[:, mixer * W + i * LANES:mixer * W + (i + 1) * LANES]
                prod = do * o_ref[:, cols]
                d0 = jnp.sum(jnp.where(sel0, prod, 0.0), axis=1, keepdims=True)
                d1 = jnp.sum(jnp.where(sel0, 0.0, prod), axis=1, keepdims=True)
                delta = jnp.where(sel0, d0, d1)
                lse = l_ref[:, cols]
                stat = jnp.where(lo, lse, delta)
                if mixer == 0:
                    sta_ref[:, cols] = stat
                    doa_ref[:, cols] = do.astype(doa_ref.dtype)
                    dsink_ref[:, cols] += -jnp.sum(jnp.exp(sink_ref[:, cols] - lse) * delta, axis=0, keepdims=True)
                else:
                    st_tile[i] = stat
                    do_tile[i] = do
        for t, r in enumerate(dils):
            _from_tokens(dob_refs[t], r, do_tile)
            _from_tokens(stb_refs[t], r, st_tile)

    class_specs = [_class_rows_spec(ROW_BLK, r, W) for r in dils]
    out_shape = [jax.ShapeDtypeStruct((T, W), BF16), jax.ShapeDtypeStruct((T, W), F32), jax.ShapeDtypeStruct((1, W), F32)]
    out_shape += [jax.ShapeDtypeStruct((T // r, r * W), BF16) for r in dils]
    out_shape += [jax.ShapeDtypeStruct((T // r, r * W), F32) for r in dils]
    res = pl.pallas_call(
        body, name=name, grid=(T // ROW_BLK,),
        in_specs=[_row_spec(2 * W), _row_spec(W), _row_spec(W), _row_spec(W), _row_spec(W), _vec_spec(W)],
        out_specs=tuple([_row_spec(W), _row_spec(W), _vec_spec(W)] + class_specs + class_specs),
        out_shape=tuple(out_shape),
        scratch_shapes=[_token_scratch(ROW_BLK, W), _token_scratch(ROW_BLK, W)],
        compiler_params=pltpu.CompilerParams(dimension_semantics=("arbitrary",)),
    )(dmix, outs[0], outs[1], lses[0], lses[1], sink)
    return res[0], res[1], res[2], res[3:3 + nd], res[3 + nd:]


def _band_bwd(qa, ka, va, d_out, stat, bias, *, r, q_col, k_col, v_col, stride, pattern, name, carry=None):
    L = qa.shape[0]
    nb = L // QBLK
    dn_nt = (((1,), (1,)), ((), ()))
    dn_tn = (((0,), (0,)), ((), ()))
    scale = HEAD_DIM ** -0.5

    pps = N_PAIRS if r > 1 else N_PAIRS // 2

    def body(q_ref, k_ref, v_ref, do_ref, st_ref, bias_ref, dq_ref, dk_ref, dv_ref, db_ref, dk_carry, dv_carry):
        @pl.when((pl.program_id(0) == 0) & (pl.program_id(1) == 0))
        def _():
            db_ref[...] = jnp.zeros_like(db_ref)

        lo, sel0 = _stat_lo(), _head_sel(0)
        pair0 = pl.program_id(1) * pps

        def block(b, first):
            rows = _rows(b)
            for i in range(pps):
                heads = pl.ds(2 * (pair0 + i), 2)
                cols = slice(i * LANES, (i + 1) * LANES)
                q2 = _stack_heads(q_ref[rows, cols], scale)
                k = _key_window(k_ref, b, cols, first)
                v = _key_window(v_ref, b, cols, first)
                do2 = _stack_heads(do_ref[rows, cols])
                st = st_ref[rows, cols]
                lse2 = jnp.concatenate(
                    [jnp.max(jnp.where(_head_sel(h) & lo, st, -jnp.inf), axis=1, keepdims=True) for h in range(2)], axis=0)
                delta2 = jnp.concatenate(
                    [jnp.sum(jnp.where(_head_sel(h) & ~lo, st, 0.0), axis=1, keepdims=True) for h in range(2)],
                    axis=0) * (2.0 / HEAD_DIM)
                bias2 = bias_ref[1 if first else 0, heads].reshape(2 * QBLK, 2 * QBLK)
                s = lax.dot_general(q2, k, dn_nt, preferred_element_type=F32) + bias2
                p = jnp.exp(s - lse2)
                dp = lax.dot_general(do2, v, dn_nt, preferred_element_type=F32)
                ds = p * (dp - delta2)
                db_ref[heads] += ds.reshape(2, QBLK, 2 * QBLK)
                dsb = ds.astype(k.dtype)
                dq2 = jnp.dot(dsb, k, preferred_element_type=F32)
                dq_ref[rows, cols] = (jnp.where(sel0, dq2[0:QBLK], dq2[QBLK:]) * scale).astype(dq_ref.dtype)
                dk_acc = lax.dot_general(dsb, q2, dn_tn, preferred_element_type=F32)
                dv_acc = lax.dot_general(p.astype(k.dtype), do2, dn_tn, preferred_element_type=F32)
                if not first:
                    prev = _rows(b - 1)
                    dk_ref[prev, cols] = (dk_carry[:, cols] + dk_acc[0:QBLK]).astype(dk_ref.dtype)
                    dv_ref[prev, cols] = (dv_carry[:, cols] + dv_acc[0:QBLK]).astype(dv_ref.dtype)
                dk_carry[:, cols] = dk_acc[QBLK:2 * QBLK]
                dv_carry[:, cols] = dv_acc[QBLK:2 * QBLK]

        block(0, True)
        if nb > 1:
            pl.loop(1, nb)(lambda b: block(b, False))

        last = _rows(nb - 1)
        dk_ref[last, :] = dk_carry[...].astype(dk_ref.dtype)
        dv_ref[last, :] = dv_carry[...].astype(dv_ref.dtype)

    tok_spec = _class_spec(L, r, 0, 1, pps)
    bias_spec = pl.BlockSpec((None, 2, 2 * N_PAIRS, QBLK, 2 * QBLK), lambda j, c: (pattern, 0, 0, 0, 0))
    dbias_spec = pl.BlockSpec((2 * N_PAIRS, QBLK, 2 * QBLK), lambda j, c: (0, 0, 0))
    grad = jax.ShapeDtypeStruct((L, r * MIX_W), BF16)
    blk_bytes = _nbytes((L, pps * LANES), BF16)
    carry_shape = pltpu.VMEM((QBLK, pps * LANES), F32)
    return _call(
        body, name=name, grid=(r, N_PAIRS // pps),
        in_specs=[_class_spec(L, r, q_col, stride, pps), _class_spec(L, r, k_col, stride, pps),
                  _class_spec(L, r, v_col, stride, pps), tok_spec, tok_spec, bias_spec],
        out_specs=[tok_spec, tok_spec, tok_spec, dbias_spec],
        out_shape=[grad, grad, grad, jax.ShapeDtypeStruct((2 * N_PAIRS, QBLK, 2 * QBLK), F32)],
        args=[qa, ka, va, d_out, stat, bias], scratch=[carry_shape, carry_shape],
        vmem_limit=_vmem_limit(*([blk_bytes] * 9)), semantics=("arbitrary", "arbitrary"), carry=carry)


def _dz_assemble(dq_a, dk_a, dv_a, b_grads, dils, *, name):
    T = dq_a.shape[0]
    W = MIX_W

    def group_sum(x):
        u0 = x[:, 0:LANES] + x[:, LANES:2 * LANES]
        u1 = x[:, 2 * LANES:3 * LANES] + x[:, 3 * LANES:4 * LANES]
        s0 = u0 + pltpu.roll(u0, HEAD_DIM, 1)
        s1 = u1 + pltpu.roll(u1, HEAD_DIM, 1)
        return jnp.where(_head_sel(0), s0, s1)

    def body(*refs):
        dqa_ref, dka_ref, dva_ref = refs[0:3]
        b_refs = refs[3:12]
        dza_ref, dzb_ref, sa_ref, sb_ref, scratch = refs[12:]

        @pl.when(pl.program_id(0) == 0)
        def _():
            sa_ref[...] = jnp.zeros_like(sa_ref)
            sb_ref[...] = jnp.zeros_like(sb_ref)

        def put(dst, acc, col, part):
            w = part.shape[1]
            dst[:, col:col + w] = part.astype(dst.dtype)
            acc[:, col:col + w] += jnp.sum(part, axis=0, keepdims=True)

        put(dza_ref, sa_ref, 0, dqa_ref[...].astype(F32))
        put(dza_ref, sa_ref, W, group_sum(dka_ref[...].astype(F32)))
        put(dza_ref, sa_ref, W + LANES, group_sum(dva_ref[...].astype(F32)))
        for t in range(3):
            part = _to_tokens(b_refs[t], dils[0], scratch)
            for pat in range(1, len(dils)):
                part = part + _to_tokens(b_refs[3 * pat + t], dils[pat], scratch)
            put(dzb_ref, sb_ref, t * W, part)

    args = [dq_a, dk_a, dv_a] + [g[t] for g in b_grads for t in range(3)]
    return pl.pallas_call(
        body, name=name, grid=(T // ROW_BLK,),
        in_specs=[_row_spec(W)] * 3 + [_class_rows_spec(ROW_BLK, r, W) for r in dils for _ in range(3)],
        out_specs=(_row_spec(ZA_W), _row_spec(ZB_W), _vec_spec(ZA_W), _vec_spec(ZB_W)),
        out_shape=(jax.ShapeDtypeStruct((T, ZA_W), BF16), jax.ShapeDtypeStruct((T, ZB_W), BF16),
                   jax.ShapeDtypeStruct((1, ZA_W), F32), jax.ShapeDtypeStruct((1, ZB_W), F32)),
        scratch_shapes=[_token_scratch(ROW_BLK, W)],
        compiler_params=pltpu.CompilerParams(dimension_semantics=("arbitrary",)),
    )(*args)


def _place():
    x, y, c = lax.axis_index("x"), lax.axis_index("y"), lax.axis_index("c")
    chips = [(1 - x, y), (x, 1 - y), (1 - x, 1 - y)]
    return x, y, c, chips


def _cast_place(shard, k_idx, *, name):
    rs, cs = shard.shape
    rb = _row_block(rs, 256, 16)
    nblk = rs // rb

    def body(k_ref, s_ref, o_ref):
        o_ref[...] = s_ref[...].astype(o_ref.dtype)

    grid_spec = pltpu.PrefetchScalarGridSpec(
        num_scalar_prefetch=1, grid=(nblk,),
        in_specs=[pl.BlockSpec((rb, cs), lambda t, k_ref: (t, 0))],
        out_specs=pl.BlockSpec((rb, cs), lambda t, k_ref: (k_ref[0] * nblk + t, 0)))
    return pl.pallas_call(
        body, name=name, grid_spec=grid_spec, out_shape=jax.ShapeDtypeStruct((N_CHIPS * rs, cs), BF16),
        compiler_params=pltpu.CompilerParams(dimension_semantics=("parallel",)),
    )(k_idx, shard)


def _gather_carry(fulls, part=(0, 1)):
    n = len(fulls)
    p_idx, n_parts = part

    def piece(ref, chip_idx, half):
        rs = ref.shape[0] // N_CHIPS
        rh = rs // 2
        rows = rh // n_parts
        return ref.at[pl.ds(chip_idx * rs + half * rh + p_idx * rows, rows)]

    def copy(sems, sem, src_ref, dst_ref, to):
        return pltpu.make_async_remote_copy(src_ref=src_ref, dst_ref=dst_ref, send_sem=sems[0].at[sem],
                                            recv_sem=sems[1].at[sem], device_id=to, device_id_type=MESH)

    def ici_copy(ins, outs, sems, i, j, chip, k, c):
        return copy(sems, 3 * i + j, piece(ins[i], k, c), piece(outs[i], k, c), (*chip, c))

    def start(ins, outs, sems):
        x, y, c, chips = _place()
        for i in range(n):
            for j, chip in enumerate(chips):
                ici_copy(ins, outs, sems, i, j, chip, 2 * x + y, c).start()

    def finish(ins, outs, sems):
        x, y, c, chips = _place()
        sibling = (x, y, 1 - c)
        passed = []
        for i in range(n):
            for j, chip in enumerate(chips):
                region = piece(outs[i], 2 * chip[0] + chip[1], c)
                copy(sems, 3 * i + j, region, region, (*chip, c)).wait_recv()
                cp = copy(sems, 3 * n + 3 * i + j, region, region, sibling)
                cp.start()
                passed.append(cp)
        for i in range(n):
            for j, chip in enumerate(chips):
                region = piece(outs[i], 2 * chip[0] + chip[1], 1 - c)
                copy(sems, 3 * n + 3 * i + j, region, region, sibling).wait_recv()
        for i in range(n):
            for j, chip in enumerate(chips):
                ici_copy(ins, outs, sems, i, j, chip, 2 * x + y, c).wait_send()
        for cp in passed:
            cp.wait_send()

    return _Carry(fulls, [jax.ShapeDtypeStruct(a.shape, a.dtype) for a in fulls], {i: i for i in range(n)},
                  [pltpu.SemaphoreType.DMA((6 * n,)), pltpu.SemaphoreType.DMA((6 * n,))], start, finish)


def _pair_swap_carry(grads):
    n = len(grads)

    def copy(ins, outs, sems, i):
        x, y, c, _ = _place()
        return pltpu.make_async_remote_copy(src_ref=ins[i].at[:, 1 - c], dst_ref=outs[i], send_sem=sems[0].at[i],
                                            recv_sem=sems[1].at[i], device_id=(x, y, 1 - c), device_id_type=MESH)

    def start(ins, outs, sems):
        for i in range(n):
            copy(ins, outs, sems, i).start()

    def finish(ins, outs, sems):
        for i in range(n):
            copy(ins, outs, sems, i).wait()

    return _Carry(grads, [jax.ShapeDtypeStruct((a.shape[0], a.shape[2], a.shape[3]), a.dtype) for a in grads], {},
                  [pltpu.SemaphoreType.DMA((n,)), pltpu.SemaphoreType.DMA((n,))], start, finish)


def _pair_sum(g4, got, c_idx, *, name):
    _, _, rh, cs = g4.shape
    rb = _row_block(rh, 256, 16)

    def body(c_ref, own_ref, got_ref, o_ref):
        o_ref[...] = (own_ref[...].astype(F32) + got_ref[...].astype(F32)).astype(o_ref.dtype)

    grid_spec = pltpu.PrefetchScalarGridSpec(
        num_scalar_prefetch=1, grid=(N_CHIPS, rh // rb),
        in_specs=[pl.BlockSpec((None, None, rb, cs), lambda k, t, c_ref: (k, c_ref[0], t, 0)),
                  pl.BlockSpec((None, rb, cs), lambda k, t, c_ref: (k, t, 0))],
        out_specs=pl.BlockSpec((None, rb, cs), lambda k, t, c_ref: (k, t, 0)))
    return pl.pallas_call(
        body, name=name, grid_spec=grid_spec, out_shape=jax.ShapeDtypeStruct((N_CHIPS, rh, cs), BF16),
        compiler_params=pltpu.CompilerParams(dimension_semantics=("parallel", "parallel")),
    )(c_idx, g4, got)


def _chip_scatter_carry(sums):
    n = len(sums)

    def copies(ins, outs, sems):
        x, y, c, chips = _place()
        return [pltpu.make_async_remote_copy(src_ref=ins[i].at[2 * chip[0] + chip[1]], dst_ref=outs[i].at[j],
                                             send_sem=sems[0].at[3 * i + j], recv_sem=sems[1].at[3 * i + j],
                                             device_id=(*chip, c), device_id_type=MESH)
                for i in range(n) for j, chip in enumerate(chips)]

    def start(ins, outs, sems):
        for cp in copies(ins, outs, sems):
            cp.start()

    def finish(ins, outs, sems):
        for cp in copies(ins, outs, sems):
            cp.wait()

    return _Carry(sums, [jax.ShapeDtypeStruct((3,) + a.shape[1:], a.dtype) for a in sums], {},
                  [pltpu.SemaphoreType.DMA((3 * n,)), pltpu.SemaphoreType.DMA((3 * n,))], start, finish)


def _chip_sum(sums, got, kc_idx, *, name):
    _, rh, cs = sums.shape
    rb = _row_block(rh, 256, 16)
    nblk = rh // rb

    def body(kc_ref, own_ref, got_ref, o_ref):
        acc = own_ref[...].astype(F32)
        for j in range(3):
            acc = acc + got_ref[j].astype(F32)
        o_ref[...] = acc

    grid_spec = pltpu.PrefetchScalarGridSpec(
        num_scalar_prefetch=1, grid=(nblk,),
        in_specs=[pl.BlockSpec((None, rb, cs), lambda t, kc: (kc[0], t, 0)),
                  pl.BlockSpec((3, rb, cs), lambda t, kc: (0, t, 0))],
        out_specs=pl.BlockSpec((rb, cs), lambda t, kc: (kc[1] * nblk + t, 0)))
    return pl.pallas_call(
        body, name=name, grid_spec=grid_spec, out_shape=jax.ShapeDtypeStruct((2 * rh, cs), F32),
        compiler_params=pltpu.CompilerParams(dimension_semantics=("parallel",)),
    )(kc_idx, sums, got)


def _half_swap_carry(shards):
    n = len(shards)

    def copy(ins, outs, sems, i, to_my_half):
        x, y, c, _ = _place()
        rh = ins[i].shape[0] // 2
        half = c if to_my_half else 1 - c
        return pltpu.make_async_remote_copy(
            src_ref=ins[i].at[pl.ds(c * rh, rh)], dst_ref=outs[i].at[pl.ds(half * rh, rh)],
            send_sem=sems[0].at[i], recv_sem=sems[1].at[i], device_id=(x, y, 1 - c), device_id_type=MESH)

    def start(ins, outs, sems):
        for i in range(n):
            copy(ins, outs, sems, i, True).start()

    def finish(ins, outs, sems):
        for i in range(n):
            copy(ins, outs, sems, i, False).wait_recv()
        for i in range(n):
            copy(ins, outs, sems, i, True).wait_send()

    return _Carry(shards, [jax.ShapeDtypeStruct(a.shape, a.dtype) for a in shards], {i: i for i in range(n)},
                  [pltpu.SemaphoreType.DMA((n,)), pltpu.SemaphoreType.DMA((n,))], start, finish)


SMALL_ROW = 1024


def _small_layout(shapes):
    starts, row = [], 0
    for r, c in shapes:
        starts.append(row)
        row += -(-c // SMALL_ROW) if r == 1 else r
    return starts, -(-row // SUBLANES) * SUBLANES


def _small_pieces(shape, start):
    r, c = shape
    if r == 1:
        return [(start + q, min(SMALL_ROW, c - q * SMALL_ROW), q * SMALL_ROW) for q in range(-(-c // SMALL_ROW))]
    return None


def _small_all_reduce(parts, *, name):
    shapes = [a.shape for a in parts]
    starts, rows = _small_layout(shapes)
    n = len(parts)

    def body(*refs):
        ins, outs = refs[0:n], refs[n:2 * n]
        buf, send_sems, recv_sems = refs[2 * n:]
        x, y, c, _ = _place()
        me = 4 * x + 2 * y + c
        mine = buf.at[me]
        mine[...] = jnp.zeros((rows, SMALL_ROW), F32)
        for ref, shape, start in zip(ins, shapes, starts):
            pieces = _small_pieces(shape, start)
            if pieces is None:
                mine[start:start + shape[0], 0:shape[1]] = ref[...]
            else:
                for row, width, col in pieces:
                    mine[row:row + 1, 0:width] = ref[:, col:col + width]
        copies = []
        for d in range(1, 8):
            to = (1 - x if d & 4 else x, 1 - y if d & 2 else y, 1 - c if d & 1 else c)
            cp = pltpu.make_async_remote_copy(src_ref=mine, dst_ref=mine, send_sem=send_sems.at[d - 1],
                                              recv_sem=recv_sems.at[d - 1], device_id=to, device_id_type=MESH)
            cp.start()
            copies.append((cp, to))
        for d in range(1, 8):
            to = copies[d - 1][1]
            sender = 4 * to[0] + 2 * to[1] + to[2]
            pltpu.make_async_remote_copy(src_ref=mine, dst_ref=buf.at[sender], send_sem=send_sems.at[d - 1],
                                         recv_sem=recv_sems.at[d - 1], device_id=to, device_id_type=MESH).wait_recv()
        for cp, _ in copies:
            cp.wait_send()
        acc = buf[0]
        for s in range(1, 8):
            acc = acc + buf[s]
        mine[...] = acc
        for ref, shape, start in zip(outs, shapes, starts):
            pieces = _small_pieces(shape, start)
            if pieces is None:
                ref[...] = mine[start:start + shape[0], 0:shape[1]]
            else:
                for row, width, col in pieces:
                    ref[:, col:col + width] = mine[row:row + 1, 0:width]

    vm = pl.BlockSpec(memory_space=pltpu.VMEM)
    return pl.pallas_call(
        body, name=name, in_specs=[vm] * n, out_specs=tuple([vm] * n),
        out_shape=tuple(jax.ShapeDtypeStruct(s, F32) for s in shapes),
        scratch_shapes=[pltpu.VMEM((8, rows, SMALL_ROW), F32), pltpu.SemaphoreType.DMA((7,)),
                        pltpu.SemaphoreType.DMA((7,))],
    )(*parts)


def _adamw_small(ws, gs, ms, vs, *, name):
    n = len(ws)
    c1 = 1.0 - ADAM_B1 ** ADAM_STEP
    c2 = 1.0 - ADAM_B2 ** ADAM_STEP

    def body(*refs):
        for k in range(n):
            w_ref, g_ref, m_ref, v_ref = (refs[t * n + k] for t in range(4))
            go_ref, d_ref, mo_ref, vo_ref = (refs[(4 + t) * n + k] for t in range(4))
            gv = g_ref[...]
            go_ref[...] = gv
            mn = ADAM_B1 * m_ref[...] + (1.0 - ADAM_B1) * gv
            vn = ADAM_B2 * v_ref[...] + (1.0 - ADAM_B2) * (gv * gv)
            d_ref[...] = -ADAM_LR * ((mn / c1) / (jnp.sqrt(vn / c2) + ADAM_EPS) + ADAM_WD * w_ref[...])
            mo_ref[...] = mn
            vo_ref[...] = vn

    vm = pl.BlockSpec(memory_space=pltpu.VMEM)
    shapes = tuple(jax.ShapeDtypeStruct(a.shape, F32) for a in ws)
    res = pl.pallas_call(
        body, name=name, in_specs=[vm] * (4 * n), out_specs=tuple([vm] * (4 * n)), out_shape=shapes * 4,
    )(*ws, *gs, *ms, *vs)
    return res[0:n], res[n:2 * n], res[2 * n:3 * n], res[3 * n:4 * n]


def _adamw(w, g, m, v, *, name):
    R, C = w.shape
    rb = _row_block(R, 256, SUBLANES)
    c1 = 1.0 - ADAM_B1 ** ADAM_STEP
    c2 = 1.0 - ADAM_B2 ** ADAM_STEP

    def body(w_ref, g_ref, m_ref, v_ref, go_ref, d_ref, mo_ref, vo_ref):
        gv = g_ref[...]
        go_ref[...] = gv
        mn = ADAM_B1 * m_ref[...] + (1.0 - ADAM_B1) * gv
        vn = ADAM_B2 * v_ref[...] + (1.0 - ADAM_B2) * (gv * gv)
        d_ref[...] = -ADAM_LR * ((mn / c1) / (jnp.sqrt(vn / c2) + ADAM_EPS) + ADAM_WD * w_ref[...])
        mo_ref[...] = mn
        vo_ref[...] = vn

    spec = pl.BlockSpec((rb, C), lambda i: (i, 0))
    shp = jax.ShapeDtypeStruct((R, C), F32)
    return pl.pallas_call(
        body, name=name, grid=(R // rb,), in_specs=[spec] * 4, out_specs=(spec,) * 4,
        out_shape=(shp,) * 4, compiler_params=pltpu.CompilerParams(dimension_semantics=("parallel",)),
    )(w, g, m, v)


def _local_step(x, p, target, small, sched):
    T = x.shape[0]
    W = MIX_W
    rel_bias = small["rel_bias"]
    b_in_a, b_in_b = small["b_in"][:, 0:ZA_W], small["b_in"][:, ZA_W:]

    idx_np = [_t5_index(A_WINDOW - 1, 1)] + [_t5_index(window // dil, dil) for window, dil in B_PATTERNS]
    idx_all = jnp.asarray(np.stack(idx_np))
    n1, bias_all = sched.run(_prologue, x, small["ffn1_pre_g"], rel_bias, idx_all, name="prologue")
    w_gu1 = sched.weight("ffn1_w_gu")
    *gu1, a1 = sched.run(_ffn_up, n1, w_gu1, bm=512, name="ffn1_gu")
    w_d1 = sched.weight("ffn1_w_down")
    f1, h1, n2 = sched.run(_mm_resid_norm, a1, w_d1, x, small["ffn1_post_g"], 0.5, small["attn_pre_g"], bm=512,
                           name="ffn1_down")
    win_a, win_b = sched.weight("w_in")
    za = _mm_nt(n2, win_a, bm=1024, bn=ZA_W, out_dtype=BF16, name="in_proj_a", add=b_in_a)
    dils = tuple(dil for _, dil in B_PATTERNS)
    zb_by_dil = _mm_nt_classes(n2, win_b, b_in_b, dils, bm=512, name="in_proj_b")

    k_a = za[:, W:W + LANES].reshape(T, 2, 1, HEAD_DIM)
    v_a = za[:, W + LANES:W + 2 * LANES].reshape(T, 2, 1, HEAD_DIM)
    k_a = jnp.broadcast_to(k_a, (T, 2, 4, HEAD_DIM)).reshape(T, W)
    v_a = jnp.broadcast_to(v_a, (T, 2, 4, HEAD_DIM)).reshape(T, W)
    a_args = dict(r=1, q_col=0, k_col=0, v_col=0, stride=0, pattern=0)
    o_a, lse_a1 = sched.run(_band_fwd, za, k_a, v_a, bias_all, name="band_a_fwd", **a_args)
    sink_row = jnp.repeat(small["sinks"], HEAD_DIM, axis=1)
    out_a, out_a_bf, lse_a = _merge([(o_a, lse_a1)], (1,), sink_row, name="merge_a")

    no_sink = jnp.full((1, W), NEG, F32)
    b_ctx, branches = [], []
    for t, dil in enumerate(dils):
        zb_r = zb_by_dil[t]
        args = dict(r=dil, q_col=0, k_col=1, v_col=2, stride=ZB_W // W, pattern=1 + t)
        branches.append(sched.run(_band_fwd, zb_r, zb_r, zb_r, bias_all, name=f"band_b{t}_fwd", **args))
        b_ctx.append((jnp.asarray(idx_np[1 + t]), zb_r, args))
    out_b, out_b_bf, lse_b = _merge(branches, dils, no_sink, name="merge_b")

    mix = jnp.concatenate([out_a_bf, out_b_bf], axis=1)
    w_out = sched.weight("w_out")
    att, h2, n3 = _mm_resid_norm(mix, w_out, h1, small["attn_post_g"], 1.0, small["ffn2_pre_g"], bm=512,
                                 name="out_proj", bias=small["b_out"])
    w_gu2, w_d2 = sched.weight("ffn2_w_gu"), sched.weight("ffn2_w_down")
    *gu2, a2 = _ffn_up(n3, w_gu2, bm=512, name="ffn2_gu")
    f2, h3, n4 = _mm_resid_norm(a2, w_d2, h2, small["ffn2_post_g"], 0.5, small["ple_pre_g"], bm=512,
                                name="ffn2_down")
    w_gate = sched.weight("w_ple_gate")
    p_bf = p.astype(BF16)
    dh4, de, dgp, loss, d_ple_post = _ple_head(n4, w_gate, p_bf, sched.weight("w_ple_proj"), h3, small["ple_post_g"],
                                               target, bm=512, name="ple_head")

    gs = {"ple_post_g": d_ple_post}
    sched.grad("w_ple_proj", _mm_tn(p_bf, de, bm=256, bn=1024, out_dtype=BF16, name="d_w_ple"))
    sched.grad("w_ple_gate", _mm_tn(n4, dgp, bm=512, bn=1024, out_dtype=BF16, name="d_w_gate"))
    dh3, df2, gs["ple_pre_g"], gs["ffn2_post_g"], _ = sched.run(
        _mm_nt_norms_bwd, dgp, w_gate, dh4, h3, small["ple_pre_g"], f2, small["ffn2_post_g"], 0.5, bm=512, name="d_n4")

    def ffn_bwd(df, a, gu, n, w_down, w_gu, tag, *norm_args):
        dgu = sched.run(_ffn_down_bwd, df, w_down, gu[0], gu[1], bm=256, name=f"ffn{tag}_d_a")
        sched.grad(f"ffn{tag}_w_down", _mm_tn(a, df, bm=256, bn=1024, out_dtype=BF16, name=f"ffn{tag}_d_w_down"))
        sched.grad(f"ffn{tag}_w_gu", sched.run(_mm_tn, n, dgu, bm=512, bn=1408, out_dtype=BF16,
                                              name=f"ffn{tag}_d_w_gu", n_stack=N_CHIPS))
        return sched.run(_mm_nt_norms_bwd, dgu, w_gu, *norm_args, bm=512, name=f"ffn{tag}_d_n")

    dh2, datt, gs["ffn2_pre_g"], gs["attn_post_g"], gs["b_out"] = ffn_bwd(
        df2, a2, gu2, n3, w_d2, w_gu2, "2", dh3, h2, small["ffn2_pre_g"], att, small["attn_post_g"], 1.0)
    sched.grad("w_out", _mm_tn(mix, datt, bm=512, bn=1024, out_dtype=BF16, name="d_w_out"))
    dmix = sched.run(_mm_nt, datt, w_out, bm=1024, bn=1024, out_dtype=F32, name="d_mix")

    do_a, stat_a, dsink, do_b, stat_b = _attn_delta(dmix, (out_a, out_b), (lse_a, lse_b), sink_row, dils,
                                                    name="attn_delta")
    gs["sinks"] = dsink[:, ::HEAD_DIM]
    dq_a, dk_a, dv_a, dbias_a = sched.run(_band_bwd, za, k_a, v_a, do_a, stat_a, bias_all, name="band_a_bwd", **a_args)
    d_rel_a = _bias_grad(dbias_a, jnp.asarray(idx_np[0]), name="d_rel_a")
    b_grads = []
    d_rel_b = None
    for t, (idx, zb_r, args) in enumerate(b_ctx):
        dq, dk, dv, dbias = sched.run(_band_bwd, zb_r, zb_r, zb_r, do_b[t], stat_b[t], bias_all,
                                      name=f"band_b{t}_bwd", **args)
        b_grads.append((dq, dk, dv))
        d_rel = _bias_grad(dbias, idx, name=f"d_rel_b{t}")
        d_rel_b = d_rel if d_rel_b is None else d_rel_b + d_rel
    gs["rel_bias"] = jnp.concatenate([d_rel_a[:, 0:2 * N_PAIRS], d_rel_b[:, 0:2 * N_PAIRS]], axis=1)
    dza, dzb, dsum_a, dsum_b = _dz_assemble(dq_a, dk_a, dv_a, b_grads, dils, name="d_z")
    gs["b_in"] = jnp.concatenate([dsum_a, dsum_b], axis=1)
    sched.grad("w_in", (_mm_tn(dza, n2, bm=ZA_W, bn=1024, out_dtype=BF16, name="d_w_in_a"),
                        _mm_tn(dzb, n2, bm=ZB_W // 2, bn=1024, out_dtype=BF16, name="d_w_in_b")))
    dn2_a = _mm_nn(dza, win_a, bm=1024, bn=1024, out_dtype=F32, name="d_n2_a")
    dh1, df1, gs["attn_pre_g"], gs["ffn1_post_g"], _ = sched.run(
        _mm_nt_norms_bwd, dzb, win_b, dh2, h1, small["attn_pre_g"], f1, small["ffn1_post_g"], 0.5, bm=512,
        name="d_n2_b", add=dn2_a, b_is_kn=True)
    grad_x, gs["ffn1_pre_g"] = ffn_bwd(df1, a1, gu1, n1, w_d1, w_gu1, "1", dh1, x, small["ffn1_pre_g"], None, None, 0.0)
    return loss, grad_x, gs


def _to_compute(name, full):
    st = full.reshape(N_CHIPS, full.shape[0] // N_CHIPS, full.shape[1])
    if name in ("ffn1_w_gu", "ffn2_w_gu"):
        return st
    if name == "w_ple_proj":
        return jnp.transpose(st, (1, 0, 2)).reshape(st.shape[1], -1)
    if name == "w_in":
        return full[0:ZA_W], full[ZA_W:]
    return full


def _to_comm(name, g):
    if name == "w_in":
        g = jnp.concatenate(g, axis=0)
    if name == "w_ple_proj":
        g = jnp.transpose(g.reshape(g.shape[0], N_CHIPS, -1), (1, 0, 2))
    rs = g.shape[1] if g.ndim == 3 else g.shape[0] // N_CHIPS
    return g.reshape(N_CHIPS, 2, rs // 2, g.shape[-1])


class _Schedule:
    GATHER = {
        "prologue": [("ffn1_w_gu", (0, 1))],
        "ffn1_gu": [("ffn1_w_down", (0, 1)), ("w_in", (0, 1))],
        "ffn1_down": [("w_out", (0, 1))],
        "band_a_fwd": [("ffn2_w_gu", (0, 2))],
        "band_b0_fwd": [("ffn2_w_gu", (1, 2))],
        "band_b1_fwd": [("ffn2_w_down", (0, 1))],
        "band_b2_fwd": [("w_ple_gate", (0, 1)), ("w_ple_proj", (0, 1))],
    }
    SWAP = {"d_n4": ["w_ple_proj", "w_ple_gate"], "ffn2_d_w_gu": ["ffn2_w_down"], "ffn2_d_n": ["ffn2_w_gu"],
            "d_mix": ["w_out"], "d_n2_b": ["w_in"], "ffn1_d_w_gu": ["ffn1_w_down"], "ffn1_d_n": ["ffn1_w_gu"]}
    SCATTER = {"ffn2_d_a": ["w_ple_proj", "w_ple_gate"], "ffn2_d_n": ["ffn2_w_down"], "band_a_bwd": ["ffn2_w_gu"],
               "band_b0_bwd": ["w_out"], "ffn1_d_a": ["w_in"], "ffn1_d_n": ["ffn1_w_down"]}

    def __init__(self, placed, c_idx, kc_idx):
        self.full = dict(placed)
        self.missing = {n: 1.0 for n in placed}
        self.c_idx, self.kc_idx = c_idx, kc_idx
        self.grads, self.swapped, self.pair_sums, self.scattered = {}, {}, {}, {}

    def _gather(self, entries):
        carries, after = [], []
        for part in sorted({pt for _, pt in entries}):
            names = [n for n, pt in entries if pt == part]
            carry = _gather_carry([self.full[n] for n in names], part)
            carries.append(carry)

            def done(carry=carry, names=names, part=part):
                for n, a in zip(names, carry.results):
                    self.full[n] = a
                    self.missing[n] -= 1.0 / part[1]
            after.append(done)
        return carries, after

    def weight(self, name):
        assert abs(self.missing[name]) < 1e-9, (name, self.missing[name])
        return _to_compute(name, self.full[name])

    def grad(self, name, g):
        self.grads[name] = _to_comm(name, g)

    def _swap(self, names):
        carry = _pair_swap_carry([self.grads[n] for n in names])

        def done():
            self.swapped.update(zip(names, carry.results))
        return carry, done

    def _scatter(self, names):
        for n in names:
            self.pair_sums[n] = _pair_sum(self.grads[n], self.swapped[n], self.c_idx, name=f"grad_pair_sum_{n}")
        carry = _chip_scatter_carry([self.pair_sums[n] for n in names])

        def done():
            self.scattered.update(zip(names, carry.results))
        return carry, done

    def run(self, fn, *args, name, **kw):
        carries, after = self._gather(self.GATHER.get(name, []))
        for names, make in ((self.SWAP.get(name), self._swap), (self.SCATTER.get(name), self._scatter)):
            if names:
                carry, done = make(names)
                carries.append(carry)
                after.append(done)
        out = fn(*args, name=name, carry=_join(carries), **kw)
        for done in after:
            done()
        return out

    def finish(self):
        left = [n for n in BIG if n not in self.scattered]
        to_swap = [n for n in left if n not in self.swapped]
        if to_swap:
            carry, done = self._swap(to_swap)
            _comm_call(carry, name="grad_pair_swap")
            done()
        if left:
            carry, done = self._scatter(left)
            _comm_call(carry, name="grad_chip_scatter")
            done()
        halves = [_chip_sum(self.pair_sums[n], self.scattered[n], self.kc_idx, name=f"grad_chip_sum_{n}") for n in BIG]
        carry = _half_swap_carry(halves)
        _comm_call(carry, name="grad_half_swap")
        return dict(zip(BIG, carry.results))


def kernel(x, p, rel_bias, ffn1_pre_g, ffn1_w_gu, ffn1_w_down, ffn1_post_g, attn_pre_g, w_in, b_in, sinks, w_out, b_out, attn_post_g, ffn2_pre_g, ffn2_w_gu, ffn2_w_down, ffn2_post_g, ple_pre_g, w_ple_gate, w_ple_proj, ple_post_g, loss_target, m_rel_bias, m_ffn1_pre_g, m_ffn1_w_gu, m_ffn1_w_down, m_ffn1_post_g, m_attn_pre_g, m_w_in, m_b_in, m_sinks, m_w_out, m_b_out, m_attn_post_g, m_ffn2_pre_g, m_ffn2_w_gu, m_ffn2_w_down, m_ffn2_post_g, m_ple_pre_g, m_w_ple_gate, m_w_ple_proj, m_ple_post_g, v_rel_bias, v_ffn1_pre_g, v_ffn1_w_gu, v_ffn1_w_down, v_ffn1_post_g, v_attn_pre_g, v_w_in, v_b_in, v_sinks, v_w_out, v_b_out, v_attn_post_g, v_ffn2_pre_g, v_ffn2_w_gu, v_ffn2_w_down, v_ffn2_post_g, v_ple_pre_g, v_w_ple_gate, v_w_ple_proj, v_ple_post_g):
    given = dict(locals())
    w = {n: given[n] for n in WEIGHTS}
    m = {n: given["m_" + n] for n in WEIGHTS}
    v = {n: given["v_" + n] for n in WEIGHTS}
    c_pos = lax.axis_index("c").astype(jnp.int32)
    k_pos = (2 * lax.axis_index("x") + lax.axis_index("y")).astype(jnp.int32)
    c_idx, k_idx, kc_idx = c_pos.reshape(1), k_pos.reshape(1), jnp.stack([k_pos, c_pos])

    def shard(a, n):
        return jnp.transpose(a[0]) if n == "w_in" else a[0]

    def unshard(a, n):
        return (jnp.transpose(a) if n == "w_in" else a)[None]

    sched = _Schedule({n: _cast_place(shard(w[n], n), k_idx, name=f"place_{n}") for n in BIG}, c_idx, kc_idx)
    small = {n: (w[n] if n == "rel_bias" else w[n].reshape(1, -1)) for n in SMALL}

    loss_part, grad_x, gs = _local_step(x[0], p[0, 0], loss_target[0], small, sched)

    grads_big = sched.finish()

    *small_grads, loss_row = _small_all_reduce([gs[n] for n in SMALL] + [loss_part], name="small_all_reduce")
    loss = loss_row[0, 0]

    grads, delta, new_m, new_v = {}, {}, {}, {}
    for n in BIG:
        res = _adamw(shard(w[n], n), grads_big[n], shard(m[n], n), shard(v[n], n), name=f"adamw_{n}")
        grads[n], delta[n], new_m[n], new_v[n] = (unshard(a, n) for a in res)
    res = _adamw_small([w[n] for n in SMALL], small_grads, [m[n] for n in SMALL], [v[n] for n in SMALL],
                       name="adamw_small")
    for name, gg, dd, mm, vv in zip(SMALL, *res):
        grads[name], delta[name], new_m[name], new_v[name] = gg, dd, mm, vv

    return (loss, grad_x[None], *[grads[n] for n in WEIGHTS], *[delta[n] for n in WEIGHTS],
            *[new_m[n] for n in WEIGHTS], *[new_v[n] for n in WEIGHTS])
```

```python
import math

import jax
import jax.numpy as jnp
import numpy as np
from jax import lax
from jax.experimental import pallas as pl
from jax.experimental.pallas import tpu as pltpu

F32 = jnp.float32
BF16 = jnp.bfloat16
MESH = pl.DeviceIdType.MESH

EPS = 1e-6
NEG = -1e30
LANES = 128
SUBLANES = 8
HEAD_DIM = 64
QBLK = 128
N_PAIRS = 4
MIX_W = N_PAIRS * LANES
A_WINDOW = 128
B_PATTERNS = ((128, 1), (512, 4), (2048, 16))
NUM_BUCKETS = 32
MAX_DISTANCE = 2048
ZA_W = 768
ZB_W = 1536
N_CHIPS = 4
VMEM_BYTES_V7X = 64 * 2**20

ADAM_LR, ADAM_B1, ADAM_B2, ADAM_EPS, ADAM_WD, ADAM_STEP = 0.001, 0.9, 0.999, 1e-08, 0.01, 10

BIG = ("ffn1_w_gu", "ffn1_w_down", "w_in", "w_out", "ffn2_w_gu", "ffn2_w_down", "w_ple_gate", "w_ple_proj")
SMALL = ("rel_bias", "ffn1_pre_g", "ffn1_post_g", "attn_pre_g", "b_in", "sinks", "b_out", "attn_post_g",
         "ffn2_pre_g", "ffn2_post_g", "ple_pre_g", "ple_post_g")
WEIGHTS = ("rel_bias", "ffn1_pre_g", "ffn1_w_gu", "ffn1_w_down", "ffn1_post_g", "attn_pre_g", "w_in", "b_in",
           "sinks", "w_out", "b_out", "attn_post_g", "ffn2_pre_g", "ffn2_w_gu", "ffn2_w_down", "ffn2_post_g",
           "ple_pre_g", "w_ple_gate", "w_ple_proj", "ple_post_g")


def _vmem_limit(*block_bytes):
    need = 2 * sum(block_bytes) + max(block_bytes) * 2 + (4 << 20)
    return int(min(max(need, 32 << 20), VMEM_BYTES_V7X - (6 << 20)))


def _nbytes(shape, dtype):
    return int(np.prod(shape)) * jnp.dtype(dtype).itemsize


MXU_WIDTH_V7X = 256


def _col_chunks(n):
    return [slice(c, min(c + MXU_WIDTH_V7X, n)) for c in range(0, n, MXU_WIDTH_V7X)]


def _row_halves(rows):
    return [slice(0, rows // 2), slice(rows // 2, rows)]


def _row_block(rows, cap, mult):
    for cand in range(min(rows, cap), 0, -1):
        if rows % cand == 0 and cand % mult == 0:
            return cand
    raise ValueError((rows, cap, mult))


class _Carry:
    def __init__(self, operands, out_shapes, aliases, sems, start, finish):
        self.operands, self.out_shapes, self.aliases, self.sems = list(operands), list(out_shapes), dict(aliases), list(sems)
        self.start, self.finish = start, finish
        self.results = None


def _join(carries):
    carries = [c for c in carries if c is not None]
    if not carries:
        return None
    if len(carries) == 1:
        return carries[0]
    offs, pos = [], [0, 0, 0]
    for c in carries:
        offs.append(tuple(pos))
        pos = [pos[0] + len(c.operands), pos[1] + len(c.out_shapes), pos[2] + len(c.sems)]
    aliases = {}
    for c, (oi, oo, _) in zip(carries, offs):
        aliases.update({oi + i: oo + o for i, o in c.aliases.items()})

    def run(which):
        def fn(ins, outs, sems):
            for c, (oi, oo, os_) in zip(carries, offs):
                getattr(c, which)(ins[oi:oi + len(c.operands)], outs[oo:oo + len(c.out_shapes)],
                                  sems[os_:os_ + len(c.sems)])
        return fn

    joined = _Carry([a for c in carries for a in c.operands], [s for c in carries for s in c.out_shapes], aliases,
                    [s for c in carries for s in c.sems], run("start"), run("finish"))
    joined.parts = (carries, offs)
    return joined


def _set_results(carry, outs):
    carry.results = list(outs)
    if hasattr(carry, "parts"):
        for c, (_, oo, _) in zip(*carry.parts):
            _set_results(c, outs[oo:oo + len(c.out_shapes)])


ANY = pl.BlockSpec(memory_space=pl.ANY)


def _call(body, *, name, grid, in_specs, out_specs, out_shape, args, scratch=(), vmem_limit=None, carry=None,
          semantics=None):
    n_ci, n_co, n_cs = len(args), len(out_shape), len(scratch)
    semantics = semantics or ("parallel",) * len(grid)
    if carry is None:
        res = pl.pallas_call(
            body, name=name, grid=grid, in_specs=list(in_specs), out_specs=tuple(out_specs), out_shape=tuple(out_shape),
            scratch_shapes=list(scratch),
            compiler_params=pltpu.CompilerParams(dimension_semantics=semantics, vmem_limit_bytes=vmem_limit),
        )(*args)
        return list(res)
    n_pi, n_po = len(carry.operands), len(carry.out_shapes)

    def full_body(*refs):
        ci, pi = refs[:n_ci], refs[n_ci:n_ci + n_pi]
        o0 = n_ci + n_pi
        co, po = refs[o0:o0 + n_co], refs[o0 + n_co:o0 + n_co + n_po]
        s0 = o0 + n_co + n_po
        cs, ps = refs[s0:s0 + n_cs], refs[s0 + n_cs:]
        ids = [pl.program_id(ax) for ax in range(len(grid))]
        first, last = ids[0] == 0, ids[0] == grid[0] - 1
        for ax in range(1, len(grid)):
            first, last = first & (ids[ax] == 0), last & (ids[ax] == grid[ax] - 1)

        @pl.when(first)
        def _():
            carry.start(pi, po, ps)

        body(*ci, *co, *cs)

        @pl.when(last)
        def _():
            carry.finish(pi, po, ps)

    res = pl.pallas_call(
        full_body, name=name, grid=grid, in_specs=list(in_specs) + [ANY] * n_pi,
        out_specs=tuple(out_specs) + tuple([ANY] * n_po), out_shape=tuple(out_shape) + tuple(carry.out_shapes),
        scratch_shapes=list(scratch) + carry.sems,
        input_output_aliases={n_ci + i: n_co + o for i, o in carry.aliases.items()},
        compiler_params=pltpu.CompilerParams(dimension_semantics=("arbitrary",) * len(grid),
                                             vmem_limit_bytes=vmem_limit),
    )(*args, *carry.operands)
    _set_results(carry, res[n_co:])
    return list(res[:n_co])


def _comm_call(carry, *, name):
    n_pi, n_po = len(carry.operands), len(carry.out_shapes)

    def body(*refs):
        pi, po, ps = refs[:n_pi], refs[n_pi:n_pi + n_po], refs[n_pi + n_po:]
        carry.start(pi, po, ps)
        carry.finish(pi, po, ps)

    res = pl.pallas_call(
        body, name=name, in_specs=[ANY] * n_pi, out_specs=tuple([ANY] * n_po), out_shape=tuple(carry.out_shapes),
        scratch_shapes=carry.sems, input_output_aliases=dict(carry.aliases),
    )(*carry.operands)
    _set_results(carry, res)


def _mm_nn(a, b, *, bm, bn, out_dtype, name, bias=None, carry=None):
    M, K = a.shape
    stacked = b.ndim == 3
    N = b.shape[0] * b.shape[2] if stacked else b.shape[1]
    assert M % bm == 0 and N % bn == 0 and (not stacked or bn == b.shape[2])

    def body(*refs):
        a_ref, b_ref = refs[0], refs[1]
        o_ref = refs[-1]
        acc = jnp.dot(a_ref[...], b_ref[...], preferred_element_type=F32)
        if bias is not None:
            acc = acc + refs[2][...]
        o_ref[...] = acc.astype(o_ref.dtype)

    if stacked:
        b_spec = pl.BlockSpec((None, K, bn), lambda i, j: (j, 0, 0))
    else:
        b_spec = pl.BlockSpec((K, bn), lambda i, j: (0, j))
    in_specs = [pl.BlockSpec((bm, K), lambda i, j: (i, 0)), b_spec]
    args = [a, b]
    if bias is not None:
        in_specs.append(pl.BlockSpec((1, bn), lambda i, j: (0, j)))
        args.append(bias)
    limit = _vmem_limit(_nbytes((bm, K), a.dtype), _nbytes((K, bn), b.dtype), _nbytes((bm, bn), F32))
    return _call(body, name=name, grid=(M // bm, N // bn), in_specs=in_specs,
                 out_specs=[pl.BlockSpec((bm, bn), lambda i, j: (i, j))],
                 out_shape=[jax.ShapeDtypeStruct((M, N), out_dtype)], args=args, vmem_limit=limit, carry=carry)[0]


def _mm_nt(a, b, *, bm, bn, out_dtype, name, add=None, carry=None):
    M, K = a.shape
    stacked = b.ndim == 3
    N = b.shape[1] if stacked else b.shape[0]
    n_sh = b.shape[0] if stacked else 1
    ks = b.shape[2] if stacked else K
    assert M % bm == 0 and N % bn == 0 and n_sh * ks == K
    dn = (((1,), (1,)), ((), ()))

    def body(*refs):
        a_ref, b_ref = refs[0], refs[1]
        o_ref = refs[-1]
        if stacked:
            acc = lax.dot_general(a_ref[:, 0:ks], b_ref[0], dn, preferred_element_type=F32)
            for s in range(1, n_sh):
                acc = acc + lax.dot_general(a_ref[:, s * ks:(s + 1) * ks], b_ref[s], dn, preferred_element_type=F32)
        else:
            acc = lax.dot_general(a_ref[...], b_ref[...], dn, preferred_element_type=F32)
        if add is not None:
            acc = acc + refs[2][...]
        o_ref[...] = acc.astype(o_ref.dtype)

    if stacked:
        b_spec = pl.BlockSpec((n_sh, bn, ks), lambda i, j: (0, j, 0))
    else:
        b_spec = pl.BlockSpec((bn, K), lambda i, j: (j, 0))
    in_specs = [pl.BlockSpec((bm, K), lambda i, j: (i, 0)), b_spec]
    args = [a, b]
    if add is not None:
        rows = bm if add.shape[0] == M else 1
        in_specs.append(pl.BlockSpec((rows, bn), lambda i, j: (i if rows == bm else 0, j)))
        args.append(add)
    limit = _vmem_limit(_nbytes((bm, K), a.dtype), _nbytes((bn, K), b.dtype), _nbytes((bm, bn), F32))
    return _call(body, name=name, grid=(M // bm, N // bn), in_specs=in_specs,
                 out_specs=[pl.BlockSpec((bm, bn), lambda i, j: (i, j))],
                 out_shape=[jax.ShapeDtypeStruct((M, N), out_dtype)], args=args, vmem_limit=limit, carry=carry)[0]


def _mm_tn(a, b, *, bm, bn, out_dtype, name, n_stack=None, carry=None):
    T, M = a.shape
    N = b.shape[1]
    assert M % bm == 0 and N % bn == 0 and (n_stack is None or bn * n_stack == N)
    dn = (((0,), (0,)), ((), ()))

    def body(a_ref, b_ref, o_ref):
        acc = lax.dot_general(a_ref[...], b_ref[...], dn, preferred_element_type=F32)
        o_ref[...] = acc.astype(o_ref.dtype)

    if n_stack is None:
        out_spec = pl.BlockSpec((bm, bn), lambda i, j: (i, j))
        out_shape = jax.ShapeDtypeStruct((M, N), out_dtype)
    else:
        out_spec = pl.BlockSpec((None, bm, bn), lambda i, j: (j, i, 0))
        out_shape = jax.ShapeDtypeStruct((n_stack, M, bn), out_dtype)
    limit = _vmem_limit(_nbytes((T, bm), a.dtype), _nbytes((T, bn), b.dtype), _nbytes((bm, bn), F32))
    return _call(body, name=name, grid=(M // bm, N // bn),
                 in_specs=[pl.BlockSpec((T, bm), lambda i, j: (0, i)), pl.BlockSpec((T, bn), lambda i, j: (0, j))],
                 out_specs=[out_spec], out_shape=[out_shape], args=[a, b], vmem_limit=limit, carry=carry)[0]


ROW_BLK = 256


def _rstd(x):
    return lax.rsqrt(jnp.mean(x * x, axis=-1, keepdims=True) + EPS)


def _norm_bwd(xhat, rstd, dxhat):
    return rstd * (dxhat - xhat * jnp.mean(dxhat * xhat, axis=-1, keepdims=True))


def _row_spec(cols):
    return pl.BlockSpec((ROW_BLK, cols), lambda i: (i, 0))


def _vec_spec(cols):
    return pl.BlockSpec((1, cols), lambda i: (0, 0))


def _prologue(x, g, rel_bias, idx_all, *, name, carry=None):
    T, D = x.shape
    n_pat = idx_all.shape[0]
    heads = 2 * N_PAIRS
    steps = n_pat * heads
    rows = T // steps

    def body(rb_ref, x_ref, g_ref, idx_ref, n_ref, bias_ref):
        s = pl.program_id(0)
        head = jnp.where(s < heads, s, heads + s % heads)
        xv = x_ref[...]
        n_ref[...] = (xv * _rstd(xv) * g_ref[...]).astype(n_ref.dtype)
        idxv = idx_ref[...]
        acc = jnp.where(idxv < 0, NEG, 0.0).astype(F32)
        for b in range(NUM_BUCKETS):
            acc = acc + jnp.where(idxv == b, rb_ref[b, head], 0.0)
        bias_ref[0] = acc
        kap = lax.broadcasted_iota(jnp.int32, (1, 2 * QBLK), 1)
        bias_ref[1] = jnp.where(kap < QBLK, NEG, acc)

    return _call(
        body, name=name, grid=(steps,),
        in_specs=[pl.BlockSpec(memory_space=pltpu.SMEM), pl.BlockSpec((rows, D), lambda s: (s, 0)),
                  pl.BlockSpec((1, D), lambda s: (0, 0)),
                  pl.BlockSpec((None, QBLK, 2 * QBLK), lambda s: (s // heads, 0, 0))],
        out_specs=[pl.BlockSpec((rows, D), lambda s: (s, 0)),
                   pl.BlockSpec((None, 2, None, QBLK, 2 * QBLK), lambda s: (s // heads, 0, s % heads, 0, 0))],
        out_shape=[jax.ShapeDtypeStruct((T, D), BF16),
                   jax.ShapeDtypeStruct((n_pat, 2, heads, QBLK, 2 * QBLK), F32)],
        args=[rel_bias, x, g, idx_all], carry=carry)


def _mm_resid_norm(a, b, h, g_post, coef, g_next, *, bm, name, bias=None, carry=None):
    M, K = a.shape
    N = b.shape[1]

    def body(*refs):
        a_ref, b_ref, h_ref, gp_ref, gn_ref = refs[0:5]
        f_ref, hn_ref, n_ref = refs[-3:]
        for rows in _row_halves(bm):
            f = jnp.dot(a_ref[rows, :], b_ref[...], preferred_element_type=F32)
            if bias is not None:
                f = f + refs[5][...]
            f_ref[rows, :] = f
            hn = h_ref[rows, :] + coef * (f * _rstd(f) * gp_ref[...])
            hn_ref[rows, :] = hn
            n_ref[rows, :] = (hn * _rstd(hn) * gn_ref[...]).astype(n_ref.dtype)

    row = lambda w: pl.BlockSpec((bm, w), lambda i: (i, 0))
    vec = pl.BlockSpec((1, N), lambda i: (0, 0))
    in_specs = [row(K), pl.BlockSpec((K, N), lambda i: (0, 0), pipeline_mode=pl.Buffered(1)), row(N), vec, vec]
    args = [a, b, h, g_post, g_next]
    if bias is not None:
        in_specs.append(vec)
        args.append(bias)
    limit = _vmem_limit(_nbytes((bm, K), a.dtype), _nbytes((K, N), b.dtype) // 2, 4 * _nbytes((bm, N), F32))
    return _call(body, name=name, grid=(M // bm,), in_specs=in_specs, out_specs=[row(N), row(N), row(N)],
                 out_shape=[jax.ShapeDtypeStruct((M, N), F32), jax.ShapeDtypeStruct((M, N), F32),
                            jax.ShapeDtypeStruct((M, N), BF16)], args=args, vmem_limit=limit, carry=carry)


def _mm_nt_norms_bwd(a, b, dh_in, h, g_pre, f, g_post, coef, *, bm, name, add=None, b_is_kn=False, carry=None):
    M, K = a.shape
    stacked = b.ndim == 3
    N = b.shape[1] if (stacked or b_is_kn) else b.shape[0]
    n_sh = b.shape[0] if stacked else 1
    ks = b.shape[2] if stacked else K
    assert n_sh * ks == K
    dn_dims = (((1,), (0,)), ((), ())) if b_is_kn else (((1,), (1,)), ((), ()))
    with_f = f is not None
    n_in = 5 + (2 if with_f else 0) + (1 if add is not None else 0)

    def body(*refs):
        a_ref, b_ref, dhi_ref, h_ref, gpre_ref = refs[0:5]
        outs = refs[n_in:]

        @pl.when(pl.program_id(0) == 0)
        def _():
            for acc in (outs[2:] if with_f else outs[1:]):
                acc[...] = jnp.zeros_like(acc)

        for rows in _row_halves(bm):
            if stacked:
                dn = lax.dot_general(a_ref[rows, 0:ks], b_ref[0], dn_dims, preferred_element_type=F32)
                for s in range(1, n_sh):
                    dn = dn + lax.dot_general(a_ref[rows, s * ks:(s + 1) * ks], b_ref[s], dn_dims,
                                              preferred_element_type=F32)
            else:
                dn = lax.dot_general(a_ref[rows, :], b_ref[...], dn_dims, preferred_element_type=F32)
            if add is not None:
                dn = dn + refs[n_in - 1][rows, :]
            hv = h_ref[rows, :]
            r = _rstd(hv)
            hh = hv * r
            dh = dhi_ref[rows, :] + _norm_bwd(hh, r, dn * gpre_ref[...])
            outs[0][rows, :] = dh
            if not with_f:
                outs[1][...] += jnp.sum(dn * hh, axis=0, keepdims=True)
                continue
            f_ref, gpost_ref = refs[5], refs[6]
            df_ref, dgpre_ref, dgpost_ref, dsum_ref = outs[1:]
            dgpre_ref[...] += jnp.sum(dn * hh, axis=0, keepdims=True)
            fv = f_ref[rows, :]
            rf = _rstd(fv)
            fh = fv * rf
            dy = coef * dh
            dgpost_ref[...] += jnp.sum(dy * fh, axis=0, keepdims=True)
            df = _norm_bwd(fh, rf, dy * gpost_ref[...])
            dsum_ref[...] += jnp.sum(df, axis=0, keepdims=True)
            df_ref[rows, :] = df.astype(df_ref.dtype)

    row = lambda w: pl.BlockSpec((bm, w), lambda i: (i, 0))
    vec = pl.BlockSpec((1, N), lambda i: (0, 0))
    once = pl.Buffered(1)
    if stacked:
        b_spec = pl.BlockSpec((n_sh, N, ks), lambda i: (0, 0, 0), pipeline_mode=once)
    else:
        b_spec = pl.BlockSpec(b.shape, lambda i: (0, 0), pipeline_mode=once)
    in_specs = [row(K), b_spec, row(N), row(N), vec]
    args = [a, b, dh_in, h, g_pre]
    if with_f:
        in_specs += [row(N), vec]
        args += [f, g_post]
    if add is not None:
        in_specs.append(row(N))
        args.append(add)
    vec_shape = jax.ShapeDtypeStruct((1, N), F32)
    out_specs = [row(N)] + ([row(N), vec, vec, vec] if with_f else [vec])
    out_shape = [jax.ShapeDtypeStruct((M, N), F32)]
    out_shape += [jax.ShapeDtypeStruct((M, N), BF16), vec_shape, vec_shape, vec_shape] if with_f else [vec_shape]
    limit = _vmem_limit(_nbytes((bm, K), a.dtype), _nbytes((N, K), b.dtype) // 2, 6 * _nbytes((bm, N), F32))
    return _call(body, name=name, grid=(M // bm,), in_specs=in_specs, out_specs=out_specs, out_shape=out_shape,
                 args=args, vmem_limit=limit, semantics=("arbitrary",), carry=carry)


def _ffn_up(n, w_gu, *, bm, name, carry=None):
    M, K = n.shape
    n_sh, _, ns = w_gu.shape
    half = n_sh // 2

    def body(n_ref, wg_ref, wu_ref, g_ref, u_ref, a_ref):
        nv = n_ref[...]
        for cols in _col_chunks(ns):
            g = jnp.dot(nv, wg_ref[:, cols], preferred_element_type=F32)
            u = jnp.dot(nv, wu_ref[:, cols], preferred_element_type=F32)
            g_ref[:, cols] = g.astype(g_ref.dtype)
            u_ref[:, cols] = u.astype(u_ref.dtype)
            a_ref[:, cols] = (g * jax.nn.sigmoid(g) * u).astype(a_ref.dtype)

    out_spec = pl.BlockSpec((bm, ns), lambda i, j: (i, j))
    out = jax.ShapeDtypeStruct((M, half * ns), BF16)
    limit = _vmem_limit(_nbytes((bm, K), n.dtype), 2 * _nbytes((K, ns), w_gu.dtype), 3 * _nbytes((bm, ns), F32))
    return _call(body, name=name, grid=(M // bm, half),
                 in_specs=[pl.BlockSpec((bm, K), lambda i, j: (i, 0)),
                           pl.BlockSpec((None, K, ns), lambda i, j: (j, 0, 0)),
                           pl.BlockSpec((None, K, ns), lambda i, j: (j + half, 0, 0))],
                 out_specs=[out_spec, out_spec, out_spec], out_shape=[out, out, out], args=[n, w_gu, w_gu],
                 vmem_limit=limit, carry=carry)


def _ffn_down_bwd(df, w_down, g, u, *, bm, name, carry=None):
    M, D = df.shape
    F = w_down.shape[0]
    dn = (((1,), (1,)), ((), ()))

    def body(df_ref, w_ref, g_ref, u_ref, o_ref):
        dfv = df_ref[...]
        for cols in _col_chunks(F):
            da = lax.dot_general(dfv, w_ref[cols, :], dn, preferred_element_type=F32)
            gv = g_ref[:, cols].astype(F32)
            s = jax.nn.sigmoid(gv)
            o_ref[:, cols] = (da * u_ref[:, cols].astype(F32) * (s * (1.0 + gv * (1.0 - s)))).astype(o_ref.dtype)
            o_ref[:, F + cols.start:F + cols.stop] = (da * (gv * s)).astype(o_ref.dtype)

    row = lambda w: pl.BlockSpec((bm, w), lambda i: (i, 0))
    limit = _vmem_limit(_nbytes((F, D), w_down.dtype) // 2, 2 * _nbytes((bm, F), BF16), _nbytes((bm, 2 * F), BF16))
    return _call(body, name=name, grid=(M // bm,),
                 in_specs=[row(D), pl.BlockSpec((F, D), lambda i: (0, 0), pipeline_mode=pl.Buffered(1)), row(F), row(F)],
                 out_specs=[row(2 * F)], out_shape=[jax.ShapeDtypeStruct((M, 2 * F), BF16)],
                 args=[df, w_down, g, u], vmem_limit=limit, carry=carry)[0]


def _ple_head(n4, w_gate, p, w_ple, h3, g_post, target, *, bm, name):
    T, D = h3.shape
    kp = p.shape[1]

    def body(n_ref, wg_ref, p_ref, wp_ref, h_ref, g_ref, t_ref, dh_ref, de_ref, dgp_ref, loss_ref, dg_ref):
        @pl.when(pl.program_id(0) == 0)
        def _():
            loss_ref[...] = jnp.zeros_like(loss_ref)
            dg_ref[...] = jnp.zeros_like(dg_ref)

        gpost = g_ref[...]
        for rows in _row_halves(bm):
            gate = jax.nn.sigmoid(jnp.dot(n_ref[rows, :], wg_ref[...], preferred_element_type=F32))
            ev = jnp.dot(p_ref[rows, :], wp_ref[...], preferred_element_type=F32)
            ge = gate * ev
            r = _rstd(ge)
            gh = ge * r
            diff = h_ref[rows, :] + gh * gpost - t_ref[rows, :]
            loss_ref[...] += jnp.sum(diff * diff) * (0.5 / D)
            dh = diff * (1.0 / D)
            dh_ref[rows, :] = dh
            dg_ref[...] += jnp.sum(dh * gh, axis=0, keepdims=True)
            dge = _norm_bwd(gh, r, dh * gpost)
            de_ref[rows, :] = (dge * gate).astype(de_ref.dtype)
            dgp_ref[rows, :] = (dge * ev * gate * (1.0 - gate)).astype(dgp_ref.dtype)

    row = lambda w: pl.BlockSpec((bm, w), lambda i: (i, 0))
    whole = lambda a: pl.BlockSpec(a.shape, lambda i: (0, 0), pipeline_mode=pl.Buffered(1))
    limit = _vmem_limit(_nbytes((bm, D), BF16), _nbytes(w_gate.shape, w_gate.dtype) // 2, 6 * _nbytes((bm, D), F32))
    return pl.pallas_call(
        body, name=name, grid=(T // bm,),
        in_specs=[row(D), whole(w_gate), row(kp), whole(w_ple), row(D), _vec_spec(D), row(D)],
        out_specs=(row(D), row(D), row(D), _vec_spec(LANES), _vec_spec(D)),
        out_shape=(jax.ShapeDtypeStruct((T, D), F32), jax.ShapeDtypeStruct((T, D), BF16),
                   jax.ShapeDtypeStruct((T, D), BF16), jax.ShapeDtypeStruct((1, LANES), F32),
                   jax.ShapeDtypeStruct((1, D), F32)),
        compiler_params=pltpu.CompilerParams(dimension_semantics=("arbitrary",), vmem_limit_bytes=limit),
    )(n4, w_gate, p, w_ple, h3, g_post, target)


def _t5_index(max_dist, stride):
    rho = np.arange(QBLK)[:, None]
    kap = np.arange(2 * QBLK)[None, :]
    d = rho + QBLK - kap
    valid = (d >= 0) & (d <= max_dist)
    n = np.maximum(d, 0) * stride
    max_exact = NUM_BUCKETS // 2
    nf = np.maximum(n, 1).astype(np.float32)
    large = max_exact + (np.log(nf / np.float32(max_exact)) / np.float32(math.log(MAX_DISTANCE / max_exact))
                         * np.float32(NUM_BUCKETS - max_exact)).astype(np.int32)
    large = np.minimum(large, NUM_BUCKETS - 1)
    bucket = np.where(n < max_exact, n, large)
    return np.where(valid, bucket, -1).astype(np.int32)


def _bias_grad(d_bias, idx, *, name):
    def body(db_ref, idx_ref, o_ref):
        h = pl.program_id(0)

        @pl.when(h == 0)
        def _():
            o_ref[...] = jnp.zeros_like(o_ref)

        idxv = idx_ref[...]
        dv = db_ref[0]
        rows = lax.broadcasted_iota(jnp.int32, (NUM_BUCKETS, LANES), 0)
        lanes = lax.broadcasted_iota(jnp.int32, (NUM_BUCKETS, LANES), 1)
        acc = o_ref[...]
        for b in range(NUM_BUCKETS):
            s = jnp.sum(jnp.where(idxv == b, dv, 0.0))
            acc = acc + jnp.where((rows == b) & (lanes == h), s, 0.0)
        o_ref[...] = acc

    return pl.pallas_call(
        body, name=name, grid=(2 * N_PAIRS,),
        in_specs=[pl.BlockSpec((1, QBLK, 2 * QBLK), lambda h: (h, 0, 0)),
                  pl.BlockSpec((QBLK, 2 * QBLK), lambda h: (0, 0))],
        out_specs=pl.BlockSpec((NUM_BUCKETS, LANES), lambda h: (0, 0)),
        out_shape=jax.ShapeDtypeStruct((NUM_BUCKETS, LANES), F32),
        compiler_params=pltpu.CompilerParams(dimension_semantics=("arbitrary",)),
    )(d_bias, idx)


def _lane():
    return lax.broadcasted_iota(jnp.int32, (1, LANES), 1)


def _head_sel(h):
    return (_lane() < HEAD_DIM) if h == 0 else (_lane() >= HEAD_DIM)


def _stat_lo():
    return (_lane() % HEAD_DIM) < (HEAD_DIM // 2)


def _stack_heads(t, scale=None):
    if scale is not None:
        t = t * scale
    zero = jnp.zeros_like(t)
    return jnp.concatenate([jnp.where(_head_sel(0), t, zero), jnp.where(_head_sel(1), t, zero)], axis=0)


def _class_spec(L, r, cols, stride, pps):
    chunks = N_PAIRS // pps
    mode = pl.Buffered(1) if r * chunks == 1 else None
    return pl.BlockSpec((L, MIX_W // chunks), lambda j, c: (0, (cols + j * stride) * chunks + c), pipeline_mode=mode)


def _rows(b, n_blocks=1):
    return pl.ds(pl.multiple_of(b * QBLK, QBLK), n_blocks * QBLK)


def _key_window(ref, b, cols, first):
    if first:
        blk = ref[0:QBLK, cols]
        return jnp.concatenate([blk, blk], axis=0)
    return ref[_rows(b - 1, 2), cols]


def _band_fwd(qa, ka, va, bias, *, r, q_col, k_col, v_col, stride, pattern, name, carry=None):
    L = qa.shape[0]
    nb = L // QBLK
    dn = (((1,), (1,)), ((), ()))
    scale = HEAD_DIM ** -0.5

    pps = N_PAIRS

    def body(q_ref, k_ref, v_ref, bias_ref, o_ref, lse_ref):
        sel0 = _head_sel(0)
        pair0 = pl.program_id(1) * pps

        def block(b, first):
            rows = _rows(b)
            for i in range(pps):
                cols = slice(i * LANES, (i + 1) * LANES)
                q2 = _stack_heads(q_ref[rows, cols], scale)
                k = _key_window(k_ref, b, cols, first)
                v = _key_window(v_ref, b, cols, first)
                bias2 = bias_ref[1 if first else 0, pl.ds(2 * (pair0 + i), 2)].reshape(2 * QBLK, 2 * QBLK)
                s = lax.dot_general(q2, k, dn, preferred_element_type=F32) + bias2
                m = jnp.max(s, axis=1, keepdims=True)
                p = jnp.exp(s - m)
                l = jnp.sum(p, axis=1, keepdims=True)
                o = jnp.dot(p.astype(v.dtype), v, preferred_element_type=F32) / l
                lse = m + jnp.log(l)
                o_ref[rows, cols] = jnp.where(sel0, o[0:QBLK], o[QBLK:]).astype(o_ref.dtype)
                lse_ref[rows, cols] = jnp.where(sel0, lse[0:QBLK], lse[QBLK:])

        block(0, True)
        if nb > 1:
            pl.loop(1, nb)(lambda b: block(b, False))

    bias_spec = pl.BlockSpec((None, 2, 2 * N_PAIRS, QBLK, 2 * QBLK), lambda j, c: (pattern, 0, 0, 0, 0))
    out_spec = _class_spec(L, r, 0, 1, pps)
    blk_bytes = _nbytes((L, pps * LANES), F32)
    return _call(
        body, name=name, grid=(r, N_PAIRS // pps),
        in_specs=[_class_spec(L, r, q_col, stride, pps), _class_spec(L, r, k_col, stride, pps),
                  _class_spec(L, r, v_col, stride, pps), bias_spec],
        out_specs=[out_spec, out_spec],
        out_shape=[jax.ShapeDtypeStruct((L, r * MIX_W), BF16), jax.ShapeDtypeStruct((L, r * MIX_W), F32)],
        args=[qa, ka, va, bias], vmem_limit=_vmem_limit(*([blk_bytes] * 4)), carry=carry)


def _class_rows_spec(rows, r, width):
    return pl.BlockSpec((rows // r, r * width), lambda i, *_: (i, 0))


def _token_scratch(rows, width):
    return pltpu.VMEM((width // LANES, rows, LANES), F32)


def _fill_tokens(scratch, value):
    for c in range(scratch.shape[0]):
        scratch[c] = value[:, c * LANES:(c + 1) * LANES]


def _read_tokens(scratch):
    return jnp.concatenate([scratch[c] for c in range(scratch.shape[0])], axis=1)


def _to_tokens(blk_ref, r, scratch):
    if r == 1:
        return blk_ref[...].astype(F32)
    nt, rows, _ = scratch.shape
    for j in range(r):
        for c in range(nt):
            lanes = slice((j * nt + c) * LANES, (j * nt + c + 1) * LANES)
            scratch.at[c][pl.ds(j, rows // r, stride=r), :] = blk_ref[:, lanes].astype(F32)
    return _read_tokens(scratch)


def _from_tokens(dst_ref, r, scratch):
    nt, rows, _ = scratch.shape
    if r == 1:
        dst_ref[...] = _read_tokens(scratch).astype(dst_ref.dtype)
        return
    for j in range(r):
        for c in range(nt):
            lanes = slice((j * nt + c) * LANES, (j * nt + c + 1) * LANES)
            dst_ref[:, lanes] = scratch.at[c][pl.ds(j, rows // r, stride=r), :].astype(dst_ref.dtype)


def _mm_nt_classes(a, b, bias, dils, *, bm, name):
    M, K = a.shape
    N = b.shape[0]
    dn = (((1,), (1,)), ((), ()))

    def body(a_ref, b_ref, bias_ref, *rest):
        outs, tile = rest[:-1], rest[-1]
        _fill_tokens(tile, lax.dot_general(a_ref[...], b_ref[...], dn, preferred_element_type=F32) + bias_ref[...])
        for out, r in zip(outs, dils):
            _from_tokens(out, r, tile)

    limit = _vmem_limit(_nbytes((bm, K), a.dtype), _nbytes((K, N), b.dtype) // 2, (1 + len(dils)) * _nbytes((bm, N), F32))
    return pl.pallas_call(
        body, name=name, grid=(M // bm,),
        in_specs=[pl.BlockSpec((bm, K), lambda i: (i, 0)),
                  pl.BlockSpec((N, K), lambda i: (0, 0), pipeline_mode=pl.Buffered(1)),
                  pl.BlockSpec((1, N), lambda i: (0, 0))],
        out_specs=tuple(_class_rows_spec(bm, r, N) for r in dils),
        out_shape=tuple(jax.ShapeDtypeStruct((M // r, r * N), BF16) for r in dils),
        scratch_shapes=[_token_scratch(bm, N)],
        compiler_params=pltpu.CompilerParams(dimension_semantics=("parallel",), vmem_limit_bytes=limit),
    )(a, b, bias)


def _merge(branches, dils, sink, *, name):
    W = MIX_W
    T = branches[0][0].shape[0] * dils[0]
    nbr = len(branches)

    def body(*refs):
        sink_ref = refs[2 * nbr]
        out_ref, outb_ref, lse_ref, scratch = refs[2 * nbr + 1:]
        sk = sink_ref[...]
        lses = [_to_tokens(refs[2 * t + 1], dils[t], scratch) for t in range(nbr)]
        mx = sk
        for lse in lses:
            mx = jnp.maximum(mx, lse)
        den = jnp.exp(sk - mx)
        num = jnp.zeros_like(mx)
        for t in range(nbr):
            w = jnp.exp(lses[t] - mx)
            den = den + w
            num = num + w * _to_tokens(refs[2 * t], dils[t], scratch)
        out = num / den
        out_ref[...] = out
        outb_ref[...] = out.astype(outb_ref.dtype)
        lse_ref[...] = mx + jnp.log(den)

    args = [a for br in branches for a in br] + [sink]
    in_specs = [_class_rows_spec(ROW_BLK, r, W) for r in dils for _ in range(2)] + [_vec_spec(W)]
    return pl.pallas_call(
        body, name=name, grid=(T // ROW_BLK,), in_specs=in_specs,
        out_specs=(_row_spec(W), _row_spec(W), _row_spec(W)),
        out_shape=(jax.ShapeDtypeStruct((T, W), F32), jax.ShapeDtypeStruct((T, W), BF16),
                   jax.ShapeDtypeStruct((T, W), F32)),
        scratch_shapes=[_token_scratch(ROW_BLK, W)],
        compiler_params=pltpu.CompilerParams(dimension_semantics=("parallel",)),
    )(*args)


def _attn_delta(dmix, outs, lses, sink, dils, *, name):
    T = dmix.shape[0]
    W = MIX_W
    nd = len(dils)

    def body(dmix_ref, oa_ref, ob_ref, la_ref, lb_ref, sink_ref, *rest):
        doa_ref, sta_ref, dsink_ref = rest[0:3]
        dob_refs, stb_refs = rest[3:3 + nd], rest[3 + nd:3 + 2 * nd]
        do_tile, st_tile = rest[3 + 2 * nd:]

        @pl.when(pl.program_id(0) == 0)
        def _():
            dsink_ref[...] = jnp.zeros_like(dsink_ref)

        sel0, lo = _head_sel(0), _stat_lo()
        for mixer, (o_ref, l_ref) in enumerate(((oa_ref, la_ref), (ob_ref, lb_ref))):
            for i in range(N_PAIRS):
                cols = slice(i * LANES, (i + 1) * LANES)
                do = dmix_ref[:, mixer * W + i * LANES:mixer * W + (i + 1) * LANES]
                prod = do * o_ref[:, cols]
                d0 = jnp.sum(jnp.where(sel0, prod, 0.0), axis=1, keepdims=True)
                d1 = jnp.sum(jnp.where(sel0, 0.0, prod), axis=1, keepdims=True)
                delta = jnp.where(sel0, d0, d1)
                lse = l_ref[:, cols]
                stat = jnp.where(lo, lse, delta)
                if mixer == 0:
                    sta_ref[:, cols] = stat
                    doa_ref[:, cols] = do.astype(doa_ref.dtype)
                    dsink_ref[:, cols] += -jnp.sum(jnp.exp(sink_ref[:, cols] - lse) * delta, axis=0, keepdims=True)
                else:
                    st_tile[i] = stat
                    do_tile[i] = do
        for t, r in enumerate(dils):
            _from_tokens(dob_refs[t], r, do_tile)
            _from_tokens(stb_refs[t], r, st_tile)

    class_specs = [_class_rows_spec(ROW_BLK, r, W) for r in dils]
    out_shape = [jax.ShapeDtypeStruct((T, W), BF16), jax.ShapeDtypeStruct((T, W), F32), jax.ShapeDtypeStruct((1, W), F32)]
    out_shape += [jax.ShapeDtypeStruct((T // r, r * W), BF16) for r in dils]
    out_shape += [jax.ShapeDtypeStruct((T // r, r * W), F32) for r in dils]
    res = pl.pallas_call(
        body, name=name, grid=(T // ROW_BLK,),
        in_specs=[_row_spec(2 * W), _row_spec(W), _row_spec(W), _row_spec(W), _row_spec(W), _vec_spec(W)],
        out_specs=tuple([_row_spec(W), _row_spec(W), _vec_spec(W)] + class_specs + class_specs),
        out_shape=tuple(out_shape),
        scratch_shapes=[_token_scratch(ROW_BLK, W), _token_scratch(ROW_BLK, W)],
        compiler_params=pltpu.CompilerParams(dimension_semantics=("arbitrary",)),
    )(dmix, outs[0], outs[1], lses[0], lses[1], sink)
    return res[0], res[1], res[2], res[3:3 + nd], res[3 + nd:]


def _band_bwd(qa, ka, va, d_out, stat, bias, *, r, q_col, k_col, v_col, stride, pattern, name, carry=None):
    L = qa.shape[0]
    nb = L // QBLK
    dn_nt = (((1,), (1,)), ((), ()))
    dn_tn = (((0,), (0,)), ((), ()))
    scale = HEAD_DIM ** -0.5

    pps = N_PAIRS if r > 1 else N_PAIRS // 2

    def body(q_ref, k_ref, v_ref, do_ref, st_ref, bias_ref, dq_ref, dk_ref, dv_ref, db_ref, dk_carry, dv_carry):
        @pl.when((pl.program_id(0) == 0) & (pl.program_id(1) == 0))
        def _():
            db_ref[...] = jnp.zeros_like(db_ref)

        lo, sel0 = _stat_lo(), _head_sel(0)
        pair0 = pl.program_id(1) * pps

        def block(b, first):
            rows = _rows(b)
            for i in range(pps):
                heads = pl.ds(2 * (pair0 + i), 2)
                cols = slice(i * LANES, (i + 1) * LANES)
                q2 = _stack_heads(q_ref[rows, cols], scale)
                k = _key_window(k_ref, b, cols, first)
                v = _key_window(v_ref, b, cols, first)
                do2 = _stack_heads(do_ref[rows, cols])
                st = st_ref[rows, cols]
                lse2 = jnp.concatenate(
                    [jnp.max(jnp.where(_head_sel(h) & lo, st, -jnp.inf), axis=1, keepdims=True) for h in range(2)], axis=0)
                delta2 = jnp.concatenate(
                    [jnp.sum(jnp.where(_head_sel(h) & ~lo, st, 0.0), axis=1, keepdims=True) for h in range(2)],
                    axis=0) * (2.0 / HEAD_DIM)
                bias2 = bias_ref[1 if first else 0, heads].reshape(2 * QBLK, 2 * QBLK)
                s = lax.dot_general(q2, k, dn_nt, preferred_element_type=F32) + bias2
                p = jnp.exp(s - lse2)
                dp = lax.dot_general(do2, v, dn_nt, preferred_element_type=F32)
                ds = p * (dp - delta2)
                db_ref[heads] += ds.reshape(2, QBLK, 2 * QBLK)
                dsb = ds.astype(k.dtype)
                dq2 = jnp.dot(dsb, k, preferred_element_type=F32)
                dq_ref[rows, cols] = (jnp.where(sel0, dq2[0:QBLK], dq2[QBLK:]) * scale).astype(dq_ref.dtype)
                dk_acc = lax.dot_general(dsb, q2, dn_tn, preferred_element_type=F32)
                dv_acc = lax.dot_general(p.astype(k.dtype), do2, dn_tn, preferred_element_type=F32)
                if not first:
                    prev = _rows(b - 1)
                    dk_ref[prev, cols] = (dk_carry[:, cols] + dk_acc[0:QBLK]).astype(dk_ref.dtype)
                    dv_ref[prev, cols] = (dv_carry[:, cols] + dv_acc[0:QBLK]).astype(dv_ref.dtype)
                dk_carry[:, cols] = dk_acc[QBLK:2 * QBLK]
                dv_carry[:, cols] = dv_acc[QBLK:2 * QBLK]

        block(0, True)
        if nb > 1:
            pl.loop(1, nb)(lambda b: block(b, False))

        last = _rows(nb - 1)
        dk_ref[last, :] = dk_carry[...].astype(dk_ref.dtype)
        dv_ref[last, :] = dv_carry[...].astype(dv_ref.dtype)

    tok_spec = _class_spec(L, r, 0, 1, pps)
    bias_spec = pl.BlockSpec((None, 2, 2 * N_PAIRS, QBLK, 2 * QBLK), lambda j, c: (pattern, 0, 0, 0, 0))
    dbias_spec = pl.BlockSpec((2 * N_PAIRS, QBLK, 2 * QBLK), lambda j, c: (0, 0, 0))
    grad = jax.ShapeDtypeStruct((L, r * MIX_W), BF16)
    blk_bytes = _nbytes((L, pps * LANES), BF16)
    carry_shape = pltpu.VMEM((QBLK, pps * LANES), F32)
    return _call(
        body, name=name, grid=(r, N_PAIRS // pps),
        in_specs=[_class_spec(L, r, q_col, stride, pps), _class_spec(L, r, k_col, stride, pps),
                  _class_spec(L, r, v_col, stride, pps), tok_spec, tok_spec, bias_spec],
        out_specs=[tok_spec, tok_spec, tok_spec, dbias_spec],
        out_shape=[grad, grad, grad, jax.ShapeDtypeStruct((2 * N_PAIRS, QBLK, 2 * QBLK), F32)],
        args=[qa, ka, va, d_out, stat, bias], scratch=[carry_shape, carry_shape],
        vmem_limit=_vmem_limit(*([blk_bytes] * 9)), semantics=("arbitrary", "arbitrary"), carry=carry)


def _dz_assemble(dq_a, dk_a, dv_a, b_grads, dils, *, name):
    T = dq_a.shape[0]
    W = MIX_W

    def group_sum(x):
        u0 = x[:, 0:LANES] + x[:, LANES:2 * LANES]
        u1 = x[:, 2 * LANES:3 * LANES] + x[:, 3 * LANES:4 * LANES]
        s0 = u0 + pltpu.roll(u0, HEAD_DIM, 1)
        s1 = u1 + pltpu.roll(u1, HEAD_DIM, 1)
        return jnp.where(_head_sel(0), s0, s1)

    def body(*refs):
        dqa_ref, dka_ref, dva_ref = refs[0:3]
        b_refs = refs[3:12]
        dza_ref, dzb_ref, sa_ref, sb_ref, scratch = refs[12:]

        @pl.when(pl.program_id(0) == 0)
        def _():
            sa_ref[...] = jnp.zeros_like(sa_ref)
            sb_ref[...] = jnp.zeros_like(sb_ref)

        def put(dst, acc, col, part):
            w = part.shape[1]
            dst[:, col:col + w] = part.astype(dst.dtype)
            acc[:, col:col + w] += jnp.sum(part, axis=0, keepdims=True)

        put(dza_ref, sa_ref, 0, dqa_ref[...].astype(F32))
        put(dza_ref, sa_ref, W, group_sum(dka_ref[...].astype(F32)))
        put(dza_ref, sa_ref, W + LANES, group_sum(dva_ref[...].astype(F32)))
        for t in range(3):
            part = _to_tokens(b_refs[t], dils[0], scratch)
            for pat in range(1, len(dils)):
                part = part + _to_tokens(b_refs[3 * pat + t], dils[pat], scratch)
            put(dzb_ref, sb_ref, t * W, part)

    args = [dq_a, dk_a, dv_a] + [g[t] for g in b_grads for t in range(3)]
    return pl.pallas_call(
        body, name=name, grid=(T // ROW_BLK,),
        in_specs=[_row_spec(W)] * 3 + [_class_rows_spec(ROW_BLK, r, W) for r in dils for _ in range(3)],
        out_specs=(_row_spec(ZA_W), _row_spec(ZB_W), _vec_spec(ZA_W), _vec_spec(ZB_W)),
        out_shape=(jax.ShapeDtypeStruct((T, ZA_W), BF16), jax.ShapeDtypeStruct((T, ZB_W), BF16),
                   jax.ShapeDtypeStruct((1, ZA_W), F32), jax.ShapeDtypeStruct((1, ZB_W), F32)),
        scratch_shapes=[_token_scratch(ROW_BLK, W)],
        compiler_params=pltpu.CompilerParams(dimension_semantics=("arbitrary",)),
    )(*args)


def _place():
    x, y, c = lax.axis_index("x"), lax.axis_index("y"), lax.axis_index("c")
    chips = [(1 - x, y), (x, 1 - y), (1 - x, 1 - y)]
    return x, y, c, chips


def _cast_place(shard, k_idx, *, name):
    rs, cs = shard.shape
    rb = _row_block(rs, 256, 16)
    nblk = rs // rb

    def body(k_ref, s_ref, o_ref):
        o_ref[...] = s_ref[...].astype(o_ref.dtype)

    grid_spec = pltpu.PrefetchScalarGridSpec(
        num_scalar_prefetch=1, grid=(nblk,),
        in_specs=[pl.BlockSpec((rb, cs), lambda t, k_ref: (t, 0))],
        out_specs=pl.BlockSpec((rb, cs), lambda t, k_ref: (k_ref[0] * nblk + t, 0)))
    return pl.pallas_call(
        body, name=name, grid_spec=grid_spec, out_shape=jax.ShapeDtypeStruct((N_CHIPS * rs, cs), BF16),
        compiler_params=pltpu.CompilerParams(dimension_semantics=("parallel",)),
    )(k_idx, shard)


def _gather_carry(fulls, part=(0, 1)):
    n = len(fulls)
    p_idx, n_parts = part

    def piece(ref, chip_idx, half):
        rs = ref.shape[0] // N_CHIPS
        rh = rs // 2
        rows = rh // n_parts
        return ref.at[pl.ds(chip_idx * rs + half * rh + p_idx * rows, rows)]

    def copy(sems, sem, src_ref, dst_ref, to):
        return pltpu.make_async_remote_copy(src_ref=src_ref, dst_ref=dst_ref, send_sem=sems[0].at[sem],
                                            recv_sem=sems[1].at[sem], device_id=to, device_id_type=MESH)

    def ici_copy(ins, outs, sems, i, j, chip, k, c):
        return copy(sems, 3 * i + j, piece(ins[i], k, c), piece(outs[i], k, c), (*chip, c))

    def start(ins, outs, sems):
        x, y, c, chips = _place()
        for i in range(n):
            for j, chip in enumerate(chips):
                ici_copy(ins, outs, sems, i, j, chip, 2 * x + y, c).start()

    def finish(ins, outs, sems):
        x, y, c, chips = _place()
        sibling = (x, y, 1 - c)
        passed = []
        for i in range(n):
            for j, chip in enumerate(chips):
                region = piece(outs[i], 2 * chip[0] + chip[1], c)
                copy(sems, 3 * i + j, region, region, (*chip, c)).wait_recv()
                cp = copy(sems, 3 * n + 3 * i + j, region, region, sibling)
                cp.start()
                passed.append(cp)
        for i in range(n):
            for j, chip in enumerate(chips):
                region = piece(outs[i], 2 * chip[0] + chip[1], 1 - c)
                copy(sems, 3 * n + 3 * i + j, region, region, sibling).wait_recv()
        for i in range(n):
            for j, chip in enumerate(chips):
                ici_copy(ins, outs, sems, i, j, chip, 2 * x + y, c).wait_send()
        for cp in passed:
            cp.wait_send()

    return _Carry(fulls, [jax.ShapeDtypeStruct(a.shape, a.dtype) for a in fulls], {i: i for i in range(n)},
                  [pltpu.SemaphoreType.DMA((6 * n,)), pltpu.SemaphoreType.DMA((6 * n,))], start, finish)


def _pair_swap_carry(grads):
    n = len(grads)

    def copy(ins, outs, sems, i):
        x, y, c, _ = _place()
        return pltpu.make_async_remote_copy(src_ref=ins[i].at[:, 1 - c], dst_ref=outs[i], send_sem=sems[0].at[i],
                                            recv_sem=sems[1].at[i], device_id=(x, y, 1 - c), device_id_type=MESH)

    def start(ins, outs, sems):
        for i in range(n):
            copy(ins, outs, sems, i).start()

    def finish(ins, outs, sems):
        for i in range(n):
            copy(ins, outs, sems, i).wait()

    return _Carry(grads, [jax.ShapeDtypeStruct((a.shape[0], a.shape[2], a.shape[3]), a.dtype) for a in grads], {},
                  [pltpu.SemaphoreType.DMA((n,)), pltpu.SemaphoreType.DMA((n,))], start, finish)


def _pair_sum(g4, got, c_idx, *, name):
    _, _, rh, cs = g4.shape
    rb = _row_block(rh, 256, 16)

    def body(c_ref, own_ref, got_ref, o_ref):
        o_ref[...] = (own_ref[...].astype(F32) + got_ref[...].astype(F32)).astype(o_ref.dtype)

    grid_spec = pltpu.PrefetchScalarGridSpec(
        num_scalar_prefetch=1, grid=(N_CHIPS, rh // rb),
        in_specs=[pl.BlockSpec((None, None, rb, cs), lambda k, t, c_ref: (k, c_ref[0], t, 0)),
                  pl.BlockSpec((None, rb, cs), lambda k, t, c_ref: (k, t, 0))],
        out_specs=pl.BlockSpec((None, rb, cs), lambda k, t, c_ref: (k, t, 0)))
    return pl.pallas_call(
        body, name=name, grid_spec=grid_spec, out_shape=jax.ShapeDtypeStruct((N_CHIPS, rh, cs), BF16),
        compiler_params=pltpu.CompilerParams(dimension_semantics=("parallel", "parallel")),
    )(c_idx, g4, got)


def _chip_scatter_carry(sums):
    n = len(sums)

    def copies(ins, outs, sems):
        x, y, c, chips = _place()
        return [pltpu.make_async_remote_copy(src_ref=ins[i].at[2 * chip[0] + chip[1]], dst_ref=outs[i].at[j],
                                             send_sem=sems[0].at[3 * i + j], recv_sem=sems[1].at[3 * i + j],
                                             device_id=(*chip, c), device_id_type=MESH)
                for i in range(n) for j, chip in enumerate(chips)]

    def start(ins, outs, sems):
        for cp in copies(ins, outs, sems):
            cp.start()

    def finish(ins, outs, sems):
        for cp in copies(ins, outs, sems):
            cp.wait()

    return _Carry(sums, [jax.ShapeDtypeStruct((3,) + a.shape[1:], a.dtype) for a in sums], {},
                  [pltpu.SemaphoreType.DMA((3 * n,)), pltpu.SemaphoreType.DMA((3 * n,))], start, finish)


def _chip_sum(sums, got, kc_idx, *, name):
    _, rh, cs = sums.shape
    rb = _row_block(rh, 256, 16)
    nblk = rh // rb

    def body(kc_ref, own_ref, got_ref, o_ref):
        acc = own_ref[...].astype(F32)
        for j in range(3):
            acc = acc + got_ref[j].astype(F32)
        o_ref[...] = acc

    grid_spec = pltpu.PrefetchScalarGridSpec(
        num_scalar_prefetch=1, grid=(nblk,),
        in_specs=[pl.BlockSpec((None, rb, cs), lambda t, kc: (kc[0], t, 0)),
                  pl.BlockSpec((3, rb, cs), lambda t, kc: (0, t, 0))],
        out_specs=pl.BlockSpec((rb, cs), lambda t, kc: (kc[1] * nblk + t, 0)))
    return pl.pallas_call(
        body, name=name, grid_spec=grid_spec, out_shape=jax.ShapeDtypeStruct((2 * rh, cs), F32),
        compiler_params=pltpu.CompilerParams(dimension_semantics=("parallel",)),
    )(kc_idx, sums, got)


def _half_swap_carry(shards):
    n = len(shards)

    def copy(ins, outs, sems, i, to_my_half):
        x, y, c, _ = _place()
        rh = ins[i].shape[0] // 2
        half = c if to_my_half else 1 - c
        return pltpu.make_async_remote_copy(
            src_ref=ins[i].at[pl.ds(c * rh, rh)], dst_ref=outs[i].at[pl.ds(half * rh, rh)],
            send_sem=sems[0].at[i], recv_sem=sems[1].at[i], device_id=(x, y, 1 - c), device_id_type=MESH)

    def start(ins, outs, sems):
        for i in range(n):
            copy(ins, outs, sems, i, True).start()

    def finish(ins, outs, sems):
        for i in range(n):
            copy(ins, outs, sems, i, False).wait_recv()
        for i in range(n):
            copy(ins, outs, sems, i, True).wait_send()

    return _Carry(shards, [jax.ShapeDtypeStruct(a.shape, a.dtype) for a in shards], {i: i for i in range(n)},
                  [pltpu.SemaphoreType.DMA((n,)), pltpu.SemaphoreType.DMA((n,))], start, finish)


SMALL_ROW = 1024


def _small_layout(shapes):
    starts, row = [], 0
    for r, c in shapes:
        starts.append(row)
        row += -(-c // SMALL_ROW) if r == 1 else r
    return starts, -(-row // SUBLANES) * SUBLANES


def _small_pieces(shape, start):
    r, c = shape
    if r == 1:
        return [(start + q, min(SMALL_ROW, c - q * SMALL_ROW), q * SMALL_ROW) for q in range(-(-c // SMALL_ROW))]
    return None


def _small_all_reduce(parts, *, name):
    shapes = [a.shape for a in parts]
    starts, rows = _small_layout(shapes)
    n = len(parts)

    def body(*refs):
        ins, outs = refs[0:n], refs[n:2 * n]
        buf, send_sems, recv_sems = refs[2 * n:]
        x, y, c, _ = _place()
        me = 4 * x + 2 * y + c
        mine = buf.at[me]
        mine[...] = jnp.zeros((rows, SMALL_ROW), F32)
        for ref, shape, start in zip(ins, shapes, starts):
            pieces = _small_pieces(shape, start)
            if pieces is None:
                mine[start:start + shape[0], 0:shape[1]] = ref[...]
            else:
                for row, width, col in pieces:
                    mine[row:row + 1, 0:width] = ref[:, col:col + width]
        copies = []
        for d in range(1, 8):
            to = (1 - x if d & 4 else x, 1 - y if d & 2 else y, 1 - c if d & 1 else c)
            cp = pltpu.make_async_remote_copy(src_ref=mine, dst_ref=mine, send_sem=send_sems.at[d - 1],
                                              recv_sem=recv_sems.at[d - 1], device_id=to, device_id_type=MESH)
            cp.start()
            copies.append((cp, to))
        for d in range(1, 8):
            to = copies[d - 1][1]
            sender = 4 * to[0] + 2 * to[1] + to[2]
            pltpu.make_async_remote_copy(src_ref=mine, dst_ref=buf.at[sender], send_sem=send_sems.at[d - 1],
                                         recv_sem=recv_sems.at[d - 1], device_id=to, device_id_type=MESH).wait_recv()
        for cp, _ in copies:
            cp.wait_send()
        acc = buf[0]
        for s in range(1, 8):
            acc = acc + buf[s]
        mine[...] = acc
        for ref, shape, start in zip(outs, shapes, starts):
            pieces = _small_pieces(shape, start)
            if pieces is None:
                ref[...] = mine[start:start + shape[0], 0:shape[1]]
            else:
                for row, width, col in pieces:
                    ref[:, col:col + width] = mine[row:row + 1, 0:width]

    vm = pl.BlockSpec(memory_space=pltpu.VMEM)
    return pl.pallas_call(
        body, name=name, in_specs=[vm] * n, out_specs=tuple([vm] * n),
        out_shape=tuple(jax.ShapeDtypeStruct(s, F32) for s in shapes),
        scratch_shapes=[pltpu.VMEM((8, rows, SMALL_ROW), F32), pltpu.SemaphoreType.DMA((7,)),
                        pltpu.SemaphoreType.DMA((7,))],
    )(*parts)


def _adamw_small(ws, gs, ms, vs, *, name):
    n = len(ws)
    c1 = 1.0 - ADAM_B1 ** ADAM_STEP
    c2 = 1.0 - ADAM_B2 ** ADAM_STEP

    def body(*refs):
        for k in range(n):
            w_ref, g_ref, m_ref, v_ref = (refs[t * n + k] for t in range(4))
            go_ref, d_ref, mo_ref, vo_ref = (refs[(4 + t) * n + k] for t in range(4))
            gv = g_ref[...]
            go_ref[...] = gv
            mn = ADAM_B1 * m_ref[...] + (1.0 - ADAM_B1) * gv
            vn = ADAM_B2 * v_ref[...] + (1.0 - ADAM_B2) * (gv * gv)
            d_ref[...] = -ADAM_LR * ((mn / c1) / (jnp.sqrt(vn / c2) + ADAM_EPS) + ADAM_WD * w_ref[...])
            mo_ref[...] = mn
            vo_ref[...] = vn

    vm = pl.BlockSpec(memory_space=pltpu.VMEM)
    shapes = tuple(jax.ShapeDtypeStruct(a.shape, F32) for a in ws)
    res = pl.pallas_call(
        body, name=name, in_specs=[vm] * (4 * n), out_specs=tuple([vm] * (4 * n)), out_shape=shapes * 4,
    )(*ws, *gs, *ms, *vs)
    return res[0:n], res[n:2 * n], res[2 * n:3 * n], res[3 * n:4 * n]


def _adamw(w, g, m, v, *, name):
    R, C = w.shape
    rb = _row_block(R, 256, SUBLANES)
    c1 = 1.0 - ADAM_B1 ** ADAM_STEP
    c2 = 1.0 - ADAM_B2 ** ADAM_STEP

    def body(w_ref, g_ref, m_ref, v_ref, go_ref, d_ref, mo_ref, vo_ref):
        gv = g_ref[...]
        go_ref[...] = gv
        mn = ADAM_B1 * m_ref[...] + (1.0 - ADAM_B1) * gv
        vn = ADAM_B2 * v_ref[...] + (1.0 - ADAM_B2) * (gv * gv)
        d_ref[...] = -ADAM_LR * ((mn / c1) / (jnp.sqrt(vn / c2) + ADAM_EPS) + ADAM_WD * w_ref[...])
        mo_ref[...] = mn
        vo_ref[...] = vn

    spec = pl.BlockSpec((rb, C), lambda i: (i, 0))
    shp = jax.ShapeDtypeStruct((R, C), F32)
    return pl.pallas_call(
        body, name=name, grid=(R // rb,), in_specs=[spec] * 4, out_specs=(spec,) * 4,
        out_shape=(shp,) * 4, compiler_params=pltpu.CompilerParams(dimension_semantics=("parallel",)),
    )(w, g, m, v)


def _local_step(x, p, target, small, sched):
    T = x.shape[0]
    W = MIX_W
    rel_bias = small["rel_bias"]
    b_in_a, b_in_b = small["b_in"][:, 0:ZA_W], small["b_in"][:, ZA_W:]

    idx_np = [_t5_index(A_WINDOW - 1, 1)] + [_t5_index(window // dil, dil) for window, dil in B_PATTERNS]
    idx_all = jnp.asarray(np.stack(idx_np))
    n1, bias_all = sched.run(_prologue, x, small["ffn1_pre_g"], rel_bias, idx_all, name="prologue")
    w_gu1 = sched.weight("ffn1_w_gu")
    *gu1, a1 = sched.run(_ffn_up, n1, w_gu1, bm=512, name="ffn1_gu")
    w_d1 = sched.weight("ffn1_w_down")
    f1, h1, n2 = sched.run(_mm_resid_norm, a1, w_d1, x, small["ffn1_post_g"], 0.5, small["attn_pre_g"], bm=512,
                           name="ffn1_down")
    win_a, win_b = sched.weight("w_in")
    za = _mm_nt(n2, win_a, bm=1024, bn=ZA_W, out_dtype=BF16, name="in_proj_a", add=b_in_a)
    dils = tuple(dil for _, dil in B_PATTERNS)
    zb_by_dil = _mm_nt_classes(n2, win_b, b_in_b, dils, bm=512, name="in_proj_b")

    k_a = za[:, W:W + LANES].reshape(T, 2, 1, HEAD_DIM)
    v_a = za[:, W + LANES:W + 2 * LANES].reshape(T, 2, 1, HEAD_DIM)
    k_a = jnp.broadcast_to(k_a, (T, 2, 4, HEAD_DIM)).reshape(T, W)
    v_a = jnp.broadcast_to(v_a, (T, 2, 4, HEAD_DIM)).reshape(T, W)
    a_args = dict(r=1, q_col=0, k_col=0, v_col=0, stride=0, pattern=0)
    o_a, lse_a1 = sched.run(_band_fwd, za, k_a, v_a, bias_all, name="band_a_fwd", **a_args)
    sink_row = jnp.repeat(small["sinks"], HEAD_DIM, axis=1)
    out_a, out_a_bf, lse_a = _merge([(o_a, lse_a1)], (1,), sink_row, name="merge_a")

    no_sink = jnp.full((1, W), NEG, F32)
    b_ctx, branches = [], []
    for t, dil in enumerate(dils):
        zb_r = zb_by_dil[t]
        args = dict(r=dil, q_col=0, k_col=1, v_col=2, stride=ZB_W // W, pattern=1 + t)
        branches.append(sched.run(_band_fwd, zb_r, zb_r, zb_r, bias_all, name=f"band_b{t}_fwd", **args))
        b_ctx.append((jnp.asarray(idx_np[1 + t]), zb_r, args))
    out_b, out_b_bf, lse_b = _merge(branches, dils, no_sink, name="merge_b")

    mix = jnp.concatenate([out_a_bf, out_b_bf], axis=1)
    w_out = sched.weight("w_out")
    att, h2, n3 = _mm_resid_norm(mix, w_out, h1, small["attn_post_g"], 1.0, small["ffn2_pre_g"], bm=512,
                                 name="out_proj", bias=small["b_out"])
    w_gu2, w_d2 = sched.weight("ffn2_w_gu"), sched.weight("ffn2_w_down")
    *gu2, a2 = _ffn_up(n3, w_gu2, bm=512, name="ffn2_gu")
    f2, h3, n4 = _mm_resid_norm(a2, w_d2, h2, small["ffn2_post_g"], 0.5, small["ple_pre_g"], bm=512,
                                name="ffn2_down")
    w_gate = sched.weight("w_ple_gate")
    p_bf = p.astype(BF16)
    dh4, de, dgp, loss, d_ple_post = _ple_head(n4, w_gate, p_bf, sched.weight("w_ple_proj"), h3, small["ple_post_g"],
                                               target, bm=512, name="ple_head")

    gs = {"ple_post_g": d_ple_post}
    sched.grad("w_ple_proj", _mm_tn(p_bf, de, bm=256, bn=1024, out_dtype=BF16, name="d_w_ple"))
    sched.grad("w_ple_gate", _mm_tn(n4, dgp, bm=512, bn=1024, out_dtype=BF16, name="d_w_gate"))
    dh3, df2, gs["ple_pre_g"], gs["ffn2_post_g"], _ = sched.run(
        _mm_nt_norms_bwd, dgp, w_gate, dh4, h3, small["ple_pre_g"], f2, small["ffn2_post_g"], 0.5, bm=512, name="d_n4")

    def ffn_bwd(df, a, gu, n, w_down, w_gu, tag, *norm_args):
        dgu = sched.run(_ffn_down_bwd, df, w_down, gu[0], gu[1], bm=512, name=f"ffn{tag}_d_a")
        sched.grad(f"ffn{tag}_w_down", _mm_tn(a, df, bm=256, bn=1024, out_dtype=BF16, name=f"ffn{tag}_d_w_down"))
        sched.grad(f"ffn{tag}_w_gu", sched.run(_mm_tn, n, dgu, bm=512, bn=1408, out_dtype=BF16,
                                              name=f"ffn{tag}_d_w_gu", n_stack=N_CHIPS))
        return sched.run(_mm_nt_norms_bwd, dgu, w_gu, *norm_args, bm=512, name=f"ffn{tag}_d_n")

    dh2, datt, gs["ffn2_pre_g"], gs["attn_post_g"], gs["b_out"] = ffn_bwd(
        df2, a2, gu2, n3, w_d2, w_gu2, "2", dh3, h2, small["ffn2_pre_g"], att, small["attn_post_g"], 1.0)
    sched.grad("w_out", _mm_tn(mix, datt, bm=512, bn=1024, out_dtype=BF16, name="d_w_out"))
    dmix = sched.run(_mm_nt, datt, w_out, bm=1024, bn=1024, out_dtype=F32, name="d_mix")

    do_a, stat_a, dsink, do_b, stat_b = _attn_delta(dmix, (out_a, out_b), (lse_a, lse_b), sink_row, dils,
                                                    name="attn_delta")
    gs["sinks"] = dsink[:, ::HEAD_DIM]
    dq_a, dk_a, dv_a, dbias_a = sched.run(_band_bwd, za, k_a, v_a, do_a, stat_a, bias_all, name="band_a_bwd", **a_args)
    d_rel_a = _bias_grad(dbias_a, jnp.asarray(idx_np[0]), name="d_rel_a")
    b_grads = []
    d_rel_b = None
    for t, (idx, zb_r, args) in enumerate(b_ctx):
        dq, dk, dv, dbias = sched.run(_band_bwd, zb_r, zb_r, zb_r, do_b[t], stat_b[t], bias_all,
                                      name=f"band_b{t}_bwd", **args)
        b_grads.append((dq, dk, dv))
        d_rel = _bias_grad(dbias, idx, name=f"d_rel_b{t}")
        d_rel_b = d_rel if d_rel_b is None else d_rel_b + d_rel
    gs["rel_bias"] = jnp.concatenate([d_rel_a[:, 0:2 * N_PAIRS], d_rel_b[:, 0:2 * N_PAIRS]], axis=1)
    dza, dzb, dsum_a, dsum_b = _dz_assemble(dq_a, dk_a, dv_a, b_grads, dils, name="d_z")
    gs["b_in"] = jnp.concatenate([dsum_a, dsum_b], axis=1)
    sched.grad("w_in", (_mm_tn(dza, n2, bm=ZA_W, bn=1024, out_dtype=BF16, name="d_w_in_a"),
                        _mm_tn(dzb, n2, bm=ZB_W // 2, bn=1024, out_dtype=BF16, name="d_w_in_b")))
    dn2_a = _mm_nn(dza, win_a, bm=1024, bn=1024, out_dtype=F32, name="d_n2_a")
    dh1, df1, gs["attn_pre_g"], gs["ffn1_post_g"], _ = sched.run(
        _mm_nt_norms_bwd, dzb, win_b, dh2, h1, small["attn_pre_g"], f1, small["ffn1_post_g"], 0.5, bm=512,
        name="d_n2_b", add=dn2_a, b_is_kn=True)
    grad_x, gs["ffn1_pre_g"] = ffn_bwd(df1, a1, gu1, n1, w_d1, w_gu1, "1", dh1, x, small["ffn1_pre_g"], None, None, 0.0)
    return loss, grad_x, gs


def _to_compute(name, full):
    st = full.reshape(N_CHIPS, full.shape[0] // N_CHIPS, full.shape[1])
    if name in ("ffn1_w_gu", "ffn2_w_gu"):
        return st
    if name == "w_ple_proj":
        return jnp.transpose(st, (1, 0, 2)).reshape(st.shape[1], -1)
    if name == "w_in":
        return full[0:ZA_W], full[ZA_W:]
    return full


def _to_comm(name, g):
    if name == "w_in":
        g = jnp.concatenate(g, axis=0)
    if name == "w_ple_proj":
        g = jnp.transpose(g.reshape(g.shape[0], N_CHIPS, -1), (1, 0, 2))
    rs = g.shape[1] if g.ndim == 3 else g.shape[0] // N_CHIPS
    return g.reshape(N_CHIPS, 2, rs // 2, g.shape[-1])


class _Schedule:
    GATHER = {
        "prologue": [("ffn1_w_gu", (0, 1))],
        "ffn1_gu": [("ffn1_w_down", (0, 1)), ("w_in", (0, 1))],
        "ffn1_down": [("w_out", (0, 1))],
        "band_a_fwd": [("ffn2_w_gu", (0, 2))],
        "band_b0_fwd": [("ffn2_w_gu", (1, 2))],
        "band_b1_fwd": [("ffn2_w_down", (0, 1))],
        "band_b2_fwd": [("w_ple_gate", (0, 1)), ("w_ple_proj", (0, 1))],
    }
    SWAP = {"d_n4": ["w_ple_proj", "w_ple_gate"], "ffn2_d_w_gu": ["ffn2_w_down"], "ffn2_d_n": ["ffn2_w_gu"],
            "d_mix": ["w_out"], "d_n2_b": ["w_in"], "ffn1_d_w_gu": ["ffn1_w_down"], "ffn1_d_n": ["ffn1_w_gu"]}
    SCATTER = {"ffn2_d_a": ["w_ple_proj", "w_ple_gate"], "ffn2_d_n": ["ffn2_w_down"], "band_a_bwd": ["ffn2_w_gu"],
               "band_b0_bwd": ["w_out"], "ffn1_d_a": ["w_in"], "ffn1_d_n": ["ffn1_w_down"]}

    def __init__(self, placed, c_idx, kc_idx):
        self.full = dict(placed)
        self.missing = {n: 1.0 for n in placed}
        self.c_idx, self.kc_idx = c_idx, kc_idx
        self.grads, self.swapped, self.pair_sums, self.scattered = {}, {}, {}, {}

    def _gather(self, entries):
        carries, after = [], []
        for part in sorted({pt for _, pt in entries}):
            names = [n for n, pt in entries if pt == part]
            carry = _gather_carry([self.full[n] for n in names], part)
            carries.append(carry)

            def done(carry=carry, names=names, part=part):
                for n, a in zip(names, carry.results):
                    self.full[n] = a
                    self.missing[n] -= 1.0 / part[1]
            after.append(done)
        return carries, after

    def weight(self, name):
        assert abs(self.missing[name]) < 1e-9, (name, self.missing[name])
        return _to_compute(name, self.full[name])

    def grad(self, name, g):
        self.grads[name] = _to_comm(name, g)

    def _swap(self, names):
        carry = _pair_swap_carry([self.grads[n] for n in names])

        def done():
            self.swapped.update(zip(names, carry.results))
        return carry, done

    def _scatter(self, names):
        for n in names:
            self.pair_sums[n] = _pair_sum(self.grads[n], self.swapped[n], self.c_idx, name=f"grad_pair_sum_{n}")
        carry = _chip_scatter_carry([self.pair_sums[n] for n in names])

        def done():
            self.scattered.update(zip(names, carry.results))
        return carry, done

    def run(self, fn, *args, name, **kw):
        carries, after = self._gather(self.GATHER.get(name, []))
        for names, make in ((self.SWAP.get(name), self._swap), (self.SCATTER.get(name), self._scatter)):
            if names:
                carry, done = make(names)
                carries.append(carry)
                after.append(done)
        out = fn(*args, name=name, carry=_join(carries), **kw)
        for done in after:
            done()
        return out

    def finish(self):
        left = [n for n in BIG if n not in self.scattered]
        to_swap = [n for n in left if n not in self.swapped]
        if to_swap:
            carry, done = self._swap(to_swap)
            _comm_call(carry, name="grad_pair_swap")
            done()
        if left:
            carry, done = self._scatter(left)
            _comm_call(carry, name="grad_chip_scatter")
            done()
        halves = [_chip_sum(self.pair_sums[n], self.scattered[n], self.kc_idx, name=f"grad_chip_sum_{n}") for n in BIG]
        carry = _half_swap_carry(halves)
        _comm_call(carry, name="grad_half_swap")
        return dict(zip(BIG, carry.results))


def kernel(x, p, rel_bias, ffn1_pre_g, ffn1_w_gu, ffn1_w_down, ffn1_post_g, attn_pre_g, w_in, b_in, sinks, w_out, b_out, attn_post_g, ffn2_pre_g, ffn2_w_gu, ffn2_w_down, ffn2_post_g, ple_pre_g, w_ple_gate, w_ple_proj, ple_post_g, loss_target, m_rel_bias, m_ffn1_pre_g, m_ffn1_w_gu, m_ffn1_w_down, m_ffn1_post_g, m_attn_pre_g, m_w_in, m_b_in, m_sinks, m_w_out, m_b_out, m_attn_post_g, m_ffn2_pre_g, m_ffn2_w_gu, m_ffn2_w_down, m_ffn2_post_g, m_ple_pre_g, m_w_ple_gate, m_w_ple_proj, m_ple_post_g, v_rel_bias, v_ffn1_pre_g, v_ffn1_w_gu, v_ffn1_w_down, v_ffn1_post_g, v_attn_pre_g, v_w_in, v_b_in, v_sinks, v_w_out, v_b_out, v_attn_post_g, v_ffn2_pre_g, v_ffn2_w_gu, v_ffn2_w_down, v_ffn2_post_g, v_ple_pre_g, v_w_ple_gate, v_w_ple_proj, v_ple_post_g):
    given = dict(locals())
    w = {n: given[n] for n in WEIGHTS}
    m = {n: given["m_" + n] for n in WEIGHTS}
    v = {n: given["v_" + n] for n in WEIGHTS}
    c_pos = lax.axis_index("c").astype(jnp.int32)
    k_pos = (2 * lax.axis_index("x") + lax.axis_index("y")).astype(jnp.int32)
    c_idx, k_idx, kc_idx = c_pos.reshape(1), k_pos.reshape(1), jnp.stack([k_pos, c_pos])

    def shard(a, n):
        return jnp.transpose(a[0]) if n == "w_in" else a[0]

    def unshard(a, n):
        return (jnp.transpose(a) if n == "w_in" else a)[None]

    sched = _Schedule({n: _cast_place(shard(w[n], n), k_idx, name=f"place_{n}") for n in BIG}, c_idx, kc_idx)
    small = {n: (w[n] if n == "rel_bias" else w[n].reshape(1, -1)) for n in SMALL}

    loss_part, grad_x, gs = _local_step(x[0], p[0, 0], loss_target[0], small, sched)

    grads_big = sched.finish()

    *small_grads, loss_row = _small_all_reduce([gs[n] for n in SMALL] + [loss_part], name="small_all_reduce")
    loss = loss_row[0, 0]

    grads, delta, new_m, new_v = {}, {}, {}, {}
    for n in BIG:
        res = _adamw(shard(w[n], n), grads_big[n], shard(m[n], n), shard(v[n], n), name=f"adamw_{n}")
        grads[n], delta[n], new_m[n], new_v[n] = (unshard(a, n) for a in res)
    res = _adamw_small([w[n] for n in SMALL], small_grads, [m[n] for n in SMALL], [v[n] for n in SMALL],
                       name="adamw_small")
    for name, gg, dd, mm, vv in zip(SMALL, *res):
        grads[name], delta[name], new_m[name], new_v[name] = gg, dd, mm, vv

    return (loss, grad_x[None], *[grads[n] for n in WEIGHTS], *[delta[n] for n in WEIGHTS],
            *[new_m[n] for n in WEIGHTS], *[new_v[n] for n in WEIGHTS])
```

```python
import math

import jax
import jax.numpy as jnp
import numpy as np
from jax import lax
from jax.experimental import pallas as pl
from jax.experimental.pallas import tpu as pltpu

F32 = jnp.float32
BF16 = jnp.bfloat16
MESH = pl.DeviceIdType.MESH

EPS = 1e-6
NEG = -1e30
LANES = 128
SUBLANES = 8
HEAD_DIM = 64
QBLK = 128
N_PAIRS = 4
MIX_W = N_PAIRS * LANES
A_WINDOW = 128
B_PATTERNS = ((128, 1), (512, 4), (2048, 16))
NUM_BUCKETS = 32
MAX_DISTANCE = 2048
ZA_W = 768
ZB_W = 1536
N_CHIPS = 4
VMEM_BYTES_V7X = 64 * 2**20

ADAM_LR, ADAM_B1, ADAM_B2, ADAM_EPS, ADAM_WD, ADAM_STEP = 0.001, 0.9, 0.999, 1e-08, 0.01, 10

BIG = ("ffn1_w_gu", "ffn1_w_down", "w_in", "w_out", "ffn2_w_gu", "ffn2_w_down", "w_ple_gate", "w_ple_proj")
SMALL = ("rel_bias", "ffn1_pre_g", "ffn1_post_g", "attn_pre_g", "b_in", "sinks", "b_out", "attn_post_g",
         "ffn2_pre_g", "ffn2_post_g", "ple_pre_g", "ple_post_g")
WEIGHTS = ("rel_bias", "ffn1_pre_g", "ffn1_w_gu", "ffn1_w_down", "ffn1_post_g", "attn_pre_g", "w_in", "b_in",
           "sinks", "w_out", "b_out", "attn_post_g", "ffn2_pre_g", "ffn2_w_gu", "ffn2_w_down", "ffn2_post_g",
           "ple_pre_g", "w_ple_gate", "w_ple_proj", "ple_post_g")


def _vmem_limit(*block_bytes):
    need = 2 * sum(block_bytes) + max(block_bytes) * 2 + (4 << 20)
    return int(min(max(need, 32 << 20), VMEM_BYTES_V7X - (6 << 20)))


def _nbytes(shape, dtype):
    return int(np.prod(shape)) * jnp.dtype(dtype).itemsize


MXU_WIDTH_V7X = 256


def _col_chunks(n):
    return [slice(c, min(c + MXU_WIDTH_V7X, n)) for c in range(0, n, MXU_WIDTH_V7X)]


def _row_halves(rows):
    return [slice(0, rows // 2), slice(rows // 2, rows)]


def _row_block(rows, cap, mult):
    for cand in range(min(rows, cap), 0, -1):
        if rows % cand == 0 and cand % mult == 0:
            return cand
    raise ValueError((rows, cap, mult))


class _Carry:
    def __init__(self, operands, out_shapes, aliases, sems, start, finish):
        self.operands, self.out_shapes, self.aliases, self.sems = list(operands), list(out_shapes), dict(aliases), list(sems)
        self.start, self.finish = start, finish
        self.results = None


def _join(carries):
    carries = [c for c in carries if c is not None]
    if not carries:
        return None
    if len(carries) == 1:
        return carries[0]
    offs, pos = [], [0, 0, 0]
    for c in carries:
        offs.append(tuple(pos))
        pos = [pos[0] + len(c.operands), pos[1] + len(c.out_shapes), pos[2] + len(c.sems)]
    aliases = {}
    for c, (oi, oo, _) in zip(carries, offs):
        aliases.update({oi + i: oo + o for i, o in c.aliases.items()})

    def run(which):
        def fn(ins, outs, sems):
            for c, (oi, oo, os_) in zip(carries, offs):
                getattr(c, which)(ins[oi:oi + len(c.operands)], outs[oo:oo + len(c.out_shapes)],
                                  sems[os_:os_ + len(c.sems)])
        return fn

    joined = _Carry([a for c in carries for a in c.operands], [s for c in carries for s in c.out_shapes], aliases,
                    [s for c in carries for s in c.sems], run("start"), run("finish"))
    joined.parts = (carries, offs)
    return joined


def _set_results(carry, outs):
    carry.results = list(outs)
    if hasattr(carry, "parts"):
        for c, (_, oo, _) in zip(*carry.parts):
            _set_results(c, outs[oo:oo + len(c.out_shapes)])


ANY = pl.BlockSpec(memory_space=pl.ANY)


def _call(body, *, name, grid, in_specs, out_specs, out_shape, args, scratch=(), vmem_limit=None, carry=None,
          semantics=None):
    n_ci, n_co, n_cs = len(args), len(out_shape), len(scratch)
    semantics = semantics or ("parallel",) * len(grid)
    if carry is None:
        res = pl.pallas_call(
            body, name=name, grid=grid, in_specs=list(in_specs), out_specs=tuple(out_specs), out_shape=tuple(out_shape),
            scratch_shapes=list(scratch),
            compiler_params=pltpu.CompilerParams(dimension_semantics=semantics, vmem_limit_bytes=vmem_limit),
        )(*args)
        return list(res)
    n_pi, n_po = len(carry.operands), len(carry.out_shapes)

    def full_body(*refs):
        ci, pi = refs[:n_ci], refs[n_ci:n_ci + n_pi]
        o0 = n_ci + n_pi
        co, po = refs[o0:o0 + n_co], refs[o0 + n_co:o0 + n_co + n_po]
        s0 = o0 + n_co + n_po
        cs, ps = refs[s0:s0 + n_cs], refs[s0 + n_cs:]
        ids = [pl.program_id(ax) for ax in range(len(grid))]
        first, last = ids[0] == 0, ids[0] == grid[0] - 1
        for ax in range(1, len(grid)):
            first, last = first & (ids[ax] == 0), last & (ids[ax] == grid[ax] - 1)

        @pl.when(first)
        def _():
            carry.start(pi, po, ps)

        body(*ci, *co, *cs)

        @pl.when(last)
        def _():
            carry.finish(pi, po, ps)

    res = pl.pallas_call(
        full_body, name=name, grid=grid, in_specs=list(in_specs) + [ANY] * n_pi,
        out_specs=tuple(out_specs) + tuple([ANY] * n_po), out_shape=tuple(out_shape) + tuple(carry.out_shapes),
        scratch_shapes=list(scratch) + carry.sems,
        input_output_aliases={n_ci + i: n_co + o for i, o in carry.aliases.items()},
        compiler_params=pltpu.CompilerParams(dimension_semantics=("arbitrary",) * len(grid),
                                             vmem_limit_bytes=vmem_limit),
    )(*args, *carry.operands)
    _set_results(carry, res[n_co:])
    return list(res[:n_co])


def _comm_call(carry, *, name):
    n_pi, n_po = len(carry.operands), len(carry.out_shapes)

    def body(*refs):
        pi, po, ps = refs[:n_pi], refs[n_pi:n_pi + n_po], refs[n_pi + n_po:]
        carry.start(pi, po, ps)
        carry.finish(pi, po, ps)

    res = pl.pallas_call(
        body, name=name, in_specs=[ANY] * n_pi, out_specs=tuple([ANY] * n_po), out_shape=tuple(carry.out_shapes),
        scratch_shapes=carry.sems, input_output_aliases=dict(carry.aliases),
    )(*carry.operands)
    _set_results(carry, res)


def _mm_nn(a, b, *, bm, bn, out_dtype, name, bias=None, carry=None):
    M, K = a.shape
    stacked = b.ndim == 3
    N = b.shape[0] * b.shape[2] if stacked else b.shape[1]
    assert M % bm == 0 and N % bn == 0 and (not stacked or bn == b.shape[2])

    def body(*refs):
        a_ref, b_ref = refs[0], refs[1]
        o_ref = refs[-1]
        acc = jnp.dot(a_ref[...], b_ref[...], preferred_element_type=F32)
        if bias is not None:
            acc = acc + refs[2][...]
        o_ref[...] = acc.astype(o_ref.dtype)

    if stacked:
        b_spec = pl.BlockSpec((None, K, bn), lambda i, j: (j, 0, 0))
    else:
        b_spec = pl.BlockSpec((K, bn), lambda i, j: (0, j))
    in_specs = [pl.BlockSpec((bm, K), lambda i, j: (i, 0)), b_spec]
    args = [a, b]
    if bias is not None:
        in_specs.append(pl.BlockSpec((1, bn), lambda i, j: (0, j)))
        args.append(bias)
    limit = _vmem_limit(_nbytes((bm, K), a.dtype), _nbytes((K, bn), b.dtype), _nbytes((bm, bn), F32))
    return _call(body, name=name, grid=(M // bm, N // bn), in_specs=in_specs,
                 out_specs=[pl.BlockSpec((bm, bn), lambda i, j: (i, j))],
                 out_shape=[jax.ShapeDtypeStruct((M, N), out_dtype)], args=args, vmem_limit=limit, carry=carry)[0]


def _mm_nt(a, b, *, bm, bn, out_dtype, name, add=None, carry=None):
    M, K = a.shape
    stacked = b.ndim == 3
    N = b.shape[1] if stacked else b.shape[0]
    n_sh = b.shape[0] if stacked else 1
    ks = b.shape[2] if stacked else K
    assert M % bm == 0 and N % bn == 0 and n_sh * ks == K
    dn = (((1,), (1,)), ((), ()))

    def body(*refs):
        a_ref, b_ref = refs[0], refs[1]
        o_ref = refs[-1]
        if stacked:
            acc = lax.dot_general(a_ref[:, 0:ks], b_ref[0], dn, preferred_element_type=F32)
            for s in range(1, n_sh):
                acc = acc + lax.dot_general(a_ref[:, s * ks:(s + 1) * ks], b_ref[s], dn, preferred_element_type=F32)
        else:
            acc = lax.dot_general(a_ref[...], b_ref[...], dn, preferred_element_type=F32)
        if add is not None:
            acc = acc + refs[2][...]
        o_ref[...] = acc.astype(o_ref.dtype)

    if stacked:
        b_spec = pl.BlockSpec((n_sh, bn, ks), lambda i, j: (0, j, 0))
    else:
        b_spec = pl.BlockSpec((bn, K), lambda i, j: (j, 0))
    in_specs = [pl.BlockSpec((bm, K), lambda i, j: (i, 0)), b_spec]
    args = [a, b]
    if add is not None:
        rows = bm if add.shape[0] == M else 1
        in_specs.append(pl.BlockSpec((rows, bn), lambda i, j: (i if rows == bm else 0, j)))
        args.append(add)
    limit = _vmem_limit(_nbytes((bm, K), a.dtype), _nbytes((bn, K), b.dtype), _nbytes((bm, bn), F32))
    return _call(body, name=name, grid=(M // bm, N // bn), in_specs=in_specs,
                 out_specs=[pl.BlockSpec((bm, bn), lambda i, j: (i, j))],
                 out_shape=[jax.ShapeDtypeStruct((M, N), out_dtype)], args=args, vmem_limit=limit, carry=carry)[0]


def _mm_tn(a, b, *, bm, bn, out_dtype, name, n_stack=None, carry=None):
    T, M = a.shape
    N = b.shape[1]
    assert M % bm == 0 and N % bn == 0 and (n_stack is None or bn * n_stack == N)
    dn = (((0,), (0,)), ((), ()))

    def body(a_ref, b_ref, o_ref):
        acc = lax.dot_general(a_ref[...], b_ref[...], dn, preferred_element_type=F32)
        o_ref[...] = acc.astype(o_ref.dtype)

    if n_stack is None:
        out_spec = pl.BlockSpec((bm, bn), lambda i, j: (i, j))
        out_shape = jax.ShapeDtypeStruct((M, N), out_dtype)
    else:
        out_spec = pl.BlockSpec((None, bm, bn), lambda i, j: (j, i, 0))
        out_shape = jax.ShapeDtypeStruct((n_stack, M, bn), out_dtype)
    limit = _vmem_limit(_nbytes((T, bm), a.dtype), _nbytes((T, bn), b.dtype), _nbytes((bm, bn), F32))
    return _call(body, name=name, grid=(M // bm, N // bn),
                 in_specs=[pl.BlockSpec((T, bm), lambda i, j: (0, i)), pl.BlockSpec((T, bn), lambda i, j: (0, j))],
                 out_specs=[out_spec], out_shape=[out_shape], args=[a, b], vmem_limit=limit, carry=carry)[0]


ROW_BLK = 256


def _rstd(x):
    return lax.rsqrt(jnp.mean(x * x, axis=-1, keepdims=True) + EPS)


def _norm_bwd(xhat, rstd, dxhat):
    return rstd * (dxhat - xhat * jnp.mean(dxhat * xhat, axis=-1, keepdims=True))


def _row_spec(cols):
    return pl.BlockSpec((ROW_BLK, cols), lambda i: (i, 0))


def _vec_spec(cols):
    return pl.BlockSpec((1, cols), lambda i: (0, 0))


def _prologue(x, g, rel_bias, idx_all, *, name, carry=None):
    T, D = x.shape
    n_pat = idx_all.shape[0]
    heads = 2 * N_PAIRS
    steps = n_pat * heads
    rows = T // steps

    def body(rb_ref, x_ref, g_ref, idx_ref, n_ref, bias_ref):
        s = pl.program_id(0)
        head = jnp.where(s < heads, s, heads + s % heads)
        xv = x_ref[...]
        n_ref[...] = (xv * _rstd(xv) * g_ref[...]).astype(n_ref.dtype)
        idxv = idx_ref[...]
        acc = jnp.where(idxv < 0, NEG, 0.0).astype(F32)
        for b in range(NUM_BUCKETS):
            acc = acc + jnp.where(idxv == b, rb_ref[b, head], 0.0)
        bias_ref[0] = acc
        kap = lax.broadcasted_iota(jnp.int32, (1, 2 * QBLK), 1)
        bias_ref[1] = jnp.where(kap < QBLK, NEG, acc)

    return _call(
        body, name=name, grid=(steps,),
        in_specs=[pl.BlockSpec(memory_space=pltpu.SMEM), pl.BlockSpec((rows, D), lambda s: (s, 0)),
                  pl.BlockSpec((1, D), lambda s: (0, 0)),
                  pl.BlockSpec((None, QBLK, 2 * QBLK), lambda s: (s // heads, 0, 0))],
        out_specs=[pl.BlockSpec((rows, D), lambda s: (s, 0)),
                   pl.BlockSpec((None, 2, None, QBLK, 2 * QBLK), lambda s: (s // heads, 0, s % heads, 0, 0))],
        out_shape=[jax.ShapeDtypeStruct((T, D), BF16),
                   jax.ShapeDtypeStruct((n_pat, 2, heads, QBLK, 2 * QBLK), F32)],
        args=[rel_bias, x, g, idx_all], carry=carry)


def _mm_resid_norm(a, b, h, g_post, coef, g_next, *, bm, name, bias=None, carry=None):
    M, K = a.shape
    N = b.shape[1]

    def body(*refs):
        a_ref, b_ref, h_ref, gp_ref, gn_ref = refs[0:5]
        f_ref, hn_ref, n_ref = refs[-3:]
        for rows in _row_halves(bm):
            f = jnp.dot(a_ref[rows, :], b_ref[...], preferred_element_type=F32)
            if bias is not None:
                f = f + refs[5][...]
            f_ref[rows, :] = f
            hn = h_ref[rows, :] + coef * (f * _rstd(f) * gp_ref[...])
            hn_ref[rows, :] = hn
            n_ref[rows, :] = (hn * _rstd(hn) * gn_ref[...]).astype(n_ref.dtype)

    row = lambda w: pl.BlockSpec((bm, w), lambda i: (i, 0))
    vec = pl.BlockSpec((1, N), lambda i: (0, 0))
    in_specs = [row(K), pl.BlockSpec((K, N), lambda i: (0, 0), pipeline_mode=pl.Buffered(1)), row(N), vec, vec]
    args = [a, b, h, g_post, g_next]
    if bias is not None:
        in_specs.append(vec)
        args.append(bias)
    limit = _vmem_limit(_nbytes((bm, K), a.dtype), _nbytes((K, N), b.dtype) // 2, 4 * _nbytes((bm, N), F32))
    return _call(body, name=name, grid=(M // bm,), in_specs=in_specs, out_specs=[row(N), row(N), row(N)],
                 out_shape=[jax.ShapeDtypeStruct((M, N), F32), jax.ShapeDtypeStruct((M, N), F32),
                            jax.ShapeDtypeStruct((M, N), BF16)], args=args, vmem_limit=limit, carry=carry)


def _mm_nt_norms_bwd(a, b, dh_in, h, g_pre, f, g_post, coef, *, bm, name, add=None, b_is_kn=False, carry=None):
    M, K = a.shape
    stacked = b.ndim == 3
    N = b.shape[1] if (stacked or b_is_kn) else b.shape[0]
    n_sh = b.shape[0] if stacked else 1
    ks = b.shape[2] if stacked else K
    assert n_sh * ks == K
    dn_dims = (((1,), (0,)), ((), ())) if b_is_kn else (((1,), (1,)), ((), ()))
    with_f = f is not None
    n_in = 5 + (2 if with_f else 0) + (1 if add is not None else 0)

    def body(*refs):
        a_ref, b_ref, dhi_ref, h_ref, gpre_ref = refs[0:5]
        outs = refs[n_in:]

        @pl.when(pl.program_id(0) == 0)
        def _():
            for acc in (outs[2:] if with_f else outs[1:]):
                acc[...] = jnp.zeros_like(acc)

        for rows in _row_halves(bm):
            if stacked:
                dn = lax.dot_general(a_ref[rows, 0:ks], b_ref[0], dn_dims, preferred_element_type=F32)
                for s in range(1, n_sh):
                    dn = dn + lax.dot_general(a_ref[rows, s * ks:(s + 1) * ks], b_ref[s], dn_dims,
                                              preferred_element_type=F32)
            else:
                dn = lax.dot_general(a_ref[rows, :], b_ref[...], dn_dims, preferred_element_type=F32)
            if add is not None:
                dn = dn + refs[n_in - 1][rows, :]
            hv = h_ref[rows, :]
            r = _rstd(hv)
            hh = hv * r
            dh = dhi_ref[rows, :] + _norm_bwd(hh, r, dn * gpre_ref[...])
            outs[0][rows, :] = dh
            if not with_f:
                outs[1][...] += jnp.sum(dn * hh, axis=0, keepdims=True)
                continue
            f_ref, gpost_ref = refs[5], refs[6]
            df_ref, dgpre_ref, dgpost_ref, dsum_ref = outs[1:]
            dgpre_ref[...] += jnp.sum(dn * hh, axis=0, keepdims=True)
            fv = f_ref[rows, :]
            rf = _rstd(fv)
            fh = fv * rf
            dy = coef * dh
            dgpost_ref[...] += jnp.sum(dy * fh, axis=0, keepdims=True)
            df = _norm_bwd(fh, rf, dy * gpost_ref[...])
            dsum_ref[...] += jnp.sum(df, axis=0, keepdims=True)
            df_ref[rows, :] = df.astype(df_ref.dtype)

    row = lambda w: pl.BlockSpec((bm, w), lambda i: (i, 0))
    vec = pl.BlockSpec((1, N), lambda i: (0, 0))
    once = pl.Buffered(1)
    if stacked:
        b_spec = pl.BlockSpec((n_sh, N, ks), lambda i: (0, 0, 0), pipeline_mode=once)
    else:
        b_spec = pl.BlockSpec(b.shape, lambda i: (0, 0), pipeline_mode=once)
    in_specs = [row(K), b_spec, row(N), row(N), vec]
    args = [a, b, dh_in, h, g_pre]
    if with_f:
        in_specs += [row(N), vec]
        args += [f, g_post]
    if add is not None:
        in_specs.append(row(N))
        args.append(add)
    vec_shape = jax.ShapeDtypeStruct((1, N), F32)
    out_specs = [row(N)] + ([row(N), vec, vec, vec] if with_f else [vec])
    out_shape = [jax.ShapeDtypeStruct((M, N), F32)]
    out_shape += [jax.ShapeDtypeStruct((M, N), BF16), vec_shape, vec_shape, vec_shape] if with_f else [vec_shape]
    limit = _vmem_limit(_nbytes((bm, K), a.dtype), _nbytes((N, K), b.dtype) // 2, 6 * _nbytes((bm, N), F32))
    return _call(body, name=name, grid=(M // bm,), in_specs=in_specs, out_specs=out_specs, out_shape=out_shape,
                 args=args, vmem_limit=limit, semantics=("arbitrary",), carry=carry)


def _ffn_up(n, w_gu, *, bm, name, carry=None):
    M, K = n.shape
    n_sh, _, ns = w_gu.shape
    half = n_sh // 2

    def body(n_ref, wg_ref, wu_ref, g_ref, u_ref, a_ref):
        nv = n_ref[...]
        for cols in _col_chunks(ns):
            g = jnp.dot(nv, wg_ref[:, cols], preferred_element_type=F32)
            u = jnp.dot(nv, wu_ref[:, cols], preferred_element_type=F32)
            g_ref[:, cols] = g.astype(g_ref.dtype)
            u_ref[:, cols] = u.astype(u_ref.dtype)
            a_ref[:, cols] = (g * jax.nn.sigmoid(g) * u).astype(a_ref.dtype)

    out_spec = pl.BlockSpec((bm, ns), lambda i, j: (i, j))
    out = jax.ShapeDtypeStruct((M, half * ns), BF16)
    limit = _vmem_limit(_nbytes((bm, K), n.dtype), 2 * _nbytes((K, ns), w_gu.dtype), 3 * _nbytes((bm, ns), F32))
    return _call(body, name=name, grid=(M // bm, half),
                 in_specs=[pl.BlockSpec((bm, K), lambda i, j: (i, 0)),
                           pl.BlockSpec((None, K, ns), lambda i, j: (j, 0, 0)),
                           pl.BlockSpec((None, K, ns), lambda i, j: (j + half, 0, 0))],
                 out_specs=[out_spec, out_spec, out_spec], out_shape=[out, out, out], args=[n, w_gu, w_gu],
                 vmem_limit=limit, carry=carry)


def _ffn_down_bwd(df, w_down, g, u, *, bm, name, carry=None):
    M, D = df.shape
    F = w_down.shape[0]
    dn = (((1,), (1,)), ((), ()))

    def body(df_ref, w_ref, g_ref, u_ref, o_ref):
        dfv = df_ref[...]
        for cols in _col_chunks(F):
            da = lax.dot_general(dfv, w_ref[cols, :], dn, preferred_element_type=F32)
            gv = g_ref[:, cols].astype(F32)
            s = jax.nn.sigmoid(gv)
            o_ref[:, cols] = (da * u_ref[:, cols].astype(F32) * (s * (1.0 + gv * (1.0 - s)))).astype(o_ref.dtype)
            o_ref[:, F + cols.start:F + cols.stop] = (da * (gv * s)).astype(o_ref.dtype)

    row = lambda w: pl.BlockSpec((bm, w), lambda i: (i, 0))
    limit = _vmem_limit(_nbytes((F, D), w_down.dtype) // 2, 2 * _nbytes((bm, F), BF16), _nbytes((bm, 2 * F), BF16))
    return _call(body, name=name, grid=(M // bm,),
                 in_specs=[row(D), pl.BlockSpec((F, D), lambda i: (0, 0), pipeline_mode=pl.Buffered(1)), row(F), row(F)],
                 out_specs=[row(2 * F)], out_shape=[jax.ShapeDtypeStruct((M, 2 * F), BF16)],
                 args=[df, w_down, g, u], vmem_limit=limit, carry=carry)[0]


def _ple_head(n4, w_gate, p, w_ple, h3, g_post, target, *, bm, name):
    T, D = h3.shape
    kp = p.shape[1]

    def body(n_ref, wg_ref, p_ref, wp_ref, h_ref, g_ref, t_ref, dh_ref, de_ref, dgp_ref, loss_ref, dg_ref):
        @pl.when(pl.program_id(0) == 0)
        def _():
            loss_ref[...] = jnp.zeros_like(loss_ref)
            dg_ref[...] = jnp.zeros_like(dg_ref)

        gpost = g_ref[...]
        for rows in _row_halves(bm):
            gate = jax.nn.sigmoid(jnp.dot(n_ref[rows, :], wg_ref[...], preferred_element_type=F32))
            ev = jnp.dot(p_ref[rows, :], wp_ref[...], preferred_element_type=F32)
            ge = gate * ev
            r = _rstd(ge)
            gh = ge * r
            diff = h_ref[rows, :] + gh * gpost - t_ref[rows, :]
            loss_ref[...] += jnp.sum(diff * diff) * (0.5 / D)
            dh = diff * (1.0 / D)
            dh_ref[rows, :] = dh
            dg_ref[...] += jnp.sum(dh * gh, axis=0, keepdims=True)
            dge = _norm_bwd(gh, r, dh * gpost)
            de_ref[rows, :] = (dge * gate).astype(de_ref.dtype)
            dgp_ref[rows, :] = (dge * ev * gate * (1.0 - gate)).astype(dgp_ref.dtype)

    row = lambda w: pl.BlockSpec((bm, w), lambda i: (i, 0))
    whole = lambda a: pl.BlockSpec(a.shape, lambda i: (0, 0), pipeline_mode=pl.Buffered(1))
    limit = _vmem_limit(_nbytes((bm, D), BF16), _nbytes(w_gate.shape, w_gate.dtype) // 2, 6 * _nbytes((bm, D), F32))
    return pl.pallas_call(
        body, name=name, grid=(T // bm,),
        in_specs=[row(D), whole(w_gate), row(kp), whole(w_ple), row(D), _vec_spec(D), row(D)],
        out_specs=(row(D), row(D), row(D), _vec_spec(LANES), _vec_spec(D)),
        out_shape=(jax.ShapeDtypeStruct((T, D), F32), jax.ShapeDtypeStruct((T, D), BF16),
                   jax.ShapeDtypeStruct((T, D), BF16), jax.ShapeDtypeStruct((1, LANES), F32),
                   jax.ShapeDtypeStruct((1, D), F32)),
        compiler_params=pltpu.CompilerParams(dimension_semantics=("arbitrary",), vmem_limit_bytes=limit),
    )(n4, w_gate, p, w_ple, h3, g_post, target)


def _t5_index(max_dist, stride):
    rho = np.arange(QBLK)[:, None]
    kap = np.arange(2 * QBLK)[None, :]
    d = rho + QBLK - kap
    valid = (d >= 0) & (d <= max_dist)
    n = np.maximum(d, 0) * stride
    max_exact = NUM_BUCKETS // 2
    nf = np.maximum(n, 1).astype(np.float32)
    large = max_exact + (np.log(nf / np.float32(max_exact)) / np.float32(math.log(MAX_DISTANCE / max_exact))
                         * np.float32(NUM_BUCKETS - max_exact)).astype(np.int32)
    large = np.minimum(large, NUM_BUCKETS - 1)
    bucket = np.where(n < max_exact, n, large)
    return np.where(valid, bucket, -1).astype(np.int32)


def _bias_grad(d_bias, idx, *, name):
    def body(db_ref, idx_ref, o_ref):
        h = pl.program_id(0)

        @pl.when(h == 0)
        def _():
            o_ref[...] = jnp.zeros_like(o_ref)

        idxv = idx_ref[...]
        dv = db_ref[0]
        rows = lax.broadcasted_iota(jnp.int32, (NUM_BUCKETS, LANES), 0)
        lanes = lax.broadcasted_iota(jnp.int32, (NUM_BUCKETS, LANES), 1)
        acc = o_ref[...]
        for b in range(NUM_BUCKETS):
            s = jnp.sum(jnp.where(idxv == b, dv, 0.0))
            acc = acc + jnp.where((rows == b) & (lanes == h), s, 0.0)
        o_ref[...] = acc

    return pl.pallas_call(
        body, name=name, grid=(2 * N_PAIRS,),
        in_specs=[pl.BlockSpec((1, QBLK, 2 * QBLK), lambda h: (h, 0, 0)),
                  pl.BlockSpec((QBLK, 2 * QBLK), lambda h: (0, 0))],
        out_specs=pl.BlockSpec((NUM_BUCKETS, LANES), lambda h: (0, 0)),
        out_shape=jax.ShapeDtypeStruct((NUM_BUCKETS, LANES), F32),
        compiler_params=pltpu.CompilerParams(dimension_semantics=("arbitrary",)),
    )(d_bias, idx)


def _lane():
    return lax.broadcasted_iota(jnp.int32, (1, LANES), 1)


def _head_sel(h):
    return (_lane() < HEAD_DIM) if h == 0 else (_lane() >= HEAD_DIM)


def _stat_lo():
    return (_lane() % HEAD_DIM) < (HEAD_DIM // 2)


def _stack_heads(t, scale=None):
    if scale is not None:
        t = t * scale
    zero = jnp.zeros_like(t)
    return jnp.concatenate([jnp.where(_head_sel(0), t, zero), jnp.where(_head_sel(1), t, zero)], axis=0)


def _class_spec(L, r, cols, stride, pps):
    chunks = N_PAIRS // pps
    mode = pl.Buffered(1) if r * chunks == 1 else None
    return pl.BlockSpec((L, MIX_W // chunks), lambda j, c: (0, (cols + j * stride) * chunks + c), pipeline_mode=mode)


def _rows(b, n_blocks=1):
    return pl.ds(pl.multiple_of(b * QBLK, QBLK), n_blocks * QBLK)


def _key_window(ref, b, cols, first):
    if first:
        blk = ref[0:QBLK, cols]
        return jnp.concatenate([blk, blk], axis=0)
    return ref[_rows(b - 1, 2), cols]


def _band_fwd(qa, ka, va, bias, *, r, q_col, k_col, v_col, stride, pattern, name, carry=None):
    L = qa.shape[0]
    nb = L // QBLK
    dn = (((1,), (1,)), ((), ()))
    scale = HEAD_DIM ** -0.5

    pps = N_PAIRS

    def body(q_ref, k_ref, v_ref, bias_ref, o_ref, lse_ref):
        sel0 = _head_sel(0)
        pair0 = pl.program_id(1) * pps

        def block(b, first):
            rows = _rows(b)
            for i in range(pps):
                cols = slice(i * LANES, (i + 1) * LANES)
                q2 = _stack_heads(q_ref[rows, cols], scale)
                k = _key_window(k_ref, b, cols, first)
                v = _key_window(v_ref, b, cols, first)
                bias2 = bias_ref[1 if first else 0, pl.ds(2 * (pair0 + i), 2)].reshape(2 * QBLK, 2 * QBLK)
                s = lax.dot_general(q2, k, dn, preferred_element_type=F32) + bias2
                m = jnp.max(s, axis=1, keepdims=True)
                p = jnp.exp(s - m)
                l = jnp.sum(p, axis=1, keepdims=True)
                o = jnp.dot(p.astype(v.dtype), v, preferred_element_type=F32) / l
                lse = m + jnp.log(l)
                o_ref[rows, cols] = jnp.where(sel0, o[0:QBLK], o[QBLK:]).astype(o_ref.dtype)
                lse_ref[rows, cols] = jnp.where(sel0, lse[0:QBLK], lse[QBLK:])

        block(0, True)
        if nb > 1:
            pl.loop(1, nb)(lambda b: block(b, False))

    bias_spec = pl.BlockSpec((None, 2, 2 * N_PAIRS, QBLK, 2 * QBLK), lambda j, c: (pattern, 0, 0, 0, 0))
    out_spec = _class_spec(L, r, 0, 1, pps)
    blk_bytes = _nbytes((L, pps * LANES), F32)
    return _call(
        body, name=name, grid=(r, N_PAIRS // pps),
        in_specs=[_class_spec(L, r, q_col, stride, pps), _class_spec(L, r, k_col, stride, pps),
                  _class_spec(L, r, v_col, stride, pps), bias_spec],
        out_specs=[out_spec, out_spec],
        out_shape=[jax.ShapeDtypeStruct((L, r * MIX_W), BF16), jax.ShapeDtypeStruct((L, r * MIX_W), F32)],
        args=[qa, ka, va, bias], vmem_limit=_vmem_limit(*([blk_bytes] * 4)), carry=carry)


def _class_rows_spec(rows, r, width):
    return pl.BlockSpec((rows // r, r * width), lambda i, *_: (i, 0))


def _token_scratch(rows, width):
    return pltpu.VMEM((width // LANES, rows, LANES), F32)


def _fill_tokens(scratch, value):
    for c in range(scratch.shape[0]):
        scratch[c] = value[:, c * LANES:(c + 1) * LANES]


def _read_tokens(scratch):
    return jnp.concatenate([scratch[c] for c in range(scratch.shape[0])], axis=1)


def _to_tokens(blk_ref, r, scratch):
    if r == 1:
        return blk_ref[...].astype(F32)
    nt, rows, _ = scratch.shape
    for j in range(r):
        for c in range(nt):
            lanes = slice((j * nt + c) * LANES, (j * nt + c + 1) * LANES)
            scratch.at[c][pl.ds(j, rows // r, stride=r), :] = blk_ref[:, lanes].astype(F32)
    return _read_tokens(scratch)


def _from_tokens(dst_ref, r, scratch):
    nt, rows, _ = scratch.shape
    if r == 1:
        dst_ref[...] = _read_tokens(scratch).astype(dst_ref.dtype)
        return
    for j in range(r):
        for c in range(nt):
            lanes = slice((j * nt + c) * LANES, (j * nt + c + 1) * LANES)
            dst_ref[:, lanes] = scratch.at[c][pl.ds(j, rows // r, stride=r), :].astype(dst_ref.dtype)


def _mm_nt_classes(a, b, bias, dils, *, bm, name):
    M, K = a.shape
    N = b.shape[0]
    dn = (((1,), (1,)), ((), ()))

    def body(a_ref, b_ref, bias_ref, *rest):
        outs, tile = rest[:-1], rest[-1]
        _fill_tokens(tile, lax.dot_general(a_ref[...], b_ref[...], dn, preferred_element_type=F32) + bias_ref[...])
        for out, r in zip(outs, dils):
            _from_tokens(out, r, tile)

    limit = _vmem_limit(_nbytes((bm, K), a.dtype), _nbytes((K, N), b.dtype) // 2, (1 + len(dils)) * _nbytes((bm, N), F32))
    return pl.pallas_call(
        body, name=name, grid=(M // bm,),
        in_specs=[pl.BlockSpec((bm, K), lambda i: (i, 0)),
                  pl.BlockSpec((N, K), lambda i: (0, 0), pipeline_mode=pl.Buffered(1)),
                  pl.BlockSpec((1, N), lambda i: (0, 0))],
        out_specs=tuple(_class_rows_spec(bm, r, N) for r in dils),
        out_shape=tuple(jax.ShapeDtypeStruct((M // r, r * N), BF16) for r in dils),
        scratch_shapes=[_token_scratch(bm, N)],
        compiler_params=pltpu.CompilerParams(dimension_semantics=("parallel",), vmem_limit_bytes=limit),
    )(a, b, bias)


def _merge(branches, dils, sink, *, name):
    W = MIX_W
    T = branches[0][0].shape[0] * dils[0]
    nbr = len(branches)

    def body(*refs):
        sink_ref = refs[2 * nbr]
        out_ref, outb_ref, lse_ref, scratch = refs[2 * nbr + 1:]
        sk = sink_ref[...]
        lses = [_to_tokens(refs[2 * t + 1], dils[t], scratch) for t in range(nbr)]
        mx = sk
        for lse in lses:
            mx = jnp.maximum(mx, lse)
        den = jnp.exp(sk - mx)
        num = jnp.zeros_like(mx)
        for t in range(nbr):
            w = jnp.exp(lses[t] - mx)
            den = den + w
            num = num + w * _to_tokens(refs[2 * t], dils[t], scratch)
        out = num / den
        out_ref[...] = out
        outb_ref[...] = out.astype(outb_ref.dtype)
        lse_ref[...] = mx + jnp.log(den)

    args = [a for br in branches for a in br] + [sink]
    in_specs = [_class_rows_spec(ROW_BLK, r, W) for r in dils for _ in range(2)] + [_vec_spec(W)]
    return pl.pallas_call(
        body, name=name, grid=(T // ROW_BLK,), in_specs=in_specs,
        out_specs=(_row_spec(W), _row_spec(W), _row_spec(W)),
        out_shape=(jax.ShapeDtypeStruct((T, W), F32), jax.ShapeDtypeStruct((T, W), BF16),
                   jax.ShapeDtypeStruct((T, W), F32)),
        scratch_shapes=[_token_scratch(ROW_BLK, W)],
        compiler_params=pltpu.CompilerParams(dimension_semantics=("parallel",)),
    )(*args)


def _attn_delta(dmix, outs, lses, sink, dils, *, name):
    T = dmix.shape[0]
    W = MIX_W
    nd = len(dils)

    def body(dmix_ref, oa_ref, ob_ref, la_ref, lb_ref, sink_ref, *rest):
        doa_ref, sta_ref, dsink_ref = rest[0:3]
        dob_refs, stb_refs = rest[3:3 + nd], rest[3 + nd:3 + 2 * nd]
        do_tile, st_tile = rest[3 + 2 * nd:]

        @pl.when(pl.program_id(0) == 0)
        def _():
            dsink_ref[...] = jnp.zeros_like(dsink_ref)

        sel0, lo = _head_sel(0), _stat_lo()
        for mixer, (o_ref, l_ref) in enumerate(((oa_ref, la_ref), (ob_ref, lb_ref))):
            for i in range(N_PAIRS):
                cols = slice(i * LANES, (i + 1) * LANES)
                do = dmix_ref[:, mixer * W + i * LANES:mixer * W + (i + 1) * LANES]
                prod = do * o_ref[:, cols]
                d0 = jnp.sum(jnp.where(sel0, prod, 0.0), axis=1, keepdims=True)
                d1 = jnp.sum(jnp.where(sel0, 0.0, prod), axis=1, keepdims=True)
                delta = jnp.where(sel0, d0, d1)
                lse = l_ref[:, cols]
                stat = jnp.where(lo, lse, delta)
                if mixer == 0:
                    sta_ref[:, cols] = stat
                    doa_ref[:, cols] = do.astype(doa_ref.dtype)
                    dsink_ref[:, cols] += -jnp.sum(jnp.exp(sink_ref[:, cols] - lse) * delta, axis=0, keepdims=True)
                else:
                    st_tile[i] = stat
                    do_tile[i] = do
        for t, r in enumerate(dils):
            _from_tokens(dob_refs[t], r, do_tile)
            _from_tokens(stb_refs[t], r, st_tile)

    class_specs = [_class_rows_spec(ROW_BLK, r, W) for r in dils]
    out_shape = [jax.ShapeDtypeStruct((T, W), BF16), jax.ShapeDtypeStruct((T, W), F32), jax.ShapeDtypeStruct((1, W), F32)]
    out_shape += [jax.ShapeDtypeStruct((T // r, r * W), BF16) for r in dils]
    out_shape += [jax.ShapeDtypeStruct((T // r, r * W), F32) for r in dils]
    res = pl.pallas_call(
        body, name=name, grid=(T // ROW_BLK,),
        in_specs=[_row_spec(2 * W), _row_spec(W), _row_spec(W), _row_spec(W), _row_spec(W), _vec_spec(W)],
        out_specs=tuple([_row_spec(W), _row_spec(W), _vec_spec(W)] + class_specs + class_specs),
        out_shape=tuple(out_shape),
        scratch_shapes=[_token_scratch(ROW_BLK, W), _token_scratch(ROW_BLK, W)],
        compiler_params=pltpu.CompilerParams(dimension_semantics=("arbitrary",)),
    )(dmix, outs[0], outs[1], lses[0], lses[1], sink)
    return res[0], res[1], res[2], res[3:3 + nd], res[3 + nd:]


def _band_bwd(qa, ka, va, d_out, stat, bias, *, r, q_col, k_col, v_col, stride, pattern, name, carry=None):
    L = qa.shape[0]
    nb = L // QBLK
    dn_nt = (((1,), (1,)), ((), ()))
    dn_tn = (((0,), (0,)), ((), ()))
    scale = HEAD_DIM ** -0.5

    pps = N_PAIRS if r > 1 else N_PAIRS // 2

    def body(q_ref, k_ref, v_ref, do_ref, st_ref, bias_ref, dq_ref, dk_ref, dv_ref, db_ref, dk_carry, dv_carry):
        @pl.when((pl.program_id(0) == 0) & (pl.program_id(1) == 0))
        def _():
            db_ref[...] = jnp.zeros_like(db_ref)

        lo, sel0 = _stat_lo(), _head_sel(0)
        pair0 = pl.program_id(1) * pps

        def block(b, first):
            rows = _rows(b)
            for i in range(pps):
                heads = pl.ds(2 * (pair0 + i), 2)
                cols = slice(i * LANES, (i + 1) * LANES)
                q2 = _stack_heads(q_ref[rows, cols], scale)
                k = _key_window(k_ref, b, cols, first)
                v = _key_window(v_ref, b, cols, first)
                do2 = _stack_heads(do_ref[rows, cols])
                st = st_ref[rows, cols]
                lse2 = jnp.concatenate(
                    [jnp.max(jnp.where(_head_sel(h) & lo, st, -jnp.inf), axis=1, keepdims=True) for h in range(2)], axis=0)
                delta2 = jnp.concatenate(
                    [jnp.sum(jnp.where(_head_sel(h) & ~lo, st, 0.0), axis=1, keepdims=True) for h in range(2)],
                    axis=0) * (2.0 / HEAD_DIM)
                bias2 = bias_ref[1 if first else 0, heads].reshape(2 * QBLK, 2 * QBLK)
                s = lax.dot_general(q2, k, dn_nt, preferred_element_type=F32) + bias2
                p = jnp.exp(s - lse2)
                dp = lax.dot_general(do2, v, dn_nt, preferred_element_type=F32)
                ds = p * (dp - delta2)
                db_ref[heads] += ds.reshape(2, QBLK, 2 * QBLK)
                dsb = ds.astype(k.dtype)
                dq2 = jnp.dot(dsb, k, preferred_element_type=F32)
                dq_ref[rows, cols] = (jnp.where(sel0, dq2[0:QBLK], dq2[QBLK:]) * scale).astype(dq_ref.dtype)
                dk_acc = lax.dot_general(dsb, q2, dn_tn, preferred_element_type=F32)
                dv_acc = lax.dot_general(p.astype(k.dtype), do2, dn_tn, preferred_element_type=F32)
                if not first:
                    prev = _rows(b - 1)
                    dk_ref[prev, cols] = (dk_carry[:, cols] + dk_acc[0:QBLK]).astype(dk_ref.dtype)
                    dv_ref[prev, cols] = (dv_carry[:, cols] + dv_acc[0:QBLK]).astype(dv_ref.dtype)
                dk_carry[:, cols] = dk_acc[QBLK:2 * QBLK]
                dv_carry[:, cols] = dv_acc[QBLK:2 * QBLK]

        block(0, True)
        if nb > 1:
            pl.loop(1, nb)(lambda b: block(b, False))

        last = _rows(nb - 1)
        dk_ref[last, :] = dk_carry[...].astype(dk_ref.dtype)
        dv_ref[last, :] = dv_carry[...].astype(dv_ref.dtype)

    tok_spec = _class_spec(L, r, 0, 1, pps)
    bias_spec = pl.BlockSpec((None, 2, 2 * N_PAIRS, QBLK, 2 * QBLK), lambda j, c: (pattern, 0, 0, 0, 0))
    dbias_spec = pl.BlockSpec((2 * N_PAIRS, QBLK, 2 * QBLK), lambda j, c: (0, 0, 0))
    grad = jax.ShapeDtypeStruct((L, r * MIX_W), BF16)
    blk_bytes = _nbytes((L, pps * LANES), BF16)
    carry_shape = pltpu.VMEM((QBLK, pps * LANES), F32)
    return _call(
        body, name=name, grid=(r, N_PAIRS // pps),
        in_specs=[_class_spec(L, r, q_col, stride, pps), _class_spec(L, r, k_col, stride, pps),
                  _class_spec(L, r, v_col, stride, pps), tok_spec, tok_spec, bias_spec],
        out_specs=[tok_spec, tok_spec, tok_spec, dbias_spec],
        out_shape=[grad, grad, grad, jax.ShapeDtypeStruct((2 * N_PAIRS, QBLK, 2 * QBLK), F32)],
        args=[qa, ka, va, d_out, stat, bias], scratch=[carry_shape, carry_shape],
        vmem_limit=_vmem_limit(*([blk_bytes] * 9)), semantics=("arbitrary", "arbitrary"), carry=carry)


def _dz_assemble(dq_a, dk_a, dv_a, b_grads, dils, *, name):
    T = dq_a.shape[0]
    W = MIX_W

    def group_sum(x):
        u0 = x[:, 0:LANES] + x[:, LANES:2 * LANES]
        u1 = x[:, 2 * LANES:3 * LANES] + x[:, 3 * LANES:4 * LANES]
        s0 = u0 + pltpu.roll(u0, HEAD_DIM, 1)
        s1 = u1 + pltpu.roll(u1, HEAD_DIM, 1)
        return jnp.where(_head_sel(0), s0, s1)

    def body(*refs):
        dqa_ref, dka_ref, dva_ref = refs[0:3]
        b_refs = refs[3:12]
        dza_ref, dzb_ref, sa_ref, sb_ref, scratch = refs[12:]

        @pl.when(pl.program_id(0) == 0)
        def _():
            sa_ref[...] = jnp.zeros_like(sa_ref)
            sb_ref[...] = jnp.zeros_like(sb_ref)

        def put(dst, acc, col, part):
            w = part.shape[1]
            dst[:, col:col + w] = part.astype(dst.dtype)
            acc[:, col:col + w] += jnp.sum(part, axis=0, keepdims=True)

        put(dza_ref, sa_ref, 0, dqa_ref[...].astype(F32))
        put(dza_ref, sa_ref, W, group_sum(dka_ref[...].astype(F32)))
        put(dza_ref, sa_ref, W + LANES, group_sum(dva_ref[...].astype(F32)))
        for t in range(3):
            part = _to_tokens(b_refs[t], dils[0], scratch)
            for pat in range(1, len(dils)):
                part = part + _to_tokens(b_refs[3 * pat + t], dils[pat], scratch)
            put(dzb_ref, sb_ref, t * W, part)

    args = [dq_a, dk_a, dv_a] + [g[t] for g in b_grads for t in range(3)]
    return pl.pallas_call(
        body, name=name, grid=(T // ROW_BLK,),
        in_specs=[_row_spec(W)] * 3 + [_class_rows_spec(ROW_BLK, r, W) for r in dils for _ in range(3)],
        out_specs=(_row_spec(ZA_W), _row_spec(ZB_W), _vec_spec(ZA_W), _vec_spec(ZB_W)),
        out_shape=(jax.ShapeDtypeStruct((T, ZA_W), BF16), jax.ShapeDtypeStruct((T, ZB_W), BF16),
                   jax.ShapeDtypeStruct((1, ZA_W), F32), jax.ShapeDtypeStruct((1, ZB_W), F32)),
        scratch_shapes=[_token_scratch(ROW_BLK, W)],
        compiler_params=pltpu.CompilerParams(dimension_semantics=("arbitrary",)),
    )(*args)


def _place():
    x, y, c = lax.axis_index("x"), lax.axis_index("y"), lax.axis_index("c")
    chips = [(1 - x, y), (x, 1 - y), (1 - x, 1 - y)]
    return x, y, c, chips


def _cast_place(shard, k_idx, *, name):
    rs, cs = shard.shape
    rb = _row_block(rs, 512, 16)
    nblk = rs // rb

    def body(k_ref, s_ref, o_ref):
        o_ref[...] = s_ref[...].astype(o_ref.dtype)

    grid_spec = pltpu.PrefetchScalarGridSpec(
        num_scalar_prefetch=1, grid=(nblk,),
        in_specs=[pl.BlockSpec((rb, cs), lambda t, k_ref: (t, 0))],
        out_specs=pl.BlockSpec((rb, cs), lambda t, k_ref: (k_ref[0] * nblk + t, 0)))
    return pl.pallas_call(
        body, name=name, grid_spec=grid_spec, out_shape=jax.ShapeDtypeStruct((N_CHIPS * rs, cs), BF16),
        compiler_params=pltpu.CompilerParams(dimension_semantics=("parallel",)),
    )(k_idx, shard)


def _gather_carry(fulls, part=(0, 1)):
    n = len(fulls)
    p_idx, n_parts = part

    def piece(ref, chip_idx, half):
        rs = ref.shape[0] // N_CHIPS
        rh = rs // 2
        rows = rh // n_parts
        return ref.at[pl.ds(chip_idx * rs + half * rh + p_idx * rows, rows)]

    def copy(sems, sem, src_ref, dst_ref, to):
        return pltpu.make_async_remote_copy(src_ref=src_ref, dst_ref=dst_ref, send_sem=sems[0].at[sem],
                                            recv_sem=sems[1].at[sem], device_id=to, device_id_type=MESH)

    def ici_copy(ins, outs, sems, i, j, chip, k, c):
        return copy(sems, 3 * i + j, piece(ins[i], k, c), piece(outs[i], k, c), (*chip, c))

    def start(ins, outs, sems):
        x, y, c, chips = _place()
        for i in range(n):
            for j, chip in enumerate(chips):
                ici_copy(ins, outs, sems, i, j, chip, 2 * x + y, c).start()

    def finish(ins, outs, sems):
        x, y, c, chips = _place()
        sibling = (x, y, 1 - c)
        passed = []
        for i in range(n):
            for j, chip in enumerate(chips):
                region = piece(outs[i], 2 * chip[0] + chip[1], c)
                copy(sems, 3 * i + j, region, region, (*chip, c)).wait_recv()
                cp = copy(sems, 3 * n + 3 * i + j, region, region, sibling)
                cp.start()
                passed.append(cp)
        for i in range(n):
            for j, chip in enumerate(chips):
                region = piece(outs[i], 2 * chip[0] + chip[1], 1 - c)
                copy(sems, 3 * n + 3 * i + j, region, region, sibling).wait_recv()
        for i in range(n):
            for j, chip in enumerate(chips):
                ici_copy(ins, outs, sems, i, j, chip, 2 * x + y, c).wait_send()
        for cp in passed:
            cp.wait_send()

    return _Carry(fulls, [jax.ShapeDtypeStruct(a.shape, a.dtype) for a in fulls], {i: i for i in range(n)},
                  [pltpu.SemaphoreType.DMA((6 * n,)), pltpu.SemaphoreType.DMA((6 * n,))], start, finish)


def _pair_swap_carry(grads):
    n = len(grads)

    def copy(ins, outs, sems, i):
        x, y, c, _ = _place()
        return pltpu.make_async_remote_copy(src_ref=ins[i].at[:, 1 - c], dst_ref=outs[i], send_sem=sems[0].at[i],
                                            recv_sem=sems[1].at[i], device_id=(x, y, 1 - c), device_id_type=MESH)

    def start(ins, outs, sems):
        for i in range(n):
            copy(ins, outs, sems, i).start()

    def finish(ins, outs, sems):
        for i in range(n):
            copy(ins, outs, sems, i).wait()

    return _Carry(grads, [jax.ShapeDtypeStruct((a.shape[0], a.shape[2], a.shape[3]), a.dtype) for a in grads], {},
                  [pltpu.SemaphoreType.DMA((n,)), pltpu.SemaphoreType.DMA((n,))], start, finish)


def _pair_sum(g4, got, c_idx, *, name):
    _, _, rh, cs = g4.shape
    rb = _row_block(rh, 512, 16)

    def body(c_ref, own_ref, got_ref, o_ref):
        o_ref[...] = (own_ref[...].astype(F32) + got_ref[...].astype(F32)).astype(o_ref.dtype)

    grid_spec = pltpu.PrefetchScalarGridSpec(
        num_scalar_prefetch=1, grid=(N_CHIPS, rh // rb),
        in_specs=[pl.BlockSpec((None, None, rb, cs), lambda k, t, c_ref: (k, c_ref[0], t, 0)),
                  pl.BlockSpec((None, rb, cs), lambda k, t, c_ref: (k, t, 0))],
        out_specs=pl.BlockSpec((None, rb, cs), lambda k, t, c_ref: (k, t, 0)))
    return pl.pallas_call(
        body, name=name, grid_spec=grid_spec, out_shape=jax.ShapeDtypeStruct((N_CHIPS, rh, cs), BF16),
        compiler_params=pltpu.CompilerParams(dimension_semantics=("parallel", "parallel")),
    )(c_idx, g4, got)


def _chip_scatter_carry(sums):
    n = len(sums)

    def copies(ins, outs, sems):
        x, y, c, chips = _place()
        return [pltpu.make_async_remote_copy(src_ref=ins[i].at[2 * chip[0] + chip[1]], dst_ref=outs[i].at[j],
                                             send_sem=sems[0].at[3 * i + j], recv_sem=sems[1].at[3 * i + j],
                                             device_id=(*chip, c), device_id_type=MESH)
                for i in range(n) for j, chip in enumerate(chips)]

    def start(ins, outs, sems):
        for cp in copies(ins, outs, sems):
            cp.start()

    def finish(ins, outs, sems):
        for cp in copies(ins, outs, sems):
            cp.wait()

    return _Carry(sums, [jax.ShapeDtypeStruct((3,) + a.shape[1:], a.dtype) for a in sums], {},
                  [pltpu.SemaphoreType.DMA((3 * n,)), pltpu.SemaphoreType.DMA((3 * n,))], start, finish)


def _chip_sum(sums, got, kc_idx, *, name):
    _, rh, cs = sums.shape
    rb = _row_block(rh, 512, 16)
    nblk = rh // rb

    def body(kc_ref, own_ref, got_ref, o_ref):
        acc = own_ref[...].astype(F32)
        for j in range(3):
            acc = acc + got_ref[j].astype(F32)
        o_ref[...] = acc

    grid_spec = pltpu.PrefetchScalarGridSpec(
        num_scalar_prefetch=1, grid=(nblk,),
        in_specs=[pl.BlockSpec((None, rb, cs), lambda t, kc: (kc[0], t, 0)),
                  pl.BlockSpec((3, rb, cs), lambda t, kc: (0, t, 0))],
        out_specs=pl.BlockSpec((rb, cs), lambda t, kc: (kc[1] * nblk + t, 0)))
    return pl.pallas_call(
        body, name=name, grid_spec=grid_spec, out_shape=jax.ShapeDtypeStruct((2 * rh, cs), F32),
        compiler_params=pltpu.CompilerParams(dimension_semantics=("parallel",)),
    )(kc_idx, sums, got)


def _half_swap_carry(shards):
    n = len(shards)

    def copy(ins, outs, sems, i, to_my_half):
        x, y, c, _ = _place()
        rh = ins[i].shape[0] // 2
        half = c if to_my_half else 1 - c
        return pltpu.make_async_remote_copy(
            src_ref=ins[i].at[pl.ds(c * rh, rh)], dst_ref=outs[i].at[pl.ds(half * rh, rh)],
            send_sem=sems[0].at[i], recv_sem=sems[1].at[i], device_id=(x, y, 1 - c), device_id_type=MESH)

    def start(ins, outs, sems):
        for i in range(n):
            copy(ins, outs, sems, i, True).start()

    def finish(ins, outs, sems):
        for i in range(n):
            copy(ins, outs, sems, i, False).wait_recv()
        for i in range(n):
            copy(ins, outs, sems, i, True).wait_send()

    return _Carry(shards, [jax.ShapeDtypeStruct(a.shape, a.dtype) for a in shards], {i: i for i in range(n)},
                  [pltpu.SemaphoreType.DMA((n,)), pltpu.SemaphoreType.DMA((n,))], start, finish)


SMALL_ROW = 1024


def _small_layout(shapes):
    starts, row = [], 0
    for r, c in shapes:
        starts.append(row)
        row += -(-c // SMALL_ROW) if r == 1 else r
    return starts, -(-row // SUBLANES) * SUBLANES


def _small_pieces(shape, start):
    r, c = shape
    if r == 1:
        return [(start + q, min(SMALL_ROW, c - q * SMALL_ROW), q * SMALL_ROW) for q in range(-(-c // SMALL_ROW))]
    return None


def _small_all_reduce(parts, *, name):
    shapes = [a.shape for a in parts]
    starts, rows = _small_layout(shapes)
    n = len(parts)

    def body(*refs):
        ins, outs = refs[0:n], refs[n:2 * n]
        buf, send_sems, recv_sems = refs[2 * n:]
        x, y, c, _ = _place()
        me = 4 * x + 2 * y + c
        mine = buf.at[me]
        mine[...] = jnp.zeros((rows, SMALL_ROW), F32)
        for ref, shape, start in zip(ins, shapes, starts):
            pieces = _small_pieces(shape, start)
            if pieces is None:
                mine[start:start + shape[0], 0:shape[1]] = ref[...]
            else:
                for row, width, col in pieces:
                    mine[row:row + 1, 0:width] = ref[:, col:col + width]
        copies = []
        for d in range(1, 8):
            to = (1 - x if d & 4 else x, 1 - y if d & 2 else y, 1 - c if d & 1 else c)
            cp = pltpu.make_async_remote_copy(src_ref=mine, dst_ref=mine, send_sem=send_sems.at[d - 1],
                                              recv_sem=recv_sems.at[d - 1], device_id=to, device_id_type=MESH)
            cp.start()
            copies.append((cp, to))
        for d in range(1, 8):
            to = copies[d - 1][1]
            sender = 4 * to[0] + 2 * to[1] + to[2]
            pltpu.make_async_remote_copy(src_ref=mine, dst_ref=buf.at[sender], send_sem=send_sems.at[d - 1],
                                         recv_sem=recv_sems.at[d - 1], device_id=to, device_id_type=MESH).wait_recv()
        for cp, _ in copies:
            cp.wait_send()
        acc = buf[0]
        for s in range(1, 8):
            acc = acc + buf[s]
        mine[...] = acc
        for ref, shape, start in zip(outs, shapes, starts):
            pieces = _small_pieces(shape, start)
            if pieces is None:
                ref[...] = mine[start:start + shape[0], 0:shape[1]]
            else:
                for row, width, col in pieces:
                    ref[:, col:col + width] = mine[row:row + 1, 0:width]

    vm = pl.BlockSpec(memory_space=pltpu.VMEM)
    return pl.pallas_call(
        body, name=name, in_specs=[vm] * n, out_specs=tuple([vm] * n),
        out_shape=tuple(jax.ShapeDtypeStruct(s, F32) for s in shapes),
        scratch_shapes=[pltpu.VMEM((8, rows, SMALL_ROW), F32), pltpu.SemaphoreType.DMA((7,)),
                        pltpu.SemaphoreType.DMA((7,))],
    )(*parts)


def _adamw_small(ws, gs, ms, vs, *, name):
    n = len(ws)
    c1 = 1.0 - ADAM_B1 ** ADAM_STEP
    c2 = 1.0 - ADAM_B2 ** ADAM_STEP

    def body(*refs):
        for k in range(n):
            w_ref, g_ref, m_ref, v_ref = (refs[t * n + k] for t in range(4))
            go_ref, d_ref, mo_ref, vo_ref = (refs[(4 + t) * n + k] for t in range(4))
            gv = g_ref[...]
            go_ref[...] = gv
            mn = ADAM_B1 * m_ref[...] + (1.0 - ADAM_B1) * gv
            vn = ADAM_B2 * v_ref[...] + (1.0 - ADAM_B2) * (gv * gv)
            d_ref[...] = -ADAM_LR * ((mn / c1) / (jnp.sqrt(vn / c2) + ADAM_EPS) + ADAM_WD * w_ref[...])
            mo_ref[...] = mn
            vo_ref[...] = vn

    vm = pl.BlockSpec(memory_space=pltpu.VMEM)
    shapes = tuple(jax.ShapeDtypeStruct(a.shape, F32) for a in ws)
    res = pl.pallas_call(
        body, name=name, in_specs=[vm] * (4 * n), out_specs=tuple([vm] * (4 * n)), out_shape=shapes * 4,
    )(*ws, *gs, *ms, *vs)
    return res[0:n], res[n:2 * n], res[2 * n:3 * n], res[3 * n:4 * n]


def _adamw(w, g, m, v, *, name):
    R, C = w.shape
    rb = _row_block(R, 256, SUBLANES)
    c1 = 1.0 - ADAM_B1 ** ADAM_STEP
    c2 = 1.0 - ADAM_B2 ** ADAM_STEP

    def body(w_ref, g_ref, m_ref, v_ref, go_ref, d_ref, mo_ref, vo_ref):
        gv = g_ref[...]
        go_ref[...] = gv
        mn = ADAM_B1 * m_ref[...] + (1.0 - ADAM_B1) * gv
        vn = ADAM_B2 * v_ref[...] + (1.0 - ADAM_B2) * (gv * gv)
        d_ref[...] = -ADAM_LR * ((mn / c1) / (jnp.sqrt(vn / c2) + ADAM_EPS) + ADAM_WD * w_ref[...])
        mo_ref[...] = mn
        vo_ref[...] = vn

    spec = pl.BlockSpec((rb, C), lambda i: (i, 0))
    shp = jax.ShapeDtypeStruct((R, C), F32)
    return pl.pallas_call(
        body, name=name, grid=(R // rb,), in_specs=[spec] * 4, out_specs=(spec,) * 4,
        out_shape=(shp,) * 4, compiler_params=pltpu.CompilerParams(dimension_semantics=("parallel",)),
    )(w, g, m, v)


def _local_step(x, p, target, small, sched):
    T = x.shape[0]
    W = MIX_W
    rel_bias = small["rel_bias"]
    b_in_a, b_in_b = small["b_in"][:, 0:ZA_W], small["b_in"][:, ZA_W:]

    idx_np = [_t5_index(A_WINDOW - 1, 1)] + [_t5_index(window // dil, dil) for window, dil in B_PATTERNS]
    idx_all = jnp.asarray(np.stack(idx_np))
    n1, bias_all = sched.run(_prologue, x, small["ffn1_pre_g"], rel_bias, idx_all, name="prologue")
    w_gu1 = sched.weight("ffn1_w_gu")
    *gu1, a1 = sched.run(_ffn_up, n1, w_gu1, bm=512, name="ffn1_gu")
    w_d1 = sched.weight("ffn1_w_down")
    f1, h1, n2 = sched.run(_mm_resid_norm, a1, w_d1, x, small["ffn1_post_g"], 0.5, small["attn_pre_g"], bm=512,
                           name="ffn1_down")
    win_a, win_b = sched.weight("w_in")
    za = _mm_nt(n2, win_a, bm=1024, bn=ZA_W, out_dtype=BF16, name="in_proj_a", add=b_in_a)
    dils = tuple(dil for _, dil in B_PATTERNS)
    zb_by_dil = _mm_nt_classes(n2, win_b, b_in_b, dils, bm=512, name="in_proj_b")

    k_a = za[:, W:W + LANES].reshape(T, 2, 1, HEAD_DIM)
    v_a = za[:, W + LANES:W + 2 * LANES].reshape(T, 2, 1, HEAD_DIM)
    k_a = jnp.broadcast_to(k_a, (T, 2, 4, HEAD_DIM)).reshape(T, W)
    v_a = jnp.broadcast_to(v_a, (T, 2, 4, HEAD_DIM)).reshape(T, W)
    a_args = dict(r=1, q_col=0, k_col=0, v_col=0, stride=0, pattern=0)
    o_a, lse_a1 = sched.run(_band_fwd, za, k_a, v_a, bias_all, name="band_a_fwd", **a_args)
    sink_row = jnp.repeat(small["sinks"], HEAD_DIM, axis=1)
    out_a, out_a_bf, lse_a = _merge([(o_a, lse_a1)], (1,), sink_row, name="merge_a")

    no_sink = jnp.full((1, W), NEG, F32)
    b_ctx, branches = [], []
    for t, dil in enumerate(dils):
        zb_r = zb_by_dil[t]
        args = dict(r=dil, q_col=0, k_col=1, v_col=2, stride=ZB_W // W, pattern=1 + t)
        branches.append(sched.run(_band_fwd, zb_r, zb_r, zb_r, bias_all, name=f"band_b{t}_fwd", **args))
        b_ctx.append((jnp.asarray(idx_np[1 + t]), zb_r, args))
    out_b, out_b_bf, lse_b = _merge(branches, dils, no_sink, name="merge_b")

    mix = jnp.concatenate([out_a_bf, out_b_bf], axis=1)
    w_out = sched.weight("w_out")
    att, h2, n3 = _mm_resid_norm(mix, w_out, h1, small["attn_post_g"], 1.0, small["ffn2_pre_g"], bm=512,
                                 name="out_proj", bias=small["b_out"])
    w_gu2, w_d2 = sched.weight("ffn2_w_gu"), sched.weight("ffn2_w_down")
    *gu2, a2 = _ffn_up(n3, w_gu2, bm=512, name="ffn2_gu")
    f2, h3, n4 = _mm_resid_norm(a2, w_d2, h2, small["ffn2_post_g"], 0.5, small["ple_pre_g"], bm=512,
                                name="ffn2_down")
    w_gate = sched.weight("w_ple_gate")
    p_bf = p.astype(BF16)
    dh4, de, dgp, loss, d_ple_post = _ple_head(n4, w_gate, p_bf, sched.weight("w_ple_proj"), h3, small["ple_post_g"],
                                               target, bm=512, name="ple_head")

    gs = {"ple_post_g": d_ple_post}
    sched.grad("w_ple_proj", _mm_tn(p_bf, de, bm=256, bn=1024, out_dtype=BF16, name="d_w_ple"))
    sched.grad("w_ple_gate", _mm_tn(n4, dgp, bm=512, bn=1024, out_dtype=BF16, name="d_w_gate"))
    dh3, df2, gs["ple_pre_g"], gs["ffn2_post_g"], _ = sched.run(
        _mm_nt_norms_bwd, dgp, w_gate, dh4, h3, small["ple_pre_g"], f2, small["ffn2_post_g"], 0.5, bm=512, name="d_n4")

    def ffn_bwd(df, a, gu, n, w_down, w_gu, tag, *norm_args):
        dgu = sched.run(_ffn_down_bwd, df, w_down, gu[0], gu[1], bm=512, name=f"ffn{tag}_d_a")
        sched.grad(f"ffn{tag}_w_gu", _mm_tn(n, dgu, bm=512, bn=1408, out_dtype=BF16, name=f"ffn{tag}_d_w_gu",
                                            n_stack=N_CHIPS))
        sched.grad(f"ffn{tag}_w_down", sched.run(_mm_tn, a, df, bm=256, bn=1024, out_dtype=BF16,
                                                name=f"ffn{tag}_d_w_down"))
        return sched.run(_mm_nt_norms_bwd, dgu, w_gu, *norm_args, bm=512, name=f"ffn{tag}_d_n")

    dh2, datt, gs["ffn2_pre_g"], gs["attn_post_g"], gs["b_out"] = ffn_bwd(
        df2, a2, gu2, n3, w_d2, w_gu2, "2", dh3, h2, small["ffn2_pre_g"], att, small["attn_post_g"], 1.0)
    sched.grad("w_out", _mm_tn(mix, datt, bm=512, bn=1024, out_dtype=BF16, name="d_w_out"))
    dmix = sched.run(_mm_nt, datt, w_out, bm=1024, bn=1024, out_dtype=F32, name="d_mix")

    do_a, stat_a, dsink, do_b, stat_b = _attn_delta(dmix, (out_a, out_b), (lse_a, lse_b), sink_row, dils,
                                                    name="attn_delta")
    gs["sinks"] = dsink[:, ::HEAD_DIM]
    dq_a, dk_a, dv_a, dbias_a = sched.run(_band_bwd, za, k_a, v_a, do_a, stat_a, bias_all, name="band_a_bwd", **a_args)
    d_rel_a = _bias_grad(dbias_a, jnp.asarray(idx_np[0]), name="d_rel_a")
    b_grads = []
    d_rel_b = None
    for t, (idx, zb_r, args) in enumerate(b_ctx):
        dq, dk, dv, dbias = sched.run(_band_bwd, zb_r, zb_r, zb_r, do_b[t], stat_b[t], bias_all,
                                      name=f"band_b{t}_bwd", **args)
        b_grads.append((dq, dk, dv))
        d_rel = _bias_grad(dbias, idx, name=f"d_rel_b{t}")
        d_rel_b = d_rel if d_rel_b is None else d_rel_b + d_rel
    gs["rel_bias"] = jnp.concatenate([d_rel_a[:, 0:2 * N_PAIRS], d_rel_b[:, 0:2 * N_PAIRS]], axis=1)
    dza, dzb, dsum_a, dsum_b = _dz_assemble(dq_a, dk_a, dv_a, b_grads, dils, name="d_z")
    gs["b_in"] = jnp.concatenate([dsum_a, dsum_b], axis=1)
    sched.grad("w_in", (_mm_tn(dza, n2, bm=ZA_W, bn=1024, out_dtype=BF16, name="d_w_in_a"),
                        _mm_tn(dzb, n2, bm=ZB_W // 2, bn=1024, out_dtype=BF16, name="d_w_in_b")))
    dn2_a = _mm_nn(dza, win_a, bm=1024, bn=1024, out_dtype=F32, name="d_n2_a")
    dh1, df1, gs["attn_pre_g"], gs["ffn1_post_g"], _ = sched.run(
        _mm_nt_norms_bwd, dzb, win_b, dh2, h1, small["attn_pre_g"], f1, small["ffn1_post_g"], 0.5, bm=512,
        name="d_n2_b", add=dn2_a, b_is_kn=True)
    grad_x, gs["ffn1_pre_g"] = ffn_bwd(df1, a1, gu1, n1, w_d1, w_gu1, "1", dh1, x, small["ffn1_pre_g"], None, None, 0.0)
    return loss, grad_x, gs


def _to_compute(name, full):
    st = full.reshape(N_CHIPS, full.shape[0] // N_CHIPS, full.shape[1])
    if name in ("ffn1_w_gu", "ffn2_w_gu"):
        return st
    if name == "w_ple_proj":
        return jnp.transpose(st, (1, 0, 2)).reshape(st.shape[1], -1)
    if name == "w_in":
        return full[0:ZA_W], full[ZA_W:]
    return full


def _to_comm(name, g):
    if name == "w_in":
        g = jnp.concatenate(g, axis=0)
    if name == "w_ple_proj":
        g = jnp.transpose(g.reshape(g.shape[0], N_CHIPS, -1), (1, 0, 2))
    rs = g.shape[1] if g.ndim == 3 else g.shape[0] // N_CHIPS
    return g.reshape(N_CHIPS, 2, rs // 2, g.shape[-1])


class _Schedule:
    GATHER = {
        "prologue": [("ffn1_w_gu", (0, 1))],
        "ffn1_gu": [("ffn1_w_down", (0, 1)), ("w_in", (0, 1))],
        "ffn1_down": [("w_out", (0, 1))],
        "band_a_fwd": [("ffn2_w_gu", (0, 2))],
        "band_b0_fwd": [("ffn2_w_gu", (1, 2))],
        "band_b1_fwd": [("ffn2_w_down", (0, 1))],
        "band_b2_fwd": [("w_ple_gate", (0, 1)), ("w_ple_proj", (0, 1))],
    }
    SWAP = {"d_n4": ["w_ple_proj", "w_ple_gate"], "ffn2_d_w_down": ["ffn2_w_gu"], "ffn2_d_n": ["ffn2_w_down"],
            "d_mix": ["w_out"], "d_n2_b": ["w_in"], "ffn1_d_w_down": ["ffn1_w_gu"], "ffn1_d_n": ["ffn1_w_down"]}
    SCATTER = {"ffn2_d_a": ["w_ple_proj", "w_ple_gate"], "ffn2_d_n": ["ffn2_w_gu"], "band_a_bwd": ["ffn2_w_down"],
               "band_b0_bwd": ["w_out"], "ffn1_d_a": ["w_in"], "ffn1_d_n": ["ffn1_w_gu"]}

    def __init__(self, placed, c_idx, kc_idx):
        self.full = dict(placed)
        self.missing = {n: 1.0 for n in placed}
        self.c_idx, self.kc_idx = c_idx, kc_idx
        self.grads, self.swapped, self.pair_sums, self.scattered = {}, {}, {}, {}

    def _gather(self, entries):
        carries, after = [], []
        for part in sorted({pt for _, pt in entries}):
            names = [n for n, pt in entries if pt == part]
            carry = _gather_carry([self.full[n] for n in names], part)
            carries.append(carry)

            def done(carry=carry, names=names, part=part):
                for n, a in zip(names, carry.results):
                    self.full[n] = a
                    self.missing[n] -= 1.0 / part[1]
            after.append(done)
        return carries, after

    def weight(self, name):
        assert abs(self.missing[name]) < 1e-9, (name, self.missing[name])
        return _to_compute(name, self.full[name])

    def grad(self, name, g):
        self.grads[name] = _to_comm(name, g)

    def _swap(self, names):
        carry = _pair_swap_carry([self.grads[n] for n in names])

        def done():
            self.swapped.update(zip(names, carry.results))
        return carry, done

    def _scatter(self, names):
        for n in names:
            self.pair_sums[n] = _pair_sum(self.grads[n], self.swapped[n], self.c_idx, name=f"grad_pair_sum_{n}")
        carry = _chip_scatter_carry([self.pair_sums[n] for n in names])

        def done():
            self.scattered.update(zip(names, carry.results))
        return carry, done

    def run(self, fn, *args, name, **kw):
        carries, after = self._gather(self.GATHER.get(name, []))
        for names, make in ((self.SWAP.get(name), self._swap), (self.SCATTER.get(name), self._scatter)):
            if names:
                carry, done = make(names)
                carries.append(carry)
                after.append(done)
        out = fn(*args, name=name, carry=_join(carries), **kw)
        for done in after:
            done()
        return out

    def finish(self):
        left = [n for n in BIG if n not in self.scattered]
        to_swap = [n for n in left if n not in self.swapped]
        if to_swap:
            carry, done = self._swap(to_swap)
            _comm_call(carry, name="grad_pair_swap")
            done()
        if left:
            carry, done = self._scatter(left)
            _comm_call(carry, name="grad_chip_scatter")
            done()
        halves = [_chip_sum(self.pair_sums[n], self.scattered[n], self.kc_idx, name=f"grad_chip_sum_{n}") for n in BIG]
        carry = _half_swap_carry(halves)
        _comm_call(carry, name="grad_half_swap")
        return dict(zip(BIG, carry.results))


def kernel(x, p, rel_bias, ffn1_pre_g, ffn1_w_gu, ffn1_w_down, ffn1_post_g, attn_pre_g, w_in, b_in, sinks, w_out, b_out, attn_post_g, ffn2_pre_g, ffn2_w_gu, ffn2_w_down, ffn2_post_g, ple_pre_g, w_ple_gate, w_ple_proj, ple_post_g, loss_target, m_rel_bias, m_ffn1_pre_g, m_ffn1_w_gu, m_ffn1_w_down, m_ffn1_post_g, m_attn_pre_g, m_w_in, m_b_in, m_sinks, m_w_out, m_b_out, m_attn_post_g, m_ffn2_pre_g, m_ffn2_w_gu, m_ffn2_w_down, m_ffn2_post_g, m_ple_pre_g, m_w_ple_gate, m_w_ple_proj, m_ple_post_g, v_rel_bias, v_ffn1_pre_g, v_ffn1_w_gu, v_ffn1_w_down, v_ffn1_post_g, v_attn_pre_g, v_w_in, v_b_in, v_sinks, v_w_out, v_b_out, v_attn_post_g, v_ffn2_pre_g, v_ffn2_w_gu, v_ffn2_w_down, v_ffn2_post_g, v_ple_pre_g, v_w_ple_gate, v_w_ple_proj, v_ple_post_g):
    given = dict(locals())
    w = {n: given[n] for n in WEIGHTS}
    m = {n: given["m_" + n] for n in WEIGHTS}
    v = {n: given["v_" + n] for n in WEIGHTS}
    c_pos = lax.axis_index("c").astype(jnp.int32)
    k_pos = (2 * lax.axis_index("x") + lax.axis_index("y")).astype(jnp.int32)
    c_idx, k_idx, kc_idx = c_pos.reshape(1), k_pos.reshape(1), jnp.stack([k_pos, c_pos])

    def shard(a, n):
        return jnp.transpose(a[0]) if n == "w_in" else a[0]

    def unshard(a, n):
        return (jnp.transpose(a) if n == "w_in" else a)[None]

    sched = _Schedule({n: _cast_place(shard(w[n], n), k_idx, name=f"place_{n}") for n in BIG}, c_idx, kc_idx)
    small = {n: (w[n] if n == "rel_bias" else w[n].reshape(1, -1)) for n in SMALL}

    loss_part, grad_x, gs = _local_step(x[0], p[0, 0], loss_target[0], small, sched)

    grads_big = sched.finish()

    *small_grads, loss_row = _small_all_reduce([gs[n] for n in SMALL] + [loss_part], name="small_all_reduce")
    loss = loss_row[0, 0]

    grads, delta, new_m, new_v = {}, {}, {}, {}
    for n in BIG:
        res = _adamw(shard(w[n], n), grads_big[n], shard(m[n], n), shard(v[n], n), name=f"adamw_{n}")
        grads[n], delta[n], new_m[n], new_v[n] = (unshard(a, n) for a in res)
    res = _adamw_small([w[n] for n in SMALL], small_grads, [m[n] for n in SMALL], [v[n] for n in SMALL],
                       name="adamw_small")
    for name, gg, dd, mm, vv in zip(SMALL, *res):
        grads[name], delta[name], new_m[name], new_v[name] = gg, dd, mm, vv

    return (loss, grad_x[None], *[grads[n] for n in WEIGHTS], *[delta[n] for n in WEIGHTS],
            *[new_m[n] for n in WEIGHTS], *[new_v[n] for n in WEIGHTS])
```

```python
import math

import jax
import jax.numpy as jnp
import numpy as np
from jax import lax
from jax.experimental import pallas as pl
from jax.experimental.pallas import tpu as pltpu

F32 = jnp.float32
BF16 = jnp.bfloat16
MESH = pl.DeviceIdType.MESH

EPS = 1e-6
NEG = -1e30
LANES = 128
SUBLANES = 8
HEAD_DIM = 64
QBLK = 128
N_PAIRS = 4
MIX_W = N_PAIRS * LANES
A_WINDOW = 128
B_PATTERNS = ((128, 1), (512, 4), (2048, 16))
NUM_BUCKETS = 32
MAX_DISTANCE = 2048
ZA_W = 768
ZB_W = 1536
N_CHIPS = 4
VMEM_BYTES_V7X = 64 * 2**20

ADAM_LR, ADAM_B1, ADAM_B2, ADAM_EPS, ADAM_WD, ADAM_STEP = 0.001, 0.9, 0.999, 1e-08, 0.01, 10

BIG = ("ffn1_w_gu", "ffn1_w_down", "w_in", "w_out", "ffn2_w_gu", "ffn2_w_down", "w_ple_gate", "w_ple_proj")
SMALL = ("rel_bias", "ffn1_pre_g", "ffn1_post_g", "attn_pre_g", "b_in", "sinks", "b_out", "attn_post_g",
         "ffn2_pre_g", "ffn2_post_g", "ple_pre_g", "ple_post_g")
WEIGHTS = ("rel_bias", "ffn1_pre_g", "ffn1_w_gu", "ffn1_w_down", "ffn1_post_g", "attn_pre_g", "w_in", "b_in",
           "sinks", "w_out", "b_out", "attn_post_g", "ffn2_pre_g", "ffn2_w_gu", "ffn2_w_down", "ffn2_post_g",
           "ple_pre_g", "w_ple_gate", "w_ple_proj", "ple_post_g")


def _vmem_limit(*block_bytes):
    need = 2 * sum(block_bytes) + max(block_bytes) * 2 + (4 << 20)
    return int(min(max(need, 32 << 20), VMEM_BYTES_V7X - (6 << 20)))


def _nbytes(shape, dtype):
    return int(np.prod(shape)) * jnp.dtype(dtype).itemsize


MXU_WIDTH_V7X = 256


def _col_chunks(n):
    return [slice(c, min(c + MXU_WIDTH_V7X, n)) for c in range(0, n, MXU_WIDTH_V7X)]


def _row_halves(rows):
    return [slice(0, rows // 2), slice(rows // 2, rows)]


def _row_block(rows, cap, mult):
    for cand in range(min(rows, cap), 0, -1):
        if rows % cand == 0 and cand % mult == 0:
            return cand
    raise ValueError((rows, cap, mult))


class _Carry:
    def __init__(self, operands, out_shapes, aliases, sems, start, finish):
        self.operands, self.out_shapes, self.aliases, self.sems = list(operands), list(out_shapes), dict(aliases), list(sems)
        self.start, self.finish = start, finish
        self.results = None


def _join(carries):
    carries = [c for c in carries if c is not None]
    if not carries:
        return None
    if len(carries) == 1:
        return carries[0]
    offs, pos = [], [0, 0, 0]
    for c in carries:
        offs.append(tuple(pos))
        pos = [pos[0] + len(c.operands), pos[1] + len(c.out_shapes), pos[2] + len(c.sems)]
    aliases = {}
    for c, (oi, oo, _) in zip(carries, offs):
        aliases.update({oi + i: oo + o for i, o in c.aliases.items()})

    def run(which):
        def fn(ins, outs, sems):
            for c, (oi, oo, os_) in zip(carries, offs):
                getattr(c, which)(ins[oi:oi + len(c.operands)], outs[oo:oo + len(c.out_shapes)],
                                  sems[os_:os_ + len(c.sems)])
        return fn

    joined = _Carry([a for c in carries for a in c.operands], [s for c in carries for s in c.out_shapes], aliases,
                    [s for c in carries for s in c.sems], run("start"), run("finish"))
    joined.parts = (carries, offs)
    return joined


def _set_results(carry, outs):
    carry.results = list(outs)
    if hasattr(carry, "parts"):
        for c, (_, oo, _) in zip(*carry.parts):
            _set_results(c, outs[oo:oo + len(c.out_shapes)])


ANY = pl.BlockSpec(memory_space=pl.ANY)


def _call(body, *, name, grid, in_specs, out_specs, out_shape, args, scratch=(), vmem_limit=None, carry=None,
          semantics=None):
    n_ci, n_co, n_cs = len(args), len(out_shape), len(scratch)
    semantics = semantics or ("parallel",) * len(grid)
    if carry is None:
        res = pl.pallas_call(
            body, name=name, grid=grid, in_specs=list(in_specs), out_specs=tuple(out_specs), out_shape=tuple(out_shape),
            scratch_shapes=list(scratch),
            compiler_params=pltpu.CompilerParams(dimension_semantics=semantics, vmem_limit_bytes=vmem_limit),
        )(*args)
        return list(res)
    n_pi, n_po = len(carry.operands), len(carry.out_shapes)

    def full_body(*refs):
        ci, pi = refs[:n_ci], refs[n_ci:n_ci + n_pi]
        o0 = n_ci + n_pi
        co, po = refs[o0:o0 + n_co], refs[o0 + n_co:o0 + n_co + n_po]
        s0 = o0 + n_co + n_po
        cs, ps = refs[s0:s0 + n_cs], refs[s0 + n_cs:]
        ids = [pl.program_id(ax) for ax in range(len(grid))]
        first, last = ids[0] == 0, ids[0] == grid[0] - 1
        for ax in range(1, len(grid)):
            first, last = first & (ids[ax] == 0), last & (ids[ax] == grid[ax] - 1)

        @pl.when(first)
        def _():
            carry.start(pi, po, ps)

        body(*ci, *co, *cs)

        @pl.when(last)
        def _():
            carry.finish(pi, po, ps)

    res = pl.pallas_call(
        full_body, name=name, grid=grid, in_specs=list(in_specs) + [ANY] * n_pi,
        out_specs=tuple(out_specs) + tuple([ANY] * n_po), out_shape=tuple(out_shape) + tuple(carry.out_shapes),
        scratch_shapes=list(scratch) + carry.sems,
        input_output_aliases={n_ci + i: n_co + o for i, o in carry.aliases.items()},
        compiler_params=pltpu.CompilerParams(dimension_semantics=("arbitrary",) * len(grid),
                                             vmem_limit_bytes=vmem_limit),
    )(*args, *carry.operands)
    _set_results(carry, res[n_co:])
    return list(res[:n_co])


def _comm_call(carry, *, name):
    n_pi, n_po = len(carry.operands), len(carry.out_shapes)

    def body(*refs):
        pi, po, ps = refs[:n_pi], refs[n_pi:n_pi + n_po], refs[n_pi + n_po:]
        carry.start(pi, po, ps)
        carry.finish(pi, po, ps)

    res = pl.pallas_call(
        body, name=name, in_specs=[ANY] * n_pi, out_specs=tuple([ANY] * n_po), out_shape=tuple(carry.out_shapes),
        scratch_shapes=carry.sems, input_output_aliases=dict(carry.aliases),
    )(*carry.operands)
    _set_results(carry, res)


def _mm_nn(a, b, *, bm, bn, out_dtype, name, bias=None, carry=None):
    M, K = a.shape
    stacked = b.ndim == 3
    N = b.shape[0] * b.shape[2] if stacked else b.shape[1]
    assert M % bm == 0 and N % bn == 0 and (not stacked or bn == b.shape[2])

    def body(*refs):
        a_ref, b_ref = refs[0], refs[1]
        o_ref = refs[-1]
        acc = jnp.dot(a_ref[...], b_ref[...], preferred_element_type=F32)
        if bias is not None:
            acc = acc + refs[2][...]
        o_ref[...] = acc.astype(o_ref.dtype)

    if stacked:
        b_spec = pl.BlockSpec((None, K, bn), lambda i, j: (j, 0, 0))
    else:
        b_spec = pl.BlockSpec((K, bn), lambda i, j: (0, j))
    in_specs = [pl.BlockSpec((bm, K), lambda i, j: (i, 0)), b_spec]
    args = [a, b]
    if bias is not None:
        in_specs.append(pl.BlockSpec((1, bn), lambda i, j: (0, j)))
        args.append(bias)
    limit = _vmem_limit(_nbytes((bm, K), a.dtype), _nbytes((K, bn), b.dtype), _nbytes((bm, bn), F32))
    return _call(body, name=name, grid=(M // bm, N // bn), in_specs=in_specs,
                 out_specs=[pl.BlockSpec((bm, bn), lambda i, j: (i, j))],
                 out_shape=[jax.ShapeDtypeStruct((M, N), out_dtype)], args=args, vmem_limit=limit, carry=carry)[0]


def _mm_nt(a, b, *, bm, bn, out_dtype, name, add=None, carry=None):
    M, K = a.shape
    stacked = b.ndim == 3
    N = b.shape[1] if stacked else b.shape[0]
    n_sh = b.shape[0] if stacked else 1
    ks = b.shape[2] if stacked else K
    assert M % bm == 0 and N % bn == 0 and n_sh * ks == K
    dn = (((1,), (1,)), ((), ()))

    def body(*refs):
        a_ref, b_ref = refs[0], refs[1]
        o_ref = refs[-1]
        if stacked:
            acc = lax.dot_general(a_ref[:, 0:ks], b_ref[0], dn, preferred_element_type=F32)
            for s in range(1, n_sh):
                acc = acc + lax.dot_general(a_ref[:, s * ks:(s + 1) * ks], b_ref[s], dn, preferred_element_type=F32)
        else:
            acc = lax.dot_general(a_ref[...], b_ref[...], dn, preferred_element_type=F32)
        if add is not None:
            acc = acc + refs[2][...]
        o_ref[...] = acc.astype(o_ref.dtype)

    if stacked:
        b_spec = pl.BlockSpec((n_sh, bn, ks), lambda i, j: (0, j, 0))
    else:
        b_spec = pl.BlockSpec((bn, K), lambda i, j: (j, 0))
    in_specs = [pl.BlockSpec((bm, K), lambda i, j: (i, 0)), b_spec]
    args = [a, b]
    if add is not None:
        rows = bm if add.shape[0] == M else 1
        in_specs.append(pl.BlockSpec((rows, bn), lambda i, j: (i if rows == bm else 0, j)))
        args.append(add)
    limit = _vmem_limit(_nbytes((bm, K), a.dtype), _nbytes((bn, K), b.dtype), _nbytes((bm, bn), F32))
    return _call(body, name=name, grid=(M // bm, N // bn), in_specs=in_specs,
                 out_specs=[pl.BlockSpec((bm, bn), lambda i, j: (i, j))],
                 out_shape=[jax.ShapeDtypeStruct((M, N), out_dtype)], args=args, vmem_limit=limit, carry=carry)[0]


def _mm_tn(a, b, *, bm, bn, out_dtype, name, n_stack=None, carry=None):
    T, M = a.shape
    N = b.shape[1]
    assert M % bm == 0 and N % bn == 0 and (n_stack is None or bn * n_stack == N)
    dn = (((0,), (0,)), ((), ()))

    def body(a_ref, b_ref, o_ref):
        acc = lax.dot_general(a_ref[...], b_ref[...], dn, preferred_element_type=F32)
        o_ref[...] = acc.astype(o_ref.dtype)

    if n_stack is None:
        out_spec = pl.BlockSpec((bm, bn), lambda i, j: (i, j))
        out_shape = jax.ShapeDtypeStruct((M, N), out_dtype)
    else:
        out_spec = pl.BlockSpec((None, bm, bn), lambda i, j: (j, i, 0))
        out_shape = jax.ShapeDtypeStruct((n_stack, M, bn), out_dtype)
    limit = _vmem_limit(_nbytes((T, bm), a.dtype), _nbytes((T, bn), b.dtype), _nbytes((bm, bn), F32))
    return _call(body, name=name, grid=(M // bm, N // bn),
                 in_specs=[pl.BlockSpec((T, bm), lambda i, j: (0, i)), pl.BlockSpec((T, bn), lambda i, j: (0, j))],
                 out_specs=[out_spec], out_shape=[out_shape], args=[a, b], vmem_limit=limit, carry=carry)[0]


ROW_BLK = 256


def _rstd(x):
    return lax.rsqrt(jnp.mean(x * x, axis=-1, keepdims=True) + EPS)


def _norm_bwd(xhat, rstd, dxhat):
    return rstd * (dxhat - xhat * jnp.mean(dxhat * xhat, axis=-1, keepdims=True))


def _row_spec(cols):
    return pl.BlockSpec((ROW_BLK, cols), lambda i: (i, 0))


def _vec_spec(cols):
    return pl.BlockSpec((1, cols), lambda i: (0, 0))


def _prologue(x, g, rel_bias, idx_all, idx_all_t, *, name, carry=None):
    T, D = x.shape
    n_pat = idx_all.shape[0]
    heads = 2 * N_PAIRS
    steps = n_pat * heads
    rows = T // steps

    def tile(idxv, scalar):
        acc = jnp.where(idxv < 0, NEG, 0.0).astype(F32)
        for b in range(NUM_BUCKETS):
            acc = acc + jnp.where(idxv == b, scalar(b), 0.0)
        return acc

    def body(rb_ref, x_ref, g_ref, idx_ref, idxt_ref, n_ref, bias_ref, biast_ref):
        s = pl.program_id(0)
        head = jnp.where(s < heads, s, heads + s % heads)
        xv = x_ref[...]
        n_ref[...] = (xv * _rstd(xv) * g_ref[...]).astype(n_ref.dtype)
        acc = tile(idx_ref[...], lambda b: rb_ref[b, head])
        bias_ref[0] = acc
        kap = lax.broadcasted_iota(jnp.int32, (1, 2 * QBLK), 1)
        bias_ref[1] = jnp.where(kap < QBLK, NEG, acc)
        acc_t = tile(idxt_ref[...], lambda b: rb_ref[b, head])
        biast_ref[0] = acc_t
        kap_t = lax.broadcasted_iota(jnp.int32, (2 * QBLK, 1), 0)
        biast_ref[1] = jnp.where(kap_t < QBLK, NEG, acc_t)

    return _call(
        body, name=name, grid=(steps,),
        in_specs=[pl.BlockSpec(memory_space=pltpu.SMEM), pl.BlockSpec((rows, D), lambda s: (s, 0)),
                  pl.BlockSpec((1, D), lambda s: (0, 0)),
                  pl.BlockSpec((None, QBLK, 2 * QBLK), lambda s: (s // heads, 0, 0)),
                  pl.BlockSpec((None, 2 * QBLK, QBLK), lambda s: (s // heads, 0, 0))],
        out_specs=[pl.BlockSpec((rows, D), lambda s: (s, 0)),
                   pl.BlockSpec((None, 2, None, QBLK, 2 * QBLK), lambda s: (s // heads, 0, s % heads, 0, 0)),
                   pl.BlockSpec((None, 2, None, 2 * QBLK, QBLK),
                                lambda s: (s // heads, 0, (s % heads) // 2, 0, s % 2))],
        out_shape=[jax.ShapeDtypeStruct((T, D), BF16),
                   jax.ShapeDtypeStruct((n_pat, 2, heads, QBLK, 2 * QBLK), F32),
                   jax.ShapeDtypeStruct((n_pat, 2, N_PAIRS, 2 * QBLK, 2 * QBLK), F32)],
        args=[rel_bias, x, g, idx_all, idx_all_t], carry=carry)


def _mm_resid_norm(a, b, h, g_post, coef, g_next, *, bm, name, bias=None, carry=None):
    M, K = a.shape
    N = b.shape[1]

    def body(*refs):
        a_ref, b_ref, h_ref, gp_ref, gn_ref = refs[0:5]
        f_ref, hn_ref, n_ref = refs[-3:]
        for rows in _row_halves(bm):
            f = jnp.dot(a_ref[rows, :], b_ref[...], preferred_element_type=F32)
            if bias is not None:
                f = f + refs[5][...]
            f_ref[rows, :] = f
            hn = h_ref[rows, :] + coef * (f * _rstd(f) * gp_ref[...])
            hn_ref[rows, :] = hn
            n_ref[rows, :] = (hn * _rstd(hn) * gn_ref[...]).astype(n_ref.dtype)

    row = lambda w: pl.BlockSpec((bm, w), lambda i: (i, 0))
    vec = pl.BlockSpec((1, N), lambda i: (0, 0))
    in_specs = [row(K), pl.BlockSpec((K, N), lambda i: (0, 0), pipeline_mode=pl.Buffered(1)), row(N), vec, vec]
    args = [a, b, h, g_post, g_next]
    if bias is not None:
        in_specs.append(vec)
        args.append(bias)
    limit = _vmem_limit(_nbytes((bm, K), a.dtype), _nbytes((K, N), b.dtype) // 2, 4 * _nbytes((bm, N), F32))
    return _call(body, name=name, grid=(M // bm,), in_specs=in_specs, out_specs=[row(N), row(N), row(N)],
                 out_shape=[jax.ShapeDtypeStruct((M, N), F32), jax.ShapeDtypeStruct((M, N), F32),
                            jax.ShapeDtypeStruct((M, N), BF16)], args=args, vmem_limit=limit, carry=carry)


def _mm_nt_norms_bwd(a, b, dh_in, h, g_pre, f, g_post, coef, *, bm, name, add=None, b_is_kn=False, carry=None):
    M, K = a.shape
    stacked = b.ndim == 3
    N = b.shape[1] if (stacked or b_is_kn) else b.shape[0]
    n_sh = b.shape[0] if stacked else 1
    ks = b.shape[2] if stacked else K
    assert n_sh * ks == K
    dn_dims = (((1,), (0,)), ((), ())) if b_is_kn else (((1,), (1,)), ((), ()))
    with_f = f is not None
    n_in = 5 + (2 if with_f else 0) + (1 if add is not None else 0)

    def body(*refs):
        a_ref, b_ref, dhi_ref, h_ref, gpre_ref = refs[0:5]
        outs = refs[n_in:]

        @pl.when(pl.program_id(0) == 0)
        def _():
            for acc in (outs[2:] if with_f else outs[1:]):
                acc[...] = jnp.zeros_like(acc)

        for rows in _row_halves(bm):
            if stacked:
                dn = lax.dot_general(a_ref[rows, 0:ks], b_ref[0], dn_dims, preferred_element_type=F32)
                for s in range(1, n_sh):
                    dn = dn + lax.dot_general(a_ref[rows, s * ks:(s + 1) * ks], b_ref[s], dn_dims,
                                              preferred_element_type=F32)
            else:
                dn = lax.dot_general(a_ref[rows, :], b_ref[...], dn_dims, preferred_element_type=F32)
            if add is not None:
                dn = dn + refs[n_in - 1][rows, :]
            hv = h_ref[rows, :]
            r = _rstd(hv)
            hh = hv * r
            dh = dhi_ref[rows, :] + _norm_bwd(hh, r, dn * gpre_ref[...])
            outs[0][rows, :] = dh
            if not with_f:
                outs[1][...] += jnp.sum(dn * hh, axis=0, keepdims=True)
                continue
            f_ref, gpost_ref = refs[5], refs[6]
            df_ref, dgpre_ref, dgpost_ref, dsum_ref = outs[1:]
            dgpre_ref[...] += jnp.sum(dn * hh, axis=0, keepdims=True)
            fv = f_ref[rows, :]
            rf = _rstd(fv)
            fh = fv * rf
            dy = coef * dh
            dgpost_ref[...] += jnp.sum(dy * fh, axis=0, keepdims=True)
            df = _norm_bwd(fh, rf, dy * gpost_ref[...])
            dsum_ref[...] += jnp.sum(df, axis=0, keepdims=True)
            df_ref[rows, :] = df.astype(df_ref.dtype)

    row = lambda w: pl.BlockSpec((bm, w), lambda i: (i, 0))
    vec = pl.BlockSpec((1, N), lambda i: (0, 0))
    once = pl.Buffered(1)
    if stacked:
        b_spec = pl.BlockSpec((n_sh, N, ks), lambda i: (0, 0, 0), pipeline_mode=once)
    else:
        b_spec = pl.BlockSpec(b.shape, lambda i: (0, 0), pipeline_mode=once)
    in_specs = [row(K), b_spec, row(N), row(N), vec]
    args = [a, b, dh_in, h, g_pre]
    if with_f:
        in_specs += [row(N), vec]
        args += [f, g_post]
    if add is not None:
        in_specs.append(row(N))
        args.append(add)
    vec_shape = jax.ShapeDtypeStruct((1, N), F32)
    out_specs = [row(N)] + ([row(N), vec, vec, vec] if with_f else [vec])
    out_shape = [jax.ShapeDtypeStruct((M, N), F32)]
    out_shape += [jax.ShapeDtypeStruct((M, N), BF16), vec_shape, vec_shape, vec_shape] if with_f else [vec_shape]
    limit = _vmem_limit(_nbytes((bm, K), a.dtype), _nbytes((N, K), b.dtype) // 2, 6 * _nbytes((bm, N), F32))
    return _call(body, name=name, grid=(M // bm,), in_specs=in_specs, out_specs=out_specs, out_shape=out_shape,
                 args=args, vmem_limit=limit, semantics=("arbitrary",), carry=carry)


def _ffn_up(n, w_gu, *, bm, name, carry=None):
    M, K = n.shape
    n_sh, _, ns = w_gu.shape
    half = n_sh // 2

    def body(n_ref, wg_ref, wu_ref, g_ref, u_ref, a_ref):
        nv = n_ref[...]
        for cols in _col_chunks(ns):
            g = jnp.dot(nv, wg_ref[:, cols], preferred_element_type=F32)
            u = jnp.dot(nv, wu_ref[:, cols], preferred_element_type=F32)
            g_ref[:, cols] = g.astype(g_ref.dtype)
            u_ref[:, cols] = u.astype(u_ref.dtype)
            a_ref[:, cols] = (g * jax.nn.sigmoid(g) * u).astype(a_ref.dtype)

    out_spec = pl.BlockSpec((bm, ns), lambda i, j: (i, j))
    out = jax.ShapeDtypeStruct((M, half * ns), BF16)
    limit = _vmem_limit(_nbytes((bm, K), n.dtype), 2 * _nbytes((K, ns), w_gu.dtype), 3 * _nbytes((bm, ns), F32))
    return _call(body, name=name, grid=(M // bm, half),
                 in_specs=[pl.BlockSpec((bm, K), lambda i, j: (i, 0)),
                           pl.BlockSpec((None, K, ns), lambda i, j: (j, 0, 0)),
                           pl.BlockSpec((None, K, ns), lambda i, j: (j + half, 0, 0))],
                 out_specs=[out_spec, out_spec, out_spec], out_shape=[out, out, out], args=[n, w_gu, w_gu],
                 vmem_limit=limit, carry=carry)


def _ffn_down_bwd(df, w_down, g, u, *, bm, name, carry=None):
    M, D = df.shape
    F = w_down.shape[0]
    dn = (((1,), (1,)), ((), ()))

    def body(df_ref, w_ref, g_ref, u_ref, o_ref):
        dfv = df_ref[...]
        for cols in _col_chunks(F):
            da = lax.dot_general(dfv, w_ref[cols, :], dn, preferred_element_type=F32)
            gv = g_ref[:, cols].astype(F32)
            s = jax.nn.sigmoid(gv)
            o_ref[:, cols] = (da * u_ref[:, cols].astype(F32) * (s * (1.0 + gv * (1.0 - s)))).astype(o_ref.dtype)
            o_ref[:, F + cols.start:F + cols.stop] = (da * (gv * s)).astype(o_ref.dtype)

    row = lambda w: pl.BlockSpec((bm, w), lambda i: (i, 0))
    limit = _vmem_limit(_nbytes((F, D), w_down.dtype) // 2, 2 * _nbytes((bm, F), BF16), _nbytes((bm, 2 * F), BF16))
    return _call(body, name=name, grid=(M // bm,),
                 in_specs=[row(D), pl.BlockSpec((F, D), lambda i: (0, 0), pipeline_mode=pl.Buffered(1)), row(F), row(F)],
                 out_specs=[row(2 * F)], out_shape=[jax.ShapeDtypeStruct((M, 2 * F), BF16)],
                 args=[df, w_down, g, u], vmem_limit=limit, carry=carry)[0]


def _ple_head(n4, w_gate, p, w_ple, h3, g_post, target, *, bm, name):
    T, D = h3.shape
    kp = p.shape[1]

    def body(n_ref, wg_ref, p_ref, wp_ref, h_ref, g_ref, t_ref, dh_ref, de_ref, dgp_ref, loss_ref, dg_ref):
        @pl.when(pl.program_id(0) == 0)
        def _():
            loss_ref[...] = jnp.zeros_like(loss_ref)
            dg_ref[...] = jnp.zeros_like(dg_ref)

        gpost = g_ref[...]
        for rows in _row_halves(bm):
            gate = jax.nn.sigmoid(jnp.dot(n_ref[rows, :], wg_ref[...], preferred_element_type=F32))
            ev = jnp.dot(p_ref[rows, :], wp_ref[...], preferred_element_type=F32)
            ge = gate * ev
            r = _rstd(ge)
            gh = ge * r
            diff = h_ref[rows, :] + gh * gpost - t_ref[rows, :]
            loss_ref[...] += jnp.sum(diff * diff) * (0.5 / D)
            dh = diff * (1.0 / D)
            dh_ref[rows, :] = dh
            dg_ref[...] += jnp.sum(dh * gh, axis=0, keepdims=True)
            dge = _norm_bwd(gh, r, dh * gpost)
            de_ref[rows, :] = (dge * gate).astype(de_ref.dtype)
            dgp_ref[rows, :] = (dge * ev * gate * (1.0 - gate)).astype(dgp_ref.dtype)

    row = lambda w: pl.BlockSpec((bm, w), lambda i: (i, 0))
    whole = lambda a: pl.BlockSpec(a.shape, lambda i: (0, 0), pipeline_mode=pl.Buffered(1))
    limit = _vmem_limit(_nbytes((bm, D), BF16), _nbytes(w_gate.shape, w_gate.dtype) // 2, 6 * _nbytes((bm, D), F32))
    return pl.pallas_call(
        body, name=name, grid=(T // bm,),
        in_specs=[row(D), whole(w_gate), row(kp), whole(w_ple), row(D), _vec_spec(D), row(D)],
        out_specs=(row(D), row(D), row(D), _vec_spec(LANES), _vec_spec(D)),
        out_shape=(jax.ShapeDtypeStruct((T, D), F32), jax.ShapeDtypeStruct((T, D), BF16),
                   jax.ShapeDtypeStruct((T, D), BF16), jax.ShapeDtypeStruct((1, LANES), F32),
                   jax.ShapeDtypeStruct((1, D), F32)),
        compiler_params=pltpu.CompilerParams(dimension_semantics=("arbitrary",), vmem_limit_bytes=limit),
    )(n4, w_gate, p, w_ple, h3, g_post, target)


def _t5_index(max_dist, stride):
    rho = np.arange(QBLK)[:, None]
    kap = np.arange(2 * QBLK)[None, :]
    d = rho + QBLK - kap
    valid = (d >= 0) & (d <= max_dist)
    n = np.maximum(d, 0) * stride
    max_exact = NUM_BUCKETS // 2
    nf = np.maximum(n, 1).astype(np.float32)
    large = max_exact + (np.log(nf / np.float32(max_exact)) / np.float32(math.log(MAX_DISTANCE / max_exact))
                         * np.float32(NUM_BUCKETS - max_exact)).astype(np.int32)
    large = np.minimum(large, NUM_BUCKETS - 1)
    bucket = np.where(n < max_exact, n, large)
    return np.where(valid, bucket, -1).astype(np.int32)


def _bias_grad(d_bias_t, idx_t, *, name):
    def body(db_ref, idx_ref, o_ref):
        h = pl.program_id(0)

        @pl.when(h == 0)
        def _():
            o_ref[...] = jnp.zeros_like(o_ref)

        idxv = idx_ref[...]
        dv = db_ref[...]
        rows = lax.broadcasted_iota(jnp.int32, (NUM_BUCKETS, LANES), 0)
        lanes = lax.broadcasted_iota(jnp.int32, (NUM_BUCKETS, LANES), 1)
        acc = o_ref[...]
        for b in range(NUM_BUCKETS):
            s = jnp.sum(jnp.where(idxv == b, dv, 0.0))
            acc = acc + jnp.where((rows == b) & (lanes == h), s, 0.0)
        o_ref[...] = acc

    return pl.pallas_call(
        body, name=name, grid=(2 * N_PAIRS,),
        in_specs=[pl.BlockSpec((None, 2 * QBLK, QBLK), lambda h: (h // 2, 0, h % 2)),
                  pl.BlockSpec((2 * QBLK, QBLK), lambda h: (0, 0))],
        out_specs=pl.BlockSpec((NUM_BUCKETS, LANES), lambda h: (0, 0)),
        out_shape=jax.ShapeDtypeStruct((NUM_BUCKETS, LANES), F32),
        compiler_params=pltpu.CompilerParams(dimension_semantics=("arbitrary",)),
    )(d_bias_t, idx_t)


def _lane():
    return lax.broadcasted_iota(jnp.int32, (1, LANES), 1)


def _head_sel(h):
    return (_lane() < HEAD_DIM) if h == 0 else (_lane() >= HEAD_DIM)


def _stack_heads(t, scale=None):
    if scale is not None:
        t = t * scale
    zero = jnp.zeros_like(t)
    return jnp.concatenate([jnp.where(_head_sel(0), t, zero), jnp.where(_head_sel(1), t, zero)], axis=0)


def _class_spec(L, r, cols, stride, pps):
    chunks = N_PAIRS // pps
    mode = pl.Buffered(1) if r * chunks == 1 else None
    return pl.BlockSpec((L, MIX_W // chunks), lambda j, c: (0, (cols + j * stride) * chunks + c), pipeline_mode=mode)


def _rows(b, n_blocks=1):
    return pl.ds(pl.multiple_of(b * QBLK, QBLK), n_blocks * QBLK)


def _key_window(ref, b, cols, first):
    if first:
        blk = ref[0:QBLK, cols]
        return jnp.concatenate([blk, blk], axis=0)
    return ref[_rows(b - 1, 2), cols]


def _band_fwd(qa, ka, va, bias, *, r, q_col, k_col, v_col, stride, pattern, name, carry=None):
    L = qa.shape[0]
    nb = L // QBLK
    dn = (((1,), (1,)), ((), ()))
    scale = HEAD_DIM ** -0.5

    pps = N_PAIRS

    def body(q_ref, k_ref, v_ref, bias_ref, o_ref, lse_ref):
        sel0 = _head_sel(0)
        pair0 = pl.program_id(1) * pps

        def block(b, first):
            rows = _rows(b)
            for i in range(pps):
                cols = slice(i * LANES, (i + 1) * LANES)
                q2 = _stack_heads(q_ref[rows, cols], scale)
                k = _key_window(k_ref, b, cols, first)
                v = _key_window(v_ref, b, cols, first)
                bias2 = bias_ref[1 if first else 0, pl.ds(2 * (pair0 + i), 2)].reshape(2 * QBLK, 2 * QBLK)
                s = lax.dot_general(q2, k, dn, preferred_element_type=F32) + bias2
                m = jnp.max(s, axis=1, keepdims=True)
                p = jnp.exp(s - m)
                l = jnp.sum(p, axis=1, keepdims=True)
                o = jnp.dot(p.astype(v.dtype), v, preferred_element_type=F32) / l
                lse = m + jnp.log(l)
                o_ref[rows, cols] = jnp.where(sel0, o[0:QBLK], o[QBLK:]).astype(o_ref.dtype)
                lse_ref[rows, cols] = jnp.where(sel0, lse[0:QBLK], lse[QBLK:])

        block(0, True)
        if nb > 1:
            assert nb % 2 == 0
            block(1, False)

            @pl.loop(2, nb, step=2)
            def _(b):
                block(b, False)
                block(b + 1, False)

    bias_spec = pl.BlockSpec((None, 2, 2 * N_PAIRS, QBLK, 2 * QBLK), lambda j, c: (pattern, 0, 0, 0, 0))
    out_spec = _class_spec(L, r, 0, 1, pps)
    blk_bytes = _nbytes((L, pps * LANES), F32)
    return _call(
        body, name=name, grid=(r, N_PAIRS // pps),
        in_specs=[_class_spec(L, r, q_col, stride, pps), _class_spec(L, r, k_col, stride, pps),
                  _class_spec(L, r, v_col, stride, pps), bias_spec],
        out_specs=[out_spec, out_spec],
        out_shape=[jax.ShapeDtypeStruct((L, r * MIX_W), BF16), jax.ShapeDtypeStruct((L, r * MIX_W), F32)],
        args=[qa, ka, va, bias], vmem_limit=_vmem_limit(*([blk_bytes] * 4)), carry=carry)


def _class_rows_spec(rows, r, width):
    return pl.BlockSpec((rows // r, r * width), lambda i, *_: (i, 0))


def _token_scratch(rows, width):
    return pltpu.VMEM((width // LANES, rows, LANES), F32)


def _fill_tokens(scratch, value):
    for c in range(scratch.shape[0]):
        scratch[c] = value[:, c * LANES:(c + 1) * LANES]


def _read_tokens(scratch):
    return jnp.concatenate([scratch[c] for c in range(scratch.shape[0])], axis=1)


def _to_tokens(blk_ref, r, scratch):
    if r == 1:
        return blk_ref[...].astype(F32)
    nt, rows, _ = scratch.shape
    for j in range(r):
        for c in range(nt):
            lanes = slice((j * nt + c) * LANES, (j * nt + c + 1) * LANES)
            scratch.at[c][pl.ds(j, rows // r, stride=r), :] = blk_ref[:, lanes].astype(F32)
    return _read_tokens(scratch)


def _from_tokens(dst_ref, r, scratch):
    nt, rows, _ = scratch.shape
    if r == 1:
        dst_ref[...] = _read_tokens(scratch).astype(dst_ref.dtype)
        return
    for j in range(r):
        for c in range(nt):
            lanes = slice((j * nt + c) * LANES, (j * nt + c + 1) * LANES)
            dst_ref[:, lanes] = scratch.at[c][pl.ds(j, rows // r, stride=r), :].astype(dst_ref.dtype)


def _mm_nt_classes(a, b, bias, dils, *, bm, name):
    M, K = a.shape
    N = b.shape[0]
    dn = (((1,), (1,)), ((), ()))

    def body(a_ref, b_ref, bias_ref, *rest):
        outs, tile = rest[:-1], rest[-1]
        _fill_tokens(tile, lax.dot_general(a_ref[...], b_ref[...], dn, preferred_element_type=F32) + bias_ref[...])
        for out, r in zip(outs, dils):
            _from_tokens(out, r, tile)

    limit = _vmem_limit(_nbytes((bm, K), a.dtype), _nbytes((K, N), b.dtype) // 2, (1 + len(dils)) * _nbytes((bm, N), F32))
    return pl.pallas_call(
        body, name=name, grid=(M // bm,),
        in_specs=[pl.BlockSpec((bm, K), lambda i: (i, 0)),
                  pl.BlockSpec((N, K), lambda i: (0, 0), pipeline_mode=pl.Buffered(1)),
                  pl.BlockSpec((1, N), lambda i: (0, 0))],
        out_specs=tuple(_class_rows_spec(bm, r, N) for r in dils),
        out_shape=tuple(jax.ShapeDtypeStruct((M // r, r * N), BF16) for r in dils),
        scratch_shapes=[_token_scratch(bm, N)],
        compiler_params=pltpu.CompilerParams(dimension_semantics=("parallel",), vmem_limit_bytes=limit),
    )(a, b, bias)


def _merge(branches, dils, sink, *, name):
    W = MIX_W
    T = branches[0][0].shape[0] * dils[0]
    nbr = len(branches)

    def body(*refs):
        sink_ref = refs[2 * nbr]
        out_ref, outb_ref, lse_ref, scratch = refs[2 * nbr + 1:]
        sk = sink_ref[...]
        lses = [_to_tokens(refs[2 * t + 1], dils[t], scratch) for t in range(nbr)]
        mx = sk
        for lse in lses:
            mx = jnp.maximum(mx, lse)
        den = jnp.exp(sk - mx)
        num = jnp.zeros_like(mx)
        for t in range(nbr):
            w = jnp.exp(lses[t] - mx)
            den = den + w
            num = num + w * _to_tokens(refs[2 * t], dils[t], scratch)
        out = num / den
        out_ref[...] = out
        outb_ref[...] = out.astype(outb_ref.dtype)
        lse_ref[...] = mx + jnp.log(den)

    args = [a for br in branches for a in br] + [sink]
    in_specs = [_class_rows_spec(ROW_BLK, r, W) for r in dils for _ in range(2)] + [_vec_spec(W)]
    return pl.pallas_call(
        body, name=name, grid=(T // ROW_BLK,), in_specs=in_specs,
        out_specs=(_row_spec(W), _row_spec(W), _row_spec(W)),
        out_shape=(jax.ShapeDtypeStruct((T, W), F32), jax.ShapeDtypeStruct((T, W), BF16),
                   jax.ShapeDtypeStruct((T, W), F32)),
        scratch_shapes=[_token_scratch(ROW_BLK, W)],
        compiler_params=pltpu.CompilerParams(dimension_semantics=("parallel",)),
    )(*args)


def _attn_delta(dmix, outs, lses, sink, dils, *, name):
    T = dmix.shape[0]
    W = MIX_W
    nd = len(dils)
    heads = 2 * N_PAIRS
    dn = (((1,), (1,)), ((), ()))

    def per_head(select, tile):
        return lax.dot_general(select, tile, dn, precision=lax.Precision.HIGHEST, preferred_element_type=F32)

    def body(dmix_ref, oa_ref, ob_ref, la_ref, lb_ref, sink_ref, *rest):
        doa_ref, sta_ref, stb_ref, dsink_ref = rest[0:4]
        dob_refs, do_tile = rest[4:4 + nd], rest[4 + nd]

        @pl.when(pl.program_id(0) == 0)
        def _():
            dsink_ref[...] = jnp.zeros_like(dsink_ref)

        lane = lax.broadcasted_iota(jnp.int32, (heads, W), 1)
        head = lax.broadcasted_iota(jnp.int32, (heads, W), 0)
        head_lanes = (lane // HEAD_DIM == head).astype(F32)
        first_lane = (lane == head * HEAD_DIM).astype(F32)
        for mixer, (o_ref, l_ref, st_ref) in enumerate(((oa_ref, la_ref, sta_ref), (ob_ref, lb_ref, stb_ref))):
            do = dmix_ref[:, mixer * W:(mixer + 1) * W]
            delta = per_head(head_lanes, do * o_ref[...])
            lse = per_head(first_lane, l_ref[...])
            st_ref[0] = lse
            st_ref[1] = delta
            if mixer == 0:
                doa_ref[...] = do.astype(doa_ref.dtype)
                sink_col = jnp.sum(first_lane * sink_ref[...], axis=1, keepdims=True)
                dsink_ref[...] += -jnp.sum(jnp.exp(sink_col - lse) * delta, axis=1, keepdims=True)
            else:
                _fill_tokens(do_tile, do)
        for t, r in enumerate(dils):
            _from_tokens(dob_refs[t], r, do_tile)

    stat_spec = pl.BlockSpec((2, heads, ROW_BLK), lambda i: (0, 0, i))
    stat_shape = jax.ShapeDtypeStruct((2, heads, T), F32)
    out_shape = [jax.ShapeDtypeStruct((T, W), BF16), stat_shape, stat_shape, jax.ShapeDtypeStruct((heads, LANES), F32)]
    out_shape += [jax.ShapeDtypeStruct((T // r, r * W), BF16) for r in dils]
    res = pl.pallas_call(
        body, name=name, grid=(T // ROW_BLK,),
        in_specs=[_row_spec(2 * W), _row_spec(W), _row_spec(W), _row_spec(W), _row_spec(W), _vec_spec(W)],
        out_specs=tuple([_row_spec(W), stat_spec, stat_spec, pl.BlockSpec((heads, LANES), lambda i: (0, 0))]
                        + [_class_rows_spec(ROW_BLK, r, W) for r in dils]),
        out_shape=tuple(out_shape),
        scratch_shapes=[_token_scratch(ROW_BLK, W)],
        compiler_params=pltpu.CompilerParams(dimension_semantics=("arbitrary",)),
    )(dmix, outs[0], outs[1], lses[0], lses[1], sink)
    return res[0], res[1], res[2], res[3], res[4:]


def _band_bwd(qa, ka, va, d_out, stat, bias, *, r, q_col, k_col, v_col, stride, pattern, name, carry=None):
    L = qa.shape[0]
    nb = L // QBLK
    dn_nt = (((1,), (1,)), ((), ()))
    dn_tn = (((0,), (0,)), ((), ()))
    scale = HEAD_DIM ** -0.5

    pps = N_PAIRS if r > 1 else N_PAIRS // 2

    def body(q_ref, k_ref, v_ref, do_ref, st_ref, bias_ref, dq_ref, dk_ref, dv_ref, db_ref, dk_carry, dv_carry):
        @pl.when((pl.program_id(0) == 0) & (pl.program_id(1) == 0))
        def _():
            db_ref[...] = jnp.zeros_like(db_ref)

        sel0 = _head_sel(0)
        pair0 = pl.program_id(1) * pps

        def stat_row(which, pair, b):
            return jnp.concatenate([st_ref[which, 2 * pair + h, pl.ds(b, 1), :] for h in range(2)], axis=1)

        def block(b, first):
            rows = _rows(b)
            for i in range(pps):
                pair = pair0 + i
                cols = slice(i * LANES, (i + 1) * LANES)
                q2 = _stack_heads(q_ref[rows, cols], scale)
                k = _key_window(k_ref, b, cols, first)
                v = _key_window(v_ref, b, cols, first)
                do2 = _stack_heads(do_ref[rows, cols])
                s_t = lax.dot_general(k, q2, dn_nt, preferred_element_type=F32) + bias_ref[1 if first else 0, pair]
                p_t = jnp.exp(s_t - stat_row(0, pair, b))
                dp_t = lax.dot_general(v, do2, dn_nt, preferred_element_type=F32)
                ds_t = p_t * (dp_t - stat_row(1, pair, b))
                db_ref[pair] += ds_t
                dsb = ds_t.astype(k.dtype)
                dq2 = lax.dot_general(dsb, k, dn_tn, preferred_element_type=F32)
                dq_ref[rows, cols] = (jnp.where(sel0, dq2[0:QBLK], dq2[QBLK:]) * scale).astype(dq_ref.dtype)
                dk_acc = jnp.dot(dsb, q2, preferred_element_type=F32)
                dv_acc = jnp.dot(p_t.astype(k.dtype), do2, preferred_element_type=F32)
                if not first:
                    prev = _rows(b - 1)
                    dk_ref[prev, cols] = (dk_carry[:, cols] + dk_acc[0:QBLK]).astype(dk_ref.dtype)
                    dv_ref[prev, cols] = (dv_carry[:, cols] + dv_acc[0:QBLK]).astype(dv_ref.dtype)
                dk_carry[:, cols] = dk_acc[QBLK:2 * QBLK]
                dv_carry[:, cols] = dv_acc[QBLK:2 * QBLK]

        block(0, True)
        if nb > 1:
            assert nb % 2 == 0
            block(1, False)

            @pl.loop(2, nb, step=2)
            def _(b):
                block(b, False)
                block(b + 1, False)

        last = _rows(nb - 1)
        dk_ref[last, :] = dk_carry[...].astype(dk_ref.dtype)
        dv_ref[last, :] = dv_carry[...].astype(dv_ref.dtype)

    tok_spec = _class_spec(L, r, 0, 1, pps)
    stat_spec = pl.BlockSpec((2, 2 * N_PAIRS, None, nb, LANES), lambda j, c: (0, 0, j, 0, 0))
    bias_spec = pl.BlockSpec((None, 2, N_PAIRS, 2 * QBLK, 2 * QBLK), lambda j, c: (pattern, 0, 0, 0, 0))
    dbias_spec = pl.BlockSpec((N_PAIRS, 2 * QBLK, 2 * QBLK), lambda j, c: (0, 0, 0))
    grad = jax.ShapeDtypeStruct((L, r * MIX_W), BF16)
    blk_bytes = _nbytes((L, pps * LANES), BF16)
    carry_shape = pltpu.VMEM((QBLK, pps * LANES), F32)
    return _call(
        body, name=name, grid=(r, N_PAIRS // pps),
        in_specs=[_class_spec(L, r, q_col, stride, pps), _class_spec(L, r, k_col, stride, pps),
                  _class_spec(L, r, v_col, stride, pps), tok_spec, stat_spec, bias_spec],
        out_specs=[tok_spec, tok_spec, tok_spec, dbias_spec],
        out_shape=[grad, grad, grad, jax.ShapeDtypeStruct((N_PAIRS, 2 * QBLK, 2 * QBLK), F32)],
        args=[qa, ka, va, d_out, stat, bias], scratch=[carry_shape, carry_shape],
        vmem_limit=_vmem_limit(*([blk_bytes] * 9)), semantics=("arbitrary", "arbitrary"), carry=carry)


def _dz_assemble(dq_a, dk_a, dv_a, b_grads, dils, *, name):
    T = dq_a.shape[0]
    W = MIX_W

    def group_sum(x):
        u0 = x[:, 0:LANES] + x[:, LANES:2 * LANES]
        u1 = x[:, 2 * LANES:3 * LANES] + x[:, 3 * LANES:4 * LANES]
        s0 = u0 + pltpu.roll(u0, HEAD_DIM, 1)
        s1 = u1 + pltpu.roll(u1, HEAD_DIM, 1)
        return jnp.where(_head_sel(0), s0, s1)

    def body(*refs):
        dqa_ref, dka_ref, dva_ref = refs[0:3]
        b_refs = refs[3:12]
        dza_ref, dzb_ref, sa_ref, sb_ref, scratch = refs[12:]

        @pl.when(pl.program_id(0) == 0)
        def _():
            sa_ref[...] = jnp.zeros_like(sa_ref)
            sb_ref[...] = jnp.zeros_like(sb_ref)

        def put(dst, acc, col, part):
            w = part.shape[1]
            dst[:, col:col + w] = part.astype(dst.dtype)
            acc[:, col:col + w] += jnp.sum(part, axis=0, keepdims=True)

        put(dza_ref, sa_ref, 0, dqa_ref[...].astype(F32))
        put(dza_ref, sa_ref, W, group_sum(dka_ref[...].astype(F32)))
        put(dza_ref, sa_ref, W + LANES, group_sum(dva_ref[...].astype(F32)))
        for t in range(3):
            part = _to_tokens(b_refs[t], dils[0], scratch)
            for pat in range(1, len(dils)):
                part = part + _to_tokens(b_refs[3 * pat + t], dils[pat], scratch)
            put(dzb_ref, sb_ref, t * W, part)

    args = [dq_a, dk_a, dv_a] + [g[t] for g in b_grads for t in range(3)]
    return pl.pallas_call(
        body, name=name, grid=(T // ROW_BLK,),
        in_specs=[_row_spec(W)] * 3 + [_class_rows_spec(ROW_BLK, r, W) for r in dils for _ in range(3)],
        out_specs=(_row_spec(ZA_W), _row_spec(ZB_W), _vec_spec(ZA_W), _vec_spec(ZB_W)),
        out_shape=(jax.ShapeDtypeStruct((T, ZA_W), BF16), jax.ShapeDtypeStruct((T, ZB_W), BF16),
                   jax.ShapeDtypeStruct((1, ZA_W), F32), jax.ShapeDtypeStruct((1, ZB_W), F32)),
        scratch_shapes=[_token_scratch(ROW_BLK, W)],
        compiler_params=pltpu.CompilerParams(dimension_semantics=("arbitrary",)),
    )(*args)


def _place():
    x, y, c = lax.axis_index("x"), lax.axis_index("y"), lax.axis_index("c")
    chips = [(1 - x, y), (x, 1 - y), (1 - x, 1 - y)]
    return x, y, c, chips


def _cast_place(shard, k_idx, *, name):
    rs, cs = shard.shape
    rb = _row_block(rs, 512, 16)
    nblk = rs // rb

    def body(k_ref, s_ref, o_ref):
        o_ref[...] = s_ref[...].astype(o_ref.dtype)

    grid_spec = pltpu.PrefetchScalarGridSpec(
        num_scalar_prefetch=1, grid=(nblk,),
        in_specs=[pl.BlockSpec((rb, cs), lambda t, k_ref: (t, 0))],
        out_specs=pl.BlockSpec((rb, cs), lambda t, k_ref: (k_ref[0] * nblk + t, 0)))
    return pl.pallas_call(
        body, name=name, grid_spec=grid_spec, out_shape=jax.ShapeDtypeStruct((N_CHIPS * rs, cs), BF16),
        compiler_params=pltpu.CompilerParams(dimension_semantics=("parallel",)),
    )(k_idx, shard)


def _gather_carry(fulls, part=(0, 1)):
    n = len(fulls)
    p_idx, n_parts = part

    def piece(ref, chip_idx, half):
        rs = ref.shape[0] // N_CHIPS
        rh = rs // 2
        rows = rh // n_parts
        return ref.at[pl.ds(chip_idx * rs + half * rh + p_idx * rows, rows)]

    def copy(sems, sem, src_ref, dst_ref, to):
        return pltpu.make_async_remote_copy(src_ref=src_ref, dst_ref=dst_ref, send_sem=sems[0].at[sem],
                                            recv_sem=sems[1].at[sem], device_id=to, device_id_type=MESH)

    def ici_copy(ins, outs, sems, i, j, chip, k, c):
        return copy(sems, 3 * i + j, piece(ins[i], k, c), piece(outs[i], k, c), (*chip, c))

    def start(ins, outs, sems):
        x, y, c, chips = _place()
        for i in range(n):
            for j, chip in enumerate(chips):
                ici_copy(ins, outs, sems, i, j, chip, 2 * x + y, c).start()

    def finish(ins, outs, sems):
        x, y, c, chips = _place()
        sibling = (x, y, 1 - c)
        passed = []
        for i in range(n):
            for j, chip in enumerate(chips):
                region = piece(outs[i], 2 * chip[0] + chip[1], c)
                copy(sems, 3 * i + j, region, region, (*chip, c)).wait_recv()
                cp = copy(sems, 3 * n + 3 * i + j, region, region, sibling)
                cp.start()
                passed.append(cp)
        for i in range(n):
            for j, chip in enumerate(chips):
                region = piece(outs[i], 2 * chip[0] + chip[1], 1 - c)
                copy(sems, 3 * n + 3 * i + j, region, region, sibling).wait_recv()
        for i in range(n):
            for j, chip in enumerate(chips):
                ici_copy(ins, outs, sems, i, j, chip, 2 * x + y, c).wait_send()
        for cp in passed:
            cp.wait_send()

    return _Carry(fulls, [jax.ShapeDtypeStruct(a.shape, a.dtype) for a in fulls], {i: i for i in range(n)},
                  [pltpu.SemaphoreType.DMA((6 * n,)), pltpu.SemaphoreType.DMA((6 * n,))], start, finish)


def _pair_swap_carry(grads):
    n = len(grads)

    def copy(ins, outs, sems, i):
        x, y, c, _ = _place()
        return pltpu.make_async_remote_copy(src_ref=ins[i].at[:, 1 - c], dst_ref=outs[i], send_sem=sems[0].at[i],
                                            recv_sem=sems[1].at[i], device_id=(x, y, 1 - c), device_id_type=MESH)

    def start(ins, outs, sems):
        for i in range(n):
            copy(ins, outs, sems, i).start()

    def finish(ins, outs, sems):
        for i in range(n):
            copy(ins, outs, sems, i).wait()

    return _Carry(grads, [jax.ShapeDtypeStruct((a.shape[0], a.shape[2], a.shape[3]), a.dtype) for a in grads], {},
                  [pltpu.SemaphoreType.DMA((n,)), pltpu.SemaphoreType.DMA((n,))], start, finish)


def _pair_sum(g4, got, c_idx, *, name):
    _, _, rh, cs = g4.shape
    rb = _row_block(rh, 512, 16)

    def body(c_ref, own_ref, got_ref, o_ref):
        o_ref[...] = (own_ref[...].astype(F32) + got_ref[...].astype(F32)).astype(o_ref.dtype)

    grid_spec = pltpu.PrefetchScalarGridSpec(
        num_scalar_prefetch=1, grid=(N_CHIPS, rh // rb),
        in_specs=[pl.BlockSpec((None, None, rb, cs), lambda k, t, c_ref: (k, c_ref[0], t, 0)),
                  pl.BlockSpec((None, rb, cs), lambda k, t, c_ref: (k, t, 0))],
        out_specs=pl.BlockSpec((None, rb, cs), lambda k, t, c_ref: (k, t, 0)))
    return pl.pallas_call(
        body, name=name, grid_spec=grid_spec, out_shape=jax.ShapeDtypeStruct((N_CHIPS, rh, cs), BF16),
        compiler_params=pltpu.CompilerParams(dimension_semantics=("parallel", "parallel")),
    )(c_idx, g4, got)


def _chip_scatter_carry(sums):
    n = len(sums)

    def copies(ins, outs, sems):
        x, y, c, chips = _place()
        return [pltpu.make_async_remote_copy(src_ref=ins[i].at[2 * chip[0] + chip[1]], dst_ref=outs[i].at[j],
                                             send_sem=sems[0].at[3 * i + j], recv_sem=sems[1].at[3 * i + j],
                                             device_id=(*chip, c), device_id_type=MESH)
                for i in range(n) for j, chip in enumerate(chips)]

    def start(ins, outs, sems):
        for cp in copies(ins, outs, sems):
            cp.start()

    def finish(ins, outs, sems):
        for cp in copies(ins, outs, sems):
            cp.wait()

    return _Carry(sums, [jax.ShapeDtypeStruct((3,) + a.shape[1:], a.dtype) for a in sums], {},
                  [pltpu.SemaphoreType.DMA((3 * n,)), pltpu.SemaphoreType.DMA((3 * n,))], start, finish)


def _chip_sum(sums, got, kc_idx, *, name):
    _, rh, cs = sums.shape
    rb = _row_block(rh, 512, 16)
    nblk = rh // rb

    def body(kc_ref, own_ref, got_ref, o_ref):
        acc = own_ref[...].astype(F32)
        for j in range(3):
            acc = acc + got_ref[j].astype(F32)
        o_ref[...] = acc

    grid_spec = pltpu.PrefetchScalarGridSpec(
        num_scalar_prefetch=1, grid=(nblk,),
        in_specs=[pl.BlockSpec((None, rb, cs), lambda t, kc: (kc[0], t, 0)),
                  pl.BlockSpec((3, rb, cs), lambda t, kc: (0, t, 0))],
        out_specs=pl.BlockSpec((rb, cs), lambda t, kc: (kc[1] * nblk + t, 0)))
    return pl.pallas_call(
        body, name=name, grid_spec=grid_spec, out_shape=jax.ShapeDtypeStruct((2 * rh, cs), F32),
        compiler_params=pltpu.CompilerParams(dimension_semantics=("parallel",)),
    )(kc_idx, sums, got)


def _half_swap_carry(shards):
    n = len(shards)

    def copy(ins, outs, sems, i, to_my_half):
        x, y, c, _ = _place()
        rh = ins[i].shape[0] // 2
        half = c if to_my_half else 1 - c
        return pltpu.make_async_remote_copy(
            src_ref=ins[i].at[pl.ds(c * rh, rh)], dst_ref=outs[i].at[pl.ds(half * rh, rh)],
            send_sem=sems[0].at[i], recv_sem=sems[1].at[i], device_id=(x, y, 1 - c), device_id_type=MESH)

    def start(ins, outs, sems):
        for i in range(n):
            copy(ins, outs, sems, i, True).start()

    def finish(ins, outs, sems):
        for i in range(n):
            copy(ins, outs, sems, i, False).wait_recv()
        for i in range(n):
            copy(ins, outs, sems, i, True).wait_send()

    return _Carry(shards, [jax.ShapeDtypeStruct(a.shape, a.dtype) for a in shards], {i: i for i in range(n)},
                  [pltpu.SemaphoreType.DMA((n,)), pltpu.SemaphoreType.DMA((n,))], start, finish)


SMALL_ROW = 1024


def _small_layout(shapes):
    starts, row = [], 0
    for r, c in shapes:
        starts.append(row)
        row += -(-c // SMALL_ROW) if r == 1 else r
    return starts, -(-row // SUBLANES) * SUBLANES


def _small_pieces(shape, start):
    r, c = shape
    if r == 1:
        return [(start + q, min(SMALL_ROW, c - q * SMALL_ROW), q * SMALL_ROW) for q in range(-(-c // SMALL_ROW))]
    return None


def _small_all_reduce(parts, *, name):
    shapes = [a.shape for a in parts]
    starts, rows = _small_layout(shapes)
    n = len(parts)

    def body(*refs):
        ins, outs = refs[0:n], refs[n:2 * n]
        buf, send_sems, recv_sems = refs[2 * n:]
        x, y, c, _ = _place()
        me = 4 * x + 2 * y + c
        mine = buf.at[me]
        mine[...] = jnp.zeros((rows, SMALL_ROW), F32)
        for ref, shape, start in zip(ins, shapes, starts):
            pieces = _small_pieces(shape, start)
            if pieces is None:
                mine[start:start + shape[0], 0:shape[1]] = ref[...]
            else:
                for row, width, col in pieces:
                    mine[row:row + 1, 0:width] = ref[:, col:col + width]
        copies = []
        for d in range(1, 8):
            to = (1 - x if d & 4 else x, 1 - y if d & 2 else y, 1 - c if d & 1 else c)
            cp = pltpu.make_async_remote_copy(src_ref=mine, dst_ref=mine, send_sem=send_sems.at[d - 1],
                                              recv_sem=recv_sems.at[d - 1], device_id=to, device_id_type=MESH)
            cp.start()
            copies.append((cp, to))
        for d in range(1, 8):
            to = copies[d - 1][1]
            sender = 4 * to[0] + 2 * to[1] + to[2]
            pltpu.make_async_remote_copy(src_ref=mine, dst_ref=buf.at[sender], send_sem=send_sems.at[d - 1],
                                         recv_sem=recv_sems.at[d - 1], device_id=to, device_id_type=MESH).wait_recv()
        for cp, _ in copies:
            cp.wait_send()
        acc = buf[0]
        for s in range(1, 8):
            acc = acc + buf[s]
        mine[...] = acc
        for ref, shape, start in zip(outs, shapes, starts):
            pieces = _small_pieces(shape, start)
            if pieces is None:
                ref[...] = mine[start:start + shape[0], 0:shape[1]]
            else:
                for row, width, col in pieces:
                    ref[:, col:col + width] = mine[row:row + 1, 0:width]

    vm = pl.BlockSpec(memory_space=pltpu.VMEM)
    return pl.pallas_call(
        body, name=name, in_specs=[vm] * n, out_specs=tuple([vm] * n),
        out_shape=tuple(jax.ShapeDtypeStruct(s, F32) for s in shapes),
        scratch_shapes=[pltpu.VMEM((8, rows, SMALL_ROW), F32), pltpu.SemaphoreType.DMA((7,)),
                        pltpu.SemaphoreType.DMA((7,))],
    )(*parts)


def _adamw_small(ws, gs, ms, vs, *, name):
    n = len(ws)
    c1 = 1.0 - ADAM_B1 ** ADAM_STEP
    c2 = 1.0 - ADAM_B2 ** ADAM_STEP

    def body(*refs):
        for k in range(n):
            w_ref, g_ref, m_ref, v_ref = (refs[t * n + k] for t in range(4))
            go_ref, d_ref, mo_ref, vo_ref = (refs[(4 + t) * n + k] for t in range(4))
            gv = g_ref[...]
            go_ref[...] = gv
            mn = ADAM_B1 * m_ref[...] + (1.0 - ADAM_B1) * gv
            vn = ADAM_B2 * v_ref[...] + (1.0 - ADAM_B2) * (gv * gv)
            d_ref[...] = -ADAM_LR * ((mn / c1) / (jnp.sqrt(vn / c2) + ADAM_EPS) + ADAM_WD * w_ref[...])
            mo_ref[...] = mn
            vo_ref[...] = vn

    vm = pl.BlockSpec(memory_space=pltpu.VMEM)
    shapes = tuple(jax.ShapeDtypeStruct(a.shape, F32) for a in ws)
    res = pl.pallas_call(
        body, name=name, in_specs=[vm] * (4 * n), out_specs=tuple([vm] * (4 * n)), out_shape=shapes * 4,
    )(*ws, *gs, *ms, *vs)
    return res[0:n], res[n:2 * n], res[2 * n:3 * n], res[3 * n:4 * n]


def _adamw(w, g, m, v, *, name):
    R, C = w.shape
    rb = _row_block(R, 256, SUBLANES)
    c1 = 1.0 - ADAM_B1 ** ADAM_STEP
    c2 = 1.0 - ADAM_B2 ** ADAM_STEP

    def body(w_ref, g_ref, m_ref, v_ref, go_ref, d_ref, mo_ref, vo_ref):
        gv = g_ref[...]
        go_ref[...] = gv
        mn = ADAM_B1 * m_ref[...] + (1.0 - ADAM_B1) * gv
        vn = ADAM_B2 * v_ref[...] + (1.0 - ADAM_B2) * (gv * gv)
        d_ref[...] = -ADAM_LR * ((mn / c1) / (jnp.sqrt(vn / c2) + ADAM_EPS) + ADAM_WD * w_ref[...])
        mo_ref[...] = mn
        vo_ref[...] = vn

    spec = pl.BlockSpec((rb, C), lambda i: (i, 0))
    shp = jax.ShapeDtypeStruct((R, C), F32)
    return pl.pallas_call(
        body, name=name, grid=(R // rb,), in_specs=[spec] * 4, out_specs=(spec,) * 4,
        out_shape=(shp,) * 4, compiler_params=pltpu.CompilerParams(dimension_semantics=("parallel",)),
    )(w, g, m, v)


def _local_step(x, p, target, small, sched):
    T = x.shape[0]
    W = MIX_W
    rel_bias = small["rel_bias"]
    b_in_a, b_in_b = small["b_in"][:, 0:ZA_W], small["b_in"][:, ZA_W:]

    idx_np = [_t5_index(A_WINDOW - 1, 1)] + [_t5_index(window // dil, dil) for window, dil in B_PATTERNS]
    idx_all = jnp.asarray(np.stack(idx_np))
    idx_t = [jnp.asarray(np.ascontiguousarray(i.T)) for i in idx_np]
    n1, bias_all, bias_all_t = sched.run(_prologue, x, small["ffn1_pre_g"], rel_bias, idx_all, jnp.stack(idx_t),
                                         name="prologue")
    w_gu1 = sched.weight("ffn1_w_gu")
    *gu1, a1 = sched.run(_ffn_up, n1, w_gu1, bm=512, name="ffn1_gu")
    w_d1 = sched.weight("ffn1_w_down")
    f1, h1, n2 = sched.run(_mm_resid_norm, a1, w_d1, x, small["ffn1_post_g"], 0.5, small["attn_pre_g"], bm=512,
                           name="ffn1_down")
    win_a, win_b = sched.weight("w_in")
    za = _mm_nt(n2, win_a, bm=1024, bn=ZA_W, out_dtype=BF16, name="in_proj_a", add=b_in_a)
    dils = tuple(dil for _, dil in B_PATTERNS)
    zb_by_dil = _mm_nt_classes(n2, win_b, b_in_b, dils, bm=512, name="in_proj_b")

    k_a = za[:, W:W + LANES].reshape(T, 2, 1, HEAD_DIM)
    v_a = za[:, W + LANES:W + 2 * LANES].reshape(T, 2, 1, HEAD_DIM)
    k_a = jnp.broadcast_to(k_a, (T, 2, 4, HEAD_DIM)).reshape(T, W)
    v_a = jnp.broadcast_to(v_a, (T, 2, 4, HEAD_DIM)).reshape(T, W)
    a_args = dict(r=1, q_col=0, k_col=0, v_col=0, stride=0, pattern=0)
    o_a, lse_a1 = sched.run(_band_fwd, za, k_a, v_a, bias_all, name="band_a_fwd", **a_args)
    sink_row = jnp.repeat(small["sinks"], HEAD_DIM, axis=1)
    out_a, out_a_bf, lse_a = _merge([(o_a, lse_a1)], (1,), sink_row, name="merge_a")

    no_sink = jnp.full((1, W), NEG, F32)
    b_ctx, branches = [], []
    for t, dil in enumerate(dils):
        zb_r = zb_by_dil[t]
        args = dict(r=dil, q_col=0, k_col=1, v_col=2, stride=ZB_W // W, pattern=1 + t)
        branches.append(sched.run(_band_fwd, zb_r, zb_r, zb_r, bias_all, name=f"band_b{t}_fwd", **args))
        b_ctx.append((idx_t[1 + t], zb_r, args))
    out_b, out_b_bf, lse_b = _merge(branches, dils, no_sink, name="merge_b")

    mix = jnp.concatenate([out_a_bf, out_b_bf], axis=1)
    w_out = sched.weight("w_out")
    att, h2, n3 = _mm_resid_norm(mix, w_out, h1, small["attn_post_g"], 1.0, small["ffn2_pre_g"], bm=512,
                                 name="out_proj", bias=small["b_out"])
    w_gu2, w_d2 = sched.weight("ffn2_w_gu"), sched.weight("ffn2_w_down")
    *gu2, a2 = _ffn_up(n3, w_gu2, bm=512, name="ffn2_gu")
    f2, h3, n4 = _mm_resid_norm(a2, w_d2, h2, small["ffn2_post_g"], 0.5, small["ple_pre_g"], bm=512,
                                name="ffn2_down")
    w_gate = sched.weight("w_ple_gate")
    p_bf = p.astype(BF16)
    dh4, de, dgp, loss, d_ple_post = _ple_head(n4, w_gate, p_bf, sched.weight("w_ple_proj"), h3, small["ple_post_g"],
                                               target, bm=512, name="ple_head")

    gs = {"ple_post_g": d_ple_post}
    sched.grad("w_ple_proj", _mm_tn(p_bf, de, bm=256, bn=1024, out_dtype=BF16, name="d_w_ple"))
    sched.grad("w_ple_gate", _mm_tn(n4, dgp, bm=512, bn=1024, out_dtype=BF16, name="d_w_gate"))
    dh3, df2, gs["ple_pre_g"], gs["ffn2_post_g"], _ = sched.run(
        _mm_nt_norms_bwd, dgp, w_gate, dh4, h3, small["ple_pre_g"], f2, small["ffn2_post_g"], 0.5, bm=512, name="d_n4")

    def ffn_bwd(df, a, gu, n, w_down, w_gu, tag, *norm_args):
        dgu = sched.run(_ffn_down_bwd, df, w_down, gu[0], gu[1], bm=512, name=f"ffn{tag}_d_a")
        sched.grad(f"ffn{tag}_w_gu", _mm_tn(n, dgu, bm=512, bn=1408, out_dtype=BF16, name=f"ffn{tag}_d_w_gu",
                                            n_stack=N_CHIPS))
        sched.grad(f"ffn{tag}_w_down", sched.run(_mm_tn, a, df, bm=256, bn=1024, out_dtype=BF16,
                                                name=f"ffn{tag}_d_w_down"))
        return sched.run(_mm_nt_norms_bwd, dgu, w_gu, *norm_args, bm=512, name=f"ffn{tag}_d_n")

    dh2, datt, gs["ffn2_pre_g"], gs["attn_post_g"], gs["b_out"] = ffn_bwd(
        df2, a2, gu2, n3, w_d2, w_gu2, "2", dh3, h2, small["ffn2_pre_g"], att, small["attn_post_g"], 1.0)
    sched.grad("w_out", _mm_tn(mix, datt, bm=512, bn=1024, out_dtype=BF16, name="d_w_out"))
    dmix = sched.run(_mm_nt, datt, w_out, bm=1024, bn=1024, out_dtype=F32, name="d_mix")

    do_a, stat_a, stat_b, dsink, do_b = _attn_delta(dmix, (out_a, out_b), (lse_a, lse_b), sink_row, dils,
                                                    name="attn_delta")
    gs["sinks"] = dsink[:, 0].reshape(1, 2 * N_PAIRS)

    def class_stats(st, r):
        per_class = jnp.transpose(st.reshape(2, 2 * N_PAIRS, T // r, r), (0, 1, 3, 2))
        return per_class.reshape(2, 2 * N_PAIRS, r, T // (r * QBLK), QBLK)

    dq_a, dk_a, dv_a, dbias_a = sched.run(_band_bwd, za, k_a, v_a, do_a, class_stats(stat_a, 1), bias_all_t,
                                          name="band_a_bwd", **a_args)
    d_rel_a = _bias_grad(dbias_a, idx_t[0], name="d_rel_a")
    b_grads = []
    d_rel_b = None
    for t, (idx, zb_r, args) in enumerate(b_ctx):
        dq, dk, dv, dbias = sched.run(_band_bwd, zb_r, zb_r, zb_r, do_b[t], class_stats(stat_b, dils[t]), bias_all_t,
                                      name=f"band_b{t}_bwd", **args)
        b_grads.append((dq, dk, dv))
        d_rel = _bias_grad(dbias, idx, name=f"d_rel_b{t}")
        d_rel_b = d_rel if d_rel_b is None else d_rel_b + d_rel
    gs["rel_bias"] = jnp.concatenate([d_rel_a[:, 0:2 * N_PAIRS], d_rel_b[:, 0:2 * N_PAIRS]], axis=1)
    dza, dzb, dsum_a, dsum_b = _dz_assemble(dq_a, dk_a, dv_a, b_grads, dils, name="d_z")
    gs["b_in"] = jnp.concatenate([dsum_a, dsum_b], axis=1)
    sched.grad("w_in", (_mm_tn(dza, n2, bm=ZA_W, bn=1024, out_dtype=BF16, name="d_w_in_a"),
                        _mm_tn(dzb, n2, bm=ZB_W // 2, bn=1024, out_dtype=BF16, name="d_w_in_b")))
    dn2_a = _mm_nn(dza, win_a, bm=1024, bn=1024, out_dtype=F32, name="d_n2_a")
    dh1, df1, gs["attn_pre_g"], gs["ffn1_post_g"], _ = sched.run(
        _mm_nt_norms_bwd, dzb, win_b, dh2, h1, small["attn_pre_g"], f1, small["ffn1_post_g"], 0.5, bm=512,
        name="d_n2_b", add=dn2_a, b_is_kn=True)
    grad_x, gs["ffn1_pre_g"] = ffn_bwd(df1, a1, gu1, n1, w_d1, w_gu1, "1", dh1, x, small["ffn1_pre_g"], None, None, 0.0)
    return loss, grad_x, gs


def _to_compute(name, full):
    st = full.reshape(N_CHIPS, full.shape[0] // N_CHIPS, full.shape[1])
    if name in ("ffn1_w_gu", "ffn2_w_gu"):
        return st
    if name == "w_ple_proj":
        return jnp.transpose(st, (1, 0, 2)).reshape(st.shape[1], -1)
    if name == "w_in":
        return full[0:ZA_W], full[ZA_W:]
    return full


def _to_comm(name, g):
    if name == "w_in":
        g = jnp.concatenate(g, axis=0)
    if name == "w_ple_proj":
        g = jnp.transpose(g.reshape(g.shape[0], N_CHIPS, -1), (1, 0, 2))
    rs = g.shape[1] if g.ndim == 3 else g.shape[0] // N_CHIPS
    return g.reshape(N_CHIPS, 2, rs // 2, g.shape[-1])


class _Schedule:
    GATHER = {
        "prologue": [("ffn1_w_gu", (0, 1))],
        "ffn1_gu": [("ffn1_w_down", (0, 1)), ("w_in", (0, 1))],
        "ffn1_down": [("w_out", (0, 1))],
        "band_a_fwd": [("ffn2_w_gu", (0, 2))],
        "band_b0_fwd": [("ffn2_w_gu", (1, 2))],
        "band_b1_fwd": [("ffn2_w_down", (0, 1))],
        "band_b2_fwd": [("w_ple_gate", (0, 1)), ("w_ple_proj", (0, 1))],
    }
    SWAP = {"d_n4": ["w_ple_proj", "w_ple_gate"], "ffn2_d_w_down": ["ffn2_w_gu"], "ffn2_d_n": ["ffn2_w_down"],
            "d_mix": ["w_out"], "d_n2_b": ["w_in"], "ffn1_d_w_down": ["ffn1_w_gu"], "ffn1_d_n": ["ffn1_w_down"]}
    SCATTER = {"ffn2_d_a": ["w_ple_proj", "w_ple_gate"], "ffn2_d_n": ["ffn2_w_gu"], "band_a_bwd": ["ffn2_w_down"],
               "band_b0_bwd": ["w_out"], "ffn1_d_a": ["w_in"], "ffn1_d_n": ["ffn1_w_gu"]}

    def __init__(self, placed, c_idx, kc_idx):
        self.full = dict(placed)
        self.missing = {n: 1.0 for n in placed}
        self.c_idx, self.kc_idx = c_idx, kc_idx
        self.grads, self.swapped, self.pair_sums, self.scattered = {}, {}, {}, {}

    def _gather(self, entries):
        carries, after = [], []
        for part in sorted({pt for _, pt in entries}):
            names = [n for n, pt in entries if pt == part]
            carry = _gather_carry([self.full[n] for n in names], part)
            carries.append(carry)

            def done(carry=carry, names=names, part=part):
                for n, a in zip(names, carry.results):
                    self.full[n] = a
                    self.missing[n] -= 1.0 / part[1]
            after.append(done)
        return carries, after

    def weight(self, name):
        assert abs(self.missing[name]) < 1e-9, (name, self.missing[name])
        return _to_compute(name, self.full[name])

    def grad(self, name, g):
        self.grads[name] = _to_comm(name, g)

    def _swap(self, names):
        carry = _pair_swap_carry([self.grads[n] for n in names])

        def done():
            self.swapped.update(zip(names, carry.results))
        return carry, done

    def _scatter(self, names):
        for n in names:
            self.pair_sums[n] = _pair_sum(self.grads[n], self.swapped[n], self.c_idx, name=f"grad_pair_sum_{n}")
        carry = _chip_scatter_carry([self.pair_sums[n] for n in names])

        def done():
            self.scattered.update(zip(names, carry.results))
        return carry, done

    def run(self, fn, *args, name, **kw):
        carries, after = self._gather(self.GATHER.get(name, []))
        for names, make in ((self.SWAP.get(name), self._swap), (self.SCATTER.get(name), self._scatter)):
            if names:
                carry, done = make(names)
                carries.append(carry)
                after.append(done)
        out = fn(*args, name=name, carry=_join(carries), **kw)
        for done in after:
            done()
        return out

    def finish(self):
        left = [n for n in BIG if n not in self.scattered]
        to_swap = [n for n in left if n not in self.swapped]
        if to_swap:
            carry, done = self._swap(to_swap)
            _comm_call(carry, name="grad_pair_swap")
            done()
        if left:
            carry, done = self._scatter(left)
            _comm_call(carry, name="grad_chip_scatter")
            done()
        halves = [_chip_sum(self.pair_sums[n], self.scattered[n], self.kc_idx, name=f"grad_chip_sum_{n}") for n in BIG]
        carry = _half_swap_carry(halves)
        _comm_call(carry, name="grad_half_swap")
        return dict(zip(BIG, carry.results))


def kernel(x, p, rel_bias, ffn1_pre_g, ffn1_w_gu, ffn1_w_down, ffn1_post_g, attn_pre_g, w_in, b_in, sinks, w_out, b_out, attn_post_g, ffn2_pre_g, ffn2_w_gu, ffn2_w_down, ffn2_post_g, ple_pre_g, w_ple_gate, w_ple_proj, ple_post_g, loss_target, m_rel_bias, m_ffn1_pre_g, m_ffn1_w_gu, m_ffn1_w_down, m_ffn1_post_g, m_attn_pre_g, m_w_in, m_b_in, m_sinks, m_w_out, m_b_out, m_attn_post_g, m_ffn2_pre_g, m_ffn2_w_gu, m_ffn2_w_down, m_ffn2_post_g, m_ple_pre_g, m_w_ple_gate, m_w_ple_proj, m_ple_post_g, v_rel_bias, v_ffn1_pre_g, v_ffn1_w_gu, v_ffn1_w_down, v_ffn1_post_g, v_attn_pre_g, v_w_in, v_b_in, v_sinks, v_w_out, v_b_out, v_attn_post_g, v_ffn2_pre_g, v_ffn2_w_gu, v_ffn2_w_down, v_ffn2_post_g, v_ple_pre_g, v_w_ple_gate, v_w_ple_proj, v_ple_post_g):
    given = dict(locals())
    w = {n: given[n] for n in WEIGHTS}
    m = {n: given["m_" + n] for n in WEIGHTS}
    v = {n: given["v_" + n] for n in WEIGHTS}
    c_pos = lax.axis_index("c").astype(jnp.int32)
    k_pos = (2 * lax.axis_index("x") + lax.axis_index("y")).astype(jnp.int32)
    c_idx, k_idx, kc_idx = c_pos.reshape(1), k_pos.reshape(1), jnp.stack([k_pos, c_pos])

    def shard(a, n):
        return jnp.transpose(a[0]) if n == "w_in" else a[0]

    def unshard(a, n):
        return (jnp.transpose(a) if n == "w_in" else a)[None]

    sched = _Schedule({n: _cast_place(shard(w[n], n), k_idx, name=f"place_{n}") for n in BIG}, c_idx, kc_idx)
    small = {n: (w[n] if n == "rel_bias" else w[n].reshape(1, -1)) for n in SMALL}

    loss_part, grad_x, gs = _local_step(x[0], p[0, 0], loss_target[0], small, sched)

    grads_big = sched.finish()

    *small_grads, loss_row = _small_all_reduce([gs[n] for n in SMALL] + [loss_part], name="small_all_reduce")
    loss = loss_row[0, 0]

    grads, delta, new_m, new_v = {}, {}, {}, {}
    for n in BIG:
        res = _adamw(shard(w[n], n), grads_big[n], shard(m[n], n), shard(v[n], n), name=f"adamw_{n}")
        grads[n], delta[n], new_m[n], new_v[n] = (unshard(a, n) for a in res)
    res = _adamw_small([w[n] for n in SMALL], small_grads, [m[n] for n in SMALL], [v[n] for n in SMALL],
                       name="adamw_small")
    for name, gg, dd, mm, vv in zip(SMALL, *res):
        grads[name], delta[name], new_m[name], new_v[name] = gg, dd, mm, vv

    return (loss, grad_x[None], *[grads[n] for n in WEIGHTS], *[delta[n] for n in WEIGHTS],
            *[new_m[n] for n in WEIGHTS], *[new_v[n] for n in WEIGHTS])
```

```python
import math

import jax
import jax.numpy as jnp
import numpy as np
from jax import lax
from jax.experimental import pallas as pl
from jax.experimental.pallas import tpu as pltpu

F32 = jnp.float32
BF16 = jnp.bfloat16
MESH = pl.DeviceIdType.MESH

EPS = 1e-6
NEG = -1e30
LANES = 128
SUBLANES = 8
HEAD_DIM = 64
QBLK = 128
N_PAIRS = 4
MIX_W = N_PAIRS * LANES
A_WINDOW = 128
B_PATTERNS = ((128, 1), (512, 4), (2048, 16))
NUM_BUCKETS = 32
MAX_DISTANCE = 2048
ZA_W = 768
ZB_W = 1536
N_CHIPS = 4
VMEM_BYTES_V7X = 64 * 2**20

ADAM_LR, ADAM_B1, ADAM_B2, ADAM_EPS, ADAM_WD, ADAM_STEP = 0.001, 0.9, 0.999, 1e-08, 0.01, 10

BIG = ("ffn1_w_gu", "ffn1_w_down", "w_in", "w_out", "ffn2_w_gu", "ffn2_w_down", "w_ple_gate", "w_ple_proj")
SMALL = ("rel_bias", "ffn1_pre_g", "ffn1_post_g", "attn_pre_g", "b_in", "sinks", "b_out", "attn_post_g",
         "ffn2_pre_g", "ffn2_post_g", "ple_pre_g", "ple_post_g")
WEIGHTS = ("rel_bias", "ffn1_pre_g", "ffn1_w_gu", "ffn1_w_down", "ffn1_post_g", "attn_pre_g", "w_in", "b_in",
           "sinks", "w_out", "b_out", "attn_post_g", "ffn2_pre_g", "ffn2_w_gu", "ffn2_w_down", "ffn2_post_g",
           "ple_pre_g", "w_ple_gate", "w_ple_proj", "ple_post_g")


def _vmem_limit(*block_bytes):
    need = 2 * sum(block_bytes) + max(block_bytes) * 2 + (4 << 20)
    return int(min(max(need, 32 << 20), VMEM_BYTES_V7X - (6 << 20)))


def _nbytes(shape, dtype):
    return int(np.prod(shape)) * jnp.dtype(dtype).itemsize


MXU_WIDTH_V7X = 256


def _col_chunks(n):
    return [slice(c, min(c + MXU_WIDTH_V7X, n)) for c in range(0, n, MXU_WIDTH_V7X)]


def _row_halves(rows):
    return [slice(0, rows // 2), slice(rows // 2, rows)]


def _row_block(rows, cap, mult):
    for cand in range(min(rows, cap), 0, -1):
        if rows % cand == 0 and cand % mult == 0:
            return cand
    raise ValueError((rows, cap, mult))


class _Carry:
    def __init__(self, operands, out_shapes, aliases, sems, start, finish):
        self.operands, self.out_shapes, self.aliases, self.sems = list(operands), list(out_shapes), dict(aliases), list(sems)
        self.start, self.finish = start, finish
        self.results = None


def _join(carries):
    carries = [c for c in carries if c is not None]
    if not carries:
        return None
    if len(carries) == 1:
        return carries[0]
    offs, pos = [], [0, 0, 0]
    for c in carries:
        offs.append(tuple(pos))
        pos = [pos[0] + len(c.operands), pos[1] + len(c.out_shapes), pos[2] + len(c.sems)]
    aliases = {}
    for c, (oi, oo, _) in zip(carries, offs):
        aliases.update({oi + i: oo + o for i, o in c.aliases.items()})

    def run(which):
        def fn(ins, outs, sems):
            for c, (oi, oo, os_) in zip(carries, offs):
                getattr(c, which)(ins[oi:oi + len(c.operands)], outs[oo:oo + len(c.out_shapes)],
                                  sems[os_:os_ + len(c.sems)])
        return fn

    joined = _Carry([a for c in carries for a in c.operands], [s for c in carries for s in c.out_shapes], aliases,
                    [s for c in carries for s in c.sems], run("start"), run("finish"))
    joined.parts = (carries, offs)
    return joined


def _set_results(carry, outs):
    carry.results = list(outs)
    if hasattr(carry, "parts"):
        for c, (_, oo, _) in zip(*carry.parts):
            _set_results(c, outs[oo:oo + len(c.out_shapes)])


ANY = pl.BlockSpec(memory_space=pl.ANY)


def _call(body, *, name, grid, in_specs, out_specs, out_shape, args, scratch=(), vmem_limit=None, carry=None,
          semantics=None):
    n_ci, n_co, n_cs = len(args), len(out_shape), len(scratch)
    semantics = semantics or ("parallel",) * len(grid)
    if carry is None:
        res = pl.pallas_call(
            body, name=name, grid=grid, in_specs=list(in_specs), out_specs=tuple(out_specs), out_shape=tuple(out_shape),
            scratch_shapes=list(scratch),
            compiler_params=pltpu.CompilerParams(dimension_semantics=semantics, vmem_limit_bytes=vmem_limit),
        )(*args)
        return list(res)
    n_pi, n_po = len(carry.operands), len(carry.out_shapes)

    def full_body(*refs):
        ci, pi = refs[:n_ci], refs[n_ci:n_ci + n_pi]
        o0 = n_ci + n_pi
        co, po = refs[o0:o0 + n_co], refs[o0 + n_co:o0 + n_co + n_po]
        s0 = o0 + n_co + n_po
        cs, ps = refs[s0:s0 + n_cs], refs[s0 + n_cs:]
        ids = [pl.program_id(ax) for ax in range(len(grid))]
        first, last = ids[0] == 0, ids[0] == grid[0] - 1
        for ax in range(1, len(grid)):
            first, last = first & (ids[ax] == 0), last & (ids[ax] == grid[ax] - 1)

        @pl.when(first)
        def _():
            carry.start(pi, po, ps)

        body(*ci, *co, *cs)

        @pl.when(last)
        def _():
            carry.finish(pi, po, ps)

    res = pl.pallas_call(
        full_body, name=name, grid=grid, in_specs=list(in_specs) + [ANY] * n_pi,
        out_specs=tuple(out_specs) + tuple([ANY] * n_po), out_shape=tuple(out_shape) + tuple(carry.out_shapes),
        scratch_shapes=list(scratch) + carry.sems,
        input_output_aliases={n_ci + i: n_co + o for i, o in carry.aliases.items()},
        compiler_params=pltpu.CompilerParams(dimension_semantics=("arbitrary",) * len(grid),
                                             vmem_limit_bytes=vmem_limit),
    )(*args, *carry.operands)
    _set_results(carry, res[n_co:])
    return list(res[:n_co])


def _comm_call(carry, *, name):
    n_pi, n_po = len(carry.operands), len(carry.out_shapes)

    def body(*refs):
        pi, po, ps = refs[:n_pi], refs[n_pi:n_pi + n_po], refs[n_pi + n_po:]
        carry.start(pi, po, ps)
        carry.finish(pi, po, ps)

    res = pl.pallas_call(
        body, name=name, in_specs=[ANY] * n_pi, out_specs=tuple([ANY] * n_po), out_shape=tuple(carry.out_shapes),
        scratch_shapes=carry.sems, input_output_aliases=dict(carry.aliases),
    )(*carry.operands)
    _set_results(carry, res)


def _mm_nn(a, b, *, bm, bn, out_dtype, name, bias=None, carry=None):
    M, K = a.shape
    stacked = b.ndim == 3
    N = b.shape[0] * b.shape[2] if stacked else b.shape[1]
    assert M % bm == 0 and N % bn == 0 and (not stacked or bn == b.shape[2])

    def body(*refs):
        a_ref, b_ref = refs[0], refs[1]
        o_ref = refs[-1]
        acc = jnp.dot(a_ref[...], b_ref[...], preferred_element_type=F32)
        if bias is not None:
            acc = acc + refs[2][...]
        o_ref[...] = acc.astype(o_ref.dtype)

    if stacked:
        b_spec = pl.BlockSpec((None, K, bn), lambda i, j: (j, 0, 0))
    else:
        b_spec = pl.BlockSpec((K, bn), lambda i, j: (0, j))
    in_specs = [pl.BlockSpec((bm, K), lambda i, j: (i, 0)), b_spec]
    args = [a, b]
    if bias is not None:
        in_specs.append(pl.BlockSpec((1, bn), lambda i, j: (0, j)))
        args.append(bias)
    limit = _vmem_limit(_nbytes((bm, K), a.dtype), _nbytes((K, bn), b.dtype), _nbytes((bm, bn), F32))
    return _call(body, name=name, grid=(M // bm, N // bn), in_specs=in_specs,
                 out_specs=[pl.BlockSpec((bm, bn), lambda i, j: (i, j))],
                 out_shape=[jax.ShapeDtypeStruct((M, N), out_dtype)], args=args, vmem_limit=limit, carry=carry)[0]


def _mm_nt(a, b, *, bm, bn, out_dtype, name, add=None, carry=None):
    M, K = a.shape
    stacked = b.ndim == 3
    N = b.shape[1] if stacked else b.shape[0]
    n_sh = b.shape[0] if stacked else 1
    ks = b.shape[2] if stacked else K
    assert M % bm == 0 and N % bn == 0 and n_sh * ks == K
    dn = (((1,), (1,)), ((), ()))

    def body(*refs):
        a_ref, b_ref = refs[0], refs[1]
        o_ref = refs[-1]
        if stacked:
            acc = lax.dot_general(a_ref[:, 0:ks], b_ref[0], dn, preferred_element_type=F32)
            for s in range(1, n_sh):
                acc = acc + lax.dot_general(a_ref[:, s * ks:(s + 1) * ks], b_ref[s], dn, preferred_element_type=F32)
        else:
            acc = lax.dot_general(a_ref[...], b_ref[...], dn, preferred_element_type=F32)
        if add is not None:
            acc = acc + refs[2][...]
        o_ref[...] = acc.astype(o_ref.dtype)

    if stacked:
        b_spec = pl.BlockSpec((n_sh, bn, ks), lambda i, j: (0, j, 0))
    else:
        b_spec = pl.BlockSpec((bn, K), lambda i, j: (j, 0))
    in_specs = [pl.BlockSpec((bm, K), lambda i, j: (i, 0)), b_spec]
    args = [a, b]
    if add is not None:
        rows = bm if add.shape[0] == M else 1
        in_specs.append(pl.BlockSpec((rows, bn), lambda i, j: (i if rows == bm else 0, j)))
        args.append(add)
    limit = _vmem_limit(_nbytes((bm, K), a.dtype), _nbytes((bn, K), b.dtype), _nbytes((bm, bn), F32))
    return _call(body, name=name, grid=(M // bm, N // bn), in_specs=in_specs,
                 out_specs=[pl.BlockSpec((bm, bn), lambda i, j: (i, j))],
                 out_shape=[jax.ShapeDtypeStruct((M, N), out_dtype)], args=args, vmem_limit=limit, carry=carry)[0]


def _mm_tn(a, b, *, bm, bn, out_dtype, name, n_stack=None, carry=None):
    T, M = a.shape
    N = b.shape[1]
    assert M % bm == 0 and N % bn == 0 and (n_stack is None or bn * n_stack == N)
    dn = (((0,), (0,)), ((), ()))

    def body(a_ref, b_ref, o_ref):
        acc = lax.dot_general(a_ref[...], b_ref[...], dn, preferred_element_type=F32)
        o_ref[...] = acc.astype(o_ref.dtype)

    if n_stack is None:
        out_spec = pl.BlockSpec((bm, bn), lambda i, j: (i, j))
        out_shape = jax.ShapeDtypeStruct((M, N), out_dtype)
    else:
        out_spec = pl.BlockSpec((None, bm, bn), lambda i, j: (j, i, 0))
        out_shape = jax.ShapeDtypeStruct((n_stack, M, bn), out_dtype)
    limit = _vmem_limit(_nbytes((T, bm), a.dtype), _nbytes((T, bn), b.dtype), _nbytes((bm, bn), F32))
    return _call(body, name=name, grid=(M // bm, N // bn),
                 in_specs=[pl.BlockSpec((T, bm), lambda i, j: (0, i)), pl.BlockSpec((T, bn), lambda i, j: (0, j))],
                 out_specs=[out_spec], out_shape=[out_shape], args=[a, b], vmem_limit=limit, carry=carry)[0]


ROW_BLK = 256


def _rstd(x):
    return lax.rsqrt(jnp.mean(x * x, axis=-1, keepdims=True) + EPS)


def _norm_bwd(xhat, rstd, dxhat):
    return rstd * (dxhat - xhat * jnp.mean(dxhat * xhat, axis=-1, keepdims=True))


def _row_spec(cols):
    return pl.BlockSpec((ROW_BLK, cols), lambda i: (i, 0))


def _vec_spec(cols):
    return pl.BlockSpec((1, cols), lambda i: (0, 0))


def _prologue(x, g, rel_bias, idx_all, *, name, carry=None):
    T, D = x.shape
    n_pat = idx_all.shape[0]
    heads = 2 * N_PAIRS
    steps = n_pat * heads
    rows = T // steps

    def body(rb_ref, x_ref, g_ref, idx_ref, n_ref, bias_ref):
        s = pl.program_id(0)
        head = jnp.where(s < heads, s, heads + s % heads)
        xv = x_ref[...]
        n_ref[...] = (xv * _rstd(xv) * g_ref[...]).astype(n_ref.dtype)
        idxv = idx_ref[...]
        acc = jnp.where(idxv < 0, NEG, 0.0).astype(F32)
        for b in range(NUM_BUCKETS):
            acc = acc + jnp.where(idxv == b, rb_ref[b, head], 0.0)
        bias_ref[0] = acc
        kap = lax.broadcasted_iota(jnp.int32, (1, 2 * QBLK), 1)
        bias_ref[1] = jnp.where(kap < QBLK, NEG, acc)

    return _call(
        body, name=name, grid=(steps,),
        in_specs=[pl.BlockSpec(memory_space=pltpu.SMEM), pl.BlockSpec((rows, D), lambda s: (s, 0)),
                  pl.BlockSpec((1, D), lambda s: (0, 0)),
                  pl.BlockSpec((None, QBLK, 2 * QBLK), lambda s: (s // heads, 0, 0))],
        out_specs=[pl.BlockSpec((rows, D), lambda s: (s, 0)),
                   pl.BlockSpec((None, 2, None, QBLK, 2 * QBLK), lambda s: (s // heads, 0, s % heads, 0, 0))],
        out_shape=[jax.ShapeDtypeStruct((T, D), BF16),
                   jax.ShapeDtypeStruct((n_pat, 2, heads, QBLK, 2 * QBLK), F32)],
        args=[rel_bias, x, g, idx_all], carry=carry)


def _mm_resid_norm(a, b, h, g_post, coef, g_next, *, bm, name, bias=None, carry=None):
    M, K = a.shape
    N = b.shape[1]

    def body(*refs):
        a_ref, b_ref, h_ref, gp_ref, gn_ref = refs[0:5]
        f_ref, hn_ref, n_ref = refs[-3:]
        for rows in _row_halves(bm):
            f = jnp.dot(a_ref[rows, :], b_ref[...], preferred_element_type=F32)
            if bias is not None:
                f = f + refs[5][...]
            f_ref[rows, :] = f
            hn = h_ref[rows, :] + coef * (f * _rstd(f) * gp_ref[...])
            hn_ref[rows, :] = hn
            n_ref[rows, :] = (hn * _rstd(hn) * gn_ref[...]).astype(n_ref.dtype)

    row = lambda w: pl.BlockSpec((bm, w), lambda i: (i, 0))
    vec = pl.BlockSpec((1, N), lambda i: (0, 0))
    in_specs = [row(K), pl.BlockSpec((K, N), lambda i: (0, 0), pipeline_mode=pl.Buffered(1)), row(N), vec, vec]
    args = [a, b, h, g_post, g_next]
    if bias is not None:
        in_specs.append(vec)
        args.append(bias)
    limit = _vmem_limit(_nbytes((bm, K), a.dtype), _nbytes((K, N), b.dtype) // 2, 4 * _nbytes((bm, N), F32))
    return _call(body, name=name, grid=(M // bm,), in_specs=in_specs, out_specs=[row(N), row(N), row(N)],
                 out_shape=[jax.ShapeDtypeStruct((M, N), F32), jax.ShapeDtypeStruct((M, N), F32),
                            jax.ShapeDtypeStruct((M, N), BF16)], args=args, vmem_limit=limit, carry=carry)


def _mm_nt_norms_bwd(a, b, dh_in, h, g_pre, f, g_post, coef, *, bm, name, add=None, b_is_kn=False, carry=None):
    M, K = a.shape
    stacked = b.ndim == 3
    N = b.shape[1] if (stacked or b_is_kn) else b.shape[0]
    n_sh = b.shape[0] if stacked else 1
    ks = b.shape[2] if stacked else K
    assert n_sh * ks == K
    dn_dims = (((1,), (0,)), ((), ())) if b_is_kn else (((1,), (1,)), ((), ()))
    with_f = f is not None
    n_in = 5 + (2 if with_f else 0) + (1 if add is not None else 0)

    def body(*refs):
        a_ref, b_ref, dhi_ref, h_ref, gpre_ref = refs[0:5]
        outs = refs[n_in:]

        @pl.when(pl.program_id(0) == 0)
        def _():
            for acc in (outs[2:] if with_f else outs[1:]):
                acc[...] = jnp.zeros_like(acc)

        for rows in _row_halves(bm):
            if stacked:
                dn = lax.dot_general(a_ref[rows, 0:ks], b_ref[0], dn_dims, preferred_element_type=F32)
                for s in range(1, n_sh):
                    dn = dn + lax.dot_general(a_ref[rows, s * ks:(s + 1) * ks], b_ref[s], dn_dims,
                                              preferred_element_type=F32)
            else:
                dn = lax.dot_general(a_ref[rows, :], b_ref[...], dn_dims, preferred_element_type=F32)
            if add is not None:
                dn = dn + refs[n_in - 1][rows, :]
            hv = h_ref[rows, :]
            r = _rstd(hv)
            hh = hv * r
            dh = dhi_ref[rows, :] + _norm_bwd(hh, r, dn * gpre_ref[...])
            outs[0][rows, :] = dh
            if not with_f:
                outs[1][...] += jnp.sum(dn * hh, axis=0, keepdims=True)
                continue
            f_ref, gpost_ref = refs[5], refs[6]
            df_ref, dgpre_ref, dgpost_ref, dsum_ref = outs[1:]
            dgpre_ref[...] += jnp.sum(dn * hh, axis=0, keepdims=True)
            fv = f_ref[rows, :]
            rf = _rstd(fv)
            fh = fv * rf
            dy = coef * dh
            dgpost_ref[...] += jnp.sum(dy * fh, axis=0, keepdims=True)
            df = _norm_bwd(fh, rf, dy * gpost_ref[...])
            dsum_ref[...] += jnp.sum(df, axis=0, keepdims=True)
            df_ref[rows, :] = df.astype(df_ref.dtype)

    row = lambda w: pl.BlockSpec((bm, w), lambda i: (i, 0))
    vec = pl.BlockSpec((1, N), lambda i: (0, 0))
    once = pl.Buffered(1)
    if stacked:
        b_spec = pl.BlockSpec((n_sh, N, ks), lambda i: (0, 0, 0), pipeline_mode=once)
    else:
        b_spec = pl.BlockSpec(b.shape, lambda i: (0, 0), pipeline_mode=once)
    in_specs = [row(K), b_spec, row(N), row(N), vec]
    args = [a, b, dh_in, h, g_pre]
    if with_f:
        in_specs += [row(N), vec]
        args += [f, g_post]
    if add is not None:
        in_specs.append(row(N))
        args.append(add)
    vec_shape = jax.ShapeDtypeStruct((1, N), F32)
    out_specs = [row(N)] + ([row(N), vec, vec, vec] if with_f else [vec])
    out_shape = [jax.ShapeDtypeStruct((M, N), F32)]
    out_shape += [jax.ShapeDtypeStruct((M, N), BF16), vec_shape, vec_shape, vec_shape] if with_f else [vec_shape]
    limit = _vmem_limit(_nbytes((bm, K), a.dtype), _nbytes((N, K), b.dtype) // 2, 6 * _nbytes((bm, N), F32))
    return _call(body, name=name, grid=(M // bm,), in_specs=in_specs, out_specs=out_specs, out_shape=out_shape,
                 args=args, vmem_limit=limit, semantics=("arbitrary",), carry=carry)


def _ffn_up(n, w_gu, *, bm, name, carry=None):
    M, K = n.shape
    n_sh, _, ns = w_gu.shape
    half = n_sh // 2

    def body(n_ref, wg_ref, wu_ref, g_ref, u_ref, a_ref):
        nv = n_ref[...]
        for cols in _col_chunks(ns):
            g = jnp.dot(nv, wg_ref[:, cols], preferred_element_type=F32)
            u = jnp.dot(nv, wu_ref[:, cols], preferred_element_type=F32)
            g_ref[:, cols] = g.astype(g_ref.dtype)
            u_ref[:, cols] = u.astype(u_ref.dtype)
            a_ref[:, cols] = (g * jax.nn.sigmoid(g) * u).astype(a_ref.dtype)

    out_spec = pl.BlockSpec((bm, ns), lambda i, j: (i, j))
    out = jax.ShapeDtypeStruct((M, half * ns), BF16)
    limit = _vmem_limit(_nbytes((bm, K), n.dtype), 2 * _nbytes((K, ns), w_gu.dtype), 3 * _nbytes((bm, ns), F32))
    return _call(body, name=name, grid=(M // bm, half),
                 in_specs=[pl.BlockSpec((bm, K), lambda i, j: (i, 0)),
                           pl.BlockSpec((None, K, ns), lambda i, j: (j, 0, 0)),
                           pl.BlockSpec((None, K, ns), lambda i, j: (j + half, 0, 0))],
                 out_specs=[out_spec, out_spec, out_spec], out_shape=[out, out, out], args=[n, w_gu, w_gu],
                 vmem_limit=limit, carry=carry)


def _ffn_down_bwd(df, w_down, g, u, *, bm, name, carry=None):
    M, D = df.shape
    F = w_down.shape[0]
    dn = (((1,), (1,)), ((), ()))

    def body(df_ref, w_ref, g_ref, u_ref, o_ref):
        dfv = df_ref[...]
        for cols in _col_chunks(F):
            da = lax.dot_general(dfv, w_ref[cols, :], dn, preferred_element_type=F32)
            gv = g_ref[:, cols].astype(F32)
            s = jax.nn.sigmoid(gv)
            o_ref[:, cols] = (da * u_ref[:, cols].astype(F32) * (s * (1.0 + gv * (1.0 - s)))).astype(o_ref.dtype)
            o_ref[:, F + cols.start:F + cols.stop] = (da * (gv * s)).astype(o_ref.dtype)

    row = lambda w: pl.BlockSpec((bm, w), lambda i: (i, 0))
    limit = _vmem_limit(_nbytes((F, D), w_down.dtype) // 2, 2 * _nbytes((bm, F), BF16), _nbytes((bm, 2 * F), BF16))
    return _call(body, name=name, grid=(M // bm,),
                 in_specs=[row(D), pl.BlockSpec((F, D), lambda i: (0, 0), pipeline_mode=pl.Buffered(1)), row(F), row(F)],
                 out_specs=[row(2 * F)], out_shape=[jax.ShapeDtypeStruct((M, 2 * F), BF16)],
                 args=[df, w_down, g, u], vmem_limit=limit, carry=carry)[0]


def _ple_head(n4, w_gate, p, w_ple, h3, g_post, target, *, bm, name):
    T, D = h3.shape
    kp = p.shape[1]

    def body(n_ref, wg_ref, p_ref, wp_ref, h_ref, g_ref, t_ref, dh_ref, de_ref, dgp_ref, loss_ref, dg_ref):
        @pl.when(pl.program_id(0) == 0)
        def _():
            loss_ref[...] = jnp.zeros_like(loss_ref)
            dg_ref[...] = jnp.zeros_like(dg_ref)

        gpost = g_ref[...]
        for rows in _row_halves(bm):
            gate = jax.nn.sigmoid(jnp.dot(n_ref[rows, :], wg_ref[...], preferred_element_type=F32))
            ev = jnp.dot(p_ref[rows, :], wp_ref[...], preferred_element_type=F32)
            ge = gate * ev
            r = _rstd(ge)
            gh = ge * r
            diff = h_ref[rows, :] + gh * gpost - t_ref[rows, :]
            loss_ref[...] += jnp.sum(diff * diff) * (0.5 / D)
            dh = diff * (1.0 / D)
            dh_ref[rows, :] = dh
            dg_ref[...] += jnp.sum(dh * gh, axis=0, keepdims=True)
            dge = _norm_bwd(gh, r, dh * gpost)
            de_ref[rows, :] = (dge * gate).astype(de_ref.dtype)
            dgp_ref[rows, :] = (dge * ev * gate * (1.0 - gate)).astype(dgp_ref.dtype)

    row = lambda w: pl.BlockSpec((bm, w), lambda i: (i, 0))
    whole = lambda a: pl.BlockSpec(a.shape, lambda i: (0, 0), pipeline_mode=pl.Buffered(1))
    limit = _vmem_limit(_nbytes((bm, D), BF16), _nbytes(w_gate.shape, w_gate.dtype) // 2, 6 * _nbytes((bm, D), F32))
    return pl.pallas_call(
        body, name=name, grid=(T // bm,),
        in_specs=[row(D), whole(w_gate), row(kp), whole(w_ple), row(D), _vec_spec(D), row(D)],
        out_specs=(row(D), row(D), row(D), _vec_spec(LANES), _vec_spec(D)),
        out_shape=(jax.ShapeDtypeStruct((T, D), F32), jax.ShapeDtypeStruct((T, D), BF16),
                   jax.ShapeDtypeStruct((T, D), BF16), jax.ShapeDtypeStruct((1, LANES), F32),
                   jax.ShapeDtypeStruct((1, D), F32)),
        compiler_params=pltpu.CompilerParams(dimension_semantics=("arbitrary",), vmem_limit_bytes=limit),
    )(n4, w_gate, p, w_ple, h3, g_post, target)


def _t5_index(max_dist, stride):
    rho = np.arange(QBLK)[:, None]
    kap = np.arange(2 * QBLK)[None, :]
    d = rho + QBLK - kap
    valid = (d >= 0) & (d <= max_dist)
    n = np.maximum(d, 0) * stride
    max_exact = NUM_BUCKETS // 2
    nf = np.maximum(n, 1).astype(np.float32)
    large = max_exact + (np.log(nf / np.float32(max_exact)) / np.float32(math.log(MAX_DISTANCE / max_exact))
                         * np.float32(NUM_BUCKETS - max_exact)).astype(np.int32)
    large = np.minimum(large, NUM_BUCKETS - 1)
    bucket = np.where(n < max_exact, n, large)
    return np.where(valid, bucket, -1).astype(np.int32)


def _bias_grad(d_bias, idx, *, name):
    def body(db_ref, idx_ref, o_ref):
        h = pl.program_id(0)

        @pl.when(h == 0)
        def _():
            o_ref[...] = jnp.zeros_like(o_ref)

        idxv = idx_ref[...]
        dv = db_ref[0]
        rows = lax.broadcasted_iota(jnp.int32, (NUM_BUCKETS, LANES), 0)
        lanes = lax.broadcasted_iota(jnp.int32, (NUM_BUCKETS, LANES), 1)
        acc = o_ref[...]
        for b in range(NUM_BUCKETS):
            s = jnp.sum(jnp.where(idxv == b, dv, 0.0))
            acc = acc + jnp.where((rows == b) & (lanes == h), s, 0.0)
        o_ref[...] = acc

    return pl.pallas_call(
        body, name=name, grid=(2 * N_PAIRS,),
        in_specs=[pl.BlockSpec((1, QBLK, 2 * QBLK), lambda h: (h, 0, 0)),
                  pl.BlockSpec((QBLK, 2 * QBLK), lambda h: (0, 0))],
        out_specs=pl.BlockSpec((NUM_BUCKETS, LANES), lambda h: (0, 0)),
        out_shape=jax.ShapeDtypeStruct((NUM_BUCKETS, LANES), F32),
        compiler_params=pltpu.CompilerParams(dimension_semantics=("arbitrary",)),
    )(d_bias, idx)


def _lane():
    return lax.broadcasted_iota(jnp.int32, (1, LANES), 1)


def _head_sel(h):
    return (_lane() < HEAD_DIM) if h == 0 else (_lane() >= HEAD_DIM)


def _stat_lo():
    return (_lane() % HEAD_DIM) < (HEAD_DIM // 2)


def _stack_heads(t, scale=None):
    if scale is not None:
        t = t * scale
    zero = jnp.zeros_like(t)
    return jnp.concatenate([jnp.where(_head_sel(0), t, zero), jnp.where(_head_sel(1), t, zero)], axis=0)


def _class_spec(L, r, cols, stride, pps):
    chunks = N_PAIRS // pps
    mode = pl.Buffered(1) if r * chunks == 1 else None
    return pl.BlockSpec((L, MIX_W // chunks), lambda j, c: (0, (cols + j * stride) * chunks + c), pipeline_mode=mode)


def _rows(b, n_blocks=1):
    return pl.ds(pl.multiple_of(b * QBLK, QBLK), n_blocks * QBLK)


def _key_window(ref, b, cols, first):
    if first:
        blk = ref[0:QBLK, cols]
        return jnp.concatenate([blk, blk], axis=0)
    return ref[_rows(b - 1, 2), cols]


def _band_fwd(qa, ka, va, bias, *, r, q_col, k_col, v_col, stride, pattern, name, carry=None):
    L = qa.shape[0]
    nb = L // QBLK
    dn = (((1,), (1,)), ((), ()))
    scale = HEAD_DIM ** -0.5

    pps = N_PAIRS

    def body(q_ref, k_ref, v_ref, bias_ref, o_ref, lse_ref):
        sel0 = _head_sel(0)
        pair0 = pl.program_id(1) * pps

        def block(b, first):
            rows = _rows(b)
            for i in range(pps):
                cols = slice(i * LANES, (i + 1) * LANES)
                q2 = _stack_heads(q_ref[rows, cols], scale)
                k = _key_window(k_ref, b, cols, first)
                v = _key_window(v_ref, b, cols, first)
                bias2 = bias_ref[1 if first else 0, pl.ds(2 * (pair0 + i), 2)].reshape(2 * QBLK, 2 * QBLK)
                s = lax.dot_general(q2, k, dn, preferred_element_type=F32) + bias2
                m = jnp.max(s, axis=1, keepdims=True)
                p = jnp.exp(s - m)
                l = jnp.sum(p, axis=1, keepdims=True)
                o = jnp.dot(p.astype(v.dtype), v, preferred_element_type=F32) * (1.0 / l)
                lse = m + jnp.log(l)
                o_ref[rows, cols] = jnp.where(sel0, o[0:QBLK], o[QBLK:]).astype(o_ref.dtype)
                lse_ref[rows, cols] = jnp.where(sel0, lse[0:QBLK], lse[QBLK:])

        block(0, True)
        if nb > 1:
            pl.loop(1, nb)(lambda b: block(b, False))

    bias_spec = pl.BlockSpec((None, 2, 2 * N_PAIRS, QBLK, 2 * QBLK), lambda j, c: (pattern, 0, 0, 0, 0))
    out_spec = _class_spec(L, r, 0, 1, pps)
    blk_bytes = _nbytes((L, pps * LANES), F32)
    return _call(
        body, name=name, grid=(r, N_PAIRS // pps),
        in_specs=[_class_spec(L, r, q_col, stride, pps), _class_spec(L, r, k_col, stride, pps),
                  _class_spec(L, r, v_col, stride, pps), bias_spec],
        out_specs=[out_spec, out_spec],
        out_shape=[jax.ShapeDtypeStruct((L, r * MIX_W), BF16), jax.ShapeDtypeStruct((L, r * MIX_W), F32)],
        args=[qa, ka, va, bias], vmem_limit=_vmem_limit(*([blk_bytes] * 4)), carry=carry)


def _class_rows_spec(rows, r, width):
    return pl.BlockSpec((rows // r, r * width), lambda i, *_: (i, 0))


def _token_scratch(rows, width):
    return pltpu.VMEM((width // LANES, rows, LANES), F32)


def _fill_tokens(scratch, value):
    for c in range(scratch.shape[0]):
        scratch[c] = value[:, c * LANES:(c + 1) * LANES]


def _read_tokens(scratch):
    return jnp.concatenate([scratch[c] for c in range(scratch.shape[0])], axis=1)


def _to_tokens(blk_ref, r, scratch):
    if r == 1:
        return blk_ref[...].astype(F32)
    nt, rows, _ = scratch.shape
    for j in range(r):
        for c in range(nt):
            lanes = slice((j * nt + c) * LANES, (j * nt + c + 1) * LANES)
            scratch.at[c][pl.ds(j, rows // r, stride=r), :] = blk_ref[:, lanes].astype(F32)
    return _read_tokens(scratch)


def _from_tokens(dst_ref, r, scratch):
    nt, rows, _ = scratch.shape
    if r == 1:
        dst_ref[...] = _read_tokens(scratch).astype(dst_ref.dtype)
        return
    for j in range(r):
        for c in range(nt):
            lanes = slice((j * nt + c) * LANES, (j * nt + c + 1) * LANES)
            dst_ref[:, lanes] = scratch.at[c][pl.ds(j, rows // r, stride=r), :].astype(dst_ref.dtype)


def _mm_nt_classes(a, b, bias, dils, *, bm, name):
    M, K = a.shape
    N = b.shape[0]
    dn = (((1,), (1,)), ((), ()))

    def body(a_ref, b_ref, bias_ref, *rest):
        outs, tile = rest[:-1], rest[-1]
        _fill_tokens(tile, lax.dot_general(a_ref[...], b_ref[...], dn, preferred_element_type=F32) + bias_ref[...])
        for out, r in zip(outs, dils):
            _from_tokens(out, r, tile)

    limit = _vmem_limit(_nbytes((bm, K), a.dtype), _nbytes((K, N), b.dtype) // 2, (1 + len(dils)) * _nbytes((bm, N), F32))
    return pl.pallas_call(
        body, name=name, grid=(M // bm,),
        in_specs=[pl.BlockSpec((bm, K), lambda i: (i, 0)),
                  pl.BlockSpec((N, K), lambda i: (0, 0), pipeline_mode=pl.Buffered(1)),
                  pl.BlockSpec((1, N), lambda i: (0, 0))],
        out_specs=tuple(_class_rows_spec(bm, r, N) for r in dils),
        out_shape=tuple(jax.ShapeDtypeStruct((M // r, r * N), BF16) for r in dils),
        scratch_shapes=[_token_scratch(bm, N)],
        compiler_params=pltpu.CompilerParams(dimension_semantics=("parallel",), vmem_limit_bytes=limit),
    )(a, b, bias)


def _merge(branches, dils, sink, *, name):
    W = MIX_W
    T = branches[0][0].shape[0] * dils[0]
    nbr = len(branches)

    def body(*refs):
        sink_ref = refs[2 * nbr]
        out_ref, outb_ref, lse_ref, scratch = refs[2 * nbr + 1:]
        sk = sink_ref[...]
        lses = [_to_tokens(refs[2 * t + 1], dils[t], scratch) for t in range(nbr)]
        mx = sk
        for lse in lses:
            mx = jnp.maximum(mx, lse)
        den = jnp.exp(sk - mx)
        num = jnp.zeros_like(mx)
        for t in range(nbr):
            w = jnp.exp(lses[t] - mx)
            den = den + w
            num = num + w * _to_tokens(refs[2 * t], dils[t], scratch)
        out = num / den
        out_ref[...] = out
        outb_ref[...] = out.astype(outb_ref.dtype)
        lse_ref[...] = mx + jnp.log(den)

    args = [a for br in branches for a in br] + [sink]
    in_specs = [_class_rows_spec(ROW_BLK, r, W) for r in dils for _ in range(2)] + [_vec_spec(W)]
    return pl.pallas_call(
        body, name=name, grid=(T // ROW_BLK,), in_specs=in_specs,
        out_specs=(_row_spec(W), _row_spec(W), _row_spec(W)),
        out_shape=(jax.ShapeDtypeStruct((T, W), F32), jax.ShapeDtypeStruct((T, W), BF16),
                   jax.ShapeDtypeStruct((T, W), F32)),
        scratch_shapes=[_token_scratch(ROW_BLK, W)],
        compiler_params=pltpu.CompilerParams(dimension_semantics=("parallel",)),
    )(*args)


def _attn_delta(dmix, outs, lses, sink, dils, *, name):
    T = dmix.shape[0]
    W = MIX_W
    nd = len(dils)

    def body(dmix_ref, oa_ref, ob_ref, la_ref, lb_ref, sink_ref, *rest):
        doa_ref, sta_ref, dsink_ref = rest[0:3]
        dob_refs, stb_refs = rest[3:3 + nd], rest[3 + nd:3 + 2 * nd]
        do_tile, st_tile = rest[3 + 2 * nd:]

        @pl.when(pl.program_id(0) == 0)
        def _():
            dsink_ref[...] = jnp.zeros_like(dsink_ref)

        sel0, lo = _head_sel(0), _stat_lo()
        for mixer, (o_ref, l_ref) in enumerate(((oa_ref, la_ref), (ob_ref, lb_ref))):
            for i in range(N_PAIRS):
                cols = slice(i * LANES, (i + 1) * LANES)
                do = dmix_ref[:, mixer * W + i * LANES:mixer * W + (i + 1) * LANES]
                prod = do * o_ref[:, cols]
                d0 = jnp.sum(jnp.where(sel0, prod, 0.0), axis=1, keepdims=True)
                d1 = jnp.sum(jnp.where(sel0, 0.0, prod), axis=1, keepdims=True)
                delta = jnp.where(sel0, d0, d1)
                lse = l_ref[:, cols]
                stat = jnp.where(lo, lse, delta)
                if mixer == 0:
                    sta_ref[:, cols] = stat
                    doa_ref[:, cols] = do.astype(doa_ref.dtype)
                    dsink_ref[:, cols] += -jnp.sum(jnp.exp(sink_ref[:, cols] - lse) * delta, axis=0, keepdims=True)
                else:
                    st_tile[i] = stat
                    do_tile[i] = do
        for t, r in enumerate(dils):
            _from_tokens(dob_refs[t], r, do_tile)
            _from_tokens(stb_refs[t], r, st_tile)

    class_specs = [_class_rows_spec(ROW_BLK, r, W) for r in dils]
    out_shape = [jax.ShapeDtypeStruct((T, W), BF16), jax.ShapeDtypeStruct((T, W), F32), jax.ShapeDtypeStruct((1, W), F32)]
    out_shape += [jax.ShapeDtypeStruct((T // r, r * W), BF16) for r in dils]
    out_shape += [jax.ShapeDtypeStruct((T // r, r * W), F32) for r in dils]
    res = pl.pallas_call(
        body, name=name, grid=(T // ROW_BLK,),
        in_specs=[_row_spec(2 * W), _row_spec(W), _row_spec(W), _row_spec(W), _row_spec(W), _vec_spec(W)],
        out_specs=tuple([_row_spec(W), _row_spec(W), _vec_spec(W)] + class_specs + class_specs),
        out_shape=tuple(out_shape),
        scratch_shapes=[_token_scratch(ROW_BLK, W), _token_scratch(ROW_BLK, W)],
        compiler_params=pltpu.CompilerParams(dimension_semantics=("arbitrary",)),
    )(dmix, outs[0], outs[1], lses[0], lses[1], sink)
    return res[0], res[1], res[2], res[3:3 + nd], res[3 + nd:]


def _band_bwd(qa, ka, va, d_out, stat, bias, *, r, q_col, k_col, v_col, stride, pattern, name, carry=None):
    L = qa.shape[0]
    nb = L // QBLK
    dn_nt = (((1,), (1,)), ((), ()))
    dn_tn = (((0,), (0,)), ((), ()))
    scale = HEAD_DIM ** -0.5

    pps = N_PAIRS if r > 1 else N_PAIRS // 2

    def body(q_ref, k_ref, v_ref, do_ref, st_ref, bias_ref, dq_ref, dk_ref, dv_ref, db_ref, dk_carry, dv_carry):
        @pl.when((pl.program_id(0) == 0) & (pl.program_id(1) == 0))
        def _():
            db_ref[...] = jnp.zeros_like(db_ref)

        lo, sel0 = _stat_lo(), _head_sel(0)
        pair0 = pl.program_id(1) * pps

        def block(b, first):
            rows = _rows(b)
            for i in range(pps):
                heads = pl.ds(2 * (pair0 + i), 2)
                cols = slice(i * LANES, (i + 1) * LANES)
                q2 = _stack_heads(q_ref[rows, cols], scale)
                k = _key_window(k_ref, b, cols, first)
                v = _key_window(v_ref, b, cols, first)
                do2 = _stack_heads(do_ref[rows, cols])
                st = st_ref[rows, cols]
                lse2 = jnp.concatenate(
                    [jnp.max(jnp.where(_head_sel(h) & lo, st, -jnp.inf), axis=1, keepdims=True) for h in range(2)], axis=0)
                delta2 = jnp.concatenate(
                    [jnp.sum(jnp.where(_head_sel(h) & ~lo, st, 0.0), axis=1, keepdims=True) for h in range(2)],
                    axis=0) * (2.0 / HEAD_DIM)
                bias2 = bias_ref[1 if first else 0, heads].reshape(2 * QBLK, 2 * QBLK)
                s = lax.dot_general(q2, k, dn_nt, preferred_element_type=F32) + bias2
                p = jnp.exp(s - lse2)
                dp = lax.dot_general(do2, v, dn_nt, preferred_element_type=F32)
                ds = p * (dp - delta2)
                db_ref[heads] += ds.reshape(2, QBLK, 2 * QBLK)
                dsb = ds.astype(k.dtype)
                dq2 = jnp.dot(dsb, k, preferred_element_type=F32)
                dq_ref[rows, cols] = (jnp.where(sel0, dq2[0:QBLK], dq2[QBLK:]) * scale).astype(dq_ref.dtype)
                dk_acc = lax.dot_general(dsb, q2, dn_tn, preferred_element_type=F32)
                dv_acc = lax.dot_general(p.astype(k.dtype), do2, dn_tn, preferred_element_type=F32)
                if not first:
                    prev = _rows(b - 1)
                    dk_ref[prev, cols] = (dk_carry[:, cols] + dk_acc[0:QBLK]).astype(dk_ref.dtype)
                    dv_ref[prev, cols] = (dv_carry[:, cols] + dv_acc[0:QBLK]).astype(dv_ref.dtype)
                dk_carry[:, cols] = dk_acc[QBLK:2 * QBLK]
                dv_carry[:, cols] = dv_acc[QBLK:2 * QBLK]

        block(0, True)
        if nb > 1:
            pl.loop(1, nb)(lambda b: block(b, False))

        last = _rows(nb - 1)
        dk_ref[last, :] = dk_carry[...].astype(dk_ref.dtype)
        dv_ref[last, :] = dv_carry[...].astype(dv_ref.dtype)

    tok_spec = _class_spec(L, r, 0, 1, pps)
    bias_spec = pl.BlockSpec((None, 2, 2 * N_PAIRS, QBLK, 2 * QBLK), lambda j, c: (pattern, 0, 0, 0, 0))
    dbias_spec = pl.BlockSpec((2 * N_PAIRS, QBLK, 2 * QBLK), lambda j, c: (0, 0, 0))
    grad = jax.ShapeDtypeStruct((L, r * MIX_W), BF16)
    blk_bytes = _nbytes((L, pps * LANES), BF16)
    carry_shape = pltpu.VMEM((QBLK, pps * LANES), F32)
    return _call(
        body, name=name, grid=(r, N_PAIRS // pps),
        in_specs=[_class_spec(L, r, q_col, stride, pps), _class_spec(L, r, k_col, stride, pps),
                  _class_spec(L, r, v_col, stride, pps), tok_spec, tok_spec, bias_spec],
        out_specs=[tok_spec, tok_spec, tok_spec, dbias_spec],
        out_shape=[grad, grad, grad, jax.ShapeDtypeStruct((2 * N_PAIRS, QBLK, 2 * QBLK), F32)],
        args=[qa, ka, va, d_out, stat, bias], scratch=[carry_shape, carry_shape],
        vmem_limit=_vmem_limit(*([blk_bytes] * 9)), semantics=("arbitrary", "arbitrary"), carry=carry)


def _dz_assemble(dq_a, dk_a, dv_a, b_grads, dils, *, name):
    T = dq_a.shape[0]
    W = MIX_W

    def group_sum(x):
        u0 = x[:, 0:LANES] + x[:, LANES:2 * LANES]
        u1 = x[:, 2 * LANES:3 * LANES] + x[:, 3 * LANES:4 * LANES]
        s0 = u0 + pltpu.roll(u0, HEAD_DIM, 1)
        s1 = u1 + pltpu.roll(u1, HEAD_DIM, 1)
        return jnp.where(_head_sel(0), s0, s1)

    def body(*refs):
        dqa_ref, dka_ref, dva_ref = refs[0:3]
        b_refs = refs[3:12]
        dza_ref, dzb_ref, sa_ref, sb_ref, scratch = refs[12:]

        @pl.when(pl.program_id(0) == 0)
        def _():
            sa_ref[...] = jnp.zeros_like(sa_ref)
            sb_ref[...] = jnp.zeros_like(sb_ref)

        def put(dst, acc, col, part):
            w = part.shape[1]
            dst[:, col:col + w] = part.astype(dst.dtype)
            acc[:, col:col + w] += jnp.sum(part, axis=0, keepdims=True)

        put(dza_ref, sa_ref, 0, dqa_ref[...].astype(F32))
        put(dza_ref, sa_ref, W, group_sum(dka_ref[...].astype(F32)))
        put(dza_ref, sa_ref, W + LANES, group_sum(dva_ref[...].astype(F32)))
        for t in range(3):
            part = _to_tokens(b_refs[t], dils[0], scratch)
            for pat in range(1, len(dils)):
                part = part + _to_tokens(b_refs[3 * pat + t], dils[pat], scratch)
            put(dzb_ref, sb_ref, t * W, part)

    args = [dq_a, dk_a, dv_a] + [g[t] for g in b_grads for t in range(3)]
    return pl.pallas_call(
        body, name=name, grid=(T // ROW_BLK,),
        in_specs=[_row_spec(W)] * 3 + [_class_rows_spec(ROW_BLK, r, W) for r in dils for _ in range(3)],
        out_specs=(_row_spec(ZA_W), _row_spec(ZB_W), _vec_spec(ZA_W), _vec_spec(ZB_W)),
        out_shape=(jax.ShapeDtypeStruct((T, ZA_W), BF16), jax.ShapeDtypeStruct((T, ZB_W), BF16),
                   jax.ShapeDtypeStruct((1, ZA_W), F32), jax.ShapeDtypeStruct((1, ZB_W), F32)),
        scratch_shapes=[_token_scratch(ROW_BLK, W)],
        compiler_params=pltpu.CompilerParams(dimension_semantics=("arbitrary",)),
    )(*args)


def _place():
    x, y, c = lax.axis_index("x"), lax.axis_index("y"), lax.axis_index("c")
    chips = [(1 - x, y), (x, 1 - y), (1 - x, 1 - y)]
    return x, y, c, chips


def _cast_place(shard, k_idx, *, name):
    rs, cs = shard.shape
    rb = _row_block(rs, 512, 16)
    nblk = rs // rb

    def body(k_ref, s_ref, o_ref):
        o_ref[...] = s_ref[...].astype(o_ref.dtype)

    grid_spec = pltpu.PrefetchScalarGridSpec(
        num_scalar_prefetch=1, grid=(nblk,),
        in_specs=[pl.BlockSpec((rb, cs), lambda t, k_ref: (t, 0))],
        out_specs=pl.BlockSpec((rb, cs), lambda t, k_ref: (k_ref[0] * nblk + t, 0)))
    return pl.pallas_call(
        body, name=name, grid_spec=grid_spec, out_shape=jax.ShapeDtypeStruct((N_CHIPS * rs, cs), BF16),
        compiler_params=pltpu.CompilerParams(dimension_semantics=("parallel",)),
    )(k_idx, shard)


def _gather_carry(fulls, part=(0, 1)):
    n = len(fulls)
    p_idx, n_parts = part

    def piece(ref, chip_idx, half):
        rs = ref.shape[0] // N_CHIPS
        rh = rs // 2
        rows = rh // n_parts
        return ref.at[pl.ds(chip_idx * rs + half * rh + p_idx * rows, rows)]

    def copy(sems, sem, src_ref, dst_ref, to):
        return pltpu.make_async_remote_copy(src_ref=src_ref, dst_ref=dst_ref, send_sem=sems[0].at[sem],
                                            recv_sem=sems[1].at[sem], device_id=to, device_id_type=MESH)

    def ici_copy(ins, outs, sems, i, j, chip, k, c):
        return copy(sems, 3 * i + j, piece(ins[i], k, c), piece(outs[i], k, c), (*chip, c))

    def start(ins, outs, sems):
        x, y, c, chips = _place()
        for i in range(n):
            for j, chip in enumerate(chips):
                ici_copy(ins, outs, sems, i, j, chip, 2 * x + y, c).start()

    def finish(ins, outs, sems):
        x, y, c, chips = _place()
        sibling = (x, y, 1 - c)
        passed = []
        for i in range(n):
            for j, chip in enumerate(chips):
                region = piece(outs[i], 2 * chip[0] + chip[1], c)
                copy(sems, 3 * i + j, region, region, (*chip, c)).wait_recv()
                cp = copy(sems, 3 * n + 3 * i + j, region, region, sibling)
                cp.start()
                passed.append(cp)
        for i in range(n):
            for j, chip in enumerate(chips):
                region = piece(outs[i], 2 * chip[0] + chip[1], 1 - c)
                copy(sems, 3 * n + 3 * i + j, region, region, sibling).wait_recv()
        for i in range(n):
            for j, chip in enumerate(chips):
                ici_copy(ins, outs, sems, i, j, chip, 2 * x + y, c).wait_send()
        for cp in passed:
            cp.wait_send()

    return _Carry(fulls, [jax.ShapeDtypeStruct(a.shape, a.dtype) for a in fulls], {i: i for i in range(n)},
                  [pltpu.SemaphoreType.DMA((6 * n,)), pltpu.SemaphoreType.DMA((6 * n,))], start, finish)


def _pair_swap_carry(grads):
    n = len(grads)

    def copy(ins, outs, sems, i):
        x, y, c, _ = _place()
        return pltpu.make_async_remote_copy(src_ref=ins[i].at[:, 1 - c], dst_ref=outs[i], send_sem=sems[0].at[i],
                                            recv_sem=sems[1].at[i], device_id=(x, y, 1 - c), device_id_type=MESH)

    def start(ins, outs, sems):
        for i in range(n):
            copy(ins, outs, sems, i).start()

    def finish(ins, outs, sems):
        for i in range(n):
            copy(ins, outs, sems, i).wait()

    return _Carry(grads, [jax.ShapeDtypeStruct((a.shape[0], a.shape[2], a.shape[3]), a.dtype) for a in grads], {},
                  [pltpu.SemaphoreType.DMA((n,)), pltpu.SemaphoreType.DMA((n,))], start, finish)


def _pair_sum(g4, got, c_idx, *, name):
    _, _, rh, cs = g4.shape
    rb = _row_block(rh, 512, 16)

    def body(c_ref, own_ref, got_ref, o_ref):
        o_ref[...] = (own_ref[...].astype(F32) + got_ref[...].astype(F32)).astype(o_ref.dtype)

    grid_spec = pltpu.PrefetchScalarGridSpec(
        num_scalar_prefetch=1, grid=(N_CHIPS, rh // rb),
        in_specs=[pl.BlockSpec((None, None, rb, cs), lambda k, t, c_ref: (k, c_ref[0], t, 0)),
                  pl.BlockSpec((None, rb, cs), lambda k, t, c_ref: (k, t, 0))],
        out_specs=pl.BlockSpec((None, rb, cs), lambda k, t, c_ref: (k, t, 0)))
    return pl.pallas_call(
        body, name=name, grid_spec=grid_spec, out_shape=jax.ShapeDtypeStruct((N_CHIPS, rh, cs), BF16),
        compiler_params=pltpu.CompilerParams(dimension_semantics=("parallel", "parallel")),
    )(c_idx, g4, got)


def _chip_scatter_carry(sums):
    n = len(sums)

    def copies(ins, outs, sems):
        x, y, c, chips = _place()
        return [pltpu.make_async_remote_copy(src_ref=ins[i].at[2 * chip[0] + chip[1]], dst_ref=outs[i].at[j],
                                             send_sem=sems[0].at[3 * i + j], recv_sem=sems[1].at[3 * i + j],
                                             device_id=(*chip, c), device_id_type=MESH)
                for i in range(n) for j, chip in enumerate(chips)]

    def start(ins, outs, sems):
        for cp in copies(ins, outs, sems):
            cp.start()

    def finish(ins, outs, sems):
        for cp in copies(ins, outs, sems):
            cp.wait()

    return _Carry(sums, [jax.ShapeDtypeStruct((3,) + a.shape[1:], a.dtype) for a in sums], {},
                  [pltpu.SemaphoreType.DMA((3 * n,)), pltpu.SemaphoreType.DMA((3 * n,))], start, finish)


def _chip_sum(sums, got, kc_idx, *, name):
    _, rh, cs = sums.shape
    rb = _row_block(rh, 512, 16)
    nblk = rh // rb

    def body(kc_ref, own_ref, got_ref, o_ref):
        acc = own_ref[...].astype(F32)
        for j in range(3):
            acc = acc + got_ref[j].astype(F32)
        o_ref[...] = acc

    grid_spec = pltpu.PrefetchScalarGridSpec(
        num_scalar_prefetch=1, grid=(nblk,),
        in_specs=[pl.BlockSpec((None, rb, cs), lambda t, kc: (kc[0], t, 0)),
                  pl.BlockSpec((3, rb, cs), lambda t, kc: (0, t, 0))],
        out_specs=pl.BlockSpec((rb, cs), lambda t, kc: (kc[1] * nblk + t, 0)))
    return pl.pallas_call(
        body, name=name, grid_spec=grid_spec, out_shape=jax.ShapeDtypeStruct((2 * rh, cs), F32),
        compiler_params=pltpu.CompilerParams(dimension_semantics=("parallel",)),
    )(kc_idx, sums, got)


def _half_swap_carry(shards):
    n = len(shards)

    def copy(ins, outs, sems, i, to_my_half):
        x, y, c, _ = _place()
        rh = ins[i].shape[0] // 2
        half = c if to_my_half else 1 - c
        return pltpu.make_async_remote_copy(
            src_ref=ins[i].at[pl.ds(c * rh, rh)], dst_ref=outs[i].at[pl.ds(half * rh, rh)],
            send_sem=sems[0].at[i], recv_sem=sems[1].at[i], device_id=(x, y, 1 - c), device_id_type=MESH)

    def start(ins, outs, sems):
        for i in range(n):
            copy(ins, outs, sems, i, True).start()

    def finish(ins, outs, sems):
        for i in range(n):
            copy(ins, outs, sems, i, False).wait_recv()
        for i in range(n):
            copy(ins, outs, sems, i, True).wait_send()

    return _Carry(shards, [jax.ShapeDtypeStruct(a.shape, a.dtype) for a in shards], {i: i for i in range(n)},
                  [pltpu.SemaphoreType.DMA((n,)), pltpu.SemaphoreType.DMA((n,))], start, finish)


SMALL_ROW = 1024


def _small_layout(shapes):
    starts, row = [], 0
    for r, c in shapes:
        starts.append(row)
        row += -(-c // SMALL_ROW) if r == 1 else r
    return starts, -(-row // SUBLANES) * SUBLANES


def _small_pieces(shape, start):
    r, c = shape
    if r == 1:
        return [(start + q, min(SMALL_ROW, c - q * SMALL_ROW), q * SMALL_ROW) for q in range(-(-c // SMALL_ROW))]
    return None


def _small_all_reduce(parts, *, name):
    shapes = [a.shape for a in parts]
    starts, rows = _small_layout(shapes)
    n = len(parts)
    n_tables = sum(1 for r, _ in shapes if r > 1)
    assert all(r > 1 and c <= LANES for r, c in shapes[:n_tables]) and all(r == 1 for r, _ in shapes[n_tables:])
    table_rows = starts[n_tables]
    assert table_rows % SUBLANES == 0

    def regions(slot):
        return slot.at[pl.ds(0, table_rows), pl.ds(0, LANES)], slot.at[pl.ds(table_rows, rows - table_rows)]

    def body(*refs):
        ins, outs = refs[0:n], refs[n:2 * n]
        buf, send_sems, recv_sems = refs[2 * n:]
        x, y, c, _ = _place()
        me = 4 * x + 2 * y + c
        mine = buf.at[me]
        mine[...] = jnp.zeros((rows, SMALL_ROW), F32)
        for ref, shape, start in zip(ins, shapes, starts):
            pieces = _small_pieces(shape, start)
            if pieces is None:
                mine[start:start + shape[0], 0:shape[1]] = ref[...]
            else:
                for row, width, col in pieces:
                    mine[row:row + 1, 0:width] = ref[:, col:col + width]
        def peer(d):
            return 1 - x if d & 4 else x, 1 - y if d & 2 else y, 1 - c if d & 1 else c

        def copy(d, part, dst_slot):
            sem = 2 * (d - 1) + part
            return pltpu.make_async_remote_copy(
                src_ref=regions(mine)[part], dst_ref=regions(dst_slot)[part], send_sem=send_sems.at[sem],
                recv_sem=recv_sems.at[sem], device_id=peer(d), device_id_type=MESH)

        for d in range(1, 8):
            for part in range(2):
                copy(d, part, mine).start()
        for d in range(1, 8):
            px, py, pc = peer(d)
            for part in range(2):
                copy(d, part, buf.at[4 * px + 2 * py + pc]).wait_recv()
        for d in range(1, 8):
            for part in range(2):
                copy(d, part, mine).wait_send()
        tables, vectors = regions(buf.at[0])
        acc_t, acc_v = tables[...], vectors[...]
        for s in range(1, 8):
            tables, vectors = regions(buf.at[s])
            acc_t, acc_v = acc_t + tables[...], acc_v + vectors[...]
        tables, vectors = regions(mine)
        tables[...] = acc_t
        vectors[...] = acc_v
        for ref, shape, start in zip(outs, shapes, starts):
            pieces = _small_pieces(shape, start)
            if pieces is None:
                ref[...] = mine[start:start + shape[0], 0:shape[1]]
            else:
                for row, width, col in pieces:
                    ref[:, col:col + width] = mine[row:row + 1, 0:width]

    vm = pl.BlockSpec(memory_space=pltpu.VMEM)
    return pl.pallas_call(
        body, name=name, in_specs=[vm] * n, out_specs=tuple([vm] * n),
        out_shape=tuple(jax.ShapeDtypeStruct(s, F32) for s in shapes),
        scratch_shapes=[pltpu.VMEM((8, rows, SMALL_ROW), F32), pltpu.SemaphoreType.DMA((14,)),
                        pltpu.SemaphoreType.DMA((14,))],
    )(*parts)


def _adamw_small(ws, gs, ms, vs, *, name):
    n = len(ws)
    c1 = 1.0 - ADAM_B1 ** ADAM_STEP
    c2 = 1.0 - ADAM_B2 ** ADAM_STEP

    def body(*refs):
        for k in range(n):
            w_ref, g_ref, m_ref, v_ref = (refs[t * n + k] for t in range(4))
            go_ref, d_ref, mo_ref, vo_ref = (refs[(4 + t) * n + k] for t in range(4))
            gv = g_ref[...]
            go_ref[...] = gv
            mn = ADAM_B1 * m_ref[...] + (1.0 - ADAM_B1) * gv
            vn = ADAM_B2 * v_ref[...] + (1.0 - ADAM_B2) * (gv * gv)
            d_ref[...] = -ADAM_LR * ((mn / c1) / (jnp.sqrt(vn / c2) + ADAM_EPS) + ADAM_WD * w_ref[...])
            mo_ref[...] = mn
            vo_ref[...] = vn

    vm = pl.BlockSpec(memory_space=pltpu.VMEM)
    shapes = tuple(jax.ShapeDtypeStruct(a.shape, F32) for a in ws)
    res = pl.pallas_call(
        body, name=name, in_specs=[vm] * (4 * n), out_specs=tuple([vm] * (4 * n)), out_shape=shapes * 4,
    )(*ws, *gs, *ms, *vs)
    return res[0:n], res[n:2 * n], res[2 * n:3 * n], res[3 * n:4 * n]


def _adamw(w, g, m, v, *, name):
    R, C = w.shape
    rb = _row_block(R, 256, SUBLANES)
    c1 = 1.0 - ADAM_B1 ** ADAM_STEP
    c2 = 1.0 - ADAM_B2 ** ADAM_STEP

    def body(w_ref, g_ref, m_ref, v_ref, go_ref, d_ref, mo_ref, vo_ref):
        gv = g_ref[...]
        go_ref[...] = gv
        mn = ADAM_B1 * m_ref[...] + (1.0 - ADAM_B1) * gv
        vn = ADAM_B2 * v_ref[...] + (1.0 - ADAM_B2) * (gv * gv)
        d_ref[...] = -ADAM_LR * ((mn / c1) / (jnp.sqrt(vn / c2) + ADAM_EPS) + ADAM_WD * w_ref[...])
        mo_ref[...] = mn
        vo_ref[...] = vn

    spec = pl.BlockSpec((rb, C), lambda i: (i, 0))
    shp = jax.ShapeDtypeStruct((R, C), F32)
    return pl.pallas_call(
        body, name=name, grid=(R // rb,), in_specs=[spec] * 4, out_specs=(spec,) * 4,
        out_shape=(shp,) * 4, compiler_params=pltpu.CompilerParams(dimension_semantics=("parallel",)),
    )(w, g, m, v)


def _local_step(x, p, target, small, sched):
    T = x.shape[0]
    W = MIX_W
    rel_bias = small["rel_bias"]
    b_in_a, b_in_b = small["b_in"][:, 0:ZA_W], small["b_in"][:, ZA_W:]

    idx_np = [_t5_index(A_WINDOW - 1, 1)] + [_t5_index(window // dil, dil) for window, dil in B_PATTERNS]
    idx_all = jnp.asarray(np.stack(idx_np))
    n1, bias_all = sched.run(_prologue, x, small["ffn1_pre_g"], rel_bias, idx_all, name="prologue")
    w_gu1 = sched.weight("ffn1_w_gu")
    *gu1, a1 = sched.run(_ffn_up, n1, w_gu1, bm=512, name="ffn1_gu")
    w_d1 = sched.weight("ffn1_w_down")
    f1, h1, n2 = sched.run(_mm_resid_norm, a1, w_d1, x, small["ffn1_post_g"], 0.5, small["attn_pre_g"], bm=512,
                           name="ffn1_down")
    win_a, win_b = sched.weight("w_in")
    za = _mm_nt(n2, win_a, bm=1024, bn=ZA_W, out_dtype=BF16, name="in_proj_a", add=b_in_a)
    dils = tuple(dil for _, dil in B_PATTERNS)
    zb_by_dil = _mm_nt_classes(n2, win_b, b_in_b, dils, bm=512, name="in_proj_b")

    k_a = za[:, W:W + LANES].reshape(T, 2, 1, HEAD_DIM)
    v_a = za[:, W + LANES:W + 2 * LANES].reshape(T, 2, 1, HEAD_DIM)
    k_a = jnp.broadcast_to(k_a, (T, 2, 4, HEAD_DIM)).reshape(T, W)
    v_a = jnp.broadcast_to(v_a, (T, 2, 4, HEAD_DIM)).reshape(T, W)
    a_args = dict(r=1, q_col=0, k_col=0, v_col=0, stride=0, pattern=0)
    o_a, lse_a1 = sched.run(_band_fwd, za, k_a, v_a, bias_all, name="band_a_fwd", **a_args)
    sink_row = jnp.repeat(small["sinks"], HEAD_DIM, axis=1)
    out_a, out_a_bf, lse_a = _merge([(o_a, lse_a1)], (1,), sink_row, name="merge_a")

    no_sink = jnp.full((1, W), NEG, F32)
    b_ctx, branches = [], []
    for t, dil in enumerate(dils):
        zb_r = zb_by_dil[t]
        args = dict(r=dil, q_col=0, k_col=1, v_col=2, stride=ZB_W // W, pattern=1 + t)
        branches.append(sched.run(_band_fwd, zb_r, zb_r, zb_r, bias_all, name=f"band_b{t}_fwd", **args))
        b_ctx.append((jnp.asarray(idx_np[1 + t]), zb_r, args))
    out_b, out_b_bf, lse_b = _merge(branches, dils, no_sink, name="merge_b")

    mix = jnp.concatenate([out_a_bf, out_b_bf], axis=1)
    w_out = sched.weight("w_out")
    att, h2, n3 = _mm_resid_norm(mix, w_out, h1, small["attn_post_g"], 1.0, small["ffn2_pre_g"], bm=512,
                                 name="out_proj", bias=small["b_out"])
    w_gu2, w_d2 = sched.weight("ffn2_w_gu"), sched.weight("ffn2_w_down")
    *gu2, a2 = _ffn_up(n3, w_gu2, bm=512, name="ffn2_gu")
    f2, h3, n4 = _mm_resid_norm(a2, w_d2, h2, small["ffn2_post_g"], 0.5, small["ple_pre_g"], bm=512,
                                name="ffn2_down")
    w_gate = sched.weight("w_ple_gate")
    p_bf = p.astype(BF16)
    dh4, de, dgp, loss, d_ple_post = _ple_head(n4, w_gate, p_bf, sched.weight("w_ple_proj"), h3, small["ple_post_g"],
                                               target, bm=512, name="ple_head")

    gs = {"ple_post_g": d_ple_post}
    sched.grad("w_ple_proj", _mm_tn(p_bf, de, bm=256, bn=1024, out_dtype=BF16, name="d_w_ple"))
    sched.grad("w_ple_gate", _mm_tn(n4, dgp, bm=512, bn=1024, out_dtype=BF16, name="d_w_gate"))
    dh3, df2, gs["ple_pre_g"], gs["ffn2_post_g"], _ = sched.run(
        _mm_nt_norms_bwd, dgp, w_gate, dh4, h3, small["ple_pre_g"], f2, small["ffn2_post_g"], 0.5, bm=512, name="d_n4")

    def ffn_bwd(df, a, gu, n, w_down, w_gu, tag, *norm_args):
        dgu = sched.run(_ffn_down_bwd, df, w_down, gu[0], gu[1], bm=512, name=f"ffn{tag}_d_a")
        sched.grad(f"ffn{tag}_w_gu", _mm_tn(n, dgu, bm=512, bn=1408, out_dtype=BF16, name=f"ffn{tag}_d_w_gu",
                                            n_stack=N_CHIPS))
        sched.grad(f"ffn{tag}_w_down", sched.run(_mm_tn, a, df, bm=256, bn=1024, out_dtype=BF16,
                                                name=f"ffn{tag}_d_w_down"))
        return sched.run(_mm_nt_norms_bwd, dgu, w_gu, *norm_args, bm=512, name=f"ffn{tag}_d_n")

    dh2, datt, gs["ffn2_pre_g"], gs["attn_post_g"], gs["b_out"] = ffn_bwd(
        df2, a2, gu2, n3, w_d2, w_gu2, "2", dh3, h2, small["ffn2_pre_g"], att, small["attn_post_g"], 1.0)
    sched.grad("w_out", _mm_tn(mix, datt, bm=512, bn=1024, out_dtype=BF16, name="d_w_out"))
    dmix = sched.run(_mm_nt, datt, w_out, bm=1024, bn=1024, out_dtype=F32, name="d_mix")

    do_a, stat_a, dsink, do_b, stat_b = _attn_delta(dmix, (out_a, out_b), (lse_a, lse_b), sink_row, dils,
                                                    name="attn_delta")
    gs["sinks"] = dsink[:, ::HEAD_DIM]
    dq_a, dk_a, dv_a, dbias_a = sched.run(_band_bwd, za, k_a, v_a, do_a, stat_a, bias_all, name="band_a_bwd", **a_args)
    d_rel_a = _bias_grad(dbias_a, jnp.asarray(idx_np[0]), name="d_rel_a")
    b_grads = []
    d_rel_b = None
    for t, (idx, zb_r, args) in enumerate(b_ctx):
        dq, dk, dv, dbias = sched.run(_band_bwd, zb_r, zb_r, zb_r, do_b[t], stat_b[t], bias_all,
                                      name=f"band_b{t}_bwd", **args)
        b_grads.append((dq, dk, dv))
        d_rel = _bias_grad(dbias, idx, name=f"d_rel_b{t}")
        d_rel_b = d_rel if d_rel_b is None else d_rel_b + d_rel
    gs["rel_bias"] = jnp.concatenate([d_rel_a[:, 0:2 * N_PAIRS], d_rel_b[:, 0:2 * N_PAIRS]], axis=1)
    dza, dzb, dsum_a, dsum_b = _dz_assemble(dq_a, dk_a, dv_a, b_grads, dils, name="d_z")
    gs["b_in"] = jnp.concatenate([dsum_a, dsum_b], axis=1)
    sched.grad("w_in", (_mm_tn(dza, n2, bm=ZA_W, bn=1024, out_dtype=BF16, name="d_w_in_a"),
                        _mm_tn(dzb, n2, bm=ZB_W // 2, bn=1024, out_dtype=BF16, name="d_w_in_b")))
    dn2_a = _mm_nn(dza, win_a, bm=1024, bn=1024, out_dtype=F32, name="d_n2_a")
    dh1, df1, gs["attn_pre_g"], gs["ffn1_post_g"], _ = sched.run(
        _mm_nt_norms_bwd, dzb, win_b, dh2, h1, small["attn_pre_g"], f1, small["ffn1_post_g"], 0.5, bm=512,
        name="d_n2_b", add=dn2_a, b_is_kn=True)
    grad_x, gs["ffn1_pre_g"] = ffn_bwd(df1, a1, gu1, n1, w_d1, w_gu1, "1", dh1, x, small["ffn1_pre_g"], None, None, 0.0)
    return loss, grad_x, gs


def _to_compute(name, full):
    st = full.reshape(N_CHIPS, full.shape[0] // N_CHIPS, full.shape[1])
    if name in ("ffn1_w_gu", "ffn2_w_gu"):
        return st
    if name == "w_ple_proj":
        return jnp.transpose(st, (1, 0, 2)).reshape(st.shape[1], -1)
    if name == "w_in":
        return full[0:ZA_W], full[ZA_W:]
    return full


def _to_comm(name, g):
    if name == "w_in":
        g = jnp.concatenate(g, axis=0)
    if name == "w_ple_proj":
        g = jnp.transpose(g.reshape(g.shape[0], N_CHIPS, -1), (1, 0, 2))
    rs = g.shape[1] if g.ndim == 3 else g.shape[0] // N_CHIPS
    return g.reshape(N_CHIPS, 2, rs // 2, g.shape[-1])


class _Schedule:
    GATHER = {
        "prologue": [("ffn1_w_gu", (0, 1))],
        "ffn1_gu": [("ffn1_w_down", (0, 1)), ("w_in", (0, 1))],
        "ffn1_down": [("w_out", (0, 1))],
        "band_a_fwd": [("ffn2_w_gu", (0, 2))],
        "band_b0_fwd": [("ffn2_w_gu", (1, 2))],
        "band_b1_fwd": [("ffn2_w_down", (0, 1))],
        "band_b2_fwd": [("w_ple_gate", (0, 1)), ("w_ple_proj", (0, 1))],
    }
    SWAP = {"d_n4": ["w_ple_proj", "w_ple_gate"], "ffn2_d_w_down": ["ffn2_w_gu"], "ffn2_d_n": ["ffn2_w_down"],
            "d_mix": ["w_out"], "d_n2_b": ["w_in"], "ffn1_d_w_down": ["ffn1_w_gu"], "ffn1_d_n": ["ffn1_w_down"]}
    SCATTER = {"ffn2_d_a": ["w_ple_proj", "w_ple_gate"], "ffn2_d_n": ["ffn2_w_gu"], "band_a_bwd": ["ffn2_w_down"],
               "band_b0_bwd": ["w_out"], "ffn1_d_a": ["w_in"], "ffn1_d_n": ["ffn1_w_gu"]}

    def __init__(self, placed, c_idx, kc_idx):
        self.full = dict(placed)
        self.missing = {n: 1.0 for n in placed}
        self.c_idx, self.kc_idx = c_idx, kc_idx
        self.grads, self.swapped, self.pair_sums, self.scattered = {}, {}, {}, {}

    def _gather(self, entries):
        carries, after = [], []
        for part in sorted({pt for _, pt in entries}):
            names = [n for n, pt in entries if pt == part]
            carry = _gather_carry([self.full[n] for n in names], part)
            carries.append(carry)

            def done(carry=carry, names=names, part=part):
                for n, a in zip(names, carry.results):
                    self.full[n] = a
                    self.missing[n] -= 1.0 / part[1]
            after.append(done)
        return carries, after

    def weight(self, name):
        assert abs(self.missing[name]) < 1e-9, (name, self.missing[name])
        return _to_compute(name, self.full[name])

    def grad(self, name, g):
        self.grads[name] = _to_comm(name, g)

    def _swap(self, names):
        carry = _pair_swap_carry([self.grads[n] for n in names])

        def done():
            self.swapped.update(zip(names, carry.results))
        return carry, done

    def _scatter(self, names):
        for n in names:
            self.pair_sums[n] = _pair_sum(self.grads[n], self.swapped[n], self.c_idx, name=f"grad_pair_sum_{n}")
        carry = _chip_scatter_carry([self.pair_sums[n] for n in names])

        def done():
            self.scattered.update(zip(names, carry.results))
        return carry, done

    def run(self, fn, *args, name, **kw):
        carries, after = self._gather(self.GATHER.get(name, []))
        for names, make in ((self.SWAP.get(name), self._swap), (self.SCATTER.get(name), self._scatter)):
            if names:
                carry, done = make(names)
                carries.append(carry)
                after.append(done)
        out = fn(*args, name=name, carry=_join(carries), **kw)
        for done in after:
            done()
        return out

    def finish(self):
        left = [n for n in BIG if n not in self.scattered]
        to_swap = [n for n in left if n not in self.swapped]
        if to_swap:
            carry, done = self._swap(to_swap)
            _comm_call(carry, name="grad_pair_swap")
            done()
        if left:
            carry, done = self._scatter(left)
            _comm_call(carry, name="grad_chip_scatter")
            done()
        halves = [_chip_sum(self.pair_sums[n], self.scattered[n], self.kc_idx, name=f"grad_chip_sum_{n}") for n in BIG]
        carry = _half_swap_carry(halves)
        _comm_call(carry, name="grad_half_swap")
        return dict(zip(BIG, carry.results))


def kernel(x, p, rel_bias, ffn1_pre_g, ffn1_w_gu, ffn1_w_down, ffn1_post_g, attn_pre_g, w_in, b_in, sinks, w_out, b_out, attn_post_g, ffn2_pre_g, ffn2_w_gu, ffn2_w_down, ffn2_post_g, ple_pre_g, w_ple_gate, w_ple_proj, ple_post_g, loss_target, m_rel_bias, m_ffn1_pre_g, m_ffn1_w_gu, m_ffn1_w_down, m_ffn1_post_g, m_attn_pre_g, m_w_in, m_b_in, m_sinks, m_w_out, m_b_out, m_attn_post_g, m_ffn2_pre_g, m_ffn2_w_gu, m_ffn2_w_down, m_ffn2_post_g, m_ple_pre_g, m_w_ple_gate, m_w_ple_proj, m_ple_post_g, v_rel_bias, v_ffn1_pre_g, v_ffn1_w_gu, v_ffn1_w_down, v_ffn1_post_g, v_attn_pre_g, v_w_in, v_b_in, v_sinks, v_w_out, v_b_out, v_attn_post_g, v_ffn2_pre_g, v_ffn2_w_gu, v_ffn2_w_down, v_ffn2_post_g, v_ple_pre_g, v_w_ple_gate, v_w_ple_proj, v_ple_post_g):
    given = dict(locals())
    w = {n: given[n] for n in WEIGHTS}
    m = {n: given["m_" + n] for n in WEIGHTS}
    v = {n: given["v_" + n] for n in WEIGHTS}
    c_pos = lax.axis_index("c").astype(jnp.int32)
    k_pos = (2 * lax.axis_index("x") + lax.axis_index("y")).astype(jnp.int32)
    c_idx, k_idx, kc_idx = c_pos.reshape(1), k_pos.reshape(1), jnp.stack([k_pos, c_pos])

    def shard(a, n):
        return jnp.transpose(a[0]) if n == "w_in" else a[0]

    def unshard(a, n):
        return (jnp.transpose(a) if n == "w_in" else a)[None]

    sched = _Schedule({n: _cast_place(shard(w[n], n), k_idx, name=f"place_{n}") for n in BIG}, c_idx, kc_idx)
    small = {n: (w[n] if n == "rel_bias" else w[n].reshape(1, -1)) for n in SMALL}

    loss_part, grad_x, gs = _local_step(x[0], p[0, 0], loss_target[0], small, sched)

    grads_big = sched.finish()

    *small_grads, loss_row = _small_all_reduce([gs[n] for n in SMALL] + [loss_part], name="small_all_reduce")
    loss = loss_row[0, 0]

    grads, delta, new_m, new_v = {}, {}, {}, {}
    for n in BIG:
        res = _adamw(shard(w[n], n), grads_big[n], shard(m[n], n), shard(v[n], n), name=f"adamw_{n}")
        grads[n], delta[n], new_m[n], new_v[n] = (unshard(a, n) for a in res)
    res = _adamw_small([w[n] for n in SMALL], small_grads, [m[n] for n in SMALL], [v[n] for n in SMALL],
                       name="adamw_small")
    for name, gg, dd, mm, vv in zip(SMALL, *res):
        grads[name], delta[name], new_m[name], new_v[name] = gg, dd, mm, vv

    return (loss, grad_x[None], *[grads[n] for n in WEIGHTS], *[delta[n] for n in WEIGHTS],
            *[new_m[n] for n in WEIGHTS], *[new_v[n] for n in WEIGHTS])
```

```python
import math

import jax
import jax.numpy as jnp
import numpy as np
from jax import lax
from jax.experimental import pallas as pl
from jax.experimental.pallas import tpu as pltpu

F32 = jnp.float32
BF16 = jnp.bfloat16
MESH = pl.DeviceIdType.MESH

EPS = 1e-6
NEG = -1e30
LANES = 128
SUBLANES = 8
HEAD_DIM = 64
QBLK = 128
N_PAIRS = 4
MIX_W = N_PAIRS * LANES
A_WINDOW = 128
B_PATTERNS = ((128, 1), (512, 4), (2048, 16))
NUM_BUCKETS = 32
MAX_DISTANCE = 2048
ZA_W = 768
ZB_W = 1536
N_CHIPS = 4
VMEM_BYTES_V7X = 64 * 2**20

ADAM_LR, ADAM_B1, ADAM_B2, ADAM_EPS, ADAM_WD, ADAM_STEP = 0.001, 0.9, 0.999, 1e-08, 0.01, 10

BIG = ("ffn1_w_gu", "ffn1_w_down", "w_in", "w_out", "ffn2_w_gu", "ffn2_w_down", "w_ple_gate", "w_ple_proj")
SMALL = ("rel_bias", "ffn1_pre_g", "ffn1_post_g", "attn_pre_g", "b_in", "sinks", "b_out", "attn_post_g",
         "ffn2_pre_g", "ffn2_post_g", "ple_pre_g", "ple_post_g")
WEIGHTS = ("rel_bias", "ffn1_pre_g", "ffn1_w_gu", "ffn1_w_down", "ffn1_post_g", "attn_pre_g", "w_in", "b_in",
           "sinks", "w_out", "b_out", "attn_post_g", "ffn2_pre_g", "ffn2_w_gu", "ffn2_w_down", "ffn2_post_g",
           "ple_pre_g", "w_ple_gate", "w_ple_proj", "ple_post_g")


def _vmem_limit(*block_bytes):
    need = 2 * sum(block_bytes) + max(block_bytes) * 2 + (4 << 20)
    return int(min(max(need, 32 << 20), VMEM_BYTES_V7X - (6 << 20)))


def _nbytes(shape, dtype):
    return int(np.prod(shape)) * jnp.dtype(dtype).itemsize


MXU_WIDTH_V7X = 256


def _col_chunks(n):
    return [slice(c, min(c + MXU_WIDTH_V7X, n)) for c in range(0, n, MXU_WIDTH_V7X)]


def _row_halves(rows):
    return [slice(0, rows // 2), slice(rows // 2, rows)]


def _row_block(rows, cap, mult):
    for cand in range(min(rows, cap), 0, -1):
        if rows % cand == 0 and cand % mult == 0:
            return cand
    raise ValueError((rows, cap, mult))


class _Carry:
    def __init__(self, operands, out_shapes, aliases, sems, start, finish):
        self.operands, self.out_shapes, self.aliases, self.sems = list(operands), list(out_shapes), dict(aliases), list(sems)
        self.start, self.finish = start, finish
        self.results = None


def _join(carries):
    carries = [c for c in carries if c is not None]
    if not carries:
        return None
    if len(carries) == 1:
        return carries[0]
    offs, pos = [], [0, 0, 0]
    for c in carries:
        offs.append(tuple(pos))
        pos = [pos[0] + len(c.operands), pos[1] + len(c.out_shapes), pos[2] + len(c.sems)]
    aliases = {}
    for c, (oi, oo, _) in zip(carries, offs):
        aliases.update({oi + i: oo + o for i, o in c.aliases.items()})

    def run(which):
        def fn(ins, outs, sems):
            for c, (oi, oo, os_) in zip(carries, offs):
                getattr(c, which)(ins[oi:oi + len(c.operands)], outs[oo:oo + len(c.out_shapes)],
                                  sems[os_:os_ + len(c.sems)])
        return fn

    joined = _Carry([a for c in carries for a in c.operands], [s for c in carries for s in c.out_shapes], aliases,
                    [s for c in carries for s in c.sems], run("start"), run("finish"))
    joined.parts = (carries, offs)
    return joined


def _set_results(carry, outs):
    carry.results = list(outs)
    if hasattr(carry, "parts"):
        for c, (_, oo, _) in zip(*carry.parts):
            _set_results(c, outs[oo:oo + len(c.out_shapes)])


ANY = pl.BlockSpec(memory_space=pl.ANY)


def _call(body, *, name, grid, in_specs, out_specs, out_shape, args, scratch=(), vmem_limit=None, carry=None,
          semantics=None):
    n_ci, n_co, n_cs = len(args), len(out_shape), len(scratch)
    semantics = semantics or ("parallel",) * len(grid)
    if carry is None:
        res = pl.pallas_call(
            body, name=name, grid=grid, in_specs=list(in_specs), out_specs=tuple(out_specs), out_shape=tuple(out_shape),
            scratch_shapes=list(scratch),
            compiler_params=pltpu.CompilerParams(dimension_semantics=semantics, vmem_limit_bytes=vmem_limit),
        )(*args)
        return list(res)
    n_pi, n_po = len(carry.operands), len(carry.out_shapes)

    def full_body(*refs):
        ci, pi = refs[:n_ci], refs[n_ci:n_ci + n_pi]
        o0 = n_ci + n_pi
        co, po = refs[o0:o0 + n_co], refs[o0 + n_co:o0 + n_co + n_po]
        s0 = o0 + n_co + n_po
        cs, ps = refs[s0:s0 + n_cs], refs[s0 + n_cs:]
        ids = [pl.program_id(ax) for ax in range(len(grid))]
        first, last = ids[0] == 0, ids[0] == grid[0] - 1
        for ax in range(1, len(grid)):
            first, last = first & (ids[ax] == 0), last & (ids[ax] == grid[ax] - 1)

        @pl.when(first)
        def _():
            carry.start(pi, po, ps)

        body(*ci, *co, *cs)

        @pl.when(last)
        def _():
            carry.finish(pi, po, ps)

    res = pl.pallas_call(
        full_body, name=name, grid=grid, in_specs=list(in_specs) + [ANY] * n_pi,
        out_specs=tuple(out_specs) + tuple([ANY] * n_po), out_shape=tuple(out_shape) + tuple(carry.out_shapes),
        scratch_shapes=list(scratch) + carry.sems,
        input_output_aliases={n_ci + i: n_co + o for i, o in carry.aliases.items()},
        compiler_params=pltpu.CompilerParams(dimension_semantics=("arbitrary",) * len(grid),
                                             vmem_limit_bytes=vmem_limit),
    )(*args, *carry.operands)
    _set_results(carry, res[n_co:])
    return list(res[:n_co])


def _comm_call(carry, *, name):
    n_pi, n_po = len(carry.operands), len(carry.out_shapes)

    def body(*refs):
        pi, po, ps = refs[:n_pi], refs[n_pi:n_pi + n_po], refs[n_pi + n_po:]
        carry.start(pi, po, ps)
        carry.finish(pi, po, ps)

    res = pl.pallas_call(
        body, name=name, in_specs=[ANY] * n_pi, out_specs=tuple([ANY] * n_po), out_shape=tuple(carry.out_shapes),
        scratch_shapes=carry.sems, input_output_aliases=dict(carry.aliases),
    )(*carry.operands)
    _set_results(carry, res)


def _mm_nn(a, b, *, bm, bn, out_dtype, name, bias=None, carry=None):
    M, K = a.shape
    stacked = b.ndim == 3
    N = b.shape[0] * b.shape[2] if stacked else b.shape[1]
    assert M % bm == 0 and N % bn == 0 and (not stacked or bn == b.shape[2])

    def body(*refs):
        a_ref, b_ref = refs[0], refs[1]
        o_ref = refs[-1]
        acc = jnp.dot(a_ref[...], b_ref[...], preferred_element_type=F32)
        if bias is not None:
            acc = acc + refs[2][...]
        o_ref[...] = acc.astype(o_ref.dtype)

    if stacked:
        b_spec = pl.BlockSpec((None, K, bn), lambda i, j: (j, 0, 0))
    else:
        b_spec = pl.BlockSpec((K, bn), lambda i, j: (0, j))
    in_specs = [pl.BlockSpec((bm, K), lambda i, j: (i, 0)), b_spec]
    args = [a, b]
    if bias is not None:
        in_specs.append(pl.BlockSpec((1, bn), lambda i, j: (0, j)))
        args.append(bias)
    limit = _vmem_limit(_nbytes((bm, K), a.dtype), _nbytes((K, bn), b.dtype), _nbytes((bm, bn), F32))
    return _call(body, name=name, grid=(M // bm, N // bn), in_specs=in_specs,
                 out_specs=[pl.BlockSpec((bm, bn), lambda i, j: (i, j))],
                 out_shape=[jax.ShapeDtypeStruct((M, N), out_dtype)], args=args, vmem_limit=limit, carry=carry)[0]


def _mm_nt(a, b, *, bm, bn, out_dtype, name, add=None, carry=None):
    M, K = a.shape
    stacked = b.ndim == 3
    N = b.shape[1] if stacked else b.shape[0]
    n_sh = b.shape[0] if stacked else 1
    ks = b.shape[2] if stacked else K
    assert M % bm == 0 and N % bn == 0 and n_sh * ks == K
    dn = (((1,), (1,)), ((), ()))

    def body(*refs):
        a_ref, b_ref = refs[0], refs[1]
        o_ref = refs[-1]
        if stacked:
            acc = lax.dot_general(a_ref[:, 0:ks], b_ref[0], dn, preferred_element_type=F32)
            for s in range(1, n_sh):
                acc = acc + lax.dot_general(a_ref[:, s * ks:(s + 1) * ks], b_ref[s], dn, preferred_element_type=F32)
        else:
            acc = lax.dot_general(a_ref[...], b_ref[...], dn, preferred_element_type=F32)
        if add is not None:
            acc = acc + refs[2][...]
        o_ref[...] = acc.astype(o_ref.dtype)

    if stacked:
        b_spec = pl.BlockSpec((n_sh, bn, ks), lambda i, j: (0, j, 0))
    else:
        b_spec = pl.BlockSpec((bn, K), lambda i, j: (j, 0))
    in_specs = [pl.BlockSpec((bm, K), lambda i, j: (i, 0)), b_spec]
    args = [a, b]
    if add is not None:
        rows = bm if add.shape[0] == M else 1
        in_specs.append(pl.BlockSpec((rows, bn), lambda i, j: (i if rows == bm else 0, j)))
        args.append(add)
    limit = _vmem_limit(_nbytes((bm, K), a.dtype), _nbytes((bn, K), b.dtype), _nbytes((bm, bn), F32))
    return _call(body, name=name, grid=(M // bm, N // bn), in_specs=in_specs,
                 out_specs=[pl.BlockSpec((bm, bn), lambda i, j: (i, j))],
                 out_shape=[jax.ShapeDtypeStruct((M, N), out_dtype)], args=args, vmem_limit=limit, carry=carry)[0]


def _mm_tn(a, b, *, bm, bn, out_dtype, name, n_stack=None, carry=None):
    T, M = a.shape
    N = b.shape[1]
    assert M % bm == 0 and N % bn == 0 and (n_stack is None or bn * n_stack == N)
    dn = (((0,), (0,)), ((), ()))

    def body(a_ref, b_ref, o_ref):
        acc = lax.dot_general(a_ref[...], b_ref[...], dn, preferred_element_type=F32)
        o_ref[...] = acc.astype(o_ref.dtype)

    if n_stack is None:
        out_spec = pl.BlockSpec((bm, bn), lambda i, j: (i, j))
        out_shape = jax.ShapeDtypeStruct((M, N), out_dtype)
    else:
        out_spec = pl.BlockSpec((None, bm, bn), lambda i, j: (j, i, 0))
        out_shape = jax.ShapeDtypeStruct((n_stack, M, bn), out_dtype)
    limit = _vmem_limit(_nbytes((T, bm), a.dtype), _nbytes((T, bn), b.dtype), _nbytes((bm, bn), F32))
    return _call(body, name=name, grid=(M // bm, N // bn),
                 in_specs=[pl.BlockSpec((T, bm), lambda i, j: (0, i)), pl.BlockSpec((T, bn), lambda i, j: (0, j))],
                 out_specs=[out_spec], out_shape=[out_shape], args=[a, b], vmem_limit=limit, carry=carry)[0]


ROW_BLK = 256


def _rstd(x):
    return lax.rsqrt(jnp.mean(x * x, axis=-1, keepdims=True) + EPS)


def _norm_bwd(xhat, rstd, dxhat):
    return rstd * (dxhat - xhat * jnp.mean(dxhat * xhat, axis=-1, keepdims=True))


def _row_spec(cols):
    return pl.BlockSpec((ROW_BLK, cols), lambda i: (i, 0))


def _vec_spec(cols):
    return pl.BlockSpec((1, cols), lambda i: (0, 0))


def _prologue(x, g, rel_bias, idx_all, *, name, carry=None):
    T, D = x.shape
    n_pat = idx_all.shape[0]
    heads = 2 * N_PAIRS
    steps = n_pat * heads
    rows = T // steps

    def body(rb_ref, x_ref, g_ref, idx_ref, n_ref, bias_ref):
        s = pl.program_id(0)
        head = jnp.where(s < heads, s, heads + s % heads)
        xv = x_ref[...]
        n_ref[...] = (xv * _rstd(xv) * g_ref[...]).astype(n_ref.dtype)
        idxv = idx_ref[...]
        acc = jnp.where(idxv < 0, NEG, 0.0).astype(F32)
        for b in range(NUM_BUCKETS):
            acc = acc + jnp.where(idxv == b, rb_ref[b, head], 0.0)
        bias_ref[0] = acc
        kap = lax.broadcasted_iota(jnp.int32, (1, 2 * QBLK), 1)
        bias_ref[1] = jnp.where(kap < QBLK, NEG, acc)

    return _call(
        body, name=name, grid=(steps,),
        in_specs=[pl.BlockSpec(memory_space=pltpu.SMEM), pl.BlockSpec((rows, D), lambda s: (s, 0)),
                  pl.BlockSpec((1, D), lambda s: (0, 0)),
                  pl.BlockSpec((None, QBLK, 2 * QBLK), lambda s: (s // heads, 0, 0))],
        out_specs=[pl.BlockSpec((rows, D), lambda s: (s, 0)),
                   pl.BlockSpec((None, 2, None, QBLK, 2 * QBLK), lambda s: (s // heads, 0, s % heads, 0, 0))],
        out_shape=[jax.ShapeDtypeStruct((T, D), BF16),
                   jax.ShapeDtypeStruct((n_pat, 2, heads, QBLK, 2 * QBLK), F32)],
        args=[rel_bias, x, g, idx_all], carry=carry)


def _mm_resid_norm(a, b, h, g_post, coef, g_next, *, bm, name, bias=None, carry=None):
    M, K = a.shape
    N = b.shape[1]

    def body(*refs):
        a_ref, b_ref, h_ref, gp_ref, gn_ref = refs[0:5]
        f_ref, hn_ref, n_ref = refs[-3:]
        for rows in _row_halves(bm):
            f = jnp.dot(a_ref[rows, :], b_ref[...], preferred_element_type=F32)
            if bias is not None:
                f = f + refs[5][...]
            f_ref[rows, :] = f
            hn = h_ref[rows, :] + coef * (f * _rstd(f) * gp_ref[...])
            hn_ref[rows, :] = hn
            n_ref[rows, :] = (hn * _rstd(hn) * gn_ref[...]).astype(n_ref.dtype)

    row = lambda w: pl.BlockSpec((bm, w), lambda i: (i, 0))
    vec = pl.BlockSpec((1, N), lambda i: (0, 0))
    in_specs = [row(K), pl.BlockSpec((K, N), lambda i: (0, 0), pipeline_mode=pl.Buffered(1)), row(N), vec, vec]
    args = [a, b, h, g_post, g_next]
    if bias is not None:
        in_specs.append(vec)
        args.append(bias)
    limit = _vmem_limit(_nbytes((bm, K), a.dtype), _nbytes((K, N), b.dtype) // 2, 4 * _nbytes((bm, N), F32))
    return _call(body, name=name, grid=(M // bm,), in_specs=in_specs, out_specs=[row(N), row(N), row(N)],
                 out_shape=[jax.ShapeDtypeStruct((M, N), F32), jax.ShapeDtypeStruct((M, N), F32),
                            jax.ShapeDtypeStruct((M, N), BF16)], args=args, vmem_limit=limit, carry=carry)


def _mm_nt_norms_bwd(a, b, dh_in, h, g_pre, f, g_post, coef, *, bm, name, add=None, b_is_kn=False, carry=None):
    M, K = a.shape
    stacked = b.ndim == 3
    N = b.shape[1] if (stacked or b_is_kn) else b.shape[0]
    n_sh = b.shape[0] if stacked else 1
    ks = b.shape[2] if stacked else K
    assert n_sh * ks == K
    dn_dims = (((1,), (0,)), ((), ())) if b_is_kn else (((1,), (1,)), ((), ()))
    with_f = f is not None
    n_in = 5 + (2 if with_f else 0) + (1 if add is not None else 0)

    def body(*refs):
        a_ref, b_ref, dhi_ref, h_ref, gpre_ref = refs[0:5]
        outs = refs[n_in:]

        @pl.when(pl.program_id(0) == 0)
        def _():
            for acc in (outs[2:] if with_f else outs[1:]):
                acc[...] = jnp.zeros_like(acc)

        for rows in _row_halves(bm):
            if stacked:
                dn = lax.dot_general(a_ref[rows, 0:ks], b_ref[0], dn_dims, preferred_element_type=F32)
                for s in range(1, n_sh):
                    dn = dn + lax.dot_general(a_ref[rows, s * ks:(s + 1) * ks], b_ref[s], dn_dims,
                                              preferred_element_type=F32)
            else:
                dn = lax.dot_general(a_ref[rows, :], b_ref[...], dn_dims, preferred_element_type=F32)
            if add is not None:
                dn = dn + refs[n_in - 1][rows, :]
            hv = h_ref[rows, :]
            r = _rstd(hv)
            hh = hv * r
            dh = dhi_ref[rows, :] + _norm_bwd(hh, r, dn * gpre_ref[...])
            outs[0][rows, :] = dh
            if not with_f:
                outs[1][...] += jnp.sum(dn * hh, axis=0, keepdims=True)
                continue
            f_ref, gpost_ref = refs[5], refs[6]
            df_ref, dgpre_ref, dgpost_ref, dsum_ref = outs[1:]
            dgpre_ref[...] += jnp.sum(dn * hh, axis=0, keepdims=True)
            fv = f_ref[rows, :]
            rf = _rstd(fv)
            fh = fv * rf
            dy = coef * dh
            dgpost_ref[...] += jnp.sum(dy * fh, axis=0, keepdims=True)
            df = _norm_bwd(fh, rf, dy * gpost_ref[...])
            dsum_ref[...] += jnp.sum(df, axis=0, keepdims=True)
            df_ref[rows, :] = df.astype(df_ref.dtype)

    row = lambda w: pl.BlockSpec((bm, w), lambda i: (i, 0))
    vec = pl.BlockSpec((1, N), lambda i: (0, 0))
    once = pl.Buffered(1)
    if stacked:
        b_spec = pl.BlockSpec((n_sh, N, ks), lambda i: (0, 0, 0), pipeline_mode=once)
    else:
        b_spec = pl.BlockSpec(b.shape, lambda i: (0, 0), pipeline_mode=once)
    in_specs = [row(K), b_spec, row(N), row(N), vec]
    args = [a, b, dh_in, h, g_pre]
    if with_f:
        in_specs += [row(N), vec]
        args += [f, g_post]
    if add is not None:
        in_specs.append(row(N))
        args.append(add)
    vec_shape = jax.ShapeDtypeStruct((1, N), F32)
    out_specs = [row(N)] + ([row(N), vec, vec, vec] if with_f else [vec])
    out_shape = [jax.ShapeDtypeStruct((M, N), F32)]
    out_shape += [jax.ShapeDtypeStruct((M, N), BF16), vec_shape, vec_shape, vec_shape] if with_f else [vec_shape]
    limit = _vmem_limit(_nbytes((bm, K), a.dtype), _nbytes((N, K), b.dtype) // 2, 6 * _nbytes((bm, N), F32))
    return _call(body, name=name, grid=(M // bm,), in_specs=in_specs, out_specs=out_specs, out_shape=out_shape,
                 args=args, vmem_limit=limit, semantics=("arbitrary",), carry=carry)


def _ffn_up(n, w_gu, *, bm, name, carry=None):
    M, K = n.shape
    n_sh, _, ns = w_gu.shape
    half = n_sh // 2

    def body(n_ref, wg_ref, wu_ref, g_ref, u_ref, a_ref):
        nv = n_ref[...]
        for cols in _col_chunks(ns):
            g = jnp.dot(nv, wg_ref[:, cols], preferred_element_type=F32)
            u = jnp.dot(nv, wu_ref[:, cols], preferred_element_type=F32)
            g_ref[:, cols] = g.astype(g_ref.dtype)
            u_ref[:, cols] = u.astype(u_ref.dtype)
            a_ref[:, cols] = (g * jax.nn.sigmoid(g) * u).astype(a_ref.dtype)

    out_spec = pl.BlockSpec((bm, ns), lambda i, j: (i, j))
    out = jax.ShapeDtypeStruct((M, half * ns), BF16)
    limit = _vmem_limit(_nbytes((bm, K), n.dtype), 2 * _nbytes((K, ns), w_gu.dtype), 3 * _nbytes((bm, ns), F32))
    return _call(body, name=name, grid=(M // bm, half),
                 in_specs=[pl.BlockSpec((bm, K), lambda i, j: (i, 0)),
                           pl.BlockSpec((None, K, ns), lambda i, j: (j, 0, 0)),
                           pl.BlockSpec((None, K, ns), lambda i, j: (j + half, 0, 0))],
                 out_specs=[out_spec, out_spec, out_spec], out_shape=[out, out, out], args=[n, w_gu, w_gu],
                 vmem_limit=limit, carry=carry)


def _ffn_down_bwd(df, w_down, g, u, *, bm, name, carry=None):
    M, D = df.shape
    F = w_down.shape[0]
    dn = (((1,), (1,)), ((), ()))

    def body(df_ref, w_ref, g_ref, u_ref, o_ref):
        dfv = df_ref[...]
        for cols in _col_chunks(F):
            da = lax.dot_general(dfv, w_ref[cols, :], dn, preferred_element_type=F32)
            gv = g_ref[:, cols].astype(F32)
            s = jax.nn.sigmoid(gv)
            o_ref[:, cols] = (da * u_ref[:, cols].astype(F32) * (s * (1.0 + gv * (1.0 - s)))).astype(o_ref.dtype)
            o_ref[:, F + cols.start:F + cols.stop] = (da * (gv * s)).astype(o_ref.dtype)

    row = lambda w: pl.BlockSpec((bm, w), lambda i: (i, 0))
    limit = _vmem_limit(_nbytes((F, D), w_down.dtype) // 2, 2 * _nbytes((bm, F), BF16), _nbytes((bm, 2 * F), BF16))
    return _call(body, name=name, grid=(M // bm,),
                 in_specs=[row(D), pl.BlockSpec((F, D), lambda i: (0, 0), pipeline_mode=pl.Buffered(1)), row(F), row(F)],
                 out_specs=[row(2 * F)], out_shape=[jax.ShapeDtypeStruct((M, 2 * F), BF16)],
                 args=[df, w_down, g, u], vmem_limit=limit, carry=carry)[0]


def _ple_head(n4, w_gate, p, w_ple, h3, g_post, target, *, bm, name):
    T, D = h3.shape
    kp = p.shape[1]

    def body(n_ref, wg_ref, p_ref, wp_ref, h_ref, g_ref, t_ref, dh_ref, de_ref, dgp_ref, loss_ref, dg_ref):
        @pl.when(pl.program_id(0) == 0)
        def _():
            loss_ref[...] = jnp.zeros_like(loss_ref)
            dg_ref[...] = jnp.zeros_like(dg_ref)

        gpost = g_ref[...]
        for rows in _row_halves(bm):
            gate = jax.nn.sigmoid(jnp.dot(n_ref[rows, :], wg_ref[...], preferred_element_type=F32))
            ev = jnp.dot(p_ref[rows, :], wp_ref[...], preferred_element_type=F32)
            ge = gate * ev
            r = _rstd(ge)
            gh = ge * r
            diff = h_ref[rows, :] + gh * gpost - t_ref[rows, :]
            loss_ref[...] += jnp.sum(diff * diff) * (0.5 / D)
            dh = diff * (1.0 / D)
            dh_ref[rows, :] = dh
            dg_ref[...] += jnp.sum(dh * gh, axis=0, keepdims=True)
            dge = _norm_bwd(gh, r, dh * gpost)
            de_ref[rows, :] = (dge * gate).astype(de_ref.dtype)
            dgp_ref[rows, :] = (dge * ev * gate * (1.0 - gate)).astype(dgp_ref.dtype)

    row = lambda w: pl.BlockSpec((bm, w), lambda i: (i, 0))
    whole = lambda a: pl.BlockSpec(a.shape, lambda i: (0, 0), pipeline_mode=pl.Buffered(1))
    limit = _vmem_limit(_nbytes((bm, D), BF16), _nbytes(w_gate.shape, w_gate.dtype) // 2, 6 * _nbytes((bm, D), F32))
    return pl.pallas_call(
        body, name=name, grid=(T // bm,),
        in_specs=[row(D), whole(w_gate), row(kp), whole(w_ple), row(D), _vec_spec(D), row(D)],
        out_specs=(row(D), row(D), row(D), _vec_spec(LANES), _vec_spec(D)),
        out_shape=(jax.ShapeDtypeStruct((T, D), F32), jax.ShapeDtypeStruct((T, D), BF16),
                   jax.ShapeDtypeStruct((T, D), BF16), jax.ShapeDtypeStruct((1, LANES), F32),
                   jax.ShapeDtypeStruct((1, D), F32)),
        compiler_params=pltpu.CompilerParams(dimension_semantics=("arbitrary",), vmem_limit_bytes=limit),
    )(n4, w_gate, p, w_ple, h3, g_post, target)


def _t5_index(max_dist, stride):
    rho = np.arange(QBLK)[:, None]
    kap = np.arange(2 * QBLK)[None, :]
    d = rho + QBLK - kap
    valid = (d >= 0) & (d <= max_dist)
    n = np.maximum(d, 0) * stride
    max_exact = NUM_BUCKETS // 2
    nf = np.maximum(n, 1).astype(np.float32)
    large = max_exact + (np.log(nf / np.float32(max_exact)) / np.float32(math.log(MAX_DISTANCE / max_exact))
                         * np.float32(NUM_BUCKETS - max_exact)).astype(np.int32)
    large = np.minimum(large, NUM_BUCKETS - 1)
    bucket = np.where(n < max_exact, n, large)
    return np.where(valid, bucket, -1).astype(np.int32)


def _bias_grad(d_bias, idx, *, name):
    def body(db_ref, idx_ref, o_ref):
        h = pl.program_id(0)

        @pl.when(h == 0)
        def _():
            o_ref[...] = jnp.zeros_like(o_ref)

        idxv = idx_ref[...]
        dv = db_ref[0]
        rows = lax.broadcasted_iota(jnp.int32, (NUM_BUCKETS, LANES), 0)
        lanes = lax.broadcasted_iota(jnp.int32, (NUM_BUCKETS, LANES), 1)
        acc = o_ref[...]
        for b in range(NUM_BUCKETS):
            s = jnp.sum(jnp.where(idxv == b, dv, 0.0))
            acc = acc + jnp.where((rows == b) & (lanes == h), s, 0.0)
        o_ref[...] = acc

    return pl.pallas_call(
        body, name=name, grid=(2 * N_PAIRS,),
        in_specs=[pl.BlockSpec((1, QBLK, 2 * QBLK), lambda h: (h, 0, 0)),
                  pl.BlockSpec((QBLK, 2 * QBLK), lambda h: (0, 0))],
        out_specs=pl.BlockSpec((NUM_BUCKETS, LANES), lambda h: (0, 0)),
        out_shape=jax.ShapeDtypeStruct((NUM_BUCKETS, LANES), F32),
        compiler_params=pltpu.CompilerParams(dimension_semantics=("arbitrary",)),
    )(d_bias, idx)


def _lane():
    return lax.broadcasted_iota(jnp.int32, (1, LANES), 1)


def _head_sel(h):
    return (_lane() < HEAD_DIM) if h == 0 else (_lane() >= HEAD_DIM)


def _stat_lo():
    return (_lane() % HEAD_DIM) < (HEAD_DIM // 2)


def _stack_heads(t, scale=None):
    if scale is not None:
        t = t * scale
    zero = jnp.zeros_like(t)
    return jnp.concatenate([jnp.where(_head_sel(0), t, zero), jnp.where(_head_sel(1), t, zero)], axis=0)


def _class_spec(L, r, cols, stride, pps):
    chunks = N_PAIRS // pps
    mode = pl.Buffered(1) if r * chunks == 1 else None
    return pl.BlockSpec((L, MIX_W // chunks), lambda j, c: (0, (cols + j * stride) * chunks + c), pipeline_mode=mode)


def _rows(b, n_blocks=1):
    return pl.ds(pl.multiple_of(b * QBLK, QBLK), n_blocks * QBLK)


def _key_window(ref, b, cols, first):
    if first:
        blk = ref[0:QBLK, cols]
        return jnp.concatenate([blk, blk], axis=0)
    return ref[_rows(b - 1, 2), cols]


def _band_fwd(qa, ka, va, bias, *, r, q_col, k_col, v_col, stride, pattern, name, carry=None):
    L = qa.shape[0]
    nb = L // QBLK
    dn = (((1,), (1,)), ((), ()))
    scale = HEAD_DIM ** -0.5

    pps = N_PAIRS

    def body(q_ref, k_ref, v_ref, bias_ref, o_ref, lse_ref):
        sel0 = _head_sel(0)
        pair0 = pl.program_id(1) * pps

        def block(b, first):
            rows = _rows(b)
            for i in range(pps):
                cols = slice(i * LANES, (i + 1) * LANES)
                q2 = _stack_heads(q_ref[rows, cols], scale)
                k = _key_window(k_ref, b, cols, first)
                v = _key_window(v_ref, b, cols, first)
                bias2 = bias_ref[1 if first else 0, pl.ds(2 * (pair0 + i), 2)].reshape(2 * QBLK, 2 * QBLK)
                s = lax.dot_general(q2, k, dn, preferred_element_type=F32) + bias2
                m = jnp.max(s, axis=1, keepdims=True)
                p = jnp.exp(s - m)
                l = jnp.sum(p, axis=1, keepdims=True)
                o = jnp.dot(p.astype(v.dtype), v, preferred_element_type=F32) * (1.0 / l)
                lse = m + jnp.log(l)
                o_ref[rows, cols] = jnp.where(sel0, o[0:QBLK], o[QBLK:]).astype(o_ref.dtype)
                lse_ref[rows, cols] = jnp.where(sel0, lse[0:QBLK], lse[QBLK:])

        block(0, True)
        if nb > 1:
            pl.loop(1, nb)(lambda b: block(b, False))

    bias_spec = pl.BlockSpec((None, 2, 2 * N_PAIRS, QBLK, 2 * QBLK), lambda j, c: (pattern, 0, 0, 0, 0))
    out_spec = _class_spec(L, r, 0, 1, pps)
    blk_bytes = _nbytes((L, pps * LANES), F32)
    return _call(
        body, name=name, grid=(r, N_PAIRS // pps),
        in_specs=[_class_spec(L, r, q_col, stride, pps), _class_spec(L, r, k_col, stride, pps),
                  _class_spec(L, r, v_col, stride, pps), bias_spec],
        out_specs=[out_spec, out_spec],
        out_shape=[jax.ShapeDtypeStruct((L, r * MIX_W), BF16), jax.ShapeDtypeStruct((L, r * MIX_W), F32)],
        args=[qa, ka, va, bias], vmem_limit=_vmem_limit(*([blk_bytes] * 4)), carry=carry)


def _class_rows_spec(rows, r, width):
    return pl.BlockSpec((rows // r, r * width), lambda i, *_: (i, 0))


def _token_scratch(rows, width):
    return pltpu.VMEM((width // LANES, rows, LANES), F32)


def _fill_tokens(scratch, value):
    for c in range(scratch.shape[0]):
        scratch[c] = value[:, c * LANES:(c + 1) * LANES]


def _read_tokens(scratch):
    return jnp.concatenate([scratch[c] for c in range(scratch.shape[0])], axis=1)


def _to_tokens(blk_ref, r, scratch):
    if r == 1:
        return blk_ref[...].astype(F32)
    nt, rows, _ = scratch.shape
    for j in range(r):
        for c in range(nt):
            lanes = slice((j * nt + c) * LANES, (j * nt + c + 1) * LANES)
            scratch.at[c][pl.ds(j, rows // r, stride=r), :] = blk_ref[:, lanes].astype(F32)
    return _read_tokens(scratch)


def _from_tokens(dst_ref, r, scratch):
    nt, rows, _ = scratch.shape
    if r == 1:
        dst_ref[...] = _read_tokens(scratch).astype(dst_ref.dtype)
        return
    for j in range(r):
        for c in range(nt):
            lanes = slice((j * nt + c) * LANES, (j * nt + c + 1) * LANES)
            dst_ref[:, lanes] = scratch.at[c][pl.ds(j, rows // r, stride=r), :].astype(dst_ref.dtype)


def _mm_nt_classes(a, b, bias, dils, *, bm, name):
    M, K = a.shape
    N = b.shape[0]
    dn = (((1,), (1,)), ((), ()))

    def body(a_ref, b_ref, bias_ref, *rest):
        outs, tile = rest[:-1], rest[-1]
        _fill_tokens(tile, lax.dot_general(a_ref[...], b_ref[...], dn, preferred_element_type=F32) + bias_ref[...])
        for out, r in zip(outs, dils):
            _from_tokens(out, r, tile)

    limit = _vmem_limit(_nbytes((bm, K), a.dtype), _nbytes((K, N), b.dtype) // 2, (1 + len(dils)) * _nbytes((bm, N), F32))
    return pl.pallas_call(
        body, name=name, grid=(M // bm,),
        in_specs=[pl.BlockSpec((bm, K), lambda i: (i, 0)),
                  pl.BlockSpec((N, K), lambda i: (0, 0), pipeline_mode=pl.Buffered(1)),
                  pl.BlockSpec((1, N), lambda i: (0, 0))],
        out_specs=tuple(_class_rows_spec(bm, r, N) for r in dils),
        out_shape=tuple(jax.ShapeDtypeStruct((M // r, r * N), BF16) for r in dils),
        scratch_shapes=[_token_scratch(bm, N)],
        compiler_params=pltpu.CompilerParams(dimension_semantics=("parallel",), vmem_limit_bytes=limit),
    )(a, b, bias)


def _merge(branches, dils, sink, *, name):
    W = MIX_W
    T = branches[0][0].shape[0] * dils[0]
    nbr = len(branches)

    def body(*refs):
        sink_ref = refs[2 * nbr]
        out_ref, outb_ref, lse_ref, scratch = refs[2 * nbr + 1:]
        sk = sink_ref[...]
        lses = [_to_tokens(refs[2 * t + 1], dils[t], scratch) for t in range(nbr)]
        mx = sk
        for lse in lses:
            mx = jnp.maximum(mx, lse)
        den = jnp.exp(sk - mx)
        num = jnp.zeros_like(mx)
        for t in range(nbr):
            w = jnp.exp(lses[t] - mx)
            den = den + w
            num = num + w * _to_tokens(refs[2 * t], dils[t], scratch)
        out = num / den
        out_ref[...] = out
        outb_ref[...] = out.astype(outb_ref.dtype)
        lse_ref[...] = mx + jnp.log(den)

    args = [a for br in branches for a in br] + [sink]
    in_specs = [_class_rows_spec(ROW_BLK, r, W) for r in dils for _ in range(2)] + [_vec_spec(W)]
    return pl.pallas_call(
        body, name=name, grid=(T // ROW_BLK,), in_specs=in_specs,
        out_specs=(_row_spec(W), _row_spec(W), _row_spec(W)),
        out_shape=(jax.ShapeDtypeStruct((T, W), F32), jax.ShapeDtypeStruct((T, W), BF16),
                   jax.ShapeDtypeStruct((T, W), F32)),
        scratch_shapes=[_token_scratch(ROW_BLK, W)],
        compiler_params=pltpu.CompilerParams(dimension_semantics=("parallel",)),
    )(*args)


def _attn_delta(dmix, outs, lses, sink, dils, *, name):
    T = dmix.shape[0]
    W = MIX_W
    nd = len(dils)

    def body(dmix_ref, oa_ref, ob_ref, la_ref, lb_ref, sink_ref, *rest):
        doa_ref, sta_ref, dsink_ref = rest[0:3]
        dob_refs, stb_refs = rest[3:3 + nd], rest[3 + nd:3 + 2 * nd]
        do_tile, st_tile = rest[3 + 2 * nd:]

        @pl.when(pl.program_id(0) == 0)
        def _():
            dsink_ref[...] = jnp.zeros_like(dsink_ref)

        sel0, lo = _head_sel(0), _stat_lo()
        for mixer, (o_ref, l_ref) in enumerate(((oa_ref, la_ref), (ob_ref, lb_ref))):
            for i in range(N_PAIRS):
                cols = slice(i * LANES, (i + 1) * LANES)
                do = dmix_ref[:, mixer * W + i * LANES:mixer * W + (i + 1) * LANES]
                prod = do * o_ref[:, cols]
                d0 = jnp.sum(jnp.where(sel0, prod, 0.0), axis=1, keepdims=True)
                d1 = jnp.sum(jnp.where(sel0, 0.0, prod), axis=1, keepdims=True)
                delta = jnp.where(sel0, d0, d1)
                lse = l_ref[:, cols]
                stat = jnp.where(lo, lse, delta)
                if mixer == 0:
                    sta_ref[:, cols] = stat
                    doa_ref[:, cols] = do.astype(doa_ref.dtype)
                    dsink_ref[:, cols] += -jnp.sum(jnp.exp(sink_ref[:, cols] - lse) * delta, axis=0, keepdims=True)
                else:
                    st_tile[i] = stat
                    do_tile[i] = do
        for t, r in enumerate(dils):
            _from_tokens(dob_refs[t], r, do_tile)
            _from_tokens(stb_refs[t], r, st_tile)

    class_specs = [_class_rows_spec(ROW_BLK, r, W) for r in dils]
    out_shape = [jax.ShapeDtypeStruct((T, W), BF16), jax.ShapeDtypeStruct((T, W), F32), jax.ShapeDtypeStruct((1, W), F32)]
    out_shape += [jax.ShapeDtypeStruct((T // r, r * W), BF16) for r in dils]
    out_shape += [jax.ShapeDtypeStruct((T // r, r * W), F32) for r in dils]
    res = pl.pallas_call(
        body, name=name, grid=(T // ROW_BLK,),
        in_specs=[_row_spec(2 * W), _row_spec(W), _row_spec(W), _row_spec(W), _row_spec(W), _vec_spec(W)],
        out_specs=tuple([_row_spec(W), _row_spec(W), _vec_spec(W)] + class_specs + class_specs),
        out_shape=tuple(out_shape),
        scratch_shapes=[_token_scratch(ROW_BLK, W), _token_scratch(ROW_BLK, W)],
        compiler_params=pltpu.CompilerParams(dimension_semantics=("arbitrary",)),
    )(dmix, outs[0], outs[1], lses[0], lses[1], sink)
    return res[0], res[1], res[2], res[3:3 + nd], res[3 + nd:]


def _band_bwd(qa, ka, va, d_out, stat, bias, *, r, q_col, k_col, v_col, stride, pattern, name, carry=None):
    L = qa.shape[0]
    nb = L // QBLK
    dn_nt = (((1,), (1,)), ((), ()))
    dn_tn = (((0,), (0,)), ((), ()))
    scale = HEAD_DIM ** -0.5

    pps = N_PAIRS if r > 1 else N_PAIRS // 2

    def body(q_ref, k_ref, v_ref, do_ref, st_ref, bias_ref, dq_ref, dk_ref, dv_ref, db_ref, dk_carry, dv_carry):
        @pl.when((pl.program_id(0) == 0) & (pl.program_id(1) == 0))
        def _():
            db_ref[...] = jnp.zeros_like(db_ref)

        lo, sel0 = _stat_lo(), _head_sel(0)
        pair0 = pl.program_id(1) * pps

        def block(b, first):
            rows = _rows(b)
            for i in range(pps):
                heads = pl.ds(2 * (pair0 + i), 2)
                cols = slice(i * LANES, (i + 1) * LANES)
                q2 = _stack_heads(q_ref[rows, cols], scale)
                k = _key_window(k_ref, b, cols, first)
                v = _key_window(v_ref, b, cols, first)
                do2 = _stack_heads(do_ref[rows, cols])
                st = st_ref[rows, cols]
                lse2 = jnp.concatenate(
                    [jnp.max(jnp.where(_head_sel(h) & lo, st, -jnp.inf), axis=1, keepdims=True) for h in range(2)], axis=0)
                delta2 = jnp.concatenate(
                    [jnp.sum(jnp.where(_head_sel(h) & ~lo, st, 0.0), axis=1, keepdims=True) for h in range(2)],
                    axis=0) * (2.0 / HEAD_DIM)
                bias2 = bias_ref[1 if first else 0, heads].reshape(2 * QBLK, 2 * QBLK)
                s = lax.dot_general(q2, k, dn_nt, preferred_element_type=F32) + bias2
                p = jnp.exp(s - lse2)
                dp = lax.dot_general(do2, v, dn_nt, preferred_element_type=F32)
                ds = p * (dp - delta2)
                db_ref[heads] += ds.reshape(2, QBLK, 2 * QBLK)
                dsb = ds.astype(k.dtype)
                dq2 = jnp.dot(dsb, k, preferred_element_type=F32)
                dq_ref[rows, cols] = (jnp.where(sel0, dq2[0:QBLK], dq2[QBLK:]) * scale).astype(dq_ref.dtype)
                dk_acc = lax.dot_general(dsb, q2, dn_tn, preferred_element_type=F32)
                dv_acc = lax.dot_general(p.astype(k.dtype), do2, dn_tn, preferred_element_type=F32)
                if not first:
                    prev = _rows(b - 1)
                    dk_ref[prev, cols] = (dk_carry[:, cols] + dk_acc[0:QBLK]).astype(dk_ref.dtype)
                    dv_ref[prev, cols] = (dv_carry[:, cols] + dv_acc[0:QBLK]).astype(dv_ref.dtype)
                dk_carry[:, cols] = dk_acc[QBLK:2 * QBLK]
                dv_carry[:, cols] = dv_acc[QBLK:2 * QBLK]

        block(0, True)
        if nb > 1:
            pl.loop(1, nb)(lambda b: block(b, False))

        last = _rows(nb - 1)
        dk_ref[last, :] = dk_carry[...].astype(dk_ref.dtype)
        dv_ref[last, :] = dv_carry[...].astype(dv_ref.dtype)

    tok_spec = _class_spec(L, r, 0, 1, pps)
    bias_spec = pl.BlockSpec((None, 2, 2 * N_PAIRS, QBLK, 2 * QBLK), lambda j, c: (pattern, 0, 0, 0, 0))
    dbias_spec = pl.BlockSpec((2 * N_PAIRS, QBLK, 2 * QBLK), lambda j, c: (0, 0, 0))
    grad = jax.ShapeDtypeStruct((L, r * MIX_W), BF16)
    blk_bytes = _nbytes((L, pps * LANES), BF16)
    carry_shape = pltpu.VMEM((QBLK, pps * LANES), F32)
    return _call(
        body, name=name, grid=(r, N_PAIRS // pps),
        in_specs=[_class_spec(L, r, q_col, stride, pps), _class_spec(L, r, k_col, stride, pps),
                  _class_spec(L, r, v_col, stride, pps), tok_spec, tok_spec, bias_spec],
        out_specs=[tok_spec, tok_spec, tok_spec, dbias_spec],
        out_shape=[grad, grad, grad, jax.ShapeDtypeStruct((2 * N_PAIRS, QBLK, 2 * QBLK), F32)],
        args=[qa, ka, va, d_out, stat, bias], scratch=[carry_shape, carry_shape],
        vmem_limit=_vmem_limit(*([blk_bytes] * 9)), semantics=("arbitrary", "arbitrary"), carry=carry)


def _dz_assemble(dq_a, dk_a, dv_a, b_grads, dils, *, name):
    T = dq_a.shape[0]
    W = MIX_W

    def group_sum(x):
        u0 = x[:, 0:LANES] + x[:, LANES:2 * LANES]
        u1 = x[:, 2 * LANES:3 * LANES] + x[:, 3 * LANES:4 * LANES]
        s0 = u0 + pltpu.roll(u0, HEAD_DIM, 1)
        s1 = u1 + pltpu.roll(u1, HEAD_DIM, 1)
        return jnp.where(_head_sel(0), s0, s1)

    def body(*refs):
        dqa_ref, dka_ref, dva_ref = refs[0:3]
        b_refs = refs[3:12]
        dza_ref, dzb_ref, sa_ref, sb_ref, scratch = refs[12:]

        @pl.when(pl.program_id(0) == 0)
        def _():
            sa_ref[...] = jnp.zeros_like(sa_ref)
            sb_ref[...] = jnp.zeros_like(sb_ref)

        def put(dst, acc, col, part):
            w = part.shape[1]
            dst[:, col:col + w] = part.astype(dst.dtype)
            acc[:, col:col + w] += jnp.sum(part, axis=0, keepdims=True)

        put(dza_ref, sa_ref, 0, dqa_ref[...].astype(F32))
        put(dza_ref, sa_ref, W, group_sum(dka_ref[...].astype(F32)))
        put(dza_ref, sa_ref, W + LANES, group_sum(dva_ref[...].astype(F32)))
        for t in range(3):
            part = _to_tokens(b_refs[t], dils[0], scratch)
            for pat in range(1, len(dils)):
                part = part + _to_tokens(b_refs[3 * pat + t], dils[pat], scratch)
            put(dzb_ref, sb_ref, t * W, part)

    args = [dq_a, dk_a, dv_a] + [g[t] for g in b_grads for t in range(3)]
    return pl.pallas_call(
        body, name=name, grid=(T // ROW_BLK,),
        in_specs=[_row_spec(W)] * 3 + [_class_rows_spec(ROW_BLK, r, W) for r in dils for _ in range(3)],
        out_specs=(_row_spec(ZA_W), _row_spec(ZB_W), _vec_spec(ZA_W), _vec_spec(ZB_W)),
        out_shape=(jax.ShapeDtypeStruct((T, ZA_W), BF16), jax.ShapeDtypeStruct((T, ZB_W), BF16),
                   jax.ShapeDtypeStruct((1, ZA_W), F32), jax.ShapeDtypeStruct((1, ZB_W), F32)),
        scratch_shapes=[_token_scratch(ROW_BLK, W)],
        compiler_params=pltpu.CompilerParams(dimension_semantics=("arbitrary",)),
    )(*args)


def _place():
    x, y, c = lax.axis_index("x"), lax.axis_index("y"), lax.axis_index("c")
    chips = [(1 - x, y), (x, 1 - y), (1 - x, 1 - y)]
    return x, y, c, chips


def _cast_place(shards, k_idx, *, name):
    n, steps = len(shards), 4

    def body(k_ref, *refs):
        for s_ref, o_ref in zip(refs[:n], refs[n:]):
            o_ref[...] = s_ref[...].astype(o_ref.dtype)

    blocks = [(a.shape[0] // steps, a.shape[1]) for a in shards]
    grid_spec = pltpu.PrefetchScalarGridSpec(
        num_scalar_prefetch=1, grid=(steps,),
        in_specs=[pl.BlockSpec(blk, lambda t, k_ref: (t, 0)) for blk in blocks],
        out_specs=tuple(pl.BlockSpec(blk, lambda t, k_ref: (k_ref[0] * steps + t, 0)) for blk in blocks))
    return pl.pallas_call(
        body, name=name, grid_spec=grid_spec,
        out_shape=tuple(jax.ShapeDtypeStruct((N_CHIPS * a.shape[0], a.shape[1]), BF16) for a in shards),
        compiler_params=pltpu.CompilerParams(dimension_semantics=("parallel",),
                                             vmem_limit_bytes=_vmem_limit(*[_nbytes(b, F32) for b in blocks])),
    )(k_idx, *shards)


def _gather_carry(fulls, part=(0, 1)):
    n = len(fulls)
    p_idx, n_parts = part

    def piece(ref, chip_idx, half):
        rs = ref.shape[0] // N_CHIPS
        rh = rs // 2
        rows = rh // n_parts
        return ref.at[pl.ds(chip_idx * rs + half * rh + p_idx * rows, rows)]

    def copy(sems, sem, src_ref, dst_ref, to):
        return pltpu.make_async_remote_copy(src_ref=src_ref, dst_ref=dst_ref, send_sem=sems[0].at[sem],
                                            recv_sem=sems[1].at[sem], device_id=to, device_id_type=MESH)

    def ici_copy(ins, outs, sems, i, j, chip, k, c):
        return copy(sems, 3 * i + j, piece(ins[i], k, c), piece(outs[i], k, c), (*chip, c))

    def start(ins, outs, sems):
        x, y, c, chips = _place()
        for i in range(n):
            for j, chip in enumerate(chips):
                ici_copy(ins, outs, sems, i, j, chip, 2 * x + y, c).start()

    def finish(ins, outs, sems):
        x, y, c, chips = _place()
        sibling = (x, y, 1 - c)
        passed = []
        for i in range(n):
            for j, chip in enumerate(chips):
                region = piece(outs[i], 2 * chip[0] + chip[1], c)
                copy(sems, 3 * i + j, region, region, (*chip, c)).wait_recv()
                cp = copy(sems, 3 * n + 3 * i + j, region, region, sibling)
                cp.start()
                passed.append(cp)
        for i in range(n):
            for j, chip in enumerate(chips):
                region = piece(outs[i], 2 * chip[0] + chip[1], 1 - c)
                copy(sems, 3 * n + 3 * i + j, region, region, sibling).wait_recv()
        for i in range(n):
            for j, chip in enumerate(chips):
                ici_copy(ins, outs, sems, i, j, chip, 2 * x + y, c).wait_send()
        for cp in passed:
            cp.wait_send()

    return _Carry(fulls, [jax.ShapeDtypeStruct(a.shape, a.dtype) for a in fulls], {i: i for i in range(n)},
                  [pltpu.SemaphoreType.DMA((6 * n,)), pltpu.SemaphoreType.DMA((6 * n,))], start, finish)


def _pair_swap_carry(grads):
    n = len(grads)

    def copy(ins, outs, sems, i):
        x, y, c, _ = _place()
        return pltpu.make_async_remote_copy(src_ref=ins[i].at[:, 1 - c], dst_ref=outs[i], send_sem=sems[0].at[i],
                                            recv_sem=sems[1].at[i], device_id=(x, y, 1 - c), device_id_type=MESH)

    def start(ins, outs, sems):
        for i in range(n):
            copy(ins, outs, sems, i).start()

    def finish(ins, outs, sems):
        for i in range(n):
            copy(ins, outs, sems, i).wait()

    return _Carry(grads, [jax.ShapeDtypeStruct((a.shape[0], a.shape[2], a.shape[3]), a.dtype) for a in grads], {},
                  [pltpu.SemaphoreType.DMA((n,)), pltpu.SemaphoreType.DMA((n,))], start, finish)


def _pair_sum(g4, got, c_idx, *, name):
    _, _, rh, cs = g4.shape
    rb = _row_block(rh, 512, 16)

    def body(c_ref, own_ref, got_ref, o_ref):
        o_ref[...] = (own_ref[...].astype(F32) + got_ref[...].astype(F32)).astype(o_ref.dtype)

    grid_spec = pltpu.PrefetchScalarGridSpec(
        num_scalar_prefetch=1, grid=(N_CHIPS, rh // rb),
        in_specs=[pl.BlockSpec((None, None, rb, cs), lambda k, t, c_ref: (k, c_ref[0], t, 0)),
                  pl.BlockSpec((None, rb, cs), lambda k, t, c_ref: (k, t, 0))],
        out_specs=pl.BlockSpec((None, rb, cs), lambda k, t, c_ref: (k, t, 0)))
    return pl.pallas_call(
        body, name=name, grid_spec=grid_spec, out_shape=jax.ShapeDtypeStruct((N_CHIPS, rh, cs), BF16),
        compiler_params=pltpu.CompilerParams(dimension_semantics=("parallel", "parallel")),
    )(c_idx, g4, got)


def _chip_scatter_carry(sums):
    n = len(sums)

    def copies(ins, outs, sems):
        x, y, c, chips = _place()
        return [pltpu.make_async_remote_copy(src_ref=ins[i].at[2 * chip[0] + chip[1]], dst_ref=outs[i].at[j],
                                             send_sem=sems[0].at[3 * i + j], recv_sem=sems[1].at[3 * i + j],
                                             device_id=(*chip, c), device_id_type=MESH)
                for i in range(n) for j, chip in enumerate(chips)]

    def start(ins, outs, sems):
        for cp in copies(ins, outs, sems):
            cp.start()

    def finish(ins, outs, sems):
        for cp in copies(ins, outs, sems):
            cp.wait()

    return _Carry(sums, [jax.ShapeDtypeStruct((3,) + a.shape[1:], a.dtype) for a in sums], {},
                  [pltpu.SemaphoreType.DMA((3 * n,)), pltpu.SemaphoreType.DMA((3 * n,))], start, finish)


def _chip_sum(sums, gots, kc_idx, *, name):
    n, steps = len(sums), 2

    def body(kc_ref, *refs):
        for own_ref, got_ref, o_ref in zip(refs[:n], refs[n:2 * n], refs[2 * n:]):
            acc = own_ref[...].astype(F32)
            for j in range(3):
                acc = acc + got_ref[j].astype(F32)
            o_ref[...] = acc

    blocks = [(a.shape[1] // steps, a.shape[2]) for a in sums]
    grid_spec = pltpu.PrefetchScalarGridSpec(
        num_scalar_prefetch=1, grid=(steps,),
        in_specs=[pl.BlockSpec((None,) + blk, lambda t, kc: (kc[0], t, 0)) for blk in blocks]
        + [pl.BlockSpec((3,) + blk, lambda t, kc: (0, t, 0)) for blk in blocks],
        out_specs=tuple(pl.BlockSpec(blk, lambda t, kc: (kc[1] * steps + t, 0)) for blk in blocks))
    return pl.pallas_call(
        body, name=name, grid_spec=grid_spec,
        out_shape=tuple(jax.ShapeDtypeStruct((2 * a.shape[1], a.shape[2]), F32) for a in sums),
        compiler_params=pltpu.CompilerParams(dimension_semantics=("parallel",),
                                             vmem_limit_bytes=_vmem_limit(*[3 * _nbytes(b, F32) for b in blocks])),
    )(kc_idx, *sums, *gots)


def _half_swap_carry(shards):
    n = len(shards)

    def copy(ins, outs, sems, i, to_my_half):
        x, y, c, _ = _place()
        rh = ins[i].shape[0] // 2
        half = c if to_my_half else 1 - c
        return pltpu.make_async_remote_copy(
            src_ref=ins[i].at[pl.ds(c * rh, rh)], dst_ref=outs[i].at[pl.ds(half * rh, rh)],
            send_sem=sems[0].at[i], recv_sem=sems[1].at[i], device_id=(x, y, 1 - c), device_id_type=MESH)

    def start(ins, outs, sems):
        for i in range(n):
            copy(ins, outs, sems, i, True).start()

    def finish(ins, outs, sems):
        for i in range(n):
            copy(ins, outs, sems, i, False).wait_recv()
        for i in range(n):
            copy(ins, outs, sems, i, True).wait_send()

    return _Carry(shards, [jax.ShapeDtypeStruct(a.shape, a.dtype) for a in shards], {i: i for i in range(n)},
                  [pltpu.SemaphoreType.DMA((n,)), pltpu.SemaphoreType.DMA((n,))], start, finish)


SMALL_ROW = 1024


def _small_layout(shapes):
    starts, row = [], 0
    for r, c in shapes:
        starts.append(row)
        row += -(-c // SMALL_ROW) if r == 1 else r
    return starts, -(-row // SUBLANES) * SUBLANES


def _small_pieces(shape, start):
    r, c = shape
    if r == 1:
        return [(start + q, min(SMALL_ROW, c - q * SMALL_ROW), q * SMALL_ROW) for q in range(-(-c // SMALL_ROW))]
    return None


def _whole_spec(shape):
    return pl.BlockSpec(tuple(shape), lambda i: (0,) * len(shape))


def _small_all_reduce(parts, *, name):
    shapes = [a.shape for a in parts]
    starts, rows = _small_layout(shapes)
    n = len(parts)
    n_tables = sum(1 for r, _ in shapes if r > 1)
    assert all(r > 1 and c <= LANES for r, c in shapes[:n_tables]) and all(r == 1 for r, _ in shapes[n_tables:])
    table_rows = starts[n_tables]
    assert table_rows % SUBLANES == 0

    def regions(slot):
        return slot.at[pl.ds(0, table_rows), pl.ds(0, LANES)], slot.at[pl.ds(table_rows, rows - table_rows)]

    def body(*refs):
        ins, outs = refs[0:n], refs[n:2 * n]
        buf, send_sems, recv_sems = refs[2 * n:]
        x, y, c, _ = _place()
        me = 4 * x + 2 * y + c
        mine = buf.at[me]
        mine[...] = jnp.zeros((rows, SMALL_ROW), F32)
        for ref, shape, start in zip(ins, shapes, starts):
            pieces = _small_pieces(shape, start)
            if pieces is None:
                mine[start:start + shape[0], 0:shape[1]] = ref[...]
            else:
                for row, width, col in pieces:
                    mine[row:row + 1, 0:width] = ref[:, col:col + width]
        def peer(d):
            return 1 - x if d & 4 else x, 1 - y if d & 2 else y, 1 - c if d & 1 else c

        def copy(d, part, dst_slot):
            sem = 2 * (d - 1) + part
            return pltpu.make_async_remote_copy(
                src_ref=regions(mine)[part], dst_ref=regions(dst_slot)[part], send_sem=send_sems.at[sem],
                recv_sem=recv_sems.at[sem], device_id=peer(d), device_id_type=MESH)

        for d in range(1, 8):
            for part in range(2):
                copy(d, part, mine).start()
        for d in range(1, 8):
            px, py, pc = peer(d)
            for part in range(2):
                copy(d, part, buf.at[4 * px + 2 * py + pc]).wait_recv()
        for d in range(1, 8):
            for part in range(2):
                copy(d, part, mine).wait_send()
        tables, vectors = regions(buf.at[0])
        acc_t, acc_v = tables[...], vectors[...]
        for s in range(1, 8):
            tables, vectors = regions(buf.at[s])
            acc_t, acc_v = acc_t + tables[...], acc_v + vectors[...]
        tables, vectors = regions(mine)
        tables[...] = acc_t
        vectors[...] = acc_v
        for ref, shape, start in zip(outs, shapes, starts):
            pieces = _small_pieces(shape, start)
            if pieces is None:
                ref[...] = mine[start:start + shape[0], 0:shape[1]]
            else:
                for row, width, col in pieces:
                    ref[:, col:col + width] = mine[row:row + 1, 0:width]

    whole = [_whole_spec(s) for s in shapes]
    return pl.pallas_call(
        body, name=name, grid=(1,), in_specs=whole, out_specs=tuple(whole),
        out_shape=tuple(jax.ShapeDtypeStruct(s, F32) for s in shapes),
        scratch_shapes=[pltpu.VMEM((8, rows, SMALL_ROW), F32), pltpu.SemaphoreType.DMA((14,)),
                        pltpu.SemaphoreType.DMA((14,))],
    )(*parts)


def _adamw_small(ws, gs, ms, vs, *, name):
    n = len(ws)
    c1 = 1.0 - ADAM_B1 ** ADAM_STEP
    c2 = 1.0 - ADAM_B2 ** ADAM_STEP

    def body(*refs):
        for k in range(n):
            w_ref, g_ref, m_ref, v_ref = (refs[t * n + k] for t in range(4))
            go_ref, d_ref, mo_ref, vo_ref = (refs[(4 + t) * n + k] for t in range(4))
            gv = g_ref[...]
            go_ref[...] = gv
            mn = ADAM_B1 * m_ref[...] + (1.0 - ADAM_B1) * gv
            vn = ADAM_B2 * v_ref[...] + (1.0 - ADAM_B2) * (gv * gv)
            d_ref[...] = -ADAM_LR * ((mn / c1) / (jnp.sqrt(vn / c2) + ADAM_EPS) + ADAM_WD * w_ref[...])
            mo_ref[...] = mn
            vo_ref[...] = vn

    whole = [_whole_spec(a.shape) for a in ws]
    shapes = tuple(jax.ShapeDtypeStruct(a.shape, F32) for a in ws)
    res = pl.pallas_call(
        body, name=name, grid=(1,), in_specs=whole * 4, out_specs=tuple(whole * 4), out_shape=shapes * 4,
    )(*ws, *gs, *ms, *vs)
    return res[0:n], res[n:2 * n], res[2 * n:3 * n], res[3 * n:4 * n]


def _adamw(ws, gs, ms, vs, *, name):
    n, steps = len(ws), 8
    c1 = 1.0 - ADAM_B1 ** ADAM_STEP
    c2 = 1.0 - ADAM_B2 ** ADAM_STEP

    def body(*refs):
        for k in range(n):
            w_ref, g_ref, m_ref, v_ref = (refs[t * n + k] for t in range(4))
            go_ref, d_ref, mo_ref, vo_ref = (refs[(4 + t) * n + k] for t in range(4))
            gv = g_ref[...]
            go_ref[...] = gv
            mn = ADAM_B1 * m_ref[...] + (1.0 - ADAM_B1) * gv
            vn = ADAM_B2 * v_ref[...] + (1.0 - ADAM_B2) * (gv * gv)
            d_ref[...] = -ADAM_LR * ((mn / c1) / (jnp.sqrt(vn / c2) + ADAM_EPS) + ADAM_WD * w_ref[...])
            mo_ref[...] = mn
            vo_ref[...] = vn

    blocks = [(a.shape[0] // steps, a.shape[1]) for a in ws]
    specs = [pl.BlockSpec(blk, lambda i: (i, 0)) for blk in blocks]
    shapes = tuple(jax.ShapeDtypeStruct(a.shape, F32) for a in ws)
    res = pl.pallas_call(
        body, name=name, grid=(steps,), in_specs=specs * 4, out_specs=tuple(specs * 4), out_shape=shapes * 4,
        compiler_params=pltpu.CompilerParams(dimension_semantics=("parallel",),
                                             vmem_limit_bytes=_vmem_limit(*[8 * _nbytes(b, F32) for b in blocks])),
    )(*ws, *gs, *ms, *vs)
    return res[0:n], res[n:2 * n], res[2 * n:3 * n], res[3 * n:4 * n]


def _local_step(x, p, target, small, sched):
    T = x.shape[0]
    W = MIX_W
    rel_bias = small["rel_bias"]
    b_in_a, b_in_b = small["b_in"][:, 0:ZA_W], small["b_in"][:, ZA_W:]

    idx_np = [_t5_index(A_WINDOW - 1, 1)] + [_t5_index(window // dil, dil) for window, dil in B_PATTERNS]
    idx_all = jnp.asarray(np.stack(idx_np))
    n1, bias_all = sched.run(_prologue, x, small["ffn1_pre_g"], rel_bias, idx_all, name="prologue")
    w_gu1 = sched.weight("ffn1_w_gu")
    *gu1, a1 = sched.run(_ffn_up, n1, w_gu1, bm=512, name="ffn1_gu")
    w_d1 = sched.weight("ffn1_w_down")
    f1, h1, n2 = sched.run(_mm_resid_norm, a1, w_d1, x, small["ffn1_post_g"], 0.5, small["attn_pre_g"], bm=512,
                           name="ffn1_down")
    win_a, win_b = sched.weight("w_in")
    za = _mm_nt(n2, win_a, bm=1024, bn=ZA_W, out_dtype=BF16, name="in_proj_a", add=b_in_a)
    dils = tuple(dil for _, dil in B_PATTERNS)
    zb_by_dil = _mm_nt_classes(n2, win_b, b_in_b, dils, bm=512, name="in_proj_b")

    k_a = za[:, W:W + LANES].reshape(T, 2, 1, HEAD_DIM)
    v_a = za[:, W + LANES:W + 2 * LANES].reshape(T, 2, 1, HEAD_DIM)
    k_a = jnp.broadcast_to(k_a, (T, 2, 4, HEAD_DIM)).reshape(T, W)
    v_a = jnp.broadcast_to(v_a, (T, 2, 4, HEAD_DIM)).reshape(T, W)
    a_args = dict(r=1, q_col=0, k_col=0, v_col=0, stride=0, pattern=0)
    o_a, lse_a1 = sched.run(_band_fwd, za, k_a, v_a, bias_all, name="band_a_fwd", **a_args)
    sink_row = jnp.repeat(small["sinks"], HEAD_DIM, axis=1)
    out_a, out_a_bf, lse_a = _merge([(o_a, lse_a1)], (1,), sink_row, name="merge_a")

    no_sink = jnp.full((1, W), NEG, F32)
    b_ctx, branches = [], []
    for t, dil in enumerate(dils):
        zb_r = zb_by_dil[t]
        args = dict(r=dil, q_col=0, k_col=1, v_col=2, stride=ZB_W // W, pattern=1 + t)
        branches.append(sched.run(_band_fwd, zb_r, zb_r, zb_r, bias_all, name=f"band_b{t}_fwd", **args))
        b_ctx.append((jnp.asarray(idx_np[1 + t]), zb_r, args))
    out_b, out_b_bf, lse_b = _merge(branches, dils, no_sink, name="merge_b")

    mix = jnp.concatenate([out_a_bf, out_b_bf], axis=1)
    w_out = sched.weight("w_out")
    att, h2, n3 = _mm_resid_norm(mix, w_out, h1, small["attn_post_g"], 1.0, small["ffn2_pre_g"], bm=512,
                                 name="out_proj", bias=small["b_out"])
    w_gu2, w_d2 = sched.weight("ffn2_w_gu"), sched.weight("ffn2_w_down")
    *gu2, a2 = _ffn_up(n3, w_gu2, bm=512, name="ffn2_gu")
    f2, h3, n4 = _mm_resid_norm(a2, w_d2, h2, small["ffn2_post_g"], 0.5, small["ple_pre_g"], bm=512,
                                name="ffn2_down")
    w_gate = sched.weight("w_ple_gate")
    p_bf = p.astype(BF16)
    dh4, de, dgp, loss, d_ple_post = _ple_head(n4, w_gate, p_bf, sched.weight("w_ple_proj"), h3, small["ple_post_g"],
                                               target, bm=512, name="ple_head")

    gs = {"ple_post_g": d_ple_post}
    sched.grad("w_ple_proj", _mm_tn(p_bf, de, bm=256, bn=1024, out_dtype=BF16, name="d_w_ple"))
    sched.grad("w_ple_gate", _mm_tn(n4, dgp, bm=512, bn=1024, out_dtype=BF16, name="d_w_gate"))
    dh3, df2, gs["ple_pre_g"], gs["ffn2_post_g"], _ = sched.run(
        _mm_nt_norms_bwd, dgp, w_gate, dh4, h3, small["ple_pre_g"], f2, small["ffn2_post_g"], 0.5, bm=512, name="d_n4")

    def ffn_bwd(df, a, gu, n, w_down, w_gu, tag, *norm_args):
        dgu = sched.run(_ffn_down_bwd, df, w_down, gu[0], gu[1], bm=512, name=f"ffn{tag}_d_a")
        sched.grad(f"ffn{tag}_w_gu", _mm_tn(n, dgu, bm=512, bn=1408, out_dtype=BF16, name=f"ffn{tag}_d_w_gu",
                                            n_stack=N_CHIPS))
        sched.grad(f"ffn{tag}_w_down", sched.run(_mm_tn, a, df, bm=256, bn=1024, out_dtype=BF16,
                                                name=f"ffn{tag}_d_w_down"))
        return sched.run(_mm_nt_norms_bwd, dgu, w_gu, *norm_args, bm=512, name=f"ffn{tag}_d_n")

    dh2, datt, gs["ffn2_pre_g"], gs["attn_post_g"], gs["b_out"] = ffn_bwd(
        df2, a2, gu2, n3, w_d2, w_gu2, "2", dh3, h2, small["ffn2_pre_g"], att, small["attn_post_g"], 1.0)
    sched.grad("w_out", _mm_tn(mix, datt, bm=512, bn=1024, out_dtype=BF16, name="d_w_out"))
    dmix = sched.run(_mm_nt, datt, w_out, bm=1024, bn=1024, out_dtype=F32, name="d_mix")

    do_a, stat_a, dsink, do_b, stat_b = _attn_delta(dmix, (out_a, out_b), (lse_a, lse_b), sink_row, dils,
                                                    name="attn_delta")
    gs["sinks"] = dsink[:, ::HEAD_DIM]
    dq_a, dk_a, dv_a, dbias_a = sched.run(_band_bwd, za, k_a, v_a, do_a, stat_a, bias_all, name="band_a_bwd", **a_args)
    d_rel_a = _bias_grad(dbias_a, jnp.asarray(idx_np[0]), name="d_rel_a")
    b_grads = []
    d_rel_b = None
    for t, (idx, zb_r, args) in enumerate(b_ctx):
        dq, dk, dv, dbias = sched.run(_band_bwd, zb_r, zb_r, zb_r, do_b[t], stat_b[t], bias_all,
                                      name=f"band_b{t}_bwd", **args)
        b_grads.append((dq, dk, dv))
        d_rel = _bias_grad(dbias, idx, name=f"d_rel_b{t}")
        d_rel_b = d_rel if d_rel_b is None else d_rel_b + d_rel
    gs["rel_bias"] = jnp.concatenate([d_rel_a[:, 0:2 * N_PAIRS], d_rel_b[:, 0:2 * N_PAIRS]], axis=1)
    dza, dzb, dsum_a, dsum_b = _dz_assemble(dq_a, dk_a, dv_a, b_grads, dils, name="d_z")
    gs["b_in"] = jnp.concatenate([dsum_a, dsum_b], axis=1)
    sched.grad("w_in", (_mm_tn(dza, n2, bm=ZA_W, bn=1024, out_dtype=BF16, name="d_w_in_a"),
                        _mm_tn(dzb, n2, bm=ZB_W // 2, bn=1024, out_dtype=BF16, name="d_w_in_b")))
    dn2_a = _mm_nn(dza, win_a, bm=1024, bn=1024, out_dtype=F32, name="d_n2_a")
    dh1, df1, gs["attn_pre_g"], gs["ffn1_post_g"], _ = sched.run(
        _mm_nt_norms_bwd, dzb, win_b, dh2, h1, small["attn_pre_g"], f1, small["ffn1_post_g"], 0.5, bm=512,
        name="d_n2_b", add=dn2_a, b_is_kn=True)
    grad_x, gs["ffn1_pre_g"] = ffn_bwd(df1, a1, gu1, n1, w_d1, w_gu1, "1", dh1, x, small["ffn1_pre_g"], None, None, 0.0)
    return loss, grad_x, gs


def _to_compute(name, full):
    st = full.reshape(N_CHIPS, full.shape[0] // N_CHIPS, full.shape[1])
    if name in ("ffn1_w_gu", "ffn2_w_gu"):
        return st
    if name == "w_ple_proj":
        return jnp.transpose(st, (1, 0, 2)).reshape(st.shape[1], -1)
    if name == "w_in":
        return full[0:ZA_W], full[ZA_W:]
    return full


def _to_comm(name, g):
    if name == "w_in":
        g = jnp.concatenate(g, axis=0)
    if name == "w_ple_proj":
        g = jnp.transpose(g.reshape(g.shape[0], N_CHIPS, -1), (1, 0, 2))
    rs = g.shape[1] if g.ndim == 3 else g.shape[0] // N_CHIPS
    return g.reshape(N_CHIPS, 2, rs // 2, g.shape[-1])


class _Schedule:
    GATHER = {
        "prologue": [("ffn1_w_gu", (0, 1))],
        "ffn1_gu": [("ffn1_w_down", (0, 1)), ("w_in", (0, 1))],
        "ffn1_down": [("w_out", (0, 1))],
        "band_a_fwd": [("ffn2_w_gu", (0, 2))],
        "band_b0_fwd": [("ffn2_w_gu", (1, 2))],
        "band_b1_fwd": [("ffn2_w_down", (0, 1))],
        "band_b2_fwd": [("w_ple_gate", (0, 1)), ("w_ple_proj", (0, 1))],
    }
    SWAP = {"d_n4": ["w_ple_proj", "w_ple_gate"], "ffn2_d_w_down": ["ffn2_w_gu"], "ffn2_d_n": ["ffn2_w_down"],
            "d_mix": ["w_out"], "d_n2_b": ["w_in"], "ffn1_d_w_down": ["ffn1_w_gu"], "ffn1_d_n": ["ffn1_w_down"]}
    SCATTER = {"ffn2_d_a": ["w_ple_proj", "w_ple_gate"], "ffn2_d_n": ["ffn2_w_gu"], "band_a_bwd": ["ffn2_w_down"],
               "band_b0_bwd": ["w_out"], "ffn1_d_a": ["w_in"], "ffn1_d_n": ["ffn1_w_gu"]}

    def __init__(self, placed, c_idx, kc_idx):
        self.full = dict(placed)
        self.missing = {n: 1.0 for n in placed}
        self.c_idx, self.kc_idx = c_idx, kc_idx
        self.grads, self.swapped, self.pair_sums, self.scattered = {}, {}, {}, {}

    def _gather(self, entries):
        carries, after = [], []
        for part in sorted({pt for _, pt in entries}):
            names = [n for n, pt in entries if pt == part]
            carry = _gather_carry([self.full[n] for n in names], part)
            carries.append(carry)

            def done(carry=carry, names=names, part=part):
                for n, a in zip(names, carry.results):
                    self.full[n] = a
                    self.missing[n] -= 1.0 / part[1]
            after.append(done)
        return carries, after

    def weight(self, name):
        assert abs(self.missing[name]) < 1e-9, (name, self.missing[name])
        return _to_compute(name, self.full[name])

    def grad(self, name, g):
        self.grads[name] = _to_comm(name, g)

    def _swap(self, names):
        carry = _pair_swap_carry([self.grads[n] for n in names])

        def done():
            self.swapped.update(zip(names, carry.results))
        return carry, done

    def _scatter(self, names):
        for n in names:
            self.pair_sums[n] = _pair_sum(self.grads[n], self.swapped[n], self.c_idx, name=f"grad_pair_sum_{n}")
        carry = _chip_scatter_carry([self.pair_sums[n] for n in names])

        def done():
            self.scattered.update(zip(names, carry.results))
        return carry, done

    def run(self, fn, *args, name, **kw):
        carries, after = self._gather(self.GATHER.get(name, []))
        for names, make in ((self.SWAP.get(name), self._swap), (self.SCATTER.get(name), self._scatter)):
            if names:
                carry, done = make(names)
                carries.append(carry)
                after.append(done)
        out = fn(*args, name=name, carry=_join(carries), **kw)
        for done in after:
            done()
        return out

    def finish(self):
        left = [n for n in BIG if n not in self.scattered]
        to_swap = [n for n in left if n not in self.swapped]
        if to_swap:
            carry, done = self._swap(to_swap)
            _comm_call(carry, name="grad_pair_swap")
            done()
        if left:
            carry, done = self._scatter(left)
            _comm_call(carry, name="grad_chip_scatter")
            done()
        halves = _chip_sum([self.pair_sums[n] for n in BIG], [self.scattered[n] for n in BIG], self.kc_idx,
                           name="grad_chip_sum")
        carry = _half_swap_carry(halves)
        _comm_call(carry, name="grad_half_swap")
        return dict(zip(BIG, carry.results))


def kernel(x, p, rel_bias, ffn1_pre_g, ffn1_w_gu, ffn1_w_down, ffn1_post_g, attn_pre_g, w_in, b_in, sinks, w_out, b_out, attn_post_g, ffn2_pre_g, ffn2_w_gu, ffn2_w_down, ffn2_post_g, ple_pre_g, w_ple_gate, w_ple_proj, ple_post_g, loss_target, m_rel_bias, m_ffn1_pre_g, m_ffn1_w_gu, m_ffn1_w_down, m_ffn1_post_g, m_attn_pre_g, m_w_in, m_b_in, m_sinks, m_w_out, m_b_out, m_attn_post_g, m_ffn2_pre_g, m_ffn2_w_gu, m_ffn2_w_down, m_ffn2_post_g, m_ple_pre_g, m_w_ple_gate, m_w_ple_proj, m_ple_post_g, v_rel_bias, v_ffn1_pre_g, v_ffn1_w_gu, v_ffn1_w_down, v_ffn1_post_g, v_attn_pre_g, v_w_in, v_b_in, v_sinks, v_w_out, v_b_out, v_attn_post_g, v_ffn2_pre_g, v_ffn2_w_gu, v_ffn2_w_down, v_ffn2_post_g, v_ple_pre_g, v_w_ple_gate, v_w_ple_proj, v_ple_post_g):
    given = dict(locals())
    w = {n: given[n] for n in WEIGHTS}
    m = {n: given["m_" + n] for n in WEIGHTS}
    v = {n: given["v_" + n] for n in WEIGHTS}
    c_pos = lax.axis_index("c").astype(jnp.int32)
    k_pos = (2 * lax.axis_index("x") + lax.axis_index("y")).astype(jnp.int32)
    c_idx, k_idx, kc_idx = c_pos.reshape(1), k_pos.reshape(1), jnp.stack([k_pos, c_pos])

    def shard(a, n):
        return jnp.transpose(a[0]) if n == "w_in" else a[0]

    def unshard(a, n):
        return (jnp.transpose(a) if n == "w_in" else a)[None]

    placed = _cast_place([shard(w[n], n) for n in BIG], k_idx, name="place_shards")
    sched = _Schedule(dict(zip(BIG, placed)), c_idx, kc_idx)
    small = {n: (w[n] if n == "rel_bias" else w[n].reshape(1, -1)) for n in SMALL}

    loss_part, grad_x, gs = _local_step(x[0], p[0, 0], loss_target[0], small, sched)

    grads_big = sched.finish()

    *small_grads, loss_row = _small_all_reduce([gs[n] for n in SMALL] + [loss_part], name="small_all_reduce")
    loss = loss_row[0, 0]

    grads, delta, new_m, new_v = {}, {}, {}, {}
    res = _adamw([shard(w[n], n) for n in BIG], [grads_big[n] for n in BIG], [shard(m[n], n) for n in BIG],
                 [shard(v[n], n) for n in BIG], name="adamw_big")
    for n, gg, dd, mm, vv in zip(BIG, *res):
        grads[n], delta[n], new_m[n], new_v[n] = (unshard(a, n) for a in (gg, dd, mm, vv))
    res = _adamw_small([w[n] for n in SMALL], small_grads, [m[n] for n in SMALL], [v[n] for n in SMALL],
                       name="adamw_small")
    for name, gg, dd, mm, vv in zip(SMALL, *res):
        grads[name], delta[name], new_m[name], new_v[name] = gg, dd, mm, vv

    return (loss, grad_x[None], *[grads[n] for n in WEIGHTS], *[delta[n] for n in WEIGHTS],
            *[new_m[n] for n in WEIGHTS], *[new_v[n] for n in WEIGHTS])
```

```python
import math

import jax
import jax.numpy as jnp
import numpy as np
from jax import lax
from jax.experimental import pallas as pl
from jax.experimental.pallas import tpu as pltpu

F32 = jnp.float32
BF16 = jnp.bfloat16
MESH = pl.DeviceIdType.MESH

EPS = 1e-6
NEG = -1e30
LANES = 128
SUBLANES = 8
HEAD_DIM = 64
QBLK = 128
N_PAIRS = 4
MIX_W = N_PAIRS * LANES
A_WINDOW = 128
B_PATTERNS = ((128, 1), (512, 4), (2048, 16))
NUM_BUCKETS = 32
MAX_DISTANCE = 2048
ZA_W = 768
ZB_W = 1536
N_CHIPS = 4
VMEM_BYTES_V7X = 64 * 2**20

ADAM_LR, ADAM_B1, ADAM_B2, ADAM_EPS, ADAM_WD, ADAM_STEP = 0.001, 0.9, 0.999, 1e-08, 0.01, 10

BIG = ("ffn1_w_gu", "ffn1_w_down", "w_in", "w_out", "ffn2_w_gu", "ffn2_w_down", "w_ple_gate", "w_ple_proj")
SMALL = ("rel_bias", "ffn1_pre_g", "ffn1_post_g", "attn_pre_g", "b_in", "sinks", "b_out", "attn_post_g",
         "ffn2_pre_g", "ffn2_post_g", "ple_pre_g", "ple_post_g")
WEIGHTS = ("rel_bias", "ffn1_pre_g", "ffn1_w_gu", "ffn1_w_down", "ffn1_post_g", "attn_pre_g", "w_in", "b_in",
           "sinks", "w_out", "b_out", "attn_post_g", "ffn2_pre_g", "ffn2_w_gu", "ffn2_w_down", "ffn2_post_g",
           "ple_pre_g", "w_ple_gate", "w_ple_proj", "ple_post_g")


def _vmem_limit(*block_bytes):
    need = 2 * sum(block_bytes) + max(block_bytes) * 2 + (4 << 20)
    return int(min(max(need, 32 << 20), VMEM_BYTES_V7X - (6 << 20)))


def _nbytes(shape, dtype):
    return int(np.prod(shape)) * jnp.dtype(dtype).itemsize


MXU_WIDTH_V7X = 256


def _col_chunks(n):
    return [slice(c, min(c + MXU_WIDTH_V7X, n)) for c in range(0, n, MXU_WIDTH_V7X)]


def _row_halves(rows):
    return [slice(0, rows // 2), slice(rows // 2, rows)]


def _row_block(rows, cap, mult):
    for cand in range(min(rows, cap), 0, -1):
        if rows % cand == 0 and cand % mult == 0:
            return cand
    raise ValueError((rows, cap, mult))


class _Carry:
    def __init__(self, operands, out_shapes, aliases, sems, start, finish):
        self.operands, self.out_shapes, self.aliases, self.sems = list(operands), list(out_shapes), dict(aliases), list(sems)
        self.start, self.finish = start, finish
        self.results = None


def _join(carries):
    carries = [c for c in carries if c is not None]
    if not carries:
        return None
    if len(carries) == 1:
        return carries[0]
    offs, pos = [], [0, 0, 0]
    for c in carries:
        offs.append(tuple(pos))
        pos = [pos[0] + len(c.operands), pos[1] + len(c.out_shapes), pos[2] + len(c.sems)]
    aliases = {}
    for c, (oi, oo, _) in zip(carries, offs):
        aliases.update({oi + i: oo + o for i, o in c.aliases.items()})

    def run(which):
        def fn(ins, outs, sems):
            for c, (oi, oo, os_) in zip(carries, offs):
                getattr(c, which)(ins[oi:oi + len(c.operands)], outs[oo:oo + len(c.out_shapes)],
                                  sems[os_:os_ + len(c.sems)])
        return fn

    joined = _Carry([a for c in carries for a in c.operands], [s for c in carries for s in c.out_shapes], aliases,
                    [s for c in carries for s in c.sems], run("start"), run("finish"))
    joined.parts = (carries, offs)
    return joined


def _set_results(carry, outs):
    carry.results = list(outs)
    if hasattr(carry, "parts"):
        for c, (_, oo, _) in zip(*carry.parts):
            _set_results(c, outs[oo:oo + len(c.out_shapes)])


ANY = pl.BlockSpec(memory_space=pl.ANY)


def _call(body, *, name, grid, in_specs, out_specs, out_shape, args, scratch=(), vmem_limit=None, carry=None,
          semantics=None):
    n_ci, n_co, n_cs = len(args), len(out_shape), len(scratch)
    semantics = semantics or ("parallel",) * len(grid)
    if carry is None:
        res = pl.pallas_call(
            body, name=name, grid=grid, in_specs=list(in_specs), out_specs=tuple(out_specs), out_shape=tuple(out_shape),
            scratch_shapes=list(scratch),
            compiler_params=pltpu.CompilerParams(dimension_semantics=semantics, vmem_limit_bytes=vmem_limit),
        )(*args)
        return list(res)
    n_pi, n_po = len(carry.operands), len(carry.out_shapes)

    def full_body(*refs):
        ci, pi = refs[:n_ci], refs[n_ci:n_ci + n_pi]
        o0 = n_ci + n_pi
        co, po = refs[o0:o0 + n_co], refs[o0 + n_co:o0 + n_co + n_po]
        s0 = o0 + n_co + n_po
        cs, ps = refs[s0:s0 + n_cs], refs[s0 + n_cs:]
        ids = [pl.program_id(ax) for ax in range(len(grid))]
        first, last = ids[0] == 0, ids[0] == grid[0] - 1
        for ax in range(1, len(grid)):
            first, last = first & (ids[ax] == 0), last & (ids[ax] == grid[ax] - 1)

        @pl.when(first)
        def _():
            carry.start(pi, po, ps)

        body(*ci, *co, *cs)

        @pl.when(last)
        def _():
            carry.finish(pi, po, ps)

    res = pl.pallas_call(
        full_body, name=name, grid=grid, in_specs=list(in_specs) + [ANY] * n_pi,
        out_specs=tuple(out_specs) + tuple([ANY] * n_po), out_shape=tuple(out_shape) + tuple(carry.out_shapes),
        scratch_shapes=list(scratch) + carry.sems,
        input_output_aliases={n_ci + i: n_co + o for i, o in carry.aliases.items()},
        compiler_params=pltpu.CompilerParams(dimension_semantics=("arbitrary",) * len(grid),
                                             vmem_limit_bytes=vmem_limit),
    )(*args, *carry.operands)
    _set_results(carry, res[n_co:])
    return list(res[:n_co])


def _comm_call(carry, *, name):
    n_pi, n_po = len(carry.operands), len(carry.out_shapes)

    def body(*refs):
        pi, po, ps = refs[:n_pi], refs[n_pi:n_pi + n_po], refs[n_pi + n_po:]
        carry.start(pi, po, ps)
        carry.finish(pi, po, ps)

    res = pl.pallas_call(
        body, name=name, in_specs=[ANY] * n_pi, out_specs=tuple([ANY] * n_po), out_shape=tuple(carry.out_shapes),
        scratch_shapes=carry.sems, input_output_aliases=dict(carry.aliases),
    )(*carry.operands)
    _set_results(carry, res)


def _mm_nn(a, b, *, bm, bn, out_dtype, name, bias=None, carry=None):
    M, K = a.shape
    stacked = b.ndim == 3
    N = b.shape[0] * b.shape[2] if stacked else b.shape[1]
    assert M % bm == 0 and N % bn == 0 and (not stacked or bn == b.shape[2])

    def body(*refs):
        a_ref, b_ref = refs[0], refs[1]
        o_ref = refs[-1]
        acc = jnp.dot(a_ref[...], b_ref[...], preferred_element_type=F32)
        if bias is not None:
            acc = acc + refs[2][...]
        o_ref[...] = acc.astype(o_ref.dtype)

    if stacked:
        b_spec = pl.BlockSpec((None, K, bn), lambda i, j: (j, 0, 0))
    else:
        b_spec = pl.BlockSpec((K, bn), lambda i, j: (0, j))
    in_specs = [pl.BlockSpec((bm, K), lambda i, j: (i, 0)), b_spec]
    args = [a, b]
    if bias is not None:
        in_specs.append(pl.BlockSpec((1, bn), lambda i, j: (0, j)))
        args.append(bias)
    limit = _vmem_limit(_nbytes((bm, K), a.dtype), _nbytes((K, bn), b.dtype), _nbytes((bm, bn), F32))
    return _call(body, name=name, grid=(M // bm, N // bn), in_specs=in_specs,
                 out_specs=[pl.BlockSpec((bm, bn), lambda i, j: (i, j))],
                 out_shape=[jax.ShapeDtypeStruct((M, N), out_dtype)], args=args, vmem_limit=limit, carry=carry)[0]


def _mm_nt(a, b, *, bm, bn, out_dtype, name, add=None, carry=None):
    M, K = a.shape
    stacked = b.ndim == 3
    N = b.shape[1] if stacked else b.shape[0]
    n_sh = b.shape[0] if stacked else 1
    ks = b.shape[2] if stacked else K
    assert M % bm == 0 and N % bn == 0 and n_sh * ks == K
    dn = (((1,), (1,)), ((), ()))

    def body(*refs):
        a_ref, b_ref = refs[0], refs[1]
        o_ref = refs[-1]
        if stacked:
            acc = lax.dot_general(a_ref[:, 0:ks], b_ref[0], dn, preferred_element_type=F32)
            for s in range(1, n_sh):
                acc = acc + lax.dot_general(a_ref[:, s * ks:(s + 1) * ks], b_ref[s], dn, preferred_element_type=F32)
        else:
            acc = lax.dot_general(a_ref[...], b_ref[...], dn, preferred_element_type=F32)
        if add is not None:
            acc = acc + refs[2][...]
        o_ref[...] = acc.astype(o_ref.dtype)

    if stacked:
        b_spec = pl.BlockSpec((n_sh, bn, ks), lambda i, j: (0, j, 0))
    else:
        b_spec = pl.BlockSpec((bn, K), lambda i, j: (j, 0))
    in_specs = [pl.BlockSpec((bm, K), lambda i, j: (i, 0)), b_spec]
    args = [a, b]
    if add is not None:
        rows = bm if add.shape[0] == M else 1
        in_specs.append(pl.BlockSpec((rows, bn), lambda i, j: (i if rows == bm else 0, j)))
        args.append(add)
    limit = _vmem_limit(_nbytes((bm, K), a.dtype), _nbytes((bn, K), b.dtype), _nbytes((bm, bn), F32))
    return _call(body, name=name, grid=(M // bm, N // bn), in_specs=in_specs,
                 out_specs=[pl.BlockSpec((bm, bn), lambda i, j: (i, j))],
                 out_shape=[jax.ShapeDtypeStruct((M, N), out_dtype)], args=args, vmem_limit=limit, carry=carry)[0]


def _mm_tn(a, b, *, bm, bn, out_dtype, name, n_stack=None, carry=None):
    T, M = a.shape
    N = b.shape[1]
    assert M % bm == 0 and N % bn == 0 and (n_stack is None or bn * n_stack == N)
    dn = (((0,), (0,)), ((), ()))

    def body(a_ref, b_ref, o_ref):
        acc = lax.dot_general(a_ref[...], b_ref[...], dn, preferred_element_type=F32)
        o_ref[...] = acc.astype(o_ref.dtype)

    if n_stack is None:
        out_spec = pl.BlockSpec((bm, bn), lambda i, j: (i, j))
        out_shape = jax.ShapeDtypeStruct((M, N), out_dtype)
    else:
        out_spec = pl.BlockSpec((None, bm, bn), lambda i, j: (j, i, 0))
        out_shape = jax.ShapeDtypeStruct((n_stack, M, bn), out_dtype)
    limit = _vmem_limit(_nbytes((T, bm), a.dtype), _nbytes((T, bn), b.dtype), _nbytes((bm, bn), F32))
    return _call(body, name=name, grid=(M // bm, N // bn),
                 in_specs=[pl.BlockSpec((T, bm), lambda i, j: (0, i)), pl.BlockSpec((T, bn), lambda i, j: (0, j))],
                 out_specs=[out_spec], out_shape=[out_shape], args=[a, b], vmem_limit=limit, carry=carry)[0]


ROW_BLK = 256


def _rstd(x):
    return lax.rsqrt(jnp.mean(x * x, axis=-1, keepdims=True) + EPS)


def _norm_bwd(xhat, rstd, dxhat):
    return rstd * (dxhat - xhat * jnp.mean(dxhat * xhat, axis=-1, keepdims=True))


def _row_spec(cols):
    return pl.BlockSpec((ROW_BLK, cols), lambda i: (i, 0))


def _vec_spec(cols):
    return pl.BlockSpec((1, cols), lambda i: (0, 0))


def _prologue(x, g, rel_bias, idx_all, *, name, carry=None):
    T, D = x.shape
    n_pat = idx_all.shape[0]
    heads = 2 * N_PAIRS
    steps = n_pat * heads
    rows = T // steps

    def body(rb_ref, x_ref, g_ref, idx_ref, n_ref, bias_ref):
        s = pl.program_id(0)
        head = jnp.where(s < heads, s, heads + s % heads)
        xv = x_ref[...]
        n_ref[...] = (xv * _rstd(xv) * g_ref[...]).astype(n_ref.dtype)
        idxv = idx_ref[...]
        acc = jnp.where(idxv < 0, NEG, 0.0).astype(F32)
        for b in range(NUM_BUCKETS):
            acc = acc + jnp.where(idxv == b, rb_ref[b, head], 0.0)
        bias_ref[0] = acc
        kap = lax.broadcasted_iota(jnp.int32, (1, 2 * QBLK), 1)
        bias_ref[1] = jnp.where(kap < QBLK, NEG, acc)

    return _call(
        body, name=name, grid=(steps,),
        in_specs=[pl.BlockSpec(memory_space=pltpu.SMEM), pl.BlockSpec((rows, D), lambda s: (s, 0)),
                  pl.BlockSpec((1, D), lambda s: (0, 0)),
                  pl.BlockSpec((None, QBLK, 2 * QBLK), lambda s: (s // heads, 0, 0))],
        out_specs=[pl.BlockSpec((rows, D), lambda s: (s, 0)),
                   pl.BlockSpec((None, 2, None, QBLK, 2 * QBLK), lambda s: (s // heads, 0, s % heads, 0, 0))],
        out_shape=[jax.ShapeDtypeStruct((T, D), BF16),
                   jax.ShapeDtypeStruct((n_pat, 2, heads, QBLK, 2 * QBLK), F32)],
        args=[rel_bias, x, g, idx_all], carry=carry)


def _mm_resid_norm(a, b, h, g_post, coef, g_next, *, bm, name, bias=None, carry=None):
    M, K = a.shape
    N = b.shape[1]

    def body(*refs):
        a_ref, b_ref, h_ref, gp_ref, gn_ref = refs[0:5]
        f_ref, hn_ref, n_ref = refs[-3:]
        for rows in _row_halves(bm):
            f = jnp.dot(a_ref[rows, :], b_ref[...], preferred_element_type=F32)
            if bias is not None:
                f = f + refs[5][...]
            f_ref[rows, :] = f
            hn = h_ref[rows, :] + coef * (f * _rstd(f) * gp_ref[...])
            hn_ref[rows, :] = hn
            n_ref[rows, :] = (hn * _rstd(hn) * gn_ref[...]).astype(n_ref.dtype)

    row = lambda w: pl.BlockSpec((bm, w), lambda i: (i, 0))
    vec = pl.BlockSpec((1, N), lambda i: (0, 0))
    in_specs = [row(K), pl.BlockSpec((K, N), lambda i: (0, 0), pipeline_mode=pl.Buffered(1)), row(N), vec, vec]
    args = [a, b, h, g_post, g_next]
    if bias is not None:
        in_specs.append(vec)
        args.append(bias)
    limit = _vmem_limit(_nbytes((bm, K), a.dtype), _nbytes((K, N), b.dtype) // 2, 4 * _nbytes((bm, N), F32))
    return _call(body, name=name, grid=(M // bm,), in_specs=in_specs, out_specs=[row(N), row(N), row(N)],
                 out_shape=[jax.ShapeDtypeStruct((M, N), F32), jax.ShapeDtypeStruct((M, N), F32),
                            jax.ShapeDtypeStruct((M, N), BF16)], args=args, vmem_limit=limit, carry=carry)


def _mm_nt_norms_bwd(a, b, dh_in, h, g_pre, f, g_post, coef, *, bm, name, add=None, b_is_kn=False, carry=None):
    M, K = a.shape
    stacked = b.ndim == 3
    N = b.shape[1] if (stacked or b_is_kn) else b.shape[0]
    n_sh = b.shape[0] if stacked else 1
    ks = b.shape[2] if stacked else K
    assert n_sh * ks == K
    dn_dims = (((1,), (0,)), ((), ())) if b_is_kn else (((1,), (1,)), ((), ()))
    with_f = f is not None
    n_in = 5 + (2 if with_f else 0) + (1 if add is not None else 0)

    def body(*refs):
        a_ref, b_ref, dhi_ref, h_ref, gpre_ref = refs[0:5]
        outs = refs[n_in:]

        @pl.when(pl.program_id(0) == 0)
        def _():
            for acc in (outs[2:] if with_f else outs[1:]):
                acc[...] = jnp.zeros_like(acc)

        for rows in _row_halves(bm):
            if stacked:
                dn = lax.dot_general(a_ref[rows, 0:ks], b_ref[0], dn_dims, preferred_element_type=F32)
                for s in range(1, n_sh):
                    dn = dn + lax.dot_general(a_ref[rows, s * ks:(s + 1) * ks], b_ref[s], dn_dims,
                                              preferred_element_type=F32)
            else:
                dn = lax.dot_general(a_ref[rows, :], b_ref[...], dn_dims, preferred_element_type=F32)
            if add is not None:
                dn = dn + refs[n_in - 1][rows, :]
            hv = h_ref[rows, :]
            r = _rstd(hv)
            hh = hv * r
            dh = dhi_ref[rows, :] + _norm_bwd(hh, r, dn * gpre_ref[...])
            outs[0][rows, :] = dh
            if not with_f:
                outs[1][...] += jnp.sum(dn * hh, axis=0, keepdims=True)
                continue
            f_ref, gpost_ref = refs[5], refs[6]
            df_ref, dgpre_ref, dgpost_ref, dsum_ref = outs[1:]
            dgpre_ref[...] += jnp.sum(dn * hh, axis=0, keepdims=True)
            fv = f_ref[rows, :]
            rf = _rstd(fv)
            fh = fv * rf
            dy = coef * dh
            dgpost_ref[...] += jnp.sum(dy * fh, axis=0, keepdims=True)
            df = _norm_bwd(fh, rf, dy * gpost_ref[...])
            dsum_ref[...] += jnp.sum(df, axis=0, keepdims=True)
            df_ref[rows, :] = df.astype(df_ref.dtype)

    row = lambda w: pl.BlockSpec((bm, w), lambda i: (i, 0))
    vec = pl.BlockSpec((1, N), lambda i: (0, 0))
    once = pl.Buffered(1)
    if stacked:
        b_spec = pl.BlockSpec((n_sh, N, ks), lambda i: (0, 0, 0), pipeline_mode=once)
    else:
        b_spec = pl.BlockSpec(b.shape, lambda i: (0, 0), pipeline_mode=once)
    in_specs = [row(K), b_spec, row(N), row(N), vec]
    args = [a, b, dh_in, h, g_pre]
    if with_f:
        in_specs += [row(N), vec]
        args += [f, g_post]
    if add is not None:
        in_specs.append(row(N))
        args.append(add)
    vec_shape = jax.ShapeDtypeStruct((1, N), F32)
    out_specs = [row(N)] + ([row(N), vec, vec, vec] if with_f else [vec])
    out_shape = [jax.ShapeDtypeStruct((M, N), F32)]
    out_shape += [jax.ShapeDtypeStruct((M, N), BF16), vec_shape, vec_shape, vec_shape] if with_f else [vec_shape]
    limit = _vmem_limit(_nbytes((bm, K), a.dtype), _nbytes((N, K), b.dtype) // 2, 6 * _nbytes((bm, N), F32))
    return _call(body, name=name, grid=(M // bm,), in_specs=in_specs, out_specs=out_specs, out_shape=out_shape,
                 args=args, vmem_limit=limit, semantics=("arbitrary",), carry=carry)


def _ffn_up(n, w_gu, *, bm, name, carry=None):
    M, K = n.shape
    n_sh, _, ns = w_gu.shape
    half = n_sh // 2

    def body(n_ref, wg_ref, wu_ref, g_ref, u_ref, a_ref):
        nv = n_ref[...]
        for cols in _col_chunks(ns):
            g = jnp.dot(nv, wg_ref[:, cols], preferred_element_type=F32)
            u = jnp.dot(nv, wu_ref[:, cols], preferred_element_type=F32)
            g_ref[:, cols] = g.astype(g_ref.dtype)
            u_ref[:, cols] = u.astype(u_ref.dtype)
            a_ref[:, cols] = (g * jax.nn.sigmoid(g) * u).astype(a_ref.dtype)

    out_spec = pl.BlockSpec((bm, ns), lambda i, j: (i, j))
    out = jax.ShapeDtypeStruct((M, half * ns), BF16)
    limit = _vmem_limit(_nbytes((bm, K), n.dtype), 2 * _nbytes((K, ns), w_gu.dtype), 3 * _nbytes((bm, ns), F32))
    return _call(body, name=name, grid=(M // bm, half),
                 in_specs=[pl.BlockSpec((bm, K), lambda i, j: (i, 0)),
                           pl.BlockSpec((None, K, ns), lambda i, j: (j, 0, 0)),
                           pl.BlockSpec((None, K, ns), lambda i, j: (j + half, 0, 0))],
                 out_specs=[out_spec, out_spec, out_spec], out_shape=[out, out, out], args=[n, w_gu, w_gu],
                 vmem_limit=limit, carry=carry)


def _ffn_down_bwd(df, w_down, g, u, *, bm, name, carry=None):
    M, D = df.shape
    F = w_down.shape[0]
    dn = (((1,), (1,)), ((), ()))

    def body(df_ref, w_ref, g_ref, u_ref, o_ref):
        dfv = df_ref[...]
        for cols in _col_chunks(F):
            da = lax.dot_general(dfv, w_ref[cols, :], dn, preferred_element_type=F32)
            gv = g_ref[:, cols].astype(F32)
            s = jax.nn.sigmoid(gv)
            o_ref[:, cols] = (da * u_ref[:, cols].astype(F32) * (s * (1.0 + gv * (1.0 - s)))).astype(o_ref.dtype)
            o_ref[:, F + cols.start:F + cols.stop] = (da * (gv * s)).astype(o_ref.dtype)

    row = lambda w: pl.BlockSpec((bm, w), lambda i: (i, 0))
    limit = _vmem_limit(_nbytes((F, D), w_down.dtype) // 2, 2 * _nbytes((bm, F), BF16), _nbytes((bm, 2 * F), BF16))
    return _call(body, name=name, grid=(M // bm,),
                 in_specs=[row(D), pl.BlockSpec((F, D), lambda i: (0, 0), pipeline_mode=pl.Buffered(1)), row(F), row(F)],
                 out_specs=[row(2 * F)], out_shape=[jax.ShapeDtypeStruct((M, 2 * F), BF16)],
                 args=[df, w_down, g, u], vmem_limit=limit, carry=carry)[0]


def _ple_head(n4, w_gate, p, w_ple, h3, g_post, target, *, bm, name):
    T, D = h3.shape
    kp = p.shape[1]

    def body(n_ref, wg_ref, p_ref, wp_ref, h_ref, g_ref, t_ref, dh_ref, de_ref, dgp_ref, loss_ref, dg_ref):
        @pl.when(pl.program_id(0) == 0)
        def _():
            loss_ref[...] = jnp.zeros_like(loss_ref)
            dg_ref[...] = jnp.zeros_like(dg_ref)

        gpost = g_ref[...]
        for rows in _row_halves(bm):
            gate = jax.nn.sigmoid(jnp.dot(n_ref[rows, :], wg_ref[...], preferred_element_type=F32))
            ev = jnp.dot(p_ref[rows, :], wp_ref[...], preferred_element_type=F32)
            ge = gate * ev
            r = _rstd(ge)
            gh = ge * r
            diff = h_ref[rows, :] + gh * gpost - t_ref[rows, :]
            loss_ref[...] += jnp.sum(diff * diff) * (0.5 / D)
            dh = diff * (1.0 / D)
            dh_ref[rows, :] = dh
            dg_ref[...] += jnp.sum(dh * gh, axis=0, keepdims=True)
            dge = _norm_bwd(gh, r, dh * gpost)
            de_ref[rows, :] = (dge * gate).astype(de_ref.dtype)
            dgp_ref[rows, :] = (dge * ev * gate * (1.0 - gate)).astype(dgp_ref.dtype)

    row = lambda w: pl.BlockSpec((bm, w), lambda i: (i, 0))
    whole = lambda a: pl.BlockSpec(a.shape, lambda i: (0, 0), pipeline_mode=pl.Buffered(1))
    limit = _vmem_limit(_nbytes((bm, D), BF16), _nbytes(w_gate.shape, w_gate.dtype) // 2, 6 * _nbytes((bm, D), F32))
    return pl.pallas_call(
        body, name=name, grid=(T // bm,),
        in_specs=[row(D), whole(w_gate), row(kp), whole(w_ple), row(D), _vec_spec(D), row(D)],
        out_specs=(row(D), row(D), row(D), _vec_spec(LANES), _vec_spec(D)),
        out_shape=(jax.ShapeDtypeStruct((T, D), F32), jax.ShapeDtypeStruct((T, D), BF16),
                   jax.ShapeDtypeStruct((T, D), BF16), jax.ShapeDtypeStruct((1, LANES), F32),
                   jax.ShapeDtypeStruct((1, D), F32)),
        compiler_params=pltpu.CompilerParams(dimension_semantics=("arbitrary",), vmem_limit_bytes=limit),
    )(n4, w_gate, p, w_ple, h3, g_post, target)


def _t5_index(max_dist, stride):
    rho = np.arange(QBLK)[:, None]
    kap = np.arange(2 * QBLK)[None, :]
    d = rho + QBLK - kap
    valid = (d >= 0) & (d <= max_dist)
    n = np.maximum(d, 0) * stride
    max_exact = NUM_BUCKETS // 2
    nf = np.maximum(n, 1).astype(np.float32)
    large = max_exact + (np.log(nf / np.float32(max_exact)) / np.float32(math.log(MAX_DISTANCE / max_exact))
                         * np.float32(NUM_BUCKETS - max_exact)).astype(np.int32)
    large = np.minimum(large, NUM_BUCKETS - 1)
    bucket = np.where(n < max_exact, n, large)
    return np.where(valid, bucket, -1).astype(np.int32)


def _bias_grad(d_bias, idx, *, name):
    def body(db_ref, idx_ref, o_ref):
        h = pl.program_id(0)

        @pl.when(h == 0)
        def _():
            o_ref[...] = jnp.zeros_like(o_ref)

        idxv = idx_ref[...]
        dv = db_ref[0]
        rows = lax.broadcasted_iota(jnp.int32, (NUM_BUCKETS, LANES), 0)
        lanes = lax.broadcasted_iota(jnp.int32, (NUM_BUCKETS, LANES), 1)
        acc = o_ref[...]
        for b in range(NUM_BUCKETS):
            s = jnp.sum(jnp.where(idxv == b, dv, 0.0))
            acc = acc + jnp.where((rows == b) & (lanes == h), s, 0.0)
        o_ref[...] = acc

    return pl.pallas_call(
        body, name=name, grid=(2 * N_PAIRS,),
        in_specs=[pl.BlockSpec((1, QBLK, 2 * QBLK), lambda h: (h, 0, 0)),
                  pl.BlockSpec((QBLK, 2 * QBLK), lambda h: (0, 0))],
        out_specs=pl.BlockSpec((NUM_BUCKETS, LANES), lambda h: (0, 0)),
        out_shape=jax.ShapeDtypeStruct((NUM_BUCKETS, LANES), F32),
        compiler_params=pltpu.CompilerParams(dimension_semantics=("arbitrary",)),
    )(d_bias, idx)


def _lane():
    return lax.broadcasted_iota(jnp.int32, (1, LANES), 1)


def _head_sel(h):
    return (_lane() < HEAD_DIM) if h == 0 else (_lane() >= HEAD_DIM)


def _stat_lo():
    return (_lane() % HEAD_DIM) < (HEAD_DIM // 2)


def _stack_heads(t, scale=None):
    if scale is not None:
        t = t * scale
    zero = jnp.zeros_like(t)
    return jnp.concatenate([jnp.where(_head_sel(0), t, zero), jnp.where(_head_sel(1), t, zero)], axis=0)


def _class_spec(L, r, cols, stride, pps):
    chunks = N_PAIRS // pps
    mode = pl.Buffered(1) if r * chunks == 1 else None
    return pl.BlockSpec((L, MIX_W // chunks), lambda j, c: (0, (cols + j * stride) * chunks + c), pipeline_mode=mode)


def _rows(b, n_blocks=1):
    return pl.ds(pl.multiple_of(b * QBLK, QBLK), n_blocks * QBLK)


def _key_window(ref, b, cols, first, kv_head=None):
    if kv_head is not None:
        cols = slice(0, LANES)
    if first:
        blk = ref[0:QBLK, cols]
        tile = jnp.concatenate([blk, blk], axis=0)
    else:
        tile = ref[_rows(b - 1, 2), cols]
    if kv_head is None:
        return tile
    wide = tile.astype(F32)
    keep = jnp.logical_xor(_lane() < HEAD_DIM, kv_head == 1)
    return jnp.where(keep, wide, pltpu.roll(wide, HEAD_DIM, 1)).astype(tile.dtype)


def _kv_spec(L, r, col, stride, pps, kv_shared):
    if not kv_shared:
        return _class_spec(L, r, col, stride, pps)
    mode = pl.Buffered(1) if pps == N_PAIRS else None
    return pl.BlockSpec((L, LANES), lambda j, c: (0, col), pipeline_mode=mode)


def _band_fwd(qa, ka, va, bias, *, r, q_col, k_col, v_col, stride, pattern, name, kv_shared=False, carry=None):
    L = qa.shape[0]
    nb = L // QBLK
    dn = (((1,), (1,)), ((), ()))
    scale = HEAD_DIM ** -0.5

    pps = N_PAIRS

    def body(q_ref, k_ref, v_ref, bias_ref, o_ref, lse_ref):
        sel0 = _head_sel(0)
        pair0 = pl.program_id(1) * pps

        def block(b, first):
            rows = _rows(b)
            for i in range(pps):
                cols = slice(i * LANES, (i + 1) * LANES)
                q2 = _stack_heads(q_ref[rows, cols], scale)
                kv_head = (pair0 + i) // 2 if kv_shared else None
                k = _key_window(k_ref, b, cols, first, kv_head)
                v = _key_window(v_ref, b, cols, first, kv_head)
                bias2 = bias_ref[1 if first else 0, pl.ds(2 * (pair0 + i), 2)].reshape(2 * QBLK, 2 * QBLK)
                s = lax.dot_general(q2, k, dn, preferred_element_type=F32) + bias2
                m = jnp.max(s, axis=1, keepdims=True)
                p = jnp.exp(s - m)
                l = jnp.sum(p, axis=1, keepdims=True)
                o = jnp.dot(p.astype(v.dtype), v, preferred_element_type=F32) * (1.0 / l)
                lse = m + jnp.log(l)
                o_ref[rows, cols] = jnp.where(sel0, o[0:QBLK], o[QBLK:]).astype(o_ref.dtype)
                lse_ref[rows, cols] = jnp.where(sel0, lse[0:QBLK], lse[QBLK:])

        block(0, True)
        if nb > 1:
            pl.loop(1, nb)(lambda b: block(b, False))

    bias_spec = pl.BlockSpec((None, 2, 2 * N_PAIRS, QBLK, 2 * QBLK), lambda j, c: (pattern, 0, 0, 0, 0))
    out_spec = _class_spec(L, r, 0, 1, pps)
    blk_bytes = _nbytes((L, pps * LANES), F32)
    return _call(
        body, name=name, grid=(r, N_PAIRS // pps),
        in_specs=[_class_spec(L, r, q_col, stride, pps), _kv_spec(L, r, k_col, stride, pps, kv_shared),
                  _kv_spec(L, r, v_col, stride, pps, kv_shared), bias_spec],
        out_specs=[out_spec, out_spec],
        out_shape=[jax.ShapeDtypeStruct((L, r * MIX_W), BF16), jax.ShapeDtypeStruct((L, r * MIX_W), F32)],
        args=[qa, ka, va, bias], vmem_limit=_vmem_limit(*([blk_bytes] * 4)), carry=carry)


def _class_rows_spec(rows, r, width):
    return pl.BlockSpec((rows // r, r * width), lambda i, *_: (i, 0))


def _token_scratch(rows, width):
    return pltpu.VMEM((width // LANES, rows, LANES), F32)


def _fill_tokens(scratch, value):
    for c in range(scratch.shape[0]):
        scratch[c] = value[:, c * LANES:(c + 1) * LANES]


def _read_tokens(scratch):
    return jnp.concatenate([scratch[c] for c in range(scratch.shape[0])], axis=1)


def _to_tokens(blk_ref, r, scratch):
    if r == 1:
        return blk_ref[...].astype(F32)
    nt, rows, _ = scratch.shape
    for j in range(r):
        for c in range(nt):
            lanes = slice((j * nt + c) * LANES, (j * nt + c + 1) * LANES)
            scratch.at[c][pl.ds(j, rows // r, stride=r), :] = blk_ref[:, lanes].astype(F32)
    return _read_tokens(scratch)


def _from_tokens(dst_ref, r, scratch):
    nt, rows, _ = scratch.shape
    if r == 1:
        dst_ref[...] = _read_tokens(scratch).astype(dst_ref.dtype)
        return
    for j in range(r):
        for c in range(nt):
            lanes = slice((j * nt + c) * LANES, (j * nt + c + 1) * LANES)
            dst_ref[:, lanes] = scratch.at[c][pl.ds(j, rows // r, stride=r), :].astype(dst_ref.dtype)


def _mm_nt_classes(a, b, bias, dils, *, bm, name):
    M, K = a.shape
    N = b.shape[0]
    dn = (((1,), (1,)), ((), ()))

    def body(a_ref, b_ref, bias_ref, *rest):
        outs, tile = rest[:-1], rest[-1]
        _fill_tokens(tile, lax.dot_general(a_ref[...], b_ref[...], dn, preferred_element_type=F32) + bias_ref[...])
        for out, r in zip(outs, dils):
            _from_tokens(out, r, tile)

    limit = _vmem_limit(_nbytes((bm, K), a.dtype), _nbytes((K, N), b.dtype) // 2, (1 + len(dils)) * _nbytes((bm, N), F32))
    return pl.pallas_call(
        body, name=name, grid=(M // bm,),
        in_specs=[pl.BlockSpec((bm, K), lambda i: (i, 0)),
                  pl.BlockSpec((N, K), lambda i: (0, 0), pipeline_mode=pl.Buffered(1)),
                  pl.BlockSpec((1, N), lambda i: (0, 0))],
        out_specs=tuple(_class_rows_spec(bm, r, N) for r in dils),
        out_shape=tuple(jax.ShapeDtypeStruct((M // r, r * N), BF16) for r in dils),
        scratch_shapes=[_token_scratch(bm, N)],
        compiler_params=pltpu.CompilerParams(dimension_semantics=("parallel",), vmem_limit_bytes=limit),
    )(a, b, bias)


def _merge(branches, dils, sink, *, name):
    W = MIX_W
    T = branches[0][0].shape[0] * dils[0]
    nbr = len(branches)

    def body(*refs):
        sink_ref = refs[2 * nbr]
        out_ref, outb_ref, lse_ref, scratch = refs[2 * nbr + 1:]
        sk = sink_ref[...]
        lses = [_to_tokens(refs[2 * t + 1], dils[t], scratch) for t in range(nbr)]
        mx = sk
        for lse in lses:
            mx = jnp.maximum(mx, lse)
        den = jnp.exp(sk - mx)
        num = jnp.zeros_like(mx)
        for t in range(nbr):
            w = jnp.exp(lses[t] - mx)
            den = den + w
            num = num + w * _to_tokens(refs[2 * t], dils[t], scratch)
        out = num / den
        out_ref[...] = out
        outb_ref[...] = out.astype(outb_ref.dtype)
        lse_ref[...] = mx + jnp.log(den)

    args = [a for br in branches for a in br] + [sink]
    in_specs = [_class_rows_spec(ROW_BLK, r, W) for r in dils for _ in range(2)] + [_vec_spec(W)]
    return pl.pallas_call(
        body, name=name, grid=(T // ROW_BLK,), in_specs=in_specs,
        out_specs=(_row_spec(W), _row_spec(W), _row_spec(W)),
        out_shape=(jax.ShapeDtypeStruct((T, W), F32), jax.ShapeDtypeStruct((T, W), BF16),
                   jax.ShapeDtypeStruct((T, W), F32)),
        scratch_shapes=[_token_scratch(ROW_BLK, W)],
        compiler_params=pltpu.CompilerParams(dimension_semantics=("parallel",)),
    )(*args)


def _attn_delta(dmix, outs, lses, sink, dils, *, name):
    T = dmix.shape[0]
    W = MIX_W
    nd = len(dils)

    def body(dmix_ref, oa_ref, ob_ref, la_ref, lb_ref, sink_ref, *rest):
        doa_ref, sta_ref, dsink_ref = rest[0:3]
        dob_refs, stb_refs = rest[3:3 + nd], rest[3 + nd:3 + 2 * nd]
        do_tile, st_tile = rest[3 + 2 * nd:]

        @pl.when(pl.program_id(0) == 0)
        def _():
            dsink_ref[...] = jnp.zeros_like(dsink_ref)

        sel0, lo = _head_sel(0), _stat_lo()
        for mixer, (o_ref, l_ref) in enumerate(((oa_ref, la_ref), (ob_ref, lb_ref))):
            for i in range(N_PAIRS):
                cols = slice(i * LANES, (i + 1) * LANES)
                do = dmix_ref[:, mixer * W + i * LANES:mixer * W + (i + 1) * LANES]
                prod = do * o_ref[:, cols]
                d0 = jnp.sum(jnp.where(sel0, prod, 0.0), axis=1, keepdims=True)
                d1 = jnp.sum(jnp.where(sel0, 0.0, prod), axis=1, keepdims=True)
                delta = jnp.where(sel0, d0, d1)
                lse = l_ref[:, cols]
                stat = jnp.where(lo, lse, delta)
                if mixer == 0:
                    sta_ref[:, cols] = stat
                    doa_ref[:, cols] = do.astype(doa_ref.dtype)
                    dsink_ref[:, cols] += -jnp.sum(jnp.exp(sink_ref[:, cols] - lse) * delta, axis=0, keepdims=True)
                else:
                    st_tile[i] = stat
                    do_tile[i] = do
        for t, r in enumerate(dils):
            _from_tokens(dob_refs[t], r, do_tile)
            _from_tokens(stb_refs[t], r, st_tile)

    class_specs = [_class_rows_spec(ROW_BLK, r, W) for r in dils]
    out_shape = [jax.ShapeDtypeStruct((T, W), BF16), jax.ShapeDtypeStruct((T, W), F32), jax.ShapeDtypeStruct((1, W), F32)]
    out_shape += [jax.ShapeDtypeStruct((T // r, r * W), BF16) for r in dils]
    out_shape += [jax.ShapeDtypeStruct((T // r, r * W), F32) for r in dils]
    res = pl.pallas_call(
        body, name=name, grid=(T // ROW_BLK,),
        in_specs=[_row_spec(2 * W), _row_spec(W), _row_spec(W), _row_spec(W), _row_spec(W), _vec_spec(W)],
        out_specs=tuple([_row_spec(W), _row_spec(W), _vec_spec(W)] + class_specs + class_specs),
        out_shape=tuple(out_shape),
        scratch_shapes=[_token_scratch(ROW_BLK, W), _token_scratch(ROW_BLK, W)],
        compiler_params=pltpu.CompilerParams(dimension_semantics=("arbitrary",)),
    )(dmix, outs[0], outs[1], lses[0], lses[1], sink)
    return res[0], res[1], res[2], res[3:3 + nd], res[3 + nd:]


def _band_bwd(qa, ka, va, d_out, stat, bias, *, r, q_col, k_col, v_col, stride, pattern, name, kv_shared=False,
              carry=None):
    L = qa.shape[0]
    nb = L // QBLK
    dn_nt = (((1,), (1,)), ((), ()))
    dn_tn = (((0,), (0,)), ((), ()))
    scale = HEAD_DIM ** -0.5

    pps = N_PAIRS if r > 1 else N_PAIRS // 2

    def body(q_ref, k_ref, v_ref, do_ref, st_ref, bias_ref, dq_ref, dk_ref, dv_ref, db_ref, dk_carry, dv_carry):
        @pl.when((pl.program_id(0) == 0) & (pl.program_id(1) == 0))
        def _():
            db_ref[...] = jnp.zeros_like(db_ref)

        lo, sel0 = _stat_lo(), _head_sel(0)
        pair0 = pl.program_id(1) * pps

        def block(b, first):
            rows = _rows(b)
            for i in range(pps):
                heads = pl.ds(2 * (pair0 + i), 2)
                cols = slice(i * LANES, (i + 1) * LANES)
                q2 = _stack_heads(q_ref[rows, cols], scale)
                kv_head = (pair0 + i) // 2 if kv_shared else None
                k = _key_window(k_ref, b, cols, first, kv_head)
                v = _key_window(v_ref, b, cols, first, kv_head)
                do2 = _stack_heads(do_ref[rows, cols])
                st = st_ref[rows, cols]
                lse2 = jnp.concatenate(
                    [jnp.max(jnp.where(_head_sel(h) & lo, st, -jnp.inf), axis=1, keepdims=True) for h in range(2)], axis=0)
                delta2 = jnp.concatenate(
                    [jnp.sum(jnp.where(_head_sel(h) & ~lo, st, 0.0), axis=1, keepdims=True) for h in range(2)],
                    axis=0) * (2.0 / HEAD_DIM)
                bias2 = bias_ref[1 if first else 0, heads].reshape(2 * QBLK, 2 * QBLK)
                s = lax.dot_general(q2, k, dn_nt, preferred_element_type=F32) + bias2
                p = jnp.exp(s - lse2)
                dp = lax.dot_general(do2, v, dn_nt, preferred_element_type=F32)
                ds = p * (dp - delta2)
                db_ref[heads] += ds.reshape(2, QBLK, 2 * QBLK)
                dsb = ds.astype(k.dtype)
                dq2 = jnp.dot(dsb, k, preferred_element_type=F32)
                dq_ref[rows, cols] = (jnp.where(sel0, dq2[0:QBLK], dq2[QBLK:]) * scale).astype(dq_ref.dtype)
                dk_acc = lax.dot_general(dsb, q2, dn_tn, preferred_element_type=F32)
                dv_acc = lax.dot_general(p.astype(k.dtype), do2, dn_tn, preferred_element_type=F32)
                if not first:
                    prev = _rows(b - 1)
                    dk_ref[prev, cols] = (dk_carry[:, cols] + dk_acc[0:QBLK]).astype(dk_ref.dtype)
                    dv_ref[prev, cols] = (dv_carry[:, cols] + dv_acc[0:QBLK]).astype(dv_ref.dtype)
                dk_carry[:, cols] = dk_acc[QBLK:2 * QBLK]
                dv_carry[:, cols] = dv_acc[QBLK:2 * QBLK]

        block(0, True)
        if nb > 1:
            pl.loop(1, nb)(lambda b: block(b, False))

        last = _rows(nb - 1)
        dk_ref[last, :] = dk_carry[...].astype(dk_ref.dtype)
        dv_ref[last, :] = dv_carry[...].astype(dv_ref.dtype)

    tok_spec = _class_spec(L, r, 0, 1, pps)
    bias_spec = pl.BlockSpec((None, 2, 2 * N_PAIRS, QBLK, 2 * QBLK), lambda j, c: (pattern, 0, 0, 0, 0))
    dbias_spec = pl.BlockSpec((2 * N_PAIRS, QBLK, 2 * QBLK), lambda j, c: (0, 0, 0))
    grad = jax.ShapeDtypeStruct((L, r * MIX_W), BF16)
    blk_bytes = _nbytes((L, pps * LANES), BF16)
    carry_shape = pltpu.VMEM((QBLK, pps * LANES), F32)
    return _call(
        body, name=name, grid=(r, N_PAIRS // pps),
        in_specs=[_class_spec(L, r, q_col, stride, pps), _kv_spec(L, r, k_col, stride, pps, kv_shared),
                  _kv_spec(L, r, v_col, stride, pps, kv_shared), tok_spec, tok_spec, bias_spec],
        out_specs=[tok_spec, tok_spec, tok_spec, dbias_spec],
        out_shape=[grad, grad, grad, jax.ShapeDtypeStruct((2 * N_PAIRS, QBLK, 2 * QBLK), F32)],
        args=[qa, ka, va, d_out, stat, bias], scratch=[carry_shape, carry_shape],
        vmem_limit=_vmem_limit(*([blk_bytes] * 9)), semantics=("arbitrary", "arbitrary"), carry=carry)


def _dz_assemble(dq_a, dk_a, dv_a, b_grads, dils, *, name):
    T = dq_a.shape[0]
    W = MIX_W

    def group_sum(x):
        u0 = x[:, 0:LANES] + x[:, LANES:2 * LANES]
        u1 = x[:, 2 * LANES:3 * LANES] + x[:, 3 * LANES:4 * LANES]
        s0 = u0 + pltpu.roll(u0, HEAD_DIM, 1)
        s1 = u1 + pltpu.roll(u1, HEAD_DIM, 1)
        return jnp.where(_head_sel(0), s0, s1)

    def body(*refs):
        dqa_ref, dka_ref, dva_ref = refs[0:3]
        b_refs = refs[3:12]
        dza_ref, dzb_ref, sa_ref, sb_ref, scratch = refs[12:]

        @pl.when(pl.program_id(0) == 0)
        def _():
            sa_ref[...] = jnp.zeros_like(sa_ref)
            sb_ref[...] = jnp.zeros_like(sb_ref)

        def put(dst, acc, col, part):
            w = part.shape[1]
            dst[:, col:col + w] = part.astype(dst.dtype)
            acc[:, col:col + w] += jnp.sum(part, axis=0, keepdims=True)

        put(dza_ref, sa_ref, 0, dqa_ref[...].astype(F32))
        put(dza_ref, sa_ref, W, group_sum(dka_ref[...].astype(F32)))
        put(dza_ref, sa_ref, W + LANES, group_sum(dva_ref[...].astype(F32)))
        for t in range(3):
            part = _to_tokens(b_refs[t], dils[0], scratch)
            for pat in range(1, len(dils)):
                part = part + _to_tokens(b_refs[3 * pat + t], dils[pat], scratch)
            put(dzb_ref, sb_ref, t * W, part)

    args = [dq_a, dk_a, dv_a] + [g[t] for g in b_grads for t in range(3)]
    return pl.pallas_call(
        body, name=name, grid=(T // ROW_BLK,),
        in_specs=[_row_spec(W)] * 3 + [_class_rows_spec(ROW_BLK, r, W) for r in dils for _ in range(3)],
        out_specs=(_row_spec(ZA_W), _row_spec(ZB_W), _vec_spec(ZA_W), _vec_spec(ZB_W)),
        out_shape=(jax.ShapeDtypeStruct((T, ZA_W), BF16), jax.ShapeDtypeStruct((T, ZB_W), BF16),
                   jax.ShapeDtypeStruct((1, ZA_W), F32), jax.ShapeDtypeStruct((1, ZB_W), F32)),
        scratch_shapes=[_token_scratch(ROW_BLK, W)],
        compiler_params=pltpu.CompilerParams(dimension_semantics=("arbitrary",)),
    )(*args)


def _place():
    x, y, c = lax.axis_index("x"), lax.axis_index("y"), lax.axis_index("c")
    chips = [(1 - x, y), (x, 1 - y), (1 - x, 1 - y)]
    return x, y, c, chips


def _cast_place(shards, k_idx, *, name):
    n, steps = len(shards), 4

    def body(k_ref, *refs):
        for s_ref, o_ref in zip(refs[:n], refs[n:]):
            o_ref[...] = s_ref[...].astype(o_ref.dtype)

    blocks = [(a.shape[0] // steps, a.shape[1]) for a in shards]
    grid_spec = pltpu.PrefetchScalarGridSpec(
        num_scalar_prefetch=1, grid=(steps,),
        in_specs=[pl.BlockSpec(blk, lambda t, k_ref: (t, 0)) for blk in blocks],
        out_specs=tuple(pl.BlockSpec(blk, lambda t, k_ref: (k_ref[0] * steps + t, 0)) for blk in blocks))
    return pl.pallas_call(
        body, name=name, grid_spec=grid_spec,
        out_shape=tuple(jax.ShapeDtypeStruct((N_CHIPS * a.shape[0], a.shape[1]), BF16) for a in shards),
        compiler_params=pltpu.CompilerParams(dimension_semantics=("parallel",),
                                             vmem_limit_bytes=_vmem_limit(*[_nbytes(b, F32) for b in blocks])),
    )(k_idx, *shards)


def _gather_carry(fulls, part=(0, 1)):
    n = len(fulls)
    p_idx, n_parts = part

    def piece(ref, chip_idx, half):
        rs = ref.shape[0] // N_CHIPS
        rh = rs // 2
        rows = rh // n_parts
        return ref.at[pl.ds(chip_idx * rs + half * rh + p_idx * rows, rows)]

    def copy(sems, sem, src_ref, dst_ref, to):
        return pltpu.make_async_remote_copy(src_ref=src_ref, dst_ref=dst_ref, send_sem=sems[0].at[sem],
                                            recv_sem=sems[1].at[sem], device_id=to, device_id_type=MESH)

    def ici_copy(ins, outs, sems, i, j, chip, k, c):
        return copy(sems, 3 * i + j, piece(ins[i], k, c), piece(outs[i], k, c), (*chip, c))

    def start(ins, outs, sems):
        x, y, c, chips = _place()
        for i in range(n):
            for j, chip in enumerate(chips):
                ici_copy(ins, outs, sems, i, j, chip, 2 * x + y, c).start()

    def finish(ins, outs, sems):
        x, y, c, chips = _place()
        sibling = (x, y, 1 - c)
        passed = []
        for i in range(n):
            for j, chip in enumerate(chips):
                region = piece(outs[i], 2 * chip[0] + chip[1], c)
                copy(sems, 3 * i + j, region, region, (*chip, c)).wait_recv()
                cp = copy(sems, 3 * n + 3 * i + j, region, region, sibling)
                cp.start()
                passed.append(cp)
        for i in range(n):
            for j, chip in enumerate(chips):
                region = piece(outs[i], 2 * chip[0] + chip[1], 1 - c)
                copy(sems, 3 * n + 3 * i + j, region, region, sibling).wait_recv()
        for i in range(n):
            for j, chip in enumerate(chips):
                ici_copy(ins, outs, sems, i, j, chip, 2 * x + y, c).wait_send()
        for cp in passed:
            cp.wait_send()

    return _Carry(fulls, [jax.ShapeDtypeStruct(a.shape, a.dtype) for a in fulls], {i: i for i in range(n)},
                  [pltpu.SemaphoreType.DMA((6 * n,)), pltpu.SemaphoreType.DMA((6 * n,))], start, finish)


def _pair_swap_carry(grads):
    n = len(grads)

    def copy(ins, outs, sems, i):
        x, y, c, _ = _place()
        return pltpu.make_async_remote_copy(src_ref=ins[i].at[:, 1 - c], dst_ref=outs[i], send_sem=sems[0].at[i],
                                            recv_sem=sems[1].at[i], device_id=(x, y, 1 - c), device_id_type=MESH)

    def start(ins, outs, sems):
        for i in range(n):
            copy(ins, outs, sems, i).start()

    def finish(ins, outs, sems):
        for i in range(n):
            copy(ins, outs, sems, i).wait()

    return _Carry(grads, [jax.ShapeDtypeStruct((a.shape[0], a.shape[2], a.shape[3]), a.dtype) for a in grads], {},
                  [pltpu.SemaphoreType.DMA((n,)), pltpu.SemaphoreType.DMA((n,))], start, finish)


def _pair_sum(g4, got, c_idx, *, name):
    _, _, rh, cs = g4.shape
    rb = _row_block(rh, 512, 16)

    def body(c_ref, own_ref, got_ref, o_ref):
        o_ref[...] = (own_ref[...].astype(F32) + got_ref[...].astype(F32)).astype(o_ref.dtype)

    grid_spec = pltpu.PrefetchScalarGridSpec(
        num_scalar_prefetch=1, grid=(N_CHIPS, rh // rb),
        in_specs=[pl.BlockSpec((None, None, rb, cs), lambda k, t, c_ref: (k, c_ref[0], t, 0)),
                  pl.BlockSpec((None, rb, cs), lambda k, t, c_ref: (k, t, 0))],
        out_specs=pl.BlockSpec((None, rb, cs), lambda k, t, c_ref: (k, t, 0)))
    return pl.pallas_call(
        body, name=name, grid_spec=grid_spec, out_shape=jax.ShapeDtypeStruct((N_CHIPS, rh, cs), BF16),
        compiler_params=pltpu.CompilerParams(dimension_semantics=("parallel", "parallel")),
    )(c_idx, g4, got)


def _chip_scatter_carry(sums):
    n = len(sums)

    def copies(ins, outs, sems):
        x, y, c, chips = _place()
        return [pltpu.make_async_remote_copy(src_ref=ins[i].at[2 * chip[0] + chip[1]], dst_ref=outs[i].at[j],
                                             send_sem=sems[0].at[3 * i + j], recv_sem=sems[1].at[3 * i + j],
                                             device_id=(*chip, c), device_id_type=MESH)
                for i in range(n) for j, chip in enumerate(chips)]

    def start(ins, outs, sems):
        for cp in copies(ins, outs, sems):
            cp.start()

    def finish(ins, outs, sems):
        for cp in copies(ins, outs, sems):
            cp.wait()

    return _Carry(sums, [jax.ShapeDtypeStruct((3,) + a.shape[1:], a.dtype) for a in sums], {},
                  [pltpu.SemaphoreType.DMA((3 * n,)), pltpu.SemaphoreType.DMA((3 * n,))], start, finish)


def _chip_sum(sums, gots, kc_idx, *, name):
    n, steps = len(sums), 2

    def body(kc_ref, *refs):
        for own_ref, got_ref, o_ref in zip(refs[:n], refs[n:2 * n], refs[2 * n:]):
            acc = own_ref[...].astype(F32)
            for j in range(3):
                acc = acc + got_ref[j].astype(F32)
            o_ref[...] = acc

    blocks = [(a.shape[1] // steps, a.shape[2]) for a in sums]
    grid_spec = pltpu.PrefetchScalarGridSpec(
        num_scalar_prefetch=1, grid=(steps,),
        in_specs=[pl.BlockSpec((None,) + blk, lambda t, kc: (kc[0], t, 0)) for blk in blocks]
        + [pl.BlockSpec((3,) + blk, lambda t, kc: (0, t, 0)) for blk in blocks],
        out_specs=tuple(pl.BlockSpec(blk, lambda t, kc: (kc[1] * steps + t, 0)) for blk in blocks))
    return pl.pallas_call(
        body, name=name, grid_spec=grid_spec,
        out_shape=tuple(jax.ShapeDtypeStruct((2 * a.shape[1], a.shape[2]), F32) for a in sums),
        compiler_params=pltpu.CompilerParams(dimension_semantics=("parallel",),
                                             vmem_limit_bytes=_vmem_limit(*[3 * _nbytes(b, F32) for b in blocks])),
    )(kc_idx, *sums, *gots)


def _half_swap_carry(shards):
    n = len(shards)

    def copy(ins, outs, sems, i, to_my_half):
        x, y, c, _ = _place()
        rh = ins[i].shape[0] // 2
        half = c if to_my_half else 1 - c
        return pltpu.make_async_remote_copy(
            src_ref=ins[i].at[pl.ds(c * rh, rh)], dst_ref=outs[i].at[pl.ds(half * rh, rh)],
            send_sem=sems[0].at[i], recv_sem=sems[1].at[i], device_id=(x, y, 1 - c), device_id_type=MESH)

    def start(ins, outs, sems):
        for i in range(n):
            copy(ins, outs, sems, i, True).start()

    def finish(ins, outs, sems):
        for i in range(n):
            copy(ins, outs, sems, i, False).wait_recv()
        for i in range(n):
            copy(ins, outs, sems, i, True).wait_send()

    return _Carry(shards, [jax.ShapeDtypeStruct(a.shape, a.dtype) for a in shards], {i: i for i in range(n)},
                  [pltpu.SemaphoreType.DMA((n,)), pltpu.SemaphoreType.DMA((n,))], start, finish)


SMALL_ROW = 1024


def _small_layout(shapes):
    starts, row = [], 0
    for r, c in shapes:
        starts.append(row)
        row += -(-c // SMALL_ROW) if r == 1 else r
    return starts, -(-row // SUBLANES) * SUBLANES


def _small_pieces(shape, start):
    r, c = shape
    if r == 1:
        return [(start + q, min(SMALL_ROW, c - q * SMALL_ROW), q * SMALL_ROW) for q in range(-(-c // SMALL_ROW))]
    return None


def _whole_spec(shape):
    return pl.BlockSpec(tuple(shape), lambda i: (0,) * len(shape))


def _small_all_reduce(parts, *, name):
    shapes = [a.shape for a in parts]
    starts, rows = _small_layout(shapes)
    n = len(parts)
    n_tables = sum(1 for r, _ in shapes if r > 1)
    assert all(r > 1 and c <= LANES for r, c in shapes[:n_tables]) and all(r == 1 for r, _ in shapes[n_tables:])
    table_rows = starts[n_tables]
    assert table_rows % SUBLANES == 0

    def regions(slot):
        return slot.at[pl.ds(0, table_rows), pl.ds(0, LANES)], slot.at[pl.ds(table_rows, rows - table_rows)]

    def body(*refs):
        ins, outs = refs[0:n], refs[n:2 * n]
        buf, send_sems, recv_sems = refs[2 * n:]
        x, y, c, _ = _place()
        me = 4 * x + 2 * y + c
        mine = buf.at[me]
        mine[...] = jnp.zeros((rows, SMALL_ROW), F32)
        for ref, shape, start in zip(ins, shapes, starts):
            pieces = _small_pieces(shape, start)
            if pieces is None:
                mine[start:start + shape[0], 0:shape[1]] = ref[...]
            else:
                for row, width, col in pieces:
                    mine[row:row + 1, 0:width] = ref[:, col:col + width]
        def peer(d):
            return 1 - x if d & 4 else x, 1 - y if d & 2 else y, 1 - c if d & 1 else c

        def copy(d, part, dst_slot):
            sem = 2 * (d - 1) + part
            return pltpu.make_async_remote_copy(
                src_ref=regions(mine)[part], dst_ref=regions(dst_slot)[part], send_sem=send_sems.at[sem],
                recv_sem=recv_sems.at[sem], device_id=peer(d), device_id_type=MESH)

        for d in range(1, 8):
            for part in range(2):
                copy(d, part, mine).start()
        for d in range(1, 8):
            px, py, pc = peer(d)
            for part in range(2):
                copy(d, part, buf.at[4 * px + 2 * py + pc]).wait_recv()
        for d in range(1, 8):
            for part in range(2):
                copy(d, part, mine).wait_send()
        tables, vectors = regions(buf.at[0])
        acc_t, acc_v = tables[...], vectors[...]
        for s in range(1, 8):
            tables, vectors = regions(buf.at[s])
            acc_t, acc_v = acc_t + tables[...], acc_v + vectors[...]
        tables, vectors = regions(mine)
        tables[...] = acc_t
        vectors[...] = acc_v
        for ref, shape, start in zip(outs, shapes, starts):
            pieces = _small_pieces(shape, start)
            if pieces is None:
                ref[...] = mine[start:start + shape[0], 0:shape[1]]
            else:
                for row, width, col in pieces:
                    ref[:, col:col + width] = mine[row:row + 1, 0:width]

    whole = [_whole_spec(s) for s in shapes]
    return pl.pallas_call(
        body, name=name, grid=(1,), in_specs=whole, out_specs=tuple(whole),
        out_shape=tuple(jax.ShapeDtypeStruct(s, F32) for s in shapes),
        scratch_shapes=[pltpu.VMEM((8, rows, SMALL_ROW), F32), pltpu.SemaphoreType.DMA((14,)),
                        pltpu.SemaphoreType.DMA((14,))],
    )(*parts)


def _adamw_small(ws, gs, ms, vs, *, name):
    n = len(ws)
    c1 = 1.0 - ADAM_B1 ** ADAM_STEP
    c2 = 1.0 - ADAM_B2 ** ADAM_STEP

    def body(*refs):
        for k in range(n):
            w_ref, g_ref, m_ref, v_ref = (refs[t * n + k] for t in range(4))
            go_ref, d_ref, mo_ref, vo_ref = (refs[(4 + t) * n + k] for t in range(4))
            gv = g_ref[...]
            go_ref[...] = gv
            mn = ADAM_B1 * m_ref[...] + (1.0 - ADAM_B1) * gv
            vn = ADAM_B2 * v_ref[...] + (1.0 - ADAM_B2) * (gv * gv)
            d_ref[...] = -ADAM_LR * ((mn / c1) / (jnp.sqrt(vn / c2) + ADAM_EPS) + ADAM_WD * w_ref[...])
            mo_ref[...] = mn
            vo_ref[...] = vn

    whole = [_whole_spec(a.shape) for a in ws]
    shapes = tuple(jax.ShapeDtypeStruct(a.shape, F32) for a in ws)
    res = pl.pallas_call(
        body, name=name, grid=(1,), in_specs=whole * 4, out_specs=tuple(whole * 4), out_shape=shapes * 4,
    )(*ws, *gs, *ms, *vs)
    return res[0:n], res[n:2 * n], res[2 * n:3 * n], res[3 * n:4 * n]


def _adamw(ws, gs, ms, vs, *, name):
    n, steps = len(ws), 8
    c1 = 1.0 - ADAM_B1 ** ADAM_STEP
    c2 = 1.0 - ADAM_B2 ** ADAM_STEP

    def body(*refs):
        for k in range(n):
            w_ref, g_ref, m_ref, v_ref = (refs[t * n + k] for t in range(4))
            go_ref, d_ref, mo_ref, vo_ref = (refs[(4 + t) * n + k] for t in range(4))
            gv = g_ref[...]
            go_ref[...] = gv
            mn = ADAM_B1 * m_ref[...] + (1.0 - ADAM_B1) * gv
            vn = ADAM_B2 * v_ref[...] + (1.0 - ADAM_B2) * (gv * gv)
            d_ref[...] = -ADAM_LR * ((mn / c1) / (jnp.sqrt(vn / c2) + ADAM_EPS) + ADAM_WD * w_ref[...])
            mo_ref[...] = mn
            vo_ref[...] = vn

    blocks = [(a.shape[0] // steps, a.shape[1]) for a in ws]
    specs = [pl.BlockSpec(blk, lambda i: (i, 0)) for blk in blocks]
    shapes = tuple(jax.ShapeDtypeStruct(a.shape, F32) for a in ws)
    res = pl.pallas_call(
        body, name=name, grid=(steps,), in_specs=specs * 4, out_specs=tuple(specs * 4), out_shape=shapes * 4,
        compiler_params=pltpu.CompilerParams(dimension_semantics=("parallel",),
                                             vmem_limit_bytes=_vmem_limit(*[8 * _nbytes(b, F32) for b in blocks])),
    )(*ws, *gs, *ms, *vs)
    return res[0:n], res[n:2 * n], res[2 * n:3 * n], res[3 * n:4 * n]


def _local_step(x, p, target, small, sched):
    T = x.shape[0]
    W = MIX_W
    rel_bias = small["rel_bias"]
    b_in_a, b_in_b = small["b_in"][:, 0:ZA_W], small["b_in"][:, ZA_W:]

    idx_np = [_t5_index(A_WINDOW - 1, 1)] + [_t5_index(window // dil, dil) for window, dil in B_PATTERNS]
    idx_all = jnp.asarray(np.stack(idx_np))
    n1, bias_all = sched.run(_prologue, x, small["ffn1_pre_g"], rel_bias, idx_all, name="prologue")
    w_gu1 = sched.weight("ffn1_w_gu")
    *gu1, a1 = sched.run(_ffn_up, n1, w_gu1, bm=512, name="ffn1_gu")
    w_d1 = sched.weight("ffn1_w_down")
    f1, h1, n2 = sched.run(_mm_resid_norm, a1, w_d1, x, small["ffn1_post_g"], 0.5, small["attn_pre_g"], bm=512,
                           name="ffn1_down")
    win_a, win_b = sched.weight("w_in")
    za = _mm_nt(n2, win_a, bm=1024, bn=ZA_W, out_dtype=BF16, name="in_proj_a", add=b_in_a)
    dils = tuple(dil for _, dil in B_PATTERNS)
    zb_by_dil = _mm_nt_classes(n2, win_b, b_in_b, dils, bm=512, name="in_proj_b")

    a_args = dict(r=1, q_col=0, k_col=W // LANES, v_col=W // LANES + 1, stride=0, pattern=0, kv_shared=True)
    o_a, lse_a1 = sched.run(_band_fwd, za, za, za, bias_all, name="band_a_fwd", **a_args)
    sink_row = jnp.repeat(small["sinks"], HEAD_DIM, axis=1)
    out_a, out_a_bf, lse_a = _merge([(o_a, lse_a1)], (1,), sink_row, name="merge_a")

    no_sink = jnp.full((1, W), NEG, F32)
    b_ctx, branches = [], []
    for t, dil in enumerate(dils):
        zb_r = zb_by_dil[t]
        args = dict(r=dil, q_col=0, k_col=1, v_col=2, stride=ZB_W // W, pattern=1 + t)
        branches.append(sched.run(_band_fwd, zb_r, zb_r, zb_r, bias_all, name=f"band_b{t}_fwd", **args))
        b_ctx.append((jnp.asarray(idx_np[1 + t]), zb_r, args))
    out_b, out_b_bf, lse_b = _merge(branches, dils, no_sink, name="merge_b")

    mix = jnp.concatenate([out_a_bf, out_b_bf], axis=1)
    w_out = sched.weight("w_out")
    att, h2, n3 = _mm_resid_norm(mix, w_out, h1, small["attn_post_g"], 1.0, small["ffn2_pre_g"], bm=512,
                                 name="out_proj", bias=small["b_out"])
    w_gu2, w_d2 = sched.weight("ffn2_w_gu"), sched.weight("ffn2_w_down")
    *gu2, a2 = _ffn_up(n3, w_gu2, bm=512, name="ffn2_gu")
    f2, h3, n4 = _mm_resid_norm(a2, w_d2, h2, small["ffn2_post_g"], 0.5, small["ple_pre_g"], bm=512,
                                name="ffn2_down")
    w_gate = sched.weight("w_ple_gate")
    p_bf = p.astype(BF16)
    dh4, de, dgp, loss, d_ple_post = _ple_head(n4, w_gate, p_bf, sched.weight("w_ple_proj"), h3, small["ple_post_g"],
                                               target, bm=512, name="ple_head")

    gs = {"ple_post_g": d_ple_post}
    sched.grad("w_ple_proj", _mm_tn(p_bf, de, bm=256, bn=1024, out_dtype=BF16, name="d_w_ple"))
    sched.grad("w_ple_gate", _mm_tn(n4, dgp, bm=512, bn=1024, out_dtype=BF16, name="d_w_gate"))
    dh3, df2, gs["ple_pre_g"], gs["ffn2_post_g"], _ = sched.run(
        _mm_nt_norms_bwd, dgp, w_gate, dh4, h3, small["ple_pre_g"], f2, small["ffn2_post_g"], 0.5, bm=512, name="d_n4")

    def ffn_bwd(df, a, gu, n, w_down, w_gu, tag, *norm_args):
        dgu = sched.run(_ffn_down_bwd, df, w_down, gu[0], gu[1], bm=512, name=f"ffn{tag}_d_a")
        sched.grad(f"ffn{tag}_w_gu", _mm_tn(n, dgu, bm=512, bn=1408, out_dtype=BF16, name=f"ffn{tag}_d_w_gu",
                                            n_stack=N_CHIPS))
        sched.grad(f"ffn{tag}_w_down", sched.run(_mm_tn, a, df, bm=256, bn=1024, out_dtype=BF16,
                                                name=f"ffn{tag}_d_w_down"))
        return sched.run(_mm_nt_norms_bwd, dgu, w_gu, *norm_args, bm=512, name=f"ffn{tag}_d_n")

    dh2, datt, gs["ffn2_pre_g"], gs["attn_post_g"], gs["b_out"] = ffn_bwd(
        df2, a2, gu2, n3, w_d2, w_gu2, "2", dh3, h2, small["ffn2_pre_g"], att, small["attn_post_g"], 1.0)
    sched.grad("w_out", _mm_tn(mix, datt, bm=512, bn=1024, out_dtype=BF16, name="d_w_out"))
    dmix = sched.run(_mm_nt, datt, w_out, bm=1024, bn=1024, out_dtype=F32, name="d_mix")

    do_a, stat_a, dsink, do_b, stat_b = _attn_delta(dmix, (out_a, out_b), (lse_a, lse_b), sink_row, dils,
                                                    name="attn_delta")
    gs["sinks"] = dsink[:, ::HEAD_DIM]
    dq_a, dk_a, dv_a, dbias_a = sched.run(_band_bwd, za, za, za, do_a, stat_a, bias_all, name="band_a_bwd", **a_args)
    d_rel_a = _bias_grad(dbias_a, jnp.asarray(idx_np[0]), name="d_rel_a")
    b_grads = []
    d_rel_b = None
    for t, (idx, zb_r, args) in enumerate(b_ctx):
        dq, dk, dv, dbias = sched.run(_band_bwd, zb_r, zb_r, zb_r, do_b[t], stat_b[t], bias_all,
                                      name=f"band_b{t}_bwd", **args)
        b_grads.append((dq, dk, dv))
        d_rel = _bias_grad(dbias, idx, name=f"d_rel_b{t}")
        d_rel_b = d_rel if d_rel_b is None else d_rel_b + d_rel
    gs["rel_bias"] = jnp.concatenate([d_rel_a[:, 0:2 * N_PAIRS], d_rel_b[:, 0:2 * N_PAIRS]], axis=1)
    dza, dzb, dsum_a, dsum_b = _dz_assemble(dq_a, dk_a, dv_a, b_grads, dils, name="d_z")
    gs["b_in"] = jnp.concatenate([dsum_a, dsum_b], axis=1)
    sched.grad("w_in", (_mm_tn(dza, n2, bm=ZA_W, bn=1024, out_dtype=BF16, name="d_w_in_a"),
                        _mm_tn(dzb, n2, bm=ZB_W // 2, bn=1024, out_dtype=BF16, name="d_w_in_b")))
    dn2_a = _mm_nn(dza, win_a, bm=1024, bn=1024, out_dtype=F32, name="d_n2_a")
    dh1, df1, gs["attn_pre_g"], gs["ffn1_post_g"], _ = sched.run(
        _mm_nt_norms_bwd, dzb, win_b, dh2, h1, small["attn_pre_g"], f1, small["ffn1_post_g"], 0.5, bm=512,
        name="d_n2_b", add=dn2_a, b_is_kn=True)
    grad_x, gs["ffn1_pre_g"] = ffn_bwd(df1, a1, gu1, n1, w_d1, w_gu1, "1", dh1, x, small["ffn1_pre_g"], None, None, 0.0)
    return loss, grad_x, gs


def _to_compute(name, full):
    st = full.reshape(N_CHIPS, full.shape[0] // N_CHIPS, full.shape[1])
    if name in ("ffn1_w_gu", "ffn2_w_gu"):
        return st
    if name == "w_ple_proj":
        return jnp.transpose(st, (1, 0, 2)).reshape(st.shape[1], -1)
    if name == "w_in":
        return full[0:ZA_W], full[ZA_W:]
    return full


def _to_comm(name, g):
    if name == "w_in":
        g = jnp.concatenate(g, axis=0)
    if name == "w_ple_proj":
        g = jnp.transpose(g.reshape(g.shape[0], N_CHIPS, -1), (1, 0, 2))
    rs = g.shape[1] if g.ndim == 3 else g.shape[0] // N_CHIPS
    return g.reshape(N_CHIPS, 2, rs // 2, g.shape[-1])


class _Schedule:
    GATHER = {
        "prologue": [("ffn1_w_gu", (0, 1))],
        "ffn1_gu": [("ffn1_w_down", (0, 1)), ("w_in", (0, 1))],
        "ffn1_down": [("w_out", (0, 1))],
        "band_a_fwd": [("ffn2_w_gu", (0, 2))],
        "band_b0_fwd": [("ffn2_w_gu", (1, 2))],
        "band_b1_fwd": [("ffn2_w_down", (0, 1))],
        "band_b2_fwd": [("w_ple_gate", (0, 1)), ("w_ple_proj", (0, 1))],
    }
    SWAP = {"d_n4": ["w_ple_proj", "w_ple_gate"], "ffn2_d_w_down": ["ffn2_w_gu"], "ffn2_d_n": ["ffn2_w_down"],
            "d_mix": ["w_out"], "d_n2_b": ["w_in"], "ffn1_d_w_down": ["ffn1_w_gu"], "ffn1_d_n": ["ffn1_w_down"]}
    SCATTER = {"ffn2_d_a": ["w_ple_proj", "w_ple_gate"], "ffn2_d_n": ["ffn2_w_gu"], "band_a_bwd": ["ffn2_w_down"],
               "band_b0_bwd": ["w_out"], "ffn1_d_a": ["w_in"], "ffn1_d_n": ["ffn1_w_gu"]}

    def __init__(self, placed, c_idx, kc_idx):
        self.full = dict(placed)
        self.missing = {n: 1.0 for n in placed}
        self.c_idx, self.kc_idx = c_idx, kc_idx
        self.grads, self.swapped, self.pair_sums, self.scattered = {}, {}, {}, {}

    def _gather(self, entries):
        carries, after = [], []
        for part in sorted({pt for _, pt in entries}):
            names = [n for n, pt in entries if pt == part]
            carry = _gather_carry([self.full[n] for n in names], part)
            carries.append(carry)

            def done(carry=carry, names=names, part=part):
                for n, a in zip(names, carry.results):
                    self.full[n] = a
                    self.missing[n] -= 1.0 / part[1]
            after.append(done)
        return carries, after

    def weight(self, name):
        assert abs(self.missing[name]) < 1e-9, (name, self.missing[name])
        return _to_compute(name, self.full[name])

    def grad(self, name, g):
        self.grads[name] = _to_comm(name, g)

    def _swap(self, names):
        carry = _pair_swap_carry([self.grads[n] for n in names])

        def done():
            self.swapped.update(zip(names, carry.results))
        return carry, done

    def _scatter(self, names):
        for n in names:
            self.pair_sums[n] = _pair_sum(self.grads[n], self.swapped[n], self.c_idx, name=f"grad_pair_sum_{n}")
        carry = _chip_scatter_carry([self.pair_sums[n] for n in names])

        def done():
            self.scattered.update(zip(names, carry.results))
        return carry, done

    def run(self, fn, *args, name, **kw):
        carries, after = self._gather(self.GATHER.get(name, []))
        for names, make in ((self.SWAP.get(name), self._swap), (self.SCATTER.get(name), self._scatter)):
            if names:
                carry, done = make(names)
                carries.append(carry)
                after.append(done)
        out = fn(*args, name=name, carry=_join(carries), **kw)
        for done in after:
            done()
        return out

    def finish(self):
        left = [n for n in BIG if n not in self.scattered]
        to_swap = [n for n in left if n not in self.swapped]
        if to_swap:
            carry, done = self._swap(to_swap)
            _comm_call(carry, name="grad_pair_swap")
            done()
        if left:
            carry, done = self._scatter(left)
            _comm_call(carry, name="grad_chip_scatter")
            done()
        halves = _chip_sum([self.pair_sums[n] for n in BIG], [self.scattered[n] for n in BIG], self.kc_idx,
                           name="grad_chip_sum")
        carry = _half_swap_carry(halves)
        _comm_call(carry, name="grad_half_swap")
        return dict(zip(BIG, carry.results))


def kernel(x, p, rel_bias, ffn1_pre_g, ffn1_w_gu, ffn1_w_down, ffn1_post_g, attn_pre_g, w_in, b_in, sinks, w_out, b_out, attn_post_g, ffn2_pre_g, ffn2_w_gu, ffn2_w_down, ffn2_post_g, ple_pre_g, w_ple_gate, w_ple_proj, ple_post_g, loss_target, m_rel_bias, m_ffn1_pre_g, m_ffn1_w_gu, m_ffn1_w_down, m_ffn1_post_g, m_attn_pre_g, m_w_in, m_b_in, m_sinks, m_w_out, m_b_out, m_attn_post_g, m_ffn2_pre_g, m_ffn2_w_gu, m_ffn2_w_down, m_ffn2_post_g, m_ple_pre_g, m_w_ple_gate, m_w_ple_proj, m_ple_post_g, v_rel_bias, v_ffn1_pre_g, v_ffn1_w_gu, v_ffn1_w_down, v_ffn1_post_g, v_attn_pre_g, v_w_in, v_b_in, v_sinks, v_w_out, v_b_out, v_attn_post_g, v_ffn2_pre_g, v_ffn2_w_gu, v_ffn2_w_down, v_ffn2_post_g, v_ple_pre_g, v_w_ple_gate, v_w_ple_proj, v_ple_post_g):
    given = dict(locals())
    w = {n: given[n] for n in WEIGHTS}
    m = {n: given["m_" + n] for n in WEIGHTS}
    v = {n: given["v_" + n] for n in WEIGHTS}
    c_pos = lax.axis_index("c").astype(jnp.int32)
    k_pos = (2 * lax.axis_index("x") + lax.axis_index("y")).astype(jnp.int32)
    c_idx, k_idx, kc_idx = c_pos.reshape(1), k_pos.reshape(1), jnp.stack([k_pos, c_pos])

    def shard(a, n):
        return jnp.transpose(a[0]) if n == "w_in" else a[0]

    def unshard(a, n):
        return (jnp.transpose(a) if n == "w_in" else a)[None]

    placed = _cast_place([shard(w[n], n) for n in BIG], k_idx, name="place_shards")
    sched = _Schedule(dict(zip(BIG, placed)), c_idx, kc_idx)
    small = {n: (w[n] if n == "rel_bias" else w[n].reshape(1, -1)) for n in SMALL}

    loss_part, grad_x, gs = _local_step(x[0], p[0, 0], loss_target[0], small, sched)

    grads_big = sched.finish()

    *small_grads, loss_row = _small_all_reduce([gs[n] for n in SMALL] + [loss_part], name="small_all_reduce")
    loss = loss_row[0, 0]

    grads, delta, new_m, new_v = {}, {}, {}, {}
    res = _adamw([shard(w[n], n) for n in BIG], [grads_big[n] for n in BIG], [shard(m[n], n) for n in BIG],
                 [shard(v[n], n) for n in BIG], name="adamw_big")
    for n, gg, dd, mm, vv in zip(BIG, *res):
        grads[n], delta[n], new_m[n], new_v[n] = (unshard(a, n) for a in (gg, dd, mm, vv))
    res = _adamw_small([w[n] for n in SMALL], small_grads, [m[n] for n in SMALL], [v[n] for n in SMALL],
                       name="adamw_small")
    for name, gg, dd, mm, vv in zip(SMALL, *res):
        grads[name], delta[name], new_m[name], new_v[name] = gg, dd, mm, vv

    return (loss, grad_x[None], *[grads[n] for n in WEIGHTS], *[delta[n] for n in WEIGHTS],
            *[new_m[n] for n in WEIGHTS], *[new_v[n] for n in WEIGHTS])
```

```python
import math

import jax
import jax.numpy as jnp
import numpy as np
from jax import lax
from jax.experimental import pallas as pl
from jax.experimental.pallas import tpu as pltpu

F32 = jnp.float32
BF16 = jnp.bfloat16
MESH = pl.DeviceIdType.MESH

EPS = 1e-6
NEG = -1e30
LANES = 128
SUBLANES = 8
HEAD_DIM = 64
QBLK = 128
N_PAIRS = 4
MIX_W = N_PAIRS * LANES
A_WINDOW = 128
B_PATTERNS = ((128, 1), (512, 4), (2048, 16))
NUM_BUCKETS = 32
MAX_DISTANCE = 2048
ZA_W = 768
ZB_W = 1536
N_CHIPS = 4
VMEM_BYTES_V7X = 64 * 2**20

ADAM_LR, ADAM_B1, ADAM_B2, ADAM_EPS, ADAM_WD, ADAM_STEP = 0.001, 0.9, 0.999, 1e-08, 0.01, 10

BIG = ("ffn1_w_gu", "ffn1_w_down", "w_in", "w_out", "ffn2_w_gu", "ffn2_w_down", "w_ple_gate", "w_ple_proj")
SMALL = ("rel_bias", "ffn1_pre_g", "ffn1_post_g", "attn_pre_g", "b_in", "sinks", "b_out", "attn_post_g",
         "ffn2_pre_g", "ffn2_post_g", "ple_pre_g", "ple_post_g")
WEIGHTS = ("rel_bias", "ffn1_pre_g", "ffn1_w_gu", "ffn1_w_down", "ffn1_post_g", "attn_pre_g", "w_in", "b_in",
           "sinks", "w_out", "b_out", "attn_post_g", "ffn2_pre_g", "ffn2_w_gu", "ffn2_w_down", "ffn2_post_g",
           "ple_pre_g", "w_ple_gate", "w_ple_proj", "ple_post_g")


def _vmem_limit(*block_bytes):
    need = 2 * sum(block_bytes) + max(block_bytes) * 2 + (4 << 20)
    return int(min(max(need, 32 << 20), VMEM_BYTES_V7X - (6 << 20)))


def _nbytes(shape, dtype):
    return int(np.prod(shape)) * jnp.dtype(dtype).itemsize


MXU_WIDTH_V7X = 256


def _col_chunks(n):
    return [slice(c, min(c + MXU_WIDTH_V7X, n)) for c in range(0, n, MXU_WIDTH_V7X)]


def _row_halves(rows):
    return [slice(0, rows // 2), slice(rows // 2, rows)]


def _row_block(rows, cap, mult):
    for cand in range(min(rows, cap), 0, -1):
        if rows % cand == 0 and cand % mult == 0:
            return cand
    raise ValueError((rows, cap, mult))


class _Carry:
    def __init__(self, operands, out_shapes, aliases, sems, start, finish):
        self.operands, self.out_shapes, self.aliases, self.sems = list(operands), list(out_shapes), dict(aliases), list(sems)
        self.start, self.finish = start, finish
        self.results = None


def _join(carries):
    carries = [c for c in carries if c is not None]
    if not carries:
        return None
    if len(carries) == 1:
        return carries[0]
    offs, pos = [], [0, 0, 0]
    for c in carries:
        offs.append(tuple(pos))
        pos = [pos[0] + len(c.operands), pos[1] + len(c.out_shapes), pos[2] + len(c.sems)]
    aliases = {}
    for c, (oi, oo, _) in zip(carries, offs):
        aliases.update({oi + i: oo + o for i, o in c.aliases.items()})

    def run(which):
        def fn(ins, outs, sems):
            for c, (oi, oo, os_) in zip(carries, offs):
                getattr(c, which)(ins[oi:oi + len(c.operands)], outs[oo:oo + len(c.out_shapes)],
                                  sems[os_:os_ + len(c.sems)])
        return fn

    joined = _Carry([a for c in carries for a in c.operands], [s for c in carries for s in c.out_shapes], aliases,
                    [s for c in carries for s in c.sems], run("start"), run("finish"))
    joined.parts = (carries, offs)
    return joined


def _set_results(carry, outs):
    carry.results = list(outs)
    if hasattr(carry, "parts"):
        for c, (_, oo, _) in zip(*carry.parts):
            _set_results(c, outs[oo:oo + len(c.out_shapes)])


ANY = pl.BlockSpec(memory_space=pl.ANY)


def _call(body, *, name, grid, in_specs, out_specs, out_shape, args, scratch=(), vmem_limit=None, carry=None,
          semantics=None):
    n_ci, n_co, n_cs = len(args), len(out_shape), len(scratch)
    semantics = semantics or ("parallel",) * len(grid)
    if carry is None:
        res = pl.pallas_call(
            body, name=name, grid=grid, in_specs=list(in_specs), out_specs=tuple(out_specs), out_shape=tuple(out_shape),
            scratch_shapes=list(scratch),
            compiler_params=pltpu.CompilerParams(dimension_semantics=semantics, vmem_limit_bytes=vmem_limit),
        )(*args)
        return list(res)
    n_pi, n_po = len(carry.operands), len(carry.out_shapes)

    def full_body(*refs):
        ci, pi = refs[:n_ci], refs[n_ci:n_ci + n_pi]
        o0 = n_ci + n_pi
        co, po = refs[o0:o0 + n_co], refs[o0 + n_co:o0 + n_co + n_po]
        s0 = o0 + n_co + n_po
        cs, ps = refs[s0:s0 + n_cs], refs[s0 + n_cs:]
        ids = [pl.program_id(ax) for ax in range(len(grid))]
        first, last = ids[0] == 0, ids[0] == grid[0] - 1
        for ax in range(1, len(grid)):
            first, last = first & (ids[ax] == 0), last & (ids[ax] == grid[ax] - 1)

        @pl.when(first)
        def _():
            carry.start(pi, po, ps)

        body(*ci, *co, *cs)

        @pl.when(last)
        def _():
            carry.finish(pi, po, ps)

    res = pl.pallas_call(
        full_body, name=name, grid=grid, in_specs=list(in_specs) + [ANY] * n_pi,
        out_specs=tuple(out_specs) + tuple([ANY] * n_po), out_shape=tuple(out_shape) + tuple(carry.out_shapes),
        scratch_shapes=list(scratch) + carry.sems,
        input_output_aliases={n_ci + i: n_co + o for i, o in carry.aliases.items()},
        compiler_params=pltpu.CompilerParams(dimension_semantics=("arbitrary",) * len(grid),
                                             vmem_limit_bytes=vmem_limit),
    )(*args, *carry.operands)
    _set_results(carry, res[n_co:])
    return list(res[:n_co])


def _comm_call(carry, *, name):
    n_pi, n_po = len(carry.operands), len(carry.out_shapes)

    def body(*refs):
        pi, po, ps = refs[:n_pi], refs[n_pi:n_pi + n_po], refs[n_pi + n_po:]
        carry.start(pi, po, ps)
        carry.finish(pi, po, ps)

    res = pl.pallas_call(
        body, name=name, in_specs=[ANY] * n_pi, out_specs=tuple([ANY] * n_po), out_shape=tuple(carry.out_shapes),
        scratch_shapes=carry.sems, input_output_aliases=dict(carry.aliases),
    )(*carry.operands)
    _set_results(carry, res)


def _mm_nn(a, b, *, bm, bn, out_dtype, name, bias=None, carry=None):
    M, K = a.shape
    stacked = b.ndim == 3
    N = b.shape[0] * b.shape[2] if stacked else b.shape[1]
    assert M % bm == 0 and N % bn == 0 and (not stacked or bn == b.shape[2])

    def body(*refs):
        a_ref, b_ref = refs[0], refs[1]
        o_ref = refs[-1]
        acc = jnp.dot(a_ref[...], b_ref[...], preferred_element_type=F32)
        if bias is not None:
            acc = acc + refs[2][...]
        o_ref[...] = acc.astype(o_ref.dtype)

    if stacked:
        b_spec = pl.BlockSpec((None, K, bn), lambda i, j: (j, 0, 0))
    else:
        b_spec = pl.BlockSpec((K, bn), lambda i, j: (0, j))
    in_specs = [pl.BlockSpec((bm, K), lambda i, j: (i, 0)), b_spec]
    args = [a, b]
    if bias is not None:
        in_specs.append(pl.BlockSpec((1, bn), lambda i, j: (0, j)))
        args.append(bias)
    limit = _vmem_limit(_nbytes((bm, K), a.dtype), _nbytes((K, bn), b.dtype), _nbytes((bm, bn), F32))
    return _call(body, name=name, grid=(M // bm, N // bn), in_specs=in_specs,
                 out_specs=[pl.BlockSpec((bm, bn), lambda i, j: (i, j))],
                 out_shape=[jax.ShapeDtypeStruct((M, N), out_dtype)], args=args, vmem_limit=limit, carry=carry)[0]


def _mm_nt(a, b, *, bm, bn, out_dtype, name, add=None, carry=None):
    M, K = a.shape
    stacked = b.ndim == 3
    N = b.shape[1] if stacked else b.shape[0]
    n_sh = b.shape[0] if stacked else 1
    ks = b.shape[2] if stacked else K
    assert M % bm == 0 and N % bn == 0 and n_sh * ks == K
    dn = (((1,), (1,)), ((), ()))

    def body(*refs):
        a_ref, b_ref = refs[0], refs[1]
        o_ref = refs[-1]
        if stacked:
            acc = lax.dot_general(a_ref[:, 0:ks], b_ref[0], dn, preferred_element_type=F32)
            for s in range(1, n_sh):
                acc = acc + lax.dot_general(a_ref[:, s * ks:(s + 1) * ks], b_ref[s], dn, preferred_element_type=F32)
        else:
            acc = lax.dot_general(a_ref[...], b_ref[...], dn, preferred_element_type=F32)
        if add is not None:
            acc = acc + refs[2][...]
        o_ref[...] = acc.astype(o_ref.dtype)

    if stacked:
        b_spec = pl.BlockSpec((n_sh, bn, ks), lambda i, j: (0, j, 0))
    else:
        b_spec = pl.BlockSpec((bn, K), lambda i, j: (j, 0))
    in_specs = [pl.BlockSpec((bm, K), lambda i, j: (i, 0)), b_spec]
    args = [a, b]
    if add is not None:
        rows = bm if add.shape[0] == M else 1
        in_specs.append(pl.BlockSpec((rows, bn), lambda i, j: (i if rows == bm else 0, j)))
        args.append(add)
    limit = _vmem_limit(_nbytes((bm, K), a.dtype), _nbytes((bn, K), b.dtype), _nbytes((bm, bn), F32))
    return _call(body, name=name, grid=(M // bm, N // bn), in_specs=in_specs,
                 out_specs=[pl.BlockSpec((bm, bn), lambda i, j: (i, j))],
                 out_shape=[jax.ShapeDtypeStruct((M, N), out_dtype)], args=args, vmem_limit=limit, carry=carry)[0]


def _mm_tn(a, b, *, bm, bn, out_dtype, name, n_stack=None, carry=None):
    T, M = a.shape
    N = b.shape[1]
    assert M % bm == 0 and N % bn == 0 and (n_stack is None or bn * n_stack == N)
    dn = (((0,), (0,)), ((), ()))

    def body(a_ref, b_ref, o_ref):
        acc = lax.dot_general(a_ref[...], b_ref[...], dn, preferred_element_type=F32)
        o_ref[...] = acc.astype(o_ref.dtype)

    if n_stack is None:
        out_spec = pl.BlockSpec((bm, bn), lambda i, j: (i, j))
        out_shape = jax.ShapeDtypeStruct((M, N), out_dtype)
    else:
        out_spec = pl.BlockSpec((None, bm, bn), lambda i, j: (j, i, 0))
        out_shape = jax.ShapeDtypeStruct((n_stack, M, bn), out_dtype)
    limit = _vmem_limit(_nbytes((T, bm), a.dtype), _nbytes((T, bn), b.dtype), _nbytes((bm, bn), F32))
    return _call(body, name=name, grid=(M // bm, N // bn),
                 in_specs=[pl.BlockSpec((T, bm), lambda i, j: (0, i)), pl.BlockSpec((T, bn), lambda i, j: (0, j))],
                 out_specs=[out_spec], out_shape=[out_shape], args=[a, b], vmem_limit=limit, carry=carry)[0]


ROW_BLK = 256


def _rstd(x):
    return lax.rsqrt(jnp.mean(x * x, axis=-1, keepdims=True) + EPS)


def _norm_bwd(xhat, rstd, dxhat):
    return rstd * (dxhat - xhat * jnp.mean(dxhat * xhat, axis=-1, keepdims=True))


def _row_spec(cols):
    return pl.BlockSpec((ROW_BLK, cols), lambda i: (i, 0))


def _vec_spec(cols):
    return pl.BlockSpec((1, cols), lambda i: (0, 0))


def _prologue(x, g, rel_bias, idx_all, *, name, carry=None):
    T, D = x.shape
    n_pat = idx_all.shape[0]
    heads = 2 * N_PAIRS
    steps = n_pat * heads
    rows = T // steps

    def body(rb_ref, x_ref, g_ref, idx_ref, n_ref, bias_ref):
        s = pl.program_id(0)
        head = jnp.where(s < heads, s, heads + s % heads)
        xv = x_ref[...]
        n_ref[...] = (xv * _rstd(xv) * g_ref[...]).astype(n_ref.dtype)
        idxv = idx_ref[...]
        acc = jnp.where(idxv < 0, NEG, 0.0).astype(F32)
        for b in range(NUM_BUCKETS):
            acc = acc + jnp.where(idxv == b, rb_ref[b, head], 0.0)
        bias_ref[0] = acc
        kap = lax.broadcasted_iota(jnp.int32, (1, 2 * QBLK), 1)
        bias_ref[1] = jnp.where(kap < QBLK, NEG, acc)

    return _call(
        body, name=name, grid=(steps,),
        in_specs=[pl.BlockSpec(memory_space=pltpu.SMEM), pl.BlockSpec((rows, D), lambda s: (s, 0)),
                  pl.BlockSpec((1, D), lambda s: (0, 0)),
                  pl.BlockSpec((None, QBLK, 2 * QBLK), lambda s: (s // heads, 0, 0))],
        out_specs=[pl.BlockSpec((rows, D), lambda s: (s, 0)),
                   pl.BlockSpec((None, 2, None, QBLK, 2 * QBLK), lambda s: (s // heads, 0, s % heads, 0, 0))],
        out_shape=[jax.ShapeDtypeStruct((T, D), BF16),
                   jax.ShapeDtypeStruct((n_pat, 2, heads, QBLK, 2 * QBLK), F32)],
        args=[rel_bias, x, g, idx_all], carry=carry)


def _mm_resid_norm(a, b, h, g_post, coef, g_next, *, bm, name, bias=None, carry=None):
    M, K = a.shape
    N = b.shape[1]

    def body(*refs):
        a_ref, b_ref, h_ref, gp_ref, gn_ref = refs[0:5]
        f_ref, hn_ref, n_ref = refs[-3:]
        for rows in _row_halves(bm):
            f = jnp.dot(a_ref[rows, :], b_ref[...], preferred_element_type=F32)
            if bias is not None:
                f = f + refs[5][...]
            f_ref[rows, :] = f
            hn = h_ref[rows, :] + coef * (f * _rstd(f) * gp_ref[...])
            hn_ref[rows, :] = hn
            n_ref[rows, :] = (hn * _rstd(hn) * gn_ref[...]).astype(n_ref.dtype)

    row = lambda w: pl.BlockSpec((bm, w), lambda i: (i, 0))
    vec = pl.BlockSpec((1, N), lambda i: (0, 0))
    in_specs = [row(K), pl.BlockSpec((K, N), lambda i: (0, 0), pipeline_mode=pl.Buffered(1)), row(N), vec, vec]
    args = [a, b, h, g_post, g_next]
    if bias is not None:
        in_specs.append(vec)
        args.append(bias)
    limit = _vmem_limit(_nbytes((bm, K), a.dtype), _nbytes((K, N), b.dtype) // 2, 4 * _nbytes((bm, N), F32))
    return _call(body, name=name, grid=(M // bm,), in_specs=in_specs, out_specs=[row(N), row(N), row(N)],
                 out_shape=[jax.ShapeDtypeStruct((M, N), F32), jax.ShapeDtypeStruct((M, N), F32),
                            jax.ShapeDtypeStruct((M, N), BF16)], args=args, vmem_limit=limit, carry=carry)


def _mm_nt_norms_bwd(a, b, dh_in, h, g_pre, f, g_post, coef, *, bm, name, add=None, b_is_kn=False, carry=None):
    M, K = a.shape
    stacked = b.ndim == 3
    N = b.shape[1] if (stacked or b_is_kn) else b.shape[0]
    n_sh = b.shape[0] if stacked else 1
    ks = b.shape[2] if stacked else K
    assert n_sh * ks == K
    dn_dims = (((1,), (0,)), ((), ())) if b_is_kn else (((1,), (1,)), ((), ()))
    with_f = f is not None
    n_in = 5 + (2 if with_f else 0) + (1 if add is not None else 0)

    def body(*refs):
        a_ref, b_ref, dhi_ref, h_ref, gpre_ref = refs[0:5]
        outs = refs[n_in:]

        @pl.when(pl.program_id(0) == 0)
        def _():
            for acc in (outs[2:] if with_f else outs[1:]):
                acc[...] = jnp.zeros_like(acc)

        for rows in _row_halves(bm):
            if stacked:
                dn = lax.dot_general(a_ref[rows, 0:ks], b_ref[0], dn_dims, preferred_element_type=F32)
                for s in range(1, n_sh):
                    dn = dn + lax.dot_general(a_ref[rows, s * ks:(s + 1) * ks], b_ref[s], dn_dims,
                                              preferred_element_type=F32)
            else:
                dn = lax.dot_general(a_ref[rows, :], b_ref[...], dn_dims, preferred_element_type=F32)
            if add is not None:
                dn = dn + refs[n_in - 1][rows, :]
            hv = h_ref[rows, :]
            r = _rstd(hv)
            hh = hv * r
            dh = dhi_ref[rows, :] + _norm_bwd(hh, r, dn * gpre_ref[...])
            outs[0][rows, :] = dh
            if not with_f:
                outs[1][...] += jnp.sum(dn * hh, axis=0, keepdims=True)
                continue
            f_ref, gpost_ref = refs[5], refs[6]
            df_ref, dgpre_ref, dgpost_ref, dsum_ref = outs[1:]
            dgpre_ref[...] += jnp.sum(dn * hh, axis=0, keepdims=True)
            fv = f_ref[rows, :]
            rf = _rstd(fv)
            fh = fv * rf
            dy = coef * dh
            dgpost_ref[...] += jnp.sum(dy * fh, axis=0, keepdims=True)
            df = _norm_bwd(fh, rf, dy * gpost_ref[...])
            dsum_ref[...] += jnp.sum(df, axis=0, keepdims=True)
            df_ref[rows, :] = df.astype(df_ref.dtype)

    row = lambda w: pl.BlockSpec((bm, w), lambda i: (i, 0))
    vec = pl.BlockSpec((1, N), lambda i: (0, 0))
    once = pl.Buffered(1)
    if stacked:
        b_spec = pl.BlockSpec((n_sh, N, ks), lambda i: (0, 0, 0), pipeline_mode=once)
    else:
        b_spec = pl.BlockSpec(b.shape, lambda i: (0, 0), pipeline_mode=once)
    in_specs = [row(K), b_spec, row(N), row(N), vec]
    args = [a, b, dh_in, h, g_pre]
    if with_f:
        in_specs += [row(N), vec]
        args += [f, g_post]
    if add is not None:
        in_specs.append(row(N))
        args.append(add)
    vec_shape = jax.ShapeDtypeStruct((1, N), F32)
    out_specs = [row(N)] + ([row(N), vec, vec, vec] if with_f else [vec])
    out_shape = [jax.ShapeDtypeStruct((M, N), F32)]
    out_shape += [jax.ShapeDtypeStruct((M, N), BF16), vec_shape, vec_shape, vec_shape] if with_f else [vec_shape]
    limit = _vmem_limit(_nbytes((bm, K), a.dtype), _nbytes((N, K), b.dtype) // 2, 6 * _nbytes((bm, N), F32))
    return _call(body, name=name, grid=(M // bm,), in_specs=in_specs, out_specs=out_specs, out_shape=out_shape,
                 args=args, vmem_limit=limit, semantics=("arbitrary",), carry=carry)


def _ffn_up(n, w_gu, *, bm, name, carry=None):
    M, K = n.shape
    n_sh, _, ns = w_gu.shape
    half = n_sh // 2

    def body(n_ref, wg_ref, wu_ref, g_ref, u_ref, a_ref):
        nv = n_ref[...]
        for cols in _col_chunks(ns):
            g = jnp.dot(nv, wg_ref[:, cols], preferred_element_type=F32)
            u = jnp.dot(nv, wu_ref[:, cols], preferred_element_type=F32)
            g_ref[:, cols] = g.astype(g_ref.dtype)
            u_ref[:, cols] = u.astype(u_ref.dtype)
            a_ref[:, cols] = (g * jax.nn.sigmoid(g) * u).astype(a_ref.dtype)

    out_spec = pl.BlockSpec((bm, ns), lambda i, j: (i, j))
    out = jax.ShapeDtypeStruct((M, half * ns), BF16)
    limit = _vmem_limit(_nbytes((bm, K), n.dtype), 2 * _nbytes((K, ns), w_gu.dtype), 3 * _nbytes((bm, ns), F32))
    return _call(body, name=name, grid=(M // bm, half),
                 in_specs=[pl.BlockSpec((bm, K), lambda i, j: (i, 0)),
                           pl.BlockSpec((None, K, ns), lambda i, j: (j, 0, 0)),
                           pl.BlockSpec((None, K, ns), lambda i, j: (j + half, 0, 0))],
                 out_specs=[out_spec, out_spec, out_spec], out_shape=[out, out, out], args=[n, w_gu, w_gu],
                 vmem_limit=limit, carry=carry)


def _ffn_down_bwd(df, w_down, g, u, *, bm, name, carry=None):
    M, D = df.shape
    F = w_down.shape[0]
    dn = (((1,), (1,)), ((), ()))

    def body(df_ref, w_ref, g_ref, u_ref, o_ref):
        dfv = df_ref[...]
        for cols in _col_chunks(F):
            da = lax.dot_general(dfv, w_ref[cols, :], dn, preferred_element_type=F32)
            gv = g_ref[:, cols].astype(F32)
            s = jax.nn.sigmoid(gv)
            o_ref[:, cols] = (da * u_ref[:, cols].astype(F32) * (s * (1.0 + gv * (1.0 - s)))).astype(o_ref.dtype)
            o_ref[:, F + cols.start:F + cols.stop] = (da * (gv * s)).astype(o_ref.dtype)

    row = lambda w: pl.BlockSpec((bm, w), lambda i: (i, 0))
    limit = _vmem_limit(_nbytes((F, D), w_down.dtype) // 2, 2 * _nbytes((bm, F), BF16), _nbytes((bm, 2 * F), BF16))
    return _call(body, name=name, grid=(M // bm,),
                 in_specs=[row(D), pl.BlockSpec((F, D), lambda i: (0, 0), pipeline_mode=pl.Buffered(1)), row(F), row(F)],
                 out_specs=[row(2 * F)], out_shape=[jax.ShapeDtypeStruct((M, 2 * F), BF16)],
                 args=[df, w_down, g, u], vmem_limit=limit, carry=carry)[0]


def _ple_head(n4, w_gate, p, w_ple, h3, g_post, target, *, bm, name):
    T, D = h3.shape
    kp = p.shape[1]

    def body(n_ref, wg_ref, p_ref, wp_ref, h_ref, g_ref, t_ref, dh_ref, de_ref, dgp_ref, loss_ref, dg_ref):
        @pl.when(pl.program_id(0) == 0)
        def _():
            loss_ref[...] = jnp.zeros_like(loss_ref)
            dg_ref[...] = jnp.zeros_like(dg_ref)

        gpost = g_ref[...]
        for rows in _row_halves(bm):
            gate = jax.nn.sigmoid(jnp.dot(n_ref[rows, :], wg_ref[...], preferred_element_type=F32))
            ev = jnp.dot(p_ref[rows, :], wp_ref[...], preferred_element_type=F32)
            ge = gate * ev
            r = _rstd(ge)
            gh = ge * r
            diff = h_ref[rows, :] + gh * gpost - t_ref[rows, :]
            loss_ref[...] += jnp.sum(diff * diff) * (0.5 / D)
            dh = diff * (1.0 / D)
            dh_ref[rows, :] = dh
            dg_ref[...] += jnp.sum(dh * gh, axis=0, keepdims=True)
            dge = _norm_bwd(gh, r, dh * gpost)
            de_ref[rows, :] = (dge * gate).astype(de_ref.dtype)
            dgp_ref[rows, :] = (dge * ev * gate * (1.0 - gate)).astype(dgp_ref.dtype)

    row = lambda w: pl.BlockSpec((bm, w), lambda i: (i, 0))
    whole = lambda a: pl.BlockSpec(a.shape, lambda i: (0, 0), pipeline_mode=pl.Buffered(1))
    limit = _vmem_limit(_nbytes((bm, D), BF16), _nbytes(w_gate.shape, w_gate.dtype) // 2, 6 * _nbytes((bm, D), F32))
    return pl.pallas_call(
        body, name=name, grid=(T // bm,),
        in_specs=[row(D), whole(w_gate), row(kp), whole(w_ple), row(D), _vec_spec(D), row(D)],
        out_specs=(row(D), row(D), row(D), _vec_spec(LANES), _vec_spec(D)),
        out_shape=(jax.ShapeDtypeStruct((T, D), F32), jax.ShapeDtypeStruct((T, D), BF16),
                   jax.ShapeDtypeStruct((T, D), BF16), jax.ShapeDtypeStruct((1, LANES), F32),
                   jax.ShapeDtypeStruct((1, D), F32)),
        compiler_params=pltpu.CompilerParams(dimension_semantics=("arbitrary",), vmem_limit_bytes=limit),
    )(n4, w_gate, p, w_ple, h3, g_post, target)


def _t5_index(max_dist, stride):
    rho = np.arange(QBLK)[:, None]
    kap = np.arange(2 * QBLK)[None, :]
    d = rho + QBLK - kap
    valid = (d >= 0) & (d <= max_dist)
    n = np.maximum(d, 0) * stride
    max_exact = NUM_BUCKETS // 2
    nf = np.maximum(n, 1).astype(np.float32)
    large = max_exact + (np.log(nf / np.float32(max_exact)) / np.float32(math.log(MAX_DISTANCE / max_exact))
                         * np.float32(NUM_BUCKETS - max_exact)).astype(np.int32)
    large = np.minimum(large, NUM_BUCKETS - 1)
    bucket = np.where(n < max_exact, n, large)
    return np.where(valid, bucket, -1).astype(np.int32)


def _bias_grad(d_bias, idx, *, name):
    def body(db_ref, idx_ref, o_ref):
        h = pl.program_id(0)

        @pl.when(h == 0)
        def _():
            o_ref[...] = jnp.zeros_like(o_ref)

        idxv = idx_ref[...]
        dv = db_ref[0]
        rows = lax.broadcasted_iota(jnp.int32, (NUM_BUCKETS, LANES), 0)
        lanes = lax.broadcasted_iota(jnp.int32, (NUM_BUCKETS, LANES), 1)
        acc = o_ref[...]
        for b in range(NUM_BUCKETS):
            s = jnp.sum(jnp.where(idxv == b, dv, 0.0))
            acc = acc + jnp.where((rows == b) & (lanes == h), s, 0.0)
        o_ref[...] = acc

    return pl.pallas_call(
        body, name=name, grid=(2 * N_PAIRS,),
        in_specs=[pl.BlockSpec((1, QBLK, 2 * QBLK), lambda h: (h, 0, 0)),
                  pl.BlockSpec((QBLK, 2 * QBLK), lambda h: (0, 0))],
        out_specs=pl.BlockSpec((NUM_BUCKETS, LANES), lambda h: (0, 0)),
        out_shape=jax.ShapeDtypeStruct((NUM_BUCKETS, LANES), F32),
        compiler_params=pltpu.CompilerParams(dimension_semantics=("arbitrary",)),
    )(d_bias, idx)


def _lane():
    return lax.broadcasted_iota(jnp.int32, (1, LANES), 1)


def _head_sel(h):
    return (_lane() < HEAD_DIM) if h == 0 else (_lane() >= HEAD_DIM)


def _stat_lo():
    return (_lane() % HEAD_DIM) < (HEAD_DIM // 2)


def _stack_heads(t, scale=None):
    if scale is not None:
        t = t * scale
    zero = jnp.zeros_like(t)
    return jnp.concatenate([jnp.where(_head_sel(0), t, zero), jnp.where(_head_sel(1), t, zero)], axis=0)


def _class_spec(L, r, cols, stride, pps):
    chunks = N_PAIRS // pps
    mode = pl.Buffered(1) if r * chunks == 1 else None
    return pl.BlockSpec((L, MIX_W // chunks), lambda j, c: (0, (cols + j * stride) * chunks + c), pipeline_mode=mode)


def _rows(b, n_blocks=1):
    return pl.ds(pl.multiple_of(b * QBLK, QBLK), n_blocks * QBLK)


def _key_window(ref, b, cols, first, kv_head=None):
    if kv_head is not None:
        cols = slice(0, LANES)
    if first:
        blk = ref[0:QBLK, cols]
        tile = jnp.concatenate([blk, blk], axis=0)
    else:
        tile = ref[_rows(b - 1, 2), cols]
    if kv_head is None:
        return tile
    wide = tile.astype(F32)
    keep = jnp.logical_xor(_lane() < HEAD_DIM, kv_head == 1)
    return jnp.where(keep, wide, pltpu.roll(wide, HEAD_DIM, 1)).astype(tile.dtype)


def _kv_spec(L, r, col, stride, pps, kv_shared):
    if not kv_shared:
        return _class_spec(L, r, col, stride, pps)
    mode = pl.Buffered(1) if pps == N_PAIRS else None
    return pl.BlockSpec((L, LANES), lambda j, c: (0, col), pipeline_mode=mode)


def _band_fwd(qa, ka, va, bias, *, r, q_col, k_col, v_col, stride, pattern, name, kv_shared=False, sink=None,
              carry=None):
    L = qa.shape[0]
    nb = L // QBLK
    dn = (((1,), (1,)), ((), ()))
    scale = HEAD_DIM ** -0.5

    pps = N_PAIRS

    def body(q_ref, k_ref, v_ref, bias_ref, *rest):
        sel0 = _head_sel(0)
        pair0 = pl.program_id(1) * pps

        def emit(rows, cols, o, lse):
            if sink is None:
                o_ref, lse_ref = rest
                o_ref[rows, cols] = o.astype(o_ref.dtype)
                lse_ref[rows, cols] = lse
                return
            sink_ref, out_ref, outb_ref, lse_ref = rest
            sk = sink_ref[:, cols]
            mx = jnp.maximum(lse, sk)
            w = jnp.exp(lse - mx)
            den = w + jnp.exp(sk - mx)
            out = o * (w / den)
            out_ref[rows, cols] = out
            outb_ref[rows, cols] = out.astype(outb_ref.dtype)
            lse_ref[rows, cols] = mx + jnp.log(den)

        def block(b, first):
            rows = _rows(b)
            for i in range(pps):
                cols = slice(i * LANES, (i + 1) * LANES)
                q2 = _stack_heads(q_ref[rows, cols], scale)
                kv_head = (pair0 + i) // 2 if kv_shared else None
                k = _key_window(k_ref, b, cols, first, kv_head)
                v = _key_window(v_ref, b, cols, first, kv_head)
                bias2 = bias_ref[1 if first else 0, pl.ds(2 * (pair0 + i), 2)].reshape(2 * QBLK, 2 * QBLK)
                s = lax.dot_general(q2, k, dn, preferred_element_type=F32) + bias2
                m = jnp.max(s, axis=1, keepdims=True)
                p = jnp.exp(s - m)
                l = jnp.sum(p, axis=1, keepdims=True)
                o = jnp.dot(p.astype(v.dtype), v, preferred_element_type=F32) * (1.0 / l)
                lse = m + jnp.log(l)
                emit(rows, cols, jnp.where(sel0, o[0:QBLK], o[QBLK:]), jnp.where(sel0, lse[0:QBLK], lse[QBLK:]))

        block(0, True)
        if nb > 1:
            pl.loop(1, nb)(lambda b: block(b, False))

    bias_spec = pl.BlockSpec((None, 2, 2 * N_PAIRS, QBLK, 2 * QBLK), lambda j, c: (pattern, 0, 0, 0, 0))
    out_spec = _class_spec(L, r, 0, 1, pps)
    blk_bytes = _nbytes((L, pps * LANES), F32)
    in_specs = [_class_spec(L, r, q_col, stride, pps), _kv_spec(L, r, k_col, stride, pps, kv_shared),
                _kv_spec(L, r, v_col, stride, pps, kv_shared), bias_spec]
    args = [qa, ka, va, bias]
    f32_out, bf16_out = jax.ShapeDtypeStruct((L, r * MIX_W), F32), jax.ShapeDtypeStruct((L, r * MIX_W), BF16)
    out_shape = [bf16_out, f32_out]
    if sink is not None:
        in_specs.append(pl.BlockSpec((1, MIX_W), lambda j, c: (0, 0)))
        args.append(sink)
        out_shape = [f32_out, bf16_out, f32_out]
    return _call(body, name=name, grid=(r, N_PAIRS // pps), in_specs=in_specs, out_specs=[out_spec] * len(out_shape),
                 out_shape=out_shape, args=args, vmem_limit=_vmem_limit(*([blk_bytes] * 4)), carry=carry)


def _class_rows_spec(rows, r, width):
    return pl.BlockSpec((rows // r, r * width), lambda i, *_: (i, 0))


def _token_scratch(rows, width):
    return pltpu.VMEM((width // LANES, rows, LANES), F32)


def _fill_tokens(scratch, value):
    for c in range(scratch.shape[0]):
        scratch[c] = value[:, c * LANES:(c + 1) * LANES]


def _read_tokens(scratch):
    return jnp.concatenate([scratch[c] for c in range(scratch.shape[0])], axis=1)


def _to_tokens(blk_ref, r, scratch):
    if r == 1:
        return blk_ref[...].astype(F32)
    nt, rows, _ = scratch.shape
    for j in range(r):
        for c in range(nt):
            lanes = slice((j * nt + c) * LANES, (j * nt + c + 1) * LANES)
            scratch.at[c][pl.ds(j, rows // r, stride=r), :] = blk_ref[:, lanes].astype(F32)
    return _read_tokens(scratch)


def _from_tokens(dst_ref, r, scratch):
    nt, rows, _ = scratch.shape
    if r == 1:
        dst_ref[...] = _read_tokens(scratch).astype(dst_ref.dtype)
        return
    for j in range(r):
        for c in range(nt):
            lanes = slice((j * nt + c) * LANES, (j * nt + c + 1) * LANES)
            dst_ref[:, lanes] = scratch.at[c][pl.ds(j, rows // r, stride=r), :].astype(dst_ref.dtype)


def _mm_nt_classes(a, b, bias, dils, *, bm, name):
    M, K = a.shape
    N = b.shape[0]
    dn = (((1,), (1,)), ((), ()))

    def body(a_ref, b_ref, bias_ref, *rest):
        outs, tile = rest[:-1], rest[-1]
        _fill_tokens(tile, lax.dot_general(a_ref[...], b_ref[...], dn, preferred_element_type=F32) + bias_ref[...])
        for out, r in zip(outs, dils):
            _from_tokens(out, r, tile)

    limit = _vmem_limit(_nbytes((bm, K), a.dtype), _nbytes((K, N), b.dtype) // 2, (1 + len(dils)) * _nbytes((bm, N), F32))
    return pl.pallas_call(
        body, name=name, grid=(M // bm,),
        in_specs=[pl.BlockSpec((bm, K), lambda i: (i, 0)),
                  pl.BlockSpec((N, K), lambda i: (0, 0), pipeline_mode=pl.Buffered(1)),
                  pl.BlockSpec((1, N), lambda i: (0, 0))],
        out_specs=tuple(_class_rows_spec(bm, r, N) for r in dils),
        out_shape=tuple(jax.ShapeDtypeStruct((M // r, r * N), BF16) for r in dils),
        scratch_shapes=[_token_scratch(bm, N)],
        compiler_params=pltpu.CompilerParams(dimension_semantics=("parallel",), vmem_limit_bytes=limit),
    )(a, b, bias)


def _merge(branches, dils, sink, *, name):
    W = MIX_W
    T = branches[0][0].shape[0] * dils[0]
    nbr = len(branches)

    def body(*refs):
        sink_ref = refs[2 * nbr]
        out_ref, outb_ref, lse_ref, scratch = refs[2 * nbr + 1:]
        sk = sink_ref[...]
        lses = [_to_tokens(refs[2 * t + 1], dils[t], scratch) for t in range(nbr)]
        mx = sk
        for lse in lses:
            mx = jnp.maximum(mx, lse)
        den = jnp.exp(sk - mx)
        num = jnp.zeros_like(mx)
        for t in range(nbr):
            w = jnp.exp(lses[t] - mx)
            den = den + w
            num = num + w * _to_tokens(refs[2 * t], dils[t], scratch)
        out = num / den
        out_ref[...] = out
        outb_ref[...] = out.astype(outb_ref.dtype)
        lse_ref[...] = mx + jnp.log(den)

    args = [a for br in branches for a in br] + [sink]
    in_specs = [_class_rows_spec(ROW_BLK, r, W) for r in dils for _ in range(2)] + [_vec_spec(W)]
    return pl.pallas_call(
        body, name=name, grid=(T // ROW_BLK,), in_specs=in_specs,
        out_specs=(_row_spec(W), _row_spec(W), _row_spec(W)),
        out_shape=(jax.ShapeDtypeStruct((T, W), F32), jax.ShapeDtypeStruct((T, W), BF16),
                   jax.ShapeDtypeStruct((T, W), F32)),
        scratch_shapes=[_token_scratch(ROW_BLK, W)],
        compiler_params=pltpu.CompilerParams(dimension_semantics=("parallel",)),
    )(*args)


def _attn_delta(dmix, outs, lses, sink, dils, *, name):
    T = dmix.shape[0]
    W = MIX_W
    nd = len(dils)

    def body(dmix_ref, oa_ref, ob_ref, la_ref, lb_ref, sink_ref, *rest):
        doa_ref, sta_ref, dsink_ref = rest[0:3]
        dob_refs, stb_refs = rest[3:3 + nd], rest[3 + nd:3 + 2 * nd]
        do_tile, st_tile = rest[3 + 2 * nd:]

        @pl.when(pl.program_id(0) == 0)
        def _():
            dsink_ref[...] = jnp.zeros_like(dsink_ref)

        sel0, lo = _head_sel(0), _stat_lo()
        for mixer, (o_ref, l_ref) in enumerate(((oa_ref, la_ref), (ob_ref, lb_ref))):
            for i in range(N_PAIRS):
                cols = slice(i * LANES, (i + 1) * LANES)
                do = dmix_ref[:, mixer * W + i * LANES:mixer * W + (i + 1) * LANES]
                prod = do * o_ref[:, cols]
                d0 = jnp.sum(jnp.where(sel0, prod, 0.0), axis=1, keepdims=True)
                d1 = jnp.sum(jnp.where(sel0, 0.0, prod), axis=1, keepdims=True)
                delta = jnp.where(sel0, d0, d1)
                lse = l_ref[:, cols]
                stat = jnp.where(lo, lse, delta)
                if mixer == 0:
                    sta_ref[:, cols] = stat
                    doa_ref[:, cols] = do.astype(doa_ref.dtype)
                    dsink_ref[:, cols] += -jnp.sum(jnp.exp(sink_ref[:, cols] - lse) * delta, axis=0, keepdims=True)
                else:
                    st_tile[i] = stat
                    do_tile[i] = do
        for t, r in enumerate(dils):
            _from_tokens(dob_refs[t], r, do_tile)
            _from_tokens(stb_refs[t], r, st_tile)

    class_specs = [_class_rows_spec(ROW_BLK, r, W) for r in dils]
    out_shape = [jax.ShapeDtypeStruct((T, W), BF16), jax.ShapeDtypeStruct((T, W), F32), jax.ShapeDtypeStruct((1, W), F32)]
    out_shape += [jax.ShapeDtypeStruct((T // r, r * W), BF16) for r in dils]
    out_shape += [jax.ShapeDtypeStruct((T // r, r * W), F32) for r in dils]
    res = pl.pallas_call(
        body, name=name, grid=(T // ROW_BLK,),
        in_specs=[_row_spec(2 * W), _row_spec(W), _row_spec(W), _row_spec(W), _row_spec(W), _vec_spec(W)],
        out_specs=tuple([_row_spec(W), _row_spec(W), _vec_spec(W)] + class_specs + class_specs),
        out_shape=tuple(out_shape),
        scratch_shapes=[_token_scratch(ROW_BLK, W), _token_scratch(ROW_BLK, W)],
        compiler_params=pltpu.CompilerParams(dimension_semantics=("arbitrary",)),
    )(dmix, outs[0], outs[1], lses[0], lses[1], sink)
    return res[0], res[1], res[2], res[3:3 + nd], res[3 + nd:]


def _band_bwd(qa, ka, va, d_out, stat, bias, *, r, q_col, k_col, v_col, stride, pattern, name, kv_shared=False,
              carry=None):
    L = qa.shape[0]
    nb = L // QBLK
    dn_nt = (((1,), (1,)), ((), ()))
    dn_tn = (((0,), (0,)), ((), ()))
    scale = HEAD_DIM ** -0.5

    pps = N_PAIRS if r > 1 else N_PAIRS // 2

    def body(q_ref, k_ref, v_ref, do_ref, st_ref, bias_ref, dq_ref, dk_ref, dv_ref, db_ref, dk_carry, dv_carry):
        @pl.when((pl.program_id(0) == 0) & (pl.program_id(1) == 0))
        def _():
            db_ref[...] = jnp.zeros_like(db_ref)

        lo, sel0 = _stat_lo(), _head_sel(0)
        pair0 = pl.program_id(1) * pps

        def block(b, first):
            rows = _rows(b)
            for i in range(pps):
                heads = pl.ds(2 * (pair0 + i), 2)
                cols = slice(i * LANES, (i + 1) * LANES)
                q2 = _stack_heads(q_ref[rows, cols], scale)
                kv_head = (pair0 + i) // 2 if kv_shared else None
                k = _key_window(k_ref, b, cols, first, kv_head)
                v = _key_window(v_ref, b, cols, first, kv_head)
                do2 = _stack_heads(do_ref[rows, cols])
                st = st_ref[rows, cols]
                lse2 = jnp.concatenate(
                    [jnp.max(jnp.where(_head_sel(h) & lo, st, -jnp.inf), axis=1, keepdims=True) for h in range(2)], axis=0)
                delta2 = jnp.concatenate(
                    [jnp.sum(jnp.where(_head_sel(h) & ~lo, st, 0.0), axis=1, keepdims=True) for h in range(2)],
                    axis=0) * (2.0 / HEAD_DIM)
                bias2 = bias_ref[1 if first else 0, heads].reshape(2 * QBLK, 2 * QBLK)
                s = lax.dot_general(q2, k, dn_nt, preferred_element_type=F32) + bias2
                p = jnp.exp(s - lse2)
                dp = lax.dot_general(do2, v, dn_nt, preferred_element_type=F32)
                ds = p * (dp - delta2)
                db_ref[heads] += ds.reshape(2, QBLK, 2 * QBLK)
                dsb = ds.astype(k.dtype)
                dq2 = jnp.dot(dsb, k, preferred_element_type=F32)
                dq_ref[rows, cols] = (jnp.where(sel0, dq2[0:QBLK], dq2[QBLK:]) * scale).astype(dq_ref.dtype)
                dk_acc = lax.dot_general(dsb, q2, dn_tn, preferred_element_type=F32)
                dv_acc = lax.dot_general(p.astype(k.dtype), do2, dn_tn, preferred_element_type=F32)
                if not first:
                    prev = _rows(b - 1)
                    dk_ref[prev, cols] = (dk_carry[:, cols] + dk_acc[0:QBLK]).astype(dk_ref.dtype)
                    dv_ref[prev, cols] = (dv_carry[:, cols] + dv_acc[0:QBLK]).astype(dv_ref.dtype)
                dk_carry[:, cols] = dk_acc[QBLK:2 * QBLK]
                dv_carry[:, cols] = dv_acc[QBLK:2 * QBLK]

        block(0, True)
        if nb > 1:
            pl.loop(1, nb)(lambda b: block(b, False))

        last = _rows(nb - 1)
        dk_ref[last, :] = dk_carry[...].astype(dk_ref.dtype)
        dv_ref[last, :] = dv_carry[...].astype(dv_ref.dtype)

    tok_spec = _class_spec(L, r, 0, 1, pps)
    bias_spec = pl.BlockSpec((None, 2, 2 * N_PAIRS, QBLK, 2 * QBLK), lambda j, c: (pattern, 0, 0, 0, 0))
    dbias_spec = pl.BlockSpec((2 * N_PAIRS, QBLK, 2 * QBLK), lambda j, c: (0, 0, 0))
    grad = jax.ShapeDtypeStruct((L, r * MIX_W), BF16)
    blk_bytes = _nbytes((L, pps * LANES), BF16)
    carry_shape = pltpu.VMEM((QBLK, pps * LANES), F32)
    return _call(
        body, name=name, grid=(r, N_PAIRS // pps),
        in_specs=[_class_spec(L, r, q_col, stride, pps), _kv_spec(L, r, k_col, stride, pps, kv_shared),
                  _kv_spec(L, r, v_col, stride, pps, kv_shared), tok_spec, tok_spec, bias_spec],
        out_specs=[tok_spec, tok_spec, tok_spec, dbias_spec],
        out_shape=[grad, grad, grad, jax.ShapeDtypeStruct((2 * N_PAIRS, QBLK, 2 * QBLK), F32)],
        args=[qa, ka, va, d_out, stat, bias], scratch=[carry_shape, carry_shape],
        vmem_limit=_vmem_limit(*([blk_bytes] * 9)), semantics=("arbitrary", "arbitrary"), carry=carry)


def _dz_assemble(dq_a, dk_a, dv_a, b_grads, dils, *, name):
    T = dq_a.shape[0]
    W = MIX_W

    def group_sum(x):
        u0 = x[:, 0:LANES] + x[:, LANES:2 * LANES]
        u1 = x[:, 2 * LANES:3 * LANES] + x[:, 3 * LANES:4 * LANES]
        s0 = u0 + pltpu.roll(u0, HEAD_DIM, 1)
        s1 = u1 + pltpu.roll(u1, HEAD_DIM, 1)
        return jnp.where(_head_sel(0), s0, s1)

    def body(*refs):
        dqa_ref, dka_ref, dva_ref = refs[0:3]
        b_refs = refs[3:12]
        dza_ref, dzb_ref, sa_ref, sb_ref, scratch = refs[12:]

        @pl.when(pl.program_id(0) == 0)
        def _():
            sa_ref[...] = jnp.zeros_like(sa_ref)
            sb_ref[...] = jnp.zeros_like(sb_ref)

        def put(dst, acc, col, part):
            w = part.shape[1]
            dst[:, col:col + w] = part.astype(dst.dtype)
            acc[:, col:col + w] += jnp.sum(part, axis=0, keepdims=True)

        put(dza_ref, sa_ref, 0, dqa_ref[...].astype(F32))
        put(dza_ref, sa_ref, W, group_sum(dka_ref[...].astype(F32)))
        put(dza_ref, sa_ref, W + LANES, group_sum(dva_ref[...].astype(F32)))
        for t in range(3):
            part = _to_tokens(b_refs[t], dils[0], scratch)
            for pat in range(1, len(dils)):
                part = part + _to_tokens(b_refs[3 * pat + t], dils[pat], scratch)
            put(dzb_ref, sb_ref, t * W, part)

    args = [dq_a, dk_a, dv_a] + [g[t] for g in b_grads for t in range(3)]
    return pl.pallas_call(
        body, name=name, grid=(T // ROW_BLK,),
        in_specs=[_row_spec(W)] * 3 + [_class_rows_spec(ROW_BLK, r, W) for r in dils for _ in range(3)],
        out_specs=(_row_spec(ZA_W), _row_spec(ZB_W), _vec_spec(ZA_W), _vec_spec(ZB_W)),
        out_shape=(jax.ShapeDtypeStruct((T, ZA_W), BF16), jax.ShapeDtypeStruct((T, ZB_W), BF16),
                   jax.ShapeDtypeStruct((1, ZA_W), F32), jax.ShapeDtypeStruct((1, ZB_W), F32)),
        scratch_shapes=[_token_scratch(ROW_BLK, W)],
        compiler_params=pltpu.CompilerParams(dimension_semantics=("arbitrary",)),
    )(*args)


def _place():
    x, y, c = lax.axis_index("x"), lax.axis_index("y"), lax.axis_index("c")
    chips = [(1 - x, y), (x, 1 - y), (1 - x, 1 - y)]
    return x, y, c, chips


def _cast_place(shards, k_idx, *, name):
    n, steps = len(shards), 4

    def body(k_ref, *refs):
        for s_ref, o_ref in zip(refs[:n], refs[n:]):
            o_ref[...] = s_ref[...].astype(o_ref.dtype)

    blocks = [(a.shape[0] // steps, a.shape[1]) for a in shards]
    grid_spec = pltpu.PrefetchScalarGridSpec(
        num_scalar_prefetch=1, grid=(steps,),
        in_specs=[pl.BlockSpec(blk, lambda t, k_ref: (t, 0)) for blk in blocks],
        out_specs=tuple(pl.BlockSpec(blk, lambda t, k_ref: (k_ref[0] * steps + t, 0)) for blk in blocks))
    return pl.pallas_call(
        body, name=name, grid_spec=grid_spec,
        out_shape=tuple(jax.ShapeDtypeStruct((N_CHIPS * a.shape[0], a.shape[1]), BF16) for a in shards),
        compiler_params=pltpu.CompilerParams(dimension_semantics=("parallel",),
                                             vmem_limit_bytes=_vmem_limit(*[_nbytes(b, F32) for b in blocks])),
    )(k_idx, *shards)


def _gather_carry(fulls, part=(0, 1)):
    n = len(fulls)
    p_idx, n_parts = part

    def piece(ref, chip_idx, half):
        rs = ref.shape[0] // N_CHIPS
        rh = rs // 2
        rows = rh // n_parts
        return ref.at[pl.ds(chip_idx * rs + half * rh + p_idx * rows, rows)]

    def copy(sems, sem, src_ref, dst_ref, to):
        return pltpu.make_async_remote_copy(src_ref=src_ref, dst_ref=dst_ref, send_sem=sems[0].at[sem],
                                            recv_sem=sems[1].at[sem], device_id=to, device_id_type=MESH)

    def ici_copy(ins, outs, sems, i, j, chip, k, c):
        return copy(sems, 3 * i + j, piece(ins[i], k, c), piece(outs[i], k, c), (*chip, c))

    def start(ins, outs, sems):
        x, y, c, chips = _place()
        for i in range(n):
            for j, chip in enumerate(chips):
                ici_copy(ins, outs, sems, i, j, chip, 2 * x + y, c).start()

    def finish(ins, outs, sems):
        x, y, c, chips = _place()
        sibling = (x, y, 1 - c)
        passed = []
        for i in range(n):
            for j, chip in enumerate(chips):
                region = piece(outs[i], 2 * chip[0] + chip[1], c)
                copy(sems, 3 * i + j, region, region, (*chip, c)).wait_recv()
                cp = copy(sems, 3 * n + 3 * i + j, region, region, sibling)
                cp.start()
                passed.append(cp)
        for i in range(n):
            for j, chip in enumerate(chips):
                region = piece(outs[i], 2 * chip[0] + chip[1], 1 - c)
                copy(sems, 3 * n + 3 * i + j, region, region, sibling).wait_recv()
        for i in range(n):
            for j, chip in enumerate(chips):
                ici_copy(ins, outs, sems, i, j, chip, 2 * x + y, c).wait_send()
        for cp in passed:
            cp.wait_send()

    return _Carry(fulls, [jax.ShapeDtypeStruct(a.shape, a.dtype) for a in fulls], {i: i for i in range(n)},
                  [pltpu.SemaphoreType.DMA((6 * n,)), pltpu.SemaphoreType.DMA((6 * n,))], start, finish)


def _pair_swap_carry(grads):
    n = len(grads)

    def copy(ins, outs, sems, i):
        x, y, c, _ = _place()
        return pltpu.make_async_remote_copy(src_ref=ins[i].at[:, 1 - c], dst_ref=outs[i], send_sem=sems[0].at[i],
                                            recv_sem=sems[1].at[i], device_id=(x, y, 1 - c), device_id_type=MESH)

    def start(ins, outs, sems):
        for i in range(n):
            copy(ins, outs, sems, i).start()

    def finish(ins, outs, sems):
        for i in range(n):
            copy(ins, outs, sems, i).wait()

    return _Carry(grads, [jax.ShapeDtypeStruct((a.shape[0], a.shape[2], a.shape[3]), a.dtype) for a in grads], {},
                  [pltpu.SemaphoreType.DMA((n,)), pltpu.SemaphoreType.DMA((n,))], start, finish)


def _pair_sum(g4, got, c_idx, *, name):
    _, _, rh, cs = g4.shape
    rb = _row_block(rh, 512, 16)

    def body(c_ref, own_ref, got_ref, o_ref):
        o_ref[...] = (own_ref[...].astype(F32) + got_ref[...].astype(F32)).astype(o_ref.dtype)

    grid_spec = pltpu.PrefetchScalarGridSpec(
        num_scalar_prefetch=1, grid=(N_CHIPS, rh // rb),
        in_specs=[pl.BlockSpec((None, None, rb, cs), lambda k, t, c_ref: (k, c_ref[0], t, 0)),
                  pl.BlockSpec((None, rb, cs), lambda k, t, c_ref: (k, t, 0))],
        out_specs=pl.BlockSpec((None, rb, cs), lambda k, t, c_ref: (k, t, 0)))
    return pl.pallas_call(
        body, name=name, grid_spec=grid_spec, out_shape=jax.ShapeDtypeStruct((N_CHIPS, rh, cs), BF16),
        compiler_params=pltpu.CompilerParams(dimension_semantics=("parallel", "parallel")),
    )(c_idx, g4, got)


def _chip_scatter_carry(sums):
    n = len(sums)

    def copies(ins, outs, sems):
        x, y, c, chips = _place()
        return [pltpu.make_async_remote_copy(src_ref=ins[i].at[2 * chip[0] + chip[1]], dst_ref=outs[i].at[j],
                                             send_sem=sems[0].at[3 * i + j], recv_sem=sems[1].at[3 * i + j],
                                             device_id=(*chip, c), device_id_type=MESH)
                for i in range(n) for j, chip in enumerate(chips)]

    def start(ins, outs, sems):
        for cp in copies(ins, outs, sems):
            cp.start()

    def finish(ins, outs, sems):
        for cp in copies(ins, outs, sems):
            cp.wait()

    return _Carry(sums, [jax.ShapeDtypeStruct((3,) + a.shape[1:], a.dtype) for a in sums], {},
                  [pltpu.SemaphoreType.DMA((3 * n,)), pltpu.SemaphoreType.DMA((3 * n,))], start, finish)


def _chip_sum(sums, gots, kc_idx, *, name):
    n, steps = len(sums), 2

    def body(kc_ref, *refs):
        for own_ref, got_ref, o_ref in zip(refs[:n], refs[n:2 * n], refs[2 * n:]):
            acc = own_ref[...].astype(F32)
            for j in range(3):
                acc = acc + got_ref[j].astype(F32)
            o_ref[...] = acc

    blocks = [(a.shape[1] // steps, a.shape[2]) for a in sums]
    grid_spec = pltpu.PrefetchScalarGridSpec(
        num_scalar_prefetch=1, grid=(steps,),
        in_specs=[pl.BlockSpec((None,) + blk, lambda t, kc: (kc[0], t, 0)) for blk in blocks]
        + [pl.BlockSpec((3,) + blk, lambda t, kc: (0, t, 0)) for blk in blocks],
        out_specs=tuple(pl.BlockSpec(blk, lambda t, kc: (kc[1] * steps + t, 0)) for blk in blocks))
    return pl.pallas_call(
        body, name=name, grid_spec=grid_spec,
        out_shape=tuple(jax.ShapeDtypeStruct((2 * a.shape[1], a.shape[2]), F32) for a in sums),
        compiler_params=pltpu.CompilerParams(dimension_semantics=("parallel",),
                                             vmem_limit_bytes=_vmem_limit(*[3 * _nbytes(b, F32) for b in blocks])),
    )(kc_idx, *sums, *gots)


def _half_swap_carry(shards):
    n = len(shards)

    def copy(ins, outs, sems, i, to_my_half):
        x, y, c, _ = _place()
        rh = ins[i].shape[0] // 2
        half = c if to_my_half else 1 - c
        return pltpu.make_async_remote_copy(
            src_ref=ins[i].at[pl.ds(c * rh, rh)], dst_ref=outs[i].at[pl.ds(half * rh, rh)],
            send_sem=sems[0].at[i], recv_sem=sems[1].at[i], device_id=(x, y, 1 - c), device_id_type=MESH)

    def start(ins, outs, sems):
        for i in range(n):
            copy(ins, outs, sems, i, True).start()

    def finish(ins, outs, sems):
        for i in range(n):
            copy(ins, outs, sems, i, False).wait_recv()
        for i in range(n):
            copy(ins, outs, sems, i, True).wait_send()

    return _Carry(shards, [jax.ShapeDtypeStruct(a.shape, a.dtype) for a in shards], {i: i for i in range(n)},
                  [pltpu.SemaphoreType.DMA((n,)), pltpu.SemaphoreType.DMA((n,))], start, finish)


SMALL_ROW = 1024


def _small_layout(shapes):
    starts, row = [], 0
    for r, c in shapes:
        starts.append(row)
        row += -(-c // SMALL_ROW) if r == 1 else r
    return starts, -(-row // SUBLANES) * SUBLANES


def _small_pieces(shape, start):
    r, c = shape
    if r == 1:
        return [(start + q, min(SMALL_ROW, c - q * SMALL_ROW), q * SMALL_ROW) for q in range(-(-c // SMALL_ROW))]
    return None


def _whole_spec(shape):
    return pl.BlockSpec(tuple(shape), lambda i: (0,) * len(shape))


def _small_all_reduce(parts, *, name):
    shapes = [a.shape for a in parts]
    starts, rows = _small_layout(shapes)
    n = len(parts)
    n_tables = sum(1 for r, _ in shapes if r > 1)
    assert all(r > 1 and c <= LANES for r, c in shapes[:n_tables]) and all(r == 1 for r, _ in shapes[n_tables:])
    table_rows = starts[n_tables]
    assert table_rows % SUBLANES == 0

    def regions(slot):
        return slot.at[pl.ds(0, table_rows), pl.ds(0, LANES)], slot.at[pl.ds(table_rows, rows - table_rows)]

    def body(*refs):
        ins, outs = refs[0:n], refs[n:2 * n]
        buf, send_sems, recv_sems = refs[2 * n:]
        x, y, c, _ = _place()
        me = 4 * x + 2 * y + c
        mine = buf.at[me]
        mine[...] = jnp.zeros((rows, SMALL_ROW), F32)
        for ref, shape, start in zip(ins, shapes, starts):
            pieces = _small_pieces(shape, start)
            if pieces is None:
                mine[start:start + shape[0], 0:shape[1]] = ref[...]
            else:
                for row, width, col in pieces:
                    mine[row:row + 1, 0:width] = ref[:, col:col + width]
        def peer(d):
            return 1 - x if d & 4 else x, 1 - y if d & 2 else y, 1 - c if d & 1 else c

        def copy(d, part, dst_slot):
            sem = 2 * (d - 1) + part
            return pltpu.make_async_remote_copy(
                src_ref=regions(mine)[part], dst_ref=regions(dst_slot)[part], send_sem=send_sems.at[sem],
                recv_sem=recv_sems.at[sem], device_id=peer(d), device_id_type=MESH)

        for d in range(1, 8):
            for part in range(2):
                copy(d, part, mine).start()
        for d in range(1, 8):
            px, py, pc = peer(d)
            for part in range(2):
                copy(d, part, buf.at[4 * px + 2 * py + pc]).wait_recv()
        for d in range(1, 8):
            for part in range(2):
                copy(d, part, mine).wait_send()
        tables, vectors = regions(buf.at[0])
        acc_t, acc_v = tables[...], vectors[...]
        for s in range(1, 8):
            tables, vectors = regions(buf.at[s])
            acc_t, acc_v = acc_t + tables[...], acc_v + vectors[...]
        tables, vectors = regions(mine)
        tables[...] = acc_t
        vectors[...] = acc_v
        for ref, shape, start in zip(outs, shapes, starts):
            pieces = _small_pieces(shape, start)
            if pieces is None:
                ref[...] = mine[start:start + shape[0], 0:shape[1]]
            else:
                for row, width, col in pieces:
                    ref[:, col:col + width] = mine[row:row + 1, 0:width]

    whole = [_whole_spec(s) for s in shapes]
    return pl.pallas_call(
        body, name=name, grid=(1,), in_specs=whole, out_specs=tuple(whole),
        out_shape=tuple(jax.ShapeDtypeStruct(s, F32) for s in shapes),
        scratch_shapes=[pltpu.VMEM((8, rows, SMALL_ROW), F32), pltpu.SemaphoreType.DMA((14,)),
                        pltpu.SemaphoreType.DMA((14,))],
    )(*parts)


def _adamw_small(ws, gs, ms, vs, *, name):
    n = len(ws)
    c1 = 1.0 - ADAM_B1 ** ADAM_STEP
    c2 = 1.0 - ADAM_B2 ** ADAM_STEP

    def body(*refs):
        for k in range(n):
            w_ref, g_ref, m_ref, v_ref = (refs[t * n + k] for t in range(4))
            go_ref, d_ref, mo_ref, vo_ref = (refs[(4 + t) * n + k] for t in range(4))
            gv = g_ref[...]
            go_ref[...] = gv
            mn = ADAM_B1 * m_ref[...] + (1.0 - ADAM_B1) * gv
            vn = ADAM_B2 * v_ref[...] + (1.0 - ADAM_B2) * (gv * gv)
            d_ref[...] = -ADAM_LR * ((mn / c1) / (jnp.sqrt(vn / c2) + ADAM_EPS) + ADAM_WD * w_ref[...])
            mo_ref[...] = mn
            vo_ref[...] = vn

    whole = [_whole_spec(a.shape) for a in ws]
    shapes = tuple(jax.ShapeDtypeStruct(a.shape, F32) for a in ws)
    res = pl.pallas_call(
        body, name=name, grid=(1,), in_specs=whole * 4, out_specs=tuple(whole * 4), out_shape=shapes * 4,
    )(*ws, *gs, *ms, *vs)
    return res[0:n], res[n:2 * n], res[2 * n:3 * n], res[3 * n:4 * n]


def _adamw(ws, gs, ms, vs, *, name):
    n, steps = len(ws), 8
    c1 = 1.0 - ADAM_B1 ** ADAM_STEP
    c2 = 1.0 - ADAM_B2 ** ADAM_STEP

    def body(*refs):
        for k in range(n):
            w_ref, g_ref, m_ref, v_ref = (refs[t * n + k] for t in range(4))
            go_ref, d_ref, mo_ref, vo_ref = (refs[(4 + t) * n + k] for t in range(4))
            gv = g_ref[...]
            go_ref[...] = gv
            mn = ADAM_B1 * m_ref[...] + (1.0 - ADAM_B1) * gv
            vn = ADAM_B2 * v_ref[...] + (1.0 - ADAM_B2) * (gv * gv)
            d_ref[...] = -ADAM_LR * ((mn / c1) / (jnp.sqrt(vn / c2) + ADAM_EPS) + ADAM_WD * w_ref[...])
            mo_ref[...] = mn
            vo_ref[...] = vn

    blocks = [(a.shape[0] // steps, a.shape[1]) for a in ws]
    specs = [pl.BlockSpec(blk, lambda i: (i, 0)) for blk in blocks]
    shapes = tuple(jax.ShapeDtypeStruct(a.shape, F32) for a in ws)
    res = pl.pallas_call(
        body, name=name, grid=(steps,), in_specs=specs * 4, out_specs=tuple(specs * 4), out_shape=shapes * 4,
        compiler_params=pltpu.CompilerParams(dimension_semantics=("parallel",),
                                             vmem_limit_bytes=_vmem_limit(*[8 * _nbytes(b, F32) for b in blocks])),
    )(*ws, *gs, *ms, *vs)
    return res[0:n], res[n:2 * n], res[2 * n:3 * n], res[3 * n:4 * n]


def _local_step(x, p, target, small, sched):
    T = x.shape[0]
    W = MIX_W
    rel_bias = small["rel_bias"]
    b_in_a, b_in_b = small["b_in"][:, 0:ZA_W], small["b_in"][:, ZA_W:]

    idx_np = [_t5_index(A_WINDOW - 1, 1)] + [_t5_index(window // dil, dil) for window, dil in B_PATTERNS]
    idx_all = jnp.asarray(np.stack(idx_np))
    n1, bias_all = sched.run(_prologue, x, small["ffn1_pre_g"], rel_bias, idx_all, name="prologue")
    w_gu1 = sched.weight("ffn1_w_gu")
    *gu1, a1 = sched.run(_ffn_up, n1, w_gu1, bm=512, name="ffn1_gu")
    w_d1 = sched.weight("ffn1_w_down")
    f1, h1, n2 = sched.run(_mm_resid_norm, a1, w_d1, x, small["ffn1_post_g"], 0.5, small["attn_pre_g"], bm=512,
                           name="ffn1_down")
    win_a, win_b = sched.weight("w_in")
    za = _mm_nt(n2, win_a, bm=1024, bn=ZA_W, out_dtype=BF16, name="in_proj_a", add=b_in_a)
    dils = tuple(dil for _, dil in B_PATTERNS)
    zb_by_dil = _mm_nt_classes(n2, win_b, b_in_b, dils, bm=512, name="in_proj_b")

    a_args = dict(r=1, q_col=0, k_col=W // LANES, v_col=W // LANES + 1, stride=0, pattern=0, kv_shared=True)
    sink_row = jnp.repeat(small["sinks"], HEAD_DIM, axis=1)
    out_a, out_a_bf, lse_a = sched.run(_band_fwd, za, za, za, bias_all, name="band_a_fwd", sink=sink_row, **a_args)

    no_sink = jnp.full((1, W), NEG, F32)
    b_ctx, branches = [], []
    for t, dil in enumerate(dils):
        zb_r = zb_by_dil[t]
        args = dict(r=dil, q_col=0, k_col=1, v_col=2, stride=ZB_W // W, pattern=1 + t)
        branches.append(sched.run(_band_fwd, zb_r, zb_r, zb_r, bias_all, name=f"band_b{t}_fwd", **args))
        b_ctx.append((jnp.asarray(idx_np[1 + t]), zb_r, args))
    out_b, out_b_bf, lse_b = _merge(branches, dils, no_sink, name="merge_b")

    mix = jnp.concatenate([out_a_bf, out_b_bf], axis=1)
    w_out = sched.weight("w_out")
    att, h2, n3 = _mm_resid_norm(mix, w_out, h1, small["attn_post_g"], 1.0, small["ffn2_pre_g"], bm=512,
                                 name="out_proj", bias=small["b_out"])
    w_gu2, w_d2 = sched.weight("ffn2_w_gu"), sched.weight("ffn2_w_down")
    *gu2, a2 = _ffn_up(n3, w_gu2, bm=512, name="ffn2_gu")
    f2, h3, n4 = _mm_resid_norm(a2, w_d2, h2, small["ffn2_post_g"], 0.5, small["ple_pre_g"], bm=512,
                                name="ffn2_down")
    w_gate = sched.weight("w_ple_gate")
    p_bf = p.astype(BF16)
    dh4, de, dgp, loss, d_ple_post = _ple_head(n4, w_gate, p_bf, sched.weight("w_ple_proj"), h3, small["ple_post_g"],
                                               target, bm=512, name="ple_head")

    gs = {"ple_post_g": d_ple_post}
    sched.grad("w_ple_proj", _mm_tn(p_bf, de, bm=256, bn=1024, out_dtype=BF16, name="d_w_ple"))
    sched.grad("w_ple_gate", _mm_tn(n4, dgp, bm=512, bn=1024, out_dtype=BF16, name="d_w_gate"))
    dh3, df2, gs["ple_pre_g"], gs["ffn2_post_g"], _ = sched.run(
        _mm_nt_norms_bwd, dgp, w_gate, dh4, h3, small["ple_pre_g"], f2, small["ffn2_post_g"], 0.5, bm=512, name="d_n4")

    def ffn_bwd(df, a, gu, n, w_down, w_gu, tag, *norm_args):
        dgu = sched.run(_ffn_down_bwd, df, w_down, gu[0], gu[1], bm=512, name=f"ffn{tag}_d_a")
        sched.grad(f"ffn{tag}_w_gu", _mm_tn(n, dgu, bm=512, bn=1408, out_dtype=BF16, name=f"ffn{tag}_d_w_gu",
                                            n_stack=N_CHIPS))
        sched.grad(f"ffn{tag}_w_down", sched.run(_mm_tn, a, df, bm=256, bn=1024, out_dtype=BF16,
                                                name=f"ffn{tag}_d_w_down"))
        return sched.run(_mm_nt_norms_bwd, dgu, w_gu, *norm_args, bm=512, name=f"ffn{tag}_d_n")

    dh2, datt, gs["ffn2_pre_g"], gs["attn_post_g"], gs["b_out"] = ffn_bwd(
        df2, a2, gu2, n3, w_d2, w_gu2, "2", dh3, h2, small["ffn2_pre_g"], att, small["attn_post_g"], 1.0)
    sched.grad("w_out", _mm_tn(mix, datt, bm=512, bn=1024, out_dtype=BF16, name="d_w_out"))
    dmix = sched.run(_mm_nt, datt, w_out, bm=1024, bn=1024, out_dtype=F32, name="d_mix")

    do_a, stat_a, dsink, do_b, stat_b = _attn_delta(dmix, (out_a, out_b), (lse_a, lse_b), sink_row, dils,
                                                    name="attn_delta")
    gs["sinks"] = dsink[:, ::HEAD_DIM]
    dq_a, dk_a, dv_a, dbias_a = sched.run(_band_bwd, za, za, za, do_a, stat_a, bias_all, name="band_a_bwd", **a_args)
    d_rel_a = _bias_grad(dbias_a, jnp.asarray(idx_np[0]), name="d_rel_a")
    b_grads = []
    d_rel_b = None
    for t, (idx, zb_r, args) in enumerate(b_ctx):
        dq, dk, dv, dbias = sched.run(_band_bwd, zb_r, zb_r, zb_r, do_b[t], stat_b[t], bias_all,
                                      name=f"band_b{t}_bwd", **args)
        b_grads.append((dq, dk, dv))
        d_rel = _bias_grad(dbias, idx, name=f"d_rel_b{t}")
        d_rel_b = d_rel if d_rel_b is None else d_rel_b + d_rel
    gs["rel_bias"] = jnp.concatenate([d_rel_a[:, 0:2 * N_PAIRS], d_rel_b[:, 0:2 * N_PAIRS]], axis=1)
    dza, dzb, dsum_a, dsum_b = _dz_assemble(dq_a, dk_a, dv_a, b_grads, dils, name="d_z")
    gs["b_in"] = jnp.concatenate([dsum_a, dsum_b], axis=1)
    sched.grad("w_in", (_mm_tn(dza, n2, bm=ZA_W, bn=1024, out_dtype=BF16, name="d_w_in_a"),
                        _mm_tn(dzb, n2, bm=ZB_W // 2, bn=1024, out_dtype=BF16, name="d_w_in_b")))
    dn2_a = _mm_nn(dza, win_a, bm=1024, bn=1024, out_dtype=F32, name="d_n2_a")
    dh1, df1, gs["attn_pre_g"], gs["ffn1_post_g"], _ = sched.run(
        _mm_nt_norms_bwd, dzb, win_b, dh2, h1, small["attn_pre_g"], f1, small["ffn1_post_g"], 0.5, bm=512,
        name="d_n2_b", add=dn2_a, b_is_kn=True)
    grad_x, gs["ffn1_pre_g"] = ffn_bwd(df1, a1, gu1, n1, w_d1, w_gu1, "1", dh1, x, small["ffn1_pre_g"], None, None, 0.0)
    return loss, grad_x, gs


def _to_compute(name, full):
    st = full.reshape(N_CHIPS, full.shape[0] // N_CHIPS, full.shape[1])
    if name in ("ffn1_w_gu", "ffn2_w_gu"):
        return st
    if name == "w_ple_proj":
        return jnp.transpose(st, (1, 0, 2)).reshape(st.shape[1], -1)
    if name == "w_in":
        return full[0:ZA_W], full[ZA_W:]
    return full


def _to_comm(name, g):
    if name == "w_in":
        g = jnp.concatenate(g, axis=0)
    if name == "w_ple_proj":
        g = jnp.transpose(g.reshape(g.shape[0], N_CHIPS, -1), (1, 0, 2))
    rs = g.shape[1] if g.ndim == 3 else g.shape[0] // N_CHIPS
    return g.reshape(N_CHIPS, 2, rs // 2, g.shape[-1])


class _Schedule:
    GATHER = {
        "prologue": [("ffn1_w_gu", (0, 1))],
        "ffn1_gu": [("ffn1_w_down", (0, 1))],
        "ffn1_down": [("w_in", (0, 1))],
        "band_a_fwd": [("ffn2_w_gu", (0, 2))],
        "band_b0_fwd": [("ffn2_w_gu", (1, 2))],
        "band_b1_fwd": [("ffn2_w_down", (0, 1))],
        "band_b2_fwd": [("w_out", (0, 1)), ("w_ple_gate", (0, 1)), ("w_ple_proj", (0, 1))],
    }
    SWAP = {"d_n4": ["w_ple_proj", "w_ple_gate"], "ffn2_d_w_down": ["ffn2_w_gu"], "ffn2_d_n": ["ffn2_w_down"],
            "d_mix": ["w_out"], "d_n2_b": ["w_in"], "ffn1_d_w_down": ["ffn1_w_gu"], "ffn1_d_n": ["ffn1_w_down"]}
    SCATTER = {"ffn2_d_a": ["w_ple_proj", "w_ple_gate"], "ffn2_d_n": ["ffn2_w_gu"], "band_a_bwd": ["ffn2_w_down"],
               "band_b0_bwd": ["w_out"], "ffn1_d_a": ["w_in"], "ffn1_d_n": ["ffn1_w_gu"]}

    def __init__(self, placed, c_idx, kc_idx):
        self.full = dict(placed)
        self.missing = {n: 1.0 for n in placed}
        self.c_idx, self.kc_idx = c_idx, kc_idx
        self.grads, self.swapped, self.pair_sums, self.scattered = {}, {}, {}, {}

    def _gather(self, entries):
        carries, after = [], []
        for part in sorted({pt for _, pt in entries}):
            names = [n for n, pt in entries if pt == part]
            carry = _gather_carry([self.full[n] for n in names], part)
            carries.append(carry)

            def done(carry=carry, names=names, part=part):
                for n, a in zip(names, carry.results):
                    self.full[n] = a
                    self.missing[n] -= 1.0 / part[1]
            after.append(done)
        return carries, after

    def weight(self, name):
        assert abs(self.missing[name]) < 1e-9, (name, self.missing[name])
        return _to_compute(name, self.full[name])

    def grad(self, name, g):
        self.grads[name] = _to_comm(name, g)

    def _swap(self, names):
        carry = _pair_swap_carry([self.grads[n] for n in names])

        def done():
            self.swapped.update(zip(names, carry.results))
        return carry, done

    def _scatter(self, names):
        for n in names:
            self.pair_sums[n] = _pair_sum(self.grads[n], self.swapped[n], self.c_idx, name=f"grad_pair_sum_{n}")
        carry = _chip_scatter_carry([self.pair_sums[n] for n in names])

        def done():
            self.scattered.update(zip(names, carry.results))
        return carry, done

    def run(self, fn, *args, name, **kw):
        carries, after = self._gather(self.GATHER.get(name, []))
        for names, make in ((self.SWAP.get(name), self._swap), (self.SCATTER.get(name), self._scatter)):
            if names:
                carry, done = make(names)
                carries.append(carry)
                after.append(done)
        out = fn(*args, name=name, carry=_join(carries), **kw)
        for done in after:
            done()
        return out

    def finish(self):
        left = [n for n in BIG if n not in self.scattered]
        to_swap = [n for n in left if n not in self.swapped]
        if to_swap:
            carry, done = self._swap(to_swap)
            _comm_call(carry, name="grad_pair_swap")
            done()
        if left:
            carry, done = self._scatter(left)
            _comm_call(carry, name="grad_chip_scatter")
            done()
        halves = _chip_sum([self.pair_sums[n] for n in BIG], [self.scattered[n] for n in BIG], self.kc_idx,
                           name="grad_chip_sum")
        carry = _half_swap_carry(halves)
        _comm_call(carry, name="grad_half_swap")
        return dict(zip(BIG, carry.results))


def kernel(x, p, rel_bias, ffn1_pre_g, ffn1_w_gu, ffn1_w_down, ffn1_post_g, attn_pre_g, w_in, b_in, sinks, w_out, b_out, attn_post_g, ffn2_pre_g, ffn2_w_gu, ffn2_w_down, ffn2_post_g, ple_pre_g, w_ple_gate, w_ple_proj, ple_post_g, loss_target, m_rel_bias, m_ffn1_pre_g, m_ffn1_w_gu, m_ffn1_w_down, m_ffn1_post_g, m_attn_pre_g, m_w_in, m_b_in, m_sinks, m_w_out, m_b_out, m_attn_post_g, m_ffn2_pre_g, m_ffn2_w_gu, m_ffn2_w_down, m_ffn2_post_g, m_ple_pre_g, m_w_ple_gate, m_w_ple_proj, m_ple_post_g, v_rel_bias, v_ffn1_pre_g, v_ffn1_w_gu, v_ffn1_w_down, v_ffn1_post_g, v_attn_pre_g, v_w_in, v_b_in, v_sinks, v_w_out, v_b_out, v_attn_post_g, v_ffn2_pre_g, v_ffn2_w_gu, v_ffn2_w_down, v_ffn2_post_g, v_ple_pre_g, v_w_ple_gate, v_w_ple_proj, v_ple_post_g):
    given = dict(locals())
    w = {n: given[n] for n in WEIGHTS}
    m = {n: given["m_" + n] for n in WEIGHTS}
    v = {n: given["v_" + n] for n in WEIGHTS}
    c_pos = lax.axis_index("c").astype(jnp.int32)
    k_pos = (2 * lax.axis_index("x") + lax.axis_index("y")).astype(jnp.int32)
    c_idx, k_idx, kc_idx = c_pos.reshape(1), k_pos.reshape(1), jnp.stack([k_pos, c_pos])

    def shard(a, n):
        return jnp.transpose(a[0]) if n == "w_in" else a[0]

    def unshard(a, n):
        return (jnp.transpose(a) if n == "w_in" else a)[None]

    placed = _cast_place([shard(w[n], n) for n in BIG], k_idx, name="place_shards")
    sched = _Schedule(dict(zip(BIG, placed)), c_idx, kc_idx)
    small = {n: (w[n] if n == "rel_bias" else w[n].reshape(1, -1)) for n in SMALL}

    loss_part, grad_x, gs = _local_step(x[0], p[0, 0], loss_target[0], small, sched)

    grads_big = sched.finish()

    *small_grads, loss_row = _small_all_reduce([gs[n] for n in SMALL] + [loss_part], name="small_all_reduce")
    loss = loss_row[0, 0]

    grads, delta, new_m, new_v = {}, {}, {}, {}
    res = _adamw([shard(w[n], n) for n in BIG], [grads_big[n] for n in BIG], [shard(m[n], n) for n in BIG],
                 [shard(v[n], n) for n in BIG], name="adamw_big")
    for n, gg, dd, mm, vv in zip(BIG, *res):
        grads[n], delta[n], new_m[n], new_v[n] = (unshard(a, n) for a in (gg, dd, mm, vv))
    res = _adamw_small([w[n] for n in SMALL], small_grads, [m[n] for n in SMALL], [v[n] for n in SMALL],
                       name="adamw_small")
    for name, gg, dd, mm, vv in zip(SMALL, *res):
        grads[name], delta[name], new_m[name], new_v[name] = gg, dd, mm, vv

    return (loss, grad_x[None], *[grads[n] for n in WEIGHTS], *[delta[n] for n in WEIGHTS],
            *[new_m[n] for n in WEIGHTS], *[new_v[n] for n in WEIGHTS])
```

```python
import math

import jax
import jax.numpy as jnp
import numpy as np
from jax import lax
from jax.experimental import pallas as pl
from jax.experimental.pallas import tpu as pltpu

F32 = jnp.float32
BF16 = jnp.bfloat16
MESH = pl.DeviceIdType.MESH

EPS = 1e-6
NEG = -1e30
LANES = 128
SUBLANES = 8
HEAD_DIM = 64
QBLK = 128
N_PAIRS = 4
MIX_W = N_PAIRS * LANES
A_WINDOW = 128
B_PATTERNS = ((128, 1), (512, 4), (2048, 16))
NUM_BUCKETS = 32
MAX_DISTANCE = 2048
ZA_W = 768
ZB_W = 1536
N_CHIPS = 4
VMEM_BYTES_V7X = 64 * 2**20

ADAM_LR, ADAM_B1, ADAM_B2, ADAM_EPS, ADAM_WD, ADAM_STEP = 0.001, 0.9, 0.999, 1e-08, 0.01, 10

BIG = ("ffn1_w_gu", "ffn1_w_down", "w_in", "w_out", "ffn2_w_gu", "ffn2_w_down", "w_ple_gate", "w_ple_proj")
SMALL = ("rel_bias", "ffn1_pre_g", "ffn1_post_g", "attn_pre_g", "b_in", "sinks", "b_out", "attn_post_g",
         "ffn2_pre_g", "ffn2_post_g", "ple_pre_g", "ple_post_g")
WEIGHTS = ("rel_bias", "ffn1_pre_g", "ffn1_w_gu", "ffn1_w_down", "ffn1_post_g", "attn_pre_g", "w_in", "b_in",
           "sinks", "w_out", "b_out", "attn_post_g", "ffn2_pre_g", "ffn2_w_gu", "ffn2_w_down", "ffn2_post_g",
           "ple_pre_g", "w_ple_gate", "w_ple_proj", "ple_post_g")


def _vmem_limit(*block_bytes):
    need = 2 * sum(block_bytes) + max(block_bytes) * 2 + (4 << 20)
    return int(min(max(need, 32 << 20), VMEM_BYTES_V7X - (6 << 20)))


def _nbytes(shape, dtype):
    return int(np.prod(shape)) * jnp.dtype(dtype).itemsize


MXU_WIDTH_V7X = 256


def _col_chunks(n):
    return [slice(c, min(c + MXU_WIDTH_V7X, n)) for c in range(0, n, MXU_WIDTH_V7X)]


def _row_halves(rows):
    return [slice(0, rows // 2), slice(rows // 2, rows)]


def _row_block(rows, cap, mult):
    for cand in range(min(rows, cap), 0, -1):
        if rows % cand == 0 and cand % mult == 0:
            return cand
    raise ValueError((rows, cap, mult))


class _Carry:
    def __init__(self, operands, out_shapes, aliases, sems, start, finish):
        self.operands, self.out_shapes, self.aliases, self.sems = list(operands), list(out_shapes), dict(aliases), list(sems)
        self.start, self.finish = start, finish
        self.results = None


def _join(carries):
    carries = [c for c in carries if c is not None]
    if not carries:
        return None
    if len(carries) == 1:
        return carries[0]
    offs, pos = [], [0, 0, 0]
    for c in carries:
        offs.append(tuple(pos))
        pos = [pos[0] + len(c.operands), pos[1] + len(c.out_shapes), pos[2] + len(c.sems)]
    aliases = {}
    for c, (oi, oo, _) in zip(carries, offs):
        aliases.update({oi + i: oo + o for i, o in c.aliases.items()})

    def run(which):
        def fn(ins, outs, sems):
            for c, (oi, oo, os_) in zip(carries, offs):
                getattr(c, which)(ins[oi:oi + len(c.operands)], outs[oo:oo + len(c.out_shapes)],
                                  sems[os_:os_ + len(c.sems)])
        return fn

    joined = _Carry([a for c in carries for a in c.operands], [s for c in carries for s in c.out_shapes], aliases,
                    [s for c in carries for s in c.sems], run("start"), run("finish"))
    joined.parts = (carries, offs)
    return joined


def _set_results(carry, outs):
    carry.results = list(outs)
    if hasattr(carry, "parts"):
        for c, (_, oo, _) in zip(*carry.parts):
            _set_results(c, outs[oo:oo + len(c.out_shapes)])


ANY = pl.BlockSpec(memory_space=pl.ANY)


def _call(body, *, name, grid, in_specs, out_specs, out_shape, args, scratch=(), vmem_limit=None, carry=None,
          semantics=None):
    n_ci, n_co, n_cs = len(args), len(out_shape), len(scratch)
    semantics = semantics or ("parallel",) * len(grid)
    if carry is None:
        res = pl.pallas_call(
            body, name=name, grid=grid, in_specs=list(in_specs), out_specs=tuple(out_specs), out_shape=tuple(out_shape),
            scratch_shapes=list(scratch),
            compiler_params=pltpu.CompilerParams(dimension_semantics=semantics, vmem_limit_bytes=vmem_limit),
        )(*args)
        return list(res)
    n_pi, n_po = len(carry.operands), len(carry.out_shapes)

    def full_body(*refs):
        ci, pi = refs[:n_ci], refs[n_ci:n_ci + n_pi]
        o0 = n_ci + n_pi
        co, po = refs[o0:o0 + n_co], refs[o0 + n_co:o0 + n_co + n_po]
        s0 = o0 + n_co + n_po
        cs, ps = refs[s0:s0 + n_cs], refs[s0 + n_cs:]
        ids = [pl.program_id(ax) for ax in range(len(grid))]
        first, last = ids[0] == 0, ids[0] == grid[0] - 1
        for ax in range(1, len(grid)):
            first, last = first & (ids[ax] == 0), last & (ids[ax] == grid[ax] - 1)

        @pl.when(first)
        def _():
            carry.start(pi, po, ps)

        body(*ci, *co, *cs)

        @pl.when(last)
        def _():
            carry.finish(pi, po, ps)

    res = pl.pallas_call(
        full_body, name=name, grid=grid, in_specs=list(in_specs) + [ANY] * n_pi,
        out_specs=tuple(out_specs) + tuple([ANY] * n_po), out_shape=tuple(out_shape) + tuple(carry.out_shapes),
        scratch_shapes=list(scratch) + carry.sems,
        input_output_aliases={n_ci + i: n_co + o for i, o in carry.aliases.items()},
        compiler_params=pltpu.CompilerParams(dimension_semantics=("arbitrary",) * len(grid),
                                             vmem_limit_bytes=vmem_limit),
    )(*args, *carry.operands)
    _set_results(carry, res[n_co:])
    return list(res[:n_co])


def _comm_call(carry, *, name):
    n_pi, n_po = len(carry.operands), len(carry.out_shapes)

    def body(*refs):
        pi, po, ps = refs[:n_pi], refs[n_pi:n_pi + n_po], refs[n_pi + n_po:]
        carry.start(pi, po, ps)
        carry.finish(pi, po, ps)

    res = pl.pallas_call(
        body, name=name, in_specs=[ANY] * n_pi, out_specs=tuple([ANY] * n_po), out_shape=tuple(carry.out_shapes),
        scratch_shapes=carry.sems, input_output_aliases=dict(carry.aliases),
    )(*carry.operands)
    _set_results(carry, res)


def _mm_nn(a, b, *, bm, bn, out_dtype, name, bias=None, carry=None):
    M, K = a.shape
    stacked = b.ndim == 3
    N = b.shape[0] * b.shape[2] if stacked else b.shape[1]
    assert M % bm == 0 and N % bn == 0 and (not stacked or bn == b.shape[2])

    def body(*refs):
        a_ref, b_ref = refs[0], refs[1]
        o_ref = refs[-1]
        acc = jnp.dot(a_ref[...], b_ref[...], preferred_element_type=F32)
        if bias is not None:
            acc = acc + refs[2][...]
        o_ref[...] = acc.astype(o_ref.dtype)

    if stacked:
        b_spec = pl.BlockSpec((None, K, bn), lambda i, j: (j, 0, 0))
    else:
        b_spec = pl.BlockSpec((K, bn), lambda i, j: (0, j))
    in_specs = [pl.BlockSpec((bm, K), lambda i, j: (i, 0)), b_spec]
    args = [a, b]
    if bias is not None:
        in_specs.append(pl.BlockSpec((1, bn), lambda i, j: (0, j)))
        args.append(bias)
    limit = _vmem_limit(_nbytes((bm, K), a.dtype), _nbytes((K, bn), b.dtype), _nbytes((bm, bn), F32))
    return _call(body, name=name, grid=(M // bm, N // bn), in_specs=in_specs,
                 out_specs=[pl.BlockSpec((bm, bn), lambda i, j: (i, j))],
                 out_shape=[jax.ShapeDtypeStruct((M, N), out_dtype)], args=args, vmem_limit=limit, carry=carry)[0]


def _mm_nt(a, b, *, bm, bn, out_dtype, name, add=None, carry=None):
    M, K = a.shape
    stacked = b.ndim == 3
    N = b.shape[1] if stacked else b.shape[0]
    n_sh = b.shape[0] if stacked else 1
    ks = b.shape[2] if stacked else K
    assert M % bm == 0 and N % bn == 0 and n_sh * ks == K
    dn = (((1,), (1,)), ((), ()))

    def body(*refs):
        a_ref, b_ref = refs[0], refs[1]
        o_ref = refs[-1]
        if stacked:
            acc = lax.dot_general(a_ref[:, 0:ks], b_ref[0], dn, preferred_element_type=F32)
            for s in range(1, n_sh):
                acc = acc + lax.dot_general(a_ref[:, s * ks:(s + 1) * ks], b_ref[s], dn, preferred_element_type=F32)
        else:
            acc = lax.dot_general(a_ref[...], b_ref[...], dn, preferred_element_type=F32)
        if add is not None:
            acc = acc + refs[2][...]
        o_ref[...] = acc.astype(o_ref.dtype)

    if stacked:
        b_spec = pl.BlockSpec((n_sh, bn, ks), lambda i, j: (0, j, 0))
    else:
        b_spec = pl.BlockSpec((bn, K), lambda i, j: (j, 0))
    in_specs = [pl.BlockSpec((bm, K), lambda i, j: (i, 0)), b_spec]
    args = [a, b]
    if add is not None:
        rows = bm if add.shape[0] == M else 1
        in_specs.append(pl.BlockSpec((rows, bn), lambda i, j: (i if rows == bm else 0, j)))
        args.append(add)
    limit = _vmem_limit(_nbytes((bm, K), a.dtype), _nbytes((bn, K), b.dtype), _nbytes((bm, bn), F32))
    return _call(body, name=name, grid=(M // bm, N // bn), in_specs=in_specs,
                 out_specs=[pl.BlockSpec((bm, bn), lambda i, j: (i, j))],
                 out_shape=[jax.ShapeDtypeStruct((M, N), out_dtype)], args=args, vmem_limit=limit, carry=carry)[0]


def _mm_tn(a, b, *, bm, bn, out_dtype, name, n_stack=None, carry=None):
    T, M = a.shape
    N = b.shape[1]
    assert M % bm == 0 and N % bn == 0 and (n_stack is None or bn * n_stack == N)
    dn = (((0,), (0,)), ((), ()))

    def body(a_ref, b_ref, o_ref):
        acc = lax.dot_general(a_ref[...], b_ref[...], dn, preferred_element_type=F32)
        o_ref[...] = acc.astype(o_ref.dtype)

    if n_stack is None:
        out_spec = pl.BlockSpec((bm, bn), lambda i, j: (i, j))
        out_shape = jax.ShapeDtypeStruct((M, N), out_dtype)
    else:
        out_spec = pl.BlockSpec((None, bm, bn), lambda i, j: (j, i, 0))
        out_shape = jax.ShapeDtypeStruct((n_stack, M, bn), out_dtype)
    limit = _vmem_limit(_nbytes((T, bm), a.dtype), _nbytes((T, bn), b.dtype), _nbytes((bm, bn), F32))
    return _call(body, name=name, grid=(M // bm, N // bn),
                 in_specs=[pl.BlockSpec((T, bm), lambda i, j: (0, i)), pl.BlockSpec((T, bn), lambda i, j: (0, j))],
                 out_specs=[out_spec], out_shape=[out_shape], args=[a, b], vmem_limit=limit, carry=carry)[0]


ROW_BLK = 256


def _rstd(x):
    return lax.rsqrt(jnp.mean(x * x, axis=-1, keepdims=True) + EPS)


def _norm_bwd(xhat, rstd, dxhat):
    return rstd * (dxhat - xhat * jnp.mean(dxhat * xhat, axis=-1, keepdims=True))


def _row_spec(cols):
    return pl.BlockSpec((ROW_BLK, cols), lambda i: (i, 0))


def _vec_spec(cols):
    return pl.BlockSpec((1, cols), lambda i: (0, 0))


def _prologue(x, g, rel_bias, idx_all, *, name, carry=None):
    T, D = x.shape
    n_pat = idx_all.shape[0]
    heads = 2 * N_PAIRS
    steps = n_pat * heads
    rows = T // steps

    def body(rb_ref, x_ref, g_ref, idx_ref, n_ref, bias_ref):
        s = pl.program_id(0)
        head = jnp.where(s < heads, s, heads + s % heads)
        xv = x_ref[...]
        n_ref[...] = (xv * _rstd(xv) * g_ref[...]).astype(n_ref.dtype)
        idxv = idx_ref[...]
        acc = jnp.where(idxv < 0, NEG, 0.0).astype(F32)
        for b in range(NUM_BUCKETS):
            acc = acc + jnp.where(idxv == b, rb_ref[b, head], 0.0)
        bias_ref[0] = acc
        kap = lax.broadcasted_iota(jnp.int32, (1, 2 * QBLK), 1)
        bias_ref[1] = jnp.where(kap < QBLK, NEG, acc)

    return _call(
        body, name=name, grid=(steps,),
        in_specs=[pl.BlockSpec(memory_space=pltpu.SMEM), pl.BlockSpec((rows, D), lambda s: (s, 0)),
                  pl.BlockSpec((1, D), lambda s: (0, 0)),
                  pl.BlockSpec((None, QBLK, 2 * QBLK), lambda s: (s // heads, 0, 0))],
        out_specs=[pl.BlockSpec((rows, D), lambda s: (s, 0)),
                   pl.BlockSpec((None, 2, None, QBLK, 2 * QBLK), lambda s: (s // heads, 0, s % heads, 0, 0))],
        out_shape=[jax.ShapeDtypeStruct((T, D), BF16),
                   jax.ShapeDtypeStruct((n_pat, 2, heads, QBLK, 2 * QBLK), F32)],
        args=[rel_bias, x, g, idx_all], carry=carry)


def _mm_resid_norm(a, b, h, g_post, coef, g_next, *, bm, name, bias=None, carry=None):
    M, K = a.shape
    N = b.shape[1]

    def body(*refs):
        a_ref, b_ref, h_ref, gp_ref, gn_ref = refs[0:5]
        f_ref, hn_ref, n_ref = refs[-3:]
        for rows in _row_halves(bm):
            f = jnp.dot(a_ref[rows, :], b_ref[...], preferred_element_type=F32)
            if bias is not None:
                f = f + refs[5][...]
            f_ref[rows, :] = f
            hn = h_ref[rows, :] + coef * (f * _rstd(f) * gp_ref[...])
            hn_ref[rows, :] = hn
            n_ref[rows, :] = (hn * _rstd(hn) * gn_ref[...]).astype(n_ref.dtype)

    row = lambda w: pl.BlockSpec((bm, w), lambda i: (i, 0))
    vec = pl.BlockSpec((1, N), lambda i: (0, 0))
    in_specs = [row(K), pl.BlockSpec((K, N), lambda i: (0, 0), pipeline_mode=pl.Buffered(1)), row(N), vec, vec]
    args = [a, b, h, g_post, g_next]
    if bias is not None:
        in_specs.append(vec)
        args.append(bias)
    limit = _vmem_limit(_nbytes((bm, K), a.dtype), _nbytes((K, N), b.dtype) // 2, 4 * _nbytes((bm, N), F32))
    return _call(body, name=name, grid=(M // bm,), in_specs=in_specs, out_specs=[row(N), row(N), row(N)],
                 out_shape=[jax.ShapeDtypeStruct((M, N), F32), jax.ShapeDtypeStruct((M, N), F32),
                            jax.ShapeDtypeStruct((M, N), BF16)], args=args, vmem_limit=limit, carry=carry)


def _mm_nt_norms_bwd(a, b, dh_in, h, g_pre, f, g_post, coef, *, bm, name, add=None, b_is_kn=False, carry=None):
    M, K = a.shape
    stacked = b.ndim == 3
    N = b.shape[1] if (stacked or b_is_kn) else b.shape[0]
    n_sh = b.shape[0] if stacked else 1
    ks = b.shape[2] if stacked else K
    assert n_sh * ks == K
    dn_dims = (((1,), (0,)), ((), ())) if b_is_kn else (((1,), (1,)), ((), ()))
    with_f = f is not None
    n_in = 5 + (2 if with_f else 0) + (1 if add is not None else 0)

    def body(*refs):
        a_ref, b_ref, dhi_ref, h_ref, gpre_ref = refs[0:5]
        outs = refs[n_in:]

        @pl.when(pl.program_id(0) == 0)
        def _():
            for acc in (outs[2:] if with_f else outs[1:]):
                acc[...] = jnp.zeros_like(acc)

        for rows in _row_halves(bm):
            if stacked:
                dn = lax.dot_general(a_ref[rows, 0:ks], b_ref[0], dn_dims, preferred_element_type=F32)
                for s in range(1, n_sh):
                    dn = dn + lax.dot_general(a_ref[rows, s * ks:(s + 1) * ks], b_ref[s], dn_dims,
                                              preferred_element_type=F32)
            else:
                dn = lax.dot_general(a_ref[rows, :], b_ref[...], dn_dims, preferred_element_type=F32)
            if add is not None:
                dn = dn + refs[n_in - 1][rows, :]
            hv = h_ref[rows, :]
            r = _rstd(hv)
            hh = hv * r
            dh = dhi_ref[rows, :] + _norm_bwd(hh, r, dn * gpre_ref[...])
            outs[0][rows, :] = dh
            if not with_f:
                outs[1][...] += jnp.sum(dn * hh, axis=0, keepdims=True)
                continue
            f_ref, gpost_ref = refs[5], refs[6]
            df_ref, dgpre_ref, dgpost_ref, dsum_ref = outs[1:]
            dgpre_ref[...] += jnp.sum(dn * hh, axis=0, keepdims=True)
            fv = f_ref[rows, :]
            rf = _rstd(fv)
            fh = fv * rf
            dy = coef * dh
            dgpost_ref[...] += jnp.sum(dy * fh, axis=0, keepdims=True)
            df = _norm_bwd(fh, rf, dy * gpost_ref[...])
            dsum_ref[...] += jnp.sum(df, axis=0, keepdims=True)
            df_ref[rows, :] = df.astype(df_ref.dtype)

    row = lambda w: pl.BlockSpec((bm, w), lambda i: (i, 0))
    vec = pl.BlockSpec((1, N), lambda i: (0, 0))
    once = pl.Buffered(1)
    if stacked:
        b_spec = pl.BlockSpec((n_sh, N, ks), lambda i: (0, 0, 0), pipeline_mode=once)
    else:
        b_spec = pl.BlockSpec(b.shape, lambda i: (0, 0), pipeline_mode=once)
    in_specs = [row(K), b_spec, row(N), row(N), vec]
    args = [a, b, dh_in, h, g_pre]
    if with_f:
        in_specs += [row(N), vec]
        args += [f, g_post]
    if add is not None:
        in_specs.append(row(N))
        args.append(add)
    vec_shape = jax.ShapeDtypeStruct((1, N), F32)
    out_specs = [row(N)] + ([row(N), vec, vec, vec] if with_f else [vec])
    out_shape = [jax.ShapeDtypeStruct((M, N), F32)]
    out_shape += [jax.ShapeDtypeStruct((M, N), BF16), vec_shape, vec_shape, vec_shape] if with_f else [vec_shape]
    limit = _vmem_limit(_nbytes((bm, K), a.dtype), _nbytes((N, K), b.dtype) // 2, 6 * _nbytes((bm, N), F32))
    return _call(body, name=name, grid=(M // bm,), in_specs=in_specs, out_specs=out_specs, out_shape=out_shape,
                 args=args, vmem_limit=limit, semantics=("arbitrary",), carry=carry)


def _ffn_up(n, w_gu, *, bm, name, carry=None):
    M, K = n.shape
    n_sh, _, ns = w_gu.shape
    half = n_sh // 2

    def body(n_ref, wg_ref, wu_ref, g_ref, u_ref, a_ref):
        nv = n_ref[...]
        for cols in _col_chunks(ns):
            g = jnp.dot(nv, wg_ref[:, cols], preferred_element_type=F32)
            u = jnp.dot(nv, wu_ref[:, cols], preferred_element_type=F32)
            g_ref[:, cols] = g.astype(g_ref.dtype)
            u_ref[:, cols] = u.astype(u_ref.dtype)
            a_ref[:, cols] = (g * jax.nn.sigmoid(g) * u).astype(a_ref.dtype)

    out_spec = pl.BlockSpec((bm, ns), lambda i, j: (i, j))
    out = jax.ShapeDtypeStruct((M, half * ns), BF16)
    limit = _vmem_limit(_nbytes((bm, K), n.dtype), 2 * _nbytes((K, ns), w_gu.dtype), 3 * _nbytes((bm, ns), F32))
    return _call(body, name=name, grid=(M // bm, half),
                 in_specs=[pl.BlockSpec((bm, K), lambda i, j: (i, 0)),
                           pl.BlockSpec((None, K, ns), lambda i, j: (j, 0, 0)),
                           pl.BlockSpec((None, K, ns), lambda i, j: (j + half, 0, 0))],
                 out_specs=[out_spec, out_spec, out_spec], out_shape=[out, out, out], args=[n, w_gu, w_gu],
                 vmem_limit=limit, carry=carry)


def _ffn_down_bwd(df, w_down, g, u, *, bm, name, carry=None):
    M, D = df.shape
    F = w_down.shape[0]
    dn = (((1,), (1,)), ((), ()))

    def body(df_ref, w_ref, g_ref, u_ref, o_ref):
        dfv = df_ref[...]
        for cols in _col_chunks(F):
            da = lax.dot_general(dfv, w_ref[cols, :], dn, preferred_element_type=F32)
            gv = g_ref[:, cols].astype(F32)
            s = jax.nn.sigmoid(gv)
            o_ref[:, cols] = (da * u_ref[:, cols].astype(F32) * (s * (1.0 + gv * (1.0 - s)))).astype(o_ref.dtype)
            o_ref[:, F + cols.start:F + cols.stop] = (da * (gv * s)).astype(o_ref.dtype)

    row = lambda w: pl.BlockSpec((bm, w), lambda i: (i, 0))
    limit = _vmem_limit(_nbytes((F, D), w_down.dtype) // 2, 2 * _nbytes((bm, F), BF16), _nbytes((bm, 2 * F), BF16))
    return _call(body, name=name, grid=(M // bm,),
                 in_specs=[row(D), pl.BlockSpec((F, D), lambda i: (0, 0), pipeline_mode=pl.Buffered(1)), row(F), row(F)],
                 out_specs=[row(2 * F)], out_shape=[jax.ShapeDtypeStruct((M, 2 * F), BF16)],
                 args=[df, w_down, g, u], vmem_limit=limit, carry=carry)[0]


def _ple_head(n4, w_gate, p, w_ple, h3, g_post, target, *, bm, name):
    T, D = h3.shape
    kp = p.shape[1]

    def body(n_ref, wg_ref, p_ref, wp_ref, h_ref, g_ref, t_ref, dh_ref, de_ref, dgp_ref, loss_ref, dg_ref):
        @pl.when(pl.program_id(0) == 0)
        def _():
            loss_ref[...] = jnp.zeros_like(loss_ref)
            dg_ref[...] = jnp.zeros_like(dg_ref)

        gpost = g_ref[...]
        for rows in _row_halves(bm):
            gate = jax.nn.sigmoid(jnp.dot(n_ref[rows, :], wg_ref[...], preferred_element_type=F32))
            ev = jnp.dot(p_ref[rows, :], wp_ref[...], preferred_element_type=F32)
            ge = gate * ev
            r = _rstd(ge)
            gh = ge * r
            diff = h_ref[rows, :] + gh * gpost - t_ref[rows, :]
            loss_ref[...] += jnp.sum(diff * diff) * (0.5 / D)
            dh = diff * (1.0 / D)
            dh_ref[rows, :] = dh
            dg_ref[...] += jnp.sum(dh * gh, axis=0, keepdims=True)
            dge = _norm_bwd(gh, r, dh * gpost)
            de_ref[rows, :] = (dge * gate).astype(de_ref.dtype)
            dgp_ref[rows, :] = (dge * ev * gate * (1.0 - gate)).astype(dgp_ref.dtype)

    row = lambda w: pl.BlockSpec((bm, w), lambda i: (i, 0))
    whole = lambda a: pl.BlockSpec(a.shape, lambda i: (0, 0), pipeline_mode=pl.Buffered(1))
    limit = _vmem_limit(_nbytes((bm, D), BF16), _nbytes(w_gate.shape, w_gate.dtype) // 2, 6 * _nbytes((bm, D), F32))
    return pl.pallas_call(
        body, name=name, grid=(T // bm,),
        in_specs=[row(D), whole(w_gate), row(kp), whole(w_ple), row(D), _vec_spec(D), row(D)],
        out_specs=(row(D), row(D), row(D), _vec_spec(LANES), _vec_spec(D)),
        out_shape=(jax.ShapeDtypeStruct((T, D), F32), jax.ShapeDtypeStruct((T, D), BF16),
                   jax.ShapeDtypeStruct((T, D), BF16), jax.ShapeDtypeStruct((1, LANES), F32),
                   jax.ShapeDtypeStruct((1, D), F32)),
        compiler_params=pltpu.CompilerParams(dimension_semantics=("arbitrary",), vmem_limit_bytes=limit),
    )(n4, w_gate, p, w_ple, h3, g_post, target)


def _t5_index(max_dist, stride):
    rho = np.arange(QBLK)[:, None]
    kap = np.arange(2 * QBLK)[None, :]
    d = rho + QBLK - kap
    valid = (d >= 0) & (d <= max_dist)
    n = np.maximum(d, 0) * stride
    max_exact = NUM_BUCKETS // 2
    nf = np.maximum(n, 1).astype(np.float32)
    large = max_exact + (np.log(nf / np.float32(max_exact)) / np.float32(math.log(MAX_DISTANCE / max_exact))
                         * np.float32(NUM_BUCKETS - max_exact)).astype(np.int32)
    large = np.minimum(large, NUM_BUCKETS - 1)
    bucket = np.where(n < max_exact, n, large)
    return np.where(valid, bucket, -1).astype(np.int32)


def _bias_grad(d_bias, idx, *, name):
    def body(db_ref, idx_ref, o_ref):
        h = pl.program_id(0)

        @pl.when(h == 0)
        def _():
            o_ref[...] = jnp.zeros_like(o_ref)

        idxv = idx_ref[...]
        dv = db_ref[0]
        rows = lax.broadcasted_iota(jnp.int32, (NUM_BUCKETS, LANES), 0)
        lanes = lax.broadcasted_iota(jnp.int32, (NUM_BUCKETS, LANES), 1)
        acc = o_ref[...]
        for b in range(NUM_BUCKETS):
            s = jnp.sum(jnp.where(idxv == b, dv, 0.0))
            acc = acc + jnp.where((rows == b) & (lanes == h), s, 0.0)
        o_ref[...] = acc

    return pl.pallas_call(
        body, name=name, grid=(2 * N_PAIRS,),
        in_specs=[pl.BlockSpec((1, QBLK, 2 * QBLK), lambda h: (h, 0, 0)),
                  pl.BlockSpec((QBLK, 2 * QBLK), lambda h: (0, 0))],
        out_specs=pl.BlockSpec((NUM_BUCKETS, LANES), lambda h: (0, 0)),
        out_shape=jax.ShapeDtypeStruct((NUM_BUCKETS, LANES), F32),
        compiler_params=pltpu.CompilerParams(dimension_semantics=("arbitrary",)),
    )(d_bias, idx)


def _lane():
    return lax.broadcasted_iota(jnp.int32, (1, LANES), 1)


def _head_sel(h):
    return (_lane() < HEAD_DIM) if h == 0 else (_lane() >= HEAD_DIM)


def _stat_lo():
    return (_lane() % HEAD_DIM) < (HEAD_DIM // 2)


def _stack_heads(t, scale=None):
    if scale is not None:
        t = t * scale
    zero = jnp.zeros_like(t)
    return jnp.concatenate([jnp.where(_head_sel(0), t, zero), jnp.where(_head_sel(1), t, zero)], axis=0)


def _class_spec(L, r, cols, stride, pps):
    chunks = N_PAIRS // pps
    mode = pl.Buffered(1) if r * chunks == 1 else None
    return pl.BlockSpec((L, MIX_W // chunks), lambda j, c: (0, (cols + j * stride) * chunks + c), pipeline_mode=mode)


def _rows(b, n_blocks=1):
    return pl.ds(pl.multiple_of(b * QBLK, QBLK), n_blocks * QBLK)


def _key_window(ref, b, cols, first, kv_head=None):
    if kv_head is not None:
        cols = slice(0, LANES)
    if first:
        blk = ref[0:QBLK, cols]
        tile = jnp.concatenate([blk, blk], axis=0)
    else:
        tile = ref[_rows(b - 1, 2), cols]
    if kv_head is None:
        return tile
    wide = tile.astype(F32)
    keep = jnp.logical_xor(_lane() < HEAD_DIM, kv_head == 1)
    return jnp.where(keep, wide, pltpu.roll(wide, HEAD_DIM, 1)).astype(tile.dtype)


def _kv_spec(L, r, col, stride, pps, kv_shared):
    if not kv_shared:
        return _class_spec(L, r, col, stride, pps)
    mode = pl.Buffered(1) if pps == N_PAIRS else None
    return pl.BlockSpec((L, LANES), lambda j, c: (0, col), pipeline_mode=mode)


def _band_fwd(qa, ka, va, bias, *, r, q_col, k_col, v_col, stride, pattern, name, kv_shared=False, sink=None,
              carry=None):
    L = qa.shape[0]
    nb = L // QBLK
    dn = (((1,), (1,)), ((), ()))
    scale = HEAD_DIM ** -0.5

    pps = N_PAIRS

    def body(q_ref, k_ref, v_ref, bias_ref, *rest):
        sel0 = _head_sel(0)
        pair0 = pl.program_id(1) * pps

        def emit(rows, cols, o, lse):
            if sink is None:
                o_ref, lse_ref = rest
                o_ref[rows, cols] = o.astype(o_ref.dtype)
                lse_ref[rows, cols] = lse
                return
            sink_ref, out_ref, outb_ref, lse_ref = rest
            sk = sink_ref[:, cols]
            mx = jnp.maximum(lse, sk)
            w = jnp.exp(lse - mx)
            den = w + jnp.exp(sk - mx)
            out = o * (w / den)
            out_ref[rows, cols] = out
            outb_ref[rows, cols] = out.astype(outb_ref.dtype)
            lse_ref[rows, cols] = mx + jnp.log(den)

        def block(b, first):
            rows = _rows(b)
            for i in range(pps):
                cols = slice(i * LANES, (i + 1) * LANES)
                q2 = _stack_heads(q_ref[rows, cols], scale)
                kv_head = (pair0 + i) // 2 if kv_shared else None
                k = _key_window(k_ref, b, cols, first, kv_head)
                v = _key_window(v_ref, b, cols, first, kv_head)
                bias2 = bias_ref[1 if first else 0, pl.ds(2 * (pair0 + i), 2)].reshape(2 * QBLK, 2 * QBLK)
                s = lax.dot_general(q2, k, dn, preferred_element_type=F32) + bias2
                m = jnp.max(s, axis=1, keepdims=True)
                p = jnp.exp(s - m)
                l = jnp.sum(p, axis=1, keepdims=True)
                o = jnp.dot(p.astype(v.dtype), v, preferred_element_type=F32) * (1.0 / l)
                lse = m + jnp.log(l)
                emit(rows, cols, jnp.where(sel0, o[0:QBLK], o[QBLK:]), jnp.where(sel0, lse[0:QBLK], lse[QBLK:]))

        block(0, True)
        if nb > 1:
            pl.loop(1, nb)(lambda b: block(b, False))

    bias_spec = pl.BlockSpec((None, 2, 2 * N_PAIRS, QBLK, 2 * QBLK), lambda j, c: (pattern, 0, 0, 0, 0))
    out_spec = _class_spec(L, r, 0, 1, pps)
    blk_bytes = _nbytes((L, pps * LANES), F32)
    in_specs = [_class_spec(L, r, q_col, stride, pps), _kv_spec(L, r, k_col, stride, pps, kv_shared),
                _kv_spec(L, r, v_col, stride, pps, kv_shared), bias_spec]
    args = [qa, ka, va, bias]
    f32_out, bf16_out = jax.ShapeDtypeStruct((L, r * MIX_W), F32), jax.ShapeDtypeStruct((L, r * MIX_W), BF16)
    out_shape = [bf16_out, f32_out]
    if sink is not None:
        in_specs.append(pl.BlockSpec((1, MIX_W), lambda j, c: (0, 0)))
        args.append(sink)
        out_shape = [f32_out, bf16_out, f32_out]
    return _call(body, name=name, grid=(r, N_PAIRS // pps), in_specs=in_specs, out_specs=[out_spec] * len(out_shape),
                 out_shape=out_shape, args=args, vmem_limit=_vmem_limit(*([blk_bytes] * 4)), carry=carry)


def _class_rows_spec(rows, r, width):
    return pl.BlockSpec((rows // r, r * width), lambda i, *_: (i, 0))


def _token_scratch(rows, width):
    return pltpu.VMEM((width // LANES, rows, LANES), F32)


def _fill_tokens(scratch, value):
    for c in range(scratch.shape[0]):
        scratch[c] = value[:, c * LANES:(c + 1) * LANES]


def _read_tokens(scratch):
    return jnp.concatenate([scratch[c] for c in range(scratch.shape[0])], axis=1)


def _to_tokens(blk_ref, r, scratch):
    if r == 1:
        return blk_ref[...].astype(F32)
    nt, rows, _ = scratch.shape
    for j in range(r):
        for c in range(nt):
            lanes = slice((j * nt + c) * LANES, (j * nt + c + 1) * LANES)
            scratch.at[c][pl.ds(j, rows // r, stride=r), :] = blk_ref[:, lanes].astype(F32)
    return _read_tokens(scratch)


def _from_tokens(dst_ref, r, scratch):
    nt, rows, _ = scratch.shape
    if r == 1:
        dst_ref[...] = _read_tokens(scratch).astype(dst_ref.dtype)
        return
    for j in range(r):
        for c in range(nt):
            lanes = slice((j * nt + c) * LANES, (j * nt + c + 1) * LANES)
            dst_ref[:, lanes] = scratch.at[c][pl.ds(j, rows // r, stride=r), :].astype(dst_ref.dtype)


def _mm_nt_classes(a, b, bias, dils, *, bm, name):
    M, K = a.shape
    N = b.shape[0]
    dn = (((1,), (1,)), ((), ()))

    def body(a_ref, b_ref, bias_ref, *rest):
        outs, tile = rest[:-1], rest[-1]
        _fill_tokens(tile, lax.dot_general(a_ref[...], b_ref[...], dn, preferred_element_type=F32) + bias_ref[...])
        for out, r in zip(outs, dils):
            _from_tokens(out, r, tile)

    limit = _vmem_limit(_nbytes((bm, K), a.dtype), _nbytes((K, N), b.dtype) // 2, (1 + len(dils)) * _nbytes((bm, N), F32))
    return pl.pallas_call(
        body, name=name, grid=(M // bm,),
        in_specs=[pl.BlockSpec((bm, K), lambda i: (i, 0)),
                  pl.BlockSpec((N, K), lambda i: (0, 0), pipeline_mode=pl.Buffered(1)),
                  pl.BlockSpec((1, N), lambda i: (0, 0))],
        out_specs=tuple(_class_rows_spec(bm, r, N) for r in dils),
        out_shape=tuple(jax.ShapeDtypeStruct((M // r, r * N), BF16) for r in dils),
        scratch_shapes=[_token_scratch(bm, N)],
        compiler_params=pltpu.CompilerParams(dimension_semantics=("parallel",), vmem_limit_bytes=limit),
    )(a, b, bias)


def _merge(branches, dils, sink, *, name):
    W = MIX_W
    T = branches[0][0].shape[0] * dils[0]
    nbr = len(branches)

    def body(*refs):
        sink_ref = refs[2 * nbr]
        out_ref, outb_ref, lse_ref, scratch = refs[2 * nbr + 1:]
        sk = sink_ref[...]
        lses = [_to_tokens(refs[2 * t + 1], dils[t], scratch) for t in range(nbr)]
        mx = sk
        for lse in lses:
            mx = jnp.maximum(mx, lse)
        den = jnp.exp(sk - mx)
        num = jnp.zeros_like(mx)
        for t in range(nbr):
            w = jnp.exp(lses[t] - mx)
            den = den + w
            num = num + w * _to_tokens(refs[2 * t], dils[t], scratch)
        out = num / den
        out_ref[...] = out
        outb_ref[...] = out.astype(outb_ref.dtype)
        lse_ref[...] = mx + jnp.log(den)

    args = [a for br in branches for a in br] + [sink]
    in_specs = [_class_rows_spec(ROW_BLK, r, W) for r in dils for _ in range(2)] + [_vec_spec(W)]
    return pl.pallas_call(
        body, name=name, grid=(T // ROW_BLK,), in_specs=in_specs,
        out_specs=(_row_spec(W), _row_spec(W), _row_spec(W)),
        out_shape=(jax.ShapeDtypeStruct((T, W), F32), jax.ShapeDtypeStruct((T, W), BF16),
                   jax.ShapeDtypeStruct((T, W), F32)),
        scratch_shapes=[_token_scratch(ROW_BLK, W)],
        compiler_params=pltpu.CompilerParams(dimension_semantics=("parallel",)),
    )(*args)


def _attn_delta(dmix, outs, lses, sink, dils, *, name):
    T = dmix.shape[0]
    W = MIX_W
    nd = len(dils)

    def body(dmix_ref, oa_ref, ob_ref, la_ref, lb_ref, sink_ref, *rest):
        doa_ref, sta_ref, dsink_ref = rest[0:3]
        dob_refs, stb_refs = rest[3:3 + nd], rest[3 + nd:3 + 2 * nd]
        do_tile, st_tile = rest[3 + 2 * nd:]

        @pl.when(pl.program_id(0) == 0)
        def _():
            dsink_ref[...] = jnp.zeros_like(dsink_ref)

        sel0, lo = _head_sel(0), _stat_lo()
        for mixer, (o_ref, l_ref) in enumerate(((oa_ref, la_ref), (ob_ref, lb_ref))):
            for i in range(N_PAIRS):
                cols = slice(i * LANES, (i + 1) * LANES)
                do = dmix_ref[:, mixer * W + i * LANES:mixer * W + (i + 1) * LANES]
                prod = do * o_ref[:, cols]
                d0 = jnp.sum(jnp.where(sel0, prod, 0.0), axis=1, keepdims=True)
                d1 = jnp.sum(jnp.where(sel0, 0.0, prod), axis=1, keepdims=True)
                delta = jnp.where(sel0, d0, d1)
                lse = l_ref[:, cols]
                stat = jnp.where(lo, lse, delta)
                if mixer == 0:
                    sta_ref[:, cols] = stat
                    doa_ref[:, cols] = do.astype(doa_ref.dtype)
                    dsink_ref[:, cols] += -jnp.sum(jnp.exp(sink_ref[:, cols] - lse) * delta, axis=0, keepdims=True)
                else:
                    st_tile[i] = stat
                    do_tile[i] = do
        for t, r in enumerate(dils):
            _from_tokens(dob_refs[t], r, do_tile)
            _from_tokens(stb_refs[t], r, st_tile)

    class_specs = [_class_rows_spec(ROW_BLK, r, W) for r in dils]
    out_shape = [jax.ShapeDtypeStruct((T, W), BF16), jax.ShapeDtypeStruct((T, W), F32), jax.ShapeDtypeStruct((1, W), F32)]
    out_shape += [jax.ShapeDtypeStruct((T // r, r * W), BF16) for r in dils]
    out_shape += [jax.ShapeDtypeStruct((T // r, r * W), F32) for r in dils]
    res = pl.pallas_call(
        body, name=name, grid=(T // ROW_BLK,),
        in_specs=[_row_spec(2 * W), _row_spec(W), _row_spec(W), _row_spec(W), _row_spec(W), _vec_spec(W)],
        out_specs=tuple([_row_spec(W), _row_spec(W), _vec_spec(W)] + class_specs + class_specs),
        out_shape=tuple(out_shape),
        scratch_shapes=[_token_scratch(ROW_BLK, W), _token_scratch(ROW_BLK, W)],
        compiler_params=pltpu.CompilerParams(dimension_semantics=("arbitrary",)),
    )(dmix, outs[0], outs[1], lses[0], lses[1], sink)
    return res[0], res[1], res[2], res[3:3 + nd], res[3 + nd:]


def _band_bwd(qa, ka, va, d_out, stat, bias, *, r, q_col, k_col, v_col, stride, pattern, name, kv_shared=False,
              carry=None):
    L = qa.shape[0]
    nb = L // QBLK
    dn_nt = (((1,), (1,)), ((), ()))
    dn_tn = (((0,), (0,)), ((), ()))
    scale = HEAD_DIM ** -0.5

    pps = N_PAIRS if r > 1 else N_PAIRS // 2

    def body(q_ref, k_ref, v_ref, do_ref, st_ref, bias_ref, dq_ref, dk_ref, dv_ref, db_ref, dk_carry, dv_carry):
        @pl.when((pl.program_id(0) == 0) & (pl.program_id(1) == 0))
        def _():
            db_ref[...] = jnp.zeros_like(db_ref)

        lo, sel0 = _stat_lo(), _head_sel(0)
        pair0 = pl.program_id(1) * pps

        def block(b, first):
            rows = _rows(b)
            for i in range(pps):
                heads = pl.ds(2 * (pair0 + i), 2)
                cols = slice(i * LANES, (i + 1) * LANES)
                q2 = _stack_heads(q_ref[rows, cols], scale)
                kv_head = (pair0 + i) // 2 if kv_shared else None
                k = _key_window(k_ref, b, cols, first, kv_head)
                v = _key_window(v_ref, b, cols, first, kv_head)
                do2 = _stack_heads(do_ref[rows, cols])
                st = st_ref[rows, cols]
                lse2 = jnp.concatenate(
                    [jnp.max(jnp.where(_head_sel(h) & lo, st, -jnp.inf), axis=1, keepdims=True) for h in range(2)], axis=0)
                delta2 = jnp.concatenate(
                    [jnp.sum(jnp.where(_head_sel(h) & ~lo, st, 0.0), axis=1, keepdims=True) for h in range(2)],
                    axis=0) * (2.0 / HEAD_DIM)
                bias2 = bias_ref[1 if first else 0, heads].reshape(2 * QBLK, 2 * QBLK)
                s = lax.dot_general(q2, k, dn_nt, preferred_element_type=F32) + bias2
                p = jnp.exp(s - lse2)
                dp = lax.dot_general(do2, v, dn_nt, preferred_element_type=F32)
                ds = p * (dp - delta2)
                db_ref[heads] += ds.reshape(2, QBLK, 2 * QBLK)
                dsb = ds.astype(k.dtype)
                dq2 = jnp.dot(dsb, k, preferred_element_type=F32)
                dq_ref[rows, cols] = (jnp.where(sel0, dq2[0:QBLK], dq2[QBLK:]) * scale).astype(dq_ref.dtype)
                dk_acc = lax.dot_general(dsb, q2, dn_tn, preferred_element_type=F32)
                dv_acc = lax.dot_general(p.astype(k.dtype), do2, dn_tn, preferred_element_type=F32)
                if not first:
                    prev = _rows(b - 1)
                    dk_ref[prev, cols] = (dk_carry[:, cols] + dk_acc[0:QBLK]).astype(dk_ref.dtype)
                    dv_ref[prev, cols] = (dv_carry[:, cols] + dv_acc[0:QBLK]).astype(dv_ref.dtype)
                dk_carry[:, cols] = dk_acc[QBLK:2 * QBLK]
                dv_carry[:, cols] = dv_acc[QBLK:2 * QBLK]

        block(0, True)
        if nb > 1:
            pl.loop(1, nb)(lambda b: block(b, False))

        last = _rows(nb - 1)
        dk_ref[last, :] = dk_carry[...].astype(dk_ref.dtype)
        dv_ref[last, :] = dv_carry[...].astype(dv_ref.dtype)

    tok_spec = _class_spec(L, r, 0, 1, pps)
    bias_spec = pl.BlockSpec((None, 2, 2 * N_PAIRS, QBLK, 2 * QBLK), lambda j, c: (pattern, 0, 0, 0, 0))
    dbias_spec = pl.BlockSpec((2 * N_PAIRS, QBLK, 2 * QBLK), lambda j, c: (0, 0, 0))
    grad = jax.ShapeDtypeStruct((L, r * MIX_W), BF16)
    blk_bytes = _nbytes((L, pps * LANES), BF16)
    carry_shape = pltpu.VMEM((QBLK, pps * LANES), F32)
    return _call(
        body, name=name, grid=(r, N_PAIRS // pps),
        in_specs=[_class_spec(L, r, q_col, stride, pps), _kv_spec(L, r, k_col, stride, pps, kv_shared),
                  _kv_spec(L, r, v_col, stride, pps, kv_shared), tok_spec, tok_spec, bias_spec],
        out_specs=[tok_spec, tok_spec, tok_spec, dbias_spec],
        out_shape=[grad, grad, grad, jax.ShapeDtypeStruct((2 * N_PAIRS, QBLK, 2 * QBLK), F32)],
        args=[qa, ka, va, d_out, stat, bias], scratch=[carry_shape, carry_shape],
        vmem_limit=_vmem_limit(*([blk_bytes] * 9)), semantics=("arbitrary", "arbitrary"), carry=carry)


def _dz_assemble(dq_a, dk_a, dv_a, b_grads, dils, *, name):
    T = dq_a.shape[0]
    W = MIX_W

    def group_sum(x):
        u0 = x[:, 0:LANES] + x[:, LANES:2 * LANES]
        u1 = x[:, 2 * LANES:3 * LANES] + x[:, 3 * LANES:4 * LANES]
        s0 = u0 + pltpu.roll(u0, HEAD_DIM, 1)
        s1 = u1 + pltpu.roll(u1, HEAD_DIM, 1)
        return jnp.where(_head_sel(0), s0, s1)

    def body(*refs):
        dqa_ref, dka_ref, dva_ref = refs[0:3]
        b_refs = refs[3:12]
        dza_ref, dzb_ref, sa_ref, sb_ref, scratch = refs[12:]

        @pl.when(pl.program_id(0) == 0)
        def _():
            sa_ref[...] = jnp.zeros_like(sa_ref)
            sb_ref[...] = jnp.zeros_like(sb_ref)

        def put(dst, acc, col, part):
            w = part.shape[1]
            dst[:, col:col + w] = part.astype(dst.dtype)
            acc[:, col:col + w] += jnp.sum(part, axis=0, keepdims=True)

        put(dza_ref, sa_ref, 0, dqa_ref[...].astype(F32))
        put(dza_ref, sa_ref, W, group_sum(dka_ref[...].astype(F32)))
        put(dza_ref, sa_ref, W + LANES, group_sum(dva_ref[...].astype(F32)))
        for t in range(3):
            part = _to_tokens(b_refs[t], dils[0], scratch)
            for pat in range(1, len(dils)):
                part = part + _to_tokens(b_refs[3 * pat + t], dils[pat], scratch)
            put(dzb_ref, sb_ref, t * W, part)

    args = [dq_a, dk_a, dv_a] + [g[t] for g in b_grads for t in range(3)]
    return pl.pallas_call(
        body, name=name, grid=(T // ROW_BLK,),
        in_specs=[_row_spec(W)] * 3 + [_class_rows_spec(ROW_BLK, r, W) for r in dils for _ in range(3)],
        out_specs=(_row_spec(ZA_W), _row_spec(ZB_W), _vec_spec(ZA_W), _vec_spec(ZB_W)),
        out_shape=(jax.ShapeDtypeStruct((T, ZA_W), BF16), jax.ShapeDtypeStruct((T, ZB_W), BF16),
                   jax.ShapeDtypeStruct((1, ZA_W), F32), jax.ShapeDtypeStruct((1, ZB_W), F32)),
        scratch_shapes=[_token_scratch(ROW_BLK, W)],
        compiler_params=pltpu.CompilerParams(dimension_semantics=("arbitrary",)),
    )(*args)


def _place():
    x, y, c = lax.axis_index("x"), lax.axis_index("y"), lax.axis_index("c")
    chips = [(1 - x, y), (x, 1 - y), (1 - x, 1 - y)]
    return x, y, c, chips


def _cast_place(shards, k_idx, *, name):
    n, steps = len(shards), 4

    def body(k_ref, *refs):
        for s_ref, o_ref in zip(refs[:n], refs[n:]):
            o_ref[...] = s_ref[...].astype(o_ref.dtype)

    blocks = [(a.shape[0] // steps, a.shape[1]) for a in shards]
    grid_spec = pltpu.PrefetchScalarGridSpec(
        num_scalar_prefetch=1, grid=(steps,),
        in_specs=[pl.BlockSpec(blk, lambda t, k_ref: (t, 0)) for blk in blocks],
        out_specs=tuple(pl.BlockSpec(blk, lambda t, k_ref: (k_ref[0] * steps + t, 0)) for blk in blocks))
    return pl.pallas_call(
        body, name=name, grid_spec=grid_spec,
        out_shape=tuple(jax.ShapeDtypeStruct((N_CHIPS * a.shape[0], a.shape[1]), BF16) for a in shards),
        compiler_params=pltpu.CompilerParams(dimension_semantics=("parallel",),
                                             vmem_limit_bytes=_vmem_limit(*[_nbytes(b, F32) for b in blocks])),
    )(k_idx, *shards)


def _gather_carry(fulls, part=(0, 1)):
    n = len(fulls)
    p_idx, n_parts = part

    def piece(ref, chip_idx, half):
        rs = ref.shape[0] // N_CHIPS
        rh = rs // 2
        rows = rh // n_parts
        return ref.at[pl.ds(chip_idx * rs + half * rh + p_idx * rows, rows)]

    def copy(sems, sem, src_ref, dst_ref, to):
        return pltpu.make_async_remote_copy(src_ref=src_ref, dst_ref=dst_ref, send_sem=sems[0].at[sem],
                                            recv_sem=sems[1].at[sem], device_id=to, device_id_type=MESH)

    def ici_copy(ins, outs, sems, i, j, chip, k, c):
        return copy(sems, 3 * i + j, piece(ins[i], k, c), piece(outs[i], k, c), (*chip, c))

    def start(ins, outs, sems):
        x, y, c, chips = _place()
        for i in range(n):
            for j, chip in enumerate(chips):
                ici_copy(ins, outs, sems, i, j, chip, 2 * x + y, c).start()

    def finish(ins, outs, sems):
        x, y, c, chips = _place()
        sibling = (x, y, 1 - c)
        passed = []
        for i in range(n):
            for j, chip in enumerate(chips):
                region = piece(outs[i], 2 * chip[0] + chip[1], c)
                copy(sems, 3 * i + j, region, region, (*chip, c)).wait_recv()
                cp = copy(sems, 3 * n + 3 * i + j, region, region, sibling)
                cp.start()
                passed.append(cp)
        for i in range(n):
            for j, chip in enumerate(chips):
                region = piece(outs[i], 2 * chip[0] + chip[1], 1 - c)
                copy(sems, 3 * n + 3 * i + j, region, region, sibling).wait_recv()
        for i in range(n):
            for j, chip in enumerate(chips):
                ici_copy(ins, outs, sems, i, j, chip, 2 * x + y, c).wait_send()
        for cp in passed:
            cp.wait_send()

    return _Carry(fulls, [jax.ShapeDtypeStruct(a.shape, a.dtype) for a in fulls], {i: i for i in range(n)},
                  [pltpu.SemaphoreType.DMA((6 * n,)), pltpu.SemaphoreType.DMA((6 * n,))], start, finish)


def _pair_swap_carry(grads):
    n = len(grads)

    def copy(ins, outs, sems, i):
        x, y, c, _ = _place()
        return pltpu.make_async_remote_copy(src_ref=ins[i].at[:, 1 - c], dst_ref=outs[i], send_sem=sems[0].at[i],
                                            recv_sem=sems[1].at[i], device_id=(x, y, 1 - c), device_id_type=MESH)

    def start(ins, outs, sems):
        for i in range(n):
            copy(ins, outs, sems, i).start()

    def finish(ins, outs, sems):
        for i in range(n):
            copy(ins, outs, sems, i).wait()

    return _Carry(grads, [jax.ShapeDtypeStruct((a.shape[0], a.shape[2], a.shape[3]), a.dtype) for a in grads], {},
                  [pltpu.SemaphoreType.DMA((n,)), pltpu.SemaphoreType.DMA((n,))], start, finish)


def _pair_sum(g4, got, c_idx, *, name):
    _, _, rh, cs = g4.shape
    rb = _row_block(rh, 512, 16)

    def body(c_ref, own_ref, got_ref, o_ref):
        o_ref[...] = (own_ref[...].astype(F32) + got_ref[...].astype(F32)).astype(o_ref.dtype)

    grid_spec = pltpu.PrefetchScalarGridSpec(
        num_scalar_prefetch=1, grid=(N_CHIPS, rh // rb),
        in_specs=[pl.BlockSpec((None, None, rb, cs), lambda k, t, c_ref: (k, c_ref[0], t, 0)),
                  pl.BlockSpec((None, rb, cs), lambda k, t, c_ref: (k, t, 0))],
        out_specs=pl.BlockSpec((None, rb, cs), lambda k, t, c_ref: (k, t, 0)))
    return pl.pallas_call(
        body, name=name, grid_spec=grid_spec, out_shape=jax.ShapeDtypeStruct((N_CHIPS, rh, cs), BF16),
        compiler_params=pltpu.CompilerParams(dimension_semantics=("parallel", "parallel")),
    )(c_idx, g4, got)


def _chip_scatter_carry(sums):
    n = len(sums)

    def copies(ins, outs, sems):
        x, y, c, chips = _place()
        return [pltpu.make_async_remote_copy(src_ref=ins[i].at[2 * chip[0] + chip[1]], dst_ref=outs[i].at[j],
                                             send_sem=sems[0].at[3 * i + j], recv_sem=sems[1].at[3 * i + j],
                                             device_id=(*chip, c), device_id_type=MESH)
                for i in range(n) for j, chip in enumerate(chips)]

    def start(ins, outs, sems):
        for cp in copies(ins, outs, sems):
            cp.start()

    def finish(ins, outs, sems):
        for cp in copies(ins, outs, sems):
            cp.wait()

    return _Carry(sums, [jax.ShapeDtypeStruct((3,) + a.shape[1:], a.dtype) for a in sums], {},
                  [pltpu.SemaphoreType.DMA((3 * n,)), pltpu.SemaphoreType.DMA((3 * n,))], start, finish)


def _chip_sum(sums, gots, kc_idx, *, name):
    n, steps = len(sums), 2

    def body(kc_ref, *refs):
        for own_ref, got_ref, o_ref in zip(refs[:n], refs[n:2 * n], refs[2 * n:]):
            acc = own_ref[...].astype(F32)
            for j in range(3):
                acc = acc + got_ref[j].astype(F32)
            o_ref[...] = acc

    blocks = [(a.shape[1] // steps, a.shape[2]) for a in sums]
    grid_spec = pltpu.PrefetchScalarGridSpec(
        num_scalar_prefetch=1, grid=(steps,),
        in_specs=[pl.BlockSpec((None,) + blk, lambda t, kc: (kc[0], t, 0)) for blk in blocks]
        + [pl.BlockSpec((3,) + blk, lambda t, kc: (0, t, 0)) for blk in blocks],
        out_specs=tuple(pl.BlockSpec(blk, lambda t, kc: (kc[1] * steps + t, 0)) for blk in blocks))
    return pl.pallas_call(
        body, name=name, grid_spec=grid_spec,
        out_shape=tuple(jax.ShapeDtypeStruct((2 * a.shape[1], a.shape[2]), F32) for a in sums),
        compiler_params=pltpu.CompilerParams(dimension_semantics=("parallel",),
                                             vmem_limit_bytes=_vmem_limit(*[3 * _nbytes(b, F32) for b in blocks])),
    )(kc_idx, *sums, *gots)


def _half_swap_carry(shards):
    n = len(shards)

    def copy(ins, outs, sems, i, to_my_half):
        x, y, c, _ = _place()
        rh = ins[i].shape[0] // 2
        half = c if to_my_half else 1 - c
        return pltpu.make_async_remote_copy(
            src_ref=ins[i].at[pl.ds(c * rh, rh)], dst_ref=outs[i].at[pl.ds(half * rh, rh)],
            send_sem=sems[0].at[i], recv_sem=sems[1].at[i], device_id=(x, y, 1 - c), device_id_type=MESH)

    def start(ins, outs, sems):
        for i in range(n):
            copy(ins, outs, sems, i, True).start()

    def finish(ins, outs, sems):
        for i in range(n):
            copy(ins, outs, sems, i, False).wait_recv()
        for i in range(n):
            copy(ins, outs, sems, i, True).wait_send()

    return _Carry(shards, [jax.ShapeDtypeStruct(a.shape, a.dtype) for a in shards], {i: i for i in range(n)},
                  [pltpu.SemaphoreType.DMA((n,)), pltpu.SemaphoreType.DMA((n,))], start, finish)


SMALL_ROW = 1024


def _small_layout(shapes):
    starts, row = [], 0
    for r, c in shapes:
        starts.append(row)
        row += -(-c // SMALL_ROW) if r == 1 else r
    return starts, -(-row // SUBLANES) * SUBLANES


def _small_pieces(shape, start):
    r, c = shape
    if r == 1:
        return [(start + q, min(SMALL_ROW, c - q * SMALL_ROW), q * SMALL_ROW) for q in range(-(-c // SMALL_ROW))]
    return None


def _whole_spec(shape):
    return pl.BlockSpec(tuple(shape), lambda i: (0,) * len(shape))


def _small_all_reduce(parts, *, name):
    shapes = [a.shape for a in parts]
    starts, rows = _small_layout(shapes)
    n = len(parts)
    n_tables = sum(1 for r, _ in shapes if r > 1)
    assert all(r > 1 and c <= LANES for r, c in shapes[:n_tables]) and all(r == 1 for r, _ in shapes[n_tables:])
    table_rows = starts[n_tables]
    assert table_rows % SUBLANES == 0

    def regions(slot):
        return slot.at[pl.ds(0, table_rows), pl.ds(0, LANES)], slot.at[pl.ds(table_rows, rows - table_rows)]

    def body(*refs):
        ins, outs = refs[0:n], refs[n:2 * n]
        buf, send_sems, recv_sems = refs[2 * n:]
        x, y, c, _ = _place()
        me = 4 * x + 2 * y + c
        mine = buf.at[me]
        mine[...] = jnp.zeros((rows, SMALL_ROW), F32)
        for ref, shape, start in zip(ins, shapes, starts):
            pieces = _small_pieces(shape, start)
            if pieces is None:
                mine[start:start + shape[0], 0:shape[1]] = ref[...]
            else:
                for row, width, col in pieces:
                    mine[row:row + 1, 0:width] = ref[:, col:col + width]
        def peer(d):
            return 1 - x if d & 4 else x, 1 - y if d & 2 else y, 1 - c if d & 1 else c

        def copy(d, part, dst_slot):
            sem = 2 * (d - 1) + part
            return pltpu.make_async_remote_copy(
                src_ref=regions(mine)[part], dst_ref=regions(dst_slot)[part], send_sem=send_sems.at[sem],
                recv_sem=recv_sems.at[sem], device_id=peer(d), device_id_type=MESH)

        for d in range(1, 8):
            for part in range(2):
                copy(d, part, mine).start()
        for d in range(1, 8):
            px, py, pc = peer(d)
            for part in range(2):
                copy(d, part, buf.at[4 * px + 2 * py + pc]).wait_recv()
        for d in range(1, 8):
            for part in range(2):
                copy(d, part, mine).wait_send()
        tables, vectors = regions(buf.at[0])
        acc_t, acc_v = tables[...], vectors[...]
        for s in range(1, 8):
            tables, vectors = regions(buf.at[s])
            acc_t, acc_v = acc_t + tables[...], acc_v + vectors[...]
        tables, vectors = regions(mine)
        tables[...] = acc_t
        vectors[...] = acc_v
        for ref, shape, start in zip(outs, shapes, starts):
            pieces = _small_pieces(shape, start)
            if pieces is None:
                ref[...] = mine[start:start + shape[0], 0:shape[1]]
            else:
                for row, width, col in pieces:
                    ref[:, col:col + width] = mine[row:row + 1, 0:width]

    whole = [_whole_spec(s) for s in shapes]
    return pl.pallas_call(
        body, name=name, grid=(1,), in_specs=whole, out_specs=tuple(whole),
        out_shape=tuple(jax.ShapeDtypeStruct(s, F32) for s in shapes),
        scratch_shapes=[pltpu.VMEM((8, rows, SMALL_ROW), F32), pltpu.SemaphoreType.DMA((14,)),
                        pltpu.SemaphoreType.DMA((14,))],
    )(*parts)


def _adamw_small(ws, gs, ms, vs, *, name):
    n = len(ws)
    c1 = 1.0 - ADAM_B1 ** ADAM_STEP
    c2 = 1.0 - ADAM_B2 ** ADAM_STEP

    def body(*refs):
        for k in range(n):
            w_ref, g_ref, m_ref, v_ref = (refs[t * n + k] for t in range(4))
            go_ref, d_ref, mo_ref, vo_ref = (refs[(4 + t) * n + k] for t in range(4))
            gv = g_ref[...]
            go_ref[...] = gv
            mn = ADAM_B1 * m_ref[...] + (1.0 - ADAM_B1) * gv
            vn = ADAM_B2 * v_ref[...] + (1.0 - ADAM_B2) * (gv * gv)
            d_ref[...] = -ADAM_LR * ((mn / c1) / (jnp.sqrt(vn / c2) + ADAM_EPS) + ADAM_WD * w_ref[...])
            mo_ref[...] = mn
            vo_ref[...] = vn

    whole = [_whole_spec(a.shape) for a in ws]
    shapes = tuple(jax.ShapeDtypeStruct(a.shape, F32) for a in ws)
    res = pl.pallas_call(
        body, name=name, grid=(1,), in_specs=whole * 4, out_specs=tuple(whole * 4), out_shape=shapes * 4,
    )(*ws, *gs, *ms, *vs)
    return res[0:n], res[n:2 * n], res[2 * n:3 * n], res[3 * n:4 * n]


def _adamw(ws, gs, ms, vs, *, name):
    n, steps = len(ws), 8
    c1 = 1.0 - ADAM_B1 ** ADAM_STEP
    c2 = 1.0 - ADAM_B2 ** ADAM_STEP

    def body(*refs):
        for k in range(n):
            w_ref, g_ref, m_ref, v_ref = (refs[t * n + k] for t in range(4))
            go_ref, d_ref, mo_ref, vo_ref = (refs[(4 + t) * n + k] for t in range(4))
            gv = g_ref[...]
            go_ref[...] = gv
            mn = ADAM_B1 * m_ref[...] + (1.0 - ADAM_B1) * gv
            vn = ADAM_B2 * v_ref[...] + (1.0 - ADAM_B2) * (gv * gv)
            d_ref[...] = -ADAM_LR * ((mn / c1) / (jnp.sqrt(vn / c2) + ADAM_EPS) + ADAM_WD * w_ref[...])
            mo_ref[...] = mn
            vo_ref[...] = vn

    blocks = [(a.shape[0] // steps, a.shape[1]) for a in ws]
    specs = [pl.BlockSpec(blk, lambda i: (i, 0)) for blk in blocks]
    shapes = tuple(jax.ShapeDtypeStruct(a.shape, F32) for a in ws)
    res = pl.pallas_call(
        body, name=name, grid=(steps,), in_specs=specs * 4, out_specs=tuple(specs * 4), out_shape=shapes * 4,
        compiler_params=pltpu.CompilerParams(dimension_semantics=("parallel",),
                                             vmem_limit_bytes=_vmem_limit(*[8 * _nbytes(b, F32) for b in blocks])),
    )(*ws, *gs, *ms, *vs)
    return res[0:n], res[n:2 * n], res[2 * n:3 * n], res[3 * n:4 * n]


def _local_step(x, p, target, small, sched):
    T = x.shape[0]
    W = MIX_W
    rel_bias = small["rel_bias"]
    b_in_a, b_in_b = small["b_in"][:, 0:ZA_W], small["b_in"][:, ZA_W:]

    idx_np = [_t5_index(A_WINDOW - 1, 1)] + [_t5_index(window // dil, dil) for window, dil in B_PATTERNS]
    idx_all = jnp.asarray(np.stack(idx_np))
    n1, bias_all = sched.run(_prologue, x, small["ffn1_pre_g"], rel_bias, idx_all, name="prologue")
    w_gu1 = sched.weight("ffn1_w_gu")
    *gu1, a1 = sched.run(_ffn_up, n1, w_gu1, bm=512, name="ffn1_gu")
    w_d1 = sched.weight("ffn1_w_down")
    f1, h1, n2 = sched.run(_mm_resid_norm, a1, w_d1, x, small["ffn1_post_g"], 0.5, small["attn_pre_g"], bm=512,
                           name="ffn1_down")
    win_a, win_b = sched.weight("w_in")
    za = _mm_nt(n2, win_a, bm=1024, bn=ZA_W, out_dtype=BF16, name="in_proj_a", add=b_in_a)
    dils = tuple(dil for _, dil in B_PATTERNS)
    zb_by_dil = _mm_nt_classes(n2, win_b, b_in_b, dils, bm=512, name="in_proj_b")

    a_args = dict(r=1, q_col=0, k_col=W // LANES, v_col=W // LANES + 1, stride=0, pattern=0, kv_shared=True)
    sink_row = jnp.repeat(small["sinks"], HEAD_DIM, axis=1)
    out_a, out_a_bf, lse_a = sched.run(_band_fwd, za, za, za, bias_all, name="band_a_fwd", sink=sink_row, **a_args)

    no_sink = jnp.full((1, W), NEG, F32)
    b_ctx, branches = [], []
    for t, dil in enumerate(dils):
        zb_r = zb_by_dil[t]
        args = dict(r=dil, q_col=0, k_col=1, v_col=2, stride=ZB_W // W, pattern=1 + t)
        branches.append(sched.run(_band_fwd, zb_r, zb_r, zb_r, bias_all, name=f"band_b{t}_fwd", **args))
        b_ctx.append((jnp.asarray(idx_np[1 + t]), zb_r, args))
    out_b, out_b_bf, lse_b = _merge(branches, dils, no_sink, name="merge_b")

    mix = jnp.concatenate([out_a_bf, out_b_bf], axis=1)
    w_out = sched.weight("w_out")
    att, h2, n3 = _mm_resid_norm(mix, w_out, h1, small["attn_post_g"], 1.0, small["ffn2_pre_g"], bm=512,
                                 name="out_proj", bias=small["b_out"])
    w_gu2, w_d2 = sched.weight("ffn2_w_gu"), sched.weight("ffn2_w_down")
    *gu2, a2 = _ffn_up(n3, w_gu2, bm=512, name="ffn2_gu")
    f2, h3, n4 = _mm_resid_norm(a2, w_d2, h2, small["ffn2_post_g"], 0.5, small["ple_pre_g"], bm=512,
                                name="ffn2_down")
    w_gate = sched.weight("w_ple_gate")
    p_bf = p.astype(BF16)
    dh4, de, dgp, loss, d_ple_post = _ple_head(n4, w_gate, p_bf, sched.weight("w_ple_proj"), h3, small["ple_post_g"],
                                               target, bm=512, name="ple_head")

    gs = {"ple_post_g": d_ple_post}
    sched.grad("w_ple_proj", _mm_tn(p_bf, de, bm=256, bn=1024, out_dtype=BF16, name="d_w_ple"))
    sched.grad("w_ple_gate", _mm_tn(n4, dgp, bm=512, bn=1024, out_dtype=BF16, name="d_w_gate"))
    dh3, df2, gs["ple_pre_g"], gs["ffn2_post_g"], _ = sched.run(
        _mm_nt_norms_bwd, dgp, w_gate, dh4, h3, small["ple_pre_g"], f2, small["ffn2_post_g"], 0.5, bm=512, name="d_n4")

    def ffn_bwd(df, a, gu, n, w_down, w_gu, tag, *norm_args):
        dgu = sched.run(_ffn_down_bwd, df, w_down, gu[0], gu[1], bm=512, name=f"ffn{tag}_d_a")
        sched.grad(f"ffn{tag}_w_gu", _mm_tn(n, dgu, bm=512, bn=1408, out_dtype=BF16, name=f"ffn{tag}_d_w_gu",
                                            n_stack=N_CHIPS))
        sched.grad(f"ffn{tag}_w_down", sched.run(_mm_tn, a, df, bm=256, bn=1024, out_dtype=BF16,
                                                name=f"ffn{tag}_d_w_down"))
        return sched.run(_mm_nt_norms_bwd, dgu, w_gu, *norm_args, bm=512, name=f"ffn{tag}_d_n")

    dh2, datt, gs["ffn2_pre_g"], gs["attn_post_g"], gs["b_out"] = ffn_bwd(
        df2, a2, gu2, n3, w_d2, w_gu2, "2", dh3, h2, small["ffn2_pre_g"], att, small["attn_post_g"], 1.0)
    sched.grad("w_out", _mm_tn(mix, datt, bm=512, bn=1024, out_dtype=BF16, name="d_w_out"))
    dmix = sched.run(_mm_nt, datt, w_out, bm=1024, bn=1024, out_dtype=F32, name="d_mix")

    do_a, stat_a, dsink, do_b, stat_b = _attn_delta(dmix, (out_a, out_b), (lse_a, lse_b), sink_row, dils,
                                                    name="attn_delta")
    gs["sinks"] = dsink[:, ::HEAD_DIM]
    dq_a, dk_a, dv_a, dbias_a = sched.run(_band_bwd, za, za, za, do_a, stat_a, bias_all, name="band_a_bwd", **a_args)
    d_rel_a = _bias_grad(dbias_a, jnp.asarray(idx_np[0]), name="d_rel_a")
    b_grads = []
    d_rel_b = None
    for t, (idx, zb_r, args) in enumerate(b_ctx):
        dq, dk, dv, dbias = sched.run(_band_bwd, zb_r, zb_r, zb_r, do_b[t], stat_b[t], bias_all,
                                      name=f"band_b{t}_bwd", **args)
        b_grads.append((dq, dk, dv))
        d_rel = _bias_grad(dbias, idx, name=f"d_rel_b{t}")
        d_rel_b = d_rel if d_rel_b is None else d_rel_b + d_rel
    gs["rel_bias"] = jnp.concatenate([d_rel_a[:, 0:2 * N_PAIRS], d_rel_b[:, 0:2 * N_PAIRS]], axis=1)
    dza, dzb, dsum_a, dsum_b = _dz_assemble(dq_a, dk_a, dv_a, b_grads, dils, name="d_z")
    gs["b_in"] = jnp.concatenate([dsum_a, dsum_b], axis=1)
    sched.grad("w_in", (_mm_tn(dza, n2, bm=ZA_W, bn=1024, out_dtype=BF16, name="d_w_in_a"),
                        _mm_tn(dzb, n2, bm=ZB_W // 2, bn=1024, out_dtype=BF16, name="d_w_in_b")))
    dn2_a = _mm_nn(dza, win_a, bm=1024, bn=1024, out_dtype=F32, name="d_n2_a")
    dh1, df1, gs["attn_pre_g"], gs["ffn1_post_g"], _ = sched.run(
        _mm_nt_norms_bwd, dzb, win_b, dh2, h1, small["attn_pre_g"], f1, small["ffn1_post_g"], 0.5, bm=512,
        name="d_n2_b", add=dn2_a, b_is_kn=True)
    grad_x, gs["ffn1_pre_g"] = ffn_bwd(df1, a1, gu1, n1, w_d1, w_gu1, "1", dh1, x, small["ffn1_pre_g"], None, None, 0.0)
    return loss, grad_x, gs


def _to_compute(name, full):
    st = full.reshape(N_CHIPS, full.shape[0] // N_CHIPS, full.shape[1])
    if name in ("ffn1_w_gu", "ffn2_w_gu"):
        return st
    if name == "w_ple_proj":
        return jnp.transpose(st, (1, 0, 2)).reshape(st.shape[1], -1)
    if name == "w_in":
        return full[0:ZA_W], full[ZA_W:]
    return full


def _to_comm(name, g):
    if name == "w_in":
        g = jnp.concatenate(g, axis=0)
    if name == "w_ple_proj":
        g = jnp.transpose(g.reshape(g.shape[0], N_CHIPS, -1), (1, 0, 2))
    rs = g.shape[1] if g.ndim == 3 else g.shape[0] // N_CHIPS
    return g.reshape(N_CHIPS, 2, rs // 2, g.shape[-1])


class _Schedule:
    GATHER = {
        "prologue": [("ffn1_w_gu", (0, 1))],
        "ffn1_gu": [("ffn1_w_down", (0, 1)), ("w_in", (0, 1))],
        "band_a_fwd": [("ffn2_w_gu", (0, 2)), ("w_out", (0, 1))],
        "band_b0_fwd": [("ffn2_w_gu", (1, 2)), ("w_ple_gate", (0, 1)), ("w_ple_proj", (0, 1))],
        "band_b1_fwd": [("ffn2_w_down", (0, 1))],
    }
    SWAP = {"ffn2_d_a": ["w_ple_proj", "w_ple_gate"], "ffn2_d_w_down": ["ffn2_w_gu"], "ffn2_d_n": ["ffn2_w_down"],
            "band_a_bwd": ["w_out"], "d_n2_b": ["w_in"], "ffn1_d_w_down": ["ffn1_w_gu"], "ffn1_d_n": ["ffn1_w_down"]}
    SCATTER = {"ffn2_d_w_down": ["w_ple_proj", "w_ple_gate"], "ffn2_d_n": ["ffn2_w_gu"], "band_a_bwd": ["ffn2_w_down"],
               "band_b0_bwd": ["w_out"], "ffn1_d_a": ["w_in"], "ffn1_d_n": ["ffn1_w_gu"]}

    def __init__(self, placed, c_idx, kc_idx):
        self.full = dict(placed)
        self.missing = {n: 1.0 for n in placed}
        self.c_idx, self.kc_idx = c_idx, kc_idx
        self.grads, self.swapped, self.pair_sums, self.scattered = {}, {}, {}, {}

    def _gather(self, entries):
        carries, after = [], []
        for part in sorted({pt for _, pt in entries}):
            names = [n for n, pt in entries if pt == part]
            carry = _gather_carry([self.full[n] for n in names], part)
            carries.append(carry)

            def done(carry=carry, names=names, part=part):
                for n, a in zip(names, carry.results):
                    self.full[n] = a
                    self.missing[n] -= 1.0 / part[1]
            after.append(done)
        return carries, after

    def weight(self, name):
        assert abs(self.missing[name]) < 1e-9, (name, self.missing[name])
        return _to_compute(name, self.full[name])

    def grad(self, name, g):
        self.grads[name] = _to_comm(name, g)

    def _swap(self, names):
        carry = _pair_swap_carry([self.grads[n] for n in names])

        def done():
            self.swapped.update(zip(names, carry.results))
        return carry, done

    def _scatter(self, names):
        for n in names:
            self.pair_sums[n] = _pair_sum(self.grads[n], self.swapped[n], self.c_idx, name=f"grad_pair_sum_{n}")
        carry = _chip_scatter_carry([self.pair_sums[n] for n in names])

        def done():
            self.scattered.update(zip(names, carry.results))
        return carry, done

    def run(self, fn, *args, name, **kw):
        carries, after = self._gather(self.GATHER.get(name, []))
        for names, make in ((self.SWAP.get(name), self._swap), (self.SCATTER.get(name), self._scatter)):
            if names:
                carry, done = make(names)
                carries.append(carry)
                after.append(done)
        out = fn(*args, name=name, carry=_join(carries), **kw)
        for done in after:
            done()
        return out

    def finish(self):
        left = [n for n in BIG if n not in self.scattered]
        to_swap = [n for n in left if n not in self.swapped]
        if to_swap:
            carry, done = self._swap(to_swap)
            _comm_call(carry, name="grad_pair_swap")
            done()
        if left:
            carry, done = self._scatter(left)
            _comm_call(carry, name="grad_chip_scatter")
            done()
        halves = _chip_sum([self.pair_sums[n] for n in BIG], [self.scattered[n] for n in BIG], self.kc_idx,
                           name="grad_chip_sum")
        carry = _half_swap_carry(halves)
        _comm_call(carry, name="grad_half_swap")
        return dict(zip(BIG, carry.results))


def kernel(x, p, rel_bias, ffn1_pre_g, ffn1_w_gu, ffn1_w_down, ffn1_post_g, attn_pre_g, w_in, b_in, sinks, w_out, b_out, attn_post_g, ffn2_pre_g, ffn2_w_gu, ffn2_w_down, ffn2_post_g, ple_pre_g, w_ple_gate, w_ple_proj, ple_post_g, loss_target, m_rel_bias, m_ffn1_pre_g, m_ffn1_w_gu, m_ffn1_w_down, m_ffn1_post_g, m_attn_pre_g, m_w_in, m_b_in, m_sinks, m_w_out, m_b_out, m_attn_post_g, m_ffn2_pre_g, m_ffn2_w_gu, m_ffn2_w_down, m_ffn2_post_g, m_ple_pre_g, m_w_ple_gate, m_w_ple_proj, m_ple_post_g, v_rel_bias, v_ffn1_pre_g, v_ffn1_w_gu, v_ffn1_w_down, v_ffn1_post_g, v_attn_pre_g, v_w_in, v_b_in, v_sinks, v_w_out, v_b_out, v_attn_post_g, v_ffn2_pre_g, v_ffn2_w_gu, v_ffn2_w_down, v_ffn2_post_g, v_ple_pre_g, v_w_ple_gate, v_w_ple_proj, v_ple_post_g):
    given = dict(locals())
    w = {n: given[n] for n in WEIGHTS}
    m = {n: given["m_" + n] for n in WEIGHTS}
    v = {n: given["v_" + n] for n in WEIGHTS}
    c_pos = lax.axis_index("c").astype(jnp.int32)
    k_pos = (2 * lax.axis_index("x") + lax.axis_index("y")).astype(jnp.int32)
    c_idx, k_idx, kc_idx = c_pos.reshape(1), k_pos.reshape(1), jnp.stack([k_pos, c_pos])

    def shard(a, n):
        return jnp.transpose(a[0]) if n == "w_in" else a[0]

    def unshard(a, n):
        return (jnp.transpose(a) if n == "w_in" else a)[None]

    placed = _cast_place([shard(w[n], n) for n in BIG], k_idx, name="place_shards")
    sched = _Schedule(dict(zip(BIG, placed)), c_idx, kc_idx)
    small = {n: (w[n] if n == "rel_bias" else w[n].reshape(1, -1)) for n in SMALL}

    loss_part, grad_x, gs = _local_step(x[0], p[0, 0], loss_target[0], small, sched)

    grads_big = sched.finish()

    *small_grads, loss_row = _small_all_reduce([gs[n] for n in SMALL] + [loss_part], name="small_all_reduce")
    loss = loss_row[0, 0]

    grads, delta, new_m, new_v = {}, {}, {}, {}
    res = _adamw([shard(w[n], n) for n in BIG], [grads_big[n] for n in BIG], [shard(m[n], n) for n in BIG],
                 [shard(v[n], n) for n in BIG], name="adamw_big")
    for n, gg, dd, mm, vv in zip(BIG, *res):
        grads[n], delta[n], new_m[n], new_v[n] = (unshard(a, n) for a in (gg, dd, mm, vv))
    res = _adamw_small([w[n] for n in SMALL], small_grads, [m[n] for n in SMALL], [v[n] for n in SMALL],
                       name="adamw_small")
    for name, gg, dd, mm, vv in zip(SMALL, *res):
        grads[name], delta[name], new_m[name], new_v[name] = gg, dd, mm, vv

    return (loss, grad_x[None], *[grads[n] for n in WEIGHTS], *[delta[n] for n in WEIGHTS],
            *[new_m[n] for n in WEIGHTS], *[new_v[n] for n in WEIGHTS])
```

```python
import math

import jax
import jax.numpy as jnp
import numpy as np
from jax import lax
from jax.experimental import pallas as pl
from jax.experimental.pallas import tpu as pltpu

F32 = jnp.float32
BF16 = jnp.bfloat16
MESH = pl.DeviceIdType.MESH

EPS = 1e-6
NEG = -1e30
LANES = 128
SUBLANES = 8
HEAD_DIM = 64
QBLK = 128
N_PAIRS = 4
MIX_W = N_PAIRS * LANES
A_WINDOW = 128
B_PATTERNS = ((128, 1), (512, 4), (2048, 16))
NUM_BUCKETS = 32
MAX_DISTANCE = 2048
ZA_W = 768
ZB_W = 1536
N_CHIPS = 4
VMEM_BYTES_V7X = 64 * 2**20

ADAM_LR, ADAM_B1, ADAM_B2, ADAM_EPS, ADAM_WD, ADAM_STEP = 0.001, 0.9, 0.999, 1e-08, 0.01, 10

BIG = ("ffn1_w_gu", "ffn1_w_down", "w_in", "w_out", "ffn2_w_gu", "ffn2_w_down", "w_ple_gate", "w_ple_proj")
SMALL = ("rel_bias", "ffn1_pre_g", "ffn1_post_g", "attn_pre_g", "b_in", "sinks", "b_out", "attn_post_g",
         "ffn2_pre_g", "ffn2_post_g", "ple_pre_g", "ple_post_g")
WEIGHTS = ("rel_bias", "ffn1_pre_g", "ffn1_w_gu", "ffn1_w_down", "ffn1_post_g", "attn_pre_g", "w_in", "b_in",
           "sinks", "w_out", "b_out", "attn_post_g", "ffn2_pre_g", "ffn2_w_gu", "ffn2_w_down", "ffn2_post_g",
           "ple_pre_g", "w_ple_gate", "w_ple_proj", "ple_post_g")


def _vmem_limit(*block_bytes):
    need = 2 * sum(block_bytes) + max(block_bytes) * 2 + (4 << 20)
    return int(min(max(need, 32 << 20), VMEM_BYTES_V7X - (6 << 20)))


def _nbytes(shape, dtype):
    return int(np.prod(shape)) * jnp.dtype(dtype).itemsize


MXU_WIDTH_V7X = 256


def _col_chunks(n):
    return [slice(c, min(c + MXU_WIDTH_V7X, n)) for c in range(0, n, MXU_WIDTH_V7X)]


def _row_halves(rows):
    return [slice(0, rows // 2), slice(rows // 2, rows)]


def _row_block(rows, cap, mult):
    for cand in range(min(rows, cap), 0, -1):
        if rows % cand == 0 and cand % mult == 0:
            return cand
    raise ValueError((rows, cap, mult))


class _Carry:
    def __init__(self, operands, out_shapes, aliases, sems, start, finish):
        self.operands, self.out_shapes, self.aliases, self.sems = list(operands), list(out_shapes), dict(aliases), list(sems)
        self.start, self.finish = start, finish
        self.results = None


def _join(carries):
    carries = [c for c in carries if c is not None]
    if not carries:
        return None
    if len(carries) == 1:
        return carries[0]
    offs, pos = [], [0, 0, 0]
    for c in carries:
        offs.append(tuple(pos))
        pos = [pos[0] + len(c.operands), pos[1] + len(c.out_shapes), pos[2] + len(c.sems)]
    aliases = {}
    for c, (oi, oo, _) in zip(carries, offs):
        aliases.update({oi + i: oo + o for i, o in c.aliases.items()})

    def run(which):
        def fn(ins, outs, sems):
            for c, (oi, oo, os_) in zip(carries, offs):
                getattr(c, which)(ins[oi:oi + len(c.operands)], outs[oo:oo + len(c.out_shapes)],
                                  sems[os_:os_ + len(c.sems)])
        return fn

    joined = _Carry([a for c in carries for a in c.operands], [s for c in carries for s in c.out_shapes], aliases,
                    [s for c in carries for s in c.sems], run("start"), run("finish"))
    joined.parts = (carries, offs)
    return joined


def _set_results(carry, outs):
    carry.results = list(outs)
    if hasattr(carry, "parts"):
        for c, (_, oo, _) in zip(*carry.parts):
            _set_results(c, outs[oo:oo + len(c.out_shapes)])


ANY = pl.BlockSpec(memory_space=pl.ANY)


def _call(body, *, name, grid, in_specs, out_specs, out_shape, args, scratch=(), vmem_limit=None, carry=None,
          semantics=None):
    n_ci, n_co, n_cs = len(args), len(out_shape), len(scratch)
    semantics = semantics or ("parallel",) * len(grid)
    if carry is None:
        res = pl.pallas_call(
            body, name=name, grid=grid, in_specs=list(in_specs), out_specs=tuple(out_specs), out_shape=tuple(out_shape),
            scratch_shapes=list(scratch),
            compiler_params=pltpu.CompilerParams(dimension_semantics=semantics, vmem_limit_bytes=vmem_limit),
        )(*args)
        return list(res)
    n_pi, n_po = len(carry.operands), len(carry.out_shapes)

    def full_body(*refs):
        ci, pi = refs[:n_ci], refs[n_ci:n_ci + n_pi]
        o0 = n_ci + n_pi
        co, po = refs[o0:o0 + n_co], refs[o0 + n_co:o0 + n_co + n_po]
        s0 = o0 + n_co + n_po
        cs, ps = refs[s0:s0 + n_cs], refs[s0 + n_cs:]
        ids = [pl.program_id(ax) for ax in range(len(grid))]
        first, last = ids[0] == 0, ids[0] == grid[0] - 1
        for ax in range(1, len(grid)):
            first, last = first & (ids[ax] == 0), last & (ids[ax] == grid[ax] - 1)

        @pl.when(first)
        def _():
            carry.start(pi, po, ps)

        body(*ci, *co, *cs)

        @pl.when(last)
        def _():
            carry.finish(pi, po, ps)

    res = pl.pallas_call(
        full_body, name=name, grid=grid, in_specs=list(in_specs) + [ANY] * n_pi,
        out_specs=tuple(out_specs) + tuple([ANY] * n_po), out_shape=tuple(out_shape) + tuple(carry.out_shapes),
        scratch_shapes=list(scratch) + carry.sems,
        input_output_aliases={n_ci + i: n_co + o for i, o in carry.aliases.items()},
        compiler_params=pltpu.CompilerParams(dimension_semantics=("arbitrary",) * len(grid),
                                             vmem_limit_bytes=vmem_limit),
    )(*args, *carry.operands)
    _set_results(carry, res[n_co:])
    return list(res[:n_co])


def _comm_call(carry, *, name):
    n_pi, n_po = len(carry.operands), len(carry.out_shapes)

    def body(*refs):
        pi, po, ps = refs[:n_pi], refs[n_pi:n_pi + n_po], refs[n_pi + n_po:]
        carry.start(pi, po, ps)
        carry.finish(pi, po, ps)

    res = pl.pallas_call(
        body, name=name, in_specs=[ANY] * n_pi, out_specs=tuple([ANY] * n_po), out_shape=tuple(carry.out_shapes),
        scratch_shapes=carry.sems, input_output_aliases=dict(carry.aliases),
    )(*carry.operands)
    _set_results(carry, res)


def _mm_nn(a, b, *, bm, bn, out_dtype, name, bias=None, carry=None):
    M, K = a.shape
    stacked = b.ndim == 3
    N = b.shape[0] * b.shape[2] if stacked else b.shape[1]
    assert M % bm == 0 and N % bn == 0 and (not stacked or bn == b.shape[2])

    def body(*refs):
        a_ref, b_ref = refs[0], refs[1]
        o_ref = refs[-1]
        acc = jnp.dot(a_ref[...], b_ref[...], preferred_element_type=F32)
        if bias is not None:
            acc = acc + refs[2][...]
        o_ref[...] = acc.astype(o_ref.dtype)

    if stacked:
        b_spec = pl.BlockSpec((None, K, bn), lambda i, j: (j, 0, 0))
    else:
        b_spec = pl.BlockSpec((K, bn), lambda i, j: (0, j))
    in_specs = [pl.BlockSpec((bm, K), lambda i, j: (i, 0)), b_spec]
    args = [a, b]
    if bias is not None:
        in_specs.append(pl.BlockSpec((1, bn), lambda i, j: (0, j)))
        args.append(bias)
    limit = _vmem_limit(_nbytes((bm, K), a.dtype), _nbytes((K, bn), b.dtype), _nbytes((bm, bn), F32))
    return _call(body, name=name, grid=(M // bm, N // bn), in_specs=in_specs,
                 out_specs=[pl.BlockSpec((bm, bn), lambda i, j: (i, j))],
                 out_shape=[jax.ShapeDtypeStruct((M, N), out_dtype)], args=args, vmem_limit=limit, carry=carry)[0]


def _mm_nt(a, b, *, bm, bn, out_dtype, name, add=None, carry=None):
    M, K = a.shape
    stacked = b.ndim == 3
    N = b.shape[1] if stacked else b.shape[0]
    n_sh = b.shape[0] if stacked else 1
    ks = b.shape[2] if stacked else K
    assert M % bm == 0 and N % bn == 0 and n_sh * ks == K
    dn = (((1,), (1,)), ((), ()))

    def body(*refs):
        a_ref, b_ref = refs[0], refs[1]
        o_ref = refs[-1]
        if stacked:
            acc = lax.dot_general(a_ref[:, 0:ks], b_ref[0], dn, preferred_element_type=F32)
            for s in range(1, n_sh):
                acc = acc + lax.dot_general(a_ref[:, s * ks:(s + 1) * ks], b_ref[s], dn, preferred_element_type=F32)
        else:
            acc = lax.dot_general(a_ref[...], b_ref[...], dn, preferred_element_type=F32)
        if add is not None:
            acc = acc + refs[2][...]
        o_ref[...] = acc.astype(o_ref.dtype)

    if stacked:
        b_spec = pl.BlockSpec((n_sh, bn, ks), lambda i, j: (0, j, 0))
    else:
        b_spec = pl.BlockSpec((bn, K), lambda i, j: (j, 0))
    in_specs = [pl.BlockSpec((bm, K), lambda i, j: (i, 0)), b_spec]
    args = [a, b]
    if add is not None:
        rows = bm if add.shape[0] == M else 1
        in_specs.append(pl.BlockSpec((rows, bn), lambda i, j: (i if rows == bm else 0, j)))
        args.append(add)
    limit = _vmem_limit(_nbytes((bm, K), a.dtype), _nbytes((bn, K), b.dtype), _nbytes((bm, bn), F32))
    return _call(body, name=name, grid=(M // bm, N // bn), in_specs=in_specs,
                 out_specs=[pl.BlockSpec((bm, bn), lambda i, j: (i, j))],
                 out_shape=[jax.ShapeDtypeStruct((M, N), out_dtype)], args=args, vmem_limit=limit, carry=carry)[0]


def _mm_tn(a, b, *, bm, bn, out_dtype, name, n_stack=None, carry=None):
    T, M = a.shape
    N = b.shape[1]
    assert M % bm == 0 and N % bn == 0 and (n_stack is None or bn * n_stack == N)
    dn = (((0,), (0,)), ((), ()))

    def body(a_ref, b_ref, o_ref):
        acc = lax.dot_general(a_ref[...], b_ref[...], dn, preferred_element_type=F32)
        o_ref[...] = acc.astype(o_ref.dtype)

    if n_stack is None:
        out_spec = pl.BlockSpec((bm, bn), lambda i, j: (i, j))
        out_shape = jax.ShapeDtypeStruct((M, N), out_dtype)
    else:
        out_spec = pl.BlockSpec((None, bm, bn), lambda i, j: (j, i, 0))
        out_shape = jax.ShapeDtypeStruct((n_stack, M, bn), out_dtype)
    limit = _vmem_limit(_nbytes((T, bm), a.dtype), _nbytes((T, bn), b.dtype), _nbytes((bm, bn), F32))
    return _call(body, name=name, grid=(M // bm, N // bn),
                 in_specs=[pl.BlockSpec((T, bm), lambda i, j: (0, i)), pl.BlockSpec((T, bn), lambda i, j: (0, j))],
                 out_specs=[out_spec], out_shape=[out_shape], args=[a, b], vmem_limit=limit, carry=carry)[0]


ROW_BLK = 256


def _rstd(x):
    return lax.rsqrt(jnp.mean(x * x, axis=-1, keepdims=True) + EPS)


def _norm_bwd(xhat, rstd, dxhat):
    return rstd * (dxhat - xhat * jnp.mean(dxhat * xhat, axis=-1, keepdims=True))


def _row_spec(cols):
    return pl.BlockSpec((ROW_BLK, cols), lambda i: (i, 0))


def _vec_spec(cols):
    return pl.BlockSpec((1, cols), lambda i: (0, 0))


def _prologue(x, g, rel_bias, idx_all, *, name, carry=None):
    T, D = x.shape
    n_pat = idx_all.shape[0]
    heads = 2 * N_PAIRS
    steps = n_pat * heads
    rows = T // steps

    def body(rb_ref, x_ref, g_ref, idx_ref, n_ref, bias_ref):
        s = pl.program_id(0)
        head = jnp.where(s < heads, s, heads + s % heads)
        xv = x_ref[...]
        n_ref[...] = (xv * _rstd(xv) * g_ref[...]).astype(n_ref.dtype)
        idxv = idx_ref[...]
        acc = jnp.where(idxv < 0, NEG, 0.0).astype(F32)
        for b in range(NUM_BUCKETS):
            acc = acc + jnp.where(idxv == b, rb_ref[b, head], 0.0)
        bias_ref[0] = acc
        kap = lax.broadcasted_iota(jnp.int32, (1, 2 * QBLK), 1)
        bias_ref[1] = jnp.where(kap < QBLK, NEG, acc)

    return _call(
        body, name=name, grid=(steps,),
        in_specs=[pl.BlockSpec(memory_space=pltpu.SMEM), pl.BlockSpec((rows, D), lambda s: (s, 0)),
                  pl.BlockSpec((1, D), lambda s: (0, 0)),
                  pl.BlockSpec((None, QBLK, 2 * QBLK), lambda s: (s // heads, 0, 0))],
        out_specs=[pl.BlockSpec((rows, D), lambda s: (s, 0)),
                   pl.BlockSpec((None, 2, None, QBLK, 2 * QBLK), lambda s: (s // heads, 0, s % heads, 0, 0))],
        out_shape=[jax.ShapeDtypeStruct((T, D), BF16),
                   jax.ShapeDtypeStruct((n_pat, 2, heads, QBLK, 2 * QBLK), F32)],
        args=[rel_bias, x, g, idx_all], carry=carry)


def _mm_resid_norm(a, b, h, g_post, coef, g_next, *, bm, name, bias=None, carry=None):
    M, K = a.shape
    N = b.shape[1]

    def body(*refs):
        a_ref, b_ref, h_ref, gp_ref, gn_ref = refs[0:5]
        f_ref, hn_ref, n_ref = refs[-3:]
        for rows in _row_halves(bm):
            f = jnp.dot(a_ref[rows, :], b_ref[...], preferred_element_type=F32)
            if bias is not None:
                f = f + refs[5][...]
            f_ref[rows, :] = f
            hn = h_ref[rows, :] + coef * (f * _rstd(f) * gp_ref[...])
            hn_ref[rows, :] = hn
            n_ref[rows, :] = (hn * _rstd(hn) * gn_ref[...]).astype(n_ref.dtype)

    row = lambda w: pl.BlockSpec((bm, w), lambda i: (i, 0))
    vec = pl.BlockSpec((1, N), lambda i: (0, 0))
    in_specs = [row(K), pl.BlockSpec((K, N), lambda i: (0, 0), pipeline_mode=pl.Buffered(1)), row(N), vec, vec]
    args = [a, b, h, g_post, g_next]
    if bias is not None:
        in_specs.append(vec)
        args.append(bias)
    limit = _vmem_limit(_nbytes((bm, K), a.dtype), _nbytes((K, N), b.dtype) // 2, 4 * _nbytes((bm, N), F32))
    return _call(body, name=name, grid=(M // bm,), in_specs=in_specs, out_specs=[row(N), row(N), row(N)],
                 out_shape=[jax.ShapeDtypeStruct((M, N), F32), jax.ShapeDtypeStruct((M, N), F32),
                            jax.ShapeDtypeStruct((M, N), BF16)], args=args, vmem_limit=limit, carry=carry)


def _mm_nt_norms_bwd(a, b, dh_in, h, g_pre, f, g_post, coef, *, bm, name, add=None, b_is_kn=False, carry=None):
    M, K = a.shape
    stacked = b.ndim == 3
    N = b.shape[1] if (stacked or b_is_kn) else b.shape[0]
    n_sh = b.shape[0] if stacked else 1
    ks = b.shape[2] if stacked else K
    assert n_sh * ks == K
    dn_dims = (((1,), (0,)), ((), ())) if b_is_kn else (((1,), (1,)), ((), ()))
    with_f = f is not None
    n_in = 5 + (2 if with_f else 0) + (1 if add is not None else 0)

    def body(*refs):
        a_ref, b_ref, dhi_ref, h_ref, gpre_ref = refs[0:5]
        outs = refs[n_in:]

        @pl.when(pl.program_id(0) == 0)
        def _():
            for acc in (outs[2:] if with_f else outs[1:]):
                acc[...] = jnp.zeros_like(acc)

        for rows in _row_halves(bm):
            if stacked:
                dn = lax.dot_general(a_ref[rows, 0:ks], b_ref[0], dn_dims, preferred_element_type=F32)
                for s in range(1, n_sh):
                    dn = dn + lax.dot_general(a_ref[rows, s * ks:(s + 1) * ks], b_ref[s], dn_dims,
                                              preferred_element_type=F32)
            else:
                dn = lax.dot_general(a_ref[rows, :], b_ref[...], dn_dims, preferred_element_type=F32)
            if add is not None:
                dn = dn + refs[n_in - 1][rows, :]
            hv = h_ref[rows, :]
            r = _rstd(hv)
            hh = hv * r
            dh = dhi_ref[rows, :] + _norm_bwd(hh, r, dn * gpre_ref[...])
            outs[0][rows, :] = dh
            if not with_f:
                outs[1][...] += jnp.sum(dn * hh, axis=0, keepdims=True)
                continue
            f_ref, gpost_ref = refs[5], refs[6]
            df_ref, dgpre_ref, dgpost_ref, dsum_ref = outs[1:]
            dgpre_ref[...] += jnp.sum(dn * hh, axis=0, keepdims=True)
            fv = f_ref[rows, :]
            rf = _rstd(fv)
            fh = fv * rf
            dy = coef * dh
            dgpost_ref[...] += jnp.sum(dy * fh, axis=0, keepdims=True)
            df = _norm_bwd(fh, rf, dy * gpost_ref[...])
            dsum_ref[...] += jnp.sum(df, axis=0, keepdims=True)
            df_ref[rows, :] = df.astype(df_ref.dtype)

    row = lambda w: pl.BlockSpec((bm, w), lambda i: (i, 0))
    vec = pl.BlockSpec((1, N), lambda i: (0, 0))
    once = pl.Buffered(1)
    if stacked:
        b_spec = pl.BlockSpec((n_sh, N, ks), lambda i: (0, 0, 0), pipeline_mode=once)
    else:
        b_spec = pl.BlockSpec(b.shape, lambda i: (0, 0), pipeline_mode=once)
    in_specs = [row(K), b_spec, row(N), row(N), vec]
    args = [a, b, dh_in, h, g_pre]
    if with_f:
        in_specs += [row(N), vec]
        args += [f, g_post]
    if add is not None:
        in_specs.append(row(N))
        args.append(add)
    vec_shape = jax.ShapeDtypeStruct((1, N), F32)
    out_specs = [row(N)] + ([row(N), vec, vec, vec] if with_f else [vec])
    out_shape = [jax.ShapeDtypeStruct((M, N), F32)]
    out_shape += [jax.ShapeDtypeStruct((M, N), BF16), vec_shape, vec_shape, vec_shape] if with_f else [vec_shape]
    limit = _vmem_limit(_nbytes((bm, K), a.dtype), _nbytes((N, K), b.dtype) // 2, 6 * _nbytes((bm, N), F32))
    return _call(body, name=name, grid=(M // bm,), in_specs=in_specs, out_specs=out_specs, out_shape=out_shape,
                 args=args, vmem_limit=limit, semantics=("arbitrary",), carry=carry)


def _ffn_up(n, w_gu, *, bm, name, carry=None):
    M, K = n.shape
    n_sh, _, ns = w_gu.shape
    half = n_sh // 2

    def body(n_ref, wg_ref, wu_ref, g_ref, u_ref, a_ref):
        nv = n_ref[...]
        for cols in _col_chunks(ns):
            g = jnp.dot(nv, wg_ref[:, cols], preferred_element_type=F32)
            u = jnp.dot(nv, wu_ref[:, cols], preferred_element_type=F32)
            g_ref[:, cols] = g.astype(g_ref.dtype)
            u_ref[:, cols] = u.astype(u_ref.dtype)
            a_ref[:, cols] = (g * jax.nn.sigmoid(g) * u).astype(a_ref.dtype)

    out_spec = pl.BlockSpec((bm, ns), lambda i, j: (i, j))
    out = jax.ShapeDtypeStruct((M, half * ns), BF16)
    limit = _vmem_limit(_nbytes((bm, K), n.dtype), 2 * _nbytes((K, ns), w_gu.dtype), 3 * _nbytes((bm, ns), F32))
    return _call(body, name=name, grid=(M // bm, half),
                 in_specs=[pl.BlockSpec((bm, K), lambda i, j: (i, 0)),
                           pl.BlockSpec((None, K, ns), lambda i, j: (j, 0, 0)),
                           pl.BlockSpec((None, K, ns), lambda i, j: (j + half, 0, 0))],
                 out_specs=[out_spec, out_spec, out_spec], out_shape=[out, out, out], args=[n, w_gu, w_gu],
                 vmem_limit=limit, carry=carry)


def _ffn_down_bwd(df, w_down, g, u, *, bm, name, carry=None):
    M, D = df.shape
    F = w_down.shape[0]
    dn = (((1,), (1,)), ((), ()))

    def body(df_ref, w_ref, g_ref, u_ref, o_ref):
        dfv = df_ref[...]
        for cols in _col_chunks(F):
            da = lax.dot_general(dfv, w_ref[cols, :], dn, preferred_element_type=F32)
            gv = g_ref[:, cols].astype(F32)
            s = jax.nn.sigmoid(gv)
            o_ref[:, cols] = (da * u_ref[:, cols].astype(F32) * (s * (1.0 + gv * (1.0 - s)))).astype(o_ref.dtype)
            o_ref[:, F + cols.start:F + cols.stop] = (da * (gv * s)).astype(o_ref.dtype)

    row = lambda w: pl.BlockSpec((bm, w), lambda i: (i, 0))
    limit = _vmem_limit(_nbytes((F, D), w_down.dtype) // 2, 2 * _nbytes((bm, F), BF16), _nbytes((bm, 2 * F), BF16))
    return _call(body, name=name, grid=(M // bm,),
                 in_specs=[row(D), pl.BlockSpec((F, D), lambda i: (0, 0), pipeline_mode=pl.Buffered(1)), row(F), row(F)],
                 out_specs=[row(2 * F)], out_shape=[jax.ShapeDtypeStruct((M, 2 * F), BF16)],
                 args=[df, w_down, g, u], vmem_limit=limit, carry=carry)[0]


def _ple_head(n4, w_gate, p, w_ple, h3, g_post, target, *, bm, name):
    T, D = h3.shape
    kp = p.shape[1]

    def body(n_ref, wg_ref, p_ref, wp_ref, h_ref, g_ref, t_ref, dh_ref, de_ref, dgp_ref, loss_ref, dg_ref):
        @pl.when(pl.program_id(0) == 0)
        def _():
            loss_ref[...] = jnp.zeros_like(loss_ref)
            dg_ref[...] = jnp.zeros_like(dg_ref)

        gpost = g_ref[...]
        for rows in _row_halves(bm):
            gate = jax.nn.sigmoid(jnp.dot(n_ref[rows, :], wg_ref[...], preferred_element_type=F32))
            ev = jnp.dot(p_ref[rows, :], wp_ref[...], preferred_element_type=F32)
            ge = gate * ev
            r = _rstd(ge)
            gh = ge * r
            diff = h_ref[rows, :] + gh * gpost - t_ref[rows, :]
            loss_ref[...] += jnp.sum(diff * diff) * (0.5 / D)
            dh = diff * (1.0 / D)
            dh_ref[rows, :] = dh
            dg_ref[...] += jnp.sum(dh * gh, axis=0, keepdims=True)
            dge = _norm_bwd(gh, r, dh * gpost)
            de_ref[rows, :] = (dge * gate).astype(de_ref.dtype)
            dgp_ref[rows, :] = (dge * ev * gate * (1.0 - gate)).astype(dgp_ref.dtype)

    row = lambda w: pl.BlockSpec((bm, w), lambda i: (i, 0))
    whole = lambda a: pl.BlockSpec(a.shape, lambda i: (0, 0), pipeline_mode=pl.Buffered(1))
    limit = _vmem_limit(_nbytes((bm, D), BF16), _nbytes(w_gate.shape, w_gate.dtype) // 2, 6 * _nbytes((bm, D), F32))
    return pl.pallas_call(
        body, name=name, grid=(T // bm,),
        in_specs=[row(D), whole(w_gate), row(kp), whole(w_ple), row(D), _vec_spec(D), row(D)],
        out_specs=(row(D), row(D), row(D), _vec_spec(LANES), _vec_spec(D)),
        out_shape=(jax.ShapeDtypeStruct((T, D), F32), jax.ShapeDtypeStruct((T, D), BF16),
                   jax.ShapeDtypeStruct((T, D), BF16), jax.ShapeDtypeStruct((1, LANES), F32),
                   jax.ShapeDtypeStruct((1, D), F32)),
        compiler_params=pltpu.CompilerParams(dimension_semantics=("arbitrary",), vmem_limit_bytes=limit),
    )(n4, w_gate, p, w_ple, h3, g_post, target)


def _t5_index(max_dist, stride):
    rho = np.arange(QBLK)[:, None]
    kap = np.arange(2 * QBLK)[None, :]
    d = rho + QBLK - kap
    valid = (d >= 0) & (d <= max_dist)
    n = np.maximum(d, 0) * stride
    max_exact = NUM_BUCKETS // 2
    nf = np.maximum(n, 1).astype(np.float32)
    large = max_exact + (np.log(nf / np.float32(max_exact)) / np.float32(math.log(MAX_DISTANCE / max_exact))
                         * np.float32(NUM_BUCKETS - max_exact)).astype(np.int32)
    large = np.minimum(large, NUM_BUCKETS - 1)
    bucket = np.where(n < max_exact, n, large)
    return np.where(valid, bucket, -1).astype(np.int32)


def _bias_grad(d_bias, idx, *, name):
    def body(db_ref, idx_ref, o_ref):
        h = pl.program_id(0)

        @pl.when(h == 0)
        def _():
            o_ref[...] = jnp.zeros_like(o_ref)

        idxv = idx_ref[...]
        dv = db_ref[0]
        rows = lax.broadcasted_iota(jnp.int32, (NUM_BUCKETS, LANES), 0)
        lanes = lax.broadcasted_iota(jnp.int32, (NUM_BUCKETS, LANES), 1)
        acc = o_ref[...]
        for b in range(NUM_BUCKETS):
            s = jnp.sum(jnp.where(idxv == b, dv, 0.0))
            acc = acc + jnp.where((rows == b) & (lanes == h), s, 0.0)
        o_ref[...] = acc

    return pl.pallas_call(
        body, name=name, grid=(2 * N_PAIRS,),
        in_specs=[pl.BlockSpec((1, QBLK, 2 * QBLK), lambda h: (h, 0, 0)),
                  pl.BlockSpec((QBLK, 2 * QBLK), lambda h: (0, 0))],
        out_specs=pl.BlockSpec((NUM_BUCKETS, LANES), lambda h: (0, 0)),
        out_shape=jax.ShapeDtypeStruct((NUM_BUCKETS, LANES), F32),
        compiler_params=pltpu.CompilerParams(dimension_semantics=("arbitrary",)),
    )(d_bias, idx)


def _lane():
    return lax.broadcasted_iota(jnp.int32, (1, LANES), 1)


def _head_sel(h):
    return (_lane() < HEAD_DIM) if h == 0 else (_lane() >= HEAD_DIM)


def _stat_lo():
    return (_lane() % HEAD_DIM) < (HEAD_DIM // 2)


def _stack_heads(t, scale=None):
    if scale is not None:
        t = t * scale
    zero = jnp.zeros_like(t)
    return jnp.concatenate([jnp.where(_head_sel(0), t, zero), jnp.where(_head_sel(1), t, zero)], axis=0)


def _class_spec(L, r, cols, stride, pps):
    chunks = N_PAIRS // pps
    mode = pl.Buffered(1) if r * chunks == 1 else None
    return pl.BlockSpec((L, MIX_W // chunks), lambda j, c: (0, (cols + j * stride) * chunks + c), pipeline_mode=mode)


def _rows(b, n_blocks=1):
    return pl.ds(pl.multiple_of(b * QBLK, QBLK), n_blocks * QBLK)


def _key_window(ref, b, cols, first, kv_head=None):
    if kv_head is not None:
        cols = slice(0, LANES)
    if first:
        blk = ref[0:QBLK, cols]
        tile = jnp.concatenate([blk, blk], axis=0)
    else:
        tile = ref[_rows(b - 1, 2), cols]
    if kv_head is None:
        return tile
    wide = tile.astype(F32)
    keep = jnp.logical_xor(_lane() < HEAD_DIM, kv_head == 1)
    return jnp.where(keep, wide, pltpu.roll(wide, HEAD_DIM, 1)).astype(tile.dtype)


def _kv_spec(L, r, col, stride, pps, kv_shared):
    if not kv_shared:
        return _class_spec(L, r, col, stride, pps)
    mode = pl.Buffered(1) if pps == N_PAIRS else None
    return pl.BlockSpec((L, LANES), lambda j, c: (0, col), pipeline_mode=mode)


def _band_fwd(qa, ka, va, bias, *, r, q_col, k_col, v_col, stride, pattern, name, kv_shared=False, sink=None,
              carry=None):
    L = qa.shape[0]
    nb = L // QBLK
    dn = (((1,), (1,)), ((), ()))
    scale = HEAD_DIM ** -0.5

    pps = N_PAIRS

    def body(q_ref, k_ref, v_ref, bias_ref, *rest):
        sel0 = _head_sel(0)
        pair0 = pl.program_id(1) * pps

        def emit(rows, cols, o, lse):
            if sink is None:
                o_ref, lse_ref = rest
                o_ref[rows, cols] = o.astype(o_ref.dtype)
                lse_ref[rows, cols] = lse
                return
            sink_ref, out_ref, outb_ref, lse_ref = rest
            sk = sink_ref[:, cols]
            mx = jnp.maximum(lse, sk)
            w = jnp.exp(lse - mx)
            den = w + jnp.exp(sk - mx)
            out = o * (w / den)
            out_ref[rows, cols] = out
            outb_ref[rows, cols] = out.astype(outb_ref.dtype)
            lse_ref[rows, cols] = mx + jnp.log(den)

        def block(b, first):
            rows = _rows(b)
            for i in range(pps):
                cols = slice(i * LANES, (i + 1) * LANES)
                q2 = _stack_heads(q_ref[rows, cols], scale)
                kv_head = (pair0 + i) // 2 if kv_shared else None
                k = _key_window(k_ref, b, cols, first, kv_head)
                v = _key_window(v_ref, b, cols, first, kv_head)
                bias2 = bias_ref[1 if first else 0, pl.ds(2 * (pair0 + i), 2)].reshape(2 * QBLK, 2 * QBLK)
                s = lax.dot_general(q2, k, dn, preferred_element_type=F32) + bias2
                m = jnp.max(s, axis=1, keepdims=True)
                p = jnp.exp(s - m)
                l = jnp.sum(p, axis=1, keepdims=True)
                o = jnp.dot(p.astype(v.dtype), v, preferred_element_type=F32) * (1.0 / l)
                lse = m + jnp.log(l)
                emit(rows, cols, jnp.where(sel0, o[0:QBLK], o[QBLK:]), jnp.where(sel0, lse[0:QBLK], lse[QBLK:]))

        block(0, True)
        if nb > 1:
            pl.loop(1, nb)(lambda b: block(b, False))

    bias_spec = pl.BlockSpec((None, 2, 2 * N_PAIRS, QBLK, 2 * QBLK), lambda j, c: (pattern, 0, 0, 0, 0))
    out_spec = _class_spec(L, r, 0, 1, pps)
    blk_bytes = _nbytes((L, pps * LANES), F32)
    in_specs = [_class_spec(L, r, q_col, stride, pps), _kv_spec(L, r, k_col, stride, pps, kv_shared),
                _kv_spec(L, r, v_col, stride, pps, kv_shared), bias_spec]
    args = [qa, ka, va, bias]
    f32_out, bf16_out = jax.ShapeDtypeStruct((L, r * MIX_W), F32), jax.ShapeDtypeStruct((L, r * MIX_W), BF16)
    out_shape = [bf16_out, f32_out]
    if sink is not None:
        in_specs.append(pl.BlockSpec((1, MIX_W), lambda j, c: (0, 0)))
        args.append(sink)
        out_shape = [f32_out, bf16_out, f32_out]
    return _call(body, name=name, grid=(r, N_PAIRS // pps), in_specs=in_specs, out_specs=[out_spec] * len(out_shape),
                 out_shape=out_shape, args=args, vmem_limit=_vmem_limit(*([blk_bytes] * 4)), carry=carry)


def _class_rows_spec(rows, r, width):
    return pl.BlockSpec((rows // r, r * width), lambda i, *_: (i, 0))


def _token_scratch(rows, width):
    return pltpu.VMEM((width // LANES, rows, LANES), F32)


def _fill_tokens(scratch, value):
    for c in range(scratch.shape[0]):
        scratch[c] = value[:, c * LANES:(c + 1) * LANES]


def _read_tokens(scratch):
    return jnp.concatenate([scratch[c] for c in range(scratch.shape[0])], axis=1)


def _to_tokens(blk_ref, r, scratch):
    if r == 1:
        return blk_ref[...].astype(F32)
    nt, rows, _ = scratch.shape
    for j in range(r):
        for c in range(nt):
            lanes = slice((j * nt + c) * LANES, (j * nt + c + 1) * LANES)
            scratch.at[c][pl.ds(j, rows // r, stride=r), :] = blk_ref[:, lanes].astype(F32)
    return _read_tokens(scratch)


def _from_tokens(dst_ref, r, scratch):
    nt, rows, _ = scratch.shape
    if r == 1:
        dst_ref[...] = _read_tokens(scratch).astype(dst_ref.dtype)
        return
    for j in range(r):
        for c in range(nt):
            lanes = slice((j * nt + c) * LANES, (j * nt + c + 1) * LANES)
            dst_ref[:, lanes] = scratch.at[c][pl.ds(j, rows // r, stride=r), :].astype(dst_ref.dtype)


def _mm_nt_classes(a, b, bias, dils, *, bm, name):
    M, K = a.shape
    N = b.shape[0]
    dn = (((1,), (1,)), ((), ()))

    def body(a_ref, b_ref, bias_ref, *rest):
        outs, tile = rest[:-1], rest[-1]
        _fill_tokens(tile, lax.dot_general(a_ref[...], b_ref[...], dn, preferred_element_type=F32) + bias_ref[...])
        for out, r in zip(outs, dils):
            _from_tokens(out, r, tile)

    limit = _vmem_limit(_nbytes((bm, K), a.dtype), _nbytes((K, N), b.dtype) // 2, (1 + len(dils)) * _nbytes((bm, N), F32))
    return pl.pallas_call(
        body, name=name, grid=(M // bm,),
        in_specs=[pl.BlockSpec((bm, K), lambda i: (i, 0)),
                  pl.BlockSpec((N, K), lambda i: (0, 0), pipeline_mode=pl.Buffered(1)),
                  pl.BlockSpec((1, N), lambda i: (0, 0))],
        out_specs=tuple(_class_rows_spec(bm, r, N) for r in dils),
        out_shape=tuple(jax.ShapeDtypeStruct((M // r, r * N), BF16) for r in dils),
        scratch_shapes=[_token_scratch(bm, N)],
        compiler_params=pltpu.CompilerParams(dimension_semantics=("parallel",), vmem_limit_bytes=limit),
    )(a, b, bias)


def _merge(branches, dils, sink, *, name):
    W = MIX_W
    T = branches[0][0].shape[0] * dils[0]
    nbr = len(branches)

    def body(*refs):
        sink_ref = refs[2 * nbr]
        out_ref, outb_ref, lse_ref, scratch = refs[2 * nbr + 1:]
        sk = sink_ref[...]
        lses = [_to_tokens(refs[2 * t + 1], dils[t], scratch) for t in range(nbr)]
        mx = sk
        for lse in lses:
            mx = jnp.maximum(mx, lse)
        den = jnp.exp(sk - mx)
        num = jnp.zeros_like(mx)
        for t in range(nbr):
            w = jnp.exp(lses[t] - mx)
            den = den + w
            num = num + w * _to_tokens(refs[2 * t], dils[t], scratch)
        out = num / den
        out_ref[...] = out
        outb_ref[...] = out.astype(outb_ref.dtype)
        lse_ref[...] = mx + jnp.log(den)

    args = [a for br in branches for a in br] + [sink]
    in_specs = [_class_rows_spec(ROW_BLK, r, W) for r in dils for _ in range(2)] + [_vec_spec(W)]
    return pl.pallas_call(
        body, name=name, grid=(T // ROW_BLK,), in_specs=in_specs,
        out_specs=(_row_spec(W), _row_spec(W), _row_spec(W)),
        out_shape=(jax.ShapeDtypeStruct((T, W), F32), jax.ShapeDtypeStruct((T, W), BF16),
                   jax.ShapeDtypeStruct((T, W), F32)),
        scratch_shapes=[_token_scratch(ROW_BLK, W)],
        compiler_params=pltpu.CompilerParams(dimension_semantics=("parallel",)),
    )(*args)


def _attn_delta(dmix, outs, lses, sink, dils, *, name):
    T = dmix.shape[0]
    W = MIX_W
    nd = len(dils)

    def body(dmix_ref, oa_ref, ob_ref, la_ref, lb_ref, sink_ref, *rest):
        doa_ref, sta_ref, dsink_ref = rest[0:3]
        dob_refs, stb_refs = rest[3:3 + nd], rest[3 + nd:3 + 2 * nd]
        do_tile, st_tile = rest[3 + 2 * nd:]

        @pl.when(pl.program_id(0) == 0)
        def _():
            dsink_ref[...] = jnp.zeros_like(dsink_ref)

        sel0, lo = _head_sel(0), _stat_lo()
        for mixer, (o_ref, l_ref) in enumerate(((oa_ref, la_ref), (ob_ref, lb_ref))):
            for i in range(N_PAIRS):
                cols = slice(i * LANES, (i + 1) * LANES)
                do = dmix_ref[:, mixer * W + i * LANES:mixer * W + (i + 1) * LANES]
                prod = do * o_ref[:, cols]
                d0 = jnp.sum(jnp.where(sel0, prod, 0.0), axis=1, keepdims=True)
                d1 = jnp.sum(jnp.where(sel0, 0.0, prod), axis=1, keepdims=True)
                delta = jnp.where(sel0, d0, d1)
                lse = l_ref[:, cols]
                stat = jnp.where(lo, lse, delta)
                if mixer == 0:
                    sta_ref[:, cols] = stat
                    doa_ref[:, cols] = do.astype(doa_ref.dtype)
                    dsink_ref[:, cols] += -jnp.sum(jnp.exp(sink_ref[:, cols] - lse) * delta, axis=0, keepdims=True)
                else:
                    st_tile[i] = stat
                    do_tile[i] = do
        for t, r in enumerate(dils):
            _from_tokens(dob_refs[t], r, do_tile)
            _from_tokens(stb_refs[t], r, st_tile)

    class_specs = [_class_rows_spec(ROW_BLK, r, W) for r in dils]
    out_shape = [jax.ShapeDtypeStruct((T, W), BF16), jax.ShapeDtypeStruct((T, W), F32), jax.ShapeDtypeStruct((1, W), F32)]
    out_shape += [jax.ShapeDtypeStruct((T // r, r * W), BF16) for r in dils]
    out_shape += [jax.ShapeDtypeStruct((T // r, r * W), F32) for r in dils]
    res = pl.pallas_call(
        body, name=name, grid=(T // ROW_BLK,),
        in_specs=[_row_spec(2 * W), _row_spec(W), _row_spec(W), _row_spec(W), _row_spec(W), _vec_spec(W)],
        out_specs=tuple([_row_spec(W), _row_spec(W), _vec_spec(W)] + class_specs + class_specs),
        out_shape=tuple(out_shape),
        scratch_shapes=[_token_scratch(ROW_BLK, W), _token_scratch(ROW_BLK, W)],
        compiler_params=pltpu.CompilerParams(dimension_semantics=("arbitrary",)),
    )(dmix, outs[0], outs[1], lses[0], lses[1], sink)
    return res[0], res[1], res[2], res[3:3 + nd], res[3 + nd:]


def _band_bwd(qa, ka, va, d_out, stat, bias, *, r, q_col, k_col, v_col, stride, pattern, name, kv_shared=False,
              carry=None):
    L = qa.shape[0]
    nb = L // QBLK
    dn_nt = (((1,), (1,)), ((), ()))
    dn_tn = (((0,), (0,)), ((), ()))
    scale = HEAD_DIM ** -0.5

    pps = N_PAIRS if r > 1 else N_PAIRS // 2

    def body(q_ref, k_ref, v_ref, do_ref, st_ref, bias_ref, dq_ref, dk_ref, dv_ref, db_ref, dk_carry, dv_carry):
        @pl.when((pl.program_id(0) == 0) & (pl.program_id(1) == 0))
        def _():
            db_ref[...] = jnp.zeros_like(db_ref)

        lo, sel0 = _stat_lo(), _head_sel(0)
        pair0 = pl.program_id(1) * pps

        def block(b, first):
            rows = _rows(b)
            for i in range(pps):
                heads = pl.ds(2 * (pair0 + i), 2)
                cols = slice(i * LANES, (i + 1) * LANES)
                q2 = _stack_heads(q_ref[rows, cols], scale)
                kv_head = (pair0 + i) // 2 if kv_shared else None
                k = _key_window(k_ref, b, cols, first, kv_head)
                v = _key_window(v_ref, b, cols, first, kv_head)
                do2 = _stack_heads(do_ref[rows, cols])
                st = st_ref[rows, cols]
                lse2 = jnp.concatenate(
                    [jnp.max(jnp.where(_head_sel(h) & lo, st, -jnp.inf), axis=1, keepdims=True) for h in range(2)], axis=0)
                delta2 = jnp.concatenate(
                    [jnp.sum(jnp.where(_head_sel(h) & ~lo, st, 0.0), axis=1, keepdims=True) for h in range(2)],
                    axis=0) * (2.0 / HEAD_DIM)
                bias2 = bias_ref[1 if first else 0, heads].reshape(2 * QBLK, 2 * QBLK)
                s = lax.dot_general(q2, k, dn_nt, preferred_element_type=F32) + bias2
                p = jnp.exp(s - lse2)
                dp = lax.dot_general(do2, v, dn_nt, preferred_element_type=F32)
                ds = p * (dp - delta2)
                db_ref[heads] += ds.reshape(2, QBLK, 2 * QBLK)
                dsb = ds.astype(k.dtype)
                dq2 = jnp.dot(dsb, k, preferred_element_type=F32)
                dq_ref[rows, cols] = (jnp.where(sel0, dq2[0:QBLK], dq2[QBLK:]) * scale).astype(dq_ref.dtype)
                dk_acc = lax.dot_general(dsb, q2, dn_tn, preferred_element_type=F32)
                dv_acc = lax.dot_general(p.astype(k.dtype), do2, dn_tn, preferred_element_type=F32)
                if not first:
                    prev = _rows(b - 1)
                    dk_ref[prev, cols] = (dk_carry[:, cols] + dk_acc[0:QBLK]).astype(dk_ref.dtype)
                    dv_ref[prev, cols] = (dv_carry[:, cols] + dv_acc[0:QBLK]).astype(dv_ref.dtype)
                dk_carry[:, cols] = dk_acc[QBLK:2 * QBLK]
                dv_carry[:, cols] = dv_acc[QBLK:2 * QBLK]

        block(0, True)
        if nb > 1:
            pl.loop(1, nb)(lambda b: block(b, False))

        last = _rows(nb - 1)
        dk_ref[last, :] = dk_carry[...].astype(dk_ref.dtype)
        dv_ref[last, :] = dv_carry[...].astype(dv_ref.dtype)

    tok_spec = _class_spec(L, r, 0, 1, pps)
    bias_spec = pl.BlockSpec((None, 2, 2 * N_PAIRS, QBLK, 2 * QBLK), lambda j, c: (pattern, 0, 0, 0, 0))
    dbias_spec = pl.BlockSpec((2 * N_PAIRS, QBLK, 2 * QBLK), lambda j, c: (0, 0, 0))
    grad = jax.ShapeDtypeStruct((L, r * MIX_W), BF16)
    blk_bytes = _nbytes((L, pps * LANES), BF16)
    carry_shape = pltpu.VMEM((QBLK, pps * LANES), F32)
    return _call(
        body, name=name, grid=(r, N_PAIRS // pps),
        in_specs=[_class_spec(L, r, q_col, stride, pps), _kv_spec(L, r, k_col, stride, pps, kv_shared),
                  _kv_spec(L, r, v_col, stride, pps, kv_shared), tok_spec, tok_spec, bias_spec],
        out_specs=[tok_spec, tok_spec, tok_spec, dbias_spec],
        out_shape=[grad, grad, grad, jax.ShapeDtypeStruct((2 * N_PAIRS, QBLK, 2 * QBLK), F32)],
        args=[qa, ka, va, d_out, stat, bias], scratch=[carry_shape, carry_shape],
        vmem_limit=_vmem_limit(*([blk_bytes] * 9)), semantics=("arbitrary", "arbitrary"), carry=carry)


def _dz_assemble(dq_a, dk_a, dv_a, b_grads, dils, *, name):
    T = dq_a.shape[0]
    W = MIX_W

    def group_sum(x):
        u0 = x[:, 0:LANES] + x[:, LANES:2 * LANES]
        u1 = x[:, 2 * LANES:3 * LANES] + x[:, 3 * LANES:4 * LANES]
        s0 = u0 + pltpu.roll(u0, HEAD_DIM, 1)
        s1 = u1 + pltpu.roll(u1, HEAD_DIM, 1)
        return jnp.where(_head_sel(0), s0, s1)

    def body(*refs):
        dqa_ref, dka_ref, dva_ref = refs[0:3]
        b_refs = refs[3:12]
        dza_ref, dzb_ref, sa_ref, sb_ref, scratch = refs[12:]

        @pl.when(pl.program_id(0) == 0)
        def _():
            sa_ref[...] = jnp.zeros_like(sa_ref)
            sb_ref[...] = jnp.zeros_like(sb_ref)

        def put(dst, acc, col, part):
            w = part.shape[1]
            dst[:, col:col + w] = part.astype(dst.dtype)
            acc[:, col:col + w] += jnp.sum(part, axis=0, keepdims=True)

        put(dza_ref, sa_ref, 0, dqa_ref[...].astype(F32))
        put(dza_ref, sa_ref, W, group_sum(dka_ref[...].astype(F32)))
        put(dza_ref, sa_ref, W + LANES, group_sum(dva_ref[...].astype(F32)))
        for t in range(3):
            part = _to_tokens(b_refs[t], dils[0], scratch)
            for pat in range(1, len(dils)):
                part = part + _to_tokens(b_refs[3 * pat + t], dils[pat], scratch)
            put(dzb_ref, sb_ref, t * W, part)

    args = [dq_a, dk_a, dv_a] + [g[t] for g in b_grads for t in range(3)]
    return pl.pallas_call(
        body, name=name, grid=(T // ROW_BLK,),
        in_specs=[_row_spec(W)] * 3 + [_class_rows_spec(ROW_BLK, r, W) for r in dils for _ in range(3)],
        out_specs=(_row_spec(ZA_W), _row_spec(ZB_W), _vec_spec(ZA_W), _vec_spec(ZB_W)),
        out_shape=(jax.ShapeDtypeStruct((T, ZA_W), BF16), jax.ShapeDtypeStruct((T, ZB_W), BF16),
                   jax.ShapeDtypeStruct((1, ZA_W), F32), jax.ShapeDtypeStruct((1, ZB_W), F32)),
        scratch_shapes=[_token_scratch(ROW_BLK, W)],
        compiler_params=pltpu.CompilerParams(dimension_semantics=("arbitrary",)),
    )(*args)


def _place():
    x, y, c = lax.axis_index("x"), lax.axis_index("y"), lax.axis_index("c")
    chips = [(1 - x, y), (x, 1 - y), (1 - x, 1 - y)]
    return x, y, c, chips


def _cast_place(shards, k_idx, *, name):
    n, steps = len(shards), 4

    def body(k_ref, *refs):
        for s_ref, o_ref in zip(refs[:n], refs[n:]):
            o_ref[...] = s_ref[...].astype(o_ref.dtype)

    blocks = [(a.shape[0] // steps, a.shape[1]) for a in shards]
    grid_spec = pltpu.PrefetchScalarGridSpec(
        num_scalar_prefetch=1, grid=(steps,),
        in_specs=[pl.BlockSpec(blk, lambda t, k_ref: (t, 0)) for blk in blocks],
        out_specs=tuple(pl.BlockSpec(blk, lambda t, k_ref: (k_ref[0] * steps + t, 0)) for blk in blocks))
    return pl.pallas_call(
        body, name=name, grid_spec=grid_spec,
        out_shape=tuple(jax.ShapeDtypeStruct((N_CHIPS * a.shape[0], a.shape[1]), BF16) for a in shards),
        compiler_params=pltpu.CompilerParams(dimension_semantics=("parallel",),
                                             vmem_limit_bytes=_vmem_limit(*[_nbytes(b, F32) for b in blocks])),
    )(k_idx, *shards)


def _gather_carry(fulls, part=(0, 1)):
    n = len(fulls)
    p_idx, n_parts = part

    def piece(ref, chip_idx, half):
        rs = ref.shape[0] // N_CHIPS
        rh = rs // 2
        rows = rh // n_parts
        return ref.at[pl.ds(chip_idx * rs + half * rh + p_idx * rows, rows)]

    def copy(sems, sem, src_ref, dst_ref, to):
        return pltpu.make_async_remote_copy(src_ref=src_ref, dst_ref=dst_ref, send_sem=sems[0].at[sem],
                                            recv_sem=sems[1].at[sem], device_id=to, device_id_type=MESH)

    def ici_copy(ins, outs, sems, i, j, chip, k, c):
        return copy(sems, 3 * i + j, piece(ins[i], k, c), piece(outs[i], k, c), (*chip, c))

    def start(ins, outs, sems):
        x, y, c, chips = _place()
        for i in range(n):
            for j, chip in enumerate(chips):
                ici_copy(ins, outs, sems, i, j, chip, 2 * x + y, c).start()

    def finish(ins, outs, sems):
        x, y, c, chips = _place()
        sibling = (x, y, 1 - c)
        passed = []
        for i in range(n):
            for j, chip in enumerate(chips):
                region = piece(outs[i], 2 * chip[0] + chip[1], c)
                copy(sems, 3 * i + j, region, region, (*chip, c)).wait_recv()
                cp = copy(sems, 3 * n + 3 * i + j, region, region, sibling)
                cp.start()
                passed.append(cp)
        for i in range(n):
            for j, chip in enumerate(chips):
                region = piece(outs[i], 2 * chip[0] + chip[1], 1 - c)
                copy(sems, 3 * n + 3 * i + j, region, region, sibling).wait_recv()
        for i in range(n):
            for j, chip in enumerate(chips):
                ici_copy(ins, outs, sems, i, j, chip, 2 * x + y, c).wait_send()
        for cp in passed:
            cp.wait_send()

    return _Carry(fulls, [jax.ShapeDtypeStruct(a.shape, a.dtype) for a in fulls], {i: i for i in range(n)},
                  [pltpu.SemaphoreType.DMA((6 * n,)), pltpu.SemaphoreType.DMA((6 * n,))], start, finish)


def _pair_swap_carry(grads):
    n = len(grads)

    def copy(ins, outs, sems, i):
        x, y, c, _ = _place()
        return pltpu.make_async_remote_copy(src_ref=ins[i].at[:, 1 - c], dst_ref=outs[i], send_sem=sems[0].at[i],
                                            recv_sem=sems[1].at[i], device_id=(x, y, 1 - c), device_id_type=MESH)

    def start(ins, outs, sems):
        for i in range(n):
            copy(ins, outs, sems, i).start()

    def finish(ins, outs, sems):
        for i in range(n):
            copy(ins, outs, sems, i).wait()

    return _Carry(grads, [jax.ShapeDtypeStruct((a.shape[0], a.shape[2], a.shape[3]), a.dtype) for a in grads], {},
                  [pltpu.SemaphoreType.DMA((n,)), pltpu.SemaphoreType.DMA((n,))], start, finish)


def _pair_sum(g4, got, c_idx, *, name):
    _, _, rh, cs = g4.shape
    rb = _row_block(rh, 512, 16)

    def body(c_ref, own_ref, got_ref, o_ref):
        o_ref[...] = (own_ref[...].astype(F32) + got_ref[...].astype(F32)).astype(o_ref.dtype)

    grid_spec = pltpu.PrefetchScalarGridSpec(
        num_scalar_prefetch=1, grid=(N_CHIPS, rh // rb),
        in_specs=[pl.BlockSpec((None, None, rb, cs), lambda k, t, c_ref: (k, c_ref[0], t, 0)),
                  pl.BlockSpec((None, rb, cs), lambda k, t, c_ref: (k, t, 0))],
        out_specs=pl.BlockSpec((None, rb, cs), lambda k, t, c_ref: (k, t, 0)))
    return pl.pallas_call(
        body, name=name, grid_spec=grid_spec, out_shape=jax.ShapeDtypeStruct((N_CHIPS, rh, cs), BF16),
        compiler_params=pltpu.CompilerParams(dimension_semantics=("parallel", "parallel")),
    )(c_idx, g4, got)


def _chip_scatter_carry(sums):
    n = len(sums)

    def copies(ins, outs, sems):
        x, y, c, chips = _place()
        return [pltpu.make_async_remote_copy(src_ref=ins[i].at[2 * chip[0] + chip[1]], dst_ref=outs[i].at[j],
                                             send_sem=sems[0].at[3 * i + j], recv_sem=sems[1].at[3 * i + j],
                                             device_id=(*chip, c), device_id_type=MESH)
                for i in range(n) for j, chip in enumerate(chips)]

    def start(ins, outs, sems):
        for cp in copies(ins, outs, sems):
            cp.start()

    def finish(ins, outs, sems):
        for cp in copies(ins, outs, sems):
            cp.wait()

    return _Carry(sums, [jax.ShapeDtypeStruct((3,) + a.shape[1:], a.dtype) for a in sums], {},
                  [pltpu.SemaphoreType.DMA((3 * n,)), pltpu.SemaphoreType.DMA((3 * n,))], start, finish)


def _chip_sum(sums, gots, kc_idx, *, name):
    n, steps = len(sums), 2

    def body(kc_ref, *refs):
        for own_ref, got_ref, o_ref in zip(refs[:n], refs[n:2 * n], refs[2 * n:]):
            acc = own_ref[...].astype(F32)
            for j in range(3):
                acc = acc + got_ref[j].astype(F32)
            o_ref[...] = acc

    blocks = [(a.shape[1] // steps, a.shape[2]) for a in sums]
    grid_spec = pltpu.PrefetchScalarGridSpec(
        num_scalar_prefetch=1, grid=(steps,),
        in_specs=[pl.BlockSpec((None,) + blk, lambda t, kc: (kc[0], t, 0)) for blk in blocks]
        + [pl.BlockSpec((3,) + blk, lambda t, kc: (0, t, 0)) for blk in blocks],
        out_specs=tuple(pl.BlockSpec(blk, lambda t, kc: (kc[1] * steps + t, 0)) for blk in blocks))
    return pl.pallas_call(
        body, name=name, grid_spec=grid_spec,
        out_shape=tuple(jax.ShapeDtypeStruct((2 * a.shape[1], a.shape[2]), F32) for a in sums),
        compiler_params=pltpu.CompilerParams(dimension_semantics=("parallel",),
                                             vmem_limit_bytes=_vmem_limit(*[3 * _nbytes(b, F32) for b in blocks])),
    )(kc_idx, *sums, *gots)


def _half_swap_carry(shards):
    n = len(shards)

    def copy(ins, outs, sems, i, to_my_half):
        x, y, c, _ = _place()
        rh = ins[i].shape[0] // 2
        half = c if to_my_half else 1 - c
        return pltpu.make_async_remote_copy(
            src_ref=ins[i].at[pl.ds(c * rh, rh)], dst_ref=outs[i].at[pl.ds(half * rh, rh)],
            send_sem=sems[0].at[i], recv_sem=sems[1].at[i], device_id=(x, y, 1 - c), device_id_type=MESH)

    def start(ins, outs, sems):
        for i in range(n):
            copy(ins, outs, sems, i, True).start()

    def finish(ins, outs, sems):
        for i in range(n):
            copy(ins, outs, sems, i, False).wait_recv()
        for i in range(n):
            copy(ins, outs, sems, i, True).wait_send()

    return _Carry(shards, [jax.ShapeDtypeStruct(a.shape, a.dtype) for a in shards], {i: i for i in range(n)},
                  [pltpu.SemaphoreType.DMA((n,)), pltpu.SemaphoreType.DMA((n,))], start, finish)


SMALL_ROW = 1024


def _small_layout(shapes):
    starts, row = [], 0
    for r, c in shapes:
        starts.append(row)
        row += -(-c // SMALL_ROW) if r == 1 else r
    return starts, -(-row // SUBLANES) * SUBLANES


def _small_pieces(shape, start):
    r, c = shape
    if r == 1:
        return [(start + q, min(SMALL_ROW, c - q * SMALL_ROW), q * SMALL_ROW) for q in range(-(-c // SMALL_ROW))]
    return None


def _whole_spec(shape):
    return pl.BlockSpec(tuple(shape), lambda i: (0,) * len(shape))


def _small_all_reduce(parts, *, name):
    shapes = [a.shape for a in parts]
    starts, rows = _small_layout(shapes)
    n = len(parts)
    n_tables = sum(1 for r, _ in shapes if r > 1)
    assert all(r > 1 and c <= LANES for r, c in shapes[:n_tables]) and all(r == 1 for r, _ in shapes[n_tables:])
    table_rows = starts[n_tables]
    assert table_rows % SUBLANES == 0

    def regions(slot):
        return slot.at[pl.ds(0, table_rows), pl.ds(0, LANES)], slot.at[pl.ds(table_rows, rows - table_rows)]

    def body(*refs):
        ins, outs = refs[0:n], refs[n:2 * n]
        buf, send_sems, recv_sems = refs[2 * n:]
        x, y, c, _ = _place()
        me = 4 * x + 2 * y + c
        mine = buf.at[me]
        mine[...] = jnp.zeros((rows, SMALL_ROW), F32)
        for ref, shape, start in zip(ins, shapes, starts):
            pieces = _small_pieces(shape, start)
            if pieces is None:
                mine[start:start + shape[0], 0:shape[1]] = ref[...]
            else:
                for row, width, col in pieces:
                    mine[row:row + 1, 0:width] = ref[:, col:col + width]
        def peer(d):
            return 1 - x if d & 4 else x, 1 - y if d & 2 else y, 1 - c if d & 1 else c

        def copy(d, part, dst_slot):
            sem = 2 * (d - 1) + part
            return pltpu.make_async_remote_copy(
                src_ref=regions(mine)[part], dst_ref=regions(dst_slot)[part], send_sem=send_sems.at[sem],
                recv_sem=recv_sems.at[sem], device_id=peer(d), device_id_type=MESH)

        for d in range(1, 8):
            for part in range(2):
                copy(d, part, mine).start()
        for d in range(1, 8):
            px, py, pc = peer(d)
            for part in range(2):
                copy(d, part, buf.at[4 * px + 2 * py + pc]).wait_recv()
        for d in range(1, 8):
            for part in range(2):
                copy(d, part, mine).wait_send()
        tables, vectors = regions(buf.at[0])
        acc_t, acc_v = tables[...], vectors[...]
        for s in range(1, 8):
            tables, vectors = regions(buf.at[s])
            acc_t, acc_v = acc_t + tables[...], acc_v + vectors[...]
        tables, vectors = regions(mine)
        tables[...] = acc_t
        vectors[...] = acc_v
        for ref, shape, start in zip(outs, shapes, starts):
            pieces = _small_pieces(shape, start)
            if pieces is None:
                ref[...] = mine[start:start + shape[0], 0:shape[1]]
            else:
                for row, width, col in pieces:
                    ref[:, col:col + width] = mine[row:row + 1, 0:width]

    whole = [_whole_spec(s) for s in shapes]
    return pl.pallas_call(
        body, name=name, grid=(1,), in_specs=whole, out_specs=tuple(whole),
        out_shape=tuple(jax.ShapeDtypeStruct(s, F32) for s in shapes),
        scratch_shapes=[pltpu.VMEM((8, rows, SMALL_ROW), F32), pltpu.SemaphoreType.DMA((14,)),
                        pltpu.SemaphoreType.DMA((14,))],
    )(*parts)


def _adamw_small(ws, gs, ms, vs, *, name):
    n = len(ws)
    c1 = 1.0 - ADAM_B1 ** ADAM_STEP
    c2 = 1.0 - ADAM_B2 ** ADAM_STEP

    def body(*refs):
        for k in range(n):
            w_ref, g_ref, m_ref, v_ref = (refs[t * n + k] for t in range(4))
            go_ref, d_ref, mo_ref, vo_ref = (refs[(4 + t) * n + k] for t in range(4))
            gv = g_ref[...]
            go_ref[...] = gv
            mn = ADAM_B1 * m_ref[...] + (1.0 - ADAM_B1) * gv
            vn = ADAM_B2 * v_ref[...] + (1.0 - ADAM_B2) * (gv * gv)
            d_ref[...] = -ADAM_LR * ((mn / c1) / (jnp.sqrt(vn / c2) + ADAM_EPS) + ADAM_WD * w_ref[...])
            mo_ref[...] = mn
            vo_ref[...] = vn

    whole = [_whole_spec(a.shape) for a in ws]
    shapes = tuple(jax.ShapeDtypeStruct(a.shape, F32) for a in ws)
    res = pl.pallas_call(
        body, name=name, grid=(1,), in_specs=whole * 4, out_specs=tuple(whole * 4), out_shape=shapes * 4,
    )(*ws, *gs, *ms, *vs)
    return res[0:n], res[n:2 * n], res[2 * n:3 * n], res[3 * n:4 * n]


def _adamw(ws, gs, ms, vs, *, name):
    n, steps = len(ws), 8
    c1 = 1.0 - ADAM_B1 ** ADAM_STEP
    c2 = 1.0 - ADAM_B2 ** ADAM_STEP

    def body(*refs):
        for k in range(n):
            w_ref, g_ref, m_ref, v_ref = (refs[t * n + k] for t in range(4))
            go_ref, d_ref, mo_ref, vo_ref = (refs[(4 + t) * n + k] for t in range(4))
            gv = g_ref[...]
            go_ref[...] = gv
            mn = ADAM_B1 * m_ref[...] + (1.0 - ADAM_B1) * gv
            vn = ADAM_B2 * v_ref[...] + (1.0 - ADAM_B2) * (gv * gv)
            d_ref[...] = -ADAM_LR * ((mn / c1) / (jnp.sqrt(vn / c2) + ADAM_EPS) + ADAM_WD * w_ref[...])
            mo_ref[...] = mn
            vo_ref[...] = vn

    blocks = [(a.shape[0] // steps, a.shape[1]) for a in ws]
    specs = [pl.BlockSpec(blk, lambda i: (i, 0)) for blk in blocks]
    shapes = tuple(jax.ShapeDtypeStruct(a.shape, F32) for a in ws)
    res = pl.pallas_call(
        body, name=name, grid=(steps,), in_specs=specs * 4, out_specs=tuple(specs * 4), out_shape=shapes * 4,
        compiler_params=pltpu.CompilerParams(dimension_semantics=("parallel",),
                                             vmem_limit_bytes=_vmem_limit(*[8 * _nbytes(b, F32) for b in blocks])),
    )(*ws, *gs, *ms, *vs)
    return res[0:n], res[n:2 * n], res[2 * n:3 * n], res[3 * n:4 * n]


def _local_step(x, p, target, small, sched):
    T = x.shape[0]
    W = MIX_W
    rel_bias = small["rel_bias"]
    b_in_a, b_in_b = small["b_in"][:, 0:ZA_W], small["b_in"][:, ZA_W:]

    idx_np = [_t5_index(A_WINDOW - 1, 1)] + [_t5_index(window // dil, dil) for window, dil in B_PATTERNS]
    idx_all = jnp.asarray(np.stack(idx_np))
    n1, bias_all = sched.run(_prologue, x, small["ffn1_pre_g"], rel_bias, idx_all, name="prologue")
    w_gu1 = sched.weight("ffn1_w_gu")
    *gu1, a1 = sched.run(_ffn_up, n1, w_gu1, bm=512, name="ffn1_gu")
    w_d1 = sched.weight("ffn1_w_down")
    f1, h1, n2 = sched.run(_mm_resid_norm, a1, w_d1, x, small["ffn1_post_g"], 0.5, small["attn_pre_g"], bm=512,
                           name="ffn1_down")
    win_a, win_b = sched.weight("w_in")
    za = _mm_nt(n2, win_a, bm=1024, bn=ZA_W, out_dtype=BF16, name="in_proj_a", add=b_in_a)
    dils = tuple(dil for _, dil in B_PATTERNS)
    zb_by_dil = _mm_nt_classes(n2, win_b, b_in_b, dils, bm=512, name="in_proj_b")

    a_args = dict(r=1, q_col=0, k_col=W // LANES, v_col=W // LANES + 1, stride=0, pattern=0, kv_shared=True)
    sink_row = jnp.repeat(small["sinks"], HEAD_DIM, axis=1)
    out_a, out_a_bf, lse_a = sched.run(_band_fwd, za, za, za, bias_all, name="band_a_fwd", sink=sink_row, **a_args)

    no_sink = jnp.full((1, W), NEG, F32)
    b_ctx, branches = [], []
    for t, dil in enumerate(dils):
        zb_r = zb_by_dil[t]
        args = dict(r=dil, q_col=0, k_col=1, v_col=2, stride=ZB_W // W, pattern=1 + t)
        branches.append(sched.run(_band_fwd, zb_r, zb_r, zb_r, bias_all, name=f"band_b{t}_fwd", **args))
        b_ctx.append((jnp.asarray(idx_np[1 + t]), zb_r, args))
    out_b, out_b_bf, lse_b = _merge(branches, dils, no_sink, name="merge_b")

    mix = jnp.concatenate([out_a_bf, out_b_bf], axis=1)
    w_out = sched.weight("w_out")
    att, h2, n3 = _mm_resid_norm(mix, w_out, h1, small["attn_post_g"], 1.0, small["ffn2_pre_g"], bm=512,
                                 name="out_proj", bias=small["b_out"])
    w_gu2, w_d2 = sched.weight("ffn2_w_gu"), sched.weight("ffn2_w_down")
    *gu2, a2 = _ffn_up(n3, w_gu2, bm=512, name="ffn2_gu")
    f2, h3, n4 = _mm_resid_norm(a2, w_d2, h2, small["ffn2_post_g"], 0.5, small["ple_pre_g"], bm=512,
                                name="ffn2_down")
    w_gate = sched.weight("w_ple_gate")
    p_bf = p.astype(BF16)
    dh4, de, dgp, loss, d_ple_post = _ple_head(n4, w_gate, p_bf, sched.weight("w_ple_proj"), h3, small["ple_post_g"],
                                               target, bm=512, name="ple_head")

    gs = {"ple_post_g": d_ple_post}
    sched.grad("w_ple_proj", _mm_tn(p_bf, de, bm=256, bn=1024, out_dtype=BF16, name="d_w_ple"))
    sched.grad("w_ple_gate", _mm_tn(n4, dgp, bm=512, bn=1024, out_dtype=BF16, name="d_w_gate"))
    dh3, df2, gs["ple_pre_g"], gs["ffn2_post_g"], _ = sched.run(
        _mm_nt_norms_bwd, dgp, w_gate, dh4, h3, small["ple_pre_g"], f2, small["ffn2_post_g"], 0.5, bm=512, name="d_n4")

    def ffn_bwd(df, a, gu, n, w_down, w_gu, tag, *norm_args):
        dgu = sched.run(_ffn_down_bwd, df, w_down, gu[0], gu[1], bm=512, name=f"ffn{tag}_d_a")
        sched.grad(f"ffn{tag}_w_gu", _mm_tn(n, dgu, bm=512, bn=1408, out_dtype=BF16, name=f"ffn{tag}_d_w_gu",
                                            n_stack=N_CHIPS))
        sched.grad(f"ffn{tag}_w_down", sched.run(_mm_tn, a, df, bm=256, bn=1024, out_dtype=BF16,
                                                name=f"ffn{tag}_d_w_down"))
        return sched.run(_mm_nt_norms_bwd, dgu, w_gu, *norm_args, bm=512, name=f"ffn{tag}_d_n")

    dh2, datt, gs["ffn2_pre_g"], gs["attn_post_g"], gs["b_out"] = ffn_bwd(
        df2, a2, gu2, n3, w_d2, w_gu2, "2", dh3, h2, small["ffn2_pre_g"], att, small["attn_post_g"], 1.0)
    sched.grad("w_out", _mm_tn(mix, datt, bm=512, bn=1024, out_dtype=BF16, name="d_w_out"))
    dmix = sched.run(_mm_nt, datt, w_out, bm=1024, bn=1024, out_dtype=F32, name="d_mix")

    do_a, stat_a, dsink, do_b, stat_b = _attn_delta(dmix, (out_a, out_b), (lse_a, lse_b), sink_row, dils,
                                                    name="attn_delta")
    gs["sinks"] = dsink[:, ::HEAD_DIM]
    dq_a, dk_a, dv_a, dbias_a = sched.run(_band_bwd, za, za, za, do_a, stat_a, bias_all, name="band_a_bwd", **a_args)
    d_rel_a = _bias_grad(dbias_a, jnp.asarray(idx_np[0]), name="d_rel_a")
    b_grads = []
    d_rel_b = None
    for t, (idx, zb_r, args) in enumerate(b_ctx):
        dq, dk, dv, dbias = sched.run(_band_bwd, zb_r, zb_r, zb_r, do_b[t], stat_b[t], bias_all,
                                      name=f"band_b{t}_bwd", **args)
        b_grads.append((dq, dk, dv))
        d_rel = _bias_grad(dbias, idx, name=f"d_rel_b{t}")
        d_rel_b = d_rel if d_rel_b is None else d_rel_b + d_rel
    gs["rel_bias"] = jnp.concatenate([d_rel_a[:, 0:2 * N_PAIRS], d_rel_b[:, 0:2 * N_PAIRS]], axis=1)
    dza, dzb, dsum_a, dsum_b = _dz_assemble(dq_a, dk_a, dv_a, b_grads, dils, name="d_z")
    gs["b_in"] = jnp.concatenate([dsum_a, dsum_b], axis=1)
    sched.grad("w_in", (_mm_tn(dza, n2, bm=ZA_W, bn=1024, out_dtype=BF16, name="d_w_in_a"),
                        _mm_tn(dzb, n2, bm=ZB_W // 2, bn=1024, out_dtype=BF16, name="d_w_in_b")))
    dn2_a = _mm_nn(dza, win_a, bm=1024, bn=1024, out_dtype=F32, name="d_n2_a")
    dh1, df1, gs["attn_pre_g"], gs["ffn1_post_g"], _ = sched.run(
        _mm_nt_norms_bwd, dzb, win_b, dh2, h1, small["attn_pre_g"], f1, small["ffn1_post_g"], 0.5, bm=512,
        name="d_n2_b", add=dn2_a, b_is_kn=True)
    grad_x, gs["ffn1_pre_g"] = ffn_bwd(df1, a1, gu1, n1, w_d1, w_gu1, "1", dh1, x, small["ffn1_pre_g"], None, None, 0.0)
    return loss, grad_x, gs


def _to_compute(name, full):
    st = full.reshape(N_CHIPS, full.shape[0] // N_CHIPS, full.shape[1])
    if name in ("ffn1_w_gu", "ffn2_w_gu"):
        return st
    if name == "w_ple_proj":
        return jnp.transpose(st, (1, 0, 2)).reshape(st.shape[1], -1)
    if name == "w_in":
        return full[0:ZA_W], full[ZA_W:]
    return full


def _to_comm(name, g):
    if name == "w_in":
        g = jnp.concatenate(g, axis=0)
    if name == "w_ple_proj":
        g = jnp.transpose(g.reshape(g.shape[0], N_CHIPS, -1), (1, 0, 2))
    rs = g.shape[1] if g.ndim == 3 else g.shape[0] // N_CHIPS
    return g.reshape(N_CHIPS, 2, rs // 2, g.shape[-1])


class _Schedule:
    GATHER = {
        "prologue": [("ffn1_w_gu", (0, 1))],
        "ffn1_gu": [("ffn1_w_down", (0, 1)), ("w_in", (0, 1))],
        "band_a_fwd": [("ffn2_w_gu", (0, 2)), ("w_out", (0, 1))],
        "band_b0_fwd": [("ffn2_w_gu", (1, 2)), ("w_ple_gate", (0, 1)), ("w_ple_proj", (0, 1))],
        "band_b1_fwd": [("ffn2_w_down", (0, 1))],
    }
    SWAP = {"ffn2_d_w_down": ["w_ple_proj", "w_ple_gate", "ffn2_w_gu"], "ffn2_d_n": ["ffn2_w_down"],
            "band_a_bwd": ["w_out"], "d_n2_b": ["w_in"], "ffn1_d_w_down": ["ffn1_w_gu"], "ffn1_d_n": ["ffn1_w_down"]}
    SCATTER = {"ffn2_d_n": ["ffn2_w_gu"], "band_a_bwd": ["w_ple_proj", "w_ple_gate", "ffn2_w_down"],
               "d_n2_b": ["w_out"], "ffn1_d_w_down": ["w_in"], "ffn1_d_n": ["ffn1_w_gu"]}

    def __init__(self, placed, c_idx, kc_idx):
        self.full = dict(placed)
        self.missing = {n: 1.0 for n in placed}
        self.c_idx, self.kc_idx = c_idx, kc_idx
        self.grads, self.swapped, self.pair_sums, self.scattered = {}, {}, {}, {}

    def _gather(self, entries):
        carries, after = [], []
        for part in sorted({pt for _, pt in entries}):
            names = [n for n, pt in entries if pt == part]
            carry = _gather_carry([self.full[n] for n in names], part)
            carries.append(carry)

            def done(carry=carry, names=names, part=part):
                for n, a in zip(names, carry.results):
                    self.full[n] = a
                    self.missing[n] -= 1.0 / part[1]
            after.append(done)
        return carries, after

    def weight(self, name):
        assert abs(self.missing[name]) < 1e-9, (name, self.missing[name])
        return _to_compute(name, self.full[name])

    def grad(self, name, g):
        self.grads[name] = _to_comm(name, g)

    def _swap(self, names):
        carry = _pair_swap_carry([self.grads[n] for n in names])

        def done():
            self.swapped.update(zip(names, carry.results))
        return carry, done

    def _scatter(self, names):
        for n in names:
            self.pair_sums[n] = _pair_sum(self.grads[n], self.swapped[n], self.c_idx, name=f"grad_pair_sum_{n}")
        carry = _chip_scatter_carry([self.pair_sums[n] for n in names])

        def done():
            self.scattered.update(zip(names, carry.results))
        return carry, done

    def run(self, fn, *args, name, **kw):
        carries, after = self._gather(self.GATHER.get(name, []))
        for names, make in ((self.SWAP.get(name), self._swap), (self.SCATTER.get(name), self._scatter)):
            if names:
                carry, done = make(names)
                carries.append(carry)
                after.append(done)
        out = fn(*args, name=name, carry=_join(carries), **kw)
        for done in after:
            done()
        return out

    def finish(self):
        left = [n for n in BIG if n not in self.scattered]
        to_swap = [n for n in left if n not in self.swapped]
        if to_swap:
            carry, done = self._swap(to_swap)
            _comm_call(carry, name="grad_pair_swap")
            done()
        if left:
            carry, done = self._scatter(left)
            _comm_call(carry, name="grad_chip_scatter")
            done()
        halves = _chip_sum([self.pair_sums[n] for n in BIG], [self.scattered[n] for n in BIG], self.kc_idx,
                           name="grad_chip_sum")
        carry = _half_swap_carry(halves)
        _comm_call(carry, name="grad_half_swap")
        return dict(zip(BIG, carry.results))


def kernel(x, p, rel_bias, ffn1_pre_g, ffn1_w_gu, ffn1_w_down, ffn1_post_g, attn_pre_g, w_in, b_in, sinks, w_out, b_out, attn_post_g, ffn2_pre_g, ffn2_w_gu, ffn2_w_down, ffn2_post_g, ple_pre_g, w_ple_gate, w_ple_proj, ple_post_g, loss_target, m_rel_bias, m_ffn1_pre_g, m_ffn1_w_gu, m_ffn1_w_down, m_ffn1_post_g, m_attn_pre_g, m_w_in, m_b_in, m_sinks, m_w_out, m_b_out, m_attn_post_g, m_ffn2_pre_g, m_ffn2_w_gu, m_ffn2_w_down, m_ffn2_post_g, m_ple_pre_g, m_w_ple_gate, m_w_ple_proj, m_ple_post_g, v_rel_bias, v_ffn1_pre_g, v_ffn1_w_gu, v_ffn1_w_down, v_ffn1_post_g, v_attn_pre_g, v_w_in, v_b_in, v_sinks, v_w_out, v_b_out, v_attn_post_g, v_ffn2_pre_g, v_ffn2_w_gu, v_ffn2_w_down, v_ffn2_post_g, v_ple_pre_g, v_w_ple_gate, v_w_ple_proj, v_ple_post_g):
    given = dict(locals())
    w = {n: given[n] for n in WEIGHTS}
    m = {n: given["m_" + n] for n in WEIGHTS}
    v = {n: given["v_" + n] for n in WEIGHTS}
    c_pos = lax.axis_index("c").astype(jnp.int32)
    k_pos = (2 * lax.axis_index("x") + lax.axis_index("y")).astype(jnp.int32)
    c_idx, k_idx, kc_idx = c_pos.reshape(1), k_pos.reshape(1), jnp.stack([k_pos, c_pos])

    def shard(a, n):
        return jnp.transpose(a[0]) if n == "w_in" else a[0]

    def unshard(a, n):
        return (jnp.transpose(a) if n == "w_in" else a)[None]

    placed = _cast_place([shard(w[n], n) for n in BIG], k_idx, name="place_shards")
    sched = _Schedule(dict(zip(BIG, placed)), c_idx, kc_idx)
    small = {n: (w[n] if n == "rel_bias" else w[n].reshape(1, -1)) for n in SMALL}

    loss_part, grad_x, gs = _local_step(x[0], p[0, 0], loss_target[0], small, sched)

    grads_big = sched.finish()

    *small_grads, loss_row = _small_all_reduce([gs[n] for n in SMALL] + [loss_part], name="small_all_reduce")
    loss = loss_row[0, 0]

    grads, delta, new_m, new_v = {}, {}, {}, {}
    res = _adamw([shard(w[n], n) for n in BIG], [grads_big[n] for n in BIG], [shard(m[n], n) for n in BIG],
                 [shard(v[n], n) for n in BIG], name="adamw_big")
    for n, gg, dd, mm, vv in zip(BIG, *res):
        grads[n], delta[n], new_m[n], new_v[n] = (unshard(a, n) for a in (gg, dd, mm, vv))
    res = _adamw_small([w[n] for n in SMALL], small_grads, [m[n] for n in SMALL], [v[n] for n in SMALL],
                       name="adamw_small")
    for name, gg, dd, mm, vv in zip(SMALL, *res):
        grads[name], delta[name], new_m[name], new_v[name] = gg, dd, mm, vv

    return (loss, grad_x[None], *[grads[n] for n in WEIGHTS], *[delta[n] for n in WEIGHTS],
            *[new_m[n] for n in WEIGHTS], *[new_v[n] for n in WEIGHTS])
```

```python
import math

import jax
import jax.numpy as jnp
import numpy as np
from jax import lax
from jax.experimental import pallas as pl
from jax.experimental.pallas import tpu as pltpu

F32 = jnp.float32
BF16 = jnp.bfloat16
MESH = pl.DeviceIdType.MESH

EPS = 1e-6
NEG = -1e30
LANES = 128
SUBLANES = 8
HEAD_DIM = 64
QBLK = 128
N_PAIRS = 4
MIX_W = N_PAIRS * LANES
A_WINDOW = 128
B_PATTERNS = ((128, 1), (512, 4), (2048, 16))
NUM_BUCKETS = 32
MAX_DISTANCE = 2048
ZA_W = 768
ZB_W = 1536
N_CHIPS = 4
VMEM_BYTES_V7X = 64 * 2**20

ADAM_LR, ADAM_B1, ADAM_B2, ADAM_EPS, ADAM_WD, ADAM_STEP = 0.001, 0.9, 0.999, 1e-08, 0.01, 10

BIG = ("ffn1_w_gu", "ffn1_w_down", "w_in", "w_out", "ffn2_w_gu", "ffn2_w_down", "w_ple_gate", "w_ple_proj")
SMALL = ("rel_bias", "ffn1_pre_g", "ffn1_post_g", "attn_pre_g", "b_in", "sinks", "b_out", "attn_post_g",
         "ffn2_pre_g", "ffn2_post_g", "ple_pre_g", "ple_post_g")
WEIGHTS = ("rel_bias", "ffn1_pre_g", "ffn1_w_gu", "ffn1_w_down", "ffn1_post_g", "attn_pre_g", "w_in", "b_in",
           "sinks", "w_out", "b_out", "attn_post_g", "ffn2_pre_g", "ffn2_w_gu", "ffn2_w_down", "ffn2_post_g",
           "ple_pre_g", "w_ple_gate", "w_ple_proj", "ple_post_g")


def _vmem_limit(*block_bytes):
    need = 2 * sum(block_bytes) + max(block_bytes) * 2 + (4 << 20)
    return int(min(max(need, 32 << 20), VMEM_BYTES_V7X - (6 << 20)))


def _nbytes(shape, dtype):
    return int(np.prod(shape)) * jnp.dtype(dtype).itemsize


MXU_WIDTH_V7X = 256


def _col_chunks(n):
    return [slice(c, min(c + MXU_WIDTH_V7X, n)) for c in range(0, n, MXU_WIDTH_V7X)]


def _row_halves(rows):
    return [slice(0, rows // 2), slice(rows // 2, rows)]


def _row_block(rows, cap, mult):
    for cand in range(min(rows, cap), 0, -1):
        if rows % cand == 0 and cand % mult == 0:
            return cand
    raise ValueError((rows, cap, mult))


class _Carry:
    def __init__(self, operands, out_shapes, aliases, sems, start, finish, before_last=None):
        self.operands, self.out_shapes, self.aliases, self.sems = list(operands), list(out_shapes), dict(aliases), list(sems)
        self.start, self.finish = start, finish
        self.before_last = before_last or (lambda ins, outs, sems: None)
        self.results = None


def _join(carries):
    carries = [c for c in carries if c is not None]
    if not carries:
        return None
    if len(carries) == 1:
        return carries[0]
    offs, pos = [], [0, 0, 0]
    for c in carries:
        offs.append(tuple(pos))
        pos = [pos[0] + len(c.operands), pos[1] + len(c.out_shapes), pos[2] + len(c.sems)]
    aliases = {}
    for c, (oi, oo, _) in zip(carries, offs):
        aliases.update({oi + i: oo + o for i, o in c.aliases.items()})

    def run(which):
        def fn(ins, outs, sems):
            for c, (oi, oo, os_) in zip(carries, offs):
                getattr(c, which)(ins[oi:oi + len(c.operands)], outs[oo:oo + len(c.out_shapes)],
                                  sems[os_:os_ + len(c.sems)])
        return fn

    joined = _Carry([a for c in carries for a in c.operands], [s for c in carries for s in c.out_shapes], aliases,
                    [s for c in carries for s in c.sems], run("start"), run("finish"), run("before_last"))
    joined.parts = (carries, offs)
    return joined


def _set_results(carry, outs):
    carry.results = list(outs)
    if hasattr(carry, "parts"):
        for c, (_, oo, _) in zip(*carry.parts):
            _set_results(c, outs[oo:oo + len(c.out_shapes)])


ANY = pl.BlockSpec(memory_space=pl.ANY)


def _call(body, *, name, grid, in_specs, out_specs, out_shape, args, scratch=(), vmem_limit=None, carry=None,
          semantics=None):
    n_ci, n_co, n_cs = len(args), len(out_shape), len(scratch)
    semantics = semantics or ("parallel",) * len(grid)
    if carry is None:
        res = pl.pallas_call(
            body, name=name, grid=grid, in_specs=list(in_specs), out_specs=tuple(out_specs), out_shape=tuple(out_shape),
            scratch_shapes=list(scratch),
            compiler_params=pltpu.CompilerParams(dimension_semantics=semantics, vmem_limit_bytes=vmem_limit),
        )(*args)
        return list(res)
    n_pi, n_po = len(carry.operands), len(carry.out_shapes)

    def full_body(*refs):
        ci, pi = refs[:n_ci], refs[n_ci:n_ci + n_pi]
        o0 = n_ci + n_pi
        co, po = refs[o0:o0 + n_co], refs[o0 + n_co:o0 + n_co + n_po]
        s0 = o0 + n_co + n_po
        cs, ps = refs[s0:s0 + n_cs], refs[s0 + n_cs:]
        ids = [pl.program_id(ax) for ax in range(len(grid))]
        first, last = ids[0] == 0, ids[0] == grid[0] - 1
        for ax in range(1, len(grid)):
            first, last = first & (ids[ax] == 0), last & (ids[ax] == grid[ax] - 1)

        @pl.when(first)
        def _():
            carry.start(pi, po, ps)

        several_steps = any(g > 1 for g in grid)
        if several_steps:
            @pl.when(last)
            def _():
                carry.before_last(pi, po, ps)

        body(*ci, *co, *cs)

        @pl.when(last)
        def _():
            if not several_steps:
                carry.before_last(pi, po, ps)
            carry.finish(pi, po, ps)

    res = pl.pallas_call(
        full_body, name=name, grid=grid, in_specs=list(in_specs) + [ANY] * n_pi,
        out_specs=tuple(out_specs) + tuple([ANY] * n_po), out_shape=tuple(out_shape) + tuple(carry.out_shapes),
        scratch_shapes=list(scratch) + carry.sems,
        input_output_aliases={n_ci + i: n_co + o for i, o in carry.aliases.items()},
        compiler_params=pltpu.CompilerParams(dimension_semantics=("arbitrary",) * len(grid),
                                             vmem_limit_bytes=vmem_limit),
    )(*args, *carry.operands)
    _set_results(carry, res[n_co:])
    return list(res[:n_co])


def _comm_call(carry, *, name):
    n_pi, n_po = len(carry.operands), len(carry.out_shapes)

    def body(*refs):
        pi, po, ps = refs[:n_pi], refs[n_pi:n_pi + n_po], refs[n_pi + n_po:]
        carry.start(pi, po, ps)
        carry.before_last(pi, po, ps)
        carry.finish(pi, po, ps)

    res = pl.pallas_call(
        body, name=name, in_specs=[ANY] * n_pi, out_specs=tuple([ANY] * n_po), out_shape=tuple(carry.out_shapes),
        scratch_shapes=carry.sems, input_output_aliases=dict(carry.aliases),
    )(*carry.operands)
    _set_results(carry, res)


def _mm_nn(a, b, *, bm, bn, out_dtype, name, bias=None, carry=None):
    M, K = a.shape
    stacked = b.ndim == 3
    N = b.shape[0] * b.shape[2] if stacked else b.shape[1]
    assert M % bm == 0 and N % bn == 0 and (not stacked or bn == b.shape[2])

    def body(*refs):
        a_ref, b_ref = refs[0], refs[1]
        o_ref = refs[-1]
        acc = jnp.dot(a_ref[...], b_ref[...], preferred_element_type=F32)
        if bias is not None:
            acc = acc + refs[2][...]
        o_ref[...] = acc.astype(o_ref.dtype)

    if stacked:
        b_spec = pl.BlockSpec((None, K, bn), lambda i, j: (j, 0, 0))
    else:
        b_spec = pl.BlockSpec((K, bn), lambda i, j: (0, j))
    in_specs = [pl.BlockSpec((bm, K), lambda i, j: (i, 0)), b_spec]
    args = [a, b]
    if bias is not None:
        in_specs.append(pl.BlockSpec((1, bn), lambda i, j: (0, j)))
        args.append(bias)
    limit = _vmem_limit(_nbytes((bm, K), a.dtype), _nbytes((K, bn), b.dtype), _nbytes((bm, bn), F32))
    return _call(body, name=name, grid=(M // bm, N // bn), in_specs=in_specs,
                 out_specs=[pl.BlockSpec((bm, bn), lambda i, j: (i, j))],
                 out_shape=[jax.ShapeDtypeStruct((M, N), out_dtype)], args=args, vmem_limit=limit, carry=carry)[0]


def _mm_nt(a, b, *, bm, bn, out_dtype, name, add=None, carry=None):
    M, K = a.shape
    stacked = b.ndim == 3
    N = b.shape[1] if stacked else b.shape[0]
    n_sh = b.shape[0] if stacked else 1
    ks = b.shape[2] if stacked else K
    assert M % bm == 0 and N % bn == 0 and n_sh * ks == K
    dn = (((1,), (1,)), ((), ()))

    def body(*refs):
        a_ref, b_ref = refs[0], refs[1]
        o_ref = refs[-1]
        if stacked:
            acc = lax.dot_general(a_ref[:, 0:ks], b_ref[0], dn, preferred_element_type=F32)
            for s in range(1, n_sh):
                acc = acc + lax.dot_general(a_ref[:, s * ks:(s + 1) * ks], b_ref[s], dn, preferred_element_type=F32)
        else:
            acc = lax.dot_general(a_ref[...], b_ref[...], dn, preferred_element_type=F32)
        if add is not None:
            acc = acc + refs[2][...]
        o_ref[...] = acc.astype(o_ref.dtype)

    if stacked:
        b_spec = pl.BlockSpec((n_sh, bn, ks), lambda i, j: (0, j, 0))
    else:
        b_spec = pl.BlockSpec((bn, K), lambda i, j: (j, 0))
    in_specs = [pl.BlockSpec((bm, K), lambda i, j: (i, 0)), b_spec]
    args = [a, b]
    if add is not None:
        rows = bm if add.shape[0] == M else 1
        in_specs.append(pl.BlockSpec((rows, bn), lambda i, j: (i if rows == bm else 0, j)))
        args.append(add)
    limit = _vmem_limit(_nbytes((bm, K), a.dtype), _nbytes((bn, K), b.dtype), _nbytes((bm, bn), F32))
    return _call(body, name=name, grid=(M // bm, N // bn), in_specs=in_specs,
                 out_specs=[pl.BlockSpec((bm, bn), lambda i, j: (i, j))],
                 out_shape=[jax.ShapeDtypeStruct((M, N), out_dtype)], args=args, vmem_limit=limit, carry=carry)[0]


def _mm_tn(a, b, *, bm, bn, out_dtype, name, n_stack=None, carry=None):
    T, M = a.shape
    N = b.shape[1]
    assert M % bm == 0 and N % bn == 0 and (n_stack is None or bn * n_stack == N)
    dn = (((0,), (0,)), ((), ()))

    def body(a_ref, b_ref, o_ref):
        acc = lax.dot_general(a_ref[...], b_ref[...], dn, preferred_element_type=F32)
        o_ref[...] = acc.astype(o_ref.dtype)

    if n_stack is None:
        out_spec = pl.BlockSpec((bm, bn), lambda i, j: (i, j))
        out_shape = jax.ShapeDtypeStruct((M, N), out_dtype)
    else:
        out_spec = pl.BlockSpec((None, bm, bn), lambda i, j: (j, i, 0))
        out_shape = jax.ShapeDtypeStruct((n_stack, M, bn), out_dtype)
    limit = _vmem_limit(_nbytes((T, bm), a.dtype), _nbytes((T, bn), b.dtype), _nbytes((bm, bn), F32))
    return _call(body, name=name, grid=(M // bm, N // bn),
                 in_specs=[pl.BlockSpec((T, bm), lambda i, j: (0, i)), pl.BlockSpec((T, bn), lambda i, j: (0, j))],
                 out_specs=[out_spec], out_shape=[out_shape], args=[a, b], vmem_limit=limit, carry=carry)[0]


ROW_BLK = 256


def _rstd(x):
    return lax.rsqrt(jnp.mean(x * x, axis=-1, keepdims=True) + EPS)


def _norm_bwd(xhat, rstd, dxhat):
    return rstd * (dxhat - xhat * jnp.mean(dxhat * xhat, axis=-1, keepdims=True))


def _row_spec(cols):
    return pl.BlockSpec((ROW_BLK, cols), lambda i: (i, 0))


def _vec_spec(cols):
    return pl.BlockSpec((1, cols), lambda i: (0, 0))


def _prologue(x, g, rel_bias, idx_all, *, name, carry=None):
    T, D = x.shape
    n_pat = idx_all.shape[0]
    heads = 2 * N_PAIRS
    steps = n_pat * heads
    rows = T // steps

    def body(rb_ref, x_ref, g_ref, idx_ref, n_ref, bias_ref):
        s = pl.program_id(0)
        head = jnp.where(s < heads, s, heads + s % heads)
        xv = x_ref[...]
        n_ref[...] = (xv * _rstd(xv) * g_ref[...]).astype(n_ref.dtype)
        idxv = idx_ref[...]
        acc = jnp.where(idxv < 0, NEG, 0.0).astype(F32)
        for b in range(NUM_BUCKETS):
            acc = acc + jnp.where(idxv == b, rb_ref[b, head], 0.0)
        bias_ref[0] = acc
        kap = lax.broadcasted_iota(jnp.int32, (1, 2 * QBLK), 1)
        bias_ref[1] = jnp.where(kap < QBLK, NEG, acc)

    return _call(
        body, name=name, grid=(steps,),
        in_specs=[pl.BlockSpec(memory_space=pltpu.SMEM), pl.BlockSpec((rows, D), lambda s: (s, 0)),
                  pl.BlockSpec((1, D), lambda s: (0, 0)),
                  pl.BlockSpec((None, QBLK, 2 * QBLK), lambda s: (s // heads, 0, 0))],
        out_specs=[pl.BlockSpec((rows, D), lambda s: (s, 0)),
                   pl.BlockSpec((None, 2, None, QBLK, 2 * QBLK), lambda s: (s // heads, 0, s % heads, 0, 0))],
        out_shape=[jax.ShapeDtypeStruct((T, D), BF16),
                   jax.ShapeDtypeStruct((n_pat, 2, heads, QBLK, 2 * QBLK), F32)],
        args=[rel_bias, x, g, idx_all], carry=carry)


def _mm_resid_norm(a, b, h, g_post, coef, g_next, *, bm, name, bias=None, carry=None):
    M, K = a.shape
    N = b.shape[1]

    def body(*refs):
        a_ref, b_ref, h_ref, gp_ref, gn_ref = refs[0:5]
        f_ref, hn_ref, n_ref = refs[-3:]
        for rows in _row_halves(bm):
            f = jnp.dot(a_ref[rows, :], b_ref[...], preferred_element_type=F32)
            if bias is not None:
                f = f + refs[5][...]
            f_ref[rows, :] = f
            hn = h_ref[rows, :] + coef * (f * _rstd(f) * gp_ref[...])
            hn_ref[rows, :] = hn
            n_ref[rows, :] = (hn * _rstd(hn) * gn_ref[...]).astype(n_ref.dtype)

    row = lambda w: pl.BlockSpec((bm, w), lambda i: (i, 0))
    vec = pl.BlockSpec((1, N), lambda i: (0, 0))
    in_specs = [row(K), pl.BlockSpec((K, N), lambda i: (0, 0), pipeline_mode=pl.Buffered(1)), row(N), vec, vec]
    args = [a, b, h, g_post, g_next]
    if bias is not None:
        in_specs.append(vec)
        args.append(bias)
    limit = _vmem_limit(_nbytes((bm, K), a.dtype), _nbytes((K, N), b.dtype) // 2, 4 * _nbytes((bm, N), F32))
    return _call(body, name=name, grid=(M // bm,), in_specs=in_specs, out_specs=[row(N), row(N), row(N)],
                 out_shape=[jax.ShapeDtypeStruct((M, N), F32), jax.ShapeDtypeStruct((M, N), F32),
                            jax.ShapeDtypeStruct((M, N), BF16)], args=args, vmem_limit=limit, carry=carry)


def _mm_nt_norms_bwd(a, b, dh_in, h, g_pre, f, g_post, coef, *, bm, name, add=None, b_is_kn=False, carry=None):
    M, K = a.shape
    stacked = b.ndim == 3
    N = b.shape[1] if (stacked or b_is_kn) else b.shape[0]
    n_sh = b.shape[0] if stacked else 1
    ks = b.shape[2] if stacked else K
    assert n_sh * ks == K
    dn_dims = (((1,), (0,)), ((), ())) if b_is_kn else (((1,), (1,)), ((), ()))
    with_f = f is not None
    n_in = 5 + (2 if with_f else 0) + (1 if add is not None else 0)

    def body(*refs):
        a_ref, b_ref, dhi_ref, h_ref, gpre_ref = refs[0:5]
        outs = refs[n_in:]

        @pl.when(pl.program_id(0) == 0)
        def _():
            for acc in (outs[2:] if with_f else outs[1:]):
                acc[...] = jnp.zeros_like(acc)

        for rows in _row_halves(bm):
            if stacked:
                dn = lax.dot_general(a_ref[rows, 0:ks], b_ref[0], dn_dims, preferred_element_type=F32)
                for s in range(1, n_sh):
                    dn = dn + lax.dot_general(a_ref[rows, s * ks:(s + 1) * ks], b_ref[s], dn_dims,
                                              preferred_element_type=F32)
            else:
                dn = lax.dot_general(a_ref[rows, :], b_ref[...], dn_dims, preferred_element_type=F32)
            if add is not None:
                dn = dn + refs[n_in - 1][rows, :]
            hv = h_ref[rows, :]
            r = _rstd(hv)
            hh = hv * r
            dh = dhi_ref[rows, :] + _norm_bwd(hh, r, dn * gpre_ref[...])
            outs[0][rows, :] = dh
            if not with_f:
                outs[1][...] += jnp.sum(dn * hh, axis=0, keepdims=True)
                continue
            f_ref, gpost_ref = refs[5], refs[6]
            df_ref, dgpre_ref, dgpost_ref, dsum_ref = outs[1:]
            dgpre_ref[...] += jnp.sum(dn * hh, axis=0, keepdims=True)
            fv = f_ref[rows, :]
            rf = _rstd(fv)
            fh = fv * rf
            dy = coef * dh
            dgpost_ref[...] += jnp.sum(dy * fh, axis=0, keepdims=True)
            df = _norm_bwd(fh, rf, dy * gpost_ref[...])
            dsum_ref[...] += jnp.sum(df, axis=0, keepdims=True)
            df_ref[rows, :] = df.astype(df_ref.dtype)

    row = lambda w: pl.BlockSpec((bm, w), lambda i: (i, 0))
    vec = pl.BlockSpec((1, N), lambda i: (0, 0))
    once = pl.Buffered(1)
    if stacked:
        b_spec = pl.BlockSpec((n_sh, N, ks), lambda i: (0, 0, 0), pipeline_mode=once)
    else:
        b_spec = pl.BlockSpec(b.shape, lambda i: (0, 0), pipeline_mode=once)
    in_specs = [row(K), b_spec, row(N), row(N), vec]
    args = [a, b, dh_in, h, g_pre]
    if with_f:
        in_specs += [row(N), vec]
        args += [f, g_post]
    if add is not None:
        in_specs.append(row(N))
        args.append(add)
    vec_shape = jax.ShapeDtypeStruct((1, N), F32)
    out_specs = [row(N)] + ([row(N), vec, vec, vec] if with_f else [vec])
    out_shape = [jax.ShapeDtypeStruct((M, N), F32)]
    out_shape += [jax.ShapeDtypeStruct((M, N), BF16), vec_shape, vec_shape, vec_shape] if with_f else [vec_shape]
    limit = _vmem_limit(_nbytes((bm, K), a.dtype), _nbytes((N, K), b.dtype) // 2, 6 * _nbytes((bm, N), F32))
    return _call(body, name=name, grid=(M // bm,), in_specs=in_specs, out_specs=out_specs, out_shape=out_shape,
                 args=args, vmem_limit=limit, semantics=("arbitrary",), carry=carry)


def _ffn_up(n, w_gu, *, bm, name, carry=None):
    M, K = n.shape
    n_sh, _, ns = w_gu.shape
    half = n_sh // 2

    def body(n_ref, wg_ref, wu_ref, g_ref, u_ref, a_ref):
        nv = n_ref[...]
        for cols in _col_chunks(ns):
            g = jnp.dot(nv, wg_ref[:, cols], preferred_element_type=F32)
            u = jnp.dot(nv, wu_ref[:, cols], preferred_element_type=F32)
            g_ref[:, cols] = g.astype(g_ref.dtype)
            u_ref[:, cols] = u.astype(u_ref.dtype)
            a_ref[:, cols] = (g * jax.nn.sigmoid(g) * u).astype(a_ref.dtype)

    out_spec = pl.BlockSpec((bm, ns), lambda i, j: (i, j))
    out = jax.ShapeDtypeStruct((M, half * ns), BF16)
    limit = _vmem_limit(_nbytes((bm, K), n.dtype), 2 * _nbytes((K, ns), w_gu.dtype), 3 * _nbytes((bm, ns), F32))
    return _call(body, name=name, grid=(M // bm, half),
                 in_specs=[pl.BlockSpec((bm, K), lambda i, j: (i, 0)),
                           pl.BlockSpec((None, K, ns), lambda i, j: (j, 0, 0)),
                           pl.BlockSpec((None, K, ns), lambda i, j: (j + half, 0, 0))],
                 out_specs=[out_spec, out_spec, out_spec], out_shape=[out, out, out], args=[n, w_gu, w_gu],
                 vmem_limit=limit, carry=carry)


def _ffn_down_bwd(df, w_down, g, u, *, bm, name, carry=None):
    M, D = df.shape
    F = w_down.shape[0]
    dn = (((1,), (1,)), ((), ()))

    def body(df_ref, w_ref, g_ref, u_ref, o_ref):
        dfv = df_ref[...]
        for cols in _col_chunks(F):
            da = lax.dot_general(dfv, w_ref[cols, :], dn, preferred_element_type=F32)
            gv = g_ref[:, cols].astype(F32)
            s = jax.nn.sigmoid(gv)
            o_ref[:, cols] = (da * u_ref[:, cols].astype(F32) * (s * (1.0 + gv * (1.0 - s)))).astype(o_ref.dtype)
            o_ref[:, F + cols.start:F + cols.stop] = (da * (gv * s)).astype(o_ref.dtype)

    row = lambda w: pl.BlockSpec((bm, w), lambda i: (i, 0))
    limit = _vmem_limit(_nbytes((F, D), w_down.dtype) // 2, 2 * _nbytes((bm, F), BF16), _nbytes((bm, 2 * F), BF16))
    return _call(body, name=name, grid=(M // bm,),
                 in_specs=[row(D), pl.BlockSpec((F, D), lambda i: (0, 0), pipeline_mode=pl.Buffered(1)), row(F), row(F)],
                 out_specs=[row(2 * F)], out_shape=[jax.ShapeDtypeStruct((M, 2 * F), BF16)],
                 args=[df, w_down, g, u], vmem_limit=limit, carry=carry)[0]


def _ple_head(n4, w_gate, p, w_ple, h3, g_post, target, *, bm, name):
    T, D = h3.shape
    kp = p.shape[1]

    def body(n_ref, wg_ref, p_ref, wp_ref, h_ref, g_ref, t_ref, dh_ref, de_ref, dgp_ref, loss_ref, dg_ref):
        @pl.when(pl.program_id(0) == 0)
        def _():
            loss_ref[...] = jnp.zeros_like(loss_ref)
            dg_ref[...] = jnp.zeros_like(dg_ref)

        gpost = g_ref[...]
        for rows in _row_halves(bm):
            gate = jax.nn.sigmoid(jnp.dot(n_ref[rows, :], wg_ref[...], preferred_element_type=F32))
            ev = jnp.dot(p_ref[rows, :], wp_ref[...], preferred_element_type=F32)
            ge = gate * ev
            r = _rstd(ge)
            gh = ge * r
            diff = h_ref[rows, :] + gh * gpost - t_ref[rows, :]
            loss_ref[...] += jnp.sum(diff * diff) * (0.5 / D)
            dh = diff * (1.0 / D)
            dh_ref[rows, :] = dh
            dg_ref[...] += jnp.sum(dh * gh, axis=0, keepdims=True)
            dge = _norm_bwd(gh, r, dh * gpost)
            de_ref[rows, :] = (dge * gate).astype(de_ref.dtype)
            dgp_ref[rows, :] = (dge * ev * gate * (1.0 - gate)).astype(dgp_ref.dtype)

    row = lambda w: pl.BlockSpec((bm, w), lambda i: (i, 0))
    whole = lambda a: pl.BlockSpec(a.shape, lambda i: (0, 0), pipeline_mode=pl.Buffered(1))
    limit = _vmem_limit(_nbytes((bm, D), BF16), _nbytes(w_gate.shape, w_gate.dtype) // 2, 6 * _nbytes((bm, D), F32))
    return pl.pallas_call(
        body, name=name, grid=(T // bm,),
        in_specs=[row(D), whole(w_gate), row(kp), whole(w_ple), row(D), _vec_spec(D), row(D)],
        out_specs=(row(D), row(D), row(D), _vec_spec(LANES), _vec_spec(D)),
        out_shape=(jax.ShapeDtypeStruct((T, D), F32), jax.ShapeDtypeStruct((T, D), BF16),
                   jax.ShapeDtypeStruct((T, D), BF16), jax.ShapeDtypeStruct((1, LANES), F32),
                   jax.ShapeDtypeStruct((1, D), F32)),
        compiler_params=pltpu.CompilerParams(dimension_semantics=("arbitrary",), vmem_limit_bytes=limit),
    )(n4, w_gate, p, w_ple, h3, g_post, target)


def _t5_index(max_dist, stride):
    rho = np.arange(QBLK)[:, None]
    kap = np.arange(2 * QBLK)[None, :]
    d = rho + QBLK - kap
    valid = (d >= 0) & (d <= max_dist)
    n = np.maximum(d, 0) * stride
    max_exact = NUM_BUCKETS // 2
    nf = np.maximum(n, 1).astype(np.float32)
    large = max_exact + (np.log(nf / np.float32(max_exact)) / np.float32(math.log(MAX_DISTANCE / max_exact))
                         * np.float32(NUM_BUCKETS - max_exact)).astype(np.int32)
    large = np.minimum(large, NUM_BUCKETS - 1)
    bucket = np.where(n < max_exact, n, large)
    return np.where(valid, bucket, -1).astype(np.int32)


def _bias_grad(d_bias, idx, *, name):
    def body(db_ref, idx_ref, o_ref):
        h = pl.program_id(0)

        @pl.when(h == 0)
        def _():
            o_ref[...] = jnp.zeros_like(o_ref)

        idxv = idx_ref[...]
        dv = db_ref[0]
        rows = lax.broadcasted_iota(jnp.int32, (NUM_BUCKETS, LANES), 0)
        lanes = lax.broadcasted_iota(jnp.int32, (NUM_BUCKETS, LANES), 1)
        acc = o_ref[...]
        for b in range(NUM_BUCKETS):
            s = jnp.sum(jnp.where(idxv == b, dv, 0.0))
            acc = acc + jnp.where((rows == b) & (lanes == h), s, 0.0)
        o_ref[...] = acc

    return pl.pallas_call(
        body, name=name, grid=(2 * N_PAIRS,),
        in_specs=[pl.BlockSpec((1, QBLK, 2 * QBLK), lambda h: (h, 0, 0)),
                  pl.BlockSpec((QBLK, 2 * QBLK), lambda h: (0, 0))],
        out_specs=pl.BlockSpec((NUM_BUCKETS, LANES), lambda h: (0, 0)),
        out_shape=jax.ShapeDtypeStruct((NUM_BUCKETS, LANES), F32),
        compiler_params=pltpu.CompilerParams(dimension_semantics=("arbitrary",)),
    )(d_bias, idx)


def _lane():
    return lax.broadcasted_iota(jnp.int32, (1, LANES), 1)


def _head_sel(h):
    return (_lane() < HEAD_DIM) if h == 0 else (_lane() >= HEAD_DIM)


def _stat_lo():
    return (_lane() % HEAD_DIM) < (HEAD_DIM // 2)


def _stack_heads(t, scale=None):
    if scale is not None:
        t = t * scale
    zero = jnp.zeros_like(t)
    return jnp.concatenate([jnp.where(_head_sel(0), t, zero), jnp.where(_head_sel(1), t, zero)], axis=0)


def _class_spec(L, r, cols, stride, pps):
    chunks = N_PAIRS // pps
    mode = pl.Buffered(1) if r * chunks == 1 else None
    return pl.BlockSpec((L, MIX_W // chunks), lambda j, c: (0, (cols + j * stride) * chunks + c), pipeline_mode=mode)


def _rows(b, n_blocks=1):
    return pl.ds(pl.multiple_of(b * QBLK, QBLK), n_blocks * QBLK)


def _key_window(ref, b, cols, first, kv_head=None):
    if kv_head is not None:
        cols = slice(0, LANES)
    if first:
        blk = ref[0:QBLK, cols]
        tile = jnp.concatenate([blk, blk], axis=0)
    else:
        tile = ref[_rows(b - 1, 2), cols]
    if kv_head is None:
        return tile
    wide = tile.astype(F32)
    keep = jnp.logical_xor(_lane() < HEAD_DIM, kv_head == 1)
    return jnp.where(keep, wide, pltpu.roll(wide, HEAD_DIM, 1)).astype(tile.dtype)


def _kv_spec(L, r, col, stride, pps, kv_shared):
    if not kv_shared:
        return _class_spec(L, r, col, stride, pps)
    mode = pl.Buffered(1) if pps == N_PAIRS else None
    return pl.BlockSpec((L, LANES), lambda j, c: (0, col), pipeline_mode=mode)


def _band_fwd(qa, ka, va, bias, *, r, q_col, k_col, v_col, stride, pattern, name, kv_shared=False, sink=None,
              carry=None):
    L = qa.shape[0]
    nb = L // QBLK
    dn = (((1,), (1,)), ((), ()))
    scale = HEAD_DIM ** -0.5

    pps = N_PAIRS

    def body(q_ref, k_ref, v_ref, bias_ref, *rest):
        sel0 = _head_sel(0)
        pair0 = pl.program_id(1) * pps

        def emit(rows, cols, o, lse):
            if sink is None:
                o_ref, lse_ref = rest
                o_ref[rows, cols] = o.astype(o_ref.dtype)
                lse_ref[rows, cols] = lse
                return
            sink_ref, out_ref, outb_ref, lse_ref = rest
            sk = sink_ref[:, cols]
            mx = jnp.maximum(lse, sk)
            w = jnp.exp(lse - mx)
            den = w + jnp.exp(sk - mx)
            out = o * (w / den)
            out_ref[rows, cols] = out
            outb_ref[rows, cols] = out.astype(outb_ref.dtype)
            lse_ref[rows, cols] = mx + jnp.log(den)

        def block(b, first):
            rows = _rows(b)
            for i in range(pps):
                cols = slice(i * LANES, (i + 1) * LANES)
                q2 = _stack_heads(q_ref[rows, cols], scale)
                kv_head = (pair0 + i) // 2 if kv_shared else None
                k = _key_window(k_ref, b, cols, first, kv_head)
                v = _key_window(v_ref, b, cols, first, kv_head)
                bias2 = bias_ref[1 if first else 0, pl.ds(2 * (pair0 + i), 2)].reshape(2 * QBLK, 2 * QBLK)
                s = lax.dot_general(q2, k, dn, preferred_element_type=F32) + bias2
                m = jnp.max(s, axis=1, keepdims=True)
                p = jnp.exp(s - m)
                l = jnp.sum(p, axis=1, keepdims=True)
                o = jnp.dot(p.astype(v.dtype), v, preferred_element_type=F32) * (1.0 / l)
                lse = m + jnp.log(l)
                emit(rows, cols, jnp.where(sel0, o[0:QBLK], o[QBLK:]), jnp.where(sel0, lse[0:QBLK], lse[QBLK:]))

        block(0, True)
        if nb > 1:
            pl.loop(1, nb)(lambda b: block(b, False))

    bias_spec = pl.BlockSpec((None, 2, 2 * N_PAIRS, QBLK, 2 * QBLK), lambda j, c: (pattern, 0, 0, 0, 0))
    out_spec = _class_spec(L, r, 0, 1, pps)
    blk_bytes = _nbytes((L, pps * LANES), F32)
    in_specs = [_class_spec(L, r, q_col, stride, pps), _kv_spec(L, r, k_col, stride, pps, kv_shared),
                _kv_spec(L, r, v_col, stride, pps, kv_shared), bias_spec]
    args = [qa, ka, va, bias]
    f32_out, bf16_out = jax.ShapeDtypeStruct((L, r * MIX_W), F32), jax.ShapeDtypeStruct((L, r * MIX_W), BF16)
    out_shape = [bf16_out, f32_out]
    if sink is not None:
        in_specs.append(pl.BlockSpec((1, MIX_W), lambda j, c: (0, 0)))
        args.append(sink)
        out_shape = [f32_out, bf16_out, f32_out]
    return _call(body, name=name, grid=(r, N_PAIRS // pps), in_specs=in_specs, out_specs=[out_spec] * len(out_shape),
                 out_shape=out_shape, args=args, vmem_limit=_vmem_limit(*([blk_bytes] * 4)), carry=carry)


def _class_rows_spec(rows, r, width):
    return pl.BlockSpec((rows // r, r * width), lambda i, *_: (i, 0))


def _token_scratch(rows, width):
    return pltpu.VMEM((width // LANES, rows, LANES), F32)


def _fill_tokens(scratch, value):
    for c in range(scratch.shape[0]):
        scratch[c] = value[:, c * LANES:(c + 1) * LANES]


def _read_tokens(scratch):
    return jnp.concatenate([scratch[c] for c in range(scratch.shape[0])], axis=1)


def _to_tokens(blk_ref, r, scratch):
    if r == 1:
        return blk_ref[...].astype(F32)
    nt, rows, _ = scratch.shape
    for j in range(r):
        for c in range(nt):
            lanes = slice((j * nt + c) * LANES, (j * nt + c + 1) * LANES)
            scratch.at[c][pl.ds(j, rows // r, stride=r), :] = blk_ref[:, lanes].astype(F32)
    return _read_tokens(scratch)


def _from_tokens(dst_ref, r, scratch):
    nt, rows, _ = scratch.shape
    if r == 1:
        dst_ref[...] = _read_tokens(scratch).astype(dst_ref.dtype)
        return
    for j in range(r):
        for c in range(nt):
            lanes = slice((j * nt + c) * LANES, (j * nt + c + 1) * LANES)
            dst_ref[:, lanes] = scratch.at[c][pl.ds(j, rows // r, stride=r), :].astype(dst_ref.dtype)


def _mm_nt_classes(a, b, bias, dils, *, bm, name):
    M, K = a.shape
    N = b.shape[0]
    dn = (((1,), (1,)), ((), ()))

    def body(a_ref, b_ref, bias_ref, *rest):
        outs, tile = rest[:-1], rest[-1]
        _fill_tokens(tile, lax.dot_general(a_ref[...], b_ref[...], dn, preferred_element_type=F32) + bias_ref[...])
        for out, r in zip(outs, dils):
            _from_tokens(out, r, tile)

    limit = _vmem_limit(_nbytes((bm, K), a.dtype), _nbytes((K, N), b.dtype) // 2, (1 + len(dils)) * _nbytes((bm, N), F32))
    return pl.pallas_call(
        body, name=name, grid=(M // bm,),
        in_specs=[pl.BlockSpec((bm, K), lambda i: (i, 0)),
                  pl.BlockSpec((N, K), lambda i: (0, 0), pipeline_mode=pl.Buffered(1)),
                  pl.BlockSpec((1, N), lambda i: (0, 0))],
        out_specs=tuple(_class_rows_spec(bm, r, N) for r in dils),
        out_shape=tuple(jax.ShapeDtypeStruct((M // r, r * N), BF16) for r in dils),
        scratch_shapes=[_token_scratch(bm, N)],
        compiler_params=pltpu.CompilerParams(dimension_semantics=("parallel",), vmem_limit_bytes=limit),
    )(a, b, bias)


def _merge(branches, dils, sink, *, name):
    W = MIX_W
    T = branches[0][0].shape[0] * dils[0]
    nbr = len(branches)

    def body(*refs):
        sink_ref = refs[2 * nbr]
        out_ref, outb_ref, lse_ref, scratch = refs[2 * nbr + 1:]
        sk = sink_ref[...]
        lses = [_to_tokens(refs[2 * t + 1], dils[t], scratch) for t in range(nbr)]
        mx = sk
        for lse in lses:
            mx = jnp.maximum(mx, lse)
        den = jnp.exp(sk - mx)
        num = jnp.zeros_like(mx)
        for t in range(nbr):
            w = jnp.exp(lses[t] - mx)
            den = den + w
            num = num + w * _to_tokens(refs[2 * t], dils[t], scratch)
        out = num / den
        out_ref[...] = out
        outb_ref[...] = out.astype(outb_ref.dtype)
        lse_ref[...] = mx + jnp.log(den)

    args = [a for br in branches for a in br] + [sink]
    in_specs = [_class_rows_spec(ROW_BLK, r, W) for r in dils for _ in range(2)] + [_vec_spec(W)]
    return pl.pallas_call(
        body, name=name, grid=(T // ROW_BLK,), in_specs=in_specs,
        out_specs=(_row_spec(W), _row_spec(W), _row_spec(W)),
        out_shape=(jax.ShapeDtypeStruct((T, W), F32), jax.ShapeDtypeStruct((T, W), BF16),
                   jax.ShapeDtypeStruct((T, W), F32)),
        scratch_shapes=[_token_scratch(ROW_BLK, W)],
        compiler_params=pltpu.CompilerParams(dimension_semantics=("parallel",)),
    )(*args)


def _attn_delta(dmix, outs, lses, sink, dils, *, name):
    T = dmix.shape[0]
    W = MIX_W
    nd = len(dils)

    def body(dmix_ref, oa_ref, ob_ref, la_ref, lb_ref, sink_ref, *rest):
        doa_ref, sta_ref, dsink_ref = rest[0:3]
        dob_refs, stb_refs = rest[3:3 + nd], rest[3 + nd:3 + 2 * nd]
        do_tile, st_tile = rest[3 + 2 * nd:]

        @pl.when(pl.program_id(0) == 0)
        def _():
            dsink_ref[...] = jnp.zeros_like(dsink_ref)

        sel0, lo = _head_sel(0), _stat_lo()
        for mixer, (o_ref, l_ref) in enumerate(((oa_ref, la_ref), (ob_ref, lb_ref))):
            for i in range(N_PAIRS):
                cols = slice(i * LANES, (i + 1) * LANES)
                do = dmix_ref[:, mixer * W + i * LANES:mixer * W + (i + 1) * LANES]
                prod = do * o_ref[:, cols]
                d0 = jnp.sum(jnp.where(sel0, prod, 0.0), axis=1, keepdims=True)
                d1 = jnp.sum(jnp.where(sel0, 0.0, prod), axis=1, keepdims=True)
                delta = jnp.where(sel0, d0, d1)
                lse = l_ref[:, cols]
                stat = jnp.where(lo, lse, delta)
                if mixer == 0:
                    sta_ref[:, cols] = stat
                    doa_ref[:, cols] = do.astype(doa_ref.dtype)
                    dsink_ref[:, cols] += -jnp.sum(jnp.exp(sink_ref[:, cols] - lse) * delta, axis=0, keepdims=True)
                else:
                    st_tile[i] = stat
                    do_tile[i] = do
        for t, r in enumerate(dils):
            _from_tokens(dob_refs[t], r, do_tile)
            _from_tokens(stb_refs[t], r, st_tile)

    class_specs = [_class_rows_spec(ROW_BLK, r, W) for r in dils]
    out_shape = [jax.ShapeDtypeStruct((T, W), BF16), jax.ShapeDtypeStruct((T, W), F32), jax.ShapeDtypeStruct((1, W), F32)]
    out_shape += [jax.ShapeDtypeStruct((T // r, r * W), BF16) for r in dils]
    out_shape += [jax.ShapeDtypeStruct((T // r, r * W), F32) for r in dils]
    res = pl.pallas_call(
        body, name=name, grid=(T // ROW_BLK,),
        in_specs=[_row_spec(2 * W), _row_spec(W), _row_spec(W), _row_spec(W), _row_spec(W), _vec_spec(W)],
        out_specs=tuple([_row_spec(W), _row_spec(W), _vec_spec(W)] + class_specs + class_specs),
        out_shape=tuple(out_shape),
        scratch_shapes=[_token_scratch(ROW_BLK, W), _token_scratch(ROW_BLK, W)],
        compiler_params=pltpu.CompilerParams(dimension_semantics=("arbitrary",)),
    )(dmix, outs[0], outs[1], lses[0], lses[1], sink)
    return res[0], res[1], res[2], res[3:3 + nd], res[3 + nd:]


def _band_bwd(qa, ka, va, d_out, stat, bias, *, r, q_col, k_col, v_col, stride, pattern, name, kv_shared=False,
              carry=None):
    L = qa.shape[0]
    nb = L // QBLK
    dn_nt = (((1,), (1,)), ((), ()))
    dn_tn = (((0,), (0,)), ((), ()))
    scale = HEAD_DIM ** -0.5

    pps = N_PAIRS if r > 1 else N_PAIRS // 2

    def body(q_ref, k_ref, v_ref, do_ref, st_ref, bias_ref, dq_ref, dk_ref, dv_ref, db_ref, dk_carry, dv_carry):
        @pl.when((pl.program_id(0) == 0) & (pl.program_id(1) == 0))
        def _():
            db_ref[...] = jnp.zeros_like(db_ref)

        lo, sel0 = _stat_lo(), _head_sel(0)
        pair0 = pl.program_id(1) * pps

        def block(b, first):
            rows = _rows(b)
            for i in range(pps):
                heads = pl.ds(2 * (pair0 + i), 2)
                cols = slice(i * LANES, (i + 1) * LANES)
                q2 = _stack_heads(q_ref[rows, cols], scale)
                kv_head = (pair0 + i) // 2 if kv_shared else None
                k = _key_window(k_ref, b, cols, first, kv_head)
                v = _key_window(v_ref, b, cols, first, kv_head)
                do2 = _stack_heads(do_ref[rows, cols])
                st = st_ref[rows, cols]
                lse2 = jnp.concatenate(
                    [jnp.max(jnp.where(_head_sel(h) & lo, st, -jnp.inf), axis=1, keepdims=True) for h in range(2)], axis=0)
                delta2 = jnp.concatenate(
                    [jnp.sum(jnp.where(_head_sel(h) & ~lo, st, 0.0), axis=1, keepdims=True) for h in range(2)],
                    axis=0) * (2.0 / HEAD_DIM)
                bias2 = bias_ref[1 if first else 0, heads].reshape(2 * QBLK, 2 * QBLK)
                s = lax.dot_general(q2, k, dn_nt, preferred_element_type=F32) + bias2
                p = jnp.exp(s - lse2)
                dp = lax.dot_general(do2, v, dn_nt, preferred_element_type=F32)
                ds = p * (dp - delta2)
                db_ref[heads] += ds.reshape(2, QBLK, 2 * QBLK)
                dsb = ds.astype(k.dtype)
                dq2 = jnp.dot(dsb, k, preferred_element_type=F32)
                dq_ref[rows, cols] = (jnp.where(sel0, dq2[0:QBLK], dq2[QBLK:]) * scale).astype(dq_ref.dtype)
                dk_acc = lax.dot_general(dsb, q2, dn_tn, preferred_element_type=F32)
                dv_acc = lax.dot_general(p.astype(k.dtype), do2, dn_tn, preferred_element_type=F32)
                if not first:
                    prev = _rows(b - 1)
                    dk_ref[prev, cols] = (dk_carry[:, cols] + dk_acc[0:QBLK]).astype(dk_ref.dtype)
                    dv_ref[prev, cols] = (dv_carry[:, cols] + dv_acc[0:QBLK]).astype(dv_ref.dtype)
                dk_carry[:, cols] = dk_acc[QBLK:2 * QBLK]
                dv_carry[:, cols] = dv_acc[QBLK:2 * QBLK]

        block(0, True)
        if nb > 1:
            pl.loop(1, nb)(lambda b: block(b, False))

        last = _rows(nb - 1)
        dk_ref[last, :] = dk_carry[...].astype(dk_ref.dtype)
        dv_ref[last, :] = dv_carry[...].astype(dv_ref.dtype)

    tok_spec = _class_spec(L, r, 0, 1, pps)
    bias_spec = pl.BlockSpec((None, 2, 2 * N_PAIRS, QBLK, 2 * QBLK), lambda j, c: (pattern, 0, 0, 0, 0))
    dbias_spec = pl.BlockSpec((2 * N_PAIRS, QBLK, 2 * QBLK), lambda j, c: (0, 0, 0))
    grad = jax.ShapeDtypeStruct((L, r * MIX_W), BF16)
    blk_bytes = _nbytes((L, pps * LANES), BF16)
    carry_shape = pltpu.VMEM((QBLK, pps * LANES), F32)
    return _call(
        body, name=name, grid=(r, N_PAIRS // pps),
        in_specs=[_class_spec(L, r, q_col, stride, pps), _kv_spec(L, r, k_col, stride, pps, kv_shared),
                  _kv_spec(L, r, v_col, stride, pps, kv_shared), tok_spec, tok_spec, bias_spec],
        out_specs=[tok_spec, tok_spec, tok_spec, dbias_spec],
        out_shape=[grad, grad, grad, jax.ShapeDtypeStruct((2 * N_PAIRS, QBLK, 2 * QBLK), F32)],
        args=[qa, ka, va, d_out, stat, bias], scratch=[carry_shape, carry_shape],
        vmem_limit=_vmem_limit(*([blk_bytes] * 9)), semantics=("arbitrary", "arbitrary"), carry=carry)


def _dz_assemble(dq_a, dk_a, dv_a, b_grads, dils, *, name):
    T = dq_a.shape[0]
    W = MIX_W

    def group_sum(x):
        u0 = x[:, 0:LANES] + x[:, LANES:2 * LANES]
        u1 = x[:, 2 * LANES:3 * LANES] + x[:, 3 * LANES:4 * LANES]
        s0 = u0 + pltpu.roll(u0, HEAD_DIM, 1)
        s1 = u1 + pltpu.roll(u1, HEAD_DIM, 1)
        return jnp.where(_head_sel(0), s0, s1)

    def body(*refs):
        dqa_ref, dka_ref, dva_ref = refs[0:3]
        b_refs = refs[3:12]
        dza_ref, dzb_ref, sa_ref, sb_ref, scratch = refs[12:]

        @pl.when(pl.program_id(0) == 0)
        def _():
            sa_ref[...] = jnp.zeros_like(sa_ref)
            sb_ref[...] = jnp.zeros_like(sb_ref)

        def put(dst, acc, col, part):
            w = part.shape[1]
            dst[:, col:col + w] = part.astype(dst.dtype)
            acc[:, col:col + w] += jnp.sum(part, axis=0, keepdims=True)

        put(dza_ref, sa_ref, 0, dqa_ref[...].astype(F32))
        put(dza_ref, sa_ref, W, group_sum(dka_ref[...].astype(F32)))
        put(dza_ref, sa_ref, W + LANES, group_sum(dva_ref[...].astype(F32)))
        for t in range(3):
            part = _to_tokens(b_refs[t], dils[0], scratch)
            for pat in range(1, len(dils)):
                part = part + _to_tokens(b_refs[3 * pat + t], dils[pat], scratch)
            put(dzb_ref, sb_ref, t * W, part)

    args = [dq_a, dk_a, dv_a] + [g[t] for g in b_grads for t in range(3)]
    return pl.pallas_call(
        body, name=name, grid=(T // ROW_BLK,),
        in_specs=[_row_spec(W)] * 3 + [_class_rows_spec(ROW_BLK, r, W) for r in dils for _ in range(3)],
        out_specs=(_row_spec(ZA_W), _row_spec(ZB_W), _vec_spec(ZA_W), _vec_spec(ZB_W)),
        out_shape=(jax.ShapeDtypeStruct((T, ZA_W), BF16), jax.ShapeDtypeStruct((T, ZB_W), BF16),
                   jax.ShapeDtypeStruct((1, ZA_W), F32), jax.ShapeDtypeStruct((1, ZB_W), F32)),
        scratch_shapes=[_token_scratch(ROW_BLK, W)],
        compiler_params=pltpu.CompilerParams(dimension_semantics=("arbitrary",)),
    )(*args)


def _place():
    x, y, c = lax.axis_index("x"), lax.axis_index("y"), lax.axis_index("c")
    chips = [(1 - x, y), (x, 1 - y), (1 - x, 1 - y)]
    return x, y, c, chips


def _cast_place(shards, k_idx, *, name):
    n, steps = len(shards), 4

    def body(k_ref, *refs):
        for s_ref, o_ref in zip(refs[:n], refs[n:]):
            o_ref[...] = s_ref[...].astype(o_ref.dtype)

    blocks = [(a.shape[0] // steps, a.shape[1]) for a in shards]
    grid_spec = pltpu.PrefetchScalarGridSpec(
        num_scalar_prefetch=1, grid=(steps,),
        in_specs=[pl.BlockSpec(blk, lambda t, k_ref: (t, 0)) for blk in blocks],
        out_specs=tuple(pl.BlockSpec(blk, lambda t, k_ref: (k_ref[0] * steps + t, 0)) for blk in blocks))
    return pl.pallas_call(
        body, name=name, grid_spec=grid_spec,
        out_shape=tuple(jax.ShapeDtypeStruct((N_CHIPS * a.shape[0], a.shape[1]), BF16) for a in shards),
        compiler_params=pltpu.CompilerParams(dimension_semantics=("parallel",),
                                             vmem_limit_bytes=_vmem_limit(*[_nbytes(b, F32) for b in blocks])),
    )(k_idx, *shards)


def _gather_carry(fulls, jobs):
    n = len(jobs)

    def piece(ref, chip_idx, half, part):
        rs = ref.shape[0] // N_CHIPS
        rh = rs // 2
        rows = rh // part[1]
        return ref.at[pl.ds(chip_idx * rs + half * rh + part[0] * rows, rows)]

    def copy(sems, sem, src_ref, dst_ref, to):
        return pltpu.make_async_remote_copy(src_ref=src_ref, dst_ref=dst_ref, send_sem=sems[0].at[sem],
                                            recv_sem=sems[1].at[sem], device_id=to, device_id_type=MESH)

    def to_chips(ins, outs, sems, j, arrival, hops=("both", "chips")):
        i, part, hop = jobs[j]
        x, y, c, chips = _place()
        res = []
        for t, chip in enumerate(chips if hop in hops else ()):
            theirs = piece(outs[i], 2 * chip[0] + chip[1], c, part)
            src, dst = (theirs, theirs) if arrival else (piece(ins[i], 2 * x + y, c, part), piece(outs[i], 2 * x + y, c, part))
            res.append(copy(sems, 3 * j + t, src, dst, (*chip, c)))
        return res

    def to_sibling(outs, sems, j, arrival, hops=("both", "sibling")):
        i, part, hop = jobs[j]
        x, y, c, chips = _place()
        res = []
        for t, chip in enumerate(chips if hop in hops else ()):
            region = piece(outs[i], 2 * chip[0] + chip[1], 1 - c if arrival else c, part)
            res.append(copy(sems, 3 * n + 3 * j + t, region, region, (x, y, 1 - c)))
        return res

    def start(ins, outs, sems):
        for j in range(n):
            for send in to_chips(ins, outs, sems, j, False) + to_sibling(outs, sems, j, False, ("sibling",)):
                send.start()

    def before_last(ins, outs, sems):
        for j in range(n):
            for arrival, send in zip(to_chips(ins, outs, sems, j, True, ("both",)),
                                     to_sibling(outs, sems, j, False, ("both",))):
                arrival.wait_recv()
                send.start()

    def finish(ins, outs, sems):
        for j in range(n):
            for arrival in to_chips(ins, outs, sems, j, True, ("chips",)) + to_sibling(outs, sems, j, True):
                arrival.wait_recv()
        for j in range(n):
            for send in to_chips(ins, outs, sems, j, False) + to_sibling(outs, sems, j, False):
                send.wait_send()

    return _Carry(fulls, [jax.ShapeDtypeStruct(a.shape, a.dtype) for a in fulls], {i: i for i in range(len(fulls))},
                  [pltpu.SemaphoreType.DMA((6 * n,)), pltpu.SemaphoreType.DMA((6 * n,))], start, finish, before_last)


def _pair_swap_carry(grads):
    n = len(grads)

    def copy(ins, outs, sems, i):
        x, y, c, _ = _place()
        return pltpu.make_async_remote_copy(src_ref=ins[i].at[:, 1 - c], dst_ref=outs[i], send_sem=sems[0].at[i],
                                            recv_sem=sems[1].at[i], device_id=(x, y, 1 - c), device_id_type=MESH)

    def start(ins, outs, sems):
        for i in range(n):
            copy(ins, outs, sems, i).start()

    def finish(ins, outs, sems):
        for i in range(n):
            copy(ins, outs, sems, i).wait()

    return _Carry(grads, [jax.ShapeDtypeStruct((a.shape[0], a.shape[2], a.shape[3]), a.dtype) for a in grads], {},
                  [pltpu.SemaphoreType.DMA((n,)), pltpu.SemaphoreType.DMA((n,))], start, finish)


def _pair_sum(g4, got, c_idx, *, name):
    _, _, rh, cs = g4.shape
    rb = _row_block(rh, 512, 16)

    def body(c_ref, own_ref, got_ref, o_ref):
        o_ref[...] = (own_ref[...].astype(F32) + got_ref[...].astype(F32)).astype(o_ref.dtype)

    grid_spec = pltpu.PrefetchScalarGridSpec(
        num_scalar_prefetch=1, grid=(N_CHIPS, rh // rb),
        in_specs=[pl.BlockSpec((None, None, rb, cs), lambda k, t, c_ref: (k, c_ref[0], t, 0)),
                  pl.BlockSpec((None, rb, cs), lambda k, t, c_ref: (k, t, 0))],
        out_specs=pl.BlockSpec((None, rb, cs), lambda k, t, c_ref: (k, t, 0)))
    return pl.pallas_call(
        body, name=name, grid_spec=grid_spec, out_shape=jax.ShapeDtypeStruct((N_CHIPS, rh, cs), BF16),
        compiler_params=pltpu.CompilerParams(dimension_semantics=("parallel", "parallel")),
    )(c_idx, g4, got)


def _chip_scatter_carry(sums):
    n = len(sums)

    def copies(ins, outs, sems):
        x, y, c, chips = _place()
        return [pltpu.make_async_remote_copy(src_ref=ins[i].at[2 * chip[0] + chip[1]], dst_ref=outs[i].at[j],
                                             send_sem=sems[0].at[3 * i + j], recv_sem=sems[1].at[3 * i + j],
                                             device_id=(*chip, c), device_id_type=MESH)
                for i in range(n) for j, chip in enumerate(chips)]

    def start(ins, outs, sems):
        for cp in copies(ins, outs, sems):
            cp.start()

    def finish(ins, outs, sems):
        for cp in copies(ins, outs, sems):
            cp.wait()

    return _Carry(sums, [jax.ShapeDtypeStruct((3,) + a.shape[1:], a.dtype) for a in sums], {},
                  [pltpu.SemaphoreType.DMA((3 * n,)), pltpu.SemaphoreType.DMA((3 * n,))], start, finish)


def _chip_sum(sums, gots, kc_idx, *, name):
    n, steps = len(sums), 2

    def body(kc_ref, *refs):
        for own_ref, got_ref, o_ref in zip(refs[:n], refs[n:2 * n], refs[2 * n:]):
            acc = own_ref[...].astype(F32)
            for j in range(3):
                acc = acc + got_ref[j].astype(F32)
            o_ref[...] = acc

    blocks = [(a.shape[1] // steps, a.shape[2]) for a in sums]
    grid_spec = pltpu.PrefetchScalarGridSpec(
        num_scalar_prefetch=1, grid=(steps,),
        in_specs=[pl.BlockSpec((None,) + blk, lambda t, kc: (kc[0], t, 0)) for blk in blocks]
        + [pl.BlockSpec((3,) + blk, lambda t, kc: (0, t, 0)) for blk in blocks],
        out_specs=tuple(pl.BlockSpec(blk, lambda t, kc: (kc[1] * steps + t, 0)) for blk in blocks))
    return pl.pallas_call(
        body, name=name, grid_spec=grid_spec,
        out_shape=tuple(jax.ShapeDtypeStruct((2 * a.shape[1], a.shape[2]), F32) for a in sums),
        compiler_params=pltpu.CompilerParams(dimension_semantics=("parallel",),
                                             vmem_limit_bytes=_vmem_limit(*[3 * _nbytes(b, F32) for b in blocks])),
    )(kc_idx, *sums, *gots)


def _half_swap_carry(shards):
    n = len(shards)

    def copy(ins, outs, sems, i, to_my_half):
        x, y, c, _ = _place()
        rh = ins[i].shape[0] // 2
        half = c if to_my_half else 1 - c
        return pltpu.make_async_remote_copy(
            src_ref=ins[i].at[pl.ds(c * rh, rh)], dst_ref=outs[i].at[pl.ds(half * rh, rh)],
            send_sem=sems[0].at[i], recv_sem=sems[1].at[i], device_id=(x, y, 1 - c), device_id_type=MESH)

    def start(ins, outs, sems):
        for i in range(n):
            copy(ins, outs, sems, i, True).start()

    def finish(ins, outs, sems):
        for i in range(n):
            copy(ins, outs, sems, i, False).wait_recv()
        for i in range(n):
            copy(ins, outs, sems, i, True).wait_send()

    return _Carry(shards, [jax.ShapeDtypeStruct(a.shape, a.dtype) for a in shards], {i: i for i in range(n)},
                  [pltpu.SemaphoreType.DMA((n,)), pltpu.SemaphoreType.DMA((n,))], start, finish)


SMALL_ROW = 1024


def _small_layout(shapes):
    starts, row = [], 0
    for r, c in shapes:
        starts.append(row)
        row += -(-c // SMALL_ROW) if r == 1 else r
    return starts, -(-row // SUBLANES) * SUBLANES


def _small_pieces(shape, start):
    r, c = shape
    if r == 1:
        return [(start + q, min(SMALL_ROW, c - q * SMALL_ROW), q * SMALL_ROW) for q in range(-(-c // SMALL_ROW))]
    return None


def _whole_spec(shape):
    return pl.BlockSpec(tuple(shape), lambda i: (0,) * len(shape))


def _small_all_reduce(parts, *, name):
    shapes = [a.shape for a in parts]
    starts, rows = _small_layout(shapes)
    n = len(parts)
    n_tables = sum(1 for r, _ in shapes if r > 1)
    assert all(r > 1 and c <= LANES for r, c in shapes[:n_tables]) and all(r == 1 for r, _ in shapes[n_tables:])
    table_rows = starts[n_tables]
    assert table_rows % SUBLANES == 0

    def regions(slot):
        return slot.at[pl.ds(0, table_rows), pl.ds(0, LANES)], slot.at[pl.ds(table_rows, rows - table_rows)]

    def body(*refs):
        ins, outs = refs[0:n], refs[n:2 * n]
        buf, send_sems, recv_sems = refs[2 * n:]
        x, y, c, _ = _place()
        me = 4 * x + 2 * y + c
        mine = buf.at[me]
        mine[...] = jnp.zeros((rows, SMALL_ROW), F32)
        for ref, shape, start in zip(ins, shapes, starts):
            pieces = _small_pieces(shape, start)
            if pieces is None:
                mine[start:start + shape[0], 0:shape[1]] = ref[...]
            else:
                for row, width, col in pieces:
                    mine[row:row + 1, 0:width] = ref[:, col:col + width]
        def peer(d):
            return 1 - x if d & 4 else x, 1 - y if d & 2 else y, 1 - c if d & 1 else c

        def copy(d, part, dst_slot):
            sem = 2 * (d - 1) + part
            return pltpu.make_async_remote_copy(
                src_ref=regions(mine)[part], dst_ref=regions(dst_slot)[part], send_sem=send_sems.at[sem],
                recv_sem=recv_sems.at[sem], device_id=peer(d), device_id_type=MESH)

        for d in range(1, 8):
            for part in range(2):
                copy(d, part, mine).start()
        for d in range(1, 8):
            px, py, pc = peer(d)
            for part in range(2):
                copy(d, part, buf.at[4 * px + 2 * py + pc]).wait_recv()
        for d in range(1, 8):
            for part in range(2):
                copy(d, part, mine).wait_send()
        tables, vectors = regions(buf.at[0])
        acc_t, acc_v = tables[...], vectors[...]
        for s in range(1, 8):
            tables, vectors = regions(buf.at[s])
            acc_t, acc_v = acc_t + tables[...], acc_v + vectors[...]
        tables, vectors = regions(mine)
        tables[...] = acc_t
        vectors[...] = acc_v
        for ref, shape, start in zip(outs, shapes, starts):
            pieces = _small_pieces(shape, start)
            if pieces is None:
                ref[...] = mine[start:start + shape[0], 0:shape[1]]
            else:
                for row, width, col in pieces:
                    ref[:, col:col + width] = mine[row:row + 1, 0:width]

    whole = [_whole_spec(s) for s in shapes]
    return pl.pallas_call(
        body, name=name, grid=(1,), in_specs=whole, out_specs=tuple(whole),
        out_shape=tuple(jax.ShapeDtypeStruct(s, F32) for s in shapes),
        scratch_shapes=[pltpu.VMEM((8, rows, SMALL_ROW), F32), pltpu.SemaphoreType.DMA((14,)),
                        pltpu.SemaphoreType.DMA((14,))],
    )(*parts)


def _adamw_small(ws, gs, ms, vs, *, name):
    n = len(ws)
    c1 = 1.0 - ADAM_B1 ** ADAM_STEP
    c2 = 1.0 - ADAM_B2 ** ADAM_STEP

    def body(*refs):
        for k in range(n):
            w_ref, g_ref, m_ref, v_ref = (refs[t * n + k] for t in range(4))
            go_ref, d_ref, mo_ref, vo_ref = (refs[(4 + t) * n + k] for t in range(4))
            gv = g_ref[...]
            go_ref[...] = gv
            mn = ADAM_B1 * m_ref[...] + (1.0 - ADAM_B1) * gv
            vn = ADAM_B2 * v_ref[...] + (1.0 - ADAM_B2) * (gv * gv)
            d_ref[...] = -ADAM_LR * ((mn / c1) / (jnp.sqrt(vn / c2) + ADAM_EPS) + ADAM_WD * w_ref[...])
            mo_ref[...] = mn
            vo_ref[...] = vn

    whole = [_whole_spec(a.shape) for a in ws]
    shapes = tuple(jax.ShapeDtypeStruct(a.shape, F32) for a in ws)
    res = pl.pallas_call(
        body, name=name, grid=(1,), in_specs=whole * 4, out_specs=tuple(whole * 4), out_shape=shapes * 4,
    )(*ws, *gs, *ms, *vs)
    return res[0:n], res[n:2 * n], res[2 * n:3 * n], res[3 * n:4 * n]


def _adamw(ws, gs, ms, vs, *, name):
    n, steps = len(ws), 8
    c1 = 1.0 - ADAM_B1 ** ADAM_STEP
    c2 = 1.0 - ADAM_B2 ** ADAM_STEP

    def body(*refs):
        for k in range(n):
            w_ref, g_ref, m_ref, v_ref = (refs[t * n + k] for t in range(4))
            go_ref, d_ref, mo_ref, vo_ref = (refs[(4 + t) * n + k] for t in range(4))
            gv = g_ref[...]
            go_ref[...] = gv
            mn = ADAM_B1 * m_ref[...] + (1.0 - ADAM_B1) * gv
            vn = ADAM_B2 * v_ref[...] + (1.0 - ADAM_B2) * (gv * gv)
            d_ref[...] = -ADAM_LR * ((mn / c1) / (jnp.sqrt(vn / c2) + ADAM_EPS) + ADAM_WD * w_ref[...])
            mo_ref[...] = mn
            vo_ref[...] = vn

    blocks = [(a.shape[0] // steps, a.shape[1]) for a in ws]
    specs = [pl.BlockSpec(blk, lambda i: (i, 0)) for blk in blocks]
    shapes = tuple(jax.ShapeDtypeStruct(a.shape, F32) for a in ws)
    res = pl.pallas_call(
        body, name=name, grid=(steps,), in_specs=specs * 4, out_specs=tuple(specs * 4), out_shape=shapes * 4,
        compiler_params=pltpu.CompilerParams(dimension_semantics=("parallel",),
                                             vmem_limit_bytes=_vmem_limit(*[8 * _nbytes(b, F32) for b in blocks])),
    )(*ws, *gs, *ms, *vs)
    return res[0:n], res[n:2 * n], res[2 * n:3 * n], res[3 * n:4 * n]


def _local_step(x, p, target, small, sched):
    T = x.shape[0]
    W = MIX_W
    rel_bias = small["rel_bias"]
    b_in_a, b_in_b = small["b_in"][:, 0:ZA_W], small["b_in"][:, ZA_W:]

    idx_np = [_t5_index(A_WINDOW - 1, 1)] + [_t5_index(window // dil, dil) for window, dil in B_PATTERNS]
    idx_all = jnp.asarray(np.stack(idx_np))
    n1, bias_all = sched.run(_prologue, x, small["ffn1_pre_g"], rel_bias, idx_all, name="prologue")
    w_gu1 = sched.weight("ffn1_w_gu")
    *gu1, a1 = sched.run(_ffn_up, n1, w_gu1, bm=512, name="ffn1_gu")
    w_d1 = sched.weight("ffn1_w_down")
    f1, h1, n2 = sched.run(_mm_resid_norm, a1, w_d1, x, small["ffn1_post_g"], 0.5, small["attn_pre_g"], bm=512,
                           name="ffn1_down")
    win_a, win_b = sched.weight("w_in")
    za = _mm_nt(n2, win_a, bm=1024, bn=ZA_W, out_dtype=BF16, name="in_proj_a", add=b_in_a)
    dils = tuple(dil for _, dil in B_PATTERNS)
    zb_by_dil = _mm_nt_classes(n2, win_b, b_in_b, dils, bm=512, name="in_proj_b")

    a_args = dict(r=1, q_col=0, k_col=W // LANES, v_col=W // LANES + 1, stride=0, pattern=0, kv_shared=True)
    sink_row = jnp.repeat(small["sinks"], HEAD_DIM, axis=1)
    out_a, out_a_bf, lse_a = sched.run(_band_fwd, za, za, za, bias_all, name="band_a_fwd", sink=sink_row, **a_args)

    no_sink = jnp.full((1, W), NEG, F32)
    b_ctx, branches = [], []
    for t, dil in enumerate(dils):
        zb_r = zb_by_dil[t]
        args = dict(r=dil, q_col=0, k_col=1, v_col=2, stride=ZB_W // W, pattern=1 + t)
        branches.append(sched.run(_band_fwd, zb_r, zb_r, zb_r, bias_all, name=f"band_b{t}_fwd", **args))
        b_ctx.append((jnp.asarray(idx_np[1 + t]), zb_r, args))
    out_b, out_b_bf, lse_b = _merge(branches, dils, no_sink, name="merge_b")

    mix = jnp.concatenate([out_a_bf, out_b_bf], axis=1)
    w_out = sched.weight("w_out")
    att, h2, n3 = _mm_resid_norm(mix, w_out, h1, small["attn_post_g"], 1.0, small["ffn2_pre_g"], bm=512,
                                 name="out_proj", bias=small["b_out"])
    w_gu2, w_d2 = sched.weight("ffn2_w_gu"), sched.weight("ffn2_w_down")
    *gu2, a2 = _ffn_up(n3, w_gu2, bm=512, name="ffn2_gu")
    f2, h3, n4 = _mm_resid_norm(a2, w_d2, h2, small["ffn2_post_g"], 0.5, small["ple_pre_g"], bm=512,
                                name="ffn2_down")
    w_gate = sched.weight("w_ple_gate")
    p_bf = p.astype(BF16)
    dh4, de, dgp, loss, d_ple_post = _ple_head(n4, w_gate, p_bf, sched.weight("w_ple_proj"), h3, small["ple_post_g"],
                                               target, bm=512, name="ple_head")

    gs = {"ple_post_g": d_ple_post}
    sched.grad("w_ple_proj", _mm_tn(p_bf, de, bm=256, bn=1024, out_dtype=BF16, name="d_w_ple"))
    sched.grad("w_ple_gate", _mm_tn(n4, dgp, bm=512, bn=1024, out_dtype=BF16, name="d_w_gate"))
    dh3, df2, gs["ple_pre_g"], gs["ffn2_post_g"], _ = sched.run(
        _mm_nt_norms_bwd, dgp, w_gate, dh4, h3, small["ple_pre_g"], f2, small["ffn2_post_g"], 0.5, bm=512, name="d_n4")

    def ffn_bwd(df, a, gu, n, w_down, w_gu, tag, *norm_args):
        dgu = sched.run(_ffn_down_bwd, df, w_down, gu[0], gu[1], bm=512, name=f"ffn{tag}_d_a")
        sched.grad(f"ffn{tag}_w_gu", _mm_tn(n, dgu, bm=512, bn=1408, out_dtype=BF16, name=f"ffn{tag}_d_w_gu",
                                            n_stack=N_CHIPS))
        sched.grad(f"ffn{tag}_w_down", sched.run(_mm_tn, a, df, bm=256, bn=1024, out_dtype=BF16,
                                                name=f"ffn{tag}_d_w_down"))
        return sched.run(_mm_nt_norms_bwd, dgu, w_gu, *norm_args, bm=512, name=f"ffn{tag}_d_n")

    dh2, datt, gs["ffn2_pre_g"], gs["attn_post_g"], gs["b_out"] = ffn_bwd(
        df2, a2, gu2, n3, w_d2, w_gu2, "2", dh3, h2, small["ffn2_pre_g"], att, small["attn_post_g"], 1.0)
    sched.grad("w_out", _mm_tn(mix, datt, bm=512, bn=1024, out_dtype=BF16, name="d_w_out"))
    dmix = sched.run(_mm_nt, datt, w_out, bm=1024, bn=1024, out_dtype=F32, name="d_mix")

    do_a, stat_a, dsink, do_b, stat_b = _attn_delta(dmix, (out_a, out_b), (lse_a, lse_b), sink_row, dils,
                                                    name="attn_delta")
    gs["sinks"] = dsink[:, ::HEAD_DIM]
    dq_a, dk_a, dv_a, dbias_a = sched.run(_band_bwd, za, za, za, do_a, stat_a, bias_all, name="band_a_bwd", **a_args)
    d_rel_a = _bias_grad(dbias_a, jnp.asarray(idx_np[0]), name="d_rel_a")
    b_grads = []
    d_rel_b = None
    for t, (idx, zb_r, args) in enumerate(b_ctx):
        dq, dk, dv, dbias = sched.run(_band_bwd, zb_r, zb_r, zb_r, do_b[t], stat_b[t], bias_all,
                                      name=f"band_b{t}_bwd", **args)
        b_grads.append((dq, dk, dv))
        d_rel = _bias_grad(dbias, idx, name=f"d_rel_b{t}")
        d_rel_b = d_rel if d_rel_b is None else d_rel_b + d_rel
    gs["rel_bias"] = jnp.concatenate([d_rel_a[:, 0:2 * N_PAIRS], d_rel_b[:, 0:2 * N_PAIRS]], axis=1)
    dza, dzb, dsum_a, dsum_b = _dz_assemble(dq_a, dk_a, dv_a, b_grads, dils, name="d_z")
    gs["b_in"] = jnp.concatenate([dsum_a, dsum_b], axis=1)
    sched.grad("w_in", (_mm_tn(dza, n2, bm=ZA_W, bn=1024, out_dtype=BF16, name="d_w_in_a"),
                        _mm_tn(dzb, n2, bm=ZB_W // 2, bn=1024, out_dtype=BF16, name="d_w_in_b")))
    dn2_a = _mm_nn(dza, win_a, bm=1024, bn=1024, out_dtype=F32, name="d_n2_a")
    dh1, df1, gs["attn_pre_g"], gs["ffn1_post_g"], _ = sched.run(
        _mm_nt_norms_bwd, dzb, win_b, dh2, h1, small["attn_pre_g"], f1, small["ffn1_post_g"], 0.5, bm=512,
        name="d_n2_b", add=dn2_a, b_is_kn=True)
    grad_x, gs["ffn1_pre_g"] = ffn_bwd(df1, a1, gu1, n1, w_d1, w_gu1, "1", dh1, x, small["ffn1_pre_g"], None, None, 0.0)
    return loss, grad_x, gs


def _to_compute(name, full):
    st = full.reshape(N_CHIPS, full.shape[0] // N_CHIPS, full.shape[1])
    if name in ("ffn1_w_gu", "ffn2_w_gu"):
        return st
    if name == "w_ple_proj":
        return jnp.transpose(st, (1, 0, 2)).reshape(st.shape[1], -1)
    if name == "w_in":
        return full[0:ZA_W], full[ZA_W:]
    return full


def _to_comm(name, g):
    if name == "w_in":
        g = jnp.concatenate(g, axis=0)
    if name == "w_ple_proj":
        g = jnp.transpose(g.reshape(g.shape[0], N_CHIPS, -1), (1, 0, 2))
    rs = g.shape[1] if g.ndim == 3 else g.shape[0] // N_CHIPS
    return g.reshape(N_CHIPS, 2, rs // 2, g.shape[-1])


class _Schedule:
    GATHER = {
        "prologue": [("ffn1_w_gu", (0, 1), "both")],
        "ffn1_gu": [("ffn1_w_down", (0, 1), "both"), ("w_in", (0, 1), "both")],
        "band_a_fwd": [("ffn2_w_gu", (0, 2), "chips"), ("w_out", (0, 1), "chips")],
        "band_b0_fwd": [("ffn2_w_gu", (0, 2), "sibling"), ("w_out", (0, 1), "sibling"),
                        ("ffn2_w_gu", (1, 2), "chips"), ("w_ple_gate", (0, 1), "chips"), ("w_ple_proj", (0, 1), "chips")],
        "band_b1_fwd": [("ffn2_w_gu", (1, 2), "sibling"), ("w_ple_gate", (0, 1), "sibling"),
                        ("w_ple_proj", (0, 1), "sibling"), ("ffn2_w_down", (0, 1), "both")],
    }
    SWAP = {"ffn2_d_w_down": ["w_ple_proj", "w_ple_gate", "ffn2_w_gu"], "ffn2_d_n": ["ffn2_w_down"],
            "band_a_bwd": ["w_out"], "d_n2_b": ["w_in"], "ffn1_d_w_down": ["ffn1_w_gu"], "ffn1_d_n": ["ffn1_w_down"]}
    SCATTER = {"ffn2_d_n": ["ffn2_w_gu"], "band_a_bwd": ["w_ple_proj", "w_ple_gate", "ffn2_w_down"],
               "d_n2_b": ["w_out"], "ffn1_d_w_down": ["w_in"], "ffn1_d_n": ["ffn1_w_gu"]}

    def __init__(self, placed, c_idx, kc_idx):
        self.full = dict(placed)
        self.missing = {n: 1.0 for n in placed}
        self.c_idx, self.kc_idx = c_idx, kc_idx
        self.grads, self.swapped, self.pair_sums, self.scattered = {}, {}, {}, {}

    def _gather(self, entries):
        if not entries:
            return [], []
        names = list(dict.fromkeys(n for n, _, _ in entries))
        carry = _gather_carry([self.full[n] for n in names], [(names.index(n), pt, hops) for n, pt, hops in entries])

        def done():
            self.full.update(zip(names, carry.results))
            for n, part, hops in entries:
                if hops != "chips":
                    self.missing[n] -= 1.0 / part[1]
        return [carry], [done]

    def weight(self, name):
        assert abs(self.missing[name]) < 1e-9, (name, self.missing[name])
        return _to_compute(name, self.full[name])

    def grad(self, name, g):
        self.grads[name] = _to_comm(name, g)

    def _swap(self, names):
        carry = _pair_swap_carry([self.grads[n] for n in names])

        def done():
            self.swapped.update(zip(names, carry.results))
        return carry, done

    def _scatter(self, names):
        for n in names:
            self.pair_sums[n] = _pair_sum(self.grads[n], self.swapped[n], self.c_idx, name=f"grad_pair_sum_{n}")
        carry = _chip_scatter_carry([self.pair_sums[n] for n in names])

        def done():
            self.scattered.update(zip(names, carry.results))
        return carry, done

    def run(self, fn, *args, name, **kw):
        carries, after = self._gather(self.GATHER.get(name, []))
        for names, make in ((self.SWAP.get(name), self._swap), (self.SCATTER.get(name), self._scatter)):
            if names:
                carry, done = make(names)
                carries.append(carry)
                after.append(done)
        out = fn(*args, name=name, carry=_join(carries), **kw)
        for done in after:
            done()
        return out

    def finish(self):
        left = [n for n in BIG if n not in self.scattered]
        to_swap = [n for n in left if n not in self.swapped]
        if to_swap:
            carry, done = self._swap(to_swap)
            _comm_call(carry, name="grad_pair_swap")
            done()
        if left:
            carry, done = self._scatter(left)
            _comm_call(carry, name="grad_chip_scatter")
            done()
        halves = _chip_sum([self.pair_sums[n] for n in BIG], [self.scattered[n] for n in BIG], self.kc_idx,
                           name="grad_chip_sum")
        carry = _half_swap_carry(halves)
        _comm_call(carry, name="grad_half_swap")
        return dict(zip(BIG, carry.results))


def kernel(x, p, rel_bias, ffn1_pre_g, ffn1_w_gu, ffn1_w_down, ffn1_post_g, attn_pre_g, w_in, b_in, sinks, w_out, b_out, attn_post_g, ffn2_pre_g, ffn2_w_gu, ffn2_w_down, ffn2_post_g, ple_pre_g, w_ple_gate, w_ple_proj, ple_post_g, loss_target, m_rel_bias, m_ffn1_pre_g, m_ffn1_w_gu, m_ffn1_w_down, m_ffn1_post_g, m_attn_pre_g, m_w_in, m_b_in, m_sinks, m_w_out, m_b_out, m_attn_post_g, m_ffn2_pre_g, m_ffn2_w_gu, m_ffn2_w_down, m_ffn2_post_g, m_ple_pre_g, m_w_ple_gate, m_w_ple_proj, m_ple_post_g, v_rel_bias, v_ffn1_pre_g, v_ffn1_w_gu, v_ffn1_w_down, v_ffn1_post_g, v_attn_pre_g, v_w_in, v_b_in, v_sinks, v_w_out, v_b_out, v_attn_post_g, v_ffn2_pre_g, v_ffn2_w_gu, v_ffn2_w_down, v_ffn2_post_g, v_ple_pre_g, v_w_ple_gate, v_w_ple_proj, v_ple_post_g):
    given = dict(locals())
    w = {n: given[n] for n in WEIGHTS}
    m = {n: given["m_" + n] for n in WEIGHTS}
    v = {n: given["v_" + n] for n in WEIGHTS}
    c_pos = lax.axis_index("c").astype(jnp.int32)
    k_pos = (2 * lax.axis_index("x") + lax.axis_index("y")).astype(jnp.int32)
    c_idx, k_idx, kc_idx = c_pos.reshape(1), k_pos.reshape(1), jnp.stack([k_pos, c_pos])

    def shard(a, n):
        return jnp.transpose(a[0]) if n == "w_in" else a[0]

    def unshard(a, n):
        return (jnp.transpose(a) if n == "w_in" else a)[None]

    placed = _cast_place([shard(w[n], n) for n in BIG], k_idx, name="place_shards")
    sched = _Schedule(dict(zip(BIG, placed)), c_idx, kc_idx)
    small = {n: (w[n] if n == "rel_bias" else w[n].reshape(1, -1)) for n in SMALL}

    loss_part, grad_x, gs = _local_step(x[0], p[0, 0], loss_target[0], small, sched)

    grads_big = sched.finish()

    *small_grads, loss_row = _small_all_reduce([gs[n] for n in SMALL] + [loss_part], name="small_all_reduce")
    loss = loss_row[0, 0]

    grads, delta, new_m, new_v = {}, {}, {}, {}
    res = _adamw([shard(w[n], n) for n in BIG], [grads_big[n] for n in BIG], [shard(m[n], n) for n in BIG],
                 [shard(v[n], n) for n in BIG], name="adamw_big")
    for n, gg, dd, mm, vv in zip(BIG, *res):
        grads[n], delta[n], new_m[n], new_v[n] = (unshard(a, n) for a in (gg, dd, mm, vv))
    res = _adamw_small([w[n] for n in SMALL], small_grads, [m[n] for n in SMALL], [v[n] for n in SMALL],
                       name="adamw_small")
    for name, gg, dd, mm, vv in zip(SMALL, *res):
        grads[name], delta[name], new_m[name], new_v[name] = gg, dd, mm, vv

    return (loss, grad_x[None], *[grads[n] for n in WEIGHTS], *[delta[n] for n in WEIGHTS],
            *[new_m[n] for n in WEIGHTS], *[new_v[n] for n in WEIGHTS])
```

```python
import math
from functools import partial

import jax
import jax.numpy as jnp
import numpy as np
from jax import lax
from jax.experimental import pallas as pl
from jax.experimental.pallas import tpu as pltpu

F32 = jnp.float32
BF16 = jnp.bfloat16
MESH = pl.DeviceIdType.MESH

EPS = 1e-6
NEG = -1e30
LANES = 128
SUBLANES = 8
HEAD_DIM = 64
QBLK = 128
N_PAIRS = 4
MIX_W = N_PAIRS * LANES
A_WINDOW = 128
B_PATTERNS = ((128, 1), (512, 4), (2048, 16))
NUM_BUCKETS = 32
MAX_DISTANCE = 2048
ZA_W = 768
ZB_W = 1536
N_CHIPS = 4
VMEM_BYTES_V7X = 64 * 2**20

ADAM_LR, ADAM_B1, ADAM_B2, ADAM_EPS, ADAM_WD, ADAM_STEP = 0.001, 0.9, 0.999, 1e-08, 0.01, 10

BIG = ("ffn1_w_gu", "ffn1_w_down", "w_in", "w_out", "ffn2_w_gu", "ffn2_w_down", "w_ple_gate", "w_ple_proj")
SMALL = ("rel_bias", "ffn1_pre_g", "ffn1_post_g", "attn_pre_g", "b_in", "sinks", "b_out", "attn_post_g",
         "ffn2_pre_g", "ffn2_post_g", "ple_pre_g", "ple_post_g")
WEIGHTS = ("rel_bias", "ffn1_pre_g", "ffn1_w_gu", "ffn1_w_down", "ffn1_post_g", "attn_pre_g", "w_in", "b_in",
           "sinks", "w_out", "b_out", "attn_post_g", "ffn2_pre_g", "ffn2_w_gu", "ffn2_w_down", "ffn2_post_g",
           "ple_pre_g", "w_ple_gate", "w_ple_proj", "ple_post_g")


def _vmem_limit(*block_bytes):
    need = 2 * sum(block_bytes) + max(block_bytes) * 2 + (4 << 20)
    return int(min(max(need, 32 << 20), VMEM_BYTES_V7X - (6 << 20)))


def _nbytes(shape, dtype):
    return int(np.prod(shape)) * jnp.dtype(dtype).itemsize


MXU_WIDTH_V7X = 256


def _col_chunks(n):
    return [slice(c, min(c + MXU_WIDTH_V7X, n)) for c in range(0, n, MXU_WIDTH_V7X)]


def _row_halves(rows):
    return [slice(0, rows // 2), slice(rows // 2, rows)]


def _row_block(rows, cap, mult):
    for cand in range(min(rows, cap), 0, -1):
        if rows % cand == 0 and cand % mult == 0:
            return cand
    raise ValueError((rows, cap, mult))


class _Carry:
    def __init__(self, operands, out_shapes, aliases, sems, start, finish, before_last=None):
        self.operands, self.out_shapes, self.aliases, self.sems = list(operands), list(out_shapes), dict(aliases), list(sems)
        self.start, self.finish = start, finish
        self.before_last = before_last or (lambda ins, outs, sems: None)
        self.results = None


def _join(carries):
    carries = [c for c in carries if c is not None]
    if not carries:
        return None
    if len(carries) == 1:
        return carries[0]
    offs, pos = [], [0, 0, 0]
    for c in carries:
        offs.append(tuple(pos))
        pos = [pos[0] + len(c.operands), pos[1] + len(c.out_shapes), pos[2] + len(c.sems)]
    aliases = {}
    for c, (oi, oo, _) in zip(carries, offs):
        aliases.update({oi + i: oo + o for i, o in c.aliases.items()})

    def run(which):
        def fn(ins, outs, sems):
            for c, (oi, oo, os_) in zip(carries, offs):
                getattr(c, which)(ins[oi:oi + len(c.operands)], outs[oo:oo + len(c.out_shapes)],
                                  sems[os_:os_ + len(c.sems)])
        return fn

    joined = _Carry([a for c in carries for a in c.operands], [s for c in carries for s in c.out_shapes], aliases,
                    [s for c in carries for s in c.sems], run("start"), run("finish"), run("before_last"))
    joined.parts = (carries, offs)
    return joined


def _set_results(carry, outs):
    carry.results = list(outs)
    if hasattr(carry, "parts"):
        for c, (_, oo, _) in zip(*carry.parts):
            _set_results(c, outs[oo:oo + len(c.out_shapes)])


ANY = pl.BlockSpec(memory_space=pl.ANY)


def _call(body, *, name, grid, in_specs, out_specs, out_shape, args, scratch=(), vmem_limit=None, carry=None,
          semantics=None):
    n_ci, n_co, n_cs = len(args), len(out_shape), len(scratch)
    semantics = semantics or ("parallel",) * len(grid)
    if carry is None:
        res = pl.pallas_call(
            body, name=name, grid=grid, in_specs=list(in_specs), out_specs=tuple(out_specs), out_shape=tuple(out_shape),
            scratch_shapes=list(scratch),
            compiler_params=pltpu.CompilerParams(dimension_semantics=semantics, vmem_limit_bytes=vmem_limit),
        )(*args)
        return list(res)
    n_pi, n_po = len(carry.operands), len(carry.out_shapes)

    def full_body(*refs):
        ci, pi = refs[:n_ci], refs[n_ci:n_ci + n_pi]
        o0 = n_ci + n_pi
        co, po = refs[o0:o0 + n_co], refs[o0 + n_co:o0 + n_co + n_po]
        s0 = o0 + n_co + n_po
        cs, ps = refs[s0:s0 + n_cs], refs[s0 + n_cs:]
        ids = [pl.program_id(ax) for ax in range(len(grid))]
        first, last = ids[0] == 0, ids[0] == grid[0] - 1
        for ax in range(1, len(grid)):
            first, last = first & (ids[ax] == 0), last & (ids[ax] == grid[ax] - 1)

        @pl.when(first)
        def _():
            carry.start(pi, po, ps)

        several_steps = any(g > 1 for g in grid)
        if several_steps:
            @pl.when(last)
            def _():
                carry.before_last(pi, po, ps)

        body(*ci, *co, *cs)

        @pl.when(last)
        def _():
            if not several_steps:
                carry.before_last(pi, po, ps)
            carry.finish(pi, po, ps)

    res = pl.pallas_call(
        full_body, name=name, grid=grid, in_specs=list(in_specs) + [ANY] * n_pi,
        out_specs=tuple(out_specs) + tuple([ANY] * n_po), out_shape=tuple(out_shape) + tuple(carry.out_shapes),
        scratch_shapes=list(scratch) + carry.sems,
        input_output_aliases={n_ci + i: n_co + o for i, o in carry.aliases.items()},
        compiler_params=pltpu.CompilerParams(dimension_semantics=("arbitrary",) * len(grid),
                                             vmem_limit_bytes=vmem_limit),
    )(*args, *carry.operands)
    _set_results(carry, res[n_co:])
    return list(res[:n_co])


def _comm_call(carry, *, name):
    n_pi, n_po = len(carry.operands), len(carry.out_shapes)

    def body(*refs):
        pi, po, ps = refs[:n_pi], refs[n_pi:n_pi + n_po], refs[n_pi + n_po:]
        carry.start(pi, po, ps)
        carry.before_last(pi, po, ps)
        carry.finish(pi, po, ps)

    res = pl.pallas_call(
        body, name=name, in_specs=[ANY] * n_pi, out_specs=tuple([ANY] * n_po), out_shape=tuple(carry.out_shapes),
        scratch_shapes=carry.sems, input_output_aliases=dict(carry.aliases),
    )(*carry.operands)
    _set_results(carry, res)


def _mm_nn(a, b, *, bm, bn, out_dtype, name, bias=None, carry=None):
    M, K = a.shape
    stacked = b.ndim == 3
    N = b.shape[0] * b.shape[2] if stacked else b.shape[1]
    assert M % bm == 0 and N % bn == 0 and (not stacked or bn == b.shape[2])

    def body(*refs):
        a_ref, b_ref = refs[0], refs[1]
        o_ref = refs[-1]
        acc = jnp.dot(a_ref[...], b_ref[...], preferred_element_type=F32)
        if bias is not None:
            acc = acc + refs[2][...]
        o_ref[...] = acc.astype(o_ref.dtype)

    if stacked:
        b_spec = pl.BlockSpec((None, K, bn), lambda i, j: (j, 0, 0))
    else:
        b_spec = pl.BlockSpec((K, bn), lambda i, j: (0, j))
    in_specs = [pl.BlockSpec((bm, K), lambda i, j: (i, 0)), b_spec]
    args = [a, b]
    if bias is not None:
        in_specs.append(pl.BlockSpec((1, bn), lambda i, j: (0, j)))
        args.append(bias)
    limit = _vmem_limit(_nbytes((bm, K), a.dtype), _nbytes((K, bn), b.dtype), _nbytes((bm, bn), F32))
    return _call(body, name=name, grid=(M // bm, N // bn), in_specs=in_specs,
                 out_specs=[pl.BlockSpec((bm, bn), lambda i, j: (i, j))],
                 out_shape=[jax.ShapeDtypeStruct((M, N), out_dtype)], args=args, vmem_limit=limit, carry=carry)[0]


def _mm_nt(a, b, *, bm, bn, out_dtype, name, add=None, carry=None):
    M, K = a.shape
    stacked = b.ndim == 3
    N = b.shape[1] if stacked else b.shape[0]
    n_sh = b.shape[0] if stacked else 1
    ks = b.shape[2] if stacked else K
    assert M % bm == 0 and N % bn == 0 and n_sh * ks == K
    dn = (((1,), (1,)), ((), ()))

    def body(*refs):
        a_ref, b_ref = refs[0], refs[1]
        o_ref = refs[-1]
        if stacked:
            acc = lax.dot_general(a_ref[:, 0:ks], b_ref[0], dn, preferred_element_type=F32)
            for s in range(1, n_sh):
                acc = acc + lax.dot_general(a_ref[:, s * ks:(s + 1) * ks], b_ref[s], dn, preferred_element_type=F32)
        else:
            acc = lax.dot_general(a_ref[...], b_ref[...], dn, preferred_element_type=F32)
        if add is not None:
            acc = acc + refs[2][...]
        o_ref[...] = acc.astype(o_ref.dtype)

    if stacked:
        b_spec = pl.BlockSpec((n_sh, bn, ks), lambda i, j: (0, j, 0))
    else:
        b_spec = pl.BlockSpec((bn, K), lambda i, j: (j, 0))
    in_specs = [pl.BlockSpec((bm, K), lambda i, j: (i, 0)), b_spec]
    args = [a, b]
    if add is not None:
        rows = bm if add.shape[0] == M else 1
        in_specs.append(pl.BlockSpec((rows, bn), lambda i, j: (i if rows == bm else 0, j)))
        args.append(add)
    limit = _vmem_limit(_nbytes((bm, K), a.dtype), _nbytes((bn, K), b.dtype), _nbytes((bm, bn), F32))
    return _call(body, name=name, grid=(M // bm, N // bn), in_specs=in_specs,
                 out_specs=[pl.BlockSpec((bm, bn), lambda i, j: (i, j))],
                 out_shape=[jax.ShapeDtypeStruct((M, N), out_dtype)], args=args, vmem_limit=limit, carry=carry)[0]


def _mm_tn(a, b, *, bm, bn, out_dtype, name, n_stack=None, carry=None):
    T, M = a.shape
    N = b.shape[1]
    assert M % bm == 0 and N % bn == 0 and (n_stack is None or bn * n_stack == N)
    dn = (((0,), (0,)), ((), ()))

    def body(a_ref, b_ref, o_ref):
        acc = lax.dot_general(a_ref[...], b_ref[...], dn, preferred_element_type=F32)
        o_ref[...] = acc.astype(o_ref.dtype)

    if n_stack is None:
        out_spec = pl.BlockSpec((bm, bn), lambda i, j: (i, j))
        out_shape = jax.ShapeDtypeStruct((M, N), out_dtype)
    else:
        out_spec = pl.BlockSpec((None, bm, bn), lambda i, j: (j, i, 0))
        out_shape = jax.ShapeDtypeStruct((n_stack, M, bn), out_dtype)
    limit = _vmem_limit(_nbytes((T, bm), a.dtype), _nbytes((T, bn), b.dtype), _nbytes((bm, bn), F32))
    return _call(body, name=name, grid=(M // bm, N // bn),
                 in_specs=[pl.BlockSpec((T, bm), lambda i, j: (0, i)), pl.BlockSpec((T, bn), lambda i, j: (0, j))],
                 out_specs=[out_spec], out_shape=[out_shape], args=[a, b], vmem_limit=limit, carry=carry)[0]


ROW_BLK = 256


def _rstd(x):
    return lax.rsqrt(jnp.mean(x * x, axis=-1, keepdims=True) + EPS)


def _norm_bwd(xhat, rstd, dxhat):
    return rstd * (dxhat - xhat * jnp.mean(dxhat * xhat, axis=-1, keepdims=True))


def _row_spec(cols):
    return pl.BlockSpec((ROW_BLK, cols), lambda i: (i, 0))


def _vec_spec(cols):
    return pl.BlockSpec((1, cols), lambda i: (0, 0))


def _prologue(x, g, rel_bias, idx_all, *, name, carry=None):
    T, D = x.shape
    n_pat = idx_all.shape[0]
    heads = 2 * N_PAIRS
    steps = n_pat * heads
    rows = T // steps

    def body(rb_ref, x_ref, g_ref, idx_ref, n_ref, bias_ref):
        s = pl.program_id(0)
        head = jnp.where(s < heads, s, heads + s % heads)
        xv = x_ref[...]
        n_ref[...] = (xv * _rstd(xv) * g_ref[...]).astype(n_ref.dtype)
        idxv = idx_ref[...]
        acc = jnp.where(idxv < 0, NEG, 0.0).astype(F32)
        for b in range(NUM_BUCKETS):
            acc = acc + jnp.where(idxv == b, rb_ref[b, head], 0.0)
        bias_ref[0] = acc
        kap = lax.broadcasted_iota(jnp.int32, (1, 2 * QBLK), 1)
        bias_ref[1] = jnp.where(kap < QBLK, NEG, acc)

    return _call(
        body, name=name, grid=(steps,),
        in_specs=[pl.BlockSpec(memory_space=pltpu.SMEM), pl.BlockSpec((rows, D), lambda s: (s, 0)),
                  pl.BlockSpec((1, D), lambda s: (0, 0)),
                  pl.BlockSpec((None, QBLK, 2 * QBLK), lambda s: (s // heads, 0, 0))],
        out_specs=[pl.BlockSpec((rows, D), lambda s: (s, 0)),
                   pl.BlockSpec((None, 2, None, QBLK, 2 * QBLK), lambda s: (s // heads, 0, s % heads, 0, 0))],
        out_shape=[jax.ShapeDtypeStruct((T, D), BF16),
                   jax.ShapeDtypeStruct((n_pat, 2, heads, QBLK, 2 * QBLK), F32)],
        args=[rel_bias, x, g, idx_all], carry=carry)


def _mm_resid_norm(a, b, h, g_post, coef, g_next, *, bm, name, bias=None, carry=None):
    M, K = a.shape
    N = b.shape[1]

    def body(*refs):
        a_ref, b_ref, h_ref, gp_ref, gn_ref = refs[0:5]
        f_ref, hn_ref, n_ref = refs[-3:]
        for rows in _row_halves(bm):
            f = jnp.dot(a_ref[rows, :], b_ref[...], preferred_element_type=F32)
            if bias is not None:
                f = f + refs[5][...]
            f_ref[rows, :] = f
            hn = h_ref[rows, :] + coef * (f * _rstd(f) * gp_ref[...])
            hn_ref[rows, :] = hn
            n_ref[rows, :] = (hn * _rstd(hn) * gn_ref[...]).astype(n_ref.dtype)

    row = lambda w: pl.BlockSpec((bm, w), lambda i: (i, 0))
    vec = pl.BlockSpec((1, N), lambda i: (0, 0))
    in_specs = [row(K), pl.BlockSpec((K, N), lambda i: (0, 0), pipeline_mode=pl.Buffered(1)), row(N), vec, vec]
    args = [a, b, h, g_post, g_next]
    if bias is not None:
        in_specs.append(vec)
        args.append(bias)
    limit = _vmem_limit(_nbytes((bm, K), a.dtype), _nbytes((K, N), b.dtype) // 2, 4 * _nbytes((bm, N), F32))
    return _call(body, name=name, grid=(M // bm,), in_specs=in_specs, out_specs=[row(N), row(N), row(N)],
                 out_shape=[jax.ShapeDtypeStruct((M, N), F32), jax.ShapeDtypeStruct((M, N), F32),
                            jax.ShapeDtypeStruct((M, N), BF16)], args=args, vmem_limit=limit, carry=carry)


def _mm_nt_norms_bwd(a, b, dh_in, h, g_pre, f, g_post, coef, *, bm, name, add=None, b_is_kn=False, carry=None):
    M, K = a.shape
    stacked = b.ndim == 3
    N = b.shape[1] if (stacked or b_is_kn) else b.shape[0]
    n_sh = b.shape[0] if stacked else 1
    ks = b.shape[2] if stacked else K
    assert n_sh * ks == K
    dn_dims = (((1,), (0,)), ((), ())) if b_is_kn else (((1,), (1,)), ((), ()))
    with_f = f is not None
    n_in = 5 + (2 if with_f else 0) + (1 if add is not None else 0)

    def body(*refs):
        a_ref, b_ref, dhi_ref, h_ref, gpre_ref = refs[0:5]
        outs = refs[n_in:]

        @pl.when(pl.program_id(0) == 0)
        def _():
            for acc in (outs[2:] if with_f else outs[1:]):
                acc[...] = jnp.zeros_like(acc)

        for rows in _row_halves(bm):
            if stacked:
                dn = lax.dot_general(a_ref[rows, 0:ks], b_ref[0], dn_dims, preferred_element_type=F32)
                for s in range(1, n_sh):
                    dn = dn + lax.dot_general(a_ref[rows, s * ks:(s + 1) * ks], b_ref[s], dn_dims,
                                              preferred_element_type=F32)
            else:
                dn = lax.dot_general(a_ref[rows, :], b_ref[...], dn_dims, preferred_element_type=F32)
            if add is not None:
                dn = dn + refs[n_in - 1][rows, :]
            hv = h_ref[rows, :]
            r = _rstd(hv)
            hh = hv * r
            dh = dhi_ref[rows, :] + _norm_bwd(hh, r, dn * gpre_ref[...])
            outs[0][rows, :] = dh
            if not with_f:
                outs[1][...] += jnp.sum(dn * hh, axis=0, keepdims=True)
                continue
            f_ref, gpost_ref = refs[5], refs[6]
            df_ref, dgpre_ref, dgpost_ref, dsum_ref = outs[1:]
            dgpre_ref[...] += jnp.sum(dn * hh, axis=0, keepdims=True)
            fv = f_ref[rows, :]
            rf = _rstd(fv)
            fh = fv * rf
            dy = coef * dh
            dgpost_ref[...] += jnp.sum(dy * fh, axis=0, keepdims=True)
            df = _norm_bwd(fh, rf, dy * gpost_ref[...])
            dsum_ref[...] += jnp.sum(df, axis=0, keepdims=True)
            df_ref[rows, :] = df.astype(df_ref.dtype)

    row = lambda w: pl.BlockSpec((bm, w), lambda i: (i, 0))
    vec = pl.BlockSpec((1, N), lambda i: (0, 0))
    once = pl.Buffered(1)
    if stacked:
        b_spec = pl.BlockSpec((n_sh, N, ks), lambda i: (0, 0, 0), pipeline_mode=once)
    else:
        b_spec = pl.BlockSpec(b.shape, lambda i: (0, 0), pipeline_mode=once)
    in_specs = [row(K), b_spec, row(N), row(N), vec]
    args = [a, b, dh_in, h, g_pre]
    if with_f:
        in_specs += [row(N), vec]
        args += [f, g_post]
    if add is not None:
        in_specs.append(row(N))
        args.append(add)
    vec_shape = jax.ShapeDtypeStruct((1, N), F32)
    out_specs = [row(N)] + ([row(N), vec, vec, vec] if with_f else [vec])
    out_shape = [jax.ShapeDtypeStruct((M, N), F32)]
    out_shape += [jax.ShapeDtypeStruct((M, N), BF16), vec_shape, vec_shape, vec_shape] if with_f else [vec_shape]
    limit = _vmem_limit(_nbytes((bm, K), a.dtype), _nbytes((N, K), b.dtype) // 2, 6 * _nbytes((bm, N), F32))
    return _call(body, name=name, grid=(M // bm,), in_specs=in_specs, out_specs=out_specs, out_shape=out_shape,
                 args=args, vmem_limit=limit, semantics=("arbitrary",), carry=carry)


def _ffn_up(n, w_gu, *, bm, name, carry=None):
    M, K = n.shape
    n_sh, _, ns = w_gu.shape
    half = n_sh // 2

    def body(n_ref, wg_ref, wu_ref, g_ref, u_ref, a_ref):
        nv = n_ref[...]
        for cols in _col_chunks(ns):
            g = jnp.dot(nv, wg_ref[:, cols], preferred_element_type=F32)
            u = jnp.dot(nv, wu_ref[:, cols], preferred_element_type=F32)
            g_ref[:, cols] = g.astype(g_ref.dtype)
            u_ref[:, cols] = u.astype(u_ref.dtype)
            a_ref[:, cols] = (g * jax.nn.sigmoid(g) * u).astype(a_ref.dtype)

    out_spec = pl.BlockSpec((bm, ns), lambda i, j: (i, j))
    out = jax.ShapeDtypeStruct((M, half * ns), BF16)
    limit = _vmem_limit(_nbytes((bm, K), n.dtype), 2 * _nbytes((K, ns), w_gu.dtype), 3 * _nbytes((bm, ns), F32))
    return _call(body, name=name, grid=(M // bm, half),
                 in_specs=[pl.BlockSpec((bm, K), lambda i, j: (i, 0)),
                           pl.BlockSpec((None, K, ns), lambda i, j: (j, 0, 0)),
                           pl.BlockSpec((None, K, ns), lambda i, j: (j + half, 0, 0))],
                 out_specs=[out_spec, out_spec, out_spec], out_shape=[out, out, out], args=[n, w_gu, w_gu],
                 vmem_limit=limit, carry=carry)


def _ffn_down_bwd(df, w_down, g, u, *, bm, name, carry=None):
    M, D = df.shape
    F = w_down.shape[0]
    dn = (((1,), (1,)), ((), ()))

    def body(df_ref, w_ref, g_ref, u_ref, o_ref):
        dfv = df_ref[...]
        for cols in _col_chunks(F):
            da = lax.dot_general(dfv, w_ref[cols, :], dn, preferred_element_type=F32)
            gv = g_ref[:, cols].astype(F32)
            s = jax.nn.sigmoid(gv)
            o_ref[:, cols] = (da * u_ref[:, cols].astype(F32) * (s * (1.0 + gv * (1.0 - s)))).astype(o_ref.dtype)
            o_ref[:, F + cols.start:F + cols.stop] = (da * (gv * s)).astype(o_ref.dtype)

    row = lambda w: pl.BlockSpec((bm, w), lambda i: (i, 0))
    limit = _vmem_limit(_nbytes((F, D), w_down.dtype) // 2, 2 * _nbytes((bm, F), BF16), _nbytes((bm, 2 * F), BF16))
    return _call(body, name=name, grid=(M // bm,),
                 in_specs=[row(D), pl.BlockSpec((F, D), lambda i: (0, 0), pipeline_mode=pl.Buffered(1)), row(F), row(F)],
                 out_specs=[row(2 * F)], out_shape=[jax.ShapeDtypeStruct((M, 2 * F), BF16)],
                 args=[df, w_down, g, u], vmem_limit=limit, carry=carry)[0]


def _ple_head(n4, w_gate, p, w_ple, h3, g_post, target, *, bm, name):
    T, D = h3.shape
    kp = p.shape[1]

    def body(n_ref, wg_ref, p_ref, wp_ref, h_ref, g_ref, t_ref, dh_ref, de_ref, dgp_ref, loss_ref, dg_ref):
        @pl.when(pl.program_id(0) == 0)
        def _():
            loss_ref[...] = jnp.zeros_like(loss_ref)
            dg_ref[...] = jnp.zeros_like(dg_ref)

        gpost = g_ref[...]
        for rows in _row_halves(bm):
            gate = jax.nn.sigmoid(jnp.dot(n_ref[rows, :], wg_ref[...], preferred_element_type=F32))
            ev = jnp.dot(p_ref[rows, :], wp_ref[...], preferred_element_type=F32)
            ge = gate * ev
            r = _rstd(ge)
            gh = ge * r
            diff = h_ref[rows, :] + gh * gpost - t_ref[rows, :]
            loss_ref[...] += jnp.sum(diff * diff) * (0.5 / D)
            dh = diff * (1.0 / D)
            dh_ref[rows, :] = dh
            dg_ref[...] += jnp.sum(dh * gh, axis=0, keepdims=True)
            dge = _norm_bwd(gh, r, dh * gpost)
            de_ref[rows, :] = (dge * gate).astype(de_ref.dtype)
            dgp_ref[rows, :] = (dge * ev * gate * (1.0 - gate)).astype(dgp_ref.dtype)

    row = lambda w: pl.BlockSpec((bm, w), lambda i: (i, 0))
    whole = lambda a: pl.BlockSpec(a.shape, lambda i: (0, 0), pipeline_mode=pl.Buffered(1))
    limit = _vmem_limit(_nbytes((bm, D), BF16), _nbytes(w_gate.shape, w_gate.dtype) // 2, 6 * _nbytes((bm, D), F32))
    return pl.pallas_call(
        body, name=name, grid=(T // bm,),
        in_specs=[row(D), whole(w_gate), row(kp), whole(w_ple), row(D), _vec_spec(D), row(D)],
        out_specs=(row(D), row(D), row(D), _vec_spec(LANES), _vec_spec(D)),
        out_shape=(jax.ShapeDtypeStruct((T, D), F32), jax.ShapeDtypeStruct((T, D), BF16),
                   jax.ShapeDtypeStruct((T, D), BF16), jax.ShapeDtypeStruct((1, LANES), F32),
                   jax.ShapeDtypeStruct((1, D), F32)),
        compiler_params=pltpu.CompilerParams(dimension_semantics=("arbitrary",), vmem_limit_bytes=limit),
    )(n4, w_gate, p, w_ple, h3, g_post, target)


def _t5_index(max_dist, stride):
    rho = np.arange(QBLK)[:, None]
    kap = np.arange(2 * QBLK)[None, :]
    d = rho + QBLK - kap
    valid = (d >= 0) & (d <= max_dist)
    n = np.maximum(d, 0) * stride
    max_exact = NUM_BUCKETS // 2
    nf = np.maximum(n, 1).astype(np.float32)
    large = max_exact + (np.log(nf / np.float32(max_exact)) / np.float32(math.log(MAX_DISTANCE / max_exact))
                         * np.float32(NUM_BUCKETS - max_exact)).astype(np.int32)
    large = np.minimum(large, NUM_BUCKETS - 1)
    bucket = np.where(n < max_exact, n, large)
    return np.where(valid, bucket, -1).astype(np.int32)


def _bias_grad(d_bias, idx, *, name):
    def body(db_ref, idx_ref, o_ref):
        h = pl.program_id(0)

        @pl.when(h == 0)
        def _():
            o_ref[...] = jnp.zeros_like(o_ref)

        idxv = idx_ref[...]
        dv = db_ref[0]
        rows = lax.broadcasted_iota(jnp.int32, (NUM_BUCKETS, LANES), 0)
        lanes = lax.broadcasted_iota(jnp.int32, (NUM_BUCKETS, LANES), 1)
        acc = o_ref[...]
        for b in range(NUM_BUCKETS):
            s = jnp.sum(jnp.where(idxv == b, dv, 0.0))
            acc = acc + jnp.where((rows == b) & (lanes == h), s, 0.0)
        o_ref[...] = acc

    return pl.pallas_call(
        body, name=name, grid=(2 * N_PAIRS,),
        in_specs=[pl.BlockSpec((1, QBLK, 2 * QBLK), lambda h: (h, 0, 0)),
                  pl.BlockSpec((QBLK, 2 * QBLK), lambda h: (0, 0))],
        out_specs=pl.BlockSpec((NUM_BUCKETS, LANES), lambda h: (0, 0)),
        out_shape=jax.ShapeDtypeStruct((NUM_BUCKETS, LANES), F32),
        compiler_params=pltpu.CompilerParams(dimension_semantics=("arbitrary",)),
    )(d_bias, idx)


def _lane():
    return lax.broadcasted_iota(jnp.int32, (1, LANES), 1)


def _head_sel(h):
    return (_lane() < HEAD_DIM) if h == 0 else (_lane() >= HEAD_DIM)


def _stat_lo():
    return (_lane() % HEAD_DIM) < (HEAD_DIM // 2)


def _stack_heads(t, scale=None):
    if scale is not None:
        t = t * scale
    zero = jnp.zeros_like(t)
    return jnp.concatenate([jnp.where(_head_sel(0), t, zero), jnp.where(_head_sel(1), t, zero)], axis=0)


def _class_spec(L, r, cols, stride, pps):
    chunks = N_PAIRS // pps
    mode = pl.Buffered(1) if r * chunks == 1 else None
    return pl.BlockSpec((L, MIX_W // chunks), lambda j, c: (0, (cols + j * stride) * chunks + c), pipeline_mode=mode)


def _rows(b, n_blocks=1):
    return pl.ds(pl.multiple_of(b * QBLK, QBLK), n_blocks * QBLK)


def _key_window(ref, b, cols, first, kv_head=None):
    if kv_head is not None:
        cols = slice(0, LANES)
    if first:
        blk = ref[0:QBLK, cols]
        tile = jnp.concatenate([blk, blk], axis=0)
    else:
        tile = ref[_rows(b - 1, 2), cols]
    if kv_head is None:
        return tile
    wide = tile.astype(F32)
    keep = jnp.logical_xor(_lane() < HEAD_DIM, kv_head == 1)
    return jnp.where(keep, wide, pltpu.roll(wide, HEAD_DIM, 1)).astype(tile.dtype)


def _kv_spec(L, r, col, stride, pps, kv_shared):
    if not kv_shared:
        return _class_spec(L, r, col, stride, pps)
    mode = pl.Buffered(1) if pps == N_PAIRS else None
    return pl.BlockSpec((L, LANES), lambda j, c: (0, col), pipeline_mode=mode)


def _band_fwd(qa, ka, va, bias, *, r, q_col, k_col, v_col, stride, pattern, name, kv_shared=False, sink=None,
              carry=None):
    L = qa.shape[0]
    nb = L // QBLK
    dn = (((1,), (1,)), ((), ()))
    scale = HEAD_DIM ** -0.5

    pps = N_PAIRS

    def body(q_ref, k_ref, v_ref, bias_ref, *rest):
        sel0 = _head_sel(0)
        pair0 = pl.program_id(1) * pps

        def emit(rows, cols, o, lse):
            if sink is None:
                o_ref, lse_ref = rest
                o_ref[rows, cols] = o.astype(o_ref.dtype)
                lse_ref[rows, cols] = lse
                return
            sink_ref, out_ref, outb_ref, lse_ref = rest
            sk = sink_ref[:, cols]
            mx = jnp.maximum(lse, sk)
            w = jnp.exp(lse - mx)
            den = w + jnp.exp(sk - mx)
            out = o * (w / den)
            out_ref[rows, cols] = out
            outb_ref[rows, cols] = out.astype(outb_ref.dtype)
            lse_ref[rows, cols] = mx + jnp.log(den)

        def block(b, first):
            rows = _rows(b)
            for i in range(pps):
                cols = slice(i * LANES, (i + 1) * LANES)
                q2 = _stack_heads(q_ref[rows, cols], scale)
                kv_head = (pair0 + i) // 2 if kv_shared else None
                k = _key_window(k_ref, b, cols, first, kv_head)
                v = _key_window(v_ref, b, cols, first, kv_head)
                bias2 = bias_ref[1 if first else 0, pl.ds(2 * (pair0 + i), 2)].reshape(2 * QBLK, 2 * QBLK)
                s = lax.dot_general(q2, k, dn, preferred_element_type=F32) + bias2
                m = jnp.max(s, axis=1, keepdims=True)
                p = jnp.exp(s - m)
                l = jnp.sum(p, axis=1, keepdims=True)
                o = jnp.dot(p.astype(v.dtype), v, preferred_element_type=F32) * (1.0 / l)
                lse = m + jnp.log(l)
                emit(rows, cols, jnp.where(sel0, o[0:QBLK], o[QBLK:]), jnp.where(sel0, lse[0:QBLK], lse[QBLK:]))

        block(0, True)
        if nb > 1:
            pl.loop(1, nb)(lambda b: block(b, False))

    bias_spec = pl.BlockSpec((None, 2, 2 * N_PAIRS, QBLK, 2 * QBLK), lambda j, c: (pattern, 0, 0, 0, 0))
    out_spec = _class_spec(L, r, 0, 1, pps)
    blk_bytes = _nbytes((L, pps * LANES), F32)
    in_specs = [_class_spec(L, r, q_col, stride, pps), _kv_spec(L, r, k_col, stride, pps, kv_shared),
                _kv_spec(L, r, v_col, stride, pps, kv_shared), bias_spec]
    args = [qa, ka, va, bias]
    f32_out, bf16_out = jax.ShapeDtypeStruct((L, r * MIX_W), F32), jax.ShapeDtypeStruct((L, r * MIX_W), BF16)
    out_shape = [bf16_out, f32_out]
    if sink is not None:
        in_specs.append(pl.BlockSpec((1, MIX_W), lambda j, c: (0, 0)))
        args.append(sink)
        out_shape = [f32_out, bf16_out, f32_out]
    return _call(body, name=name, grid=(r, N_PAIRS // pps), in_specs=in_specs, out_specs=[out_spec] * len(out_shape),
                 out_shape=out_shape, args=args, vmem_limit=_vmem_limit(*([blk_bytes] * 4)), carry=carry)


def _class_rows_spec(rows, r, width):
    return pl.BlockSpec((rows // r, r * width), lambda i, *_: (i, 0))


def _token_scratch(rows, width):
    return pltpu.VMEM((width // LANES, rows, LANES), F32)


def _fill_tokens(scratch, value):
    for c in range(scratch.shape[0]):
        scratch[c] = value[:, c * LANES:(c + 1) * LANES]


def _read_tokens(scratch):
    return jnp.concatenate([scratch[c] for c in range(scratch.shape[0])], axis=1)


def _to_tokens(blk_ref, r, scratch):
    if r == 1:
        return blk_ref[...].astype(F32)
    nt, rows, _ = scratch.shape
    for j in range(r):
        for c in range(nt):
            lanes = slice((j * nt + c) * LANES, (j * nt + c + 1) * LANES)
            scratch.at[c][pl.ds(j, rows // r, stride=r), :] = blk_ref[:, lanes].astype(F32)
    return _read_tokens(scratch)


def _from_tokens(dst_ref, r, scratch):
    nt, rows, _ = scratch.shape
    if r == 1:
        dst_ref[...] = _read_tokens(scratch).astype(dst_ref.dtype)
        return
    for j in range(r):
        for c in range(nt):
            lanes = slice((j * nt + c) * LANES, (j * nt + c + 1) * LANES)
            dst_ref[:, lanes] = scratch.at[c][pl.ds(j, rows // r, stride=r), :].astype(dst_ref.dtype)


def _mm_nt_classes(a, b, bias, dils, *, bm, name):
    M, K = a.shape
    N = b.shape[0]
    dn = (((1,), (1,)), ((), ()))

    def body(a_ref, b_ref, bias_ref, *rest):
        outs, tile = rest[:-1], rest[-1]
        _fill_tokens(tile, lax.dot_general(a_ref[...], b_ref[...], dn, preferred_element_type=F32) + bias_ref[...])
        for out, r in zip(outs, dils):
            _from_tokens(out, r, tile)

    limit = _vmem_limit(_nbytes((bm, K), a.dtype), _nbytes((K, N), b.dtype) // 2, (1 + len(dils)) * _nbytes((bm, N), F32))
    return pl.pallas_call(
        body, name=name, grid=(M // bm,),
        in_specs=[pl.BlockSpec((bm, K), lambda i: (i, 0)),
                  pl.BlockSpec((N, K), lambda i: (0, 0), pipeline_mode=pl.Buffered(1)),
                  pl.BlockSpec((1, N), lambda i: (0, 0))],
        out_specs=tuple(_class_rows_spec(bm, r, N) for r in dils),
        out_shape=tuple(jax.ShapeDtypeStruct((M // r, r * N), BF16) for r in dils),
        scratch_shapes=[_token_scratch(bm, N)],
        compiler_params=pltpu.CompilerParams(dimension_semantics=("parallel",), vmem_limit_bytes=limit),
    )(a, b, bias)


def _merge(branches, dils, sink, *, name):
    W = MIX_W
    T = branches[0][0].shape[0] * dils[0]
    nbr = len(branches)

    def body(*refs):
        sink_ref = refs[2 * nbr]
        out_ref, outb_ref, lse_ref, scratch = refs[2 * nbr + 1:]
        sk = sink_ref[...]
        lses = [_to_tokens(refs[2 * t + 1], dils[t], scratch) for t in range(nbr)]
        mx = sk
        for lse in lses:
            mx = jnp.maximum(mx, lse)
        den = jnp.exp(sk - mx)
        num = jnp.zeros_like(mx)
        for t in range(nbr):
            w = jnp.exp(lses[t] - mx)
            den = den + w
            num = num + w * _to_tokens(refs[2 * t], dils[t], scratch)
        out = num / den
        out_ref[...] = out
        outb_ref[...] = out.astype(outb_ref.dtype)
        lse_ref[...] = mx + jnp.log(den)

    args = [a for br in branches for a in br] + [sink]
    in_specs = [_class_rows_spec(ROW_BLK, r, W) for r in dils for _ in range(2)] + [_vec_spec(W)]
    return pl.pallas_call(
        body, name=name, grid=(T // ROW_BLK,), in_specs=in_specs,
        out_specs=(_row_spec(W), _row_spec(W), _row_spec(W)),
        out_shape=(jax.ShapeDtypeStruct((T, W), F32), jax.ShapeDtypeStruct((T, W), BF16),
                   jax.ShapeDtypeStruct((T, W), F32)),
        scratch_shapes=[_token_scratch(ROW_BLK, W)],
        compiler_params=pltpu.CompilerParams(dimension_semantics=("parallel",)),
    )(*args)


def _attn_delta(dmix, outs, lses, sink, dils, *, name):
    T = dmix.shape[0]
    W = MIX_W
    nd = len(dils)

    def body(dmix_ref, oa_ref, ob_ref, la_ref, lb_ref, sink_ref, *rest):
        doa_ref, sta_ref, dsink_ref = rest[0:3]
        dob_refs, stb_refs = rest[3:3 + nd], rest[3 + nd:3 + 2 * nd]
        do_tile, st_tile = rest[3 + 2 * nd:]

        @pl.when(pl.program_id(0) == 0)
        def _():
            dsink_ref[...] = jnp.zeros_like(dsink_ref)

        sel0, lo = _head_sel(0), _stat_lo()
        for mixer, (o_ref, l_ref) in enumerate(((oa_ref, la_ref), (ob_ref, lb_ref))):
            for i in range(N_PAIRS):
                cols = slice(i * LANES, (i + 1) * LANES)
                do = dmix_ref[:, mixer * W + i * LANES:mixer * W + (i + 1) * LANES]
                prod = do * o_ref[:, cols]
                d0 = jnp.sum(jnp.where(sel0, prod, 0.0), axis=1, keepdims=True)
                d1 = jnp.sum(jnp.where(sel0, 0.0, prod), axis=1, keepdims=True)
                delta = jnp.where(sel0, d0, d1)
                lse = l_ref[:, cols]
                stat = jnp.where(lo, lse, delta)
                if mixer == 0:
                    sta_ref[:, cols] = stat
                    doa_ref[:, cols] = do.astype(doa_ref.dtype)
                    dsink_ref[:, cols] += -jnp.sum(jnp.exp(sink_ref[:, cols] - lse) * delta, axis=0, keepdims=True)
                else:
                    st_tile[i] = stat
                    do_tile[i] = do
        for t, r in enumerate(dils):
            _from_tokens(dob_refs[t], r, do_tile)
            _from_tokens(stb_refs[t], r, st_tile)

    class_specs = [_class_rows_spec(ROW_BLK, r, W) for r in dils]
    out_shape = [jax.ShapeDtypeStruct((T, W), BF16), jax.ShapeDtypeStruct((T, W), F32), jax.ShapeDtypeStruct((1, W), F32)]
    out_shape += [jax.ShapeDtypeStruct((T // r, r * W), BF16) for r in dils]
    out_shape += [jax.ShapeDtypeStruct((T // r, r * W), F32) for r in dils]
    res = pl.pallas_call(
        body, name=name, grid=(T // ROW_BLK,),
        in_specs=[_row_spec(2 * W), _row_spec(W), _row_spec(W), _row_spec(W), _row_spec(W), _vec_spec(W)],
        out_specs=tuple([_row_spec(W), _row_spec(W), _vec_spec(W)] + class_specs + class_specs),
        out_shape=tuple(out_shape),
        scratch_shapes=[_token_scratch(ROW_BLK, W), _token_scratch(ROW_BLK, W)],
        compiler_params=pltpu.CompilerParams(dimension_semantics=("arbitrary",)),
    )(dmix, outs[0], outs[1], lses[0], lses[1], sink)
    return res[0], res[1], res[2], res[3:3 + nd], res[3 + nd:]


def _band_bwd(qa, ka, va, d_out, stat, bias, *, r, q_col, k_col, v_col, stride, pattern, name, kv_shared=False,
              carry=None):
    L = qa.shape[0]
    nb = L // QBLK
    dn_nt = (((1,), (1,)), ((), ()))
    dn_tn = (((0,), (0,)), ((), ()))
    scale = HEAD_DIM ** -0.5

    pps = N_PAIRS if r > 1 else N_PAIRS // 2

    def body(q_ref, k_ref, v_ref, do_ref, st_ref, bias_ref, dq_ref, dk_ref, dv_ref, db_ref, dk_carry, dv_carry):
        @pl.when((pl.program_id(0) == 0) & (pl.program_id(1) == 0))
        def _():
            db_ref[...] = jnp.zeros_like(db_ref)

        lo, sel0 = _stat_lo(), _head_sel(0)
        pair0 = pl.program_id(1) * pps

        def block(b, first):
            rows = _rows(b)
            for i in range(pps):
                heads = pl.ds(2 * (pair0 + i), 2)
                cols = slice(i * LANES, (i + 1) * LANES)
                q2 = _stack_heads(q_ref[rows, cols], scale)
                kv_head = (pair0 + i) // 2 if kv_shared else None
                k = _key_window(k_ref, b, cols, first, kv_head)
                v = _key_window(v_ref, b, cols, first, kv_head)
                do2 = _stack_heads(do_ref[rows, cols])
                st = st_ref[rows, cols]
                lse2 = jnp.concatenate(
                    [jnp.max(jnp.where(_head_sel(h) & lo, st, -jnp.inf), axis=1, keepdims=True) for h in range(2)], axis=0)
                delta2 = jnp.concatenate(
                    [jnp.sum(jnp.where(_head_sel(h) & ~lo, st, 0.0), axis=1, keepdims=True) for h in range(2)],
                    axis=0) * (2.0 / HEAD_DIM)
                bias2 = bias_ref[1 if first else 0, heads].reshape(2 * QBLK, 2 * QBLK)
                s = lax.dot_general(q2, k, dn_nt, preferred_element_type=F32) + bias2
                p = jnp.exp(s - lse2)
                dp = lax.dot_general(do2, v, dn_nt, preferred_element_type=F32)
                ds = p * (dp - delta2)
                db_ref[heads] += ds.reshape(2, QBLK, 2 * QBLK)
                dsb = ds.astype(k.dtype)
                dq2 = jnp.dot(dsb, k, preferred_element_type=F32)
                dq_ref[rows, cols] = (jnp.where(sel0, dq2[0:QBLK], dq2[QBLK:]) * scale).astype(dq_ref.dtype)
                dk_acc = lax.dot_general(dsb, q2, dn_tn, preferred_element_type=F32)
                dv_acc = lax.dot_general(p.astype(k.dtype), do2, dn_tn, preferred_element_type=F32)
                if not first:
                    prev = _rows(b - 1)
                    dk_ref[prev, cols] = (dk_carry[:, cols] + dk_acc[0:QBLK]).astype(dk_ref.dtype)
                    dv_ref[prev, cols] = (dv_carry[:, cols] + dv_acc[0:QBLK]).astype(dv_ref.dtype)
                dk_carry[:, cols] = dk_acc[QBLK:2 * QBLK]
                dv_carry[:, cols] = dv_acc[QBLK:2 * QBLK]

        block(0, True)
        if nb > 1:
            pl.loop(1, nb)(lambda b: block(b, False))

        last = _rows(nb - 1)
        dk_ref[last, :] = dk_carry[...].astype(dk_ref.dtype)
        dv_ref[last, :] = dv_carry[...].astype(dv_ref.dtype)

    tok_spec = _class_spec(L, r, 0, 1, pps)
    bias_spec = pl.BlockSpec((None, 2, 2 * N_PAIRS, QBLK, 2 * QBLK), lambda j, c: (pattern, 0, 0, 0, 0))
    dbias_spec = pl.BlockSpec((2 * N_PAIRS, QBLK, 2 * QBLK), lambda j, c: (0, 0, 0))
    grad = jax.ShapeDtypeStruct((L, r * MIX_W), BF16)
    blk_bytes = _nbytes((L, pps * LANES), BF16)
    carry_shape = pltpu.VMEM((QBLK, pps * LANES), F32)
    return _call(
        body, name=name, grid=(r, N_PAIRS // pps),
        in_specs=[_class_spec(L, r, q_col, stride, pps), _kv_spec(L, r, k_col, stride, pps, kv_shared),
                  _kv_spec(L, r, v_col, stride, pps, kv_shared), tok_spec, tok_spec, bias_spec],
        out_specs=[tok_spec, tok_spec, tok_spec, dbias_spec],
        out_shape=[grad, grad, grad, jax.ShapeDtypeStruct((2 * N_PAIRS, QBLK, 2 * QBLK), F32)],
        args=[qa, ka, va, d_out, stat, bias], scratch=[carry_shape, carry_shape],
        vmem_limit=_vmem_limit(*([blk_bytes] * 9)), semantics=("arbitrary", "arbitrary"), carry=carry)


def _dz_assemble(dq_a, dk_a, dv_a, b_grads, dils, *, name):
    T = dq_a.shape[0]
    W = MIX_W

    def group_sum(x):
        u0 = x[:, 0:LANES] + x[:, LANES:2 * LANES]
        u1 = x[:, 2 * LANES:3 * LANES] + x[:, 3 * LANES:4 * LANES]
        s0 = u0 + pltpu.roll(u0, HEAD_DIM, 1)
        s1 = u1 + pltpu.roll(u1, HEAD_DIM, 1)
        return jnp.where(_head_sel(0), s0, s1)

    def body(*refs):
        dqa_ref, dka_ref, dva_ref = refs[0:3]
        b_refs = refs[3:12]
        dza_ref, dzb_ref, sa_ref, sb_ref, scratch = refs[12:]

        @pl.when(pl.program_id(0) == 0)
        def _():
            sa_ref[...] = jnp.zeros_like(sa_ref)
            sb_ref[...] = jnp.zeros_like(sb_ref)

        def put(dst, acc, col, part):
            w = part.shape[1]
            dst[:, col:col + w] = part.astype(dst.dtype)
            acc[:, col:col + w] += jnp.sum(part, axis=0, keepdims=True)

        put(dza_ref, sa_ref, 0, dqa_ref[...].astype(F32))
        put(dza_ref, sa_ref, W, group_sum(dka_ref[...].astype(F32)))
        put(dza_ref, sa_ref, W + LANES, group_sum(dva_ref[...].astype(F32)))
        for t in range(3):
            part = _to_tokens(b_refs[t], dils[0], scratch)
            for pat in range(1, len(dils)):
                part = part + _to_tokens(b_refs[3 * pat + t], dils[pat], scratch)
            put(dzb_ref, sb_ref, t * W, part)

    args = [dq_a, dk_a, dv_a] + [g[t] for g in b_grads for t in range(3)]
    return pl.pallas_call(
        body, name=name, grid=(T // ROW_BLK,),
        in_specs=[_row_spec(W)] * 3 + [_class_rows_spec(ROW_BLK, r, W) for r in dils for _ in range(3)],
        out_specs=(_row_spec(ZA_W), _row_spec(ZB_W), _vec_spec(ZA_W), _vec_spec(ZB_W)),
        out_shape=(jax.ShapeDtypeStruct((T, ZA_W), BF16), jax.ShapeDtypeStruct((T, ZB_W), BF16),
                   jax.ShapeDtypeStruct((1, ZA_W), F32), jax.ShapeDtypeStruct((1, ZB_W), F32)),
        scratch_shapes=[_token_scratch(ROW_BLK, W)],
        compiler_params=pltpu.CompilerParams(dimension_semantics=("arbitrary",)),
    )(*args)


def _place():
    x, y, c = lax.axis_index("x"), lax.axis_index("y"), lax.axis_index("c")
    chips = [(1 - x, y), (x, 1 - y), (1 - x, 1 - y)]
    return x, y, c, chips


def _cast_place(shards, k_idx, *, name):
    n, steps = len(shards), 4

    def body(k_ref, *refs):
        for s_ref, o_ref in zip(refs[:n], refs[n:]):
            o_ref[...] = s_ref[...].astype(o_ref.dtype)

    blocks = [(a.shape[0] // steps, a.shape[1]) for a in shards]
    grid_spec = pltpu.PrefetchScalarGridSpec(
        num_scalar_prefetch=1, grid=(steps,),
        in_specs=[pl.BlockSpec(blk, lambda t, k_ref: (t, 0)) for blk in blocks],
        out_specs=tuple(pl.BlockSpec(blk, lambda t, k_ref: (k_ref[0] * steps + t, 0)) for blk in blocks))
    return pl.pallas_call(
        body, name=name, grid_spec=grid_spec,
        out_shape=tuple(jax.ShapeDtypeStruct((N_CHIPS * a.shape[0], a.shape[1]), BF16) for a in shards),
        compiler_params=pltpu.CompilerParams(dimension_semantics=("parallel",),
                                             vmem_limit_bytes=_vmem_limit(*[_nbytes(b, F32) for b in blocks])),
    )(k_idx, *shards)


def _gather_carry(fulls, jobs):
    n = len(jobs)

    def piece(ref, chip_idx, half, part):
        rs = ref.shape[0] // N_CHIPS
        rh = rs // 2
        rows = rh // part[1]
        return ref.at[pl.ds(chip_idx * rs + half * rh + part[0] * rows, rows)]

    def copy(sems, sem, src_ref, dst_ref, to):
        return pltpu.make_async_remote_copy(src_ref=src_ref, dst_ref=dst_ref, send_sem=sems[0].at[sem],
                                            recv_sem=sems[1].at[sem], device_id=to, device_id_type=MESH)

    def to_chips(ins, outs, sems, j, arrival, hops=("both", "chips")):
        i, part, hop = jobs[j]
        x, y, c, chips = _place()
        res = []
        for t, chip in enumerate(chips if hop in hops else ()):
            theirs = piece(outs[i], 2 * chip[0] + chip[1], c, part)
            src, dst = (theirs, theirs) if arrival else (piece(ins[i], 2 * x + y, c, part), piece(outs[i], 2 * x + y, c, part))
            res.append(copy(sems, 3 * j + t, src, dst, (*chip, c)))
        return res

    def to_sibling(outs, sems, j, arrival, hops=("both", "sibling")):
        i, part, hop = jobs[j]
        x, y, c, chips = _place()
        res = []
        for t, chip in enumerate(chips if hop in hops else ()):
            region = piece(outs[i], 2 * chip[0] + chip[1], 1 - c if arrival else c, part)
            res.append(copy(sems, 3 * n + 3 * j + t, region, region, (x, y, 1 - c)))
        return res

    def start(ins, outs, sems):
        for j in range(n):
            for send in to_chips(ins, outs, sems, j, False) + to_sibling(outs, sems, j, False, ("sibling",)):
                send.start()

    def before_last(ins, outs, sems):
        for j in range(n):
            for arrival, send in zip(to_chips(ins, outs, sems, j, True, ("both",)),
                                     to_sibling(outs, sems, j, False, ("both",))):
                arrival.wait_recv()
                send.start()

    def finish(ins, outs, sems):
        for j in range(n):
            for arrival in to_chips(ins, outs, sems, j, True, ("chips",)) + to_sibling(outs, sems, j, True):
                arrival.wait_recv()
        for j in range(n):
            for send in to_chips(ins, outs, sems, j, False) + to_sibling(outs, sems, j, False):
                send.wait_send()

    return _Carry(fulls, [jax.ShapeDtypeStruct(a.shape, a.dtype) for a in fulls], {i: i for i in range(len(fulls))},
                  [pltpu.SemaphoreType.DMA((6 * n,)), pltpu.SemaphoreType.DMA((6 * n,))], start, finish, before_last)


def _pair_swap_carry(grads):
    n = len(grads)

    def copy(ins, outs, sems, i):
        x, y, c, _ = _place()
        return pltpu.make_async_remote_copy(src_ref=ins[i].at[:, 1 - c], dst_ref=outs[i], send_sem=sems[0].at[i],
                                            recv_sem=sems[1].at[i], device_id=(x, y, 1 - c), device_id_type=MESH)

    def start(ins, outs, sems):
        for i in range(n):
            copy(ins, outs, sems, i).start()

    def finish(ins, outs, sems):
        for i in range(n):
            copy(ins, outs, sems, i).wait()

    return _Carry(grads, [jax.ShapeDtypeStruct((a.shape[0], a.shape[2], a.shape[3]), a.dtype) for a in grads], {},
                  [pltpu.SemaphoreType.DMA((n,)), pltpu.SemaphoreType.DMA((n,))], start, finish)


def _pair_sum(g4, got, c_idx, *, name):
    _, _, rh, cs = g4.shape
    rb = _row_block(rh, 512, 16)

    def body(c_ref, own_ref, got_ref, o_ref):
        o_ref[...] = (own_ref[...].astype(F32) + got_ref[...].astype(F32)).astype(o_ref.dtype)

    grid_spec = pltpu.PrefetchScalarGridSpec(
        num_scalar_prefetch=1, grid=(N_CHIPS, rh // rb),
        in_specs=[pl.BlockSpec((None, None, rb, cs), lambda k, t, c_ref: (k, c_ref[0], t, 0)),
                  pl.BlockSpec((None, rb, cs), lambda k, t, c_ref: (k, t, 0))],
        out_specs=pl.BlockSpec((None, rb, cs), lambda k, t, c_ref: (k, t, 0)))
    return pl.pallas_call(
        body, name=name, grid_spec=grid_spec, out_shape=jax.ShapeDtypeStruct((N_CHIPS, rh, cs), BF16),
        compiler_params=pltpu.CompilerParams(dimension_semantics=("parallel", "parallel")),
    )(c_idx, g4, got)


def _chip_scatter_carry(sums):
    n = len(sums)

    def copies(ins, outs, sems):
        x, y, c, chips = _place()
        return [pltpu.make_async_remote_copy(src_ref=ins[i].at[2 * chip[0] + chip[1]], dst_ref=outs[i].at[j],
                                             send_sem=sems[0].at[3 * i + j], recv_sem=sems[1].at[3 * i + j],
                                             device_id=(*chip, c), device_id_type=MESH)
                for i in range(n) for j, chip in enumerate(chips)]

    def start(ins, outs, sems):
        for cp in copies(ins, outs, sems):
            cp.start()

    def finish(ins, outs, sems):
        for cp in copies(ins, outs, sems):
            cp.wait()

    return _Carry(sums, [jax.ShapeDtypeStruct((3,) + a.shape[1:], a.dtype) for a in sums], {},
                  [pltpu.SemaphoreType.DMA((3 * n,)), pltpu.SemaphoreType.DMA((3 * n,))], start, finish)


def _chip_sum(sums, gots, kc_idx, *, name):
    n, steps = len(sums), 2

    def body(kc_ref, *refs):
        for own_ref, got_ref, o_ref in zip(refs[:n], refs[n:2 * n], refs[2 * n:]):
            acc = own_ref[...].astype(F32)
            for j in range(3):
                acc = acc + got_ref[j].astype(F32)
            o_ref[...] = acc

    blocks = [(a.shape[1] // steps, a.shape[2]) for a in sums]
    grid_spec = pltpu.PrefetchScalarGridSpec(
        num_scalar_prefetch=1, grid=(steps,),
        in_specs=[pl.BlockSpec((None,) + blk, lambda t, kc: (kc[0], t, 0)) for blk in blocks]
        + [pl.BlockSpec((3,) + blk, lambda t, kc: (0, t, 0)) for blk in blocks],
        out_specs=tuple(pl.BlockSpec(blk, lambda t, kc: (kc[1] * steps + t, 0)) for blk in blocks))
    return pl.pallas_call(
        body, name=name, grid_spec=grid_spec,
        out_shape=tuple(jax.ShapeDtypeStruct((2 * a.shape[1], a.shape[2]), F32) for a in sums),
        compiler_params=pltpu.CompilerParams(dimension_semantics=("parallel",),
                                             vmem_limit_bytes=_vmem_limit(*[3 * _nbytes(b, F32) for b in blocks])),
    )(kc_idx, *sums, *gots)


def _half_swap_carry(shards):
    n = len(shards)

    def copy(ins, outs, sems, i, to_my_half):
        x, y, c, _ = _place()
        rh = ins[i].shape[0] // 2
        half = c if to_my_half else 1 - c
        return pltpu.make_async_remote_copy(
            src_ref=ins[i].at[pl.ds(c * rh, rh)], dst_ref=outs[i].at[pl.ds(half * rh, rh)],
            send_sem=sems[0].at[i], recv_sem=sems[1].at[i], device_id=(x, y, 1 - c), device_id_type=MESH)

    def start(ins, outs, sems):
        for i in range(n):
            copy(ins, outs, sems, i, True).start()

    def finish(ins, outs, sems):
        for i in range(n):
            copy(ins, outs, sems, i, False).wait_recv()
        for i in range(n):
            copy(ins, outs, sems, i, True).wait_send()

    return _Carry(shards, [jax.ShapeDtypeStruct(a.shape, a.dtype) for a in shards], {i: i for i in range(n)},
                  [pltpu.SemaphoreType.DMA((n,)), pltpu.SemaphoreType.DMA((n,))], start, finish)


SMALL_ROW = 1024


def _small_layout(shapes):
    starts, row = [], 0
    for r, c in shapes:
        starts.append(row)
        row += -(-c // SMALL_ROW) if r == 1 else r
    return starts, -(-row // SUBLANES) * SUBLANES


def _small_pieces(shape, start):
    r, c = shape
    if r == 1:
        return [(start + q, min(SMALL_ROW, c - q * SMALL_ROW), q * SMALL_ROW) for q in range(-(-c // SMALL_ROW))]
    return None


def _whole_spec(shape):
    return pl.BlockSpec(tuple(shape), lambda i: (0,) * len(shape))


def _small_all_reduce(parts, *, name, carry=None):
    shapes = [a.shape for a in parts]
    starts, rows = _small_layout(shapes)
    n = len(parts)
    n_tables = sum(1 for r, _ in shapes if r > 1)
    assert all(r > 1 and c <= LANES for r, c in shapes[:n_tables]) and all(r == 1 for r, _ in shapes[n_tables:])
    table_rows = starts[n_tables]
    assert table_rows % SUBLANES == 0

    def regions(slot):
        return slot.at[pl.ds(0, table_rows), pl.ds(0, LANES)], slot.at[pl.ds(table_rows, rows - table_rows)]

    def body(*refs):
        ins, outs = refs[0:n], refs[n:2 * n]
        buf, send_sems, recv_sems = refs[2 * n:]
        x, y, c, _ = _place()
        me = 4 * x + 2 * y + c
        mine = buf.at[me]
        mine[...] = jnp.zeros((rows, SMALL_ROW), F32)
        for ref, shape, start in zip(ins, shapes, starts):
            pieces = _small_pieces(shape, start)
            if pieces is None:
                mine[start:start + shape[0], 0:shape[1]] = ref[...]
            else:
                for row, width, col in pieces:
                    mine[row:row + 1, 0:width] = ref[:, col:col + width]
        def peer(d):
            return 1 - x if d & 4 else x, 1 - y if d & 2 else y, 1 - c if d & 1 else c

        def copy(d, part, dst_slot):
            sem = 2 * (d - 1) + part
            return pltpu.make_async_remote_copy(
                src_ref=regions(mine)[part], dst_ref=regions(dst_slot)[part], send_sem=send_sems.at[sem],
                recv_sem=recv_sems.at[sem], device_id=peer(d), device_id_type=MESH)

        for d in range(1, 8):
            for part in range(2):
                copy(d, part, mine).start()
        for d in range(1, 8):
            px, py, pc = peer(d)
            for part in range(2):
                copy(d, part, buf.at[4 * px + 2 * py + pc]).wait_recv()
        for d in range(1, 8):
            for part in range(2):
                copy(d, part, mine).wait_send()
        tables, vectors = regions(buf.at[0])
        acc_t, acc_v = tables[...], vectors[...]
        for s in range(1, 8):
            tables, vectors = regions(buf.at[s])
            acc_t, acc_v = acc_t + tables[...], acc_v + vectors[...]
        tables, vectors = regions(mine)
        tables[...] = acc_t
        vectors[...] = acc_v
        for ref, shape, start in zip(outs, shapes, starts):
            pieces = _small_pieces(shape, start)
            if pieces is None:
                ref[...] = mine[start:start + shape[0], 0:shape[1]]
            else:
                for row, width, col in pieces:
                    ref[:, col:col + width] = mine[row:row + 1, 0:width]

    whole = [_whole_spec(s) for s in shapes]
    return _call(body, name=name, grid=(1,), in_specs=whole, out_specs=whole,
                 out_shape=[jax.ShapeDtypeStruct(s, F32) for s in shapes], args=parts,
                 scratch=[pltpu.VMEM((8, rows, SMALL_ROW), F32), pltpu.SemaphoreType.DMA((14,)),
                          pltpu.SemaphoreType.DMA((14,))], carry=carry)


def _adamw_small(ws, gs, ms, vs, *, name):
    n = len(ws)
    c1 = 1.0 - ADAM_B1 ** ADAM_STEP
    c2 = 1.0 - ADAM_B2 ** ADAM_STEP

    def body(*refs):
        for k in range(n):
            w_ref, g_ref, m_ref, v_ref = (refs[t * n + k] for t in range(4))
            go_ref, d_ref, mo_ref, vo_ref = (refs[(4 + t) * n + k] for t in range(4))
            gv = g_ref[...]
            go_ref[...] = gv
            mn = ADAM_B1 * m_ref[...] + (1.0 - ADAM_B1) * gv
            vn = ADAM_B2 * v_ref[...] + (1.0 - ADAM_B2) * (gv * gv)
            d_ref[...] = -ADAM_LR * ((mn / c1) / (jnp.sqrt(vn / c2) + ADAM_EPS) + ADAM_WD * w_ref[...])
            mo_ref[...] = mn
            vo_ref[...] = vn

    whole = [_whole_spec(a.shape) for a in ws]
    shapes = tuple(jax.ShapeDtypeStruct(a.shape, F32) for a in ws)
    res = pl.pallas_call(
        body, name=name, grid=(1,), in_specs=whole * 4, out_specs=tuple(whole * 4), out_shape=shapes * 4,
    )(*ws, *gs, *ms, *vs)
    return res[0:n], res[n:2 * n], res[2 * n:3 * n], res[3 * n:4 * n]


def _adamw(ws, gs, ms, vs, *, name):
    n, steps = len(ws), 8
    c1 = 1.0 - ADAM_B1 ** ADAM_STEP
    c2 = 1.0 - ADAM_B2 ** ADAM_STEP

    def body(*refs):
        for k in range(n):
            w_ref, g_ref, m_ref, v_ref = (refs[t * n + k] for t in range(4))
            go_ref, d_ref, mo_ref, vo_ref = (refs[(4 + t) * n + k] for t in range(4))
            gv = g_ref[...]
            go_ref[...] = gv
            mn = ADAM_B1 * m_ref[...] + (1.0 - ADAM_B1) * gv
            vn = ADAM_B2 * v_ref[...] + (1.0 - ADAM_B2) * (gv * gv)
            d_ref[...] = -ADAM_LR * ((mn / c1) / (jnp.sqrt(vn / c2) + ADAM_EPS) + ADAM_WD * w_ref[...])
            mo_ref[...] = mn
            vo_ref[...] = vn

    blocks = [(a.shape[0] // steps, a.shape[1]) for a in ws]
    specs = [pl.BlockSpec(blk, lambda i: (i, 0)) for blk in blocks]
    shapes = tuple(jax.ShapeDtypeStruct(a.shape, F32) for a in ws)
    res = pl.pallas_call(
        body, name=name, grid=(steps,), in_specs=specs * 4, out_specs=tuple(specs * 4), out_shape=shapes * 4,
        compiler_params=pltpu.CompilerParams(dimension_semantics=("parallel",),
                                             vmem_limit_bytes=_vmem_limit(*[8 * _nbytes(b, F32) for b in blocks])),
    )(*ws, *gs, *ms, *vs)
    return res[0:n], res[n:2 * n], res[2 * n:3 * n], res[3 * n:4 * n]


def _local_step(x, p, target, small, sched):
    T = x.shape[0]
    W = MIX_W
    rel_bias = small["rel_bias"]
    b_in_a, b_in_b = small["b_in"][:, 0:ZA_W], small["b_in"][:, ZA_W:]

    idx_np = [_t5_index(A_WINDOW - 1, 1)] + [_t5_index(window // dil, dil) for window, dil in B_PATTERNS]
    idx_all = jnp.asarray(np.stack(idx_np))
    n1, bias_all = sched.run(_prologue, x, small["ffn1_pre_g"], rel_bias, idx_all, name="prologue")
    w_gu1 = sched.weight("ffn1_w_gu")
    *gu1, a1 = sched.run(_ffn_up, n1, w_gu1, bm=512, name="ffn1_gu")
    w_d1 = sched.weight("ffn1_w_down")
    f1, h1, n2 = sched.run(_mm_resid_norm, a1, w_d1, x, small["ffn1_post_g"], 0.5, small["attn_pre_g"], bm=512,
                           name="ffn1_down")
    win_a, win_b = sched.weight("w_in")
    za = _mm_nt(n2, win_a, bm=1024, bn=ZA_W, out_dtype=BF16, name="in_proj_a", add=b_in_a)
    dils = tuple(dil for _, dil in B_PATTERNS)
    zb_by_dil = _mm_nt_classes(n2, win_b, b_in_b, dils, bm=512, name="in_proj_b")

    a_args = dict(r=1, q_col=0, k_col=W // LANES, v_col=W // LANES + 1, stride=0, pattern=0, kv_shared=True)
    sink_row = jnp.repeat(small["sinks"], HEAD_DIM, axis=1)
    out_a, out_a_bf, lse_a = sched.run(_band_fwd, za, za, za, bias_all, name="band_a_fwd", sink=sink_row, **a_args)

    no_sink = jnp.full((1, W), NEG, F32)
    b_ctx, branches = [], []
    for t, dil in enumerate(dils):
        zb_r = zb_by_dil[t]
        args = dict(r=dil, q_col=0, k_col=1, v_col=2, stride=ZB_W // W, pattern=1 + t)
        branches.append(sched.run(_band_fwd, zb_r, zb_r, zb_r, bias_all, name=f"band_b{t}_fwd", **args))
        b_ctx.append((jnp.asarray(idx_np[1 + t]), zb_r, args))
    out_b, out_b_bf, lse_b = _merge(branches, dils, no_sink, name="merge_b")

    mix = jnp.concatenate([out_a_bf, out_b_bf], axis=1)
    w_out = sched.weight("w_out")
    att, h2, n3 = _mm_resid_norm(mix, w_out, h1, small["attn_post_g"], 1.0, small["ffn2_pre_g"], bm=512,
                                 name="out_proj", bias=small["b_out"])
    w_gu2, w_d2 = sched.weight("ffn2_w_gu"), sched.weight("ffn2_w_down")
    *gu2, a2 = _ffn_up(n3, w_gu2, bm=512, name="ffn2_gu")
    f2, h3, n4 = _mm_resid_norm(a2, w_d2, h2, small["ffn2_post_g"], 0.5, small["ple_pre_g"], bm=512,
                                name="ffn2_down")
    w_gate = sched.weight("w_ple_gate")
    p_bf = p.astype(BF16)
    dh4, de, dgp, loss, d_ple_post = _ple_head(n4, w_gate, p_bf, sched.weight("w_ple_proj"), h3, small["ple_post_g"],
                                               target, bm=512, name="ple_head")

    gs = {"ple_post_g": d_ple_post}
    sched.grad("w_ple_proj", _mm_tn(p_bf, de, bm=256, bn=1024, out_dtype=BF16, name="d_w_ple"))
    sched.grad("w_ple_gate", _mm_tn(n4, dgp, bm=512, bn=1024, out_dtype=BF16, name="d_w_gate"))
    dh3, df2, gs["ple_pre_g"], gs["ffn2_post_g"], _ = sched.run(
        _mm_nt_norms_bwd, dgp, w_gate, dh4, h3, small["ple_pre_g"], f2, small["ffn2_post_g"], 0.5, bm=512, name="d_n4")

    def ffn_bwd(df, a, gu, n, w_down, w_gu, tag, *norm_args):
        dgu = sched.run(_ffn_down_bwd, df, w_down, gu[0], gu[1], bm=512, name=f"ffn{tag}_d_a")
        sched.grad(f"ffn{tag}_w_gu", _mm_tn(n, dgu, bm=512, bn=1408, out_dtype=BF16, name=f"ffn{tag}_d_w_gu",
                                            n_stack=N_CHIPS))
        sched.grad(f"ffn{tag}_w_down", sched.run(_mm_tn, a, df, bm=256, bn=1024, out_dtype=BF16,
                                                name=f"ffn{tag}_d_w_down"))
        return sched.run(_mm_nt_norms_bwd, dgu, w_gu, *norm_args, bm=512, name=f"ffn{tag}_d_n")

    dh2, datt, gs["ffn2_pre_g"], gs["attn_post_g"], gs["b_out"] = ffn_bwd(
        df2, a2, gu2, n3, w_d2, w_gu2, "2", dh3, h2, small["ffn2_pre_g"], att, small["attn_post_g"], 1.0)
    sched.grad("w_out", _mm_tn(mix, datt, bm=512, bn=1024, out_dtype=BF16, name="d_w_out"))
    dmix = sched.run(_mm_nt, datt, w_out, bm=1024, bn=1024, out_dtype=F32, name="d_mix")

    do_a, stat_a, dsink, do_b, stat_b = _attn_delta(dmix, (out_a, out_b), (lse_a, lse_b), sink_row, dils,
                                                    name="attn_delta")
    gs["sinks"] = dsink[:, ::HEAD_DIM]
    dq_a, dk_a, dv_a, dbias_a = sched.run(_band_bwd, za, za, za, do_a, stat_a, bias_all, name="band_a_bwd", **a_args)
    d_rel_a = _bias_grad(dbias_a, jnp.asarray(idx_np[0]), name="d_rel_a")
    b_grads = []
    d_rel_b = None
    for t, (idx, zb_r, args) in enumerate(b_ctx):
        dq, dk, dv, dbias = sched.run(_band_bwd, zb_r, zb_r, zb_r, do_b[t], stat_b[t], bias_all,
                                      name=f"band_b{t}_bwd", **args)
        b_grads.append((dq, dk, dv))
        d_rel = _bias_grad(dbias, idx, name=f"d_rel_b{t}")
        d_rel_b = d_rel if d_rel_b is None else d_rel_b + d_rel
    gs["rel_bias"] = jnp.concatenate([d_rel_a[:, 0:2 * N_PAIRS], d_rel_b[:, 0:2 * N_PAIRS]], axis=1)
    dza, dzb, dsum_a, dsum_b = _dz_assemble(dq_a, dk_a, dv_a, b_grads, dils, name="d_z")
    gs["b_in"] = jnp.concatenate([dsum_a, dsum_b], axis=1)
    sched.grad("w_in", (_mm_tn(dza, n2, bm=ZA_W, bn=1024, out_dtype=BF16, name="d_w_in_a"),
                        _mm_tn(dzb, n2, bm=ZB_W // 2, bn=1024, out_dtype=BF16, name="d_w_in_b")))
    dn2_a = _mm_nn(dza, win_a, bm=1024, bn=1024, out_dtype=F32, name="d_n2_a")
    dh1, df1, gs["attn_pre_g"], gs["ffn1_post_g"], _ = sched.run(
        _mm_nt_norms_bwd, dzb, win_b, dh2, h1, small["attn_pre_g"], f1, small["ffn1_post_g"], 0.5, bm=512,
        name="d_n2_b", add=dn2_a, b_is_kn=True)
    grad_x, gs["ffn1_pre_g"] = ffn_bwd(df1, a1, gu1, n1, w_d1, w_gu1, "1", dh1, x, small["ffn1_pre_g"], None, None, 0.0)
    return loss, grad_x, gs


def _to_compute(name, full):
    st = full.reshape(N_CHIPS, full.shape[0] // N_CHIPS, full.shape[1])
    if name in ("ffn1_w_gu", "ffn2_w_gu"):
        return st
    if name == "w_ple_proj":
        return jnp.transpose(st, (1, 0, 2)).reshape(st.shape[1], -1)
    if name == "w_in":
        return full[0:ZA_W], full[ZA_W:]
    return full


def _to_comm(name, g):
    if name == "w_in":
        g = jnp.concatenate(g, axis=0)
    if name == "w_ple_proj":
        g = jnp.transpose(g.reshape(g.shape[0], N_CHIPS, -1), (1, 0, 2))
    rs = g.shape[1] if g.ndim == 3 else g.shape[0] // N_CHIPS
    return g.reshape(N_CHIPS, 2, rs // 2, g.shape[-1])


class _Schedule:
    GATHER = {
        "prologue": [("ffn1_w_gu", (0, 1), "both")],
        "ffn1_gu": [("ffn1_w_down", (0, 1), "both"), ("w_in", (0, 1), "both")],
        "band_a_fwd": [("ffn2_w_gu", (0, 2), "chips"), ("w_out", (0, 1), "chips")],
        "band_b0_fwd": [("ffn2_w_gu", (0, 2), "sibling"), ("w_out", (0, 1), "sibling"),
                        ("ffn2_w_gu", (1, 2), "chips"), ("w_ple_gate", (0, 1), "chips"), ("w_ple_proj", (0, 1), "chips")],
        "band_b1_fwd": [("ffn2_w_gu", (1, 2), "sibling"), ("w_ple_gate", (0, 1), "sibling"),
                        ("w_ple_proj", (0, 1), "sibling"), ("ffn2_w_down", (0, 1), "both")],
    }
    SWAP = {"ffn2_d_w_down": ["w_ple_proj", "w_ple_gate", "ffn2_w_gu"], "ffn2_d_n": ["ffn2_w_down"],
            "band_a_bwd": ["w_out"], "d_n2_b": ["w_in"], "ffn1_d_w_down": ["ffn1_w_gu"], "ffn1_d_n": ["ffn1_w_down"]}
    SCATTER = {"ffn2_d_n": ["ffn2_w_gu"], "band_a_bwd": ["w_ple_proj", "w_ple_gate", "ffn2_w_down"],
               "d_n2_b": ["w_out"], "ffn1_d_w_down": ["w_in"], "ffn1_d_n": ["ffn1_w_gu"]}

    def __init__(self, placed, c_idx, kc_idx):
        self.full = dict(placed)
        self.missing = {n: 1.0 for n in placed}
        self.c_idx, self.kc_idx = c_idx, kc_idx
        self.grads, self.swapped, self.pair_sums, self.scattered = {}, {}, {}, {}

    def _gather(self, entries):
        if not entries:
            return [], []
        names = list(dict.fromkeys(n for n, _, _ in entries))
        carry = _gather_carry([self.full[n] for n in names], [(names.index(n), pt, hops) for n, pt, hops in entries])

        def done():
            self.full.update(zip(names, carry.results))
            for n, part, hops in entries:
                if hops != "chips":
                    self.missing[n] -= 1.0 / part[1]
        return [carry], [done]

    def weight(self, name):
        assert abs(self.missing[name]) < 1e-9, (name, self.missing[name])
        return _to_compute(name, self.full[name])

    def grad(self, name, g):
        self.grads[name] = _to_comm(name, g)

    def _swap(self, names):
        carry = _pair_swap_carry([self.grads[n] for n in names])

        def done():
            self.swapped.update(zip(names, carry.results))
        return carry, done

    def _scatter(self, names):
        for n in names:
            self.pair_sums[n] = _pair_sum(self.grads[n], self.swapped[n], self.c_idx, name=f"grad_pair_sum_{n}")
        carry = _chip_scatter_carry([self.pair_sums[n] for n in names])

        def done():
            self.scattered.update(zip(names, carry.results))
        return carry, done

    def run(self, fn, *args, name, **kw):
        carries, after = self._gather(self.GATHER.get(name, []))
        for names, make in ((self.SWAP.get(name), self._swap), (self.SCATTER.get(name), self._scatter)):
            if names:
                carry, done = make(names)
                carries.append(carry)
                after.append(done)
        out = fn(*args, name=name, carry=_join(carries), **kw)
        for done in after:
            done()
        return out

    def finish(self, last_carrier):
        left = [n for n in BIG if n not in self.scattered]
        to_swap = [n for n in left if n not in self.swapped]
        if to_swap:
            carry, done = self._swap(to_swap)
            _comm_call(carry, name="grad_pair_swap")
            done()
        carry, done = self._scatter(left) if left else (None, lambda: None)
        carried = last_carrier(carry=carry)
        done()
        halves = _chip_sum([self.pair_sums[n] for n in BIG], [self.scattered[n] for n in BIG], self.kc_idx,
                           name="grad_chip_sum")
        carry = _half_swap_carry(halves)
        _comm_call(carry, name="grad_half_swap")
        return dict(zip(BIG, carry.results)), carried


def kernel(x, p, rel_bias, ffn1_pre_g, ffn1_w_gu, ffn1_w_down, ffn1_post_g, attn_pre_g, w_in, b_in, sinks, w_out, b_out, attn_post_g, ffn2_pre_g, ffn2_w_gu, ffn2_w_down, ffn2_post_g, ple_pre_g, w_ple_gate, w_ple_proj, ple_post_g, loss_target, m_rel_bias, m_ffn1_pre_g, m_ffn1_w_gu, m_ffn1_w_down, m_ffn1_post_g, m_attn_pre_g, m_w_in, m_b_in, m_sinks, m_w_out, m_b_out, m_attn_post_g, m_ffn2_pre_g, m_ffn2_w_gu, m_ffn2_w_down, m_ffn2_post_g, m_ple_pre_g, m_w_ple_gate, m_w_ple_proj, m_ple_post_g, v_rel_bias, v_ffn1_pre_g, v_ffn1_w_gu, v_ffn1_w_down, v_ffn1_post_g, v_attn_pre_g, v_w_in, v_b_in, v_sinks, v_w_out, v_b_out, v_attn_post_g, v_ffn2_pre_g, v_ffn2_w_gu, v_ffn2_w_down, v_ffn2_post_g, v_ple_pre_g, v_w_ple_gate, v_w_ple_proj, v_ple_post_g):
    given = dict(locals())
    w = {n: given[n] for n in WEIGHTS}
    m = {n: given["m_" + n] for n in WEIGHTS}
    v = {n: given["v_" + n] for n in WEIGHTS}
    c_pos = lax.axis_index("c").astype(jnp.int32)
    k_pos = (2 * lax.axis_index("x") + lax.axis_index("y")).astype(jnp.int32)
    c_idx, k_idx, kc_idx = c_pos.reshape(1), k_pos.reshape(1), jnp.stack([k_pos, c_pos])

    def shard(a, n):
        return jnp.transpose(a[0]) if n == "w_in" else a[0]

    def unshard(a, n):
        return (jnp.transpose(a) if n == "w_in" else a)[None]

    placed = _cast_place([shard(w[n], n) for n in BIG], k_idx, name="place_shards")
    sched = _Schedule(dict(zip(BIG, placed)), c_idx, kc_idx)
    small = {n: (w[n] if n == "rel_bias" else w[n].reshape(1, -1)) for n in SMALL}

    loss_part, grad_x, gs = _local_step(x[0], p[0, 0], loss_target[0], small, sched)

    grads_big, (*small_grads, loss_row) = sched.finish(
        partial(_small_all_reduce, [gs[n] for n in SMALL] + [loss_part], name="small_all_reduce"))
    loss = loss_row[0, 0]

    grads, delta, new_m, new_v = {}, {}, {}, {}
    res = _adamw([shard(w[n], n) for n in BIG], [grads_big[n] for n in BIG], [shard(m[n], n) for n in BIG],
                 [shard(v[n], n) for n in BIG], name="adamw_big")
    for n, gg, dd, mm, vv in zip(BIG, *res):
        grads[n], delta[n], new_m[n], new_v[n] = (unshard(a, n) for a in (gg, dd, mm, vv))
    res = _adamw_small([w[n] for n in SMALL], small_grads, [m[n] for n in SMALL], [v[n] for n in SMALL],
                       name="adamw_small")
    for name, gg, dd, mm, vv in zip(SMALL, *res):
        grads[name], delta[name], new_m[name], new_v[name] = gg, dd, mm, vv

    return (loss, grad_x[None], *[grads[n] for n in WEIGHTS], *[delta[n] for n in WEIGHTS],
            *[new_m[n] for n in WEIGHTS], *[new_v[n] for n in WEIGHTS])
```

```python
import math
from functools import partial

import jax
import jax.numpy as jnp
import numpy as np
from jax import lax
from jax.experimental import pallas as pl
from jax.experimental.pallas import tpu as pltpu

F32 = jnp.float32
BF16 = jnp.bfloat16
MESH = pl.DeviceIdType.MESH

EPS = 1e-6
NEG = -1e30
LANES = 128
SUBLANES = 8
HEAD_DIM = 64
QBLK = 128
N_PAIRS = 4
MIX_W = N_PAIRS * LANES
A_WINDOW = 128
B_PATTERNS = ((128, 1), (512, 4), (2048, 16))
NUM_BUCKETS = 32
MAX_DISTANCE = 2048
ZA_W = 768
ZB_W = 1536
N_CHIPS = 4
VMEM_BYTES_V7X = 64 * 2**20

ADAM_LR, ADAM_B1, ADAM_B2, ADAM_EPS, ADAM_WD, ADAM_STEP = 0.001, 0.9, 0.999, 1e-08, 0.01, 10

BIG = ("ffn1_w_gu", "ffn1_w_down", "w_in", "w_out", "ffn2_w_gu", "ffn2_w_down", "w_ple_gate", "w_ple_proj")
SMALL = ("rel_bias", "ffn1_pre_g", "ffn1_post_g", "attn_pre_g", "b_in", "sinks", "b_out", "attn_post_g",
         "ffn2_pre_g", "ffn2_post_g", "ple_pre_g", "ple_post_g")
WEIGHTS = ("rel_bias", "ffn1_pre_g", "ffn1_w_gu", "ffn1_w_down", "ffn1_post_g", "attn_pre_g", "w_in", "b_in",
           "sinks", "w_out", "b_out", "attn_post_g", "ffn2_pre_g", "ffn2_w_gu", "ffn2_w_down", "ffn2_post_g",
           "ple_pre_g", "w_ple_gate", "w_ple_proj", "ple_post_g")


def _vmem_limit(*block_bytes):
    need = 2 * sum(block_bytes) + max(block_bytes) * 2 + (4 << 20)
    return int(min(max(need, 32 << 20), VMEM_BYTES_V7X - (6 << 20)))


def _nbytes(shape, dtype):
    return int(np.prod(shape)) * jnp.dtype(dtype).itemsize


MXU_WIDTH_V7X = 256


def _col_chunks(n):
    return [slice(c, min(c + MXU_WIDTH_V7X, n)) for c in range(0, n, MXU_WIDTH_V7X)]


def _row_halves(rows):
    return [slice(0, rows // 2), slice(rows // 2, rows)]


def _row_block(rows, cap, mult):
    for cand in range(min(rows, cap), 0, -1):
        if rows % cand == 0 and cand % mult == 0:
            return cand
    raise ValueError((rows, cap, mult))


class _Carry:
    def __init__(self, operands, out_shapes, aliases, sems, start, finish, before_last=None):
        self.operands, self.out_shapes, self.aliases, self.sems = list(operands), list(out_shapes), dict(aliases), list(sems)
        self.start, self.finish = start, finish
        self.before_last = before_last or (lambda ins, outs, sems: None)
        self.results = None


def _join(carries):
    carries = [c for c in carries if c is not None]
    if not carries:
        return None
    if len(carries) == 1:
        return carries[0]
    offs, pos = [], [0, 0, 0]
    for c in carries:
        offs.append(tuple(pos))
        pos = [pos[0] + len(c.operands), pos[1] + len(c.out_shapes), pos[2] + len(c.sems)]
    aliases = {}
    for c, (oi, oo, _) in zip(carries, offs):
        aliases.update({oi + i: oo + o for i, o in c.aliases.items()})

    def run(which):
        def fn(ins, outs, sems):
            for c, (oi, oo, os_) in zip(carries, offs):
                getattr(c, which)(ins[oi:oi + len(c.operands)], outs[oo:oo + len(c.out_shapes)],
                                  sems[os_:os_ + len(c.sems)])
        return fn

    joined = _Carry([a for c in carries for a in c.operands], [s for c in carries for s in c.out_shapes], aliases,
                    [s for c in carries for s in c.sems], run("start"), run("finish"), run("before_last"))
    joined.parts = (carries, offs)
    return joined


def _set_results(carry, outs):
    carry.results = list(outs)
    if hasattr(carry, "parts"):
        for c, (_, oo, _) in zip(*carry.parts):
            _set_results(c, outs[oo:oo + len(c.out_shapes)])


ANY = pl.BlockSpec(memory_space=pl.ANY)


def _call(body, *, name, grid, in_specs, out_specs, out_shape, args, scratch=(), vmem_limit=None, carry=None,
          semantics=None, before_carry=None):
    assert before_carry is None or carry is not None
    n_ci, n_co, n_cs = len(args), len(out_shape), len(scratch)
    semantics = semantics or ("parallel",) * len(grid)
    if carry is None:
        res = pl.pallas_call(
            body, name=name, grid=grid, in_specs=list(in_specs), out_specs=tuple(out_specs), out_shape=tuple(out_shape),
            scratch_shapes=list(scratch),
            compiler_params=pltpu.CompilerParams(dimension_semantics=semantics, vmem_limit_bytes=vmem_limit),
        )(*args)
        return list(res)
    n_pi, n_po = len(carry.operands), len(carry.out_shapes)

    def full_body(*refs):
        ci, pi = refs[:n_ci], refs[n_ci:n_ci + n_pi]
        o0 = n_ci + n_pi
        co, po = refs[o0:o0 + n_co], refs[o0 + n_co:o0 + n_co + n_po]
        s0 = o0 + n_co + n_po
        cs, ps = refs[s0:s0 + n_cs], refs[s0 + n_cs:]
        ids = [pl.program_id(ax) for ax in range(len(grid))]
        first, last = ids[0] == 0, ids[0] == grid[0] - 1
        for ax in range(1, len(grid)):
            first, last = first & (ids[ax] == 0), last & (ids[ax] == grid[ax] - 1)

        @pl.when(first)
        def _():
            if before_carry is not None:
                before_carry(*ci, *co, *cs)
            carry.start(pi, po, ps)

        several_steps = any(g > 1 for g in grid)
        if several_steps:
            @pl.when(last)
            def _():
                carry.before_last(pi, po, ps)

        body(*ci, *co, *cs)

        @pl.when(last)
        def _():
            if not several_steps:
                carry.before_last(pi, po, ps)
            carry.finish(pi, po, ps)

    res = pl.pallas_call(
        full_body, name=name, grid=grid, in_specs=list(in_specs) + [ANY] * n_pi,
        out_specs=tuple(out_specs) + tuple([ANY] * n_po), out_shape=tuple(out_shape) + tuple(carry.out_shapes),
        scratch_shapes=list(scratch) + carry.sems,
        input_output_aliases={n_ci + i: n_co + o for i, o in carry.aliases.items()},
        compiler_params=pltpu.CompilerParams(dimension_semantics=("arbitrary",) * len(grid),
                                             vmem_limit_bytes=vmem_limit),
    )(*args, *carry.operands)
    _set_results(carry, res[n_co:])
    return list(res[:n_co])


def _comm_call(carry, *, name):
    n_pi, n_po = len(carry.operands), len(carry.out_shapes)

    def body(*refs):
        pi, po, ps = refs[:n_pi], refs[n_pi:n_pi + n_po], refs[n_pi + n_po:]
        carry.start(pi, po, ps)
        carry.before_last(pi, po, ps)
        carry.finish(pi, po, ps)

    res = pl.pallas_call(
        body, name=name, in_specs=[ANY] * n_pi, out_specs=tuple([ANY] * n_po), out_shape=tuple(carry.out_shapes),
        scratch_shapes=carry.sems, input_output_aliases=dict(carry.aliases),
    )(*carry.operands)
    _set_results(carry, res)


def _mm_nn(a, b, *, bm, bn, out_dtype, name, bias=None, carry=None):
    M, K = a.shape
    stacked = b.ndim == 3
    N = b.shape[0] * b.shape[2] if stacked else b.shape[1]
    assert M % bm == 0 and N % bn == 0 and (not stacked or bn == b.shape[2])

    def body(*refs):
        a_ref, b_ref = refs[0], refs[1]
        o_ref = refs[-1]
        acc = jnp.dot(a_ref[...], b_ref[...], preferred_element_type=F32)
        if bias is not None:
            acc = acc + refs[2][...]
        o_ref[...] = acc.astype(o_ref.dtype)

    if stacked:
        b_spec = pl.BlockSpec((None, K, bn), lambda i, j: (j, 0, 0))
    else:
        b_spec = pl.BlockSpec((K, bn), lambda i, j: (0, j))
    in_specs = [pl.BlockSpec((bm, K), lambda i, j: (i, 0)), b_spec]
    args = [a, b]
    if bias is not None:
        in_specs.append(pl.BlockSpec((1, bn), lambda i, j: (0, j)))
        args.append(bias)
    limit = _vmem_limit(_nbytes((bm, K), a.dtype), _nbytes((K, bn), b.dtype), _nbytes((bm, bn), F32))
    return _call(body, name=name, grid=(M // bm, N // bn), in_specs=in_specs,
                 out_specs=[pl.BlockSpec((bm, bn), lambda i, j: (i, j))],
                 out_shape=[jax.ShapeDtypeStruct((M, N), out_dtype)], args=args, vmem_limit=limit, carry=carry)[0]


def _mm_nt(a, b, *, bm, bn, out_dtype, name, add=None, carry=None):
    M, K = a.shape
    stacked = b.ndim == 3
    N = b.shape[1] if stacked else b.shape[0]
    n_sh = b.shape[0] if stacked else 1
    ks = b.shape[2] if stacked else K
    assert M % bm == 0 and N % bn == 0 and n_sh * ks == K
    dn = (((1,), (1,)), ((), ()))

    def body(*refs):
        a_ref, b_ref = refs[0], refs[1]
        o_ref = refs[-1]
        if stacked:
            acc = lax.dot_general(a_ref[:, 0:ks], b_ref[0], dn, preferred_element_type=F32)
            for s in range(1, n_sh):
                acc = acc + lax.dot_general(a_ref[:, s * ks:(s + 1) * ks], b_ref[s], dn, preferred_element_type=F32)
        else:
            acc = lax.dot_general(a_ref[...], b_ref[...], dn, preferred_element_type=F32)
        if add is not None:
            acc = acc + refs[2][...]
        o_ref[...] = acc.astype(o_ref.dtype)

    if stacked:
        b_spec = pl.BlockSpec((n_sh, bn, ks), lambda i, j: (0, j, 0))
    else:
        b_spec = pl.BlockSpec((bn, K), lambda i, j: (j, 0))
    in_specs = [pl.BlockSpec((bm, K), lambda i, j: (i, 0)), b_spec]
    args = [a, b]
    if add is not None:
        rows = bm if add.shape[0] == M else 1
        in_specs.append(pl.BlockSpec((rows, bn), lambda i, j: (i if rows == bm else 0, j)))
        args.append(add)
    limit = _vmem_limit(_nbytes((bm, K), a.dtype), _nbytes((bn, K), b.dtype), _nbytes((bm, bn), F32))
    return _call(body, name=name, grid=(M // bm, N // bn), in_specs=in_specs,
                 out_specs=[pl.BlockSpec((bm, bn), lambda i, j: (i, j))],
                 out_shape=[jax.ShapeDtypeStruct((M, N), out_dtype)], args=args, vmem_limit=limit, carry=carry)[0]


def _mm_tn(a, b, *, bm, bn, out_dtype, name, n_stack=None, carry=None):
    T, M = a.shape
    N = b.shape[1]
    assert M % bm == 0 and N % bn == 0 and (n_stack is None or bn * n_stack == N)
    dn = (((0,), (0,)), ((), ()))

    def body(a_ref, b_ref, o_ref):
        acc = lax.dot_general(a_ref[...], b_ref[...], dn, preferred_element_type=F32)
        o_ref[...] = acc.astype(o_ref.dtype)

    if n_stack is None:
        out_spec = pl.BlockSpec((bm, bn), lambda i, j: (i, j))
        out_shape = jax.ShapeDtypeStruct((M, N), out_dtype)
    else:
        out_spec = pl.BlockSpec((None, bm, bn), lambda i, j: (j, i, 0))
        out_shape = jax.ShapeDtypeStruct((n_stack, M, bn), out_dtype)
    limit = _vmem_limit(_nbytes((T, bm), a.dtype), _nbytes((T, bn), b.dtype), _nbytes((bm, bn), F32))
    return _call(body, name=name, grid=(M // bm, N // bn),
                 in_specs=[pl.BlockSpec((T, bm), lambda i, j: (0, i)), pl.BlockSpec((T, bn), lambda i, j: (0, j))],
                 out_specs=[out_spec], out_shape=[out_shape], args=[a, b], vmem_limit=limit, carry=carry)[0]


ROW_BLK = 256


def _rstd(x):
    return lax.rsqrt(jnp.mean(x * x, axis=-1, keepdims=True) + EPS)


def _norm_bwd(xhat, rstd, dxhat):
    return rstd * (dxhat - xhat * jnp.mean(dxhat * xhat, axis=-1, keepdims=True))


def _row_spec(cols):
    return pl.BlockSpec((ROW_BLK, cols), lambda i: (i, 0))


def _vec_spec(cols):
    return pl.BlockSpec((1, cols), lambda i: (0, 0))


def _prologue(x, g, rel_bias, idx_all, *, name, carry=None):
    T, D = x.shape
    n_pat = idx_all.shape[0]
    heads = 2 * N_PAIRS
    steps = n_pat * heads
    rows = T // steps

    def body(rb_ref, x_ref, g_ref, idx_ref, n_ref, bias_ref):
        s = pl.program_id(0)
        head = jnp.where(s < heads, s, heads + s % heads)
        xv = x_ref[...]
        n_ref[...] = (xv * _rstd(xv) * g_ref[...]).astype(n_ref.dtype)
        idxv = idx_ref[...]
        acc = jnp.where(idxv < 0, NEG, 0.0).astype(F32)
        for b in range(NUM_BUCKETS):
            acc = acc + jnp.where(idxv == b, rb_ref[b, head], 0.0)
        bias_ref[0] = acc
        kap = lax.broadcasted_iota(jnp.int32, (1, 2 * QBLK), 1)
        bias_ref[1] = jnp.where(kap < QBLK, NEG, acc)

    return _call(
        body, name=name, grid=(steps,),
        in_specs=[pl.BlockSpec(memory_space=pltpu.SMEM), pl.BlockSpec((rows, D), lambda s: (s, 0)),
                  pl.BlockSpec((1, D), lambda s: (0, 0)),
                  pl.BlockSpec((None, QBLK, 2 * QBLK), lambda s: (s // heads, 0, 0))],
        out_specs=[pl.BlockSpec((rows, D), lambda s: (s, 0)),
                   pl.BlockSpec((None, 2, None, QBLK, 2 * QBLK), lambda s: (s // heads, 0, s % heads, 0, 0))],
        out_shape=[jax.ShapeDtypeStruct((T, D), BF16),
                   jax.ShapeDtypeStruct((n_pat, 2, heads, QBLK, 2 * QBLK), F32)],
        args=[rel_bias, x, g, idx_all], carry=carry)


def _mm_resid_norm(a, b, h, g_post, coef, g_next, *, bm, name, bias=None, carry=None):
    M, K = a.shape
    N = b.shape[1]

    def body(*refs):
        a_ref, b_ref, h_ref, gp_ref, gn_ref = refs[0:5]
        f_ref, hn_ref, n_ref = refs[-3:]
        for rows in _row_halves(bm):
            f = jnp.dot(a_ref[rows, :], b_ref[...], preferred_element_type=F32)
            if bias is not None:
                f = f + refs[5][...]
            f_ref[rows, :] = f
            hn = h_ref[rows, :] + coef * (f * _rstd(f) * gp_ref[...])
            hn_ref[rows, :] = hn
            n_ref[rows, :] = (hn * _rstd(hn) * gn_ref[...]).astype(n_ref.dtype)

    row = lambda w: pl.BlockSpec((bm, w), lambda i: (i, 0))
    vec = pl.BlockSpec((1, N), lambda i: (0, 0))
    in_specs = [row(K), pl.BlockSpec((K, N), lambda i: (0, 0), pipeline_mode=pl.Buffered(1)), row(N), vec, vec]
    args = [a, b, h, g_post, g_next]
    if bias is not None:
        in_specs.append(vec)
        args.append(bias)
    limit = _vmem_limit(_nbytes((bm, K), a.dtype), _nbytes((K, N), b.dtype) // 2, 4 * _nbytes((bm, N), F32))
    return _call(body, name=name, grid=(M // bm,), in_specs=in_specs, out_specs=[row(N), row(N), row(N)],
                 out_shape=[jax.ShapeDtypeStruct((M, N), F32), jax.ShapeDtypeStruct((M, N), F32),
                            jax.ShapeDtypeStruct((M, N), BF16)], args=args, vmem_limit=limit, carry=carry)


def _mm_nt_norms_bwd(a, b, dh_in, h, g_pre, f, g_post, coef, *, bm, name, add=None, b_is_kn=False, carry=None):
    M, K = a.shape
    stacked = b.ndim == 3
    N = b.shape[1] if (stacked or b_is_kn) else b.shape[0]
    n_sh = b.shape[0] if stacked else 1
    ks = b.shape[2] if stacked else K
    assert n_sh * ks == K
    dn_dims = (((1,), (0,)), ((), ())) if b_is_kn else (((1,), (1,)), ((), ()))
    with_f = f is not None
    n_in = 5 + (2 if with_f else 0) + (1 if add is not None else 0)

    def body(*refs):
        a_ref, b_ref, dhi_ref, h_ref, gpre_ref = refs[0:5]
        outs = refs[n_in:]

        @pl.when(pl.program_id(0) == 0)
        def _():
            for acc in (outs[2:] if with_f else outs[1:]):
                acc[...] = jnp.zeros_like(acc)

        for rows in _row_halves(bm):
            if stacked:
                dn = lax.dot_general(a_ref[rows, 0:ks], b_ref[0], dn_dims, preferred_element_type=F32)
                for s in range(1, n_sh):
                    dn = dn + lax.dot_general(a_ref[rows, s * ks:(s + 1) * ks], b_ref[s], dn_dims,
                                              preferred_element_type=F32)
            else:
                dn = lax.dot_general(a_ref[rows, :], b_ref[...], dn_dims, preferred_element_type=F32)
            if add is not None:
                dn = dn + refs[n_in - 1][rows, :]
            hv = h_ref[rows, :]
            r = _rstd(hv)
            hh = hv * r
            dh = dhi_ref[rows, :] + _norm_bwd(hh, r, dn * gpre_ref[...])
            outs[0][rows, :] = dh
            if not with_f:
                outs[1][...] += jnp.sum(dn * hh, axis=0, keepdims=True)
                continue
            f_ref, gpost_ref = refs[5], refs[6]
            df_ref, dgpre_ref, dgpost_ref, dsum_ref = outs[1:]
            dgpre_ref[...] += jnp.sum(dn * hh, axis=0, keepdims=True)
            fv = f_ref[rows, :]
            rf = _rstd(fv)
            fh = fv * rf
            dy = coef * dh
            dgpost_ref[...] += jnp.sum(dy * fh, axis=0, keepdims=True)
            df = _norm_bwd(fh, rf, dy * gpost_ref[...])
            dsum_ref[...] += jnp.sum(df, axis=0, keepdims=True)
            df_ref[rows, :] = df.astype(df_ref.dtype)

    row = lambda w: pl.BlockSpec((bm, w), lambda i: (i, 0))
    vec = pl.BlockSpec((1, N), lambda i: (0, 0))
    once = pl.Buffered(1)
    if stacked:
        b_spec = pl.BlockSpec((n_sh, N, ks), lambda i: (0, 0, 0), pipeline_mode=once)
    else:
        b_spec = pl.BlockSpec(b.shape, lambda i: (0, 0), pipeline_mode=once)
    in_specs = [row(K), b_spec, row(N), row(N), vec]
    args = [a, b, dh_in, h, g_pre]
    if with_f:
        in_specs += [row(N), vec]
        args += [f, g_post]
    if add is not None:
        in_specs.append(row(N))
        args.append(add)
    vec_shape = jax.ShapeDtypeStruct((1, N), F32)
    out_specs = [row(N)] + ([row(N), vec, vec, vec] if with_f else [vec])
    out_shape = [jax.ShapeDtypeStruct((M, N), F32)]
    out_shape += [jax.ShapeDtypeStruct((M, N), BF16), vec_shape, vec_shape, vec_shape] if with_f else [vec_shape]
    limit = _vmem_limit(_nbytes((bm, K), a.dtype), _nbytes((N, K), b.dtype) // 2, 6 * _nbytes((bm, N), F32))
    return _call(body, name=name, grid=(M // bm,), in_specs=in_specs, out_specs=out_specs, out_shape=out_shape,
                 args=args, vmem_limit=limit, semantics=("arbitrary",), carry=carry)


def _ffn_up(n, w_gu, *, bm, name, carry=None):
    M, K = n.shape
    n_sh, _, ns = w_gu.shape
    half = n_sh // 2

    def body(n_ref, wg_ref, wu_ref, g_ref, u_ref, a_ref):
        nv = n_ref[...]
        for cols in _col_chunks(ns):
            g = jnp.dot(nv, wg_ref[:, cols], preferred_element_type=F32)
            u = jnp.dot(nv, wu_ref[:, cols], preferred_element_type=F32)
            g_ref[:, cols] = g.astype(g_ref.dtype)
            u_ref[:, cols] = u.astype(u_ref.dtype)
            a_ref[:, cols] = (g * jax.nn.sigmoid(g) * u).astype(a_ref.dtype)

    out_spec = pl.BlockSpec((bm, ns), lambda i, j: (i, j))
    out = jax.ShapeDtypeStruct((M, half * ns), BF16)
    limit = _vmem_limit(_nbytes((bm, K), n.dtype), 2 * _nbytes((K, ns), w_gu.dtype), 3 * _nbytes((bm, ns), F32))
    return _call(body, name=name, grid=(M // bm, half),
                 in_specs=[pl.BlockSpec((bm, K), lambda i, j: (i, 0)),
                           pl.BlockSpec((None, K, ns), lambda i, j: (j, 0, 0)),
                           pl.BlockSpec((None, K, ns), lambda i, j: (j + half, 0, 0))],
                 out_specs=[out_spec, out_spec, out_spec], out_shape=[out, out, out], args=[n, w_gu, w_gu],
                 vmem_limit=limit, carry=carry)


def _ffn_down_bwd(df, w_down, g, u, *, bm, name, carry=None):
    M, D = df.shape
    F = w_down.shape[0]
    dn = (((1,), (1,)), ((), ()))

    def body(df_ref, w_ref, g_ref, u_ref, o_ref):
        dfv = df_ref[...]
        for cols in _col_chunks(F):
            da = lax.dot_general(dfv, w_ref[cols, :], dn, preferred_element_type=F32)
            gv = g_ref[:, cols].astype(F32)
            s = jax.nn.sigmoid(gv)
            o_ref[:, cols] = (da * u_ref[:, cols].astype(F32) * (s * (1.0 + gv * (1.0 - s)))).astype(o_ref.dtype)
            o_ref[:, F + cols.start:F + cols.stop] = (da * (gv * s)).astype(o_ref.dtype)

    row = lambda w: pl.BlockSpec((bm, w), lambda i: (i, 0))
    limit = _vmem_limit(_nbytes((F, D), w_down.dtype) // 2, 2 * _nbytes((bm, F), BF16), _nbytes((bm, 2 * F), BF16))
    return _call(body, name=name, grid=(M // bm,),
                 in_specs=[row(D), pl.BlockSpec((F, D), lambda i: (0, 0), pipeline_mode=pl.Buffered(1)), row(F), row(F)],
                 out_specs=[row(2 * F)], out_shape=[jax.ShapeDtypeStruct((M, 2 * F), BF16)],
                 args=[df, w_down, g, u], vmem_limit=limit, carry=carry)[0]


def _ple_head(n4, w_gate, p, w_ple, h3, g_post, target, *, bm, name):
    T, D = h3.shape
    kp = p.shape[1]

    def body(n_ref, wg_ref, p_ref, wp_ref, h_ref, g_ref, t_ref, dh_ref, de_ref, dgp_ref, loss_ref, dg_ref):
        @pl.when(pl.program_id(0) == 0)
        def _():
            loss_ref[...] = jnp.zeros_like(loss_ref)
            dg_ref[...] = jnp.zeros_like(dg_ref)

        gpost = g_ref[...]
        for rows in _row_halves(bm):
            gate = jax.nn.sigmoid(jnp.dot(n_ref[rows, :], wg_ref[...], preferred_element_type=F32))
            ev = jnp.dot(p_ref[rows, :], wp_ref[...], preferred_element_type=F32)
            ge = gate * ev
            r = _rstd(ge)
            gh = ge * r
            diff = h_ref[rows, :] + gh * gpost - t_ref[rows, :]
            loss_ref[...] += jnp.sum(diff * diff) * (0.5 / D)
            dh = diff * (1.0 / D)
            dh_ref[rows, :] = dh
            dg_ref[...] += jnp.sum(dh * gh, axis=0, keepdims=True)
            dge = _norm_bwd(gh, r, dh * gpost)
            de_ref[rows, :] = (dge * gate).astype(de_ref.dtype)
            dgp_ref[rows, :] = (dge * ev * gate * (1.0 - gate)).astype(dgp_ref.dtype)

    row = lambda w: pl.BlockSpec((bm, w), lambda i: (i, 0))
    whole = lambda a: pl.BlockSpec(a.shape, lambda i: (0, 0), pipeline_mode=pl.Buffered(1))
    limit = _vmem_limit(_nbytes((bm, D), BF16), _nbytes(w_gate.shape, w_gate.dtype) // 2, 6 * _nbytes((bm, D), F32))
    return pl.pallas_call(
        body, name=name, grid=(T // bm,),
        in_specs=[row(D), whole(w_gate), row(kp), whole(w_ple), row(D), _vec_spec(D), row(D)],
        out_specs=(row(D), row(D), row(D), _vec_spec(LANES), _vec_spec(D)),
        out_shape=(jax.ShapeDtypeStruct((T, D), F32), jax.ShapeDtypeStruct((T, D), BF16),
                   jax.ShapeDtypeStruct((T, D), BF16), jax.ShapeDtypeStruct((1, LANES), F32),
                   jax.ShapeDtypeStruct((1, D), F32)),
        compiler_params=pltpu.CompilerParams(dimension_semantics=("arbitrary",), vmem_limit_bytes=limit),
    )(n4, w_gate, p, w_ple, h3, g_post, target)


def _t5_index(max_dist, stride):
    rho = np.arange(QBLK)[:, None]
    kap = np.arange(2 * QBLK)[None, :]
    d = rho + QBLK - kap
    valid = (d >= 0) & (d <= max_dist)
    n = np.maximum(d, 0) * stride
    max_exact = NUM_BUCKETS // 2
    nf = np.maximum(n, 1).astype(np.float32)
    large = max_exact + (np.log(nf / np.float32(max_exact)) / np.float32(math.log(MAX_DISTANCE / max_exact))
                         * np.float32(NUM_BUCKETS - max_exact)).astype(np.int32)
    large = np.minimum(large, NUM_BUCKETS - 1)
    bucket = np.where(n < max_exact, n, large)
    return np.where(valid, bucket, -1).astype(np.int32)


def _bias_grad(d_bias, idx, *, name):
    def body(db_ref, idx_ref, o_ref):
        h = pl.program_id(0)

        @pl.when(h == 0)
        def _():
            o_ref[...] = jnp.zeros_like(o_ref)

        idxv = idx_ref[...]
        dv = db_ref[0]
        rows = lax.broadcasted_iota(jnp.int32, (NUM_BUCKETS, LANES), 0)
        lanes = lax.broadcasted_iota(jnp.int32, (NUM_BUCKETS, LANES), 1)
        acc = o_ref[...]
        for b in range(NUM_BUCKETS):
            s = jnp.sum(jnp.where(idxv == b, dv, 0.0))
            acc = acc + jnp.where((rows == b) & (lanes == h), s, 0.0)
        o_ref[...] = acc

    return pl.pallas_call(
        body, name=name, grid=(2 * N_PAIRS,),
        in_specs=[pl.BlockSpec((1, QBLK, 2 * QBLK), lambda h: (h, 0, 0)),
                  pl.BlockSpec((QBLK, 2 * QBLK), lambda h: (0, 0))],
        out_specs=pl.BlockSpec((NUM_BUCKETS, LANES), lambda h: (0, 0)),
        out_shape=jax.ShapeDtypeStruct((NUM_BUCKETS, LANES), F32),
        compiler_params=pltpu.CompilerParams(dimension_semantics=("arbitrary",)),
    )(d_bias, idx)


def _lane():
    return lax.broadcasted_iota(jnp.int32, (1, LANES), 1)


def _head_sel(h):
    return (_lane() < HEAD_DIM) if h == 0 else (_lane() >= HEAD_DIM)


def _stat_lo():
    return (_lane() % HEAD_DIM) < (HEAD_DIM // 2)


def _stack_heads(t, scale=None):
    if scale is not None:
        t = t * scale
    zero = jnp.zeros_like(t)
    return jnp.concatenate([jnp.where(_head_sel(0), t, zero), jnp.where(_head_sel(1), t, zero)], axis=0)


def _class_spec(L, r, cols, stride, pps):
    chunks = N_PAIRS // pps
    mode = pl.Buffered(1) if r * chunks == 1 else None
    return pl.BlockSpec((L, MIX_W // chunks), lambda j, c: (0, (cols + j * stride) * chunks + c), pipeline_mode=mode)


def _rows(b, n_blocks=1):
    return pl.ds(pl.multiple_of(b * QBLK, QBLK), n_blocks * QBLK)


def _key_window(ref, b, cols, first, kv_head=None):
    if kv_head is not None:
        cols = slice(0, LANES)
    if first:
        blk = ref[0:QBLK, cols]
        tile = jnp.concatenate([blk, blk], axis=0)
    else:
        tile = ref[_rows(b - 1, 2), cols]
    if kv_head is None:
        return tile
    wide = tile.astype(F32)
    keep = jnp.logical_xor(_lane() < HEAD_DIM, kv_head == 1)
    return jnp.where(keep, wide, pltpu.roll(wide, HEAD_DIM, 1)).astype(tile.dtype)


def _kv_spec(L, r, col, stride, pps, kv_shared):
    if not kv_shared:
        return _class_spec(L, r, col, stride, pps)
    mode = pl.Buffered(1) if pps == N_PAIRS else None
    return pl.BlockSpec((L, LANES), lambda j, c: (0, col), pipeline_mode=mode)


def _band_fwd(qa, ka, va, bias, *, r, q_col, k_col, v_col, stride, pattern, name, kv_shared=False, sink=None,
              carry=None):
    L = qa.shape[0]
    nb = L // QBLK
    dn = (((1,), (1,)), ((), ()))
    scale = HEAD_DIM ** -0.5

    pps = N_PAIRS

    def body(q_ref, k_ref, v_ref, bias_ref, *rest):
        sel0 = _head_sel(0)
        pair0 = pl.program_id(1) * pps

        def emit(rows, cols, o, lse):
            if sink is None:
                o_ref, lse_ref = rest
                o_ref[rows, cols] = o.astype(o_ref.dtype)
                lse_ref[rows, cols] = lse
                return
            sink_ref, out_ref, outb_ref, lse_ref = rest
            sk = sink_ref[:, cols]
            mx = jnp.maximum(lse, sk)
            w = jnp.exp(lse - mx)
            den = w + jnp.exp(sk - mx)
            out = o * (w / den)
            out_ref[rows, cols] = out
            outb_ref[rows, cols] = out.astype(outb_ref.dtype)
            lse_ref[rows, cols] = mx + jnp.log(den)

        def block(b, first):
            rows = _rows(b)
            for i in range(pps):
                cols = slice(i * LANES, (i + 1) * LANES)
                q2 = _stack_heads(q_ref[rows, cols], scale)
                kv_head = (pair0 + i) // 2 if kv_shared else None
                k = _key_window(k_ref, b, cols, first, kv_head)
                v = _key_window(v_ref, b, cols, first, kv_head)
                bias2 = bias_ref[1 if first else 0, pl.ds(2 * (pair0 + i), 2)].reshape(2 * QBLK, 2 * QBLK)
                s = lax.dot_general(q2, k, dn, preferred_element_type=F32) + bias2
                m = jnp.max(s, axis=1, keepdims=True)
                p = jnp.exp(s - m)
                l = jnp.sum(p, axis=1, keepdims=True)
                o = jnp.dot(p.astype(v.dtype), v, preferred_element_type=F32) * (1.0 / l)
                lse = m + jnp.log(l)
                emit(rows, cols, jnp.where(sel0, o[0:QBLK], o[QBLK:]), jnp.where(sel0, lse[0:QBLK], lse[QBLK:]))

        block(0, True)
        if nb > 1:
            pl.loop(1, nb)(lambda b: block(b, False))

    bias_spec = pl.BlockSpec((None, 2, 2 * N_PAIRS, QBLK, 2 * QBLK), lambda j, c: (pattern, 0, 0, 0, 0))
    out_spec = _class_spec(L, r, 0, 1, pps)
    blk_bytes = _nbytes((L, pps * LANES), F32)
    in_specs = [_class_spec(L, r, q_col, stride, pps), _kv_spec(L, r, k_col, stride, pps, kv_shared),
                _kv_spec(L, r, v_col, stride, pps, kv_shared), bias_spec]
    args = [qa, ka, va, bias]
    f32_out, bf16_out = jax.ShapeDtypeStruct((L, r * MIX_W), F32), jax.ShapeDtypeStruct((L, r * MIX_W), BF16)
    out_shape = [bf16_out, f32_out]
    if sink is not None:
        in_specs.append(pl.BlockSpec((1, MIX_W), lambda j, c: (0, 0)))
        args.append(sink)
        out_shape = [f32_out, bf16_out, f32_out]
    return _call(body, name=name, grid=(r, N_PAIRS // pps), in_specs=in_specs, out_specs=[out_spec] * len(out_shape),
                 out_shape=out_shape, args=args, vmem_limit=_vmem_limit(*([blk_bytes] * 4)), carry=carry)


def _class_rows_spec(rows, r, width):
    return pl.BlockSpec((rows // r, r * width), lambda i, *_: (i, 0))


def _token_scratch(rows, width):
    return pltpu.VMEM((width // LANES, rows, LANES), F32)


def _fill_tokens(scratch, value):
    for c in range(scratch.shape[0]):
        scratch[c] = value[:, c * LANES:(c + 1) * LANES]


def _read_tokens(scratch):
    return jnp.concatenate([scratch[c] for c in range(scratch.shape[0])], axis=1)


def _to_tokens(blk_ref, r, scratch):
    if r == 1:
        return blk_ref[...].astype(F32)
    nt, rows, _ = scratch.shape
    for j in range(r):
        for c in range(nt):
            lanes = slice((j * nt + c) * LANES, (j * nt + c + 1) * LANES)
            scratch.at[c][pl.ds(j, rows // r, stride=r), :] = blk_ref[:, lanes].astype(F32)
    return _read_tokens(scratch)


def _from_tokens(dst_ref, r, scratch):
    nt, rows, _ = scratch.shape
    if r == 1:
        dst_ref[...] = _read_tokens(scratch).astype(dst_ref.dtype)
        return
    for j in range(r):
        for c in range(nt):
            lanes = slice((j * nt + c) * LANES, (j * nt + c + 1) * LANES)
            dst_ref[:, lanes] = scratch.at[c][pl.ds(j, rows // r, stride=r), :].astype(dst_ref.dtype)


def _mm_nt_classes(a, b, bias, dils, *, bm, name):
    M, K = a.shape
    N = b.shape[0]
    dn = (((1,), (1,)), ((), ()))

    def body(a_ref, b_ref, bias_ref, *rest):
        outs, tile = rest[:-1], rest[-1]
        _fill_tokens(tile, lax.dot_general(a_ref[...], b_ref[...], dn, preferred_element_type=F32) + bias_ref[...])
        for out, r in zip(outs, dils):
            _from_tokens(out, r, tile)

    limit = _vmem_limit(_nbytes((bm, K), a.dtype), _nbytes((K, N), b.dtype) // 2, (1 + len(dils)) * _nbytes((bm, N), F32))
    return pl.pallas_call(
        body, name=name, grid=(M // bm,),
        in_specs=[pl.BlockSpec((bm, K), lambda i: (i, 0)),
                  pl.BlockSpec((N, K), lambda i: (0, 0), pipeline_mode=pl.Buffered(1)),
                  pl.BlockSpec((1, N), lambda i: (0, 0))],
        out_specs=tuple(_class_rows_spec(bm, r, N) for r in dils),
        out_shape=tuple(jax.ShapeDtypeStruct((M // r, r * N), BF16) for r in dils),
        scratch_shapes=[_token_scratch(bm, N)],
        compiler_params=pltpu.CompilerParams(dimension_semantics=("parallel",), vmem_limit_bytes=limit),
    )(a, b, bias)


def _merge(branches, dils, sink, *, name):
    W = MIX_W
    T = branches[0][0].shape[0] * dils[0]
    nbr = len(branches)

    def body(*refs):
        sink_ref = refs[2 * nbr]
        out_ref, outb_ref, lse_ref, scratch = refs[2 * nbr + 1:]
        sk = sink_ref[...]
        lses = [_to_tokens(refs[2 * t + 1], dils[t], scratch) for t in range(nbr)]
        mx = sk
        for lse in lses:
            mx = jnp.maximum(mx, lse)
        den = jnp.exp(sk - mx)
        num = jnp.zeros_like(mx)
        for t in range(nbr):
            w = jnp.exp(lses[t] - mx)
            den = den + w
            num = num + w * _to_tokens(refs[2 * t], dils[t], scratch)
        out = num / den
        out_ref[...] = out
        outb_ref[...] = out.astype(outb_ref.dtype)
        lse_ref[...] = mx + jnp.log(den)

    args = [a for br in branches for a in br] + [sink]
    in_specs = [_class_rows_spec(ROW_BLK, r, W) for r in dils for _ in range(2)] + [_vec_spec(W)]
    return pl.pallas_call(
        body, name=name, grid=(T // ROW_BLK,), in_specs=in_specs,
        out_specs=(_row_spec(W), _row_spec(W), _row_spec(W)),
        out_shape=(jax.ShapeDtypeStruct((T, W), F32), jax.ShapeDtypeStruct((T, W), BF16),
                   jax.ShapeDtypeStruct((T, W), F32)),
        scratch_shapes=[_token_scratch(ROW_BLK, W)],
        compiler_params=pltpu.CompilerParams(dimension_semantics=("parallel",)),
    )(*args)


def _attn_delta(dmix, outs, lses, sink, dils, *, name):
    T = dmix.shape[0]
    W = MIX_W
    nd = len(dils)

    def body(dmix_ref, oa_ref, ob_ref, la_ref, lb_ref, sink_ref, *rest):
        doa_ref, sta_ref, dsink_ref = rest[0:3]
        dob_refs, stb_refs = rest[3:3 + nd], rest[3 + nd:3 + 2 * nd]
        do_tile, st_tile = rest[3 + 2 * nd:]

        @pl.when(pl.program_id(0) == 0)
        def _():
            dsink_ref[...] = jnp.zeros_like(dsink_ref)

        sel0, lo = _head_sel(0), _stat_lo()
        for mixer, (o_ref, l_ref) in enumerate(((oa_ref, la_ref), (ob_ref, lb_ref))):
            for i in range(N_PAIRS):
                cols = slice(i * LANES, (i + 1) * LANES)
                do = dmix_ref[:, mixer * W + i * LANES:mixer * W + (i + 1) * LANES]
                prod = do * o_ref[:, cols]
                d0 = jnp.sum(jnp.where(sel0, prod, 0.0), axis=1, keepdims=True)
                d1 = jnp.sum(jnp.where(sel0, 0.0, prod), axis=1, keepdims=True)
                delta = jnp.where(sel0, d0, d1)
                lse = l_ref[:, cols]
                stat = jnp.where(lo, lse, delta)
                if mixer == 0:
                    sta_ref[:, cols] = stat
                    doa_ref[:, cols] = do.astype(doa_ref.dtype)
                    dsink_ref[:, cols] += -jnp.sum(jnp.exp(sink_ref[:, cols] - lse) * delta, axis=0, keepdims=True)
                else:
                    st_tile[i] = stat
                    do_tile[i] = do
        for t, r in enumerate(dils):
            _from_tokens(dob_refs[t], r, do_tile)
            _from_tokens(stb_refs[t], r, st_tile)

    class_specs = [_class_rows_spec(ROW_BLK, r, W) for r in dils]
    out_shape = [jax.ShapeDtypeStruct((T, W), BF16), jax.ShapeDtypeStruct((T, W), F32), jax.ShapeDtypeStruct((1, W), F32)]
    out_shape += [jax.ShapeDtypeStruct((T // r, r * W), BF16) for r in dils]
    out_shape += [jax.ShapeDtypeStruct((T // r, r * W), F32) for r in dils]
    res = pl.pallas_call(
        body, name=name, grid=(T // ROW_BLK,),
        in_specs=[_row_spec(2 * W), _row_spec(W), _row_spec(W), _row_spec(W), _row_spec(W), _vec_spec(W)],
        out_specs=tuple([_row_spec(W), _row_spec(W), _vec_spec(W)] + class_specs + class_specs),
        out_shape=tuple(out_shape),
        scratch_shapes=[_token_scratch(ROW_BLK, W), _token_scratch(ROW_BLK, W)],
        compiler_params=pltpu.CompilerParams(dimension_semantics=("arbitrary",)),
    )(dmix, outs[0], outs[1], lses[0], lses[1], sink)
    return res[0], res[1], res[2], res[3:3 + nd], res[3 + nd:]


def _band_bwd(qa, ka, va, d_out, stat, bias, *, r, q_col, k_col, v_col, stride, pattern, name, kv_shared=False,
              carry=None):
    L = qa.shape[0]
    nb = L // QBLK
    dn_nt = (((1,), (1,)), ((), ()))
    dn_tn = (((0,), (0,)), ((), ()))
    scale = HEAD_DIM ** -0.5

    pps = N_PAIRS if r > 1 else N_PAIRS // 2

    def body(q_ref, k_ref, v_ref, do_ref, st_ref, bias_ref, dq_ref, dk_ref, dv_ref, db_ref, dk_carry, dv_carry):
        @pl.when((pl.program_id(0) == 0) & (pl.program_id(1) == 0))
        def _():
            db_ref[...] = jnp.zeros_like(db_ref)

        lo, sel0 = _stat_lo(), _head_sel(0)
        pair0 = pl.program_id(1) * pps

        def block(b, first):
            rows = _rows(b)
            for i in range(pps):
                heads = pl.ds(2 * (pair0 + i), 2)
                cols = slice(i * LANES, (i + 1) * LANES)
                q2 = _stack_heads(q_ref[rows, cols], scale)
                kv_head = (pair0 + i) // 2 if kv_shared else None
                k = _key_window(k_ref, b, cols, first, kv_head)
                v = _key_window(v_ref, b, cols, first, kv_head)
                do2 = _stack_heads(do_ref[rows, cols])
                st = st_ref[rows, cols]
                lse2 = jnp.concatenate(
                    [jnp.max(jnp.where(_head_sel(h) & lo, st, -jnp.inf), axis=1, keepdims=True) for h in range(2)], axis=0)
                delta2 = jnp.concatenate(
                    [jnp.sum(jnp.where(_head_sel(h) & ~lo, st, 0.0), axis=1, keepdims=True) for h in range(2)],
                    axis=0) * (2.0 / HEAD_DIM)
                bias2 = bias_ref[1 if first else 0, heads].reshape(2 * QBLK, 2 * QBLK)
                s = lax.dot_general(q2, k, dn_nt, preferred_element_type=F32) + bias2
                p = jnp.exp(s - lse2)
                dp = lax.dot_general(do2, v, dn_nt, preferred_element_type=F32)
                ds = p * (dp - delta2)
                db_ref[heads] += ds.reshape(2, QBLK, 2 * QBLK)
                dsb = ds.astype(k.dtype)
                dq2 = jnp.dot(dsb, k, preferred_element_type=F32)
                dq_ref[rows, cols] = (jnp.where(sel0, dq2[0:QBLK], dq2[QBLK:]) * scale).astype(dq_ref.dtype)
                dk_acc = lax.dot_general(dsb, q2, dn_tn, preferred_element_type=F32)
                dv_acc = lax.dot_general(p.astype(k.dtype), do2, dn_tn, preferred_element_type=F32)
                if not first:
                    prev = _rows(b - 1)
                    dk_ref[prev, cols] = (dk_carry[:, cols] + dk_acc[0:QBLK]).astype(dk_ref.dtype)
                    dv_ref[prev, cols] = (dv_carry[:, cols] + dv_acc[0:QBLK]).astype(dv_ref.dtype)
                dk_carry[:, cols] = dk_acc[QBLK:2 * QBLK]
                dv_carry[:, cols] = dv_acc[QBLK:2 * QBLK]

        block(0, True)
        if nb > 1:
            pl.loop(1, nb)(lambda b: block(b, False))

        last = _rows(nb - 1)
        dk_ref[last, :] = dk_carry[...].astype(dk_ref.dtype)
        dv_ref[last, :] = dv_carry[...].astype(dv_ref.dtype)

    tok_spec = _class_spec(L, r, 0, 1, pps)
    bias_spec = pl.BlockSpec((None, 2, 2 * N_PAIRS, QBLK, 2 * QBLK), lambda j, c: (pattern, 0, 0, 0, 0))
    dbias_spec = pl.BlockSpec((2 * N_PAIRS, QBLK, 2 * QBLK), lambda j, c: (0, 0, 0))
    grad = jax.ShapeDtypeStruct((L, r * MIX_W), BF16)
    blk_bytes = _nbytes((L, pps * LANES), BF16)
    carry_shape = pltpu.VMEM((QBLK, pps * LANES), F32)
    return _call(
        body, name=name, grid=(r, N_PAIRS // pps),
        in_specs=[_class_spec(L, r, q_col, stride, pps), _kv_spec(L, r, k_col, stride, pps, kv_shared),
                  _kv_spec(L, r, v_col, stride, pps, kv_shared), tok_spec, tok_spec, bias_spec],
        out_specs=[tok_spec, tok_spec, tok_spec, dbias_spec],
        out_shape=[grad, grad, grad, jax.ShapeDtypeStruct((2 * N_PAIRS, QBLK, 2 * QBLK), F32)],
        args=[qa, ka, va, d_out, stat, bias], scratch=[carry_shape, carry_shape],
        vmem_limit=_vmem_limit(*([blk_bytes] * 9)), semantics=("arbitrary", "arbitrary"), carry=carry)


def _dz_assemble(dq_a, dk_a, dv_a, b_grads, dils, *, name):
    T = dq_a.shape[0]
    W = MIX_W

    def group_sum(x):
        u0 = x[:, 0:LANES] + x[:, LANES:2 * LANES]
        u1 = x[:, 2 * LANES:3 * LANES] + x[:, 3 * LANES:4 * LANES]
        s0 = u0 + pltpu.roll(u0, HEAD_DIM, 1)
        s1 = u1 + pltpu.roll(u1, HEAD_DIM, 1)
        return jnp.where(_head_sel(0), s0, s1)

    def body(*refs):
        dqa_ref, dka_ref, dva_ref = refs[0:3]
        b_refs = refs[3:12]
        dza_ref, dzb_ref, sa_ref, sb_ref, scratch = refs[12:]

        @pl.when(pl.program_id(0) == 0)
        def _():
            sa_ref[...] = jnp.zeros_like(sa_ref)
            sb_ref[...] = jnp.zeros_like(sb_ref)

        def put(dst, acc, col, part):
            w = part.shape[1]
            dst[:, col:col + w] = part.astype(dst.dtype)
            acc[:, col:col + w] += jnp.sum(part, axis=0, keepdims=True)

        put(dza_ref, sa_ref, 0, dqa_ref[...].astype(F32))
        put(dza_ref, sa_ref, W, group_sum(dka_ref[...].astype(F32)))
        put(dza_ref, sa_ref, W + LANES, group_sum(dva_ref[...].astype(F32)))
        for t in range(3):
            part = _to_tokens(b_refs[t], dils[0], scratch)
            for pat in range(1, len(dils)):
                part = part + _to_tokens(b_refs[3 * pat + t], dils[pat], scratch)
            put(dzb_ref, sb_ref, t * W, part)

    args = [dq_a, dk_a, dv_a] + [g[t] for g in b_grads for t in range(3)]
    return pl.pallas_call(
        body, name=name, grid=(T // ROW_BLK,),
        in_specs=[_row_spec(W)] * 3 + [_class_rows_spec(ROW_BLK, r, W) for r in dils for _ in range(3)],
        out_specs=(_row_spec(ZA_W), _row_spec(ZB_W), _vec_spec(ZA_W), _vec_spec(ZB_W)),
        out_shape=(jax.ShapeDtypeStruct((T, ZA_W), BF16), jax.ShapeDtypeStruct((T, ZB_W), BF16),
                   jax.ShapeDtypeStruct((1, ZA_W), F32), jax.ShapeDtypeStruct((1, ZB_W), F32)),
        scratch_shapes=[_token_scratch(ROW_BLK, W)],
        compiler_params=pltpu.CompilerParams(dimension_semantics=("arbitrary",)),
    )(*args)


def _place():
    x, y, c = lax.axis_index("x"), lax.axis_index("y"), lax.axis_index("c")
    chips = [(1 - x, y), (x, 1 - y), (1 - x, 1 - y)]
    return x, y, c, chips


def _cast_place(shards, k_idx, *, name):
    n, steps = len(shards), 4

    def body(k_ref, *refs):
        for s_ref, o_ref in zip(refs[:n], refs[n:]):
            o_ref[...] = s_ref[...].astype(o_ref.dtype)

    blocks = [(a.shape[0] // steps, a.shape[1]) for a in shards]
    grid_spec = pltpu.PrefetchScalarGridSpec(
        num_scalar_prefetch=1, grid=(steps,),
        in_specs=[pl.BlockSpec(blk, lambda t, k_ref: (t, 0)) for blk in blocks],
        out_specs=tuple(pl.BlockSpec(blk, lambda t, k_ref: (k_ref[0] * steps + t, 0)) for blk in blocks))
    return pl.pallas_call(
        body, name=name, grid_spec=grid_spec,
        out_shape=tuple(jax.ShapeDtypeStruct((N_CHIPS * a.shape[0], a.shape[1]), BF16) for a in shards),
        compiler_params=pltpu.CompilerParams(dimension_semantics=("parallel",),
                                             vmem_limit_bytes=_vmem_limit(*[_nbytes(b, F32) for b in blocks])),
    )(k_idx, *shards)


def _gather_carry(fulls, jobs):
    n = len(jobs)

    def piece(ref, chip_idx, half, part):
        rs = ref.shape[0] // N_CHIPS
        rh = rs // 2
        rows = rh // part[1]
        return ref.at[pl.ds(chip_idx * rs + half * rh + part[0] * rows, rows)]

    def copy(sems, sem, src_ref, dst_ref, to):
        return pltpu.make_async_remote_copy(src_ref=src_ref, dst_ref=dst_ref, send_sem=sems[0].at[sem],
                                            recv_sem=sems[1].at[sem], device_id=to, device_id_type=MESH)

    def to_chips(ins, outs, sems, j, arrival, hops=("both", "chips")):
        i, part, hop = jobs[j]
        x, y, c, chips = _place()
        res = []
        for t, chip in enumerate(chips if hop in hops else ()):
            theirs = piece(outs[i], 2 * chip[0] + chip[1], c, part)
            src, dst = (theirs, theirs) if arrival else (piece(ins[i], 2 * x + y, c, part), piece(outs[i], 2 * x + y, c, part))
            res.append(copy(sems, 3 * j + t, src, dst, (*chip, c)))
        return res

    def to_sibling(outs, sems, j, arrival, hops=("both", "sibling")):
        i, part, hop = jobs[j]
        x, y, c, chips = _place()
        res = []
        for t, chip in enumerate(chips if hop in hops else ()):
            region = piece(outs[i], 2 * chip[0] + chip[1], 1 - c if arrival else c, part)
            res.append(copy(sems, 3 * n + 3 * j + t, region, region, (x, y, 1 - c)))
        return res

    def start(ins, outs, sems):
        for j in range(n):
            for send in to_chips(ins, outs, sems, j, False) + to_sibling(outs, sems, j, False, ("sibling",)):
                send.start()

    def before_last(ins, outs, sems):
        for j in range(n):
            for arrival, send in zip(to_chips(ins, outs, sems, j, True, ("both",)),
                                     to_sibling(outs, sems, j, False, ("both",))):
                arrival.wait_recv()
                send.start()

    def finish(ins, outs, sems):
        for j in range(n):
            for arrival in to_chips(ins, outs, sems, j, True, ("chips",)) + to_sibling(outs, sems, j, True):
                arrival.wait_recv()
        for j in range(n):
            for send in to_chips(ins, outs, sems, j, False) + to_sibling(outs, sems, j, False):
                send.wait_send()

    return _Carry(fulls, [jax.ShapeDtypeStruct(a.shape, a.dtype) for a in fulls], {i: i for i in range(len(fulls))},
                  [pltpu.SemaphoreType.DMA((6 * n,)), pltpu.SemaphoreType.DMA((6 * n,))], start, finish, before_last)


def _pair_swap_carry(grads):
    n = len(grads)

    def copy(ins, outs, sems, i):
        x, y, c, _ = _place()
        return pltpu.make_async_remote_copy(src_ref=ins[i].at[:, 1 - c], dst_ref=outs[i], send_sem=sems[0].at[i],
                                            recv_sem=sems[1].at[i], device_id=(x, y, 1 - c), device_id_type=MESH)

    def start(ins, outs, sems):
        for i in range(n):
            copy(ins, outs, sems, i).start()

    def finish(ins, outs, sems):
        for i in range(n):
            copy(ins, outs, sems, i).wait()

    return _Carry(grads, [jax.ShapeDtypeStruct((a.shape[0], a.shape[2], a.shape[3]), a.dtype) for a in grads], {},
                  [pltpu.SemaphoreType.DMA((n,)), pltpu.SemaphoreType.DMA((n,))], start, finish)


def _pair_sum(g4, got, c_idx, *, name):
    _, _, rh, cs = g4.shape
    rb = _row_block(rh, 512, 16)

    def body(c_ref, own_ref, got_ref, o_ref):
        o_ref[...] = (own_ref[...].astype(F32) + got_ref[...].astype(F32)).astype(o_ref.dtype)

    grid_spec = pltpu.PrefetchScalarGridSpec(
        num_scalar_prefetch=1, grid=(N_CHIPS, rh // rb),
        in_specs=[pl.BlockSpec((None, None, rb, cs), lambda k, t, c_ref: (k, c_ref[0], t, 0)),
                  pl.BlockSpec((None, rb, cs), lambda k, t, c_ref: (k, t, 0))],
        out_specs=pl.BlockSpec((None, rb, cs), lambda k, t, c_ref: (k, t, 0)))
    return pl.pallas_call(
        body, name=name, grid_spec=grid_spec, out_shape=jax.ShapeDtypeStruct((N_CHIPS, rh, cs), BF16),
        compiler_params=pltpu.CompilerParams(dimension_semantics=("parallel", "parallel")),
    )(c_idx, g4, got)


def _chip_scatter_carry(sums):
    n = len(sums)

    def copies(ins, outs, sems):
        x, y, c, chips = _place()
        return [pltpu.make_async_remote_copy(src_ref=ins[i].at[2 * chip[0] + chip[1]], dst_ref=outs[i].at[j],
                                             send_sem=sems[0].at[3 * i + j], recv_sem=sems[1].at[3 * i + j],
                                             device_id=(*chip, c), device_id_type=MESH)
                for i in range(n) for j, chip in enumerate(chips)]

    def start(ins, outs, sems):
        for cp in copies(ins, outs, sems):
            cp.start()

    def finish(ins, outs, sems):
        for cp in copies(ins, outs, sems):
            cp.wait()

    return _Carry(sums, [jax.ShapeDtypeStruct((3,) + a.shape[1:], a.dtype) for a in sums], {},
                  [pltpu.SemaphoreType.DMA((3 * n,)), pltpu.SemaphoreType.DMA((3 * n,))], start, finish)


def _chip_sum(sums, gots, kc_idx, *, name):
    n, steps = len(sums), 2

    def body(kc_ref, *refs):
        for own_ref, got_ref, o_ref in zip(refs[:n], refs[n:2 * n], refs[2 * n:]):
            acc = own_ref[...].astype(F32)
            for j in range(3):
                acc = acc + got_ref[j].astype(F32)
            o_ref[...] = acc

    blocks = [(a.shape[1] // steps, a.shape[2]) for a in sums]
    grid_spec = pltpu.PrefetchScalarGridSpec(
        num_scalar_prefetch=1, grid=(steps,),
        in_specs=[pl.BlockSpec((None,) + blk, lambda t, kc: (kc[0], t, 0)) for blk in blocks]
        + [pl.BlockSpec((3,) + blk, lambda t, kc: (0, t, 0)) for blk in blocks],
        out_specs=tuple(pl.BlockSpec(blk, lambda t, kc: (kc[1] * steps + t, 0)) for blk in blocks))
    return pl.pallas_call(
        body, name=name, grid_spec=grid_spec,
        out_shape=tuple(jax.ShapeDtypeStruct((2 * a.shape[1], a.shape[2]), F32) for a in sums),
        compiler_params=pltpu.CompilerParams(dimension_semantics=("parallel",),
                                             vmem_limit_bytes=_vmem_limit(*[3 * _nbytes(b, F32) for b in blocks])),
    )(kc_idx, *sums, *gots)


def _half_swap_carry(shards):
    n = len(shards)

    def copy(ins, outs, sems, i, to_my_half):
        x, y, c, _ = _place()
        rh = ins[i].shape[0] // 2
        half = c if to_my_half else 1 - c
        return pltpu.make_async_remote_copy(
            src_ref=ins[i].at[pl.ds(c * rh, rh)], dst_ref=outs[i].at[pl.ds(half * rh, rh)],
            send_sem=sems[0].at[i], recv_sem=sems[1].at[i], device_id=(x, y, 1 - c), device_id_type=MESH)

    def start(ins, outs, sems):
        for i in range(n):
            copy(ins, outs, sems, i, True).start()

    def finish(ins, outs, sems):
        for i in range(n):
            copy(ins, outs, sems, i, False).wait_recv()
        for i in range(n):
            copy(ins, outs, sems, i, True).wait_send()

    return _Carry(shards, [jax.ShapeDtypeStruct(a.shape, a.dtype) for a in shards], {i: i for i in range(n)},
                  [pltpu.SemaphoreType.DMA((n,)), pltpu.SemaphoreType.DMA((n,))], start, finish)


SMALL_ROW = 1024


def _small_layout(shapes):
    starts, row = [], 0
    for r, c in shapes:
        starts.append(row)
        row += -(-c // SMALL_ROW) if r == 1 else r
    return starts, -(-row // SUBLANES) * SUBLANES


def _small_pieces(shape, start):
    r, c = shape
    if r == 1:
        return [(start + q, min(SMALL_ROW, c - q * SMALL_ROW), q * SMALL_ROW) for q in range(-(-c // SMALL_ROW))]
    return None


def _whole_spec(shape):
    return pl.BlockSpec(tuple(shape), lambda i: (0,) * len(shape))


def _small_all_reduce(parts, *, name, carry=None):
    shapes = [a.shape for a in parts]
    starts, rows = _small_layout(shapes)
    n = len(parts)
    n_tables = sum(1 for r, _ in shapes if r > 1)
    assert all(r > 1 and c <= LANES for r, c in shapes[:n_tables]) and all(r == 1 for r, _ in shapes[n_tables:])
    table_rows = starts[n_tables]
    assert table_rows % SUBLANES == 0

    def regions(slot):
        return slot.at[pl.ds(0, table_rows), pl.ds(0, LANES)], slot.at[pl.ds(table_rows, rows - table_rows)]

    def peer(d):
        x, y, c, _ = _place()
        return 1 - x if d & 4 else x, 1 - y if d & 2 else y, 1 - c if d & 1 else c

    def slot_of(d):
        px, py, pc = peer(d)
        return 4 * px + 2 * py + pc

    def copy(refs, d, part, dst_slot):
        buf, send_sems, recv_sems = refs[2 * n:]
        sem = 2 * (d - 1) + part
        return pltpu.make_async_remote_copy(
            src_ref=regions(buf.at[slot_of(0)])[part], dst_ref=regions(buf.at[dst_slot])[part],
            send_sem=send_sems.at[sem], recv_sem=recv_sems.at[sem], device_id=peer(d), device_id_type=MESH)

    def send(*refs):
        ins, mine = refs[0:n], refs[2 * n].at[slot_of(0)]
        mine[...] = jnp.zeros((rows, SMALL_ROW), F32)
        for ref, shape, start in zip(ins, shapes, starts):
            pieces = _small_pieces(shape, start)
            if pieces is None:
                mine[start:start + shape[0], 0:shape[1]] = ref[...]
            else:
                for row, width, col in pieces:
                    mine[row:row + 1, 0:width] = ref[:, col:col + width]
        for d in range(1, 8):
            for part in range(2):
                copy(refs, d, part, slot_of(0)).start()

    def receive(*refs):
        outs, buf = refs[n:2 * n], refs[2 * n]
        mine = buf.at[slot_of(0)]
        for d in range(1, 8):
            for part in range(2):
                copy(refs, d, part, slot_of(d)).wait_recv()
        for d in range(1, 8):
            for part in range(2):
                copy(refs, d, part, slot_of(0)).wait_send()
        tables, vectors = regions(buf.at[0])
        acc_t, acc_v = tables[...], vectors[...]
        for s in range(1, 8):
            tables, vectors = regions(buf.at[s])
            acc_t, acc_v = acc_t + tables[...], acc_v + vectors[...]
        tables, vectors = regions(mine)
        tables[...] = acc_t
        vectors[...] = acc_v
        for ref, shape, start in zip(outs, shapes, starts):
            pieces = _small_pieces(shape, start)
            if pieces is None:
                ref[...] = mine[start:start + shape[0], 0:shape[1]]
            else:
                for row, width, col in pieces:
                    ref[:, col:col + width] = mine[row:row + 1, 0:width]

    def body(*refs):
        send(*refs)
        receive(*refs)

    whole = [_whole_spec(s) for s in shapes]
    return _call(body if carry is None else receive, before_carry=None if carry is None else send,
                 name=name, grid=(1,), in_specs=whole, out_specs=whole,
                 out_shape=[jax.ShapeDtypeStruct(s, F32) for s in shapes], args=parts,
                 scratch=[pltpu.VMEM((8, rows, SMALL_ROW), F32), pltpu.SemaphoreType.DMA((14,)),
                          pltpu.SemaphoreType.DMA((14,))], carry=carry)


def _adamw_small(ws, gs, ms, vs, *, name):
    n = len(ws)
    c1 = 1.0 - ADAM_B1 ** ADAM_STEP
    c2 = 1.0 - ADAM_B2 ** ADAM_STEP

    def body(*refs):
        for k in range(n):
            w_ref, g_ref, m_ref, v_ref = (refs[t * n + k] for t in range(4))
            go_ref, d_ref, mo_ref, vo_ref = (refs[(4 + t) * n + k] for t in range(4))
            gv = g_ref[...]
            go_ref[...] = gv
            mn = ADAM_B1 * m_ref[...] + (1.0 - ADAM_B1) * gv
            vn = ADAM_B2 * v_ref[...] + (1.0 - ADAM_B2) * (gv * gv)
            d_ref[...] = -ADAM_LR * ((mn / c1) / (jnp.sqrt(vn / c2) + ADAM_EPS) + ADAM_WD * w_ref[...])
            mo_ref[...] = mn
            vo_ref[...] = vn

    whole = [_whole_spec(a.shape) for a in ws]
    shapes = tuple(jax.ShapeDtypeStruct(a.shape, F32) for a in ws)
    res = pl.pallas_call(
        body, name=name, grid=(1,), in_specs=whole * 4, out_specs=tuple(whole * 4), out_shape=shapes * 4,
    )(*ws, *gs, *ms, *vs)
    return res[0:n], res[n:2 * n], res[2 * n:3 * n], res[3 * n:4 * n]


def _adamw(ws, gs, ms, vs, *, name):
    n, steps = len(ws), 8
    c1 = 1.0 - ADAM_B1 ** ADAM_STEP
    c2 = 1.0 - ADAM_B2 ** ADAM_STEP

    def body(*refs):
        for k in range(n):
            w_ref, g_ref, m_ref, v_ref = (refs[t * n + k] for t in range(4))
            go_ref, d_ref, mo_ref, vo_ref = (refs[(4 + t) * n + k] for t in range(4))
            gv = g_ref[...]
            go_ref[...] = gv
            mn = ADAM_B1 * m_ref[...] + (1.0 - ADAM_B1) * gv
            vn = ADAM_B2 * v_ref[...] + (1.0 - ADAM_B2) * (gv * gv)
            d_ref[...] = -ADAM_LR * ((mn / c1) / (jnp.sqrt(vn / c2) + ADAM_EPS) + ADAM_WD * w_ref[...])
            mo_ref[...] = mn
            vo_ref[...] = vn

    blocks = [(a.shape[0] // steps, a.shape[1]) for a in ws]
    specs = [pl.BlockSpec(blk, lambda i: (i, 0)) for blk in blocks]
    shapes = tuple(jax.ShapeDtypeStruct(a.shape, F32) for a in ws)
    res = pl.pallas_call(
        body, name=name, grid=(steps,), in_specs=specs * 4, out_specs=tuple(specs * 4), out_shape=shapes * 4,
        compiler_params=pltpu.CompilerParams(dimension_semantics=("parallel",),
                                             vmem_limit_bytes=_vmem_limit(*[8 * _nbytes(b, F32) for b in blocks])),
    )(*ws, *gs, *ms, *vs)
    return res[0:n], res[n:2 * n], res[2 * n:3 * n], res[3 * n:4 * n]


def _local_step(x, p, target, small, sched):
    T = x.shape[0]
    W = MIX_W
    rel_bias = small["rel_bias"]
    b_in_a, b_in_b = small["b_in"][:, 0:ZA_W], small["b_in"][:, ZA_W:]

    idx_np = [_t5_index(A_WINDOW - 1, 1)] + [_t5_index(window // dil, dil) for window, dil in B_PATTERNS]
    idx_all = jnp.asarray(np.stack(idx_np))
    n1, bias_all = sched.run(_prologue, x, small["ffn1_pre_g"], rel_bias, idx_all, name="prologue")
    w_gu1 = sched.weight("ffn1_w_gu")
    *gu1, a1 = sched.run(_ffn_up, n1, w_gu1, bm=512, name="ffn1_gu")
    w_d1 = sched.weight("ffn1_w_down")
    f1, h1, n2 = sched.run(_mm_resid_norm, a1, w_d1, x, small["ffn1_post_g"], 0.5, small["attn_pre_g"], bm=512,
                           name="ffn1_down")
    win_a, win_b = sched.weight("w_in")
    za = _mm_nt(n2, win_a, bm=1024, bn=ZA_W, out_dtype=BF16, name="in_proj_a", add=b_in_a)
    dils = tuple(dil for _, dil in B_PATTERNS)
    zb_by_dil = _mm_nt_classes(n2, win_b, b_in_b, dils, bm=512, name="in_proj_b")

    a_args = dict(r=1, q_col=0, k_col=W // LANES, v_col=W // LANES + 1, stride=0, pattern=0, kv_shared=True)
    sink_row = jnp.repeat(small["sinks"], HEAD_DIM, axis=1)
    out_a, out_a_bf, lse_a = sched.run(_band_fwd, za, za, za, bias_all, name="band_a_fwd", sink=sink_row, **a_args)

    no_sink = jnp.full((1, W), NEG, F32)
    b_ctx, branches = [], []
    for t, dil in enumerate(dils):
        zb_r = zb_by_dil[t]
        args = dict(r=dil, q_col=0, k_col=1, v_col=2, stride=ZB_W // W, pattern=1 + t)
        branches.append(sched.run(_band_fwd, zb_r, zb_r, zb_r, bias_all, name=f"band_b{t}_fwd", **args))
        b_ctx.append((jnp.asarray(idx_np[1 + t]), zb_r, args))
    out_b, out_b_bf, lse_b = _merge(branches, dils, no_sink, name="merge_b")

    mix = jnp.concatenate([out_a_bf, out_b_bf], axis=1)
    w_out = sched.weight("w_out")
    att, h2, n3 = _mm_resid_norm(mix, w_out, h1, small["attn_post_g"], 1.0, small["ffn2_pre_g"], bm=512,
                                 name="out_proj", bias=small["b_out"])
    w_gu2, w_d2 = sched.weight("ffn2_w_gu"), sched.weight("ffn2_w_down")
    *gu2, a2 = _ffn_up(n3, w_gu2, bm=512, name="ffn2_gu")
    f2, h3, n4 = _mm_resid_norm(a2, w_d2, h2, small["ffn2_post_g"], 0.5, small["ple_pre_g"], bm=512,
                                name="ffn2_down")
    w_gate = sched.weight("w_ple_gate")
    p_bf = p.astype(BF16)
    dh4, de, dgp, loss, d_ple_post = _ple_head(n4, w_gate, p_bf, sched.weight("w_ple_proj"), h3, small["ple_post_g"],
                                               target, bm=512, name="ple_head")

    gs = {"ple_post_g": d_ple_post}
    sched.grad("w_ple_proj", _mm_tn(p_bf, de, bm=256, bn=1024, out_dtype=BF16, name="d_w_ple"))
    sched.grad("w_ple_gate", _mm_tn(n4, dgp, bm=512, bn=1024, out_dtype=BF16, name="d_w_gate"))
    dh3, df2, gs["ple_pre_g"], gs["ffn2_post_g"], _ = sched.run(
        _mm_nt_norms_bwd, dgp, w_gate, dh4, h3, small["ple_pre_g"], f2, small["ffn2_post_g"], 0.5, bm=512, name="d_n4")

    def ffn_bwd(df, a, gu, n, w_down, w_gu, tag, *norm_args):
        dgu = sched.run(_ffn_down_bwd, df, w_down, gu[0], gu[1], bm=512, name=f"ffn{tag}_d_a")
        sched.grad(f"ffn{tag}_w_gu", _mm_tn(n, dgu, bm=512, bn=1408, out_dtype=BF16, name=f"ffn{tag}_d_w_gu",
                                            n_stack=N_CHIPS))
        sched.grad(f"ffn{tag}_w_down", sched.run(_mm_tn, a, df, bm=256, bn=1024, out_dtype=BF16,
                                                name=f"ffn{tag}_d_w_down"))
        return sched.run(_mm_nt_norms_bwd, dgu, w_gu, *norm_args, bm=512, name=f"ffn{tag}_d_n")

    dh2, datt, gs["ffn2_pre_g"], gs["attn_post_g"], gs["b_out"] = ffn_bwd(
        df2, a2, gu2, n3, w_d2, w_gu2, "2", dh3, h2, small["ffn2_pre_g"], att, small["attn_post_g"], 1.0)
    sched.grad("w_out", _mm_tn(mix, datt, bm=512, bn=1024, out_dtype=BF16, name="d_w_out"))
    dmix = sched.run(_mm_nt, datt, w_out, bm=1024, bn=1024, out_dtype=F32, name="d_mix")

    do_a, stat_a, dsink, do_b, stat_b = _attn_delta(dmix, (out_a, out_b), (lse_a, lse_b), sink_row, dils,
                                                    name="attn_delta")
    gs["sinks"] = dsink[:, ::HEAD_DIM]
    dq_a, dk_a, dv_a, dbias_a = sched.run(_band_bwd, za, za, za, do_a, stat_a, bias_all, name="band_a_bwd", **a_args)
    d_rel_a = _bias_grad(dbias_a, jnp.asarray(idx_np[0]), name="d_rel_a")
    b_grads = []
    d_rel_b = None
    for t, (idx, zb_r, args) in enumerate(b_ctx):
        dq, dk, dv, dbias = sched.run(_band_bwd, zb_r, zb_r, zb_r, do_b[t], stat_b[t], bias_all,
                                      name=f"band_b{t}_bwd", **args)
        b_grads.append((dq, dk, dv))
        d_rel = _bias_grad(dbias, idx, name=f"d_rel_b{t}")
        d_rel_b = d_rel if d_rel_b is None else d_rel_b + d_rel
    gs["rel_bias"] = jnp.concatenate([d_rel_a[:, 0:2 * N_PAIRS], d_rel_b[:, 0:2 * N_PAIRS]], axis=1)
    dza, dzb, dsum_a, dsum_b = _dz_assemble(dq_a, dk_a, dv_a, b_grads, dils, name="d_z")
    gs["b_in"] = jnp.concatenate([dsum_a, dsum_b], axis=1)
    sched.grad("w_in", (_mm_tn(dza, n2, bm=ZA_W, bn=1024, out_dtype=BF16, name="d_w_in_a"),
                        _mm_tn(dzb, n2, bm=ZB_W // 2, bn=1024, out_dtype=BF16, name="d_w_in_b")))
    dn2_a = _mm_nn(dza, win_a, bm=1024, bn=1024, out_dtype=F32, name="d_n2_a")
    dh1, df1, gs["attn_pre_g"], gs["ffn1_post_g"], _ = sched.run(
        _mm_nt_norms_bwd, dzb, win_b, dh2, h1, small["attn_pre_g"], f1, small["ffn1_post_g"], 0.5, bm=512,
        name="d_n2_b", add=dn2_a, b_is_kn=True)
    grad_x, gs["ffn1_pre_g"] = ffn_bwd(df1, a1, gu1, n1, w_d1, w_gu1, "1", dh1, x, small["ffn1_pre_g"], None, None, 0.0)
    return loss, grad_x, gs


def _to_compute(name, full):
    st = full.reshape(N_CHIPS, full.shape[0] // N_CHIPS, full.shape[1])
    if name in ("ffn1_w_gu", "ffn2_w_gu"):
        return st
    if name == "w_ple_proj":
        return jnp.transpose(st, (1, 0, 2)).reshape(st.shape[1], -1)
    if name == "w_in":
        return full[0:ZA_W], full[ZA_W:]
    return full


def _to_comm(name, g):
    if name == "w_in":
        g = jnp.concatenate(g, axis=0)
    if name == "w_ple_proj":
        g = jnp.transpose(g.reshape(g.shape[0], N_CHIPS, -1), (1, 0, 2))
    rs = g.shape[1] if g.ndim == 3 else g.shape[0] // N_CHIPS
    return g.reshape(N_CHIPS, 2, rs // 2, g.shape[-1])


class _Schedule:
    GATHER = {
        "prologue": [("ffn1_w_gu", (0, 1), "both")],
        "ffn1_gu": [("ffn1_w_down", (0, 1), "both"), ("w_in", (0, 1), "both")],
        "band_a_fwd": [("ffn2_w_gu", (0, 2), "chips"), ("w_out", (0, 1), "chips")],
        "band_b0_fwd": [("ffn2_w_gu", (0, 2), "sibling"), ("w_out", (0, 1), "sibling"),
                        ("ffn2_w_gu", (1, 2), "chips"), ("w_ple_gate", (0, 1), "chips"), ("w_ple_proj", (0, 1), "chips")],
        "band_b1_fwd": [("ffn2_w_gu", (1, 2), "sibling"), ("w_ple_gate", (0, 1), "sibling"),
                        ("w_ple_proj", (0, 1), "sibling"), ("ffn2_w_down", (0, 1), "both")],
    }
    SWAP = {"ffn2_d_w_down": ["w_ple_proj", "w_ple_gate", "ffn2_w_gu"], "ffn2_d_n": ["ffn2_w_down"],
            "band_a_bwd": ["w_out"], "d_n2_b": ["w_in"], "ffn1_d_w_down": ["ffn1_w_gu"], "ffn1_d_n": ["ffn1_w_down"]}
    SCATTER = {"ffn2_d_n": ["ffn2_w_gu"], "band_a_bwd": ["w_ple_proj", "w_ple_gate", "ffn2_w_down"],
               "d_n2_b": ["w_out"], "ffn1_d_w_down": ["w_in"], "ffn1_d_n": ["ffn1_w_gu"]}

    def __init__(self, placed, c_idx, kc_idx):
        self.full = dict(placed)
        self.missing = {n: 1.0 for n in placed}
        self.c_idx, self.kc_idx = c_idx, kc_idx
        self.grads, self.swapped, self.pair_sums, self.scattered = {}, {}, {}, {}

    def _gather(self, entries):
        if not entries:
            return [], []
        names = list(dict.fromkeys(n for n, _, _ in entries))
        carry = _gather_carry([self.full[n] for n in names], [(names.index(n), pt, hops) for n, pt, hops in entries])

        def done():
            self.full.update(zip(names, carry.results))
            for n, part, hops in entries:
                if hops != "chips":
                    self.missing[n] -= 1.0 / part[1]
        return [carry], [done]

    def weight(self, name):
        assert abs(self.missing[name]) < 1e-9, (name, self.missing[name])
        return _to_compute(name, self.full[name])

    def grad(self, name, g):
        self.grads[name] = _to_comm(name, g)

    def _swap(self, names):
        carry = _pair_swap_carry([self.grads[n] for n in names])

        def done():
            self.swapped.update(zip(names, carry.results))
        return carry, done

    def _scatter(self, names):
        for n in names:
            self.pair_sums[n] = _pair_sum(self.grads[n], self.swapped[n], self.c_idx, name=f"grad_pair_sum_{n}")
        carry = _chip_scatter_carry([self.pair_sums[n] for n in names])

        def done():
            self.scattered.update(zip(names, carry.results))
        return carry, done

    def run(self, fn, *args, name, **kw):
        carries, after = self._gather(self.GATHER.get(name, []))
        for names, make in ((self.SWAP.get(name), self._swap), (self.SCATTER.get(name), self._scatter)):
            if names:
                carry, done = make(names)
                carries.append(carry)
                after.append(done)
        out = fn(*args, name=name, carry=_join(carries), **kw)
        for done in after:
            done()
        return out

    def finish(self, last_carrier):
        left = [n for n in BIG if n not in self.scattered]
        to_swap = [n for n in left if n not in self.swapped]
        if to_swap:
            carry, done = self._swap(to_swap)
            _comm_call(carry, name="grad_pair_swap")
            done()
        carry, done = self._scatter(left) if left else (None, lambda: None)
        carried = last_carrier(carry=carry)
        done()
        halves = _chip_sum([self.pair_sums[n] for n in BIG], [self.scattered[n] for n in BIG], self.kc_idx,
                           name="grad_chip_sum")
        carry = _half_swap_carry(halves)
        _comm_call(carry, name="grad_half_swap")
        return dict(zip(BIG, carry.results)), carried


def kernel(x, p, rel_bias, ffn1_pre_g, ffn1_w_gu, ffn1_w_down, ffn1_post_g, attn_pre_g, w_in, b_in, sinks, w_out, b_out, attn_post_g, ffn2_pre_g, ffn2_w_gu, ffn2_w_down, ffn2_post_g, ple_pre_g, w_ple_gate, w_ple_proj, ple_post_g, loss_target, m_rel_bias, m_ffn1_pre_g, m_ffn1_w_gu, m_ffn1_w_down, m_ffn1_post_g, m_attn_pre_g, m_w_in, m_b_in, m_sinks, m_w_out, m_b_out, m_attn_post_g, m_ffn2_pre_g, m_ffn2_w_gu, m_ffn2_w_down, m_ffn2_post_g, m_ple_pre_g, m_w_ple_gate, m_w_ple_proj, m_ple_post_g, v_rel_bias, v_ffn1_pre_g, v_ffn1_w_gu, v_ffn1_w_down, v_ffn1_post_g, v_attn_pre_g, v_w_in, v_b_in, v_sinks, v_w_out, v_b_out, v_attn_post_g, v_ffn2_pre_g, v_ffn2_w_gu, v_ffn2_w_down, v_ffn2_post_g, v_ple_pre_g, v_w_ple_gate, v_w_ple_proj, v_ple_post_g):
    given = dict(locals())
    w = {n: given[n] for n in WEIGHTS}
    m = {n: given["m_" + n] for n in WEIGHTS}
    v = {n: given["v_" + n] for n in WEIGHTS}
    c_pos = lax.axis_index("c").astype(jnp.int32)
    k_pos = (2 * lax.axis_index("x") + lax.axis_index("y")).astype(jnp.int32)
    c_idx, k_idx, kc_idx = c_pos.reshape(1), k_pos.reshape(1), jnp.stack([k_pos, c_pos])

    def shard(a, n):
        return jnp.transpose(a[0]) if n == "w_in" else a[0]

    def unshard(a, n):
        return (jnp.transpose(a) if n == "w_in" else a)[None]

    placed = _cast_place([shard(w[n], n) for n in BIG], k_idx, name="place_shards")
    sched = _Schedule(dict(zip(BIG, placed)), c_idx, kc_idx)
    small = {n: (w[n] if n == "rel_bias" else w[n].reshape(1, -1)) for n in SMALL}

    loss_part, grad_x, gs = _local_step(x[0], p[0, 0], loss_target[0], small, sched)

    grads_big, (*small_grads, loss_row) = sched.finish(
        partial(_small_all_reduce, [gs[n] for n in SMALL] + [loss_part], name="small_all_reduce"))
    loss = loss_row[0, 0]

    grads, delta, new_m, new_v = {}, {}, {}, {}
    res = _adamw([shard(w[n], n) for n in BIG], [grads_big[n] for n in BIG], [shard(m[n], n) for n in BIG],
                 [shard(v[n], n) for n in BIG], name="adamw_big")
    for n, gg, dd, mm, vv in zip(BIG, *res):
        grads[n], delta[n], new_m[n], new_v[n] = (unshard(a, n) for a in (gg, dd, mm, vv))
    res = _adamw_small([w[n] for n in SMALL], small_grads, [m[n] for n in SMALL], [v[n] for n in SMALL],
                       name="adamw_small")
    for name, gg, dd, mm, vv in zip(SMALL, *res):
        grads[name], delta[name], new_m[name], new_v[name] = gg, dd, mm, vv

    return (loss, grad_x[None], *[grads[n] for n in WEIGHTS], *[delta[n] for n in WEIGHTS],
            *[new_m[n] for n in WEIGHTS], *[new_v[n] for n in WEIGHTS])
```

```python
import math
from functools import partial

import jax
import jax.numpy as jnp
import numpy as np
from jax import lax
from jax.experimental import pallas as pl
from jax.experimental.pallas import tpu as pltpu

F32 = jnp.float32
BF16 = jnp.bfloat16
MESH = pl.DeviceIdType.MESH

EPS = 1e-6
NEG = -1e30
LANES = 128
SUBLANES = 8
HEAD_DIM = 64
QBLK = 128
N_PAIRS = 4
MIX_W = N_PAIRS * LANES
A_WINDOW = 128
B_PATTERNS = ((128, 1), (512, 4), (2048, 16))
NUM_BUCKETS = 32
MAX_DISTANCE = 2048
ZA_W = 768
ZB_W = 1536
N_CHIPS = 4
VMEM_BYTES_V7X = 64 * 2**20

ADAM_LR, ADAM_B1, ADAM_B2, ADAM_EPS, ADAM_WD, ADAM_STEP = 0.001, 0.9, 0.999, 1e-08, 0.01, 10

BIG = ("ffn1_w_gu", "ffn1_w_down", "w_in", "w_out", "ffn2_w_gu", "ffn2_w_down", "w_ple_gate", "w_ple_proj")
SMALL = ("rel_bias", "ffn1_pre_g", "ffn1_post_g", "attn_pre_g", "b_in", "sinks", "b_out", "attn_post_g",
         "ffn2_pre_g", "ffn2_post_g", "ple_pre_g", "ple_post_g")
WEIGHTS = ("rel_bias", "ffn1_pre_g", "ffn1_w_gu", "ffn1_w_down", "ffn1_post_g", "attn_pre_g", "w_in", "b_in",
           "sinks", "w_out", "b_out", "attn_post_g", "ffn2_pre_g", "ffn2_w_gu", "ffn2_w_down", "ffn2_post_g",
           "ple_pre_g", "w_ple_gate", "w_ple_proj", "ple_post_g")


def _vmem_limit(*block_bytes):
    need = 2 * sum(block_bytes) + max(block_bytes) * 2 + (4 << 20)
    return int(min(max(need, 32 << 20), VMEM_BYTES_V7X - (6 << 20)))


def _nbytes(shape, dtype):
    return int(np.prod(shape)) * jnp.dtype(dtype).itemsize


MXU_WIDTH_V7X = 256


def _col_chunks(n):
    return [slice(c, min(c + MXU_WIDTH_V7X, n)) for c in range(0, n, MXU_WIDTH_V7X)]


def _row_halves(rows):
    return [slice(0, rows // 2), slice(rows // 2, rows)]


def _row_block(rows, cap, mult):
    for cand in range(min(rows, cap), 0, -1):
        if rows % cand == 0 and cand % mult == 0:
            return cand
    raise ValueError((rows, cap, mult))


class _Carry:
    def __init__(self, operands, out_shapes, aliases, sems, start, finish, before_last=None):
        self.operands, self.out_shapes, self.aliases, self.sems = list(operands), list(out_shapes), dict(aliases), list(sems)
        self.start, self.finish = start, finish
        self.before_last = before_last or (lambda ins, outs, sems: None)
        self.results = None


def _join(carries):
    carries = [c for c in carries if c is not None]
    if not carries:
        return None
    if len(carries) == 1:
        return carries[0]
    offs, pos = [], [0, 0, 0]
    for c in carries:
        offs.append(tuple(pos))
        pos = [pos[0] + len(c.operands), pos[1] + len(c.out_shapes), pos[2] + len(c.sems)]
    aliases = {}
    for c, (oi, oo, _) in zip(carries, offs):
        aliases.update({oi + i: oo + o for i, o in c.aliases.items()})

    def run(which):
        def fn(ins, outs, sems):
            for c, (oi, oo, os_) in zip(carries, offs):
                getattr(c, which)(ins[oi:oi + len(c.operands)], outs[oo:oo + len(c.out_shapes)],
                                  sems[os_:os_ + len(c.sems)])
        return fn

    joined = _Carry([a for c in carries for a in c.operands], [s for c in carries for s in c.out_shapes], aliases,
                    [s for c in carries for s in c.sems], run("start"), run("finish"), run("before_last"))
    joined.parts = (carries, offs)
    return joined


def _set_results(carry, outs):
    carry.results = list(outs)
    if hasattr(carry, "parts"):
        for c, (_, oo, _) in zip(*carry.parts):
            _set_results(c, outs[oo:oo + len(c.out_shapes)])


ANY = pl.BlockSpec(memory_space=pl.ANY)


def _call(body, *, name, grid, in_specs, out_specs, out_shape, args, scratch=(), vmem_limit=None, carry=None,
          semantics=None, before_carry=None):
    assert before_carry is None or carry is not None
    n_ci, n_co, n_cs = len(args), len(out_shape), len(scratch)
    semantics = semantics or ("parallel",) * len(grid)
    if carry is None:
        res = pl.pallas_call(
            body, name=name, grid=grid, in_specs=list(in_specs), out_specs=tuple(out_specs), out_shape=tuple(out_shape),
            scratch_shapes=list(scratch),
            compiler_params=pltpu.CompilerParams(dimension_semantics=semantics, vmem_limit_bytes=vmem_limit),
        )(*args)
        return list(res)
    n_pi, n_po = len(carry.operands), len(carry.out_shapes)

    def full_body(*refs):
        ci, pi = refs[:n_ci], refs[n_ci:n_ci + n_pi]
        o0 = n_ci + n_pi
        co, po = refs[o0:o0 + n_co], refs[o0 + n_co:o0 + n_co + n_po]
        s0 = o0 + n_co + n_po
        cs, ps = refs[s0:s0 + n_cs], refs[s0 + n_cs:]
        ids = [pl.program_id(ax) for ax in range(len(grid))]
        first, last = ids[0] == 0, ids[0] == grid[0] - 1
        for ax in range(1, len(grid)):
            first, last = first & (ids[ax] == 0), last & (ids[ax] == grid[ax] - 1)

        @pl.when(first)
        def _():
            if before_carry is not None:
                before_carry(*ci, *co, *cs)
            carry.start(pi, po, ps)

        several_steps = any(g > 1 for g in grid)
        if several_steps:
            @pl.when(last)
            def _():
                carry.before_last(pi, po, ps)

        body(*ci, *co, *cs)

        @pl.when(last)
        def _():
            if not several_steps:
                carry.before_last(pi, po, ps)
            carry.finish(pi, po, ps)

    res = pl.pallas_call(
        full_body, name=name, grid=grid, in_specs=list(in_specs) + [ANY] * n_pi,
        out_specs=tuple(out_specs) + tuple([ANY] * n_po), out_shape=tuple(out_shape) + tuple(carry.out_shapes),
        scratch_shapes=list(scratch) + carry.sems,
        input_output_aliases={n_ci + i: n_co + o for i, o in carry.aliases.items()},
        compiler_params=pltpu.CompilerParams(dimension_semantics=("arbitrary",) * len(grid),
                                             vmem_limit_bytes=vmem_limit),
    )(*args, *carry.operands)
    _set_results(carry, res[n_co:])
    return list(res[:n_co])


def _comm_call(carry, *, name):
    n_pi, n_po = len(carry.operands), len(carry.out_shapes)

    def body(*refs):
        pi, po, ps = refs[:n_pi], refs[n_pi:n_pi + n_po], refs[n_pi + n_po:]
        carry.start(pi, po, ps)
        carry.before_last(pi, po, ps)
        carry.finish(pi, po, ps)

    res = pl.pallas_call(
        body, name=name, in_specs=[ANY] * n_pi, out_specs=tuple([ANY] * n_po), out_shape=tuple(carry.out_shapes),
        scratch_shapes=carry.sems, input_output_aliases=dict(carry.aliases),
    )(*carry.operands)
    _set_results(carry, res)


def _mm_nn(a, b, *, bm, bn, out_dtype, name, bias=None, carry=None):
    M, K = a.shape
    stacked = b.ndim == 3
    N = b.shape[0] * b.shape[2] if stacked else b.shape[1]
    assert M % bm == 0 and N % bn == 0 and (not stacked or bn == b.shape[2])

    def body(*refs):
        a_ref, b_ref = refs[0], refs[1]
        o_ref = refs[-1]
        acc = jnp.dot(a_ref[...], b_ref[...], preferred_element_type=F32)
        if bias is not None:
            acc = acc + refs[2][...]
        o_ref[...] = acc.astype(o_ref.dtype)

    if stacked:
        b_spec = pl.BlockSpec((None, K, bn), lambda i, j: (j, 0, 0))
    else:
        b_spec = pl.BlockSpec((K, bn), lambda i, j: (0, j))
    in_specs = [pl.BlockSpec((bm, K), lambda i, j: (i, 0)), b_spec]
    args = [a, b]
    if bias is not None:
        in_specs.append(pl.BlockSpec((1, bn), lambda i, j: (0, j)))
        args.append(bias)
    limit = _vmem_limit(_nbytes((bm, K), a.dtype), _nbytes((K, bn), b.dtype), _nbytes((bm, bn), F32))
    return _call(body, name=name, grid=(M // bm, N // bn), in_specs=in_specs,
                 out_specs=[pl.BlockSpec((bm, bn), lambda i, j: (i, j))],
                 out_shape=[jax.ShapeDtypeStruct((M, N), out_dtype)], args=args, vmem_limit=limit, carry=carry)[0]


def _mm_nt(a, b, *, bm, bn, out_dtype, name, add=None, carry=None):
    M, K = a.shape
    stacked = b.ndim == 3
    N = b.shape[1] if stacked else b.shape[0]
    n_sh = b.shape[0] if stacked else 1
    ks = b.shape[2] if stacked else K
    assert M % bm == 0 and N % bn == 0 and n_sh * ks == K
    dn = (((1,), (1,)), ((), ()))

    def body(*refs):
        a_ref, b_ref = refs[0], refs[1]
        o_ref = refs[-1]
        if stacked:
            acc = lax.dot_general(a_ref[:, 0:ks], b_ref[0], dn, preferred_element_type=F32)
            for s in range(1, n_sh):
                acc = acc + lax.dot_general(a_ref[:, s * ks:(s + 1) * ks], b_ref[s], dn, preferred_element_type=F32)
        else:
            acc = lax.dot_general(a_ref[...], b_ref[...], dn, preferred_element_type=F32)
        if add is not None:
            acc = acc + refs[2][...]
        o_ref[...] = acc.astype(o_ref.dtype)

    if stacked:
        b_spec = pl.BlockSpec((n_sh, bn, ks), lambda i, j: (0, j, 0))
    else:
        b_spec = pl.BlockSpec((bn, K), lambda i, j: (j, 0))
    in_specs = [pl.BlockSpec((bm, K), lambda i, j: (i, 0)), b_spec]
    args = [a, b]
    if add is not None:
        rows = bm if add.shape[0] == M else 1
        in_specs.append(pl.BlockSpec((rows, bn), lambda i, j: (i if rows == bm else 0, j)))
        args.append(add)
    limit = _vmem_limit(_nbytes((bm, K), a.dtype), _nbytes((bn, K), b.dtype), _nbytes((bm, bn), F32))
    return _call(body, name=name, grid=(M // bm, N // bn), in_specs=in_specs,
                 out_specs=[pl.BlockSpec((bm, bn), lambda i, j: (i, j))],
                 out_shape=[jax.ShapeDtypeStruct((M, N), out_dtype)], args=args, vmem_limit=limit, carry=carry)[0]


def _mm_tn(a, b, *, bm, bn, out_dtype, name, n_stack=None, carry=None):
    T, M = a.shape
    N = b.shape[1]
    assert M % bm == 0 and N % bn == 0 and (n_stack is None or bn * n_stack == N)
    dn = (((0,), (0,)), ((), ()))

    def body(a_ref, b_ref, o_ref):
        acc = lax.dot_general(a_ref[...], b_ref[...], dn, preferred_element_type=F32)
        o_ref[...] = acc.astype(o_ref.dtype)

    if n_stack is None:
        out_spec = pl.BlockSpec((bm, bn), lambda i, j: (i, j))
        out_shape = jax.ShapeDtypeStruct((M, N), out_dtype)
    else:
        out_spec = pl.BlockSpec((None, bm, bn), lambda i, j: (j, i, 0))
        out_shape = jax.ShapeDtypeStruct((n_stack, M, bn), out_dtype)
    limit = _vmem_limit(_nbytes((T, bm), a.dtype), _nbytes((T, bn), b.dtype), _nbytes((bm, bn), F32))
    return _call(body, name=name, grid=(M // bm, N // bn),
                 in_specs=[pl.BlockSpec((T, bm), lambda i, j: (0, i)), pl.BlockSpec((T, bn), lambda i, j: (0, j))],
                 out_specs=[out_spec], out_shape=[out_shape], args=[a, b], vmem_limit=limit, carry=carry)[0]


ROW_BLK = 256


def _rstd(x):
    return lax.rsqrt(jnp.mean(x * x, axis=-1, keepdims=True) + EPS)


def _norm_bwd(xhat, rstd, dxhat):
    return rstd * (dxhat - xhat * jnp.mean(dxhat * xhat, axis=-1, keepdims=True))


def _row_spec(cols):
    return pl.BlockSpec((ROW_BLK, cols), lambda i: (i, 0))


def _vec_spec(cols):
    return pl.BlockSpec((1, cols), lambda i: (0, 0))


def _prologue(x, g, rel_bias, idx_all, *, name, carry=None):
    T, D = x.shape
    n_pat = idx_all.shape[0]
    heads = 2 * N_PAIRS
    steps = n_pat * heads
    rows = T // steps

    def body(rb_ref, x_ref, g_ref, idx_ref, n_ref, bias_ref):
        s = pl.program_id(0)
        head = jnp.where(s < heads, s, heads + s % heads)
        xv = x_ref[...]
        n_ref[...] = (xv * _rstd(xv) * g_ref[...]).astype(n_ref.dtype)
        idxv = idx_ref[...]
        acc = jnp.where(idxv < 0, NEG, 0.0).astype(F32)
        for b in range(NUM_BUCKETS):
            acc = acc + jnp.where(idxv == b, rb_ref[b, head], 0.0)
        bias_ref[0] = acc
        kap = lax.broadcasted_iota(jnp.int32, (1, 2 * QBLK), 1)
        bias_ref[1] = jnp.where(kap < QBLK, NEG, acc)

    return _call(
        body, name=name, grid=(steps,),
        in_specs=[pl.BlockSpec(memory_space=pltpu.SMEM), pl.BlockSpec((rows, D), lambda s: (s, 0)),
                  pl.BlockSpec((1, D), lambda s: (0, 0)),
                  pl.BlockSpec((None, QBLK, 2 * QBLK), lambda s: (s // heads, 0, 0))],
        out_specs=[pl.BlockSpec((rows, D), lambda s: (s, 0)),
                   pl.BlockSpec((None, 2, None, QBLK, 2 * QBLK), lambda s: (s // heads, 0, s % heads, 0, 0))],
        out_shape=[jax.ShapeDtypeStruct((T, D), BF16),
                   jax.ShapeDtypeStruct((n_pat, 2, heads, QBLK, 2 * QBLK), F32)],
        args=[rel_bias, x, g, idx_all], carry=carry)


def _mm_resid_norm(a, b, h, g_post, coef, g_next, *, bm, name, bias=None, carry=None):
    M, K = a.shape
    N = b.shape[1]

    def body(*refs):
        a_ref, b_ref, h_ref, gp_ref, gn_ref = refs[0:5]
        f_ref, hn_ref, n_ref = refs[-3:]
        for rows in _row_halves(bm):
            f = jnp.dot(a_ref[rows, :], b_ref[...], preferred_element_type=F32)
            if bias is not None:
                f = f + refs[5][...]
            f_ref[rows, :] = f
            hn = h_ref[rows, :] + coef * (f * _rstd(f) * gp_ref[...])
            hn_ref[rows, :] = hn
            n_ref[rows, :] = (hn * _rstd(hn) * gn_ref[...]).astype(n_ref.dtype)

    row = lambda w: pl.BlockSpec((bm, w), lambda i: (i, 0))
    vec = pl.BlockSpec((1, N), lambda i: (0, 0))
    in_specs = [row(K), pl.BlockSpec((K, N), lambda i: (0, 0), pipeline_mode=pl.Buffered(1)), row(N), vec, vec]
    args = [a, b, h, g_post, g_next]
    if bias is not None:
        in_specs.append(vec)
        args.append(bias)
    limit = _vmem_limit(_nbytes((bm, K), a.dtype), _nbytes((K, N), b.dtype) // 2, 4 * _nbytes((bm, N), F32))
    return _call(body, name=name, grid=(M // bm,), in_specs=in_specs, out_specs=[row(N), row(N), row(N)],
                 out_shape=[jax.ShapeDtypeStruct((M, N), F32), jax.ShapeDtypeStruct((M, N), F32),
                            jax.ShapeDtypeStruct((M, N), BF16)], args=args, vmem_limit=limit, carry=carry)


def _mm_nt_norms_bwd(a, b, dh_in, h, g_pre, f, g_post, coef, *, bm, name, add=None, b_is_kn=False, carry=None):
    M, K = a.shape
    stacked = b.ndim == 3
    N = b.shape[1] if (stacked or b_is_kn) else b.shape[0]
    n_sh = b.shape[0] if stacked else 1
    ks = b.shape[2] if stacked else K
    assert n_sh * ks == K
    dn_dims = (((1,), (0,)), ((), ())) if b_is_kn else (((1,), (1,)), ((), ()))
    with_f = f is not None
    n_in = 5 + (2 if with_f else 0) + (1 if add is not None else 0)

    def body(*refs):
        a_ref, b_ref, dhi_ref, h_ref, gpre_ref = refs[0:5]
        outs = refs[n_in:]

        @pl.when(pl.program_id(0) == 0)
        def _():
            for acc in (outs[2:] if with_f else outs[1:]):
                acc[...] = jnp.zeros_like(acc)

        for rows in _row_halves(bm):
            if stacked:
                dn = lax.dot_general(a_ref[rows, 0:ks], b_ref[0], dn_dims, preferred_element_type=F32)
                for s in range(1, n_sh):
                    dn = dn + lax.dot_general(a_ref[rows, s * ks:(s + 1) * ks], b_ref[s], dn_dims,
                                              preferred_element_type=F32)
            else:
                dn = lax.dot_general(a_ref[rows, :], b_ref[...], dn_dims, preferred_element_type=F32)
            if add is not None:
                dn = dn + refs[n_in - 1][rows, :]
            hv = h_ref[rows, :]
            r = _rstd(hv)
            hh = hv * r
            dh = dhi_ref[rows, :] + _norm_bwd(hh, r, dn * gpre_ref[...])
            outs[0][rows, :] = dh
            if not with_f:
                outs[1][...] += jnp.sum(dn * hh, axis=0, keepdims=True)
                continue
            f_ref, gpost_ref = refs[5], refs[6]
            df_ref, dgpre_ref, dgpost_ref, dsum_ref = outs[1:]
            dgpre_ref[...] += jnp.sum(dn * hh, axis=0, keepdims=True)
            fv = f_ref[rows, :]
            rf = _rstd(fv)
            fh = fv * rf
            dy = coef * dh
            dgpost_ref[...] += jnp.sum(dy * fh, axis=0, keepdims=True)
            df = _norm_bwd(fh, rf, dy * gpost_ref[...])
            dsum_ref[...] += jnp.sum(df, axis=0, keepdims=True)
            df_ref[rows, :] = df.astype(df_ref.dtype)

    row = lambda w: pl.BlockSpec((bm, w), lambda i: (i, 0))
    vec = pl.BlockSpec((1, N), lambda i: (0, 0))
    once = pl.Buffered(1)
    if stacked:
        b_spec = pl.BlockSpec((n_sh, N, ks), lambda i: (0, 0, 0), pipeline_mode=once)
    else:
        b_spec = pl.BlockSpec(b.shape, lambda i: (0, 0), pipeline_mode=once)
    in_specs = [row(K), b_spec, row(N), row(N), vec]
    args = [a, b, dh_in, h, g_pre]
    if with_f:
        in_specs += [row(N), vec]
        args += [f, g_post]
    if add is not None:
        in_specs.append(row(N))
        args.append(add)
    vec_shape = jax.ShapeDtypeStruct((1, N), F32)
    out_specs = [row(N)] + ([row(N), vec, vec, vec] if with_f else [vec])
    out_shape = [jax.ShapeDtypeStruct((M, N), F32)]
    out_shape += [jax.ShapeDtypeStruct((M, N), BF16), vec_shape, vec_shape, vec_shape] if with_f else [vec_shape]
    limit = _vmem_limit(_nbytes((bm, K), a.dtype), _nbytes((N, K), b.dtype) // 2, 6 * _nbytes((bm, N), F32))
    return _call(body, name=name, grid=(M // bm,), in_specs=in_specs, out_specs=out_specs, out_shape=out_shape,
                 args=args, vmem_limit=limit, semantics=("arbitrary",), carry=carry)


def _ffn_up(n, w_gu, *, bm, name, carry=None):
    M, K = n.shape
    n_sh, _, ns = w_gu.shape
    half = n_sh // 2

    def body(n_ref, wg_ref, wu_ref, g_ref, u_ref, a_ref):
        nv = n_ref[...]
        for cols in _col_chunks(ns):
            g = jnp.dot(nv, wg_ref[:, cols], preferred_element_type=F32)
            u = jnp.dot(nv, wu_ref[:, cols], preferred_element_type=F32)
            g_ref[:, cols] = g.astype(g_ref.dtype)
            u_ref[:, cols] = u.astype(u_ref.dtype)
            a_ref[:, cols] = (g * jax.nn.sigmoid(g) * u).astype(a_ref.dtype)

    out_spec = pl.BlockSpec((bm, ns), lambda i, j: (i, j))
    out = jax.ShapeDtypeStruct((M, half * ns), BF16)
    limit = _vmem_limit(_nbytes((bm, K), n.dtype), 2 * _nbytes((K, ns), w_gu.dtype), 3 * _nbytes((bm, ns), F32))
    return _call(body, name=name, grid=(M // bm, half),
                 in_specs=[pl.BlockSpec((bm, K), lambda i, j: (i, 0)),
                           pl.BlockSpec((None, K, ns), lambda i, j: (j, 0, 0)),
                           pl.BlockSpec((None, K, ns), lambda i, j: (j + half, 0, 0))],
                 out_specs=[out_spec, out_spec, out_spec], out_shape=[out, out, out], args=[n, w_gu, w_gu],
                 vmem_limit=limit, carry=carry)


def _ffn_down_bwd(df, w_down, g, u, *, bm, name, carry=None):
    M, D = df.shape
    F = w_down.shape[0]
    dn = (((1,), (1,)), ((), ()))

    def body(df_ref, w_ref, g_ref, u_ref, o_ref):
        dfv = df_ref[...]
        for cols in _col_chunks(F):
            da = lax.dot_general(dfv, w_ref[cols, :], dn, preferred_element_type=F32)
            gv = g_ref[:, cols].astype(F32)
            s = jax.nn.sigmoid(gv)
            o_ref[:, cols] = (da * u_ref[:, cols].astype(F32) * (s * (1.0 + gv * (1.0 - s)))).astype(o_ref.dtype)
            o_ref[:, F + cols.start:F + cols.stop] = (da * (gv * s)).astype(o_ref.dtype)

    row = lambda w: pl.BlockSpec((bm, w), lambda i: (i, 0))
    limit = _vmem_limit(_nbytes((F, D), w_down.dtype) // 2, 2 * _nbytes((bm, F), BF16), _nbytes((bm, 2 * F), BF16))
    return _call(body, name=name, grid=(M // bm,),
                 in_specs=[row(D), pl.BlockSpec((F, D), lambda i: (0, 0), pipeline_mode=pl.Buffered(1)), row(F), row(F)],
                 out_specs=[row(2 * F)], out_shape=[jax.ShapeDtypeStruct((M, 2 * F), BF16)],
                 args=[df, w_down, g, u], vmem_limit=limit, carry=carry)[0]


def _ple_head(n4, w_gate, p, w_ple, h3, g_post, target, *, bm, name):
    T, D = h3.shape
    kp = p.shape[1]

    def body(n_ref, wg_ref, p_ref, wp_ref, h_ref, g_ref, t_ref, dh_ref, de_ref, dgp_ref, loss_ref, dg_ref):
        @pl.when(pl.program_id(0) == 0)
        def _():
            loss_ref[...] = jnp.zeros_like(loss_ref)
            dg_ref[...] = jnp.zeros_like(dg_ref)

        gpost = g_ref[...]
        for rows in _row_halves(bm):
            gate = jax.nn.sigmoid(jnp.dot(n_ref[rows, :], wg_ref[...], preferred_element_type=F32))
            ev = jnp.dot(p_ref[rows, :], wp_ref[...], preferred_element_type=F32)
            ge = gate * ev
            r = _rstd(ge)
            gh = ge * r
            diff = h_ref[rows, :] + gh * gpost - t_ref[rows, :]
            loss_ref[...] += jnp.sum(diff * diff) * (0.5 / D)
            dh = diff * (1.0 / D)
            dh_ref[rows, :] = dh
            dg_ref[...] += jnp.sum(dh * gh, axis=0, keepdims=True)
            dge = _norm_bwd(gh, r, dh * gpost)
            de_ref[rows, :] = (dge * gate).astype(de_ref.dtype)
            dgp_ref[rows, :] = (dge * ev * gate * (1.0 - gate)).astype(dgp_ref.dtype)

    row = lambda w: pl.BlockSpec((bm, w), lambda i: (i, 0))
    whole = lambda a: pl.BlockSpec(a.shape, lambda i: (0, 0), pipeline_mode=pl.Buffered(1))
    limit = _vmem_limit(_nbytes((bm, D), BF16), _nbytes(w_gate.shape, w_gate.dtype) // 2, 6 * _nbytes((bm, D), F32))
    return pl.pallas_call(
        body, name=name, grid=(T // bm,),
        in_specs=[row(D), whole(w_gate), row(kp), whole(w_ple), row(D), _vec_spec(D), row(D)],
        out_specs=(row(D), row(D), row(D), _vec_spec(LANES), _vec_spec(D)),
        out_shape=(jax.ShapeDtypeStruct((T, D), F32), jax.ShapeDtypeStruct((T, D), BF16),
                   jax.ShapeDtypeStruct((T, D), BF16), jax.ShapeDtypeStruct((1, LANES), F32),
                   jax.ShapeDtypeStruct((1, D), F32)),
        compiler_params=pltpu.CompilerParams(dimension_semantics=("arbitrary",), vmem_limit_bytes=limit),
    )(n4, w_gate, p, w_ple, h3, g_post, target)


def _t5_index(max_dist, stride):
    rho = np.arange(QBLK)[:, None]
    kap = np.arange(2 * QBLK)[None, :]
    d = rho + QBLK - kap
    valid = (d >= 0) & (d <= max_dist)
    n = np.maximum(d, 0) * stride
    max_exact = NUM_BUCKETS // 2
    nf = np.maximum(n, 1).astype(np.float32)
    large = max_exact + (np.log(nf / np.float32(max_exact)) / np.float32(math.log(MAX_DISTANCE / max_exact))
                         * np.float32(NUM_BUCKETS - max_exact)).astype(np.int32)
    large = np.minimum(large, NUM_BUCKETS - 1)
    bucket = np.where(n < max_exact, n, large)
    return np.where(valid, bucket, -1).astype(np.int32)


def _bias_grad(d_bias, idx, *, name):
    def body(db_ref, idx_ref, o_ref):
        h = pl.program_id(0)

        @pl.when(h == 0)
        def _():
            o_ref[...] = jnp.zeros_like(o_ref)

        idxv = idx_ref[...]
        dv = db_ref[0]
        rows = lax.broadcasted_iota(jnp.int32, (NUM_BUCKETS, LANES), 0)
        lanes = lax.broadcasted_iota(jnp.int32, (NUM_BUCKETS, LANES), 1)
        acc = o_ref[...]
        for b in range(NUM_BUCKETS):
            s = jnp.sum(jnp.where(idxv == b, dv, 0.0))
            acc = acc + jnp.where((rows == b) & (lanes == h), s, 0.0)
        o_ref[...] = acc

    return pl.pallas_call(
        body, name=name, grid=(2 * N_PAIRS,),
        in_specs=[pl.BlockSpec((1, QBLK, 2 * QBLK), lambda h: (h, 0, 0)),
                  pl.BlockSpec((QBLK, 2 * QBLK), lambda h: (0, 0))],
        out_specs=pl.BlockSpec((NUM_BUCKETS, LANES), lambda h: (0, 0)),
        out_shape=jax.ShapeDtypeStruct((NUM_BUCKETS, LANES), F32),
        compiler_params=pltpu.CompilerParams(dimension_semantics=("arbitrary",)),
    )(d_bias, idx)


def _lane():
    return lax.broadcasted_iota(jnp.int32, (1, LANES), 1)


def _head_sel(h):
    return (_lane() < HEAD_DIM) if h == 0 else (_lane() >= HEAD_DIM)


def _stat_lo():
    return (_lane() % HEAD_DIM) < (HEAD_DIM // 2)


def _stack_heads(t, scale=None):
    if scale is not None:
        t = t * scale
    zero = jnp.zeros_like(t)
    return jnp.concatenate([jnp.where(_head_sel(0), t, zero), jnp.where(_head_sel(1), t, zero)], axis=0)


def _class_spec(L, r, cols, stride, pps):
    chunks = N_PAIRS // pps
    mode = pl.Buffered(1) if r * chunks == 1 else None
    return pl.BlockSpec((L, MIX_W // chunks), lambda j, c: (0, (cols + j * stride) * chunks + c), pipeline_mode=mode)


def _rows(b, n_blocks=1):
    return pl.ds(pl.multiple_of(b * QBLK, QBLK), n_blocks * QBLK)


def _key_window(ref, b, cols, first, kv_head=None):
    if kv_head is not None:
        cols = slice(0, LANES)
    if first:
        blk = ref[0:QBLK, cols]
        tile = jnp.concatenate([blk, blk], axis=0)
    else:
        tile = ref[_rows(b - 1, 2), cols]
    if kv_head is None:
        return tile
    wide = tile.astype(F32)
    keep = jnp.logical_xor(_lane() < HEAD_DIM, kv_head == 1)
    return jnp.where(keep, wide, pltpu.roll(wide, HEAD_DIM, 1)).astype(tile.dtype)


def _kv_spec(L, r, col, stride, pps, kv_shared):
    if not kv_shared:
        return _class_spec(L, r, col, stride, pps)
    mode = pl.Buffered(1) if pps == N_PAIRS else None
    return pl.BlockSpec((L, LANES), lambda j, c: (0, col), pipeline_mode=mode)


def _band_fwd(qa, ka, va, bias, *, r, q_col, k_col, v_col, stride, pattern, name, kv_shared=False, sink=None,
              carry=None):
    L = qa.shape[0]
    nb = L // QBLK
    dn = (((1,), (1,)), ((), ()))
    scale = HEAD_DIM ** -0.5

    pps = N_PAIRS

    def body(q_ref, k_ref, v_ref, bias_ref, *rest):
        sel0 = _head_sel(0)
        pair0 = pl.program_id(1) * pps

        def emit(rows, cols, o, lse):
            if sink is None:
                o_ref, lse_ref = rest
                o_ref[rows, cols] = o.astype(o_ref.dtype)
                lse_ref[rows, cols] = lse
                return
            sink_ref, out_ref, outb_ref, lse_ref = rest
            sk = sink_ref[:, cols]
            mx = jnp.maximum(lse, sk)
            w = jnp.exp(lse - mx)
            den = w + jnp.exp(sk - mx)
            out = o * (w / den)
            out_ref[rows, cols] = out
            outb_ref[rows, cols] = out.astype(outb_ref.dtype)
            lse_ref[rows, cols] = mx + jnp.log(den)

        def block(b, first):
            rows = _rows(b)
            for i in range(pps):
                cols = slice(i * LANES, (i + 1) * LANES)
                q2 = _stack_heads(q_ref[rows, cols], scale)
                kv_head = (pair0 + i) // 2 if kv_shared else None
                k = _key_window(k_ref, b, cols, first, kv_head)
                v = _key_window(v_ref, b, cols, first, kv_head)
                bias2 = bias_ref[1 if first else 0, pl.ds(2 * (pair0 + i), 2)].reshape(2 * QBLK, 2 * QBLK)
                s = lax.dot_general(q2, k, dn, preferred_element_type=F32) + bias2
                m = jnp.max(s, axis=1, keepdims=True)
                p = jnp.exp(s - m)
                l = jnp.sum(p, axis=1, keepdims=True)
                o = jnp.dot(p.astype(v.dtype), v, preferred_element_type=F32) * (1.0 / l)
                lse = m + jnp.log(l)
                emit(rows, cols, jnp.where(sel0, o[0:QBLK], o[QBLK:]), jnp.where(sel0, lse[0:QBLK], lse[QBLK:]))

        block(0, True)
        if nb > 1:
            pl.loop(1, nb)(lambda b: block(b, False))

    bias_spec = pl.BlockSpec((None, 2, 2 * N_PAIRS, QBLK, 2 * QBLK), lambda j, c: (pattern, 0, 0, 0, 0))
    out_spec = _class_spec(L, r, 0, 1, pps)
    blk_bytes = _nbytes((L, pps * LANES), F32)
    in_specs = [_class_spec(L, r, q_col, stride, pps), _kv_spec(L, r, k_col, stride, pps, kv_shared),
                _kv_spec(L, r, v_col, stride, pps, kv_shared), bias_spec]
    args = [qa, ka, va, bias]
    f32_out, bf16_out = jax.ShapeDtypeStruct((L, r * MIX_W), F32), jax.ShapeDtypeStruct((L, r * MIX_W), BF16)
    out_shape = [bf16_out, f32_out]
    if sink is not None:
        in_specs.append(pl.BlockSpec((1, MIX_W), lambda j, c: (0, 0)))
        args.append(sink)
        out_shape = [f32_out, bf16_out, f32_out]
    return _call(body, name=name, grid=(r, N_PAIRS // pps), in_specs=in_specs, out_specs=[out_spec] * len(out_shape),
                 out_shape=out_shape, args=args, vmem_limit=_vmem_limit(*([blk_bytes] * 4)), carry=carry)


def _class_rows_spec(rows, r, width):
    return pl.BlockSpec((rows // r, r * width), lambda i, *_: (i, 0))


def _token_scratch(rows, width):
    return pltpu.VMEM((width // LANES, rows, LANES), F32)


def _fill_tokens(scratch, value):
    for c in range(scratch.shape[0]):
        scratch[c] = value[:, c * LANES:(c + 1) * LANES]


def _read_tokens(scratch):
    return jnp.concatenate([scratch[c] for c in range(scratch.shape[0])], axis=1)


def _to_tokens(blk_ref, r, scratch):
    if r == 1:
        return blk_ref[...].astype(F32)
    nt, rows, _ = scratch.shape
    for j in range(r):
        for c in range(nt):
            lanes = slice((j * nt + c) * LANES, (j * nt + c + 1) * LANES)
            scratch.at[c][pl.ds(j, rows // r, stride=r), :] = blk_ref[:, lanes].astype(F32)
    return _read_tokens(scratch)


def _from_tokens(dst_ref, r, scratch):
    nt, rows, _ = scratch.shape
    if r == 1:
        dst_ref[...] = _read_tokens(scratch).astype(dst_ref.dtype)
        return
    for j in range(r):
        for c in range(nt):
            lanes = slice((j * nt + c) * LANES, (j * nt + c + 1) * LANES)
            dst_ref[:, lanes] = scratch.at[c][pl.ds(j, rows // r, stride=r), :].astype(dst_ref.dtype)


def _mm_nt_classes(a, b, bias, dils, *, bm, name):
    M, K = a.shape
    N = b.shape[0]
    dn = (((1,), (1,)), ((), ()))

    def body(a_ref, b_ref, bias_ref, *rest):
        outs, tile = rest[:-1], rest[-1]
        _fill_tokens(tile, lax.dot_general(a_ref[...], b_ref[...], dn, preferred_element_type=F32) + bias_ref[...])
        for out, r in zip(outs, dils):
            _from_tokens(out, r, tile)

    limit = _vmem_limit(_nbytes((bm, K), a.dtype), _nbytes((K, N), b.dtype) // 2, (1 + len(dils)) * _nbytes((bm, N), F32))
    return pl.pallas_call(
        body, name=name, grid=(M // bm,),
        in_specs=[pl.BlockSpec((bm, K), lambda i: (i, 0)),
                  pl.BlockSpec((N, K), lambda i: (0, 0), pipeline_mode=pl.Buffered(1)),
                  pl.BlockSpec((1, N), lambda i: (0, 0))],
        out_specs=tuple(_class_rows_spec(bm, r, N) for r in dils),
        out_shape=tuple(jax.ShapeDtypeStruct((M // r, r * N), BF16) for r in dils),
        scratch_shapes=[_token_scratch(bm, N)],
        compiler_params=pltpu.CompilerParams(dimension_semantics=("parallel",), vmem_limit_bytes=limit),
    )(a, b, bias)


def _merge(branches, dils, sink, *, name):
    W = MIX_W
    T = branches[0][0].shape[0] * dils[0]
    nbr = len(branches)

    def body(*refs):
        sink_ref = refs[2 * nbr]
        out_ref, outb_ref, lse_ref, scratch = refs[2 * nbr + 1:]
        sk = sink_ref[...]
        lses = [_to_tokens(refs[2 * t + 1], dils[t], scratch) for t in range(nbr)]
        mx = sk
        for lse in lses:
            mx = jnp.maximum(mx, lse)
        den = jnp.exp(sk - mx)
        num = jnp.zeros_like(mx)
        for t in range(nbr):
            w = jnp.exp(lses[t] - mx)
            den = den + w
            num = num + w * _to_tokens(refs[2 * t], dils[t], scratch)
        out = num / den
        out_ref[...] = out
        outb_ref[...] = out.astype(outb_ref.dtype)
        lse_ref[...] = mx + jnp.log(den)

    args = [a for br in branches for a in br] + [sink]
    in_specs = [_class_rows_spec(ROW_BLK, r, W) for r in dils for _ in range(2)] + [_vec_spec(W)]
    return pl.pallas_call(
        body, name=name, grid=(T // ROW_BLK,), in_specs=in_specs,
        out_specs=(_row_spec(W), _row_spec(W), _row_spec(W)),
        out_shape=(jax.ShapeDtypeStruct((T, W), F32), jax.ShapeDtypeStruct((T, W), BF16),
                   jax.ShapeDtypeStruct((T, W), F32)),
        scratch_shapes=[_token_scratch(ROW_BLK, W)],
        compiler_params=pltpu.CompilerParams(dimension_semantics=("parallel",)),
    )(*args)


def _attn_delta(dmix, outs, lses, sink, dils, *, name):
    T = dmix.shape[0]
    W = MIX_W
    nd = len(dils)

    def body(dmix_ref, oa_ref, ob_ref, la_ref, lb_ref, sink_ref, *rest):
        doa_ref, sta_ref, dsink_ref = rest[0:3]
        dob_refs, stb_refs = rest[3:3 + nd], rest[3 + nd:3 + 2 * nd]
        do_tile, st_tile = rest[3 + 2 * nd:]

        @pl.when(pl.program_id(0) == 0)
        def _():
            dsink_ref[...] = jnp.zeros_like(dsink_ref)

        sel0, lo = _head_sel(0), _stat_lo()
        for mixer, (o_ref, l_ref) in enumerate(((oa_ref, la_ref), (ob_ref, lb_ref))):
            for i in range(N_PAIRS):
                cols = slice(i * LANES, (i + 1) * LANES)
                do = dmix_ref[:, mixer * W + i * LANES:mixer * W + (i + 1) * LANES]
                prod = do * o_ref[:, cols]
                d0 = jnp.sum(jnp.where(sel0, prod, 0.0), axis=1, keepdims=True)
                d1 = jnp.sum(jnp.where(sel0, 0.0, prod), axis=1, keepdims=True)
                delta = jnp.where(sel0, d0, d1)
                lse = l_ref[:, cols]
                stat = jnp.where(lo, lse, delta)
                if mixer == 0:
                    sta_ref[:, cols] = stat
                    doa_ref[:, cols] = do.astype(doa_ref.dtype)
                    dsink_ref[:, cols] += -jnp.sum(jnp.exp(sink_ref[:, cols] - lse) * delta, axis=0, keepdims=True)
                else:
                    st_tile[i] = stat
                    do_tile[i] = do
        for t, r in enumerate(dils):
            _from_tokens(dob_refs[t], r, do_tile)
            _from_tokens(stb_refs[t], r, st_tile)

    class_specs = [_class_rows_spec(ROW_BLK, r, W) for r in dils]
    out_shape = [jax.ShapeDtypeStruct((T, W), BF16), jax.ShapeDtypeStruct((T, W), F32), jax.ShapeDtypeStruct((1, W), F32)]
    out_shape += [jax.ShapeDtypeStruct((T // r, r * W), BF16) for r in dils]
    out_shape += [jax.ShapeDtypeStruct((T // r, r * W), F32) for r in dils]
    res = pl.pallas_call(
        body, name=name, grid=(T // ROW_BLK,),
        in_specs=[_row_spec(2 * W), _row_spec(W), _row_spec(W), _row_spec(W), _row_spec(W), _vec_spec(W)],
        out_specs=tuple([_row_spec(W), _row_spec(W), _vec_spec(W)] + class_specs + class_specs),
        out_shape=tuple(out_shape),
        scratch_shapes=[_token_scratch(ROW_BLK, W), _token_scratch(ROW_BLK, W)],
        compiler_params=pltpu.CompilerParams(dimension_semantics=("arbitrary",)),
    )(dmix, outs[0], outs[1], lses[0], lses[1], sink)
    return res[0], res[1], res[2], res[3:3 + nd], res[3 + nd:]


def _band_bwd(qa, ka, va, d_out, stat, bias, *, r, q_col, k_col, v_col, stride, pattern, name, kv_shared=False,
              carry=None):
    L = qa.shape[0]
    nb = L // QBLK
    dn_nt = (((1,), (1,)), ((), ()))
    dn_tn = (((0,), (0,)), ((), ()))
    scale = HEAD_DIM ** -0.5

    pps = N_PAIRS if r > 1 else N_PAIRS // 2

    def body(q_ref, k_ref, v_ref, do_ref, st_ref, bias_ref, dq_ref, dk_ref, dv_ref, db_ref, dk_carry, dv_carry):
        @pl.when((pl.program_id(0) == 0) & (pl.program_id(1) == 0))
        def _():
            db_ref[...] = jnp.zeros_like(db_ref)

        lo, sel0 = _stat_lo(), _head_sel(0)
        pair0 = pl.program_id(1) * pps

        def block(b, first):
            rows = _rows(b)
            for i in range(pps):
                heads = pl.ds(2 * (pair0 + i), 2)
                cols = slice(i * LANES, (i + 1) * LANES)
                q2 = _stack_heads(q_ref[rows, cols], scale)
                kv_head = (pair0 + i) // 2 if kv_shared else None
                k = _key_window(k_ref, b, cols, first, kv_head)
                v = _key_window(v_ref, b, cols, first, kv_head)
                do2 = _stack_heads(do_ref[rows, cols])
                st = st_ref[rows, cols]
                lse2 = jnp.concatenate(
                    [jnp.max(jnp.where(_head_sel(h) & lo, st, -jnp.inf), axis=1, keepdims=True) for h in range(2)], axis=0)
                delta2 = jnp.concatenate(
                    [jnp.sum(jnp.where(_head_sel(h) & ~lo, st, 0.0), axis=1, keepdims=True) for h in range(2)],
                    axis=0) * (2.0 / HEAD_DIM)
                bias2 = bias_ref[1 if first else 0, heads].reshape(2 * QBLK, 2 * QBLK)
                s = lax.dot_general(q2, k, dn_nt, preferred_element_type=F32) + bias2
                p = jnp.exp(s - lse2)
                dp = lax.dot_general(do2, v, dn_nt, preferred_element_type=F32)
                ds = p * (dp - delta2)
                db_ref[heads] += ds.reshape(2, QBLK, 2 * QBLK)
                dsb = ds.astype(k.dtype)
                dq2 = jnp.dot(dsb, k, preferred_element_type=F32)
                dq_ref[rows, cols] = (jnp.where(sel0, dq2[0:QBLK], dq2[QBLK:]) * scale).astype(dq_ref.dtype)
                dk_acc = lax.dot_general(dsb, q2, dn_tn, preferred_element_type=F32)
                dv_acc = lax.dot_general(p.astype(k.dtype), do2, dn_tn, preferred_element_type=F32)
                if not first:
                    prev = _rows(b - 1)
                    dk_ref[prev, cols] = (dk_carry[:, cols] + dk_acc[0:QBLK]).astype(dk_ref.dtype)
                    dv_ref[prev, cols] = (dv_carry[:, cols] + dv_acc[0:QBLK]).astype(dv_ref.dtype)
                dk_carry[:, cols] = dk_acc[QBLK:2 * QBLK]
                dv_carry[:, cols] = dv_acc[QBLK:2 * QBLK]

        block(0, True)
        if nb > 1:
            pl.loop(1, nb)(lambda b: block(b, False))

        last = _rows(nb - 1)
        dk_ref[last, :] = dk_carry[...].astype(dk_ref.dtype)
        dv_ref[last, :] = dv_carry[...].astype(dv_ref.dtype)

    tok_spec = _class_spec(L, r, 0, 1, pps)
    bias_spec = pl.BlockSpec((None, 2, 2 * N_PAIRS, QBLK, 2 * QBLK), lambda j, c: (pattern, 0, 0, 0, 0))
    dbias_spec = pl.BlockSpec((2 * N_PAIRS, QBLK, 2 * QBLK), lambda j, c: (0, 0, 0))
    grad = jax.ShapeDtypeStruct((L, r * MIX_W), BF16)
    blk_bytes = _nbytes((L, pps * LANES), BF16)
    carry_shape = pltpu.VMEM((QBLK, pps * LANES), F32)
    return _call(
        body, name=name, grid=(r, N_PAIRS // pps),
        in_specs=[_class_spec(L, r, q_col, stride, pps), _kv_spec(L, r, k_col, stride, pps, kv_shared),
                  _kv_spec(L, r, v_col, stride, pps, kv_shared), tok_spec, tok_spec, bias_spec],
        out_specs=[tok_spec, tok_spec, tok_spec, dbias_spec],
        out_shape=[grad, grad, grad, jax.ShapeDtypeStruct((2 * N_PAIRS, QBLK, 2 * QBLK), F32)],
        args=[qa, ka, va, d_out, stat, bias], scratch=[carry_shape, carry_shape],
        vmem_limit=_vmem_limit(*([blk_bytes] * 9)), semantics=("arbitrary", "arbitrary"), carry=carry)


def _dz_assemble(dq_a, dk_a, dv_a, b_grads, dils, *, name):
    T = dq_a.shape[0]
    W = MIX_W

    def group_sum(x):
        u0 = x[:, 0:LANES] + x[:, LANES:2 * LANES]
        u1 = x[:, 2 * LANES:3 * LANES] + x[:, 3 * LANES:4 * LANES]
        s0 = u0 + pltpu.roll(u0, HEAD_DIM, 1)
        s1 = u1 + pltpu.roll(u1, HEAD_DIM, 1)
        return jnp.where(_head_sel(0), s0, s1)

    def body(*refs):
        dqa_ref, dka_ref, dva_ref = refs[0:3]
        b_refs = refs[3:12]
        dza_ref, dzb_ref, sa_ref, sb_ref, scratch = refs[12:]

        @pl.when(pl.program_id(0) == 0)
        def _():
            sa_ref[...] = jnp.zeros_like(sa_ref)
            sb_ref[...] = jnp.zeros_like(sb_ref)

        def put(dst, acc, col, part):
            w = part.shape[1]
            dst[:, col:col + w] = part.astype(dst.dtype)
            acc[:, col:col + w] += jnp.sum(part, axis=0, keepdims=True)

        put(dza_ref, sa_ref, 0, dqa_ref[...].astype(F32))
        put(dza_ref, sa_ref, W, group_sum(dka_ref[...].astype(F32)))
        put(dza_ref, sa_ref, W + LANES, group_sum(dva_ref[...].astype(F32)))
        for t in range(3):
            part = _to_tokens(b_refs[t], dils[0], scratch)
            for pat in range(1, len(dils)):
                part = part + _to_tokens(b_refs[3 * pat + t], dils[pat], scratch)
            put(dzb_ref, sb_ref, t * W, part)

    args = [dq_a, dk_a, dv_a] + [g[t] for g in b_grads for t in range(3)]
    return pl.pallas_call(
        body, name=name, grid=(T // ROW_BLK,),
        in_specs=[_row_spec(W)] * 3 + [_class_rows_spec(ROW_BLK, r, W) for r in dils for _ in range(3)],
        out_specs=(_row_spec(ZA_W), _row_spec(ZB_W), _vec_spec(ZA_W), _vec_spec(ZB_W)),
        out_shape=(jax.ShapeDtypeStruct((T, ZA_W), BF16), jax.ShapeDtypeStruct((T, ZB_W), BF16),
                   jax.ShapeDtypeStruct((1, ZA_W), F32), jax.ShapeDtypeStruct((1, ZB_W), F32)),
        scratch_shapes=[_token_scratch(ROW_BLK, W)],
        compiler_params=pltpu.CompilerParams(dimension_semantics=("arbitrary",)),
    )(*args)


def _place():
    x, y, c = lax.axis_index("x"), lax.axis_index("y"), lax.axis_index("c")
    chips = [(1 - x, y), (x, 1 - y), (1 - x, 1 - y)]
    return x, y, c, chips


def _cast_place(shards, k_idx, *, name):
    n, steps = len(shards), 4

    def body(k_ref, *refs):
        for s_ref, o_ref in zip(refs[:n], refs[n:]):
            o_ref[...] = s_ref[...].astype(o_ref.dtype)

    blocks = [(a.shape[0] // steps, a.shape[1]) for a in shards]
    grid_spec = pltpu.PrefetchScalarGridSpec(
        num_scalar_prefetch=1, grid=(steps,),
        in_specs=[pl.BlockSpec(blk, lambda t, k_ref: (t, 0)) for blk in blocks],
        out_specs=tuple(pl.BlockSpec(blk, lambda t, k_ref: (k_ref[0] * steps + t, 0)) for blk in blocks))
    return pl.pallas_call(
        body, name=name, grid_spec=grid_spec,
        out_shape=tuple(jax.ShapeDtypeStruct((N_CHIPS * a.shape[0], a.shape[1]), BF16) for a in shards),
        compiler_params=pltpu.CompilerParams(dimension_semantics=("parallel",),
                                             vmem_limit_bytes=_vmem_limit(*[_nbytes(b, F32) for b in blocks])),
    )(k_idx, *shards)


def _gather_carry(fulls, jobs):
    n = len(jobs)

    def piece(ref, chip_idx, half, part):
        rs = ref.shape[0] // N_CHIPS
        rh = rs // 2
        rows = rh // part[1]
        return ref.at[pl.ds(chip_idx * rs + half * rh + part[0] * rows, rows)]

    def copy(sems, sem, src_ref, dst_ref, to):
        return pltpu.make_async_remote_copy(src_ref=src_ref, dst_ref=dst_ref, send_sem=sems[0].at[sem],
                                            recv_sem=sems[1].at[sem], device_id=to, device_id_type=MESH)

    def to_chips(ins, outs, sems, j, arrival, hops=("both", "chips")):
        i, part, hop = jobs[j]
        x, y, c, chips = _place()
        res = []
        for t, chip in enumerate(chips if hop in hops else ()):
            theirs = piece(outs[i], 2 * chip[0] + chip[1], c, part)
            src, dst = (theirs, theirs) if arrival else (piece(ins[i], 2 * x + y, c, part), piece(outs[i], 2 * x + y, c, part))
            res.append(copy(sems, 3 * j + t, src, dst, (*chip, c)))
        return res

    def to_sibling(outs, sems, j, arrival, hops=("both", "sibling")):
        i, part, hop = jobs[j]
        x, y, c, chips = _place()
        res = []
        for t, chip in enumerate(chips if hop in hops else ()):
            region = piece(outs[i], 2 * chip[0] + chip[1], 1 - c if arrival else c, part)
            res.append(copy(sems, 3 * n + 3 * j + t, region, region, (x, y, 1 - c)))
        return res

    def start(ins, outs, sems):
        for j in range(n):
            for send in to_chips(ins, outs, sems, j, False) + to_sibling(outs, sems, j, False, ("sibling",)):
                send.start()

    def before_last(ins, outs, sems):
        for j in range(n):
            for arrival, send in zip(to_chips(ins, outs, sems, j, True, ("both",)),
                                     to_sibling(outs, sems, j, False, ("both",))):
                arrival.wait_recv()
                send.start()

    def finish(ins, outs, sems):
        for j in range(n):
            for arrival in to_chips(ins, outs, sems, j, True, ("chips",)) + to_sibling(outs, sems, j, True):
                arrival.wait_recv()
        for j in range(n):
            for send in to_chips(ins, outs, sems, j, False) + to_sibling(outs, sems, j, False):
                send.wait_send()

    return _Carry(fulls, [jax.ShapeDtypeStruct(a.shape, a.dtype) for a in fulls], {i: i for i in range(len(fulls))},
                  [pltpu.SemaphoreType.DMA((6 * n,)), pltpu.SemaphoreType.DMA((6 * n,))], start, finish, before_last)


def _pair_swap_carry(grads):
    n = len(grads)

    def copy(ins, outs, sems, i):
        x, y, c, _ = _place()
        return pltpu.make_async_remote_copy(src_ref=ins[i].at[:, 1 - c], dst_ref=outs[i], send_sem=sems[0].at[i],
                                            recv_sem=sems[1].at[i], device_id=(x, y, 1 - c), device_id_type=MESH)

    def start(ins, outs, sems):
        for i in range(n):
            copy(ins, outs, sems, i).start()

    def finish(ins, outs, sems):
        for i in range(n):
            copy(ins, outs, sems, i).wait()

    return _Carry(grads, [jax.ShapeDtypeStruct((a.shape[0], a.shape[2], a.shape[3]), a.dtype) for a in grads], {},
                  [pltpu.SemaphoreType.DMA((n,)), pltpu.SemaphoreType.DMA((n,))], start, finish)


def _pair_sum(g4, got, c_idx, *, name):
    _, _, rh, cs = g4.shape
    rb = _row_block(rh, 512, 16)

    def body(c_ref, own_ref, got_ref, o_ref):
        o_ref[...] = (own_ref[...].astype(F32) + got_ref[...].astype(F32)).astype(o_ref.dtype)

    grid_spec = pltpu.PrefetchScalarGridSpec(
        num_scalar_prefetch=1, grid=(N_CHIPS, rh // rb),
        in_specs=[pl.BlockSpec((None, None, rb, cs), lambda k, t, c_ref: (k, c_ref[0], t, 0)),
                  pl.BlockSpec((None, rb, cs), lambda k, t, c_ref: (k, t, 0))],
        out_specs=pl.BlockSpec((None, rb, cs), lambda k, t, c_ref: (k, t, 0)))
    return pl.pallas_call(
        body, name=name, grid_spec=grid_spec, out_shape=jax.ShapeDtypeStruct((N_CHIPS, rh, cs), BF16),
        compiler_params=pltpu.CompilerParams(dimension_semantics=("parallel", "parallel")),
    )(c_idx, g4, got)


def _chip_scatter_carry(sums):
    n = len(sums)

    def copies(ins, outs, sems):
        x, y, c, chips = _place()
        return [pltpu.make_async_remote_copy(src_ref=ins[i].at[2 * chip[0] + chip[1]], dst_ref=outs[i].at[j],
                                             send_sem=sems[0].at[3 * i + j], recv_sem=sems[1].at[3 * i + j],
                                             device_id=(*chip, c), device_id_type=MESH)
                for i in range(n) for j, chip in enumerate(chips)]

    def start(ins, outs, sems):
        for cp in copies(ins, outs, sems):
            cp.start()

    def finish(ins, outs, sems):
        for cp in copies(ins, outs, sems):
            cp.wait()

    return _Carry(sums, [jax.ShapeDtypeStruct((3,) + a.shape[1:], a.dtype) for a in sums], {},
                  [pltpu.SemaphoreType.DMA((3 * n,)), pltpu.SemaphoreType.DMA((3 * n,))], start, finish)


def _chip_sum(sums, gots, kc_idx, *, name):
    n, steps = len(sums), 2

    def body(kc_ref, *refs):
        for own_ref, got_ref, o_ref in zip(refs[:n], refs[n:2 * n], refs[2 * n:]):
            acc = own_ref[...].astype(F32)
            for j in range(3):
                acc = acc + got_ref[j].astype(F32)
            o_ref[...] = acc

    blocks = [(a.shape[1] // steps, a.shape[2]) for a in sums]
    grid_spec = pltpu.PrefetchScalarGridSpec(
        num_scalar_prefetch=1, grid=(steps,),
        in_specs=[pl.BlockSpec((None,) + blk, lambda t, kc: (kc[0], t, 0)) for blk in blocks]
        + [pl.BlockSpec((3,) + blk, lambda t, kc: (0, t, 0)) for blk in blocks],
        out_specs=tuple(pl.BlockSpec(blk, lambda t, kc: (kc[1] * steps + t, 0)) for blk in blocks))
    return pl.pallas_call(
        body, name=name, grid_spec=grid_spec,
        out_shape=tuple(jax.ShapeDtypeStruct((2 * a.shape[1], a.shape[2]), F32) for a in sums),
        compiler_params=pltpu.CompilerParams(dimension_semantics=("parallel",),
                                             vmem_limit_bytes=_vmem_limit(*[3 * _nbytes(b, F32) for b in blocks])),
    )(kc_idx, *sums, *gots)


def _half_swap_carry(shards):
    n = len(shards)

    def copy(ins, outs, sems, i, to_my_half):
        x, y, c, _ = _place()
        rh = ins[i].shape[0] // 2
        half = c if to_my_half else 1 - c
        return pltpu.make_async_remote_copy(
            src_ref=ins[i].at[pl.ds(c * rh, rh)], dst_ref=outs[i].at[pl.ds(half * rh, rh)],
            send_sem=sems[0].at[i], recv_sem=sems[1].at[i], device_id=(x, y, 1 - c), device_id_type=MESH)

    def start(ins, outs, sems):
        for i in range(n):
            copy(ins, outs, sems, i, True).start()

    def finish(ins, outs, sems):
        for i in range(n):
            copy(ins, outs, sems, i, False).wait_recv()
        for i in range(n):
            copy(ins, outs, sems, i, True).wait_send()

    return _Carry(shards, [jax.ShapeDtypeStruct(a.shape, a.dtype) for a in shards], {i: i for i in range(n)},
                  [pltpu.SemaphoreType.DMA((n,)), pltpu.SemaphoreType.DMA((n,))], start, finish)


SMALL_ROW = 1024


def _small_layout(shapes):
    starts, row = [], 0
    for r, c in shapes:
        starts.append(row)
        row += -(-c // SMALL_ROW) if r == 1 else r
    return starts, -(-row // SUBLANES) * SUBLANES


def _small_pieces(shape, start):
    r, c = shape
    if r == 1:
        return [(start + q, min(SMALL_ROW, c - q * SMALL_ROW), q * SMALL_ROW) for q in range(-(-c // SMALL_ROW))]
    return None


def _whole_spec(shape):
    return pl.BlockSpec(tuple(shape), lambda i: (0,) * len(shape))


def _small_all_reduce(parts, *, name, carry=None):
    shapes = [a.shape for a in parts]
    starts, rows = _small_layout(shapes)
    n = len(parts)
    n_tables = sum(1 for r, _ in shapes if r > 1)
    assert all(r > 1 and c <= LANES for r, c in shapes[:n_tables]) and all(r == 1 for r, _ in shapes[n_tables:])
    table_rows = starts[n_tables]
    assert table_rows % SUBLANES == 0

    def regions(slot):
        return slot.at[pl.ds(0, table_rows), pl.ds(0, LANES)], slot.at[pl.ds(table_rows, rows - table_rows)]

    def peer(d):
        x, y, c, _ = _place()
        return 1 - x if d & 4 else x, 1 - y if d & 2 else y, 1 - c if d & 1 else c

    def slot_of(d):
        px, py, pc = peer(d)
        return 4 * px + 2 * py + pc

    def copy(refs, d, part, dst_slot):
        buf, send_sems, recv_sems = refs[2 * n:]
        sem = 2 * (d - 1) + part
        return pltpu.make_async_remote_copy(
            src_ref=regions(buf.at[slot_of(0)])[part], dst_ref=regions(buf.at[dst_slot])[part],
            send_sem=send_sems.at[sem], recv_sem=recv_sems.at[sem], device_id=peer(d), device_id_type=MESH)

    def send(*refs):
        ins, mine = refs[0:n], refs[2 * n].at[slot_of(0)]
        mine[...] = jnp.zeros((rows, SMALL_ROW), F32)
        for ref, shape, start in zip(ins, shapes, starts):
            pieces = _small_pieces(shape, start)
            if pieces is None:
                mine[start:start + shape[0], 0:shape[1]] = ref[...]
            else:
                for row, width, col in pieces:
                    mine[row:row + 1, 0:width] = ref[:, col:col + width]
        for d in range(1, 8):
            for part in range(2):
                copy(refs, d, part, slot_of(0)).start()

    def receive(*refs):
        outs, buf = refs[n:2 * n], refs[2 * n]
        mine = buf.at[slot_of(0)]
        for d in range(1, 8):
            for part in range(2):
                copy(refs, d, part, slot_of(d)).wait_recv()
        for d in range(1, 8):
            for part in range(2):
                copy(refs, d, part, slot_of(0)).wait_send()
        tables, vectors = regions(buf.at[0])
        acc_t, acc_v = tables[...], vectors[...]
        for s in range(1, 8):
            tables, vectors = regions(buf.at[s])
            acc_t, acc_v = acc_t + tables[...], acc_v + vectors[...]
        tables, vectors = regions(mine)
        tables[...] = acc_t
        vectors[...] = acc_v
        for ref, shape, start in zip(outs, shapes, starts):
            pieces = _small_pieces(shape, start)
            if pieces is None:
                ref[...] = mine[start:start + shape[0], 0:shape[1]]
            else:
                for row, width, col in pieces:
                    ref[:, col:col + width] = mine[row:row + 1, 0:width]

    def body(*refs):
        send(*refs)
        receive(*refs)

    whole = [_whole_spec(s) for s in shapes]
    return _call(body if carry is None else receive, before_carry=None if carry is None else send,
                 name=name, grid=(1,), in_specs=whole, out_specs=whole,
                 out_shape=[jax.ShapeDtypeStruct(s, F32) for s in shapes], args=parts,
                 scratch=[pltpu.VMEM((8, rows, SMALL_ROW), F32), pltpu.SemaphoreType.DMA((14,)),
                          pltpu.SemaphoreType.DMA((14,))], carry=carry)


def _adamw_small(ws, gs, ms, vs, *, name):
    n = len(ws)
    c1 = 1.0 - ADAM_B1 ** ADAM_STEP
    c2 = 1.0 - ADAM_B2 ** ADAM_STEP

    def body(*refs):
        for k in range(n):
            w_ref, g_ref, m_ref, v_ref = (refs[t * n + k] for t in range(4))
            go_ref, d_ref, mo_ref, vo_ref = (refs[(4 + t) * n + k] for t in range(4))
            gv = g_ref[...]
            go_ref[...] = gv
            mn = ADAM_B1 * m_ref[...] + (1.0 - ADAM_B1) * gv
            vn = ADAM_B2 * v_ref[...] + (1.0 - ADAM_B2) * (gv * gv)
            d_ref[...] = -ADAM_LR * ((mn / c1) / (jnp.sqrt(vn / c2) + ADAM_EPS) + ADAM_WD * w_ref[...])
            mo_ref[...] = mn
            vo_ref[...] = vn

    whole = [_whole_spec(a.shape) for a in ws]
    shapes = tuple(jax.ShapeDtypeStruct(a.shape, F32) for a in ws)
    res = pl.pallas_call(
        body, name=name, grid=(1,), in_specs=whole * 4, out_specs=tuple(whole * 4), out_shape=shapes * 4,
    )(*ws, *gs, *ms, *vs)
    return res[0:n], res[n:2 * n], res[2 * n:3 * n], res[3 * n:4 * n]


def _adamw(ws, gs, ms, vs, *, name):
    n, steps = len(ws), 8
    c1 = 1.0 - ADAM_B1 ** ADAM_STEP
    c2 = 1.0 - ADAM_B2 ** ADAM_STEP

    def body(*refs):
        for k in range(n):
            w_ref, g_ref, m_ref, v_ref = (refs[t * n + k] for t in range(4))
            go_ref, d_ref, mo_ref, vo_ref = (refs[(4 + t) * n + k] for t in range(4))
            gv = g_ref[...]
            go_ref[...] = gv
            mn = ADAM_B1 * m_ref[...] + (1.0 - ADAM_B1) * gv
            vn = ADAM_B2 * v_ref[...] + (1.0 - ADAM_B2) * (gv * gv)
            d_ref[...] = -ADAM_LR * ((mn / c1) / (jnp.sqrt(vn / c2) + ADAM_EPS) + ADAM_WD * w_ref[...])
            mo_ref[...] = mn
            vo_ref[...] = vn

    blocks = [(a.shape[0] // steps, a.shape[1]) for a in ws]
    specs = [pl.BlockSpec(blk, lambda i: (i, 0)) for blk in blocks]
    shapes = tuple(jax.ShapeDtypeStruct(a.shape, F32) for a in ws)
    res = pl.pallas_call(
        body, name=name, grid=(steps,), in_specs=specs * 4, out_specs=tuple(specs * 4), out_shape=shapes * 4,
        compiler_params=pltpu.CompilerParams(dimension_semantics=("parallel",),
                                             vmem_limit_bytes=_vmem_limit(*[8 * _nbytes(b, F32) for b in blocks])),
    )(*ws, *gs, *ms, *vs)
    return res[0:n], res[n:2 * n], res[2 * n:3 * n], res[3 * n:4 * n]


def _local_step(x, p, target, small, sched):
    T = x.shape[0]
    W = MIX_W
    rel_bias = small["rel_bias"]
    b_in_a, b_in_b = small["b_in"][:, 0:ZA_W], small["b_in"][:, ZA_W:]

    idx_np = [_t5_index(A_WINDOW - 1, 1)] + [_t5_index(window // dil, dil) for window, dil in B_PATTERNS]
    idx_all = jnp.asarray(np.stack(idx_np))
    n1, bias_all = sched.run(_prologue, x, small["ffn1_pre_g"], rel_bias, idx_all, name="prologue")
    w_gu1 = sched.weight("ffn1_w_gu")
    *gu1, a1 = sched.run(_ffn_up, n1, w_gu1, bm=512, name="ffn1_gu")
    w_d1 = sched.weight("ffn1_w_down")
    f1, h1, n2 = sched.run(_mm_resid_norm, a1, w_d1, x, small["ffn1_post_g"], 0.5, small["attn_pre_g"], bm=512,
                           name="ffn1_down")
    win_a, win_b = sched.weight("w_in")
    za = _mm_nt(n2, win_a, bm=1024, bn=ZA_W, out_dtype=BF16, name="in_proj_a", add=b_in_a)
    dils = tuple(dil for _, dil in B_PATTERNS)
    zb_by_dil = _mm_nt_classes(n2, win_b, b_in_b, dils, bm=512, name="in_proj_b")

    a_args = dict(r=1, q_col=0, k_col=W // LANES, v_col=W // LANES + 1, stride=0, pattern=0, kv_shared=True)
    sink_row = jnp.repeat(small["sinks"], HEAD_DIM, axis=1)
    out_a, out_a_bf, lse_a = sched.run(_band_fwd, za, za, za, bias_all, name="band_a_fwd", sink=sink_row, **a_args)

    no_sink = jnp.full((1, W), NEG, F32)
    b_ctx, branches = [], []
    for t, dil in enumerate(dils):
        zb_r = zb_by_dil[t]
        args = dict(r=dil, q_col=0, k_col=1, v_col=2, stride=ZB_W // W, pattern=1 + t)
        branches.append(sched.run(_band_fwd, zb_r, zb_r, zb_r, bias_all, name=f"band_b{t}_fwd", **args))
        b_ctx.append((jnp.asarray(idx_np[1 + t]), zb_r, args))
    out_b, out_b_bf, lse_b = _merge(branches, dils, no_sink, name="merge_b")

    mix = jnp.concatenate([out_a_bf, out_b_bf], axis=1)
    w_out = sched.weight("w_out")
    att, h2, n3 = _mm_resid_norm(mix, w_out, h1, small["attn_post_g"], 1.0, small["ffn2_pre_g"], bm=512,
                                 name="out_proj", bias=small["b_out"])
    w_gu2, w_d2 = sched.weight("ffn2_w_gu"), sched.weight("ffn2_w_down")
    *gu2, a2 = _ffn_up(n3, w_gu2, bm=512, name="ffn2_gu")
    f2, h3, n4 = _mm_resid_norm(a2, w_d2, h2, small["ffn2_post_g"], 0.5, small["ple_pre_g"], bm=512,
                                name="ffn2_down")
    w_gate = sched.weight("w_ple_gate")
    p_bf = p.astype(BF16)
    dh4, de, dgp, loss, d_ple_post = _ple_head(n4, w_gate, p_bf, sched.weight("w_ple_proj"), h3, small["ple_post_g"],
                                               target, bm=512, name="ple_head")

    gs = {"ple_post_g": d_ple_post}
    sched.grad("w_ple_proj", _mm_tn(p_bf, de, bm=256, bn=1024, out_dtype=BF16, name="d_w_ple"))
    sched.grad("w_ple_gate", _mm_tn(n4, dgp, bm=512, bn=1024, out_dtype=BF16, name="d_w_gate"))
    dh3, df2, gs["ple_pre_g"], gs["ffn2_post_g"], _ = sched.run(
        _mm_nt_norms_bwd, dgp, w_gate, dh4, h3, small["ple_pre_g"], f2, small["ffn2_post_g"], 0.5, bm=512, name="d_n4")

    def ffn_bwd(df, a, gu, n, w_down, w_gu, tag, *norm_args):
        dgu = sched.run(_ffn_down_bwd, df, w_down, gu[0], gu[1], bm=512, name=f"ffn{tag}_d_a")
        sched.grad(f"ffn{tag}_w_gu", _mm_tn(n, dgu, bm=512, bn=1408, out_dtype=BF16, name=f"ffn{tag}_d_w_gu",
                                            n_stack=N_CHIPS))
        sched.grad(f"ffn{tag}_w_down", sched.run(_mm_tn, a, df, bm=256, bn=1024, out_dtype=BF16,
                                                name=f"ffn{tag}_d_w_down"))
        return sched.run(_mm_nt_norms_bwd, dgu, w_gu, *norm_args, bm=512, name=f"ffn{tag}_d_n")

    dh2, datt, gs["ffn2_pre_g"], gs["attn_post_g"], gs["b_out"] = ffn_bwd(
        df2, a2, gu2, n3, w_d2, w_gu2, "2", dh3, h2, small["ffn2_pre_g"], att, small["attn_post_g"], 1.0)
    sched.grad("w_out", _mm_tn(mix, datt, bm=512, bn=1024, out_dtype=BF16, name="d_w_out"))
    dmix = sched.run(_mm_nt, datt, w_out, bm=1024, bn=1024, out_dtype=F32, name="d_mix")

    do_a, stat_a, dsink, do_b, stat_b = _attn_delta(dmix, (out_a, out_b), (lse_a, lse_b), sink_row, dils,
                                                    name="attn_delta")
    gs["sinks"] = dsink[:, ::HEAD_DIM]
    dq_a, dk_a, dv_a, dbias_a = sched.run(_band_bwd, za, za, za, do_a, stat_a, bias_all, name="band_a_bwd", **a_args)
    d_rel_a = _bias_grad(dbias_a, jnp.asarray(idx_np[0]), name="d_rel_a")
    b_grads = []
    d_rel_b = None
    for t, (idx, zb_r, args) in enumerate(b_ctx):
        dq, dk, dv, dbias = sched.run(_band_bwd, zb_r, zb_r, zb_r, do_b[t], stat_b[t], bias_all,
                                      name=f"band_b{t}_bwd", **args)
        b_grads.append((dq, dk, dv))
        d_rel = _bias_grad(dbias, idx, name=f"d_rel_b{t}")
        d_rel_b = d_rel if d_rel_b is None else d_rel_b + d_rel
    gs["rel_bias"] = jnp.concatenate([d_rel_a[:, 0:2 * N_PAIRS], d_rel_b[:, 0:2 * N_PAIRS]], axis=1)
    dza, dzb, dsum_a, dsum_b = _dz_assemble(dq_a, dk_a, dv_a, b_grads, dils, name="d_z")
    gs["b_in"] = jnp.concatenate([dsum_a, dsum_b], axis=1)
    sched.grad("w_in", (_mm_tn(dza, n2, bm=ZA_W, bn=1024, out_dtype=BF16, name="d_w_in_a"),
                        _mm_tn(dzb, n2, bm=ZB_W // 2, bn=1024, out_dtype=BF16, name="d_w_in_b")))
    dn2_a = _mm_nn(dza, win_a, bm=1024, bn=1024, out_dtype=F32, name="d_n2_a")
    dh1, df1, gs["attn_pre_g"], gs["ffn1_post_g"], _ = sched.run(
        _mm_nt_norms_bwd, dzb, win_b, dh2, h1, small["attn_pre_g"], f1, small["ffn1_post_g"], 0.5, bm=512,
        name="d_n2_b", add=dn2_a, b_is_kn=True)
    grad_x, gs["ffn1_pre_g"] = ffn_bwd(df1, a1, gu1, n1, w_d1, w_gu1, "1", dh1, x, small["ffn1_pre_g"], None, None, 0.0)
    return loss, grad_x, gs


def _to_compute(name, full):
    st = full.reshape(N_CHIPS, full.shape[0] // N_CHIPS, full.shape[1])
    if name in ("ffn1_w_gu", "ffn2_w_gu"):
        return st
    if name == "w_ple_proj":
        return jnp.transpose(st, (1, 0, 2)).reshape(st.shape[1], -1)
    if name == "w_in":
        return full[0:ZA_W], full[ZA_W:]
    return full


def _to_comm(name, g):
    if name == "w_in":
        g = jnp.concatenate(g, axis=0)
    if name == "w_ple_proj":
        g = jnp.transpose(g.reshape(g.shape[0], N_CHIPS, -1), (1, 0, 2))
    rs = g.shape[1] if g.ndim == 3 else g.shape[0] // N_CHIPS
    return g.reshape(N_CHIPS, 2, rs // 2, g.shape[-1])


class _Schedule:
    GATHER = {
        "prologue": [("ffn1_w_gu", (0, 1), "both")],
        "ffn1_gu": [("ffn1_w_down", (0, 1), "both"), ("w_in", (0, 1), "both")],
        "band_a_fwd": [("ffn2_w_gu", (0, 2), "chips"), ("w_out", (0, 1), "chips")],
        "band_b0_fwd": [("ffn2_w_gu", (0, 2), "sibling"), ("w_out", (0, 1), "sibling"),
                        ("ffn2_w_gu", (1, 2), "chips"), ("w_ple_gate", (0, 1), "chips"), ("w_ple_proj", (0, 1), "chips")],
        "band_b1_fwd": [("ffn2_w_gu", (1, 2), "sibling"), ("w_ple_gate", (0, 1), "sibling"),
                        ("w_ple_proj", (0, 1), "sibling"), ("ffn2_w_down", (0, 1), "both")],
    }
    SWAP = {"ffn2_d_w_down": ["w_ple_proj", "w_ple_gate", "ffn2_w_gu"], "ffn2_d_n": ["ffn2_w_down"],
            "band_a_bwd": ["w_out"], "d_n2_b": ["w_in"], "ffn1_d_w_down": ["ffn1_w_gu"], "ffn1_d_n": ["ffn1_w_down"]}
    SCATTER = {"ffn2_d_n": ["ffn2_w_gu"], "band_a_bwd": ["w_ple_proj", "w_ple_gate", "ffn2_w_down"],
               "d_n2_b": ["w_out"], "ffn1_d_w_down": ["w_in"], "ffn1_d_n": ["ffn1_w_gu"]}
    HALF_SWAP = {"ffn1_d_n": ["w_ple_proj", "w_ple_gate", "ffn2_w_gu", "ffn2_w_down", "w_out", "w_in"]}

    def __init__(self, placed, c_idx, kc_idx):
        self.full = dict(placed)
        self.missing = {n: 1.0 for n in placed}
        self.c_idx, self.kc_idx = c_idx, kc_idx
        self.grads, self.swapped, self.pair_sums, self.scattered, self.summed = {}, {}, {}, {}, {}

    def _gather(self, entries):
        if not entries:
            return [], []
        names = list(dict.fromkeys(n for n, _, _ in entries))
        carry = _gather_carry([self.full[n] for n in names], [(names.index(n), pt, hops) for n, pt, hops in entries])

        def done():
            self.full.update(zip(names, carry.results))
            for n, part, hops in entries:
                if hops != "chips":
                    self.missing[n] -= 1.0 / part[1]
        return [carry], [done]

    def weight(self, name):
        assert abs(self.missing[name]) < 1e-9, (name, self.missing[name])
        return _to_compute(name, self.full[name])

    def grad(self, name, g):
        self.grads[name] = _to_comm(name, g)

    def _swap(self, names):
        carry = _pair_swap_carry([self.grads[n] for n in names])

        def done():
            self.swapped.update(zip(names, carry.results))
        return carry, done

    def _scatter(self, names):
        for n in names:
            self.pair_sums[n] = _pair_sum(self.grads[n], self.swapped[n], self.c_idx, name=f"grad_pair_sum_{n}")
        carry = _chip_scatter_carry([self.pair_sums[n] for n in names])

        def done():
            self.scattered.update(zip(names, carry.results))
        return carry, done

    def _half_swap(self, names):
        halves = _chip_sum([self.pair_sums[n] for n in names], [self.scattered[n] for n in names], self.kc_idx,
                           name=f"grad_chip_sum_{names[0]}")
        carry = _half_swap_carry(halves)

        def done():
            self.summed.update(zip(names, carry.results))
        return carry, done

    def run(self, fn, *args, name, **kw):
        carries, after = self._gather(self.GATHER.get(name, []))
        for names, make in ((self.SWAP.get(name), self._swap), (self.SCATTER.get(name), self._scatter),
                            (self.HALF_SWAP.get(name), self._half_swap)):
            if names:
                carry, done = make(names)
                carries.append(carry)
                after.append(done)
        out = fn(*args, name=name, carry=_join(carries), **kw)
        for done in after:
            done()
        return out

    def finish(self, last_carrier):
        left = [n for n in BIG if n not in self.scattered]
        to_swap = [n for n in left if n not in self.swapped]
        if to_swap:
            carry, done = self._swap(to_swap)
            _comm_call(carry, name="grad_pair_swap")
            done()
        carry, done = self._scatter(left) if left else (None, lambda: None)
        carried = last_carrier(carry=carry)
        done()
        carry, done = self._half_swap([n for n in BIG if n not in self.summed])
        _comm_call(carry, name="grad_half_swap")
        done()
        return self.summed, carried


def kernel(x, p, rel_bias, ffn1_pre_g, ffn1_w_gu, ffn1_w_down, ffn1_post_g, attn_pre_g, w_in, b_in, sinks, w_out, b_out, attn_post_g, ffn2_pre_g, ffn2_w_gu, ffn2_w_down, ffn2_post_g, ple_pre_g, w_ple_gate, w_ple_proj, ple_post_g, loss_target, m_rel_bias, m_ffn1_pre_g, m_ffn1_w_gu, m_ffn1_w_down, m_ffn1_post_g, m_attn_pre_g, m_w_in, m_b_in, m_sinks, m_w_out, m_b_out, m_attn_post_g, m_ffn2_pre_g, m_ffn2_w_gu, m_ffn2_w_down, m_ffn2_post_g, m_ple_pre_g, m_w_ple_gate, m_w_ple_proj, m_ple_post_g, v_rel_bias, v_ffn1_pre_g, v_ffn1_w_gu, v_ffn1_w_down, v_ffn1_post_g, v_attn_pre_g, v_w_in, v_b_in, v_sinks, v_w_out, v_b_out, v_attn_post_g, v_ffn2_pre_g, v_ffn2_w_gu, v_ffn2_w_down, v_ffn2_post_g, v_ple_pre_g, v_w_ple_gate, v_w_ple_proj, v_ple_post_g):
    given = dict(locals())
    w = {n: given[n] for n in WEIGHTS}
    m = {n: given["m_" + n] for n in WEIGHTS}
    v = {n: given["v_" + n] for n in WEIGHTS}
    c_pos = lax.axis_index("c").astype(jnp.int32)
    k_pos = (2 * lax.axis_index("x") + lax.axis_index("y")).astype(jnp.int32)
    c_idx, k_idx, kc_idx = c_pos.reshape(1), k_pos.reshape(1), jnp.stack([k_pos, c_pos])

    def shard(a, n):
        return jnp.transpose(a[0]) if n == "w_in" else a[0]

    def unshard(a, n):
        return (jnp.transpose(a) if n == "w_in" else a)[None]

    placed = _cast_place([shard(w[n], n) for n in BIG], k_idx, name="place_shards")
    sched = _Schedule(dict(zip(BIG, placed)), c_idx, kc_idx)
    small = {n: (w[n] if n == "rel_bias" else w[n].reshape(1, -1)) for n in SMALL}

    loss_part, grad_x, gs = _local_step(x[0], p[0, 0], loss_target[0], small, sched)

    grads_big, (*small_grads, loss_row) = sched.finish(
        partial(_small_all_reduce, [gs[n] for n in SMALL] + [loss_part], name="small_all_reduce"))
    loss = loss_row[0, 0]

    grads, delta, new_m, new_v = {}, {}, {}, {}
    res = _adamw([shard(w[n], n) for n in BIG], [grads_big[n] for n in BIG], [shard(m[n], n) for n in BIG],
                 [shard(v[n], n) for n in BIG], name="adamw_big")
    for n, gg, dd, mm, vv in zip(BIG, *res):
        grads[n], delta[n], new_m[n], new_v[n] = (unshard(a, n) for a in (gg, dd, mm, vv))
    res = _adamw_small([w[n] for n in SMALL], small_grads, [m[n] for n in SMALL], [v[n] for n in SMALL],
                       name="adamw_small")
    for name, gg, dd, mm, vv in zip(SMALL, *res):
        grads[name], delta[name], new_m[name], new_v[name] = gg, dd, mm, vv

    return (loss, grad_x[None], *[grads[n] for n in WEIGHTS], *[delta[n] for n in WEIGHTS],
            *[new_m[n] for n in WEIGHTS], *[new_v[n] for n in WEIGHTS])
```

```python
import math
from functools import partial

import jax
import jax.numpy as jnp
import numpy as np
from jax import lax
from jax.experimental import pallas as pl
from jax.experimental.pallas import tpu as pltpu

F32 = jnp.float32
BF16 = jnp.bfloat16
MESH = pl.DeviceIdType.MESH

EPS = 1e-6
NEG = -1e30
LANES = 128
SUBLANES = 8
HEAD_DIM = 64
QBLK = 128
N_PAIRS = 4
MIX_W = N_PAIRS * LANES
A_WINDOW = 128
B_PATTERNS = ((128, 1), (512, 4), (2048, 16))
NUM_BUCKETS = 32
MAX_DISTANCE = 2048
ZA_W = 768
ZB_W = 1536
N_CHIPS = 4
VMEM_BYTES_V7X = 64 * 2**20
VMEM_CLAIM = VMEM_BYTES_V7X - (6 << 20)

ADAM_LR, ADAM_B1, ADAM_B2, ADAM_EPS, ADAM_WD, ADAM_STEP = 0.001, 0.9, 0.999, 1e-08, 0.01, 10

BIG = ("ffn1_w_gu", "ffn1_w_down", "w_in", "w_out", "ffn2_w_gu", "ffn2_w_down", "w_ple_gate", "w_ple_proj")
SMALL = ("rel_bias", "ffn1_pre_g", "ffn1_post_g", "attn_pre_g", "b_in", "sinks", "b_out", "attn_post_g",
         "ffn2_pre_g", "ffn2_post_g", "ple_pre_g", "ple_post_g")
WEIGHTS = ("rel_bias", "ffn1_pre_g", "ffn1_w_gu", "ffn1_w_down", "ffn1_post_g", "attn_pre_g", "w_in", "b_in",
           "sinks", "w_out", "b_out", "attn_post_g", "ffn2_pre_g", "ffn2_w_gu", "ffn2_w_down", "ffn2_post_g",
           "ple_pre_g", "w_ple_gate", "w_ple_proj", "ple_post_g")


def _vmem_limit(*block_bytes):
    del block_bytes
    return VMEM_CLAIM


def _nbytes(shape, dtype):
    return int(np.prod(shape)) * jnp.dtype(dtype).itemsize


MXU_WIDTH_V7X = 256


def _col_chunks(n):
    return [slice(c, min(c + MXU_WIDTH_V7X, n)) for c in range(0, n, MXU_WIDTH_V7X)]


def _row_halves(rows):
    return [slice(0, rows // 2), slice(rows // 2, rows)]


def _row_block(rows, cap, mult):
    for cand in range(min(rows, cap), 0, -1):
        if rows % cand == 0 and cand % mult == 0:
            return cand
    raise ValueError((rows, cap, mult))


class _Carry:
    def __init__(self, operands, out_shapes, aliases, sems, start, finish, before_last=None):
        self.operands, self.out_shapes, self.aliases, self.sems = list(operands), list(out_shapes), dict(aliases), list(sems)
        self.start, self.finish = start, finish
        self.before_last = before_last or (lambda ins, outs, sems: None)
        self.results = None


def _join(carries):
    carries = [c for c in carries if c is not None]
    if not carries:
        return None
    if len(carries) == 1:
        return carries[0]
    offs, pos = [], [0, 0, 0]
    for c in carries:
        offs.append(tuple(pos))
        pos = [pos[0] + len(c.operands), pos[1] + len(c.out_shapes), pos[2] + len(c.sems)]
    aliases = {}
    for c, (oi, oo, _) in zip(carries, offs):
        aliases.update({oi + i: oo + o for i, o in c.aliases.items()})

    def run(which):
        def fn(ins, outs, sems):
            for c, (oi, oo, os_) in zip(carries, offs):
                getattr(c, which)(ins[oi:oi + len(c.operands)], outs[oo:oo + len(c.out_shapes)],
                                  sems[os_:os_ + len(c.sems)])
        return fn

    joined = _Carry([a for c in carries for a in c.operands], [s for c in carries for s in c.out_shapes], aliases,
                    [s for c in carries for s in c.sems], run("start"), run("finish"), run("before_last"))
    joined.parts = (carries, offs)
    return joined


def _set_results(carry, outs):
    carry.results = list(outs)
    if hasattr(carry, "parts"):
        for c, (_, oo, _) in zip(*carry.parts):
            _set_results(c, outs[oo:oo + len(c.out_shapes)])


ANY = pl.BlockSpec(memory_space=pl.ANY)


def _call(body, *, name, grid, in_specs, out_specs, out_shape, args, scratch=(), vmem_limit=None, carry=None,
          semantics=None, before_carry=None):
    assert before_carry is None or carry is not None
    n_ci, n_co, n_cs = len(args), len(out_shape), len(scratch)
    semantics = semantics or ("parallel",) * len(grid)
    vmem_limit = vmem_limit or VMEM_CLAIM
    if carry is None:
        res = pl.pallas_call(
            body, name=name, grid=grid, in_specs=list(in_specs), out_specs=tuple(out_specs), out_shape=tuple(out_shape),
            scratch_shapes=list(scratch),
            compiler_params=pltpu.CompilerParams(dimension_semantics=semantics, vmem_limit_bytes=vmem_limit),
        )(*args)
        return list(res)
    n_pi, n_po = len(carry.operands), len(carry.out_shapes)

    def full_body(*refs):
        ci, pi = refs[:n_ci], refs[n_ci:n_ci + n_pi]
        o0 = n_ci + n_pi
        co, po = refs[o0:o0 + n_co], refs[o0 + n_co:o0 + n_co + n_po]
        s0 = o0 + n_co + n_po
        cs, ps = refs[s0:s0 + n_cs], refs[s0 + n_cs:]
        ids = [pl.program_id(ax) for ax in range(len(grid))]
        first, last = ids[0] == 0, ids[0] == grid[0] - 1
        for ax in range(1, len(grid)):
            first, last = first & (ids[ax] == 0), last & (ids[ax] == grid[ax] - 1)

        @pl.when(first)
        def _():
            if before_carry is not None:
                before_carry(*ci, *co, *cs)
            carry.start(pi, po, ps)

        several_steps = any(g > 1 for g in grid)
        if several_steps:
            @pl.when(last)
            def _():
                carry.before_last(pi, po, ps)

        body(*ci, *co, *cs)

        @pl.when(last)
        def _():
            if not several_steps:
                carry.before_last(pi, po, ps)
            carry.finish(pi, po, ps)

    res = pl.pallas_call(
        full_body, name=name, grid=grid, in_specs=list(in_specs) + [ANY] * n_pi,
        out_specs=tuple(out_specs) + tuple([ANY] * n_po), out_shape=tuple(out_shape) + tuple(carry.out_shapes),
        scratch_shapes=list(scratch) + carry.sems,
        input_output_aliases={n_ci + i: n_co + o for i, o in carry.aliases.items()},
        compiler_params=pltpu.CompilerParams(dimension_semantics=("arbitrary",) * len(grid),
                                             vmem_limit_bytes=vmem_limit),
    )(*args, *carry.operands)
    _set_results(carry, res[n_co:])
    return list(res[:n_co])


def _comm_call(carry, *, name):
    n_pi, n_po = len(carry.operands), len(carry.out_shapes)

    def body(*refs):
        pi, po, ps = refs[:n_pi], refs[n_pi:n_pi + n_po], refs[n_pi + n_po:]
        carry.start(pi, po, ps)
        carry.before_last(pi, po, ps)
        carry.finish(pi, po, ps)

    res = pl.pallas_call(
        body, name=name, in_specs=[ANY] * n_pi, out_specs=tuple([ANY] * n_po), out_shape=tuple(carry.out_shapes),
        scratch_shapes=carry.sems, input_output_aliases=dict(carry.aliases),
    )(*carry.operands)
    _set_results(carry, res)


def _mm_nn(a, b, *, bm, bn, out_dtype, name, bias=None, carry=None):
    M, K = a.shape
    stacked = b.ndim == 3
    N = b.shape[0] * b.shape[2] if stacked else b.shape[1]
    assert M % bm == 0 and N % bn == 0 and (not stacked or bn == b.shape[2])

    def body(*refs):
        a_ref, b_ref = refs[0], refs[1]
        o_ref = refs[-1]
        acc = jnp.dot(a_ref[...], b_ref[...], preferred_element_type=F32)
        if bias is not None:
            acc = acc + refs[2][...]
        o_ref[...] = acc.astype(o_ref.dtype)

    if stacked:
        b_spec = pl.BlockSpec((None, K, bn), lambda i, j: (j, 0, 0))
    else:
        b_spec = pl.BlockSpec((K, bn), lambda i, j: (0, j))
    in_specs = [pl.BlockSpec((bm, K), lambda i, j: (i, 0)), b_spec]
    args = [a, b]
    if bias is not None:
        in_specs.append(pl.BlockSpec((1, bn), lambda i, j: (0, j)))
        args.append(bias)
    limit = _vmem_limit(_nbytes((bm, K), a.dtype), _nbytes((K, bn), b.dtype), _nbytes((bm, bn), F32))
    return _call(body, name=name, grid=(M // bm, N // bn), in_specs=in_specs,
                 out_specs=[pl.BlockSpec((bm, bn), lambda i, j: (i, j))],
                 out_shape=[jax.ShapeDtypeStruct((M, N), out_dtype)], args=args, vmem_limit=limit, carry=carry)[0]


def _mm_nt(a, b, *, bm, bn, out_dtype, name, add=None, carry=None):
    M, K = a.shape
    stacked = b.ndim == 3
    N = b.shape[1] if stacked else b.shape[0]
    n_sh = b.shape[0] if stacked else 1
    ks = b.shape[2] if stacked else K
    assert M % bm == 0 and N % bn == 0 and n_sh * ks == K
    dn = (((1,), (1,)), ((), ()))

    def body(*refs):
        a_ref, b_ref = refs[0], refs[1]
        o_ref = refs[-1]
        if stacked:
            acc = lax.dot_general(a_ref[:, 0:ks], b_ref[0], dn, preferred_element_type=F32)
            for s in range(1, n_sh):
                acc = acc + lax.dot_general(a_ref[:, s * ks:(s + 1) * ks], b_ref[s], dn, preferred_element_type=F32)
        else:
            acc = lax.dot_general(a_ref[...], b_ref[...], dn, preferred_element_type=F32)
        if add is not None:
            acc = acc + refs[2][...]
        o_ref[...] = acc.astype(o_ref.dtype)

    if stacked:
        b_spec = pl.BlockSpec((n_sh, bn, ks), lambda i, j: (0, j, 0))
    else:
        b_spec = pl.BlockSpec((bn, K), lambda i, j: (j, 0))
    in_specs = [pl.BlockSpec((bm, K), lambda i, j: (i, 0)), b_spec]
    args = [a, b]
    if add is not None:
        rows = bm if add.shape[0] == M else 1
        in_specs.append(pl.BlockSpec((rows, bn), lambda i, j: (i if rows == bm else 0, j)))
        args.append(add)
    limit = _vmem_limit(_nbytes((bm, K), a.dtype), _nbytes((bn, K), b.dtype), _nbytes((bm, bn), F32))
    return _call(body, name=name, grid=(M // bm, N // bn), in_specs=in_specs,
                 out_specs=[pl.BlockSpec((bm, bn), lambda i, j: (i, j))],
                 out_shape=[jax.ShapeDtypeStruct((M, N), out_dtype)], args=args, vmem_limit=limit, carry=carry)[0]


def _mm_tn(a, b, *, bm, bn, out_dtype, name, n_stack=None, carry=None):
    T, M = a.shape
    N = b.shape[1]
    assert M % bm == 0 and N % bn == 0 and (n_stack is None or bn * n_stack == N)
    dn = (((0,), (0,)), ((), ()))

    def body(a_ref, b_ref, o_ref):
        acc = lax.dot_general(a_ref[...], b_ref[...], dn, preferred_element_type=F32)
        o_ref[...] = acc.astype(o_ref.dtype)

    if n_stack is None:
        out_spec = pl.BlockSpec((bm, bn), lambda i, j: (i, j))
        out_shape = jax.ShapeDtypeStruct((M, N), out_dtype)
    else:
        out_spec = pl.BlockSpec((None, bm, bn), lambda i, j: (j, i, 0))
        out_shape = jax.ShapeDtypeStruct((n_stack, M, bn), out_dtype)
    limit = _vmem_limit(_nbytes((T, bm), a.dtype), _nbytes((T, bn), b.dtype), _nbytes((bm, bn), F32))
    return _call(body, name=name, grid=(M // bm, N // bn),
                 in_specs=[pl.BlockSpec((T, bm), lambda i, j: (0, i)), pl.BlockSpec((T, bn), lambda i, j: (0, j))],
                 out_specs=[out_spec], out_shape=[out_shape], args=[a, b], vmem_limit=limit, carry=carry)[0]


ROW_BLK = 256


def _rstd(x):
    return lax.rsqrt(jnp.mean(x * x, axis=-1, keepdims=True) + EPS)


def _norm_bwd(xhat, rstd, dxhat):
    return rstd * (dxhat - xhat * jnp.mean(dxhat * xhat, axis=-1, keepdims=True))


def _row_spec(cols):
    return pl.BlockSpec((ROW_BLK, cols), lambda i: (i, 0))


def _vec_spec(cols):
    return pl.BlockSpec((1, cols), lambda i: (0, 0))


def _prologue(x, g, rel_bias, idx_all, *, name, carry=None):
    T, D = x.shape
    n_pat = idx_all.shape[0]
    heads = 2 * N_PAIRS
    steps = n_pat * heads
    rows = T // steps

    def body(rb_ref, x_ref, g_ref, idx_ref, n_ref, bias_ref):
        s = pl.program_id(0)
        head = jnp.where(s < heads, s, heads + s % heads)
        xv = x_ref[...]
        n_ref[...] = (xv * _rstd(xv) * g_ref[...]).astype(n_ref.dtype)
        idxv = idx_ref[...]
        acc = jnp.where(idxv < 0, NEG, 0.0).astype(F32)
        for b in range(NUM_BUCKETS):
            acc = acc + jnp.where(idxv == b, rb_ref[b, head], 0.0)
        bias_ref[0] = acc
        kap = lax.broadcasted_iota(jnp.int32, (1, 2 * QBLK), 1)
        bias_ref[1] = jnp.where(kap < QBLK, NEG, acc)

    return _call(
        body, name=name, grid=(steps,),
        in_specs=[pl.BlockSpec(memory_space=pltpu.SMEM), pl.BlockSpec((rows, D), lambda s: (s, 0)),
                  pl.BlockSpec((1, D), lambda s: (0, 0)),
                  pl.BlockSpec((None, QBLK, 2 * QBLK), lambda s: (s // heads, 0, 0))],
        out_specs=[pl.BlockSpec((rows, D), lambda s: (s, 0)),
                   pl.BlockSpec((None, 2, None, QBLK, 2 * QBLK), lambda s: (s // heads, 0, s % heads, 0, 0))],
        out_shape=[jax.ShapeDtypeStruct((T, D), BF16),
                   jax.ShapeDtypeStruct((n_pat, 2, heads, QBLK, 2 * QBLK), F32)],
        args=[rel_bias, x, g, idx_all], carry=carry)


def _mm_resid_norm(a, b, h, g_post, coef, g_next, *, bm, name, bias=None, carry=None):
    M, K = a.shape
    N = b.shape[1]

    def body(*refs):
        a_ref, b_ref, h_ref, gp_ref, gn_ref = refs[0:5]
        f_ref, hn_ref, n_ref = refs[-3:]
        for rows in _row_halves(bm):
            f = jnp.dot(a_ref[rows, :], b_ref[...], preferred_element_type=F32)
            if bias is not None:
                f = f + refs[5][...]
            f_ref[rows, :] = f
            hn = h_ref[rows, :] + coef * (f * _rstd(f) * gp_ref[...])
            hn_ref[rows, :] = hn
            n_ref[rows, :] = (hn * _rstd(hn) * gn_ref[...]).astype(n_ref.dtype)

    row = lambda w: pl.BlockSpec((bm, w), lambda i: (i, 0))
    vec = pl.BlockSpec((1, N), lambda i: (0, 0))
    in_specs = [row(K), pl.BlockSpec((K, N), lambda i: (0, 0), pipeline_mode=pl.Buffered(1)), row(N), vec, vec]
    args = [a, b, h, g_post, g_next]
    if bias is not None:
        in_specs.append(vec)
        args.append(bias)
    limit = _vmem_limit(_nbytes((bm, K), a.dtype), _nbytes((K, N), b.dtype) // 2, 4 * _nbytes((bm, N), F32))
    return _call(body, name=name, grid=(M // bm,), in_specs=in_specs, out_specs=[row(N), row(N), row(N)],
                 out_shape=[jax.ShapeDtypeStruct((M, N), F32), jax.ShapeDtypeStruct((M, N), F32),
                            jax.ShapeDtypeStruct((M, N), BF16)], args=args, vmem_limit=limit, carry=carry)


def _mm_nt_norms_bwd(a, b, dh_in, h, g_pre, f, g_post, coef, *, bm, name, add=None, b_is_kn=False, carry=None):
    M, K = a.shape
    stacked = b.ndim == 3
    N = b.shape[1] if (stacked or b_is_kn) else b.shape[0]
    n_sh = b.shape[0] if stacked else 1
    ks = b.shape[2] if stacked else K
    assert n_sh * ks == K
    dn_dims = (((1,), (0,)), ((), ())) if b_is_kn else (((1,), (1,)), ((), ()))
    with_f = f is not None
    n_in = 5 + (2 if with_f else 0) + (1 if add is not None else 0)

    def body(*refs):
        a_ref, b_ref, dhi_ref, h_ref, gpre_ref = refs[0:5]
        outs = refs[n_in:]

        @pl.when(pl.program_id(0) == 0)
        def _():
            for acc in (outs[2:] if with_f else outs[1:]):
                acc[...] = jnp.zeros_like(acc)

        for rows in _row_halves(bm):
            if stacked:
                dn = lax.dot_general(a_ref[rows, 0:ks], b_ref[0], dn_dims, preferred_element_type=F32)
                for s in range(1, n_sh):
                    dn = dn + lax.dot_general(a_ref[rows, s * ks:(s + 1) * ks], b_ref[s], dn_dims,
                                              preferred_element_type=F32)
            else:
                dn = lax.dot_general(a_ref[rows, :], b_ref[...], dn_dims, preferred_element_type=F32)
            if add is not None:
                dn = dn + refs[n_in - 1][rows, :]
            hv = h_ref[rows, :]
            r = _rstd(hv)
            hh = hv * r
            dh = dhi_ref[rows, :] + _norm_bwd(hh, r, dn * gpre_ref[...])
            outs[0][rows, :] = dh
            if not with_f:
                outs[1][...] += jnp.sum(dn * hh, axis=0, keepdims=True)
                continue
            f_ref, gpost_ref = refs[5], refs[6]
            df_ref, dgpre_ref, dgpost_ref, dsum_ref = outs[1:]
            dgpre_ref[...] += jnp.sum(dn * hh, axis=0, keepdims=True)
            fv = f_ref[rows, :]
            rf = _rstd(fv)
            fh = fv * rf
            dy = coef * dh
            dgpost_ref[...] += jnp.sum(dy * fh, axis=0, keepdims=True)
            df = _norm_bwd(fh, rf, dy * gpost_ref[...])
            dsum_ref[...] += jnp.sum(df, axis=0, keepdims=True)
            df_ref[rows, :] = df.astype(df_ref.dtype)

    row = lambda w: pl.BlockSpec((bm, w), lambda i: (i, 0))
    vec = pl.BlockSpec((1, N), lambda i: (0, 0))
    once = pl.Buffered(1)
    if stacked:
        b_spec = pl.BlockSpec((n_sh, N, ks), lambda i: (0, 0, 0), pipeline_mode=once)
    else:
        b_spec = pl.BlockSpec(b.shape, lambda i: (0, 0), pipeline_mode=once)
    in_specs = [row(K), b_spec, row(N), row(N), vec]
    args = [a, b, dh_in, h, g_pre]
    if with_f:
        in_specs += [row(N), vec]
        args += [f, g_post]
    if add is not None:
        in_specs.append(row(N))
        args.append(add)
    vec_shape = jax.ShapeDtypeStruct((1, N), F32)
    out_specs = [row(N)] + ([row(N), vec, vec, vec] if with_f else [vec])
    out_shape = [jax.ShapeDtypeStruct((M, N), F32)]
    out_shape += [jax.ShapeDtypeStruct((M, N), BF16), vec_shape, vec_shape, vec_shape] if with_f else [vec_shape]
    limit = _vmem_limit(_nbytes((bm, K), a.dtype), _nbytes((N, K), b.dtype) // 2, 6 * _nbytes((bm, N), F32))
    return _call(body, name=name, grid=(M // bm,), in_specs=in_specs, out_specs=out_specs, out_shape=out_shape,
                 args=args, vmem_limit=limit, semantics=("arbitrary",), carry=carry)


def _ffn_up(n, w_gu, *, bm, name, carry=None):
    M, K = n.shape
    n_sh, _, ns = w_gu.shape
    half = n_sh // 2

    def body(n_ref, wg_ref, wu_ref, g_ref, u_ref, a_ref):
        nv = n_ref[...]
        for cols in _col_chunks(ns):
            g = jnp.dot(nv, wg_ref[:, cols], preferred_element_type=F32)
            u = jnp.dot(nv, wu_ref[:, cols], preferred_element_type=F32)
            g_ref[:, cols] = g.astype(g_ref.dtype)
            u_ref[:, cols] = u.astype(u_ref.dtype)
            a_ref[:, cols] = (g * jax.nn.sigmoid(g) * u).astype(a_ref.dtype)

    out_spec = pl.BlockSpec((bm, ns), lambda i, j: (i, j))
    out = jax.ShapeDtypeStruct((M, half * ns), BF16)
    limit = _vmem_limit(_nbytes((bm, K), n.dtype), 2 * _nbytes((K, ns), w_gu.dtype), 3 * _nbytes((bm, ns), F32))
    return _call(body, name=name, grid=(M // bm, half),
                 in_specs=[pl.BlockSpec((bm, K), lambda i, j: (i, 0)),
                           pl.BlockSpec((None, K, ns), lambda i, j: (j, 0, 0)),
                           pl.BlockSpec((None, K, ns), lambda i, j: (j + half, 0, 0))],
                 out_specs=[out_spec, out_spec, out_spec], out_shape=[out, out, out], args=[n, w_gu, w_gu],
                 vmem_limit=limit, carry=carry)


def _ffn_down_bwd(df, w_down, g, u, *, bm, name, carry=None):
    M, D = df.shape
    F = w_down.shape[0]
    dn = (((1,), (1,)), ((), ()))

    def body(df_ref, w_ref, g_ref, u_ref, o_ref):
        dfv = df_ref[...]
        for cols in _col_chunks(F):
            da = lax.dot_general(dfv, w_ref[cols, :], dn, preferred_element_type=F32)
            gv = g_ref[:, cols].astype(F32)
            s = jax.nn.sigmoid(gv)
            o_ref[:, cols] = (da * u_ref[:, cols].astype(F32) * (s * (1.0 + gv * (1.0 - s)))).astype(o_ref.dtype)
            o_ref[:, F + cols.start:F + cols.stop] = (da * (gv * s)).astype(o_ref.dtype)

    row = lambda w: pl.BlockSpec((bm, w), lambda i: (i, 0))
    limit = _vmem_limit(_nbytes((F, D), w_down.dtype) // 2, 2 * _nbytes((bm, F), BF16), _nbytes((bm, 2 * F), BF16))
    return _call(body, name=name, grid=(M // bm,),
                 in_specs=[row(D), pl.BlockSpec((F, D), lambda i: (0, 0), pipeline_mode=pl.Buffered(1)), row(F), row(F)],
                 out_specs=[row(2 * F)], out_shape=[jax.ShapeDtypeStruct((M, 2 * F), BF16)],
                 args=[df, w_down, g, u], vmem_limit=limit, carry=carry)[0]


def _ple_head(n4, w_gate, p, w_ple, h3, g_post, target, *, bm, name):
    T, D = h3.shape
    kp = p.shape[1]

    def body(n_ref, wg_ref, p_ref, wp_ref, h_ref, g_ref, t_ref, dh_ref, de_ref, dgp_ref, loss_ref, dg_ref):
        @pl.when(pl.program_id(0) == 0)
        def _():
            loss_ref[...] = jnp.zeros_like(loss_ref)
            dg_ref[...] = jnp.zeros_like(dg_ref)

        gpost = g_ref[...]
        for rows in _row_halves(bm):
            gate = jax.nn.sigmoid(jnp.dot(n_ref[rows, :], wg_ref[...], preferred_element_type=F32))
            ev = jnp.dot(p_ref[rows, :], wp_ref[...], preferred_element_type=F32)
            ge = gate * ev
            r = _rstd(ge)
            gh = ge * r
            diff = h_ref[rows, :] + gh * gpost - t_ref[rows, :]
            loss_ref[...] += jnp.sum(diff * diff) * (0.5 / D)
            dh = diff * (1.0 / D)
            dh_ref[rows, :] = dh
            dg_ref[...] += jnp.sum(dh * gh, axis=0, keepdims=True)
            dge = _norm_bwd(gh, r, dh * gpost)
            de_ref[rows, :] = (dge * gate).astype(de_ref.dtype)
            dgp_ref[rows, :] = (dge * ev * gate * (1.0 - gate)).astype(dgp_ref.dtype)

    row = lambda w: pl.BlockSpec((bm, w), lambda i: (i, 0))
    whole = lambda a: pl.BlockSpec(a.shape, lambda i: (0, 0), pipeline_mode=pl.Buffered(1))
    limit = _vmem_limit(_nbytes((bm, D), BF16), _nbytes(w_gate.shape, w_gate.dtype) // 2, 6 * _nbytes((bm, D), F32))
    return pl.pallas_call(
        body, name=name, grid=(T // bm,),
        in_specs=[row(D), whole(w_gate), row(kp), whole(w_ple), row(D), _vec_spec(D), row(D)],
        out_specs=(row(D), row(D), row(D), _vec_spec(LANES), _vec_spec(D)),
        out_shape=(jax.ShapeDtypeStruct((T, D), F32), jax.ShapeDtypeStruct((T, D), BF16),
                   jax.ShapeDtypeStruct((T, D), BF16), jax.ShapeDtypeStruct((1, LANES), F32),
                   jax.ShapeDtypeStruct((1, D), F32)),
        compiler_params=pltpu.CompilerParams(dimension_semantics=("arbitrary",), vmem_limit_bytes=limit),
    )(n4, w_gate, p, w_ple, h3, g_post, target)


def _t5_index(max_dist, stride):
    rho = np.arange(QBLK)[:, None]
    kap = np.arange(2 * QBLK)[None, :]
    d = rho + QBLK - kap
    valid = (d >= 0) & (d <= max_dist)
    n = np.maximum(d, 0) * stride
    max_exact = NUM_BUCKETS // 2
    nf = np.maximum(n, 1).astype(np.float32)
    large = max_exact + (np.log(nf / np.float32(max_exact)) / np.float32(math.log(MAX_DISTANCE / max_exact))
                         * np.float32(NUM_BUCKETS - max_exact)).astype(np.int32)
    large = np.minimum(large, NUM_BUCKETS - 1)
    bucket = np.where(n < max_exact, n, large)
    return np.where(valid, bucket, -1).astype(np.int32)


def _bias_grad(d_bias, idx, *, name):
    def body(db_ref, idx_ref, o_ref):
        h = pl.program_id(0)

        @pl.when(h == 0)
        def _():
            o_ref[...] = jnp.zeros_like(o_ref)

        idxv = idx_ref[...]
        dv = db_ref[0]
        rows = lax.broadcasted_iota(jnp.int32, (NUM_BUCKETS, LANES), 0)
        lanes = lax.broadcasted_iota(jnp.int32, (NUM_BUCKETS, LANES), 1)
        acc = o_ref[...]
        for b in range(NUM_BUCKETS):
            s = jnp.sum(jnp.where(idxv == b, dv, 0.0))
            acc = acc + jnp.where((rows == b) & (lanes == h), s, 0.0)
        o_ref[...] = acc

    return pl.pallas_call(
        body, name=name, grid=(2 * N_PAIRS,),
        in_specs=[pl.BlockSpec((1, QBLK, 2 * QBLK), lambda h: (h, 0, 0)),
                  pl.BlockSpec((QBLK, 2 * QBLK), lambda h: (0, 0))],
        out_specs=pl.BlockSpec((NUM_BUCKETS, LANES), lambda h: (0, 0)),
        out_shape=jax.ShapeDtypeStruct((NUM_BUCKETS, LANES), F32),
        compiler_params=pltpu.CompilerParams(dimension_semantics=("arbitrary",)),
    )(d_bias, idx)


def _lane():
    return lax.broadcasted_iota(jnp.int32, (1, LANES), 1)


def _head_sel(h):
    return (_lane() < HEAD_DIM) if h == 0 else (_lane() >= HEAD_DIM)


def _stat_lo():
    return (_lane() % HEAD_DIM) < (HEAD_DIM // 2)


def _stack_heads(t, scale=None):
    if scale is not None:
        t = t * scale
    zero = jnp.zeros_like(t)
    return jnp.concatenate([jnp.where(_head_sel(0), t, zero), jnp.where(_head_sel(1), t, zero)], axis=0)


def _class_spec(L, r, cols, stride, pps):
    chunks = N_PAIRS // pps
    mode = pl.Buffered(1) if r * chunks == 1 else None
    return pl.BlockSpec((L, MIX_W // chunks), lambda j, c: (0, (cols + j * stride) * chunks + c), pipeline_mode=mode)


def _rows(b, n_blocks=1):
    return pl.ds(pl.multiple_of(b * QBLK, QBLK), n_blocks * QBLK)


def _key_window(ref, b, cols, first, kv_head=None):
    if kv_head is not None:
        cols = slice(0, LANES)
    if first:
        blk = ref[0:QBLK, cols]
        tile = jnp.concatenate([blk, blk], axis=0)
    else:
        tile = ref[_rows(b - 1, 2), cols]
    if kv_head is None:
        return tile
    wide = tile.astype(F32)
    keep = jnp.logical_xor(_lane() < HEAD_DIM, kv_head == 1)
    return jnp.where(keep, wide, pltpu.roll(wide, HEAD_DIM, 1)).astype(tile.dtype)


def _kv_spec(L, r, col, stride, pps, kv_shared):
    if not kv_shared:
        return _class_spec(L, r, col, stride, pps)
    mode = pl.Buffered(1) if pps == N_PAIRS else None
    return pl.BlockSpec((L, LANES), lambda j, c: (0, col), pipeline_mode=mode)


def _band_fwd(qa, ka, va, bias, *, r, q_col, k_col, v_col, stride, pattern, name, kv_shared=False, sink=None,
              carry=None):
    L = qa.shape[0]
    nb = L // QBLK
    dn = (((1,), (1,)), ((), ()))
    scale = HEAD_DIM ** -0.5

    pps = N_PAIRS

    def body(q_ref, k_ref, v_ref, bias_ref, *rest):
        sel0 = _head_sel(0)
        pair0 = pl.program_id(1) * pps

        def emit(rows, cols, o, lse):
            if sink is None:
                o_ref, lse_ref = rest
                o_ref[rows, cols] = o.astype(o_ref.dtype)
                lse_ref[rows, cols] = lse
                return
            sink_ref, out_ref, outb_ref, lse_ref = rest
            sk = sink_ref[:, cols]
            mx = jnp.maximum(lse, sk)
            w = jnp.exp(lse - mx)
            den = w + jnp.exp(sk - mx)
            out = o * (w / den)
            out_ref[rows, cols] = out
            outb_ref[rows, cols] = out.astype(outb_ref.dtype)
            lse_ref[rows, cols] = mx + jnp.log(den)

        def block(b, first):
            rows = _rows(b)
            for i in range(pps):
                cols = slice(i * LANES, (i + 1) * LANES)
                q2 = _stack_heads(q_ref[rows, cols], scale)
                kv_head = (pair0 + i) // 2 if kv_shared else None
                k = _key_window(k_ref, b, cols, first, kv_head)
                v = _key_window(v_ref, b, cols, first, kv_head)
                bias2 = bias_ref[1 if first else 0, pl.ds(2 * (pair0 + i), 2)].reshape(2 * QBLK, 2 * QBLK)
                s = lax.dot_general(q2, k, dn, preferred_element_type=F32) + bias2
                m = jnp.max(s, axis=1, keepdims=True)
                p = jnp.exp(s - m)
                l = jnp.sum(p, axis=1, keepdims=True)
                o = jnp.dot(p.astype(v.dtype), v, preferred_element_type=F32) * (1.0 / l)
                lse = m + jnp.log(l)
                emit(rows, cols, jnp.where(sel0, o[0:QBLK], o[QBLK:]), jnp.where(sel0, lse[0:QBLK], lse[QBLK:]))

        block(0, True)
        if nb > 1:
            pl.loop(1, nb)(lambda b: block(b, False))

    bias_spec = pl.BlockSpec((None, 2, 2 * N_PAIRS, QBLK, 2 * QBLK), lambda j, c: (pattern, 0, 0, 0, 0))
    out_spec = _class_spec(L, r, 0, 1, pps)
    blk_bytes = _nbytes((L, pps * LANES), F32)
    in_specs = [_class_spec(L, r, q_col, stride, pps), _kv_spec(L, r, k_col, stride, pps, kv_shared),
                _kv_spec(L, r, v_col, stride, pps, kv_shared), bias_spec]
    args = [qa, ka, va, bias]
    f32_out, bf16_out = jax.ShapeDtypeStruct((L, r * MIX_W), F32), jax.ShapeDtypeStruct((L, r * MIX_W), BF16)
    out_shape = [bf16_out, f32_out]
    if sink is not None:
        in_specs.append(pl.BlockSpec((1, MIX_W), lambda j, c: (0, 0)))
        args.append(sink)
        out_shape = [f32_out, bf16_out, f32_out]
    return _call(body, name=name, grid=(r, N_PAIRS // pps), in_specs=in_specs, out_specs=[out_spec] * len(out_shape),
                 out_shape=out_shape, args=args, vmem_limit=_vmem_limit(*([blk_bytes] * 4)), carry=carry)


def _class_rows_spec(rows, r, width):
    return pl.BlockSpec((rows // r, r * width), lambda i, *_: (i, 0))


def _token_scratch(rows, width):
    return pltpu.VMEM((width // LANES, rows, LANES), F32)


def _fill_tokens(scratch, value):
    for c in range(scratch.shape[0]):
        scratch[c] = value[:, c * LANES:(c + 1) * LANES]


def _read_tokens(scratch):
    return jnp.concatenate([scratch[c] for c in range(scratch.shape[0])], axis=1)


def _to_tokens(blk_ref, r, scratch):
    if r == 1:
        return blk_ref[...].astype(F32)
    nt, rows, _ = scratch.shape
    for j in range(r):
        for c in range(nt):
            lanes = slice((j * nt + c) * LANES, (j * nt + c + 1) * LANES)
            scratch.at[c][pl.ds(j, rows // r, stride=r), :] = blk_ref[:, lanes].astype(F32)
    return _read_tokens(scratch)


def _from_tokens(dst_ref, r, scratch):
    nt, rows, _ = scratch.shape
    if r == 1:
        dst_ref[...] = _read_tokens(scratch).astype(dst_ref.dtype)
        return
    for j in range(r):
        for c in range(nt):
            lanes = slice((j * nt + c) * LANES, (j * nt + c + 1) * LANES)
            dst_ref[:, lanes] = scratch.at[c][pl.ds(j, rows // r, stride=r), :].astype(dst_ref.dtype)


def _mm_nt_classes(a, b, bias, dils, *, bm, name):
    M, K = a.shape
    N = b.shape[0]
    dn = (((1,), (1,)), ((), ()))

    def body(a_ref, b_ref, bias_ref, *rest):
        outs, tile = rest[:-1], rest[-1]
        _fill_tokens(tile, lax.dot_general(a_ref[...], b_ref[...], dn, preferred_element_type=F32) + bias_ref[...])
        for out, r in zip(outs, dils):
            _from_tokens(out, r, tile)

    limit = _vmem_limit(_nbytes((bm, K), a.dtype), _nbytes((K, N), b.dtype) // 2, (1 + len(dils)) * _nbytes((bm, N), F32))
    return pl.pallas_call(
        body, name=name, grid=(M // bm,),
        in_specs=[pl.BlockSpec((bm, K), lambda i: (i, 0)),
                  pl.BlockSpec((N, K), lambda i: (0, 0), pipeline_mode=pl.Buffered(1)),
                  pl.BlockSpec((1, N), lambda i: (0, 0))],
        out_specs=tuple(_class_rows_spec(bm, r, N) for r in dils),
        out_shape=tuple(jax.ShapeDtypeStruct((M // r, r * N), BF16) for r in dils),
        scratch_shapes=[_token_scratch(bm, N)],
        compiler_params=pltpu.CompilerParams(dimension_semantics=("parallel",), vmem_limit_bytes=limit),
    )(a, b, bias)


def _merge(branches, dils, sink, *, name):
    W = MIX_W
    T = branches[0][0].shape[0] * dils[0]
    nbr = len(branches)

    def body(*refs):
        sink_ref = refs[2 * nbr]
        out_ref, outb_ref, lse_ref, scratch = refs[2 * nbr + 1:]
        sk = sink_ref[...]
        lses = [_to_tokens(refs[2 * t + 1], dils[t], scratch) for t in range(nbr)]
        mx = sk
        for lse in lses:
            mx = jnp.maximum(mx, lse)
        den = jnp.exp(sk - mx)
        num = jnp.zeros_like(mx)
        for t in range(nbr):
            w = jnp.exp(lses[t] - mx)
            den = den + w
            num = num + w * _to_tokens(refs[2 * t], dils[t], scratch)
        out = num / den
        out_ref[...] = out
        outb_ref[...] = out.astype(outb_ref.dtype)
        lse_ref[...] = mx + jnp.log(den)

    args = [a for br in branches for a in br] + [sink]
    in_specs = [_class_rows_spec(ROW_BLK, r, W) for r in dils for _ in range(2)] + [_vec_spec(W)]
    return pl.pallas_call(
        body, name=name, grid=(T // ROW_BLK,), in_specs=in_specs,
        out_specs=(_row_spec(W), _row_spec(W), _row_spec(W)),
        out_shape=(jax.ShapeDtypeStruct((T, W), F32), jax.ShapeDtypeStruct((T, W), BF16),
                   jax.ShapeDtypeStruct((T, W), F32)),
        scratch_shapes=[_token_scratch(ROW_BLK, W)],
        compiler_params=pltpu.CompilerParams(dimension_semantics=("parallel",)),
    )(*args)


def _attn_delta(dmix, outs, lses, sink, dils, *, name):
    T = dmix.shape[0]
    W = MIX_W
    nd = len(dils)

    def body(dmix_ref, oa_ref, ob_ref, la_ref, lb_ref, sink_ref, *rest):
        doa_ref, sta_ref, dsink_ref = rest[0:3]
        dob_refs, stb_refs = rest[3:3 + nd], rest[3 + nd:3 + 2 * nd]
        do_tile, st_tile = rest[3 + 2 * nd:]

        @pl.when(pl.program_id(0) == 0)
        def _():
            dsink_ref[...] = jnp.zeros_like(dsink_ref)

        sel0, lo = _head_sel(0), _stat_lo()
        for mixer, (o_ref, l_ref) in enumerate(((oa_ref, la_ref), (ob_ref, lb_ref))):
            for i in range(N_PAIRS):
                cols = slice(i * LANES, (i + 1) * LANES)
                do = dmix_ref[:, mixer * W + i * LANES:mixer * W + (i + 1) * LANES]
                prod = do * o_ref[:, cols]
                d0 = jnp.sum(jnp.where(sel0, prod, 0.0), axis=1, keepdims=True)
                d1 = jnp.sum(jnp.where(sel0, 0.0, prod), axis=1, keepdims=True)
                delta = jnp.where(sel0, d0, d1)
                lse = l_ref[:, cols]
                stat = jnp.where(lo, lse, delta)
                if mixer == 0:
                    sta_ref[:, cols] = stat
                    doa_ref[:, cols] = do.astype(doa_ref.dtype)
                    dsink_ref[:, cols] += -jnp.sum(jnp.exp(sink_ref[:, cols] - lse) * delta, axis=0, keepdims=True)
                else:
                    st_tile[i] = stat
                    do_tile[i] = do
        for t, r in enumerate(dils):
            _from_tokens(dob_refs[t], r, do_tile)
            _from_tokens(stb_refs[t], r, st_tile)

    class_specs = [_class_rows_spec(ROW_BLK, r, W) for r in dils]
    out_shape = [jax.ShapeDtypeStruct((T, W), BF16), jax.ShapeDtypeStruct((T, W), F32), jax.ShapeDtypeStruct((1, W), F32)]
    out_shape += [jax.ShapeDtypeStruct((T // r, r * W), BF16) for r in dils]
    out_shape += [jax.ShapeDtypeStruct((T // r, r * W), F32) for r in dils]
    res = pl.pallas_call(
        body, name=name, grid=(T // ROW_BLK,),
        in_specs=[_row_spec(2 * W), _row_spec(W), _row_spec(W), _row_spec(W), _row_spec(W), _vec_spec(W)],
        out_specs=tuple([_row_spec(W), _row_spec(W), _vec_spec(W)] + class_specs + class_specs),
        out_shape=tuple(out_shape),
        scratch_shapes=[_token_scratch(ROW_BLK, W), _token_scratch(ROW_BLK, W)],
        compiler_params=pltpu.CompilerParams(dimension_semantics=("arbitrary",)),
    )(dmix, outs[0], outs[1], lses[0], lses[1], sink)
    return res[0], res[1], res[2], res[3:3 + nd], res[3 + nd:]


def _band_bwd(qa, ka, va, d_out, stat, bias, *, r, q_col, k_col, v_col, stride, pattern, name, kv_shared=False,
              carry=None):
    L = qa.shape[0]
    nb = L // QBLK
    dn_nt = (((1,), (1,)), ((), ()))
    dn_tn = (((0,), (0,)), ((), ()))
    scale = HEAD_DIM ** -0.5

    pps = N_PAIRS if r > 1 else N_PAIRS // 2

    def body(q_ref, k_ref, v_ref, do_ref, st_ref, bias_ref, dq_ref, dk_ref, dv_ref, db_ref, dk_carry, dv_carry):
        @pl.when((pl.program_id(0) == 0) & (pl.program_id(1) == 0))
        def _():
            db_ref[...] = jnp.zeros_like(db_ref)

        lo, sel0 = _stat_lo(), _head_sel(0)
        pair0 = pl.program_id(1) * pps

        def block(b, first):
            rows = _rows(b)
            for i in range(pps):
                heads = pl.ds(2 * (pair0 + i), 2)
                cols = slice(i * LANES, (i + 1) * LANES)
                q2 = _stack_heads(q_ref[rows, cols], scale)
                kv_head = (pair0 + i) // 2 if kv_shared else None
                k = _key_window(k_ref, b, cols, first, kv_head)
                v = _key_window(v_ref, b, cols, first, kv_head)
                do2 = _stack_heads(do_ref[rows, cols])
                st = st_ref[rows, cols]
                lse2 = jnp.concatenate(
                    [jnp.max(jnp.where(_head_sel(h) & lo, st, -jnp.inf), axis=1, keepdims=True) for h in range(2)], axis=0)
                delta2 = jnp.concatenate(
                    [jnp.sum(jnp.where(_head_sel(h) & ~lo, st, 0.0), axis=1, keepdims=True) for h in range(2)],
                    axis=0) * (2.0 / HEAD_DIM)
                bias2 = bias_ref[1 if first else 0, heads].reshape(2 * QBLK, 2 * QBLK)
                s = lax.dot_general(q2, k, dn_nt, preferred_element_type=F32) + bias2
                p = jnp.exp(s - lse2)
                dp = lax.dot_general(do2, v, dn_nt, preferred_element_type=F32)
                ds = p * (dp - delta2)
                db_ref[heads] += ds.reshape(2, QBLK, 2 * QBLK)
                dsb = ds.astype(k.dtype)
                dq2 = jnp.dot(dsb, k, preferred_element_type=F32)
                dq_ref[rows, cols] = (jnp.where(sel0, dq2[0:QBLK], dq2[QBLK:]) * scale).astype(dq_ref.dtype)
                dk_acc = lax.dot_general(dsb, q2, dn_tn, preferred_element_type=F32)
                dv_acc = lax.dot_general(p.astype(k.dtype), do2, dn_tn, preferred_element_type=F32)
                if not first:
                    prev = _rows(b - 1)
                    dk_ref[prev, cols] = (dk_carry[:, cols] + dk_acc[0:QBLK]).astype(dk_ref.dtype)
                    dv_ref[prev, cols] = (dv_carry[:, cols] + dv_acc[0:QBLK]).astype(dv_ref.dtype)
                dk_carry[:, cols] = dk_acc[QBLK:2 * QBLK]
                dv_carry[:, cols] = dv_acc[QBLK:2 * QBLK]

        block(0, True)
        if nb > 1:
            pl.loop(1, nb)(lambda b: block(b, False))

        last = _rows(nb - 1)
        dk_ref[last, :] = dk_carry[...].astype(dk_ref.dtype)
        dv_ref[last, :] = dv_carry[...].astype(dv_ref.dtype)

    tok_spec = _class_spec(L, r, 0, 1, pps)
    bias_spec = pl.BlockSpec((None, 2, 2 * N_PAIRS, QBLK, 2 * QBLK), lambda j, c: (pattern, 0, 0, 0, 0))
    dbias_spec = pl.BlockSpec((2 * N_PAIRS, QBLK, 2 * QBLK), lambda j, c: (0, 0, 0))
    grad = jax.ShapeDtypeStruct((L, r * MIX_W), BF16)
    blk_bytes = _nbytes((L, pps * LANES), BF16)
    carry_shape = pltpu.VMEM((QBLK, pps * LANES), F32)
    return _call(
        body, name=name, grid=(r, N_PAIRS // pps),
        in_specs=[_class_spec(L, r, q_col, stride, pps), _kv_spec(L, r, k_col, stride, pps, kv_shared),
                  _kv_spec(L, r, v_col, stride, pps, kv_shared), tok_spec, tok_spec, bias_spec],
        out_specs=[tok_spec, tok_spec, tok_spec, dbias_spec],
        out_shape=[grad, grad, grad, jax.ShapeDtypeStruct((2 * N_PAIRS, QBLK, 2 * QBLK), F32)],
        args=[qa, ka, va, d_out, stat, bias], scratch=[carry_shape, carry_shape],
        vmem_limit=_vmem_limit(*([blk_bytes] * 9)), semantics=("arbitrary", "arbitrary"), carry=carry)


def _dz_assemble(dq_a, dk_a, dv_a, b_grads, dils, *, name):
    T = dq_a.shape[0]
    W = MIX_W

    def group_sum(x):
        u0 = x[:, 0:LANES] + x[:, LANES:2 * LANES]
        u1 = x[:, 2 * LANES:3 * LANES] + x[:, 3 * LANES:4 * LANES]
        s0 = u0 + pltpu.roll(u0, HEAD_DIM, 1)
        s1 = u1 + pltpu.roll(u1, HEAD_DIM, 1)
        return jnp.where(_head_sel(0), s0, s1)

    def body(*refs):
        dqa_ref, dka_ref, dva_ref = refs[0:3]
        b_refs = refs[3:12]
        dza_ref, dzb_ref, sa_ref, sb_ref, scratch = refs[12:]

        @pl.when(pl.program_id(0) == 0)
        def _():
            sa_ref[...] = jnp.zeros_like(sa_ref)
            sb_ref[...] = jnp.zeros_like(sb_ref)

        def put(dst, acc, col, part):
            w = part.shape[1]
            dst[:, col:col + w] = part.astype(dst.dtype)
            acc[:, col:col + w] += jnp.sum(part, axis=0, keepdims=True)

        put(dza_ref, sa_ref, 0, dqa_ref[...].astype(F32))
        put(dza_ref, sa_ref, W, group_sum(dka_ref[...].astype(F32)))
        put(dza_ref, sa_ref, W + LANES, group_sum(dva_ref[...].astype(F32)))
        for t in range(3):
            part = _to_tokens(b_refs[t], dils[0], scratch)
            for pat in range(1, len(dils)):
                part = part + _to_tokens(b_refs[3 * pat + t], dils[pat], scratch)
            put(dzb_ref, sb_ref, t * W, part)

    args = [dq_a, dk_a, dv_a] + [g[t] for g in b_grads for t in range(3)]
    return pl.pallas_call(
        body, name=name, grid=(T // ROW_BLK,),
        in_specs=[_row_spec(W)] * 3 + [_class_rows_spec(ROW_BLK, r, W) for r in dils for _ in range(3)],
        out_specs=(_row_spec(ZA_W), _row_spec(ZB_W), _vec_spec(ZA_W), _vec_spec(ZB_W)),
        out_shape=(jax.ShapeDtypeStruct((T, ZA_W), BF16), jax.ShapeDtypeStruct((T, ZB_W), BF16),
                   jax.ShapeDtypeStruct((1, ZA_W), F32), jax.ShapeDtypeStruct((1, ZB_W), F32)),
        scratch_shapes=[_token_scratch(ROW_BLK, W)],
        compiler_params=pltpu.CompilerParams(dimension_semantics=("arbitrary",)),
    )(*args)


def _place():
    x, y, c = lax.axis_index("x"), lax.axis_index("y"), lax.axis_index("c")
    chips = [(1 - x, y), (x, 1 - y), (1 - x, 1 - y)]
    return x, y, c, chips


def _cast_place(shards, k_idx, *, name):
    n, steps = len(shards), 4

    def body(k_ref, *refs):
        for s_ref, o_ref in zip(refs[:n], refs[n:]):
            o_ref[...] = s_ref[...].astype(o_ref.dtype)

    blocks = [(a.shape[0] // steps, a.shape[1]) for a in shards]
    grid_spec = pltpu.PrefetchScalarGridSpec(
        num_scalar_prefetch=1, grid=(steps,),
        in_specs=[pl.BlockSpec(blk, lambda t, k_ref: (t, 0)) for blk in blocks],
        out_specs=tuple(pl.BlockSpec(blk, lambda t, k_ref: (k_ref[0] * steps + t, 0)) for blk in blocks))
    return pl.pallas_call(
        body, name=name, grid_spec=grid_spec,
        out_shape=tuple(jax.ShapeDtypeStruct((N_CHIPS * a.shape[0], a.shape[1]), BF16) for a in shards),
        compiler_params=pltpu.CompilerParams(dimension_semantics=("parallel",),
                                             vmem_limit_bytes=_vmem_limit(*[_nbytes(b, F32) for b in blocks])),
    )(k_idx, *shards)


def _gather_carry(fulls, jobs):
    n = len(jobs)

    def piece(ref, chip_idx, half, part):
        rs = ref.shape[0] // N_CHIPS
        rh = rs // 2
        rows = rh // part[1]
        return ref.at[pl.ds(chip_idx * rs + half * rh + part[0] * rows, rows)]

    def copy(sems, sem, src_ref, dst_ref, to):
        return pltpu.make_async_remote_copy(src_ref=src_ref, dst_ref=dst_ref, send_sem=sems[0].at[sem],
                                            recv_sem=sems[1].at[sem], device_id=to, device_id_type=MESH)

    def to_chips(ins, outs, sems, j, arrival, hops=("both", "chips")):
        i, part, hop = jobs[j]
        x, y, c, chips = _place()
        res = []
        for t, chip in enumerate(chips if hop in hops else ()):
            theirs = piece(outs[i], 2 * chip[0] + chip[1], c, part)
            src, dst = (theirs, theirs) if arrival else (piece(ins[i], 2 * x + y, c, part), piece(outs[i], 2 * x + y, c, part))
            res.append(copy(sems, 3 * j + t, src, dst, (*chip, c)))
        return res

    def to_sibling(outs, sems, j, arrival, hops=("both", "sibling")):
        i, part, hop = jobs[j]
        x, y, c, chips = _place()
        res = []
        for t, chip in enumerate(chips if hop in hops else ()):
            region = piece(outs[i], 2 * chip[0] + chip[1], 1 - c if arrival else c, part)
            res.append(copy(sems, 3 * n + 3 * j + t, region, region, (x, y, 1 - c)))
        return res

    def start(ins, outs, sems):
        for j in range(n):
            for send in to_chips(ins, outs, sems, j, False) + to_sibling(outs, sems, j, False, ("sibling",)):
                send.start()

    def before_last(ins, outs, sems):
        for j in range(n):
            for arrival, send in zip(to_chips(ins, outs, sems, j, True, ("both",)),
                                     to_sibling(outs, sems, j, False, ("both",))):
                arrival.wait_recv()
                send.start()

    def finish(ins, outs, sems):
        for j in range(n):
            for arrival in to_chips(ins, outs, sems, j, True, ("chips",)) + to_sibling(outs, sems, j, True):
                arrival.wait_recv()
        for j in range(n):
            for send in to_chips(ins, outs, sems, j, False) + to_sibling(outs, sems, j, False):
                send.wait_send()

    return _Carry(fulls, [jax.ShapeDtypeStruct(a.shape, a.dtype) for a in fulls], {i: i for i in range(len(fulls))},
                  [pltpu.SemaphoreType.DMA((6 * n,)), pltpu.SemaphoreType.DMA((6 * n,))], start, finish, before_last)


def _pair_swap_carry(grads):
    n = len(grads)

    def copy(ins, outs, sems, i):
        x, y, c, _ = _place()
        return pltpu.make_async_remote_copy(src_ref=ins[i].at[:, 1 - c], dst_ref=outs[i], send_sem=sems[0].at[i],
                                            recv_sem=sems[1].at[i], device_id=(x, y, 1 - c), device_id_type=MESH)

    def start(ins, outs, sems):
        for i in range(n):
            copy(ins, outs, sems, i).start()

    def finish(ins, outs, sems):
        for i in range(n):
            copy(ins, outs, sems, i).wait()

    return _Carry(grads, [jax.ShapeDtypeStruct((a.shape[0], a.shape[2], a.shape[3]), a.dtype) for a in grads], {},
                  [pltpu.SemaphoreType.DMA((n,)), pltpu.SemaphoreType.DMA((n,))], start, finish)


def _pair_sum(g4, got, c_idx, *, name):
    _, _, rh, cs = g4.shape
    rb = _row_block(rh, 512, 16)

    def body(c_ref, own_ref, got_ref, o_ref):
        o_ref[...] = (own_ref[...].astype(F32) + got_ref[...].astype(F32)).astype(o_ref.dtype)

    grid_spec = pltpu.PrefetchScalarGridSpec(
        num_scalar_prefetch=1, grid=(N_CHIPS, rh // rb),
        in_specs=[pl.BlockSpec((None, None, rb, cs), lambda k, t, c_ref: (k, c_ref[0], t, 0)),
                  pl.BlockSpec((None, rb, cs), lambda k, t, c_ref: (k, t, 0))],
        out_specs=pl.BlockSpec((None, rb, cs), lambda k, t, c_ref: (k, t, 0)))
    return pl.pallas_call(
        body, name=name, grid_spec=grid_spec, out_shape=jax.ShapeDtypeStruct((N_CHIPS, rh, cs), BF16),
        compiler_params=pltpu.CompilerParams(dimension_semantics=("parallel", "parallel")),
    )(c_idx, g4, got)


def _chip_scatter_carry(sums):
    n = len(sums)

    def copies(ins, outs, sems):
        x, y, c, chips = _place()
        return [pltpu.make_async_remote_copy(src_ref=ins[i].at[2 * chip[0] + chip[1]], dst_ref=outs[i].at[j],
                                             send_sem=sems[0].at[3 * i + j], recv_sem=sems[1].at[3 * i + j],
                                             device_id=(*chip, c), device_id_type=MESH)
                for i in range(n) for j, chip in enumerate(chips)]

    def start(ins, outs, sems):
        for cp in copies(ins, outs, sems):
            cp.start()

    def finish(ins, outs, sems):
        for cp in copies(ins, outs, sems):
            cp.wait()

    return _Carry(sums, [jax.ShapeDtypeStruct((3,) + a.shape[1:], a.dtype) for a in sums], {},
                  [pltpu.SemaphoreType.DMA((3 * n,)), pltpu.SemaphoreType.DMA((3 * n,))], start, finish)


def _chip_sum(sums, gots, kc_idx, *, name):
    n, steps = len(sums), 2

    def body(kc_ref, *refs):
        for own_ref, got_ref, o_ref in zip(refs[:n], refs[n:2 * n], refs[2 * n:]):
            acc = own_ref[...].astype(F32)
            for j in range(3):
                acc = acc + got_ref[j].astype(F32)
            o_ref[...] = acc

    blocks = [(a.shape[1] // steps, a.shape[2]) for a in sums]
    grid_spec = pltpu.PrefetchScalarGridSpec(
        num_scalar_prefetch=1, grid=(steps,),
        in_specs=[pl.BlockSpec((None,) + blk, lambda t, kc: (kc[0], t, 0)) for blk in blocks]
        + [pl.BlockSpec((3,) + blk, lambda t, kc: (0, t, 0)) for blk in blocks],
        out_specs=tuple(pl.BlockSpec(blk, lambda t, kc: (kc[1] * steps + t, 0)) for blk in blocks))
    return pl.pallas_call(
        body, name=name, grid_spec=grid_spec,
        out_shape=tuple(jax.ShapeDtypeStruct((2 * a.shape[1], a.shape[2]), F32) for a in sums),
        compiler_params=pltpu.CompilerParams(dimension_semantics=("parallel",),
                                             vmem_limit_bytes=_vmem_limit(*[3 * _nbytes(b, F32) for b in blocks])),
    )(kc_idx, *sums, *gots)


def _half_swap_carry(shards):
    n = len(shards)

    def copy(ins, outs, sems, i, to_my_half):
        x, y, c, _ = _place()
        rh = ins[i].shape[0] // 2
        half = c if to_my_half else 1 - c
        return pltpu.make_async_remote_copy(
            src_ref=ins[i].at[pl.ds(c * rh, rh)], dst_ref=outs[i].at[pl.ds(half * rh, rh)],
            send_sem=sems[0].at[i], recv_sem=sems[1].at[i], device_id=(x, y, 1 - c), device_id_type=MESH)

    def start(ins, outs, sems):
        for i in range(n):
            copy(ins, outs, sems, i, True).start()

    def finish(ins, outs, sems):
        for i in range(n):
            copy(ins, outs, sems, i, False).wait_recv()
        for i in range(n):
            copy(ins, outs, sems, i, True).wait_send()

    return _Carry(shards, [jax.ShapeDtypeStruct(a.shape, a.dtype) for a in shards], {i: i for i in range(n)},
                  [pltpu.SemaphoreType.DMA((n,)), pltpu.SemaphoreType.DMA((n,))], start, finish)


SMALL_ROW = 1024


def _small_layout(shapes):
    starts, row = [], 0
    for r, c in shapes:
        starts.append(row)
        row += -(-c // SMALL_ROW) if r == 1 else r
    return starts, -(-row // SUBLANES) * SUBLANES


def _small_pieces(shape, start):
    r, c = shape
    if r == 1:
        return [(start + q, min(SMALL_ROW, c - q * SMALL_ROW), q * SMALL_ROW) for q in range(-(-c // SMALL_ROW))]
    return None


def _whole_spec(shape):
    return pl.BlockSpec(tuple(shape), lambda i: (0,) * len(shape))


def _small_all_reduce(parts, *, name, carry=None):
    shapes = [a.shape for a in parts]
    starts, rows = _small_layout(shapes)
    n = len(parts)
    n_tables = sum(1 for r, _ in shapes if r > 1)
    assert all(r > 1 and c <= LANES for r, c in shapes[:n_tables]) and all(r == 1 for r, _ in shapes[n_tables:])
    table_rows = starts[n_tables]
    assert table_rows % SUBLANES == 0

    def regions(slot):
        return slot.at[pl.ds(0, table_rows), pl.ds(0, LANES)], slot.at[pl.ds(table_rows, rows - table_rows)]

    def peer(d):
        x, y, c, _ = _place()
        return 1 - x if d & 4 else x, 1 - y if d & 2 else y, 1 - c if d & 1 else c

    def slot_of(d):
        px, py, pc = peer(d)
        return 4 * px + 2 * py + pc

    def copy(refs, d, part, dst_slot):
        buf, send_sems, recv_sems = refs[2 * n:]
        sem = 2 * (d - 1) + part
        return pltpu.make_async_remote_copy(
            src_ref=regions(buf.at[slot_of(0)])[part], dst_ref=regions(buf.at[dst_slot])[part],
            send_sem=send_sems.at[sem], recv_sem=recv_sems.at[sem], device_id=peer(d), device_id_type=MESH)

    def send(*refs):
        ins, mine = refs[0:n], refs[2 * n].at[slot_of(0)]
        mine[...] = jnp.zeros((rows, SMALL_ROW), F32)
        for ref, shape, start in zip(ins, shapes, starts):
            pieces = _small_pieces(shape, start)
            if pieces is None:
                mine[start:start + shape[0], 0:shape[1]] = ref[...]
            else:
                for row, width, col in pieces:
                    mine[row:row + 1, 0:width] = ref[:, col:col + width]
        for d in range(1, 8):
            for part in range(2):
                copy(refs, d, part, slot_of(0)).start()

    def receive(*refs):
        outs, buf = refs[n:2 * n], refs[2 * n]
        mine = buf.at[slot_of(0)]
        for d in range(1, 8):
            for part in range(2):
                copy(refs, d, part, slot_of(d)).wait_recv()
        for d in range(1, 8):
            for part in range(2):
                copy(refs, d, part, slot_of(0)).wait_send()
        tables, vectors = regions(buf.at[0])
        acc_t, acc_v = tables[...], vectors[...]
        for s in range(1, 8):
            tables, vectors = regions(buf.at[s])
            acc_t, acc_v = acc_t + tables[...], acc_v + vectors[...]
        tables, vectors = regions(mine)
        tables[...] = acc_t
        vectors[...] = acc_v
        for ref, shape, start in zip(outs, shapes, starts):
            pieces = _small_pieces(shape, start)
            if pieces is None:
                ref[...] = mine[start:start + shape[0], 0:shape[1]]
            else:
                for row, width, col in pieces:
                    ref[:, col:col + width] = mine[row:row + 1, 0:width]

    def body(*refs):
        send(*refs)
        receive(*refs)

    whole = [_whole_spec(s) for s in shapes]
    return _call(body if carry is None else receive, before_carry=None if carry is None else send,
                 name=name, grid=(1,), in_specs=whole, out_specs=whole,
                 out_shape=[jax.ShapeDtypeStruct(s, F32) for s in shapes], args=parts,
                 scratch=[pltpu.VMEM((8, rows, SMALL_ROW), F32), pltpu.SemaphoreType.DMA((14,)),
                          pltpu.SemaphoreType.DMA((14,))], carry=carry)


def _adamw_small(ws, gs, ms, vs, *, name):
    n = len(ws)
    c1 = 1.0 - ADAM_B1 ** ADAM_STEP
    c2 = 1.0 - ADAM_B2 ** ADAM_STEP

    def body(*refs):
        for k in range(n):
            w_ref, g_ref, m_ref, v_ref = (refs[t * n + k] for t in range(4))
            go_ref, d_ref, mo_ref, vo_ref = (refs[(4 + t) * n + k] for t in range(4))
            gv = g_ref[...]
            go_ref[...] = gv
            mn = ADAM_B1 * m_ref[...] + (1.0 - ADAM_B1) * gv
            vn = ADAM_B2 * v_ref[...] + (1.0 - ADAM_B2) * (gv * gv)
            d_ref[...] = -ADAM_LR * ((mn / c1) / (jnp.sqrt(vn / c2) + ADAM_EPS) + ADAM_WD * w_ref[...])
            mo_ref[...] = mn
            vo_ref[...] = vn

    whole = [_whole_spec(a.shape) for a in ws]
    shapes = tuple(jax.ShapeDtypeStruct(a.shape, F32) for a in ws)
    res = pl.pallas_call(
        body, name=name, grid=(1,), in_specs=whole * 4, out_specs=tuple(whole * 4), out_shape=shapes * 4,
    )(*ws, *gs, *ms, *vs)
    return res[0:n], res[n:2 * n], res[2 * n:3 * n], res[3 * n:4 * n]


def _adamw(ws, gs, ms, vs, *, name):
    n, steps = len(ws), 8
    c1 = 1.0 - ADAM_B1 ** ADAM_STEP
    c2 = 1.0 - ADAM_B2 ** ADAM_STEP

    def body(*refs):
        for k in range(n):
            w_ref, g_ref, m_ref, v_ref = (refs[t * n + k] for t in range(4))
            go_ref, d_ref, mo_ref, vo_ref = (refs[(4 + t) * n + k] for t in range(4))
            gv = g_ref[...]
            go_ref[...] = gv
            mn = ADAM_B1 * m_ref[...] + (1.0 - ADAM_B1) * gv
            vn = ADAM_B2 * v_ref[...] + (1.0 - ADAM_B2) * (gv * gv)
            d_ref[...] = -ADAM_LR * ((mn / c1) / (jnp.sqrt(vn / c2) + ADAM_EPS) + ADAM_WD * w_ref[...])
            mo_ref[...] = mn
            vo_ref[...] = vn

    blocks = [(a.shape[0] // steps, a.shape[1]) for a in ws]
    specs = [pl.BlockSpec(blk, lambda i: (i, 0)) for blk in blocks]
    shapes = tuple(jax.ShapeDtypeStruct(a.shape, F32) for a in ws)
    res = pl.pallas_call(
        body, name=name, grid=(steps,), in_specs=specs * 4, out_specs=tuple(specs * 4), out_shape=shapes * 4,
        compiler_params=pltpu.CompilerParams(dimension_semantics=("parallel",),
                                             vmem_limit_bytes=_vmem_limit(*[8 * _nbytes(b, F32) for b in blocks])),
    )(*ws, *gs, *ms, *vs)
    return res[0:n], res[n:2 * n], res[2 * n:3 * n], res[3 * n:4 * n]


def _local_step(x, p, target, small, sched):
    T = x.shape[0]
    W = MIX_W
    rel_bias = small["rel_bias"]
    b_in_a, b_in_b = small["b_in"][:, 0:ZA_W], small["b_in"][:, ZA_W:]

    idx_np = [_t5_index(A_WINDOW - 1, 1)] + [_t5_index(window // dil, dil) for window, dil in B_PATTERNS]
    idx_all = jnp.asarray(np.stack(idx_np))
    n1, bias_all = sched.run(_prologue, x, small["ffn1_pre_g"], rel_bias, idx_all, name="prologue")
    w_gu1 = sched.weight("ffn1_w_gu")
    *gu1, a1 = sched.run(_ffn_up, n1, w_gu1, bm=512, name="ffn1_gu")
    w_d1 = sched.weight("ffn1_w_down")
    f1, h1, n2 = sched.run(_mm_resid_norm, a1, w_d1, x, small["ffn1_post_g"], 0.5, small["attn_pre_g"], bm=512,
                           name="ffn1_down")
    win_a, win_b = sched.weight("w_in")
    za = _mm_nt(n2, win_a, bm=1024, bn=ZA_W, out_dtype=BF16, name="in_proj_a", add=b_in_a)
    dils = tuple(dil for _, dil in B_PATTERNS)
    zb_by_dil = _mm_nt_classes(n2, win_b, b_in_b, dils, bm=512, name="in_proj_b")

    a_args = dict(r=1, q_col=0, k_col=W // LANES, v_col=W // LANES + 1, stride=0, pattern=0, kv_shared=True)
    sink_row = jnp.repeat(small["sinks"], HEAD_DIM, axis=1)
    out_a, out_a_bf, lse_a = sched.run(_band_fwd, za, za, za, bias_all, name="band_a_fwd", sink=sink_row, **a_args)

    no_sink = jnp.full((1, W), NEG, F32)
    b_ctx, branches = [], []
    for t, dil in enumerate(dils):
        zb_r = zb_by_dil[t]
        args = dict(r=dil, q_col=0, k_col=1, v_col=2, stride=ZB_W // W, pattern=1 + t)
        branches.append(sched.run(_band_fwd, zb_r, zb_r, zb_r, bias_all, name=f"band_b{t}_fwd", **args))
        b_ctx.append((jnp.asarray(idx_np[1 + t]), zb_r, args))
    out_b, out_b_bf, lse_b = _merge(branches, dils, no_sink, name="merge_b")

    mix = jnp.concatenate([out_a_bf, out_b_bf], axis=1)
    w_out = sched.weight("w_out")
    att, h2, n3 = _mm_resid_norm(mix, w_out, h1, small["attn_post_g"], 1.0, small["ffn2_pre_g"], bm=512,
                                 name="out_proj", bias=small["b_out"])
    w_gu2, w_d2 = sched.weight("ffn2_w_gu"), sched.weight("ffn2_w_down")
    *gu2, a2 = _ffn_up(n3, w_gu2, bm=512, name="ffn2_gu")
    f2, h3, n4 = _mm_resid_norm(a2, w_d2, h2, small["ffn2_post_g"], 0.5, small["ple_pre_g"], bm=512,
                                name="ffn2_down")
    w_gate = sched.weight("w_ple_gate")
    p_bf = p.astype(BF16)
    dh4, de, dgp, loss, d_ple_post = _ple_head(n4, w_gate, p_bf, sched.weight("w_ple_proj"), h3, small["ple_post_g"],
                                               target, bm=512, name="ple_head")

    gs = {"ple_post_g": d_ple_post}
    sched.grad("w_ple_proj", _mm_tn(p_bf, de, bm=256, bn=1024, out_dtype=BF16, name="d_w_ple"))
    sched.grad("w_ple_gate", _mm_tn(n4, dgp, bm=512, bn=1024, out_dtype=BF16, name="d_w_gate"))
    dh3, df2, gs["ple_pre_g"], gs["ffn2_post_g"], _ = sched.run(
        _mm_nt_norms_bwd, dgp, w_gate, dh4, h3, small["ple_pre_g"], f2, small["ffn2_post_g"], 0.5, bm=512, name="d_n4")

    def ffn_bwd(df, a, gu, n, w_down, w_gu, tag, *norm_args):
        dgu = sched.run(_ffn_down_bwd, df, w_down, gu[0], gu[1], bm=512, name=f"ffn{tag}_d_a")
        sched.grad(f"ffn{tag}_w_gu", _mm_tn(n, dgu, bm=512, bn=1408, out_dtype=BF16, name=f"ffn{tag}_d_w_gu",
                                            n_stack=N_CHIPS))
        sched.grad(f"ffn{tag}_w_down", sched.run(_mm_tn, a, df, bm=256, bn=1024, out_dtype=BF16,
                                                name=f"ffn{tag}_d_w_down"))
        return sched.run(_mm_nt_norms_bwd, dgu, w_gu, *norm_args, bm=512, name=f"ffn{tag}_d_n")

    dh2, datt, gs["ffn2_pre_g"], gs["attn_post_g"], gs["b_out"] = ffn_bwd(
        df2, a2, gu2, n3, w_d2, w_gu2, "2", dh3, h2, small["ffn2_pre_g"], att, small["attn_post_g"], 1.0)
    sched.grad("w_out", _mm_tn(mix, datt, bm=512, bn=1024, out_dtype=BF16, name="d_w_out"))
    dmix = sched.run(_mm_nt, datt, w_out, bm=1024, bn=1024, out_dtype=F32, name="d_mix")

    do_a, stat_a, dsink, do_b, stat_b = _attn_delta(dmix, (out_a, out_b), (lse_a, lse_b), sink_row, dils,
                                                    name="attn_delta")
    gs["sinks"] = dsink[:, ::HEAD_DIM]
    dq_a, dk_a, dv_a, dbias_a = sched.run(_band_bwd, za, za, za, do_a, stat_a, bias_all, name="band_a_bwd", **a_args)
    d_rel_a = _bias_grad(dbias_a, jnp.asarray(idx_np[0]), name="d_rel_a")
    b_grads = []
    d_rel_b = None
    for t, (idx, zb_r, args) in enumerate(b_ctx):
        dq, dk, dv, dbias = sched.run(_band_bwd, zb_r, zb_r, zb_r, do_b[t], stat_b[t], bias_all,
                                      name=f"band_b{t}_bwd", **args)
        b_grads.append((dq, dk, dv))
        d_rel = _bias_grad(dbias, idx, name=f"d_rel_b{t}")
        d_rel_b = d_rel if d_rel_b is None else d_rel_b + d_rel
    gs["rel_bias"] = jnp.concatenate([d_rel_a[:, 0:2 * N_PAIRS], d_rel_b[:, 0:2 * N_PAIRS]], axis=1)
    dza, dzb, dsum_a, dsum_b = _dz_assemble(dq_a, dk_a, dv_a, b_grads, dils, name="d_z")
    gs["b_in"] = jnp.concatenate([dsum_a, dsum_b], axis=1)
    sched.grad("w_in", (_mm_tn(dza, n2, bm=ZA_W, bn=1024, out_dtype=BF16, name="d_w_in_a"),
                        _mm_tn(dzb, n2, bm=ZB_W // 2, bn=1024, out_dtype=BF16, name="d_w_in_b")))
    dn2_a = _mm_nn(dza, win_a, bm=1024, bn=1024, out_dtype=F32, name="d_n2_a")
    dh1, df1, gs["attn_pre_g"], gs["ffn1_post_g"], _ = sched.run(
        _mm_nt_norms_bwd, dzb, win_b, dh2, h1, small["attn_pre_g"], f1, small["ffn1_post_g"], 0.5, bm=512,
        name="d_n2_b", add=dn2_a, b_is_kn=True)
    grad_x, gs["ffn1_pre_g"] = ffn_bwd(df1, a1, gu1, n1, w_d1, w_gu1, "1", dh1, x, small["ffn1_pre_g"], None, None, 0.0)
    return loss, grad_x, gs


def _to_compute(name, full):
    st = full.reshape(N_CHIPS, full.shape[0] // N_CHIPS, full.shape[1])
    if name in ("ffn1_w_gu", "ffn2_w_gu"):
        return st
    if name == "w_ple_proj":
        return jnp.transpose(st, (1, 0, 2)).reshape(st.shape[1], -1)
    if name == "w_in":
        return full[0:ZA_W], full[ZA_W:]
    return full


def _to_comm(name, g):
    if name == "w_in":
        g = jnp.concatenate(g, axis=0)
    if name == "w_ple_proj":
        g = jnp.transpose(g.reshape(g.shape[0], N_CHIPS, -1), (1, 0, 2))
    rs = g.shape[1] if g.ndim == 3 else g.shape[0] // N_CHIPS
    return g.reshape(N_CHIPS, 2, rs // 2, g.shape[-1])


class _Schedule:
    GATHER = {
        "prologue": [("ffn1_w_gu", (0, 1), "both")],
        "ffn1_gu": [("ffn1_w_down", (0, 1), "both"), ("w_in", (0, 1), "both")],
        "band_a_fwd": [("ffn2_w_gu", (0, 2), "chips"), ("w_out", (0, 1), "chips")],
        "band_b0_fwd": [("ffn2_w_gu", (0, 2), "sibling"), ("w_out", (0, 1), "sibling"),
                        ("ffn2_w_gu", (1, 2), "chips"), ("w_ple_gate", (0, 1), "chips"), ("w_ple_proj", (0, 1), "chips")],
        "band_b1_fwd": [("ffn2_w_gu", (1, 2), "sibling"), ("w_ple_gate", (0, 1), "sibling"),
                        ("w_ple_proj", (0, 1), "sibling"), ("ffn2_w_down", (0, 1), "both")],
    }
    SWAP = {"ffn2_d_w_down": ["w_ple_proj", "w_ple_gate", "ffn2_w_gu"], "ffn2_d_n": ["ffn2_w_down"],
            "band_a_bwd": ["w_out"], "d_n2_b": ["w_in"], "ffn1_d_w_down": ["ffn1_w_gu"], "ffn1_d_n": ["ffn1_w_down"]}
    SCATTER = {"ffn2_d_n": ["ffn2_w_gu"], "band_a_bwd": ["w_ple_proj", "w_ple_gate", "ffn2_w_down"],
               "d_n2_b": ["w_out"], "ffn1_d_w_down": ["w_in"], "ffn1_d_n": ["ffn1_w_gu"]}
    HALF_SWAP = {"ffn1_d_n": ["w_ple_proj", "w_ple_gate", "ffn2_w_gu", "ffn2_w_down", "w_out", "w_in"]}

    def __init__(self, placed, c_idx, kc_idx):
        self.full = dict(placed)
        self.missing = {n: 1.0 for n in placed}
        self.c_idx, self.kc_idx = c_idx, kc_idx
        self.grads, self.swapped, self.pair_sums, self.scattered, self.summed = {}, {}, {}, {}, {}

    def _gather(self, entries):
        if not entries:
            return [], []
        names = list(dict.fromkeys(n for n, _, _ in entries))
        carry = _gather_carry([self.full[n] for n in names], [(names.index(n), pt, hops) for n, pt, hops in entries])

        def done():
            self.full.update(zip(names, carry.results))
            for n, part, hops in entries:
                if hops != "chips":
                    self.missing[n] -= 1.0 / part[1]
        return [carry], [done]

    def weight(self, name):
        assert abs(self.missing[name]) < 1e-9, (name, self.missing[name])
        return _to_compute(name, self.full[name])

    def grad(self, name, g):
        self.grads[name] = _to_comm(name, g)

    def _swap(self, names):
        carry = _pair_swap_carry([self.grads[n] for n in names])

        def done():
            self.swapped.update(zip(names, carry.results))
        return carry, done

    def _scatter(self, names):
        for n in names:
            self.pair_sums[n] = _pair_sum(self.grads[n], self.swapped[n], self.c_idx, name=f"grad_pair_sum_{n}")
        carry = _chip_scatter_carry([self.pair_sums[n] for n in names])

        def done():
            self.scattered.update(zip(names, carry.results))
        return carry, done

    def _half_swap(self, names):
        halves = _chip_sum([self.pair_sums[n] for n in names], [self.scattered[n] for n in names], self.kc_idx,
                           name=f"grad_chip_sum_{names[0]}")
        carry = _half_swap_carry(halves)

        def done():
            self.summed.update(zip(names, carry.results))
        return carry, done

    def run(self, fn, *args, name, **kw):
        carries, after = self._gather(self.GATHER.get(name, []))
        for names, make in ((self.SWAP.get(name), self._swap), (self.SCATTER.get(name), self._scatter),
                            (self.HALF_SWAP.get(name), self._half_swap)):
            if names:
                carry, done = make(names)
                carries.append(carry)
                after.append(done)
        out = fn(*args, name=name, carry=_join(carries), **kw)
        for done in after:
            done()
        return out

    def finish(self, last_carrier):
        left = [n for n in BIG if n not in self.scattered]
        to_swap = [n for n in left if n not in self.swapped]
        if to_swap:
            carry, done = self._swap(to_swap)
            _comm_call(carry, name="grad_pair_swap")
            done()
        carry, done = self._scatter(left) if left else (None, lambda: None)
        carried = last_carrier(carry=carry)
        done()
        carry, done = self._half_swap([n for n in BIG if n not in self.summed])
        _comm_call(carry, name="grad_half_swap")
        done()
        return self.summed, carried


def kernel(x, p, rel_bias, ffn1_pre_g, ffn1_w_gu, ffn1_w_down, ffn1_post_g, attn_pre_g, w_in, b_in, sinks, w_out, b_out, attn_post_g, ffn2_pre_g, ffn2_w_gu, ffn2_w_down, ffn2_post_g, ple_pre_g, w_ple_gate, w_ple_proj, ple_post_g, loss_target, m_rel_bias, m_ffn1_pre_g, m_ffn1_w_gu, m_ffn1_w_down, m_ffn1_post_g, m_attn_pre_g, m_w_in, m_b_in, m_sinks, m_w_out, m_b_out, m_attn_post_g, m_ffn2_pre_g, m_ffn2_w_gu, m_ffn2_w_down, m_ffn2_post_g, m_ple_pre_g, m_w_ple_gate, m_w_ple_proj, m_ple_post_g, v_rel_bias, v_ffn1_pre_g, v_ffn1_w_gu, v_ffn1_w_down, v_ffn1_post_g, v_attn_pre_g, v_w_in, v_b_in, v_sinks, v_w_out, v_b_out, v_attn_post_g, v_ffn2_pre_g, v_ffn2_w_gu, v_ffn2_w_down, v_ffn2_post_g, v_ple_pre_g, v_w_ple_gate, v_w_ple_proj, v_ple_post_g):
    given = dict(locals())
    w = {n: given[n] for n in WEIGHTS}
    m = {n: given["m_" + n] for n in WEIGHTS}
    v = {n: given["v_" + n] for n in WEIGHTS}
    c_pos = lax.axis_index("c").astype(jnp.int32)
    k_pos = (2 * lax.axis_index("x") + lax.axis_index("y")).astype(jnp.int32)
    c_idx, k_idx, kc_idx = c_pos.reshape(1), k_pos.reshape(1), jnp.stack([k_pos, c_pos])

    def shard(a, n):
        return jnp.transpose(a[0]) if n == "w_in" else a[0]

    def unshard(a, n):
        return (jnp.transpose(a) if n == "w_in" else a)[None]

    placed = _cast_place([shard(w[n], n) for n in BIG], k_idx, name="place_shards")
    sched = _Schedule(dict(zip(BIG, placed)), c_idx, kc_idx)
    small = {n: (w[n] if n == "rel_bias" else w[n].reshape(1, -1)) for n in SMALL}

    loss_part, grad_x, gs = _local_step(x[0], p[0, 0], loss_target[0], small, sched)

    grads_big, (*small_grads, loss_row) = sched.finish(
        partial(_small_all_reduce, [gs[n] for n in SMALL] + [loss_part], name="small_all_reduce"))
    loss = loss_row[0, 0]

    grads, delta, new_m, new_v = {}, {}, {}, {}
    res = _adamw([shard(w[n], n) for n in BIG], [grads_big[n] for n in BIG], [shard(m[n], n) for n in BIG],
                 [shard(v[n], n) for n in BIG], name="adamw_big")
    for n, gg, dd, mm, vv in zip(BIG, *res):
        grads[n], delta[n], new_m[n], new_v[n] = (unshard(a, n) for a in (gg, dd, mm, vv))
    res = _adamw_small([w[n] for n in SMALL], small_grads, [m[n] for n in SMALL], [v[n] for n in SMALL],
                       name="adamw_small")
    for name, gg, dd, mm, vv in zip(SMALL, *res):
        grads[name], delta[name], new_m[name], new_v[name] = gg, dd, mm, vv

    return (loss, grad_x[None], *[grads[n] for n in WEIGHTS], *[delta[n] for n in WEIGHTS],
            *[new_m[n] for n in WEIGHTS], *[new_v[n] for n in WEIGHTS])
```

```python
import math
from functools import partial

import jax
import jax.numpy as jnp
import numpy as np
from jax import lax
from jax.experimental import pallas as pl
from jax.experimental.pallas import tpu as pltpu

F32 = jnp.float32
BF16 = jnp.bfloat16
MESH = pl.DeviceIdType.MESH

EPS = 1e-6
NEG = -1e30
LANES = 128
SUBLANES = 8
HEAD_DIM = 64
QBLK = 128
N_PAIRS = 4
MIX_W = N_PAIRS * LANES
A_WINDOW = 128
B_PATTERNS = ((128, 1), (512, 4), (2048, 16))
NUM_BUCKETS = 32
MAX_DISTANCE = 2048
ZA_W = 768
ZB_W = 1536
N_CHIPS = 4
VMEM_BYTES_V7X = 64 * 2**20
VMEM_CLAIM = VMEM_BYTES_V7X - (6 << 20)

ADAM_LR, ADAM_B1, ADAM_B2, ADAM_EPS, ADAM_WD, ADAM_STEP = 0.001, 0.9, 0.999, 1e-08, 0.01, 10

BIG = ("ffn1_w_gu", "ffn1_w_down", "w_in", "w_out", "ffn2_w_gu", "ffn2_w_down", "w_ple_gate", "w_ple_proj")
SMALL = ("rel_bias", "ffn1_pre_g", "ffn1_post_g", "attn_pre_g", "b_in", "sinks", "b_out", "attn_post_g",
         "ffn2_pre_g", "ffn2_post_g", "ple_pre_g", "ple_post_g")
WEIGHTS = ("rel_bias", "ffn1_pre_g", "ffn1_w_gu", "ffn1_w_down", "ffn1_post_g", "attn_pre_g", "w_in", "b_in",
           "sinks", "w_out", "b_out", "attn_post_g", "ffn2_pre_g", "ffn2_w_gu", "ffn2_w_down", "ffn2_post_g",
           "ple_pre_g", "w_ple_gate", "w_ple_proj", "ple_post_g")


def _vmem_limit(*block_bytes):
    del block_bytes
    return VMEM_CLAIM


def _nbytes(shape, dtype):
    return int(np.prod(shape)) * jnp.dtype(dtype).itemsize


MXU_WIDTH_V7X = 256


def _col_chunks(n):
    return [slice(c, min(c + MXU_WIDTH_V7X, n)) for c in range(0, n, MXU_WIDTH_V7X)]


def _row_halves(rows):
    return [slice(0, rows // 2), slice(rows // 2, rows)]


def _row_block(rows, cap, mult):
    for cand in range(min(rows, cap), 0, -1):
        if rows % cand == 0 and cand % mult == 0:
            return cand
    raise ValueError((rows, cap, mult))


class _Carry:
    def __init__(self, operands, out_shapes, aliases, sems, start, finish, before_last=None):
        self.operands, self.out_shapes, self.aliases, self.sems = list(operands), list(out_shapes), dict(aliases), list(sems)
        self.start, self.finish = start, finish
        self.before_last = before_last or (lambda ins, outs, sems: None)
        self.results = None


def _join(carries):
    carries = [c for c in carries if c is not None]
    if not carries:
        return None
    if len(carries) == 1:
        return carries[0]
    offs, pos = [], [0, 0, 0]
    for c in carries:
        offs.append(tuple(pos))
        pos = [pos[0] + len(c.operands), pos[1] + len(c.out_shapes), pos[2] + len(c.sems)]
    aliases = {}
    for c, (oi, oo, _) in zip(carries, offs):
        aliases.update({oi + i: oo + o for i, o in c.aliases.items()})

    def run(which):
        def fn(ins, outs, sems):
            for c, (oi, oo, os_) in zip(carries, offs):
                getattr(c, which)(ins[oi:oi + len(c.operands)], outs[oo:oo + len(c.out_shapes)],
                                  sems[os_:os_ + len(c.sems)])
        return fn

    joined = _Carry([a for c in carries for a in c.operands], [s for c in carries for s in c.out_shapes], aliases,
                    [s for c in carries for s in c.sems], run("start"), run("finish"), run("before_last"))
    joined.parts = (carries, offs)
    return joined


def _set_results(carry, outs):
    carry.results = list(outs)
    if hasattr(carry, "parts"):
        for c, (_, oo, _) in zip(*carry.parts):
            _set_results(c, outs[oo:oo + len(c.out_shapes)])


ANY = pl.BlockSpec(memory_space=pl.ANY)


def _call(body, *, name, grid, in_specs, out_specs, out_shape, args, scratch=(), vmem_limit=None, carry=None,
          semantics=None, before_carry=None):
    assert before_carry is None or carry is not None
    n_ci, n_co, n_cs = len(args), len(out_shape), len(scratch)
    semantics = semantics or ("parallel",) * len(grid)
    vmem_limit = vmem_limit or VMEM_CLAIM
    if carry is None:
        res = pl.pallas_call(
            body, name=name, grid=grid, in_specs=list(in_specs), out_specs=tuple(out_specs), out_shape=tuple(out_shape),
            scratch_shapes=list(scratch),
            compiler_params=pltpu.CompilerParams(dimension_semantics=semantics, vmem_limit_bytes=vmem_limit),
        )(*args)
        return list(res)
    n_pi, n_po = len(carry.operands), len(carry.out_shapes)

    def full_body(*refs):
        ci, pi = refs[:n_ci], refs[n_ci:n_ci + n_pi]
        o0 = n_ci + n_pi
        co, po = refs[o0:o0 + n_co], refs[o0 + n_co:o0 + n_co + n_po]
        s0 = o0 + n_co + n_po
        cs, ps = refs[s0:s0 + n_cs], refs[s0 + n_cs:]
        ids = [pl.program_id(ax) for ax in range(len(grid))]
        first, last = ids[0] == 0, ids[0] == grid[0] - 1
        for ax in range(1, len(grid)):
            first, last = first & (ids[ax] == 0), last & (ids[ax] == grid[ax] - 1)

        @pl.when(first)
        def _():
            if before_carry is not None:
                before_carry(*ci, *co, *cs)
            carry.start(pi, po, ps)

        several_steps = any(g > 1 for g in grid)
        if several_steps:
            @pl.when(last)
            def _():
                carry.before_last(pi, po, ps)

        body(*ci, *co, *cs)

        @pl.when(last)
        def _():
            if not several_steps:
                carry.before_last(pi, po, ps)
            carry.finish(pi, po, ps)

    res = pl.pallas_call(
        full_body, name=name, grid=grid, in_specs=list(in_specs) + [ANY] * n_pi,
        out_specs=tuple(out_specs) + tuple([ANY] * n_po), out_shape=tuple(out_shape) + tuple(carry.out_shapes),
        scratch_shapes=list(scratch) + carry.sems,
        input_output_aliases={n_ci + i: n_co + o for i, o in carry.aliases.items()},
        compiler_params=pltpu.CompilerParams(dimension_semantics=("arbitrary",) * len(grid),
                                             vmem_limit_bytes=vmem_limit),
    )(*args, *carry.operands)
    _set_results(carry, res[n_co:])
    return list(res[:n_co])


def _comm_call(carry, *, name):
    n_pi, n_po = len(carry.operands), len(carry.out_shapes)

    def body(*refs):
        pi, po, ps = refs[:n_pi], refs[n_pi:n_pi + n_po], refs[n_pi + n_po:]
        carry.start(pi, po, ps)
        carry.before_last(pi, po, ps)
        carry.finish(pi, po, ps)

    res = pl.pallas_call(
        body, name=name, in_specs=[ANY] * n_pi, out_specs=tuple([ANY] * n_po), out_shape=tuple(carry.out_shapes),
        scratch_shapes=carry.sems, input_output_aliases=dict(carry.aliases),
    )(*carry.operands)
    _set_results(carry, res)


def _mm_nn(a, b, *, bm, bn, out_dtype, name, bias=None, carry=None):
    M, K = a.shape
    stacked = b.ndim == 3
    N = b.shape[0] * b.shape[2] if stacked else b.shape[1]
    assert M % bm == 0 and N % bn == 0 and (not stacked or bn == b.shape[2])

    def body(*refs):
        a_ref, b_ref = refs[0], refs[1]
        o_ref = refs[-1]
        acc = jnp.dot(a_ref[...], b_ref[...], preferred_element_type=F32)
        if bias is not None:
            acc = acc + refs[2][...]
        o_ref[...] = acc.astype(o_ref.dtype)

    if stacked:
        b_spec = pl.BlockSpec((None, K, bn), lambda i, j: (j, 0, 0))
    else:
        b_spec = pl.BlockSpec((K, bn), lambda i, j: (0, j))
    in_specs = [pl.BlockSpec((bm, K), lambda i, j: (i, 0)), b_spec]
    args = [a, b]
    if bias is not None:
        in_specs.append(pl.BlockSpec((1, bn), lambda i, j: (0, j)))
        args.append(bias)
    limit = _vmem_limit(_nbytes((bm, K), a.dtype), _nbytes((K, bn), b.dtype), _nbytes((bm, bn), F32))
    return _call(body, name=name, grid=(M // bm, N // bn), in_specs=in_specs,
                 out_specs=[pl.BlockSpec((bm, bn), lambda i, j: (i, j))],
                 out_shape=[jax.ShapeDtypeStruct((M, N), out_dtype)], args=args, vmem_limit=limit, carry=carry)[0]


def _mm_nt(a, b, *, bm, bn, out_dtype, name, add=None, carry=None):
    M, K = a.shape
    stacked = b.ndim == 3
    N = b.shape[1] if stacked else b.shape[0]
    n_sh = b.shape[0] if stacked else 1
    ks = b.shape[2] if stacked else K
    assert M % bm == 0 and N % bn == 0 and n_sh * ks == K
    dn = (((1,), (1,)), ((), ()))

    def body(*refs):
        a_ref, b_ref = refs[0], refs[1]
        o_ref = refs[-1]
        if stacked:
            acc = lax.dot_general(a_ref[:, 0:ks], b_ref[0], dn, preferred_element_type=F32)
            for s in range(1, n_sh):
                acc = acc + lax.dot_general(a_ref[:, s * ks:(s + 1) * ks], b_ref[s], dn, preferred_element_type=F32)
        else:
            acc = lax.dot_general(a_ref[...], b_ref[...], dn, preferred_element_type=F32)
        if add is not None:
            acc = acc + refs[2][...]
        o_ref[...] = acc.astype(o_ref.dtype)

    if stacked:
        b_spec = pl.BlockSpec((n_sh, bn, ks), lambda i, j: (0, j, 0))
    else:
        b_spec = pl.BlockSpec((bn, K), lambda i, j: (j, 0))
    in_specs = [pl.BlockSpec((bm, K), lambda i, j: (i, 0)), b_spec]
    args = [a, b]
    if add is not None:
        rows = bm if add.shape[0] == M else 1
        in_specs.append(pl.BlockSpec((rows, bn), lambda i, j: (i if rows == bm else 0, j)))
        args.append(add)
    limit = _vmem_limit(_nbytes((bm, K), a.dtype), _nbytes((bn, K), b.dtype), _nbytes((bm, bn), F32))
    return _call(body, name=name, grid=(M // bm, N // bn), in_specs=in_specs,
                 out_specs=[pl.BlockSpec((bm, bn), lambda i, j: (i, j))],
                 out_shape=[jax.ShapeDtypeStruct((M, N), out_dtype)], args=args, vmem_limit=limit, carry=carry)[0]


def _mm_tn(a, b, *, bm, bn, out_dtype, name, n_stack=None, carry=None):
    T, M = a.shape
    N = b.shape[1]
    assert M % bm == 0 and N % bn == 0 and (n_stack is None or bn * n_stack == N)
    dn = (((0,), (0,)), ((), ()))

    def body(a_ref, b_ref, o_ref):
        acc = lax.dot_general(a_ref[...], b_ref[...], dn, preferred_element_type=F32)
        o_ref[...] = acc.astype(o_ref.dtype)

    if n_stack is None:
        out_spec = pl.BlockSpec((bm, bn), lambda i, j: (i, j))
        out_shape = jax.ShapeDtypeStruct((M, N), out_dtype)
    else:
        out_spec = pl.BlockSpec((None, bm, bn), lambda i, j: (j, i, 0))
        out_shape = jax.ShapeDtypeStruct((n_stack, M, bn), out_dtype)
    limit = _vmem_limit(_nbytes((T, bm), a.dtype), _nbytes((T, bn), b.dtype), _nbytes((bm, bn), F32))
    return _call(body, name=name, grid=(M // bm, N // bn),
                 in_specs=[pl.BlockSpec((T, bm), lambda i, j: (0, i)), pl.BlockSpec((T, bn), lambda i, j: (0, j))],
                 out_specs=[out_spec], out_shape=[out_shape], args=[a, b], vmem_limit=limit, carry=carry)[0]


ROW_BLK = 256


def _rstd(x):
    return lax.rsqrt(jnp.mean(x * x, axis=-1, keepdims=True) + EPS)


def _norm_bwd(xhat, rstd, dxhat):
    return rstd * (dxhat - xhat * jnp.mean(dxhat * xhat, axis=-1, keepdims=True))


def _row_spec(cols):
    return pl.BlockSpec((ROW_BLK, cols), lambda i: (i, 0))


def _vec_spec(cols):
    return pl.BlockSpec((1, cols), lambda i: (0, 0))


def _prologue(x, g, rel_bias, idx_all, *, name, carry=None):
    T, D = x.shape
    n_pat = idx_all.shape[0]
    heads = 2 * N_PAIRS
    steps = n_pat * heads
    rows = T // steps

    def body(rb_ref, x_ref, g_ref, idx_ref, n_ref, bias_ref):
        s = pl.program_id(0)
        head = jnp.where(s < heads, s, heads + s % heads)
        xv = x_ref[...]
        n_ref[...] = (xv * _rstd(xv) * g_ref[...]).astype(n_ref.dtype)
        idxv = idx_ref[...]
        acc = jnp.where(idxv < 0, NEG, 0.0).astype(F32)
        for b in range(NUM_BUCKETS):
            acc = acc + jnp.where(idxv == b, rb_ref[b, head], 0.0)
        bias_ref[0] = acc
        kap = lax.broadcasted_iota(jnp.int32, (1, 2 * QBLK), 1)
        bias_ref[1] = jnp.where(kap < QBLK, NEG, acc)

    return _call(
        body, name=name, grid=(steps,),
        in_specs=[pl.BlockSpec(memory_space=pltpu.SMEM), pl.BlockSpec((rows, D), lambda s: (s, 0)),
                  pl.BlockSpec((1, D), lambda s: (0, 0)),
                  pl.BlockSpec((None, QBLK, 2 * QBLK), lambda s: (s // heads, 0, 0))],
        out_specs=[pl.BlockSpec((rows, D), lambda s: (s, 0)),
                   pl.BlockSpec((None, 2, None, QBLK, 2 * QBLK), lambda s: (s // heads, 0, s % heads, 0, 0))],
        out_shape=[jax.ShapeDtypeStruct((T, D), BF16),
                   jax.ShapeDtypeStruct((n_pat, 2, heads, QBLK, 2 * QBLK), F32)],
        args=[rel_bias, x, g, idx_all], carry=carry)


def _mm_resid_norm(a, b, h, g_post, coef, g_next, *, bm, name, bias=None, carry=None):
    M, K = a.shape
    N = b.shape[1]

    def body(*refs):
        a_ref, b_ref, h_ref, gp_ref, gn_ref = refs[0:5]
        f_ref, hn_ref, n_ref = refs[-3:]
        for rows in _row_halves(bm):
            f = jnp.dot(a_ref[rows, :], b_ref[...], preferred_element_type=F32)
            if bias is not None:
                f = f + refs[5][...]
            f_ref[rows, :] = f
            hn = h_ref[rows, :] + coef * (f * _rstd(f) * gp_ref[...])
            hn_ref[rows, :] = hn
            n_ref[rows, :] = (hn * _rstd(hn) * gn_ref[...]).astype(n_ref.dtype)

    row = lambda w: pl.BlockSpec((bm, w), lambda i: (i, 0))
    vec = pl.BlockSpec((1, N), lambda i: (0, 0))
    in_specs = [row(K), pl.BlockSpec((K, N), lambda i: (0, 0), pipeline_mode=pl.Buffered(1)), row(N), vec, vec]
    args = [a, b, h, g_post, g_next]
    if bias is not None:
        in_specs.append(vec)
        args.append(bias)
    limit = _vmem_limit(_nbytes((bm, K), a.dtype), _nbytes((K, N), b.dtype) // 2, 4 * _nbytes((bm, N), F32))
    return _call(body, name=name, grid=(M // bm,), in_specs=in_specs, out_specs=[row(N), row(N), row(N)],
                 out_shape=[jax.ShapeDtypeStruct((M, N), F32), jax.ShapeDtypeStruct((M, N), F32),
                            jax.ShapeDtypeStruct((M, N), BF16)], args=args, vmem_limit=limit, carry=carry)


def _mm_nt_norms_bwd(a, b, dh_in, h, g_pre, f, g_post, coef, *, bm, name, add=None, b_is_kn=False, carry=None):
    M, K = a.shape
    stacked = b.ndim == 3
    N = b.shape[1] if (stacked or b_is_kn) else b.shape[0]
    n_sh = b.shape[0] if stacked else 1
    ks = b.shape[2] if stacked else K
    assert n_sh * ks == K
    dn_dims = (((1,), (0,)), ((), ())) if b_is_kn else (((1,), (1,)), ((), ()))
    with_f = f is not None
    n_in = 5 + (2 if with_f else 0) + (1 if add is not None else 0)

    def body(*refs):
        a_ref, b_ref, dhi_ref, h_ref, gpre_ref = refs[0:5]
        outs = refs[n_in:]

        @pl.when(pl.program_id(0) == 0)
        def _():
            for acc in (outs[2:] if with_f else outs[1:]):
                acc[...] = jnp.zeros_like(acc)

        for rows in _row_halves(bm):
            if stacked:
                dn = lax.dot_general(a_ref[rows, 0:ks], b_ref[0], dn_dims, preferred_element_type=F32)
                for s in range(1, n_sh):
                    dn = dn + lax.dot_general(a_ref[rows, s * ks:(s + 1) * ks], b_ref[s], dn_dims,
                                              preferred_element_type=F32)
            else:
                dn = lax.dot_general(a_ref[rows, :], b_ref[...], dn_dims, preferred_element_type=F32)
            if add is not None:
                dn = dn + refs[n_in - 1][rows, :]
            hv = h_ref[rows, :]
            r = _rstd(hv)
            hh = hv * r
            dh = dhi_ref[rows, :] + _norm_bwd(hh, r, dn * gpre_ref[...])
            outs[0][rows, :] = dh
            if not with_f:
                outs[1][...] += jnp.sum(dn * hh, axis=0, keepdims=True)
                continue
            f_ref, gpost_ref = refs[5], refs[6]
            df_ref, dgpre_ref, dgpost_ref, dsum_ref = outs[1:]
            dgpre_ref[...] += jnp.sum(dn * hh, axis=0, keepdims=True)
            fv = f_ref[rows, :]
            rf = _rstd(fv)
            fh = fv * rf
            dy = coef * dh
            dgpost_ref[...] += jnp.sum(dy * fh, axis=0, keepdims=True)
            df = _norm_bwd(fh, rf, dy * gpost_ref[...])
            dsum_ref[...] += jnp.sum(df, axis=0, keepdims=True)
            df_ref[rows, :] = df.astype(df_ref.dtype)

    row = lambda w: pl.BlockSpec((bm, w), lambda i: (i, 0))
    vec = pl.BlockSpec((1, N), lambda i: (0, 0))
    once = pl.Buffered(1)
    if stacked:
        b_spec = pl.BlockSpec((n_sh, N, ks), lambda i: (0, 0, 0), pipeline_mode=once)
    else:
        b_spec = pl.BlockSpec(b.shape, lambda i: (0, 0), pipeline_mode=once)
    in_specs = [row(K), b_spec, row(N), row(N), vec]
    args = [a, b, dh_in, h, g_pre]
    if with_f:
        in_specs += [row(N), vec]
        args += [f, g_post]
    if add is not None:
        in_specs.append(row(N))
        args.append(add)
    vec_shape = jax.ShapeDtypeStruct((1, N), F32)
    out_specs = [row(N)] + ([row(N), vec, vec, vec] if with_f else [vec])
    out_shape = [jax.ShapeDtypeStruct((M, N), F32)]
    out_shape += [jax.ShapeDtypeStruct((M, N), BF16), vec_shape, vec_shape, vec_shape] if with_f else [vec_shape]
    limit = _vmem_limit(_nbytes((bm, K), a.dtype), _nbytes((N, K), b.dtype) // 2, 6 * _nbytes((bm, N), F32))
    return _call(body, name=name, grid=(M // bm,), in_specs=in_specs, out_specs=out_specs, out_shape=out_shape,
                 args=args, vmem_limit=limit, semantics=("arbitrary",), carry=carry)


def _ffn_up(n, w_gu, *, bm, name, carry=None):
    M, K = n.shape
    n_sh, _, ns = w_gu.shape
    half = n_sh // 2

    def body(n_ref, wg_ref, wu_ref, g_ref, u_ref, a_ref):
        nv = n_ref[...]
        for cols in _col_chunks(ns):
            g = jnp.dot(nv, wg_ref[:, cols], preferred_element_type=F32)
            u = jnp.dot(nv, wu_ref[:, cols], preferred_element_type=F32)
            g_ref[:, cols] = g.astype(g_ref.dtype)
            u_ref[:, cols] = u.astype(u_ref.dtype)
            a_ref[:, cols] = (g * jax.nn.sigmoid(g) * u).astype(a_ref.dtype)

    out_spec = pl.BlockSpec((bm, ns), lambda i, j: (i, j))
    out = jax.ShapeDtypeStruct((M, half * ns), BF16)
    limit = _vmem_limit(_nbytes((bm, K), n.dtype), 2 * _nbytes((K, ns), w_gu.dtype), 3 * _nbytes((bm, ns), F32))
    return _call(body, name=name, grid=(M // bm, half),
                 in_specs=[pl.BlockSpec((bm, K), lambda i, j: (i, 0)),
                           pl.BlockSpec((None, K, ns), lambda i, j: (j, 0, 0)),
                           pl.BlockSpec((None, K, ns), lambda i, j: (j + half, 0, 0))],
                 out_specs=[out_spec, out_spec, out_spec], out_shape=[out, out, out], args=[n, w_gu, w_gu],
                 vmem_limit=limit, carry=carry)


def _ffn_down_bwd(df, w_down, g, u, *, bm, name, carry=None):
    M, D = df.shape
    F = w_down.shape[0]
    dn = (((1,), (1,)), ((), ()))

    def body(df_ref, w_ref, g_ref, u_ref, o_ref):
        dfv = df_ref[...]
        for cols in _col_chunks(F):
            da = lax.dot_general(dfv, w_ref[cols, :], dn, preferred_element_type=F32)
            gv = g_ref[:, cols].astype(F32)
            s = jax.nn.sigmoid(gv)
            o_ref[:, cols] = (da * u_ref[:, cols].astype(F32) * (s * (1.0 + gv * (1.0 - s)))).astype(o_ref.dtype)
            o_ref[:, F + cols.start:F + cols.stop] = (da * (gv * s)).astype(o_ref.dtype)

    row = lambda w: pl.BlockSpec((bm, w), lambda i: (i, 0))
    limit = _vmem_limit(_nbytes((F, D), w_down.dtype) // 2, 2 * _nbytes((bm, F), BF16), _nbytes((bm, 2 * F), BF16))
    return _call(body, name=name, grid=(M // bm,),
                 in_specs=[row(D), pl.BlockSpec((F, D), lambda i: (0, 0), pipeline_mode=pl.Buffered(1)), row(F), row(F)],
                 out_specs=[row(2 * F)], out_shape=[jax.ShapeDtypeStruct((M, 2 * F), BF16)],
                 args=[df, w_down, g, u], vmem_limit=limit, carry=carry)[0]


def _ple_head(n4, w_gate, p, w_ple, h3, g_post, target, *, bm, name):
    T, D = h3.shape
    kp = p.shape[1]

    def body(n_ref, wg_ref, p_ref, wp_ref, h_ref, g_ref, t_ref, dh_ref, de_ref, dgp_ref, loss_ref, dg_ref):
        @pl.when(pl.program_id(0) == 0)
        def _():
            loss_ref[...] = jnp.zeros_like(loss_ref)
            dg_ref[...] = jnp.zeros_like(dg_ref)

        gpost = g_ref[...]
        for rows in _row_halves(bm):
            gate = jax.nn.sigmoid(jnp.dot(n_ref[rows, :], wg_ref[...], preferred_element_type=F32))
            ev = jnp.dot(p_ref[rows, :], wp_ref[...], preferred_element_type=F32)
            ge = gate * ev
            r = _rstd(ge)
            gh = ge * r
            diff = h_ref[rows, :] + gh * gpost - t_ref[rows, :]
            loss_ref[...] += jnp.sum(diff * diff) * (0.5 / D)
            dh = diff * (1.0 / D)
            dh_ref[rows, :] = dh
            dg_ref[...] += jnp.sum(dh * gh, axis=0, keepdims=True)
            dge = _norm_bwd(gh, r, dh * gpost)
            de_ref[rows, :] = (dge * gate).astype(de_ref.dtype)
            dgp_ref[rows, :] = (dge * ev * gate * (1.0 - gate)).astype(dgp_ref.dtype)

    row = lambda w: pl.BlockSpec((bm, w), lambda i: (i, 0))
    whole = lambda a: pl.BlockSpec(a.shape, lambda i: (0, 0), pipeline_mode=pl.Buffered(1))
    limit = _vmem_limit(_nbytes((bm, D), BF16), _nbytes(w_gate.shape, w_gate.dtype) // 2, 6 * _nbytes((bm, D), F32))
    return pl.pallas_call(
        body, name=name, grid=(T // bm,),
        in_specs=[row(D), whole(w_gate), row(kp), whole(w_ple), row(D), _vec_spec(D), row(D)],
        out_specs=(row(D), row(D), row(D), _vec_spec(LANES), _vec_spec(D)),
        out_shape=(jax.ShapeDtypeStruct((T, D), F32), jax.ShapeDtypeStruct((T, D), BF16),
                   jax.ShapeDtypeStruct((T, D), BF16), jax.ShapeDtypeStruct((1, LANES), F32),
                   jax.ShapeDtypeStruct((1, D), F32)),
        compiler_params=pltpu.CompilerParams(dimension_semantics=("arbitrary",), vmem_limit_bytes=limit),
    )(n4, w_gate, p, w_ple, h3, g_post, target)


def _t5_index(max_dist, stride):
    rho = np.arange(QBLK)[:, None]
    kap = np.arange(2 * QBLK)[None, :]
    d = rho + QBLK - kap
    valid = (d >= 0) & (d <= max_dist)
    n = np.maximum(d, 0) * stride
    max_exact = NUM_BUCKETS // 2
    nf = np.maximum(n, 1).astype(np.float32)
    large = max_exact + (np.log(nf / np.float32(max_exact)) / np.float32(math.log(MAX_DISTANCE / max_exact))
                         * np.float32(NUM_BUCKETS - max_exact)).astype(np.int32)
    large = np.minimum(large, NUM_BUCKETS - 1)
    bucket = np.where(n < max_exact, n, large)
    return np.where(valid, bucket, -1).astype(np.int32)


def _bias_grad(d_bias, idx, *, name):
    def body(db_ref, idx_ref, o_ref, part_ref):
        h = pl.program_id(0)

        @pl.when(h == 0)
        def _():
            o_ref[...] = jnp.zeros_like(o_ref)

        idxv = idx_ref[...]
        dv = db_ref[0]
        for b in range(NUM_BUCKETS):
            masked = jnp.where(idxv == b, dv, 0.0)
            part_ref[b] = jnp.sum(masked.reshape(QBLK // SUBLANES, SUBLANES, 2 * QBLK), axis=0)
        sums = jnp.sum(jnp.sum(part_ref[...], axis=1), axis=1, keepdims=True)
        lanes = lax.broadcasted_iota(jnp.int32, (NUM_BUCKETS, LANES), 1)
        o_ref[...] += jnp.where(lanes == h, sums, 0.0)

    return pl.pallas_call(
        body, name=name, grid=(2 * N_PAIRS,),
        in_specs=[pl.BlockSpec((1, QBLK, 2 * QBLK), lambda h: (h, 0, 0)),
                  pl.BlockSpec((QBLK, 2 * QBLK), lambda h: (0, 0))],
        out_specs=pl.BlockSpec((NUM_BUCKETS, LANES), lambda h: (0, 0)),
        out_shape=jax.ShapeDtypeStruct((NUM_BUCKETS, LANES), F32),
        scratch_shapes=[pltpu.VMEM((NUM_BUCKETS, SUBLANES, 2 * QBLK), F32)],
        compiler_params=pltpu.CompilerParams(dimension_semantics=("arbitrary",)),
    )(d_bias, idx)


def _lane():
    return lax.broadcasted_iota(jnp.int32, (1, LANES), 1)


def _head_sel(h):
    return (_lane() < HEAD_DIM) if h == 0 else (_lane() >= HEAD_DIM)


def _stat_lo():
    return (_lane() % HEAD_DIM) < (HEAD_DIM // 2)


def _stack_heads(t, scale=None):
    if scale is not None:
        t = t * scale
    zero = jnp.zeros_like(t)
    return jnp.concatenate([jnp.where(_head_sel(0), t, zero), jnp.where(_head_sel(1), t, zero)], axis=0)


def _class_spec(L, r, cols, stride, pps):
    chunks = N_PAIRS // pps
    mode = pl.Buffered(1) if r * chunks == 1 else None
    return pl.BlockSpec((L, MIX_W // chunks), lambda j, c: (0, (cols + j * stride) * chunks + c), pipeline_mode=mode)


def _rows(b, n_blocks=1):
    return pl.ds(pl.multiple_of(b * QBLK, QBLK), n_blocks * QBLK)


def _key_window(ref, b, cols, first, kv_head=None):
    if kv_head is not None:
        cols = slice(0, LANES)
    if first:
        blk = ref[0:QBLK, cols]
        tile = jnp.concatenate([blk, blk], axis=0)
    else:
        tile = ref[_rows(b - 1, 2), cols]
    if kv_head is None:
        return tile
    wide = tile.astype(F32)
    keep = jnp.logical_xor(_lane() < HEAD_DIM, kv_head == 1)
    return jnp.where(keep, wide, pltpu.roll(wide, HEAD_DIM, 1)).astype(tile.dtype)


def _kv_spec(L, r, col, stride, pps, kv_shared):
    if not kv_shared:
        return _class_spec(L, r, col, stride, pps)
    mode = pl.Buffered(1) if pps == N_PAIRS else None
    return pl.BlockSpec((L, LANES), lambda j, c: (0, col), pipeline_mode=mode)


def _band_fwd(qa, ka, va, bias, *, r, q_col, k_col, v_col, stride, pattern, name, kv_shared=False, sink=None,
              carry=None):
    L = qa.shape[0]
    nb = L // QBLK
    dn = (((1,), (1,)), ((), ()))
    scale = HEAD_DIM ** -0.5

    pps = N_PAIRS

    def body(q_ref, k_ref, v_ref, bias_ref, *rest):
        sel0 = _head_sel(0)
        pair0 = pl.program_id(1) * pps

        def emit(rows, cols, o, lse):
            if sink is None:
                o_ref, lse_ref = rest
                o_ref[rows, cols] = o.astype(o_ref.dtype)
                lse_ref[rows, cols] = lse
                return
            sink_ref, out_ref, outb_ref, lse_ref = rest
            sk = sink_ref[:, cols]
            mx = jnp.maximum(lse, sk)
            w = jnp.exp(lse - mx)
            den = w + jnp.exp(sk - mx)
            out = o * (w / den)
            out_ref[rows, cols] = out
            outb_ref[rows, cols] = out.astype(outb_ref.dtype)
            lse_ref[rows, cols] = mx + jnp.log(den)

        def block(b, first):
            rows = _rows(b)
            for i in range(pps):
                cols = slice(i * LANES, (i + 1) * LANES)
                q2 = _stack_heads(q_ref[rows, cols], scale)
                kv_head = (pair0 + i) // 2 if kv_shared else None
                k = _key_window(k_ref, b, cols, first, kv_head)
                v = _key_window(v_ref, b, cols, first, kv_head)
                bias2 = bias_ref[1 if first else 0, pl.ds(2 * (pair0 + i), 2)].reshape(2 * QBLK, 2 * QBLK)
                s = lax.dot_general(q2, k, dn, preferred_element_type=F32) + bias2
                m = jnp.max(s, axis=1, keepdims=True)
                p = jnp.exp(s - m)
                l = jnp.sum(p, axis=1, keepdims=True)
                o = jnp.dot(p.astype(v.dtype), v, preferred_element_type=F32) * (1.0 / l)
                lse = m + jnp.log(l)
                emit(rows, cols, jnp.where(sel0, o[0:QBLK], o[QBLK:]), jnp.where(sel0, lse[0:QBLK], lse[QBLK:]))

        block(0, True)
        if nb > 1:
            pl.loop(1, nb)(lambda b: block(b, False))

    bias_spec = pl.BlockSpec((None, 2, 2 * N_PAIRS, QBLK, 2 * QBLK), lambda j, c: (pattern, 0, 0, 0, 0))
    out_spec = _class_spec(L, r, 0, 1, pps)
    blk_bytes = _nbytes((L, pps * LANES), F32)
    in_specs = [_class_spec(L, r, q_col, stride, pps), _kv_spec(L, r, k_col, stride, pps, kv_shared),
                _kv_spec(L, r, v_col, stride, pps, kv_shared), bias_spec]
    args = [qa, ka, va, bias]
    f32_out, bf16_out = jax.ShapeDtypeStruct((L, r * MIX_W), F32), jax.ShapeDtypeStruct((L, r * MIX_W), BF16)
    out_shape = [bf16_out, f32_out]
    if sink is not None:
        in_specs.append(pl.BlockSpec((1, MIX_W), lambda j, c: (0, 0)))
        args.append(sink)
        out_shape = [f32_out, bf16_out, f32_out]
    return _call(body, name=name, grid=(r, N_PAIRS // pps), in_specs=in_specs, out_specs=[out_spec] * len(out_shape),
                 out_shape=out_shape, args=args, vmem_limit=_vmem_limit(*([blk_bytes] * 4)), carry=carry)


def _class_rows_spec(rows, r, width):
    return pl.BlockSpec((rows // r, r * width), lambda i, *_: (i, 0))


def _token_scratch(rows, width):
    return pltpu.VMEM((width // LANES, rows, LANES), F32)


def _fill_tokens(scratch, value):
    for c in range(scratch.shape[0]):
        scratch[c] = value[:, c * LANES:(c + 1) * LANES]


def _read_tokens(scratch):
    return jnp.concatenate([scratch[c] for c in range(scratch.shape[0])], axis=1)


def _to_tokens(blk_ref, r, scratch):
    if r == 1:
        return blk_ref[...].astype(F32)
    nt, rows, _ = scratch.shape
    for j in range(r):
        for c in range(nt):
            lanes = slice((j * nt + c) * LANES, (j * nt + c + 1) * LANES)
            scratch.at[c][pl.ds(j, rows // r, stride=r), :] = blk_ref[:, lanes].astype(F32)
    return _read_tokens(scratch)


def _from_tokens(dst_ref, r, scratch):
    nt, rows, _ = scratch.shape
    if r == 1:
        dst_ref[...] = _read_tokens(scratch).astype(dst_ref.dtype)
        return
    for j in range(r):
        for c in range(nt):
            lanes = slice((j * nt + c) * LANES, (j * nt + c + 1) * LANES)
            dst_ref[:, lanes] = scratch.at[c][pl.ds(j, rows // r, stride=r), :].astype(dst_ref.dtype)


def _mm_nt_classes(a, b, bias, dils, *, bm, name):
    M, K = a.shape
    N = b.shape[0]
    dn = (((1,), (1,)), ((), ()))

    def body(a_ref, b_ref, bias_ref, *rest):
        outs, tile = rest[:-1], rest[-1]
        _fill_tokens(tile, lax.dot_general(a_ref[...], b_ref[...], dn, preferred_element_type=F32) + bias_ref[...])
        for out, r in zip(outs, dils):
            _from_tokens(out, r, tile)

    limit = _vmem_limit(_nbytes((bm, K), a.dtype), _nbytes((K, N), b.dtype) // 2, (1 + len(dils)) * _nbytes((bm, N), F32))
    return pl.pallas_call(
        body, name=name, grid=(M // bm,),
        in_specs=[pl.BlockSpec((bm, K), lambda i: (i, 0)),
                  pl.BlockSpec((N, K), lambda i: (0, 0), pipeline_mode=pl.Buffered(1)),
                  pl.BlockSpec((1, N), lambda i: (0, 0))],
        out_specs=tuple(_class_rows_spec(bm, r, N) for r in dils),
        out_shape=tuple(jax.ShapeDtypeStruct((M // r, r * N), BF16) for r in dils),
        scratch_shapes=[_token_scratch(bm, N)],
        compiler_params=pltpu.CompilerParams(dimension_semantics=("parallel",), vmem_limit_bytes=limit),
    )(a, b, bias)


def _merge(branches, dils, sink, *, name):
    W = MIX_W
    T = branches[0][0].shape[0] * dils[0]
    nbr = len(branches)

    def body(*refs):
        sink_ref = refs[2 * nbr]
        out_ref, outb_ref, lse_ref, scratch = refs[2 * nbr + 1:]
        sk = sink_ref[...]
        lses = [_to_tokens(refs[2 * t + 1], dils[t], scratch) for t in range(nbr)]
        mx = sk
        for lse in lses:
            mx = jnp.maximum(mx, lse)
        den = jnp.exp(sk - mx)
        num = jnp.zeros_like(mx)
        for t in range(nbr):
            w = jnp.exp(lses[t] - mx)
            den = den + w
            num = num + w * _to_tokens(refs[2 * t], dils[t], scratch)
        out = num / den
        out_ref[...] = out
        outb_ref[...] = out.astype(outb_ref.dtype)
        lse_ref[...] = mx + jnp.log(den)

    args = [a for br in branches for a in br] + [sink]
    in_specs = [_class_rows_spec(ROW_BLK, r, W) for r in dils for _ in range(2)] + [_vec_spec(W)]
    return pl.pallas_call(
        body, name=name, grid=(T // ROW_BLK,), in_specs=in_specs,
        out_specs=(_row_spec(W), _row_spec(W), _row_spec(W)),
        out_shape=(jax.ShapeDtypeStruct((T, W), F32), jax.ShapeDtypeStruct((T, W), BF16),
                   jax.ShapeDtypeStruct((T, W), F32)),
        scratch_shapes=[_token_scratch(ROW_BLK, W)],
        compiler_params=pltpu.CompilerParams(dimension_semantics=("parallel",)),
    )(*args)


def _attn_delta(dmix, outs, lses, sink, dils, *, name):
    T = dmix.shape[0]
    W = MIX_W
    nd = len(dils)

    def body(dmix_ref, oa_ref, ob_ref, la_ref, lb_ref, sink_ref, *rest):
        doa_ref, sta_ref, dsink_ref = rest[0:3]
        dob_refs, stb_refs = rest[3:3 + nd], rest[3 + nd:3 + 2 * nd]
        do_tile, st_tile = rest[3 + 2 * nd:]

        @pl.when(pl.program_id(0) == 0)
        def _():
            dsink_ref[...] = jnp.zeros_like(dsink_ref)

        sel0, lo = _head_sel(0), _stat_lo()
        for mixer, (o_ref, l_ref) in enumerate(((oa_ref, la_ref), (ob_ref, lb_ref))):
            for i in range(N_PAIRS):
                cols = slice(i * LANES, (i + 1) * LANES)
                do = dmix_ref[:, mixer * W + i * LANES:mixer * W + (i + 1) * LANES]
                prod = do * o_ref[:, cols]
                d0 = jnp.sum(jnp.where(sel0, prod, 0.0), axis=1, keepdims=True)
                d1 = jnp.sum(jnp.where(sel0, 0.0, prod), axis=1, keepdims=True)
                delta = jnp.where(sel0, d0, d1)
                lse = l_ref[:, cols]
                stat = jnp.where(lo, lse, delta)
                if mixer == 0:
                    sta_ref[:, cols] = stat
                    doa_ref[:, cols] = do.astype(doa_ref.dtype)
                    dsink_ref[:, cols] += -jnp.sum(jnp.exp(sink_ref[:, cols] - lse) * delta, axis=0, keepdims=True)
                else:
                    st_tile[i] = stat
                    do_tile[i] = do
        for t, r in enumerate(dils):
            _from_tokens(dob_refs[t], r, do_tile)
            _from_tokens(stb_refs[t], r, st_tile)

    class_specs = [_class_rows_spec(ROW_BLK, r, W) for r in dils]
    out_shape = [jax.ShapeDtypeStruct((T, W), BF16), jax.ShapeDtypeStruct((T, W), F32), jax.ShapeDtypeStruct((1, W), F32)]
    out_shape += [jax.ShapeDtypeStruct((T // r, r * W), BF16) for r in dils]
    out_shape += [jax.ShapeDtypeStruct((T // r, r * W), F32) for r in dils]
    res = pl.pallas_call(
        body, name=name, grid=(T // ROW_BLK,),
        in_specs=[_row_spec(2 * W), _row_spec(W), _row_spec(W), _row_spec(W), _row_spec(W), _vec_spec(W)],
        out_specs=tuple([_row_spec(W), _row_spec(W), _vec_spec(W)] + class_specs + class_specs),
        out_shape=tuple(out_shape),
        scratch_shapes=[_token_scratch(ROW_BLK, W), _token_scratch(ROW_BLK, W)],
        compiler_params=pltpu.CompilerParams(dimension_semantics=("arbitrary",)),
    )(dmix, outs[0], outs[1], lses[0], lses[1], sink)
    return res[0], res[1], res[2], res[3:3 + nd], res[3 + nd:]


def _band_bwd(qa, ka, va, d_out, stat, bias, *, r, q_col, k_col, v_col, stride, pattern, name, kv_shared=False,
              carry=None):
    L = qa.shape[0]
    nb = L // QBLK
    dn_nt = (((1,), (1,)), ((), ()))
    dn_tn = (((0,), (0,)), ((), ()))
    scale = HEAD_DIM ** -0.5

    pps = N_PAIRS if r > 1 else N_PAIRS // 2

    def body(q_ref, k_ref, v_ref, do_ref, st_ref, bias_ref, dq_ref, dk_ref, dv_ref, db_ref, dk_carry, dv_carry):
        @pl.when((pl.program_id(0) == 0) & (pl.program_id(1) == 0))
        def _():
            db_ref[...] = jnp.zeros_like(db_ref)

        lo, sel0 = _stat_lo(), _head_sel(0)
        pair0 = pl.program_id(1) * pps

        def block(b, first):
            rows = _rows(b)
            for i in range(pps):
                heads = pl.ds(2 * (pair0 + i), 2)
                cols = slice(i * LANES, (i + 1) * LANES)
                q2 = _stack_heads(q_ref[rows, cols], scale)
                kv_head = (pair0 + i) // 2 if kv_shared else None
                k = _key_window(k_ref, b, cols, first, kv_head)
                v = _key_window(v_ref, b, cols, first, kv_head)
                do2 = _stack_heads(do_ref[rows, cols])
                st = st_ref[rows, cols]
                lse2 = jnp.concatenate(
                    [jnp.max(jnp.where(_head_sel(h) & lo, st, -jnp.inf), axis=1, keepdims=True) for h in range(2)], axis=0)
                delta2 = jnp.concatenate(
                    [jnp.sum(jnp.where(_head_sel(h) & ~lo, st, 0.0), axis=1, keepdims=True) for h in range(2)],
                    axis=0) * (2.0 / HEAD_DIM)
                bias2 = bias_ref[1 if first else 0, heads].reshape(2 * QBLK, 2 * QBLK)
                s = lax.dot_general(q2, k, dn_nt, preferred_element_type=F32) + bias2
                p = jnp.exp(s - lse2)
                dp = lax.dot_general(do2, v, dn_nt, preferred_element_type=F32)
                ds = p * (dp - delta2)
                db_ref[heads] += ds.reshape(2, QBLK, 2 * QBLK)
                dsb = ds.astype(k.dtype)
                dq2 = jnp.dot(dsb, k, preferred_element_type=F32)
                dq_ref[rows, cols] = (jnp.where(sel0, dq2[0:QBLK], dq2[QBLK:]) * scale).astype(dq_ref.dtype)
                dk_acc = lax.dot_general(dsb, q2, dn_tn, preferred_element_type=F32)
                dv_acc = lax.dot_general(p.astype(k.dtype), do2, dn_tn, preferred_element_type=F32)
                if not first:
                    prev = _rows(b - 1)
                    dk_ref[prev, cols] = (dk_carry[:, cols] + dk_acc[0:QBLK]).astype(dk_ref.dtype)
                    dv_ref[prev, cols] = (dv_carry[:, cols] + dv_acc[0:QBLK]).astype(dv_ref.dtype)
                dk_carry[:, cols] = dk_acc[QBLK:2 * QBLK]
                dv_carry[:, cols] = dv_acc[QBLK:2 * QBLK]

        block(0, True)
        if nb > 1:
            pl.loop(1, nb)(lambda b: block(b, False))

        last = _rows(nb - 1)
        dk_ref[last, :] = dk_carry[...].astype(dk_ref.dtype)
        dv_ref[last, :] = dv_carry[...].astype(dv_ref.dtype)

    tok_spec = _class_spec(L, r, 0, 1, pps)
    bias_spec = pl.BlockSpec((None, 2, 2 * N_PAIRS, QBLK, 2 * QBLK), lambda j, c: (pattern, 0, 0, 0, 0))
    dbias_spec = pl.BlockSpec((2 * N_PAIRS, QBLK, 2 * QBLK), lambda j, c: (0, 0, 0))
    grad = jax.ShapeDtypeStruct((L, r * MIX_W), BF16)
    blk_bytes = _nbytes((L, pps * LANES), BF16)
    carry_shape = pltpu.VMEM((QBLK, pps * LANES), F32)
    return _call(
        body, name=name, grid=(r, N_PAIRS // pps),
        in_specs=[_class_spec(L, r, q_col, stride, pps), _kv_spec(L, r, k_col, stride, pps, kv_shared),
                  _kv_spec(L, r, v_col, stride, pps, kv_shared), tok_spec, tok_spec, bias_spec],
        out_specs=[tok_spec, tok_spec, tok_spec, dbias_spec],
        out_shape=[grad, grad, grad, jax.ShapeDtypeStruct((2 * N_PAIRS, QBLK, 2 * QBLK), F32)],
        args=[qa, ka, va, d_out, stat, bias], scratch=[carry_shape, carry_shape],
        vmem_limit=_vmem_limit(*([blk_bytes] * 9)), semantics=("arbitrary", "arbitrary"), carry=carry)


def _dz_assemble(dq_a, dk_a, dv_a, b_grads, dils, *, name):
    T = dq_a.shape[0]
    W = MIX_W

    def group_sum(x):
        u0 = x[:, 0:LANES] + x[:, LANES:2 * LANES]
        u1 = x[:, 2 * LANES:3 * LANES] + x[:, 3 * LANES:4 * LANES]
        s0 = u0 + pltpu.roll(u0, HEAD_DIM, 1)
        s1 = u1 + pltpu.roll(u1, HEAD_DIM, 1)
        return jnp.where(_head_sel(0), s0, s1)

    def body(*refs):
        dqa_ref, dka_ref, dva_ref = refs[0:3]
        b_refs = refs[3:12]
        dza_ref, dzb_ref, sa_ref, sb_ref, scratch = refs[12:]

        @pl.when(pl.program_id(0) == 0)
        def _():
            sa_ref[...] = jnp.zeros_like(sa_ref)
            sb_ref[...] = jnp.zeros_like(sb_ref)

        def put(dst, acc, col, part):
            w = part.shape[1]
            dst[:, col:col + w] = part.astype(dst.dtype)
            acc[:, col:col + w] += jnp.sum(part, axis=0, keepdims=True)

        put(dza_ref, sa_ref, 0, dqa_ref[...].astype(F32))
        put(dza_ref, sa_ref, W, group_sum(dka_ref[...].astype(F32)))
        put(dza_ref, sa_ref, W + LANES, group_sum(dva_ref[...].astype(F32)))
        for t in range(3):
            part = _to_tokens(b_refs[t], dils[0], scratch)
            for pat in range(1, len(dils)):
                part = part + _to_tokens(b_refs[3 * pat + t], dils[pat], scratch)
            put(dzb_ref, sb_ref, t * W, part)

    args = [dq_a, dk_a, dv_a] + [g[t] for g in b_grads for t in range(3)]
    return pl.pallas_call(
        body, name=name, grid=(T // ROW_BLK,),
        in_specs=[_row_spec(W)] * 3 + [_class_rows_spec(ROW_BLK, r, W) for r in dils for _ in range(3)],
        out_specs=(_row_spec(ZA_W), _row_spec(ZB_W), _vec_spec(ZA_W), _vec_spec(ZB_W)),
        out_shape=(jax.ShapeDtypeStruct((T, ZA_W), BF16), jax.ShapeDtypeStruct((T, ZB_W), BF16),
                   jax.ShapeDtypeStruct((1, ZA_W), F32), jax.ShapeDtypeStruct((1, ZB_W), F32)),
        scratch_shapes=[_token_scratch(ROW_BLK, W)],
        compiler_params=pltpu.CompilerParams(dimension_semantics=("arbitrary",)),
    )(*args)


def _place():
    x, y, c = lax.axis_index("x"), lax.axis_index("y"), lax.axis_index("c")
    chips = [(1 - x, y), (x, 1 - y), (1 - x, 1 - y)]
    return x, y, c, chips


def _cast_place(shards, k_idx, *, name):
    n, steps = len(shards), 4

    def body(k_ref, *refs):
        for s_ref, o_ref in zip(refs[:n], refs[n:]):
            o_ref[...] = s_ref[...].astype(o_ref.dtype)

    blocks = [(a.shape[0] // steps, a.shape[1]) for a in shards]
    grid_spec = pltpu.PrefetchScalarGridSpec(
        num_scalar_prefetch=1, grid=(steps,),
        in_specs=[pl.BlockSpec(blk, lambda t, k_ref: (t, 0)) for blk in blocks],
        out_specs=tuple(pl.BlockSpec(blk, lambda t, k_ref: (k_ref[0] * steps + t, 0)) for blk in blocks))
    return pl.pallas_call(
        body, name=name, grid_spec=grid_spec,
        out_shape=tuple(jax.ShapeDtypeStruct((N_CHIPS * a.shape[0], a.shape[1]), BF16) for a in shards),
        compiler_params=pltpu.CompilerParams(dimension_semantics=("parallel",),
                                             vmem_limit_bytes=_vmem_limit(*[_nbytes(b, F32) for b in blocks])),
    )(k_idx, *shards)


def _gather_carry(fulls, jobs):
    n = len(jobs)

    def piece(ref, chip_idx, half, part):
        rs = ref.shape[0] // N_CHIPS
        rh = rs // 2
        rows = rh // part[1]
        return ref.at[pl.ds(chip_idx * rs + half * rh + part[0] * rows, rows)]

    def copy(sems, sem, src_ref, dst_ref, to):
        return pltpu.make_async_remote_copy(src_ref=src_ref, dst_ref=dst_ref, send_sem=sems[0].at[sem],
                                            recv_sem=sems[1].at[sem], device_id=to, device_id_type=MESH)

    def to_chips(ins, outs, sems, j, arrival, hops=("both", "chips")):
        i, part, hop = jobs[j]
        x, y, c, chips = _place()
        res = []
        for t, chip in enumerate(chips if hop in hops else ()):
            theirs = piece(outs[i], 2 * chip[0] + chip[1], c, part)
            src, dst = (theirs, theirs) if arrival else (piece(ins[i], 2 * x + y, c, part), piece(outs[i], 2 * x + y, c, part))
            res.append(copy(sems, 3 * j + t, src, dst, (*chip, c)))
        return res

    def to_sibling(outs, sems, j, arrival, hops=("both", "sibling")):
        i, part, hop = jobs[j]
        x, y, c, chips = _place()
        res = []
        for t, chip in enumerate(chips if hop in hops else ()):
            region = piece(outs[i], 2 * chip[0] + chip[1], 1 - c if arrival else c, part)
            res.append(copy(sems, 3 * n + 3 * j + t, region, region, (x, y, 1 - c)))
        return res

    def start(ins, outs, sems):
        for j in range(n):
            for send in to_chips(ins, outs, sems, j, False) + to_sibling(outs, sems, j, False, ("sibling",)):
                send.start()

    def before_last(ins, outs, sems):
        for j in range(n):
            for arrival, send in zip(to_chips(ins, outs, sems, j, True, ("both",)),
                                     to_sibling(outs, sems, j, False, ("both",))):
                arrival.wait_recv()
                send.start()

    def finish(ins, outs, sems):
        for j in range(n):
            for arrival in to_chips(ins, outs, sems, j, True, ("chips",)) + to_sibling(outs, sems, j, True):
                arrival.wait_recv()
        for j in range(n):
            for send in to_chips(ins, outs, sems, j, False) + to_sibling(outs, sems, j, False):
                send.wait_send()

    return _Carry(fulls, [jax.ShapeDtypeStruct(a.shape, a.dtype) for a in fulls], {i: i for i in range(len(fulls))},
                  [pltpu.SemaphoreType.DMA((6 * n,)), pltpu.SemaphoreType.DMA((6 * n,))], start, finish, before_last)


def _pair_swap_carry(grads):
    n = len(grads)

    def copy(ins, outs, sems, i):
        x, y, c, _ = _place()
        return pltpu.make_async_remote_copy(src_ref=ins[i].at[:, 1 - c], dst_ref=outs[i], send_sem=sems[0].at[i],
                                            recv_sem=sems[1].at[i], device_id=(x, y, 1 - c), device_id_type=MESH)

    def start(ins, outs, sems):
        for i in range(n):
            copy(ins, outs, sems, i).start()

    def finish(ins, outs, sems):
        for i in range(n):
            copy(ins, outs, sems, i).wait()

    return _Carry(grads, [jax.ShapeDtypeStruct((a.shape[0], a.shape[2], a.shape[3]), a.dtype) for a in grads], {},
                  [pltpu.SemaphoreType.DMA((n,)), pltpu.SemaphoreType.DMA((n,))], start, finish)


def _pair_sum(g4, got, c_idx, *, name):
    _, _, rh, cs = g4.shape
    rb = _row_block(rh, 512, 16)

    def body(c_ref, own_ref, got_ref, o_ref):
        o_ref[...] = (own_ref[...].astype(F32) + got_ref[...].astype(F32)).astype(o_ref.dtype)

    grid_spec = pltpu.PrefetchScalarGridSpec(
        num_scalar_prefetch=1, grid=(N_CHIPS, rh // rb),
        in_specs=[pl.BlockSpec((None, None, rb, cs), lambda k, t, c_ref: (k, c_ref[0], t, 0)),
                  pl.BlockSpec((None, rb, cs), lambda k, t, c_ref: (k, t, 0))],
        out_specs=pl.BlockSpec((None, rb, cs), lambda k, t, c_ref: (k, t, 0)))
    return pl.pallas_call(
        body, name=name, grid_spec=grid_spec, out_shape=jax.ShapeDtypeStruct((N_CHIPS, rh, cs), BF16),
        compiler_params=pltpu.CompilerParams(dimension_semantics=("parallel", "parallel")),
    )(c_idx, g4, got)


def _chip_scatter_carry(sums):
    n = len(sums)

    def copies(ins, outs, sems):
        x, y, c, chips = _place()
        return [pltpu.make_async_remote_copy(src_ref=ins[i].at[2 * chip[0] + chip[1]], dst_ref=outs[i].at[j],
                                             send_sem=sems[0].at[3 * i + j], recv_sem=sems[1].at[3 * i + j],
                                             device_id=(*chip, c), device_id_type=MESH)
                for i in range(n) for j, chip in enumerate(chips)]

    def start(ins, outs, sems):
        for cp in copies(ins, outs, sems):
            cp.start()

    def finish(ins, outs, sems):
        for cp in copies(ins, outs, sems):
            cp.wait()

    return _Carry(sums, [jax.ShapeDtypeStruct((3,) + a.shape[1:], a.dtype) for a in sums], {},
                  [pltpu.SemaphoreType.DMA((3 * n,)), pltpu.SemaphoreType.DMA((3 * n,))], start, finish)


def _chip_sum(sums, gots, kc_idx, *, name):
    n, steps = len(sums), 2

    def body(kc_ref, *refs):
        for own_ref, got_ref, o_ref in zip(refs[:n], refs[n:2 * n], refs[2 * n:]):
            acc = own_ref[...].astype(F32)
            for j in range(3):
                acc = acc + got_ref[j].astype(F32)
            o_ref[...] = acc

    blocks = [(a.shape[1] // steps, a.shape[2]) for a in sums]
    grid_spec = pltpu.PrefetchScalarGridSpec(
        num_scalar_prefetch=1, grid=(steps,),
        in_specs=[pl.BlockSpec((None,) + blk, lambda t, kc: (kc[0], t, 0)) for blk in blocks]
        + [pl.BlockSpec((3,) + blk, lambda t, kc: (0, t, 0)) for blk in blocks],
        out_specs=tuple(pl.BlockSpec(blk, lambda t, kc: (kc[1] * steps + t, 0)) for blk in blocks))
    return pl.pallas_call(
        body, name=name, grid_spec=grid_spec,
        out_shape=tuple(jax.ShapeDtypeStruct((2 * a.shape[1], a.shape[2]), F32) for a in sums),
        compiler_params=pltpu.CompilerParams(dimension_semantics=("parallel",),
                                             vmem_limit_bytes=_vmem_limit(*[3 * _nbytes(b, F32) for b in blocks])),
    )(kc_idx, *sums, *gots)


def _half_swap_carry(shards):
    n = len(shards)

    def copy(ins, outs, sems, i, to_my_half):
        x, y, c, _ = _place()
        rh = ins[i].shape[0] // 2
        half = c if to_my_half else 1 - c
        return pltpu.make_async_remote_copy(
            src_ref=ins[i].at[pl.ds(c * rh, rh)], dst_ref=outs[i].at[pl.ds(half * rh, rh)],
            send_sem=sems[0].at[i], recv_sem=sems[1].at[i], device_id=(x, y, 1 - c), device_id_type=MESH)

    def start(ins, outs, sems):
        for i in range(n):
            copy(ins, outs, sems, i, True).start()

    def finish(ins, outs, sems):
        for i in range(n):
            copy(ins, outs, sems, i, False).wait_recv()
        for i in range(n):
            copy(ins, outs, sems, i, True).wait_send()

    return _Carry(shards, [jax.ShapeDtypeStruct(a.shape, a.dtype) for a in shards], {i: i for i in range(n)},
                  [pltpu.SemaphoreType.DMA((n,)), pltpu.SemaphoreType.DMA((n,))], start, finish)


SMALL_ROW = 1024


def _small_layout(shapes):
    starts, row = [], 0
    for r, c in shapes:
        starts.append(row)
        row += -(-c // SMALL_ROW) if r == 1 else r
    return starts, -(-row // SUBLANES) * SUBLANES


def _small_pieces(shape, start):
    r, c = shape
    if r == 1:
        return [(start + q, min(SMALL_ROW, c - q * SMALL_ROW), q * SMALL_ROW) for q in range(-(-c // SMALL_ROW))]
    return None


def _whole_spec(shape):
    return pl.BlockSpec(tuple(shape), lambda i: (0,) * len(shape))


def _small_all_reduce(parts, *, name, carry=None):
    shapes = [a.shape for a in parts]
    starts, rows = _small_layout(shapes)
    n = len(parts)
    n_tables = sum(1 for r, _ in shapes if r > 1)
    assert all(r > 1 and c <= LANES for r, c in shapes[:n_tables]) and all(r == 1 for r, _ in shapes[n_tables:])
    table_rows = starts[n_tables]
    assert table_rows % SUBLANES == 0

    def regions(slot):
        return slot.at[pl.ds(0, table_rows), pl.ds(0, LANES)], slot.at[pl.ds(table_rows, rows - table_rows)]

    def peer(d):
        x, y, c, _ = _place()
        return 1 - x if d & 4 else x, 1 - y if d & 2 else y, 1 - c if d & 1 else c

    def slot_of(d):
        px, py, pc = peer(d)
        return 4 * px + 2 * py + pc

    def copy(refs, d, part, dst_slot):
        buf, send_sems, recv_sems = refs[2 * n:]
        sem = 2 * (d - 1) + part
        return pltpu.make_async_remote_copy(
            src_ref=regions(buf.at[slot_of(0)])[part], dst_ref=regions(buf.at[dst_slot])[part],
            send_sem=send_sems.at[sem], recv_sem=recv_sems.at[sem], device_id=peer(d), device_id_type=MESH)

    def send(*refs):
        ins, mine = refs[0:n], refs[2 * n].at[slot_of(0)]
        mine[...] = jnp.zeros((rows, SMALL_ROW), F32)
        for ref, shape, start in zip(ins, shapes, starts):
            pieces = _small_pieces(shape, start)
            if pieces is None:
                mine[start:start + shape[0], 0:shape[1]] = ref[...]
            else:
                for row, width, col in pieces:
                    mine[row:row + 1, 0:width] = ref[:, col:col + width]
        for d in range(1, 8):
            for part in range(2):
                copy(refs, d, part, slot_of(0)).start()

    def receive(*refs):
        outs, buf = refs[n:2 * n], refs[2 * n]
        mine = buf.at[slot_of(0)]
        for d in range(1, 8):
            for part in range(2):
                copy(refs, d, part, slot_of(d)).wait_recv()
        for d in range(1, 8):
            for part in range(2):
                copy(refs, d, part, slot_of(0)).wait_send()
        tables, vectors = regions(buf.at[0])
        acc_t, acc_v = tables[...], vectors[...]
        for s in range(1, 8):
            tables, vectors = regions(buf.at[s])
            acc_t, acc_v = acc_t + tables[...], acc_v + vectors[...]
        tables, vectors = regions(mine)
        tables[...] = acc_t
        vectors[...] = acc_v
        for ref, shape, start in zip(outs, shapes, starts):
            pieces = _small_pieces(shape, start)
            if pieces is None:
                ref[...] = mine[start:start + shape[0], 0:shape[1]]
            else:
                for row, width, col in pieces:
                    ref[:, col:col + width] = mine[row:row + 1, 0:width]

    def body(*refs):
        send(*refs)
        receive(*refs)

    whole = [_whole_spec(s) for s in shapes]
    return _call(body if carry is None else receive, before_carry=None if carry is None else send,
                 name=name, grid=(1,), in_specs=whole, out_specs=whole,
                 out_shape=[jax.ShapeDtypeStruct(s, F32) for s in shapes], args=parts,
                 scratch=[pltpu.VMEM((8, rows, SMALL_ROW), F32), pltpu.SemaphoreType.DMA((14,)),
                          pltpu.SemaphoreType.DMA((14,))], carry=carry)


def _adamw_small(ws, gs, ms, vs, *, name):
    n = len(ws)
    c1 = 1.0 - ADAM_B1 ** ADAM_STEP
    c2 = 1.0 - ADAM_B2 ** ADAM_STEP

    def body(*refs):
        for k in range(n):
            w_ref, g_ref, m_ref, v_ref = (refs[t * n + k] for t in range(4))
            go_ref, d_ref, mo_ref, vo_ref = (refs[(4 + t) * n + k] for t in range(4))
            gv = g_ref[...]
            go_ref[...] = gv
            mn = ADAM_B1 * m_ref[...] + (1.0 - ADAM_B1) * gv
            vn = ADAM_B2 * v_ref[...] + (1.0 - ADAM_B2) * (gv * gv)
            d_ref[...] = -ADAM_LR * ((mn / c1) / (jnp.sqrt(vn / c2) + ADAM_EPS) + ADAM_WD * w_ref[...])
            mo_ref[...] = mn
            vo_ref[...] = vn

    whole = [_whole_spec(a.shape) for a in ws]
    shapes = tuple(jax.ShapeDtypeStruct(a.shape, F32) for a in ws)
    res = pl.pallas_call(
        body, name=name, grid=(1,), in_specs=whole * 4, out_specs=tuple(whole * 4), out_shape=shapes * 4,
    )(*ws, *gs, *ms, *vs)
    return res[0:n], res[n:2 * n], res[2 * n:3 * n], res[3 * n:4 * n]


def _adamw(ws, gs, ms, vs, *, name):
    n, steps = len(ws), 8
    c1 = 1.0 - ADAM_B1 ** ADAM_STEP
    c2 = 1.0 - ADAM_B2 ** ADAM_STEP

    def body(*refs):
        for k in range(n):
            w_ref, g_ref, m_ref, v_ref = (refs[t * n + k] for t in range(4))
            go_ref, d_ref, mo_ref, vo_ref = (refs[(4 + t) * n + k] for t in range(4))
            gv = g_ref[...]
            go_ref[...] = gv
            mn = ADAM_B1 * m_ref[...] + (1.0 - ADAM_B1) * gv
            vn = ADAM_B2 * v_ref[...] + (1.0 - ADAM_B2) * (gv * gv)
            d_ref[...] = -ADAM_LR * ((mn / c1) / (jnp.sqrt(vn / c2) + ADAM_EPS) + ADAM_WD * w_ref[...])
            mo_ref[...] = mn
            vo_ref[...] = vn

    blocks = [(a.shape[0] // steps, a.shape[1]) for a in ws]
    specs = [pl.BlockSpec(blk, lambda i: (i, 0)) for blk in blocks]
    shapes = tuple(jax.ShapeDtypeStruct(a.shape, F32) for a in ws)
    res = pl.pallas_call(
        body, name=name, grid=(steps,), in_specs=specs * 4, out_specs=tuple(specs * 4), out_shape=shapes * 4,
        compiler_params=pltpu.CompilerParams(dimension_semantics=("parallel",),
                                             vmem_limit_bytes=_vmem_limit(*[8 * _nbytes(b, F32) for b in blocks])),
    )(*ws, *gs, *ms, *vs)
    return res[0:n], res[n:2 * n], res[2 * n:3 * n], res[3 * n:4 * n]


def _local_step(x, p, target, small, sched):
    T = x.shape[0]
    W = MIX_W
    rel_bias = small["rel_bias"]
    b_in_a, b_in_b = small["b_in"][:, 0:ZA_W], small["b_in"][:, ZA_W:]

    idx_np = [_t5_index(A_WINDOW - 1, 1)] + [_t5_index(window // dil, dil) for window, dil in B_PATTERNS]
    idx_all = jnp.asarray(np.stack(idx_np))
    n1, bias_all = sched.run(_prologue, x, small["ffn1_pre_g"], rel_bias, idx_all, name="prologue")
    w_gu1 = sched.weight("ffn1_w_gu")
    *gu1, a1 = sched.run(_ffn_up, n1, w_gu1, bm=512, name="ffn1_gu")
    w_d1 = sched.weight("ffn1_w_down")
    f1, h1, n2 = sched.run(_mm_resid_norm, a1, w_d1, x, small["ffn1_post_g"], 0.5, small["attn_pre_g"], bm=512,
                           name="ffn1_down")
    win_a, win_b = sched.weight("w_in")
    za = _mm_nt(n2, win_a, bm=1024, bn=ZA_W, out_dtype=BF16, name="in_proj_a", add=b_in_a)
    dils = tuple(dil for _, dil in B_PATTERNS)
    zb_by_dil = _mm_nt_classes(n2, win_b, b_in_b, dils, bm=512, name="in_proj_b")

    a_args = dict(r=1, q_col=0, k_col=W // LANES, v_col=W // LANES + 1, stride=0, pattern=0, kv_shared=True)
    sink_row = jnp.repeat(small["sinks"], HEAD_DIM, axis=1)
    out_a, out_a_bf, lse_a = sched.run(_band_fwd, za, za, za, bias_all, name="band_a_fwd", sink=sink_row, **a_args)

    no_sink = jnp.full((1, W), NEG, F32)
    b_ctx, branches = [], []
    for t, dil in enumerate(dils):
        zb_r = zb_by_dil[t]
        args = dict(r=dil, q_col=0, k_col=1, v_col=2, stride=ZB_W // W, pattern=1 + t)
        branches.append(sched.run(_band_fwd, zb_r, zb_r, zb_r, bias_all, name=f"band_b{t}_fwd", **args))
        b_ctx.append((jnp.asarray(idx_np[1 + t]), zb_r, args))
    out_b, out_b_bf, lse_b = _merge(branches, dils, no_sink, name="merge_b")

    mix = jnp.concatenate([out_a_bf, out_b_bf], axis=1)
    w_out = sched.weight("w_out")
    att, h2, n3 = _mm_resid_norm(mix, w_out, h1, small["attn_post_g"], 1.0, small["ffn2_pre_g"], bm=512,
                                 name="out_proj", bias=small["b_out"])
    w_gu2, w_d2 = sched.weight("ffn2_w_gu"), sched.weight("ffn2_w_down")
    *gu2, a2 = _ffn_up(n3, w_gu2, bm=512, name="ffn2_gu")
    f2, h3, n4 = _mm_resid_norm(a2, w_d2, h2, small["ffn2_post_g"], 0.5, small["ple_pre_g"], bm=512,
                                name="ffn2_down")
    w_gate = sched.weight("w_ple_gate")
    p_bf = p.astype(BF16)
    dh4, de, dgp, loss, d_ple_post = _ple_head(n4, w_gate, p_bf, sched.weight("w_ple_proj"), h3, small["ple_post_g"],
                                               target, bm=512, name="ple_head")

    gs = {"ple_post_g": d_ple_post}
    sched.grad("w_ple_proj", _mm_tn(p_bf, de, bm=256, bn=1024, out_dtype=BF16, name="d_w_ple"))
    sched.grad("w_ple_gate", _mm_tn(n4, dgp, bm=512, bn=1024, out_dtype=BF16, name="d_w_gate"))
    dh3, df2, gs["ple_pre_g"], gs["ffn2_post_g"], _ = sched.run(
        _mm_nt_norms_bwd, dgp, w_gate, dh4, h3, small["ple_pre_g"], f2, small["ffn2_post_g"], 0.5, bm=512, name="d_n4")

    def ffn_bwd(df, a, gu, n, w_down, w_gu, tag, *norm_args):
        dgu = sched.run(_ffn_down_bwd, df, w_down, gu[0], gu[1], bm=512, name=f"ffn{tag}_d_a")
        sched.grad(f"ffn{tag}_w_gu", _mm_tn(n, dgu, bm=512, bn=1408, out_dtype=BF16, name=f"ffn{tag}_d_w_gu",
                                            n_stack=N_CHIPS))
        sched.grad(f"ffn{tag}_w_down", sched.run(_mm_tn, a, df, bm=256, bn=1024, out_dtype=BF16,
                                                name=f"ffn{tag}_d_w_down"))
        return sched.run(_mm_nt_norms_bwd, dgu, w_gu, *norm_args, bm=512, name=f"ffn{tag}_d_n")

    dh2, datt, gs["ffn2_pre_g"], gs["attn_post_g"], gs["b_out"] = ffn_bwd(
        df2, a2, gu2, n3, w_d2, w_gu2, "2", dh3, h2, small["ffn2_pre_g"], att, small["attn_post_g"], 1.0)
    sched.grad("w_out", _mm_tn(mix, datt, bm=512, bn=1024, out_dtype=BF16, name="d_w_out"))
    dmix = sched.run(_mm_nt, datt, w_out, bm=1024, bn=1024, out_dtype=F32, name="d_mix")

    do_a, stat_a, dsink, do_b, stat_b = _attn_delta(dmix, (out_a, out_b), (lse_a, lse_b), sink_row, dils,
                                                    name="attn_delta")
    gs["sinks"] = dsink[:, ::HEAD_DIM]
    dq_a, dk_a, dv_a, dbias_a = sched.run(_band_bwd, za, za, za, do_a, stat_a, bias_all, name="band_a_bwd", **a_args)
    d_rel_a = _bias_grad(dbias_a, jnp.asarray(idx_np[0]), name="d_rel_a")
    b_grads = []
    d_rel_b = None
    for t, (idx, zb_r, args) in enumerate(b_ctx):
        dq, dk, dv, dbias = sched.run(_band_bwd, zb_r, zb_r, zb_r, do_b[t], stat_b[t], bias_all,
                                      name=f"band_b{t}_bwd", **args)
        b_grads.append((dq, dk, dv))
        d_rel = _bias_grad(dbias, idx, name=f"d_rel_b{t}")
        d_rel_b = d_rel if d_rel_b is None else d_rel_b + d_rel
    gs["rel_bias"] = jnp.concatenate([d_rel_a[:, 0:2 * N_PAIRS], d_rel_b[:, 0:2 * N_PAIRS]], axis=1)
    dza, dzb, dsum_a, dsum_b = _dz_assemble(dq_a, dk_a, dv_a, b_grads, dils, name="d_z")
    gs["b_in"] = jnp.concatenate([dsum_a, dsum_b], axis=1)
    sched.grad("w_in", (_mm_tn(dza, n2, bm=ZA_W, bn=1024, out_dtype=BF16, name="d_w_in_a"),
                        _mm_tn(dzb, n2, bm=ZB_W // 2, bn=1024, out_dtype=BF16, name="d_w_in_b")))
    dn2_a = _mm_nn(dza, win_a, bm=1024, bn=1024, out_dtype=F32, name="d_n2_a")
    dh1, df1, gs["attn_pre_g"], gs["ffn1_post_g"], _ = sched.run(
        _mm_nt_norms_bwd, dzb, win_b, dh2, h1, small["attn_pre_g"], f1, small["ffn1_post_g"], 0.5, bm=512,
        name="d_n2_b", add=dn2_a, b_is_kn=True)
    grad_x, gs["ffn1_pre_g"] = ffn_bwd(df1, a1, gu1, n1, w_d1, w_gu1, "1", dh1, x, small["ffn1_pre_g"], None, None, 0.0)
    return loss, grad_x, gs


def _to_compute(name, full):
    st = full.reshape(N_CHIPS, full.shape[0] // N_CHIPS, full.shape[1])
    if name in ("ffn1_w_gu", "ffn2_w_gu"):
        return st
    if name == "w_ple_proj":
        return jnp.transpose(st, (1, 0, 2)).reshape(st.shape[1], -1)
    if name == "w_in":
        return full[0:ZA_W], full[ZA_W:]
    return full


def _to_comm(name, g):
    if name == "w_in":
        g = jnp.concatenate(g, axis=0)
    if name == "w_ple_proj":
        g = jnp.transpose(g.reshape(g.shape[0], N_CHIPS, -1), (1, 0, 2))
    rs = g.shape[1] if g.ndim == 3 else g.shape[0] // N_CHIPS
    return g.reshape(N_CHIPS, 2, rs // 2, g.shape[-1])


class _Schedule:
    GATHER = {
        "prologue": [("ffn1_w_gu", (0, 1), "both")],
        "ffn1_gu": [("ffn1_w_down", (0, 1), "both"), ("w_in", (0, 1), "both")],
        "band_a_fwd": [("ffn2_w_gu", (0, 2), "chips"), ("w_out", (0, 1), "chips")],
        "band_b0_fwd": [("ffn2_w_gu", (0, 2), "sibling"), ("w_out", (0, 1), "sibling"),
                        ("ffn2_w_gu", (1, 2), "chips"), ("w_ple_gate", (0, 1), "chips"), ("w_ple_proj", (0, 1), "chips")],
        "band_b1_fwd": [("ffn2_w_gu", (1, 2), "sibling"), ("w_ple_gate", (0, 1), "sibling"),
                        ("w_ple_proj", (0, 1), "sibling"), ("ffn2_w_down", (0, 1), "both")],
    }
    SWAP = {"ffn2_d_w_down": ["w_ple_proj", "w_ple_gate", "ffn2_w_gu"], "ffn2_d_n": ["ffn2_w_down"],
            "band_a_bwd": ["w_out"], "d_n2_b": ["w_in"], "ffn1_d_w_down": ["ffn1_w_gu"], "ffn1_d_n": ["ffn1_w_down"]}
    SCATTER = {"ffn2_d_n": ["ffn2_w_gu"], "band_a_bwd": ["w_ple_proj", "w_ple_gate", "ffn2_w_down"],
               "d_n2_b": ["w_out"], "ffn1_d_w_down": ["w_in"], "ffn1_d_n": ["ffn1_w_gu"]}
    HALF_SWAP = {"ffn1_d_n": ["w_ple_proj", "w_ple_gate", "ffn2_w_gu", "ffn2_w_down", "w_out", "w_in"]}

    def __init__(self, placed, c_idx, kc_idx):
        self.full = dict(placed)
        self.missing = {n: 1.0 for n in placed}
        self.c_idx, self.kc_idx = c_idx, kc_idx
        self.grads, self.swapped, self.pair_sums, self.scattered, self.summed = {}, {}, {}, {}, {}

    def _gather(self, entries):
        if not entries:
            return [], []
        names = list(dict.fromkeys(n for n, _, _ in entries))
        carry = _gather_carry([self.full[n] for n in names], [(names.index(n), pt, hops) for n, pt, hops in entries])

        def done():
            self.full.update(zip(names, carry.results))
            for n, part, hops in entries:
                if hops != "chips":
                    self.missing[n] -= 1.0 / part[1]
        return [carry], [done]

    def weight(self, name):
        assert abs(self.missing[name]) < 1e-9, (name, self.missing[name])
        return _to_compute(name, self.full[name])

    def grad(self, name, g):
        self.grads[name] = _to_comm(name, g)

    def _swap(self, names):
        carry = _pair_swap_carry([self.grads[n] for n in names])

        def done():
            self.swapped.update(zip(names, carry.results))
        return carry, done

    def _scatter(self, names):
        for n in names:
            self.pair_sums[n] = _pair_sum(self.grads[n], self.swapped[n], self.c_idx, name=f"grad_pair_sum_{n}")
        carry = _chip_scatter_carry([self.pair_sums[n] for n in names])

        def done():
            self.scattered.update(zip(names, carry.results))
        return carry, done

    def _half_swap(self, names):
        halves = _chip_sum([self.pair_sums[n] for n in names], [self.scattered[n] for n in names], self.kc_idx,
                           name=f"grad_chip_sum_{names[0]}")
        carry = _half_swap_carry(halves)

        def done():
            self.summed.update(zip(names, carry.results))
        return carry, done

    def run(self, fn, *args, name, **kw):
        carries, after = self._gather(self.GATHER.get(name, []))
        for names, make in ((self.SWAP.get(name), self._swap), (self.SCATTER.get(name), self._scatter),
                            (self.HALF_SWAP.get(name), self._half_swap)):
            if names:
                carry, done = make(names)
                carries.append(carry)
                after.append(done)
        out = fn(*args, name=name, carry=_join(carries), **kw)
        for done in after:
            done()
        return out

    def finish(self, last_carrier):
        left = [n for n in BIG if n not in self.scattered]
        to_swap = [n for n in left if n not in self.swapped]
        if to_swap:
            carry, done = self._swap(to_swap)
            _comm_call(carry, name="grad_pair_swap")
            done()
        carry, done = self._scatter(left) if left else (None, lambda: None)
        carried = last_carrier(carry=carry)
        done()
        carry, done = self._half_swap([n for n in BIG if n not in self.summed])
        _comm_call(carry, name="grad_half_swap")
        done()
        return self.summed, carried


def kernel(x, p, rel_bias, ffn1_pre_g, ffn1_w_gu, ffn1_w_down, ffn1_post_g, attn_pre_g, w_in, b_in, sinks, w_out, b_out, attn_post_g, ffn2_pre_g, ffn2_w_gu, ffn2_w_down, ffn2_post_g, ple_pre_g, w_ple_gate, w_ple_proj, ple_post_g, loss_target, m_rel_bias, m_ffn1_pre_g, m_ffn1_w_gu, m_ffn1_w_down, m_ffn1_post_g, m_attn_pre_g, m_w_in, m_b_in, m_sinks, m_w_out, m_b_out, m_attn_post_g, m_ffn2_pre_g, m_ffn2_w_gu, m_ffn2_w_down, m_ffn2_post_g, m_ple_pre_g, m_w_ple_gate, m_w_ple_proj, m_ple_post_g, v_rel_bias, v_ffn1_pre_g, v_ffn1_w_gu, v_ffn1_w_down, v_ffn1_post_g, v_attn_pre_g, v_w_in, v_b_in, v_sinks, v_w_out, v_b_out, v_attn_post_g, v_ffn2_pre_g, v_ffn2_w_gu, v_ffn2_w_down, v_ffn2_post_g, v_ple_pre_g, v_w_ple_gate, v_w_ple_proj, v_ple_post_g):
    given = dict(locals())
    w = {n: given[n] for n in WEIGHTS}
    m = {n: given["m_" + n] for n in WEIGHTS}
    v = {n: given["v_" + n] for n in WEIGHTS}
    c_pos = lax.axis_index("c").astype(jnp.int32)
    k_pos = (2 * lax.axis_index("x") + lax.axis_index("y")).astype(jnp.int32)
    c_idx, k_idx, kc_idx = c_pos.reshape(1), k_pos.reshape(1), jnp.stack([k_pos, c_pos])

    def shard(a, n):
        return jnp.transpose(a[0]) if n == "w_in" else a[0]

    def unshard(a, n):
        return (jnp.transpose(a) if n == "w_in" else a)[None]

    placed = _cast_place([shard(w[n], n) for n in BIG], k_idx, name="place_shards")
    sched = _Schedule(dict(zip(BIG, placed)), c_idx, kc_idx)
    small = {n: (w[n] if n == "rel_bias" else w[n].reshape(1, -1)) for n in SMALL}

    loss_part, grad_x, gs = _local_step(x[0], p[0, 0], loss_target[0], small, sched)

    grads_big, (*small_grads, loss_row) = sched.finish(
        partial(_small_all_reduce, [gs[n] for n in SMALL] + [loss_part], name="small_all_reduce"))
    loss = loss_row[0, 0]

    grads, delta, new_m, new_v = {}, {}, {}, {}
    res = _adamw([shard(w[n], n) for n in BIG], [grads_big[n] for n in BIG], [shard(m[n], n) for n in BIG],
                 [shard(v[n], n) for n in BIG], name="adamw_big")
    for n, gg, dd, mm, vv in zip(BIG, *res):
        grads[n], delta[n], new_m[n], new_v[n] = (unshard(a, n) for a in (gg, dd, mm, vv))
    res = _adamw_small([w[n] for n in SMALL], small_grads, [m[n] for n in SMALL], [v[n] for n in SMALL],
                       name="adamw_small")
    for name, gg, dd, mm, vv in zip(SMALL, *res):
        grads[name], delta[name], new_m[name], new_v[name] = gg, dd, mm, vv

    return (loss, grad_x[None], *[grads[n] for n in WEIGHTS], *[delta[n] for n in WEIGHTS],
            *[new_m[n] for n in WEIGHTS], *[new_v[n] for n in WEIGHTS])
```

```python
import math
from functools import partial

import jax
import jax.numpy as jnp
import numpy as np
from jax import lax
from jax.experimental import pallas as pl
from jax.experimental.pallas import tpu as pltpu

F32 = jnp.float32
BF16 = jnp.bfloat16
MESH = pl.DeviceIdType.MESH

EPS = 1e-6
NEG = -1e30
LANES = 128
SUBLANES = 8
HEAD_DIM = 64
QBLK = 128
N_PAIRS = 4
MIX_W = N_PAIRS * LANES
A_WINDOW = 128
B_PATTERNS = ((128, 1), (512, 4), (2048, 16))
NUM_BUCKETS = 32
MAX_DISTANCE = 2048
ZA_W = 768
ZB_W = 1536
N_CHIPS = 4
VMEM_BYTES_V7X = 64 * 2**20
VMEM_CLAIM = VMEM_BYTES_V7X - (6 << 20)

ADAM_LR, ADAM_B1, ADAM_B2, ADAM_EPS, ADAM_WD, ADAM_STEP = 0.001, 0.9, 0.999, 1e-08, 0.01, 10

BIG = ("ffn1_w_gu", "ffn1_w_down", "w_in", "w_out", "ffn2_w_gu", "ffn2_w_down", "w_ple_gate", "w_ple_proj")
SMALL = ("rel_bias", "ffn1_pre_g", "ffn1_post_g", "attn_pre_g", "b_in", "sinks", "b_out", "attn_post_g",
         "ffn2_pre_g", "ffn2_post_g", "ple_pre_g", "ple_post_g")
WEIGHTS = ("rel_bias", "ffn1_pre_g", "ffn1_w_gu", "ffn1_w_down", "ffn1_post_g", "attn_pre_g", "w_in", "b_in",
           "sinks", "w_out", "b_out", "attn_post_g", "ffn2_pre_g", "ffn2_w_gu", "ffn2_w_down", "ffn2_post_g",
           "ple_pre_g", "w_ple_gate", "w_ple_proj", "ple_post_g")


def _vmem_limit(*block_bytes):
    del block_bytes
    return VMEM_CLAIM


def _nbytes(shape, dtype):
    return int(np.prod(shape)) * jnp.dtype(dtype).itemsize


MXU_WIDTH_V7X = 256


def _col_chunks(n):
    return [slice(c, min(c + MXU_WIDTH_V7X, n)) for c in range(0, n, MXU_WIDTH_V7X)]


def _row_halves(rows):
    return [slice(0, rows // 2), slice(rows // 2, rows)]


def _row_block(rows, cap, mult):
    for cand in range(min(rows, cap), 0, -1):
        if rows % cand == 0 and cand % mult == 0:
            return cand
    raise ValueError((rows, cap, mult))


class _Carry:
    def __init__(self, operands, out_shapes, aliases, sems, start, finish, before_last=None):
        self.operands, self.out_shapes, self.aliases, self.sems = list(operands), list(out_shapes), dict(aliases), list(sems)
        self.start, self.finish = start, finish
        self.before_last = before_last or (lambda ins, outs, sems: None)
        self.results = None


def _join(carries):
    carries = [c for c in carries if c is not None]
    if not carries:
        return None
    if len(carries) == 1:
        return carries[0]
    offs, pos = [], [0, 0, 0]
    for c in carries:
        offs.append(tuple(pos))
        pos = [pos[0] + len(c.operands), pos[1] + len(c.out_shapes), pos[2] + len(c.sems)]
    aliases = {}
    for c, (oi, oo, _) in zip(carries, offs):
        aliases.update({oi + i: oo + o for i, o in c.aliases.items()})

    def run(which):
        def fn(ins, outs, sems):
            for c, (oi, oo, os_) in zip(carries, offs):
                getattr(c, which)(ins[oi:oi + len(c.operands)], outs[oo:oo + len(c.out_shapes)],
                                  sems[os_:os_ + len(c.sems)])
        return fn

    joined = _Carry([a for c in carries for a in c.operands], [s for c in carries for s in c.out_shapes], aliases,
                    [s for c in carries for s in c.sems], run("start"), run("finish"), run("before_last"))
    joined.parts = (carries, offs)
    return joined


def _set_results(carry, outs):
    carry.results = list(outs)
    if hasattr(carry, "parts"):
        for c, (_, oo, _) in zip(*carry.parts):
            _set_results(c, outs[oo:oo + len(c.out_shapes)])


ANY = pl.BlockSpec(memory_space=pl.ANY)


def _call(body, *, name, grid, in_specs, out_specs, out_shape, args, scratch=(), vmem_limit=None, carry=None,
          semantics=None, before_carry=None):
    assert before_carry is None or carry is not None
    n_ci, n_co, n_cs = len(args), len(out_shape), len(scratch)
    semantics = semantics or ("parallel",) * len(grid)
    vmem_limit = vmem_limit or VMEM_CLAIM
    if carry is None:
        res = pl.pallas_call(
            body, name=name, grid=grid, in_specs=list(in_specs), out_specs=tuple(out_specs), out_shape=tuple(out_shape),
            scratch_shapes=list(scratch),
            compiler_params=pltpu.CompilerParams(dimension_semantics=semantics, vmem_limit_bytes=vmem_limit),
        )(*args)
        return list(res)
    n_pi, n_po = len(carry.operands), len(carry.out_shapes)

    def full_body(*refs):
        ci, pi = refs[:n_ci], refs[n_ci:n_ci + n_pi]
        o0 = n_ci + n_pi
        co, po = refs[o0:o0 + n_co], refs[o0 + n_co:o0 + n_co + n_po]
        s0 = o0 + n_co + n_po
        cs, ps = refs[s0:s0 + n_cs], refs[s0 + n_cs:]
        ids = [pl.program_id(ax) for ax in range(len(grid))]
        first, last = ids[0] == 0, ids[0] == grid[0] - 1
        for ax in range(1, len(grid)):
            first, last = first & (ids[ax] == 0), last & (ids[ax] == grid[ax] - 1)

        @pl.when(first)
        def _():
            if before_carry is not None:
                before_carry(*ci, *co, *cs)
            carry.start(pi, po, ps)

        several_steps = any(g > 1 for g in grid)
        if several_steps:
            @pl.when(last)
            def _():
                carry.before_last(pi, po, ps)

        body(*ci, *co, *cs)

        @pl.when(last)
        def _():
            if not several_steps:
                carry.before_last(pi, po, ps)
            carry.finish(pi, po, ps)

    res = pl.pallas_call(
        full_body, name=name, grid=grid, in_specs=list(in_specs) + [ANY] * n_pi,
        out_specs=tuple(out_specs) + tuple([ANY] * n_po), out_shape=tuple(out_shape) + tuple(carry.out_shapes),
        scratch_shapes=list(scratch) + carry.sems,
        input_output_aliases={n_ci + i: n_co + o for i, o in carry.aliases.items()},
        compiler_params=pltpu.CompilerParams(dimension_semantics=("arbitrary",) * len(grid),
                                             vmem_limit_bytes=vmem_limit),
    )(*args, *carry.operands)
    _set_results(carry, res[n_co:])
    return list(res[:n_co])


def _comm_call(carry, *, name):
    n_pi, n_po = len(carry.operands), len(carry.out_shapes)

    def body(*refs):
        pi, po, ps = refs[:n_pi], refs[n_pi:n_pi + n_po], refs[n_pi + n_po:]
        carry.start(pi, po, ps)
        carry.before_last(pi, po, ps)
        carry.finish(pi, po, ps)

    res = pl.pallas_call(
        body, name=name, in_specs=[ANY] * n_pi, out_specs=tuple([ANY] * n_po), out_shape=tuple(carry.out_shapes),
        scratch_shapes=carry.sems, input_output_aliases=dict(carry.aliases),
    )(*carry.operands)
    _set_results(carry, res)


def _mm_nn(a, b, *, bm, bn, out_dtype, name, bias=None, carry=None):
    M, K = a.shape
    stacked = b.ndim == 3
    N = b.shape[0] * b.shape[2] if stacked else b.shape[1]
    assert M % bm == 0 and N % bn == 0 and (not stacked or bn == b.shape[2])

    def body(*refs):
        a_ref, b_ref = refs[0], refs[1]
        o_ref = refs[-1]
        acc = jnp.dot(a_ref[...], b_ref[...], preferred_element_type=F32)
        if bias is not None:
            acc = acc + refs[2][...]
        o_ref[...] = acc.astype(o_ref.dtype)

    if stacked:
        b_spec = pl.BlockSpec((None, K, bn), lambda i, j: (j, 0, 0))
    else:
        b_spec = pl.BlockSpec((K, bn), lambda i, j: (0, j))
    in_specs = [pl.BlockSpec((bm, K), lambda i, j: (i, 0)), b_spec]
    args = [a, b]
    if bias is not None:
        in_specs.append(pl.BlockSpec((1, bn), lambda i, j: (0, j)))
        args.append(bias)
    limit = _vmem_limit(_nbytes((bm, K), a.dtype), _nbytes((K, bn), b.dtype), _nbytes((bm, bn), F32))
    return _call(body, name=name, grid=(M // bm, N // bn), in_specs=in_specs,
                 out_specs=[pl.BlockSpec((bm, bn), lambda i, j: (i, j))],
                 out_shape=[jax.ShapeDtypeStruct((M, N), out_dtype)], args=args, vmem_limit=limit, carry=carry)[0]


def _mm_nt(a, b, *, bm, bn, out_dtype, name, add=None, carry=None):
    M, K = a.shape
    stacked = b.ndim == 3
    N = b.shape[1] if stacked else b.shape[0]
    n_sh = b.shape[0] if stacked else 1
    ks = b.shape[2] if stacked else K
    assert M % bm == 0 and N % bn == 0 and n_sh * ks == K
    dn = (((1,), (1,)), ((), ()))

    def body(*refs):
        a_ref, b_ref = refs[0], refs[1]
        o_ref = refs[-1]
        if stacked:
            acc = lax.dot_general(a_ref[:, 0:ks], b_ref[0], dn, preferred_element_type=F32)
            for s in range(1, n_sh):
                acc = acc + lax.dot_general(a_ref[:, s * ks:(s + 1) * ks], b_ref[s], dn, preferred_element_type=F32)
        else:
            acc = lax.dot_general(a_ref[...], b_ref[...], dn, preferred_element_type=F32)
        if add is not None:
            acc = acc + refs[2][...]
        o_ref[...] = acc.astype(o_ref.dtype)

    if stacked:
        b_spec = pl.BlockSpec((n_sh, bn, ks), lambda i, j: (0, j, 0))
    else:
        b_spec = pl.BlockSpec((bn, K), lambda i, j: (j, 0))
    in_specs = [pl.BlockSpec((bm, K), lambda i, j: (i, 0)), b_spec]
    args = [a, b]
    if add is not None:
        rows = bm if add.shape[0] == M else 1
        in_specs.append(pl.BlockSpec((rows, bn), lambda i, j: (i if rows == bm else 0, j)))
        args.append(add)
    limit = _vmem_limit(_nbytes((bm, K), a.dtype), _nbytes((bn, K), b.dtype), _nbytes((bm, bn), F32))
    return _call(body, name=name, grid=(M // bm, N // bn), in_specs=in_specs,
                 out_specs=[pl.BlockSpec((bm, bn), lambda i, j: (i, j))],
                 out_shape=[jax.ShapeDtypeStruct((M, N), out_dtype)], args=args, vmem_limit=limit, carry=carry)[0]


def _mm_tn(a, b, *, bm, bn, out_dtype, name, n_stack=None, carry=None):
    T, M = a.shape
    N = b.shape[1]
    assert M % bm == 0 and N % bn == 0 and (n_stack is None or bn * n_stack == N)
    dn = (((0,), (0,)), ((), ()))

    def body(a_ref, b_ref, o_ref):
        acc = lax.dot_general(a_ref[...], b_ref[...], dn, preferred_element_type=F32)
        o_ref[...] = acc.astype(o_ref.dtype)

    if n_stack is None:
        out_spec = pl.BlockSpec((bm, bn), lambda i, j: (i, j))
        out_shape = jax.ShapeDtypeStruct((M, N), out_dtype)
    else:
        out_spec = pl.BlockSpec((None, bm, bn), lambda i, j: (j, i, 0))
        out_shape = jax.ShapeDtypeStruct((n_stack, M, bn), out_dtype)
    limit = _vmem_limit(_nbytes((T, bm), a.dtype), _nbytes((T, bn), b.dtype), _nbytes((bm, bn), F32))
    return _call(body, name=name, grid=(M // bm, N // bn),
                 in_specs=[pl.BlockSpec((T, bm), lambda i, j: (0, i)), pl.BlockSpec((T, bn), lambda i, j: (0, j))],
                 out_specs=[out_spec], out_shape=[out_shape], args=[a, b], vmem_limit=limit, carry=carry)[0]


ROW_BLK = 256


def _rstd(x):
    return lax.rsqrt(jnp.mean(x * x, axis=-1, keepdims=True) + EPS)


def _norm_bwd(xhat, rstd, dxhat):
    return rstd * (dxhat - xhat * jnp.mean(dxhat * xhat, axis=-1, keepdims=True))


def _row_spec(cols):
    return pl.BlockSpec((ROW_BLK, cols), lambda i: (i, 0))


def _vec_spec(cols):
    return pl.BlockSpec((1, cols), lambda i: (0, 0))


def _prologue(x, g, rel_bias, idx_all, *, name, carry=None):
    T, D = x.shape
    n_pat = idx_all.shape[0]
    heads = 2 * N_PAIRS
    steps = n_pat * heads
    rows = T // steps

    def body(rb_ref, x_ref, g_ref, idx_ref, n_ref, bias_ref):
        s = pl.program_id(0)
        head = jnp.where(s < heads, s, heads + s % heads)
        xv = x_ref[...]
        n_ref[...] = (xv * _rstd(xv) * g_ref[...]).astype(n_ref.dtype)
        idxv = idx_ref[...]
        acc = jnp.where(idxv < 0, NEG, 0.0).astype(F32)
        for b in range(NUM_BUCKETS):
            acc = acc + jnp.where(idxv == b, rb_ref[b, head], 0.0)
        bias_ref[0] = acc
        kap = lax.broadcasted_iota(jnp.int32, (1, 2 * QBLK), 1)
        bias_ref[1] = jnp.where(kap < QBLK, NEG, acc)

    return _call(
        body, name=name, grid=(steps,),
        in_specs=[pl.BlockSpec(memory_space=pltpu.SMEM), pl.BlockSpec((rows, D), lambda s: (s, 0)),
                  pl.BlockSpec((1, D), lambda s: (0, 0)),
                  pl.BlockSpec((None, QBLK, 2 * QBLK), lambda s: (s // heads, 0, 0))],
        out_specs=[pl.BlockSpec((rows, D), lambda s: (s, 0)),
                   pl.BlockSpec((None, 2, None, QBLK, 2 * QBLK), lambda s: (s // heads, 0, s % heads, 0, 0))],
        out_shape=[jax.ShapeDtypeStruct((T, D), BF16),
                   jax.ShapeDtypeStruct((n_pat, 2, heads, QBLK, 2 * QBLK), F32)],
        args=[rel_bias, x, g, idx_all], carry=carry)


def _mm_resid_norm(a, b, h, g_post, coef, g_next, *, bm, name, bias=None, carry=None):
    M, K = a.shape
    N = b.shape[1]

    def body(*refs):
        a_ref, b_ref, h_ref, gp_ref, gn_ref = refs[0:5]
        f_ref, hn_ref, n_ref = refs[-3:]
        for rows in _row_halves(bm):
            f = jnp.dot(a_ref[rows, :], b_ref[...], preferred_element_type=F32)
            if bias is not None:
                f = f + refs[5][...]
            f_ref[rows, :] = f
            hn = h_ref[rows, :] + coef * (f * _rstd(f) * gp_ref[...])
            hn_ref[rows, :] = hn
            n_ref[rows, :] = (hn * _rstd(hn) * gn_ref[...]).astype(n_ref.dtype)

    row = lambda w: pl.BlockSpec((bm, w), lambda i: (i, 0))
    vec = pl.BlockSpec((1, N), lambda i: (0, 0))
    in_specs = [row(K), pl.BlockSpec((K, N), lambda i: (0, 0), pipeline_mode=pl.Buffered(1)), row(N), vec, vec]
    args = [a, b, h, g_post, g_next]
    if bias is not None:
        in_specs.append(vec)
        args.append(bias)
    limit = _vmem_limit(_nbytes((bm, K), a.dtype), _nbytes((K, N), b.dtype) // 2, 4 * _nbytes((bm, N), F32))
    return _call(body, name=name, grid=(M // bm,), in_specs=in_specs, out_specs=[row(N), row(N), row(N)],
                 out_shape=[jax.ShapeDtypeStruct((M, N), F32), jax.ShapeDtypeStruct((M, N), F32),
                            jax.ShapeDtypeStruct((M, N), BF16)], args=args, vmem_limit=limit, carry=carry)


def _mm_nt_norms_bwd(a, b, dh_in, h, g_pre, f, g_post, coef, *, bm, name, add=None, b_is_kn=False, carry=None):
    M, K = a.shape
    stacked = b.ndim == 3
    N = b.shape[1] if (stacked or b_is_kn) else b.shape[0]
    n_sh = b.shape[0] if stacked else 1
    ks = b.shape[2] if stacked else K
    assert n_sh * ks == K
    dn_dims = (((1,), (0,)), ((), ())) if b_is_kn else (((1,), (1,)), ((), ()))
    with_f = f is not None
    n_in = 5 + (2 if with_f else 0) + (1 if add is not None else 0)

    def body(*refs):
        a_ref, b_ref, dhi_ref, h_ref, gpre_ref = refs[0:5]
        outs = refs[n_in:]

        @pl.when(pl.program_id(0) == 0)
        def _():
            for acc in (outs[2:] if with_f else outs[1:]):
                acc[...] = jnp.zeros_like(acc)

        for rows in _row_halves(bm):
            if stacked:
                dn = lax.dot_general(a_ref[rows, 0:ks], b_ref[0], dn_dims, preferred_element_type=F32)
                for s in range(1, n_sh):
                    dn = dn + lax.dot_general(a_ref[rows, s * ks:(s + 1) * ks], b_ref[s], dn_dims,
                                              preferred_element_type=F32)
            else:
                dn = lax.dot_general(a_ref[rows, :], b_ref[...], dn_dims, preferred_element_type=F32)
            if add is not None:
                dn = dn + refs[n_in - 1][rows, :]
            hv = h_ref[rows, :]
            r = _rstd(hv)
            hh = hv * r
            dh = dhi_ref[rows, :] + _norm_bwd(hh, r, dn * gpre_ref[...])
            outs[0][rows, :] = dh
            if not with_f:
                outs[1][...] += jnp.sum(dn * hh, axis=0, keepdims=True)
                continue
            f_ref, gpost_ref = refs[5], refs[6]
            df_ref, dgpre_ref, dgpost_ref, dsum_ref = outs[1:]
            dgpre_ref[...] += jnp.sum(dn * hh, axis=0, keepdims=True)
            fv = f_ref[rows, :]
            rf = _rstd(fv)
            fh = fv * rf
            dy = coef * dh
            dgpost_ref[...] += jnp.sum(dy * fh, axis=0, keepdims=True)
            df = _norm_bwd(fh, rf, dy * gpost_ref[...])
            dsum_ref[...] += jnp.sum(df, axis=0, keepdims=True)
            df_ref[rows, :] = df.astype(df_ref.dtype)

    row = lambda w: pl.BlockSpec((bm, w), lambda i: (i, 0))
    vec = pl.BlockSpec((1, N), lambda i: (0, 0))
    once = pl.Buffered(1)
    if stacked:
        b_spec = pl.BlockSpec((n_sh, N, ks), lambda i: (0, 0, 0), pipeline_mode=once)
    else:
        b_spec = pl.BlockSpec(b.shape, lambda i: (0, 0), pipeline_mode=once)
    in_specs = [row(K), b_spec, row(N), row(N), vec]
    args = [a, b, dh_in, h, g_pre]
    if with_f:
        in_specs += [row(N), vec]
        args += [f, g_post]
    if add is not None:
        in_specs.append(row(N))
        args.append(add)
    vec_shape = jax.ShapeDtypeStruct((1, N), F32)
    out_specs = [row(N)] + ([row(N), vec, vec, vec] if with_f else [vec])
    out_shape = [jax.ShapeDtypeStruct((M, N), F32)]
    out_shape += [jax.ShapeDtypeStruct((M, N), BF16), vec_shape, vec_shape, vec_shape] if with_f else [vec_shape]
    limit = _vmem_limit(_nbytes((bm, K), a.dtype), _nbytes((N, K), b.dtype) // 2, 6 * _nbytes((bm, N), F32))
    return _call(body, name=name, grid=(M // bm,), in_specs=in_specs, out_specs=out_specs, out_shape=out_shape,
                 args=args, vmem_limit=limit, semantics=("arbitrary",), carry=carry)


def _ffn_up(n, w_gu, *, bm, name, carry=None):
    M, K = n.shape
    n_sh, _, ns = w_gu.shape
    half = n_sh // 2

    def body(n_ref, wg_ref, wu_ref, g_ref, u_ref, a_ref):
        nv = n_ref[...]
        for cols in _col_chunks(ns):
            g = jnp.dot(nv, wg_ref[:, cols], preferred_element_type=F32)
            u = jnp.dot(nv, wu_ref[:, cols], preferred_element_type=F32)
            g_ref[:, cols] = g.astype(g_ref.dtype)
            u_ref[:, cols] = u.astype(u_ref.dtype)
            a_ref[:, cols] = (g * jax.nn.sigmoid(g) * u).astype(a_ref.dtype)

    out_spec = pl.BlockSpec((bm, ns), lambda i, j: (i, j))
    out = jax.ShapeDtypeStruct((M, half * ns), BF16)
    limit = _vmem_limit(_nbytes((bm, K), n.dtype), 2 * _nbytes((K, ns), w_gu.dtype), 3 * _nbytes((bm, ns), F32))
    return _call(body, name=name, grid=(M // bm, half),
                 in_specs=[pl.BlockSpec((bm, K), lambda i, j: (i, 0)),
                           pl.BlockSpec((None, K, ns), lambda i, j: (j, 0, 0)),
                           pl.BlockSpec((None, K, ns), lambda i, j: (j + half, 0, 0))],
                 out_specs=[out_spec, out_spec, out_spec], out_shape=[out, out, out], args=[n, w_gu, w_gu],
                 vmem_limit=limit, carry=carry)


def _ffn_down_bwd(df, w_down, g, u, *, bm, name, carry=None):
    M, D = df.shape
    F = w_down.shape[0]
    dn = (((1,), (1,)), ((), ()))

    def body(df_ref, w_ref, g_ref, u_ref, o_ref):
        dfv = df_ref[...]
        for cols in _col_chunks(F):
            da = lax.dot_general(dfv, w_ref[cols, :], dn, preferred_element_type=F32)
            gv = g_ref[:, cols].astype(F32)
            s = jax.nn.sigmoid(gv)
            o_ref[:, cols] = (da * u_ref[:, cols].astype(F32) * (s * (1.0 + gv * (1.0 - s)))).astype(o_ref.dtype)
            o_ref[:, F + cols.start:F + cols.stop] = (da * (gv * s)).astype(o_ref.dtype)

    row = lambda w: pl.BlockSpec((bm, w), lambda i: (i, 0))
    limit = _vmem_limit(_nbytes((F, D), w_down.dtype) // 2, 2 * _nbytes((bm, F), BF16), _nbytes((bm, 2 * F), BF16))
    return _call(body, name=name, grid=(M // bm,),
                 in_specs=[row(D), pl.BlockSpec((F, D), lambda i: (0, 0), pipeline_mode=pl.Buffered(1)), row(F), row(F)],
                 out_specs=[row(2 * F)], out_shape=[jax.ShapeDtypeStruct((M, 2 * F), BF16)],
                 args=[df, w_down, g, u], vmem_limit=limit, carry=carry)[0]


def _ple_head(n4, w_gate, p, w_ple, h3, g_post, target, *, bm, name):
    T, D = h3.shape
    kp = p.shape[1]

    def body(n_ref, wg_ref, p_ref, wp_ref, h_ref, g_ref, t_ref, dh_ref, de_ref, dgp_ref, loss_ref, dg_ref):
        @pl.when(pl.program_id(0) == 0)
        def _():
            loss_ref[...] = jnp.zeros_like(loss_ref)
            dg_ref[...] = jnp.zeros_like(dg_ref)

        gpost = g_ref[...]
        for rows in _row_halves(bm):
            gate = jax.nn.sigmoid(jnp.dot(n_ref[rows, :], wg_ref[...], preferred_element_type=F32))
            ev = jnp.dot(p_ref[rows, :], wp_ref[...], preferred_element_type=F32)
            ge = gate * ev
            r = _rstd(ge)
            gh = ge * r
            diff = h_ref[rows, :] + gh * gpost - t_ref[rows, :]
            loss_ref[...] += jnp.sum(diff * diff) * (0.5 / D)
            dh = diff * (1.0 / D)
            dh_ref[rows, :] = dh
            dg_ref[...] += jnp.sum(dh * gh, axis=0, keepdims=True)
            dge = _norm_bwd(gh, r, dh * gpost)
            de_ref[rows, :] = (dge * gate).astype(de_ref.dtype)
            dgp_ref[rows, :] = (dge * ev * gate * (1.0 - gate)).astype(dgp_ref.dtype)

    row = lambda w: pl.BlockSpec((bm, w), lambda i: (i, 0))
    whole = lambda a: pl.BlockSpec(a.shape, lambda i: (0, 0), pipeline_mode=pl.Buffered(1))
    limit = _vmem_limit(_nbytes((bm, D), BF16), _nbytes(w_gate.shape, w_gate.dtype) // 2, 6 * _nbytes((bm, D), F32))
    return pl.pallas_call(
        body, name=name, grid=(T // bm,),
        in_specs=[row(D), whole(w_gate), row(kp), whole(w_ple), row(D), _vec_spec(D), row(D)],
        out_specs=(row(D), row(D), row(D), _vec_spec(LANES), _vec_spec(D)),
        out_shape=(jax.ShapeDtypeStruct((T, D), F32), jax.ShapeDtypeStruct((T, D), BF16),
                   jax.ShapeDtypeStruct((T, D), BF16), jax.ShapeDtypeStruct((1, LANES), F32),
                   jax.ShapeDtypeStruct((1, D), F32)),
        compiler_params=pltpu.CompilerParams(dimension_semantics=("arbitrary",), vmem_limit_bytes=limit),
    )(n4, w_gate, p, w_ple, h3, g_post, target)


def _t5_index(max_dist, stride):
    rho = np.arange(QBLK)[:, None]
    kap = np.arange(2 * QBLK)[None, :]
    d = rho + QBLK - kap
    valid = (d >= 0) & (d <= max_dist)
    n = np.maximum(d, 0) * stride
    max_exact = NUM_BUCKETS // 2
    nf = np.maximum(n, 1).astype(np.float32)
    large = max_exact + (np.log(nf / np.float32(max_exact)) / np.float32(math.log(MAX_DISTANCE / max_exact))
                         * np.float32(NUM_BUCKETS - max_exact)).astype(np.int32)
    large = np.minimum(large, NUM_BUCKETS - 1)
    bucket = np.where(n < max_exact, n, large)
    return np.where(valid, bucket, -1).astype(np.int32)


def _bias_grad(d_bias, idx, *, name):
    heads = 2 * N_PAIRS

    def body(db_ref, idx_ref, o_ref, part_ref):
        idxv = idx_ref[...]
        for b in range(NUM_BUCKETS):
            in_bucket = idxv == b
            for h in range(heads):
                masked = jnp.where(in_bucket, db_ref[h], 0.0)
                part_ref[h, b] = jnp.sum(masked.reshape(QBLK // SUBLANES, SUBLANES, 2 * QBLK), axis=0)
        lanes = lax.broadcasted_iota(jnp.int32, (NUM_BUCKETS, LANES), 1)
        acc = jnp.zeros((NUM_BUCKETS, LANES), F32)
        for h in range(heads):
            sums = jnp.sum(jnp.sum(part_ref[h], axis=1), axis=1, keepdims=True)
            acc = acc + jnp.where(lanes == h, sums, 0.0)
        o_ref[...] = acc

    return pl.pallas_call(
        body, name=name, grid=(1,),
        in_specs=[pl.BlockSpec((heads, QBLK, 2 * QBLK), lambda i: (0, 0, 0)),
                  pl.BlockSpec((QBLK, 2 * QBLK), lambda i: (0, 0))],
        out_specs=pl.BlockSpec((NUM_BUCKETS, LANES), lambda i: (0, 0)),
        out_shape=jax.ShapeDtypeStruct((NUM_BUCKETS, LANES), F32),
        scratch_shapes=[pltpu.VMEM((heads, NUM_BUCKETS, SUBLANES, 2 * QBLK), F32)],
        compiler_params=pltpu.CompilerParams(dimension_semantics=("arbitrary",)),
    )(d_bias, idx)


def _lane():
    return lax.broadcasted_iota(jnp.int32, (1, LANES), 1)


def _head_sel(h):
    return (_lane() < HEAD_DIM) if h == 0 else (_lane() >= HEAD_DIM)


def _stat_lo():
    return (_lane() % HEAD_DIM) < (HEAD_DIM // 2)


def _stack_heads(t, scale=None):
    if scale is not None:
        t = t * scale
    zero = jnp.zeros_like(t)
    return jnp.concatenate([jnp.where(_head_sel(0), t, zero), jnp.where(_head_sel(1), t, zero)], axis=0)


def _class_spec(L, r, cols, stride, pps):
    chunks = N_PAIRS // pps
    mode = pl.Buffered(1) if r * chunks == 1 else None
    return pl.BlockSpec((L, MIX_W // chunks), lambda j, c: (0, (cols + j * stride) * chunks + c), pipeline_mode=mode)


def _rows(b, n_blocks=1):
    return pl.ds(pl.multiple_of(b * QBLK, QBLK), n_blocks * QBLK)


def _key_window(ref, b, cols, first, kv_head=None):
    if kv_head is not None:
        cols = slice(0, LANES)
    if first:
        blk = ref[0:QBLK, cols]
        tile = jnp.concatenate([blk, blk], axis=0)
    else:
        tile = ref[_rows(b - 1, 2), cols]
    if kv_head is None:
        return tile
    wide = tile.astype(F32)
    keep = jnp.logical_xor(_lane() < HEAD_DIM, kv_head == 1)
    return jnp.where(keep, wide, pltpu.roll(wide, HEAD_DIM, 1)).astype(tile.dtype)


def _kv_spec(L, r, col, stride, pps, kv_shared):
    if not kv_shared:
        return _class_spec(L, r, col, stride, pps)
    mode = pl.Buffered(1) if pps == N_PAIRS else None
    return pl.BlockSpec((L, LANES), lambda j, c: (0, col), pipeline_mode=mode)


def _band_fwd(qa, ka, va, bias, *, r, q_col, k_col, v_col, stride, pattern, name, kv_shared=False, sink=None,
              carry=None):
    L = qa.shape[0]
    nb = L // QBLK
    dn = (((1,), (1,)), ((), ()))
    scale = HEAD_DIM ** -0.5

    pps = N_PAIRS

    def body(q_ref, k_ref, v_ref, bias_ref, *rest):
        sel0 = _head_sel(0)
        pair0 = pl.program_id(1) * pps

        def emit(rows, cols, o, lse):
            if sink is None:
                o_ref, lse_ref = rest
                o_ref[rows, cols] = o.astype(o_ref.dtype)
                lse_ref[rows, cols] = lse
                return
            sink_ref, out_ref, outb_ref, lse_ref = rest
            sk = sink_ref[:, cols]
            mx = jnp.maximum(lse, sk)
            w = jnp.exp(lse - mx)
            den = w + jnp.exp(sk - mx)
            out = o * (w / den)
            out_ref[rows, cols] = out
            outb_ref[rows, cols] = out.astype(outb_ref.dtype)
            lse_ref[rows, cols] = mx + jnp.log(den)

        def block(b, first):
            rows = _rows(b)
            for i in range(pps):
                cols = slice(i * LANES, (i + 1) * LANES)
                q2 = _stack_heads(q_ref[rows, cols], scale)
                kv_head = (pair0 + i) // 2 if kv_shared else None
                k = _key_window(k_ref, b, cols, first, kv_head)
                v = _key_window(v_ref, b, cols, first, kv_head)
                bias2 = bias_ref[1 if first else 0, pl.ds(2 * (pair0 + i), 2)].reshape(2 * QBLK, 2 * QBLK)
                s = lax.dot_general(q2, k, dn, preferred_element_type=F32) + bias2
                m = jnp.max(s, axis=1, keepdims=True)
                p = jnp.exp(s - m)
                l = jnp.sum(p, axis=1, keepdims=True)
                o = jnp.dot(p.astype(v.dtype), v, preferred_element_type=F32) * (1.0 / l)
                lse = m + jnp.log(l)
                emit(rows, cols, jnp.where(sel0, o[0:QBLK], o[QBLK:]), jnp.where(sel0, lse[0:QBLK], lse[QBLK:]))

        block(0, True)
        if nb > 1:
            pl.loop(1, nb)(lambda b: block(b, False))

    bias_spec = pl.BlockSpec((None, 2, 2 * N_PAIRS, QBLK, 2 * QBLK), lambda j, c: (pattern, 0, 0, 0, 0))
    out_spec = _class_spec(L, r, 0, 1, pps)
    blk_bytes = _nbytes((L, pps * LANES), F32)
    in_specs = [_class_spec(L, r, q_col, stride, pps), _kv_spec(L, r, k_col, stride, pps, kv_shared),
                _kv_spec(L, r, v_col, stride, pps, kv_shared), bias_spec]
    args = [qa, ka, va, bias]
    f32_out, bf16_out = jax.ShapeDtypeStruct((L, r * MIX_W), F32), jax.ShapeDtypeStruct((L, r * MIX_W), BF16)
    out_shape = [bf16_out, f32_out]
    if sink is not None:
        in_specs.append(pl.BlockSpec((1, MIX_W), lambda j, c: (0, 0)))
        args.append(sink)
        out_shape = [f32_out, bf16_out, f32_out]
    return _call(body, name=name, grid=(r, N_PAIRS // pps), in_specs=in_specs, out_specs=[out_spec] * len(out_shape),
                 out_shape=out_shape, args=args, vmem_limit=_vmem_limit(*([blk_bytes] * 4)), carry=carry)


def _class_rows_spec(rows, r, width):
    return pl.BlockSpec((rows // r, r * width), lambda i, *_: (i, 0))


def _token_scratch(rows, width):
    return pltpu.VMEM((width // LANES, rows, LANES), F32)


def _fill_tokens(scratch, value):
    for c in range(scratch.shape[0]):
        scratch[c] = value[:, c * LANES:(c + 1) * LANES]


def _read_tokens(scratch):
    return jnp.concatenate([scratch[c] for c in range(scratch.shape[0])], axis=1)


def _to_tokens(blk_ref, r, scratch):
    if r == 1:
        return blk_ref[...].astype(F32)
    nt, rows, _ = scratch.shape
    for j in range(r):
        for c in range(nt):
            lanes = slice((j * nt + c) * LANES, (j * nt + c + 1) * LANES)
            scratch.at[c][pl.ds(j, rows // r, stride=r), :] = blk_ref[:, lanes].astype(F32)
    return _read_tokens(scratch)


def _from_tokens(dst_ref, r, scratch):
    nt, rows, _ = scratch.shape
    if r == 1:
        dst_ref[...] = _read_tokens(scratch).astype(dst_ref.dtype)
        return
    for j in range(r):
        for c in range(nt):
            lanes = slice((j * nt + c) * LANES, (j * nt + c + 1) * LANES)
            dst_ref[:, lanes] = scratch.at[c][pl.ds(j, rows // r, stride=r), :].astype(dst_ref.dtype)


def _mm_nt_classes(a, b, bias, dils, *, bm, name):
    M, K = a.shape
    N = b.shape[0]
    dn = (((1,), (1,)), ((), ()))

    def body(a_ref, b_ref, bias_ref, *rest):
        outs, tile = rest[:-1], rest[-1]
        _fill_tokens(tile, lax.dot_general(a_ref[...], b_ref[...], dn, preferred_element_type=F32) + bias_ref[...])
        for out, r in zip(outs, dils):
            _from_tokens(out, r, tile)

    limit = _vmem_limit(_nbytes((bm, K), a.dtype), _nbytes((K, N), b.dtype) // 2, (1 + len(dils)) * _nbytes((bm, N), F32))
    return pl.pallas_call(
        body, name=name, grid=(M // bm,),
        in_specs=[pl.BlockSpec((bm, K), lambda i: (i, 0)),
                  pl.BlockSpec((N, K), lambda i: (0, 0), pipeline_mode=pl.Buffered(1)),
                  pl.BlockSpec((1, N), lambda i: (0, 0))],
        out_specs=tuple(_class_rows_spec(bm, r, N) for r in dils),
        out_shape=tuple(jax.ShapeDtypeStruct((M // r, r * N), BF16) for r in dils),
        scratch_shapes=[_token_scratch(bm, N)],
        compiler_params=pltpu.CompilerParams(dimension_semantics=("parallel",), vmem_limit_bytes=limit),
    )(a, b, bias)


def _merge(branches, dils, sink, *, name):
    W = MIX_W
    T = branches[0][0].shape[0] * dils[0]
    nbr = len(branches)

    def body(*refs):
        sink_ref = refs[2 * nbr]
        out_ref, outb_ref, lse_ref, scratch = refs[2 * nbr + 1:]
        sk = sink_ref[...]
        lses = [_to_tokens(refs[2 * t + 1], dils[t], scratch) for t in range(nbr)]
        mx = sk
        for lse in lses:
            mx = jnp.maximum(mx, lse)
        den = jnp.exp(sk - mx)
        num = jnp.zeros_like(mx)
        for t in range(nbr):
            w = jnp.exp(lses[t] - mx)
            den = den + w
            num = num + w * _to_tokens(refs[2 * t], dils[t], scratch)
        out = num / den
        out_ref[...] = out
        outb_ref[...] = out.astype(outb_ref.dtype)
        lse_ref[...] = mx + jnp.log(den)

    args = [a for br in branches for a in br] + [sink]
    in_specs = [_class_rows_spec(ROW_BLK, r, W) for r in dils for _ in range(2)] + [_vec_spec(W)]
    return pl.pallas_call(
        body, name=name, grid=(T // ROW_BLK,), in_specs=in_specs,
        out_specs=(_row_spec(W), _row_spec(W), _row_spec(W)),
        out_shape=(jax.ShapeDtypeStruct((T, W), F32), jax.ShapeDtypeStruct((T, W), BF16),
                   jax.ShapeDtypeStruct((T, W), F32)),
        scratch_shapes=[_token_scratch(ROW_BLK, W)],
        compiler_params=pltpu.CompilerParams(dimension_semantics=("parallel",)),
    )(*args)


def _attn_delta(dmix, outs, lses, sink, dils, *, name):
    T = dmix.shape[0]
    W = MIX_W
    nd = len(dils)

    def body(dmix_ref, oa_ref, ob_ref, la_ref, lb_ref, sink_ref, *rest):
        doa_ref, sta_ref, dsink_ref = rest[0:3]
        dob_refs, stb_refs = rest[3:3 + nd], rest[3 + nd:3 + 2 * nd]
        do_tile, st_tile = rest[3 + 2 * nd:]

        @pl.when(pl.program_id(0) == 0)
        def _():
            dsink_ref[...] = jnp.zeros_like(dsink_ref)

        sel0, lo = _head_sel(0), _stat_lo()
        for mixer, (o_ref, l_ref) in enumerate(((oa_ref, la_ref), (ob_ref, lb_ref))):
            for i in range(N_PAIRS):
                cols = slice(i * LANES, (i + 1) * LANES)
                do = dmix_ref[:, mixer * W + i * LANES:mixer * W + (i + 1) * LANES]
                prod = do * o_ref[:, cols]
                d0 = jnp.sum(jnp.where(sel0, prod, 0.0), axis=1, keepdims=True)
                d1 = jnp.sum(jnp.where(sel0, 0.0, prod), axis=1, keepdims=True)
                delta = jnp.where(sel0, d0, d1)
                lse = l_ref[:, cols]
                stat = jnp.where(lo, lse, delta)
                if mixer == 0:
                    sta_ref[:, cols] = stat
                    doa_ref[:, cols] = do.astype(doa_ref.dtype)
                    dsink_ref[:, cols] += -jnp.sum(jnp.exp(sink_ref[:, cols] - lse) * delta, axis=0, keepdims=True)
                else:
                    st_tile[i] = stat
                    do_tile[i] = do
        for t, r in enumerate(dils):
            _from_tokens(dob_refs[t], r, do_tile)
            _from_tokens(stb_refs[t], r, st_tile)

    class_specs = [_class_rows_spec(ROW_BLK, r, W) for r in dils]
    out_shape = [jax.ShapeDtypeStruct((T, W), BF16), jax.ShapeDtypeStruct((T, W), F32), jax.ShapeDtypeStruct((1, W), F32)]
    out_shape += [jax.ShapeDtypeStruct((T // r, r * W), BF16) for r in dils]
    out_shape += [jax.ShapeDtypeStruct((T // r, r * W), F32) for r in dils]
    res = pl.pallas_call(
        body, name=name, grid=(T // ROW_BLK,),
        in_specs=[_row_spec(2 * W), _row_spec(W), _row_spec(W), _row_spec(W), _row_spec(W), _vec_spec(W)],
        out_specs=tuple([_row_spec(W), _row_spec(W), _vec_spec(W)] + class_specs + class_specs),
        out_shape=tuple(out_shape),
        scratch_shapes=[_token_scratch(ROW_BLK, W), _token_scratch(ROW_BLK, W)],
        compiler_params=pltpu.CompilerParams(dimension_semantics=("arbitrary",)),
    )(dmix, outs[0], outs[1], lses[0], lses[1], sink)
    return res[0], res[1], res[2], res[3:3 + nd], res[3 + nd:]


def _band_bwd(qa, ka, va, d_out, stat, bias, *, r, q_col, k_col, v_col, stride, pattern, name, kv_shared=False,
              carry=None):
    L = qa.shape[0]
    nb = L // QBLK
    dn_nt = (((1,), (1,)), ((), ()))
    dn_tn = (((0,), (0,)), ((), ()))
    scale = HEAD_DIM ** -0.5

    pps = N_PAIRS if r > 1 else N_PAIRS // 2

    def body(q_ref, k_ref, v_ref, do_ref, st_ref, bias_ref, dq_ref, dk_ref, dv_ref, db_ref, dk_carry, dv_carry):
        @pl.when((pl.program_id(0) == 0) & (pl.program_id(1) == 0))
        def _():
            db_ref[...] = jnp.zeros_like(db_ref)

        lo, sel0 = _stat_lo(), _head_sel(0)
        pair0 = pl.program_id(1) * pps

        def block(b, first):
            rows = _rows(b)
            for i in range(pps):
                heads = pl.ds(2 * (pair0 + i), 2)
                cols = slice(i * LANES, (i + 1) * LANES)
                q2 = _stack_heads(q_ref[rows, cols], scale)
                kv_head = (pair0 + i) // 2 if kv_shared else None
                k = _key_window(k_ref, b, cols, first, kv_head)
                v = _key_window(v_ref, b, cols, first, kv_head)
                do2 = _stack_heads(do_ref[rows, cols])
                st = st_ref[rows, cols]
                lse2 = jnp.concatenate(
                    [jnp.max(jnp.where(_head_sel(h) & lo, st, -jnp.inf), axis=1, keepdims=True) for h in range(2)], axis=0)
                delta2 = jnp.concatenate(
                    [jnp.sum(jnp.where(_head_sel(h) & ~lo, st, 0.0), axis=1, keepdims=True) for h in range(2)],
                    axis=0) * (2.0 / HEAD_DIM)
                bias2 = bias_ref[1 if first else 0, heads].reshape(2 * QBLK, 2 * QBLK)
                s = lax.dot_general(q2, k, dn_nt, preferred_element_type=F32) + bias2
                p = jnp.exp(s - lse2)
                dp = lax.dot_general(do2, v, dn_nt, preferred_element_type=F32)
                ds = p * (dp - delta2)
                db_ref[heads] += ds.reshape(2, QBLK, 2 * QBLK)
                dsb = ds.astype(k.dtype)
                dq2 = jnp.dot(dsb, k, preferred_element_type=F32)
                dq_ref[rows, cols] = (jnp.where(sel0, dq2[0:QBLK], dq2[QBLK:]) * scale).astype(dq_ref.dtype)
                dk_acc = lax.dot_general(dsb, q2, dn_tn, preferred_element_type=F32)
                dv_acc = lax.dot_general(p.astype(k.dtype), do2, dn_tn, preferred_element_type=F32)
                if not first:
                    prev = _rows(b - 1)
                    dk_ref[prev, cols] = (dk_carry[:, cols] + dk_acc[0:QBLK]).astype(dk_ref.dtype)
                    dv_ref[prev, cols] = (dv_carry[:, cols] + dv_acc[0:QBLK]).astype(dv_ref.dtype)
                dk_carry[:, cols] = dk_acc[QBLK:2 * QBLK]
                dv_carry[:, cols] = dv_acc[QBLK:2 * QBLK]

        block(0, True)
        if nb > 1:
            pl.loop(1, nb)(lambda b: block(b, False))

        last = _rows(nb - 1)
        dk_ref[last, :] = dk_carry[...].astype(dk_ref.dtype)
        dv_ref[last, :] = dv_carry[...].astype(dv_ref.dtype)

    tok_spec = _class_spec(L, r, 0, 1, pps)
    bias_spec = pl.BlockSpec((None, 2, 2 * N_PAIRS, QBLK, 2 * QBLK), lambda j, c: (pattern, 0, 0, 0, 0))
    dbias_spec = pl.BlockSpec((2 * N_PAIRS, QBLK, 2 * QBLK), lambda j, c: (0, 0, 0))
    grad = jax.ShapeDtypeStruct((L, r * MIX_W), BF16)
    blk_bytes = _nbytes((L, pps * LANES), BF16)
    carry_shape = pltpu.VMEM((QBLK, pps * LANES), F32)
    return _call(
        body, name=name, grid=(r, N_PAIRS // pps),
        in_specs=[_class_spec(L, r, q_col, stride, pps), _kv_spec(L, r, k_col, stride, pps, kv_shared),
                  _kv_spec(L, r, v_col, stride, pps, kv_shared), tok_spec, tok_spec, bias_spec],
        out_specs=[tok_spec, tok_spec, tok_spec, dbias_spec],
        out_shape=[grad, grad, grad, jax.ShapeDtypeStruct((2 * N_PAIRS, QBLK, 2 * QBLK), F32)],
        args=[qa, ka, va, d_out, stat, bias], scratch=[carry_shape, carry_shape],
        vmem_limit=_vmem_limit(*([blk_bytes] * 9)), semantics=("arbitrary", "arbitrary"), carry=carry)


def _dz_assemble(dq_a, dk_a, dv_a, b_grads, dils, *, name):
    T = dq_a.shape[0]
    W = MIX_W

    def group_sum(x):
        u0 = x[:, 0:LANES] + x[:, LANES:2 * LANES]
        u1 = x[:, 2 * LANES:3 * LANES] + x[:, 3 * LANES:4 * LANES]
        s0 = u0 + pltpu.roll(u0, HEAD_DIM, 1)
        s1 = u1 + pltpu.roll(u1, HEAD_DIM, 1)
        return jnp.where(_head_sel(0), s0, s1)

    def body(*refs):
        dqa_ref, dka_ref, dva_ref = refs[0:3]
        b_refs = refs[3:12]
        dza_ref, dzb_ref, sa_ref, sb_ref, scratch = refs[12:]

        @pl.when(pl.program_id(0) == 0)
        def _():
            sa_ref[...] = jnp.zeros_like(sa_ref)
            sb_ref[...] = jnp.zeros_like(sb_ref)

        def put(dst, acc, col, part):
            w = part.shape[1]
            dst[:, col:col + w] = part.astype(dst.dtype)
            acc[:, col:col + w] += jnp.sum(part, axis=0, keepdims=True)

        put(dza_ref, sa_ref, 0, dqa_ref[...].astype(F32))
        put(dza_ref, sa_ref, W, group_sum(dka_ref[...].astype(F32)))
        put(dza_ref, sa_ref, W + LANES, group_sum(dva_ref[...].astype(F32)))
        for t in range(3):
            part = _to_tokens(b_refs[t], dils[0], scratch)
            for pat in range(1, len(dils)):
                part = part + _to_tokens(b_refs[3 * pat + t], dils[pat], scratch)
            put(dzb_ref, sb_ref, t * W, part)

    args = [dq_a, dk_a, dv_a] + [g[t] for g in b_grads for t in range(3)]
    return pl.pallas_call(
        body, name=name, grid=(T // ROW_BLK,),
        in_specs=[_row_spec(W)] * 3 + [_class_rows_spec(ROW_BLK, r, W) for r in dils for _ in range(3)],
        out_specs=(_row_spec(ZA_W), _row_spec(ZB_W), _vec_spec(ZA_W), _vec_spec(ZB_W)),
        out_shape=(jax.ShapeDtypeStruct((T, ZA_W), BF16), jax.ShapeDtypeStruct((T, ZB_W), BF16),
                   jax.ShapeDtypeStruct((1, ZA_W), F32), jax.ShapeDtypeStruct((1, ZB_W), F32)),
        scratch_shapes=[_token_scratch(ROW_BLK, W)],
        compiler_params=pltpu.CompilerParams(dimension_semantics=("arbitrary",)),
    )(*args)


def _place():
    x, y, c = lax.axis_index("x"), lax.axis_index("y"), lax.axis_index("c")
    chips = [(1 - x, y), (x, 1 - y), (1 - x, 1 - y)]
    return x, y, c, chips


def _cast_place(shards, k_idx, *, name):
    n, steps = len(shards), 4

    def body(k_ref, *refs):
        for s_ref, o_ref in zip(refs[:n], refs[n:]):
            o_ref[...] = s_ref[...].astype(o_ref.dtype)

    blocks = [(a.shape[0] // steps, a.shape[1]) for a in shards]
    grid_spec = pltpu.PrefetchScalarGridSpec(
        num_scalar_prefetch=1, grid=(steps,),
        in_specs=[pl.BlockSpec(blk, lambda t, k_ref: (t, 0)) for blk in blocks],
        out_specs=tuple(pl.BlockSpec(blk, lambda t, k_ref: (k_ref[0] * steps + t, 0)) for blk in blocks))
    return pl.pallas_call(
        body, name=name, grid_spec=grid_spec,
        out_shape=tuple(jax.ShapeDtypeStruct((N_CHIPS * a.shape[0], a.shape[1]), BF16) for a in shards),
        compiler_params=pltpu.CompilerParams(dimension_semantics=("parallel",),
                                             vmem_limit_bytes=_vmem_limit(*[_nbytes(b, F32) for b in blocks])),
    )(k_idx, *shards)


def _gather_carry(fulls, jobs):
    n = len(jobs)

    def piece(ref, chip_idx, half, part):
        rs = ref.shape[0] // N_CHIPS
        rh = rs // 2
        rows = rh // part[1]
        return ref.at[pl.ds(chip_idx * rs + half * rh + part[0] * rows, rows)]

    def copy(sems, sem, src_ref, dst_ref, to):
        return pltpu.make_async_remote_copy(src_ref=src_ref, dst_ref=dst_ref, send_sem=sems[0].at[sem],
                                            recv_sem=sems[1].at[sem], device_id=to, device_id_type=MESH)

    def to_chips(ins, outs, sems, j, arrival, hops=("both", "chips")):
        i, part, hop = jobs[j]
        x, y, c, chips = _place()
        res = []
        for t, chip in enumerate(chips if hop in hops else ()):
            theirs = piece(outs[i], 2 * chip[0] + chip[1], c, part)
            src, dst = (theirs, theirs) if arrival else (piece(ins[i], 2 * x + y, c, part), piece(outs[i], 2 * x + y, c, part))
            res.append(copy(sems, 3 * j + t, src, dst, (*chip, c)))
        return res

    def to_sibling(outs, sems, j, arrival, hops=("both", "sibling")):
        i, part, hop = jobs[j]
        x, y, c, chips = _place()
        res = []
        for t, chip in enumerate(chips if hop in hops else ()):
            region = piece(outs[i], 2 * chip[0] + chip[1], 1 - c if arrival else c, part)
            res.append(copy(sems, 3 * n + 3 * j + t, region, region, (x, y, 1 - c)))
        return res

    def start(ins, outs, sems):
        for j in range(n):
            for send in to_chips(ins, outs, sems, j, False) + to_sibling(outs, sems, j, False, ("sibling",)):
                send.start()

    def before_last(ins, outs, sems):
        for j in range(n):
            for arrival, send in zip(to_chips(ins, outs, sems, j, True, ("both",)),
                                     to_sibling(outs, sems, j, False, ("both",))):
                arrival.wait_recv()
                send.start()

    def finish(ins, outs, sems):
        for j in range(n):
            for arrival in to_chips(ins, outs, sems, j, True, ("chips",)) + to_sibling(outs, sems, j, True):
                arrival.wait_recv()
        for j in range(n):
            for send in to_chips(ins, outs, sems, j, False) + to_sibling(outs, sems, j, False):
                send.wait_send()

    return _Carry(fulls, [jax.ShapeDtypeStruct(a.shape, a.dtype) for a in fulls], {i: i for i in range(len(fulls))},
                  [pltpu.SemaphoreType.DMA((6 * n,)), pltpu.SemaphoreType.DMA((6 * n,))], start, finish, before_last)


def _pair_swap_carry(grads):
    n = len(grads)

    def copy(ins, outs, sems, i):
        x, y, c, _ = _place()
        return pltpu.make_async_remote_copy(src_ref=ins[i].at[:, 1 - c], dst_ref=outs[i], send_sem=sems[0].at[i],
                                            recv_sem=sems[1].at[i], device_id=(x, y, 1 - c), device_id_type=MESH)

    def start(ins, outs, sems):
        for i in range(n):
            copy(ins, outs, sems, i).start()

    def finish(ins, outs, sems):
        for i in range(n):
            copy(ins, outs, sems, i).wait()

    return _Carry(grads, [jax.ShapeDtypeStruct((a.shape[0], a.shape[2], a.shape[3]), a.dtype) for a in grads], {},
                  [pltpu.SemaphoreType.DMA((n,)), pltpu.SemaphoreType.DMA((n,))], start, finish)


def _pair_sum(g4, got, c_idx, *, name):
    _, _, rh, cs = g4.shape
    rb = _row_block(rh, 512, 16)

    def body(c_ref, own_ref, got_ref, o_ref):
        o_ref[...] = (own_ref[...].astype(F32) + got_ref[...].astype(F32)).astype(o_ref.dtype)

    grid_spec = pltpu.PrefetchScalarGridSpec(
        num_scalar_prefetch=1, grid=(N_CHIPS, rh // rb),
        in_specs=[pl.BlockSpec((None, None, rb, cs), lambda k, t, c_ref: (k, c_ref[0], t, 0)),
                  pl.BlockSpec((None, rb, cs), lambda k, t, c_ref: (k, t, 0))],
        out_specs=pl.BlockSpec((None, rb, cs), lambda k, t, c_ref: (k, t, 0)))
    return pl.pallas_call(
        body, name=name, grid_spec=grid_spec, out_shape=jax.ShapeDtypeStruct((N_CHIPS, rh, cs), BF16),
        compiler_params=pltpu.CompilerParams(dimension_semantics=("parallel", "parallel")),
    )(c_idx, g4, got)


def _chip_scatter_carry(sums):
    n = len(sums)

    def copies(ins, outs, sems):
        x, y, c, chips = _place()
        return [pltpu.make_async_remote_copy(src_ref=ins[i].at[2 * chip[0] + chip[1]], dst_ref=outs[i].at[j],
                                             send_sem=sems[0].at[3 * i + j], recv_sem=sems[1].at[3 * i + j],
                                             device_id=(*chip, c), device_id_type=MESH)
                for i in range(n) for j, chip in enumerate(chips)]

    def start(ins, outs, sems):
        for cp in copies(ins, outs, sems):
            cp.start()

    def finish(ins, outs, sems):
        for cp in copies(ins, outs, sems):
            cp.wait()

    return _Carry(sums, [jax.ShapeDtypeStruct((3,) + a.shape[1:], a.dtype) for a in sums], {},
                  [pltpu.SemaphoreType.DMA((3 * n,)), pltpu.SemaphoreType.DMA((3 * n,))], start, finish)


def _chip_sum(sums, gots, kc_idx, *, name):
    n, steps = len(sums), 2

    def body(kc_ref, *refs):
        for own_ref, got_ref, o_ref in zip(refs[:n], refs[n:2 * n], refs[2 * n:]):
            acc = own_ref[...].astype(F32)
            for j in range(3):
                acc = acc + got_ref[j].astype(F32)
            o_ref[...] = acc

    blocks = [(a.shape[1] // steps, a.shape[2]) for a in sums]
    grid_spec = pltpu.PrefetchScalarGridSpec(
        num_scalar_prefetch=1, grid=(steps,),
        in_specs=[pl.BlockSpec((None,) + blk, lambda t, kc: (kc[0], t, 0)) for blk in blocks]
        + [pl.BlockSpec((3,) + blk, lambda t, kc: (0, t, 0)) for blk in blocks],
        out_specs=tuple(pl.BlockSpec(blk, lambda t, kc: (kc[1] * steps + t, 0)) for blk in blocks))
    return pl.pallas_call(
        body, name=name, grid_spec=grid_spec,
        out_shape=tuple(jax.ShapeDtypeStruct((2 * a.shape[1], a.shape[2]), F32) for a in sums),
        compiler_params=pltpu.CompilerParams(dimension_semantics=("parallel",),
                                             vmem_limit_bytes=_vmem_limit(*[3 * _nbytes(b, F32) for b in blocks])),
    )(kc_idx, *sums, *gots)


def _half_swap_carry(shards):
    n = len(shards)

    def copy(ins, outs, sems, i, to_my_half):
        x, y, c, _ = _place()
        rh = ins[i].shape[0] // 2
        half = c if to_my_half else 1 - c
        return pltpu.make_async_remote_copy(
            src_ref=ins[i].at[pl.ds(c * rh, rh)], dst_ref=outs[i].at[pl.ds(half * rh, rh)],
            send_sem=sems[0].at[i], recv_sem=sems[1].at[i], device_id=(x, y, 1 - c), device_id_type=MESH)

    def start(ins, outs, sems):
        for i in range(n):
            copy(ins, outs, sems, i, True).start()

    def finish(ins, outs, sems):
        for i in range(n):
            copy(ins, outs, sems, i, False).wait_recv()
        for i in range(n):
            copy(ins, outs, sems, i, True).wait_send()

    return _Carry(shards, [jax.ShapeDtypeStruct(a.shape, a.dtype) for a in shards], {i: i for i in range(n)},
                  [pltpu.SemaphoreType.DMA((n,)), pltpu.SemaphoreType.DMA((n,))], start, finish)


SMALL_ROW = 1024


def _small_layout(shapes):
    starts, row = [], 0
    for r, c in shapes:
        starts.append(row)
        row += -(-c // SMALL_ROW) if r == 1 else r
    return starts, -(-row // SUBLANES) * SUBLANES


def _small_pieces(shape, start):
    r, c = shape
    if r == 1:
        return [(start + q, min(SMALL_ROW, c - q * SMALL_ROW), q * SMALL_ROW) for q in range(-(-c // SMALL_ROW))]
    return None


def _whole_spec(shape):
    return pl.BlockSpec(tuple(shape), lambda i: (0,) * len(shape))


def _small_all_reduce(parts, *, name, carry=None):
    shapes = [a.shape for a in parts]
    starts, rows = _small_layout(shapes)
    n = len(parts)
    n_tables = sum(1 for r, _ in shapes if r > 1)
    assert all(r > 1 and c <= LANES for r, c in shapes[:n_tables]) and all(r == 1 for r, _ in shapes[n_tables:])
    table_rows = starts[n_tables]
    assert table_rows % SUBLANES == 0

    def regions(slot):
        return slot.at[pl.ds(0, table_rows), pl.ds(0, LANES)], slot.at[pl.ds(table_rows, rows - table_rows)]

    def peer(d):
        x, y, c, _ = _place()
        return 1 - x if d & 4 else x, 1 - y if d & 2 else y, 1 - c if d & 1 else c

    def slot_of(d):
        px, py, pc = peer(d)
        return 4 * px + 2 * py + pc

    def copy(refs, d, part, dst_slot):
        buf, send_sems, recv_sems = refs[2 * n:]
        sem = 2 * (d - 1) + part
        return pltpu.make_async_remote_copy(
            src_ref=regions(buf.at[slot_of(0)])[part], dst_ref=regions(buf.at[dst_slot])[part],
            send_sem=send_sems.at[sem], recv_sem=recv_sems.at[sem], device_id=peer(d), device_id_type=MESH)

    def send(*refs):
        ins, mine = refs[0:n], refs[2 * n].at[slot_of(0)]
        mine[...] = jnp.zeros((rows, SMALL_ROW), F32)
        for ref, shape, start in zip(ins, shapes, starts):
            pieces = _small_pieces(shape, start)
            if pieces is None:
                mine[start:start + shape[0], 0:shape[1]] = ref[...]
            else:
                for row, width, col in pieces:
                    mine[row:row + 1, 0:width] = ref[:, col:col + width]
        for d in range(1, 8):
            for part in range(2):
                copy(refs, d, part, slot_of(0)).start()

    def receive(*refs):
        outs, buf = refs[n:2 * n], refs[2 * n]
        mine = buf.at[slot_of(0)]
        for d in range(1, 8):
            for part in range(2):
                copy(refs, d, part, slot_of(d)).wait_recv()
        for d in range(1, 8):
            for part in range(2):
                copy(refs, d, part, slot_of(0)).wait_send()
        tables, vectors = regions(buf.at[0])
        acc_t, acc_v = tables[...], vectors[...]
        for s in range(1, 8):
            tables, vectors = regions(buf.at[s])
            acc_t, acc_v = acc_t + tables[...], acc_v + vectors[...]
        tables, vectors = regions(mine)
        tables[...] = acc_t
        vectors[...] = acc_v
        for ref, shape, start in zip(outs, shapes, starts):
            pieces = _small_pieces(shape, start)
            if pieces is None:
                ref[...] = mine[start:start + shape[0], 0:shape[1]]
            else:
                for row, width, col in pieces:
                    ref[:, col:col + width] = mine[row:row + 1, 0:width]

    def body(*refs):
        send(*refs)
        receive(*refs)

    whole = [_whole_spec(s) for s in shapes]
    return _call(body if carry is None else receive, before_carry=None if carry is None else send,
                 name=name, grid=(1,), in_specs=whole, out_specs=whole,
                 out_shape=[jax.ShapeDtypeStruct(s, F32) for s in shapes], args=parts,
                 scratch=[pltpu.VMEM((8, rows, SMALL_ROW), F32), pltpu.SemaphoreType.DMA((14,)),
                          pltpu.SemaphoreType.DMA((14,))], carry=carry)


def _adamw_small(ws, gs, ms, vs, *, name):
    n = len(ws)
    c1 = 1.0 - ADAM_B1 ** ADAM_STEP
    c2 = 1.0 - ADAM_B2 ** ADAM_STEP

    def body(*refs):
        for k in range(n):
            w_ref, g_ref, m_ref, v_ref = (refs[t * n + k] for t in range(4))
            go_ref, d_ref, mo_ref, vo_ref = (refs[(4 + t) * n + k] for t in range(4))
            gv = g_ref[...]
            go_ref[...] = gv
            mn = ADAM_B1 * m_ref[...] + (1.0 - ADAM_B1) * gv
            vn = ADAM_B2 * v_ref[...] + (1.0 - ADAM_B2) * (gv * gv)
            d_ref[...] = -ADAM_LR * ((mn / c1) / (jnp.sqrt(vn / c2) + ADAM_EPS) + ADAM_WD * w_ref[...])
            mo_ref[...] = mn
            vo_ref[...] = vn

    whole = [_whole_spec(a.shape) for a in ws]
    shapes = tuple(jax.ShapeDtypeStruct(a.shape, F32) for a in ws)
    res = pl.pallas_call(
        body, name=name, grid=(1,), in_specs=whole * 4, out_specs=tuple(whole * 4), out_shape=shapes * 4,
    )(*ws, *gs, *ms, *vs)
    return res[0:n], res[n:2 * n], res[2 * n:3 * n], res[3 * n:4 * n]


def _adamw(ws, gs, ms, vs, *, name):
    n, steps = len(ws), 8
    c1 = 1.0 - ADAM_B1 ** ADAM_STEP
    c2 = 1.0 - ADAM_B2 ** ADAM_STEP

    def body(*refs):
        for k in range(n):
            w_ref, g_ref, m_ref, v_ref = (refs[t * n + k] for t in range(4))
            go_ref, d_ref, mo_ref, vo_ref = (refs[(4 + t) * n + k] for t in range(4))
            gv = g_ref[...]
            go_ref[...] = gv
            mn = ADAM_B1 * m_ref[...] + (1.0 - ADAM_B1) * gv
            vn = ADAM_B2 * v_ref[...] + (1.0 - ADAM_B2) * (gv * gv)
            d_ref[...] = -ADAM_LR * ((mn / c1) / (jnp.sqrt(vn / c2) + ADAM_EPS) + ADAM_WD * w_ref[...])
            mo_ref[...] = mn
            vo_ref[...] = vn

    blocks = [(a.shape[0] // steps, a.shape[1]) for a in ws]
    specs = [pl.BlockSpec(blk, lambda i: (i, 0)) for blk in blocks]
    shapes = tuple(jax.ShapeDtypeStruct(a.shape, F32) for a in ws)
    res = pl.pallas_call(
        body, name=name, grid=(steps,), in_specs=specs * 4, out_specs=tuple(specs * 4), out_shape=shapes * 4,
        compiler_params=pltpu.CompilerParams(dimension_semantics=("parallel",),
                                             vmem_limit_bytes=_vmem_limit(*[8 * _nbytes(b, F32) for b in blocks])),
    )(*ws, *gs, *ms, *vs)
    return res[0:n], res[n:2 * n], res[2 * n:3 * n], res[3 * n:4 * n]


def _local_step(x, p, target, small, sched):
    T = x.shape[0]
    W = MIX_W
    rel_bias = small["rel_bias"]
    b_in_a, b_in_b = small["b_in"][:, 0:ZA_W], small["b_in"][:, ZA_W:]

    idx_np = [_t5_index(A_WINDOW - 1, 1)] + [_t5_index(window // dil, dil) for window, dil in B_PATTERNS]
    idx_all = jnp.asarray(np.stack(idx_np))
    n1, bias_all = sched.run(_prologue, x, small["ffn1_pre_g"], rel_bias, idx_all, name="prologue")
    w_gu1 = sched.weight("ffn1_w_gu")
    *gu1, a1 = sched.run(_ffn_up, n1, w_gu1, bm=512, name="ffn1_gu")
    w_d1 = sched.weight("ffn1_w_down")
    f1, h1, n2 = sched.run(_mm_resid_norm, a1, w_d1, x, small["ffn1_post_g"], 0.5, small["attn_pre_g"], bm=512,
                           name="ffn1_down")
    win_a, win_b = sched.weight("w_in")
    za = _mm_nt(n2, win_a, bm=1024, bn=ZA_W, out_dtype=BF16, name="in_proj_a", add=b_in_a)
    dils = tuple(dil for _, dil in B_PATTERNS)
    zb_by_dil = _mm_nt_classes(n2, win_b, b_in_b, dils, bm=512, name="in_proj_b")

    a_args = dict(r=1, q_col=0, k_col=W // LANES, v_col=W // LANES + 1, stride=0, pattern=0, kv_shared=True)
    sink_row = jnp.repeat(small["sinks"], HEAD_DIM, axis=1)
    out_a, out_a_bf, lse_a = sched.run(_band_fwd, za, za, za, bias_all, name="band_a_fwd", sink=sink_row, **a_args)

    no_sink = jnp.full((1, W), NEG, F32)
    b_ctx, branches = [], []
    for t, dil in enumerate(dils):
        zb_r = zb_by_dil[t]
        args = dict(r=dil, q_col=0, k_col=1, v_col=2, stride=ZB_W // W, pattern=1 + t)
        branches.append(sched.run(_band_fwd, zb_r, zb_r, zb_r, bias_all, name=f"band_b{t}_fwd", **args))
        b_ctx.append((jnp.asarray(idx_np[1 + t]), zb_r, args))
    out_b, out_b_bf, lse_b = _merge(branches, dils, no_sink, name="merge_b")

    mix = jnp.concatenate([out_a_bf, out_b_bf], axis=1)
    w_out = sched.weight("w_out")
    att, h2, n3 = _mm_resid_norm(mix, w_out, h1, small["attn_post_g"], 1.0, small["ffn2_pre_g"], bm=512,
                                 name="out_proj", bias=small["b_out"])
    w_gu2, w_d2 = sched.weight("ffn2_w_gu"), sched.weight("ffn2_w_down")
    *gu2, a2 = _ffn_up(n3, w_gu2, bm=512, name="ffn2_gu")
    f2, h3, n4 = _mm_resid_norm(a2, w_d2, h2, small["ffn2_post_g"], 0.5, small["ple_pre_g"], bm=512,
                                name="ffn2_down")
    w_gate = sched.weight("w_ple_gate")
    p_bf = p.astype(BF16)
    dh4, de, dgp, loss, d_ple_post = _ple_head(n4, w_gate, p_bf, sched.weight("w_ple_proj"), h3, small["ple_post_g"],
                                               target, bm=512, name="ple_head")

    gs = {"ple_post_g": d_ple_post}
    sched.grad("w_ple_proj", _mm_tn(p_bf, de, bm=256, bn=1024, out_dtype=BF16, name="d_w_ple"))
    sched.grad("w_ple_gate", _mm_tn(n4, dgp, bm=512, bn=1024, out_dtype=BF16, name="d_w_gate"))
    dh3, df2, gs["ple_pre_g"], gs["ffn2_post_g"], _ = sched.run(
        _mm_nt_norms_bwd, dgp, w_gate, dh4, h3, small["ple_pre_g"], f2, small["ffn2_post_g"], 0.5, bm=512, name="d_n4")

    def ffn_bwd(df, a, gu, n, w_down, w_gu, tag, *norm_args):
        dgu = sched.run(_ffn_down_bwd, df, w_down, gu[0], gu[1], bm=512, name=f"ffn{tag}_d_a")
        sched.grad(f"ffn{tag}_w_gu", _mm_tn(n, dgu, bm=512, bn=1408, out_dtype=BF16, name=f"ffn{tag}_d_w_gu",
                                            n_stack=N_CHIPS))
        sched.grad(f"ffn{tag}_w_down", sched.run(_mm_tn, a, df, bm=256, bn=1024, out_dtype=BF16,
                                                name=f"ffn{tag}_d_w_down"))
        return sched.run(_mm_nt_norms_bwd, dgu, w_gu, *norm_args, bm=512, name=f"ffn{tag}_d_n")

    dh2, datt, gs["ffn2_pre_g"], gs["attn_post_g"], gs["b_out"] = ffn_bwd(
        df2, a2, gu2, n3, w_d2, w_gu2, "2", dh3, h2, small["ffn2_pre_g"], att, small["attn_post_g"], 1.0)
    sched.grad("w_out", _mm_tn(mix, datt, bm=512, bn=1024, out_dtype=BF16, name="d_w_out"))
    dmix = sched.run(_mm_nt, datt, w_out, bm=1024, bn=1024, out_dtype=F32, name="d_mix")

    do_a, stat_a, dsink, do_b, stat_b = _attn_delta(dmix, (out_a, out_b), (lse_a, lse_b), sink_row, dils,
                                                    name="attn_delta")
    gs["sinks"] = dsink[:, ::HEAD_DIM]
    dq_a, dk_a, dv_a, dbias_a = sched.run(_band_bwd, za, za, za, do_a, stat_a, bias_all, name="band_a_bwd", **a_args)
    d_rel_a = _bias_grad(dbias_a, jnp.asarray(idx_np[0]), name="d_rel_a")
    b_grads = []
    d_rel_b = None
    for t, (idx, zb_r, args) in enumerate(b_ctx):
        dq, dk, dv, dbias = sched.run(_band_bwd, zb_r, zb_r, zb_r, do_b[t], stat_b[t], bias_all,
                                      name=f"band_b{t}_bwd", **args)
        b_grads.append((dq, dk, dv))
        d_rel = _bias_grad(dbias, idx, name=f"d_rel_b{t}")
        d_rel_b = d_rel if d_rel_b is None else d_rel_b + d_rel
    gs["rel_bias"] = jnp.concatenate([d_rel_a[:, 0:2 * N_PAIRS], d_rel_b[:, 0:2 * N_PAIRS]], axis=1)
    dza, dzb, dsum_a, dsum_b = _dz_assemble(dq_a, dk_a, dv_a, b_grads, dils, name="d_z")
    gs["b_in"] = jnp.concatenate([dsum_a, dsum_b], axis=1)
    sched.grad("w_in", (_mm_tn(dza, n2, bm=ZA_W, bn=1024, out_dtype=BF16, name="d_w_in_a"),
                        _mm_tn(dzb, n2, bm=ZB_W // 2, bn=1024, out_dtype=BF16, name="d_w_in_b")))
    dn2_a = _mm_nn(dza, win_a, bm=1024, bn=1024, out_dtype=F32, name="d_n2_a")
    dh1, df1, gs["attn_pre_g"], gs["ffn1_post_g"], _ = sched.run(
        _mm_nt_norms_bwd, dzb, win_b, dh2, h1, small["attn_pre_g"], f1, small["ffn1_post_g"], 0.5, bm=512,
        name="d_n2_b", add=dn2_a, b_is_kn=True)
    grad_x, gs["ffn1_pre_g"] = ffn_bwd(df1, a1, gu1, n1, w_d1, w_gu1, "1", dh1, x, small["ffn1_pre_g"], None, None, 0.0)
    return loss, grad_x, gs


def _to_compute(name, full):
    st = full.reshape(N_CHIPS, full.shape[0] // N_CHIPS, full.shape[1])
    if name in ("ffn1_w_gu", "ffn2_w_gu"):
        return st
    if name == "w_ple_proj":
        return jnp.transpose(st, (1, 0, 2)).reshape(st.shape[1], -1)
    if name == "w_in":
        return full[0:ZA_W], full[ZA_W:]
    return full


def _to_comm(name, g):
    if name == "w_in":
        g = jnp.concatenate(g, axis=0)
    if name == "w_ple_proj":
        g = jnp.transpose(g.reshape(g.shape[0], N_CHIPS, -1), (1, 0, 2))
    rs = g.shape[1] if g.ndim == 3 else g.shape[0] // N_CHIPS
    return g.reshape(N_CHIPS, 2, rs // 2, g.shape[-1])


class _Schedule:
    GATHER = {
        "prologue": [("ffn1_w_gu", (0, 1), "both")],
        "ffn1_gu": [("ffn1_w_down", (0, 1), "both"), ("w_in", (0, 1), "both")],
        "band_a_fwd": [("ffn2_w_gu", (0, 2), "chips"), ("w_out", (0, 1), "chips")],
        "band_b0_fwd": [("ffn2_w_gu", (0, 2), "sibling"), ("w_out", (0, 1), "sibling"),
                        ("ffn2_w_gu", (1, 2), "chips"), ("w_ple_gate", (0, 1), "chips"), ("w_ple_proj", (0, 1), "chips")],
        "band_b1_fwd": [("ffn2_w_gu", (1, 2), "sibling"), ("w_ple_gate", (0, 1), "sibling"),
                        ("w_ple_proj", (0, 1), "sibling"), ("ffn2_w_down", (0, 1), "both")],
    }
    SWAP = {"ffn2_d_w_down": ["w_ple_proj", "w_ple_gate", "ffn2_w_gu"], "ffn2_d_n": ["ffn2_w_down"],
            "band_a_bwd": ["w_out"], "d_n2_b": ["w_in"], "ffn1_d_w_down": ["ffn1_w_gu"], "ffn1_d_n": ["ffn1_w_down"]}
    SCATTER = {"ffn2_d_n": ["ffn2_w_gu"], "band_a_bwd": ["w_ple_proj", "w_ple_gate", "ffn2_w_down"],
               "d_n2_b": ["w_out"], "ffn1_d_w_down": ["w_in"], "ffn1_d_n": ["ffn1_w_gu"]}
    HALF_SWAP = {"ffn1_d_n": ["w_ple_proj", "w_ple_gate", "ffn2_w_gu", "ffn2_w_down", "w_out", "w_in"]}

    def __init__(self, placed, c_idx, kc_idx):
        self.full = dict(placed)
        self.missing = {n: 1.0 for n in placed}
        self.c_idx, self.kc_idx = c_idx, kc_idx
        self.grads, self.swapped, self.pair_sums, self.scattered, self.summed = {}, {}, {}, {}, {}

    def _gather(self, entries):
        if not entries:
            return [], []
        names = list(dict.fromkeys(n for n, _, _ in entries))
        carry = _gather_carry([self.full[n] for n in names], [(names.index(n), pt, hops) for n, pt, hops in entries])

        def done():
            self.full.update(zip(names, carry.results))
            for n, part, hops in entries:
                if hops != "chips":
                    self.missing[n] -= 1.0 / part[1]
        return [carry], [done]

    def weight(self, name):
        assert abs(self.missing[name]) < 1e-9, (name, self.missing[name])
        return _to_compute(name, self.full[name])

    def grad(self, name, g):
        self.grads[name] = _to_comm(name, g)

    def _swap(self, names):
        carry = _pair_swap_carry([self.grads[n] for n in names])

        def done():
            self.swapped.update(zip(names, carry.results))
        return carry, done

    def _scatter(self, names):
        for n in names:
            self.pair_sums[n] = _pair_sum(self.grads[n], self.swapped[n], self.c_idx, name=f"grad_pair_sum_{n}")
        carry = _chip_scatter_carry([self.pair_sums[n] for n in names])

        def done():
            self.scattered.update(zip(names, carry.results))
        return carry, done

    def _half_swap(self, names):
        halves = _chip_sum([self.pair_sums[n] for n in names], [self.scattered[n] for n in names], self.kc_idx,
                           name=f"grad_chip_sum_{names[0]}")
        carry = _half_swap_carry(halves)

        def done():
            self.summed.update(zip(names, carry.results))
        return carry, done

    def run(self, fn, *args, name, **kw):
        carries, after = self._gather(self.GATHER.get(name, []))
        for names, make in ((self.SWAP.get(name), self._swap), (self.SCATTER.get(name), self._scatter),
                            (self.HALF_SWAP.get(name), self._half_swap)):
            if names:
                carry, done = make(names)
                carries.append(carry)
                after.append(done)
        out = fn(*args, name=name, carry=_join(carries), **kw)
        for done in after:
            done()
        return out

    def finish(self, last_carrier):
        left = [n for n in BIG if n not in self.scattered]
        to_swap = [n for n in left if n not in self.swapped]
        if to_swap:
            carry, done = self._swap(to_swap)
            _comm_call(carry, name="grad_pair_swap")
            done()
        carry, done = self._scatter(left) if left else (None, lambda: None)
        carried = last_carrier(carry=carry)
        done()
        carry, done = self._half_swap([n for n in BIG if n not in self.summed])
        _comm_call(carry, name="grad_half_swap")
        done()
        return self.summed, carried


def kernel(x, p, rel_bias, ffn1_pre_g, ffn1_w_gu, ffn1_w_down, ffn1_post_g, attn_pre_g, w_in, b_in, sinks, w_out, b_out, attn_post_g, ffn2_pre_g, ffn2_w_gu, ffn2_w_down, ffn2_post_g, ple_pre_g, w_ple_gate, w_ple_proj, ple_post_g, loss_target, m_rel_bias, m_ffn1_pre_g, m_ffn1_w_gu, m_ffn1_w_down, m_ffn1_post_g, m_attn_pre_g, m_w_in, m_b_in, m_sinks, m_w_out, m_b_out, m_attn_post_g, m_ffn2_pre_g, m_ffn2_w_gu, m_ffn2_w_down, m_ffn2_post_g, m_ple_pre_g, m_w_ple_gate, m_w_ple_proj, m_ple_post_g, v_rel_bias, v_ffn1_pre_g, v_ffn1_w_gu, v_ffn1_w_down, v_ffn1_post_g, v_attn_pre_g, v_w_in, v_b_in, v_sinks, v_w_out, v_b_out, v_attn_post_g, v_ffn2_pre_g, v_ffn2_w_gu, v_ffn2_w_down, v_ffn2_post_g, v_ple_pre_g, v_w_ple_gate, v_w_ple_proj, v_ple_post_g):
    given = dict(locals())
    w = {n: given[n] for n in WEIGHTS}
    m = {n: given["m_" + n] for n in WEIGHTS}
    v = {n: given["v_" + n] for n in WEIGHTS}
    c_pos = lax.axis_index("c").astype(jnp.int32)
    k_pos = (2 * lax.axis_index("x") + lax.axis_index("y")).astype(jnp.int32)
    c_idx, k_idx, kc_idx = c_pos.reshape(1), k_pos.reshape(1), jnp.stack([k_pos, c_pos])

    def shard(a, n):
        return jnp.transpose(a[0]) if n == "w_in" else a[0]

    def unshard(a, n):
        return (jnp.transpose(a) if n == "w_in" else a)[None]

    placed = _cast_place([shard(w[n], n) for n in BIG], k_idx, name="place_shards")
    sched = _Schedule(dict(zip(BIG, placed)), c_idx, kc_idx)
    small = {n: (w[n] if n == "rel_bias" else w[n].reshape(1, -1)) for n in SMALL}

    loss_part, grad_x, gs = _local_step(x[0], p[0, 0], loss_target[0], small, sched)

    grads_big, (*small_grads, loss_row) = sched.finish(
        partial(_small_all_reduce, [gs[n] for n in SMALL] + [loss_part], name="small_all_reduce"))
    loss = loss_row[0, 0]

    grads, delta, new_m, new_v = {}, {}, {}, {}
    res = _adamw([shard(w[n], n) for n in BIG], [grads_big[n] for n in BIG], [shard(m[n], n) for n in BIG],
                 [shard(v[n], n) for n in BIG], name="adamw_big")
    for n, gg, dd, mm, vv in zip(BIG, *res):
        grads[n], delta[n], new_m[n], new_v[n] = (unshard(a, n) for a in (gg, dd, mm, vv))
    res = _adamw_small([w[n] for n in SMALL], small_grads, [m[n] for n in SMALL], [v[n] for n in SMALL],
                       name="adamw_small")
    for name, gg, dd, mm, vv in zip(SMALL, *res):
        grads[name], delta[name], new_m[name], new_v[name] = gg, dd, mm, vv

    return (loss, grad_x[None], *[grads[n] for n in WEIGHTS], *[delta[n] for n in WEIGHTS],
            *[new_m[n] for n in WEIGHTS], *[new_v[n] for n in WEIGHTS])
```

```python
import math
from functools import partial

import jax
import jax.numpy as jnp
import numpy as np
from jax import lax
from jax.experimental import pallas as pl
from jax.experimental.pallas import tpu as pltpu

F32 = jnp.float32
BF16 = jnp.bfloat16
MESH = pl.DeviceIdType.MESH

EPS = 1e-6
NEG = -1e30
LANES = 128
SUBLANES = 8
HEAD_DIM = 64
QBLK = 128
N_PAIRS = 4
MIX_W = N_PAIRS * LANES
A_WINDOW = 128
B_PATTERNS = ((128, 1), (512, 4), (2048, 16))
NUM_BUCKETS = 32
MAX_DISTANCE = 2048
ZA_W = 768
ZB_W = 1536
N_CHIPS = 4
VMEM_BYTES_V7X = 64 * 2**20
VMEM_CLAIM = VMEM_BYTES_V7X - (6 << 20)
RING_SLOTS = 3

ADAM_LR, ADAM_B1, ADAM_B2, ADAM_EPS, ADAM_WD, ADAM_STEP = 0.001, 0.9, 0.999, 1e-08, 0.01, 10

BIG = ("ffn1_w_gu", "ffn1_w_down", "w_in", "w_out", "ffn2_w_gu", "ffn2_w_down", "w_ple_gate", "w_ple_proj")
SMALL = ("rel_bias", "ffn1_pre_g", "ffn1_post_g", "attn_pre_g", "b_in", "sinks", "b_out", "attn_post_g",
         "ffn2_pre_g", "ffn2_post_g", "ple_pre_g", "ple_post_g")
WEIGHTS = ("rel_bias", "ffn1_pre_g", "ffn1_w_gu", "ffn1_w_down", "ffn1_post_g", "attn_pre_g", "w_in", "b_in",
           "sinks", "w_out", "b_out", "attn_post_g", "ffn2_pre_g", "ffn2_w_gu", "ffn2_w_down", "ffn2_post_g",
           "ple_pre_g", "w_ple_gate", "w_ple_proj", "ple_post_g")


def _vmem_limit(*block_bytes):
    del block_bytes
    return VMEM_CLAIM


def _nbytes(shape, dtype):
    return int(np.prod(shape)) * jnp.dtype(dtype).itemsize


MXU_WIDTH_V7X = 256


def _col_chunks(n):
    return [slice(c, min(c + MXU_WIDTH_V7X, n)) for c in range(0, n, MXU_WIDTH_V7X)]


def _row_halves(rows):
    return [slice(0, rows // 2), slice(rows // 2, rows)]


def _row_block(rows, cap, mult):
    for cand in range(min(rows, cap), 0, -1):
        if rows % cand == 0 and cand % mult == 0:
            return cand
    raise ValueError((rows, cap, mult))


class _Carry:
    def __init__(self, operands, out_shapes, aliases, sems, start, finish, before_last=None):
        self.operands, self.out_shapes, self.aliases, self.sems = list(operands), list(out_shapes), dict(aliases), list(sems)
        self.start, self.finish = start, finish
        self.before_last = before_last or (lambda ins, outs, sems: None)
        self.results = None


def _join(carries):
    carries = [c for c in carries if c is not None]
    if not carries:
        return None
    if len(carries) == 1:
        return carries[0]
    offs, pos = [], [0, 0, 0]
    for c in carries:
        offs.append(tuple(pos))
        pos = [pos[0] + len(c.operands), pos[1] + len(c.out_shapes), pos[2] + len(c.sems)]
    aliases = {}
    for c, (oi, oo, _) in zip(carries, offs):
        aliases.update({oi + i: oo + o for i, o in c.aliases.items()})

    def run(which):
        def fn(ins, outs, sems):
            for c, (oi, oo, os_) in zip(carries, offs):
                getattr(c, which)(ins[oi:oi + len(c.operands)], outs[oo:oo + len(c.out_shapes)],
                                  sems[os_:os_ + len(c.sems)])
        return fn

    joined = _Carry([a for c in carries for a in c.operands], [s for c in carries for s in c.out_shapes], aliases,
                    [s for c in carries for s in c.sems], run("start"), run("finish"), run("before_last"))
    joined.parts = (carries, offs)
    return joined


def _set_results(carry, outs):
    carry.results = list(outs)
    if hasattr(carry, "parts"):
        for c, (_, oo, _) in zip(*carry.parts):
            _set_results(c, outs[oo:oo + len(c.out_shapes)])


ANY = pl.BlockSpec(memory_space=pl.ANY)


def _call(body, *, name, grid, in_specs, out_specs, out_shape, args, scratch=(), vmem_limit=None, carry=None,
          semantics=None, before_carry=None):
    assert before_carry is None or carry is not None
    n_ci, n_co, n_cs = len(args), len(out_shape), len(scratch)
    semantics = semantics or ("parallel",) * len(grid)
    vmem_limit = vmem_limit or VMEM_CLAIM
    if carry is None:
        res = pl.pallas_call(
            body, name=name, grid=grid, in_specs=list(in_specs), out_specs=tuple(out_specs), out_shape=tuple(out_shape),
            scratch_shapes=list(scratch),
            compiler_params=pltpu.CompilerParams(dimension_semantics=semantics, vmem_limit_bytes=vmem_limit),
        )(*args)
        return list(res)
    n_pi, n_po = len(carry.operands), len(carry.out_shapes)

    def full_body(*refs):
        ci, pi = refs[:n_ci], refs[n_ci:n_ci + n_pi]
        o0 = n_ci + n_pi
        co, po = refs[o0:o0 + n_co], refs[o0 + n_co:o0 + n_co + n_po]
        s0 = o0 + n_co + n_po
        cs, ps = refs[s0:s0 + n_cs], refs[s0 + n_cs:]
        ids = [pl.program_id(ax) for ax in range(len(grid))]
        first, last = ids[0] == 0, ids[0] == grid[0] - 1
        for ax in range(1, len(grid)):
            first, last = first & (ids[ax] == 0), last & (ids[ax] == grid[ax] - 1)

        @pl.when(first)
        def _():
            if before_carry is not None:
                before_carry(*ci, *co, *cs)
            carry.start(pi, po, ps)

        several_steps = any(g > 1 for g in grid)
        if several_steps:
            @pl.when(last)
            def _():
                carry.before_last(pi, po, ps)

        body(*ci, *co, *cs)

        @pl.when(last)
        def _():
            if not several_steps:
                carry.before_last(pi, po, ps)
            carry.finish(pi, po, ps)

    res = pl.pallas_call(
        full_body, name=name, grid=grid, in_specs=list(in_specs) + [ANY] * n_pi,
        out_specs=tuple(out_specs) + tuple([ANY] * n_po), out_shape=tuple(out_shape) + tuple(carry.out_shapes),
        scratch_shapes=list(scratch) + carry.sems,
        input_output_aliases={n_ci + i: n_co + o for i, o in carry.aliases.items()},
        compiler_params=pltpu.CompilerParams(dimension_semantics=("arbitrary",) * len(grid),
                                             vmem_limit_bytes=vmem_limit),
    )(*args, *carry.operands)
    _set_results(carry, res[n_co:])
    return list(res[:n_co])


def _comm_call(carry, *, name):
    n_pi, n_po = len(carry.operands), len(carry.out_shapes)

    def body(*refs):
        pi, po, ps = refs[:n_pi], refs[n_pi:n_pi + n_po], refs[n_pi + n_po:]
        carry.start(pi, po, ps)
        carry.before_last(pi, po, ps)
        carry.finish(pi, po, ps)

    res = pl.pallas_call(
        body, name=name, in_specs=[ANY] * n_pi, out_specs=tuple([ANY] * n_po), out_shape=tuple(carry.out_shapes),
        scratch_shapes=carry.sems, input_output_aliases=dict(carry.aliases),
    )(*carry.operands)
    _set_results(carry, res)


def _mm_nn(a, b, *, bm, bn, out_dtype, name, bias=None, carry=None):
    M, K = a.shape
    stacked = b.ndim == 3
    N = b.shape[0] * b.shape[2] if stacked else b.shape[1]
    assert M % bm == 0 and N % bn == 0 and (not stacked or bn == b.shape[2])

    def body(*refs):
        a_ref, b_ref = refs[0], refs[1]
        o_ref = refs[-1]
        acc = jnp.dot(a_ref[...], b_ref[...], preferred_element_type=F32)
        if bias is not None:
            acc = acc + refs[2][...]
        o_ref[...] = acc.astype(o_ref.dtype)

    if stacked:
        b_spec = pl.BlockSpec((None, K, bn), lambda i, j: (j, 0, 0))
    else:
        b_spec = pl.BlockSpec((K, bn), lambda i, j: (0, j))
    in_specs = [pl.BlockSpec((bm, K), lambda i, j: (i, 0)), b_spec]
    args = [a, b]
    if bias is not None:
        in_specs.append(pl.BlockSpec((1, bn), lambda i, j: (0, j)))
        args.append(bias)
    limit = _vmem_limit(_nbytes((bm, K), a.dtype), _nbytes((K, bn), b.dtype), _nbytes((bm, bn), F32))
    return _call(body, name=name, grid=(M // bm, N // bn), in_specs=in_specs,
                 out_specs=[pl.BlockSpec((bm, bn), lambda i, j: (i, j))],
                 out_shape=[jax.ShapeDtypeStruct((M, N), out_dtype)], args=args, vmem_limit=limit, carry=carry)[0]


def _mm_nt(a, b, *, bm, bn, out_dtype, name, add=None, carry=None):
    M, K = a.shape
    stacked = b.ndim == 3
    N = b.shape[1] if stacked else b.shape[0]
    n_sh = b.shape[0] if stacked else 1
    ks = b.shape[2] if stacked else K
    assert M % bm == 0 and N % bn == 0 and n_sh * ks == K
    dn = (((1,), (1,)), ((), ()))

    def body(*refs):
        a_ref, b_ref = refs[0], refs[1]
        o_ref = refs[-1]
        if stacked:
            acc = lax.dot_general(a_ref[:, 0:ks], b_ref[0], dn, preferred_element_type=F32)
            for s in range(1, n_sh):
                acc = acc + lax.dot_general(a_ref[:, s * ks:(s + 1) * ks], b_ref[s], dn, preferred_element_type=F32)
        else:
            acc = lax.dot_general(a_ref[...], b_ref[...], dn, preferred_element_type=F32)
        if add is not None:
            acc = acc + refs[2][...]
        o_ref[...] = acc.astype(o_ref.dtype)

    if stacked:
        b_spec = pl.BlockSpec((n_sh, bn, ks), lambda i, j: (0, j, 0))
    else:
        b_spec = pl.BlockSpec((bn, K), lambda i, j: (j, 0))
    in_specs = [pl.BlockSpec((bm, K), lambda i, j: (i, 0)), b_spec]
    args = [a, b]
    if add is not None:
        rows = bm if add.shape[0] == M else 1
        in_specs.append(pl.BlockSpec((rows, bn), lambda i, j: (i if rows == bm else 0, j)))
        args.append(add)
    limit = _vmem_limit(_nbytes((bm, K), a.dtype), _nbytes((bn, K), b.dtype), _nbytes((bm, bn), F32))
    return _call(body, name=name, grid=(M // bm, N // bn), in_specs=in_specs,
                 out_specs=[pl.BlockSpec((bm, bn), lambda i, j: (i, j))],
                 out_shape=[jax.ShapeDtypeStruct((M, N), out_dtype)], args=args, vmem_limit=limit, carry=carry)[0]


def _mm_tn(a, b, *, bm, bn, out_dtype, name, n_stack=None, carry=None):
    T, M = a.shape
    N = b.shape[1]
    assert M % bm == 0 and N % bn == 0 and (n_stack is None or bn * n_stack == N)
    dn = (((0,), (0,)), ((), ()))

    def body(a_ref, b_ref, o_ref):
        acc = lax.dot_general(a_ref[...], b_ref[...], dn, preferred_element_type=F32)
        o_ref[...] = acc.astype(o_ref.dtype)

    if n_stack is None:
        out_spec = pl.BlockSpec((bm, bn), lambda i, j: (i, j))
        out_shape = jax.ShapeDtypeStruct((M, N), out_dtype)
    else:
        out_spec = pl.BlockSpec((None, bm, bn), lambda i, j: (j, i, 0))
        out_shape = jax.ShapeDtypeStruct((n_stack, M, bn), out_dtype)
    limit = _vmem_limit(_nbytes((T, bm), a.dtype), _nbytes((T, bn), b.dtype), _nbytes((bm, bn), F32))
    return _call(body, name=name, grid=(M // bm, N // bn),
                 in_specs=[pl.BlockSpec((T, bm), lambda i, j: (0, i)), pl.BlockSpec((T, bn), lambda i, j: (0, j))],
                 out_specs=[out_spec], out_shape=[out_shape], args=[a, b], vmem_limit=limit, carry=carry)[0]


ROW_BLK = 256


def _rstd(x):
    return lax.rsqrt(jnp.mean(x * x, axis=-1, keepdims=True) + EPS)


def _norm_bwd(xhat, rstd, dxhat):
    return rstd * (dxhat - xhat * jnp.mean(dxhat * xhat, axis=-1, keepdims=True))


def _row_spec(cols):
    return pl.BlockSpec((ROW_BLK, cols), lambda i: (i, 0))


def _vec_spec(cols):
    return pl.BlockSpec((1, cols), lambda i: (0, 0))


def _prologue(x, g, rel_bias, idx_all, *, name, carry=None):
    T, D = x.shape
    n_pat = idx_all.shape[0]
    heads = 2 * N_PAIRS
    steps = n_pat * heads
    rows = T // steps

    def body(rb_ref, x_ref, g_ref, idx_ref, n_ref, bias_ref):
        s = pl.program_id(0)
        head = jnp.where(s < heads, s, heads + s % heads)
        xv = x_ref[...]
        n_ref[...] = (xv * _rstd(xv) * g_ref[...]).astype(n_ref.dtype)
        idxv = idx_ref[...]
        acc = jnp.where(idxv < 0, NEG, 0.0).astype(F32)
        for b in range(NUM_BUCKETS):
            acc = acc + jnp.where(idxv == b, rb_ref[b, head], 0.0)
        bias_ref[0] = acc
        kap = lax.broadcasted_iota(jnp.int32, (1, 2 * QBLK), 1)
        bias_ref[1] = jnp.where(kap < QBLK, NEG, acc)

    return _call(
        body, name=name, grid=(steps,),
        in_specs=[pl.BlockSpec(memory_space=pltpu.SMEM), pl.BlockSpec((rows, D), lambda s: (s, 0)),
                  pl.BlockSpec((1, D), lambda s: (0, 0)),
                  pl.BlockSpec((None, QBLK, 2 * QBLK), lambda s: (s // heads, 0, 0))],
        out_specs=[pl.BlockSpec((rows, D), lambda s: (s, 0)),
                   pl.BlockSpec((None, 2, None, QBLK, 2 * QBLK), lambda s: (s // heads, 0, s % heads, 0, 0))],
        out_shape=[jax.ShapeDtypeStruct((T, D), BF16),
                   jax.ShapeDtypeStruct((n_pat, 2, heads, QBLK, 2 * QBLK), F32)],
        args=[rel_bias, x, g, idx_all], carry=carry)


def _mm_resid_norm(a, b, h, g_post, coef, g_next, *, bm, name, bias=None, carry=None):
    M, K = a.shape
    N = b.shape[1]

    def body(*refs):
        a_ref, b_ref, h_ref, gp_ref, gn_ref = refs[0:5]
        f_ref, hn_ref, n_ref = refs[-3:]
        for rows in _row_halves(bm):
            f = jnp.dot(a_ref[rows, :], b_ref[...], preferred_element_type=F32)
            if bias is not None:
                f = f + refs[5][...]
            f_ref[rows, :] = f
            hn = h_ref[rows, :] + coef * (f * _rstd(f) * gp_ref[...])
            hn_ref[rows, :] = hn
            n_ref[rows, :] = (hn * _rstd(hn) * gn_ref[...]).astype(n_ref.dtype)

    row = lambda w: pl.BlockSpec((bm, w), lambda i: (i, 0))
    vec = pl.BlockSpec((1, N), lambda i: (0, 0))
    in_specs = [row(K), pl.BlockSpec((K, N), lambda i: (0, 0), pipeline_mode=pl.Buffered(1)), row(N), vec, vec]
    args = [a, b, h, g_post, g_next]
    if bias is not None:
        in_specs.append(vec)
        args.append(bias)
    limit = _vmem_limit(_nbytes((bm, K), a.dtype), _nbytes((K, N), b.dtype) // 2, 4 * _nbytes((bm, N), F32))
    return _call(body, name=name, grid=(M // bm,), in_specs=in_specs, out_specs=[row(N), row(N), row(N)],
                 out_shape=[jax.ShapeDtypeStruct((M, N), F32), jax.ShapeDtypeStruct((M, N), F32),
                            jax.ShapeDtypeStruct((M, N), BF16)], args=args, vmem_limit=limit, carry=carry)


def _mm_nt_norms_bwd(a, b, dh_in, h, g_pre, f, g_post, coef, *, bm, name, add=None, b_is_kn=False, carry=None):
    M, K = a.shape
    stacked = b.ndim == 3
    N = b.shape[1] if (stacked or b_is_kn) else b.shape[0]
    n_sh = b.shape[0] if stacked else 1
    ks = b.shape[2] if stacked else K
    assert n_sh * ks == K
    dn_dims = (((1,), (0,)), ((), ())) if b_is_kn else (((1,), (1,)), ((), ()))
    with_f = f is not None
    n_in = 5 + (2 if with_f else 0) + (1 if add is not None else 0)

    def body(*refs):
        a_ref, b_ref, dhi_ref, h_ref, gpre_ref = refs[0:5]
        outs = refs[n_in:]

        @pl.when(pl.program_id(0) == 0)
        def _():
            for acc in (outs[2:] if with_f else outs[1:]):
                acc[...] = jnp.zeros_like(acc)

        for rows in _row_halves(bm):
            if stacked:
                dn = lax.dot_general(a_ref[rows, 0:ks], b_ref[0], dn_dims, preferred_element_type=F32)
                for s in range(1, n_sh):
                    dn = dn + lax.dot_general(a_ref[rows, s * ks:(s + 1) * ks], b_ref[s], dn_dims,
                                              preferred_element_type=F32)
            else:
                dn = lax.dot_general(a_ref[rows, :], b_ref[...], dn_dims, preferred_element_type=F32)
            if add is not None:
                dn = dn + refs[n_in - 1][rows, :]
            hv = h_ref[rows, :]
            r = _rstd(hv)
            hh = hv * r
            dh = dhi_ref[rows, :] + _norm_bwd(hh, r, dn * gpre_ref[...])
            outs[0][rows, :] = dh
            if not with_f:
                outs[1][...] += jnp.sum(dn * hh, axis=0, keepdims=True)
                continue
            f_ref, gpost_ref = refs[5], refs[6]
            df_ref, dgpre_ref, dgpost_ref, dsum_ref = outs[1:]
            dgpre_ref[...] += jnp.sum(dn * hh, axis=0, keepdims=True)
            fv = f_ref[rows, :]
            rf = _rstd(fv)
            fh = fv * rf
            dy = coef * dh
            dgpost_ref[...] += jnp.sum(dy * fh, axis=0, keepdims=True)
            df = _norm_bwd(fh, rf, dy * gpost_ref[...])
            dsum_ref[...] += jnp.sum(df, axis=0, keepdims=True)
            df_ref[rows, :] = df.astype(df_ref.dtype)

    row = lambda w: pl.BlockSpec((bm, w), lambda i: (i, 0))
    vec = pl.BlockSpec((1, N), lambda i: (0, 0))
    once = pl.Buffered(1)
    if stacked:
        b_spec = pl.BlockSpec((n_sh, N, ks), lambda i: (0, 0, 0), pipeline_mode=once)
    else:
        b_spec = pl.BlockSpec(b.shape, lambda i: (0, 0), pipeline_mode=once)
    in_specs = [row(K), b_spec, row(N), row(N), vec]
    args = [a, b, dh_in, h, g_pre]
    if with_f:
        in_specs += [row(N), vec]
        args += [f, g_post]
    if add is not None:
        in_specs.append(row(N))
        args.append(add)
    vec_shape = jax.ShapeDtypeStruct((1, N), F32)
    out_specs = [row(N)] + ([row(N), vec, vec, vec] if with_f else [vec])
    out_shape = [jax.ShapeDtypeStruct((M, N), F32)]
    out_shape += [jax.ShapeDtypeStruct((M, N), BF16), vec_shape, vec_shape, vec_shape] if with_f else [vec_shape]
    limit = _vmem_limit(_nbytes((bm, K), a.dtype), _nbytes((N, K), b.dtype) // 2, 6 * _nbytes((bm, N), F32))
    return _call(body, name=name, grid=(M // bm,), in_specs=in_specs, out_specs=out_specs, out_shape=out_shape,
                 args=args, vmem_limit=limit, semantics=("arbitrary",), carry=carry)


def _ffn_up(n, w_gu, *, bm, name, carry=None):
    M, K = n.shape
    n_sh, _, ns = w_gu.shape
    half = n_sh // 2

    def body(n_ref, wg_ref, wu_ref, g_ref, u_ref, a_ref):
        nv = n_ref[...]
        for cols in _col_chunks(ns):
            g = jnp.dot(nv, wg_ref[:, cols], preferred_element_type=F32)
            u = jnp.dot(nv, wu_ref[:, cols], preferred_element_type=F32)
            g_ref[:, cols] = g.astype(g_ref.dtype)
            u_ref[:, cols] = u.astype(u_ref.dtype)
            a_ref[:, cols] = (g * jax.nn.sigmoid(g) * u).astype(a_ref.dtype)

    out_spec = pl.BlockSpec((bm, ns), lambda i, j: (i, j))
    out = jax.ShapeDtypeStruct((M, half * ns), BF16)
    limit = _vmem_limit(_nbytes((bm, K), n.dtype), 2 * _nbytes((K, ns), w_gu.dtype), 3 * _nbytes((bm, ns), F32))
    return _call(body, name=name, grid=(M // bm, half),
                 in_specs=[pl.BlockSpec((bm, K), lambda i, j: (i, 0)),
                           pl.BlockSpec((None, K, ns), lambda i, j: (j, 0, 0)),
                           pl.BlockSpec((None, K, ns), lambda i, j: (j + half, 0, 0))],
                 out_specs=[out_spec, out_spec, out_spec], out_shape=[out, out, out], args=[n, w_gu, w_gu],
                 vmem_limit=limit, carry=carry)


def _ffn_down_bwd(df, w_down, g, u, *, bm, name, carry=None):
    M, D = df.shape
    F = w_down.shape[0]
    dn = (((1,), (1,)), ((), ()))

    steps = M // bm
    assert steps >= RING_SLOTS - 1

    def body(df_ref, w_ref, g_hbm, u_hbm, o_ref, g_buf, u_buf, sems):
        i = pl.program_id(0)

        def copies(step, slot):
            return [pltpu.make_async_copy(src.at[pl.ds(step * bm, bm)], buf.at[slot], sems.at[k, slot])
                    for k, (src, buf) in enumerate(((g_hbm, g_buf), (u_hbm, u_buf)))]

        @pl.when(i == 0)
        def _():
            for s in range(RING_SLOTS - 1):
                for cp in copies(s, s):
                    cp.start()

        ahead = i + (RING_SLOTS - 1)

        @pl.when(ahead < steps)
        def _():
            for cp in copies(ahead, ahead % RING_SLOTS):
                cp.start()

        slot = i % RING_SLOTS
        for cp in copies(i, slot):
            cp.wait()
        g_ref, u_ref = g_buf.at[slot], u_buf.at[slot]
        dfv = df_ref[...]
        for cols in _col_chunks(F):
            da = lax.dot_general(dfv, w_ref[cols, :], dn, preferred_element_type=F32)
            gv = g_ref[:, cols].astype(F32)
            s = jax.nn.sigmoid(gv)
            o_ref[:, cols] = (da * u_ref[:, cols].astype(F32) * (s * (1.0 + gv * (1.0 - s)))).astype(o_ref.dtype)
            o_ref[:, F + cols.start:F + cols.stop] = (da * (gv * s)).astype(o_ref.dtype)

    row = lambda w: pl.BlockSpec((bm, w), lambda i: (i, 0))
    limit = _vmem_limit(_nbytes((F, D), w_down.dtype) // 2, RING_SLOTS * _nbytes((bm, F), BF16), _nbytes((bm, 2 * F), BF16))
    return _call(body, name=name, grid=(steps,),
                 in_specs=[row(D), pl.BlockSpec((F, D), lambda i: (0, 0), pipeline_mode=pl.Buffered(1)), ANY, ANY],
                 out_specs=[row(2 * F)], out_shape=[jax.ShapeDtypeStruct((M, 2 * F), BF16)],
                 args=[df, w_down, g, u], vmem_limit=limit, carry=carry, semantics=("arbitrary",),
                 scratch=[pltpu.VMEM((RING_SLOTS, bm, F), g.dtype), pltpu.VMEM((RING_SLOTS, bm, F), u.dtype),
                          pltpu.SemaphoreType.DMA((2, RING_SLOTS))])[0]


def _ple_head(n4, w_gate, p, w_ple, h3, g_post, target, *, bm, name):
    T, D = h3.shape
    kp = p.shape[1]

    def body(n_ref, wg_ref, p_ref, wp_ref, h_ref, g_ref, t_ref, dh_ref, de_ref, dgp_ref, loss_ref, dg_ref):
        @pl.when(pl.program_id(0) == 0)
        def _():
            loss_ref[...] = jnp.zeros_like(loss_ref)
            dg_ref[...] = jnp.zeros_like(dg_ref)

        gpost = g_ref[...]
        for rows in _row_halves(bm):
            gate = jax.nn.sigmoid(jnp.dot(n_ref[rows, :], wg_ref[...], preferred_element_type=F32))
            ev = jnp.dot(p_ref[rows, :], wp_ref[...], preferred_element_type=F32)
            ge = gate * ev
            r = _rstd(ge)
            gh = ge * r
            diff = h_ref[rows, :] + gh * gpost - t_ref[rows, :]
            loss_ref[...] += jnp.sum(diff * diff) * (0.5 / D)
            dh = diff * (1.0 / D)
            dh_ref[rows, :] = dh
            dg_ref[...] += jnp.sum(dh * gh, axis=0, keepdims=True)
            dge = _norm_bwd(gh, r, dh * gpost)
            de_ref[rows, :] = (dge * gate).astype(de_ref.dtype)
            dgp_ref[rows, :] = (dge * ev * gate * (1.0 - gate)).astype(dgp_ref.dtype)

    row = lambda w: pl.BlockSpec((bm, w), lambda i: (i, 0))
    whole = lambda a: pl.BlockSpec(a.shape, lambda i: (0, 0), pipeline_mode=pl.Buffered(1))
    limit = _vmem_limit(_nbytes((bm, D), BF16), _nbytes(w_gate.shape, w_gate.dtype) // 2, 6 * _nbytes((bm, D), F32))
    return pl.pallas_call(
        body, name=name, grid=(T // bm,),
        in_specs=[row(D), whole(w_gate), row(kp), whole(w_ple), row(D), _vec_spec(D), row(D)],
        out_specs=(row(D), row(D), row(D), _vec_spec(LANES), _vec_spec(D)),
        out_shape=(jax.ShapeDtypeStruct((T, D), F32), jax.ShapeDtypeStruct((T, D), BF16),
                   jax.ShapeDtypeStruct((T, D), BF16), jax.ShapeDtypeStruct((1, LANES), F32),
                   jax.ShapeDtypeStruct((1, D), F32)),
        compiler_params=pltpu.CompilerParams(dimension_semantics=("arbitrary",), vmem_limit_bytes=limit),
    )(n4, w_gate, p, w_ple, h3, g_post, target)


def _t5_index(max_dist, stride):
    rho = np.arange(QBLK)[:, None]
    kap = np.arange(2 * QBLK)[None, :]
    d = rho + QBLK - kap
    valid = (d >= 0) & (d <= max_dist)
    n = np.maximum(d, 0) * stride
    max_exact = NUM_BUCKETS // 2
    nf = np.maximum(n, 1).astype(np.float32)
    large = max_exact + (np.log(nf / np.float32(max_exact)) / np.float32(math.log(MAX_DISTANCE / max_exact))
                         * np.float32(NUM_BUCKETS - max_exact)).astype(np.int32)
    large = np.minimum(large, NUM_BUCKETS - 1)
    bucket = np.where(n < max_exact, n, large)
    return np.where(valid, bucket, -1).astype(np.int32)


def _bias_grad(d_bias, idx, *, name):
    heads = 2 * N_PAIRS

    def body(db_ref, idx_ref, o_ref, part_ref):
        idxv = idx_ref[...]
        for b in range(NUM_BUCKETS):
            in_bucket = idxv == b
            for h in range(heads):
                masked = jnp.where(in_bucket, db_ref[h], 0.0)
                part_ref[h, b] = jnp.sum(masked.reshape(QBLK // SUBLANES, SUBLANES, 2 * QBLK), axis=0)
        lanes = lax.broadcasted_iota(jnp.int32, (NUM_BUCKETS, LANES), 1)
        acc = jnp.zeros((NUM_BUCKETS, LANES), F32)
        for h in range(heads):
            sums = jnp.sum(jnp.sum(part_ref[h], axis=1), axis=1, keepdims=True)
            acc = acc + jnp.where(lanes == h, sums, 0.0)
        o_ref[...] = acc

    return pl.pallas_call(
        body, name=name, grid=(1,),
        in_specs=[pl.BlockSpec((heads, QBLK, 2 * QBLK), lambda i: (0, 0, 0)),
                  pl.BlockSpec((QBLK, 2 * QBLK), lambda i: (0, 0))],
        out_specs=pl.BlockSpec((NUM_BUCKETS, LANES), lambda i: (0, 0)),
        out_shape=jax.ShapeDtypeStruct((NUM_BUCKETS, LANES), F32),
        scratch_shapes=[pltpu.VMEM((heads, NUM_BUCKETS, SUBLANES, 2 * QBLK), F32)],
        compiler_params=pltpu.CompilerParams(dimension_semantics=("arbitrary",)),
    )(d_bias, idx)


def _lane():
    return lax.broadcasted_iota(jnp.int32, (1, LANES), 1)


def _head_sel(h):
    return (_lane() < HEAD_DIM) if h == 0 else (_lane() >= HEAD_DIM)


def _stat_lo():
    return (_lane() % HEAD_DIM) < (HEAD_DIM // 2)


def _stack_heads(t, scale=None):
    if scale is not None:
        t = t * scale
    zero = jnp.zeros_like(t)
    return jnp.concatenate([jnp.where(_head_sel(0), t, zero), jnp.where(_head_sel(1), t, zero)], axis=0)


def _class_spec(L, r, cols, stride, pps):
    chunks = N_PAIRS // pps
    mode = pl.Buffered(1) if r * chunks == 1 else None
    return pl.BlockSpec((L, MIX_W // chunks), lambda j, c: (0, (cols + j * stride) * chunks + c), pipeline_mode=mode)


def _rows(b, n_blocks=1):
    return pl.ds(pl.multiple_of(b * QBLK, QBLK), n_blocks * QBLK)


def _key_window(ref, b, cols, first, kv_head=None):
    if kv_head is not None:
        cols = slice(0, LANES)
    if first:
        blk = ref[0:QBLK, cols]
        tile = jnp.concatenate([blk, blk], axis=0)
    else:
        tile = ref[_rows(b - 1, 2), cols]
    if kv_head is None:
        return tile
    wide = tile.astype(F32)
    keep = jnp.logical_xor(_lane() < HEAD_DIM, kv_head == 1)
    return jnp.where(keep, wide, pltpu.roll(wide, HEAD_DIM, 1)).astype(tile.dtype)


def _kv_spec(L, r, col, stride, pps, kv_shared):
    if not kv_shared:
        return _class_spec(L, r, col, stride, pps)
    mode = pl.Buffered(1) if pps == N_PAIRS else None
    return pl.BlockSpec((L, LANES), lambda j, c: (0, col), pipeline_mode=mode)


def _band_fwd(qa, ka, va, bias, *, r, q_col, k_col, v_col, stride, pattern, name, kv_shared=False, sink=None,
              carry=None):
    L = qa.shape[0]
    nb = L // QBLK
    dn = (((1,), (1,)), ((), ()))
    scale = HEAD_DIM ** -0.5

    pps = N_PAIRS

    def body(q_ref, k_ref, v_ref, bias_ref, *rest):
        sel0 = _head_sel(0)
        pair0 = pl.program_id(1) * pps

        def emit(rows, cols, o, lse):
            if sink is None:
                o_ref, lse_ref = rest
                o_ref[rows, cols] = o.astype(o_ref.dtype)
                lse_ref[rows, cols] = lse
                return
            sink_ref, out_ref, outb_ref, lse_ref = rest
            sk = sink_ref[:, cols]
            mx = jnp.maximum(lse, sk)
            w = jnp.exp(lse - mx)
            den = w + jnp.exp(sk - mx)
            out = o * (w / den)
            out_ref[rows, cols] = out
            outb_ref[rows, cols] = out.astype(outb_ref.dtype)
            lse_ref[rows, cols] = mx + jnp.log(den)

        def block(b, first):
            rows = _rows(b)
            for i in range(pps):
                cols = slice(i * LANES, (i + 1) * LANES)
                q2 = _stack_heads(q_ref[rows, cols], scale)
                kv_head = (pair0 + i) // 2 if kv_shared else None
                k = _key_window(k_ref, b, cols, first, kv_head)
                v = _key_window(v_ref, b, cols, first, kv_head)
                bias2 = bias_ref[1 if first else 0, pl.ds(2 * (pair0 + i), 2)].reshape(2 * QBLK, 2 * QBLK)
                s = lax.dot_general(q2, k, dn, preferred_element_type=F32) + bias2
                m = jnp.max(s, axis=1, keepdims=True)
                p = jnp.exp(s - m)
                l = jnp.sum(p, axis=1, keepdims=True)
                o = jnp.dot(p.astype(v.dtype), v, preferred_element_type=F32) * (1.0 / l)
                lse = m + jnp.log(l)
                emit(rows, cols, jnp.where(sel0, o[0:QBLK], o[QBLK:]), jnp.where(sel0, lse[0:QBLK], lse[QBLK:]))

        block(0, True)
        if nb > 1:
            pl.loop(1, nb)(lambda b: block(b, False))

    bias_spec = pl.BlockSpec((None, 2, 2 * N_PAIRS, QBLK, 2 * QBLK), lambda j, c: (pattern, 0, 0, 0, 0))
    out_spec = _class_spec(L, r, 0, 1, pps)
    blk_bytes = _nbytes((L, pps * LANES), F32)
    in_specs = [_class_spec(L, r, q_col, stride, pps), _kv_spec(L, r, k_col, stride, pps, kv_shared),
                _kv_spec(L, r, v_col, stride, pps, kv_shared), bias_spec]
    args = [qa, ka, va, bias]
    f32_out, bf16_out = jax.ShapeDtypeStruct((L, r * MIX_W), F32), jax.ShapeDtypeStruct((L, r * MIX_W), BF16)
    out_shape = [bf16_out, f32_out]
    if sink is not None:
        in_specs.append(pl.BlockSpec((1, MIX_W), lambda j, c: (0, 0)))
        args.append(sink)
        out_shape = [f32_out, bf16_out, f32_out]
    return _call(body, name=name, grid=(r, N_PAIRS // pps), in_specs=in_specs, out_specs=[out_spec] * len(out_shape),
                 out_shape=out_shape, args=args, vmem_limit=_vmem_limit(*([blk_bytes] * 4)), carry=carry)


def _class_rows_spec(rows, r, width):
    return pl.BlockSpec((rows // r, r * width), lambda i, *_: (i, 0))


def _token_scratch(rows, width):
    return pltpu.VMEM((width // LANES, rows, LANES), F32)


def _fill_tokens(scratch, value):
    for c in range(scratch.shape[0]):
        scratch[c] = value[:, c * LANES:(c + 1) * LANES]


def _read_tokens(scratch):
    return jnp.concatenate([scratch[c] for c in range(scratch.shape[0])], axis=1)


def _to_tokens(blk_ref, r, scratch):
    if r == 1:
        return blk_ref[...].astype(F32)
    nt, rows, _ = scratch.shape
    for j in range(r):
        for c in range(nt):
            lanes = slice((j * nt + c) * LANES, (j * nt + c + 1) * LANES)
            scratch.at[c][pl.ds(j, rows // r, stride=r), :] = blk_ref[:, lanes].astype(F32)
    return _read_tokens(scratch)


def _from_tokens(dst_ref, r, scratch):
    nt, rows, _ = scratch.shape
    if r == 1:
        dst_ref[...] = _read_tokens(scratch).astype(dst_ref.dtype)
        return
    for j in range(r):
        for c in range(nt):
            lanes = slice((j * nt + c) * LANES, (j * nt + c + 1) * LANES)
            dst_ref[:, lanes] = scratch.at[c][pl.ds(j, rows // r, stride=r), :].astype(dst_ref.dtype)


def _mm_nt_classes(a, b, bias, dils, *, bm, name):
    M, K = a.shape
    N = b.shape[0]
    dn = (((1,), (1,)), ((), ()))

    def body(a_ref, b_ref, bias_ref, *rest):
        outs, tile = rest[:-1], rest[-1]
        _fill_tokens(tile, lax.dot_general(a_ref[...], b_ref[...], dn, preferred_element_type=F32) + bias_ref[...])
        for out, r in zip(outs, dils):
            _from_tokens(out, r, tile)

    limit = _vmem_limit(_nbytes((bm, K), a.dtype), _nbytes((K, N), b.dtype) // 2, (1 + len(dils)) * _nbytes((bm, N), F32))
    return pl.pallas_call(
        body, name=name, grid=(M // bm,),
        in_specs=[pl.BlockSpec((bm, K), lambda i: (i, 0)),
                  pl.BlockSpec((N, K), lambda i: (0, 0), pipeline_mode=pl.Buffered(1)),
                  pl.BlockSpec((1, N), lambda i: (0, 0))],
        out_specs=tuple(_class_rows_spec(bm, r, N) for r in dils),
        out_shape=tuple(jax.ShapeDtypeStruct((M // r, r * N), BF16) for r in dils),
        scratch_shapes=[_token_scratch(bm, N)],
        compiler_params=pltpu.CompilerParams(dimension_semantics=("parallel",), vmem_limit_bytes=limit),
    )(a, b, bias)


def _merge(branches, dils, sink, *, name):
    W = MIX_W
    T = branches[0][0].shape[0] * dils[0]
    nbr = len(branches)

    def body(*refs):
        sink_ref = refs[2 * nbr]
        out_ref, outb_ref, lse_ref, scratch = refs[2 * nbr + 1:]
        sk = sink_ref[...]
        lses = [_to_tokens(refs[2 * t + 1], dils[t], scratch) for t in range(nbr)]
        mx = sk
        for lse in lses:
            mx = jnp.maximum(mx, lse)
        den = jnp.exp(sk - mx)
        num = jnp.zeros_like(mx)
        for t in range(nbr):
            w = jnp.exp(lses[t] - mx)
            den = den + w
            num = num + w * _to_tokens(refs[2 * t], dils[t], scratch)
        out = num / den
        out_ref[...] = out
        outb_ref[...] = out.astype(outb_ref.dtype)
        lse_ref[...] = mx + jnp.log(den)

    args = [a for br in branches for a in br] + [sink]
    in_specs = [_class_rows_spec(ROW_BLK, r, W) for r in dils for _ in range(2)] + [_vec_spec(W)]
    return pl.pallas_call(
        body, name=name, grid=(T // ROW_BLK,), in_specs=in_specs,
        out_specs=(_row_spec(W), _row_spec(W), _row_spec(W)),
        out_shape=(jax.ShapeDtypeStruct((T, W), F32), jax.ShapeDtypeStruct((T, W), BF16),
                   jax.ShapeDtypeStruct((T, W), F32)),
        scratch_shapes=[_token_scratch(ROW_BLK, W)],
        compiler_params=pltpu.CompilerParams(dimension_semantics=("parallel",)),
    )(*args)


def _attn_delta(dmix, outs, lses, sink, dils, *, name):
    T = dmix.shape[0]
    W = MIX_W
    nd = len(dils)

    def body(dmix_ref, oa_ref, ob_ref, la_ref, lb_ref, sink_ref, *rest):
        doa_ref, sta_ref, dsink_ref = rest[0:3]
        dob_refs, stb_refs = rest[3:3 + nd], rest[3 + nd:3 + 2 * nd]
        do_tile, st_tile = rest[3 + 2 * nd:]

        @pl.when(pl.program_id(0) == 0)
        def _():
            dsink_ref[...] = jnp.zeros_like(dsink_ref)

        sel0, lo = _head_sel(0), _stat_lo()
        for mixer, (o_ref, l_ref) in enumerate(((oa_ref, la_ref), (ob_ref, lb_ref))):
            for i in range(N_PAIRS):
                cols = slice(i * LANES, (i + 1) * LANES)
                do = dmix_ref[:, mixer * W + i * LANES:mixer * W + (i + 1) * LANES]
                prod = do * o_ref[:, cols]
                d0 = jnp.sum(jnp.where(sel0, prod, 0.0), axis=1, keepdims=True)
                d1 = jnp.sum(jnp.where(sel0, 0.0, prod), axis=1, keepdims=True)
                delta = jnp.where(sel0, d0, d1)
                lse = l_ref[:, cols]
                stat = jnp.where(lo, lse, delta)
                if mixer == 0:
                    sta_ref[:, cols] = stat
                    doa_ref[:, cols] = do.astype(doa_ref.dtype)
                    dsink_ref[:, cols] += -jnp.sum(jnp.exp(sink_ref[:, cols] - lse) * delta, axis=0, keepdims=True)
                else:
                    st_tile[i] = stat
                    do_tile[i] = do
        for t, r in enumerate(dils):
            _from_tokens(dob_refs[t], r, do_tile)
            _from_tokens(stb_refs[t], r, st_tile)

    class_specs = [_class_rows_spec(ROW_BLK, r, W) for r in dils]
    out_shape = [jax.ShapeDtypeStruct((T, W), BF16), jax.ShapeDtypeStruct((T, W), F32), jax.ShapeDtypeStruct((1, W), F32)]
    out_shape += [jax.ShapeDtypeStruct((T // r, r * W), BF16) for r in dils]
    out_shape += [jax.ShapeDtypeStruct((T // r, r * W), F32) for r in dils]
    res = pl.pallas_call(
        body, name=name, grid=(T // ROW_BLK,),
        in_specs=[_row_spec(2 * W), _row_spec(W), _row_spec(W), _row_spec(W), _row_spec(W), _vec_spec(W)],
        out_specs=tuple([_row_spec(W), _row_spec(W), _vec_spec(W)] + class_specs + class_specs),
        out_shape=tuple(out_shape),
        scratch_shapes=[_token_scratch(ROW_BLK, W), _token_scratch(ROW_BLK, W)],
        compiler_params=pltpu.CompilerParams(dimension_semantics=("arbitrary",)),
    )(dmix, outs[0], outs[1], lses[0], lses[1], sink)
    return res[0], res[1], res[2], res[3:3 + nd], res[3 + nd:]


def _band_bwd(qa, ka, va, d_out, stat, bias, *, r, q_col, k_col, v_col, stride, pattern, name, kv_shared=False,
              carry=None):
    L = qa.shape[0]
    nb = L // QBLK
    dn_nt = (((1,), (1,)), ((), ()))
    dn_tn = (((0,), (0,)), ((), ()))
    scale = HEAD_DIM ** -0.5

    pps = N_PAIRS if r > 1 else N_PAIRS // 2

    def body(q_ref, k_ref, v_ref, do_ref, st_ref, bias_ref, dq_ref, dk_ref, dv_ref, db_ref, dk_carry, dv_carry):
        @pl.when((pl.program_id(0) == 0) & (pl.program_id(1) == 0))
        def _():
            db_ref[...] = jnp.zeros_like(db_ref)

        lo, sel0 = _stat_lo(), _head_sel(0)
        pair0 = pl.program_id(1) * pps

        def block(b, first):
            rows = _rows(b)
            for i in range(pps):
                heads = pl.ds(2 * (pair0 + i), 2)
                cols = slice(i * LANES, (i + 1) * LANES)
                q2 = _stack_heads(q_ref[rows, cols], scale)
                kv_head = (pair0 + i) // 2 if kv_shared else None
                k = _key_window(k_ref, b, cols, first, kv_head)
                v = _key_window(v_ref, b, cols, first, kv_head)
                do2 = _stack_heads(do_ref[rows, cols])
                st = st_ref[rows, cols]
                lse2 = jnp.concatenate(
                    [jnp.max(jnp.where(_head_sel(h) & lo, st, -jnp.inf), axis=1, keepdims=True) for h in range(2)], axis=0)
                delta2 = jnp.concatenate(
                    [jnp.sum(jnp.where(_head_sel(h) & ~lo, st, 0.0), axis=1, keepdims=True) for h in range(2)],
                    axis=0) * (2.0 / HEAD_DIM)
                bias2 = bias_ref[1 if first else 0, heads].reshape(2 * QBLK, 2 * QBLK)
                s = lax.dot_general(q2, k, dn_nt, preferred_element_type=F32) + bias2
                p = jnp.exp(s - lse2)
                dp = lax.dot_general(do2, v, dn_nt, preferred_element_type=F32)
                ds = p * (dp - delta2)
                db_ref[heads] += ds.reshape(2, QBLK, 2 * QBLK)
                dsb = ds.astype(k.dtype)
                dq2 = jnp.dot(dsb, k, preferred_element_type=F32)
                dq_ref[rows, cols] = (jnp.where(sel0, dq2[0:QBLK], dq2[QBLK:]) * scale).astype(dq_ref.dtype)
                dk_acc = lax.dot_general(dsb, q2, dn_tn, preferred_element_type=F32)
                dv_acc = lax.dot_general(p.astype(k.dtype), do2, dn_tn, preferred_element_type=F32)
                if not first:
                    prev = _rows(b - 1)
                    dk_ref[prev, cols] = (dk_carry[:, cols] + dk_acc[0:QBLK]).astype(dk_ref.dtype)
                    dv_ref[prev, cols] = (dv_carry[:, cols] + dv_acc[0:QBLK]).astype(dv_ref.dtype)
                dk_carry[:, cols] = dk_acc[QBLK:2 * QBLK]
                dv_carry[:, cols] = dv_acc[QBLK:2 * QBLK]

        block(0, True)
        if nb > 1:
            pl.loop(1, nb)(lambda b: block(b, False))

        last = _rows(nb - 1)
        dk_ref[last, :] = dk_carry[...].astype(dk_ref.dtype)
        dv_ref[last, :] = dv_carry[...].astype(dv_ref.dtype)

    tok_spec = _class_spec(L, r, 0, 1, pps)
    bias_spec = pl.BlockSpec((None, 2, 2 * N_PAIRS, QBLK, 2 * QBLK), lambda j, c: (pattern, 0, 0, 0, 0))
    dbias_spec = pl.BlockSpec((2 * N_PAIRS, QBLK, 2 * QBLK), lambda j, c: (0, 0, 0))
    grad = jax.ShapeDtypeStruct((L, r * MIX_W), BF16)
    blk_bytes = _nbytes((L, pps * LANES), BF16)
    carry_shape = pltpu.VMEM((QBLK, pps * LANES), F32)
    return _call(
        body, name=name, grid=(r, N_PAIRS // pps),
        in_specs=[_class_spec(L, r, q_col, stride, pps), _kv_spec(L, r, k_col, stride, pps, kv_shared),
                  _kv_spec(L, r, v_col, stride, pps, kv_shared), tok_spec, tok_spec, bias_spec],
        out_specs=[tok_spec, tok_spec, tok_spec, dbias_spec],
        out_shape=[grad, grad, grad, jax.ShapeDtypeStruct((2 * N_PAIRS, QBLK, 2 * QBLK), F32)],
        args=[qa, ka, va, d_out, stat, bias], scratch=[carry_shape, carry_shape],
        vmem_limit=_vmem_limit(*([blk_bytes] * 9)), semantics=("arbitrary", "arbitrary"), carry=carry)


def _dz_assemble(dq_a, dk_a, dv_a, b_grads, dils, *, name):
    T = dq_a.shape[0]
    W = MIX_W

    def group_sum(x):
        u0 = x[:, 0:LANES] + x[:, LANES:2 * LANES]
        u1 = x[:, 2 * LANES:3 * LANES] + x[:, 3 * LANES:4 * LANES]
        s0 = u0 + pltpu.roll(u0, HEAD_DIM, 1)
        s1 = u1 + pltpu.roll(u1, HEAD_DIM, 1)
        return jnp.where(_head_sel(0), s0, s1)

    def body(*refs):
        dqa_ref, dka_ref, dva_ref = refs[0:3]
        b_refs = refs[3:12]
        dza_ref, dzb_ref, sa_ref, sb_ref, scratch = refs[12:]

        @pl.when(pl.program_id(0) == 0)
        def _():
            sa_ref[...] = jnp.zeros_like(sa_ref)
            sb_ref[...] = jnp.zeros_like(sb_ref)

        def put(dst, acc, col, part):
            w = part.shape[1]
            dst[:, col:col + w] = part.astype(dst.dtype)
            acc[:, col:col + w] += jnp.sum(part, axis=0, keepdims=True)

        put(dza_ref, sa_ref, 0, dqa_ref[...].astype(F32))
        put(dza_ref, sa_ref, W, group_sum(dka_ref[...].astype(F32)))
        put(dza_ref, sa_ref, W + LANES, group_sum(dva_ref[...].astype(F32)))
        for t in range(3):
            part = _to_tokens(b_refs[t], dils[0], scratch)
            for pat in range(1, len(dils)):
                part = part + _to_tokens(b_refs[3 * pat + t], dils[pat], scratch)
            put(dzb_ref, sb_ref, t * W, part)

    args = [dq_a, dk_a, dv_a] + [g[t] for g in b_grads for t in range(3)]
    return pl.pallas_call(
        body, name=name, grid=(T // ROW_BLK,),
        in_specs=[_row_spec(W)] * 3 + [_class_rows_spec(ROW_BLK, r, W) for r in dils for _ in range(3)],
        out_specs=(_row_spec(ZA_W), _row_spec(ZB_W), _vec_spec(ZA_W), _vec_spec(ZB_W)),
        out_shape=(jax.ShapeDtypeStruct((T, ZA_W), BF16), jax.ShapeDtypeStruct((T, ZB_W), BF16),
                   jax.ShapeDtypeStruct((1, ZA_W), F32), jax.ShapeDtypeStruct((1, ZB_W), F32)),
        scratch_shapes=[_token_scratch(ROW_BLK, W)],
        compiler_params=pltpu.CompilerParams(dimension_semantics=("arbitrary",)),
    )(*args)


def _place():
    x, y, c = lax.axis_index("x"), lax.axis_index("y"), lax.axis_index("c")
    chips = [(1 - x, y), (x, 1 - y), (1 - x, 1 - y)]
    return x, y, c, chips


def _cast_place(shards, k_idx, *, name):
    n, steps = len(shards), 4

    def body(k_ref, *refs):
        for s_ref, o_ref in zip(refs[:n], refs[n:]):
            o_ref[...] = s_ref[...].astype(o_ref.dtype)

    blocks = [(a.shape[0] // steps, a.shape[1]) for a in shards]
    grid_spec = pltpu.PrefetchScalarGridSpec(
        num_scalar_prefetch=1, grid=(steps,),
        in_specs=[pl.BlockSpec(blk, lambda t, k_ref: (t, 0)) for blk in blocks],
        out_specs=tuple(pl.BlockSpec(blk, lambda t, k_ref: (k_ref[0] * steps + t, 0)) for blk in blocks))
    return pl.pallas_call(
        body, name=name, grid_spec=grid_spec,
        out_shape=tuple(jax.ShapeDtypeStruct((N_CHIPS * a.shape[0], a.shape[1]), BF16) for a in shards),
        compiler_params=pltpu.CompilerParams(dimension_semantics=("parallel",),
                                             vmem_limit_bytes=_vmem_limit(*[_nbytes(b, F32) for b in blocks])),
    )(k_idx, *shards)


def _gather_carry(fulls, jobs):
    n = len(jobs)

    def piece(ref, chip_idx, half, part):
        rs = ref.shape[0] // N_CHIPS
        rh = rs // 2
        rows = rh // part[1]
        return ref.at[pl.ds(chip_idx * rs + half * rh + part[0] * rows, rows)]

    def copy(sems, sem, src_ref, dst_ref, to):
        return pltpu.make_async_remote_copy(src_ref=src_ref, dst_ref=dst_ref, send_sem=sems[0].at[sem],
                                            recv_sem=sems[1].at[sem], device_id=to, device_id_type=MESH)

    def to_chips(ins, outs, sems, j, arrival, hops=("both", "chips")):
        i, part, hop = jobs[j]
        x, y, c, chips = _place()
        res = []
        for t, chip in enumerate(chips if hop in hops else ()):
            theirs = piece(outs[i], 2 * chip[0] + chip[1], c, part)
            src, dst = (theirs, theirs) if arrival else (piece(ins[i], 2 * x + y, c, part), piece(outs[i], 2 * x + y, c, part))
            res.append(copy(sems, 3 * j + t, src, dst, (*chip, c)))
        return res

    def to_sibling(outs, sems, j, arrival, hops=("both", "sibling")):
        i, part, hop = jobs[j]
        x, y, c, chips = _place()
        res = []
        for t, chip in enumerate(chips if hop in hops else ()):
            region = piece(outs[i], 2 * chip[0] + chip[1], 1 - c if arrival else c, part)
            res.append(copy(sems, 3 * n + 3 * j + t, region, region, (x, y, 1 - c)))
        return res

    def start(ins, outs, sems):
        for j in range(n):
            for send in to_chips(ins, outs, sems, j, False) + to_sibling(outs, sems, j, False, ("sibling",)):
                send.start()

    def before_last(ins, outs, sems):
        for j in range(n):
            for arrival, send in zip(to_chips(ins, outs, sems, j, True, ("both",)),
                                     to_sibling(outs, sems, j, False, ("both",))):
                arrival.wait_recv()
                send.start()

    def finish(ins, outs, sems):
        for j in range(n):
            for arrival in to_chips(ins, outs, sems, j, True, ("chips",)) + to_sibling(outs, sems, j, True):
                arrival.wait_recv()
        for j in range(n):
            for send in to_chips(ins, outs, sems, j, False) + to_sibling(outs, sems, j, False):
                send.wait_send()

    return _Carry(fulls, [jax.ShapeDtypeStruct(a.shape, a.dtype) for a in fulls], {i: i for i in range(len(fulls))},
                  [pltpu.SemaphoreType.DMA((6 * n,)), pltpu.SemaphoreType.DMA((6 * n,))], start, finish, before_last)


def _pair_swap_carry(grads):
    n = len(grads)

    def copy(ins, outs, sems, i):
        x, y, c, _ = _place()
        return pltpu.make_async_remote_copy(src_ref=ins[i].at[:, 1 - c], dst_ref=outs[i], send_sem=sems[0].at[i],
                                            recv_sem=sems[1].at[i], device_id=(x, y, 1 - c), device_id_type=MESH)

    def start(ins, outs, sems):
        for i in range(n):
            copy(ins, outs, sems, i).start()

    def finish(ins, outs, sems):
        for i in range(n):
            copy(ins, outs, sems, i).wait()

    return _Carry(grads, [jax.ShapeDtypeStruct((a.shape[0], a.shape[2], a.shape[3]), a.dtype) for a in grads], {},
                  [pltpu.SemaphoreType.DMA((n,)), pltpu.SemaphoreType.DMA((n,))], start, finish)


def _pair_sum(g4, got, c_idx, *, name):
    _, _, rh, cs = g4.shape
    rb = _row_block(rh, 512, 16)

    def body(c_ref, own_ref, got_ref, o_ref):
        o_ref[...] = (own_ref[...].astype(F32) + got_ref[...].astype(F32)).astype(o_ref.dtype)

    grid_spec = pltpu.PrefetchScalarGridSpec(
        num_scalar_prefetch=1, grid=(N_CHIPS, rh // rb),
        in_specs=[pl.BlockSpec((None, None, rb, cs), lambda k, t, c_ref: (k, c_ref[0], t, 0)),
                  pl.BlockSpec((None, rb, cs), lambda k, t, c_ref: (k, t, 0))],
        out_specs=pl.BlockSpec((None, rb, cs), lambda k, t, c_ref: (k, t, 0)))
    return pl.pallas_call(
        body, name=name, grid_spec=grid_spec, out_shape=jax.ShapeDtypeStruct((N_CHIPS, rh, cs), BF16),
        compiler_params=pltpu.CompilerParams(dimension_semantics=("parallel", "parallel")),
    )(c_idx, g4, got)


def _chip_scatter_carry(sums):
    n = len(sums)

    def copies(ins, outs, sems):
        x, y, c, chips = _place()
        return [pltpu.make_async_remote_copy(src_ref=ins[i].at[2 * chip[0] + chip[1]], dst_ref=outs[i].at[j],
                                             send_sem=sems[0].at[3 * i + j], recv_sem=sems[1].at[3 * i + j],
                                             device_id=(*chip, c), device_id_type=MESH)
                for i in range(n) for j, chip in enumerate(chips)]

    def start(ins, outs, sems):
        for cp in copies(ins, outs, sems):
            cp.start()

    def finish(ins, outs, sems):
        for cp in copies(ins, outs, sems):
            cp.wait()

    return _Carry(sums, [jax.ShapeDtypeStruct((3,) + a.shape[1:], a.dtype) for a in sums], {},
                  [pltpu.SemaphoreType.DMA((3 * n,)), pltpu.SemaphoreType.DMA((3 * n,))], start, finish)


def _chip_sum(sums, gots, kc_idx, *, name):
    n, steps = len(sums), 2

    def body(kc_ref, *refs):
        for own_ref, got_ref, o_ref in zip(refs[:n], refs[n:2 * n], refs[2 * n:]):
            acc = own_ref[...].astype(F32)
            for j in range(3):
                acc = acc + got_ref[j].astype(F32)
            o_ref[...] = acc

    blocks = [(a.shape[1] // steps, a.shape[2]) for a in sums]
    grid_spec = pltpu.PrefetchScalarGridSpec(
        num_scalar_prefetch=1, grid=(steps,),
        in_specs=[pl.BlockSpec((None,) + blk, lambda t, kc: (kc[0], t, 0)) for blk in blocks]
        + [pl.BlockSpec((3,) + blk, lambda t, kc: (0, t, 0)) for blk in blocks],
        out_specs=tuple(pl.BlockSpec(blk, lambda t, kc: (kc[1] * steps + t, 0)) for blk in blocks))
    return pl.pallas_call(
        body, name=name, grid_spec=grid_spec,
        out_shape=tuple(jax.ShapeDtypeStruct((2 * a.shape[1], a.shape[2]), F32) for a in sums),
        compiler_params=pltpu.CompilerParams(dimension_semantics=("parallel",),
                                             vmem_limit_bytes=_vmem_limit(*[3 * _nbytes(b, F32) for b in blocks])),
    )(kc_idx, *sums, *gots)


def _half_swap_carry(shards):
    n = len(shards)

    def copy(ins, outs, sems, i, to_my_half):
        x, y, c, _ = _place()
        rh = ins[i].shape[0] // 2
        half = c if to_my_half else 1 - c
        return pltpu.make_async_remote_copy(
            src_ref=ins[i].at[pl.ds(c * rh, rh)], dst_ref=outs[i].at[pl.ds(half * rh, rh)],
            send_sem=sems[0].at[i], recv_sem=sems[1].at[i], device_id=(x, y, 1 - c), device_id_type=MESH)

    def start(ins, outs, sems):
        for i in range(n):
            copy(ins, outs, sems, i, True).start()

    def finish(ins, outs, sems):
        for i in range(n):
            copy(ins, outs, sems, i, False).wait_recv()
        for i in range(n):
            copy(ins, outs, sems, i, True).wait_send()

    return _Carry(shards, [jax.ShapeDtypeStruct(a.shape, a.dtype) for a in shards], {i: i for i in range(n)},
                  [pltpu.SemaphoreType.DMA((n,)), pltpu.SemaphoreType.DMA((n,))], start, finish)


SMALL_ROW = 1024


def _small_layout(shapes):
    starts, row = [], 0
    for r, c in shapes:
        starts.append(row)
        row += -(-c // SMALL_ROW) if r == 1 else r
    return starts, -(-row // SUBLANES) * SUBLANES


def _small_pieces(shape, start):
    r, c = shape
    if r == 1:
        return [(start + q, min(SMALL_ROW, c - q * SMALL_ROW), q * SMALL_ROW) for q in range(-(-c // SMALL_ROW))]
    return None


def _whole_spec(shape):
    return pl.BlockSpec(tuple(shape), lambda i: (0,) * len(shape))


def _small_all_reduce(parts, *, name, carry=None):
    shapes = [a.shape for a in parts]
    starts, rows = _small_layout(shapes)
    n = len(parts)
    n_tables = sum(1 for r, _ in shapes if r > 1)
    assert all(r > 1 and c <= LANES for r, c in shapes[:n_tables]) and all(r == 1 for r, _ in shapes[n_tables:])
    table_rows = starts[n_tables]
    assert table_rows % SUBLANES == 0

    def regions(slot):
        return slot.at[pl.ds(0, table_rows), pl.ds(0, LANES)], slot.at[pl.ds(table_rows, rows - table_rows)]

    def peer(d):
        x, y, c, _ = _place()
        return 1 - x if d & 4 else x, 1 - y if d & 2 else y, 1 - c if d & 1 else c

    def slot_of(d):
        px, py, pc = peer(d)
        return 4 * px + 2 * py + pc

    def copy(refs, d, part, dst_slot):
        buf, send_sems, recv_sems = refs[2 * n:]
        sem = 2 * (d - 1) + part
        return pltpu.make_async_remote_copy(
            src_ref=regions(buf.at[slot_of(0)])[part], dst_ref=regions(buf.at[dst_slot])[part],
            send_sem=send_sems.at[sem], recv_sem=recv_sems.at[sem], device_id=peer(d), device_id_type=MESH)

    def send(*refs):
        ins, mine = refs[0:n], refs[2 * n].at[slot_of(0)]
        mine[...] = jnp.zeros((rows, SMALL_ROW), F32)
        for ref, shape, start in zip(ins, shapes, starts):
            pieces = _small_pieces(shape, start)
            if pieces is None:
                mine[start:start + shape[0], 0:shape[1]] = ref[...]
            else:
                for row, width, col in pieces:
                    mine[row:row + 1, 0:width] = ref[:, col:col + width]
        for d in range(1, 8):
            for part in range(2):
                copy(refs, d, part, slot_of(0)).start()

    def receive(*refs):
        outs, buf = refs[n:2 * n], refs[2 * n]
        mine = buf.at[slot_of(0)]
        for d in range(1, 8):
            for part in range(2):
                copy(refs, d, part, slot_of(d)).wait_recv()
        for d in range(1, 8):
            for part in range(2):
                copy(refs, d, part, slot_of(0)).wait_send()
        tables, vectors = regions(buf.at[0])
        acc_t, acc_v = tables[...], vectors[...]
        for s in range(1, 8):
            tables, vectors = regions(buf.at[s])
            acc_t, acc_v = acc_t + tables[...], acc_v + vectors[...]
        tables, vectors = regions(mine)
        tables[...] = acc_t
        vectors[...] = acc_v
        for ref, shape, start in zip(outs, shapes, starts):
            pieces = _small_pieces(shape, start)
            if pieces is None:
                ref[...] = mine[start:start + shape[0], 0:shape[1]]
            else:
                for row, width, col in pieces:
                    ref[:, col:col + width] = mine[row:row + 1, 0:width]

    def body(*refs):
        send(*refs)
        receive(*refs)

    whole = [_whole_spec(s) for s in shapes]
    return _call(body if carry is None else receive, before_carry=None if carry is None else send,
                 name=name, grid=(1,), in_specs=whole, out_specs=whole,
                 out_shape=[jax.ShapeDtypeStruct(s, F32) for s in shapes], args=parts,
                 scratch=[pltpu.VMEM((8, rows, SMALL_ROW), F32), pltpu.SemaphoreType.DMA((14,)),
                          pltpu.SemaphoreType.DMA((14,))], carry=carry)


def _adamw_small(ws, gs, ms, vs, *, name):
    n = len(ws)
    c1 = 1.0 - ADAM_B1 ** ADAM_STEP
    c2 = 1.0 - ADAM_B2 ** ADAM_STEP

    def body(*refs):
        for k in range(n):
            w_ref, g_ref, m_ref, v_ref = (refs[t * n + k] for t in range(4))
            go_ref, d_ref, mo_ref, vo_ref = (refs[(4 + t) * n + k] for t in range(4))
            gv = g_ref[...]
            go_ref[...] = gv
            mn = ADAM_B1 * m_ref[...] + (1.0 - ADAM_B1) * gv
            vn = ADAM_B2 * v_ref[...] + (1.0 - ADAM_B2) * (gv * gv)
            d_ref[...] = -ADAM_LR * ((mn / c1) / (jnp.sqrt(vn / c2) + ADAM_EPS) + ADAM_WD * w_ref[...])
            mo_ref[...] = mn
            vo_ref[...] = vn

    whole = [_whole_spec(a.shape) for a in ws]
    shapes = tuple(jax.ShapeDtypeStruct(a.shape, F32) for a in ws)
    res = pl.pallas_call(
        body, name=name, grid=(1,), in_specs=whole * 4, out_specs=tuple(whole * 4), out_shape=shapes * 4,
    )(*ws, *gs, *ms, *vs)
    return res[0:n], res[n:2 * n], res[2 * n:3 * n], res[3 * n:4 * n]


def _adamw(ws, gs, ms, vs, *, name):
    n, steps = len(ws), 8
    c1 = 1.0 - ADAM_B1 ** ADAM_STEP
    c2 = 1.0 - ADAM_B2 ** ADAM_STEP

    def body(*refs):
        for k in range(n):
            w_ref, g_ref, m_ref, v_ref = (refs[t * n + k] for t in range(4))
            go_ref, d_ref, mo_ref, vo_ref = (refs[(4 + t) * n + k] for t in range(4))
            gv = g_ref[...]
            go_ref[...] = gv
            mn = ADAM_B1 * m_ref[...] + (1.0 - ADAM_B1) * gv
            vn = ADAM_B2 * v_ref[...] + (1.0 - ADAM_B2) * (gv * gv)
            d_ref[...] = -ADAM_LR * ((mn / c1) / (jnp.sqrt(vn / c2) + ADAM_EPS) + ADAM_WD * w_ref[...])
            mo_ref[...] = mn
            vo_ref[...] = vn

    blocks = [(a.shape[0] // steps, a.shape[1]) for a in ws]
    specs = [pl.BlockSpec(blk, lambda i: (i, 0)) for blk in blocks]
    shapes = tuple(jax.ShapeDtypeStruct(a.shape, F32) for a in ws)
    res = pl.pallas_call(
        body, name=name, grid=(steps,), in_specs=specs * 4, out_specs=tuple(specs * 4), out_shape=shapes * 4,
        compiler_params=pltpu.CompilerParams(dimension_semantics=("parallel",),
                                             vmem_limit_bytes=_vmem_limit(*[8 * _nbytes(b, F32) for b in blocks])),
    )(*ws, *gs, *ms, *vs)
    return res[0:n], res[n:2 * n], res[2 * n:3 * n], res[3 * n:4 * n]


def _local_step(x, p, target, small, sched):
    T = x.shape[0]
    W = MIX_W
    rel_bias = small["rel_bias"]
    b_in_a, b_in_b = small["b_in"][:, 0:ZA_W], small["b_in"][:, ZA_W:]

    idx_np = [_t5_index(A_WINDOW - 1, 1)] + [_t5_index(window // dil, dil) for window, dil in B_PATTERNS]
    idx_all = jnp.asarray(np.stack(idx_np))
    n1, bias_all = sched.run(_prologue, x, small["ffn1_pre_g"], rel_bias, idx_all, name="prologue")
    w_gu1 = sched.weight("ffn1_w_gu")
    *gu1, a1 = sched.run(_ffn_up, n1, w_gu1, bm=512, name="ffn1_gu")
    w_d1 = sched.weight("ffn1_w_down")
    f1, h1, n2 = sched.run(_mm_resid_norm, a1, w_d1, x, small["ffn1_post_g"], 0.5, small["attn_pre_g"], bm=512,
                           name="ffn1_down")
    win_a, win_b = sched.weight("w_in")
    za = _mm_nt(n2, win_a, bm=1024, bn=ZA_W, out_dtype=BF16, name="in_proj_a", add=b_in_a)
    dils = tuple(dil for _, dil in B_PATTERNS)
    zb_by_dil = _mm_nt_classes(n2, win_b, b_in_b, dils, bm=512, name="in_proj_b")

    a_args = dict(r=1, q_col=0, k_col=W // LANES, v_col=W // LANES + 1, stride=0, pattern=0, kv_shared=True)
    sink_row = jnp.repeat(small["sinks"], HEAD_DIM, axis=1)
    out_a, out_a_bf, lse_a = sched.run(_band_fwd, za, za, za, bias_all, name="band_a_fwd", sink=sink_row, **a_args)

    no_sink = jnp.full((1, W), NEG, F32)
    b_ctx, branches = [], []
    for t, dil in enumerate(dils):
        zb_r = zb_by_dil[t]
        args = dict(r=dil, q_col=0, k_col=1, v_col=2, stride=ZB_W // W, pattern=1 + t)
        branches.append(sched.run(_band_fwd, zb_r, zb_r, zb_r, bias_all, name=f"band_b{t}_fwd", **args))
        b_ctx.append((jnp.asarray(idx_np[1 + t]), zb_r, args))
    out_b, out_b_bf, lse_b = _merge(branches, dils, no_sink, name="merge_b")

    mix = jnp.concatenate([out_a_bf, out_b_bf], axis=1)
    w_out = sched.weight("w_out")
    att, h2, n3 = _mm_resid_norm(mix, w_out, h1, small["attn_post_g"], 1.0, small["ffn2_pre_g"], bm=512,
                                 name="out_proj", bias=small["b_out"])
    w_gu2, w_d2 = sched.weight("ffn2_w_gu"), sched.weight("ffn2_w_down")
    *gu2, a2 = _ffn_up(n3, w_gu2, bm=512, name="ffn2_gu")
    f2, h3, n4 = _mm_resid_norm(a2, w_d2, h2, small["ffn2_post_g"], 0.5, small["ple_pre_g"], bm=512,
                                name="ffn2_down")
    w_gate = sched.weight("w_ple_gate")
    p_bf = p.astype(BF16)
    dh4, de, dgp, loss, d_ple_post = _ple_head(n4, w_gate, p_bf, sched.weight("w_ple_proj"), h3, small["ple_post_g"],
                                               target, bm=512, name="ple_head")

    gs = {"ple_post_g": d_ple_post}
    sched.grad("w_ple_proj", _mm_tn(p_bf, de, bm=256, bn=1024, out_dtype=BF16, name="d_w_ple"))
    sched.grad("w_ple_gate", _mm_tn(n4, dgp, bm=512, bn=1024, out_dtype=BF16, name="d_w_gate"))
    dh3, df2, gs["ple_pre_g"], gs["ffn2_post_g"], _ = sched.run(
        _mm_nt_norms_bwd, dgp, w_gate, dh4, h3, small["ple_pre_g"], f2, small["ffn2_post_g"], 0.5, bm=512, name="d_n4")

    def ffn_bwd(df, a, gu, n, w_down, w_gu, tag, *norm_args):
        dgu = sched.run(_ffn_down_bwd, df, w_down, gu[0], gu[1], bm=512, name=f"ffn{tag}_d_a")
        sched.grad(f"ffn{tag}_w_gu", _mm_tn(n, dgu, bm=512, bn=1408, out_dtype=BF16, name=f"ffn{tag}_d_w_gu",
                                            n_stack=N_CHIPS))
        sched.grad(f"ffn{tag}_w_down", sched.run(_mm_tn, a, df, bm=256, bn=1024, out_dtype=BF16,
                                                name=f"ffn{tag}_d_w_down"))
        return sched.run(_mm_nt_norms_bwd, dgu, w_gu, *norm_args, bm=512, name=f"ffn{tag}_d_n")

    dh2, datt, gs["ffn2_pre_g"], gs["attn_post_g"], gs["b_out"] = ffn_bwd(
        df2, a2, gu2, n3, w_d2, w_gu2, "2", dh3, h2, small["ffn2_pre_g"], att, small["attn_post_g"], 1.0)
    sched.grad("w_out", _mm_tn(mix, datt, bm=512, bn=1024, out_dtype=BF16, name="d_w_out"))
    dmix = sched.run(_mm_nt, datt, w_out, bm=1024, bn=1024, out_dtype=F32, name="d_mix")

    do_a, stat_a, dsink, do_b, stat_b = _attn_delta(dmix, (out_a, out_b), (lse_a, lse_b), sink_row, dils,
                                                    name="attn_delta")
    gs["sinks"] = dsink[:, ::HEAD_DIM]
    dq_a, dk_a, dv_a, dbias_a = sched.run(_band_bwd, za, za, za, do_a, stat_a, bias_all, name="band_a_bwd", **a_args)
    d_rel_a = _bias_grad(dbias_a, jnp.asarray(idx_np[0]), name="d_rel_a")
    b_grads = []
    d_rel_b = None
    for t, (idx, zb_r, args) in enumerate(b_ctx):
        dq, dk, dv, dbias = sched.run(_band_bwd, zb_r, zb_r, zb_r, do_b[t], stat_b[t], bias_all,
                                      name=f"band_b{t}_bwd", **args)
        b_grads.append((dq, dk, dv))
        d_rel = _bias_grad(dbias, idx, name=f"d_rel_b{t}")
        d_rel_b = d_rel if d_rel_b is None else d_rel_b + d_rel
    gs["rel_bias"] = jnp.concatenate([d_rel_a[:, 0:2 * N_PAIRS], d_rel_b[:, 0:2 * N_PAIRS]], axis=1)
    dza, dzb, dsum_a, dsum_b = _dz_assemble(dq_a, dk_a, dv_a, b_grads, dils, name="d_z")
    gs["b_in"] = jnp.concatenate([dsum_a, dsum_b], axis=1)
    sched.grad("w_in", (_mm_tn(dza, n2, bm=ZA_W, bn=1024, out_dtype=BF16, name="d_w_in_a"),
                        _mm_tn(dzb, n2, bm=ZB_W // 2, bn=1024, out_dtype=BF16, name="d_w_in_b")))
    dn2_a = _mm_nn(dza, win_a, bm=1024, bn=1024, out_dtype=F32, name="d_n2_a")
    dh1, df1, gs["attn_pre_g"], gs["ffn1_post_g"], _ = sched.run(
        _mm_nt_norms_bwd, dzb, win_b, dh2, h1, small["attn_pre_g"], f1, small["ffn1_post_g"], 0.5, bm=512,
        name="d_n2_b", add=dn2_a, b_is_kn=True)
    grad_x, gs["ffn1_pre_g"] = ffn_bwd(df1, a1, gu1, n1, w_d1, w_gu1, "1", dh1, x, small["ffn1_pre_g"], None, None, 0.0)
    return loss, grad_x, gs


def _to_compute(name, full):
    st = full.reshape(N_CHIPS, full.shape[0] // N_CHIPS, full.shape[1])
    if name in ("ffn1_w_gu", "ffn2_w_gu"):
        return st
    if name == "w_ple_proj":
        return jnp.transpose(st, (1, 0, 2)).reshape(st.shape[1], -1)
    if name == "w_in":
        return full[0:ZA_W], full[ZA_W:]
    return full


def _to_comm(name, g):
    if name == "w_in":
        g = jnp.concatenate(g, axis=0)
    if name == "w_ple_proj":
        g = jnp.transpose(g.reshape(g.shape[0], N_CHIPS, -1), (1, 0, 2))
    rs = g.shape[1] if g.ndim == 3 else g.shape[0] // N_CHIPS
    return g.reshape(N_CHIPS, 2, rs // 2, g.shape[-1])


class _Schedule:
    GATHER = {
        "prologue": [("ffn1_w_gu", (0, 1), "both")],
        "ffn1_gu": [("ffn1_w_down", (0, 1), "both"), ("w_in", (0, 1), "both")],
        "band_a_fwd": [("ffn2_w_gu", (0, 2), "chips"), ("w_out", (0, 1), "chips")],
        "band_b0_fwd": [("ffn2_w_gu", (0, 2), "sibling"), ("w_out", (0, 1), "sibling"),
                        ("ffn2_w_gu", (1, 2), "chips"), ("w_ple_gate", (0, 1), "chips"), ("w_ple_proj", (0, 1), "chips")],
        "band_b1_fwd": [("ffn2_w_gu", (1, 2), "sibling"), ("w_ple_gate", (0, 1), "sibling"),
                        ("w_ple_proj", (0, 1), "sibling"), ("ffn2_w_down", (0, 1), "both")],
    }
    SWAP = {"ffn2_d_w_down": ["w_ple_proj", "w_ple_gate", "ffn2_w_gu"], "ffn2_d_n": ["ffn2_w_down"],
            "band_a_bwd": ["w_out"], "d_n2_b": ["w_in"], "ffn1_d_w_down": ["ffn1_w_gu"], "ffn1_d_n": ["ffn1_w_down"]}
    SCATTER = {"ffn2_d_n": ["ffn2_w_gu"], "band_a_bwd": ["w_ple_proj", "w_ple_gate", "ffn2_w_down"],
               "d_n2_b": ["w_out"], "ffn1_d_w_down": ["w_in"], "ffn1_d_n": ["ffn1_w_gu"]}
    HALF_SWAP = {"ffn1_d_n": ["w_ple_proj", "w_ple_gate", "ffn2_w_gu", "ffn2_w_down", "w_out", "w_in"]}

    def __init__(self, placed, c_idx, kc_idx):
        self.full = dict(placed)
        self.missing = {n: 1.0 for n in placed}
        self.c_idx, self.kc_idx = c_idx, kc_idx
        self.grads, self.swapped, self.pair_sums, self.scattered, self.summed = {}, {}, {}, {}, {}

    def _gather(self, entries):
        if not entries:
            return [], []
        names = list(dict.fromkeys(n for n, _, _ in entries))
        carry = _gather_carry([self.full[n] for n in names], [(names.index(n), pt, hops) for n, pt, hops in entries])

        def done():
            self.full.update(zip(names, carry.results))
            for n, part, hops in entries:
                if hops != "chips":
                    self.missing[n] -= 1.0 / part[1]
        return [carry], [done]

    def weight(self, name):
        assert abs(self.missing[name]) < 1e-9, (name, self.missing[name])
        return _to_compute(name, self.full[name])

    def grad(self, name, g):
        self.grads[name] = _to_comm(name, g)

    def _swap(self, names):
        carry = _pair_swap_carry([self.grads[n] for n in names])

        def done():
            self.swapped.update(zip(names, carry.results))
        return carry, done

    def _scatter(self, names):
        for n in names:
            self.pair_sums[n] = _pair_sum(self.grads[n], self.swapped[n], self.c_idx, name=f"grad_pair_sum_{n}")
        carry = _chip_scatter_carry([self.pair_sums[n] for n in names])

        def done():
            self.scattered.update(zip(names, carry.results))
        return carry, done

    def _half_swap(self, names):
        halves = _chip_sum([self.pair_sums[n] for n in names], [self.scattered[n] for n in names], self.kc_idx,
                           name=f"grad_chip_sum_{names[0]}")
        carry = _half_swap_carry(halves)

        def done():
            self.summed.update(zip(names, carry.results))
        return carry, done

    def run(self, fn, *args, name, **kw):
        carries, after = self._gather(self.GATHER.get(name, []))
        for names, make in ((self.SWAP.get(name), self._swap), (self.SCATTER.get(name), self._scatter),
                            (self.HALF_SWAP.get(name), self._half_swap)):
            if names:
                carry, done = make(names)
                carries.append(carry)
                after.append(done)
        out = fn(*args, name=name, carry=_join(carries), **kw)
        for done in after:
            done()
        return out

    def finish(self, last_carrier):
        left = [n for n in BIG if n not in self.scattered]
        to_swap = [n for n in left if n not in self.swapped]
        if to_swap:
            carry, done = self._swap(to_swap)
            _comm_call(carry, name="grad_pair_swap")
            done()
        carry, done = self._scatter(left) if left else (None, lambda: None)
        carried = last_carrier(carry=carry)
        done()
        carry, done = self._half_swap([n for n in BIG if n not in self.summed])
        _comm_call(carry, name="grad_half_swap")
        done()
        return self.summed, carried


def kernel(x, p, rel_bias, ffn1_pre_g, ffn1_w_gu, ffn1_w_down, ffn1_post_g, attn_pre_g, w_in, b_in, sinks, w_out, b_out, attn_post_g, ffn2_pre_g, ffn2_w_gu, ffn2_w_down, ffn2_post_g, ple_pre_g, w_ple_gate, w_ple_proj, ple_post_g, loss_target, m_rel_bias, m_ffn1_pre_g, m_ffn1_w_gu, m_ffn1_w_down, m_ffn1_post_g, m_attn_pre_g, m_w_in, m_b_in, m_sinks, m_w_out, m_b_out, m_attn_post_g, m_ffn2_pre_g, m_ffn2_w_gu, m_ffn2_w_down, m_ffn2_post_g, m_ple_pre_g, m_w_ple_gate, m_w_ple_proj, m_ple_post_g, v_rel_bias, v_ffn1_pre_g, v_ffn1_w_gu, v_ffn1_w_down, v_ffn1_post_g, v_attn_pre_g, v_w_in, v_b_in, v_sinks, v_w_out, v_b_out, v_attn_post_g, v_ffn2_pre_g, v_ffn2_w_gu, v_ffn2_w_down, v_ffn2_post_g, v_ple_pre_g, v_w_ple_gate, v_w_ple_proj, v_ple_post_g):
    given = dict(locals())
    w = {n: given[n] for n in WEIGHTS}
    m = {n: given["m_" + n] for n in WEIGHTS}
    v = {n: given["v_" + n] for n in WEIGHTS}
    c_pos = lax.axis_index("c").astype(jnp.int32)
    k_pos = (2 * lax.axis_index("x") + lax.axis_index("y")).astype(jnp.int32)
    c_idx, k_idx, kc_idx = c_pos.reshape(1), k_pos.reshape(1), jnp.stack([k_pos, c_pos])

    def shard(a, n):
        return jnp.transpose(a[0]) if n == "w_in" else a[0]

    def unshard(a, n):
        return (jnp.transpose(a) if n == "w_in" else a)[None]

    placed = _cast_place([shard(w[n], n) for n in BIG], k_idx, name="place_shards")
    sched = _Schedule(dict(zip(BIG, placed)), c_idx, kc_idx)
    small = {n: (w[n] if n == "rel_bias" else w[n].reshape(1, -1)) for n in SMALL}

    loss_part, grad_x, gs = _local_step(x[0], p[0, 0], loss_target[0], small, sched)

    grads_big, (*small_grads, loss_row) = sched.finish(
        partial(_small_all_reduce, [gs[n] for n in SMALL] + [loss_part], name="small_all_reduce"))
    loss = loss_row[0, 0]

    grads, delta, new_m, new_v = {}, {}, {}, {}
    res = _adamw([shard(w[n], n) for n in BIG], [grads_big[n] for n in BIG], [shard(m[n], n) for n in BIG],
                 [shard(v[n], n) for n in BIG], name="adamw_big")
    for n, gg, dd, mm, vv in zip(BIG, *res):
        grads[n], delta[n], new_m[n], new_v[n] = (unshard(a, n) for a in (gg, dd, mm, vv))
    res = _adamw_small([w[n] for n in SMALL], small_grads, [m[n] for n in SMALL], [v[n] for n in SMALL],
                       name="adamw_small")
    for name, gg, dd, mm, vv in zip(SMALL, *res):
        grads[name], delta[name], new_m[name], new_v[name] = gg, dd, mm, vv

    return (loss, grad_x[None], *[grads[n] for n in WEIGHTS], *[delta[n] for n in WEIGHTS],
            *[new_m[n] for n in WEIGHTS], *[new_v[n] for n in WEIGHTS])
```

```python
import math
from functools import partial

import jax
import jax.numpy as jnp
import numpy as np
from jax import lax
from jax.experimental import pallas as pl
from jax.experimental.pallas import tpu as pltpu

F32 = jnp.float32
BF16 = jnp.bfloat16
MESH = pl.DeviceIdType.MESH

EPS = 1e-6
NEG = -1e30
LANES = 128
SUBLANES = 8
HEAD_DIM = 64
QBLK = 128
N_PAIRS = 4
MIX_W = N_PAIRS * LANES
A_WINDOW = 128
B_PATTERNS = ((128, 1), (512, 4), (2048, 16))
NUM_BUCKETS = 32
MAX_DISTANCE = 2048
ZA_W = 768
ZB_W = 1536
N_CHIPS = 4
VMEM_BYTES_V7X = 64 * 2**20
VMEM_CLAIM = VMEM_BYTES_V7X - (6 << 20)
RING_SLOTS = 3

ADAM_LR, ADAM_B1, ADAM_B2, ADAM_EPS, ADAM_WD, ADAM_STEP = 0.001, 0.9, 0.999, 1e-08, 0.01, 10

BIG = ("ffn1_w_gu", "ffn1_w_down", "w_in", "w_out", "ffn2_w_gu", "ffn2_w_down", "w_ple_gate", "w_ple_proj")
SMALL = ("rel_bias", "ffn1_pre_g", "ffn1_post_g", "attn_pre_g", "b_in", "sinks", "b_out", "attn_post_g",
         "ffn2_pre_g", "ffn2_post_g", "ple_pre_g", "ple_post_g")
WEIGHTS = ("rel_bias", "ffn1_pre_g", "ffn1_w_gu", "ffn1_w_down", "ffn1_post_g", "attn_pre_g", "w_in", "b_in",
           "sinks", "w_out", "b_out", "attn_post_g", "ffn2_pre_g", "ffn2_w_gu", "ffn2_w_down", "ffn2_post_g",
           "ple_pre_g", "w_ple_gate", "w_ple_proj", "ple_post_g")


def _vmem_limit(*block_bytes):
    del block_bytes
    return VMEM_CLAIM


def _nbytes(shape, dtype):
    return int(np.prod(shape)) * jnp.dtype(dtype).itemsize


MXU_WIDTH_V7X = 256


def _col_chunks(n):
    return [slice(c, min(c + MXU_WIDTH_V7X, n)) for c in range(0, n, MXU_WIDTH_V7X)]


def _row_halves(rows):
    return [slice(0, rows // 2), slice(rows // 2, rows)]


def _row_block(rows, cap, mult):
    for cand in range(min(rows, cap), 0, -1):
        if rows % cand == 0 and cand % mult == 0:
            return cand
    raise ValueError((rows, cap, mult))


class _Carry:
    def __init__(self, operands, out_shapes, aliases, sems, start, finish, before_last=None):
        self.operands, self.out_shapes, self.aliases, self.sems = list(operands), list(out_shapes), dict(aliases), list(sems)
        self.start, self.finish = start, finish
        self.before_last = before_last or (lambda ins, outs, sems: None)
        self.results = None


def _join(carries):
    carries = [c for c in carries if c is not None]
    if not carries:
        return None
    if len(carries) == 1:
        return carries[0]
    offs, pos = [], [0, 0, 0]
    for c in carries:
        offs.append(tuple(pos))
        pos = [pos[0] + len(c.operands), pos[1] + len(c.out_shapes), pos[2] + len(c.sems)]
    aliases = {}
    for c, (oi, oo, _) in zip(carries, offs):
        aliases.update({oi + i: oo + o for i, o in c.aliases.items()})

    def run(which):
        def fn(ins, outs, sems):
            for c, (oi, oo, os_) in zip(carries, offs):
                getattr(c, which)(ins[oi:oi + len(c.operands)], outs[oo:oo + len(c.out_shapes)],
                                  sems[os_:os_ + len(c.sems)])
        return fn

    joined = _Carry([a for c in carries for a in c.operands], [s for c in carries for s in c.out_shapes], aliases,
                    [s for c in carries for s in c.sems], run("start"), run("finish"), run("before_last"))
    joined.parts = (carries, offs)
    return joined


def _set_results(carry, outs):
    carry.results = list(outs)
    if hasattr(carry, "parts"):
        for c, (_, oo, _) in zip(*carry.parts):
            _set_results(c, outs[oo:oo + len(c.out_shapes)])


ANY = pl.BlockSpec(memory_space=pl.ANY)


def _call(body, *, name, grid, in_specs, out_specs, out_shape, args, scratch=(), vmem_limit=None, carry=None,
          semantics=None, before_carry=None):
    assert before_carry is None or carry is not None
    n_ci, n_co, n_cs = len(args), len(out_shape), len(scratch)
    semantics = semantics or ("parallel",) * len(grid)
    vmem_limit = vmem_limit or VMEM_CLAIM
    if carry is None:
        res = pl.pallas_call(
            body, name=name, grid=grid, in_specs=list(in_specs), out_specs=tuple(out_specs), out_shape=tuple(out_shape),
            scratch_shapes=list(scratch),
            compiler_params=pltpu.CompilerParams(dimension_semantics=semantics, vmem_limit_bytes=vmem_limit),
        )(*args)
        return list(res)
    n_pi, n_po = len(carry.operands), len(carry.out_shapes)

    def full_body(*refs):
        ci, pi = refs[:n_ci], refs[n_ci:n_ci + n_pi]
        o0 = n_ci + n_pi
        co, po = refs[o0:o0 + n_co], refs[o0 + n_co:o0 + n_co + n_po]
        s0 = o0 + n_co + n_po
        cs, ps = refs[s0:s0 + n_cs], refs[s0 + n_cs:]
        ids = [pl.program_id(ax) for ax in range(len(grid))]
        first, last = ids[0] == 0, ids[0] == grid[0] - 1
        for ax in range(1, len(grid)):
            first, last = first & (ids[ax] == 0), last & (ids[ax] == grid[ax] - 1)

        @pl.when(first)
        def _():
            if before_carry is not None:
                before_carry(*ci, *co, *cs)
            carry.start(pi, po, ps)

        several_steps = any(g > 1 for g in grid)
        if several_steps:
            @pl.when(last)
            def _():
                carry.before_last(pi, po, ps)

        body(*ci, *co, *cs)

        @pl.when(last)
        def _():
            if not several_steps:
                carry.before_last(pi, po, ps)
            carry.finish(pi, po, ps)

    res = pl.pallas_call(
        full_body, name=name, grid=grid, in_specs=list(in_specs) + [ANY] * n_pi,
        out_specs=tuple(out_specs) + tuple([ANY] * n_po), out_shape=tuple(out_shape) + tuple(carry.out_shapes),
        scratch_shapes=list(scratch) + carry.sems,
        input_output_aliases={n_ci + i: n_co + o for i, o in carry.aliases.items()},
        compiler_params=pltpu.CompilerParams(dimension_semantics=("arbitrary",) * len(grid),
                                             vmem_limit_bytes=vmem_limit),
    )(*args, *carry.operands)
    _set_results(carry, res[n_co:])
    return list(res[:n_co])


def _comm_call(carry, *, name):
    n_pi, n_po = len(carry.operands), len(carry.out_shapes)

    def body(*refs):
        pi, po, ps = refs[:n_pi], refs[n_pi:n_pi + n_po], refs[n_pi + n_po:]
        carry.start(pi, po, ps)
        carry.before_last(pi, po, ps)
        carry.finish(pi, po, ps)

    res = pl.pallas_call(
        body, name=name, in_specs=[ANY] * n_pi, out_specs=tuple([ANY] * n_po), out_shape=tuple(carry.out_shapes),
        scratch_shapes=carry.sems, input_output_aliases=dict(carry.aliases),
    )(*carry.operands)
    _set_results(carry, res)


def _mm_nn(a, b, *, bm, bn, out_dtype, name, bias=None, carry=None):
    M, K = a.shape
    stacked = b.ndim == 3
    N = b.shape[0] * b.shape[2] if stacked else b.shape[1]
    assert M % bm == 0 and N % bn == 0 and (not stacked or bn == b.shape[2])

    def body(*refs):
        a_ref, b_ref = refs[0], refs[1]
        o_ref = refs[-1]
        acc = jnp.dot(a_ref[...], b_ref[...], preferred_element_type=F32)
        if bias is not None:
            acc = acc + refs[2][...]
        o_ref[...] = acc.astype(o_ref.dtype)

    if stacked:
        b_spec = pl.BlockSpec((None, K, bn), lambda i, j: (j, 0, 0))
    else:
        b_spec = pl.BlockSpec((K, bn), lambda i, j: (0, j))
    in_specs = [pl.BlockSpec((bm, K), lambda i, j: (i, 0)), b_spec]
    args = [a, b]
    if bias is not None:
        in_specs.append(pl.BlockSpec((1, bn), lambda i, j: (0, j)))
        args.append(bias)
    limit = _vmem_limit(_nbytes((bm, K), a.dtype), _nbytes((K, bn), b.dtype), _nbytes((bm, bn), F32))
    return _call(body, name=name, grid=(M // bm, N // bn), in_specs=in_specs,
                 out_specs=[pl.BlockSpec((bm, bn), lambda i, j: (i, j))],
                 out_shape=[jax.ShapeDtypeStruct((M, N), out_dtype)], args=args, vmem_limit=limit, carry=carry)[0]


def _mm_nt(a, b, *, bm, bn, out_dtype, name, add=None, carry=None):
    M, K = a.shape
    stacked = b.ndim == 3
    N = b.shape[1] if stacked else b.shape[0]
    n_sh = b.shape[0] if stacked else 1
    ks = b.shape[2] if stacked else K
    assert M % bm == 0 and N % bn == 0 and n_sh * ks == K
    dn = (((1,), (1,)), ((), ()))

    def body(*refs):
        a_ref, b_ref = refs[0], refs[1]
        o_ref = refs[-1]
        if stacked:
            acc = lax.dot_general(a_ref[:, 0:ks], b_ref[0], dn, preferred_element_type=F32)
            for s in range(1, n_sh):
                acc = acc + lax.dot_general(a_ref[:, s * ks:(s + 1) * ks], b_ref[s], dn, preferred_element_type=F32)
        else:
            acc = lax.dot_general(a_ref[...], b_ref[...], dn, preferred_element_type=F32)
        if add is not None:
            acc = acc + refs[2][...]
        o_ref[...] = acc.astype(o_ref.dtype)

    if stacked:
        b_spec = pl.BlockSpec((n_sh, bn, ks), lambda i, j: (0, j, 0))
    else:
        b_spec = pl.BlockSpec((bn, K), lambda i, j: (j, 0))
    in_specs = [pl.BlockSpec((bm, K), lambda i, j: (i, 0)), b_spec]
    args = [a, b]
    if add is not None:
        rows = bm if add.shape[0] == M else 1
        in_specs.append(pl.BlockSpec((rows, bn), lambda i, j: (i if rows == bm else 0, j)))
        args.append(add)
    limit = _vmem_limit(_nbytes((bm, K), a.dtype), _nbytes((bn, K), b.dtype), _nbytes((bm, bn), F32))
    return _call(body, name=name, grid=(M // bm, N // bn), in_specs=in_specs,
                 out_specs=[pl.BlockSpec((bm, bn), lambda i, j: (i, j))],
                 out_shape=[jax.ShapeDtypeStruct((M, N), out_dtype)], args=args, vmem_limit=limit, carry=carry)[0]


def _mm_tn(a, b, *, bm, bn, out_dtype, name, n_stack=None, carry=None):
    T, M = a.shape
    N = b.shape[1]
    assert M % bm == 0 and N % bn == 0 and (n_stack is None or bn * n_stack == N)
    dn = (((0,), (0,)), ((), ()))

    def body(a_ref, b_ref, o_ref):
        acc = lax.dot_general(a_ref[...], b_ref[...], dn, preferred_element_type=F32)
        o_ref[...] = acc.astype(o_ref.dtype)

    if n_stack is None:
        out_spec = pl.BlockSpec((bm, bn), lambda i, j: (i, j))
        out_shape = jax.ShapeDtypeStruct((M, N), out_dtype)
    else:
        out_spec = pl.BlockSpec((None, bm, bn), lambda i, j: (j, i, 0))
        out_shape = jax.ShapeDtypeStruct((n_stack, M, bn), out_dtype)
    limit = _vmem_limit(_nbytes((T, bm), a.dtype), _nbytes((T, bn), b.dtype), _nbytes((bm, bn), F32))
    return _call(body, name=name, grid=(M // bm, N // bn),
                 in_specs=[pl.BlockSpec((T, bm), lambda i, j: (0, i)), pl.BlockSpec((T, bn), lambda i, j: (0, j))],
                 out_specs=[out_spec], out_shape=[out_shape], args=[a, b], vmem_limit=limit, carry=carry)[0]


ROW_BLK = 256


def _rstd(x):
    return lax.rsqrt(jnp.mean(x * x, axis=-1, keepdims=True) + EPS)


def _norm_bwd(xhat, rstd, dxhat):
    return rstd * (dxhat - xhat * jnp.mean(dxhat * xhat, axis=-1, keepdims=True))


def _row_spec(cols):
    return pl.BlockSpec((ROW_BLK, cols), lambda i: (i, 0))


def _vec_spec(cols):
    return pl.BlockSpec((1, cols), lambda i: (0, 0))


def _prologue(x, g, rel_bias, idx_all, *, name, carry=None):
    T, D = x.shape
    n_pat = idx_all.shape[0]
    heads = 2 * N_PAIRS
    steps = n_pat * heads
    rows = T // steps

    def body(rb_ref, x_ref, g_ref, idx_ref, n_ref, bias_ref):
        s = pl.program_id(0)
        head = jnp.where(s < heads, s, heads + s % heads)
        xv = x_ref[...]
        n_ref[...] = (xv * _rstd(xv) * g_ref[...]).astype(n_ref.dtype)
        idxv = idx_ref[...]
        acc = jnp.where(idxv < 0, NEG, 0.0).astype(F32)
        for b in range(NUM_BUCKETS):
            acc = acc + jnp.where(idxv == b, rb_ref[b, head], 0.0)
        bias_ref[0] = acc
        kap = lax.broadcasted_iota(jnp.int32, (1, 2 * QBLK), 1)
        bias_ref[1] = jnp.where(kap < QBLK, NEG, acc)

    return _call(
        body, name=name, grid=(steps,),
        in_specs=[pl.BlockSpec(memory_space=pltpu.SMEM), pl.BlockSpec((rows, D), lambda s: (s, 0)),
                  pl.BlockSpec((1, D), lambda s: (0, 0)),
                  pl.BlockSpec((None, QBLK, 2 * QBLK), lambda s: (s // heads, 0, 0))],
        out_specs=[pl.BlockSpec((rows, D), lambda s: (s, 0)),
                   pl.BlockSpec((None, 2, None, QBLK, 2 * QBLK), lambda s: (s // heads, 0, s % heads, 0, 0))],
        out_shape=[jax.ShapeDtypeStruct((T, D), BF16),
                   jax.ShapeDtypeStruct((n_pat, 2, heads, QBLK, 2 * QBLK), F32)],
        args=[rel_bias, x, g, idx_all], carry=carry)


def _mm_resid_norm(a, b, h, g_post, coef, g_next, *, bm, name, bias=None, carry=None):
    M, K = a.shape
    N = b.shape[1]

    def body(*refs):
        a_ref, b_ref, h_ref, gp_ref, gn_ref = refs[0:5]
        f_ref, hn_ref, n_ref = refs[-3:]
        for rows in _row_halves(bm):
            f = jnp.dot(a_ref[rows, :], b_ref[...], preferred_element_type=F32)
            if bias is not None:
                f = f + refs[5][...]
            f_ref[rows, :] = f
            hn = h_ref[rows, :] + coef * (f * _rstd(f) * gp_ref[...])
            hn_ref[rows, :] = hn
            n_ref[rows, :] = (hn * _rstd(hn) * gn_ref[...]).astype(n_ref.dtype)

    row = lambda w: pl.BlockSpec((bm, w), lambda i: (i, 0))
    vec = pl.BlockSpec((1, N), lambda i: (0, 0))
    in_specs = [row(K), pl.BlockSpec((K, N), lambda i: (0, 0), pipeline_mode=pl.Buffered(1)), row(N), vec, vec]
    args = [a, b, h, g_post, g_next]
    if bias is not None:
        in_specs.append(vec)
        args.append(bias)
    limit = _vmem_limit(_nbytes((bm, K), a.dtype), _nbytes((K, N), b.dtype) // 2, 4 * _nbytes((bm, N), F32))
    return _call(body, name=name, grid=(M // bm,), in_specs=in_specs, out_specs=[row(N), row(N), row(N)],
                 out_shape=[jax.ShapeDtypeStruct((M, N), F32), jax.ShapeDtypeStruct((M, N), F32),
                            jax.ShapeDtypeStruct((M, N), BF16)], args=args, vmem_limit=limit, carry=carry)


def _mm_nt_norms_bwd(a, b, dh_in, h, g_pre, f, g_post, coef, *, bm, name, add=None, b_is_kn=False, carry=None):
    M, K = a.shape
    stacked = b.ndim == 3
    N = b.shape[1] if (stacked or b_is_kn) else b.shape[0]
    n_sh = b.shape[0] if stacked else 1
    ks = b.shape[2] if stacked else K
    assert n_sh * ks == K
    dn_dims = (((1,), (0,)), ((), ())) if b_is_kn else (((1,), (1,)), ((), ()))
    with_f = f is not None
    n_in = 5 + (2 if with_f else 0) + (1 if add is not None else 0)

    def body(*refs):
        a_ref, b_ref, dhi_ref, h_ref, gpre_ref = refs[0:5]
        outs = refs[n_in:]

        @pl.when(pl.program_id(0) == 0)
        def _():
            for acc in (outs[2:] if with_f else outs[1:]):
                acc[...] = jnp.zeros_like(acc)

        for rows in _row_halves(bm):
            if stacked:
                dn = lax.dot_general(a_ref[rows, 0:ks], b_ref[0], dn_dims, preferred_element_type=F32)
                for s in range(1, n_sh):
                    dn = dn + lax.dot_general(a_ref[rows, s * ks:(s + 1) * ks], b_ref[s], dn_dims,
                                              preferred_element_type=F32)
            else:
                dn = lax.dot_general(a_ref[rows, :], b_ref[...], dn_dims, preferred_element_type=F32)
            if add is not None:
                dn = dn + refs[n_in - 1][rows, :]
            hv = h_ref[rows, :]
            r = _rstd(hv)
            hh = hv * r
            dh = dhi_ref[rows, :] + _norm_bwd(hh, r, dn * gpre_ref[...])
            outs[0][rows, :] = dh
            if not with_f:
                outs[1][...] += jnp.sum(dn * hh, axis=0, keepdims=True)
                continue
            f_ref, gpost_ref = refs[5], refs[6]
            df_ref, dgpre_ref, dgpost_ref, dsum_ref = outs[1:]
            dgpre_ref[...] += jnp.sum(dn * hh, axis=0, keepdims=True)
            fv = f_ref[rows, :]
            rf = _rstd(fv)
            fh = fv * rf
            dy = coef * dh
            dgpost_ref[...] += jnp.sum(dy * fh, axis=0, keepdims=True)
            df = _norm_bwd(fh, rf, dy * gpost_ref[...])
            dsum_ref[...] += jnp.sum(df, axis=0, keepdims=True)
            df_ref[rows, :] = df.astype(df_ref.dtype)

    row = lambda w: pl.BlockSpec((bm, w), lambda i: (i, 0))
    vec = pl.BlockSpec((1, N), lambda i: (0, 0))
    once = pl.Buffered(1)
    if stacked:
        b_spec = pl.BlockSpec((n_sh, N, ks), lambda i: (0, 0, 0), pipeline_mode=once)
    else:
        b_spec = pl.BlockSpec(b.shape, lambda i: (0, 0), pipeline_mode=once)
    in_specs = [row(K), b_spec, row(N), row(N), vec]
    args = [a, b, dh_in, h, g_pre]
    if with_f:
        in_specs += [row(N), vec]
        args += [f, g_post]
    if add is not None:
        in_specs.append(row(N))
        args.append(add)
    vec_shape = jax.ShapeDtypeStruct((1, N), F32)
    out_specs = [row(N)] + ([row(N), vec, vec, vec] if with_f else [vec])
    out_shape = [jax.ShapeDtypeStruct((M, N), F32)]
    out_shape += [jax.ShapeDtypeStruct((M, N), BF16), vec_shape, vec_shape, vec_shape] if with_f else [vec_shape]
    limit = _vmem_limit(_nbytes((bm, K), a.dtype), _nbytes((N, K), b.dtype) // 2, 6 * _nbytes((bm, N), F32))
    return _call(body, name=name, grid=(M // bm,), in_specs=in_specs, out_specs=out_specs, out_shape=out_shape,
                 args=args, vmem_limit=limit, semantics=("arbitrary",), carry=carry)


def _ffn_up(n, w_gu, *, bm, name, carry=None):
    M, K = n.shape
    n_sh, _, ns = w_gu.shape
    half = n_sh // 2

    def body(n_ref, wg_ref, wu_ref, g_ref, u_ref, a_ref):
        nv = n_ref[...]
        for cols in _col_chunks(ns):
            g = jnp.dot(nv, wg_ref[:, cols], preferred_element_type=F32)
            u = jnp.dot(nv, wu_ref[:, cols], preferred_element_type=F32)
            g_ref[:, cols] = g.astype(g_ref.dtype)
            u_ref[:, cols] = u.astype(u_ref.dtype)
            a_ref[:, cols] = (g * jax.nn.sigmoid(g) * u).astype(a_ref.dtype)

    out_spec = pl.BlockSpec((bm, ns), lambda i, j: (i, j))
    out = jax.ShapeDtypeStruct((M, half * ns), BF16)
    limit = _vmem_limit(_nbytes((bm, K), n.dtype), 2 * _nbytes((K, ns), w_gu.dtype), 3 * _nbytes((bm, ns), F32))
    return _call(body, name=name, grid=(M // bm, half),
                 in_specs=[pl.BlockSpec((bm, K), lambda i, j: (i, 0)),
                           pl.BlockSpec((None, K, ns), lambda i, j: (j, 0, 0)),
                           pl.BlockSpec((None, K, ns), lambda i, j: (j + half, 0, 0))],
                 out_specs=[out_spec, out_spec, out_spec], out_shape=[out, out, out], args=[n, w_gu, w_gu],
                 vmem_limit=limit, carry=carry)


def _ffn_down_bwd(df, w_down, g, u, *, bm, name, carry=None):
    M, D = df.shape
    F = w_down.shape[0]
    dn = (((1,), (1,)), ((), ()))

    steps = M // bm
    assert steps >= RING_SLOTS - 1

    def body(df_ref, w_ref, g_hbm, u_hbm, o_ref, g_buf, u_buf, sems):
        i = pl.program_id(0)

        def copies(step, slot):
            return [pltpu.make_async_copy(src.at[pl.ds(step * bm, bm)], buf.at[slot], sems.at[k, slot])
                    for k, (src, buf) in enumerate(((g_hbm, g_buf), (u_hbm, u_buf)))]

        @pl.when(i == 0)
        def _():
            for s in range(RING_SLOTS - 1):
                for k, cp in enumerate(copies(s, s)):
                    cp.start(priority=k)

        ahead = i + (RING_SLOTS - 1)

        @pl.when(ahead < steps)
        def _():
            for k, cp in enumerate(copies(ahead, ahead % RING_SLOTS)):
                cp.start(priority=k)

        slot = i % RING_SLOTS
        for cp in copies(i, slot):
            cp.wait()
        g_ref, u_ref = g_buf.at[slot], u_buf.at[slot]
        dfv = df_ref[...]
        for cols in _col_chunks(F):
            da = lax.dot_general(dfv, w_ref[cols, :], dn, preferred_element_type=F32)
            gv = g_ref[:, cols].astype(F32)
            s = jax.nn.sigmoid(gv)
            o_ref[:, cols] = (da * u_ref[:, cols].astype(F32) * (s * (1.0 + gv * (1.0 - s)))).astype(o_ref.dtype)
            o_ref[:, F + cols.start:F + cols.stop] = (da * (gv * s)).astype(o_ref.dtype)

    row = lambda w: pl.BlockSpec((bm, w), lambda i: (i, 0))
    limit = _vmem_limit(_nbytes((F, D), w_down.dtype) // 2, RING_SLOTS * _nbytes((bm, F), BF16), _nbytes((bm, 2 * F), BF16))
    return _call(body, name=name, grid=(steps,),
                 in_specs=[row(D), pl.BlockSpec((F, D), lambda i: (0, 0), pipeline_mode=pl.Buffered(1)), ANY, ANY],
                 out_specs=[row(2 * F)], out_shape=[jax.ShapeDtypeStruct((M, 2 * F), BF16)],
                 args=[df, w_down, g, u], vmem_limit=limit, carry=carry, semantics=("arbitrary",),
                 scratch=[pltpu.VMEM((RING_SLOTS, bm, F), g.dtype), pltpu.VMEM((RING_SLOTS, bm, F), u.dtype),
                          pltpu.SemaphoreType.DMA((2, RING_SLOTS))])[0]


def _ple_head(n4, w_gate, p, w_ple, h3, g_post, target, *, bm, name):
    T, D = h3.shape
    kp = p.shape[1]

    def body(n_ref, wg_ref, p_ref, wp_ref, h_ref, g_ref, t_ref, dh_ref, de_ref, dgp_ref, loss_ref, dg_ref):
        @pl.when(pl.program_id(0) == 0)
        def _():
            loss_ref[...] = jnp.zeros_like(loss_ref)
            dg_ref[...] = jnp.zeros_like(dg_ref)

        gpost = g_ref[...]
        for rows in _row_halves(bm):
            gate = jax.nn.sigmoid(jnp.dot(n_ref[rows, :], wg_ref[...], preferred_element_type=F32))
            ev = jnp.dot(p_ref[rows, :], wp_ref[...], preferred_element_type=F32)
            ge = gate * ev
            r = _rstd(ge)
            gh = ge * r
            diff = h_ref[rows, :] + gh * gpost - t_ref[rows, :]
            loss_ref[...] += jnp.sum(diff * diff) * (0.5 / D)
            dh = diff * (1.0 / D)
            dh_ref[rows, :] = dh
            dg_ref[...] += jnp.sum(dh * gh, axis=0, keepdims=True)
            dge = _norm_bwd(gh, r, dh * gpost)
            de_ref[rows, :] = (dge * gate).astype(de_ref.dtype)
            dgp_ref[rows, :] = (dge * ev * gate * (1.0 - gate)).astype(dgp_ref.dtype)

    row = lambda w: pl.BlockSpec((bm, w), lambda i: (i, 0))
    whole = lambda a: pl.BlockSpec(a.shape, lambda i: (0, 0), pipeline_mode=pl.Buffered(1))
    limit = _vmem_limit(_nbytes((bm, D), BF16), _nbytes(w_gate.shape, w_gate.dtype) // 2, 6 * _nbytes((bm, D), F32))
    return pl.pallas_call(
        body, name=name, grid=(T // bm,),
        in_specs=[row(D), whole(w_gate), row(kp), whole(w_ple), row(D), _vec_spec(D), row(D)],
        out_specs=(row(D), row(D), row(D), _vec_spec(LANES), _vec_spec(D)),
        out_shape=(jax.ShapeDtypeStruct((T, D), F32), jax.ShapeDtypeStruct((T, D), BF16),
                   jax.ShapeDtypeStruct((T, D), BF16), jax.ShapeDtypeStruct((1, LANES), F32),
                   jax.ShapeDtypeStruct((1, D), F32)),
        compiler_params=pltpu.CompilerParams(dimension_semantics=("arbitrary",), vmem_limit_bytes=limit),
    )(n4, w_gate, p, w_ple, h3, g_post, target)


def _t5_index(max_dist, stride):
    rho = np.arange(QBLK)[:, None]
    kap = np.arange(2 * QBLK)[None, :]
    d = rho + QBLK - kap
    valid = (d >= 0) & (d <= max_dist)
    n = np.maximum(d, 0) * stride
    max_exact = NUM_BUCKETS // 2
    nf = np.maximum(n, 1).astype(np.float32)
    large = max_exact + (np.log(nf / np.float32(max_exact)) / np.float32(math.log(MAX_DISTANCE / max_exact))
                         * np.float32(NUM_BUCKETS - max_exact)).astype(np.int32)
    large = np.minimum(large, NUM_BUCKETS - 1)
    bucket = np.where(n < max_exact, n, large)
    return np.where(valid, bucket, -1).astype(np.int32)


def _bias_grad(d_bias, idx, *, name):
    heads = 2 * N_PAIRS

    def body(db_ref, idx_ref, o_ref, part_ref):
        idxv = idx_ref[...]
        for b in range(NUM_BUCKETS):
            in_bucket = idxv == b
            for h in range(heads):
                masked = jnp.where(in_bucket, db_ref[h], 0.0)
                part_ref[h, b] = jnp.sum(masked.reshape(QBLK // SUBLANES, SUBLANES, 2 * QBLK), axis=0)
        lanes = lax.broadcasted_iota(jnp.int32, (NUM_BUCKETS, LANES), 1)
        acc = jnp.zeros((NUM_BUCKETS, LANES), F32)
        for h in range(heads):
            sums = jnp.sum(jnp.sum(part_ref[h], axis=1), axis=1, keepdims=True)
            acc = acc + jnp.where(lanes == h, sums, 0.0)
        o_ref[...] = acc

    return pl.pallas_call(
        body, name=name, grid=(1,),
        in_specs=[pl.BlockSpec((heads, QBLK, 2 * QBLK), lambda i: (0, 0, 0)),
                  pl.BlockSpec((QBLK, 2 * QBLK), lambda i: (0, 0))],
        out_specs=pl.BlockSpec((NUM_BUCKETS, LANES), lambda i: (0, 0)),
        out_shape=jax.ShapeDtypeStruct((NUM_BUCKETS, LANES), F32),
        scratch_shapes=[pltpu.VMEM((heads, NUM_BUCKETS, SUBLANES, 2 * QBLK), F32)],
        compiler_params=pltpu.CompilerParams(dimension_semantics=("arbitrary",)),
    )(d_bias, idx)


def _lane():
    return lax.broadcasted_iota(jnp.int32, (1, LANES), 1)


def _head_sel(h):
    return (_lane() < HEAD_DIM) if h == 0 else (_lane() >= HEAD_DIM)


def _stat_lo():
    return (_lane() % HEAD_DIM) < (HEAD_DIM // 2)


def _stack_heads(t, scale=None):
    if scale is not None:
        t = t * scale
    zero = jnp.zeros_like(t)
    return jnp.concatenate([jnp.where(_head_sel(0), t, zero), jnp.where(_head_sel(1), t, zero)], axis=0)


def _class_spec(L, r, cols, stride, pps):
    chunks = N_PAIRS // pps
    mode = pl.Buffered(1) if r * chunks == 1 else None
    return pl.BlockSpec((L, MIX_W // chunks), lambda j, c: (0, (cols + j * stride) * chunks + c), pipeline_mode=mode)


def _rows(b, n_blocks=1):
    return pl.ds(pl.multiple_of(b * QBLK, QBLK), n_blocks * QBLK)


def _key_window(ref, b, cols, first, kv_head=None):
    if kv_head is not None:
        cols = slice(0, LANES)
    if first:
        blk = ref[0:QBLK, cols]
        tile = jnp.concatenate([blk, blk], axis=0)
    else:
        tile = ref[_rows(b - 1, 2), cols]
    if kv_head is None:
        return tile
    wide = tile.astype(F32)
    keep = jnp.logical_xor(_lane() < HEAD_DIM, kv_head == 1)
    return jnp.where(keep, wide, pltpu.roll(wide, HEAD_DIM, 1)).astype(tile.dtype)


def _kv_spec(L, r, col, stride, pps, kv_shared):
    if not kv_shared:
        return _class_spec(L, r, col, stride, pps)
    mode = pl.Buffered(1) if pps == N_PAIRS else None
    return pl.BlockSpec((L, LANES), lambda j, c: (0, col), pipeline_mode=mode)


def _band_fwd(qa, ka, va, bias, *, r, q_col, k_col, v_col, stride, pattern, name, kv_shared=False, sink=None,
              carry=None):
    L = qa.shape[0]
    nb = L // QBLK
    dn = (((1,), (1,)), ((), ()))
    scale = HEAD_DIM ** -0.5

    pps = N_PAIRS

    def body(q_ref, k_ref, v_ref, bias_ref, *rest):
        sel0 = _head_sel(0)
        pair0 = pl.program_id(1) * pps

        def emit(rows, cols, o, lse):
            if sink is None:
                o_ref, lse_ref = rest
                o_ref[rows, cols] = o.astype(o_ref.dtype)
                lse_ref[rows, cols] = lse
                return
            sink_ref, out_ref, outb_ref, lse_ref = rest
            sk = sink_ref[:, cols]
            mx = jnp.maximum(lse, sk)
            w = jnp.exp(lse - mx)
            den = w + jnp.exp(sk - mx)
            out = o * (w / den)
            out_ref[rows, cols] = out
            outb_ref[rows, cols] = out.astype(outb_ref.dtype)
            lse_ref[rows, cols] = mx + jnp.log(den)

        def block(b, first):
            rows = _rows(b)
            for i in range(pps):
                cols = slice(i * LANES, (i + 1) * LANES)
                q2 = _stack_heads(q_ref[rows, cols], scale)
                kv_head = (pair0 + i) // 2 if kv_shared else None
                k = _key_window(k_ref, b, cols, first, kv_head)
                v = _key_window(v_ref, b, cols, first, kv_head)
                bias2 = bias_ref[1 if first else 0, pl.ds(2 * (pair0 + i), 2)].reshape(2 * QBLK, 2 * QBLK)
                s = lax.dot_general(q2, k, dn, preferred_element_type=F32) + bias2
                m = jnp.max(s, axis=1, keepdims=True)
                p = jnp.exp(s - m)
                l = jnp.sum(p, axis=1, keepdims=True)
                o = jnp.dot(p.astype(v.dtype), v, preferred_element_type=F32) * (1.0 / l)
                lse = m + jnp.log(l)
                emit(rows, cols, jnp.where(sel0, o[0:QBLK], o[QBLK:]), jnp.where(sel0, lse[0:QBLK], lse[QBLK:]))

        block(0, True)
        if nb > 1:
            pl.loop(1, nb)(lambda b: block(b, False))

    bias_spec = pl.BlockSpec((None, 2, 2 * N_PAIRS, QBLK, 2 * QBLK), lambda j, c: (pattern, 0, 0, 0, 0))
    out_spec = _class_spec(L, r, 0, 1, pps)
    blk_bytes = _nbytes((L, pps * LANES), F32)
    in_specs = [_class_spec(L, r, q_col, stride, pps), _kv_spec(L, r, k_col, stride, pps, kv_shared),
                _kv_spec(L, r, v_col, stride, pps, kv_shared), bias_spec]
    args = [qa, ka, va, bias]
    f32_out, bf16_out = jax.ShapeDtypeStruct((L, r * MIX_W), F32), jax.ShapeDtypeStruct((L, r * MIX_W), BF16)
    out_shape = [bf16_out, f32_out]
    if sink is not None:
        in_specs.append(pl.BlockSpec((1, MIX_W), lambda j, c: (0, 0)))
        args.append(sink)
        out_shape = [f32_out, bf16_out, f32_out]
    return _call(body, name=name, grid=(r, N_PAIRS // pps), in_specs=in_specs, out_specs=[out_spec] * len(out_shape),
                 out_shape=out_shape, args=args, vmem_limit=_vmem_limit(*([blk_bytes] * 4)), carry=carry)


def _class_rows_spec(rows, r, width):
    return pl.BlockSpec((rows // r, r * width), lambda i, *_: (i, 0))


def _token_scratch(rows, width):
    return pltpu.VMEM((width // LANES, rows, LANES), F32)


def _fill_tokens(scratch, value):
    for c in range(scratch.shape[0]):
        scratch[c] = value[:, c * LANES:(c + 1) * LANES]


def _read_tokens(scratch):
    return jnp.concatenate([scratch[c] for c in range(scratch.shape[0])], axis=1)


def _to_tokens(blk_ref, r, scratch):
    if r == 1:
        return blk_ref[...].astype(F32)
    nt, rows, _ = scratch.shape
    for j in range(r):
        for c in range(nt):
            lanes = slice((j * nt + c) * LANES, (j * nt + c + 1) * LANES)
            scratch.at[c][pl.ds(j, rows // r, stride=r), :] = blk_ref[:, lanes].astype(F32)
    return _read_tokens(scratch)


def _from_tokens(dst_ref, r, scratch):
    nt, rows, _ = scratch.shape
    if r == 1:
        dst_ref[...] = _read_tokens(scratch).astype(dst_ref.dtype)
        return
    for j in range(r):
        for c in range(nt):
            lanes = slice((j * nt + c) * LANES, (j * nt + c + 1) * LANES)
            dst_ref[:, lanes] = scratch.at[c][pl.ds(j, rows // r, stride=r), :].astype(dst_ref.dtype)


def _mm_nt_classes(a, b, bias, dils, *, bm, name):
    M, K = a.shape
    N = b.shape[0]
    dn = (((1,), (1,)), ((), ()))

    def body(a_ref, b_ref, bias_ref, *rest):
        outs, tile = rest[:-1], rest[-1]
        _fill_tokens(tile, lax.dot_general(a_ref[...], b_ref[...], dn, preferred_element_type=F32) + bias_ref[...])
        for out, r in zip(outs, dils):
            _from_tokens(out, r, tile)

    limit = _vmem_limit(_nbytes((bm, K), a.dtype), _nbytes((K, N), b.dtype) // 2, (1 + len(dils)) * _nbytes((bm, N), F32))
    return pl.pallas_call(
        body, name=name, grid=(M // bm,),
        in_specs=[pl.BlockSpec((bm, K), lambda i: (i, 0)),
                  pl.BlockSpec((N, K), lambda i: (0, 0), pipeline_mode=pl.Buffered(1)),
                  pl.BlockSpec((1, N), lambda i: (0, 0))],
        out_specs=tuple(_class_rows_spec(bm, r, N) for r in dils),
        out_shape=tuple(jax.ShapeDtypeStruct((M // r, r * N), BF16) for r in dils),
        scratch_shapes=[_token_scratch(bm, N)],
        compiler_params=pltpu.CompilerParams(dimension_semantics=("parallel",), vmem_limit_bytes=limit),
    )(a, b, bias)


def _merge(branches, dils, sink, *, name):
    W = MIX_W
    T = branches[0][0].shape[0] * dils[0]
    nbr = len(branches)

    def body(*refs):
        sink_ref = refs[2 * nbr]
        out_ref, outb_ref, lse_ref, scratch = refs[2 * nbr + 1:]
        sk = sink_ref[...]
        lses = [_to_tokens(refs[2 * t + 1], dils[t], scratch) for t in range(nbr)]
        mx = sk
        for lse in lses:
            mx = jnp.maximum(mx, lse)
        den = jnp.exp(sk - mx)
        num = jnp.zeros_like(mx)
        for t in range(nbr):
            w = jnp.exp(lses[t] - mx)
            den = den + w
            num = num + w * _to_tokens(refs[2 * t], dils[t], scratch)
        out = num / den
        out_ref[...] = out
        outb_ref[...] = out.astype(outb_ref.dtype)
        lse_ref[...] = mx + jnp.log(den)

    args = [a for br in branches for a in br] + [sink]
    in_specs = [_class_rows_spec(ROW_BLK, r, W) for r in dils for _ in range(2)] + [_vec_spec(W)]
    return pl.pallas_call(
        body, name=name, grid=(T // ROW_BLK,), in_specs=in_specs,
        out_specs=(_row_spec(W), _row_spec(W), _row_spec(W)),
        out_shape=(jax.ShapeDtypeStruct((T, W), F32), jax.ShapeDtypeStruct((T, W), BF16),
                   jax.ShapeDtypeStruct((T, W), F32)),
        scratch_shapes=[_token_scratch(ROW_BLK, W)],
        compiler_params=pltpu.CompilerParams(dimension_semantics=("parallel",)),
    )(*args)


def _attn_delta(dmix, outs, lses, sink, dils, *, name):
    T = dmix.shape[0]
    W = MIX_W
    nd = len(dils)

    def body(dmix_ref, oa_ref, ob_ref, la_ref, lb_ref, sink_ref, *rest):
        doa_ref, sta_ref, dsink_ref = rest[0:3]
        dob_refs, stb_refs = rest[3:3 + nd], rest[3 + nd:3 + 2 * nd]
        do_tile, st_tile = rest[3 + 2 * nd:]

        @pl.when(pl.program_id(0) == 0)
        def _():
            dsink_ref[...] = jnp.zeros_like(dsink_ref)

        sel0, lo = _head_sel(0), _stat_lo()
        for mixer, (o_ref, l_ref) in enumerate(((oa_ref, la_ref), (ob_ref, lb_ref))):
            for i in range(N_PAIRS):
                cols = slice(i * LANES, (i + 1) * LANES)
                do = dmix_ref[:, mixer * W + i * LANES:mixer * W + (i + 1) * LANES]
                prod = do * o_ref[:, cols]
                d0 = jnp.sum(jnp.where(sel0, prod, 0.0), axis=1, keepdims=True)
                d1 = jnp.sum(jnp.where(sel0, 0.0, prod), axis=1, keepdims=True)
                delta = jnp.where(sel0, d0, d1)
                lse = l_ref[:, cols]
                stat = jnp.where(lo, lse, delta)
                if mixer == 0:
                    sta_ref[:, cols] = stat
                    doa_ref[:, cols] = do.astype(doa_ref.dtype)
                    dsink_ref[:, cols] += -jnp.sum(jnp.exp(sink_ref[:, cols] - lse) * delta, axis=0, keepdims=True)
                else:
                    st_tile[i] = stat
                    do_tile[i] = do
        for t, r in enumerate(dils):
            _from_tokens(dob_refs[t], r, do_tile)
            _from_tokens(stb_refs[t], r, st_tile)

    class_specs = [_class_rows_spec(ROW_BLK, r, W) for r in dils]
    out_shape = [jax.ShapeDtypeStruct((T, W), BF16), jax.ShapeDtypeStruct((T, W), F32), jax.ShapeDtypeStruct((1, W), F32)]
    out_shape += [jax.ShapeDtypeStruct((T // r, r * W), BF16) for r in dils]
    out_shape += [jax.ShapeDtypeStruct((T // r, r * W), F32) for r in dils]
    res = pl.pallas_call(
        body, name=name, grid=(T // ROW_BLK,),
        in_specs=[_row_spec(2 * W), _row_spec(W), _row_spec(W), _row_spec(W), _row_spec(W), _vec_spec(W)],
        out_specs=tuple([_row_spec(W), _row_spec(W), _vec_spec(W)] + class_specs + class_specs),
        out_shape=tuple(out_shape),
        scratch_shapes=[_token_scratch(ROW_BLK, W), _token_scratch(ROW_BLK, W)],
        compiler_params=pltpu.CompilerParams(dimension_semantics=("arbitrary",)),
    )(dmix, outs[0], outs[1], lses[0], lses[1], sink)
    return res[0], res[1], res[2], res[3:3 + nd], res[3 + nd:]


def _band_bwd(qa, ka, va, d_out, stat, bias, *, r, q_col, k_col, v_col, stride, pattern, name, kv_shared=False,
              carry=None):
    L = qa.shape[0]
    nb = L // QBLK
    dn_nt = (((1,), (1,)), ((), ()))
    dn_tn = (((0,), (0,)), ((), ()))
    scale = HEAD_DIM ** -0.5

    pps = N_PAIRS if r > 1 else N_PAIRS // 2

    def body(q_ref, k_ref, v_ref, do_ref, st_ref, bias_ref, dq_ref, dk_ref, dv_ref, db_ref, dk_carry, dv_carry):
        @pl.when((pl.program_id(0) == 0) & (pl.program_id(1) == 0))
        def _():
            db_ref[...] = jnp.zeros_like(db_ref)

        lo, sel0 = _stat_lo(), _head_sel(0)
        pair0 = pl.program_id(1) * pps

        def block(b, first):
            rows = _rows(b)
            for i in range(pps):
                heads = pl.ds(2 * (pair0 + i), 2)
                cols = slice(i * LANES, (i + 1) * LANES)
                q2 = _stack_heads(q_ref[rows, cols], scale)
                kv_head = (pair0 + i) // 2 if kv_shared else None
                k = _key_window(k_ref, b, cols, first, kv_head)
                v = _key_window(v_ref, b, cols, first, kv_head)
                do2 = _stack_heads(do_ref[rows, cols])
                st = st_ref[rows, cols]
                lse2 = jnp.concatenate(
                    [jnp.max(jnp.where(_head_sel(h) & lo, st, -jnp.inf), axis=1, keepdims=True) for h in range(2)], axis=0)
                delta2 = jnp.concatenate(
                    [jnp.sum(jnp.where(_head_sel(h) & ~lo, st, 0.0), axis=1, keepdims=True) for h in range(2)],
                    axis=0) * (2.0 / HEAD_DIM)
                bias2 = bias_ref[1 if first else 0, heads].reshape(2 * QBLK, 2 * QBLK)
                s = lax.dot_general(q2, k, dn_nt, preferred_element_type=F32) + bias2
                p = jnp.exp(s - lse2)
                dp = lax.dot_general(do2, v, dn_nt, preferred_element_type=F32)
                ds = p * (dp - delta2)
                db_ref[heads] += ds.reshape(2, QBLK, 2 * QBLK)
                dsb = ds.astype(k.dtype)
                dq2 = jnp.dot(dsb, k, preferred_element_type=F32)
                dq_ref[rows, cols] = (jnp.where(sel0, dq2[0:QBLK], dq2[QBLK:]) * scale).astype(dq_ref.dtype)
                dk_acc = lax.dot_general(dsb, q2, dn_tn, preferred_element_type=F32)
                dv_acc = lax.dot_general(p.astype(k.dtype), do2, dn_tn, preferred_element_type=F32)
                if not first:
                    prev = _rows(b - 1)
                    dk_ref[prev, cols] = (dk_carry[:, cols] + dk_acc[0:QBLK]).astype(dk_ref.dtype)
                    dv_ref[prev, cols] = (dv_carry[:, cols] + dv_acc[0:QBLK]).astype(dv_ref.dtype)
                dk_carry[:, cols] = dk_acc[QBLK:2 * QBLK]
                dv_carry[:, cols] = dv_acc[QBLK:2 * QBLK]

        block(0, True)
        if nb > 1:
            pl.loop(1, nb)(lambda b: block(b, False))

        last = _rows(nb - 1)
        dk_ref[last, :] = dk_carry[...].astype(dk_ref.dtype)
        dv_ref[last, :] = dv_carry[...].astype(dv_ref.dtype)

    tok_spec = _class_spec(L, r, 0, 1, pps)
    bias_spec = pl.BlockSpec((None, 2, 2 * N_PAIRS, QBLK, 2 * QBLK), lambda j, c: (pattern, 0, 0, 0, 0))
    dbias_spec = pl.BlockSpec((2 * N_PAIRS, QBLK, 2 * QBLK), lambda j, c: (0, 0, 0))
    grad = jax.ShapeDtypeStruct((L, r * MIX_W), BF16)
    blk_bytes = _nbytes((L, pps * LANES), BF16)
    carry_shape = pltpu.VMEM((QBLK, pps * LANES), F32)
    return _call(
        body, name=name, grid=(r, N_PAIRS // pps),
        in_specs=[_class_spec(L, r, q_col, stride, pps), _kv_spec(L, r, k_col, stride, pps, kv_shared),
                  _kv_spec(L, r, v_col, stride, pps, kv_shared), tok_spec, tok_spec, bias_spec],
        out_specs=[tok_spec, tok_spec, tok_spec, dbias_spec],
        out_shape=[grad, grad, grad, jax.ShapeDtypeStruct((2 * N_PAIRS, QBLK, 2 * QBLK), F32)],
        args=[qa, ka, va, d_out, stat, bias], scratch=[carry_shape, carry_shape],
        vmem_limit=_vmem_limit(*([blk_bytes] * 9)), semantics=("arbitrary", "arbitrary"), carry=carry)


def _dz_assemble(dq_a, dk_a, dv_a, b_grads, dils, *, name):
    T = dq_a.shape[0]
    W = MIX_W

    def group_sum(x):
        u0 = x[:, 0:LANES] + x[:, LANES:2 * LANES]
        u1 = x[:, 2 * LANES:3 * LANES] + x[:, 3 * LANES:4 * LANES]
        s0 = u0 + pltpu.roll(u0, HEAD_DIM, 1)
        s1 = u1 + pltpu.roll(u1, HEAD_DIM, 1)
        return jnp.where(_head_sel(0), s0, s1)

    def body(*refs):
        dqa_ref, dka_ref, dva_ref = refs[0:3]
        b_refs = refs[3:12]
        dza_ref, dzb_ref, sa_ref, sb_ref, scratch = refs[12:]

        @pl.when(pl.program_id(0) == 0)
        def _():
            sa_ref[...] = jnp.zeros_like(sa_ref)
            sb_ref[...] = jnp.zeros_like(sb_ref)

        def put(dst, acc, col, part):
            w = part.shape[1]
            dst[:, col:col + w] = part.astype(dst.dtype)
            acc[:, col:col + w] += jnp.sum(part, axis=0, keepdims=True)

        put(dza_ref, sa_ref, 0, dqa_ref[...].astype(F32))
        put(dza_ref, sa_ref, W, group_sum(dka_ref[...].astype(F32)))
        put(dza_ref, sa_ref, W + LANES, group_sum(dva_ref[...].astype(F32)))
        for t in range(3):
            part = _to_tokens(b_refs[t], dils[0], scratch)
            for pat in range(1, len(dils)):
                part = part + _to_tokens(b_refs[3 * pat + t], dils[pat], scratch)
            put(dzb_ref, sb_ref, t * W, part)

    args = [dq_a, dk_a, dv_a] + [g[t] for g in b_grads for t in range(3)]
    return pl.pallas_call(
        body, name=name, grid=(T // ROW_BLK,),
        in_specs=[_row_spec(W)] * 3 + [_class_rows_spec(ROW_BLK, r, W) for r in dils for _ in range(3)],
        out_specs=(_row_spec(ZA_W), _row_spec(ZB_W), _vec_spec(ZA_W), _vec_spec(ZB_W)),
        out_shape=(jax.ShapeDtypeStruct((T, ZA_W), BF16), jax.ShapeDtypeStruct((T, ZB_W), BF16),
                   jax.ShapeDtypeStruct((1, ZA_W), F32), jax.ShapeDtypeStruct((1, ZB_W), F32)),
        scratch_shapes=[_token_scratch(ROW_BLK, W)],
        compiler_params=pltpu.CompilerParams(dimension_semantics=("arbitrary",)),
    )(*args)


def _place():
    x, y, c = lax.axis_index("x"), lax.axis_index("y"), lax.axis_index("c")
    chips = [(1 - x, y), (x, 1 - y), (1 - x, 1 - y)]
    return x, y, c, chips


def _cast_place(shards, k_idx, *, name):
    n, steps = len(shards), 4

    def body(k_ref, *refs):
        for s_ref, o_ref in zip(refs[:n], refs[n:]):
            o_ref[...] = s_ref[...].astype(o_ref.dtype)

    blocks = [(a.shape[0] // steps, a.shape[1]) for a in shards]
    grid_spec = pltpu.PrefetchScalarGridSpec(
        num_scalar_prefetch=1, grid=(steps,),
        in_specs=[pl.BlockSpec(blk, lambda t, k_ref: (t, 0)) for blk in blocks],
        out_specs=tuple(pl.BlockSpec(blk, lambda t, k_ref: (k_ref[0] * steps + t, 0)) for blk in blocks))
    return pl.pallas_call(
        body, name=name, grid_spec=grid_spec,
        out_shape=tuple(jax.ShapeDtypeStruct((N_CHIPS * a.shape[0], a.shape[1]), BF16) for a in shards),
        compiler_params=pltpu.CompilerParams(dimension_semantics=("parallel",),
                                             vmem_limit_bytes=_vmem_limit(*[_nbytes(b, F32) for b in blocks])),
    )(k_idx, *shards)


def _gather_carry(fulls, jobs):
    n = len(jobs)

    def piece(ref, chip_idx, half, part):
        rs = ref.shape[0] // N_CHIPS
        rh = rs // 2
        rows = rh // part[1]
        return ref.at[pl.ds(chip_idx * rs + half * rh + part[0] * rows, rows)]

    def copy(sems, sem, src_ref, dst_ref, to):
        return pltpu.make_async_remote_copy(src_ref=src_ref, dst_ref=dst_ref, send_sem=sems[0].at[sem],
                                            recv_sem=sems[1].at[sem], device_id=to, device_id_type=MESH)

    def to_chips(ins, outs, sems, j, arrival, hops=("both", "chips")):
        i, part, hop = jobs[j]
        x, y, c, chips = _place()
        res = []
        for t, chip in enumerate(chips if hop in hops else ()):
            theirs = piece(outs[i], 2 * chip[0] + chip[1], c, part)
            src, dst = (theirs, theirs) if arrival else (piece(ins[i], 2 * x + y, c, part), piece(outs[i], 2 * x + y, c, part))
            res.append(copy(sems, 3 * j + t, src, dst, (*chip, c)))
        return res

    def to_sibling(outs, sems, j, arrival, hops=("both", "sibling")):
        i, part, hop = jobs[j]
        x, y, c, chips = _place()
        res = []
        for t, chip in enumerate(chips if hop in hops else ()):
            region = piece(outs[i], 2 * chip[0] + chip[1], 1 - c if arrival else c, part)
            res.append(copy(sems, 3 * n + 3 * j + t, region, region, (x, y, 1 - c)))
        return res

    def start(ins, outs, sems):
        for j in range(n):
            for send in to_chips(ins, outs, sems, j, False) + to_sibling(outs, sems, j, False, ("sibling",)):
                send.start()

    def before_last(ins, outs, sems):
        for j in range(n):
            for arrival, send in zip(to_chips(ins, outs, sems, j, True, ("both",)),
                                     to_sibling(outs, sems, j, False, ("both",))):
                arrival.wait_recv()
                send.start()

    def finish(ins, outs, sems):
        for j in range(n):
            for arrival in to_chips(ins, outs, sems, j, True, ("chips",)) + to_sibling(outs, sems, j, True):
                arrival.wait_recv()
        for j in range(n):
            for send in to_chips(ins, outs, sems, j, False) + to_sibling(outs, sems, j, False):
                send.wait_send()

    return _Carry(fulls, [jax.ShapeDtypeStruct(a.shape, a.dtype) for a in fulls], {i: i for i in range(len(fulls))},
                  [pltpu.SemaphoreType.DMA((6 * n,)), pltpu.SemaphoreType.DMA((6 * n,))], start, finish, before_last)


def _pair_swap_carry(grads):
    n = len(grads)

    def copy(ins, outs, sems, i):
        x, y, c, _ = _place()
        return pltpu.make_async_remote_copy(src_ref=ins[i].at[:, 1 - c], dst_ref=outs[i], send_sem=sems[0].at[i],
                                            recv_sem=sems[1].at[i], device_id=(x, y, 1 - c), device_id_type=MESH)

    def start(ins, outs, sems):
        for i in range(n):
            copy(ins, outs, sems, i).start()

    def finish(ins, outs, sems):
        for i in range(n):
            copy(ins, outs, sems, i).wait()

    return _Carry(grads, [jax.ShapeDtypeStruct((a.shape[0], a.shape[2], a.shape[3]), a.dtype) for a in grads], {},
                  [pltpu.SemaphoreType.DMA((n,)), pltpu.SemaphoreType.DMA((n,))], start, finish)


def _pair_sum(g4, got, c_idx, *, name):
    _, _, rh, cs = g4.shape
    rb = _row_block(rh, 512, 16)

    def body(c_ref, own_ref, got_ref, o_ref):
        o_ref[...] = (own_ref[...].astype(F32) + got_ref[...].astype(F32)).astype(o_ref.dtype)

    grid_spec = pltpu.PrefetchScalarGridSpec(
        num_scalar_prefetch=1, grid=(N_CHIPS, rh // rb),
        in_specs=[pl.BlockSpec((None, None, rb, cs), lambda k, t, c_ref: (k, c_ref[0], t, 0)),
                  pl.BlockSpec((None, rb, cs), lambda k, t, c_ref: (k, t, 0))],
        out_specs=pl.BlockSpec((None, rb, cs), lambda k, t, c_ref: (k, t, 0)))
    return pl.pallas_call(
        body, name=name, grid_spec=grid_spec, out_shape=jax.ShapeDtypeStruct((N_CHIPS, rh, cs), BF16),
        compiler_params=pltpu.CompilerParams(dimension_semantics=("parallel", "parallel")),
    )(c_idx, g4, got)


def _chip_scatter_carry(sums):
    n = len(sums)

    def copies(ins, outs, sems):
        x, y, c, chips = _place()
        return [pltpu.make_async_remote_copy(src_ref=ins[i].at[2 * chip[0] + chip[1]], dst_ref=outs[i].at[j],
                                             send_sem=sems[0].at[3 * i + j], recv_sem=sems[1].at[3 * i + j],
                                             device_id=(*chip, c), device_id_type=MESH)
                for i in range(n) for j, chip in enumerate(chips)]

    def start(ins, outs, sems):
        for cp in copies(ins, outs, sems):
            cp.start()

    def finish(ins, outs, sems):
        for cp in copies(ins, outs, sems):
            cp.wait()

    return _Carry(sums, [jax.ShapeDtypeStruct((3,) + a.shape[1:], a.dtype) for a in sums], {},
                  [pltpu.SemaphoreType.DMA((3 * n,)), pltpu.SemaphoreType.DMA((3 * n,))], start, finish)


def _chip_sum(sums, gots, kc_idx, *, name):
    n, steps = len(sums), 2

    def body(kc_ref, *refs):
        for own_ref, got_ref, o_ref in zip(refs[:n], refs[n:2 * n], refs[2 * n:]):
            acc = own_ref[...].astype(F32)
            for j in range(3):
                acc = acc + got_ref[j].astype(F32)
            o_ref[...] = acc

    blocks = [(a.shape[1] // steps, a.shape[2]) for a in sums]
    grid_spec = pltpu.PrefetchScalarGridSpec(
        num_scalar_prefetch=1, grid=(steps,),
        in_specs=[pl.BlockSpec((None,) + blk, lambda t, kc: (kc[0], t, 0)) for blk in blocks]
        + [pl.BlockSpec((3,) + blk, lambda t, kc: (0, t, 0)) for blk in blocks],
        out_specs=tuple(pl.BlockSpec(blk, lambda t, kc: (kc[1] * steps + t, 0)) for blk in blocks))
    return pl.pallas_call(
        body, name=name, grid_spec=grid_spec,
        out_shape=tuple(jax.ShapeDtypeStruct((2 * a.shape[1], a.shape[2]), F32) for a in sums),
        compiler_params=pltpu.CompilerParams(dimension_semantics=("parallel",),
                                             vmem_limit_bytes=_vmem_limit(*[3 * _nbytes(b, F32) for b in blocks])),
    )(kc_idx, *sums, *gots)


def _half_swap_carry(shards):
    n = len(shards)

    def copy(ins, outs, sems, i, to_my_half):
        x, y, c, _ = _place()
        rh = ins[i].shape[0] // 2
        half = c if to_my_half else 1 - c
        return pltpu.make_async_remote_copy(
            src_ref=ins[i].at[pl.ds(c * rh, rh)], dst_ref=outs[i].at[pl.ds(half * rh, rh)],
            send_sem=sems[0].at[i], recv_sem=sems[1].at[i], device_id=(x, y, 1 - c), device_id_type=MESH)

    def start(ins, outs, sems):
        for i in range(n):
            copy(ins, outs, sems, i, True).start()

    def finish(ins, outs, sems):
        for i in range(n):
            copy(ins, outs, sems, i, False).wait_recv()
        for i in range(n):
            copy(ins, outs, sems, i, True).wait_send()

    return _Carry(shards, [jax.ShapeDtypeStruct(a.shape, a.dtype) for a in shards], {i: i for i in range(n)},
                  [pltpu.SemaphoreType.DMA((n,)), pltpu.SemaphoreType.DMA((n,))], start, finish)


SMALL_ROW = 1024


def _small_layout(shapes):
    starts, row = [], 0
    for r, c in shapes:
        starts.append(row)
        row += -(-c // SMALL_ROW) if r == 1 else r
    return starts, -(-row // SUBLANES) * SUBLANES


def _small_pieces(shape, start):
    r, c = shape
    if r == 1:
        return [(start + q, min(SMALL_ROW, c - q * SMALL_ROW), q * SMALL_ROW) for q in range(-(-c // SMALL_ROW))]
    return None


def _whole_spec(shape):
    return pl.BlockSpec(tuple(shape), lambda i: (0,) * len(shape))


def _small_all_reduce(parts, *, name, carry=None):
    shapes = [a.shape for a in parts]
    starts, rows = _small_layout(shapes)
    n = len(parts)
    n_tables = sum(1 for r, _ in shapes if r > 1)
    assert all(r > 1 and c <= LANES for r, c in shapes[:n_tables]) and all(r == 1 for r, _ in shapes[n_tables:])
    table_rows = starts[n_tables]
    assert table_rows % SUBLANES == 0

    def regions(slot):
        return slot.at[pl.ds(0, table_rows), pl.ds(0, LANES)], slot.at[pl.ds(table_rows, rows - table_rows)]

    def peer(d):
        x, y, c, _ = _place()
        return 1 - x if d & 4 else x, 1 - y if d & 2 else y, 1 - c if d & 1 else c

    def slot_of(d):
        px, py, pc = peer(d)
        return 4 * px + 2 * py + pc

    def copy(refs, d, part, dst_slot):
        buf, send_sems, recv_sems = refs[2 * n:]
        sem = 2 * (d - 1) + part
        return pltpu.make_async_remote_copy(
            src_ref=regions(buf.at[slot_of(0)])[part], dst_ref=regions(buf.at[dst_slot])[part],
            send_sem=send_sems.at[sem], recv_sem=recv_sems.at[sem], device_id=peer(d), device_id_type=MESH)

    def send(*refs):
        ins, mine = refs[0:n], refs[2 * n].at[slot_of(0)]
        mine[...] = jnp.zeros((rows, SMALL_ROW), F32)
        for ref, shape, start in zip(ins, shapes, starts):
            pieces = _small_pieces(shape, start)
            if pieces is None:
                mine[start:start + shape[0], 0:shape[1]] = ref[...]
            else:
                for row, width, col in pieces:
                    mine[row:row + 1, 0:width] = ref[:, col:col + width]
        for d in range(1, 8):
            for part in range(2):
                copy(refs, d, part, slot_of(0)).start()

    def receive(*refs):
        outs, buf = refs[n:2 * n], refs[2 * n]
        mine = buf.at[slot_of(0)]
        for d in range(1, 8):
            for part in range(2):
                copy(refs, d, part, slot_of(d)).wait_recv()
        for d in range(1, 8):
            for part in range(2):
                copy(refs, d, part, slot_of(0)).wait_send()
        tables, vectors = regions(buf.at[0])
        acc_t, acc_v = tables[...], vectors[...]
        for s in range(1, 8):
            tables, vectors = regions(buf.at[s])
            acc_t, acc_v = acc_t + tables[...], acc_v + vectors[...]
        tables, vectors = regions(mine)
        tables[...] = acc_t
        vectors[...] = acc_v
        for ref, shape, start in zip(outs, shapes, starts):
            pieces = _small_pieces(shape, start)
            if pieces is None:
                ref[...] = mine[start:start + shape[0], 0:shape[1]]
            else:
                for row, width, col in pieces:
                    ref[:, col:col + width] = mine[row:row + 1, 0:width]

    def body(*refs):
        send(*refs)
        receive(*refs)

    whole = [_whole_spec(s) for s in shapes]
    return _call(body if carry is None else receive, before_carry=None if carry is None else send,
                 name=name, grid=(1,), in_specs=whole, out_specs=whole,
                 out_shape=[jax.ShapeDtypeStruct(s, F32) for s in shapes], args=parts,
                 scratch=[pltpu.VMEM((8, rows, SMALL_ROW), F32), pltpu.SemaphoreType.DMA((14,)),
                          pltpu.SemaphoreType.DMA((14,))], carry=carry)


def _adamw_small(ws, gs, ms, vs, *, name):
    n = len(ws)
    c1 = 1.0 - ADAM_B1 ** ADAM_STEP
    c2 = 1.0 - ADAM_B2 ** ADAM_STEP

    def body(*refs):
        for k in range(n):
            w_ref, g_ref, m_ref, v_ref = (refs[t * n + k] for t in range(4))
            go_ref, d_ref, mo_ref, vo_ref = (refs[(4 + t) * n + k] for t in range(4))
            gv = g_ref[...]
            go_ref[...] = gv
            mn = ADAM_B1 * m_ref[...] + (1.0 - ADAM_B1) * gv
            vn = ADAM_B2 * v_ref[...] + (1.0 - ADAM_B2) * (gv * gv)
            d_ref[...] = -ADAM_LR * ((mn / c1) / (jnp.sqrt(vn / c2) + ADAM_EPS) + ADAM_WD * w_ref[...])
            mo_ref[...] = mn
            vo_ref[...] = vn

    whole = [_whole_spec(a.shape) for a in ws]
    shapes = tuple(jax.ShapeDtypeStruct(a.shape, F32) for a in ws)
    res = pl.pallas_call(
        body, name=name, grid=(1,), in_specs=whole * 4, out_specs=tuple(whole * 4), out_shape=shapes * 4,
    )(*ws, *gs, *ms, *vs)
    return res[0:n], res[n:2 * n], res[2 * n:3 * n], res[3 * n:4 * n]


def _adamw(ws, gs, ms, vs, *, name):
    n, steps = len(ws), 8
    c1 = 1.0 - ADAM_B1 ** ADAM_STEP
    c2 = 1.0 - ADAM_B2 ** ADAM_STEP

    def body(*refs):
        for k in range(n):
            w_ref, g_ref, m_ref, v_ref = (refs[t * n + k] for t in range(4))
            go_ref, d_ref, mo_ref, vo_ref = (refs[(4 + t) * n + k] for t in range(4))
            gv = g_ref[...]
            go_ref[...] = gv
            mn = ADAM_B1 * m_ref[...] + (1.0 - ADAM_B1) * gv
            vn = ADAM_B2 * v_ref[...] + (1.0 - ADAM_B2) * (gv * gv)
            d_ref[...] = -ADAM_LR * ((mn / c1) / (jnp.sqrt(vn / c2) + ADAM_EPS) + ADAM_WD * w_ref[...])
            mo_ref[...] = mn
            vo_ref[...] = vn

    blocks = [(a.shape[0] // steps, a.shape[1]) for a in ws]
    specs = [pl.BlockSpec(blk, lambda i: (i, 0)) for blk in blocks]
    shapes = tuple(jax.ShapeDtypeStruct(a.shape, F32) for a in ws)
    res = pl.pallas_call(
        body, name=name, grid=(steps,), in_specs=specs * 4, out_specs=tuple(specs * 4), out_shape=shapes * 4,
        compiler_params=pltpu.CompilerParams(dimension_semantics=("parallel",),
                                             vmem_limit_bytes=_vmem_limit(*[8 * _nbytes(b, F32) for b in blocks])),
    )(*ws, *gs, *ms, *vs)
    return res[0:n], res[n:2 * n], res[2 * n:3 * n], res[3 * n:4 * n]


def _local_step(x, p, target, small, sched):
    T = x.shape[0]
    W = MIX_W
    rel_bias = small["rel_bias"]
    b_in_a, b_in_b = small["b_in"][:, 0:ZA_W], small["b_in"][:, ZA_W:]

    idx_np = [_t5_index(A_WINDOW - 1, 1)] + [_t5_index(window // dil, dil) for window, dil in B_PATTERNS]
    idx_all = jnp.asarray(np.stack(idx_np))
    n1, bias_all = sched.run(_prologue, x, small["ffn1_pre_g"], rel_bias, idx_all, name="prologue")
    w_gu1 = sched.weight("ffn1_w_gu")
    *gu1, a1 = sched.run(_ffn_up, n1, w_gu1, bm=512, name="ffn1_gu")
    w_d1 = sched.weight("ffn1_w_down")
    f1, h1, n2 = sched.run(_mm_resid_norm, a1, w_d1, x, small["ffn1_post_g"], 0.5, small["attn_pre_g"], bm=512,
                           name="ffn1_down")
    win_a, win_b = sched.weight("w_in")
    za = _mm_nt(n2, win_a, bm=1024, bn=ZA_W, out_dtype=BF16, name="in_proj_a", add=b_in_a)
    dils = tuple(dil for _, dil in B_PATTERNS)
    zb_by_dil = _mm_nt_classes(n2, win_b, b_in_b, dils, bm=512, name="in_proj_b")

    a_args = dict(r=1, q_col=0, k_col=W // LANES, v_col=W // LANES + 1, stride=0, pattern=0, kv_shared=True)
    sink_row = jnp.repeat(small["sinks"], HEAD_DIM, axis=1)
    out_a, out_a_bf, lse_a = sched.run(_band_fwd, za, za, za, bias_all, name="band_a_fwd", sink=sink_row, **a_args)

    no_sink = jnp.full((1, W), NEG, F32)
    b_ctx, branches = [], []
    for t, dil in enumerate(dils):
        zb_r = zb_by_dil[t]
        args = dict(r=dil, q_col=0, k_col=1, v_col=2, stride=ZB_W // W, pattern=1 + t)
        branches.append(sched.run(_band_fwd, zb_r, zb_r, zb_r, bias_all, name=f"band_b{t}_fwd", **args))
        b_ctx.append((jnp.asarray(idx_np[1 + t]), zb_r, args))
    out_b, out_b_bf, lse_b = _merge(branches, dils, no_sink, name="merge_b")

    mix = jnp.concatenate([out_a_bf, out_b_bf], axis=1)
    w_out = sched.weight("w_out")
    att, h2, n3 = _mm_resid_norm(mix, w_out, h1, small["attn_post_g"], 1.0, small["ffn2_pre_g"], bm=512,
                                 name="out_proj", bias=small["b_out"])
    w_gu2, w_d2 = sched.weight("ffn2_w_gu"), sched.weight("ffn2_w_down")
    *gu2, a2 = _ffn_up(n3, w_gu2, bm=512, name="ffn2_gu")
    f2, h3, n4 = _mm_resid_norm(a2, w_d2, h2, small["ffn2_post_g"], 0.5, small["ple_pre_g"], bm=512,
                                name="ffn2_down")
    w_gate = sched.weight("w_ple_gate")
    p_bf = p.astype(BF16)
    dh4, de, dgp, loss, d_ple_post = _ple_head(n4, w_gate, p_bf, sched.weight("w_ple_proj"), h3, small["ple_post_g"],
                                               target, bm=512, name="ple_head")

    gs = {"ple_post_g": d_ple_post}
    sched.grad("w_ple_proj", _mm_tn(p_bf, de, bm=256, bn=1024, out_dtype=BF16, name="d_w_ple"))
    sched.grad("w_ple_gate", _mm_tn(n4, dgp, bm=512, bn=1024, out_dtype=BF16, name="d_w_gate"))
    dh3, df2, gs["ple_pre_g"], gs["ffn2_post_g"], _ = sched.run(
        _mm_nt_norms_bwd, dgp, w_gate, dh4, h3, small["ple_pre_g"], f2, small["ffn2_post_g"], 0.5, bm=512, name="d_n4")

    def ffn_bwd(df, a, gu, n, w_down, w_gu, tag, *norm_args):
        dgu = sched.run(_ffn_down_bwd, df, w_down, gu[0], gu[1], bm=512, name=f"ffn{tag}_d_a")
        sched.grad(f"ffn{tag}_w_gu", _mm_tn(n, dgu, bm=512, bn=1408, out_dtype=BF16, name=f"ffn{tag}_d_w_gu",
                                            n_stack=N_CHIPS))
        sched.grad(f"ffn{tag}_w_down", sched.run(_mm_tn, a, df, bm=256, bn=1024, out_dtype=BF16,
                                                name=f"ffn{tag}_d_w_down"))
        return sched.run(_mm_nt_norms_bwd, dgu, w_gu, *norm_args, bm=512, name=f"ffn{tag}_d_n")

    dh2, datt, gs["ffn2_pre_g"], gs["attn_post_g"], gs["b_out"] = ffn_bwd(
        df2, a2, gu2, n3, w_d2, w_gu2, "2", dh3, h2, small["ffn2_pre_g"], att, small["attn_post_g"], 1.0)
    sched.grad("w_out", _mm_tn(mix, datt, bm=512, bn=1024, out_dtype=BF16, name="d_w_out"))
    dmix = sched.run(_mm_nt, datt, w_out, bm=1024, bn=1024, out_dtype=F32, name="d_mix")

    do_a, stat_a, dsink, do_b, stat_b = _attn_delta(dmix, (out_a, out_b), (lse_a, lse_b), sink_row, dils,
                                                    name="attn_delta")
    gs["sinks"] = dsink[:, ::HEAD_DIM]
    dq_a, dk_a, dv_a, dbias_a = sched.run(_band_bwd, za, za, za, do_a, stat_a, bias_all, name="band_a_bwd", **a_args)
    d_rel_a = _bias_grad(dbias_a, jnp.asarray(idx_np[0]), name="d_rel_a")
    b_grads = []
    d_rel_b = None
    for t, (idx, zb_r, args) in enumerate(b_ctx):
        dq, dk, dv, dbias = sched.run(_band_bwd, zb_r, zb_r, zb_r, do_b[t], stat_b[t], bias_all,
                                      name=f"band_b{t}_bwd", **args)
        b_grads.append((dq, dk, dv))
        d_rel = _bias_grad(dbias, idx, name=f"d_rel_b{t}")
        d_rel_b = d_rel if d_rel_b is None else d_rel_b + d_rel
    gs["rel_bias"] = jnp.concatenate([d_rel_a[:, 0:2 * N_PAIRS], d_rel_b[:, 0:2 * N_PAIRS]], axis=1)
    dza, dzb, dsum_a, dsum_b = _dz_assemble(dq_a, dk_a, dv_a, b_grads, dils, name="d_z")
    gs["b_in"] = jnp.concatenate([dsum_a, dsum_b], axis=1)
    sched.grad("w_in", (_mm_tn(dza, n2, bm=ZA_W, bn=1024, out_dtype=BF16, name="d_w_in_a"),
                        _mm_tn(dzb, n2, bm=ZB_W // 2, bn=1024, out_dtype=BF16, name="d_w_in_b")))
    dn2_a = _mm_nn(dza, win_a, bm=1024, bn=1024, out_dtype=F32, name="d_n2_a")
    dh1, df1, gs["attn_pre_g"], gs["ffn1_post_g"], _ = sched.run(
        _mm_nt_norms_bwd, dzb, win_b, dh2, h1, small["attn_pre_g"], f1, small["ffn1_post_g"], 0.5, bm=512,
        name="d_n2_b", add=dn2_a, b_is_kn=True)
    grad_x, gs["ffn1_pre_g"] = ffn_bwd(df1, a1, gu1, n1, w_d1, w_gu1, "1", dh1, x, small["ffn1_pre_g"], None, None, 0.0)
    return loss, grad_x, gs


def _to_compute(name, full):
    st = full.reshape(N_CHIPS, full.shape[0] // N_CHIPS, full.shape[1])
    if name in ("ffn1_w_gu", "ffn2_w_gu"):
        return st
    if name == "w_ple_proj":
        return jnp.transpose(st, (1, 0, 2)).reshape(st.shape[1], -1)
    if name == "w_in":
        return full[0:ZA_W], full[ZA_W:]
    return full


def _to_comm(name, g):
    if name == "w_in":
        g = jnp.concatenate(g, axis=0)
    if name == "w_ple_proj":
        g = jnp.transpose(g.reshape(g.shape[0], N_CHIPS, -1), (1, 0, 2))
    rs = g.shape[1] if g.ndim == 3 else g.shape[0] // N_CHIPS
    return g.reshape(N_CHIPS, 2, rs // 2, g.shape[-1])


class _Schedule:
    GATHER = {
        "prologue": [("ffn1_w_gu", (0, 1), "both")],
        "ffn1_gu": [("ffn1_w_down", (0, 1), "both"), ("w_in", (0, 1), "both")],
        "band_a_fwd": [("ffn2_w_gu", (0, 2), "chips"), ("w_out", (0, 1), "chips")],
        "band_b0_fwd": [("ffn2_w_gu", (0, 2), "sibling"), ("w_out", (0, 1), "sibling"),
                        ("ffn2_w_gu", (1, 2), "chips"), ("w_ple_gate", (0, 1), "chips"), ("w_ple_proj", (0, 1), "chips")],
        "band_b1_fwd": [("ffn2_w_gu", (1, 2), "sibling"), ("w_ple_gate", (0, 1), "sibling"),
                        ("w_ple_proj", (0, 1), "sibling"), ("ffn2_w_down", (0, 1), "both")],
    }
    SWAP = {"ffn2_d_w_down": ["w_ple_proj", "w_ple_gate", "ffn2_w_gu"], "ffn2_d_n": ["ffn2_w_down"],
            "band_a_bwd": ["w_out"], "d_n2_b": ["w_in"], "ffn1_d_w_down": ["ffn1_w_gu"], "ffn1_d_n": ["ffn1_w_down"]}
    SCATTER = {"ffn2_d_n": ["ffn2_w_gu"], "band_a_bwd": ["w_ple_proj", "w_ple_gate", "ffn2_w_down"],
               "d_n2_b": ["w_out"], "ffn1_d_w_down": ["w_in"], "ffn1_d_n": ["ffn1_w_gu"]}
    HALF_SWAP = {"ffn1_d_n": ["w_ple_proj", "w_ple_gate", "ffn2_w_gu", "ffn2_w_down", "w_out", "w_in"]}

    def __init__(self, placed, c_idx, kc_idx):
        self.full = dict(placed)
        self.missing = {n: 1.0 for n in placed}
        self.c_idx, self.kc_idx = c_idx, kc_idx
        self.grads, self.swapped, self.pair_sums, self.scattered, self.summed = {}, {}, {}, {}, {}

    def _gather(self, entries):
        if not entries:
            return [], []
        names = list(dict.fromkeys(n for n, _, _ in entries))
        carry = _gather_carry([self.full[n] for n in names], [(names.index(n), pt, hops) for n, pt, hops in entries])

        def done():
            self.full.update(zip(names, carry.results))
            for n, part, hops in entries:
                if hops != "chips":
                    self.missing[n] -= 1.0 / part[1]
        return [carry], [done]

    def weight(self, name):
        assert abs(self.missing[name]) < 1e-9, (name, self.missing[name])
        return _to_compute(name, self.full[name])

    def grad(self, name, g):
        self.grads[name] = _to_comm(name, g)

    def _swap(self, names):
        carry = _pair_swap_carry([self.grads[n] for n in names])

        def done():
            self.swapped.update(zip(names, carry.results))
        return carry, done

    def _scatter(self, names):
        for n in names:
            self.pair_sums[n] = _pair_sum(self.grads[n], self.swapped[n], self.c_idx, name=f"grad_pair_sum_{n}")
        carry = _chip_scatter_carry([self.pair_sums[n] for n in names])

        def done():
            self.scattered.update(zip(names, carry.results))
        return carry, done

    def _half_swap(self, names):
        halves = _chip_sum([self.pair_sums[n] for n in names], [self.scattered[n] for n in names], self.kc_idx,
                           name=f"grad_chip_sum_{names[0]}")
        carry = _half_swap_carry(halves)

        def done():
            self.summed.update(zip(names, carry.results))
        return carry, done

    def run(self, fn, *args, name, **kw):
        carries, after = self._gather(self.GATHER.get(name, []))
        for names, make in ((self.SWAP.get(name), self._swap), (self.SCATTER.get(name), self._scatter),
                            (self.HALF_SWAP.get(name), self._half_swap)):
            if names:
                carry, done = make(names)
                carries.append(carry)
                after.append(done)
        out = fn(*args, name=name, carry=_join(carries), **kw)
        for done in after:
            done()
        return out

    def finish(self, last_carrier):
        left = [n for n in BIG if n not in self.scattered]
        to_swap = [n for n in left if n not in self.swapped]
        if to_swap:
            carry, done = self._swap(to_swap)
            _comm_call(carry, name="grad_pair_swap")
            done()
        carry, done = self._scatter(left) if left else (None, lambda: None)
        carried = last_carrier(carry=carry)
        done()
        carry, done = self._half_swap([n for n in BIG if n not in self.summed])
        _comm_call(carry, name="grad_half_swap")
        done()
        return self.summed, carried


def kernel(x, p, rel_bias, ffn1_pre_g, ffn1_w_gu, ffn1_w_down, ffn1_post_g, attn_pre_g, w_in, b_in, sinks, w_out, b_out, attn_post_g, ffn2_pre_g, ffn2_w_gu, ffn2_w_down, ffn2_post_g, ple_pre_g, w_ple_gate, w_ple_proj, ple_post_g, loss_target, m_rel_bias, m_ffn1_pre_g, m_ffn1_w_gu, m_ffn1_w_down, m_ffn1_post_g, m_attn_pre_g, m_w_in, m_b_in, m_sinks, m_w_out, m_b_out, m_attn_post_g, m_ffn2_pre_g, m_ffn2_w_gu, m_ffn2_w_down, m_ffn2_post_g, m_ple_pre_g, m_w_ple_gate, m_w_ple_proj, m_ple_post_g, v_rel_bias, v_ffn1_pre_g, v_ffn1_w_gu, v_ffn1_w_down, v_ffn1_post_g, v_attn_pre_g, v_w_in, v_b_in, v_sinks, v_w_out, v_b_out, v_attn_post_g, v_ffn2_pre_g, v_ffn2_w_gu, v_ffn2_w_down, v_ffn2_post_g, v_ple_pre_g, v_w_ple_gate, v_w_ple_proj, v_ple_post_g):
    given = dict(locals())
    w = {n: given[n] for n in WEIGHTS}
    m = {n: given["m_" + n] for n in WEIGHTS}
    v = {n: given["v_" + n] for n in WEIGHTS}
    c_pos = lax.axis_index("c").astype(jnp.int32)
    k_pos = (2 * lax.axis_index("x") + lax.axis_index("y")).astype(jnp.int32)
    c_idx, k_idx, kc_idx = c_pos.reshape(1), k_pos.reshape(1), jnp.stack([k_pos, c_pos])

    def shard(a, n):
        return jnp.transpose(a[0]) if n == "w_in" else a[0]

    def unshard(a, n):
        return (jnp.transpose(a) if n == "w_in" else a)[None]

    placed = _cast_place([shard(w[n], n) for n in BIG], k_idx, name="place_shards")
    sched = _Schedule(dict(zip(BIG, placed)), c_idx, kc_idx)
    small = {n: (w[n] if n == "rel_bias" else w[n].reshape(1, -1)) for n in SMALL}

    loss_part, grad_x, gs = _local_step(x[0], p[0, 0], loss_target[0], small, sched)

    grads_big, (*small_grads, loss_row) = sched.finish(
        partial(_small_all_reduce, [gs[n] for n in SMALL] + [loss_part], name="small_all_reduce"))
    loss = loss_row[0, 0]

    grads, delta, new_m, new_v = {}, {}, {}, {}
    res = _adamw([shard(w[n], n) for n in BIG], [grads_big[n] for n in BIG], [shard(m[n], n) for n in BIG],
                 [shard(v[n], n) for n in BIG], name="adamw_big")
    for n, gg, dd, mm, vv in zip(BIG, *res):
        grads[n], delta[n], new_m[n], new_v[n] = (unshard(a, n) for a in (gg, dd, mm, vv))
    res = _adamw_small([w[n] for n in SMALL], small_grads, [m[n] for n in SMALL], [v[n] for n in SMALL],
                       name="adamw_small")
    for name, gg, dd, mm, vv in zip(SMALL, *res):
        grads[name], delta[name], new_m[name], new_v[name] = gg, dd, mm, vv

    return (loss, grad_x[None], *[grads[n] for n in WEIGHTS], *[delta[n] for n in WEIGHTS],
            *[new_m[n] for n in WEIGHTS], *[new_v[n] for n in WEIGHTS])
```

```python
import math
from functools import partial

import jax
import jax.numpy as jnp
import numpy as np
from jax import lax
from jax.experimental import pallas as pl
from jax.experimental.pallas import tpu as pltpu

F32 = jnp.float32
BF16 = jnp.bfloat16
MESH = pl.DeviceIdType.MESH

EPS = 1e-6
NEG = -1e30
LANES = 128
SUBLANES = 8
HEAD_DIM = 64
QBLK = 128
N_PAIRS = 4
MIX_W = N_PAIRS * LANES
A_WINDOW = 128
B_PATTERNS = ((128, 1), (512, 4), (2048, 16))
NUM_BUCKETS = 32
MAX_DISTANCE = 2048
ZA_W = 768
ZB_W = 1536
N_CHIPS = 4
VMEM_BYTES_V7X = 64 * 2**20
VMEM_CLAIM = VMEM_BYTES_V7X - (6 << 20)
RING_SLOTS = 3

ADAM_LR, ADAM_B1, ADAM_B2, ADAM_EPS, ADAM_WD, ADAM_STEP = 0.001, 0.9, 0.999, 1e-08, 0.01, 10

BIG = ("ffn1_w_gu", "ffn1_w_down", "w_in", "w_out", "ffn2_w_gu", "ffn2_w_down", "w_ple_gate", "w_ple_proj")
SMALL = ("rel_bias", "ffn1_pre_g", "ffn1_post_g", "attn_pre_g", "b_in", "sinks", "b_out", "attn_post_g",
         "ffn2_pre_g", "ffn2_post_g", "ple_pre_g", "ple_post_g")
WEIGHTS = ("rel_bias", "ffn1_pre_g", "ffn1_w_gu", "ffn1_w_down", "ffn1_post_g", "attn_pre_g", "w_in", "b_in",
           "sinks", "w_out", "b_out", "attn_post_g", "ffn2_pre_g", "ffn2_w_gu", "ffn2_w_down", "ffn2_post_g",
           "ple_pre_g", "w_ple_gate", "w_ple_proj", "ple_post_g")


def _vmem_limit(*block_bytes):
    del block_bytes
    return VMEM_CLAIM


def _nbytes(shape, dtype):
    return int(np.prod(shape)) * jnp.dtype(dtype).itemsize


MXU_WIDTH_V7X = 256


def _col_chunks(n):
    return [slice(c, min(c + MXU_WIDTH_V7X, n)) for c in range(0, n, MXU_WIDTH_V7X)]


def _row_halves(rows):
    return [slice(0, rows // 2), slice(rows // 2, rows)]


def _row_block(rows, cap, mult):
    for cand in range(min(rows, cap), 0, -1):
        if rows % cand == 0 and cand % mult == 0:
            return cand
    raise ValueError((rows, cap, mult))


class _Carry:
    def __init__(self, operands, out_shapes, aliases, sems, start, finish, before_last=None):
        self.operands, self.out_shapes, self.aliases, self.sems = list(operands), list(out_shapes), dict(aliases), list(sems)
        self.start, self.finish = start, finish
        self.before_last = before_last or (lambda ins, outs, sems: None)
        self.results = None


def _join(carries):
    carries = [c for c in carries if c is not None]
    if not carries:
        return None
    if len(carries) == 1:
        return carries[0]
    offs, pos = [], [0, 0, 0]
    for c in carries:
        offs.append(tuple(pos))
        pos = [pos[0] + len(c.operands), pos[1] + len(c.out_shapes), pos[2] + len(c.sems)]
    aliases = {}
    for c, (oi, oo, _) in zip(carries, offs):
        aliases.update({oi + i: oo + o for i, o in c.aliases.items()})

    def run(which):
        def fn(ins, outs, sems):
            for c, (oi, oo, os_) in zip(carries, offs):
                getattr(c, which)(ins[oi:oi + len(c.operands)], outs[oo:oo + len(c.out_shapes)],
                                  sems[os_:os_ + len(c.sems)])
        return fn

    joined = _Carry([a for c in carries for a in c.operands], [s for c in carries for s in c.out_shapes], aliases,
                    [s for c in carries for s in c.sems], run("start"), run("finish"), run("before_last"))
    joined.parts = (carries, offs)
    return joined


def _set_results(carry, outs):
    carry.results = list(outs)
    if hasattr(carry, "parts"):
        for c, (_, oo, _) in zip(*carry.parts):
            _set_results(c, outs[oo:oo + len(c.out_shapes)])


ANY = pl.BlockSpec(memory_space=pl.ANY)


def _call(body, *, name, grid, in_specs, out_specs, out_shape, args, scratch=(), vmem_limit=None, carry=None,
          semantics=None, before_carry=None):
    assert before_carry is None or carry is not None
    n_ci, n_co, n_cs = len(args), len(out_shape), len(scratch)
    semantics = semantics or ("parallel",) * len(grid)
    vmem_limit = vmem_limit or VMEM_CLAIM
    if carry is None:
        res = pl.pallas_call(
            body, name=name, grid=grid, in_specs=list(in_specs), out_specs=tuple(out_specs), out_shape=tuple(out_shape),
            scratch_shapes=list(scratch),
            compiler_params=pltpu.CompilerParams(dimension_semantics=semantics, vmem_limit_bytes=vmem_limit),
        )(*args)
        return list(res)
    n_pi, n_po = len(carry.operands), len(carry.out_shapes)

    def full_body(*refs):
        ci, pi = refs[:n_ci], refs[n_ci:n_ci + n_pi]
        o0 = n_ci + n_pi
        co, po = refs[o0:o0 + n_co], refs[o0 + n_co:o0 + n_co + n_po]
        s0 = o0 + n_co + n_po
        cs, ps = refs[s0:s0 + n_cs], refs[s0 + n_cs:]
        ids = [pl.program_id(ax) for ax in range(len(grid))]
        first, last = ids[0] == 0, ids[0] == grid[0] - 1
        for ax in range(1, len(grid)):
            first, last = first & (ids[ax] == 0), last & (ids[ax] == grid[ax] - 1)

        @pl.when(first)
        def _():
            if before_carry is not None:
                before_carry(*ci, *co, *cs)
            carry.start(pi, po, ps)

        several_steps = any(g > 1 for g in grid)
        if several_steps:
            @pl.when(last)
            def _():
                carry.before_last(pi, po, ps)

        body(*ci, *co, *cs)

        @pl.when(last)
        def _():
            if not several_steps:
                carry.before_last(pi, po, ps)
            carry.finish(pi, po, ps)

    res = pl.pallas_call(
        full_body, name=name, grid=grid, in_specs=list(in_specs) + [ANY] * n_pi,
        out_specs=tuple(out_specs) + tuple([ANY] * n_po), out_shape=tuple(out_shape) + tuple(carry.out_shapes),
        scratch_shapes=list(scratch) + carry.sems,
        input_output_aliases={n_ci + i: n_co + o for i, o in carry.aliases.items()},
        compiler_params=pltpu.CompilerParams(dimension_semantics=("arbitrary",) * len(grid),
                                             vmem_limit_bytes=vmem_limit),
    )(*args, *carry.operands)
    _set_results(carry, res[n_co:])
    return list(res[:n_co])


def _comm_call(carry, *, name):
    n_pi, n_po = len(carry.operands), len(carry.out_shapes)

    def body(*refs):
        pi, po, ps = refs[:n_pi], refs[n_pi:n_pi + n_po], refs[n_pi + n_po:]
        carry.start(pi, po, ps)
        carry.before_last(pi, po, ps)
        carry.finish(pi, po, ps)

    res = pl.pallas_call(
        body, name=name, in_specs=[ANY] * n_pi, out_specs=tuple([ANY] * n_po), out_shape=tuple(carry.out_shapes),
        scratch_shapes=carry.sems, input_output_aliases=dict(carry.aliases),
    )(*carry.operands)
    _set_results(carry, res)


def _mm_nn(a, b, *, bm, bn, out_dtype, name, bias=None, carry=None):
    M, K = a.shape
    stacked = b.ndim == 3
    N = b.shape[0] * b.shape[2] if stacked else b.shape[1]
    assert M % bm == 0 and N % bn == 0 and (not stacked or bn == b.shape[2])

    def body(*refs):
        a_ref, b_ref = refs[0], refs[1]
        o_ref = refs[-1]
        acc = jnp.dot(a_ref[...], b_ref[...], preferred_element_type=F32)
        if bias is not None:
            acc = acc + refs[2][...]
        o_ref[...] = acc.astype(o_ref.dtype)

    if stacked:
        b_spec = pl.BlockSpec((None, K, bn), lambda i, j: (j, 0, 0))
    else:
        b_spec = pl.BlockSpec((K, bn), lambda i, j: (0, j))
    in_specs = [pl.BlockSpec((bm, K), lambda i, j: (i, 0)), b_spec]
    args = [a, b]
    if bias is not None:
        in_specs.append(pl.BlockSpec((1, bn), lambda i, j: (0, j)))
        args.append(bias)
    limit = _vmem_limit(_nbytes((bm, K), a.dtype), _nbytes((K, bn), b.dtype), _nbytes((bm, bn), F32))
    return _call(body, name=name, grid=(M // bm, N // bn), in_specs=in_specs,
                 out_specs=[pl.BlockSpec((bm, bn), lambda i, j: (i, j))],
                 out_shape=[jax.ShapeDtypeStruct((M, N), out_dtype)], args=args, vmem_limit=limit, carry=carry)[0]


def _mm_nt(a, b, *, bm, bn, out_dtype, name, add=None, carry=None):
    M, K = a.shape
    stacked = b.ndim == 3
    N = b.shape[1] if stacked else b.shape[0]
    n_sh = b.shape[0] if stacked else 1
    ks = b.shape[2] if stacked else K
    assert M % bm == 0 and N % bn == 0 and n_sh * ks == K
    dn = (((1,), (1,)), ((), ()))

    def body(*refs):
        a_ref, b_ref = refs[0], refs[1]
        o_ref = refs[-1]
        if stacked:
            acc = lax.dot_general(a_ref[:, 0:ks], b_ref[0], dn, preferred_element_type=F32)
            for s in range(1, n_sh):
                acc = acc + lax.dot_general(a_ref[:, s * ks:(s + 1) * ks], b_ref[s], dn, preferred_element_type=F32)
        else:
            acc = lax.dot_general(a_ref[...], b_ref[...], dn, preferred_element_type=F32)
        if add is not None:
            acc = acc + refs[2][...]
        o_ref[...] = acc.astype(o_ref.dtype)

    if stacked:
        b_spec = pl.BlockSpec((n_sh, bn, ks), lambda i, j: (0, j, 0))
    else:
        b_spec = pl.BlockSpec((bn, K), lambda i, j: (j, 0))
    in_specs = [pl.BlockSpec((bm, K), lambda i, j: (i, 0)), b_spec]
    args = [a, b]
    if add is not None:
        rows = bm if add.shape[0] == M else 1
        in_specs.append(pl.BlockSpec((rows, bn), lambda i, j: (i if rows == bm else 0, j)))
        args.append(add)
    limit = _vmem_limit(_nbytes((bm, K), a.dtype), _nbytes((bn, K), b.dtype), _nbytes((bm, bn), F32))
    return _call(body, name=name, grid=(M // bm, N // bn), in_specs=in_specs,
                 out_specs=[pl.BlockSpec((bm, bn), lambda i, j: (i, j))],
                 out_shape=[jax.ShapeDtypeStruct((M, N), out_dtype)], args=args, vmem_limit=limit, carry=carry)[0]


def _mm_tn(a, b, *, bm, bn, out_dtype, name, n_stack=None, carry=None):
    T, M = a.shape
    N = b.shape[1]
    assert M % bm == 0 and N % bn == 0 and (n_stack is None or bn * n_stack == N)
    dn = (((0,), (0,)), ((), ()))

    def body(a_ref, b_ref, o_ref):
        acc = lax.dot_general(a_ref[...], b_ref[...], dn, preferred_element_type=F32)
        o_ref[...] = acc.astype(o_ref.dtype)

    if n_stack is None:
        out_spec = pl.BlockSpec((bm, bn), lambda i, j: (i, j))
        out_shape = jax.ShapeDtypeStruct((M, N), out_dtype)
    else:
        out_spec = pl.BlockSpec((None, bm, bn), lambda i, j: (j, i, 0))
        out_shape = jax.ShapeDtypeStruct((n_stack, M, bn), out_dtype)
    limit = _vmem_limit(_nbytes((T, bm), a.dtype), _nbytes((T, bn), b.dtype), _nbytes((bm, bn), F32))
    return _call(body, name=name, grid=(M // bm, N // bn),
                 in_specs=[pl.BlockSpec((T, bm), lambda i, j: (0, i)), pl.BlockSpec((T, bn), lambda i, j: (0, j))],
                 out_specs=[out_spec], out_shape=[out_shape], args=[a, b], vmem_limit=limit, carry=carry)[0]


ROW_BLK = 256


def _rstd(x):
    return lax.rsqrt(jnp.mean(x * x, axis=-1, keepdims=True) + EPS)


def _norm_bwd(xhat, rstd, dxhat):
    return rstd * (dxhat - xhat * jnp.mean(dxhat * xhat, axis=-1, keepdims=True))


def _row_spec(cols):
    return pl.BlockSpec((ROW_BLK, cols), lambda i: (i, 0))


def _vec_spec(cols):
    return pl.BlockSpec((1, cols), lambda i: (0, 0))


def _prologue(x, g, rel_bias, idx_all, *, name, carry=None):
    T, D = x.shape
    n_pat = idx_all.shape[0]
    heads = 2 * N_PAIRS
    steps = n_pat * heads
    rows = T // steps

    def body(rb_ref, x_ref, g_ref, idx_ref, n_ref, bias_ref):
        s = pl.program_id(0)
        head = jnp.where(s < heads, s, heads + s % heads)
        xv = x_ref[...]
        n_ref[...] = (xv * _rstd(xv) * g_ref[...]).astype(n_ref.dtype)
        idxv = idx_ref[...]
        acc = jnp.where(idxv < 0, NEG, 0.0).astype(F32)
        for b in range(NUM_BUCKETS):
            acc = acc + jnp.where(idxv == b, rb_ref[b, head], 0.0)
        bias_ref[0] = acc
        kap = lax.broadcasted_iota(jnp.int32, (1, 2 * QBLK), 1)
        bias_ref[1] = jnp.where(kap < QBLK, NEG, acc)

    return _call(
        body, name=name, grid=(steps,),
        in_specs=[pl.BlockSpec(memory_space=pltpu.SMEM), pl.BlockSpec((rows, D), lambda s: (s, 0)),
                  pl.BlockSpec((1, D), lambda s: (0, 0)),
                  pl.BlockSpec((None, QBLK, 2 * QBLK), lambda s: (s // heads, 0, 0))],
        out_specs=[pl.BlockSpec((rows, D), lambda s: (s, 0)),
                   pl.BlockSpec((None, 2, None, QBLK, 2 * QBLK), lambda s: (s // heads, 0, s % heads, 0, 0))],
        out_shape=[jax.ShapeDtypeStruct((T, D), BF16),
                   jax.ShapeDtypeStruct((n_pat, 2, heads, QBLK, 2 * QBLK), F32)],
        args=[rel_bias, x, g, idx_all], carry=carry)


def _mm_resid_norm(a, b, h, g_post, coef, g_next, *, bm, name, bias=None, carry=None):
    M, K = a.shape
    N = b.shape[1]

    def body(*refs):
        a_ref, b_ref, h_ref, gp_ref, gn_ref = refs[0:5]
        f_ref, hn_ref, n_ref = refs[-3:]
        for rows in _row_halves(bm):
            f = jnp.dot(a_ref[rows, :], b_ref[...], preferred_element_type=F32)
            if bias is not None:
                f = f + refs[5][...]
            f_ref[rows, :] = f
            hn = h_ref[rows, :] + coef * (f * _rstd(f) * gp_ref[...])
            hn_ref[rows, :] = hn
            n_ref[rows, :] = (hn * _rstd(hn) * gn_ref[...]).astype(n_ref.dtype)

    row = lambda w: pl.BlockSpec((bm, w), lambda i: (i, 0))
    vec = pl.BlockSpec((1, N), lambda i: (0, 0))
    in_specs = [row(K), pl.BlockSpec((K, N), lambda i: (0, 0), pipeline_mode=pl.Buffered(1)), row(N), vec, vec]
    args = [a, b, h, g_post, g_next]
    if bias is not None:
        in_specs.append(vec)
        args.append(bias)
    limit = _vmem_limit(_nbytes((bm, K), a.dtype), _nbytes((K, N), b.dtype) // 2, 4 * _nbytes((bm, N), F32))
    return _call(body, name=name, grid=(M // bm,), in_specs=in_specs, out_specs=[row(N), row(N), row(N)],
                 out_shape=[jax.ShapeDtypeStruct((M, N), F32), jax.ShapeDtypeStruct((M, N), F32),
                            jax.ShapeDtypeStruct((M, N), BF16)], args=args, vmem_limit=limit, carry=carry)


def _mm_nt_norms_bwd(a, b, dh_in, h, g_pre, f, g_post, coef, *, bm, name, add=None, b_is_kn=False, carry=None):
    M, K = a.shape
    stacked = b.ndim == 3
    N = b.shape[1] if (stacked or b_is_kn) else b.shape[0]
    n_sh = b.shape[0] if stacked else 1
    ks = b.shape[2] if stacked else K
    assert n_sh * ks == K
    dn_dims = (((1,), (0,)), ((), ())) if b_is_kn else (((1,), (1,)), ((), ()))
    with_f = f is not None
    n_in = 5 + (2 if with_f else 0) + (1 if add is not None else 0)

    def body(*refs):
        a_ref, b_ref, dhi_ref, h_ref, gpre_ref = refs[0:5]
        outs = refs[n_in:]

        @pl.when(pl.program_id(0) == 0)
        def _():
            for acc in (outs[2:] if with_f else outs[1:]):
                acc[...] = jnp.zeros_like(acc)

        for rows in _row_halves(bm):
            if stacked:
                dn = lax.dot_general(a_ref[rows, 0:ks], b_ref[0], dn_dims, preferred_element_type=F32)
                for s in range(1, n_sh):
                    dn = dn + lax.dot_general(a_ref[rows, s * ks:(s + 1) * ks], b_ref[s], dn_dims,
                                              preferred_element_type=F32)
            else:
                dn = lax.dot_general(a_ref[rows, :], b_ref[...], dn_dims, preferred_element_type=F32)
            if add is not None:
                dn = dn + refs[n_in - 1][rows, :]
            hv = h_ref[rows, :]
            r = _rstd(hv)
            hh = hv * r
            dh = dhi_ref[rows, :] + _norm_bwd(hh, r, dn * gpre_ref[...])
            outs[0][rows, :] = dh
            if not with_f:
                outs[1][...] += jnp.sum(dn * hh, axis=0, keepdims=True)
                continue
            f_ref, gpost_ref = refs[5], refs[6]
            df_ref, dgpre_ref, dgpost_ref, dsum_ref = outs[1:]
            dgpre_ref[...] += jnp.sum(dn * hh, axis=0, keepdims=True)
            fv = f_ref[rows, :]
            rf = _rstd(fv)
            fh = fv * rf
            dy = coef * dh
            dgpost_ref[...] += jnp.sum(dy * fh, axis=0, keepdims=True)
            df = _norm_bwd(fh, rf, dy * gpost_ref[...])
            dsum_ref[...] += jnp.sum(df, axis=0, keepdims=True)
            df_ref[rows, :] = df.astype(df_ref.dtype)

    row = lambda w: pl.BlockSpec((bm, w), lambda i: (i, 0))
    vec = pl.BlockSpec((1, N), lambda i: (0, 0))
    once = pl.Buffered(1)
    if stacked:
        b_spec = pl.BlockSpec((n_sh, N, ks), lambda i: (0, 0, 0), pipeline_mode=once)
    else:
        b_spec = pl.BlockSpec(b.shape, lambda i: (0, 0), pipeline_mode=once)
    in_specs = [row(K), b_spec, row(N), row(N), vec]
    args = [a, b, dh_in, h, g_pre]
    if with_f:
        in_specs += [row(N), vec]
        args += [f, g_post]
    if add is not None:
        in_specs.append(row(N))
        args.append(add)
    vec_shape = jax.ShapeDtypeStruct((1, N), F32)
    out_specs = [row(N)] + ([row(N), vec, vec, vec] if with_f else [vec])
    out_shape = [jax.ShapeDtypeStruct((M, N), F32)]
    out_shape += [jax.ShapeDtypeStruct((M, N), BF16), vec_shape, vec_shape, vec_shape] if with_f else [vec_shape]
    limit = _vmem_limit(_nbytes((bm, K), a.dtype), _nbytes((N, K), b.dtype) // 2, 6 * _nbytes((bm, N), F32))
    return _call(body, name=name, grid=(M // bm,), in_specs=in_specs, out_specs=out_specs, out_shape=out_shape,
                 args=args, vmem_limit=limit, semantics=("arbitrary",), carry=carry)


def _ffn_up(n, w_gu, *, bm, name, carry=None):
    M, K = n.shape
    n_sh, _, ns = w_gu.shape
    half = n_sh // 2

    def body(n_ref, wg_ref, wu_ref, g_ref, u_ref, a_ref):
        nv = n_ref[...]
        for cols in _col_chunks(ns):
            g = jnp.dot(nv, wg_ref[:, cols], preferred_element_type=F32)
            u = jnp.dot(nv, wu_ref[:, cols], preferred_element_type=F32)
            g_ref[:, cols] = g.astype(g_ref.dtype)
            u_ref[:, cols] = u.astype(u_ref.dtype)
            a_ref[:, cols] = (g * jax.nn.sigmoid(g) * u).astype(a_ref.dtype)

    out_spec = pl.BlockSpec((bm, ns), lambda i, j: (i, j))
    out = jax.ShapeDtypeStruct((M, half * ns), BF16)
    limit = _vmem_limit(_nbytes((bm, K), n.dtype), 2 * _nbytes((K, ns), w_gu.dtype), 3 * _nbytes((bm, ns), F32))
    return _call(body, name=name, grid=(M // bm, half),
                 in_specs=[pl.BlockSpec((bm, K), lambda i, j: (i, 0)),
                           pl.BlockSpec((None, K, ns), lambda i, j: (j, 0, 0)),
                           pl.BlockSpec((None, K, ns), lambda i, j: (j + half, 0, 0))],
                 out_specs=[out_spec, out_spec, out_spec], out_shape=[out, out, out], args=[n, w_gu, w_gu],
                 vmem_limit=limit, carry=carry)


def _ffn_down_bwd(df, w_down, g, u, *, bm, name, carry=None):
    M, D = df.shape
    F = w_down.shape[0]
    dn = (((1,), (1,)), ((), ()))

    steps = M // bm
    assert steps >= RING_SLOTS - 1

    def body(df_ref, w_ref, g_hbm, u_hbm, o_ref, g_buf, u_buf, sems):
        i = pl.program_id(0)

        def copies(step, slot):
            return [pltpu.make_async_copy(src.at[pl.ds(step * bm, bm)], buf.at[slot], sems.at[k, slot])
                    for k, (src, buf) in enumerate(((g_hbm, g_buf), (u_hbm, u_buf)))]

        @pl.when(i == 0)
        def _():
            for s in range(RING_SLOTS - 1):
                for cp in copies(s, s):
                    cp.start()

        ahead = i + (RING_SLOTS - 1)

        @pl.when(ahead < steps)
        def _():
            for cp in copies(ahead, ahead % RING_SLOTS):
                cp.start()

        slot = i % RING_SLOTS
        for cp in copies(i, slot):
            cp.wait()
        g_ref, u_ref = g_buf.at[slot], u_buf.at[slot]
        dfv = df_ref[...]
        for cols in _col_chunks(F):
            da = lax.dot_general(dfv, w_ref[cols, :], dn, preferred_element_type=F32)
            gv = g_ref[:, cols].astype(F32)
            s = jax.nn.sigmoid(gv)
            t = gv * s
            o_ref[:, cols] = (da * u_ref[:, cols].astype(F32) * (s + t * (1.0 - s))).astype(o_ref.dtype)
            o_ref[:, F + cols.start:F + cols.stop] = (da * t).astype(o_ref.dtype)

    row = lambda w: pl.BlockSpec((bm, w), lambda i: (i, 0))
    limit = _vmem_limit(_nbytes((F, D), w_down.dtype) // 2, RING_SLOTS * _nbytes((bm, F), BF16), _nbytes((bm, 2 * F), BF16))
    return _call(body, name=name, grid=(steps,),
                 in_specs=[row(D), pl.BlockSpec((F, D), lambda i: (0, 0), pipeline_mode=pl.Buffered(1)), ANY, ANY],
                 out_specs=[row(2 * F)], out_shape=[jax.ShapeDtypeStruct((M, 2 * F), BF16)],
                 args=[df, w_down, g, u], vmem_limit=limit, carry=carry, semantics=("arbitrary",),
                 scratch=[pltpu.VMEM((RING_SLOTS, bm, F), g.dtype), pltpu.VMEM((RING_SLOTS, bm, F), u.dtype),
                          pltpu.SemaphoreType.DMA((2, RING_SLOTS))])[0]


def _ple_head(n4, w_gate, p, w_ple, h3, g_post, target, *, bm, name):
    T, D = h3.shape
    kp = p.shape[1]

    def body(n_ref, wg_ref, p_ref, wp_ref, h_ref, g_ref, t_ref, dh_ref, de_ref, dgp_ref, loss_ref, dg_ref):
        @pl.when(pl.program_id(0) == 0)
        def _():
            loss_ref[...] = jnp.zeros_like(loss_ref)
            dg_ref[...] = jnp.zeros_like(dg_ref)

        gpost = g_ref[...]
        for rows in _row_halves(bm):
            gate = jax.nn.sigmoid(jnp.dot(n_ref[rows, :], wg_ref[...], preferred_element_type=F32))
            ev = jnp.dot(p_ref[rows, :], wp_ref[...], preferred_element_type=F32)
            ge = gate * ev
            r = _rstd(ge)
            gh = ge * r
            diff = h_ref[rows, :] + gh * gpost - t_ref[rows, :]
            loss_ref[...] += jnp.sum(diff * diff) * (0.5 / D)
            dh = diff * (1.0 / D)
            dh_ref[rows, :] = dh
            dg_ref[...] += jnp.sum(dh * gh, axis=0, keepdims=True)
            dge = _norm_bwd(gh, r, dh * gpost)
            de_ref[rows, :] = (dge * gate).astype(de_ref.dtype)
            dgp_ref[rows, :] = (dge * ev * gate * (1.0 - gate)).astype(dgp_ref.dtype)

    row = lambda w: pl.BlockSpec((bm, w), lambda i: (i, 0))
    whole = lambda a: pl.BlockSpec(a.shape, lambda i: (0, 0), pipeline_mode=pl.Buffered(1))
    limit = _vmem_limit(_nbytes((bm, D), BF16), _nbytes(w_gate.shape, w_gate.dtype) // 2, 6 * _nbytes((bm, D), F32))
    return pl.pallas_call(
        body, name=name, grid=(T // bm,),
        in_specs=[row(D), whole(w_gate), row(kp), whole(w_ple), row(D), _vec_spec(D), row(D)],
        out_specs=(row(D), row(D), row(D), _vec_spec(LANES), _vec_spec(D)),
        out_shape=(jax.ShapeDtypeStruct((T, D), F32), jax.ShapeDtypeStruct((T, D), BF16),
                   jax.ShapeDtypeStruct((T, D), BF16), jax.ShapeDtypeStruct((1, LANES), F32),
                   jax.ShapeDtypeStruct((1, D), F32)),
        compiler_params=pltpu.CompilerParams(dimension_semantics=("arbitrary",), vmem_limit_bytes=limit),
    )(n4, w_gate, p, w_ple, h3, g_post, target)


def _t5_index(max_dist, stride):
    rho = np.arange(QBLK)[:, None]
    kap = np.arange(2 * QBLK)[None, :]
    d = rho + QBLK - kap
    valid = (d >= 0) & (d <= max_dist)
    n = np.maximum(d, 0) * stride
    max_exact = NUM_BUCKETS // 2
    nf = np.maximum(n, 1).astype(np.float32)
    large = max_exact + (np.log(nf / np.float32(max_exact)) / np.float32(math.log(MAX_DISTANCE / max_exact))
                         * np.float32(NUM_BUCKETS - max_exact)).astype(np.int32)
    large = np.minimum(large, NUM_BUCKETS - 1)
    bucket = np.where(n < max_exact, n, large)
    return np.where(valid, bucket, -1).astype(np.int32)


def _bias_grad(d_bias, idx, *, name):
    heads = 2 * N_PAIRS

    def body(db_ref, idx_ref, o_ref, part_ref):
        idxv = idx_ref[...]
        for b in range(NUM_BUCKETS):
            in_bucket = idxv == b
            for h in range(heads):
                masked = jnp.where(in_bucket, db_ref[h], 0.0)
                part_ref[h, b] = jnp.sum(masked.reshape(QBLK // SUBLANES, SUBLANES, 2 * QBLK), axis=0)
        lanes = lax.broadcasted_iota(jnp.int32, (NUM_BUCKETS, LANES), 1)
        acc = jnp.zeros((NUM_BUCKETS, LANES), F32)
        for h in range(heads):
            sums = jnp.sum(jnp.sum(part_ref[h], axis=1), axis=1, keepdims=True)
            acc = acc + jnp.where(lanes == h, sums, 0.0)
        o_ref[...] = acc

    return pl.pallas_call(
        body, name=name, grid=(1,),
        in_specs=[pl.BlockSpec((heads, QBLK, 2 * QBLK), lambda i: (0, 0, 0)),
                  pl.BlockSpec((QBLK, 2 * QBLK), lambda i: (0, 0))],
        out_specs=pl.BlockSpec((NUM_BUCKETS, LANES), lambda i: (0, 0)),
        out_shape=jax.ShapeDtypeStruct((NUM_BUCKETS, LANES), F32),
        scratch_shapes=[pltpu.VMEM((heads, NUM_BUCKETS, SUBLANES, 2 * QBLK), F32)],
        compiler_params=pltpu.CompilerParams(dimension_semantics=("arbitrary",)),
    )(d_bias, idx)


def _lane():
    return lax.broadcasted_iota(jnp.int32, (1, LANES), 1)


def _head_sel(h):
    return (_lane() < HEAD_DIM) if h == 0 else (_lane() >= HEAD_DIM)


def _stat_lo():
    return (_lane() % HEAD_DIM) < (HEAD_DIM // 2)


def _stack_heads(t, scale=None):
    if scale is not None:
        t = t * scale
    zero = jnp.zeros_like(t)
    return jnp.concatenate([jnp.where(_head_sel(0), t, zero), jnp.where(_head_sel(1), t, zero)], axis=0)


def _class_spec(L, r, cols, stride, pps):
    chunks = N_PAIRS // pps
    mode = pl.Buffered(1) if r * chunks == 1 else None
    return pl.BlockSpec((L, MIX_W // chunks), lambda j, c: (0, (cols + j * stride) * chunks + c), pipeline_mode=mode)


def _rows(b, n_blocks=1):
    return pl.ds(pl.multiple_of(b * QBLK, QBLK), n_blocks * QBLK)


def _key_window(ref, b, cols, first, kv_head=None):
    if kv_head is not None:
        cols = slice(0, LANES)
    if first:
        blk = ref[0:QBLK, cols]
        tile = jnp.concatenate([blk, blk], axis=0)
    else:
        tile = ref[_rows(b - 1, 2), cols]
    if kv_head is None:
        return tile
    wide = tile.astype(F32)
    keep = jnp.logical_xor(_lane() < HEAD_DIM, kv_head == 1)
    return jnp.where(keep, wide, pltpu.roll(wide, HEAD_DIM, 1)).astype(tile.dtype)


def _kv_spec(L, r, col, stride, pps, kv_shared):
    if not kv_shared:
        return _class_spec(L, r, col, stride, pps)
    mode = pl.Buffered(1) if pps == N_PAIRS else None
    return pl.BlockSpec((L, LANES), lambda j, c: (0, col), pipeline_mode=mode)


def _band_fwd(qa, ka, va, bias, *, r, q_col, k_col, v_col, stride, pattern, name, kv_shared=False, sink=None,
              carry=None):
    L = qa.shape[0]
    nb = L // QBLK
    dn = (((1,), (1,)), ((), ()))
    scale = HEAD_DIM ** -0.5

    pps = N_PAIRS

    def body(q_ref, k_ref, v_ref, bias_ref, *rest):
        sel0 = _head_sel(0)
        pair0 = pl.program_id(1) * pps

        def emit(rows, cols, o, lse):
            if sink is None:
                o_ref, lse_ref = rest
                o_ref[rows, cols] = o.astype(o_ref.dtype)
                lse_ref[rows, cols] = lse
                return
            sink_ref, out_ref, outb_ref, lse_ref = rest
            sk = sink_ref[:, cols]
            mx = jnp.maximum(lse, sk)
            w = jnp.exp(lse - mx)
            den = w + jnp.exp(sk - mx)
            out = o * (w / den)
            out_ref[rows, cols] = out
            outb_ref[rows, cols] = out.astype(outb_ref.dtype)
            lse_ref[rows, cols] = mx + jnp.log(den)

        def block(b, first):
            rows = _rows(b)
            for i in range(pps):
                cols = slice(i * LANES, (i + 1) * LANES)
                q2 = _stack_heads(q_ref[rows, cols], scale)
                kv_head = (pair0 + i) // 2 if kv_shared else None
                k = _key_window(k_ref, b, cols, first, kv_head)
                v = _key_window(v_ref, b, cols, first, kv_head)
                bias2 = bias_ref[1 if first else 0, pl.ds(2 * (pair0 + i), 2)].reshape(2 * QBLK, 2 * QBLK)
                s = lax.dot_general(q2, k, dn, preferred_element_type=F32) + bias2
                m = jnp.max(s, axis=1, keepdims=True)
                p = jnp.exp(s - m)
                l = jnp.sum(p, axis=1, keepdims=True)
                o = jnp.dot(p.astype(v.dtype), v, preferred_element_type=F32) * (1.0 / l)
                lse = m + jnp.log(l)
                emit(rows, cols, jnp.where(sel0, o[0:QBLK], o[QBLK:]), jnp.where(sel0, lse[0:QBLK], lse[QBLK:]))

        block(0, True)
        if nb > 1:
            pl.loop(1, nb)(lambda b: block(b, False))

    bias_spec = pl.BlockSpec((None, 2, 2 * N_PAIRS, QBLK, 2 * QBLK), lambda j, c: (pattern, 0, 0, 0, 0))
    out_spec = _class_spec(L, r, 0, 1, pps)
    blk_bytes = _nbytes((L, pps * LANES), F32)
    in_specs = [_class_spec(L, r, q_col, stride, pps), _kv_spec(L, r, k_col, stride, pps, kv_shared),
                _kv_spec(L, r, v_col, stride, pps, kv_shared), bias_spec]
    args = [qa, ka, va, bias]
    f32_out, bf16_out = jax.ShapeDtypeStruct((L, r * MIX_W), F32), jax.ShapeDtypeStruct((L, r * MIX_W), BF16)
    out_shape = [bf16_out, f32_out]
    if sink is not None:
        in_specs.append(pl.BlockSpec((1, MIX_W), lambda j, c: (0, 0)))
        args.append(sink)
        out_shape = [f32_out, bf16_out, f32_out]
    return _call(body, name=name, grid=(r, N_PAIRS // pps), in_specs=in_specs, out_specs=[out_spec] * len(out_shape),
                 out_shape=out_shape, args=args, vmem_limit=_vmem_limit(*([blk_bytes] * 4)), carry=carry)


def _class_rows_spec(rows, r, width):
    return pl.BlockSpec((rows // r, r * width), lambda i, *_: (i, 0))


def _token_scratch(rows, width):
    return pltpu.VMEM((width // LANES, rows, LANES), F32)


def _fill_tokens(scratch, value):
    for c in range(scratch.shape[0]):
        scratch[c] = value[:, c * LANES:(c + 1) * LANES]


def _read_tokens(scratch):
    return jnp.concatenate([scratch[c] for c in range(scratch.shape[0])], axis=1)


def _to_tokens(blk_ref, r, scratch):
    if r == 1:
        return blk_ref[...].astype(F32)
    nt, rows, _ = scratch.shape
    for j in range(r):
        for c in range(nt):
            lanes = slice((j * nt + c) * LANES, (j * nt + c + 1) * LANES)
            scratch.at[c][pl.ds(j, rows // r, stride=r), :] = blk_ref[:, lanes].astype(F32)
    return _read_tokens(scratch)


def _from_tokens(dst_ref, r, scratch):
    nt, rows, _ = scratch.shape
    if r == 1:
        dst_ref[...] = _read_tokens(scratch).astype(dst_ref.dtype)
        return
    for j in range(r):
        for c in range(nt):
            lanes = slice((j * nt + c) * LANES, (j * nt + c + 1) * LANES)
            dst_ref[:, lanes] = scratch.at[c][pl.ds(j, rows // r, stride=r), :].astype(dst_ref.dtype)


def _mm_nt_classes(a, b, bias, dils, *, bm, name):
    M, K = a.shape
    N = b.shape[0]
    dn = (((1,), (1,)), ((), ()))

    def body(a_ref, b_ref, bias_ref, *rest):
        outs, tile = rest[:-1], rest[-1]
        _fill_tokens(tile, lax.dot_general(a_ref[...], b_ref[...], dn, preferred_element_type=F32) + bias_ref[...])
        for out, r in zip(outs, dils):
            _from_tokens(out, r, tile)

    limit = _vmem_limit(_nbytes((bm, K), a.dtype), _nbytes((K, N), b.dtype) // 2, (1 + len(dils)) * _nbytes((bm, N), F32))
    return pl.pallas_call(
        body, name=name, grid=(M // bm,),
        in_specs=[pl.BlockSpec((bm, K), lambda i: (i, 0)),
                  pl.BlockSpec((N, K), lambda i: (0, 0), pipeline_mode=pl.Buffered(1)),
                  pl.BlockSpec((1, N), lambda i: (0, 0))],
        out_specs=tuple(_class_rows_spec(bm, r, N) for r in dils),
        out_shape=tuple(jax.ShapeDtypeStruct((M // r, r * N), BF16) for r in dils),
        scratch_shapes=[_token_scratch(bm, N)],
        compiler_params=pltpu.CompilerParams(dimension_semantics=("parallel",), vmem_limit_bytes=limit),
    )(a, b, bias)


def _merge(branches, dils, sink, *, name):
    W = MIX_W
    T = branches[0][0].shape[0] * dils[0]
    nbr = len(branches)

    def body(*refs):
        sink_ref = refs[2 * nbr]
        out_ref, outb_ref, lse_ref, scratch = refs[2 * nbr + 1:]
        sk = sink_ref[...]
        lses = [_to_tokens(refs[2 * t + 1], dils[t], scratch) for t in range(nbr)]
        mx = sk
        for lse in lses:
            mx = jnp.maximum(mx, lse)
        den = jnp.exp(sk - mx)
        num = jnp.zeros_like(mx)
        for t in range(nbr):
            w = jnp.exp(lses[t] - mx)
            den = den + w
            num = num + w * _to_tokens(refs[2 * t], dils[t], scratch)
        out = num / den
        out_ref[...] = out
        outb_ref[...] = out.astype(outb_ref.dtype)
        lse_ref[...] = mx + jnp.log(den)

    args = [a for br in branches for a in br] + [sink]
    in_specs = [_class_rows_spec(ROW_BLK, r, W) for r in dils for _ in range(2)] + [_vec_spec(W)]
    return pl.pallas_call(
        body, name=name, grid=(T // ROW_BLK,), in_specs=in_specs,
        out_specs=(_row_spec(W), _row_spec(W), _row_spec(W)),
        out_shape=(jax.ShapeDtypeStruct((T, W), F32), jax.ShapeDtypeStruct((T, W), BF16),
                   jax.ShapeDtypeStruct((T, W), F32)),
        scratch_shapes=[_token_scratch(ROW_BLK, W)],
        compiler_params=pltpu.CompilerParams(dimension_semantics=("parallel",)),
    )(*args)


def _attn_delta(dmix, outs, lses, sink, dils, *, name):
    T = dmix.shape[0]
    W = MIX_W
    nd = len(dils)

    def body(dmix_ref, oa_ref, ob_ref, la_ref, lb_ref, sink_ref, *rest):
        doa_ref, sta_ref, dsink_ref = rest[0:3]
        dob_refs, stb_refs = rest[3:3 + nd], rest[3 + nd:3 + 2 * nd]
        do_tile, st_tile = rest[3 + 2 * nd:]

        @pl.when(pl.program_id(0) == 0)
        def _():
            dsink_ref[...] = jnp.zeros_like(dsink_ref)

        sel0, lo = _head_sel(0), _stat_lo()
        for mixer, (o_ref, l_ref) in enumerate(((oa_ref, la_ref), (ob_ref, lb_ref))):
            for i in range(N_PAIRS):
                cols = slice(i * LANES, (i + 1) * LANES)
                do = dmix_ref[:, mixer * W + i * LANES:mixer * W + (i + 1) * LANES]
                prod = do * o_ref[:, cols]
                d0 = jnp.sum(jnp.where(sel0, prod, 0.0), axis=1, keepdims=True)
                d1 = jnp.sum(jnp.where(sel0, 0.0, prod), axis=1, keepdims=True)
                delta = jnp.where(sel0, d0, d1)
                lse = l_ref[:, cols]
                stat = jnp.where(lo, lse, delta)
                if mixer == 0:
                    sta_ref[:, cols] = stat
                    doa_ref[:, cols] = do.astype(doa_ref.dtype)
                    dsink_ref[:, cols] += -jnp.sum(jnp.exp(sink_ref[:, cols] - lse) * delta, axis=0, keepdims=True)
                else:
                    st_tile[i] = stat
                    do_tile[i] = do
        for t, r in enumerate(dils):
            _from_tokens(dob_refs[t], r, do_tile)
            _from_tokens(stb_refs[t], r, st_tile)

    class_specs = [_class_rows_spec(ROW_BLK, r, W) for r in dils]
    out_shape = [jax.ShapeDtypeStruct((T, W), BF16), jax.ShapeDtypeStruct((T, W), F32), jax.ShapeDtypeStruct((1, W), F32)]
    out_shape += [jax.ShapeDtypeStruct((T // r, r * W), BF16) for r in dils]
    out_shape += [jax.ShapeDtypeStruct((T // r, r * W), F32) for r in dils]
    res = pl.pallas_call(
        body, name=name, grid=(T // ROW_BLK,),
        in_specs=[_row_spec(2 * W), _row_spec(W), _row_spec(W), _row_spec(W), _row_spec(W), _vec_spec(W)],
        out_specs=tuple([_row_spec(W), _row_spec(W), _vec_spec(W)] + class_specs + class_specs),
        out_shape=tuple(out_shape),
        scratch_shapes=[_token_scratch(ROW_BLK, W), _token_scratch(ROW_BLK, W)],
        compiler_params=pltpu.CompilerParams(dimension_semantics=("arbitrary",)),
    )(dmix, outs[0], outs[1], lses[0], lses[1], sink)
    return res[0], res[1], res[2], res[3:3 + nd], res[3 + nd:]


def _band_bwd(qa, ka, va, d_out, stat, bias, *, r, q_col, k_col, v_col, stride, pattern, name, kv_shared=False,
              carry=None):
    L = qa.shape[0]
    nb = L // QBLK
    dn_nt = (((1,), (1,)), ((), ()))
    dn_tn = (((0,), (0,)), ((), ()))
    scale = HEAD_DIM ** -0.5

    pps = N_PAIRS if r > 1 else N_PAIRS // 2

    def body(q_ref, k_ref, v_ref, do_ref, st_ref, bias_ref, dq_ref, dk_ref, dv_ref, db_ref, dk_carry, dv_carry):
        @pl.when((pl.program_id(0) == 0) & (pl.program_id(1) == 0))
        def _():
            db_ref[...] = jnp.zeros_like(db_ref)

        lo, sel0 = _stat_lo(), _head_sel(0)
        pair0 = pl.program_id(1) * pps

        def block(b, first):
            rows = _rows(b)
            for i in range(pps):
                heads = pl.ds(2 * (pair0 + i), 2)
                cols = slice(i * LANES, (i + 1) * LANES)
                q2 = _stack_heads(q_ref[rows, cols], scale)
                kv_head = (pair0 + i) // 2 if kv_shared else None
                k = _key_window(k_ref, b, cols, first, kv_head)
                v = _key_window(v_ref, b, cols, first, kv_head)
                do2 = _stack_heads(do_ref[rows, cols])
                st = st_ref[rows, cols]
                lse2 = jnp.concatenate(
                    [jnp.max(jnp.where(_head_sel(h) & lo, st, -jnp.inf), axis=1, keepdims=True) for h in range(2)], axis=0)
                delta2 = jnp.concatenate(
                    [jnp.sum(jnp.where(_head_sel(h) & ~lo, st, 0.0), axis=1, keepdims=True) for h in range(2)],
                    axis=0) * (2.0 / HEAD_DIM)
                bias2 = bias_ref[1 if first else 0, heads].reshape(2 * QBLK, 2 * QBLK)
                s = lax.dot_general(q2, k, dn_nt, preferred_element_type=F32) + bias2
                p = jnp.exp(s - lse2)
                dp = lax.dot_general(do2, v, dn_nt, preferred_element_type=F32)
                ds = p * (dp - delta2)
                db_ref[heads] += ds.reshape(2, QBLK, 2 * QBLK)
                dsb = ds.astype(k.dtype)
                dq2 = jnp.dot(dsb, k, preferred_element_type=F32)
                dq_ref[rows, cols] = (jnp.where(sel0, dq2[0:QBLK], dq2[QBLK:]) * scale).astype(dq_ref.dtype)
                dk_acc = lax.dot_general(dsb, q2, dn_tn, preferred_element_type=F32)
                dv_acc = lax.dot_general(p.astype(k.dtype), do2, dn_tn, preferred_element_type=F32)
                if not first:
                    prev = _rows(b - 1)
                    dk_ref[prev, cols] = (dk_carry[:, cols] + dk_acc[0:QBLK]).astype(dk_ref.dtype)
                    dv_ref[prev, cols] = (dv_carry[:, cols] + dv_acc[0:QBLK]).astype(dv_ref.dtype)
                dk_carry[:, cols] = dk_acc[QBLK:2 * QBLK]
                dv_carry[:, cols] = dv_acc[QBLK:2 * QBLK]

        block(0, True)
        if nb > 1:
            pl.loop(1, nb)(lambda b: block(b, False))

        last = _rows(nb - 1)
        dk_ref[last, :] = dk_carry[...].astype(dk_ref.dtype)
        dv_ref[last, :] = dv_carry[...].astype(dv_ref.dtype)

    tok_spec = _class_spec(L, r, 0, 1, pps)
    bias_spec = pl.BlockSpec((None, 2, 2 * N_PAIRS, QBLK, 2 * QBLK), lambda j, c: (pattern, 0, 0, 0, 0))
    dbias_spec = pl.BlockSpec((2 * N_PAIRS, QBLK, 2 * QBLK), lambda j, c: (0, 0, 0))
    grad = jax.ShapeDtypeStruct((L, r * MIX_W), BF16)
    blk_bytes = _nbytes((L, pps * LANES), BF16)
    carry_shape = pltpu.VMEM((QBLK, pps * LANES), F32)
    return _call(
        body, name=name, grid=(r, N_PAIRS // pps),
        in_specs=[_class_spec(L, r, q_col, stride, pps), _kv_spec(L, r, k_col, stride, pps, kv_shared),
                  _kv_spec(L, r, v_col, stride, pps, kv_shared), tok_spec, tok_spec, bias_spec],
        out_specs=[tok_spec, tok_spec, tok_spec, dbias_spec],
        out_shape=[grad, grad, grad, jax.ShapeDtypeStruct((2 * N_PAIRS, QBLK, 2 * QBLK), F32)],
        args=[qa, ka, va, d_out, stat, bias], scratch=[carry_shape, carry_shape],
        vmem_limit=_vmem_limit(*([blk_bytes] * 9)), semantics=("arbitrary", "arbitrary"), carry=carry)


def _dz_assemble(dq_a, dk_a, dv_a, b_grads, dils, *, name):
    T = dq_a.shape[0]
    W = MIX_W

    def group_sum(x):
        u0 = x[:, 0:LANES] + x[:, LANES:2 * LANES]
        u1 = x[:, 2 * LANES:3 * LANES] + x[:, 3 * LANES:4 * LANES]
        s0 = u0 + pltpu.roll(u0, HEAD_DIM, 1)
        s1 = u1 + pltpu.roll(u1, HEAD_DIM, 1)
        return jnp.where(_head_sel(0), s0, s1)

    def body(*refs):
        dqa_ref, dka_ref, dva_ref = refs[0:3]
        b_refs = refs[3:12]
        dza_ref, dzb_ref, sa_ref, sb_ref, scratch = refs[12:]

        @pl.when(pl.program_id(0) == 0)
        def _():
            sa_ref[...] = jnp.zeros_like(sa_ref)
            sb_ref[...] = jnp.zeros_like(sb_ref)

        def put(dst, acc, col, part):
            w = part.shape[1]
            dst[:, col:col + w] = part.astype(dst.dtype)
            acc[:, col:col + w] += jnp.sum(part, axis=0, keepdims=True)

        put(dza_ref, sa_ref, 0, dqa_ref[...].astype(F32))
        put(dza_ref, sa_ref, W, group_sum(dka_ref[...].astype(F32)))
        put(dza_ref, sa_ref, W + LANES, group_sum(dva_ref[...].astype(F32)))
        for t in range(3):
            part = _to_tokens(b_refs[t], dils[0], scratch)
            for pat in range(1, len(dils)):
                part = part + _to_tokens(b_refs[3 * pat + t], dils[pat], scratch)
            put(dzb_ref, sb_ref, t * W, part)

    args = [dq_a, dk_a, dv_a] + [g[t] for g in b_grads for t in range(3)]
    return pl.pallas_call(
        body, name=name, grid=(T // ROW_BLK,),
        in_specs=[_row_spec(W)] * 3 + [_class_rows_spec(ROW_BLK, r, W) for r in dils for _ in range(3)],
        out_specs=(_row_spec(ZA_W), _row_spec(ZB_W), _vec_spec(ZA_W), _vec_spec(ZB_W)),
        out_shape=(jax.ShapeDtypeStruct((T, ZA_W), BF16), jax.ShapeDtypeStruct((T, ZB_W), BF16),
                   jax.ShapeDtypeStruct((1, ZA_W), F32), jax.ShapeDtypeStruct((1, ZB_W), F32)),
        scratch_shapes=[_token_scratch(ROW_BLK, W)],
        compiler_params=pltpu.CompilerParams(dimension_semantics=("arbitrary",)),
    )(*args)


def _place():
    x, y, c = lax.axis_index("x"), lax.axis_index("y"), lax.axis_index("c")
    chips = [(1 - x, y), (x, 1 - y), (1 - x, 1 - y)]
    return x, y, c, chips


def _cast_place(shards, k_idx, *, name):
    n, steps = len(shards), 4

    def body(k_ref, *refs):
        for s_ref, o_ref in zip(refs[:n], refs[n:]):
            o_ref[...] = s_ref[...].astype(o_ref.dtype)

    blocks = [(a.shape[0] // steps, a.shape[1]) for a in shards]
    grid_spec = pltpu.PrefetchScalarGridSpec(
        num_scalar_prefetch=1, grid=(steps,),
        in_specs=[pl.BlockSpec(blk, lambda t, k_ref: (t, 0)) for blk in blocks],
        out_specs=tuple(pl.BlockSpec(blk, lambda t, k_ref: (k_ref[0] * steps + t, 0)) for blk in blocks))
    return pl.pallas_call(
        body, name=name, grid_spec=grid_spec,
        out_shape=tuple(jax.ShapeDtypeStruct((N_CHIPS * a.shape[0], a.shape[1]), BF16) for a in shards),
        compiler_params=pltpu.CompilerParams(dimension_semantics=("parallel",),
                                             vmem_limit_bytes=_vmem_limit(*[_nbytes(b, F32) for b in blocks])),
    )(k_idx, *shards)


def _gather_carry(fulls, jobs):
    n = len(jobs)

    def piece(ref, chip_idx, half, part):
        rs = ref.shape[0] // N_CHIPS
        rh = rs // 2
        rows = rh // part[1]
        return ref.at[pl.ds(chip_idx * rs + half * rh + part[0] * rows, rows)]

    def copy(sems, sem, src_ref, dst_ref, to):
        return pltpu.make_async_remote_copy(src_ref=src_ref, dst_ref=dst_ref, send_sem=sems[0].at[sem],
                                            recv_sem=sems[1].at[sem], device_id=to, device_id_type=MESH)

    def to_chips(ins, outs, sems, j, arrival, hops=("both", "chips")):
        i, part, hop = jobs[j]
        x, y, c, chips = _place()
        res = []
        for t, chip in enumerate(chips if hop in hops else ()):
            theirs = piece(outs[i], 2 * chip[0] + chip[1], c, part)
            src, dst = (theirs, theirs) if arrival else (piece(ins[i], 2 * x + y, c, part), piece(outs[i], 2 * x + y, c, part))
            res.append(copy(sems, 3 * j + t, src, dst, (*chip, c)))
        return res

    def to_sibling(outs, sems, j, arrival, hops=("both", "sibling")):
        i, part, hop = jobs[j]
        x, y, c, chips = _place()
        res = []
        for t, chip in enumerate(chips if hop in hops else ()):
            region = piece(outs[i], 2 * chip[0] + chip[1], 1 - c if arrival else c, part)
            res.append(copy(sems, 3 * n + 3 * j + t, region, region, (x, y, 1 - c)))
        return res

    def start(ins, outs, sems):
        for j in range(n):
            for send in to_chips(ins, outs, sems, j, False) + to_sibling(outs, sems, j, False, ("sibling",)):
                send.start()

    def before_last(ins, outs, sems):
        for j in range(n):
            for arrival, send in zip(to_chips(ins, outs, sems, j, True, ("both",)),
                                     to_sibling(outs, sems, j, False, ("both",))):
                arrival.wait_recv()
                send.start()

    def finish(ins, outs, sems):
        for j in range(n):
            for arrival in to_chips(ins, outs, sems, j, True, ("chips",)) + to_sibling(outs, sems, j, True):
                arrival.wait_recv()
        for j in range(n):
            for send in to_chips(ins, outs, sems, j, False) + to_sibling(outs, sems, j, False):
                send.wait_send()

    return _Carry(fulls, [jax.ShapeDtypeStruct(a.shape, a.dtype) for a in fulls], {i: i for i in range(len(fulls))},
                  [pltpu.SemaphoreType.DMA((6 * n,)), pltpu.SemaphoreType.DMA((6 * n,))], start, finish, before_last)


def _pair_swap_carry(grads):
    n = len(grads)

    def copy(ins, outs, sems, i):
        x, y, c, _ = _place()
        return pltpu.make_async_remote_copy(src_ref=ins[i].at[:, 1 - c], dst_ref=outs[i], send_sem=sems[0].at[i],
                                            recv_sem=sems[1].at[i], device_id=(x, y, 1 - c), device_id_type=MESH)

    def start(ins, outs, sems):
        for i in range(n):
            copy(ins, outs, sems, i).start()

    def finish(ins, outs, sems):
        for i in range(n):
            copy(ins, outs, sems, i).wait()

    return _Carry(grads, [jax.ShapeDtypeStruct((a.shape[0], a.shape[2], a.shape[3]), a.dtype) for a in grads], {},
                  [pltpu.SemaphoreType.DMA((n,)), pltpu.SemaphoreType.DMA((n,))], start, finish)


def _pair_sum(g4, got, c_idx, *, name):
    _, _, rh, cs = g4.shape
    rb = _row_block(rh, 512, 16)

    def body(c_ref, own_ref, got_ref, o_ref):
        o_ref[...] = (own_ref[...].astype(F32) + got_ref[...].astype(F32)).astype(o_ref.dtype)

    grid_spec = pltpu.PrefetchScalarGridSpec(
        num_scalar_prefetch=1, grid=(N_CHIPS, rh // rb),
        in_specs=[pl.BlockSpec((None, None, rb, cs), lambda k, t, c_ref: (k, c_ref[0], t, 0)),
                  pl.BlockSpec((None, rb, cs), lambda k, t, c_ref: (k, t, 0))],
        out_specs=pl.BlockSpec((None, rb, cs), lambda k, t, c_ref: (k, t, 0)))
    return pl.pallas_call(
        body, name=name, grid_spec=grid_spec, out_shape=jax.ShapeDtypeStruct((N_CHIPS, rh, cs), BF16),
        compiler_params=pltpu.CompilerParams(dimension_semantics=("parallel", "parallel")),
    )(c_idx, g4, got)


def _chip_scatter_carry(sums):
    n = len(sums)

    def copies(ins, outs, sems):
        x, y, c, chips = _place()
        return [pltpu.make_async_remote_copy(src_ref=ins[i].at[2 * chip[0] + chip[1]], dst_ref=outs[i].at[j],
                                             send_sem=sems[0].at[3 * i + j], recv_sem=sems[1].at[3 * i + j],
                                             device_id=(*chip, c), device_id_type=MESH)
                for i in range(n) for j, chip in enumerate(chips)]

    def start(ins, outs, sems):
        for cp in copies(ins, outs, sems):
            cp.start()

    def finish(ins, outs, sems):
        for cp in copies(ins, outs, sems):
            cp.wait()

    return _Carry(sums, [jax.ShapeDtypeStruct((3,) + a.shape[1:], a.dtype) for a in sums], {},
                  [pltpu.SemaphoreType.DMA((3 * n,)), pltpu.SemaphoreType.DMA((3 * n,))], start, finish)


def _chip_sum(sums, gots, kc_idx, *, name):
    n, steps = len(sums), 2

    def body(kc_ref, *refs):
        for own_ref, got_ref, o_ref in zip(refs[:n], refs[n:2 * n], refs[2 * n:]):
            acc = own_ref[...].astype(F32)
            for j in range(3):
                acc = acc + got_ref[j].astype(F32)
            o_ref[...] = acc

    blocks = [(a.shape[1] // steps, a.shape[2]) for a in sums]
    grid_spec = pltpu.PrefetchScalarGridSpec(
        num_scalar_prefetch=1, grid=(steps,),
        in_specs=[pl.BlockSpec((None,) + blk, lambda t, kc: (kc[0], t, 0)) for blk in blocks]
        + [pl.BlockSpec((3,) + blk, lambda t, kc: (0, t, 0)) for blk in blocks],
        out_specs=tuple(pl.BlockSpec(blk, lambda t, kc: (kc[1] * steps + t, 0)) for blk in blocks))
    return pl.pallas_call(
        body, name=name, grid_spec=grid_spec,
        out_shape=tuple(jax.ShapeDtypeStruct((2 * a.shape[1], a.shape[2]), F32) for a in sums),
        compiler_params=pltpu.CompilerParams(dimension_semantics=("parallel",),
                                             vmem_limit_bytes=_vmem_limit(*[3 * _nbytes(b, F32) for b in blocks])),
    )(kc_idx, *sums, *gots)


def _half_swap_carry(shards):
    n = len(shards)

    def copy(ins, outs, sems, i, to_my_half):
        x, y, c, _ = _place()
        rh = ins[i].shape[0] // 2
        half = c if to_my_half else 1 - c
        return pltpu.make_async_remote_copy(
            src_ref=ins[i].at[pl.ds(c * rh, rh)], dst_ref=outs[i].at[pl.ds(half * rh, rh)],
            send_sem=sems[0].at[i], recv_sem=sems[1].at[i], device_id=(x, y, 1 - c), device_id_type=MESH)

    def start(ins, outs, sems):
        for i in range(n):
            copy(ins, outs, sems, i, True).start()

    def finish(ins, outs, sems):
        for i in range(n):
            copy(ins, outs, sems, i, False).wait_recv()
        for i in range(n):
            copy(ins, outs, sems, i, True).wait_send()

    return _Carry(shards, [jax.ShapeDtypeStruct(a.shape, a.dtype) for a in shards], {i: i for i in range(n)},
                  [pltpu.SemaphoreType.DMA((n,)), pltpu.SemaphoreType.DMA((n,))], start, finish)


SMALL_ROW = 1024


def _small_layout(shapes):
    starts, row = [], 0
    for r, c in shapes:
        starts.append(row)
        row += -(-c // SMALL_ROW) if r == 1 else r
    return starts, -(-row // SUBLANES) * SUBLANES


def _small_pieces(shape, start):
    r, c = shape
    if r == 1:
        return [(start + q, min(SMALL_ROW, c - q * SMALL_ROW), q * SMALL_ROW) for q in range(-(-c // SMALL_ROW))]
    return None


def _whole_spec(shape):
    return pl.BlockSpec(tuple(shape), lambda i: (0,) * len(shape))


def _small_all_reduce(parts, *, name, carry=None):
    shapes = [a.shape for a in parts]
    starts, rows = _small_layout(shapes)
    n = len(parts)
    n_tables = sum(1 for r, _ in shapes if r > 1)
    assert all(r > 1 and c <= LANES for r, c in shapes[:n_tables]) and all(r == 1 for r, _ in shapes[n_tables:])
    table_rows = starts[n_tables]
    assert table_rows % SUBLANES == 0

    def regions(slot):
        return slot.at[pl.ds(0, table_rows), pl.ds(0, LANES)], slot.at[pl.ds(table_rows, rows - table_rows)]

    def peer(d):
        x, y, c, _ = _place()
        return 1 - x if d & 4 else x, 1 - y if d & 2 else y, 1 - c if d & 1 else c

    def slot_of(d):
        px, py, pc = peer(d)
        return 4 * px + 2 * py + pc

    def copy(refs, d, part, dst_slot):
        buf, send_sems, recv_sems = refs[2 * n:]
        sem = 2 * (d - 1) + part
        return pltpu.make_async_remote_copy(
            src_ref=regions(buf.at[slot_of(0)])[part], dst_ref=regions(buf.at[dst_slot])[part],
            send_sem=send_sems.at[sem], recv_sem=recv_sems.at[sem], device_id=peer(d), device_id_type=MESH)

    def send(*refs):
        ins, mine = refs[0:n], refs[2 * n].at[slot_of(0)]
        mine[...] = jnp.zeros((rows, SMALL_ROW), F32)
        for ref, shape, start in zip(ins, shapes, starts):
            pieces = _small_pieces(shape, start)
            if pieces is None:
                mine[start:start + shape[0], 0:shape[1]] = ref[...]
            else:
                for row, width, col in pieces:
                    mine[row:row + 1, 0:width] = ref[:, col:col + width]
        for d in range(1, 8):
            for part in range(2):
                copy(refs, d, part, slot_of(0)).start()

    def receive(*refs):
        outs, buf = refs[n:2 * n], refs[2 * n]
        mine = buf.at[slot_of(0)]
        for d in range(1, 8):
            for part in range(2):
                copy(refs, d, part, slot_of(d)).wait_recv()
        for d in range(1, 8):
            for part in range(2):
                copy(refs, d, part, slot_of(0)).wait_send()
        tables, vectors = regions(buf.at[0])
        acc_t, acc_v = tables[...], vectors[...]
        for s in range(1, 8):
            tables, vectors = regions(buf.at[s])
            acc_t, acc_v = acc_t + tables[...], acc_v + vectors[...]
        tables, vectors = regions(mine)
        tables[...] = acc_t
        vectors[...] = acc_v
        for ref, shape, start in zip(outs, shapes, starts):
            pieces = _small_pieces(shape, start)
            if pieces is None:
                ref[...] = mine[start:start + shape[0], 0:shape[1]]
            else:
                for row, width, col in pieces:
                    ref[:, col:col + width] = mine[row:row + 1, 0:width]

    def body(*refs):
        send(*refs)
        receive(*refs)

    whole = [_whole_spec(s) for s in shapes]
    return _call(body if carry is None else receive, before_carry=None if carry is None else send,
                 name=name, grid=(1,), in_specs=whole, out_specs=whole,
                 out_shape=[jax.ShapeDtypeStruct(s, F32) for s in shapes], args=parts,
                 scratch=[pltpu.VMEM((8, rows, SMALL_ROW), F32), pltpu.SemaphoreType.DMA((14,)),
                          pltpu.SemaphoreType.DMA((14,))], carry=carry)


def _adamw_small(ws, gs, ms, vs, *, name):
    n = len(ws)
    c1 = 1.0 - ADAM_B1 ** ADAM_STEP
    c2 = 1.0 - ADAM_B2 ** ADAM_STEP

    def body(*refs):
        for k in range(n):
            w_ref, g_ref, m_ref, v_ref = (refs[t * n + k] for t in range(4))
            go_ref, d_ref, mo_ref, vo_ref = (refs[(4 + t) * n + k] for t in range(4))
            gv = g_ref[...]
            go_ref[...] = gv
            mn = ADAM_B1 * m_ref[...] + (1.0 - ADAM_B1) * gv
            vn = ADAM_B2 * v_ref[...] + (1.0 - ADAM_B2) * (gv * gv)
            d_ref[...] = -ADAM_LR * ((mn / c1) / (jnp.sqrt(vn / c2) + ADAM_EPS) + ADAM_WD * w_ref[...])
            mo_ref[...] = mn
            vo_ref[...] = vn

    whole = [_whole_spec(a.shape) for a in ws]
    shapes = tuple(jax.ShapeDtypeStruct(a.shape, F32) for a in ws)
    res = pl.pallas_call(
        body, name=name, grid=(1,), in_specs=whole * 4, out_specs=tuple(whole * 4), out_shape=shapes * 4,
    )(*ws, *gs, *ms, *vs)
    return res[0:n], res[n:2 * n], res[2 * n:3 * n], res[3 * n:4 * n]


def _adamw(ws, gs, ms, vs, *, name):
    n, steps = len(ws), 8
    c1 = 1.0 - ADAM_B1 ** ADAM_STEP
    c2 = 1.0 - ADAM_B2 ** ADAM_STEP

    def body(*refs):
        for k in range(n):
            w_ref, g_ref, m_ref, v_ref = (refs[t * n + k] for t in range(4))
            go_ref, d_ref, mo_ref, vo_ref = (refs[(4 + t) * n + k] for t in range(4))
            gv = g_ref[...]
            go_ref[...] = gv
            mn = ADAM_B1 * m_ref[...] + (1.0 - ADAM_B1) * gv
            vn = ADAM_B2 * v_ref[...] + (1.0 - ADAM_B2) * (gv * gv)
            d_ref[...] = -ADAM_LR * ((mn / c1) / (jnp.sqrt(vn / c2) + ADAM_EPS) + ADAM_WD * w_ref[...])
            mo_ref[...] = mn
            vo_ref[...] = vn

    blocks = [(a.shape[0] // steps, a.shape[1]) for a in ws]
    specs = [pl.BlockSpec(blk, lambda i: (i, 0)) for blk in blocks]
    shapes = tuple(jax.ShapeDtypeStruct(a.shape, F32) for a in ws)
    res = pl.pallas_call(
        body, name=name, grid=(steps,), in_specs=specs * 4, out_specs=tuple(specs * 4), out_shape=shapes * 4,
        compiler_params=pltpu.CompilerParams(dimension_semantics=("parallel",),
                                             vmem_limit_bytes=_vmem_limit(*[8 * _nbytes(b, F32) for b in blocks])),
    )(*ws, *gs, *ms, *vs)
    return res[0:n], res[n:2 * n], res[2 * n:3 * n], res[3 * n:4 * n]


def _local_step(x, p, target, small, sched):
    T = x.shape[0]
    W = MIX_W
    rel_bias = small["rel_bias"]
    b_in_a, b_in_b = small["b_in"][:, 0:ZA_W], small["b_in"][:, ZA_W:]

    idx_np = [_t5_index(A_WINDOW - 1, 1)] + [_t5_index(window // dil, dil) for window, dil in B_PATTERNS]
    idx_all = jnp.asarray(np.stack(idx_np))
    n1, bias_all = sched.run(_prologue, x, small["ffn1_pre_g"], rel_bias, idx_all, name="prologue")
    w_gu1 = sched.weight("ffn1_w_gu")
    *gu1, a1 = sched.run(_ffn_up, n1, w_gu1, bm=512, name="ffn1_gu")
    w_d1 = sched.weight("ffn1_w_down")
    f1, h1, n2 = sched.run(_mm_resid_norm, a1, w_d1, x, small["ffn1_post_g"], 0.5, small["attn_pre_g"], bm=512,
                           name="ffn1_down")
    win_a, win_b = sched.weight("w_in")
    za = _mm_nt(n2, win_a, bm=1024, bn=ZA_W, out_dtype=BF16, name="in_proj_a", add=b_in_a)
    dils = tuple(dil for _, dil in B_PATTERNS)
    zb_by_dil = _mm_nt_classes(n2, win_b, b_in_b, dils, bm=512, name="in_proj_b")

    a_args = dict(r=1, q_col=0, k_col=W // LANES, v_col=W // LANES + 1, stride=0, pattern=0, kv_shared=True)
    sink_row = jnp.repeat(small["sinks"], HEAD_DIM, axis=1)
    out_a, out_a_bf, lse_a = sched.run(_band_fwd, za, za, za, bias_all, name="band_a_fwd", sink=sink_row, **a_args)

    no_sink = jnp.full((1, W), NEG, F32)
    b_ctx, branches = [], []
    for t, dil in enumerate(dils):
        zb_r = zb_by_dil[t]
        args = dict(r=dil, q_col=0, k_col=1, v_col=2, stride=ZB_W // W, pattern=1 + t)
        branches.append(sched.run(_band_fwd, zb_r, zb_r, zb_r, bias_all, name=f"band_b{t}_fwd", **args))
        b_ctx.append((jnp.asarray(idx_np[1 + t]), zb_r, args))
    out_b, out_b_bf, lse_b = _merge(branches, dils, no_sink, name="merge_b")

    mix = jnp.concatenate([out_a_bf, out_b_bf], axis=1)
    w_out = sched.weight("w_out")
    att, h2, n3 = _mm_resid_norm(mix, w_out, h1, small["attn_post_g"], 1.0, small["ffn2_pre_g"], bm=512,
                                 name="out_proj", bias=small["b_out"])
    w_gu2, w_d2 = sched.weight("ffn2_w_gu"), sched.weight("ffn2_w_down")
    *gu2, a2 = _ffn_up(n3, w_gu2, bm=512, name="ffn2_gu")
    f2, h3, n4 = _mm_resid_norm(a2, w_d2, h2, small["ffn2_post_g"], 0.5, small["ple_pre_g"], bm=512,
                                name="ffn2_down")
    w_gate = sched.weight("w_ple_gate")
    p_bf = p.astype(BF16)
    dh4, de, dgp, loss, d_ple_post = _ple_head(n4, w_gate, p_bf, sched.weight("w_ple_proj"), h3, small["ple_post_g"],
                                               target, bm=512, name="ple_head")

    gs = {"ple_post_g": d_ple_post}
    sched.grad("w_ple_proj", _mm_tn(p_bf, de, bm=256, bn=1024, out_dtype=BF16, name="d_w_ple"))
    sched.grad("w_ple_gate", _mm_tn(n4, dgp, bm=512, bn=1024, out_dtype=BF16, name="d_w_gate"))
    dh3, df2, gs["ple_pre_g"], gs["ffn2_post_g"], _ = sched.run(
        _mm_nt_norms_bwd, dgp, w_gate, dh4, h3, small["ple_pre_g"], f2, small["ffn2_post_g"], 0.5, bm=512, name="d_n4")

    def ffn_bwd(df, a, gu, n, w_down, w_gu, tag, *norm_args):
        dgu = sched.run(_ffn_down_bwd, df, w_down, gu[0], gu[1], bm=512, name=f"ffn{tag}_d_a")
        sched.grad(f"ffn{tag}_w_gu", _mm_tn(n, dgu, bm=512, bn=1408, out_dtype=BF16, name=f"ffn{tag}_d_w_gu",
                                            n_stack=N_CHIPS))
        sched.grad(f"ffn{tag}_w_down", sched.run(_mm_tn, a, df, bm=256, bn=1024, out_dtype=BF16,
                                                name=f"ffn{tag}_d_w_down"))
        return sched.run(_mm_nt_norms_bwd, dgu, w_gu, *norm_args, bm=512, name=f"ffn{tag}_d_n")

    dh2, datt, gs["ffn2_pre_g"], gs["attn_post_g"], gs["b_out"] = ffn_bwd(
        df2, a2, gu2, n3, w_d2, w_gu2, "2", dh3, h2, small["ffn2_pre_g"], att, small["attn_post_g"], 1.0)
    sched.grad("w_out", _mm_tn(mix, datt, bm=512, bn=1024, out_dtype=BF16, name="d_w_out"))
    dmix = sched.run(_mm_nt, datt, w_out, bm=1024, bn=1024, out_dtype=F32, name="d_mix")

    do_a, stat_a, dsink, do_b, stat_b = _attn_delta(dmix, (out_a, out_b), (lse_a, lse_b), sink_row, dils,
                                                    name="attn_delta")
    gs["sinks"] = dsink[:, ::HEAD_DIM]
    dq_a, dk_a, dv_a, dbias_a = sched.run(_band_bwd, za, za, za, do_a, stat_a, bias_all, name="band_a_bwd", **a_args)
    d_rel_a = _bias_grad(dbias_a, jnp.asarray(idx_np[0]), name="d_rel_a")
    b_grads = []
    d_rel_b = None
    for t, (idx, zb_r, args) in enumerate(b_ctx):
        dq, dk, dv, dbias = sched.run(_band_bwd, zb_r, zb_r, zb_r, do_b[t], stat_b[t], bias_all,
                                      name=f"band_b{t}_bwd", **args)
        b_grads.append((dq, dk, dv))
        d_rel = _bias_grad(dbias, idx, name=f"d_rel_b{t}")
        d_rel_b = d_rel if d_rel_b is None else d_rel_b + d_rel
    gs["rel_bias"] = jnp.concatenate([d_rel_a[:, 0:2 * N_PAIRS], d_rel_b[:, 0:2 * N_PAIRS]], axis=1)
    dza, dzb, dsum_a, dsum_b = _dz_assemble(dq_a, dk_a, dv_a, b_grads, dils, name="d_z")
    gs["b_in"] = jnp.concatenate([dsum_a, dsum_b], axis=1)
    sched.grad("w_in", (_mm_tn(dza, n2, bm=ZA_W, bn=1024, out_dtype=BF16, name="d_w_in_a"),
                        _mm_tn(dzb, n2, bm=ZB_W // 2, bn=1024, out_dtype=BF16, name="d_w_in_b")))
    dn2_a = _mm_nn(dza, win_a, bm=1024, bn=1024, out_dtype=F32, name="d_n2_a")
    dh1, df1, gs["attn_pre_g"], gs["ffn1_post_g"], _ = sched.run(
        _mm_nt_norms_bwd, dzb, win_b, dh2, h1, small["attn_pre_g"], f1, small["ffn1_post_g"], 0.5, bm=512,
        name="d_n2_b", add=dn2_a, b_is_kn=True)
    grad_x, gs["ffn1_pre_g"] = ffn_bwd(df1, a1, gu1, n1, w_d1, w_gu1, "1", dh1, x, small["ffn1_pre_g"], None, None, 0.0)
    return loss, grad_x, gs


def _to_compute(name, full):
    st = full.reshape(N_CHIPS, full.shape[0] // N_CHIPS, full.shape[1])
    if name in ("ffn1_w_gu", "ffn2_w_gu"):
        return st
    if name == "w_ple_proj":
        return jnp.transpose(st, (1, 0, 2)).reshape(st.shape[1], -1)
    if name == "w_in":
        return full[0:ZA_W], full[ZA_W:]
    return full


def _to_comm(name, g):
    if name == "w_in":
        g = jnp.concatenate(g, axis=0)
    if name == "w_ple_proj":
        g = jnp.transpose(g.reshape(g.shape[0], N_CHIPS, -1), (1, 0, 2))
    rs = g.shape[1] if g.ndim == 3 else g.shape[0] // N_CHIPS
    return g.reshape(N_CHIPS, 2, rs // 2, g.shape[-1])


class _Schedule:
    GATHER = {
        "prologue": [("ffn1_w_gu", (0, 1), "both")],
        "ffn1_gu": [("ffn1_w_down", (0, 1), "both"), ("w_in", (0, 1), "both")],
        "band_a_fwd": [("ffn2_w_gu", (0, 2), "chips"), ("w_out", (0, 1), "chips")],
        "band_b0_fwd": [("ffn2_w_gu", (0, 2), "sibling"), ("w_out", (0, 1), "sibling"),
                        ("ffn2_w_gu", (1, 2), "chips"), ("w_ple_gate", (0, 1), "chips"), ("w_ple_proj", (0, 1), "chips")],
        "band_b1_fwd": [("ffn2_w_gu", (1, 2), "sibling"), ("w_ple_gate", (0, 1), "sibling"),
                        ("w_ple_proj", (0, 1), "sibling"), ("ffn2_w_down", (0, 1), "both")],
    }
    SWAP = {"ffn2_d_w_down": ["w_ple_proj", "w_ple_gate", "ffn2_w_gu"], "ffn2_d_n": ["ffn2_w_down"],
            "band_a_bwd": ["w_out"], "d_n2_b": ["w_in"], "ffn1_d_w_down": ["ffn1_w_gu"], "ffn1_d_n": ["ffn1_w_down"]}
    SCATTER = {"ffn2_d_n": ["ffn2_w_gu"], "band_a_bwd": ["w_ple_proj", "w_ple_gate", "ffn2_w_down"],
               "d_n2_b": ["w_out"], "ffn1_d_w_down": ["w_in"], "ffn1_d_n": ["ffn1_w_gu"]}
    HALF_SWAP = {"ffn1_d_n": ["w_ple_proj", "w_ple_gate", "ffn2_w_gu", "ffn2_w_down", "w_out", "w_in"]}

    def __init__(self, placed, c_idx, kc_idx):
        self.full = dict(placed)
        self.missing = {n: 1.0 for n in placed}
        self.c_idx, self.kc_idx = c_idx, kc_idx
        self.grads, self.swapped, self.pair_sums, self.scattered, self.summed = {}, {}, {}, {}, {}

    def _gather(self, entries):
        if not entries:
            return [], []
        names = list(dict.fromkeys(n for n, _, _ in entries))
        carry = _gather_carry([self.full[n] for n in names], [(names.index(n), pt, hops) for n, pt, hops in entries])

        def done():
            self.full.update(zip(names, carry.results))
            for n, part, hops in entries:
                if hops != "chips":
                    self.missing[n] -= 1.0 / part[1]
        return [carry], [done]

    def weight(self, name):
        assert abs(self.missing[name]) < 1e-9, (name, self.missing[name])
        return _to_compute(name, self.full[name])

    def grad(self, name, g):
        self.grads[name] = _to_comm(name, g)

    def _swap(self, names):
        carry = _pair_swap_carry([self.grads[n] for n in names])

        def done():
            self.swapped.update(zip(names, carry.results))
        return carry, done

    def _scatter(self, names):
        for n in names:
            self.pair_sums[n] = _pair_sum(self.grads[n], self.swapped[n], self.c_idx, name=f"grad_pair_sum_{n}")
        carry = _chip_scatter_carry([self.pair_sums[n] for n in names])

        def done():
            self.scattered.update(zip(names, carry.results))
        return carry, done

    def _half_swap(self, names):
        halves = _chip_sum([self.pair_sums[n] for n in names], [self.scattered[n] for n in names], self.kc_idx,
                           name=f"grad_chip_sum_{names[0]}")
        carry = _half_swap_carry(halves)

        def done():
            self.summed.update(zip(names, carry.results))
        return carry, done

    def run(self, fn, *args, name, **kw):
        carries, after = self._gather(self.GATHER.get(name, []))
        for names, make in ((self.SWAP.get(name), self._swap), (self.SCATTER.get(name), self._scatter),
                            (self.HALF_SWAP.get(name), self._half_swap)):
            if names:
                carry, done = make(names)
                carries.append(carry)
                after.append(done)
        out = fn(*args, name=name, carry=_join(carries), **kw)
        for done in after:
            done()
        return out

    def finish(self, last_carrier):
        left = [n for n in BIG if n not in self.scattered]
        to_swap = [n for n in left if n not in self.swapped]
        if to_swap:
            carry, done = self._swap(to_swap)
            _comm_call(carry, name="grad_pair_swap")
            done()
        carry, done = self._scatter(left) if left else (None, lambda: None)
        carried = last_carrier(carry=carry)
        done()
        carry, done = self._half_swap([n for n in BIG if n not in self.summed])
        _comm_call(carry, name="grad_half_swap")
        done()
        return self.summed, carried


def kernel(x, p, rel_bias, ffn1_pre_g, ffn1_w_gu, ffn1_w_down, ffn1_post_g, attn_pre_g, w_in, b_in, sinks, w_out, b_out, attn_post_g, ffn2_pre_g, ffn2_w_gu, ffn2_w_down, ffn2_post_g, ple_pre_g, w_ple_gate, w_ple_proj, ple_post_g, loss_target, m_rel_bias, m_ffn1_pre_g, m_ffn1_w_gu, m_ffn1_w_down, m_ffn1_post_g, m_attn_pre_g, m_w_in, m_b_in, m_sinks, m_w_out, m_b_out, m_attn_post_g, m_ffn2_pre_g, m_ffn2_w_gu, m_ffn2_w_down, m_ffn2_post_g, m_ple_pre_g, m_w_ple_gate, m_w_ple_proj, m_ple_post_g, v_rel_bias, v_ffn1_pre_g, v_ffn1_w_gu, v_ffn1_w_down, v_ffn1_post_g, v_attn_pre_g, v_w_in, v_b_in, v_sinks, v_w_out, v_b_out, v_attn_post_g, v_ffn2_pre_g, v_ffn2_w_gu, v_ffn2_w_down, v_ffn2_post_g, v_ple_pre_g, v_w_ple_gate, v_w_ple_proj, v_ple_post_g):
    given = dict(locals())
    w = {n: given[n] for n in WEIGHTS}
    m = {n: given["m_" + n] for n in WEIGHTS}
    v = {n: given["v_" + n] for n in WEIGHTS}
    c_pos = lax.axis_index("c").astype(jnp.int32)
    k_pos = (2 * lax.axis_index("x") + lax.axis_index("y")).astype(jnp.int32)
    c_idx, k_idx, kc_idx = c_pos.reshape(1), k_pos.reshape(1), jnp.stack([k_pos, c_pos])

    def shard(a, n):
        return jnp.transpose(a[0]) if n == "w_in" else a[0]

    def unshard(a, n):
        return (jnp.transpose(a) if n == "w_in" else a)[None]

    placed = _cast_place([shard(w[n], n) for n in BIG], k_idx, name="place_shards")
    sched = _Schedule(dict(zip(BIG, placed)), c_idx, kc_idx)
    small = {n: (w[n] if n == "rel_bias" else w[n].reshape(1, -1)) for n in SMALL}

    loss_part, grad_x, gs = _local_step(x[0], p[0, 0], loss_target[0], small, sched)

    grads_big, (*small_grads, loss_row) = sched.finish(
        partial(_small_all_reduce, [gs[n] for n in SMALL] + [loss_part], name="small_all_reduce"))
    loss = loss_row[0, 0]

    grads, delta, new_m, new_v = {}, {}, {}, {}
    res = _adamw([shard(w[n], n) for n in BIG], [grads_big[n] for n in BIG], [shard(m[n], n) for n in BIG],
                 [shard(v[n], n) for n in BIG], name="adamw_big")
    for n, gg, dd, mm, vv in zip(BIG, *res):
        grads[n], delta[n], new_m[n], new_v[n] = (unshard(a, n) for a in (gg, dd, mm, vv))
    res = _adamw_small([w[n] for n in SMALL], small_grads, [m[n] for n in SMALL], [v[n] for n in SMALL],
                       name="adamw_small")
    for name, gg, dd, mm, vv in zip(SMALL, *res):
        grads[name], delta[name], new_m[name], new_v[name] = gg, dd, mm, vv

    return (loss, grad_x[None], *[grads[n] for n in WEIGHTS], *[delta[n] for n in WEIGHTS],
            *[new_m[n] for n in WEIGHTS], *[new_v[n] for n in WEIGHTS])
```
